```python
import math
import jax, jax.numpy as jnp
from jax import lax
import numpy as np

D_MODEL = 1024
BATCH = 8
SEQ = 2048
DEPTH = 2

F32 = jnp.float32
EPS = 1e-6
N_SUB = 3
D_FF = ((8 * D_MODEL // 3 + 127) // 128) * 128
FFN_RES_WEIGHT = 0.5
D_MIX = D_MODEL
POOL_WINDOWS = (2, 4, 8, 16)
N_POOL = len(POOL_WINDOWS)
POOL_WIDTH = D_MIX // 2
POOL_GD = POOL_WIDTH // N_POOL
SGU_WIDTH = D_MIX // 2
SGU_HEADS = 4
SGU_HD = SGU_WIDTH // SGU_HEADS
CHUNK = 128
SSM_WIDTH = D_MIX
SSM_GROUP = 16
SSM_GROUPS = SSM_WIDTH // SSM_GROUP
SSM_STATE = 64
DT_MIN = 1e-3
DT_MAX = 1e-1
N_EVEN = (DEPTH + 1) // 2
N_ODD = DEPTH // 2

kernel_name = 'hybrid_pool_sgu_s5_macaron_adaln'


def rmsnorm(x, g):
    xf = x.astype(F32)
    y = xf * lax.rsqrt(jnp.mean(xf * xf, axis=-1, keepdims=True) + EPS)
    return (y * g.astype(F32)).astype(x.dtype)


def sublayer(x, fn, mod_k, g_pre, g_post, res_weight):
    shift, scale, gate = mod_k[:, 0, None, :], mod_k[:, 1, None, :], mod_k[:, 2, None, :]
    h = rmsnorm(x, g_pre) * (1.0 + scale) + shift
    y = rmsnorm(fn(h), g_post)
    return x + res_weight * gate * y


def swiglu(h, w_in, w_out):
    a, b = jnp.split(h @ w_in, 2, axis=-1)
    return (jax.nn.silu(a) * b) @ w_out


def pool_mixer(a, w_group, ch_scale):
    s_len = a.shape[1]
    cs = jnp.cumsum(a.astype(F32), axis=1)
    pos = jnp.arange(1, s_len + 1, dtype=F32)[None, :, None]
    diffs = []
    for g, w in enumerate(POOL_WINDOWS):
        sl = slice(g * POOL_GD, (g + 1) * POOL_GD)
        c_g = cs[..., sl]
        lagged = jnp.pad(c_g, ((0, 0), (w, 0), (0, 0)))[:, :s_len]
        mean = (c_g - lagged) / jnp.minimum(pos, float(w))
        diffs.append(mean - a[..., sl].astype(F32))
    d = jnp.stack(diffs, axis=2).astype(a.dtype)
    y = jnp.einsum('bsgi,gio->bsgo', d, w_group)
    return y.reshape(a.shape) * ch_scale


def sgu_mixer(z, ln_g, ln_b, w_s, b_s):
    u, v = jnp.split(z, 2, axis=-1)
    bsz, s_len, _ = u.shape
    vf = v.astype(F32).reshape(bsz, s_len, SGU_HEADS, SGU_HD)
    mu = jnp.mean(vf, axis=-1, keepdims=True)
    var = jnp.mean(jnp.square(vf - mu), axis=-1, keepdims=True)
    vn = ((vf - mu) * lax.rsqrt(var + EPS)).reshape(bsz, s_len, SGU_WIDTH) * ln_g + ln_b
    vn = vn.astype(u.dtype).reshape(bsz, s_len // CHUNK, CHUNK, SGU_HEADS, SGU_HD)
    causal = jnp.tril(jnp.ones((CHUNK, CHUNK), dtype=bool))
    w = jnp.where(causal[None], w_s, 0.0)
    s = jnp.einsum('hts,bcshd->bcthd', w, vn) + b_s.T[None, None, :, :, None]
    return u * s.reshape(bsz, s_len, SGU_WIDTH)


def s5_mixer(u, lam_re, lam_im, b_re, b_im, c_re, c_im, d_skip, log_dt, w_glu):
    bsz, s_len, _ = u.shape
    lam = lax.complex(lam_re.astype(F32), lam_im.astype(F32))
    dt = jnp.exp(log_dt.astype(F32))[:, None]
    lam_bar = jnp.exp(lam * dt)
    bmat = lax.complex(b_re.astype(F32), b_im.astype(F32))
    b_bar = ((lam_bar - 1.0) / lam)[..., None] * bmat
    cmat = lax.complex(c_re.astype(F32), c_im.astype(F32))
    uf = u.astype(F32)
    ug = uf.reshape(bsz, s_len, SSM_GROUPS, SSM_GROUP).astype(jnp.complex64)
    bu = jnp.einsum('gpn,bsgn->bsgp', b_bar, ug)
    a_seq = jnp.broadcast_to(lam_bar, (1, s_len) + lam_bar.shape)

    def combine(left, right):
        a_l, x_l = left
        a_r, x_r = right
        return a_r * a_l, a_r * x_l + x_r

    _, states = lax.associative_scan(combine, (a_seq, bu), axis=1)
    y = jnp.einsum('gnp,bsgp->bsgn', cmat, states).real.reshape(bsz, s_len, SSM_WIDTH)
    y = y + d_skip.astype(F32) * uf
    g = jax.nn.gelu(y).astype(u.dtype)
    a, b = jnp.split(g @ w_glu, 2, axis=-1)
    return a * jax.nn.sigmoid(b)


def _fwd_setup_inputs(seed: int = 0) -> dict:
    key = jax.random.key(seed)
    ks = iter(jax.random.split(key, 32))

    def nrm(shape, scale):
        return jax.random.normal(next(ks), shape, F32) * scale

    D = D_MODEL
    x = nrm((BATCH, SEQ, D), 1.0)
    c = nrm((BATCH, D), 1.0)
    ada_w = nrm((DEPTH, D, N_SUB * 3 * D), 0.5 * D ** -0.5)
    ada_b = nrm((DEPTH, N_SUB * 3 * D), 0.02)
    norm_pre = 1.0 + nrm((DEPTH, N_SUB, D), 0.02)
    norm_post = 1.0 + nrm((DEPTH, N_SUB, D), 0.02)
    ffn_w_in = nrm((DEPTH, 2, D, 2 * D_FF), D ** -0.5)
    ffn_w_out = nrm((DEPTH, 2, D_FF, D), D_FF ** -0.5)
    ab_w_in = nrm((N_EVEN, D, POOL_WIDTH + 2 * SGU_WIDTH), D ** -0.5)
    pool_w = nrm((N_EVEN, N_POOL, POOL_GD, POOL_GD), POOL_GD ** -0.5)
    pool_scale = 1.0 + nrm((N_EVEN, POOL_WIDTH), 0.1)
    sgu_ln_g = 1.0 + nrm((N_EVEN, SGU_WIDTH), 0.02)
    sgu_ln_b = nrm((N_EVEN, SGU_WIDTH), 0.02)
    sgu_w = nrm((N_EVEN, SGU_HEADS, CHUNK, CHUNK), 0.5 * CHUNK ** -0.5)
    sgu_b = 1.0 + nrm((N_EVEN, SGU_HEADS, CHUNK), 0.02)
    ab_w_out = nrm((N_EVEN, POOL_WIDTH + SGU_WIDTH, D), (POOL_WIDTH + SGU_WIDTH) ** -0.5)
    ssm_w_in = nrm((N_ODD, D, SSM_WIDTH), D ** -0.5)
    n_idx = jnp.arange(SSM_STATE, dtype=F32)
    ssm_lam_re = -0.5 + nrm((N_ODD, SSM_GROUPS, SSM_STATE), 0.01)
    ssm_lam_im = jnp.pi * n_idx + nrm((N_ODD, SSM_GROUPS, SSM_STATE), 0.01)
    ssm_b_re = nrm((N_ODD, SSM_GROUPS, SSM_STATE, SSM_GROUP), (2 * SSM_GROUP) ** -0.5)
    ssm_b_im = nrm((N_ODD, SSM_GROUPS, SSM_STATE, SSM_GROUP), (2 * SSM_GROUP) ** -0.5)
    ssm_c_re = nrm((N_ODD, SSM_GROUPS, SSM_GROUP, SSM_STATE), SSM_STATE ** -0.5)
    ssm_c_im = nrm((N_ODD, SSM_GROUPS, SSM_GROUP, SSM_STATE), SSM_STATE ** -0.5)
    ssm_d = nrm((N_ODD, SSM_WIDTH), 1.0)
    ssm_log_dt = jax.random.uniform(next(ks), (N_ODD, SSM_GROUPS), F32,
                                    math.log(DT_MIN), math.log(DT_MAX))
    ssm_w_glu = nrm((N_ODD, SSM_WIDTH, 2 * D), SSM_WIDTH ** -0.5)
    return {'x': x, 'c': c, 'ada_w': ada_w, 'ada_b': ada_b,
            'norm_pre': norm_pre, 'norm_post': norm_post,
            'ffn_w_in': ffn_w_in, 'ffn_w_out': ffn_w_out,
            'ab_w_in': ab_w_in, 'pool_w': pool_w, 'pool_scale': pool_scale,
            'sgu_ln_g': sgu_ln_g, 'sgu_ln_b': sgu_ln_b, 'sgu_w': sgu_w, 'sgu_b': sgu_b,
            'ab_w_out': ab_w_out, 'ssm_w_in': ssm_w_in,
            'ssm_lam_re': ssm_lam_re, 'ssm_lam_im': ssm_lam_im,
            'ssm_b_re': ssm_b_re, 'ssm_b_im': ssm_b_im,
            'ssm_c_re': ssm_c_re, 'ssm_c_im': ssm_c_im,
            'ssm_d': ssm_d, 'ssm_log_dt': ssm_log_dt, 'ssm_w_glu': ssm_w_glu}


def _fwd_reference(x, c, ada_w, ada_b, norm_pre, norm_post, ffn_w_in, ffn_w_out,
              ab_w_in, pool_w, pool_scale, sgu_ln_g, sgu_ln_b, sgu_w, sgu_b, ab_w_out,
              ssm_w_in, ssm_lam_re, ssm_lam_im, ssm_b_re, ssm_b_im, ssm_c_re, ssm_c_im,
              ssm_d, ssm_log_dt, ssm_w_glu):
    cond = jax.nn.silu(c)
    for l in range(DEPTH):
        mod = (cond @ ada_w[l] + ada_b[l]).reshape(-1, N_SUB, 3, D_MODEL)
        i = l // 2

        x = sublayer(x, lambda h: swiglu(h, ffn_w_in[l, 0], ffn_w_out[l, 0]),
                     mod[:, 0], norm_pre[l, 0], norm_post[l, 0], FFN_RES_WEIGHT)

        if l % 2 == 0:
            def mix(h):
                z = h @ ab_w_in[i]
                y_a = pool_mixer(z[..., :POOL_WIDTH], pool_w[i], pool_scale[i])
                y_b = sgu_mixer(jax.nn.gelu(z[..., POOL_WIDTH:]), sgu_ln_g[i], sgu_ln_b[i],
                                sgu_w[i], sgu_b[i])
                return jnp.concatenate([y_a, y_b], axis=-1) @ ab_w_out[i]
        else:
            def mix(h):
                return s5_mixer(h @ ssm_w_in[i], ssm_lam_re[i], ssm_lam_im[i],
                                ssm_b_re[i], ssm_b_im[i], ssm_c_re[i], ssm_c_im[i],
                                ssm_d[i], ssm_log_dt[i], ssm_w_glu[i])
        x = sublayer(x, mix, mod[:, 1], norm_pre[l, 1], norm_post[l, 1], 1.0)

        x = sublayer(x, lambda h: swiglu(h, ffn_w_in[l, 1], ffn_w_out[l, 1]),
                     mod[:, 2], norm_pre[l, 2], norm_post[l, 2], FFN_RES_WEIGHT)
    return x


import jax as _jax
import jax.numpy as _jnp

TWIN_FORMAT = 'train_step'
FWD_PARAMS = ['x', 'c', 'ada_w', 'ada_b', 'norm_pre', 'norm_post', 'ffn_w_in', 'ffn_w_out', 'ab_w_in', 'pool_w', 'pool_scale', 'sgu_ln_g', 'sgu_ln_b', 'sgu_w', 'sgu_b', 'ab_w_out', 'ssm_w_in', 'ssm_lam_re', 'ssm_lam_im', 'ssm_b_re', 'ssm_b_im', 'ssm_c_re', 'ssm_c_im', 'ssm_d', 'ssm_log_dt', 'ssm_w_glu']
TWIN_WEIGHTS = ['ada_w', 'ada_b', 'norm_pre', 'norm_post', 'ffn_w_in', 'ffn_w_out', 'ab_w_in', 'pool_w', 'pool_scale', 'sgu_ln_g', 'sgu_ln_b', 'sgu_w', 'sgu_b', 'ab_w_out', 'ssm_w_in', 'ssm_lam_re', 'ssm_lam_im', 'ssm_b_re', 'ssm_b_im', 'ssm_c_re', 'ssm_c_im', 'ssm_d', 'ssm_log_dt', 'ssm_w_glu']
TWIN_DIFF_INPUT = 'x'
TWIN_INPUTS = ['x', 'c', 'ada_w', 'ada_b', 'norm_pre', 'norm_post', 'ffn_w_in', 'ffn_w_out', 'ab_w_in', 'pool_w', 'pool_scale', 'sgu_ln_g', 'sgu_ln_b', 'sgu_w', 'sgu_b', 'ab_w_out', 'ssm_w_in', 'ssm_lam_re', 'ssm_lam_im', 'ssm_b_re', 'ssm_b_im', 'ssm_c_re', 'ssm_c_im', 'ssm_d', 'ssm_log_dt', 'ssm_w_glu', 'loss_target', 'm_ada_w', 'm_ada_b', 'm_norm_pre', 'm_norm_post', 'm_ffn_w_in', 'm_ffn_w_out', 'm_ab_w_in', 'm_pool_w', 'm_pool_scale', 'm_sgu_ln_g', 'm_sgu_ln_b', 'm_sgu_w', 'm_sgu_b', 'm_ab_w_out', 'm_ssm_w_in', 'm_ssm_lam_re', 'm_ssm_lam_im', 'm_ssm_b_re', 'm_ssm_b_im', 'm_ssm_c_re', 'm_ssm_c_im', 'm_ssm_d', 'm_ssm_log_dt', 'm_ssm_w_glu', 'v_ada_w', 'v_ada_b', 'v_norm_pre', 'v_norm_post', 'v_ffn_w_in', 'v_ffn_w_out', 'v_ab_w_in', 'v_pool_w', 'v_pool_scale', 'v_sgu_ln_g', 'v_sgu_ln_b', 'v_sgu_w', 'v_sgu_b', 'v_ab_w_out', 'v_ssm_w_in', 'v_ssm_lam_re', 'v_ssm_lam_im', 'v_ssm_b_re', 'v_ssm_b_im', 'v_ssm_c_re', 'v_ssm_c_im', 'v_ssm_d', 'v_ssm_log_dt', 'v_ssm_w_glu']
TWIN_OUTPUTS = ['loss', 'grad_x', 'grad_ada_w', 'grad_ada_b', 'grad_norm_pre', 'grad_norm_post', 'grad_ffn_w_in', 'grad_ffn_w_out', 'grad_ab_w_in', 'grad_pool_w', 'grad_pool_scale', 'grad_sgu_ln_g', 'grad_sgu_ln_b', 'grad_sgu_w', 'grad_sgu_b', 'grad_ab_w_out', 'grad_ssm_w_in', 'grad_ssm_lam_re', 'grad_ssm_lam_im', 'grad_ssm_b_re', 'grad_ssm_b_im', 'grad_ssm_c_re', 'grad_ssm_c_im', 'grad_ssm_d', 'grad_ssm_log_dt', 'grad_ssm_w_glu', 'delta_ada_w', 'delta_ada_b', 'delta_norm_pre', 'delta_norm_post', 'delta_ffn_w_in', 'delta_ffn_w_out', 'delta_ab_w_in', 'delta_pool_w', 'delta_pool_scale', 'delta_sgu_ln_g', 'delta_sgu_ln_b', 'delta_sgu_w', 'delta_sgu_b', 'delta_ab_w_out', 'delta_ssm_w_in', 'delta_ssm_lam_re', 'delta_ssm_lam_im', 'delta_ssm_b_re', 'delta_ssm_b_im', 'delta_ssm_c_re', 'delta_ssm_c_im', 'delta_ssm_d', 'delta_ssm_log_dt', 'delta_ssm_w_glu', 'new_m_ada_w', 'new_m_ada_b', 'new_m_norm_pre', 'new_m_norm_post', 'new_m_ffn_w_in', 'new_m_ffn_w_out', 'new_m_ab_w_in', 'new_m_pool_w', 'new_m_pool_scale', 'new_m_sgu_ln_g', 'new_m_sgu_ln_b', 'new_m_sgu_w', 'new_m_sgu_b', 'new_m_ab_w_out', 'new_m_ssm_w_in', 'new_m_ssm_lam_re', 'new_m_ssm_lam_im', 'new_m_ssm_b_re', 'new_m_ssm_b_im', 'new_m_ssm_c_re', 'new_m_ssm_c_im', 'new_m_ssm_d', 'new_m_ssm_log_dt', 'new_m_ssm_w_glu', 'new_v_ada_w', 'new_v_ada_b', 'new_v_norm_pre', 'new_v_norm_post', 'new_v_ffn_w_in', 'new_v_ffn_w_out', 'new_v_ab_w_in', 'new_v_pool_w', 'new_v_pool_scale', 'new_v_sgu_ln_g', 'new_v_sgu_ln_b', 'new_v_sgu_w', 'new_v_sgu_b', 'new_v_ab_w_out', 'new_v_ssm_w_in', 'new_v_ssm_lam_re', 'new_v_ssm_lam_im', 'new_v_ssm_b_re', 'new_v_ssm_b_im', 'new_v_ssm_c_re', 'new_v_ssm_c_im', 'new_v_ssm_d', 'new_v_ssm_log_dt', 'new_v_ssm_w_glu']
TWIN_LEAF_KINDS = {'loss': 'loss', 'grad_x': 'grad_x', 'grad_ada_w': 'grad_w', 'grad_ada_b': 'grad_w', 'grad_norm_pre': 'grad_w', 'grad_norm_post': 'grad_w', 'grad_ffn_w_in': 'grad_w', 'grad_ffn_w_out': 'grad_w', 'grad_ab_w_in': 'grad_w', 'grad_pool_w': 'grad_w', 'grad_pool_scale': 'grad_w', 'grad_sgu_ln_g': 'grad_w', 'grad_sgu_ln_b': 'grad_w', 'grad_sgu_w': 'grad_w', 'grad_sgu_b': 'grad_w', 'grad_ab_w_out': 'grad_w', 'grad_ssm_w_in': 'grad_w', 'grad_ssm_lam_re': 'grad_w', 'grad_ssm_lam_im': 'grad_w', 'grad_ssm_b_re': 'grad_w', 'grad_ssm_b_im': 'grad_w', 'grad_ssm_c_re': 'grad_w', 'grad_ssm_c_im': 'grad_w', 'grad_ssm_d': 'grad_w', 'grad_ssm_log_dt': 'grad_w', 'grad_ssm_w_glu': 'grad_w', 'delta_ada_w': 'delta_w', 'delta_ada_b': 'delta_w', 'delta_norm_pre': 'delta_w', 'delta_norm_post': 'delta_w', 'delta_ffn_w_in': 'delta_w', 'delta_ffn_w_out': 'delta_w', 'delta_ab_w_in': 'delta_w', 'delta_pool_w': 'delta_w', 'delta_pool_scale': 'delta_w', 'delta_sgu_ln_g': 'delta_w', 'delta_sgu_ln_b': 'delta_w', 'delta_sgu_w': 'delta_w', 'delta_sgu_b': 'delta_w', 'delta_ab_w_out': 'delta_w', 'delta_ssm_w_in': 'delta_w', 'delta_ssm_lam_re': 'delta_w', 'delta_ssm_lam_im': 'delta_w', 'delta_ssm_b_re': 'delta_w', 'delta_ssm_b_im': 'delta_w', 'delta_ssm_c_re': 'delta_w', 'delta_ssm_c_im': 'delta_w', 'delta_ssm_d': 'delta_w', 'delta_ssm_log_dt': 'delta_w', 'delta_ssm_w_glu': 'delta_w', 'new_m_ada_w': 'new_m', 'new_m_ada_b': 'new_m', 'new_m_norm_pre': 'new_m', 'new_m_norm_post': 'new_m', 'new_m_ffn_w_in': 'new_m', 'new_m_ffn_w_out': 'new_m', 'new_m_ab_w_in': 'new_m', 'new_m_pool_w': 'new_m', 'new_m_pool_scale': 'new_m', 'new_m_sgu_ln_g': 'new_m', 'new_m_sgu_ln_b': 'new_m', 'new_m_sgu_w': 'new_m', 'new_m_sgu_b': 'new_m', 'new_m_ab_w_out': 'new_m', 'new_m_ssm_w_in': 'new_m', 'new_m_ssm_lam_re': 'new_m', 'new_m_ssm_lam_im': 'new_m', 'new_m_ssm_b_re': 'new_m', 'new_m_ssm_b_im': 'new_m', 'new_m_ssm_c_re': 'new_m', 'new_m_ssm_c_im': 'new_m', 'new_m_ssm_d': 'new_m', 'new_m_ssm_log_dt': 'new_m', 'new_m_ssm_w_glu': 'new_m', 'new_v_ada_w': 'new_v', 'new_v_ada_b': 'new_v', 'new_v_norm_pre': 'new_v', 'new_v_norm_post': 'new_v', 'new_v_ffn_w_in': 'new_v', 'new_v_ffn_w_out': 'new_v', 'new_v_ab_w_in': 'new_v', 'new_v_pool_w': 'new_v', 'new_v_pool_scale': 'new_v', 'new_v_sgu_ln_g': 'new_v', 'new_v_sgu_ln_b': 'new_v', 'new_v_sgu_w': 'new_v', 'new_v_sgu_b': 'new_v', 'new_v_ab_w_out': 'new_v', 'new_v_ssm_w_in': 'new_v', 'new_v_ssm_lam_re': 'new_v', 'new_v_ssm_lam_im': 'new_v', 'new_v_ssm_b_re': 'new_v', 'new_v_ssm_b_im': 'new_v', 'new_v_ssm_c_re': 'new_v', 'new_v_ssm_c_im': 'new_v', 'new_v_ssm_d': 'new_v', 'new_v_ssm_log_dt': 'new_v', 'new_v_ssm_w_glu': 'new_v'}


def _forward(args):
    return _fwd_reference(*[args[k] for k in FWD_PARAMS])


def _output_shape():
    out = _jax.eval_shape(lambda: _forward(_fwd_setup_inputs(0)))
    return out.shape, out.dtype

N_MICROBATCH = 1
ADAM_LR = 0.001
ADAM_B1 = 0.9
ADAM_B2 = 0.999
ADAM_EPS = 1e-08
ADAM_WD = 0.01
ADAM_STEP = 10
PER_EXAMPLE_BATCH_AXIS = {'x': 0, 'c': 0, 'loss_target': 0}
SHARED_INPUTS = []
_WEIGHT_DTYPES = {'ada_w': _jnp.float32, 'ada_b': _jnp.float32, 'norm_pre': _jnp.float32, 'norm_post': _jnp.float32, 'ffn_w_in': _jnp.float32, 'ffn_w_out': _jnp.float32, 'ab_w_in': _jnp.float32, 'pool_w': _jnp.float32, 'pool_scale': _jnp.float32, 'sgu_ln_g': _jnp.float32, 'sgu_ln_b': _jnp.float32, 'sgu_w': _jnp.float32, 'sgu_b': _jnp.float32, 'ab_w_out': _jnp.float32, 'ssm_w_in': _jnp.float32, 'ssm_lam_re': _jnp.float32, 'ssm_lam_im': _jnp.float32, 'ssm_b_re': _jnp.float32, 'ssm_b_im': _jnp.float32, 'ssm_c_re': _jnp.float32, 'ssm_c_im': _jnp.float32, 'ssm_d': _jnp.float32, 'ssm_log_dt': _jnp.float32, 'ssm_w_glu': _jnp.float32}
MOMENT_SCALE = {'ada_w': 5.719326e-01, 'ada_b': 1.045797e+00, 'norm_pre': 6.407003e-02, 'norm_post': 1.163401e+00, 'ffn_w_in': 2.405305e-02, 'ffn_w_out': 4.268536e-02, 'ab_w_in': 8.691621e-02, 'pool_w': 1.043155e-01, 'pool_scale': 1.073353e-01, 'sgu_ln_g': 2.067285e-02, 'sgu_ln_b': 2.053291e-02, 'sgu_w': 4.031022e-02, 'sgu_b': 6.915476e-02, 'ab_w_out': 1.678672e-01, 'ssm_w_in': 1.563326e-01, 'ssm_lam_re': 2.667077e-02, 'ssm_lam_im': 2.888524e-02, 'ssm_b_re': 1.577396e-02, 'ssm_b_im': 1.664835e-02, 'ssm_c_re': 2.235656e-02, 'ssm_c_im': 2.563248e-02, 'ssm_d': 3.787132e-01, 'ssm_log_dt': 4.190620e+00, 'ssm_w_glu': 2.509657e-01}


def _to_microbatches(a, axis):
    t = _jnp.moveaxis(a, axis, 0)
    t = t.reshape((N_MICROBATCH, t.shape[0] // N_MICROBATCH) + t.shape[1:])
    return _jnp.moveaxis(t, 1, axis + 1)


def setup_inputs(seed: int = 0) -> dict:
    inp = _fwd_setup_inputs(seed)
    key = _jax.random.fold_in(_jax.random.key(seed), 7919)
    shape, _ = _output_shape()
    out = dict(inp)
    out["loss_target"] = _jax.random.normal(_jax.random.fold_in(key, 0), shape, _jnp.float32)
    for i, name in enumerate(TWIN_WEIGHTS):
        w = inp[name].astype(_jnp.float32)
        if MOMENT_SCALE is None:
            s = _jnp.sqrt(_jnp.mean(_jnp.square(w)) + 1e-30)
        else:
            s = MOMENT_SCALE[name]
        km, kv = _jax.random.split(_jax.random.fold_in(key, i + 1))
        out[name] = w
        out["m_" + name] = s * _jax.random.normal(km, w.shape, _jnp.float32)
        out["v_" + name] = (s * s) * _jax.random.uniform(kv, w.shape, _jnp.float32, 0.5, 1.5)
    if N_MICROBATCH > 1:
        for name, axis in PER_EXAMPLE_BATCH_AXIS.items():
            out[name] = _to_microbatches(out[name], axis)
    return {'x': out['x'], 'c': out['c'], 'ada_w': out['ada_w'], 'ada_b': out['ada_b'], 'norm_pre': out['norm_pre'], 'norm_post': out['norm_post'], 'ffn_w_in': out['ffn_w_in'], 'ffn_w_out': out['ffn_w_out'], 'ab_w_in': out['ab_w_in'], 'pool_w': out['pool_w'], 'pool_scale': out['pool_scale'], 'sgu_ln_g': out['sgu_ln_g'], 'sgu_ln_b': out['sgu_ln_b'], 'sgu_w': out['sgu_w'], 'sgu_b': out['sgu_b'], 'ab_w_out': out['ab_w_out'], 'ssm_w_in': out['ssm_w_in'], 'ssm_lam_re': out['ssm_lam_re'], 'ssm_lam_im': out['ssm_lam_im'], 'ssm_b_re': out['ssm_b_re'], 'ssm_b_im': out['ssm_b_im'], 'ssm_c_re': out['ssm_c_re'], 'ssm_c_im': out['ssm_c_im'], 'ssm_d': out['ssm_d'], 'ssm_log_dt': out['ssm_log_dt'], 'ssm_w_glu': out['ssm_w_glu'], 'loss_target': out['loss_target'], 'm_ada_w': out['m_ada_w'], 'm_ada_b': out['m_ada_b'], 'm_norm_pre': out['m_norm_pre'], 'm_norm_post': out['m_norm_post'], 'm_ffn_w_in': out['m_ffn_w_in'], 'm_ffn_w_out': out['m_ffn_w_out'], 'm_ab_w_in': out['m_ab_w_in'], 'm_pool_w': out['m_pool_w'], 'm_pool_scale': out['m_pool_scale'], 'm_sgu_ln_g': out['m_sgu_ln_g'], 'm_sgu_ln_b': out['m_sgu_ln_b'], 'm_sgu_w': out['m_sgu_w'], 'm_sgu_b': out['m_sgu_b'], 'm_ab_w_out': out['m_ab_w_out'], 'm_ssm_w_in': out['m_ssm_w_in'], 'm_ssm_lam_re': out['m_ssm_lam_re'], 'm_ssm_lam_im': out['m_ssm_lam_im'], 'm_ssm_b_re': out['m_ssm_b_re'], 'm_ssm_b_im': out['m_ssm_b_im'], 'm_ssm_c_re': out['m_ssm_c_re'], 'm_ssm_c_im': out['m_ssm_c_im'], 'm_ssm_d': out['m_ssm_d'], 'm_ssm_log_dt': out['m_ssm_log_dt'], 'm_ssm_w_glu': out['m_ssm_w_glu'], 'v_ada_w': out['v_ada_w'], 'v_ada_b': out['v_ada_b'], 'v_norm_pre': out['v_norm_pre'], 'v_norm_post': out['v_norm_post'], 'v_ffn_w_in': out['v_ffn_w_in'], 'v_ffn_w_out': out['v_ffn_w_out'], 'v_ab_w_in': out['v_ab_w_in'], 'v_pool_w': out['v_pool_w'], 'v_pool_scale': out['v_pool_scale'], 'v_sgu_ln_g': out['v_sgu_ln_g'], 'v_sgu_ln_b': out['v_sgu_ln_b'], 'v_sgu_w': out['v_sgu_w'], 'v_sgu_b': out['v_sgu_b'], 'v_ab_w_out': out['v_ab_w_out'], 'v_ssm_w_in': out['v_ssm_w_in'], 'v_ssm_lam_re': out['v_ssm_lam_re'], 'v_ssm_lam_im': out['v_ssm_lam_im'], 'v_ssm_b_re': out['v_ssm_b_re'], 'v_ssm_b_im': out['v_ssm_b_im'], 'v_ssm_c_re': out['v_ssm_c_re'], 'v_ssm_c_im': out['v_ssm_c_im'], 'v_ssm_d': out['v_ssm_d'], 'v_ssm_log_dt': out['v_ssm_log_dt'], 'v_ssm_w_glu': out['v_ssm_w_glu']}


def _loss(weights, diff, rest, loss_target):
    with _jax.named_scope("forward"):
        args = {**rest, TWIN_DIFF_INPUT: diff, **{k: w.astype(_WEIGHT_DTYPES[k]) for k, w in weights.items()}}
        y = _forward(args)
    with _jax.named_scope("loss_head"):
        err = _jnp.square(y.astype(_jnp.float32) - loss_target)
        return 0.5 * _jnp.sum(_jnp.mean(err, axis=-1)) if err.ndim else 0.5 * err


def _adamw(w, g, m, v):
    m = ADAM_B1 * m + (1.0 - ADAM_B1) * g
    v = ADAM_B2 * v + (1.0 - ADAM_B2) * _jnp.square(g)
    m_hat = m / (1.0 - ADAM_B1 ** ADAM_STEP)
    v_hat = v / (1.0 - ADAM_B2 ** ADAM_STEP)
    delta = -ADAM_LR * (m_hat / (_jnp.sqrt(v_hat) + ADAM_EPS) + ADAM_WD * w)
    return delta, m, v


def reference(x, c, ada_w, ada_b, norm_pre, norm_post, ffn_w_in, ffn_w_out, ab_w_in, pool_w, pool_scale, sgu_ln_g, sgu_ln_b, sgu_w, sgu_b, ab_w_out, ssm_w_in, ssm_lam_re, ssm_lam_im, ssm_b_re, ssm_b_im, ssm_c_re, ssm_c_im, ssm_d, ssm_log_dt, ssm_w_glu, loss_target, m_ada_w, m_ada_b, m_norm_pre, m_norm_post, m_ffn_w_in, m_ffn_w_out, m_ab_w_in, m_pool_w, m_pool_scale, m_sgu_ln_g, m_sgu_ln_b, m_sgu_w, m_sgu_b, m_ab_w_out, m_ssm_w_in, m_ssm_lam_re, m_ssm_lam_im, m_ssm_b_re, m_ssm_b_im, m_ssm_c_re, m_ssm_c_im, m_ssm_d, m_ssm_log_dt, m_ssm_w_glu, v_ada_w, v_ada_b, v_norm_pre, v_norm_post, v_ffn_w_in, v_ffn_w_out, v_ab_w_in, v_pool_w, v_pool_scale, v_sgu_ln_g, v_sgu_ln_b, v_sgu_w, v_sgu_b, v_ab_w_out, v_ssm_w_in, v_ssm_lam_re, v_ssm_lam_im, v_ssm_b_re, v_ssm_b_im, v_ssm_c_re, v_ssm_c_im, v_ssm_d, v_ssm_log_dt, v_ssm_w_glu):
    given = dict(x=x, c=c, ada_w=ada_w, ada_b=ada_b, norm_pre=norm_pre, norm_post=norm_post, ffn_w_in=ffn_w_in, ffn_w_out=ffn_w_out, ab_w_in=ab_w_in, pool_w=pool_w, pool_scale=pool_scale, sgu_ln_g=sgu_ln_g, sgu_ln_b=sgu_ln_b, sgu_w=sgu_w, sgu_b=sgu_b, ab_w_out=ab_w_out, ssm_w_in=ssm_w_in, ssm_lam_re=ssm_lam_re, ssm_lam_im=ssm_lam_im, ssm_b_re=ssm_b_re, ssm_b_im=ssm_b_im, ssm_c_re=ssm_c_re, ssm_c_im=ssm_c_im, ssm_d=ssm_d, ssm_log_dt=ssm_log_dt, ssm_w_glu=ssm_w_glu, loss_target=loss_target, m_ada_w=m_ada_w, m_ada_b=m_ada_b, m_norm_pre=m_norm_pre, m_norm_post=m_norm_post, m_ffn_w_in=m_ffn_w_in, m_ffn_w_out=m_ffn_w_out, m_ab_w_in=m_ab_w_in, m_pool_w=m_pool_w, m_pool_scale=m_pool_scale, m_sgu_ln_g=m_sgu_ln_g, m_sgu_ln_b=m_sgu_ln_b, m_sgu_w=m_sgu_w, m_sgu_b=m_sgu_b, m_ab_w_out=m_ab_w_out, m_ssm_w_in=m_ssm_w_in, m_ssm_lam_re=m_ssm_lam_re, m_ssm_lam_im=m_ssm_lam_im, m_ssm_b_re=m_ssm_b_re, m_ssm_b_im=m_ssm_b_im, m_ssm_c_re=m_ssm_c_re, m_ssm_c_im=m_ssm_c_im, m_ssm_d=m_ssm_d, m_ssm_log_dt=m_ssm_log_dt, m_ssm_w_glu=m_ssm_w_glu, v_ada_w=v_ada_w, v_ada_b=v_ada_b, v_norm_pre=v_norm_pre, v_norm_post=v_norm_post, v_ffn_w_in=v_ffn_w_in, v_ffn_w_out=v_ffn_w_out, v_ab_w_in=v_ab_w_in, v_pool_w=v_pool_w, v_pool_scale=v_pool_scale, v_sgu_ln_g=v_sgu_ln_g, v_sgu_ln_b=v_sgu_ln_b, v_sgu_w=v_sgu_w, v_sgu_b=v_sgu_b, v_ab_w_out=v_ab_w_out, v_ssm_w_in=v_ssm_w_in, v_ssm_lam_re=v_ssm_lam_re, v_ssm_lam_im=v_ssm_lam_im, v_ssm_b_re=v_ssm_b_re, v_ssm_b_im=v_ssm_b_im, v_ssm_c_re=v_ssm_c_re, v_ssm_c_im=v_ssm_c_im, v_ssm_d=v_ssm_d, v_ssm_log_dt=v_ssm_log_dt, v_ssm_w_glu=v_ssm_w_glu)
    weights = {n: given[n] for n in TWIN_WEIGHTS}
    shared = {n: given[n] for n in SHARED_INPUTS}
    per_example = {n: given[n] for n in ['x', 'c']}
    grad_fn = _jax.value_and_grad(_loss, argnums=(0, 1))

    def one_microbatch(ex, loss_target):
        ex = dict(ex)
        diff = ex.pop(TWIN_DIFF_INPUT)
        return grad_fn(weights, diff, {**shared, **ex}, loss_target)

    if N_MICROBATCH == 1:
        loss, (grad_w, grad_x) = one_microbatch(per_example, given["loss_target"])
    else:
        def body(carry, xs):
            loss_sum, grad_sum = carry
            l_k, (gw_k, gx_k) = one_microbatch(xs[0], xs[1])
            with _jax.named_scope("update"):
                return (loss_sum + l_k, _jax.tree.map(_jnp.add, grad_sum, gw_k)), gx_k

        init = (_jnp.zeros((), _jnp.float32), _jax.tree.map(_jnp.zeros_like, weights))
        (loss, grad_w), grad_x = _jax.lax.scan(body, init, (per_example, given["loss_target"]))
    with _jax.named_scope("update"):
        delta_w, new_m, new_v = {}, {}, {}
        for n in TWIN_WEIGHTS:
            delta_w[n], new_m[n], new_v[n] = _adamw(weights[n], grad_w[n], given["m_" + n], given["v_" + n])
    return (loss, grad_x, *[grad_w[n] for n in TWIN_WEIGHTS], *[delta_w[n] for n in TWIN_WEIGHTS],
            *[new_m[n] for n in TWIN_WEIGHTS], *[new_v[n] for n in TWIN_WEIGHTS])
```

```python
import functools
import math

import jax
import jax.numpy as jnp
from jax import lax
from jax.experimental import pallas as pl
from jax.experimental.pallas import tpu as pltpu

F32 = jnp.float32
BF16 = jnp.bfloat16
MESH = pl.DeviceIdType.MESH
HIGHEST = lax.Precision.HIGHEST

N_DEV = 8
D = 1024
D_FF = 2816
FSH = 2 * D_FF // N_DEV
EPS = 1e-6
POOL_WINDOWS = (2, 4, 8, 16)
HD = 128
NH = 4
SSM_G, SSM_P, SSM_N = 64, 64, 16
SSM_GB = 16
SSM_NB = SSM_G // SSM_GB
SSM_L = SSM_G * SSM_P
LR, B1, B2, ADAM_EPS, WD, STEP = 0.001, 0.9, 0.999, 1e-08, 0.01, 10
GELU_C = math.sqrt(2.0 / math.pi)
VMEM_LIMIT_BYTES = 48 * 1024 * 1024
LANE = 128


def _pc(body, name, grid, in_specs, out_specs, out_shape, scratch=()):
    return pl.pallas_call(
        body, name=name, grid=grid, in_specs=in_specs, out_specs=out_specs, out_shape=out_shape,
        scratch_shapes=list(scratch),
        compiler_params=pltpu.CompilerParams(dimension_semantics=("arbitrary",) * len(grid),
                                             vmem_limit_bytes=VMEM_LIMIT_BYTES))


def _sds(shape, dtype=F32):
    return jax.ShapeDtypeStruct(tuple(shape), dtype)


def _bf(v):
    return v if v.dtype == BF16 else v.astype(BF16)


def _row_spec(ts, width, col=0):
    return pl.BlockSpec((ts, width), lambda t, _c=col: (t, _c))


def _vec_spec(width, col=0):
    return pl.BlockSpec((1, width), lambda t, _c=col: (0, _c))


def _mm(name, a, b, contract, grid, a_spec, b_spec, o_spec, out_shape, acc_axis=None):
    dn = (contract, ((), ()))

    def body(a_ref, b_ref, o_ref):
        r = lax.dot_general(_bf(a_ref[...]), _bf(b_ref[...]), dn, preferred_element_type=F32)
        if acc_axis is None:
            o_ref[...] = r.astype(o_ref.dtype)
        else:
            k = pl.program_id(acc_axis)

            @pl.when(k == 0)
            def _():
                o_ref[...] = r

            @pl.when(k > 0)
            def _():
                o_ref[...] += r

    return _pc(body, name, grid, [a_spec, b_spec], o_spec, out_shape)(a, b)


def _tile(s):
    return min(s, 1024)


def _div_tile(n, cap=1024):
    t = min(n, cap) // LANE * LANE
    while n % t:
        t -= LANE
    return t


def _mm_nn(name, a, b, out_dtype=F32):
    s, k = a.shape
    n = b.shape[1]
    ts, tn = _tile(s), _div_tile(n)
    return _mm(name, a, b, ((1,), (0,)), (n // tn, s // ts),
               pl.BlockSpec((ts, k), lambda j, t: (t, 0)), pl.BlockSpec((k, tn), lambda j, t: (0, j)),
               pl.BlockSpec((ts, tn), lambda j, t: (t, j)), _sds((s, n), out_dtype))


def _mm_nt(name, a, b, out_dtype=F32):
    s, n = a.shape
    k = b.shape[0]
    ts, tk = _tile(s), min(k, 1024)
    return _mm(name, a, b, ((1,), (1,)), (k // tk, s // ts),
               pl.BlockSpec((ts, n), lambda j, t: (t, 0)), pl.BlockSpec((tk, n), lambda j, t: (j, 0)),
               pl.BlockSpec((ts, tk), lambda j, t: (t, j)), _sds((s, k), out_dtype))


def _mm_tn(name, a, b, out_dtype=F32, tm=512, tn=512):
    s, m = a.shape
    n = b.shape[1]
    tm, tn = min(m, tm), min(n, tn)
    return _mm(name, a, b, ((0,), (0,)), (m // tm, n // tn),
               pl.BlockSpec((s, tm), lambda i, j: (0, i)), pl.BlockSpec((s, tn), lambda i, j: (0, j)),
               pl.BlockSpec((tm, tn), lambda i, j: (i, j)), _sds((m, n), out_dtype))


def _rstd(v):
    return lax.rsqrt(jnp.mean(v * v, axis=-1, keepdims=True) + EPS)


def _prenorm_fwd(x, g, scale, shift):
    s = x.shape[0]
    ts = min(s, 512)

    def body(x_ref, g_ref, sc_ref, sh_ref, h_ref):
        xv = x_ref[...]
        h_ref[...] = ((xv * _rstd(xv) * g_ref[...]) * (1.0 + sc_ref[...]) + sh_ref[...]).astype(BF16)

    return _pc(body, "prenorm_fwd", (s // ts,), [_row_spec(ts, D)] + [_vec_spec(D)] * 3, _row_spec(ts, D),
               _sds((s, D), BF16))(x, g, scale, shift)


def _postnorm_fwd(x, f, g, gate, rw):
    s = x.shape[0]
    ts = min(s, 512)

    def body(x_ref, f_ref, g_ref, gt_ref, o_ref):
        fv = f_ref[...]
        o_ref[...] = x_ref[...] + (rw * gt_ref[...]) * (fv * _rstd(fv) * g_ref[...])

    return _pc(body, "postnorm_fwd", (s // ts,), [_row_spec(ts, D)] * 2 + [_vec_spec(D)] * 2, _row_spec(ts, D),
               _sds((s, D)))(x, f, g, gate)


def _acc(ref, first, v):
    @pl.when(first)
    def _():
        ref[...] = v

    @pl.when(jnp.logical_not(first))
    def _():
        ref[...] += v


def _colsum(v):
    return jnp.sum(v, axis=0, keepdims=True)


def _postnorm_bwd(dout, f, g, gate, rw):
    s = dout.shape[0]
    ts = min(s, 512)

    def body(do_ref, f_ref, g_ref, gt_ref, df_ref, dgate_ref, dg_ref):
        first = pl.program_id(0) == 0
        do, fv, gv = do_ref[...], f_ref[...], g_ref[...]
        r = _rstd(fv)
        fn = fv * r
        _acc(dgate_ref, first, rw * _colsum(do * (fn * gv)))
        dy = (rw * gt_ref[...]) * do
        _acc(dg_ref, first, _colsum(dy * fn))
        dfn = dy * gv
        df_ref[...] = (r * (dfn - fn * jnp.mean(dfn * fn, axis=-1, keepdims=True))).astype(BF16)

    return _pc(body, "postnorm_bwd", (s // ts,), [_row_spec(ts, D)] * 2 + [_vec_spec(D)] * 2,
               [_row_spec(ts, D), _vec_spec(D), _vec_spec(D)],
               [_sds((s, D), BF16), _sds((1, D)), _sds((1, D))])(dout, f, g, gate)


def _prenorm_bwd(dout, dh, x, g, scale):
    s = dout.shape[0]
    ts = min(s, 512)

    def body(do_ref, dh_ref, x_ref, g_ref, sc_ref, dx_ref, dsh_ref, dsc_ref, dg_ref):
        first = pl.program_id(0) == 0
        dhv, xv, gv = dh_ref[...], x_ref[...], g_ref[...]
        r = _rstd(xv)
        xn = xv * r
        _acc(dsh_ref, first, _colsum(dhv))
        _acc(dsc_ref, first, _colsum(dhv * (xn * gv)))
        dhp = dhv * (1.0 + sc_ref[...])
        _acc(dg_ref, first, _colsum(dhp * xn))
        dxn = dhp * gv
        dx_ref[...] = do_ref[...] + r * (dxn - xn * jnp.mean(dxn * xn, axis=-1, keepdims=True))

    return _pc(body, "prenorm_bwd", (s // ts,), [_row_spec(ts, D)] * 3 + [_vec_spec(D)] * 2,
               [_row_spec(ts, D)] + [_vec_spec(D)] * 3,
               [_sds((s, D))] + [_sds((1, D))] * 3)(dout, dh, x, g, scale)


def _loss_fwd_bwd(y, tgt):
    s = y.shape[0]
    ts = min(s, 512)
    nt = s // ts

    def body(y_ref, t_ref, loss_ref, dy_ref, acc_ref):
        t = pl.program_id(0)
        e = y_ref[...] - t_ref[...]
        dy_ref[...] = e * (1.0 / D)
        _acc(acc_ref, t == 0, _colsum(e * e))

        @pl.when(t == nt - 1)
        def _():
            loss_ref[...] = jnp.full((1, LANE), 0.5 / D, F32) * jnp.sum(acc_ref[...])

    return _pc(body, "loss", (nt,), [_row_spec(ts, D)] * 2,
               [pl.BlockSpec((1, LANE), lambda t: (0, 0)), _row_spec(ts, D)],
               [_sds((1, LANE)), _sds((s, D))], scratch=[pltpu.VMEM((1, D), F32)])(y, tgt)


def _sigmoid(v):
    return 1.0 / (1.0 + jnp.exp(-v))


def _swiglu_fwd(z):
    _, s, _ = z.shape
    ts = min(s, 512)
    z4 = z.reshape(2, 4, s, FSH)

    def body(z_ref, o_ref):
        a, b = z_ref[0], z_ref[1]
        o_ref[...] = (a * _sigmoid(a) * b).astype(BF16)

    return _pc(body, "swiglu_fwd", (4, s // ts), [pl.BlockSpec((2, None, ts, FSH), lambda k, t: (0, k, t, 0))],
               pl.BlockSpec((None, ts, FSH), lambda k, t: (k, t, 0)), _sds((4, s, FSH), BF16))(z4)


def _swiglu_bwd(z, dact):
    _, s, _ = z.shape
    ts = min(s, 512)
    z4 = z.reshape(2, 4, s, FSH)

    def body(z_ref, d_ref, o_ref):
        a, b, d = z_ref[0], z_ref[1], d_ref[...]
        sg = _sigmoid(a)
        o_ref[0] = (d * b * (sg * (1.0 + a * (1.0 - sg)))).astype(BF16)
        o_ref[1] = (d * (a * sg)).astype(BF16)

    spec = pl.BlockSpec((2, None, ts, FSH), lambda k, t: (0, k, t, 0))
    out = _pc(body, "swiglu_bwd", (4, s // ts), [spec, pl.BlockSpec((None, ts, FSH), lambda k, t: (k, t, 0))],
              spec, _sds((2, 4, s, FSH), BF16))(z4, dact)
    return out.reshape(8, s, FSH)


def _ffn_fwd(h, win, wout, lk):
    s = h.shape[0]
    ts = _tile(s)
    z = _mm("ffn_in", h, win, ((1,), (0,)), (N_DEV, s // ts),
            pl.BlockSpec((ts, D), lambda j, t: (t, 0)), pl.BlockSpec((None, None, D, FSH), lambda j, t: (lk, j, 0, 0)),
            pl.BlockSpec((None, ts, FSH), lambda j, t: (j, t, 0)), _sds((N_DEV, s, FSH)))
    act = _swiglu_fwd(z)
    f = _mm("ffn_out", act, wout, ((1,), (0,)), (s // ts, 4),
            pl.BlockSpec((None, ts, FSH), lambda t, k: (k, t, 0)), pl.BlockSpec((None, FSH, D), lambda t, k: (lk, k, 0)),
            pl.BlockSpec((ts, D), lambda t, k: (t, 0)), _sds((s, D)), acc_axis=1)
    return f, (h, z, act)


def _ffn_bwd(df, saved, win, wout, lk):
    h, z, act = saved
    s = h.shape[0]
    ts = _tile(s)
    dact = _mm("ffn_out_dx", df, wout, ((1,), (1,)), (4, s // ts),
               pl.BlockSpec((ts, D), lambda k, t: (t, 0)), pl.BlockSpec((None, FSH, D), lambda k, t: (lk, k, 0)),
               pl.BlockSpec((None, ts, FSH), lambda k, t: (k, t, 0)), _sds((4, s, FSH)))
    dwout = _mm("ffn_out_dw", act, df, ((0,), (0,)), (4, 2),
                pl.BlockSpec((None, s, FSH), lambda k, j: (k, 0, 0)), pl.BlockSpec((s, D // 2), lambda k, j: (0, j)),
                pl.BlockSpec((None, FSH, D // 2), lambda k, j: (k, 0, j)), _sds((4, FSH, D), BF16))
    dz = _swiglu_bwd(z, dact)
    dh = _mm("ffn_in_dx", dz, win, ((1,), (1,)), (s // ts, N_DEV),
             pl.BlockSpec((None, ts, FSH), lambda t, j: (j, t, 0)), pl.BlockSpec((None, None, D, FSH), lambda t, j: (lk, j, 0, 0)),
             pl.BlockSpec((ts, D), lambda t, j: (t, 0)), _sds((s, D)), acc_axis=1)
    dwin = _mm("ffn_in_dw", h, dz, ((0,), (0,)), (N_DEV, 2),
               pl.BlockSpec((s, D // 2), lambda j, i: (0, i)), pl.BlockSpec((None, s, FSH), lambda j, i: (j, 0, 0)),
               pl.BlockSpec((None, D // 2, FSH), lambda j, i: (j, i, 0)), _sds((N_DEV, D, FSH), BF16))
    return dh, dwin, dwout.reshape(N_DEV, D_FF // N_DEV, D)


def _shift_rows(v, k, row, s, back):
    if back:
        return jnp.where(row < s - k, pltpu.roll(v, s - k, 0), 0.0)
    return jnp.where(row >= k, pltpu.roll(v, k, 0), 0.0)


def _window_sum(v, w, row, s, back):
    k = 1
    while k < w:
        v = v + _shift_rows(v, k, row, s, back)
        k *= 2
    return v


def _pool_fwd(z, pool_w, pool_scale):
    s = z.shape[0]

    def body(z_ref, w_ref, sc_ref, y_ref, d_ref):
        row = lax.broadcasted_iota(jnp.int32, (s, HD), 0)
        for g, w in enumerate(POOL_WINDOWS):
            sl = slice(g * HD, (g + 1) * HD)
            a = z_ref[:, sl]
            cnt = jnp.minimum(row + 1, w).astype(F32)
            d = (_window_sum(a, w, row, s, False) / cnt - a).astype(BF16)
            d_ref[:, sl] = d
            y = jnp.dot(d, _bf(w_ref[g]), preferred_element_type=F32)
            y_ref[:, sl] = (y * sc_ref[:, sl]).astype(BF16)

    return _pc(body, "pool_fwd", (1,),
               [pl.BlockSpec((s, NH * HD), lambda i: (0, 0)), pl.BlockSpec((NH, HD, HD), lambda i: (0, 0, 0)),
                pl.BlockSpec((1, NH * HD), lambda i: (0, 0))],
               [pl.BlockSpec((s, NH * HD), lambda i: (0, 0))] * 2,
               [_sds((s, NH * HD), BF16)] * 2)(z, pool_w, pool_scale)


def _pool_bwd(dy, d, pool_w, pool_scale):
    s = dy.shape[0]

    def body(dy_ref, d_ref, w_ref, sc_ref, dz_ref, dw_ref, dsc_ref):
        row = lax.broadcasted_iota(jnp.int32, (s, HD), 0)
        for g, w in enumerate(POOL_WINDOWS):
            sl = slice(g * HD, (g + 1) * HD)
            dyg, dg, wg = dy_ref[:, sl], d_ref[:, sl], _bf(w_ref[g])
            yraw = jnp.dot(dg, wg, preferred_element_type=F32)
            dsc_ref[:, sl] = _colsum(dyg * yraw)
            dyr = _bf(dyg * sc_ref[:, sl])
            dw_ref[g] = lax.dot_general(dg, dyr, (((0,), (0,)), ((), ())), preferred_element_type=F32)
            dd = lax.dot_general(dyr, wg, (((1,), (1,)), ((), ())), preferred_element_type=F32)
            cnt = jnp.minimum(row + 1, w).astype(F32)
            dz_ref[:, sl] = (_window_sum(dd / cnt, w, row, s, True) - dd).astype(BF16)

    return _pc(body, "pool_bwd", (1,),
               [pl.BlockSpec((s, NH * HD), lambda i: (0, 0)), pl.BlockSpec((s, NH * HD), lambda i: (0, 0)),
                pl.BlockSpec((NH, HD, HD), lambda i: (0, 0, 0)), pl.BlockSpec((1, NH * HD), lambda i: (0, 0))],
               [pl.BlockSpec((s, NH * HD), lambda i: (0, 0)), pl.BlockSpec((NH, HD, HD), lambda i: (0, 0, 0)),
                pl.BlockSpec((1, NH * HD), lambda i: (0, 0))],
               [_sds((s, NH * HD), BF16), _sds((NH, HD, HD)), _sds((1, NH * HD))])(dy, d, pool_w, pool_scale)


def _gelu(v):
    return 0.5 * v * (1.0 + jnp.tanh(GELU_C * (v + 0.044715 * (v * v * v))))


def _gelu_grad(v):
    t = jnp.tanh(GELU_C * (v + 0.044715 * (v * v * v)))
    return 0.5 * (1.0 + t) + 0.5 * v * (1.0 - t * t) * (GELU_C * (1.0 + 3.0 * 0.044715 * (v * v)))


def _causal_mask():
    return lax.broadcasted_iota(jnp.int32, (HD, HD), 0) >= lax.broadcasted_iota(jnp.int32, (HD, HD), 1)


def _sgu_specs():
    w = NH * HD
    return [pl.BlockSpec((HD, w), lambda c: (c, 1)), pl.BlockSpec((HD, w), lambda c: (c, 2)),
            pl.BlockSpec((1, w), lambda c: (0, 0)), pl.BlockSpec((1, w), lambda c: (0, 0)),
            pl.BlockSpec((NH, HD, HD), lambda c: (0, 0, 0)), pl.BlockSpec((HD, LANE), lambda c: (0, 0))]


def _sgu_head(v, lng_ref, lnb_ref, w_ref, h):
    sl = slice(h * HD, (h + 1) * HD)
    vh = v[:, sl]
    xc = vh - jnp.mean(vh, axis=-1, keepdims=True)
    rs = lax.rsqrt(jnp.mean(xc * xc, axis=-1, keepdims=True) + EPS)
    vhat = xc * rs
    vn = _bf(vhat * lng_ref[:, sl] + lnb_ref[:, sl])
    wc = _bf(jnp.where(_causal_mask(), w_ref[h], 0.0))
    return sl, rs, vhat, vn, wc


def _sgu_fwd(z, ln_g, ln_b, sgu_w, sgu_bt):
    s = z.shape[0]

    def body(zu_ref, zv_ref, lng_ref, lnb_ref, w_ref, bt_ref, y_ref):
        u, v = _gelu(zu_ref[...]), _gelu(zv_ref[...])
        for h in range(NH):
            sl, _, _, vn, wc = _sgu_head(v, lng_ref, lnb_ref, w_ref, h)
            sp = jnp.dot(wc, vn, preferred_element_type=F32) + bt_ref[:, h:h + 1]
            y_ref[:, sl] = (u[:, sl] * sp).astype(BF16)

    return _pc(body, "sgu_fwd", (s // HD,), _sgu_specs(), pl.BlockSpec((HD, NH * HD), lambda c: (c, 0)),
               _sds((s, NH * HD), BF16))(z, z, ln_g, ln_b, sgu_w, sgu_bt)


def _sgu_bwd(z, dy, ln_g, ln_b, sgu_w, sgu_bt, head_sum):
    s = z.shape[0]
    w = NH * HD
    nc = s // HD

    def body(zu_ref, zv_ref, lng_ref, lnb_ref, w_ref, bt_ref, dy_ref, hs_ref,
             dzu_ref, dzv_ref, dlng_ref, dlnb_ref, dw_ref, dbt_ref, dsacc_ref):
        c = pl.program_id(0)
        first = c == 0
        zu, zv = zu_ref[...], zv_ref[...]
        u, v = _gelu(zu), _gelu(zv)
        dyv = dy_ref[...]
        gu, gv = _gelu_grad(zu), _gelu_grad(zv)
        ds = dyv * u
        _acc(dsacc_ref, first, ds)
        for h in range(NH):
            sl, rs, vhat, vn, wc = _sgu_head(v, lng_ref, lnb_ref, w_ref, h)
            sp = jnp.dot(wc, vn, preferred_element_type=F32) + bt_ref[:, h:h + 1]
            dzu_ref[:, sl] = (dyv[:, sl] * sp * gu[:, sl]).astype(BF16)
            dsh = _bf(ds[:, sl])
            dwh = lax.dot_general(dsh, vn, (((1,), (1,)), ((), ())), preferred_element_type=F32)
            dwh = jnp.where(_causal_mask(), dwh, 0.0)

            @pl.when(first)
            def _():
                dw_ref[h] = dwh

            @pl.when(jnp.logical_not(first))
            def _():
                dw_ref[h] += dwh

            dvn = lax.dot_general(wc, dsh, (((0,), (0,)), ((), ())), preferred_element_type=F32)
            g_col = _colsum(dvn * vhat)
            b_col = _colsum(dvn)

            @pl.when(first)
            def _():
                dlng_ref[:, sl] = g_col
                dlnb_ref[:, sl] = b_col

            @pl.when(jnp.logical_not(first))
            def _():
                dlng_ref[:, sl] += g_col
                dlnb_ref[:, sl] += b_col

            dvh = dvn * lng_ref[:, sl]
            dv = rs * (dvh - jnp.mean(dvh, axis=-1, keepdims=True) - vhat * jnp.mean(dvh * vhat, axis=-1, keepdims=True))
            dzv_ref[:, sl] = (dv * gv[:, sl]).astype(BF16)

        @pl.when(c == nc - 1)
        def _():
            dbt_ref[...] = jnp.dot(dsacc_ref[...], hs_ref[...], preferred_element_type=F32, precision=HIGHEST)

    outs = _pc(body, "sgu_bwd", (nc,),
               _sgu_specs() + [pl.BlockSpec((HD, w), lambda c: (c, 1)), pl.BlockSpec((w, LANE), lambda c: (0, 0))],
               [pl.BlockSpec((HD, w), lambda c: (c, 0))] * 2 + [pl.BlockSpec((1, w), lambda c: (0, 0))] * 2
               + [pl.BlockSpec((NH, HD, HD), lambda c: (0, 0, 0)), pl.BlockSpec((HD, LANE), lambda c: (0, 0))],
               [_sds((s, w), BF16)] * 2 + [_sds((1, w))] * 2 + [_sds((NH, HD, HD)), _sds((HD, LANE))],
               scratch=[pltpu.VMEM((HD, w), F32)])(z, z, ln_g, ln_b, sgu_w, sgu_bt, dy, head_sum)
    return outs


def _cmul(ar, ai, br, bi):
    return ar * br - ai * bi, ar * bi + ai * br


def _ssm_prep(lam_re, lam_im, lam_re_rep, lam_im_rep, log_dt, b_re, b_im):
    def disc(lr, li, dt):
        mag = jnp.exp(lr * dt)
        return mag * jnp.cos(li * dt), mag * jnp.sin(li * dt)

    def body(lr_ref, li_ref, lrr_ref, lir_ref, ldt_ref, br_ref, bi_ref, or_ref, oi_ref, bbr_ref, bbi_ref):
        dt = jnp.exp(ldt_ref[...])
        or_ref[...], oi_ref[...] = disc(lr_ref[...], li_ref[...], dt)
        lr, li = lrr_ref[...], lir_ref[...]
        er, ei = disc(lr, li, dt)
        den = lr * lr + li * li
        kr = ((er - 1.0) * lr + ei * li) / den
        ki = (ei * lr - (er - 1.0) * li) / den
        bbr_ref[...], bbi_ref[...] = _cmul(kr, ki, br_ref[...], bi_ref[...])

    small = pl.BlockSpec((SSM_G, SSM_P), lambda i: (0, 0))
    wide = pl.BlockSpec((SSM_G, SSM_P * SSM_N), lambda i: (0, 0))
    col = pl.BlockSpec((SSM_G, 1), lambda i: (0, 0))
    return _pc(body, "ssm_prep", (1,), [small, small, wide, wide, col, wide, wide], [small, small, wide, wide],
               [_sds((SSM_G, SSM_P))] * 2 + [_sds((SSM_G, SSM_P * SSM_N))] * 2)(
        lam_re, lam_im, lam_re_rep, lam_im_rep, log_dt, b_re, b_im)


def _ssm_param_bwd(g_lam_re, g_lam_im, g_bb_re, g_bb_im, lam_re, lam_im, lam_re_rep, lam_im_rep, log_dt, b_re, b_im, seg):
    def body(glr_ref, gli_ref, gbr_ref, gbi_ref, lr_ref, li_ref, lrr_ref, lir_ref, ldt_ref, br_ref, bi_ref, seg_ref,
             dlr_ref, dli_ref, ddt_ref, dbr_ref, dbi_ref):
        dt = jnp.exp(ldt_ref[...])
        lr, li = lrr_ref[...], lir_ref[...]
        mag = jnp.exp(lr * dt)
        er, ei = mag * jnp.cos(li * dt), mag * jnp.sin(li * dt)
        den = lr * lr + li * li
        kr = ((er - 1.0) * lr + ei * li) / den
        ki = (ei * lr - (er - 1.0) * li) / den
        gbr, gbi = gbr_ref[...], gbi_ref[...]
        dbr_ref[...], dbi_ref[...] = _cmul(kr, -ki, gbr, gbi)
        tr, ti = _cmul(br_ref[...], -bi_ref[...], gbr, gbi)
        gkr = jnp.dot(tr, seg_ref[...], preferred_element_type=F32, precision=HIGHEST)
        gki = jnp.dot(ti, seg_ref[...], preferred_element_type=F32, precision=HIGHEST)
        lr, li = lr_ref[...], li_ref[...]
        mag = jnp.exp(lr * dt)
        er, ei = mag * jnp.cos(li * dt), mag * jnp.sin(li * dt)
        den = lr * lr + li * li
        ir, ii = lr / den, -li / den
        kr, ki = _cmul(er - 1.0, ei, ir, ii)
        ar, ai = _cmul(ir, -ii, gkr, gki)
        glr, gli = glr_ref[...] + ar, gli_ref[...] + ai
        qr, qi = _cmul(kr, ki, ir, ii)
        g1r, g1i = _cmul(-qr, qi, gkr, gki)
        g2r, g2i = _cmul(dt * er, -dt * ei, glr, gli)
        dlr_ref[...] = g1r + g2r
        dli_ref[...] = g1i + g2i
        wr, wi = _cmul(lr, li, er, ei)
        g_dt = jnp.sum(wr * glr + wi * gli, axis=-1, keepdims=True)
        ddt_ref[...] = jnp.broadcast_to(dt * g_dt, (SSM_G, LANE))

    small = pl.BlockSpec((SSM_G, SSM_P), lambda i: (0, 0))
    wide = pl.BlockSpec((SSM_G, SSM_P * SSM_N), lambda i: (0, 0))
    col = pl.BlockSpec((SSM_G, 1), lambda i: (0, 0))
    segs = pl.BlockSpec((SSM_P * SSM_N, SSM_P), lambda i: (0, 0))
    return _pc(body, "ssm_param_bwd", (1,), [small, small, wide, wide, small, small, wide, wide, col, wide, wide, segs],
               [small, small, pl.BlockSpec((SSM_G, LANE), lambda i: (0, 0)), wide, wide],
               [_sds((SSM_G, SSM_P))] * 2 + [_sds((SSM_G, LANE))] + [_sds((SSM_G, SSM_P * SSM_N))] * 2)(
        g_lam_re, g_lam_im, g_bb_re, g_bb_im, lam_re, lam_im, lam_re_rep, lam_im_rep, log_dt, b_re, b_im, seg)


SCAN_LANES = 256
SCAN_ROWS = 8


def _ssm_scan(b_re, b_im, lam_re, lam_im, reverse):
    s = b_re.shape[0]
    nt = s // SCAN_ROWS
    ln, rows = SCAN_LANES, SCAN_ROWS

    def body(lr_ref, li_ref, br_ref, bi_ref, or_ref, oi_ref):
        l1 = (lr_ref[...], li_ref[...])
        pw = [l1]
        for _ in range(rows - 1):
            pw.append(_cmul(*pw[-1], *l1))
        row = lax.broadcasted_iota(jnp.int32, (rows, ln), 0)
        expo = (rows - row) if reverse else (row + 1)
        pr = jnp.zeros((rows, ln), F32)
        pi = jnp.zeros((rows, ln), F32)
        for e in range(1, rows + 1):
            pr = jnp.where(expo == e, pw[e - 1][0], pr)
            pi = jnp.where(expo == e, pw[e - 1][1], pi)
        lk = {k: (jnp.broadcast_to(pw[k - 1][0], (rows, ln)), jnp.broadcast_to(pw[k - 1][1], (rows, ln))) for k in (1, 2, 4)}

        def step(i, carry):
            cr, ci = carry
            t = (nt - 1 - i) if reverse else i
            r0 = pl.multiple_of(t * rows, rows)
            xr, xi = br_ref[pl.ds(r0, rows), :], bi_ref[pl.ds(r0, rows), :]
            for k in (1, 2, 4):
                sr = _shift_rows(xr, k, row, rows, reverse)
                si = _shift_rows(xi, k, row, rows, reverse)
                ar, ai = _cmul(lk[k][0], lk[k][1], sr, si)
                xr, xi = xr + ar, xi + ai
            ar, ai = _cmul(pr, pi, cr, ci)
            xr, xi = xr + ar, xi + ai
            or_ref[pl.ds(r0, rows), :] = xr
            oi_ref[pl.ds(r0, rows), :] = xi
            if reverse:
                return xr[0:1], xi[0:1]
            return xr[rows - 1:rows], xi[rows - 1:rows]

        zero = jnp.zeros((1, ln), F32)
        lax.fori_loop(0, nt, step, (zero, zero))

    vec = pl.BlockSpec((1, ln), lambda j: (0, j))
    blk = pl.BlockSpec((s, ln), lambda j: (0, j))
    return _pc(body, "ssm_scan_bwd" if reverse else "ssm_scan_fwd", (SSM_L // ln,), [vec, vec, blk, blk], [blk, blk],
               [_sds((s, SSM_L))] * 2)(lam_re, lam_im, b_re, b_im)


def _ssm_in(name, v, w_bd):
    s = v.shape[0]
    ts = _tile(s)
    half = SSM_GB * SSM_P

    def body(v_ref, w_ref, or_ref, oi_ref):
        r = jnp.dot(_bf(v_ref[...]), w_ref[...], preferred_element_type=F32)
        or_ref[...] = r[:, :half]
        oi_ref[...] = r[:, half:]

    out = pl.BlockSpec((ts, half), lambda q, t: (t, q))
    return _pc(body, name, (SSM_NB, s // ts),
               [pl.BlockSpec((ts, SSM_GB * SSM_N), lambda q, t: (t, q)), pl.BlockSpec((None, SSM_GB * SSM_N, 2 * half), lambda q, t: (q, 0, 0))],
               [out, out], [_sds((s, SSM_L))] * 2)(v, w_bd)


def _ssm_out(name, x_re, x_im, w_bd):
    s = x_re.shape[0]
    ts = _tile(s)
    half = SSM_GB * SSM_P
    nt = (((1,), (1,)), ((), ()))

    def body(xr_ref, xi_ref, w_ref, o_ref):
        w = w_ref[...]
        o_ref[...] = (lax.dot_general(_bf(xr_ref[...]), w[:, :half], nt, preferred_element_type=F32)
                      + lax.dot_general(_bf(xi_ref[...]), w[:, half:], nt, preferred_element_type=F32))

    xin = pl.BlockSpec((ts, half), lambda q, t: (t, q))
    return _pc(body, name, (SSM_NB, s // ts),
               [xin, xin, pl.BlockSpec((None, SSM_GB * SSM_N, 2 * half), lambda q, t: (q, 0, 0))],
               pl.BlockSpec((ts, SSM_GB * SSM_N), lambda q, t: (t, q)), _sds((s, SSM_G * SSM_N)))(x_re, x_im, w_bd)


def _ssm_outer(name, v, x_re, x_im):
    s = v.shape[0]
    ts = min(s, 512)
    half = SSM_GB * SSM_P
    tn = (((0,), (0,)), ((), ()))

    def body(v_ref, xr_ref, xi_ref, o_ref):
        vv = _bf(v_ref[...])
        pr = lax.dot_general(vv, _bf(xr_ref[...]), tn, preferred_element_type=F32)
        pi = lax.dot_general(vv, _bf(xi_ref[...]), tn, preferred_element_type=F32)
        first = pl.program_id(1) == 0

        @pl.when(first)
        def _():
            o_ref[:, :half] = pr
            o_ref[:, half:] = pi

        @pl.when(jnp.logical_not(first))
        def _():
            o_ref[:, :half] += pr
            o_ref[:, half:] += pi

    xin = pl.BlockSpec((ts, half), lambda q, t: (t, q))
    return _pc(body, name, (SSM_NB, s // ts), [pl.BlockSpec((ts, SSM_GB * SSM_N), lambda q, t: (t, q)), xin, xin],
               pl.BlockSpec((None, SSM_GB * SSM_N, 2 * half), lambda q, t: (q, 0, 0)),
               _sds((SSM_NB, SSM_GB * SSM_N, 2 * half)))(v, x_re, x_im)


def _ssm_dlam(x_re, x_im, a_re, a_im):
    s = x_re.shape[0]
    ln = SCAN_LANES

    def body(xr_ref, xi_ref, ar_ref, ai_ref, or_ref, oi_ref):
        row = lax.broadcasted_iota(jnp.int32, (s, ln), 0)
        xr = _shift_rows(xr_ref[...], 1, row, s, False)
        xi = _shift_rows(xi_ref[...], 1, row, s, False)
        ar, ai = ar_ref[...], ai_ref[...]
        or_ref[...] = _colsum(xr * ar + xi * ai)
        oi_ref[...] = _colsum(xr * ai - xi * ar)

    blk = pl.BlockSpec((s, ln), lambda j: (0, j))
    vec = pl.BlockSpec((1, ln), lambda j: (0, j))
    return _pc(body, "ssm_dlam", (SSM_L // ln,), [blk] * 4, [vec, vec], [_sds((1, SSM_L))] * 2)(x_re, x_im, a_re, a_im)


def _ssm_act_fwd(y, u, d_skip):
    s = y.shape[0]
    ts = min(s, 512)

    def body(y_ref, u_ref, d_ref, o_ref):
        o_ref[...] = _gelu(y_ref[...] + d_ref[...] * u_ref[...]).astype(BF16)

    return _pc(body, "ssm_act_fwd", (s // ts,), [_row_spec(ts, D)] * 2 + [_vec_spec(D)], _row_spec(ts, D),
               _sds((s, D), BF16))(y, u, d_skip)


def _ssm_act_bwd(dg, y, u, d_skip):
    s = y.shape[0]
    ts = min(s, 512)

    def body(dg_ref, y_ref, u_ref, d_ref, dy_ref, dd_ref):
        uv = u_ref[...]
        dy = dg_ref[...] * _gelu_grad(y_ref[...] + d_ref[...] * uv)
        dy_ref[...] = dy.astype(BF16)
        _acc(dd_ref, pl.program_id(0) == 0, _colsum(dy * uv))

    return _pc(body, "ssm_act_bwd", (s // ts,), [_row_spec(ts, D)] * 3 + [_vec_spec(D)], [_row_spec(ts, D), _vec_spec(D)],
               [_sds((s, D), BF16), _sds((1, D))])(dg, y, u, d_skip)


def _axpy(a, b, d_skip):
    s = a.shape[0]
    ts = min(s, 512)

    def body(a_ref, b_ref, d_ref, o_ref):
        o_ref[...] = (a_ref[...] + d_ref[...] * b_ref[...].astype(F32)).astype(BF16)

    return _pc(body, "ssm_du", (s // ts,), [_row_spec(ts, D)] * 2 + [_vec_spec(D)], _row_spec(ts, D),
               _sds((s, D), BF16))(a, b, d_skip)


def _glu_fwd(zz):
    s = zz.shape[0]
    ts = min(s, 512)

    def body(a_ref, b_ref, o_ref):
        o_ref[...] = a_ref[...] * _sigmoid(b_ref[...])

    return _pc(body, "glu_fwd", (s // ts,), [_row_spec(ts, D, 0), _row_spec(ts, D, 1)], _row_spec(ts, D), _sds((s, D)))(zz, zz)


def _glu_bwd(zz, df):
    s = zz.shape[0]
    ts = min(s, 512)

    def body(a_ref, b_ref, df_ref, o_ref):
        sg = _sigmoid(b_ref[...])
        dfv = df_ref[...].astype(F32)
        o_ref[:, :D] = (dfv * sg).astype(BF16)
        o_ref[:, D:] = (dfv * a_ref[...] * sg * (1.0 - sg)).astype(BF16)

    return _pc(body, "glu_bwd", (s // ts,), [_row_spec(ts, D, 0), _row_spec(ts, D, 1), _row_spec(ts, D)],
               _row_spec(ts, 2 * D), _sds((s, 2 * D), BF16))(zz, zz, df)


def _block_diag(m):
    g, r, c = m.shape
    eye = jnp.eye(SSM_GB, dtype=m.dtype)
    m = m.reshape(SSM_NB, SSM_GB, r, c)
    return (m[:, :, :, None, :] * eye[None, :, None, :, None]).reshape(SSM_NB, SSM_GB * r, SSM_GB * c)


def _diag_blocks(m, r, c):
    m = m.reshape(SSM_NB, SSM_GB, r, SSM_GB, c)
    idx = jnp.arange(SSM_GB)
    return m[:, idx, :, idx, :].transpose(1, 0, 2, 3).reshape(SSM_G, r, c)


def _mod_part(c_all, ada_w):
    n = ada_w.shape[-1]

    def body(c_ref, w_ref, o_ref):
        cv = c_ref[...]
        cond = _bf(cv * _sigmoid(cv))
        o_ref[...] = jnp.dot(cond, _bf(w_ref[...]), preferred_element_type=F32)

    return _pc(body, "mod_part", (2,), [pl.BlockSpec((N_DEV, D), lambda l: (0, 0)), pl.BlockSpec((None, D, n), lambda l: (l, 0, 0))],
               pl.BlockSpec((None, N_DEV, n), lambda l: (l, 0, 0)), _sds((2, N_DEV, n)))(c_all, ada_w)


def _ada_w_grad(c_all_t, dmod):
    n = dmod.shape[-1]
    tr = 128

    def body(c_ref, d_ref, o_ref):
        cv = c_ref[...]
        cond = _bf(cv * _sigmoid(cv)).astype(F32)
        dm = _bf(d_ref[...]).astype(F32)
        acc = cond[:, 0:1] * dm[0:1, :]
        for b in range(1, N_DEV):
            acc = acc + cond[:, b:b + 1] * dm[b:b + 1, :]
        o_ref[...] = acc

    return _pc(body, "ada_w_grad", (2, D // tr),
               [pl.BlockSpec((tr, N_DEV), lambda l, t: (t, 0)), pl.BlockSpec((None, N_DEV, n), lambda l, t: (l, 0, 0))],
               pl.BlockSpec((None, tr, n), lambda l, t: (l, t, 0)), _sds((2, D, n)))(c_all_t, dmod)


def _adamw(name, parts, w, m, v):
    p, r, c = parts.shape
    tr = r
    while tr * c * 4 > (1 << 20) and tr % 16 == 0:
        tr //= 2

    def body(p_ref, w_ref, m_ref, v_ref, g_ref, d_ref, nm_ref, nv_ref):
        g = p_ref[0].astype(F32)
        for i in range(1, p):
            g = g + p_ref[i].astype(F32)
        g_ref[...] = g
        m2 = B1 * m_ref[...] + (1.0 - B1) * g
        v2 = B2 * v_ref[...] + (1.0 - B2) * (g * g)
        nm_ref[...] = m2
        nv_ref[...] = v2
        m_hat = m2 / (1.0 - B1 ** STEP)
        v_hat = v2 / (1.0 - B2 ** STEP)
        d_ref[...] = -LR * (m_hat / (jnp.sqrt(v_hat) + ADAM_EPS) + WD * w_ref[...])

    blk = pl.BlockSpec((tr, c), lambda t: (t, 0))
    return _pc(body, name, (r // tr,), [pl.BlockSpec((p, tr, c), lambda t: (0, t, 0)), blk, blk, blk], [blk] * 4,
               [_sds((r, c))] * 4)(parts, w, m, v)


def _sum_parts(parts):
    p, r, c = parts.shape
    tr = r
    while tr * c * 4 > (1 << 19) and tr % 16 == 0:
        tr //= 2

    def body(p_ref, o_ref):
        g = p_ref[0]
        for i in range(1, p):
            g = g + p_ref[i]
        o_ref[...] = g

    return _pc(body, "sum_parts", (r // tr,), [pl.BlockSpec((p, tr, c), lambda t: (0, t, 0))], pl.BlockSpec((tr, c), lambda t: (t, 0)),
               _sds((r, c)))(parts)


def _exchange(name, srcs, dst_shapes, plan, n_copies):
    ns, nd = len(srcs), len(dst_shapes)

    def body(*refs):
        src_refs, dst_refs = refs[:ns], refs[ns:ns + nd]
        send_sems, recv_sems, local_sems = refs[ns + nd:]
        x, y, c = lax.axis_index("x"), lax.axis_index("y"), lax.axis_index("c")
        me = 4 * x + 2 * y + c

        def at(ref, idx):
            return ref.at[idx] if idx is not None else ref

        local = []
        for i, (si, sidx, di, didx) in enumerate(plan(me, me)):
            cp = pltpu.make_async_copy(at(src_refs[si], sidx), at(dst_refs[di], didx), local_sems.at[i])
            cp.start()
            local.append(cp)
        peers = []
        for k in range(1, N_DEV):
            px = (1 - x) if k & 4 else x
            py = (1 - y) if k & 2 else y
            pc = (1 - c) if k & 1 else c
            peers.append(((px, py, pc), 4 * px + 2 * py + pc))
        sends = []
        for k, (dev, peer) in enumerate(peers):
            for i, (si, sidx, di, didx) in enumerate(plan(me, peer)):
                cp = pltpu.make_async_remote_copy(at(src_refs[si], sidx), at(dst_refs[di], didx), send_sems.at[k, i], recv_sems.at[k, i],
                                                  device_id=dev, device_id_type=MESH)
                cp.start()
                sends.append(cp)
        for k, (dev, peer) in enumerate(peers):
            for i, (si, sidx, di, didx) in enumerate(plan(peer, me)):
                pltpu.make_async_remote_copy(at(src_refs[si], sidx), at(dst_refs[di], didx), send_sems.at[k, i], recv_sems.at[k, i],
                                             device_id=dev, device_id_type=MESH).wait_recv()
        for cp in sends:
            cp.wait_send()
        for cp in local:
            cp.wait()

    any_spec = pl.BlockSpec(memory_space=pl.ANY)
    return pl.pallas_call(
        body, name=name, in_specs=[any_spec] * ns, out_specs=[any_spec] * nd, out_shape=list(dst_shapes),
        scratch_shapes=[pltpu.SemaphoreType.DMA((N_DEV - 1, n_copies)), pltpu.SemaphoreType.DMA((N_DEV - 1, n_copies)),
                        pltpu.SemaphoreType.DMA((n_copies,))])(*srcs)


def _all_gather(name, arrs):
    plan = lambda me, peer: [(i, None, i, me) for i in range(len(arrs))]
    return _exchange(name, arrs, [_sds((N_DEV,) + a.shape, a.dtype) for a in arrs], plan, len(arrs))


def _sublayer_fwd(x, fn, mod3, g_pre, g_post, rw):
    h = _prenorm_fwd(x, g_pre, mod3[1:2], mod3[0:1])
    f, saved = fn(h)
    return _postnorm_fwd(x, f, g_post, mod3[2:3], rw), (x, f, saved)


def _sublayer_bwd(dout, saved, fn_bwd, mod3, g_pre, g_post, rw):
    x, f, inner = saved
    df, dgate, dg_post = _postnorm_bwd(dout, f, g_post, mod3[2:3], rw)
    dh, extra = fn_bwd(df, inner)
    dx, dshift, dscale, dg_pre = _prenorm_bwd(dout, dh, x, g_pre, mod3[1:2])
    return dx, jnp.concatenate([dshift, dscale, dgate], axis=0), dg_pre, dg_post, extra


def _mix0_fwd(h, p):
    z = _mm_nn("mix0_in", h, p["ab_w_in"])
    y_a, d = _pool_fwd(z, p["pool_w"], p["pool_scale"])
    y_b = _sgu_fwd(z, p["sgu_ln_g"], p["sgu_ln_b"], p["sgu_w"], p["sgu_bt"])
    ycat = jnp.concatenate([y_a, y_b], axis=1)
    return _mm_nn("mix0_out", ycat, p["ab_w_out"]), (h, z, d, ycat)


def _mix0_bwd(df, saved, p):
    h, z, d, ycat = saved
    dycat = _mm_nt("mix0_out_dx", df, p["ab_w_out"])
    g = {"ab_w_out": _mm_tn("mix0_out_dw", ycat, df, BF16)}
    dz_p, g["pool_w"], g["pool_scale"] = _pool_bwd(dycat, d, p["pool_w"], p["pool_scale"])
    dz_u, dz_v, g["sgu_ln_g"], g["sgu_ln_b"], g["sgu_w"], dbt = _sgu_bwd(
        z, dycat, p["sgu_ln_g"], p["sgu_ln_b"], p["sgu_w"], p["sgu_bt"], p["head_sum"])
    g["sgu_b"] = dbt[:, :NH].T
    dz = jnp.concatenate([dz_p, dz_u, dz_v], axis=1)
    g["ab_w_in"] = _mm_tn("mix0_in_dw", h, dz, BF16)
    return _mm_nt("mix0_in_dx", dz, p["ab_w_in"]), g


def _mix1_fwd(h, p):
    u = _mm_nn("ssm_w_in", h, p["ssm_w_in"])
    bu_re, bu_im = _ssm_in("ssm_bu", u, p["wb_bd"])
    x_re, x_im = _ssm_scan(bu_re, bu_im, p["lam_bar_re"], p["lam_bar_im"], False)
    y = _ssm_out("ssm_y", x_re, x_im, p["wc_bd"])
    g = _ssm_act_fwd(y, u, p["ssm_d"])
    zz = _mm_nn("ssm_glu", g, p["ssm_w_glu"])
    return _glu_fwd(zz), (h, u, x_re, x_im, y, g, zz)


def _mix1_bwd(df, saved, p):
    h, u, x_re, x_im, y, g, zz = saved
    gr = {}
    dzz = _glu_bwd(zz, df)
    dg = _mm_nt("ssm_glu_dx", dzz, p["ssm_w_glu"])
    gr["ssm_w_glu"] = _mm_tn("ssm_glu_dw", g, dzz, BF16)
    dy, gr["ssm_d"] = _ssm_act_bwd(dg, y, u, p["ssm_d"])
    gx_re, gx_im = _ssm_in("ssm_gx", dy, p["wct_bd"])
    a_re, a_im = _ssm_scan(gx_re, gx_im, p["lam_bar_re"], -p["lam_bar_im"], True)
    du_ssm = _ssm_out("ssm_du_mm", a_re, a_im, p["wbt_bd"])
    du = _axpy(du_ssm, dy, p["ssm_d"])
    gr["ssm_w_in"] = _mm_tn("ssm_w_in_dw", h, du, BF16)
    dh = _mm_nt("ssm_w_in_dx", du, p["ssm_w_in"])
    g_lam_re, g_lam_im = _ssm_dlam(x_re, x_im, a_re, a_im)
    m_b = _ssm_outer("ssm_db", u, a_re, a_im)
    m_c = _ssm_outer("ssm_dc", dy, x_re, x_im)
    half = SSM_GB * SSM_P
    gbb_re = _diag_blocks(m_b[:, :, :half], SSM_N, SSM_P).transpose(0, 2, 1).reshape(SSM_G, SSM_P * SSM_N)
    gbb_im = _diag_blocks(m_b[:, :, half:], SSM_N, SSM_P).transpose(0, 2, 1).reshape(SSM_G, SSM_P * SSM_N)
    gr["ssm_c_re"] = _diag_blocks(m_c[:, :, :half], SSM_N, SSM_P)
    gr["ssm_c_im"] = -_diag_blocks(m_c[:, :, half:], SSM_N, SSM_P)
    dlr, dli, ddt, dbr, dbi = _ssm_param_bwd(
        g_lam_re.reshape(SSM_G, SSM_P), g_lam_im.reshape(SSM_G, SSM_P), gbb_re, gbb_im,
        p["lam_re"], p["lam_im"], p["lam_re_rep"], p["lam_im_rep"], p["log_dt"], p["b_re"], p["b_im"], p["seg"])
    gr["ssm_lam_re"], gr["ssm_lam_im"], gr["ssm_log_dt"] = dlr, dli, ddt[:, 0]
    gr["ssm_b_re"] = dbr.reshape(SSM_G, SSM_P, SSM_N)
    gr["ssm_b_im"] = dbi.reshape(SSM_G, SSM_P, SSM_N)
    return dh, gr


def _ssm_params(lam_re, lam_im, b_re, b_im, c_re, c_im, log_dt):
    p = {"lam_re": lam_re, "lam_im": lam_im, "log_dt": log_dt.reshape(SSM_G, 1),
         "lam_re_rep": jnp.repeat(lam_re, SSM_N, axis=1), "lam_im_rep": jnp.repeat(lam_im, SSM_N, axis=1),
         "b_re": b_re.reshape(SSM_G, SSM_P * SSM_N), "b_im": b_im.reshape(SSM_G, SSM_P * SSM_N)}
    lbr, lbi, bbr, bbi = _ssm_prep(lam_re, lam_im, p["lam_re_rep"], p["lam_im_rep"], p["log_dt"], p["b_re"], p["b_im"])
    p["lam_bar_re"], p["lam_bar_im"] = lbr.reshape(1, SSM_L), lbi.reshape(1, SSM_L)
    bbr, bbi = bbr.reshape(SSM_G, SSM_P, SSM_N), bbi.reshape(SSM_G, SSM_P, SSM_N)
    p["wb_bd"] = jnp.concatenate([_block_diag(bbr.transpose(0, 2, 1)), _block_diag(bbi.transpose(0, 2, 1))], axis=2).astype(BF16)
    p["wbt_bd"] = p["wb_bd"]
    p["wc_bd"] = jnp.concatenate([_block_diag(c_re), _block_diag(-c_im)], axis=2).astype(BF16)
    p["wct_bd"] = p["wc_bd"]
    p["seg"] = jnp.repeat(jnp.eye(SSM_P, dtype=F32), SSM_N, axis=0)
    return p


def _local_step(x, tgt, mod, w):
    half = 0.5
    ffn_f = lambda l, k: (lambda h: _ffn_fwd(h, w["ffn_w_in"], w["ffn_w_out"], 2 * l + k))
    ffn_b = lambda l, k: (lambda df, sv: (lambda r: (r[0], r[1:]))(_ffn_bwd(df, sv, w["ffn_w_in"], w["ffn_w_out"], 2 * l + k)))
    mix_f = [lambda h: _mix0_fwd(h, w["mix0"]), lambda h: _mix1_fwd(h, w["mix1"])]
    mix_b = [lambda df, sv: _mix0_bwd(df, sv, w["mix0"]), lambda df, sv: _mix1_bwd(df, sv, w["mix1"])]
    fwd = [(ffn_f(l, 0), half) if s == 0 else (mix_f[l], 1.0) if s == 1 else (ffn_f(l, 1), half) for l in range(2) for s in range(3)]
    bwd = [ffn_b(l, 0) if s == 0 else mix_b[l] if s == 1 else ffn_b(l, 1) for l in range(2) for s in range(3)]
    saved = []
    for i, (fn, rw) in enumerate(fwd):
        l, s = divmod(i, 3)
        x, sv = _sublayer_fwd(x, fn, mod[l, s], w["norm_pre"][l, s][None], w["norm_post"][l, s][None], rw)
        saved.append(sv)
    loss_row, dx = _loss_fwd_bwd(x, tgt)
    dmod, dpre, dpost, extras = [None] * 6, [None] * 6, [None] * 6, [None] * 6
    for i in reversed(range(6)):
        l, s = divmod(i, 3)
        dx, dmod[i], dpre[i], dpost[i], extras[i] = _sublayer_bwd(
            dx, saved[i], bwd[i], mod[l, s], w["norm_pre"][l, s][None], w["norm_post"][l, s][None], fwd[i][1])
    dmod = jnp.stack(dmod).reshape(2, 3, 3, D)
    dpre = jnp.concatenate(dpre, axis=0).reshape(2, 3, D)
    dpost = jnp.concatenate(dpost, axis=0).reshape(2, 3, D)
    return loss_row, dx, dmod, dpre, dpost, extras


def _pad_rows(v, rows):
    return jnp.pad(v, (0, rows * LANE - v.shape[0])).reshape(rows, LANE)


def _pack(parts):
    flat, layout, off = [], [], 0
    for a in parts:
        n = a.size
        padded = -(-n // LANE) * LANE
        flat.append(jnp.pad(a.reshape(-1).astype(F32), (0, padded - n)))
        layout.append((off, n, a.shape))
        off += padded
    return jnp.concatenate(flat), layout


def _unpack(flat, layout):
    return [flat[off:off + n].reshape(shape) for off, n, shape in layout]


SMALL_REPLICATED = ["ada_b", "pool_w", "pool_scale", "sgu_ln_g", "sgu_ln_b", "sgu_w", "sgu_b", "ssm_lam_re", "ssm_lam_im",
                    "ssm_b_re", "ssm_b_im", "ssm_c_re", "ssm_c_im", "ssm_log_dt"]
SMALL_SHARDED = ["norm_pre", "norm_post", "ssm_d"]
WEIGHTS = ['ada_w', 'ada_b', 'norm_pre', 'norm_post', 'ffn_w_in', 'ffn_w_out', 'ab_w_in', 'pool_w', 'pool_scale', 'sgu_ln_g',
           'sgu_ln_b', 'sgu_w', 'sgu_b', 'ab_w_out', 'ssm_w_in', 'ssm_lam_re', 'ssm_lam_im', 'ssm_b_re', 'ssm_b_im', 'ssm_c_re',
           'ssm_c_im', 'ssm_d', 'ssm_log_dt', 'ssm_w_glu']


def kernel(x, c, ada_w, ada_b, norm_pre, norm_post, ffn_w_in, ffn_w_out, ab_w_in, pool_w, pool_scale, sgu_ln_g, sgu_ln_b, sgu_w, sgu_b, ab_w_out, ssm_w_in, ssm_lam_re, ssm_lam_im, ssm_b_re, ssm_b_im, ssm_c_re, ssm_c_im, ssm_d, ssm_log_dt, ssm_w_glu, loss_target, m_ada_w, m_ada_b, m_norm_pre, m_norm_post, m_ffn_w_in, m_ffn_w_out, m_ab_w_in, m_pool_w, m_pool_scale, m_sgu_ln_g, m_sgu_ln_b, m_sgu_w, m_sgu_b, m_ab_w_out, m_ssm_w_in, m_ssm_lam_re, m_ssm_lam_im, m_ssm_b_re, m_ssm_b_im, m_ssm_c_re, m_ssm_c_im, m_ssm_d, m_ssm_log_dt, m_ssm_w_glu, v_ada_w, v_ada_b, v_norm_pre, v_norm_post, v_ffn_w_in, v_ffn_w_out, v_ab_w_in, v_pool_w, v_pool_scale, v_sgu_ln_g, v_sgu_ln_b, v_sgu_w, v_sgu_b, v_ab_w_out, v_ssm_w_in, v_ssm_lam_re, v_ssm_lam_im, v_ssm_b_re, v_ssm_b_im, v_ssm_c_re, v_ssm_c_im, v_ssm_d, v_ssm_log_dt, v_ssm_w_glu):
    args = locals()
    wts = {n: args[n] for n in WEIGHTS}
    mom = {n: args["m_" + n] for n in WEIGHTS}
    var = {n: args["v_" + n] for n in WEIGHTS}
    me = 4 * lax.axis_index("x") + 2 * lax.axis_index("y") + lax.axis_index("c")
    s = x.shape[1]
    nd = D // N_DEV

    small_in, small_in_layout = _pack([c, norm_pre, norm_post, ssm_d])
    small_rows = -(-small_in.shape[0] // (8 * LANE)) * 8
    (g_small,) = _all_gather("gather_small", [_pad_rows(small_in, small_rows)])
    g_small = g_small.reshape(N_DEV, -1)
    c_all, npre_g, npost_g, sd_g = [jnp.stack([_unpack(g_small[j], small_in_layout)[i] for j in range(N_DEV)]) for i in range(4)]
    c_all = c_all.reshape(N_DEV, D)
    norm_pre_full = npre_g.transpose(1, 2, 0, 3).reshape(2, 3, D)
    norm_post_full = npost_g.transpose(1, 2, 0, 3).reshape(2, 3, D)
    ssm_d_full = sd_g.transpose(1, 0, 2).reshape(1, D)

    nw = ada_w.shape[-1]
    (mod_g,) = _all_gather("gather_mod", [_mod_part(c_all, ada_w)])
    mod = lax.dynamic_index_in_dim(mod_g, me, axis=2, keepdims=False)
    mod = (mod.transpose(1, 0, 2).reshape(2, N_DEV * nw) + ada_b).reshape(2, 3, 3, D)

    big = [ffn_w_in.reshape(4, D, FSH), ffn_w_out.reshape(4, D_FF // N_DEV, D), ab_w_in[0], ab_w_out[0], ssm_w_in[0], ssm_w_glu[0]]
    big = [a.astype(BF16) for a in big]

    def gather_plan(me_, peer_):
        cps = [(0, lk, 0, (lk, me_)) for lk in range(4)] + [(1, lk, 1, (lk, me_)) for lk in range(4)]
        return cps + [(i, None, i, me_) for i in range(2, 6)]

    g_win, g_wout, g_abin, g_about, g_ssmin, g_glu = _exchange(
        "gather_weights", big,
        [_sds((4, N_DEV, D, FSH), BF16), _sds((4, N_DEV, D_FF // N_DEV, D), BF16)] + [_sds((N_DEV,) + a.shape, BF16) for a in big[2:]],
        gather_plan, 12)

    w = {"ffn_w_in": g_win, "ffn_w_out": g_wout.reshape(4, D_FF, D), "norm_pre": norm_pre_full, "norm_post": norm_post_full}
    head_sum = jnp.repeat(jnp.eye(NH, LANE, dtype=F32), HD, axis=0)
    w["mix0"] = {"ab_w_in": g_abin.transpose(1, 0, 2).reshape(D, -1), "ab_w_out": g_about.reshape(D, D),
                 "pool_w": pool_w[0], "pool_scale": pool_scale, "sgu_ln_g": sgu_ln_g, "sgu_ln_b": sgu_ln_b, "sgu_w": sgu_w[0],
                 "sgu_bt": jnp.pad(sgu_b[0].T, ((0, 0), (0, LANE - NH))), "head_sum": head_sum}
    w["mix1"] = _ssm_params(ssm_lam_re[0], ssm_lam_im[0], ssm_b_re[0], ssm_b_im[0], ssm_c_re[0], ssm_c_im[0], ssm_log_dt[0])
    w["mix1"].update({"ssm_w_in": g_ssmin.reshape(D, D), "ssm_w_glu": g_glu.transpose(1, 0, 2).reshape(D, -1), "ssm_d": ssm_d_full})

    loss_row, grad_x, dmod, dpre, dpost, extras = _local_step(x[0], loss_target[0], mod, w)
    g0, g1 = extras[1], extras[4]

    def shard_cols(a):
        r = a.shape[0]
        return a.reshape(r, N_DEV, -1).transpose(1, 0, 2)

    ffn = [extras[i] for i in (0, 2, 3, 5)]
    parts = [f[0] for f in ffn] + [f[1] for f in ffn] + [shard_cols(g0["ab_w_in"]), g0["ab_w_out"].reshape(N_DEV, nd, D),
                                                          g1["ssm_w_in"].reshape(N_DEV, nd, D), shard_cols(g1["ssm_w_glu"])]

    def scatter_plan(me_, peer_):
        cps = [(lk, peer_, 0, (me_, lk)) for lk in range(4)] + [(4 + lk, peer_, 1, (me_, lk)) for lk in range(4)]
        return cps + [(8 + i, peer_, 2 + i, me_) for i in range(4)]

    r_win, r_wout, r_abin, r_about, r_ssmin, r_glu = _exchange(
        "scatter_grads", parts,
        [_sds((N_DEV, 4, D, FSH), BF16), _sds((N_DEV, 4, D_FF // N_DEV, D), BF16)] + [_sds(a.shape, BF16) for a in parts[8:]],
        scatter_plan, 12)

    g1["ssm_log_dt"] = g1["ssm_log_dt"].reshape(1, SSM_G)
    small = {"ada_b": dmod.reshape(2, 9 * D), "norm_pre": dpre, "norm_post": dpost, "ssm_d": g1["ssm_d"]}
    small.update({n: g0[n] for n in ["pool_w", "pool_scale", "sgu_ln_g", "sgu_ln_b", "sgu_w", "sgu_b"]})
    small.update({n: g1[n] for n in ["ssm_lam_re", "ssm_lam_im", "ssm_b_re", "ssm_b_im", "ssm_c_re", "ssm_c_im", "ssm_log_dt"]})
    names = SMALL_REPLICATED + SMALL_SHARDED
    flat, layout = _pack([small[n].reshape(wts[n].shape if n in SMALL_REPLICATED else small[n].shape) for n in names] + [loss_row])
    rows = -(-flat.shape[0] // (8 * LANE)) * 8
    (g_parts,) = _all_gather("gather_small_grads", [_pad_rows(flat, rows)])
    total = _sum_parts(g_parts).reshape(-1)
    sums = dict(zip(names + ["loss"], _unpack(total, layout)))
    loss = sums["loss"][0, 0]

    out_g, out_d, out_m, out_v = {}, {}, {}, {}

    def adam_flat(name, gnames, grads):
        gf, lay = _pack(grads)
        r = -(-gf.shape[0] // (8 * LANE)) * 8
        packed = [_pad_rows(_pack([t[n] for n in gnames])[0], r) for t in (wts, mom, var)]
        res = _adamw(name, _pad_rows(gf, r)[None], *packed)
        for o, arr in zip((out_g, out_d, out_m, out_v), res):
            o.update(dict(zip(gnames, _unpack(arr.reshape(-1), lay))))

    adam_flat("adamw_replicated", SMALL_REPLICATED, [sums[n] for n in SMALL_REPLICATED])
    sliced = [lax.dynamic_slice_in_dim(sums[n], me * nd, nd, axis=sums[n].ndim - 1).reshape(wts[n].shape) for n in SMALL_SHARDED]
    adam_flat("adamw_sliced", SMALL_SHARDED, sliced)

    def adam_big(name, recv, n):
        shape = wts[n].shape
        c_ = shape[-1]
        res = _adamw(name, recv.reshape(recv.shape[0], -1, c_), *[t[n].reshape(-1, c_) for t in (wts, mom, var)])
        for o, arr in zip((out_g, out_d, out_m, out_v), res):
            o[n] = arr.reshape(shape)

    adam_big("adamw_ffn_w_in", r_win, "ffn_w_in")
    adam_big("adamw_ffn_w_out", r_wout, "ffn_w_out")
    adam_big("adamw_ab_w_in", r_abin, "ab_w_in")
    adam_big("adamw_ab_w_out", r_about, "ab_w_out")
    adam_big("adamw_ssm_w_in", r_ssmin, "ssm_w_in")
    adam_big("adamw_ssm_w_glu", r_glu, "ssm_w_glu")

    dmod_all = g_parts.reshape(N_DEV, -1)[:, layout[0][0]:layout[0][0] + layout[0][1]].reshape(N_DEV, 2, N_DEV, nw)
    dmod_mine = lax.dynamic_index_in_dim(dmod_all, me, axis=2, keepdims=False).transpose(1, 0, 2)
    g_ada_w = _ada_w_grad(c_all.T, dmod_mine)
    adam_big("adamw_ada_w", g_ada_w[None], "ada_w")

    return (loss, grad_x[None], *[out_g[n] for n in WEIGHTS], *[out_d[n] for n in WEIGHTS],
            *[out_m[n] for n in WEIGHTS], *[out_v[n] for n in WEIGHTS])
```

```python
import functools
import math

import jax
import jax.numpy as jnp
from jax import lax
from jax.experimental import pallas as pl
from jax.experimental.pallas import tpu as pltpu

F32 = jnp.float32
BF16 = jnp.bfloat16
MESH = pl.DeviceIdType.MESH
HIGHEST = lax.Precision.HIGHEST

N_DEV = 8
D = 1024
D_FF = 2816
FSH = 2 * D_FF // N_DEV
EPS = 1e-6
POOL_WINDOWS = (2, 4, 8, 16)
HD = 128
NH = 4
SSM_G, SSM_P, SSM_N = 64, 64, 16
SSM_GB = 16
SSM_NB = SSM_G // SSM_GB
SSM_L = SSM_G * SSM_P
LR, B1, B2, ADAM_EPS, WD, STEP = 0.001, 0.9, 0.999, 1e-08, 0.01, 10
GELU_C = math.sqrt(2.0 / math.pi)
VMEM_LIMIT_BYTES = 48 * 1024 * 1024
LANE = 128


def _pc(body, name, grid, in_specs, out_specs, out_shape, scratch=()):
    return pl.pallas_call(
        body, name=name, grid=grid, in_specs=in_specs, out_specs=out_specs, out_shape=out_shape,
        scratch_shapes=list(scratch),
        compiler_params=pltpu.CompilerParams(dimension_semantics=("arbitrary",) * len(grid),
                                             vmem_limit_bytes=VMEM_LIMIT_BYTES))


def _sds(shape, dtype=F32):
    return jax.ShapeDtypeStruct(tuple(shape), dtype)


def _bf(v):
    return v if v.dtype == BF16 else v.astype(BF16)


def _row_spec(ts, width, col=0):
    return pl.BlockSpec((ts, width), lambda t, _c=col: (t, _c))


def _vec_spec(width, col=0):
    return pl.BlockSpec((1, width), lambda t, _c=col: (0, _c))


def _mm(name, a, b, contract, grid, a_spec, b_spec, o_spec, out_shape, acc_axis=None):
    dn = (contract, ((), ()))

    def body(a_ref, b_ref, o_ref):
        r = lax.dot_general(_bf(a_ref[...]), _bf(b_ref[...]), dn, preferred_element_type=F32)
        if acc_axis is None:
            o_ref[...] = r.astype(o_ref.dtype)
        else:
            k = pl.program_id(acc_axis)

            @pl.when(k == 0)
            def _():
                o_ref[...] = r

            @pl.when(k > 0)
            def _():
                o_ref[...] += r

    return _pc(body, name, grid, [a_spec, b_spec], o_spec, out_shape)(a, b)


def _tile(s):
    return min(s, 1024)


def _div_tile(n, cap=1024):
    t = min(n, cap) // LANE * LANE
    while n % t:
        t -= LANE
    return t


def _mm_nn(name, a, b, out_dtype=F32):
    s, k = a.shape
    n = b.shape[1]
    ts, tn = _tile(s), _div_tile(n)
    return _mm(name, a, b, ((1,), (0,)), (n // tn, s // ts),
               pl.BlockSpec((ts, k), lambda j, t: (t, 0)), pl.BlockSpec((k, tn), lambda j, t: (0, j)),
               pl.BlockSpec((ts, tn), lambda j, t: (t, j)), _sds((s, n), out_dtype))


def _mm_nt(name, a, b, out_dtype=F32):
    s, n = a.shape
    k = b.shape[0]
    ts, tk = _tile(s), min(k, 1024)
    return _mm(name, a, b, ((1,), (1,)), (k // tk, s // ts),
               pl.BlockSpec((ts, n), lambda j, t: (t, 0)), pl.BlockSpec((tk, n), lambda j, t: (j, 0)),
               pl.BlockSpec((ts, tk), lambda j, t: (t, j)), _sds((s, k), out_dtype))


def _mm_tn(name, a, b, out_dtype=F32, tm=512, tn=512):
    s, m = a.shape
    n = b.shape[1]
    tm, tn = min(m, tm), min(n, tn)
    return _mm(name, a, b, ((0,), (0,)), (m // tm, n // tn),
               pl.BlockSpec((s, tm), lambda i, j: (0, i)), pl.BlockSpec((s, tn), lambda i, j: (0, j)),
               pl.BlockSpec((tm, tn), lambda i, j: (i, j)), _sds((m, n), out_dtype))


def _rstd(v):
    return lax.rsqrt(jnp.mean(v * v, axis=-1, keepdims=True) + EPS)


def _prenorm_fwd(x, g, scale, shift):
    s = x.shape[0]
    ts = min(s, 512)

    def body(x_ref, g_ref, sc_ref, sh_ref, h_ref):
        xv = x_ref[...]
        h_ref[...] = ((xv * _rstd(xv) * g_ref[...]) * (1.0 + sc_ref[...]) + sh_ref[...]).astype(BF16)

    return _pc(body, "prenorm_fwd", (s // ts,), [_row_spec(ts, D)] + [_vec_spec(D)] * 3, _row_spec(ts, D),
               _sds((s, D), BF16))(x, g, scale, shift)


def _postnorm_fwd(x, f, g, gate, rw):
    s = x.shape[0]
    ts = min(s, 512)

    def body(x_ref, f_ref, g_ref, gt_ref, o_ref):
        fv = f_ref[...]
        o_ref[...] = x_ref[...] + (rw * gt_ref[...]) * (fv * _rstd(fv) * g_ref[...])

    return _pc(body, "postnorm_fwd", (s // ts,), [_row_spec(ts, D)] * 2 + [_vec_spec(D)] * 2, _row_spec(ts, D),
               _sds((s, D)))(x, f, g, gate)


def _acc(ref, first, v):
    @pl.when(first)
    def _():
        ref[...] = v

    @pl.when(jnp.logical_not(first))
    def _():
        ref[...] += v


def _colsum(v):
    return jnp.sum(v, axis=0, keepdims=True)


def _postnorm_bwd(dout, f, g, gate, rw):
    s = dout.shape[0]
    ts = min(s, 512)

    def body(do_ref, f_ref, g_ref, gt_ref, df_ref, dgate_ref, dg_ref):
        first = pl.program_id(0) == 0
        do, fv, gv = do_ref[...], f_ref[...], g_ref[...]
        r = _rstd(fv)
        fn = fv * r
        _acc(dgate_ref, first, rw * _colsum(do * (fn * gv)))
        dy = (rw * gt_ref[...]) * do
        _acc(dg_ref, first, _colsum(dy * fn))
        dfn = dy * gv
        df_ref[...] = (r * (dfn - fn * jnp.mean(dfn * fn, axis=-1, keepdims=True))).astype(BF16)

    return _pc(body, "postnorm_bwd", (s // ts,), [_row_spec(ts, D)] * 2 + [_vec_spec(D)] * 2,
               [_row_spec(ts, D), _vec_spec(D), _vec_spec(D)],
               [_sds((s, D), BF16), _sds((1, D)), _sds((1, D))])(dout, f, g, gate)


def _prenorm_bwd(dout, dh, x, g, scale):
    s = dout.shape[0]
    ts = min(s, 512)

    def body(do_ref, dh_ref, x_ref, g_ref, sc_ref, dx_ref, dsh_ref, dsc_ref, dg_ref):
        first = pl.program_id(0) == 0
        dhv, xv, gv = dh_ref[...], x_ref[...], g_ref[...]
        r = _rstd(xv)
        xn = xv * r
        _acc(dsh_ref, first, _colsum(dhv))
        _acc(dsc_ref, first, _colsum(dhv * (xn * gv)))
        dhp = dhv * (1.0 + sc_ref[...])
        _acc(dg_ref, first, _colsum(dhp * xn))
        dxn = dhp * gv
        dx_ref[...] = do_ref[...] + r * (dxn - xn * jnp.mean(dxn * xn, axis=-1, keepdims=True))

    return _pc(body, "prenorm_bwd", (s // ts,), [_row_spec(ts, D)] * 3 + [_vec_spec(D)] * 2,
               [_row_spec(ts, D)] + [_vec_spec(D)] * 3,
               [_sds((s, D))] + [_sds((1, D))] * 3)(dout, dh, x, g, scale)


def _loss_fwd_bwd(y, tgt):
    s = y.shape[0]
    ts = min(s, 512)
    nt = s // ts

    def body(y_ref, t_ref, loss_ref, dy_ref, acc_ref):
        t = pl.program_id(0)
        e = y_ref[...] - t_ref[...]
        dy_ref[...] = e * (1.0 / D)
        _acc(acc_ref, t == 0, _colsum(e * e))

        @pl.when(t == nt - 1)
        def _():
            loss_ref[...] = jnp.full((1, LANE), 0.5 / D, F32) * jnp.sum(acc_ref[...])

    return _pc(body, "loss", (nt,), [_row_spec(ts, D)] * 2,
               [pl.BlockSpec((1, LANE), lambda t: (0, 0)), _row_spec(ts, D)],
               [_sds((1, LANE)), _sds((s, D))], scratch=[pltpu.VMEM((1, D), F32)])(y, tgt)


def _sigmoid(v):
    return 1.0 / (1.0 + jnp.exp(-v))


def _swiglu_fwd(z):
    _, s, _ = z.shape
    ts = min(s, 512)
    z4 = z.reshape(2, 4, s, FSH)

    def body(z_ref, o_ref):
        a, b = z_ref[0], z_ref[1]
        o_ref[...] = (a * _sigmoid(a) * b).astype(BF16)

    return _pc(body, "swiglu_fwd", (4, s // ts), [pl.BlockSpec((2, None, ts, FSH), lambda k, t: (0, k, t, 0))],
               pl.BlockSpec((None, ts, FSH), lambda k, t: (k, t, 0)), _sds((4, s, FSH), BF16))(z4)


def _swiglu_bwd(z, dact):
    _, s, _ = z.shape
    ts = min(s, 512)
    z4 = z.reshape(2, 4, s, FSH)

    def body(z_ref, d_ref, o_ref):
        a, b, d = z_ref[0], z_ref[1], d_ref[...]
        sg = _sigmoid(a)
        o_ref[0] = (d * b * (sg * (1.0 + a * (1.0 - sg)))).astype(BF16)
        o_ref[1] = (d * (a * sg)).astype(BF16)

    spec = pl.BlockSpec((2, None, ts, FSH), lambda k, t: (0, k, t, 0))
    out = _pc(body, "swiglu_bwd", (4, s // ts), [spec, pl.BlockSpec((None, ts, FSH), lambda k, t: (k, t, 0))],
              spec, _sds((2, 4, s, FSH), BF16))(z4, dact)
    return out.reshape(8, s, FSH)


def _ffn_fwd(h, win, wout):
    s = h.shape[0]
    ts = _tile(s)
    wout = wout.reshape(4, FSH, D)
    z = _mm("ffn_in", h, win, ((1,), (0,)), (N_DEV, s // ts),
            pl.BlockSpec((ts, D), lambda j, t: (t, 0)), pl.BlockSpec((None, D, FSH), lambda j, t: (j, 0, 0)),
            pl.BlockSpec((None, ts, FSH), lambda j, t: (j, t, 0)), _sds((N_DEV, s, FSH)))
    act = _swiglu_fwd(z)
    f = _mm("ffn_out", act, wout, ((1,), (0,)), (s // ts, 4),
            pl.BlockSpec((None, ts, FSH), lambda t, k: (k, t, 0)), pl.BlockSpec((None, FSH, D), lambda t, k: (k, 0, 0)),
            pl.BlockSpec((ts, D), lambda t, k: (t, 0)), _sds((s, D)), acc_axis=1)
    return f, (h, z, act)


def _ffn_bwd(df, saved, win, wout):
    h, z, act = saved
    s = h.shape[0]
    ts = _tile(s)
    wout = wout.reshape(4, FSH, D)
    dact = _mm("ffn_out_dx", df, wout, ((1,), (1,)), (4, s // ts),
               pl.BlockSpec((ts, D), lambda k, t: (t, 0)), pl.BlockSpec((None, FSH, D), lambda k, t: (k, 0, 0)),
               pl.BlockSpec((None, ts, FSH), lambda k, t: (k, t, 0)), _sds((4, s, FSH)))
    dwout = _mm("ffn_out_dw", act, df, ((0,), (0,)), (4, 2),
                pl.BlockSpec((None, s, FSH), lambda k, j: (k, 0, 0)), pl.BlockSpec((s, D // 2), lambda k, j: (0, j)),
                pl.BlockSpec((None, FSH, D // 2), lambda k, j: (k, 0, j)), _sds((4, FSH, D), BF16))
    dz = _swiglu_bwd(z, dact)
    dh = _mm("ffn_in_dx", dz, win, ((1,), (1,)), (s // ts, N_DEV),
             pl.BlockSpec((None, ts, FSH), lambda t, j: (j, t, 0)), pl.BlockSpec((None, D, FSH), lambda t, j: (j, 0, 0)),
             pl.BlockSpec((ts, D), lambda t, j: (t, 0)), _sds((s, D)), acc_axis=1)
    dwin = _mm("ffn_in_dw", h, dz, ((0,), (0,)), (N_DEV, 2),
               pl.BlockSpec((s, D // 2), lambda j, i: (0, i)), pl.BlockSpec((None, s, FSH), lambda j, i: (j, 0, 0)),
               pl.BlockSpec((None, D // 2, FSH), lambda j, i: (j, i, 0)), _sds((N_DEV, D, FSH), BF16))
    return dh, dwin, dwout.reshape(N_DEV, D_FF // N_DEV, D)


def _shift_rows(v, k, row, s, back):
    if back:
        return jnp.where(row < s - k, pltpu.roll(v, s - k, 0), 0.0)
    return jnp.where(row >= k, pltpu.roll(v, k, 0), 0.0)


def _window_sum(v, w, row, s, back):
    k = 1
    while k < w:
        v = v + _shift_rows(v, k, row, s, back)
        k *= 2
    return v


def _pool_fwd(z, pool_w, pool_scale):
    s = z.shape[0]

    def body(z_ref, w_ref, sc_ref, y_ref, d_ref):
        row = lax.broadcasted_iota(jnp.int32, (s, HD), 0)
        for g, w in enumerate(POOL_WINDOWS):
            sl = slice(g * HD, (g + 1) * HD)
            a = z_ref[:, sl]
            cnt = jnp.minimum(row + 1, w).astype(F32)
            d = (_window_sum(a, w, row, s, False) / cnt - a).astype(BF16)
            d_ref[:, sl] = d
            y = jnp.dot(d, _bf(w_ref[g]), preferred_element_type=F32)
            y_ref[:, sl] = (y * sc_ref[:, sl]).astype(BF16)

    return _pc(body, "pool_fwd", (1,),
               [pl.BlockSpec((s, NH * HD), lambda i: (0, 0)), pl.BlockSpec((NH, HD, HD), lambda i: (0, 0, 0)),
                pl.BlockSpec((1, NH * HD), lambda i: (0, 0))],
               [pl.BlockSpec((s, NH * HD), lambda i: (0, 0))] * 2,
               [_sds((s, NH * HD), BF16)] * 2)(z, pool_w, pool_scale)


def _pool_bwd(dy, d, pool_w, pool_scale):
    s = dy.shape[0]

    def body(dy_ref, d_ref, w_ref, sc_ref, dz_ref, dw_ref, dsc_ref):
        row = lax.broadcasted_iota(jnp.int32, (s, HD), 0)
        for g, w in enumerate(POOL_WINDOWS):
            sl = slice(g * HD, (g + 1) * HD)
            dyg, dg, wg = dy_ref[:, sl], d_ref[:, sl], _bf(w_ref[g])
            yraw = jnp.dot(dg, wg, preferred_element_type=F32)
            dsc_ref[:, sl] = _colsum(dyg * yraw)
            dyr = _bf(dyg * sc_ref[:, sl])
            dw_ref[g] = lax.dot_general(dg, dyr, (((0,), (0,)), ((), ())), preferred_element_type=F32)
            dd = lax.dot_general(dyr, wg, (((1,), (1,)), ((), ())), preferred_element_type=F32)
            cnt = jnp.minimum(row + 1, w).astype(F32)
            dz_ref[:, sl] = (_window_sum(dd / cnt, w, row, s, True) - dd).astype(BF16)

    return _pc(body, "pool_bwd", (1,),
               [pl.BlockSpec((s, NH * HD), lambda i: (0, 0)), pl.BlockSpec((s, NH * HD), lambda i: (0, 0)),
                pl.BlockSpec((NH, HD, HD), lambda i: (0, 0, 0)), pl.BlockSpec((1, NH * HD), lambda i: (0, 0))],
               [pl.BlockSpec((s, NH * HD), lambda i: (0, 0)), pl.BlockSpec((NH, HD, HD), lambda i: (0, 0, 0)),
                pl.BlockSpec((1, NH * HD), lambda i: (0, 0))],
               [_sds((s, NH * HD), BF16), _sds((NH, HD, HD)), _sds((1, NH * HD))])(dy, d, pool_w, pool_scale)


def _gelu(v):
    return 0.5 * v * (1.0 + jnp.tanh(GELU_C * (v + 0.044715 * (v * v * v))))


def _gelu_grad(v):
    t = jnp.tanh(GELU_C * (v + 0.044715 * (v * v * v)))
    return 0.5 * (1.0 + t) + 0.5 * v * (1.0 - t * t) * (GELU_C * (1.0 + 3.0 * 0.044715 * (v * v)))


def _causal_mask():
    return lax.broadcasted_iota(jnp.int32, (HD, HD), 0) >= lax.broadcasted_iota(jnp.int32, (HD, HD), 1)


def _sgu_specs():
    w = NH * HD
    return [pl.BlockSpec((HD, w), lambda c: (c, 1)), pl.BlockSpec((HD, w), lambda c: (c, 2)),
            pl.BlockSpec((1, w), lambda c: (0, 0)), pl.BlockSpec((1, w), lambda c: (0, 0)),
            pl.BlockSpec((NH, HD, HD), lambda c: (0, 0, 0)), pl.BlockSpec((HD, LANE), lambda c: (0, 0))]


def _sgu_head(v, lng_ref, lnb_ref, w_ref, h):
    sl = slice(h * HD, (h + 1) * HD)
    vh = v[:, sl]
    xc = vh - jnp.mean(vh, axis=-1, keepdims=True)
    rs = lax.rsqrt(jnp.mean(xc * xc, axis=-1, keepdims=True) + EPS)
    vhat = xc * rs
    vn = _bf(vhat * lng_ref[:, sl] + lnb_ref[:, sl])
    wc = _bf(jnp.where(_causal_mask(), w_ref[h], 0.0))
    return sl, rs, vhat, vn, wc


def _sgu_fwd(z, ln_g, ln_b, sgu_w, sgu_bt):
    s = z.shape[0]

    def body(zu_ref, zv_ref, lng_ref, lnb_ref, w_ref, bt_ref, y_ref):
        u, v = _gelu(zu_ref[...]), _gelu(zv_ref[...])
        for h in range(NH):
            sl, _, _, vn, wc = _sgu_head(v, lng_ref, lnb_ref, w_ref, h)
            sp = jnp.dot(wc, vn, preferred_element_type=F32) + bt_ref[:, h:h + 1]
            y_ref[:, sl] = (u[:, sl] * sp).astype(BF16)

    return _pc(body, "sgu_fwd", (s // HD,), _sgu_specs(), pl.BlockSpec((HD, NH * HD), lambda c: (c, 0)),
               _sds((s, NH * HD), BF16))(z, z, ln_g, ln_b, sgu_w, sgu_bt)


def _sgu_bwd(z, dy, ln_g, ln_b, sgu_w, sgu_bt, head_sum):
    s = z.shape[0]
    w = NH * HD
    nc = s // HD

    def body(zu_ref, zv_ref, lng_ref, lnb_ref, w_ref, bt_ref, dy_ref, hs_ref,
             dzu_ref, dzv_ref, dlng_ref, dlnb_ref, dw_ref, dbt_ref, dsacc_ref):
        c = pl.program_id(0)
        first = c == 0
        zu, zv = zu_ref[...], zv_ref[...]
        u, v = _gelu(zu), _gelu(zv)
        dyv = dy_ref[...]
        gu, gv = _gelu_grad(zu), _gelu_grad(zv)
        ds = dyv * u
        _acc(dsacc_ref, first, ds)
        for h in range(NH):
            sl, rs, vhat, vn, wc = _sgu_head(v, lng_ref, lnb_ref, w_ref, h)
            sp = jnp.dot(wc, vn, preferred_element_type=F32) + bt_ref[:, h:h + 1]
            dzu_ref[:, sl] = (dyv[:, sl] * sp * gu[:, sl]).astype(BF16)
            dsh = _bf(ds[:, sl])
            dwh = lax.dot_general(dsh, vn, (((1,), (1,)), ((), ())), preferred_element_type=F32)
            dwh = jnp.where(_causal_mask(), dwh, 0.0)

            @pl.when(first)
            def _():
                dw_ref[h] = dwh

            @pl.when(jnp.logical_not(first))
            def _():
                dw_ref[h] += dwh

            dvn = lax.dot_general(wc, dsh, (((0,), (0,)), ((), ())), preferred_element_type=F32)
            g_col = _colsum(dvn * vhat)
            b_col = _colsum(dvn)

            @pl.when(first)
            def _():
                dlng_ref[:, sl] = g_col
                dlnb_ref[:, sl] = b_col

            @pl.when(jnp.logical_not(first))
            def _():
                dlng_ref[:, sl] += g_col
                dlnb_ref[:, sl] += b_col

            dvh = dvn * lng_ref[:, sl]
            dv = rs * (dvh - jnp.mean(dvh, axis=-1, keepdims=True) - vhat * jnp.mean(dvh * vhat, axis=-1, keepdims=True))
            dzv_ref[:, sl] = (dv * gv[:, sl]).astype(BF16)

        @pl.when(c == nc - 1)
        def _():
            dbt_ref[...] = jnp.dot(dsacc_ref[...], hs_ref[...], preferred_element_type=F32, precision=HIGHEST)

    outs = _pc(body, "sgu_bwd", (nc,),
               _sgu_specs() + [pl.BlockSpec((HD, w), lambda c: (c, 1)), pl.BlockSpec((w, LANE), lambda c: (0, 0))],
               [pl.BlockSpec((HD, w), lambda c: (c, 0))] * 2 + [pl.BlockSpec((1, w), lambda c: (0, 0))] * 2
               + [pl.BlockSpec((NH, HD, HD), lambda c: (0, 0, 0)), pl.BlockSpec((HD, LANE), lambda c: (0, 0))],
               [_sds((s, w), BF16)] * 2 + [_sds((1, w))] * 2 + [_sds((NH, HD, HD)), _sds((HD, LANE))],
               scratch=[pltpu.VMEM((HD, w), F32)])(z, z, ln_g, ln_b, sgu_w, sgu_bt, dy, head_sum)
    return outs


def _cmul(ar, ai, br, bi):
    return ar * br - ai * bi, ar * bi + ai * br


def _ssm_prep(lam_re, lam_im, lam_re_rep, lam_im_rep, log_dt, b_re, b_im):
    def disc(lr, li, dt):
        mag = jnp.exp(lr * dt)
        return mag * jnp.cos(li * dt), mag * jnp.sin(li * dt)

    def body(lr_ref, li_ref, lrr_ref, lir_ref, ldt_ref, br_ref, bi_ref, or_ref, oi_ref, bbr_ref, bbi_ref):
        dt = jnp.exp(ldt_ref[...])
        or_ref[...], oi_ref[...] = disc(lr_ref[...], li_ref[...], dt)
        lr, li = lrr_ref[...], lir_ref[...]
        er, ei = disc(lr, li, dt)
        den = lr * lr + li * li
        kr = ((er - 1.0) * lr + ei * li) / den
        ki = (ei * lr - (er - 1.0) * li) / den
        bbr_ref[...], bbi_ref[...] = _cmul(kr, ki, br_ref[...], bi_ref[...])

    small = pl.BlockSpec((SSM_G, SSM_P), lambda i: (0, 0))
    wide = pl.BlockSpec((SSM_G, SSM_P * SSM_N), lambda i: (0, 0))
    col = pl.BlockSpec((SSM_G, 1), lambda i: (0, 0))
    return _pc(body, "ssm_prep", (1,), [small, small, wide, wide, col, wide, wide], [small, small, wide, wide],
               [_sds((SSM_G, SSM_P))] * 2 + [_sds((SSM_G, SSM_P * SSM_N))] * 2)(
        lam_re, lam_im, lam_re_rep, lam_im_rep, log_dt, b_re, b_im)


def _ssm_param_bwd(g_lam_re, g_lam_im, g_bb_re, g_bb_im, lam_re, lam_im, lam_re_rep, lam_im_rep, log_dt, b_re, b_im, seg):
    def body(glr_ref, gli_ref, gbr_ref, gbi_ref, lr_ref, li_ref, lrr_ref, lir_ref, ldt_ref, br_ref, bi_ref, seg_ref,
             dlr_ref, dli_ref, ddt_ref, dbr_ref, dbi_ref):
        dt = jnp.exp(ldt_ref[...])
        lr, li = lrr_ref[...], lir_ref[...]
        mag = jnp.exp(lr * dt)
        er, ei = mag * jnp.cos(li * dt), mag * jnp.sin(li * dt)
        den = lr * lr + li * li
        kr = ((er - 1.0) * lr + ei * li) / den
        ki = (ei * lr - (er - 1.0) * li) / den
        gbr, gbi = gbr_ref[...], gbi_ref[...]
        dbr_ref[...], dbi_ref[...] = _cmul(kr, -ki, gbr, gbi)
        tr, ti = _cmul(br_ref[...], -bi_ref[...], gbr, gbi)
        gkr = jnp.dot(tr, seg_ref[...], preferred_element_type=F32, precision=HIGHEST)
        gki = jnp.dot(ti, seg_ref[...], preferred_element_type=F32, precision=HIGHEST)
        lr, li = lr_ref[...], li_ref[...]
        mag = jnp.exp(lr * dt)
        er, ei = mag * jnp.cos(li * dt), mag * jnp.sin(li * dt)
        den = lr * lr + li * li
        ir, ii = lr / den, -li / den
        kr, ki = _cmul(er - 1.0, ei, ir, ii)
        ar, ai = _cmul(ir, -ii, gkr, gki)
        glr, gli = glr_ref[...] + ar, gli_ref[...] + ai
        qr, qi = _cmul(kr, ki, ir, ii)
        g1r, g1i = _cmul(-qr, qi, gkr, gki)
        g2r, g2i = _cmul(dt * er, -dt * ei, glr, gli)
        dlr_ref[...] = g1r + g2r
        dli_ref[...] = g1i + g2i
        wr, wi = _cmul(lr, li, er, ei)
        g_dt = jnp.sum(wr * glr + wi * gli, axis=-1, keepdims=True)
        ddt_ref[...] = jnp.broadcast_to(dt * g_dt, (SSM_G, LANE))

    small = pl.BlockSpec((SSM_G, SSM_P), lambda i: (0, 0))
    wide = pl.BlockSpec((SSM_G, SSM_P * SSM_N), lambda i: (0, 0))
    col = pl.BlockSpec((SSM_G, 1), lambda i: (0, 0))
    segs = pl.BlockSpec((SSM_P * SSM_N, SSM_P), lambda i: (0, 0))
    return _pc(body, "ssm_param_bwd", (1,), [small, small, wide, wide, small, small, wide, wide, col, wide, wide, segs],
               [small, small, pl.BlockSpec((SSM_G, LANE), lambda i: (0, 0)), wide, wide],
               [_sds((SSM_G, SSM_P))] * 2 + [_sds((SSM_G, LANE))] + [_sds((SSM_G, SSM_P * SSM_N))] * 2)(
        g_lam_re, g_lam_im, g_bb_re, g_bb_im, lam_re, lam_im, lam_re_rep, lam_im_rep, log_dt, b_re, b_im, seg)


SCAN_LANES = 256
SCAN_ROWS = 8


def _ssm_scan(b_re, b_im, lam_re, lam_im, reverse):
    s = b_re.shape[0]
    nt = s // SCAN_ROWS
    ln, rows = SCAN_LANES, SCAN_ROWS

    def body(lr_ref, li_ref, br_ref, bi_ref, or_ref, oi_ref):
        l1 = (lr_ref[...], li_ref[...])
        pw = [l1]
        for _ in range(rows - 1):
            pw.append(_cmul(*pw[-1], *l1))
        row = lax.broadcasted_iota(jnp.int32, (rows, ln), 0)
        expo = (rows - row) if reverse else (row + 1)
        pr = jnp.zeros((rows, ln), F32)
        pi = jnp.zeros((rows, ln), F32)
        for e in range(1, rows + 1):
            pr = jnp.where(expo == e, pw[e - 1][0], pr)
            pi = jnp.where(expo == e, pw[e - 1][1], pi)
        lk = {k: (jnp.broadcast_to(pw[k - 1][0], (rows, ln)), jnp.broadcast_to(pw[k - 1][1], (rows, ln))) for k in (1, 2, 4)}

        def step(i, carry):
            cr, ci = carry
            t = (nt - 1 - i) if reverse else i
            r0 = pl.multiple_of(t * rows, rows)
            xr, xi = br_ref[pl.ds(r0, rows), :], bi_ref[pl.ds(r0, rows), :]
            for k in (1, 2, 4):
                sr = _shift_rows(xr, k, row, rows, reverse)
                si = _shift_rows(xi, k, row, rows, reverse)
                ar, ai = _cmul(lk[k][0], lk[k][1], sr, si)
                xr, xi = xr + ar, xi + ai
            ar, ai = _cmul(pr, pi, cr, ci)
            xr, xi = xr + ar, xi + ai
            or_ref[pl.ds(r0, rows), :] = xr
            oi_ref[pl.ds(r0, rows), :] = xi
            if reverse:
                return xr[0:1], xi[0:1]
            return xr[rows - 1:rows], xi[rows - 1:rows]

        zero = jnp.zeros((1, ln), F32)
        lax.fori_loop(0, nt, step, (zero, zero))

    vec = pl.BlockSpec((1, ln), lambda j: (0, j))
    blk = pl.BlockSpec((s, ln), lambda j: (0, j))
    return _pc(body, "ssm_scan_bwd" if reverse else "ssm_scan_fwd", (SSM_L // ln,), [vec, vec, blk, blk], [blk, blk],
               [_sds((s, SSM_L))] * 2)(lam_re, lam_im, b_re, b_im)


def _ssm_in(name, v, w_bd):
    s = v.shape[0]
    ts = _tile(s)
    half = SSM_GB * SSM_P

    def body(v_ref, w_ref, or_ref, oi_ref):
        r = jnp.dot(_bf(v_ref[...]), w_ref[...], preferred_element_type=F32)
        or_ref[...] = r[:, :half]
        oi_ref[...] = r[:, half:]

    out = pl.BlockSpec((ts, half), lambda q, t: (t, q))
    return _pc(body, name, (SSM_NB, s // ts),
               [pl.BlockSpec((ts, SSM_GB * SSM_N), lambda q, t: (t, q)), pl.BlockSpec((None, SSM_GB * SSM_N, 2 * half), lambda q, t: (q, 0, 0))],
               [out, out], [_sds((s, SSM_L))] * 2)(v, w_bd)


def _ssm_out(name, x_re, x_im, w_bd):
    s = x_re.shape[0]
    ts = _tile(s)
    half = SSM_GB * SSM_P
    nt = (((1,), (1,)), ((), ()))

    def body(xr_ref, xi_ref, w_ref, o_ref):
        w = w_ref[...]
        o_ref[...] = (lax.dot_general(_bf(xr_ref[...]), w[:, :half], nt, preferred_element_type=F32)
                      + lax.dot_general(_bf(xi_ref[...]), w[:, half:], nt, preferred_element_type=F32))

    xin = pl.BlockSpec((ts, half), lambda q, t: (t, q))
    return _pc(body, name, (SSM_NB, s // ts),
               [xin, xin, pl.BlockSpec((None, SSM_GB * SSM_N, 2 * half), lambda q, t: (q, 0, 0))],
               pl.BlockSpec((ts, SSM_GB * SSM_N), lambda q, t: (t, q)), _sds((s, SSM_G * SSM_N)))(x_re, x_im, w_bd)


def _ssm_outer(name, v, x_re, x_im):
    s = v.shape[0]
    ts = min(s, 512)
    half = SSM_GB * SSM_P
    tn = (((0,), (0,)), ((), ()))

    def body(v_ref, xr_ref, xi_ref, o_ref):
        vv = _bf(v_ref[...])
        pr = lax.dot_general(vv, _bf(xr_ref[...]), tn, preferred_element_type=F32)
        pi = lax.dot_general(vv, _bf(xi_ref[...]), tn, preferred_element_type=F32)
        first = pl.program_id(1) == 0

        @pl.when(first)
        def _():
            o_ref[:, :half] = pr
            o_ref[:, half:] = pi

        @pl.when(jnp.logical_not(first))
        def _():
            o_ref[:, :half] += pr
            o_ref[:, half:] += pi

    xin = pl.BlockSpec((ts, half), lambda q, t: (t, q))
    return _pc(body, name, (SSM_NB, s // ts), [pl.BlockSpec((ts, SSM_GB * SSM_N), lambda q, t: (t, q)), xin, xin],
               pl.BlockSpec((None, SSM_GB * SSM_N, 2 * half), lambda q, t: (q, 0, 0)),
               _sds((SSM_NB, SSM_GB * SSM_N, 2 * half)))(v, x_re, x_im)


def _ssm_dlam(x_re, x_im, a_re, a_im):
    s = x_re.shape[0]
    ln = SCAN_LANES

    def body(xr_ref, xi_ref, ar_ref, ai_ref, or_ref, oi_ref):
        row = lax.broadcasted_iota(jnp.int32, (s, ln), 0)
        xr = _shift_rows(xr_ref[...], 1, row, s, False)
        xi = _shift_rows(xi_ref[...], 1, row, s, False)
        ar, ai = ar_ref[...], ai_ref[...]
        or_ref[...] = _colsum(xr * ar + xi * ai)
        oi_ref[...] = _colsum(xr * ai - xi * ar)

    blk = pl.BlockSpec((s, ln), lambda j: (0, j))
    vec = pl.BlockSpec((1, ln), lambda j: (0, j))
    return _pc(body, "ssm_dlam", (SSM_L // ln,), [blk] * 4, [vec, vec], [_sds((1, SSM_L))] * 2)(x_re, x_im, a_re, a_im)


def _ssm_act_fwd(y, u, d_skip):
    s = y.shape[0]
    ts = min(s, 512)

    def body(y_ref, u_ref, d_ref, o_ref):
        o_ref[...] = _gelu(y_ref[...] + d_ref[...] * u_ref[...]).astype(BF16)

    return _pc(body, "ssm_act_fwd", (s // ts,), [_row_spec(ts, D)] * 2 + [_vec_spec(D)], _row_spec(ts, D),
               _sds((s, D), BF16))(y, u, d_skip)


def _ssm_act_bwd(dg, y, u, d_skip):
    s = y.shape[0]
    ts = min(s, 512)

    def body(dg_ref, y_ref, u_ref, d_ref, dy_ref, dd_ref):
        uv = u_ref[...]
        dy = dg_ref[...] * _gelu_grad(y_ref[...] + d_ref[...] * uv)
        dy_ref[...] = dy.astype(BF16)
        _acc(dd_ref, pl.program_id(0) == 0, _colsum(dy * uv))

    return _pc(body, "ssm_act_bwd", (s // ts,), [_row_spec(ts, D)] * 3 + [_vec_spec(D)], [_row_spec(ts, D), _vec_spec(D)],
               [_sds((s, D), BF16), _sds((1, D))])(dg, y, u, d_skip)


def _axpy(a, b, d_skip):
    s = a.shape[0]
    ts = min(s, 512)

    def body(a_ref, b_ref, d_ref, o_ref):
        o_ref[...] = (a_ref[...] + d_ref[...] * b_ref[...].astype(F32)).astype(BF16)

    return _pc(body, "ssm_du", (s // ts,), [_row_spec(ts, D)] * 2 + [_vec_spec(D)], _row_spec(ts, D),
               _sds((s, D), BF16))(a, b, d_skip)


def _glu_fwd(zz):
    s = zz.shape[0]
    ts = min(s, 512)

    def body(a_ref, b_ref, o_ref):
        o_ref[...] = a_ref[...] * _sigmoid(b_ref[...])

    return _pc(body, "glu_fwd", (s // ts,), [_row_spec(ts, D, 0), _row_spec(ts, D, 1)], _row_spec(ts, D), _sds((s, D)))(zz, zz)


def _glu_bwd(zz, df):
    s = zz.shape[0]
    ts = min(s, 512)

    def body(a_ref, b_ref, df_ref, o_ref):
        sg = _sigmoid(b_ref[...])
        dfv = df_ref[...].astype(F32)
        o_ref[:, :D] = (dfv * sg).astype(BF16)
        o_ref[:, D:] = (dfv * a_ref[...] * sg * (1.0 - sg)).astype(BF16)

    return _pc(body, "glu_bwd", (s // ts,), [_row_spec(ts, D, 0), _row_spec(ts, D, 1), _row_spec(ts, D)],
               _row_spec(ts, 2 * D), _sds((s, 2 * D), BF16))(zz, zz, df)


def _block_diag(m):
    g, r, c = m.shape
    eye = jnp.eye(SSM_GB, dtype=m.dtype)
    m = m.reshape(SSM_NB, SSM_GB, r, c)
    return (m[:, :, :, None, :] * eye[None, :, None, :, None]).reshape(SSM_NB, SSM_GB * r, SSM_GB * c)


def _diag_blocks(m, r, c):
    m = m.reshape(SSM_NB, SSM_GB, r, SSM_GB, c)
    idx = jnp.arange(SSM_GB)
    return m[:, idx, :, idx, :].transpose(1, 0, 2, 3).reshape(SSM_G, r, c)


def _mod_part(c_all, ada_w):
    n = ada_w.shape[-1]

    def body(c_ref, w_ref, o_ref):
        cv = c_ref[...]
        cond = _bf(cv * _sigmoid(cv))
        o_ref[...] = jnp.dot(cond, _bf(w_ref[...]), preferred_element_type=F32)

    return _pc(body, "mod_part", (2,), [pl.BlockSpec((N_DEV, D), lambda l: (0, 0)), pl.BlockSpec((None, D, n), lambda l: (l, 0, 0))],
               pl.BlockSpec((None, N_DEV, n), lambda l: (l, 0, 0)), _sds((2, N_DEV, n)))(c_all, ada_w)


def _ada_w_grad(c_all_t, dmod):
    n = dmod.shape[-1]
    tr = 128

    def body(c_ref, d_ref, o_ref):
        cv = c_ref[...]
        cond = _bf(cv * _sigmoid(cv)).astype(F32)
        dm = _bf(d_ref[...]).astype(F32)
        acc = cond[:, 0:1] * dm[0:1, :]
        for b in range(1, N_DEV):
            acc = acc + cond[:, b:b + 1] * dm[b:b + 1, :]
        o_ref[...] = acc

    return _pc(body, "ada_w_grad", (2, D // tr),
               [pl.BlockSpec((tr, N_DEV), lambda l, t: (t, 0)), pl.BlockSpec((None, N_DEV, n), lambda l, t: (l, 0, 0))],
               pl.BlockSpec((None, tr, n), lambda l, t: (l, t, 0)), _sds((2, D, n)))(c_all_t, dmod)


def _adamw(name, parts, w, m, v, slot=0, prev=None):
    p, r, c = parts.shape
    tr = r
    while tr * c * 4 > (1 << 20) and tr % 16 == 0:
        tr //= 2
    nt = r // tr

    def body(p_ref, w_ref, m_ref, v_ref, *rest):
        g_ref, d_ref, nm_ref, nv_ref = rest[-4:]
        g = p_ref[0].astype(F32)
        for i in range(1, p):
            g = g + p_ref[i].astype(F32)
        g_ref[...] = g
        m2 = B1 * m_ref[...] + (1.0 - B1) * g
        v2 = B2 * v_ref[...] + (1.0 - B2) * (g * g)
        nm_ref[...] = m2
        nv_ref[...] = v2
        m_hat = m2 / (1.0 - B1 ** STEP)
        v_hat = v2 / (1.0 - B2 ** STEP)
        d_ref[...] = -LR * (m_hat / (jnp.sqrt(v_hat) + ADAM_EPS) + WD * w_ref[...])

    blk = pl.BlockSpec((tr, c), lambda t: (slot * nt + t, 0))
    in_specs = [pl.BlockSpec((p, tr, c), lambda t: (0, t, 0)), blk, blk, blk]
    if prev is None:
        return _pc(body, name, (nt,), in_specs, [blk] * 4, [_sds(w.shape)] * 4)(parts, w, m, v)
    return pl.pallas_call(
        body, name=name, grid=(nt,), in_specs=in_specs + [pl.BlockSpec(memory_space=pl.ANY)] * 4, out_specs=[blk] * 4,
        out_shape=[_sds(w.shape)] * 4, input_output_aliases={4 + i: i for i in range(4)},
        compiler_params=pltpu.CompilerParams(dimension_semantics=("arbitrary",), vmem_limit_bytes=VMEM_LIMIT_BYTES))(parts, w, m, v, *prev)


def _sum_parts(parts):
    p, r, c = parts.shape
    tr = r
    while tr * c * 4 > (1 << 19) and tr % 16 == 0:
        tr //= 2

    def body(p_ref, o_ref):
        g = p_ref[0]
        for i in range(1, p):
            g = g + p_ref[i]
        o_ref[...] = g

    return _pc(body, "sum_parts", (r // tr,), [pl.BlockSpec((p, tr, c), lambda t: (0, t, 0))], pl.BlockSpec((tr, c), lambda t: (t, 0)),
               _sds((r, c)))(parts)


def _place():
    x, y, c = lax.axis_index("x"), lax.axis_index("y"), lax.axis_index("c")
    peers = []
    for k in range(1, N_DEV):
        px = (1 - x) if k & 4 else x
        py = (1 - y) if k & 2 else y
        pc = (1 - c) if k & 1 else c
        peers.append(((px, py, pc), 4 * px + 2 * py + pc))
    return 4 * x + 2 * y + c, peers


def _at(ref, idx):
    return ref if idx is None else ref.at[idx]


def _exchange_copies(plan, src_refs, dst_refs, send_sems, recv_sems, local_sems, with_arrivals=True):
    me, peers = _place()
    local = [pltpu.make_async_copy(_at(src_refs[si], sx), _at(dst_refs[di], dx), local_sems.at[i])
             for i, (si, sx, di, dx) in enumerate(plan(me, me))]

    n = len(local)

    def remote(k, i, dev, entry):
        si, sx, di, dx = entry
        return pltpu.make_async_remote_copy(_at(src_refs[si], sx), _at(dst_refs[di], dx), send_sems.at[k * n + i], recv_sems.at[k * n + i],
                                            device_id=dev, device_id_type=MESH)

    sends = [remote(k, i, dev, e) for k, (dev, peer) in enumerate(peers) for i, e in enumerate(plan(me, peer))]
    if not with_arrivals:
        return local, sends, []
    arrivals = [remote(k, i, dev, e) for k, (dev, peer) in enumerate(peers) for i, e in enumerate(plan(peer, me))]
    return local, sends, arrivals


def _sem_shapes(n_copies):
    return [pltpu.SemaphoreType.DMA(((N_DEV - 1) * n_copies,)), pltpu.SemaphoreType.DMA(((N_DEV - 1) * n_copies,)),
            pltpu.SemaphoreType.DMA((n_copies,))]


def _exchange(name, srcs, dst_shapes, plan, n_copies):
    ns, nd = len(srcs), len(dst_shapes)

    def body(*refs):
        local, sends, arrivals = _exchange_copies(plan, refs[:ns], refs[ns:ns + nd], *refs[ns + nd:])
        for cp in local + sends:
            cp.start()
        for cp in arrivals:
            cp.wait_recv()
        for cp in sends:
            cp.wait_send()
        for cp in local:
            cp.wait()

    any_spec = pl.BlockSpec(memory_space=pl.ANY)
    return pl.pallas_call(
        body, name=name, in_specs=[any_spec] * ns, out_specs=[any_spec] * nd, out_shape=list(dst_shapes),
        scratch_shapes=_sem_shapes(n_copies))(*srcs)


HBM_SPEC = pl.BlockSpec(memory_space=pltpu.HBM)
SEM_SPEC = pl.BlockSpec(memory_space=pltpu.SEMAPHORE)
SIDE_EFFECT = pltpu.SideEffectType.DATAFLOW_SIDE_EFFECTING


def _exchange_start(name, srcs, dst_shapes, plan, n_copies, order):
    ns, nd = len(srcs), len(dst_shapes)
    nb = ns + nd

    def body(*refs):
        local, sends, _ = _exchange_copies(plan, refs[:ns], refs[ns:nb], *refs[nb + 1:nb + 4], with_arrivals=False)
        for cp in local + sends:
            cp.start()
        refs[-1][...] = jnp.zeros((8, LANE), F32)

    lands = [pltpu.with_memory_space_constraint(lax.empty(d.shape, d.dtype), pltpu.HBM) for d in dst_shapes]
    srcs = [pltpu.with_memory_space_constraint(a, pltpu.HBM) for a in srcs]
    bufs = srcs + lands
    out = pl.pallas_call(
        body, name=name, in_specs=[HBM_SPEC] * nb + [pl.BlockSpec(memory_space=pl.ANY)],
        out_specs=[SEM_SPEC] * 3 + [HBM_SPEC] * nb + [pl.BlockSpec(memory_space=pltpu.VMEM)],
        out_shape=_sem_shapes(n_copies) + [pltpu.HBM(a.shape, a.dtype) for a in bufs] + [_sds((8, LANE))],
        input_output_aliases={i: 3 + i for i in range(nb)},
        compiler_params=pltpu.CompilerParams(has_side_effects=SIDE_EFFECT))(*bufs, order)
    return out[:3], out[3:3 + ns], out[3 + ns:3 + nb], out[-1]


def _exchange_wait(name, sems, srcs, lands, plan, after):
    ns, nd = len(srcs), len(lands)
    nb = ns + nd

    def body(*refs):
        local, sends, arrivals = _exchange_copies(plan, refs[:ns], refs[ns:nb], *refs[nb:nb + 3])
        for cp in arrivals:
            cp.wait_recv()
        for cp in sends:
            cp.wait_send()
        for cp in local:
            cp.wait()

    bufs = list(srcs) + list(lands)
    out = pl.pallas_call(
        body, name=name, in_specs=[HBM_SPEC] * nb + [SEM_SPEC] * 3 + [pl.BlockSpec(memory_space=pl.ANY)],
        out_specs=[HBM_SPEC] * nb, out_shape=[pltpu.HBM(a.shape, a.dtype) for a in bufs],
        input_output_aliases={i: i for i in range(nb)},
        compiler_params=pltpu.CompilerParams(has_side_effects=SIDE_EFFECT))(*bufs, *sems, after)
    return out[ns:]


def _all_gather(name, arrs):
    plan = lambda me, peer: [(i, None, i, me) for i in range(len(arrs))]
    return _exchange(name, arrs, [_sds((N_DEV,) + a.shape, a.dtype) for a in arrs], plan, len(arrs))


def _sublayer_fwd(x, fn, mod3, g_pre, g_post, rw):
    h = _prenorm_fwd(x, g_pre, mod3[1:2], mod3[0:1])
    f, saved = fn(h)
    return _postnorm_fwd(x, f, g_post, mod3[2:3], rw), (x, f, saved)


def _sublayer_bwd(dout, saved, fn_bwd, mod3, g_pre, g_post, rw):
    x, f, inner = saved
    df, dgate, dg_post = _postnorm_bwd(dout, f, g_post, mod3[2:3], rw)
    dh, extra = fn_bwd(df, inner)
    dx, dshift, dscale, dg_pre = _prenorm_bwd(dout, dh, x, g_pre, mod3[1:2])
    return dx, jnp.concatenate([dshift, dscale, dgate], axis=0), dg_pre, dg_post, extra


def _mix0_fwd(h, p):
    z = _mm_nn("mix0_in", h, p["ab_w_in"])
    y_a, d = _pool_fwd(z, p["pool_w"], p["pool_scale"])
    y_b = _sgu_fwd(z, p["sgu_ln_g"], p["sgu_ln_b"], p["sgu_w"], p["sgu_bt"])
    ycat = jnp.concatenate([y_a, y_b], axis=1)
    return _mm_nn("mix0_out", ycat, p["ab_w_out"]), (h, z, d, ycat)


def _mix0_bwd(df, saved, p):
    h, z, d, ycat = saved
    dycat = _mm_nt("mix0_out_dx", df, p["ab_w_out"])
    g = {"ab_w_out": _mm_tn("mix0_out_dw", ycat, df, BF16)}
    dz_p, g["pool_w"], g["pool_scale"] = _pool_bwd(dycat, d, p["pool_w"], p["pool_scale"])
    dz_u, dz_v, g["sgu_ln_g"], g["sgu_ln_b"], g["sgu_w"], dbt = _sgu_bwd(
        z, dycat, p["sgu_ln_g"], p["sgu_ln_b"], p["sgu_w"], p["sgu_bt"], p["head_sum"])
    g["sgu_b"] = dbt[:, :NH].T
    dz = jnp.concatenate([dz_p, dz_u, dz_v], axis=1)
    g["ab_w_in"] = _mm_tn("mix0_in_dw", h, dz, BF16)
    return _mm_nt("mix0_in_dx", dz, p["ab_w_in"]), g


def _mix1_fwd(h, p):
    u = _mm_nn("ssm_w_in", h, p["ssm_w_in"])
    bu_re, bu_im = _ssm_in("ssm_bu", u, p["wb_bd"])
    x_re, x_im = _ssm_scan(bu_re, bu_im, p["lam_bar_re"], p["lam_bar_im"], False)
    y = _ssm_out("ssm_y", x_re, x_im, p["wc_bd"])
    g = _ssm_act_fwd(y, u, p["ssm_d"])
    zz = _mm_nn("ssm_glu", g, p["ssm_w_glu"])
    return _glu_fwd(zz), (h, u, x_re, x_im, y, g, zz)


def _mix1_bwd(df, saved, p):
    h, u, x_re, x_im, y, g, zz = saved
    gr = {}
    dzz = _glu_bwd(zz, df)
    dg = _mm_nt("ssm_glu_dx", dzz, p["ssm_w_glu"])
    gr["ssm_w_glu"] = _mm_tn("ssm_glu_dw", g, dzz, BF16)
    dy, gr["ssm_d"] = _ssm_act_bwd(dg, y, u, p["ssm_d"])
    gx_re, gx_im = _ssm_in("ssm_gx", dy, p["wct_bd"])
    a_re, a_im = _ssm_scan(gx_re, gx_im, p["lam_bar_re"], -p["lam_bar_im"], True)
    du_ssm = _ssm_out("ssm_du_mm", a_re, a_im, p["wbt_bd"])
    du = _axpy(du_ssm, dy, p["ssm_d"])
    gr["ssm_w_in"] = _mm_tn("ssm_w_in_dw", h, du, BF16)
    dh = _mm_nt("ssm_w_in_dx", du, p["ssm_w_in"])
    g_lam_re, g_lam_im = _ssm_dlam(x_re, x_im, a_re, a_im)
    m_b = _ssm_outer("ssm_db", u, a_re, a_im)
    m_c = _ssm_outer("ssm_dc", dy, x_re, x_im)
    half = SSM_GB * SSM_P
    gbb_re = _diag_blocks(m_b[:, :, :half], SSM_N, SSM_P).transpose(0, 2, 1).reshape(SSM_G, SSM_P * SSM_N)
    gbb_im = _diag_blocks(m_b[:, :, half:], SSM_N, SSM_P).transpose(0, 2, 1).reshape(SSM_G, SSM_P * SSM_N)
    gr["ssm_c_re"] = _diag_blocks(m_c[:, :, :half], SSM_N, SSM_P)
    gr["ssm_c_im"] = -_diag_blocks(m_c[:, :, half:], SSM_N, SSM_P)
    dlr, dli, ddt, dbr, dbi = _ssm_param_bwd(
        g_lam_re.reshape(SSM_G, SSM_P), g_lam_im.reshape(SSM_G, SSM_P), gbb_re, gbb_im,
        p["lam_re"], p["lam_im"], p["lam_re_rep"], p["lam_im_rep"], p["log_dt"], p["b_re"], p["b_im"], p["seg"])
    gr["ssm_lam_re"], gr["ssm_lam_im"], gr["ssm_log_dt"] = dlr, dli, ddt[:, 0]
    gr["ssm_b_re"] = dbr.reshape(SSM_G, SSM_P, SSM_N)
    gr["ssm_b_im"] = dbi.reshape(SSM_G, SSM_P, SSM_N)
    return dh, gr


def _ssm_params(lam_re, lam_im, b_re, b_im, c_re, c_im, log_dt):
    p = {"lam_re": lam_re, "lam_im": lam_im, "log_dt": log_dt.reshape(SSM_G, 1),
         "lam_re_rep": jnp.repeat(lam_re, SSM_N, axis=1), "lam_im_rep": jnp.repeat(lam_im, SSM_N, axis=1),
         "b_re": b_re.reshape(SSM_G, SSM_P * SSM_N), "b_im": b_im.reshape(SSM_G, SSM_P * SSM_N)}
    lbr, lbi, bbr, bbi = _ssm_prep(lam_re, lam_im, p["lam_re_rep"], p["lam_im_rep"], p["log_dt"], p["b_re"], p["b_im"])
    p["lam_bar_re"], p["lam_bar_im"] = lbr.reshape(1, SSM_L), lbi.reshape(1, SSM_L)
    bbr, bbi = bbr.reshape(SSM_G, SSM_P, SSM_N), bbi.reshape(SSM_G, SSM_P, SSM_N)
    p["wb_bd"] = jnp.concatenate([_block_diag(bbr.transpose(0, 2, 1)), _block_diag(bbi.transpose(0, 2, 1))], axis=2).astype(BF16)
    p["wbt_bd"] = p["wb_bd"]
    p["wc_bd"] = jnp.concatenate([_block_diag(c_re), _block_diag(-c_im)], axis=2).astype(BF16)
    p["wct_bd"] = p["wc_bd"]
    p["seg"] = jnp.repeat(jnp.eye(SSM_P, dtype=F32), SSM_N, axis=0)
    return p


RES_WEIGHT = (0.5, 1.0, 0.5)


def _local_step(x, tgt, mod, norm_pre, norm_post, weights_of, on_grads):
    def fns(i, w):
        if i % 3 != 1:
            return (lambda h: _ffn_fwd(h, *w)), (lambda df, sv: (lambda r: (r[0], r[1:]))(_ffn_bwd(df, sv, *w)))
        if i == 1:
            return (lambda h: _mix0_fwd(h, w)), (lambda df, sv: _mix0_bwd(df, sv, w))
        return (lambda h: _mix1_fwd(h, w)), (lambda df, sv: _mix1_bwd(df, sv, w))

    saved, bwd = [], []
    for i in range(6):
        l, s = divmod(i, 3)
        f, b = fns(i, weights_of(i, x))
        x, sv = _sublayer_fwd(x, f, mod[l, s], norm_pre[l, s][None], norm_post[l, s][None], RES_WEIGHT[s])
        saved.append(sv)
        bwd.append(b)
    loss_row, dx = _loss_fwd_bwd(x, tgt)
    dmod, dpre, dpost, extras = [None] * 6, [None] * 6, [None] * 6, [None] * 6
    token = jnp.zeros((8, LANE), F32)
    for i in reversed(range(6)):
        l, s = divmod(i, 3)
        mod3 = mod[l, s] + token[0:1, 0:1]
        dx, dmod[i], dpre[i], dpost[i], extras[i] = _sublayer_bwd(
            dx, saved[i], bwd[i], mod3, norm_pre[l, s][None], norm_post[l, s][None], RES_WEIGHT[s])
        token = on_grads(i, extras[i])
    dmod = jnp.stack(dmod).reshape(2, 3, 3, D)
    dpre = jnp.concatenate(dpre, axis=0).reshape(2, 3, D)
    dpost = jnp.concatenate(dpost, axis=0).reshape(2, 3, D)
    return loss_row, dx, dmod, dpre, dpost, extras


def _pad_rows(v, rows):
    return jnp.pad(v, (0, rows * LANE - v.shape[0])).reshape(rows, LANE)


def _pack(parts):
    flat, layout, off = [], [], 0
    for a in parts:
        n = a.size
        padded = -(-n // LANE) * LANE
        flat.append(jnp.pad(a.reshape(-1).astype(F32), (0, padded - n)))
        layout.append((off, n, a.shape))
        off += padded
    return jnp.concatenate(flat), layout


def _unpack(flat, layout):
    return [flat[off:off + n].reshape(shape) for off, n, shape in layout]


SMALL_REPLICATED = ["ada_b", "pool_w", "pool_scale", "sgu_ln_g", "sgu_ln_b", "sgu_w", "sgu_b", "ssm_lam_re", "ssm_lam_im",
                    "ssm_b_re", "ssm_b_im", "ssm_c_re", "ssm_c_im", "ssm_log_dt"]
SMALL_SHARDED = ["norm_pre", "norm_post", "ssm_d"]
WEIGHTS = ['ada_w', 'ada_b', 'norm_pre', 'norm_post', 'ffn_w_in', 'ffn_w_out', 'ab_w_in', 'pool_w', 'pool_scale', 'sgu_ln_g',
           'sgu_ln_b', 'sgu_w', 'sgu_b', 'ab_w_out', 'ssm_w_in', 'ssm_lam_re', 'ssm_lam_im', 'ssm_b_re', 'ssm_b_im', 'ssm_c_re',
           'ssm_c_im', 'ssm_d', 'ssm_log_dt', 'ssm_w_glu']


def kernel(x, c, ada_w, ada_b, norm_pre, norm_post, ffn_w_in, ffn_w_out, ab_w_in, pool_w, pool_scale, sgu_ln_g, sgu_ln_b, sgu_w, sgu_b, ab_w_out, ssm_w_in, ssm_lam_re, ssm_lam_im, ssm_b_re, ssm_b_im, ssm_c_re, ssm_c_im, ssm_d, ssm_log_dt, ssm_w_glu, loss_target, m_ada_w, m_ada_b, m_norm_pre, m_norm_post, m_ffn_w_in, m_ffn_w_out, m_ab_w_in, m_pool_w, m_pool_scale, m_sgu_ln_g, m_sgu_ln_b, m_sgu_w, m_sgu_b, m_ab_w_out, m_ssm_w_in, m_ssm_lam_re, m_ssm_lam_im, m_ssm_b_re, m_ssm_b_im, m_ssm_c_re, m_ssm_c_im, m_ssm_d, m_ssm_log_dt, m_ssm_w_glu, v_ada_w, v_ada_b, v_norm_pre, v_norm_post, v_ffn_w_in, v_ffn_w_out, v_ab_w_in, v_pool_w, v_pool_scale, v_sgu_ln_g, v_sgu_ln_b, v_sgu_w, v_sgu_b, v_ab_w_out, v_ssm_w_in, v_ssm_lam_re, v_ssm_lam_im, v_ssm_b_re, v_ssm_b_im, v_ssm_c_re, v_ssm_c_im, v_ssm_d, v_ssm_log_dt, v_ssm_w_glu):
    args = locals()
    wts = {n: args[n] for n in WEIGHTS}
    mom = {n: args["m_" + n] for n in WEIGHTS}
    var = {n: args["v_" + n] for n in WEIGHTS}
    me = 4 * lax.axis_index("x") + 2 * lax.axis_index("y") + lax.axis_index("c")
    s = x.shape[1]
    nd = D // N_DEV

    shards = [[ffn_w_in[0, 0], ffn_w_out[0, 0]], [ab_w_in[0], ab_w_out[0]], [ffn_w_in[0, 1], ffn_w_out[0, 1]],
              [ffn_w_in[1, 0], ffn_w_out[1, 0]], [ssm_w_in[0], ssm_w_glu[0]], [ffn_w_in[1, 1], ffn_w_out[1, 1]]]
    gather_plan = lambda me_, peer_: [(0, None, 0, me_), (1, None, 1, me_)]
    gathers = []
    token = jnp.zeros((8, LANE), F32)
    for i, pair in enumerate(shards):
        pair = [a.astype(BF16) for a in pair]
        sems, srcs_thru, lands, token = _exchange_start(
            f"gather_start_{i}", pair, [_sds((N_DEV,) + a.shape, BF16) for a in pair], gather_plan, 2, token)
        gathers.append((sems, srcs_thru, lands))

    small_in, small_in_layout = _pack([c, norm_pre, norm_post, ssm_d])
    small_rows = -(-small_in.shape[0] // (8 * LANE)) * 8
    (g_small,) = _all_gather("gather_small", [_pad_rows(small_in, small_rows) + token[0:1, 0:1]])
    g_small = g_small.reshape(N_DEV, -1)
    c_all, npre_g, npost_g, sd_g = [jnp.stack([_unpack(g_small[j], small_in_layout)[i] for j in range(N_DEV)]) for i in range(4)]
    c_all = c_all.reshape(N_DEV, D)
    norm_pre_full = npre_g.transpose(1, 2, 0, 3).reshape(2, 3, D)
    norm_post_full = npost_g.transpose(1, 2, 0, 3).reshape(2, 3, D)
    ssm_d_full = sd_g.transpose(1, 0, 2).reshape(1, D)

    nw = ada_w.shape[-1]
    (mod_g,) = _all_gather("gather_mod", [_mod_part(c_all, ada_w)])
    mod = lax.dynamic_index_in_dim(mod_g, me, axis=2, keepdims=False)
    mod = (mod.transpose(1, 0, 2).reshape(2, N_DEV * nw) + ada_b).reshape(2, 3, 3, D)

    head_sum = jnp.repeat(jnp.eye(NH, LANE, dtype=F32), HD, axis=0)
    mix0 = {"pool_w": pool_w[0], "pool_scale": pool_scale, "sgu_ln_g": sgu_ln_g, "sgu_ln_b": sgu_ln_b, "sgu_w": sgu_w[0],
            "sgu_bt": jnp.pad(sgu_b[0].T, ((0, 0), (0, LANE - NH))), "head_sum": head_sum}
    mix1 = _ssm_params(ssm_lam_re[0], ssm_lam_im[0], ssm_b_re[0], ssm_b_im[0], ssm_c_re[0], ssm_c_im[0], ssm_log_dt[0])
    mix1["ssm_d"] = ssm_d_full

    def weights_of(i, x_in):
        sems, srcs_thru, lands = gathers[i]
        a, b = _exchange_wait(f"gather_wait_{i}", sems, srcs_thru, lands, gather_plan, x_in)
        if i % 3 != 1:
            return a, b
        if i == 1:
            return dict(mix0, ab_w_in=a.transpose(1, 0, 2).reshape(D, -1), ab_w_out=b.reshape(D, D))
        return dict(mix1, ssm_w_in=a.reshape(D, D), ssm_w_glu=b.transpose(1, 0, 2).reshape(D, -1))

    def shard_cols(a):
        r = a.shape[0]
        return a.reshape(r, N_DEV, -1).transpose(1, 0, 2)

    scatter_plan = lambda me_, peer_: [(0, peer_, 0, me_), (1, peer_, 1, me_)]
    scatters = [None] * 6
    last_token = [jnp.zeros((8, LANE), F32)]

    def on_grads(i, extra):
        if i % 3 != 1:
            parts = list(extra)
        elif i == 1:
            parts = [shard_cols(extra["ab_w_in"]), extra["ab_w_out"].reshape(N_DEV, nd, D)]
        else:
            parts = [extra["ssm_w_in"].reshape(N_DEV, nd, D), shard_cols(extra["ssm_w_glu"])]
        sems, srcs_thru, lands, last_token[0] = _exchange_start(
            f"scatter_start_{i}", parts, [_sds(a.shape, BF16) for a in parts], scatter_plan, 2, last_token[0])
        scatters[i] = (sems, srcs_thru, lands)
        return last_token[0]

    loss_row, grad_x, dmod, dpre, dpost, extras = _local_step(x[0], loss_target[0], mod, norm_pre_full, norm_post_full, weights_of, on_grads)
    g0, g1 = extras[1], extras[4]

    g1["ssm_log_dt"] = g1["ssm_log_dt"].reshape(1, SSM_G)
    small = {"ada_b": dmod.reshape(2, 9 * D), "norm_pre": dpre, "norm_post": dpost, "ssm_d": g1["ssm_d"]}
    small.update({n: g0[n] for n in ["pool_w", "pool_scale", "sgu_ln_g", "sgu_ln_b", "sgu_w", "sgu_b"]})
    small.update({n: g1[n] for n in ["ssm_lam_re", "ssm_lam_im", "ssm_b_re", "ssm_b_im", "ssm_c_re", "ssm_c_im", "ssm_log_dt"]})
    names = SMALL_REPLICATED + SMALL_SHARDED
    flat, layout = _pack([small[n].reshape(wts[n].shape if n in SMALL_REPLICATED else small[n].shape) for n in names] + [loss_row])
    rows = -(-flat.shape[0] // (8 * LANE)) * 8
    (g_parts,) = _all_gather("gather_small_grads", [_pad_rows(flat, rows)])
    total = _sum_parts(g_parts).reshape(-1)
    sums = dict(zip(names + ["loss"], _unpack(total, layout)))
    loss = sums["loss"][0, 0]

    out_g, out_d, out_m, out_v = {}, {}, {}, {}

    def adam_flat(name, gnames, grads):
        gf, lay = _pack(grads)
        r = -(-gf.shape[0] // (8 * LANE)) * 8
        packed = [_pad_rows(_pack([t[n] for n in gnames])[0], r) for t in (wts, mom, var)]
        res = _adamw(name, _pad_rows(gf, r)[None], *packed)
        for o, arr in zip((out_g, out_d, out_m, out_v), res):
            o.update(dict(zip(gnames, _unpack(arr.reshape(-1), lay))))

    adam_flat("adamw_replicated", SMALL_REPLICATED, [sums[n] for n in SMALL_REPLICATED])
    sliced = [lax.dynamic_slice_in_dim(sums[n], me * nd, nd, axis=sums[n].ndim - 1).reshape(wts[n].shape) for n in SMALL_SHARDED]
    adam_flat("adamw_sliced", SMALL_SHARDED, sliced)

    big_out = {}

    def adam_big(name, recv, n, slot=0):
        c_ = wts[n].shape[-1]
        big_out[n] = _adamw(name, recv.reshape(recv.shape[0], -1, c_), *[t[n].reshape(-1, c_) for t in (wts, mom, var)],
                            slot=slot, prev=big_out.get(n))
        return big_out[n][0]

    big_names = [("ffn_w_in", "ffn_w_out"), ("ab_w_in", "ab_w_out"), ("ffn_w_in", "ffn_w_out"),
                 ("ffn_w_in", "ffn_w_out"), ("ssm_w_in", "ssm_w_glu"), ("ffn_w_in", "ffn_w_out")]
    ffn_slot = {0: 0, 2: 1, 3: 2, 5: 3}
    after = grad_x
    for i in reversed(range(6)):
        sems, srcs_thru, lands = scatters[i]
        recv = _exchange_wait(f"scatter_wait_{i}", sems, srcs_thru, lands, scatter_plan, after)
        for n, r in zip(big_names[i], recv):
            after = adam_big(f"adamw_{n}_{i}", r, n, ffn_slot.get(i, 0))

    dmod_all = g_parts.reshape(N_DEV, -1)[:, layout[0][0]:layout[0][0] + layout[0][1]].reshape(N_DEV, 2, N_DEV, nw)
    dmod_mine = lax.dynamic_index_in_dim(dmod_all, me, axis=2, keepdims=False).transpose(1, 0, 2)
    g_ada_w = _ada_w_grad(c_all.T, dmod_mine)
    adam_big("adamw_ada_w", g_ada_w[None], "ada_w")
    for n, res in big_out.items():
        for o, arr in zip((out_g, out_d, out_m, out_v), res):
            o[n] = arr.reshape(wts[n].shape)

    return (loss, grad_x[None], *[out_g[n] for n in WEIGHTS], *[out_d[n] for n in WEIGHTS],
            *[out_m[n] for n in WEIGHTS], *[out_v[n] for n in WEIGHTS])
```

```python
import functools
import math

import jax
import jax.numpy as jnp
from jax import lax
from jax.experimental import pallas as pl
from jax.experimental.pallas import tpu as pltpu

F32 = jnp.float32
BF16 = jnp.bfloat16
MESH = pl.DeviceIdType.MESH
HIGHEST = lax.Precision.HIGHEST

N_DEV = 8
D = 1024
D_FF = 2816
FSH = 2 * D_FF // N_DEV
EPS = 1e-6
POOL_WINDOWS = (2, 4, 8, 16)
HD = 128
NH = 4
SSM_G, SSM_P, SSM_N = 64, 64, 16
SSM_GB = 16
SSM_NB = SSM_G // SSM_GB
SSM_L = SSM_G * SSM_P
LR, B1, B2, ADAM_EPS, WD, STEP = 0.001, 0.9, 0.999, 1e-08, 0.01, 10
GELU_C = math.sqrt(2.0 / math.pi)
VMEM_LIMIT_BYTES = 48 * 1024 * 1024
LANE = 128


def _pc(body, name, grid, in_specs, out_specs, out_shape, scratch=()):
    return pl.pallas_call(
        body, name=name, grid=grid, in_specs=in_specs, out_specs=out_specs, out_shape=out_shape,
        scratch_shapes=list(scratch),
        compiler_params=pltpu.CompilerParams(dimension_semantics=("arbitrary",) * len(grid),
                                             vmem_limit_bytes=VMEM_LIMIT_BYTES))


def _sds(shape, dtype=F32):
    return jax.ShapeDtypeStruct(tuple(shape), dtype)


def _bf(v):
    return v if v.dtype == BF16 else v.astype(BF16)


def _row_spec(ts, width, col=0):
    return pl.BlockSpec((ts, width), lambda t, _c=col: (t, _c))


def _vec_spec(width, col=0):
    return pl.BlockSpec((1, width), lambda t, _c=col: (0, _c))


def _mm(name, a, b, contract, grid, a_spec, b_spec, o_spec, out_shape, acc_axis=None):
    dn = (contract, ((), ()))

    def body(a_ref, b_ref, o_ref):
        r = lax.dot_general(_bf(a_ref[...]), _bf(b_ref[...]), dn, preferred_element_type=F32)
        if acc_axis is None:
            o_ref[...] = r.astype(o_ref.dtype)
        else:
            k = pl.program_id(acc_axis)

            @pl.when(k == 0)
            def _():
                o_ref[...] = r

            @pl.when(k > 0)
            def _():
                o_ref[...] += r

    return _pc(body, name, grid, [a_spec, b_spec], o_spec, out_shape)(a, b)


def _tile(s):
    return min(s, 1024)


def _div_tile(n, cap=1024):
    t = min(n, cap) // LANE * LANE
    while n % t:
        t -= LANE
    return t


def _mm_nn(name, a, b, out_dtype=F32):
    s, k = a.shape
    n = b.shape[1]
    ts, tn = _tile(s), _div_tile(n)
    return _mm(name, a, b, ((1,), (0,)), (n // tn, s // ts),
               pl.BlockSpec((ts, k), lambda j, t: (t, 0)), pl.BlockSpec((k, tn), lambda j, t: (0, j)),
               pl.BlockSpec((ts, tn), lambda j, t: (t, j)), _sds((s, n), out_dtype))


def _mm_nt(name, a, b, out_dtype=F32):
    s, n = a.shape
    k = b.shape[0]
    ts, tk = _tile(s), min(k, 1024)
    return _mm(name, a, b, ((1,), (1,)), (k // tk, s // ts),
               pl.BlockSpec((ts, n), lambda j, t: (t, 0)), pl.BlockSpec((tk, n), lambda j, t: (j, 0)),
               pl.BlockSpec((ts, tk), lambda j, t: (t, j)), _sds((s, k), out_dtype))


def _mm_tn(name, a, b, out_dtype=F32, tm=512, tn=512):
    s, m = a.shape
    n = b.shape[1]
    tm, tn = min(m, tm), min(n, tn)
    return _mm(name, a, b, ((0,), (0,)), (m // tm, n // tn),
               pl.BlockSpec((s, tm), lambda i, j: (0, i)), pl.BlockSpec((s, tn), lambda i, j: (0, j)),
               pl.BlockSpec((tm, tn), lambda i, j: (i, j)), _sds((m, n), out_dtype))


def _rstd(v):
    return lax.rsqrt(jnp.mean(v * v, axis=-1, keepdims=True) + EPS)


def _prenorm_fwd(x, g, scale, shift):
    s = x.shape[0]
    ts = min(s, 512)

    def body(x_ref, g_ref, sc_ref, sh_ref, h_ref):
        xv = x_ref[...]
        h_ref[...] = ((xv * _rstd(xv) * g_ref[...]) * (1.0 + sc_ref[...]) + sh_ref[...]).astype(BF16)

    return _pc(body, "prenorm_fwd", (s // ts,), [_row_spec(ts, D)] + [_vec_spec(D)] * 3, _row_spec(ts, D),
               _sds((s, D), BF16))(x, g, scale, shift)


def _postnorm_fwd(x, f, g, gate, rw):
    s = x.shape[0]
    ts = min(s, 512)

    def body(x_ref, f_ref, g_ref, gt_ref, o_ref):
        fv = f_ref[...]
        o_ref[...] = x_ref[...] + (rw * gt_ref[...]) * (fv * _rstd(fv) * g_ref[...])

    return _pc(body, "postnorm_fwd", (s // ts,), [_row_spec(ts, D)] * 2 + [_vec_spec(D)] * 2, _row_spec(ts, D),
               _sds((s, D)))(x, f, g, gate)


def _acc(ref, first, v):
    @pl.when(first)
    def _():
        ref[...] = v

    @pl.when(jnp.logical_not(first))
    def _():
        ref[...] += v


def _colsum(v):
    return jnp.sum(v, axis=0, keepdims=True)


def _postnorm_bwd(dout, f, g, gate, rw):
    s = dout.shape[0]
    ts = min(s, 512)

    def body(do_ref, f_ref, g_ref, gt_ref, df_ref, dgate_ref, dg_ref):
        first = pl.program_id(0) == 0
        do, fv, gv = do_ref[...], f_ref[...], g_ref[...]
        r = _rstd(fv)
        fn = fv * r
        _acc(dgate_ref, first, rw * _colsum(do * (fn * gv)))
        dy = (rw * gt_ref[...]) * do
        _acc(dg_ref, first, _colsum(dy * fn))
        dfn = dy * gv
        df_ref[...] = (r * (dfn - fn * jnp.mean(dfn * fn, axis=-1, keepdims=True))).astype(BF16)

    return _pc(body, "postnorm_bwd", (s // ts,), [_row_spec(ts, D)] * 2 + [_vec_spec(D)] * 2,
               [_row_spec(ts, D), _vec_spec(D), _vec_spec(D)],
               [_sds((s, D), BF16), _sds((1, D)), _sds((1, D))])(dout, f, g, gate)


def _prenorm_bwd(dout, dh, x, g, scale):
    s = dout.shape[0]
    ts = min(s, 512)

    def body(do_ref, dh_ref, x_ref, g_ref, sc_ref, dx_ref, dsh_ref, dsc_ref, dg_ref):
        first = pl.program_id(0) == 0
        dhv, xv, gv = dh_ref[...], x_ref[...], g_ref[...]
        r = _rstd(xv)
        xn = xv * r
        _acc(dsh_ref, first, _colsum(dhv))
        _acc(dsc_ref, first, _colsum(dhv * (xn * gv)))
        dhp = dhv * (1.0 + sc_ref[...])
        _acc(dg_ref, first, _colsum(dhp * xn))
        dxn = dhp * gv
        dx_ref[...] = do_ref[...] + r * (dxn - xn * jnp.mean(dxn * xn, axis=-1, keepdims=True))

    return _pc(body, "prenorm_bwd", (s // ts,), [_row_spec(ts, D)] * 3 + [_vec_spec(D)] * 2,
               [_row_spec(ts, D)] + [_vec_spec(D)] * 3,
               [_sds((s, D))] + [_sds((1, D))] * 3)(dout, dh, x, g, scale)


def _loss_fwd_bwd(y, tgt):
    s = y.shape[0]
    ts = min(s, 512)
    nt = s // ts

    def body(y_ref, t_ref, loss_ref, dy_ref, acc_ref):
        t = pl.program_id(0)
        e = y_ref[...] - t_ref[...]
        dy_ref[...] = e * (1.0 / D)
        _acc(acc_ref, t == 0, _colsum(e * e))

        @pl.when(t == nt - 1)
        def _():
            loss_ref[...] = jnp.full((1, LANE), 0.5 / D, F32) * jnp.sum(acc_ref[...])

    return _pc(body, "loss", (nt,), [_row_spec(ts, D)] * 2,
               [pl.BlockSpec((1, LANE), lambda t: (0, 0)), _row_spec(ts, D)],
               [_sds((1, LANE)), _sds((s, D))], scratch=[pltpu.VMEM((1, D), F32)])(y, tgt)


def _sigmoid(v):
    return 1.0 / (1.0 + jnp.exp(-v))


def _swiglu_fwd(z):
    _, s, _ = z.shape
    ts = min(s, 512)
    z4 = z.reshape(2, 4, s, FSH)

    def body(z_ref, o_ref):
        a, b = z_ref[0], z_ref[1]
        o_ref[...] = (a * _sigmoid(a) * b).astype(BF16)

    return _pc(body, "swiglu_fwd", (4, s // ts), [pl.BlockSpec((2, None, ts, FSH), lambda k, t: (0, k, t, 0))],
               pl.BlockSpec((None, ts, FSH), lambda k, t: (k, t, 0)), _sds((4, s, FSH), BF16))(z4)


def _swiglu_bwd(z, dact):
    _, s, _ = z.shape
    ts = min(s, 512)
    z4 = z.reshape(2, 4, s, FSH)

    def body(z_ref, d_ref, o_ref):
        a, b, d = z_ref[0], z_ref[1], d_ref[...]
        sg = _sigmoid(a)
        o_ref[0] = (d * b * (sg * (1.0 + a * (1.0 - sg)))).astype(BF16)
        o_ref[1] = (d * (a * sg)).astype(BF16)

    spec = pl.BlockSpec((2, None, ts, FSH), lambda k, t: (0, k, t, 0))
    out = _pc(body, "swiglu_bwd", (4, s // ts), [spec, pl.BlockSpec((None, ts, FSH), lambda k, t: (k, t, 0))],
              spec, _sds((2, 4, s, FSH), BF16))(z4, dact)
    return out.reshape(8, s, FSH)


def _ffn_fwd(h, win, wout):
    s = h.shape[0]
    ts = _tile(s)
    wout = wout.reshape(4, FSH, D)
    z = _mm("ffn_in", h, win, ((1,), (0,)), (N_DEV, s // ts),
            pl.BlockSpec((ts, D), lambda j, t: (t, 0)), pl.BlockSpec((None, D, FSH), lambda j, t: (j, 0, 0)),
            pl.BlockSpec((None, ts, FSH), lambda j, t: (j, t, 0)), _sds((N_DEV, s, FSH)))
    act = _swiglu_fwd(z)
    f = _mm("ffn_out", act, wout, ((1,), (0,)), (s // ts, 4),
            pl.BlockSpec((None, ts, FSH), lambda t, k: (k, t, 0)), pl.BlockSpec((None, FSH, D), lambda t, k: (k, 0, 0)),
            pl.BlockSpec((ts, D), lambda t, k: (t, 0)), _sds((s, D)), acc_axis=1)
    return f, (h, z, act)


def _ffn_bwd(df, saved, win, wout):
    h, z, act = saved
    s = h.shape[0]
    ts = _tile(s)
    wout = wout.reshape(4, FSH, D)
    dact = _mm("ffn_out_dx", df, wout, ((1,), (1,)), (4, s // ts),
               pl.BlockSpec((ts, D), lambda k, t: (t, 0)), pl.BlockSpec((None, FSH, D), lambda k, t: (k, 0, 0)),
               pl.BlockSpec((None, ts, FSH), lambda k, t: (k, t, 0)), _sds((4, s, FSH)))
    dwout = _mm("ffn_out_dw", act, df, ((0,), (0,)), (4, 2),
                pl.BlockSpec((None, s, FSH), lambda k, j: (k, 0, 0)), pl.BlockSpec((s, D // 2), lambda k, j: (0, j)),
                pl.BlockSpec((None, FSH, D // 2), lambda k, j: (k, 0, j)), _sds((4, FSH, D), BF16))
    dz = _swiglu_bwd(z, dact)
    dh = _mm("ffn_in_dx", dz, win, ((1,), (1,)), (s // ts, N_DEV),
             pl.BlockSpec((None, ts, FSH), lambda t, j: (j, t, 0)), pl.BlockSpec((None, D, FSH), lambda t, j: (j, 0, 0)),
             pl.BlockSpec((ts, D), lambda t, j: (t, 0)), _sds((s, D)), acc_axis=1)
    dwin = _mm("ffn_in_dw", h, dz, ((0,), (0,)), (N_DEV, 2),
               pl.BlockSpec((s, D // 2), lambda j, i: (0, i)), pl.BlockSpec((None, s, FSH), lambda j, i: (j, 0, 0)),
               pl.BlockSpec((None, D // 2, FSH), lambda j, i: (j, i, 0)), _sds((N_DEV, D, FSH), BF16))
    return dh, dwin, dwout.reshape(N_DEV, D_FF // N_DEV, D)


def _shift_rows(v, k, row, s, back):
    if back:
        return jnp.where(row < s - k, pltpu.roll(v, s - k, 0), 0.0)
    return jnp.where(row >= k, pltpu.roll(v, k, 0), 0.0)


def _window_sum(v, w, row, s, back):
    k = 1
    while k < w:
        v = v + _shift_rows(v, k, row, s, back)
        k *= 2
    return v


def _pool_fwd(z, pool_w, pool_scale):
    s = z.shape[0]

    def body(z_ref, w_ref, sc_ref, y_ref, d_ref):
        row = lax.broadcasted_iota(jnp.int32, (s, HD), 0)
        for g, w in enumerate(POOL_WINDOWS):
            sl = slice(g * HD, (g + 1) * HD)
            a = z_ref[:, sl]
            cnt = jnp.minimum(row + 1, w).astype(F32)
            d = (_window_sum(a, w, row, s, False) / cnt - a).astype(BF16)
            d_ref[:, sl] = d
            y = jnp.dot(d, _bf(w_ref[g]), preferred_element_type=F32)
            y_ref[:, sl] = (y * sc_ref[:, sl]).astype(BF16)

    return _pc(body, "pool_fwd", (1,),
               [pl.BlockSpec((s, NH * HD), lambda i: (0, 0)), pl.BlockSpec((NH, HD, HD), lambda i: (0, 0, 0)),
                pl.BlockSpec((1, NH * HD), lambda i: (0, 0))],
               [pl.BlockSpec((s, NH * HD), lambda i: (0, 0))] * 2,
               [_sds((s, NH * HD), BF16)] * 2)(z, pool_w, pool_scale)


def _pool_bwd(dy, d, pool_w, pool_scale):
    s = dy.shape[0]

    def body(dy_ref, d_ref, w_ref, sc_ref, dz_ref, dw_ref, dsc_ref):
        row = lax.broadcasted_iota(jnp.int32, (s, HD), 0)
        for g, w in enumerate(POOL_WINDOWS):
            sl = slice(g * HD, (g + 1) * HD)
            dyg, dg, wg = dy_ref[:, sl], d_ref[:, sl], _bf(w_ref[g])
            yraw = jnp.dot(dg, wg, preferred_element_type=F32)
            dsc_ref[:, sl] = _colsum(dyg * yraw)
            dyr = _bf(dyg * sc_ref[:, sl])
            dw_ref[g] = lax.dot_general(dg, dyr, (((0,), (0,)), ((), ())), preferred_element_type=F32)
            dd = lax.dot_general(dyr, wg, (((1,), (1,)), ((), ())), preferred_element_type=F32)
            cnt = jnp.minimum(row + 1, w).astype(F32)
            dz_ref[:, sl] = (_window_sum(dd / cnt, w, row, s, True) - dd).astype(BF16)

    return _pc(body, "pool_bwd", (1,),
               [pl.BlockSpec((s, NH * HD), lambda i: (0, 0)), pl.BlockSpec((s, NH * HD), lambda i: (0, 0)),
                pl.BlockSpec((NH, HD, HD), lambda i: (0, 0, 0)), pl.BlockSpec((1, NH * HD), lambda i: (0, 0))],
               [pl.BlockSpec((s, NH * HD), lambda i: (0, 0)), pl.BlockSpec((NH, HD, HD), lambda i: (0, 0, 0)),
                pl.BlockSpec((1, NH * HD), lambda i: (0, 0))],
               [_sds((s, NH * HD), BF16), _sds((NH, HD, HD)), _sds((1, NH * HD))])(dy, d, pool_w, pool_scale)


def _gelu(v):
    return 0.5 * v * (1.0 + jnp.tanh(GELU_C * (v + 0.044715 * (v * v * v))))


def _gelu_grad(v):
    t = jnp.tanh(GELU_C * (v + 0.044715 * (v * v * v)))
    return 0.5 * (1.0 + t) + 0.5 * v * (1.0 - t * t) * (GELU_C * (1.0 + 3.0 * 0.044715 * (v * v)))


def _causal_mask():
    return lax.broadcasted_iota(jnp.int32, (HD, HD), 0) >= lax.broadcasted_iota(jnp.int32, (HD, HD), 1)


def _sgu_specs():
    w = NH * HD
    return [pl.BlockSpec((HD, w), lambda c: (c, 1)), pl.BlockSpec((HD, w), lambda c: (c, 2)),
            pl.BlockSpec((1, w), lambda c: (0, 0)), pl.BlockSpec((1, w), lambda c: (0, 0)),
            pl.BlockSpec((NH, HD, HD), lambda c: (0, 0, 0)), pl.BlockSpec((HD, LANE), lambda c: (0, 0))]


def _sgu_head(v, lng_ref, lnb_ref, w_ref, h):
    sl = slice(h * HD, (h + 1) * HD)
    vh = v[:, sl]
    xc = vh - jnp.mean(vh, axis=-1, keepdims=True)
    rs = lax.rsqrt(jnp.mean(xc * xc, axis=-1, keepdims=True) + EPS)
    vhat = xc * rs
    vn = _bf(vhat * lng_ref[:, sl] + lnb_ref[:, sl])
    wc = _bf(jnp.where(_causal_mask(), w_ref[h], 0.0))
    return sl, rs, vhat, vn, wc


def _sgu_fwd(z, ln_g, ln_b, sgu_w, sgu_bt):
    s = z.shape[0]

    def body(zu_ref, zv_ref, lng_ref, lnb_ref, w_ref, bt_ref, y_ref):
        u, v = _gelu(zu_ref[...]), _gelu(zv_ref[...])
        for h in range(NH):
            sl, _, _, vn, wc = _sgu_head(v, lng_ref, lnb_ref, w_ref, h)
            sp = jnp.dot(wc, vn, preferred_element_type=F32) + bt_ref[:, h:h + 1]
            y_ref[:, sl] = (u[:, sl] * sp).astype(BF16)

    return _pc(body, "sgu_fwd", (s // HD,), _sgu_specs(), pl.BlockSpec((HD, NH * HD), lambda c: (c, 0)),
               _sds((s, NH * HD), BF16))(z, z, ln_g, ln_b, sgu_w, sgu_bt)


def _sgu_bwd(z, dy, ln_g, ln_b, sgu_w, sgu_bt, head_sum):
    s = z.shape[0]
    w = NH * HD
    nc = s // HD

    def body(zu_ref, zv_ref, lng_ref, lnb_ref, w_ref, bt_ref, dy_ref, hs_ref,
             dzu_ref, dzv_ref, dlng_ref, dlnb_ref, dw_ref, dbt_ref, dsacc_ref):
        c = pl.program_id(0)
        first = c == 0
        zu, zv = zu_ref[...], zv_ref[...]
        u, v = _gelu(zu), _gelu(zv)
        dyv = dy_ref[...]
        gu, gv = _gelu_grad(zu), _gelu_grad(zv)
        ds = dyv * u
        _acc(dsacc_ref, first, ds)
        for h in range(NH):
            sl, rs, vhat, vn, wc = _sgu_head(v, lng_ref, lnb_ref, w_ref, h)
            sp = jnp.dot(wc, vn, preferred_element_type=F32) + bt_ref[:, h:h + 1]
            dzu_ref[:, sl] = (dyv[:, sl] * sp * gu[:, sl]).astype(BF16)
            dsh = _bf(ds[:, sl])
            dwh = lax.dot_general(dsh, vn, (((1,), (1,)), ((), ())), preferred_element_type=F32)
            dwh = jnp.where(_causal_mask(), dwh, 0.0)

            @pl.when(first)
            def _():
                dw_ref[h] = dwh

            @pl.when(jnp.logical_not(first))
            def _():
                dw_ref[h] += dwh

            dvn = lax.dot_general(wc, dsh, (((0,), (0,)), ((), ())), preferred_element_type=F32)
            g_col = _colsum(dvn * vhat)
            b_col = _colsum(dvn)

            @pl.when(first)
            def _():
                dlng_ref[:, sl] = g_col
                dlnb_ref[:, sl] = b_col

            @pl.when(jnp.logical_not(first))
            def _():
                dlng_ref[:, sl] += g_col
                dlnb_ref[:, sl] += b_col

            dvh = dvn * lng_ref[:, sl]
            dv = rs * (dvh - jnp.mean(dvh, axis=-1, keepdims=True) - vhat * jnp.mean(dvh * vhat, axis=-1, keepdims=True))
            dzv_ref[:, sl] = (dv * gv[:, sl]).astype(BF16)

        @pl.when(c == nc - 1)
        def _():
            dbt_ref[...] = jnp.dot(dsacc_ref[...], hs_ref[...], preferred_element_type=F32, precision=HIGHEST)

    outs = _pc(body, "sgu_bwd", (nc,),
               _sgu_specs() + [pl.BlockSpec((HD, w), lambda c: (c, 1)), pl.BlockSpec((w, LANE), lambda c: (0, 0))],
               [pl.BlockSpec((HD, w), lambda c: (c, 0))] * 2 + [pl.BlockSpec((1, w), lambda c: (0, 0))] * 2
               + [pl.BlockSpec((NH, HD, HD), lambda c: (0, 0, 0)), pl.BlockSpec((HD, LANE), lambda c: (0, 0))],
               [_sds((s, w), BF16)] * 2 + [_sds((1, w))] * 2 + [_sds((NH, HD, HD)), _sds((HD, LANE))],
               scratch=[pltpu.VMEM((HD, w), F32)])(z, z, ln_g, ln_b, sgu_w, sgu_bt, dy, head_sum)
    return outs


def _cmul(ar, ai, br, bi):
    return ar * br - ai * bi, ar * bi + ai * br


def _ssm_prep(lam_re, lam_im, lam_re_rep, lam_im_rep, log_dt, b_re, b_im):
    def disc(lr, li, dt):
        mag = jnp.exp(lr * dt)
        return mag * jnp.cos(li * dt), mag * jnp.sin(li * dt)

    def body(lr_ref, li_ref, lrr_ref, lir_ref, ldt_ref, br_ref, bi_ref, or_ref, oi_ref, bbr_ref, bbi_ref):
        dt = jnp.exp(ldt_ref[...])
        or_ref[...], oi_ref[...] = disc(lr_ref[...], li_ref[...], dt)
        lr, li = lrr_ref[...], lir_ref[...]
        er, ei = disc(lr, li, dt)
        den = lr * lr + li * li
        kr = ((er - 1.0) * lr + ei * li) / den
        ki = (ei * lr - (er - 1.0) * li) / den
        bbr_ref[...], bbi_ref[...] = _cmul(kr, ki, br_ref[...], bi_ref[...])

    small = pl.BlockSpec((SSM_G, SSM_P), lambda i: (0, 0))
    wide = pl.BlockSpec((SSM_G, SSM_P * SSM_N), lambda i: (0, 0))
    col = pl.BlockSpec((SSM_G, 1), lambda i: (0, 0))
    return _pc(body, "ssm_prep", (1,), [small, small, wide, wide, col, wide, wide], [small, small, wide, wide],
               [_sds((SSM_G, SSM_P))] * 2 + [_sds((SSM_G, SSM_P * SSM_N))] * 2)(
        lam_re, lam_im, lam_re_rep, lam_im_rep, log_dt, b_re, b_im)


def _ssm_param_bwd(g_lam_re, g_lam_im, g_bb_re, g_bb_im, lam_re, lam_im, lam_re_rep, lam_im_rep, log_dt, b_re, b_im, seg):
    def body(glr_ref, gli_ref, gbr_ref, gbi_ref, lr_ref, li_ref, lrr_ref, lir_ref, ldt_ref, br_ref, bi_ref, seg_ref,
             dlr_ref, dli_ref, ddt_ref, dbr_ref, dbi_ref):
        dt = jnp.exp(ldt_ref[...])
        lr, li = lrr_ref[...], lir_ref[...]
        mag = jnp.exp(lr * dt)
        er, ei = mag * jnp.cos(li * dt), mag * jnp.sin(li * dt)
        den = lr * lr + li * li
        kr = ((er - 1.0) * lr + ei * li) / den
        ki = (ei * lr - (er - 1.0) * li) / den
        gbr, gbi = gbr_ref[...], gbi_ref[...]
        dbr_ref[...], dbi_ref[...] = _cmul(kr, -ki, gbr, gbi)
        tr, ti = _cmul(br_ref[...], -bi_ref[...], gbr, gbi)
        gkr = jnp.dot(tr, seg_ref[...], preferred_element_type=F32, precision=HIGHEST)
        gki = jnp.dot(ti, seg_ref[...], preferred_element_type=F32, precision=HIGHEST)
        lr, li = lr_ref[...], li_ref[...]
        mag = jnp.exp(lr * dt)
        er, ei = mag * jnp.cos(li * dt), mag * jnp.sin(li * dt)
        den = lr * lr + li * li
        ir, ii = lr / den, -li / den
        kr, ki = _cmul(er - 1.0, ei, ir, ii)
        ar, ai = _cmul(ir, -ii, gkr, gki)
        glr, gli = glr_ref[...] + ar, gli_ref[...] + ai
        qr, qi = _cmul(kr, ki, ir, ii)
        g1r, g1i = _cmul(-qr, qi, gkr, gki)
        g2r, g2i = _cmul(dt * er, -dt * ei, glr, gli)
        dlr_ref[...] = g1r + g2r
        dli_ref[...] = g1i + g2i
        wr, wi = _cmul(lr, li, er, ei)
        g_dt = jnp.sum(wr * glr + wi * gli, axis=-1, keepdims=True)
        ddt_ref[...] = jnp.broadcast_to(dt * g_dt, (SSM_G, LANE))

    small = pl.BlockSpec((SSM_G, SSM_P), lambda i: (0, 0))
    wide = pl.BlockSpec((SSM_G, SSM_P * SSM_N), lambda i: (0, 0))
    col = pl.BlockSpec((SSM_G, 1), lambda i: (0, 0))
    segs = pl.BlockSpec((SSM_P * SSM_N, SSM_P), lambda i: (0, 0))
    return _pc(body, "ssm_param_bwd", (1,), [small, small, wide, wide, small, small, wide, wide, col, wide, wide, segs],
               [small, small, pl.BlockSpec((SSM_G, LANE), lambda i: (0, 0)), wide, wide],
               [_sds((SSM_G, SSM_P))] * 2 + [_sds((SSM_G, LANE))] + [_sds((SSM_G, SSM_P * SSM_N))] * 2)(
        g_lam_re, g_lam_im, g_bb_re, g_bb_im, lam_re, lam_im, lam_re_rep, lam_im_rep, log_dt, b_re, b_im, seg)


SCAN_LANES = 256
SCAN_ROWS = 8


def _ssm_scan(b_re, b_im, lam_re, lam_im, reverse):
    s = b_re.shape[0]
    nt = s // SCAN_ROWS
    ln, rows = SCAN_LANES, SCAN_ROWS

    def body(lr_ref, li_ref, br_ref, bi_ref, or_ref, oi_ref):
        l1 = (lr_ref[...], li_ref[...])
        pw = [l1]
        for _ in range(rows - 1):
            pw.append(_cmul(*pw[-1], *l1))
        row = lax.broadcasted_iota(jnp.int32, (rows, ln), 0)
        expo = (rows - row) if reverse else (row + 1)
        pr = jnp.zeros((rows, ln), F32)
        pi = jnp.zeros((rows, ln), F32)
        for e in range(1, rows + 1):
            pr = jnp.where(expo == e, pw[e - 1][0], pr)
            pi = jnp.where(expo == e, pw[e - 1][1], pi)
        lk = {k: (jnp.broadcast_to(pw[k - 1][0], (rows, ln)), jnp.broadcast_to(pw[k - 1][1], (rows, ln))) for k in (1, 2, 4)}

        def step(i, carry):
            cr, ci = carry
            t = (nt - 1 - i) if reverse else i
            r0 = pl.multiple_of(t * rows, rows)
            xr, xi = br_ref[pl.ds(r0, rows), :], bi_ref[pl.ds(r0, rows), :]
            for k in (1, 2, 4):
                sr = _shift_rows(xr, k, row, rows, reverse)
                si = _shift_rows(xi, k, row, rows, reverse)
                ar, ai = _cmul(lk[k][0], lk[k][1], sr, si)
                xr, xi = xr + ar, xi + ai
            ar, ai = _cmul(pr, pi, cr, ci)
            xr, xi = xr + ar, xi + ai
            or_ref[pl.ds(r0, rows), :] = xr
            oi_ref[pl.ds(r0, rows), :] = xi
            if reverse:
                return xr[0:1], xi[0:1]
            return xr[rows - 1:rows], xi[rows - 1:rows]

        zero = jnp.zeros((1, ln), F32)
        lax.fori_loop(0, nt, step, (zero, zero))

    vec = pl.BlockSpec((1, ln), lambda j: (0, j))
    blk = pl.BlockSpec((s, ln), lambda j: (0, j))
    return _pc(body, "ssm_scan_bwd" if reverse else "ssm_scan_fwd", (SSM_L // ln,), [vec, vec, blk, blk], [blk, blk],
               [_sds((s, SSM_L))] * 2)(lam_re, lam_im, b_re, b_im)


def _ssm_in(name, v, w_bd):
    s = v.shape[0]
    ts = _tile(s)
    half = SSM_GB * SSM_P

    def body(v_ref, w_ref, or_ref, oi_ref):
        r = jnp.dot(_bf(v_ref[...]), w_ref[...], preferred_element_type=F32)
        or_ref[...] = r[:, :half]
        oi_ref[...] = r[:, half:]

    out = pl.BlockSpec((ts, half), lambda q, t: (t, q))
    return _pc(body, name, (SSM_NB, s // ts),
               [pl.BlockSpec((ts, SSM_GB * SSM_N), lambda q, t: (t, q)), pl.BlockSpec((None, SSM_GB * SSM_N, 2 * half), lambda q, t: (q, 0, 0))],
               [out, out], [_sds((s, SSM_L))] * 2)(v, w_bd)


def _ssm_out(name, x_re, x_im, w_bd):
    s = x_re.shape[0]
    ts = _tile(s)
    half = SSM_GB * SSM_P
    nt = (((1,), (1,)), ((), ()))

    def body(xr_ref, xi_ref, w_ref, o_ref):
        w = w_ref[...]
        o_ref[...] = (lax.dot_general(_bf(xr_ref[...]), w[:, :half], nt, preferred_element_type=F32)
                      + lax.dot_general(_bf(xi_ref[...]), w[:, half:], nt, preferred_element_type=F32))

    xin = pl.BlockSpec((ts, half), lambda q, t: (t, q))
    return _pc(body, name, (SSM_NB, s // ts),
               [xin, xin, pl.BlockSpec((None, SSM_GB * SSM_N, 2 * half), lambda q, t: (q, 0, 0))],
               pl.BlockSpec((ts, SSM_GB * SSM_N), lambda q, t: (t, q)), _sds((s, SSM_G * SSM_N)))(x_re, x_im, w_bd)


def _ssm_outer(name, v, x_re, x_im):
    s = v.shape[0]
    ts = min(s, 512)
    half = SSM_GB * SSM_P
    tn = (((0,), (0,)), ((), ()))

    def body(v_ref, xr_ref, xi_ref, o_ref):
        vv = _bf(v_ref[...])
        pr = lax.dot_general(vv, _bf(xr_ref[...]), tn, preferred_element_type=F32)
        pi = lax.dot_general(vv, _bf(xi_ref[...]), tn, preferred_element_type=F32)
        first = pl.program_id(1) == 0

        @pl.when(first)
        def _():
            o_ref[:, :half] = pr
            o_ref[:, half:] = pi

        @pl.when(jnp.logical_not(first))
        def _():
            o_ref[:, :half] += pr
            o_ref[:, half:] += pi

    xin = pl.BlockSpec((ts, half), lambda q, t: (t, q))
    return _pc(body, name, (SSM_NB, s // ts), [pl.BlockSpec((ts, SSM_GB * SSM_N), lambda q, t: (t, q)), xin, xin],
               pl.BlockSpec((None, SSM_GB * SSM_N, 2 * half), lambda q, t: (q, 0, 0)),
               _sds((SSM_NB, SSM_GB * SSM_N, 2 * half)))(v, x_re, x_im)


def _ssm_dlam(x_re, x_im, a_re, a_im):
    s = x_re.shape[0]
    ln = SCAN_LANES

    def body(xr_ref, xi_ref, ar_ref, ai_ref, or_ref, oi_ref):
        row = lax.broadcasted_iota(jnp.int32, (s, ln), 0)
        xr = _shift_rows(xr_ref[...], 1, row, s, False)
        xi = _shift_rows(xi_ref[...], 1, row, s, False)
        ar, ai = ar_ref[...], ai_ref[...]
        or_ref[...] = _colsum(xr * ar + xi * ai)
        oi_ref[...] = _colsum(xr * ai - xi * ar)

    blk = pl.BlockSpec((s, ln), lambda j: (0, j))
    vec = pl.BlockSpec((1, ln), lambda j: (0, j))
    return _pc(body, "ssm_dlam", (SSM_L // ln,), [blk] * 4, [vec, vec], [_sds((1, SSM_L))] * 2)(x_re, x_im, a_re, a_im)


def _ssm_act_fwd(y, u, d_skip):
    s = y.shape[0]
    ts = min(s, 512)

    def body(y_ref, u_ref, d_ref, o_ref):
        o_ref[...] = _gelu(y_ref[...] + d_ref[...] * u_ref[...]).astype(BF16)

    return _pc(body, "ssm_act_fwd", (s // ts,), [_row_spec(ts, D)] * 2 + [_vec_spec(D)], _row_spec(ts, D),
               _sds((s, D), BF16))(y, u, d_skip)


def _ssm_act_bwd(dg, y, u, d_skip):
    s = y.shape[0]
    ts = min(s, 512)

    def body(dg_ref, y_ref, u_ref, d_ref, dy_ref, dd_ref):
        uv = u_ref[...]
        dy = dg_ref[...] * _gelu_grad(y_ref[...] + d_ref[...] * uv)
        dy_ref[...] = dy.astype(BF16)
        _acc(dd_ref, pl.program_id(0) == 0, _colsum(dy * uv))

    return _pc(body, "ssm_act_bwd", (s // ts,), [_row_spec(ts, D)] * 3 + [_vec_spec(D)], [_row_spec(ts, D), _vec_spec(D)],
               [_sds((s, D), BF16), _sds((1, D))])(dg, y, u, d_skip)


def _axpy(a, b, d_skip):
    s = a.shape[0]
    ts = min(s, 512)

    def body(a_ref, b_ref, d_ref, o_ref):
        o_ref[...] = (a_ref[...] + d_ref[...] * b_ref[...].astype(F32)).astype(BF16)

    return _pc(body, "ssm_du", (s // ts,), [_row_spec(ts, D)] * 2 + [_vec_spec(D)], _row_spec(ts, D),
               _sds((s, D), BF16))(a, b, d_skip)


def _glu_fwd(zz):
    s = zz.shape[0]
    ts = min(s, 512)

    def body(a_ref, b_ref, o_ref):
        o_ref[...] = a_ref[...] * _sigmoid(b_ref[...])

    return _pc(body, "glu_fwd", (s // ts,), [_row_spec(ts, D, 0), _row_spec(ts, D, 1)], _row_spec(ts, D), _sds((s, D)))(zz, zz)


def _glu_bwd(zz, df):
    s = zz.shape[0]
    ts = min(s, 512)

    def body(a_ref, b_ref, df_ref, o_ref):
        sg = _sigmoid(b_ref[...])
        dfv = df_ref[...].astype(F32)
        o_ref[:, :D] = (dfv * sg).astype(BF16)
        o_ref[:, D:] = (dfv * a_ref[...] * sg * (1.0 - sg)).astype(BF16)

    return _pc(body, "glu_bwd", (s // ts,), [_row_spec(ts, D, 0), _row_spec(ts, D, 1), _row_spec(ts, D)],
               _row_spec(ts, 2 * D), _sds((s, 2 * D), BF16))(zz, zz, df)


def _block_diag(m):
    g, r, c = m.shape
    eye = jnp.eye(SSM_GB, dtype=m.dtype)
    m = m.reshape(SSM_NB, SSM_GB, r, c)
    return (m[:, :, :, None, :] * eye[None, :, None, :, None]).reshape(SSM_NB, SSM_GB * r, SSM_GB * c)


def _diag_blocks(m, r, c):
    m = m.reshape(SSM_NB, SSM_GB, r, SSM_GB, c)
    idx = jnp.arange(SSM_GB)
    return m[:, idx, :, idx, :].transpose(1, 0, 2, 3).reshape(SSM_G, r, c)


def _mod_part(c_all, ada_w):
    n = ada_w.shape[-1]

    def body(c_ref, w_ref, o_ref):
        cv = c_ref[...]
        cond = _bf(cv * _sigmoid(cv))
        o_ref[...] = jnp.dot(cond, _bf(w_ref[...]), preferred_element_type=F32)

    return _pc(body, "mod_part", (2,), [pl.BlockSpec((N_DEV, D), lambda l: (0, 0)), pl.BlockSpec((None, D, n), lambda l: (l, 0, 0))],
               pl.BlockSpec((None, N_DEV, n), lambda l: (l, 0, 0)), _sds((2, N_DEV, n)))(c_all, ada_w)


def _ada_w_grad(c_all_t, dmod):
    n = dmod.shape[-1]
    tr = 128

    def body(c_ref, d_ref, o_ref):
        cv = c_ref[...]
        cond = _bf(cv * _sigmoid(cv)).astype(F32)
        dm = _bf(d_ref[...]).astype(F32)
        acc = cond[:, 0:1] * dm[0:1, :]
        for b in range(1, N_DEV):
            acc = acc + cond[:, b:b + 1] * dm[b:b + 1, :]
        o_ref[...] = acc

    return _pc(body, "ada_w_grad", (2, D // tr),
               [pl.BlockSpec((tr, N_DEV), lambda l, t: (t, 0)), pl.BlockSpec((None, N_DEV, n), lambda l, t: (l, 0, 0))],
               pl.BlockSpec((None, tr, n), lambda l, t: (l, t, 0)), _sds((2, D, n)))(c_all_t, dmod)


def _adamw(name, parts, w, m, v, slot=0, prev=None):
    p, r, c = parts.shape
    tr = r
    while tr * c * 4 > (1 << 20) and tr % 16 == 0:
        tr //= 2
    nt = r // tr

    def body(p_ref, w_ref, m_ref, v_ref, *rest):
        g_ref, d_ref, nm_ref, nv_ref = rest[-4:]
        g = p_ref[0].astype(F32)
        for i in range(1, p):
            g = g + p_ref[i].astype(F32)
        g_ref[...] = g
        m2 = B1 * m_ref[...] + (1.0 - B1) * g
        v2 = B2 * v_ref[...] + (1.0 - B2) * (g * g)
        nm_ref[...] = m2
        nv_ref[...] = v2
        m_hat = m2 / (1.0 - B1 ** STEP)
        v_hat = v2 / (1.0 - B2 ** STEP)
        d_ref[...] = -LR * (m_hat / (jnp.sqrt(v_hat) + ADAM_EPS) + WD * w_ref[...])

    blk = pl.BlockSpec((tr, c), lambda t: (slot * nt + t, 0))
    in_specs = [pl.BlockSpec((p, tr, c), lambda t: (0, t, 0)), blk, blk, blk]
    if prev is None:
        return _pc(body, name, (nt,), in_specs, [blk] * 4, [_sds(w.shape)] * 4)(parts, w, m, v)
    return pl.pallas_call(
        body, name=name, grid=(nt,), in_specs=in_specs + [pl.BlockSpec(memory_space=pl.ANY)] * 4, out_specs=[blk] * 4,
        out_shape=[_sds(w.shape)] * 4, input_output_aliases={4 + i: i for i in range(4)},
        compiler_params=pltpu.CompilerParams(dimension_semantics=("arbitrary",), vmem_limit_bytes=VMEM_LIMIT_BYTES))(parts, w, m, v, *prev)


def _sum_parts(parts):
    p, r, c = parts.shape
    tr = r
    while tr * c * 4 > (1 << 19) and tr % 16 == 0:
        tr //= 2

    def body(p_ref, o_ref):
        g = p_ref[0]
        for i in range(1, p):
            g = g + p_ref[i]
        o_ref[...] = g

    return _pc(body, "sum_parts", (r // tr,), [pl.BlockSpec((p, tr, c), lambda t: (0, t, 0))], pl.BlockSpec((tr, c), lambda t: (t, 0)),
               _sds((r, c)))(parts)


def _place():
    x, y, c = lax.axis_index("x"), lax.axis_index("y"), lax.axis_index("c")
    peers = []
    for k in range(1, N_DEV):
        px = (1 - x) if k & 4 else x
        py = (1 - y) if k & 2 else y
        pc = (1 - c) if k & 1 else c
        peers.append(((px, py, pc), 4 * px + 2 * py + pc))
    return 4 * x + 2 * y + c, peers


def _at(ref, idx):
    return ref if idx is None else ref.at[idx]


def _exchange_copies(plan, src_refs, dst_refs, send_sems, recv_sems, local_sems, with_arrivals=True):
    me, peers = _place()
    local = [pltpu.make_async_copy(_at(src_refs[si], sx), _at(dst_refs[di], dx), local_sems.at[i])
             for i, (si, sx, di, dx) in enumerate(plan(me, me))]

    n = len(local)

    def remote(k, i, dev, entry):
        si, sx, di, dx = entry
        return pltpu.make_async_remote_copy(_at(src_refs[si], sx), _at(dst_refs[di], dx), send_sems.at[k * n + i], recv_sems.at[k * n + i],
                                            device_id=dev, device_id_type=MESH)

    sends = [remote(k, i, dev, e) for k, (dev, peer) in enumerate(peers) for i, e in enumerate(plan(me, peer))]
    if not with_arrivals:
        return local, sends, []
    arrivals = [remote(k, i, dev, e) for k, (dev, peer) in enumerate(peers) for i, e in enumerate(plan(peer, me))]
    return local, sends, arrivals


def _sem_shapes(n_copies):
    return [pltpu.SemaphoreType.DMA(((N_DEV - 1) * n_copies,)), pltpu.SemaphoreType.DMA(((N_DEV - 1) * n_copies,)),
            pltpu.SemaphoreType.DMA((n_copies,))]


def _exchange(name, srcs, dst_shapes, plan, n_copies):
    ns, nd = len(srcs), len(dst_shapes)

    def body(*refs):
        local, sends, arrivals = _exchange_copies(plan, refs[:ns], refs[ns:ns + nd], *refs[ns + nd:])
        for cp in local + sends:
            cp.start()
        for cp in arrivals:
            cp.wait_recv()
        for cp in sends:
            cp.wait_send()
        for cp in local:
            cp.wait()

    any_spec = pl.BlockSpec(memory_space=pl.ANY)
    return pl.pallas_call(
        body, name=name, in_specs=[any_spec] * ns, out_specs=[any_spec] * nd, out_shape=list(dst_shapes),
        scratch_shapes=_sem_shapes(n_copies))(*srcs)


HBM_SPEC = pl.BlockSpec(memory_space=pltpu.HBM)
SEM_SPEC = pl.BlockSpec(memory_space=pltpu.SEMAPHORE)
SIDE_EFFECT = pltpu.SideEffectType.DATAFLOW_SIDE_EFFECTING


def _exchange_start(name, srcs, dst_shapes, plan, n_copies, order):
    ns, nd = len(srcs), len(dst_shapes)
    nb = ns + nd

    def body(*refs):
        local, sends, _ = _exchange_copies(plan, refs[:ns], refs[ns:nb], *refs[nb + 1:nb + 4], with_arrivals=False)
        for cp in local + sends:
            cp.start()
        refs[-1][...] = jnp.zeros((8, LANE), F32)

    lands = [pltpu.with_memory_space_constraint(lax.empty(d.shape, d.dtype), pltpu.HBM) for d in dst_shapes]
    srcs = [pltpu.with_memory_space_constraint(a, pltpu.HBM) for a in srcs]
    bufs = srcs + lands
    out = pl.pallas_call(
        body, name=name, in_specs=[HBM_SPEC] * nb + [pl.BlockSpec(memory_space=pl.ANY)],
        out_specs=[SEM_SPEC] * 3 + [HBM_SPEC] * nb + [pl.BlockSpec(memory_space=pltpu.VMEM)],
        out_shape=_sem_shapes(n_copies) + [pltpu.HBM(a.shape, a.dtype) for a in bufs] + [_sds((8, LANE))],
        input_output_aliases={i: 3 + i for i in range(nb)},
        compiler_params=pltpu.CompilerParams(has_side_effects=SIDE_EFFECT))(*bufs, order)
    return out[:3], out[3:3 + ns], out[3 + ns:3 + nb], out[-1]


def _exchange_wait(name, sems, srcs, lands, plan, after):
    ns, nd = len(srcs), len(lands)
    nb = ns + nd

    def body(*refs):
        local, sends, arrivals = _exchange_copies(plan, refs[:ns], refs[ns:nb], *refs[nb:nb + 3])
        for cp in arrivals:
            cp.wait_recv()
        for cp in sends:
            cp.wait_send()
        for cp in local:
            cp.wait()

    bufs = list(srcs) + list(lands)
    out = pl.pallas_call(
        body, name=name, in_specs=[HBM_SPEC] * nb + [SEM_SPEC] * 3 + [pl.BlockSpec(memory_space=pl.ANY)],
        out_specs=[HBM_SPEC] * nb, out_shape=[pltpu.HBM(a.shape, a.dtype) for a in bufs],
        input_output_aliases={i: i for i in range(nb)},
        compiler_params=pltpu.CompilerParams(has_side_effects=SIDE_EFFECT))(*bufs, *sems, after)
    return out[ns:]


def _all_gather(name, arrs):
    plan = lambda me, peer: [(i, None, i, me) for i in range(len(arrs))]
    return _exchange(name, arrs, [_sds((N_DEV,) + a.shape, a.dtype) for a in arrs], plan, len(arrs))


def _sublayer_fwd(x, fn, mod3, g_pre, g_post, rw):
    h = _prenorm_fwd(x, g_pre, mod3[1:2], mod3[0:1])
    f, saved = fn(h)
    return _postnorm_fwd(x, f, g_post, mod3[2:3], rw), (x, f, saved)


def _sublayer_bwd(dout, saved, fn_bwd, mod3, g_pre, g_post, rw):
    x, f, inner = saved
    df, dgate, dg_post = _postnorm_bwd(dout, f, g_post, mod3[2:3], rw)
    dh, extra = fn_bwd(df, inner)
    dx, dshift, dscale, dg_pre = _prenorm_bwd(dout, dh, x, g_pre, mod3[1:2])
    return dx, jnp.concatenate([dshift, dscale, dgate], axis=0), dg_pre, dg_post, extra


def _mix0_fwd(h, p):
    z = _mm_nn("mix0_in", h, p["ab_w_in"])
    y_a, d = _pool_fwd(z, p["pool_w"], p["pool_scale"])
    y_b = _sgu_fwd(z, p["sgu_ln_g"], p["sgu_ln_b"], p["sgu_w"], p["sgu_bt"])
    ycat = jnp.concatenate([y_a, y_b], axis=1)
    return _mm_nn("mix0_out", ycat, p["ab_w_out"]), (h, z, d, ycat)


def _mix0_bwd(df, saved, p):
    h, z, d, ycat = saved
    dycat = _mm_nt("mix0_out_dx", df, p["ab_w_out"])
    g = {"ab_w_out": _mm_tn("mix0_out_dw", ycat, df, BF16)}
    dz_p, g["pool_w"], g["pool_scale"] = _pool_bwd(dycat, d, p["pool_w"], p["pool_scale"])
    dz_u, dz_v, g["sgu_ln_g"], g["sgu_ln_b"], g["sgu_w"], dbt = _sgu_bwd(
        z, dycat, p["sgu_ln_g"], p["sgu_ln_b"], p["sgu_w"], p["sgu_bt"], p["head_sum"])
    g["sgu_b"] = dbt[:, :NH].T
    dz = jnp.concatenate([dz_p, dz_u, dz_v], axis=1)
    g["ab_w_in"] = _mm_tn("mix0_in_dw", h, dz, BF16)
    return _mm_nt("mix0_in_dx", dz, p["ab_w_in"]), g


def _mix1_fwd(h, p):
    u = _mm_nn("ssm_w_in", h, p["ssm_w_in"])
    bu_re, bu_im = _ssm_in("ssm_bu", u, p["wb_bd"])
    x_re, x_im = _ssm_scan(bu_re, bu_im, p["lam_bar_re"], p["lam_bar_im"], False)
    y = _ssm_out("ssm_y", x_re, x_im, p["wc_bd"])
    g = _ssm_act_fwd(y, u, p["ssm_d"])
    zz = _mm_nn("ssm_glu", g, p["ssm_w_glu"])
    return _glu_fwd(zz), (h, u, x_re, x_im, y, g, zz)


def _mix1_bwd(df, saved, p):
    h, u, x_re, x_im, y, g, zz = saved
    gr = {}
    dzz = _glu_bwd(zz, df)
    dg = _mm_nt("ssm_glu_dx", dzz, p["ssm_w_glu"])
    gr["ssm_w_glu"] = _mm_tn("ssm_glu_dw", g, dzz, BF16)
    dy, gr["ssm_d"] = _ssm_act_bwd(dg, y, u, p["ssm_d"])
    gx_re, gx_im = _ssm_in("ssm_gx", dy, p["wct_bd"])
    a_re, a_im = _ssm_scan(gx_re, gx_im, p["lam_bar_re"], -p["lam_bar_im"], True)
    du_ssm = _ssm_out("ssm_du_mm", a_re, a_im, p["wbt_bd"])
    du = _axpy(du_ssm, dy, p["ssm_d"])
    gr["ssm_w_in"] = _mm_tn("ssm_w_in_dw", h, du, BF16)
    dh = _mm_nt("ssm_w_in_dx", du, p["ssm_w_in"])
    g_lam_re, g_lam_im = _ssm_dlam(x_re, x_im, a_re, a_im)
    m_b = _ssm_outer("ssm_db", u, a_re, a_im)
    m_c = _ssm_outer("ssm_dc", dy, x_re, x_im)
    half = SSM_GB * SSM_P
    gbb_re = _diag_blocks(m_b[:, :, :half], SSM_N, SSM_P).transpose(0, 2, 1).reshape(SSM_G, SSM_P * SSM_N)
    gbb_im = _diag_blocks(m_b[:, :, half:], SSM_N, SSM_P).transpose(0, 2, 1).reshape(SSM_G, SSM_P * SSM_N)
    gr["ssm_c_re"] = _diag_blocks(m_c[:, :, :half], SSM_N, SSM_P)
    gr["ssm_c_im"] = -_diag_blocks(m_c[:, :, half:], SSM_N, SSM_P)
    dlr, dli, ddt, dbr, dbi = _ssm_param_bwd(
        g_lam_re.reshape(SSM_G, SSM_P), g_lam_im.reshape(SSM_G, SSM_P), gbb_re, gbb_im,
        p["lam_re"], p["lam_im"], p["lam_re_rep"], p["lam_im_rep"], p["log_dt"], p["b_re"], p["b_im"], p["seg"])
    gr["ssm_lam_re"], gr["ssm_lam_im"], gr["ssm_log_dt"] = dlr, dli, ddt[:, 0]
    gr["ssm_b_re"] = dbr.reshape(SSM_G, SSM_P, SSM_N)
    gr["ssm_b_im"] = dbi.reshape(SSM_G, SSM_P, SSM_N)
    return dh, gr


def _ssm_params(lam_re, lam_im, b_re, b_im, c_re, c_im, log_dt):
    p = {"lam_re": lam_re, "lam_im": lam_im, "log_dt": log_dt.reshape(SSM_G, 1),
         "lam_re_rep": jnp.repeat(lam_re, SSM_N, axis=1), "lam_im_rep": jnp.repeat(lam_im, SSM_N, axis=1),
         "b_re": b_re.reshape(SSM_G, SSM_P * SSM_N), "b_im": b_im.reshape(SSM_G, SSM_P * SSM_N)}
    lbr, lbi, bbr, bbi = _ssm_prep(lam_re, lam_im, p["lam_re_rep"], p["lam_im_rep"], p["log_dt"], p["b_re"], p["b_im"])
    p["lam_bar_re"], p["lam_bar_im"] = lbr.reshape(1, SSM_L), lbi.reshape(1, SSM_L)
    bbr, bbi = bbr.reshape(SSM_G, SSM_P, SSM_N), bbi.reshape(SSM_G, SSM_P, SSM_N)
    p["wb_bd"] = jnp.concatenate([_block_diag(bbr.transpose(0, 2, 1)), _block_diag(bbi.transpose(0, 2, 1))], axis=2).astype(BF16)
    p["wbt_bd"] = p["wb_bd"]
    p["wc_bd"] = jnp.concatenate([_block_diag(c_re), _block_diag(-c_im)], axis=2).astype(BF16)
    p["wct_bd"] = p["wc_bd"]
    p["seg"] = jnp.repeat(jnp.eye(SSM_P, dtype=F32), SSM_N, axis=0)
    return p


RES_WEIGHT = (0.5, 1.0, 0.5)


def _local_step(x, tgt, mod, norm_pre, norm_post, weights_of, on_grads):
    def fns(i, w):
        if i % 3 != 1:
            return (lambda h: _ffn_fwd(h, *w)), (lambda df, sv: (lambda r: (r[0], r[1:]))(_ffn_bwd(df, sv, *w)))
        if i == 1:
            return (lambda h: _mix0_fwd(h, w)), (lambda df, sv: _mix0_bwd(df, sv, w))
        return (lambda h: _mix1_fwd(h, w)), (lambda df, sv: _mix1_bwd(df, sv, w))

    saved, bwd = [], []
    for i in range(6):
        l, s = divmod(i, 3)
        f, b = fns(i, weights_of(i, x))
        x, sv = _sublayer_fwd(x, f, mod[l, s], norm_pre[l, s][None], norm_post[l, s][None], RES_WEIGHT[s])
        saved.append(sv)
        bwd.append(b)
    loss_row, dx = _loss_fwd_bwd(x, tgt)
    token = jnp.zeros((8, LANE), F32)
    for i in reversed(range(6)):
        l, s = divmod(i, 3)
        mod3 = mod[l, s] + token[0:1, 0:1]
        dx, dmod, dpre, dpost, extra = _sublayer_bwd(
            dx, saved[i], bwd[i], mod3, norm_pre[l, s][None], norm_post[l, s][None], RES_WEIGHT[s])
        token = on_grads(i, extra, dmod, dpre, dpost, loss_row)
    return dx


def _pad_rows(v, rows):
    return jnp.pad(v, (0, rows * LANE - v.shape[0])).reshape(rows, LANE)


def _pack(parts):
    flat, layout, off = [], [], 0
    for a in parts:
        n = a.size
        padded = -(-n // LANE) * LANE
        flat.append(jnp.pad(a.reshape(-1).astype(F32), (0, padded - n)))
        layout.append((off, n, a.shape))
        off += padded
    return jnp.concatenate(flat), layout


def _unpack(flat, layout):
    return [flat[off:off + n].reshape(shape) for off, n, shape in layout]


SMALL_REPLICATED = ["ada_b", "pool_w", "pool_scale", "sgu_ln_g", "sgu_ln_b", "sgu_w", "sgu_b", "ssm_lam_re", "ssm_lam_im",
                    "ssm_b_re", "ssm_b_im", "ssm_c_re", "ssm_c_im", "ssm_log_dt"]
SMALL_SHARDED = ["norm_pre", "norm_post", "ssm_d"]
WEIGHTS = ['ada_w', 'ada_b', 'norm_pre', 'norm_post', 'ffn_w_in', 'ffn_w_out', 'ab_w_in', 'pool_w', 'pool_scale', 'sgu_ln_g',
           'sgu_ln_b', 'sgu_w', 'sgu_b', 'ab_w_out', 'ssm_w_in', 'ssm_lam_re', 'ssm_lam_im', 'ssm_b_re', 'ssm_b_im', 'ssm_c_re',
           'ssm_c_im', 'ssm_d', 'ssm_log_dt', 'ssm_w_glu']


def kernel(x, c, ada_w, ada_b, norm_pre, norm_post, ffn_w_in, ffn_w_out, ab_w_in, pool_w, pool_scale, sgu_ln_g, sgu_ln_b, sgu_w, sgu_b, ab_w_out, ssm_w_in, ssm_lam_re, ssm_lam_im, ssm_b_re, ssm_b_im, ssm_c_re, ssm_c_im, ssm_d, ssm_log_dt, ssm_w_glu, loss_target, m_ada_w, m_ada_b, m_norm_pre, m_norm_post, m_ffn_w_in, m_ffn_w_out, m_ab_w_in, m_pool_w, m_pool_scale, m_sgu_ln_g, m_sgu_ln_b, m_sgu_w, m_sgu_b, m_ab_w_out, m_ssm_w_in, m_ssm_lam_re, m_ssm_lam_im, m_ssm_b_re, m_ssm_b_im, m_ssm_c_re, m_ssm_c_im, m_ssm_d, m_ssm_log_dt, m_ssm_w_glu, v_ada_w, v_ada_b, v_norm_pre, v_norm_post, v_ffn_w_in, v_ffn_w_out, v_ab_w_in, v_pool_w, v_pool_scale, v_sgu_ln_g, v_sgu_ln_b, v_sgu_w, v_sgu_b, v_ab_w_out, v_ssm_w_in, v_ssm_lam_re, v_ssm_lam_im, v_ssm_b_re, v_ssm_b_im, v_ssm_c_re, v_ssm_c_im, v_ssm_d, v_ssm_log_dt, v_ssm_w_glu):
    args = locals()
    wts = {n: args[n] for n in WEIGHTS}
    mom = {n: args["m_" + n] for n in WEIGHTS}
    var = {n: args["v_" + n] for n in WEIGHTS}
    me = 4 * lax.axis_index("x") + 2 * lax.axis_index("y") + lax.axis_index("c")
    s = x.shape[1]
    nd = D // N_DEV

    small_in, small_in_layout = _pack([c, norm_pre, norm_post, ssm_d])
    small_rows = -(-small_in.shape[0] // (8 * LANE)) * 8
    (g_small,) = _all_gather("gather_small", [_pad_rows(small_in, small_rows)])
    g_small = g_small.reshape(N_DEV, -1)
    c_all, npre_g, npost_g, sd_g = [jnp.stack([_unpack(g_small[j], small_in_layout)[i] for j in range(N_DEV)]) for i in range(4)]
    c_all = c_all.reshape(N_DEV, D)
    norm_pre_full = npre_g.transpose(1, 2, 0, 3).reshape(2, 3, D)
    norm_post_full = npost_g.transpose(1, 2, 0, 3).reshape(2, 3, D)
    ssm_d_full = sd_g.transpose(1, 0, 2).reshape(1, D)

    nw = ada_w.shape[-1]
    (mod_g,) = _all_gather("gather_mod", [_mod_part(c_all, ada_w)])
    mod = lax.dynamic_index_in_dim(mod_g, me, axis=2, keepdims=False)
    mod = (mod.transpose(1, 0, 2).reshape(2, N_DEV * nw) + ada_b).reshape(2, 3, 3, D)

    shards = [[ffn_w_in[0, 0], ffn_w_out[0, 0]], [ab_w_in[0], ab_w_out[0]], [ffn_w_in[0, 1], ffn_w_out[0, 1]],
              [ffn_w_in[1, 0], ffn_w_out[1, 0]], [ssm_w_in[0], ssm_w_glu[0]], [ffn_w_in[1, 1], ffn_w_out[1, 1]]]
    gather_plan = lambda me_, peer_: [(0, None, 0, me_), (1, None, 1, me_)]
    gathers = []
    token = mod_g
    for i, pair in enumerate(shards):
        pair = [a.astype(BF16) for a in pair]
        sems, srcs_thru, lands, token = _exchange_start(
            f"gather_start_{i}", pair, [_sds((N_DEV,) + a.shape, BF16) for a in pair], gather_plan, 2, token)
        gathers.append((sems, srcs_thru, lands))
    mod = mod + token[0, 0]

    head_sum = jnp.repeat(jnp.eye(NH, LANE, dtype=F32), HD, axis=0)
    mix0 = {"pool_w": pool_w[0], "pool_scale": pool_scale, "sgu_ln_g": sgu_ln_g, "sgu_ln_b": sgu_ln_b, "sgu_w": sgu_w[0],
            "sgu_bt": jnp.pad(sgu_b[0].T, ((0, 0), (0, LANE - NH))), "head_sum": head_sum}
    mix1 = _ssm_params(ssm_lam_re[0], ssm_lam_im[0], ssm_b_re[0], ssm_b_im[0], ssm_c_re[0], ssm_c_im[0], ssm_log_dt[0])
    mix1["ssm_d"] = ssm_d_full

    def weights_of(i, x_in):
        sems, srcs_thru, lands = gathers[i]
        a, b = _exchange_wait(f"gather_wait_{i}", sems, srcs_thru, lands, gather_plan, x_in)
        if i % 3 != 1:
            return a, b
        if i == 1:
            return dict(mix0, ab_w_in=a.transpose(1, 0, 2).reshape(D, -1), ab_w_out=b.reshape(D, D))
        return dict(mix1, ssm_w_in=a.reshape(D, D), ssm_w_glu=b.transpose(1, 0, 2).reshape(D, -1))

    def shard_cols(a):
        r = a.shape[0]
        return a.reshape(r, N_DEV, -1).transpose(1, 0, 2)

    scatter_plan = lambda me_, peer_: [(0, peer_, 0, me_), (1, peer_, 1, me_)]
    scatters = [None] * 6
    last_token = [jnp.zeros((8, LANE), F32)]
    pieces, mixer, bundles = {}, {}, {}
    bundle_plan = lambda me_, peer_: [(0, None, 0, me_)]
    mix0_names = ["pool_w", "pool_scale", "sgu_ln_g", "sgu_ln_b", "sgu_w", "sgu_b"]
    mix1_names = ["ssm_lam_re", "ssm_lam_im", "ssm_b_re", "ssm_b_im", "ssm_c_re", "ssm_c_im", "ssm_log_dt", "ssm_d"]

    def start_bundle(tag, arrays):
        flat, layout = _pack(arrays)
        rows = -(-flat.shape[0] // (8 * LANE)) * 8
        sems, srcs_thru, lands, last_token[0] = _exchange_start(
            f"small_start_{tag}", [_pad_rows(flat, rows)], [_sds((N_DEV, rows, LANE))], bundle_plan, 1, last_token[0])
        bundles[tag] = (sems, srcs_thru, lands, layout)

    def on_grads(i, extra, dmod_i, dpre_i, dpost_i, loss_row):
        pieces[i] = (dmod_i, dpre_i, dpost_i)
        if i == 4:
            mixer.update({n: extra[n] for n in mix1_names})
        if i == 1:
            mixer.update({n: extra[n] for n in mix0_names})
            rest = range(1, 6)
            start_bundle("a", [jnp.stack([pieces[j][0] for j in rest])] + [jnp.concatenate([pieces[j][k] for j in rest]) for k in (1, 2)]
                         + [mixer[n] for n in mix0_names + mix1_names])
        if i == 0:
            start_bundle("b", [dmod_i, dpre_i, dpost_i, loss_row])
        if i % 3 != 1:
            parts = list(extra)
        elif i == 1:
            parts = [shard_cols(extra["ab_w_in"]), extra["ab_w_out"].reshape(N_DEV, nd, D)]
        else:
            parts = [extra["ssm_w_in"].reshape(N_DEV, nd, D), shard_cols(extra["ssm_w_glu"])]
        sems, srcs_thru, lands, last_token[0] = _exchange_start(
            f"scatter_start_{i}", parts, [_sds(a.shape, BF16) for a in parts], scatter_plan, 2, last_token[0])
        scatters[i] = (sems, srcs_thru, lands)
        return last_token[0]

    grad_x = _local_step(x[0], loss_target[0], mod, norm_pre_full, norm_post_full, weights_of, on_grads)

    after = grad_x
    gathered, sums = {}, {}
    for tag in ("a", "b"):
        sems, srcs_thru, lands, layout = bundles[tag]
        (g_parts,) = _exchange_wait(f"small_wait_{tag}", sems, srcs_thru, lands, bundle_plan, after)
        total = _sum_parts(g_parts)
        after = total
        gathered[tag] = [jnp.stack(col) for col in zip(*[_unpack(g_parts[j].reshape(-1), layout) for j in range(N_DEV)])]
        sums[tag] = _unpack(total.reshape(-1), layout)
    dmod_a, dpre_a, dpost_a = sums["a"][:3]
    dmod_b, dpre_b, dpost_b, loss_sum = sums["b"]
    small = dict(zip(mix0_names + mix1_names, sums["a"][3:]))
    small["ada_b"] = jnp.concatenate([dmod_b[None], dmod_a])
    small["norm_pre"] = jnp.concatenate([dpre_b, dpre_a]).reshape(2, 3, D)
    small["norm_post"] = jnp.concatenate([dpost_b, dpost_a]).reshape(2, 3, D)
    loss = loss_sum[0, 0]

    out_g, out_d, out_m, out_v = {}, {}, {}, {}

    def adam_flat(name, gnames, grads):
        gf, lay = _pack(grads)
        r = -(-gf.shape[0] // (8 * LANE)) * 8
        packed = [_pad_rows(_pack([t[n] for n in gnames])[0], r) for t in (wts, mom, var)]
        res = _adamw(name, _pad_rows(gf, r)[None], *packed)
        for o, arr in zip((out_g, out_d, out_m, out_v), res):
            o.update(dict(zip(gnames, _unpack(arr.reshape(-1), lay))))
        return res[0]

    adam_flat("adamw_replicated", SMALL_REPLICATED, [small[n].reshape(wts[n].shape) for n in SMALL_REPLICATED])
    sliced = [lax.dynamic_slice_in_dim(small[n], me * nd, nd, axis=small[n].ndim - 1).reshape(wts[n].shape) for n in SMALL_SHARDED]
    after = adam_flat("adamw_sliced", SMALL_SHARDED, sliced)

    big_out = {}

    def adam_big(name, recv, n, slot=0):
        c_ = wts[n].shape[-1]
        big_out[n] = _adamw(name, recv.reshape(recv.shape[0], -1, c_), *[t[n].reshape(-1, c_) for t in (wts, mom, var)],
                            slot=slot, prev=big_out.get(n))
        return big_out[n][0]

    dmod_all = jnp.concatenate([gathered["b"][0][:, None], gathered["a"][0]], axis=1).reshape(N_DEV, 2, N_DEV, nw)
    dmod_mine = lax.dynamic_index_in_dim(dmod_all, me, axis=2, keepdims=False).transpose(1, 0, 2)
    g_ada_w = _ada_w_grad(c_all.T, dmod_mine)
    after = after[0:1, 0:1] + adam_big("adamw_ada_w", g_ada_w[None], "ada_w")[0:1, 0:1]

    big_names = [("ffn_w_in", "ffn_w_out"), ("ab_w_in", "ab_w_out"), ("ffn_w_in", "ffn_w_out"),
                 ("ffn_w_in", "ffn_w_out"), ("ssm_w_in", "ssm_w_glu"), ("ffn_w_in", "ffn_w_out")]
    ffn_slot = {0: 0, 2: 1, 3: 2, 5: 3}
    for i in reversed(range(6)):
        sems, srcs_thru, lands = scatters[i]
        recv = _exchange_wait(f"scatter_wait_{i}", sems, srcs_thru, lands, scatter_plan, after)
        for n, r in zip(big_names[i], recv):
            after = adam_big(f"adamw_{n}_{i}", r, n, ffn_slot.get(i, 0))
    for n, res in big_out.items():
        for o, arr in zip((out_g, out_d, out_m, out_v), res):
            o[n] = arr.reshape(wts[n].shape)

    return (loss, grad_x[None], *[out_g[n] for n in WEIGHTS], *[out_d[n] for n in WEIGHTS],
            *[out_m[n] for n in WEIGHTS], *[out_v[n] for n in WEIGHTS])
```

```python
import functools
import math

import jax
import jax.numpy as jnp
from jax import lax
from jax.experimental import pallas as pl
from jax.experimental.pallas import tpu as pltpu

F32 = jnp.float32
BF16 = jnp.bfloat16
MESH = pl.DeviceIdType.MESH
HIGHEST = lax.Precision.HIGHEST

N_DEV = 8
D = 1024
D_FF = 2816
FSH = 2 * D_FF // N_DEV
EPS = 1e-6
POOL_WINDOWS = (2, 4, 8, 16)
HD = 128
NH = 4
SSM_G, SSM_P, SSM_N = 64, 64, 16
SSM_GB = 16
SSM_NB = SSM_G // SSM_GB
SSM_L = SSM_G * SSM_P
LR, B1, B2, ADAM_EPS, WD, STEP = 0.001, 0.9, 0.999, 1e-08, 0.01, 10
GELU_C = math.sqrt(2.0 / math.pi)
VMEM_LIMIT_BYTES = 48 * 1024 * 1024
LANE = 128


def _pc(body, name, grid, in_specs, out_specs, out_shape, scratch=()):
    return pl.pallas_call(
        body, name=name, grid=grid, in_specs=in_specs, out_specs=out_specs, out_shape=out_shape,
        scratch_shapes=list(scratch),
        compiler_params=pltpu.CompilerParams(dimension_semantics=("arbitrary",) * len(grid),
                                             vmem_limit_bytes=VMEM_LIMIT_BYTES))


def _sds(shape, dtype=F32):
    return jax.ShapeDtypeStruct(tuple(shape), dtype)


def _bf(v):
    return v if v.dtype == BF16 else v.astype(BF16)


def _row_spec(ts, width, col=0):
    return pl.BlockSpec((ts, width), lambda t, _c=col: (t, _c))


def _vec_spec(width, col=0):
    return pl.BlockSpec((1, width), lambda t, _c=col: (0, _c))


def _mm(name, a, b, contract, grid, a_spec, b_spec, o_spec, out_shape, acc_axis=None):
    dn = (contract, ((), ()))

    def body(a_ref, b_ref, o_ref):
        r = lax.dot_general(_bf(a_ref[...]), _bf(b_ref[...]), dn, preferred_element_type=F32)
        if acc_axis is None:
            o_ref[...] = r.astype(o_ref.dtype)
        else:
            k = pl.program_id(acc_axis)

            @pl.when(k == 0)
            def _():
                o_ref[...] = r

            @pl.when(k > 0)
            def _():
                o_ref[...] += r

    return _pc(body, name, grid, [a_spec, b_spec], o_spec, out_shape)(a, b)


def _tile(s):
    return min(s, 1024)


def _div_tile(n, cap=1024):
    t = min(n, cap) // LANE * LANE
    while n % t:
        t -= LANE
    return t


def _mm_nn(name, a, b, out_dtype=F32):
    s, k = a.shape
    n = b.shape[1]
    ts, tn = _tile(s), _div_tile(n)
    return _mm(name, a, b, ((1,), (0,)), (n // tn, s // ts),
               pl.BlockSpec((ts, k), lambda j, t: (t, 0)), pl.BlockSpec((k, tn), lambda j, t: (0, j)),
               pl.BlockSpec((ts, tn), lambda j, t: (t, j)), _sds((s, n), out_dtype))


def _mm_nt(name, a, b, out_dtype=F32):
    s, n = a.shape
    k = b.shape[0]
    ts, tk = _tile(s), _div_tile(k)
    return _mm(name, a, b, ((1,), (1,)), (k // tk, s // ts),
               pl.BlockSpec((ts, n), lambda j, t: (t, 0)), pl.BlockSpec((tk, n), lambda j, t: (j, 0)),
               pl.BlockSpec((ts, tk), lambda j, t: (t, j)), _sds((s, k), out_dtype))


def _mm_tn(name, a, b, out_dtype=F32, tm=512, tn=512):
    s, m = a.shape
    n = b.shape[1]
    tm, tn = min(m, tm), min(n, tn)
    return _mm(name, a, b, ((0,), (0,)), (m // tm, n // tn),
               pl.BlockSpec((s, tm), lambda i, j: (0, i)), pl.BlockSpec((s, tn), lambda i, j: (0, j)),
               pl.BlockSpec((tm, tn), lambda i, j: (i, j)), _sds((m, n), out_dtype))


def _rstd(v):
    return lax.rsqrt(jnp.mean(v * v, axis=-1, keepdims=True) + EPS)


def _prenorm_fwd(x, g, scale, shift):
    s = x.shape[0]
    ts = min(s, 512)

    def body(x_ref, g_ref, sc_ref, sh_ref, h_ref):
        xv = x_ref[...]
        h_ref[...] = ((xv * _rstd(xv) * g_ref[...]) * (1.0 + sc_ref[...]) + sh_ref[...]).astype(BF16)

    return _pc(body, "prenorm_fwd", (s // ts,), [_row_spec(ts, D)] + [_vec_spec(D)] * 3, _row_spec(ts, D),
               _sds((s, D), BF16))(x, g, scale, shift)


def _postnorm_fwd(x, f, g, gate, rw):
    s = x.shape[0]
    ts = min(s, 512)

    def body(x_ref, f_ref, g_ref, gt_ref, o_ref):
        fv = f_ref[...]
        o_ref[...] = x_ref[...] + (rw * gt_ref[...]) * (fv * _rstd(fv) * g_ref[...])

    return _pc(body, "postnorm_fwd", (s // ts,), [_row_spec(ts, D)] * 2 + [_vec_spec(D)] * 2, _row_spec(ts, D),
               _sds((s, D)))(x, f, g, gate)


def _acc(ref, first, v):
    @pl.when(first)
    def _():
        ref[...] = v

    @pl.when(jnp.logical_not(first))
    def _():
        ref[...] += v


def _colsum(v):
    return jnp.sum(v, axis=0, keepdims=True)


def _postnorm_bwd(dout, f, g, gate, rw):
    s = dout.shape[0]
    ts = min(s, 512)

    def body(do_ref, f_ref, g_ref, gt_ref, df_ref, dgate_ref, dg_ref):
        first = pl.program_id(0) == 0
        do, fv, gv = do_ref[...], f_ref[...], g_ref[...]
        r = _rstd(fv)
        fn = fv * r
        _acc(dgate_ref, first, rw * _colsum(do * (fn * gv)))
        dy = (rw * gt_ref[...]) * do
        _acc(dg_ref, first, _colsum(dy * fn))
        dfn = dy * gv
        df_ref[...] = (r * (dfn - fn * jnp.mean(dfn * fn, axis=-1, keepdims=True))).astype(BF16)

    return _pc(body, "postnorm_bwd", (s // ts,), [_row_spec(ts, D)] * 2 + [_vec_spec(D)] * 2,
               [_row_spec(ts, D), _vec_spec(D), _vec_spec(D)],
               [_sds((s, D), BF16), _sds((1, D)), _sds((1, D))])(dout, f, g, gate)


def _prenorm_bwd(dout, dh, x, g, scale):
    s = dout.shape[0]
    ts = min(s, 512)

    def body(do_ref, dh_ref, x_ref, g_ref, sc_ref, dx_ref, dsh_ref, dsc_ref, dg_ref):
        first = pl.program_id(0) == 0
        dhv, xv, gv = dh_ref[...], x_ref[...], g_ref[...]
        r = _rstd(xv)
        xn = xv * r
        _acc(dsh_ref, first, _colsum(dhv))
        _acc(dsc_ref, first, _colsum(dhv * (xn * gv)))
        dhp = dhv * (1.0 + sc_ref[...])
        _acc(dg_ref, first, _colsum(dhp * xn))
        dxn = dhp * gv
        dx_ref[...] = do_ref[...] + r * (dxn - xn * jnp.mean(dxn * xn, axis=-1, keepdims=True))

    return _pc(body, "prenorm_bwd", (s // ts,), [_row_spec(ts, D)] * 3 + [_vec_spec(D)] * 2,
               [_row_spec(ts, D)] + [_vec_spec(D)] * 3,
               [_sds((s, D))] + [_sds((1, D))] * 3)(dout, dh, x, g, scale)


def _loss_fwd_bwd(y, tgt):
    s = y.shape[0]
    ts = min(s, 512)
    nt = s // ts

    def body(y_ref, t_ref, loss_ref, dy_ref, acc_ref):
        t = pl.program_id(0)
        e = y_ref[...] - t_ref[...]
        dy_ref[...] = e * (1.0 / D)
        _acc(acc_ref, t == 0, _colsum(e * e))

        @pl.when(t == nt - 1)
        def _():
            loss_ref[...] = jnp.full((1, LANE), 0.5 / D, F32) * jnp.sum(acc_ref[...])

    return _pc(body, "loss", (nt,), [_row_spec(ts, D)] * 2,
               [pl.BlockSpec((1, LANE), lambda t: (0, 0)), _row_spec(ts, D)],
               [_sds((1, LANE)), _sds((s, D))], scratch=[pltpu.VMEM((1, D), F32)])(y, tgt)


def _sigmoid(v):
    return 1.0 / (1.0 + jnp.exp(-v))


def _swiglu_fwd(z):
    _, s, _ = z.shape
    ts = min(s, 512)
    z4 = z.reshape(2, 4, s, FSH)

    def body(z_ref, o_ref):
        a, b = z_ref[0], z_ref[1]
        o_ref[...] = (a * _sigmoid(a) * b).astype(BF16)

    return _pc(body, "swiglu_fwd", (4, s // ts), [pl.BlockSpec((2, None, ts, FSH), lambda k, t: (0, k, t, 0))],
               pl.BlockSpec((None, ts, FSH), lambda k, t: (k, t, 0)), _sds((4, s, FSH), BF16))(z4)


def _swiglu_bwd(z, dact):
    _, s, _ = z.shape
    ts = min(s, 512)
    z4 = z.reshape(2, 4, s, FSH)

    def body(z_ref, d_ref, o_ref):
        a, b, d = z_ref[0], z_ref[1], d_ref[...]
        sg = _sigmoid(a)
        o_ref[0] = (d * b * (sg * (1.0 + a * (1.0 - sg)))).astype(BF16)
        o_ref[1] = (d * (a * sg)).astype(BF16)

    spec = pl.BlockSpec((2, None, ts, FSH), lambda k, t: (0, k, t, 0))
    out = _pc(body, "swiglu_bwd", (4, s // ts), [spec, pl.BlockSpec((None, ts, FSH), lambda k, t: (k, t, 0))],
              spec, _sds((2, 4, s, FSH), BF16))(z4, dact)
    return out.reshape(8, s, FSH)


def _ffn_fwd(h, win, wout):
    s = h.shape[0]
    ts = _tile(s)
    wout = wout.reshape(4, FSH, D)
    z = _mm("ffn_in", h, win, ((1,), (1,)), (N_DEV, s // ts),
            pl.BlockSpec((ts, D), lambda j, t: (t, 0)), pl.BlockSpec((None, FSH, D), lambda j, t: (j, 0, 0)),
            pl.BlockSpec((None, ts, FSH), lambda j, t: (j, t, 0)), _sds((N_DEV, s, FSH)))
    act = _swiglu_fwd(z)
    f = _mm("ffn_out", act, wout, ((1,), (0,)), (s // ts, 4),
            pl.BlockSpec((None, ts, FSH), lambda t, k: (k, t, 0)), pl.BlockSpec((None, FSH, D), lambda t, k: (k, 0, 0)),
            pl.BlockSpec((ts, D), lambda t, k: (t, 0)), _sds((s, D)), acc_axis=1)
    return f, (h, z, act)


def _ffn_bwd(df, saved, win, wout):
    h, z, act = saved
    s = h.shape[0]
    ts = _tile(s)
    wout = wout.reshape(4, FSH, D)
    dact = _mm("ffn_out_dx", df, wout, ((1,), (1,)), (4, s // ts),
               pl.BlockSpec((ts, D), lambda k, t: (t, 0)), pl.BlockSpec((None, FSH, D), lambda k, t: (k, 0, 0)),
               pl.BlockSpec((None, ts, FSH), lambda k, t: (k, t, 0)), _sds((4, s, FSH)))
    dwout = _mm("ffn_out_dw", act, df, ((0,), (0,)), (4, 2),
                pl.BlockSpec((None, s, FSH), lambda k, j: (k, 0, 0)), pl.BlockSpec((s, D // 2), lambda k, j: (0, j)),
                pl.BlockSpec((None, FSH, D // 2), lambda k, j: (k, 0, j)), _sds((4, FSH, D), BF16))
    dz = _swiglu_bwd(z, dact)
    dh = _mm("ffn_in_dx", dz, win, ((1,), (0,)), (s // ts, N_DEV),
             pl.BlockSpec((None, ts, FSH), lambda t, j: (j, t, 0)), pl.BlockSpec((None, FSH, D), lambda t, j: (j, 0, 0)),
             pl.BlockSpec((ts, D), lambda t, j: (t, 0)), _sds((s, D)), acc_axis=1)
    dwin = _mm("ffn_in_dw", dz, h, ((0,), (0,)), (N_DEV, 2),
               pl.BlockSpec((None, s, FSH), lambda j, i: (j, 0, 0)), pl.BlockSpec((s, D // 2), lambda j, i: (0, i)),
               pl.BlockSpec((None, FSH, D // 2), lambda j, i: (j, 0, i)), _sds((N_DEV, FSH, D), BF16))
    return dh, dwin, dwout.reshape(N_DEV, D_FF // N_DEV, D)


def _shift_rows(v, k, row, s, back):
    if back:
        return jnp.where(row < s - k, pltpu.roll(v, s - k, 0), 0.0)
    return jnp.where(row >= k, pltpu.roll(v, k, 0), 0.0)


def _window_sum(v, w, row, s, back):
    k = 1
    while k < w:
        v = v + _shift_rows(v, k, row, s, back)
        k *= 2
    return v


def _pool_fwd(z, pool_w, pool_scale):
    s = z.shape[0]

    def body(z_ref, w_ref, sc_ref, y_ref, d_ref):
        row = lax.broadcasted_iota(jnp.int32, (s, HD), 0)
        for g, w in enumerate(POOL_WINDOWS):
            sl = slice(g * HD, (g + 1) * HD)
            a = z_ref[:, sl]
            cnt = jnp.minimum(row + 1, w).astype(F32)
            d = (_window_sum(a, w, row, s, False) / cnt - a).astype(BF16)
            d_ref[:, sl] = d
            y = jnp.dot(d, _bf(w_ref[g]), preferred_element_type=F32)
            y_ref[:, sl] = (y * sc_ref[:, sl]).astype(BF16)

    return _pc(body, "pool_fwd", (1,),
               [pl.BlockSpec((s, NH * HD), lambda i: (0, 0)), pl.BlockSpec((NH, HD, HD), lambda i: (0, 0, 0)),
                pl.BlockSpec((1, NH * HD), lambda i: (0, 0))],
               [pl.BlockSpec((s, NH * HD), lambda i: (0, 0))] * 2,
               [_sds((s, NH * HD), BF16)] * 2)(z, pool_w, pool_scale)


def _pool_bwd(dy, d, pool_w, pool_scale):
    s = dy.shape[0]

    def body(dy_ref, d_ref, w_ref, sc_ref, dz_ref, dw_ref, dsc_ref):
        row = lax.broadcasted_iota(jnp.int32, (s, HD), 0)
        for g, w in enumerate(POOL_WINDOWS):
            sl = slice(g * HD, (g + 1) * HD)
            dyg, dg, wg = dy_ref[:, sl], d_ref[:, sl], _bf(w_ref[g])
            yraw = jnp.dot(dg, wg, preferred_element_type=F32)
            dsc_ref[:, sl] = _colsum(dyg * yraw)
            dyr = _bf(dyg * sc_ref[:, sl])
            dw_ref[g] = lax.dot_general(dg, dyr, (((0,), (0,)), ((), ())), preferred_element_type=F32)
            dd = lax.dot_general(dyr, wg, (((1,), (1,)), ((), ())), preferred_element_type=F32)
            cnt = jnp.minimum(row + 1, w).astype(F32)
            dz_ref[:, sl] = (_window_sum(dd / cnt, w, row, s, True) - dd).astype(BF16)

    return _pc(body, "pool_bwd", (1,),
               [pl.BlockSpec((s, NH * HD), lambda i: (0, 0)), pl.BlockSpec((s, NH * HD), lambda i: (0, 0)),
                pl.BlockSpec((NH, HD, HD), lambda i: (0, 0, 0)), pl.BlockSpec((1, NH * HD), lambda i: (0, 0))],
               [pl.BlockSpec((s, NH * HD), lambda i: (0, 0)), pl.BlockSpec((NH, HD, HD), lambda i: (0, 0, 0)),
                pl.BlockSpec((1, NH * HD), lambda i: (0, 0))],
               [_sds((s, NH * HD), BF16), _sds((NH, HD, HD)), _sds((1, NH * HD))])(dy, d, pool_w, pool_scale)


def _gelu(v):
    return 0.5 * v * (1.0 + jnp.tanh(GELU_C * (v + 0.044715 * (v * v * v))))


def _gelu_grad(v):
    t = jnp.tanh(GELU_C * (v + 0.044715 * (v * v * v)))
    return 0.5 * (1.0 + t) + 0.5 * v * (1.0 - t * t) * (GELU_C * (1.0 + 3.0 * 0.044715 * (v * v)))


def _causal_mask():
    return lax.broadcasted_iota(jnp.int32, (HD, HD), 0) >= lax.broadcasted_iota(jnp.int32, (HD, HD), 1)


def _sgu_specs():
    w = NH * HD
    return [pl.BlockSpec((HD, w), lambda c: (c, 1)), pl.BlockSpec((HD, w), lambda c: (c, 2)),
            pl.BlockSpec((1, w), lambda c: (0, 0)), pl.BlockSpec((1, w), lambda c: (0, 0)),
            pl.BlockSpec((NH, HD, HD), lambda c: (0, 0, 0)), pl.BlockSpec((HD, LANE), lambda c: (0, 0))]


def _sgu_head(v, lng_ref, lnb_ref, w_ref, h):
    sl = slice(h * HD, (h + 1) * HD)
    vh = v[:, sl]
    xc = vh - jnp.mean(vh, axis=-1, keepdims=True)
    rs = lax.rsqrt(jnp.mean(xc * xc, axis=-1, keepdims=True) + EPS)
    vhat = xc * rs
    vn = _bf(vhat * lng_ref[:, sl] + lnb_ref[:, sl])
    wc = _bf(jnp.where(_causal_mask(), w_ref[h], 0.0))
    return sl, rs, vhat, vn, wc


def _sgu_fwd(z, ln_g, ln_b, sgu_w, sgu_bt):
    s = z.shape[0]

    def body(zu_ref, zv_ref, lng_ref, lnb_ref, w_ref, bt_ref, y_ref):
        u, v = _gelu(zu_ref[...]), _gelu(zv_ref[...])
        for h in range(NH):
            sl, _, _, vn, wc = _sgu_head(v, lng_ref, lnb_ref, w_ref, h)
            sp = jnp.dot(wc, vn, preferred_element_type=F32) + bt_ref[:, h:h + 1]
            y_ref[:, sl] = (u[:, sl] * sp).astype(BF16)

    return _pc(body, "sgu_fwd", (s // HD,), _sgu_specs(), pl.BlockSpec((HD, NH * HD), lambda c: (c, 0)),
               _sds((s, NH * HD), BF16))(z, z, ln_g, ln_b, sgu_w, sgu_bt)


def _sgu_bwd(z, dy, ln_g, ln_b, sgu_w, sgu_bt, head_sum):
    s = z.shape[0]
    w = NH * HD
    nc = s // HD

    def body(zu_ref, zv_ref, lng_ref, lnb_ref, w_ref, bt_ref, dy_ref, hs_ref,
             dzu_ref, dzv_ref, dlng_ref, dlnb_ref, dw_ref, dbt_ref, dsacc_ref):
        c = pl.program_id(0)
        first = c == 0
        zu, zv = zu_ref[...], zv_ref[...]
        u, v = _gelu(zu), _gelu(zv)
        dyv = dy_ref[...]
        gu, gv = _gelu_grad(zu), _gelu_grad(zv)
        ds = dyv * u
        _acc(dsacc_ref, first, ds)
        for h in range(NH):
            sl, rs, vhat, vn, wc = _sgu_head(v, lng_ref, lnb_ref, w_ref, h)
            sp = jnp.dot(wc, vn, preferred_element_type=F32) + bt_ref[:, h:h + 1]
            dzu_ref[:, sl] = (dyv[:, sl] * sp * gu[:, sl]).astype(BF16)
            dsh = _bf(ds[:, sl])
            dwh = lax.dot_general(dsh, vn, (((1,), (1,)), ((), ())), preferred_element_type=F32)
            dwh = jnp.where(_causal_mask(), dwh, 0.0)

            @pl.when(first)
            def _():
                dw_ref[h] = dwh

            @pl.when(jnp.logical_not(first))
            def _():
                dw_ref[h] += dwh

            dvn = lax.dot_general(wc, dsh, (((0,), (0,)), ((), ())), preferred_element_type=F32)
            g_col = _colsum(dvn * vhat)
            b_col = _colsum(dvn)

            @pl.when(first)
            def _():
                dlng_ref[:, sl] = g_col
                dlnb_ref[:, sl] = b_col

            @pl.when(jnp.logical_not(first))
            def _():
                dlng_ref[:, sl] += g_col
                dlnb_ref[:, sl] += b_col

            dvh = dvn * lng_ref[:, sl]
            dv = rs * (dvh - jnp.mean(dvh, axis=-1, keepdims=True) - vhat * jnp.mean(dvh * vhat, axis=-1, keepdims=True))
            dzv_ref[:, sl] = (dv * gv[:, sl]).astype(BF16)

        @pl.when(c == nc - 1)
        def _():
            dbt_ref[...] = jnp.dot(dsacc_ref[...], hs_ref[...], preferred_element_type=F32, precision=HIGHEST)

    outs = _pc(body, "sgu_bwd", (nc,),
               _sgu_specs() + [pl.BlockSpec((HD, w), lambda c: (c, 1)), pl.BlockSpec((w, LANE), lambda c: (0, 0))],
               [pl.BlockSpec((HD, w), lambda c: (c, 0))] * 2 + [pl.BlockSpec((1, w), lambda c: (0, 0))] * 2
               + [pl.BlockSpec((NH, HD, HD), lambda c: (0, 0, 0)), pl.BlockSpec((HD, LANE), lambda c: (0, 0))],
               [_sds((s, w), BF16)] * 2 + [_sds((1, w))] * 2 + [_sds((NH, HD, HD)), _sds((HD, LANE))],
               scratch=[pltpu.VMEM((HD, w), F32)])(z, z, ln_g, ln_b, sgu_w, sgu_bt, dy, head_sum)
    return outs


def _cmul(ar, ai, br, bi):
    return ar * br - ai * bi, ar * bi + ai * br


def _ssm_prep(lam_re, lam_im, lam_re_rep, lam_im_rep, log_dt, b_re, b_im):
    def disc(lr, li, dt):
        mag = jnp.exp(lr * dt)
        return mag * jnp.cos(li * dt), mag * jnp.sin(li * dt)

    def body(lr_ref, li_ref, lrr_ref, lir_ref, ldt_ref, br_ref, bi_ref, or_ref, oi_ref, bbr_ref, bbi_ref):
        dt = jnp.exp(ldt_ref[...])
        or_ref[...], oi_ref[...] = disc(lr_ref[...], li_ref[...], dt)
        lr, li = lrr_ref[...], lir_ref[...]
        er, ei = disc(lr, li, dt)
        den = lr * lr + li * li
        kr = ((er - 1.0) * lr + ei * li) / den
        ki = (ei * lr - (er - 1.0) * li) / den
        bbr_ref[...], bbi_ref[...] = _cmul(kr, ki, br_ref[...], bi_ref[...])

    small = pl.BlockSpec((SSM_G, SSM_P), lambda i: (0, 0))
    wide = pl.BlockSpec((SSM_G, SSM_P * SSM_N), lambda i: (0, 0))
    col = pl.BlockSpec((SSM_G, 1), lambda i: (0, 0))
    return _pc(body, "ssm_prep", (1,), [small, small, wide, wide, col, wide, wide], [small, small, wide, wide],
               [_sds((SSM_G, SSM_P))] * 2 + [_sds((SSM_G, SSM_P * SSM_N))] * 2)(
        lam_re, lam_im, lam_re_rep, lam_im_rep, log_dt, b_re, b_im)


def _ssm_param_bwd(g_lam_re, g_lam_im, g_bb_re, g_bb_im, lam_re, lam_im, lam_re_rep, lam_im_rep, log_dt, b_re, b_im, seg):
    def body(glr_ref, gli_ref, gbr_ref, gbi_ref, lr_ref, li_ref, lrr_ref, lir_ref, ldt_ref, br_ref, bi_ref, seg_ref,
             dlr_ref, dli_ref, ddt_ref, dbr_ref, dbi_ref):
        dt = jnp.exp(ldt_ref[...])
        lr, li = lrr_ref[...], lir_ref[...]
        mag = jnp.exp(lr * dt)
        er, ei = mag * jnp.cos(li * dt), mag * jnp.sin(li * dt)
        den = lr * lr + li * li
        kr = ((er - 1.0) * lr + ei * li) / den
        ki = (ei * lr - (er - 1.0) * li) / den
        gbr, gbi = gbr_ref[...], gbi_ref[...]
        dbr_ref[...], dbi_ref[...] = _cmul(kr, -ki, gbr, gbi)
        tr, ti = _cmul(br_ref[...], -bi_ref[...], gbr, gbi)
        gkr = jnp.dot(tr, seg_ref[...], preferred_element_type=F32, precision=HIGHEST)
        gki = jnp.dot(ti, seg_ref[...], preferred_element_type=F32, precision=HIGHEST)
        lr, li = lr_ref[...], li_ref[...]
        mag = jnp.exp(lr * dt)
        er, ei = mag * jnp.cos(li * dt), mag * jnp.sin(li * dt)
        den = lr * lr + li * li
        ir, ii = lr / den, -li / den
        kr, ki = _cmul(er - 1.0, ei, ir, ii)
        ar, ai = _cmul(ir, -ii, gkr, gki)
        glr, gli = glr_ref[...] + ar, gli_ref[...] + ai
        qr, qi = _cmul(kr, ki, ir, ii)
        g1r, g1i = _cmul(-qr, qi, gkr, gki)
        g2r, g2i = _cmul(dt * er, -dt * ei, glr, gli)
        dlr_ref[...] = g1r + g2r
        dli_ref[...] = g1i + g2i
        wr, wi = _cmul(lr, li, er, ei)
        g_dt = jnp.sum(wr * glr + wi * gli, axis=-1, keepdims=True)
        ddt_ref[...] = jnp.broadcast_to(dt * g_dt, (SSM_G, LANE))

    small = pl.BlockSpec((SSM_G, SSM_P), lambda i: (0, 0))
    wide = pl.BlockSpec((SSM_G, SSM_P * SSM_N), lambda i: (0, 0))
    col = pl.BlockSpec((SSM_G, 1), lambda i: (0, 0))
    segs = pl.BlockSpec((SSM_P * SSM_N, SSM_P), lambda i: (0, 0))
    return _pc(body, "ssm_param_bwd", (1,), [small, small, wide, wide, small, small, wide, wide, col, wide, wide, segs],
               [small, small, pl.BlockSpec((SSM_G, LANE), lambda i: (0, 0)), wide, wide],
               [_sds((SSM_G, SSM_P))] * 2 + [_sds((SSM_G, LANE))] + [_sds((SSM_G, SSM_P * SSM_N))] * 2)(
        g_lam_re, g_lam_im, g_bb_re, g_bb_im, lam_re, lam_im, lam_re_rep, lam_im_rep, log_dt, b_re, b_im, seg)


SCAN_LANES = 256
SCAN_ROWS = 8


def _ssm_scan(b_re, b_im, lam_re, lam_im, reverse):
    s = b_re.shape[0]
    nt = s // SCAN_ROWS
    ln, rows = SCAN_LANES, SCAN_ROWS

    def body(lr_ref, li_ref, br_ref, bi_ref, or_ref, oi_ref):
        l1 = (lr_ref[...], li_ref[...])
        pw = [l1]
        for _ in range(rows - 1):
            pw.append(_cmul(*pw[-1], *l1))
        row = lax.broadcasted_iota(jnp.int32, (rows, ln), 0)
        expo = (rows - row) if reverse else (row + 1)
        pr = jnp.zeros((rows, ln), F32)
        pi = jnp.zeros((rows, ln), F32)
        for e in range(1, rows + 1):
            pr = jnp.where(expo == e, pw[e - 1][0], pr)
            pi = jnp.where(expo == e, pw[e - 1][1], pi)
        lk = {k: (jnp.broadcast_to(pw[k - 1][0], (rows, ln)), jnp.broadcast_to(pw[k - 1][1], (rows, ln))) for k in (1, 2, 4)}

        def step(i, carry):
            cr, ci = carry
            t = (nt - 1 - i) if reverse else i
            r0 = pl.multiple_of(t * rows, rows)
            xr, xi = br_ref[pl.ds(r0, rows), :], bi_ref[pl.ds(r0, rows), :]
            for k in (1, 2, 4):
                sr = _shift_rows(xr, k, row, rows, reverse)
                si = _shift_rows(xi, k, row, rows, reverse)
                ar, ai = _cmul(lk[k][0], lk[k][1], sr, si)
                xr, xi = xr + ar, xi + ai
            ar, ai = _cmul(pr, pi, cr, ci)
            xr, xi = xr + ar, xi + ai
            or_ref[pl.ds(r0, rows), :] = xr
            oi_ref[pl.ds(r0, rows), :] = xi
            if reverse:
                return xr[0:1], xi[0:1]
            return xr[rows - 1:rows], xi[rows - 1:rows]

        zero = jnp.zeros((1, ln), F32)
        lax.fori_loop(0, nt, step, (zero, zero))

    vec = pl.BlockSpec((1, ln), lambda j: (0, j))
    blk = pl.BlockSpec((s, ln), lambda j: (0, j))
    return _pc(body, "ssm_scan_bwd" if reverse else "ssm_scan_fwd", (SSM_L // ln,), [vec, vec, blk, blk], [blk, blk],
               [_sds((s, SSM_L))] * 2)(lam_re, lam_im, b_re, b_im)


def _ssm_in(name, v, w_bd):
    s = v.shape[0]
    ts = _tile(s)
    half = SSM_GB * SSM_P

    def body(v_ref, w_ref, or_ref, oi_ref):
        r = jnp.dot(_bf(v_ref[...]), w_ref[...], preferred_element_type=F32)
        or_ref[...] = r[:, :half]
        oi_ref[...] = r[:, half:]

    out = pl.BlockSpec((ts, half), lambda q, t: (t, q))
    return _pc(body, name, (SSM_NB, s // ts),
               [pl.BlockSpec((ts, SSM_GB * SSM_N), lambda q, t: (t, q)), pl.BlockSpec((None, SSM_GB * SSM_N, 2 * half), lambda q, t: (q, 0, 0))],
               [out, out], [_sds((s, SSM_L))] * 2)(v, w_bd)


def _ssm_out(name, x_re, x_im, w_bd):
    s = x_re.shape[0]
    ts = _tile(s)
    half = SSM_GB * SSM_P
    nt = (((1,), (1,)), ((), ()))

    def body(xr_ref, xi_ref, w_ref, o_ref):
        w = w_ref[...]
        o_ref[...] = (lax.dot_general(_bf(xr_ref[...]), w[:, :half], nt, preferred_element_type=F32)
                      + lax.dot_general(_bf(xi_ref[...]), w[:, half:], nt, preferred_element_type=F32))

    xin = pl.BlockSpec((ts, half), lambda q, t: (t, q))
    return _pc(body, name, (SSM_NB, s // ts),
               [xin, xin, pl.BlockSpec((None, SSM_GB * SSM_N, 2 * half), lambda q, t: (q, 0, 0))],
               pl.BlockSpec((ts, SSM_GB * SSM_N), lambda q, t: (t, q)), _sds((s, SSM_G * SSM_N)))(x_re, x_im, w_bd)


def _ssm_outer(name, v, x_re, x_im):
    s = v.shape[0]
    ts = min(s, 512)
    nt = s // ts
    half = SSM_GB * SSM_P
    rows = SSM_GB * SSM_N
    tn = (((0,), (0,)), ((), ()))

    def body(v_ref, xr_ref, xi_ref, or_ref, oi_ref, acc_ref):
        vv = _bf(v_ref[...])
        pr = lax.dot_general(vv, _bf(xr_ref[...]), tn, preferred_element_type=F32)
        pi = lax.dot_general(vv, _bf(xi_ref[...]), tn, preferred_element_type=F32)
        t = pl.program_id(1)

        @pl.when(t == 0)
        def _():
            acc_ref[:, :half] = pr
            acc_ref[:, half:] = pi

        @pl.when(t > 0)
        def _():
            acc_ref[:, :half] += pr
            acc_ref[:, half:] += pi

        @pl.when(t == nt - 1)
        def _():
            row_g = lax.broadcasted_iota(jnp.int32, (rows, LANE), 0) // SSM_N
            lane_g = lax.broadcasted_iota(jnp.int32, (rows, LANE), 1) // SSM_P
            for part, o_ref in enumerate((or_ref, oi_ref)):
                fold = jnp.zeros((rows, LANE), F32)
                for cb in range(half // LANE):
                    blk = acc_ref[:, part * half + cb * LANE:part * half + (cb + 1) * LANE]
                    fold = fold + jnp.where(2 * cb + lane_g == row_g, blk, 0.0)
                o_ref[...] = jnp.where(row_g % 2 == 0, fold, pltpu.roll(fold, SSM_P, 1))

    xin = pl.BlockSpec((ts, half), lambda q, t: (t, q))
    out = pl.BlockSpec((None, rows, LANE), lambda q, t: (q, 0, 0))
    return _pc(body, name, (SSM_NB, nt), [pl.BlockSpec((ts, rows), lambda q, t: (t, q)), xin, xin], [out, out],
               [_sds((SSM_NB, rows, LANE))] * 2, scratch=[pltpu.VMEM((rows, 2 * half), F32)])(v, x_re, x_im)


def _ssm_dlam(x_re, x_im, a_re, a_im):
    s = x_re.shape[0]
    ln = SCAN_LANES

    def body(xr_ref, xi_ref, ar_ref, ai_ref, or_ref, oi_ref):
        row = lax.broadcasted_iota(jnp.int32, (s, ln), 0)
        xr = _shift_rows(xr_ref[...], 1, row, s, False)
        xi = _shift_rows(xi_ref[...], 1, row, s, False)
        ar, ai = ar_ref[...], ai_ref[...]
        or_ref[...] = _colsum(xr * ar + xi * ai)
        oi_ref[...] = _colsum(xr * ai - xi * ar)

    blk = pl.BlockSpec((s, ln), lambda j: (0, j))
    vec = pl.BlockSpec((1, ln), lambda j: (0, j))
    return _pc(body, "ssm_dlam", (SSM_L // ln,), [blk] * 4, [vec, vec], [_sds((1, SSM_L))] * 2)(x_re, x_im, a_re, a_im)


def _ssm_act_fwd(y, u, d_skip):
    s = y.shape[0]
    ts = min(s, 512)

    def body(y_ref, u_ref, d_ref, o_ref):
        o_ref[...] = _gelu(y_ref[...] + d_ref[...] * u_ref[...]).astype(BF16)

    return _pc(body, "ssm_act_fwd", (s // ts,), [_row_spec(ts, D)] * 2 + [_vec_spec(D)], _row_spec(ts, D),
               _sds((s, D), BF16))(y, u, d_skip)


def _ssm_act_bwd(dg, y, u, d_skip):
    s = y.shape[0]
    ts = min(s, 512)

    def body(dg_ref, y_ref, u_ref, d_ref, dy_ref, dd_ref):
        uv = u_ref[...]
        dy = dg_ref[...] * _gelu_grad(y_ref[...] + d_ref[...] * uv)
        dy_ref[...] = dy.astype(BF16)
        _acc(dd_ref, pl.program_id(0) == 0, _colsum(dy * uv))

    return _pc(body, "ssm_act_bwd", (s // ts,), [_row_spec(ts, D)] * 3 + [_vec_spec(D)], [_row_spec(ts, D), _vec_spec(D)],
               [_sds((s, D), BF16), _sds((1, D))])(dg, y, u, d_skip)


def _axpy(a, b, d_skip):
    s = a.shape[0]
    ts = min(s, 512)

    def body(a_ref, b_ref, d_ref, o_ref):
        o_ref[...] = (a_ref[...] + d_ref[...] * b_ref[...].astype(F32)).astype(BF16)

    return _pc(body, "ssm_du", (s // ts,), [_row_spec(ts, D)] * 2 + [_vec_spec(D)], _row_spec(ts, D),
               _sds((s, D), BF16))(a, b, d_skip)


def _glu_fwd(zz):
    s = zz.shape[0]
    ts = min(s, 512)

    def body(a_ref, b_ref, o_ref):
        o_ref[...] = a_ref[...] * _sigmoid(b_ref[...])

    return _pc(body, "glu_fwd", (s // ts,), [_row_spec(ts, D, 0), _row_spec(ts, D, 1)], _row_spec(ts, D), _sds((s, D)))(zz, zz)


def _glu_bwd(zz, df):
    s = zz.shape[0]
    ts = min(s, 512)

    def body(a_ref, b_ref, df_ref, o_ref):
        sg = _sigmoid(b_ref[...])
        dfv = df_ref[...].astype(F32)
        o_ref[:, :D] = (dfv * sg).astype(BF16)
        o_ref[:, D:] = (dfv * a_ref[...] * sg * (1.0 - sg)).astype(BF16)

    return _pc(body, "glu_bwd", (s // ts,), [_row_spec(ts, D, 0), _row_spec(ts, D, 1), _row_spec(ts, D)],
               _row_spec(ts, 2 * D), _sds((s, 2 * D), BF16))(zz, zz, df)


def _ssm_block_diag(m_re, m_im):
    rows, half = SSM_GB * SSM_N, SSM_GB * SSM_P
    expand = jnp.tile(jnp.eye(SSM_P, dtype=BF16), (1, SSM_GB))

    def body(mr_ref, mi_ref, e_ref, o_ref):
        keep = (lax.broadcasted_iota(jnp.int32, (rows, half), 0) // SSM_N
                == lax.broadcasted_iota(jnp.int32, (rows, half), 1) // SSM_P)
        for part, m_ref in enumerate((mr_ref, mi_ref)):
            t = jnp.dot(_bf(m_ref[...]), e_ref[...], preferred_element_type=F32)
            o_ref[:, part * half:(part + 1) * half] = jnp.where(keep, t, 0.0).astype(BF16)

    blk = pl.BlockSpec((rows, SSM_P), lambda q: (q, 0))
    return _pc(body, "ssm_block_diag", (SSM_NB,), [blk, blk, pl.BlockSpec((SSM_P, half), lambda q: (0, 0))],
               pl.BlockSpec((None, rows, 2 * half), lambda q: (q, 0, 0)), _sds((SSM_NB, rows, 2 * half), BF16))(m_re, m_im, expand)


def _mod_part(c_all, ada_w):
    n = ada_w.shape[-1]

    def body(c_ref, w_ref, o_ref):
        cv = c_ref[...]
        cond = _bf(cv * _sigmoid(cv))
        o_ref[...] = jnp.dot(cond, _bf(w_ref[...]), preferred_element_type=F32)

    return _pc(body, "mod_part", (2,), [pl.BlockSpec((N_DEV, D), lambda l: (0, 0)), pl.BlockSpec((None, D, n), lambda l: (l, 0, 0))],
               pl.BlockSpec((None, N_DEV, n), lambda l: (l, 0, 0)), _sds((2, N_DEV, n)))(c_all, ada_w)


def _ada_w_grad(c_all_t, dmod):
    n = dmod.shape[-1]
    tr = 128

    def body(c_ref, d_ref, o_ref):
        cv = c_ref[...]
        cond = _bf(cv * _sigmoid(cv)).astype(F32)
        dm = _bf(d_ref[...]).astype(F32)
        acc = cond[:, 0:1] * dm[0:1, :]
        for b in range(1, N_DEV):
            acc = acc + cond[:, b:b + 1] * dm[b:b + 1, :]
        o_ref[...] = acc

    return _pc(body, "ada_w_grad", (2, D // tr),
               [pl.BlockSpec((tr, N_DEV), lambda l, t: (t, 0)), pl.BlockSpec((None, N_DEV, n), lambda l, t: (l, 0, 0))],
               pl.BlockSpec((None, tr, n), lambda l, t: (l, t, 0)), _sds((2, D, n)))(c_all_t, dmod)


def _adamw(name, parts, w, m, v, slot=0, prev=None):
    p, r, c = parts.shape
    tr = r
    while tr * c * 4 > (1 << 20) and tr % 16 == 0:
        tr //= 2
    nt = r // tr

    def body(p_ref, w_ref, m_ref, v_ref, *rest):
        g_ref, d_ref, nm_ref, nv_ref = rest[-4:]
        g = p_ref[0].astype(F32)
        for i in range(1, p):
            g = g + p_ref[i].astype(F32)
        g_ref[...] = g
        m2 = B1 * m_ref[...] + (1.0 - B1) * g
        v2 = B2 * v_ref[...] + (1.0 - B2) * (g * g)
        nm_ref[...] = m2
        nv_ref[...] = v2
        m_hat = m2 / (1.0 - B1 ** STEP)
        v_hat = v2 / (1.0 - B2 ** STEP)
        d_ref[...] = -LR * (m_hat / (jnp.sqrt(v_hat) + ADAM_EPS) + WD * w_ref[...])

    blk = pl.BlockSpec((tr, c), lambda t: (slot * nt + t, 0))
    in_specs = [pl.BlockSpec((p, tr, c), lambda t: (0, t, 0)), blk, blk, blk]
    if prev is None:
        return _pc(body, name, (nt,), in_specs, [blk] * 4, [_sds(w.shape)] * 4)(parts, w, m, v)
    return pl.pallas_call(
        body, name=name, grid=(nt,), in_specs=in_specs + [pl.BlockSpec(memory_space=pl.ANY)] * 4, out_specs=[blk] * 4,
        out_shape=[_sds(w.shape)] * 4, input_output_aliases={4 + i: i for i in range(4)},
        compiler_params=pltpu.CompilerParams(dimension_semantics=("arbitrary",), vmem_limit_bytes=VMEM_LIMIT_BYTES))(parts, w, m, v, *prev)


def _sum_parts(parts):
    p, r, c = parts.shape
    tr = r
    while tr * c * 4 > (1 << 19) and tr % 16 == 0:
        tr //= 2

    def body(p_ref, o_ref):
        g = p_ref[0]
        for i in range(1, p):
            g = g + p_ref[i]
        o_ref[...] = g

    return _pc(body, "sum_parts", (r // tr,), [pl.BlockSpec((p, tr, c), lambda t: (0, t, 0))], pl.BlockSpec((tr, c), lambda t: (t, 0)),
               _sds((r, c)))(parts)


def _place():
    x, y, c = lax.axis_index("x"), lax.axis_index("y"), lax.axis_index("c")
    peers = []
    for k in range(1, N_DEV):
        px = (1 - x) if k & 4 else x
        py = (1 - y) if k & 2 else y
        pc = (1 - c) if k & 1 else c
        peers.append(((px, py, pc), 4 * px + 2 * py + pc))
    return 4 * x + 2 * y + c, peers


def _at(ref, idx):
    return ref if idx is None else ref.at[idx]


def _exchange_copies(plan, src_refs, dst_refs, send_sems, recv_sems, local_sems, with_arrivals=True):
    me, peers = _place()
    local = [pltpu.make_async_copy(_at(src_refs[si], sx), _at(dst_refs[di], dx), local_sems.at[i])
             for i, (si, sx, di, dx) in enumerate(plan(me, me))]

    n = len(local)

    def remote(k, i, dev, entry):
        si, sx, di, dx = entry
        return pltpu.make_async_remote_copy(_at(src_refs[si], sx), _at(dst_refs[di], dx), send_sems.at[k * n + i], recv_sems.at[k * n + i],
                                            device_id=dev, device_id_type=MESH)

    sends = [remote(k, i, dev, e) for k, (dev, peer) in enumerate(peers) for i, e in enumerate(plan(me, peer))]
    if not with_arrivals:
        return local, sends, []
    arrivals = [remote(k, i, dev, e) for k, (dev, peer) in enumerate(peers) for i, e in enumerate(plan(peer, me))]
    return local, sends, arrivals


def _sem_shapes(n_copies):
    return [pltpu.SemaphoreType.DMA(((N_DEV - 1) * n_copies,)), pltpu.SemaphoreType.DMA(((N_DEV - 1) * n_copies,)),
            pltpu.SemaphoreType.DMA((n_copies,))]


def _exchange(name, srcs, dst_shapes, plan, n_copies):
    ns, nd = len(srcs), len(dst_shapes)

    def body(*refs):
        local, sends, arrivals = _exchange_copies(plan, refs[:ns], refs[ns:ns + nd], *refs[ns + nd:])
        for cp in local + sends:
            cp.start()
        for cp in arrivals:
            cp.wait_recv()
        for cp in sends:
            cp.wait_send()
        for cp in local:
            cp.wait()

    any_spec = pl.BlockSpec(memory_space=pl.ANY)
    return pl.pallas_call(
        body, name=name, in_specs=[any_spec] * ns, out_specs=[any_spec] * nd, out_shape=list(dst_shapes),
        scratch_shapes=_sem_shapes(n_copies))(*srcs)


HBM_SPEC = pl.BlockSpec(memory_space=pltpu.HBM)
SEM_SPEC = pl.BlockSpec(memory_space=pltpu.SEMAPHORE)
SIDE_EFFECT = pltpu.SideEffectType.DATAFLOW_SIDE_EFFECTING


def _exchange_start(name, srcs, dst_shapes, plan, n_copies, order):
    ns, nd = len(srcs), len(dst_shapes)
    nb = ns + nd

    def body(*refs):
        local, sends, _ = _exchange_copies(plan, refs[:ns], refs[ns:nb], *refs[nb + 1:nb + 4], with_arrivals=False)
        for cp in local + sends:
            cp.start()
        refs[-1][...] = jnp.zeros((8, LANE), F32)

    lands = [pltpu.with_memory_space_constraint(lax.empty(d.shape, d.dtype), pltpu.HBM) for d in dst_shapes]
    srcs = [pltpu.with_memory_space_constraint(a, pltpu.HBM) for a in srcs]
    bufs = srcs + lands
    out = pl.pallas_call(
        body, name=name, in_specs=[HBM_SPEC] * nb + [pl.BlockSpec(memory_space=pl.ANY)],
        out_specs=[SEM_SPEC] * 3 + [HBM_SPEC] * nb + [pl.BlockSpec(memory_space=pltpu.VMEM)],
        out_shape=_sem_shapes(n_copies) + [pltpu.HBM(a.shape, a.dtype) for a in bufs] + [_sds((8, LANE))],
        input_output_aliases={i: 3 + i for i in range(nb)},
        compiler_params=pltpu.CompilerParams(has_side_effects=SIDE_EFFECT))(*bufs, order)
    return out[:3], out[3:3 + ns], out[3 + ns:3 + nb], out[-1]


def _exchange_wait(name, sems, srcs, lands, plan, after):
    ns, nd = len(srcs), len(lands)
    nb = ns + nd

    def body(*refs):
        local, sends, arrivals = _exchange_copies(plan, refs[:ns], refs[ns:nb], *refs[nb:nb + 3])
        for cp in arrivals:
            cp.wait_recv()
        for cp in sends:
            cp.wait_send()
        for cp in local:
            cp.wait()

    bufs = list(srcs) + list(lands)
    out = pl.pallas_call(
        body, name=name, in_specs=[HBM_SPEC] * nb + [SEM_SPEC] * 3 + [pl.BlockSpec(memory_space=pl.ANY)],
        out_specs=[HBM_SPEC] * nb, out_shape=[pltpu.HBM(a.shape, a.dtype) for a in bufs],
        input_output_aliases={i: i for i in range(nb)},
        compiler_params=pltpu.CompilerParams(has_side_effects=SIDE_EFFECT))(*bufs, *sems, after)
    return out[ns:]


def _all_gather(name, arrs):
    plan = lambda me, peer: [(i, None, i, me) for i in range(len(arrs))]
    return _exchange(name, arrs, [_sds((N_DEV,) + a.shape, a.dtype) for a in arrs], plan, len(arrs))


def _sublayer_fwd(x, fn, mod3, g_pre, g_post, rw):
    h = _prenorm_fwd(x, g_pre, mod3[1:2], mod3[0:1])
    f, saved = fn(h)
    return _postnorm_fwd(x, f, g_post, mod3[2:3], rw), (x, f, saved)


def _sublayer_bwd(dout, saved, fn_bwd, mod3, g_pre, g_post, rw):
    x, f, inner = saved
    df, dgate, dg_post = _postnorm_bwd(dout, f, g_post, mod3[2:3], rw)
    dh, extra = fn_bwd(df, inner)
    dx, dshift, dscale, dg_pre = _prenorm_bwd(dout, dh, x, g_pre, mod3[1:2])
    return dx, jnp.concatenate([dshift, dscale, dgate], axis=0), dg_pre, dg_post, extra


def _mix0_fwd(h, p):
    z = _mm_nt("mix0_in", h, p["ab_w_in"])
    y_a, d = _pool_fwd(z, p["pool_w"], p["pool_scale"])
    y_b = _sgu_fwd(z, p["sgu_ln_g"], p["sgu_ln_b"], p["sgu_w"], p["sgu_bt"])
    ycat = jnp.concatenate([y_a, y_b], axis=1)
    return _mm_nn("mix0_out", ycat, p["ab_w_out"]), (h, z, d, ycat)


def _mix0_bwd(df, saved, p):
    h, z, d, ycat = saved
    dycat = _mm_nt("mix0_out_dx", df, p["ab_w_out"])
    g = {"ab_w_out": _mm_tn("mix0_out_dw", ycat, df, BF16)}
    dz_p, g["pool_w"], g["pool_scale"] = _pool_bwd(dycat, d, p["pool_w"], p["pool_scale"])
    dz_u, dz_v, g["sgu_ln_g"], g["sgu_ln_b"], g["sgu_w"], dbt = _sgu_bwd(
        z, dycat, p["sgu_ln_g"], p["sgu_ln_b"], p["sgu_w"], p["sgu_bt"], p["head_sum"])
    g["sgu_b"] = dbt[:, :NH].T
    dz = jnp.concatenate([dz_p, dz_u, dz_v], axis=1)
    g["ab_w_in"] = _mm_tn("mix0_in_dw", dz, h, BF16)
    return _mm_nn("mix0_in_dx", dz, p["ab_w_in"]), g


def _mix1_fwd(h, p):
    u = _mm_nn("ssm_w_in", h, p["ssm_w_in"])
    bu_re, bu_im = _ssm_in("ssm_bu", u, p["wb_bd"])
    x_re, x_im = _ssm_scan(bu_re, bu_im, p["lam_bar_re"], p["lam_bar_im"], False)
    y = _ssm_out("ssm_y", x_re, x_im, p["wc_bd"])
    g = _ssm_act_fwd(y, u, p["ssm_d"])
    zz = _mm_nn("ssm_glu", g, p["ssm_w_glu"])
    return _glu_fwd(zz), (h, u, x_re, x_im, y, g, zz)


def _mix1_bwd(df, saved, p):
    h, u, x_re, x_im, y, g, zz = saved
    gr = {}
    dzz = _glu_bwd(zz, df)
    dg = _mm_nt("ssm_glu_dx", dzz, p["ssm_w_glu"])
    gr["ssm_w_glu"] = _mm_tn("ssm_glu_dw", g, dzz, BF16)
    dy, gr["ssm_d"] = _ssm_act_bwd(dg, y, u, p["ssm_d"])
    gx_re, gx_im = _ssm_in("ssm_gx", dy, p["wct_bd"])
    a_re, a_im = _ssm_scan(gx_re, gx_im, p["lam_bar_re"], -p["lam_bar_im"], True)
    du_ssm = _ssm_out("ssm_du_mm", a_re, a_im, p["wbt_bd"])
    du = _axpy(du_ssm, dy, p["ssm_d"])
    gr["ssm_w_in"] = _mm_tn("ssm_w_in_dw", h, du, BF16)
    dh = _mm_nt("ssm_w_in_dx", du, p["ssm_w_in"])
    g_lam_re, g_lam_im = _ssm_dlam(x_re, x_im, a_re, a_im)
    mb_re, mb_im = _ssm_outer("ssm_db", u, a_re, a_im)
    mc_re, mc_im = _ssm_outer("ssm_dc", dy, x_re, x_im)
    per_group = lambda m: m[:, :, :SSM_P].reshape(SSM_G, SSM_N, SSM_P)
    gr["ssm_c_re"] = per_group(mc_re)
    gr["ssm_c_im"] = -per_group(mc_im)
    dlr, dli, ddt, dbr, dbi = _ssm_param_bwd(
        g_lam_re.reshape(SSM_G, SSM_P), g_lam_im.reshape(SSM_G, SSM_P),
        per_group(mb_re).reshape(SSM_G, SSM_N * SSM_P), per_group(mb_im).reshape(SSM_G, SSM_N * SSM_P),
        p["lam_re"], p["lam_im"], p["lam_re_rep"], p["lam_im_rep"], p["log_dt"], p["b_re"], p["b_im"], p["seg"])
    gr["ssm_lam_re"], gr["ssm_lam_im"], gr["ssm_log_dt"] = dlr, dli, ddt[:, 0]
    gr["ssm_b_re"] = dbr.reshape(SSM_G, SSM_N, SSM_P).transpose(0, 2, 1)
    gr["ssm_b_im"] = dbi.reshape(SSM_G, SSM_N, SSM_P).transpose(0, 2, 1)
    return dh, gr


def _ssm_params(lam_re, lam_im, b_re, b_im, c_re, c_im, log_dt):
    wide = lambda b: b.transpose(0, 2, 1).reshape(SSM_G, SSM_N * SSM_P)
    p = {"lam_re": lam_re, "lam_im": lam_im, "log_dt": log_dt.reshape(SSM_G, 1),
         "lam_re_rep": jnp.tile(lam_re, (1, SSM_N)), "lam_im_rep": jnp.tile(lam_im, (1, SSM_N)), "b_re": wide(b_re), "b_im": wide(b_im)}
    lbr, lbi, bbr, bbi = _ssm_prep(lam_re, lam_im, p["lam_re_rep"], p["lam_im_rep"], p["log_dt"], p["b_re"], p["b_im"])
    p["lam_bar_re"], p["lam_bar_im"] = lbr.reshape(1, SSM_L), lbi.reshape(1, SSM_L)
    rows = lambda m: m.reshape(SSM_G * SSM_N, SSM_P)
    p["wb_bd"] = p["wbt_bd"] = _ssm_block_diag(rows(bbr), rows(bbi))
    p["wc_bd"] = p["wct_bd"] = _ssm_block_diag(rows(c_re), rows(-c_im))
    p["seg"] = jnp.tile(jnp.eye(SSM_P, dtype=F32), (SSM_N, 1))
    return p


RES_WEIGHT = (0.5, 1.0, 0.5)


def _local_step(x, tgt, mod, norm_pre, norm_post, weights_of, on_grads):
    def fns(i, w):
        if i % 3 != 1:
            return (lambda h: _ffn_fwd(h, *w)), (lambda df, sv: (lambda r: (r[0], r[1:]))(_ffn_bwd(df, sv, *w)))
        if i == 1:
            return (lambda h: _mix0_fwd(h, w)), (lambda df, sv: _mix0_bwd(df, sv, w))
        return (lambda h: _mix1_fwd(h, w)), (lambda df, sv: _mix1_bwd(df, sv, w))

    saved, bwd = [], []
    for i in range(6):
        l, s = divmod(i, 3)
        f, b = fns(i, weights_of(i, x))
        x, sv = _sublayer_fwd(x, f, mod[l, s], norm_pre[l, s][None], norm_post[l, s][None], RES_WEIGHT[s])
        saved.append(sv)
        bwd.append(b)
    loss_row, dx = _loss_fwd_bwd(x, tgt)
    token = jnp.zeros((8, LANE), F32)
    for i in reversed(range(6)):
        l, s = divmod(i, 3)
        mod3 = mod[l, s] + token[0:1, 0:1]
        dx, dmod, dpre, dpost, extra = _sublayer_bwd(
            dx, saved[i], bwd[i], mod3, norm_pre[l, s][None], norm_post[l, s][None], RES_WEIGHT[s])
        token = on_grads(i, extra, dmod, dpre, dpost, loss_row)
    return dx


def _pad_rows(v, rows):
    return jnp.pad(v, (0, rows * LANE - v.shape[0])).reshape(rows, LANE)


def _pack(parts):
    flat, layout, off = [], [], 0
    for a in parts:
        n = a.size
        padded = -(-n // LANE) * LANE
        flat.append(jnp.pad(a.reshape(-1).astype(F32), (0, padded - n)))
        layout.append((off, n, a.shape))
        off += padded
    return jnp.concatenate(flat), layout


def _unpack(flat, layout):
    return [flat[off:off + n].reshape(shape) for off, n, shape in layout]


SMALL_REPLICATED = ["ada_b", "pool_w", "pool_scale", "sgu_ln_g", "sgu_ln_b", "sgu_w", "sgu_b", "ssm_lam_re", "ssm_lam_im",
                    "ssm_b_re", "ssm_b_im", "ssm_c_re", "ssm_c_im", "ssm_log_dt"]
SMALL_SHARDED = ["norm_pre", "norm_post", "ssm_d"]
TRANSPOSED = ["ffn_w_in", "ab_w_in"]
WEIGHTS = ['ada_w', 'ada_b', 'norm_pre', 'norm_post', 'ffn_w_in', 'ffn_w_out', 'ab_w_in', 'pool_w', 'pool_scale', 'sgu_ln_g',
           'sgu_ln_b', 'sgu_w', 'sgu_b', 'ab_w_out', 'ssm_w_in', 'ssm_lam_re', 'ssm_lam_im', 'ssm_b_re', 'ssm_b_im', 'ssm_c_re',
           'ssm_c_im', 'ssm_d', 'ssm_log_dt', 'ssm_w_glu']


def kernel(x, c, ada_w, ada_b, norm_pre, norm_post, ffn_w_in, ffn_w_out, ab_w_in, pool_w, pool_scale, sgu_ln_g, sgu_ln_b, sgu_w, sgu_b, ab_w_out, ssm_w_in, ssm_lam_re, ssm_lam_im, ssm_b_re, ssm_b_im, ssm_c_re, ssm_c_im, ssm_d, ssm_log_dt, ssm_w_glu, loss_target, m_ada_w, m_ada_b, m_norm_pre, m_norm_post, m_ffn_w_in, m_ffn_w_out, m_ab_w_in, m_pool_w, m_pool_scale, m_sgu_ln_g, m_sgu_ln_b, m_sgu_w, m_sgu_b, m_ab_w_out, m_ssm_w_in, m_ssm_lam_re, m_ssm_lam_im, m_ssm_b_re, m_ssm_b_im, m_ssm_c_re, m_ssm_c_im, m_ssm_d, m_ssm_log_dt, m_ssm_w_glu, v_ada_w, v_ada_b, v_norm_pre, v_norm_post, v_ffn_w_in, v_ffn_w_out, v_ab_w_in, v_pool_w, v_pool_scale, v_sgu_ln_g, v_sgu_ln_b, v_sgu_w, v_sgu_b, v_ab_w_out, v_ssm_w_in, v_ssm_lam_re, v_ssm_lam_im, v_ssm_b_re, v_ssm_b_im, v_ssm_c_re, v_ssm_c_im, v_ssm_d, v_ssm_log_dt, v_ssm_w_glu):
    args = locals()
    wts = {n: args[n] for n in WEIGHTS}
    mom = {n: args["m_" + n] for n in WEIGHTS}
    var = {n: args["v_" + n] for n in WEIGHTS}
    for n in TRANSPOSED:
        for t in (wts, mom, var):
            t[n] = jnp.swapaxes(t[n], -1, -2)
    me = 4 * lax.axis_index("x") + 2 * lax.axis_index("y") + lax.axis_index("c")
    s = x.shape[1]
    nd = D // N_DEV

    small_in, small_in_layout = _pack([c, norm_pre, norm_post, ssm_d])
    small_rows = -(-small_in.shape[0] // (8 * LANE)) * 8
    (g_small,) = _all_gather("gather_small", [_pad_rows(small_in, small_rows)])
    g_small = g_small.reshape(N_DEV, -1)
    c_all, npre_g, npost_g, sd_g = [jnp.stack([_unpack(g_small[j], small_in_layout)[i] for j in range(N_DEV)]) for i in range(4)]
    c_all = c_all.reshape(N_DEV, D)
    norm_pre_full = npre_g.transpose(1, 2, 0, 3).reshape(2, 3, D)
    norm_post_full = npost_g.transpose(1, 2, 0, 3).reshape(2, 3, D)
    ssm_d_full = sd_g.transpose(1, 0, 2).reshape(1, D)

    nw = ada_w.shape[-1]
    (mod_g,) = _all_gather("gather_mod", [_mod_part(c_all, ada_w)])
    mod = lax.dynamic_index_in_dim(mod_g, me, axis=2, keepdims=False)
    mod = (mod.transpose(1, 0, 2).reshape(2, N_DEV * nw) + ada_b).reshape(2, 3, 3, D)

    w_in_t = wts["ffn_w_in"]
    shards = [[w_in_t[0, 0], ffn_w_out[0, 0]], [wts["ab_w_in"][0], ab_w_out[0]], [w_in_t[0, 1], ffn_w_out[0, 1]],
              [w_in_t[1, 0], ffn_w_out[1, 0]], [ssm_w_in[0], ssm_w_glu[0]], [w_in_t[1, 1], ffn_w_out[1, 1]]]
    gather_plan = lambda me_, peer_: [(0, None, 0, me_), (1, None, 1, me_)]
    gathers = []
    token = mod_g
    for i, pair in enumerate(shards):
        pair = [a.astype(BF16) for a in pair]
        sems, srcs_thru, lands, token = _exchange_start(
            f"gather_start_{i}", pair, [_sds((N_DEV,) + a.shape, BF16) for a in pair], gather_plan, 2, token)
        gathers.append((sems, srcs_thru, lands))
    mod = mod + token[0, 0]

    head_sum = jnp.repeat(jnp.eye(NH, LANE, dtype=F32), HD, axis=0)
    mix0 = {"pool_w": pool_w[0], "pool_scale": pool_scale, "sgu_ln_g": sgu_ln_g, "sgu_ln_b": sgu_ln_b, "sgu_w": sgu_w[0],
            "sgu_bt": jnp.pad(sgu_b[0].T, ((0, 0), (0, LANE - NH))), "head_sum": head_sum}
    mix1 = _ssm_params(ssm_lam_re[0], ssm_lam_im[0], ssm_b_re[0], ssm_b_im[0], ssm_c_re[0], ssm_c_im[0], ssm_log_dt[0])
    mix1["ssm_d"] = ssm_d_full

    def weights_of(i, x_in):
        sems, srcs_thru, lands = gathers[i]
        a, b = _exchange_wait(f"gather_wait_{i}", sems, srcs_thru, lands, gather_plan, x_in)
        if i % 3 != 1:
            return a, b
        if i == 1:
            return dict(mix0, ab_w_in=a.reshape(-1, D), ab_w_out=b.reshape(D, D))
        return dict(mix1, ssm_w_in=a.reshape(D, D), ssm_w_glu=b.transpose(1, 0, 2).reshape(D, -1))

    def shard_cols(a):
        r = a.shape[0]
        return a.reshape(r, N_DEV, -1).transpose(1, 0, 2)

    scatter_plan = lambda me_, peer_: [(0, peer_, 0, me_), (1, peer_, 1, me_)]
    scatters = [None] * 6
    last_token = [jnp.zeros((8, LANE), F32)]
    pieces, mixer, bundles = {}, {}, {}
    bundle_plan = lambda me_, peer_: [(0, None, 0, me_)]
    mix0_names = ["pool_w", "pool_scale", "sgu_ln_g", "sgu_ln_b", "sgu_w", "sgu_b"]
    mix1_names = ["ssm_lam_re", "ssm_lam_im", "ssm_b_re", "ssm_b_im", "ssm_c_re", "ssm_c_im", "ssm_log_dt", "ssm_d"]

    def start_bundle(tag, arrays):
        flat, layout = _pack(arrays)
        rows = -(-flat.shape[0] // (8 * LANE)) * 8
        sems, srcs_thru, lands, last_token[0] = _exchange_start(
            f"small_start_{tag}", [_pad_rows(flat, rows)], [_sds((N_DEV, rows, LANE))], bundle_plan, 1, last_token[0])
        bundles[tag] = (sems, srcs_thru, lands, layout)

    def on_grads(i, extra, dmod_i, dpre_i, dpost_i, loss_row):
        pieces[i] = (dmod_i, dpre_i, dpost_i)
        if i == 4:
            mixer.update({n: extra[n] for n in mix1_names})
        if i == 1:
            mixer.update({n: extra[n] for n in mix0_names})
            rest = range(1, 6)
            start_bundle("a", [jnp.stack([pieces[j][0] for j in rest])] + [jnp.concatenate([pieces[j][k] for j in rest]) for k in (1, 2)]
                         + [mixer[n] for n in mix0_names + mix1_names])
        if i == 0:
            start_bundle("b", [dmod_i, dpre_i, dpost_i, loss_row])
        if i % 3 != 1:
            parts = list(extra)
        elif i == 1:
            parts = [extra["ab_w_in"].reshape(N_DEV, -1, D), extra["ab_w_out"].reshape(N_DEV, nd, D)]
        else:
            parts = [extra["ssm_w_in"].reshape(N_DEV, nd, D), shard_cols(extra["ssm_w_glu"])]
        sems, srcs_thru, lands, last_token[0] = _exchange_start(
            f"scatter_start_{i}", parts, [_sds(a.shape, BF16) for a in parts], scatter_plan, 2, last_token[0])
        scatters[i] = (sems, srcs_thru, lands)
        return last_token[0]

    grad_x = _local_step(x[0], loss_target[0], mod, norm_pre_full, norm_post_full, weights_of, on_grads)

    after = grad_x
    gathered, sums = {}, {}
    for tag in ("a", "b"):
        sems, srcs_thru, lands, layout = bundles[tag]
        (g_parts,) = _exchange_wait(f"small_wait_{tag}", sems, srcs_thru, lands, bundle_plan, after)
        total = _sum_parts(g_parts)
        after = total
        off, n, shape = layout[0]
        gathered[tag] = g_parts.reshape(N_DEV, -1)[:, off:off + n].reshape((N_DEV,) + shape)
        sums[tag] = _unpack(total.reshape(-1), layout)
    dmod_a, dpre_a, dpost_a = sums["a"][:3]
    dmod_b, dpre_b, dpost_b, loss_sum = sums["b"]
    small = dict(zip(mix0_names + mix1_names, sums["a"][3:]))
    small["ada_b"] = jnp.concatenate([dmod_b[None], dmod_a])
    small["norm_pre"] = jnp.concatenate([dpre_b, dpre_a]).reshape(2, 3, D)
    small["norm_post"] = jnp.concatenate([dpost_b, dpost_a]).reshape(2, 3, D)
    loss = loss_sum[0, 0]

    out_g, out_d, out_m, out_v = {}, {}, {}, {}

    def adam_flat(name, gnames, grads):
        gf, lay = _pack(grads)
        r = -(-gf.shape[0] // (8 * LANE)) * 8
        packed = [_pad_rows(_pack([t[n] for n in gnames])[0], r) for t in (wts, mom, var)]
        res = _adamw(name, _pad_rows(gf, r)[None], *packed)
        for o, arr in zip((out_g, out_d, out_m, out_v), res):
            o.update(dict(zip(gnames, _unpack(arr.reshape(-1), lay))))
        return res[0]

    adam_flat("adamw_replicated", SMALL_REPLICATED, [small[n].reshape(wts[n].shape) for n in SMALL_REPLICATED])
    sliced = [lax.dynamic_slice_in_dim(small[n], me * nd, nd, axis=small[n].ndim - 1).reshape(wts[n].shape) for n in SMALL_SHARDED]
    after = adam_flat("adamw_sliced", SMALL_SHARDED, sliced)

    big_out = {}

    def adam_big(name, recv, n, slot=0):
        c_ = wts[n].shape[-1]
        big_out[n] = _adamw(name, recv.reshape(recv.shape[0], -1, c_), *[t[n].reshape(-1, c_) for t in (wts, mom, var)],
                            slot=slot, prev=big_out.get(n))
        return big_out[n][0]

    dmod_all = jnp.concatenate([gathered["b"][:, None], gathered["a"]], axis=1).reshape(N_DEV, 2, N_DEV, nw)
    dmod_mine = lax.dynamic_index_in_dim(dmod_all, me, axis=2, keepdims=False).transpose(1, 0, 2)
    g_ada_w = _ada_w_grad(c_all.T, dmod_mine)
    after = after[0:1, 0:1] + adam_big("adamw_ada_w", g_ada_w[None], "ada_w")[0:1, 0:1]

    big_names = [("ffn_w_in", "ffn_w_out"), ("ab_w_in", "ab_w_out"), ("ffn_w_in", "ffn_w_out"),
                 ("ffn_w_in", "ffn_w_out"), ("ssm_w_in", "ssm_w_glu"), ("ffn_w_in", "ffn_w_out")]
    ffn_slot = {0: 0, 2: 1, 3: 2, 5: 3}
    for i in reversed(range(6)):
        sems, srcs_thru, lands = scatters[i]
        recv = _exchange_wait(f"scatter_wait_{i}", sems, srcs_thru, lands, scatter_plan, after)
        for n, r in zip(big_names[i], recv):
            after = adam_big(f"adamw_{n}_{i}", r, n, ffn_slot.get(i, 0))
    for n, res in big_out.items():
        for o, arr in zip((out_g, out_d, out_m, out_v), res):
            o[n] = arr.reshape(wts[n].shape)
            if n in TRANSPOSED:
                o[n] = jnp.swapaxes(o[n], -1, -2)

    return (loss, grad_x[None], *[out_g[n] for n in WEIGHTS], *[out_d[n] for n in WEIGHTS],
            *[out_m[n] for n in WEIGHTS], *[out_v[n] for n in WEIGHTS])
```

```python
import functools
import math

import jax
import jax.numpy as jnp
from jax import lax
from jax.experimental import pallas as pl
from jax.experimental.pallas import tpu as pltpu

F32 = jnp.float32
BF16 = jnp.bfloat16
MESH = pl.DeviceIdType.MESH
HIGHEST = lax.Precision.HIGHEST

N_DEV = 8
D = 1024
D_FF = 2816
FSH = 2 * D_FF // N_DEV
EPS = 1e-6
POOL_WINDOWS = (2, 4, 8, 16)
HD = 128
NH = 4
SSM_G, SSM_P, SSM_N = 64, 64, 16
SSM_GB = 16
SSM_NB = SSM_G // SSM_GB
SSM_L = SSM_G * SSM_P
LR, B1, B2, ADAM_EPS, WD, STEP = 0.001, 0.9, 0.999, 1e-08, 0.01, 10
GELU_C = math.sqrt(2.0 / math.pi)
VMEM_LIMIT_BYTES = 48 * 1024 * 1024
LANE = 128


def _pc(body, name, grid, in_specs, out_specs, out_shape, scratch=()):
    return pl.pallas_call(
        body, name=name, grid=grid, in_specs=in_specs, out_specs=out_specs, out_shape=out_shape,
        scratch_shapes=list(scratch),
        compiler_params=pltpu.CompilerParams(dimension_semantics=("arbitrary",) * len(grid),
                                             vmem_limit_bytes=VMEM_LIMIT_BYTES))


def _sds(shape, dtype=F32):
    return jax.ShapeDtypeStruct(tuple(shape), dtype)


def _bf(v):
    return v if v.dtype == BF16 else v.astype(BF16)


def _row_spec(ts, width, col=0):
    return pl.BlockSpec((ts, width), lambda t, _c=col: (t, _c))


def _vec_spec(width, col=0):
    return pl.BlockSpec((1, width), lambda t, _c=col: (0, _c))


def _mm(name, a, b, contract, grid, a_spec, b_spec, o_spec, out_shape, acc_axis=None, after=None):
    dn = (contract, ((), ()))

    def body(a_ref, b_ref, *rest):
        o_ref = rest[-1]
        r = lax.dot_general(_bf(a_ref[...]), _bf(b_ref[...]), dn, preferred_element_type=F32)
        if acc_axis is None:
            o_ref[...] = r.astype(o_ref.dtype)
        else:
            k = pl.program_id(acc_axis)

            @pl.when(k == 0)
            def _():
                o_ref[...] = r

            @pl.when(k > 0)
            def _():
                o_ref[...] += r

    if after is None:
        return _pc(body, name, grid, [a_spec, b_spec], o_spec, out_shape)(a, b)
    return _pc(body, name, grid, [a_spec, b_spec, pl.BlockSpec(memory_space=pl.ANY)], o_spec, out_shape)(a, b, after)


def _tile(s):
    return min(s, 1024)


def _div_tile(n, cap=1024):
    t = min(n, cap) // LANE * LANE
    while n % t:
        t -= LANE
    return t


def _mm_nn(name, a, b, out_dtype=F32):
    s, k = a.shape
    n = b.shape[1]
    ts, tn = _tile(s), _div_tile(n)
    return _mm(name, a, b, ((1,), (0,)), (n // tn, s // ts),
               pl.BlockSpec((ts, k), lambda j, t: (t, 0)), pl.BlockSpec((k, tn), lambda j, t: (0, j)),
               pl.BlockSpec((ts, tn), lambda j, t: (t, j)), _sds((s, n), out_dtype))


def _mm_nt(name, a, b, out_dtype=F32):
    s, n = a.shape
    k = b.shape[0]
    ts, tk = _tile(s), _div_tile(k)
    return _mm(name, a, b, ((1,), (1,)), (k // tk, s // ts),
               pl.BlockSpec((ts, n), lambda j, t: (t, 0)), pl.BlockSpec((tk, n), lambda j, t: (j, 0)),
               pl.BlockSpec((ts, tk), lambda j, t: (t, j)), _sds((s, k), out_dtype))


def _mm_tn(name, a, b, out_dtype=F32, tm=512, tn=512):
    s, m = a.shape
    n = b.shape[1]
    tm, tn = min(m, tm), min(n, tn)
    return _mm(name, a, b, ((0,), (0,)), (m // tm, n // tn),
               pl.BlockSpec((s, tm), lambda i, j: (0, i)), pl.BlockSpec((s, tn), lambda i, j: (0, j)),
               pl.BlockSpec((tm, tn), lambda i, j: (i, j)), _sds((m, n), out_dtype))


def _rstd(v):
    return lax.rsqrt(jnp.mean(v * v, axis=-1, keepdims=True) + EPS)


def _prenorm_fwd(x, g, scale, shift):
    s = x.shape[0]
    ts = min(s, 512)

    def body(x_ref, g_ref, sc_ref, sh_ref, h_ref):
        xv = x_ref[...]
        h_ref[...] = ((xv * _rstd(xv) * g_ref[...]) * (1.0 + sc_ref[...]) + sh_ref[...]).astype(BF16)

    return _pc(body, "prenorm_fwd", (s // ts,), [_row_spec(ts, D)] + [_vec_spec(D)] * 3, _row_spec(ts, D),
               _sds((s, D), BF16))(x, g, scale, shift)


def _postnorm_fwd(x, f, g, gate, rw):
    s = x.shape[0]
    ts = min(s, 512)

    def body(x_ref, f_ref, g_ref, gt_ref, o_ref):
        fv = f_ref[...]
        o_ref[...] = x_ref[...] + (rw * gt_ref[...]) * (fv * _rstd(fv) * g_ref[...])

    return _pc(body, "postnorm_fwd", (s // ts,), [_row_spec(ts, D)] * 2 + [_vec_spec(D)] * 2, _row_spec(ts, D),
               _sds((s, D)))(x, f, g, gate)


def _acc(ref, first, v):
    @pl.when(first)
    def _():
        ref[...] = v

    @pl.when(jnp.logical_not(first))
    def _():
        ref[...] += v


def _colsum(v):
    return jnp.sum(v, axis=0, keepdims=True)


def _postnorm_bwd(dout, f, g, gate, rw):
    s = dout.shape[0]
    ts = min(s, 512)

    def body(do_ref, f_ref, g_ref, gt_ref, df_ref, dgate_ref, dg_ref):
        first = pl.program_id(0) == 0
        do, fv, gv = do_ref[...], f_ref[...], g_ref[...]
        r = _rstd(fv)
        fn = fv * r
        _acc(dgate_ref, first, rw * _colsum(do * (fn * gv)))
        dy = (rw * gt_ref[...]) * do
        _acc(dg_ref, first, _colsum(dy * fn))
        dfn = dy * gv
        df_ref[...] = (r * (dfn - fn * jnp.mean(dfn * fn, axis=-1, keepdims=True))).astype(BF16)

    return _pc(body, "postnorm_bwd", (s // ts,), [_row_spec(ts, D)] * 2 + [_vec_spec(D)] * 2,
               [_row_spec(ts, D), _vec_spec(D), _vec_spec(D)],
               [_sds((s, D), BF16), _sds((1, D)), _sds((1, D))])(dout, f, g, gate)


def _prenorm_bwd(dout, dh, x, g, scale):
    s = dout.shape[0]
    ts = min(s, 512)

    def body(do_ref, dh_ref, x_ref, g_ref, sc_ref, dx_ref, dsh_ref, dsc_ref, dg_ref):
        first = pl.program_id(0) == 0
        dhv, xv, gv = dh_ref[...], x_ref[...], g_ref[...]
        r = _rstd(xv)
        xn = xv * r
        _acc(dsh_ref, first, _colsum(dhv))
        _acc(dsc_ref, first, _colsum(dhv * (xn * gv)))
        dhp = dhv * (1.0 + sc_ref[...])
        _acc(dg_ref, first, _colsum(dhp * xn))
        dxn = dhp * gv
        dx_ref[...] = do_ref[...] + r * (dxn - xn * jnp.mean(dxn * xn, axis=-1, keepdims=True))

    return _pc(body, "prenorm_bwd", (s // ts,), [_row_spec(ts, D)] * 3 + [_vec_spec(D)] * 2,
               [_row_spec(ts, D)] + [_vec_spec(D)] * 3,
               [_sds((s, D))] + [_sds((1, D))] * 3)(dout, dh, x, g, scale)


def _loss_fwd_bwd(y, tgt):
    s = y.shape[0]
    ts = min(s, 512)
    nt = s // ts

    def body(y_ref, t_ref, loss_ref, dy_ref, acc_ref):
        t = pl.program_id(0)
        e = y_ref[...] - t_ref[...]
        dy_ref[...] = e * (1.0 / D)
        _acc(acc_ref, t == 0, _colsum(e * e))

        @pl.when(t == nt - 1)
        def _():
            loss_ref[...] = jnp.full((1, LANE), 0.5 / D, F32) * jnp.sum(acc_ref[...])

    return _pc(body, "loss", (nt,), [_row_spec(ts, D)] * 2,
               [pl.BlockSpec((1, LANE), lambda t: (0, 0)), _row_spec(ts, D)],
               [_sds((1, LANE)), _sds((s, D))], scratch=[pltpu.VMEM((1, D), F32)])(y, tgt)


def _sigmoid(v):
    return 1.0 / (1.0 + jnp.exp(-v))


def _swiglu_fwd(z):
    _, s, _ = z.shape
    ts = min(s, 512)
    z4 = z.reshape(2, 4, s, FSH)

    def body(z_ref, o_ref):
        a, b = z_ref[0], z_ref[1]
        o_ref[...] = (a * _sigmoid(a) * b).astype(BF16)

    return _pc(body, "swiglu_fwd", (4, s // ts), [pl.BlockSpec((2, None, ts, FSH), lambda k, t: (0, k, t, 0))],
               pl.BlockSpec((None, ts, FSH), lambda k, t: (k, t, 0)), _sds((4, s, FSH), BF16))(z4)


def _swiglu_bwd(z, dact, after):
    _, s, _ = z.shape
    ts = min(s, 512)
    z4 = z.reshape(2, 4, s, FSH)

    def body(z_ref, d_ref, after_ref, o_ref):
        a, b, d = z_ref[0], z_ref[1], d_ref[...]
        sg = _sigmoid(a)
        o_ref[0] = (d * b * (sg * (1.0 + a * (1.0 - sg)))).astype(BF16)
        o_ref[1] = (d * (a * sg)).astype(BF16)

    spec = pl.BlockSpec((2, None, ts, FSH), lambda k, t: (0, k, t, 0))
    out = _pc(body, "swiglu_bwd", (4, s // ts),
              [spec, pl.BlockSpec((None, ts, FSH), lambda k, t: (k, t, 0)), pl.BlockSpec(memory_space=pl.ANY)],
              spec, _sds((2, 4, s, FSH), BF16))(z4, dact, after)
    return out.reshape(8, s, FSH)


def _ffn_fwd(h, win, wout):
    s = h.shape[0]
    ts = _tile(s)
    wout = wout.reshape(4, FSH, D)
    z = _mm("ffn_in", h, win, ((1,), (1,)), (N_DEV, s // ts),
            pl.BlockSpec((ts, D), lambda j, t: (t, 0)), pl.BlockSpec((None, FSH, D), lambda j, t: (j, 0, 0)),
            pl.BlockSpec((None, ts, FSH), lambda j, t: (j, t, 0)), _sds((N_DEV, s, FSH)))
    act = _swiglu_fwd(z)
    f = _mm("ffn_out", act, wout, ((1,), (0,)), (s // ts, 4),
            pl.BlockSpec((None, ts, FSH), lambda t, k: (k, t, 0)), pl.BlockSpec((None, FSH, D), lambda t, k: (k, 0, 0)),
            pl.BlockSpec((ts, D), lambda t, k: (t, 0)), _sds((s, D)), acc_axis=1)
    return f, (h, z, act)


def _ffn_bwd(df, saved, win, wout, send):
    h, z, act = saved
    s = h.shape[0]
    ts = _tile(s)
    wout = wout.reshape(4, FSH, D)
    dact = _mm("ffn_out_dx", df, wout, ((1,), (1,)), (4, s // ts),
               pl.BlockSpec((ts, D), lambda k, t: (t, 0)), pl.BlockSpec((None, FSH, D), lambda k, t: (k, 0, 0)),
               pl.BlockSpec((None, ts, FSH), lambda k, t: (k, t, 0)), _sds((4, s, FSH)))
    dwout = _mm("ffn_out_dw", act, df, ((0,), (0,)), (4, 2),
                pl.BlockSpec((None, s, FSH), lambda k, j: (k, 0, 0)), pl.BlockSpec((s, D // 2), lambda k, j: (0, j)),
                pl.BlockSpec((None, FSH, D // 2), lambda k, j: (k, 0, j)), _sds((4, FSH, D), BF16))
    dz = _swiglu_bwd(z, dact, send("w_out", dwout.reshape(N_DEV, D_FF // N_DEV, D)))
    dwin = _mm("ffn_in_dw", dz, h, ((0,), (0,)), (N_DEV, 2),
               pl.BlockSpec((None, s, FSH), lambda j, i: (j, 0, 0)), pl.BlockSpec((s, D // 2), lambda j, i: (0, i)),
               pl.BlockSpec((None, FSH, D // 2), lambda j, i: (j, 0, i)), _sds((N_DEV, FSH, D), BF16))
    return _mm("ffn_in_dx", dz, win, ((1,), (0,)), (s // ts, N_DEV),
               pl.BlockSpec((None, ts, FSH), lambda t, j: (j, t, 0)), pl.BlockSpec((None, FSH, D), lambda t, j: (j, 0, 0)),
               pl.BlockSpec((ts, D), lambda t, j: (t, 0)), _sds((s, D)), acc_axis=1, after=send("w_in", dwin))


def _shift_rows(v, k, row, s, back):
    if back:
        return jnp.where(row < s - k, pltpu.roll(v, s - k, 0), 0.0)
    return jnp.where(row >= k, pltpu.roll(v, k, 0), 0.0)


def _window_sum(v, w, row, s, back):
    k = 1
    while k < w:
        v = v + _shift_rows(v, k, row, s, back)
        k *= 2
    return v


def _pool_fwd(z, pool_w, pool_scale):
    s = z.shape[0]

    def body(z_ref, w_ref, sc_ref, y_ref, d_ref):
        row = lax.broadcasted_iota(jnp.int32, (s, HD), 0)
        for g, w in enumerate(POOL_WINDOWS):
            sl = slice(g * HD, (g + 1) * HD)
            a = z_ref[:, sl]
            cnt = jnp.minimum(row + 1, w).astype(F32)
            d = (_window_sum(a, w, row, s, False) / cnt - a).astype(BF16)
            d_ref[:, sl] = d
            y = jnp.dot(d, _bf(w_ref[g]), preferred_element_type=F32)
            y_ref[:, sl] = (y * sc_ref[:, sl]).astype(BF16)

    return _pc(body, "pool_fwd", (1,),
               [pl.BlockSpec((s, NH * HD), lambda i: (0, 0)), pl.BlockSpec((NH, HD, HD), lambda i: (0, 0, 0)),
                pl.BlockSpec((1, NH * HD), lambda i: (0, 0))],
               [pl.BlockSpec((s, NH * HD), lambda i: (0, 0))] * 2,
               [_sds((s, NH * HD), BF16)] * 2)(z, pool_w, pool_scale)


def _pool_bwd(dy, d, pool_w, pool_scale):
    s = dy.shape[0]

    def body(dy_ref, d_ref, w_ref, sc_ref, dz_ref, dw_ref, dsc_ref):
        row = lax.broadcasted_iota(jnp.int32, (s, HD), 0)
        for g, w in enumerate(POOL_WINDOWS):
            sl = slice(g * HD, (g + 1) * HD)
            dyg, dg, wg = dy_ref[:, sl], d_ref[:, sl], _bf(w_ref[g])
            yraw = jnp.dot(dg, wg, preferred_element_type=F32)
            dsc_ref[:, sl] = _colsum(dyg * yraw)
            dyr = _bf(dyg * sc_ref[:, sl])
            dw_ref[g] = lax.dot_general(dg, dyr, (((0,), (0,)), ((), ())), preferred_element_type=F32)
            dd = lax.dot_general(dyr, wg, (((1,), (1,)), ((), ())), preferred_element_type=F32)
            cnt = jnp.minimum(row + 1, w).astype(F32)
            dz_ref[:, sl] = (_window_sum(dd / cnt, w, row, s, True) - dd).astype(BF16)

    return _pc(body, "pool_bwd", (1,),
               [pl.BlockSpec((s, NH * HD), lambda i: (0, 0)), pl.BlockSpec((s, NH * HD), lambda i: (0, 0)),
                pl.BlockSpec((NH, HD, HD), lambda i: (0, 0, 0)), pl.BlockSpec((1, NH * HD), lambda i: (0, 0))],
               [pl.BlockSpec((s, NH * HD), lambda i: (0, 0)), pl.BlockSpec((NH, HD, HD), lambda i: (0, 0, 0)),
                pl.BlockSpec((1, NH * HD), lambda i: (0, 0))],
               [_sds((s, NH * HD), BF16), _sds((NH, HD, HD)), _sds((1, NH * HD))])(dy, d, pool_w, pool_scale)


def _gelu(v):
    return 0.5 * v * (1.0 + jnp.tanh(GELU_C * (v + 0.044715 * (v * v * v))))


def _gelu_grad(v):
    t = jnp.tanh(GELU_C * (v + 0.044715 * (v * v * v)))
    return 0.5 * (1.0 + t) + 0.5 * v * (1.0 - t * t) * (GELU_C * (1.0 + 3.0 * 0.044715 * (v * v)))


def _causal_mask():
    return lax.broadcasted_iota(jnp.int32, (HD, HD), 0) >= lax.broadcasted_iota(jnp.int32, (HD, HD), 1)


def _sgu_specs():
    w = NH * HD
    return [pl.BlockSpec((HD, w), lambda c: (c, 1)), pl.BlockSpec((HD, w), lambda c: (c, 2)),
            pl.BlockSpec((1, w), lambda c: (0, 0)), pl.BlockSpec((1, w), lambda c: (0, 0)),
            pl.BlockSpec((NH, HD, HD), lambda c: (0, 0, 0)), pl.BlockSpec((HD, LANE), lambda c: (0, 0))]


def _sgu_head(v, lng_ref, lnb_ref, w_ref, h):
    sl = slice(h * HD, (h + 1) * HD)
    vh = v[:, sl]
    xc = vh - jnp.mean(vh, axis=-1, keepdims=True)
    rs = lax.rsqrt(jnp.mean(xc * xc, axis=-1, keepdims=True) + EPS)
    vhat = xc * rs
    vn = _bf(vhat * lng_ref[:, sl] + lnb_ref[:, sl])
    wc = _bf(jnp.where(_causal_mask(), w_ref[h], 0.0))
    return sl, rs, vhat, vn, wc


def _sgu_fwd(z, ln_g, ln_b, sgu_w, sgu_bt):
    s = z.shape[0]

    def body(zu_ref, zv_ref, lng_ref, lnb_ref, w_ref, bt_ref, y_ref):
        u, v = _gelu(zu_ref[...]), _gelu(zv_ref[...])
        for h in range(NH):
            sl, _, _, vn, wc = _sgu_head(v, lng_ref, lnb_ref, w_ref, h)
            sp = jnp.dot(wc, vn, preferred_element_type=F32) + bt_ref[:, h:h + 1]
            y_ref[:, sl] = (u[:, sl] * sp).astype(BF16)

    return _pc(body, "sgu_fwd", (s // HD,), _sgu_specs(), pl.BlockSpec((HD, NH * HD), lambda c: (c, 0)),
               _sds((s, NH * HD), BF16))(z, z, ln_g, ln_b, sgu_w, sgu_bt)


def _sgu_bwd(z, dy, ln_g, ln_b, sgu_w, sgu_bt, head_sum):
    s = z.shape[0]
    w = NH * HD
    nc = s // HD

    def body(zu_ref, zv_ref, lng_ref, lnb_ref, w_ref, bt_ref, dy_ref, hs_ref,
             dzu_ref, dzv_ref, dlng_ref, dlnb_ref, dw_ref, dbt_ref, dsacc_ref):
        c = pl.program_id(0)
        first = c == 0
        zu, zv = zu_ref[...], zv_ref[...]
        u, v = _gelu(zu), _gelu(zv)
        dyv = dy_ref[...]
        gu, gv = _gelu_grad(zu), _gelu_grad(zv)
        ds = dyv * u
        _acc(dsacc_ref, first, ds)
        for h in range(NH):
            sl, rs, vhat, vn, wc = _sgu_head(v, lng_ref, lnb_ref, w_ref, h)
            sp = jnp.dot(wc, vn, preferred_element_type=F32) + bt_ref[:, h:h + 1]
            dzu_ref[:, sl] = (dyv[:, sl] * sp * gu[:, sl]).astype(BF16)
            dsh = _bf(ds[:, sl])
            dwh = lax.dot_general(dsh, vn, (((1,), (1,)), ((), ())), preferred_element_type=F32)
            dwh = jnp.where(_causal_mask(), dwh, 0.0)

            @pl.when(first)
            def _():
                dw_ref[h] = dwh

            @pl.when(jnp.logical_not(first))
            def _():
                dw_ref[h] += dwh

            dvn = lax.dot_general(wc, dsh, (((0,), (0,)), ((), ())), preferred_element_type=F32)
            g_col = _colsum(dvn * vhat)
            b_col = _colsum(dvn)

            @pl.when(first)
            def _():
                dlng_ref[:, sl] = g_col
                dlnb_ref[:, sl] = b_col

            @pl.when(jnp.logical_not(first))
            def _():
                dlng_ref[:, sl] += g_col
                dlnb_ref[:, sl] += b_col

            dvh = dvn * lng_ref[:, sl]
            dv = rs * (dvh - jnp.mean(dvh, axis=-1, keepdims=True) - vhat * jnp.mean(dvh * vhat, axis=-1, keepdims=True))
            dzv_ref[:, sl] = (dv * gv[:, sl]).astype(BF16)

        @pl.when(c == nc - 1)
        def _():
            dbt_ref[...] = jnp.dot(dsacc_ref[...], hs_ref[...], preferred_element_type=F32, precision=HIGHEST)

    outs = _pc(body, "sgu_bwd", (nc,),
               _sgu_specs() + [pl.BlockSpec((HD, w), lambda c: (c, 1)), pl.BlockSpec((w, LANE), lambda c: (0, 0))],
               [pl.BlockSpec((HD, w), lambda c: (c, 0))] * 2 + [pl.BlockSpec((1, w), lambda c: (0, 0))] * 2
               + [pl.BlockSpec((NH, HD, HD), lambda c: (0, 0, 0)), pl.BlockSpec((HD, LANE), lambda c: (0, 0))],
               [_sds((s, w), BF16)] * 2 + [_sds((1, w))] * 2 + [_sds((NH, HD, HD)), _sds((HD, LANE))],
               scratch=[pltpu.VMEM((HD, w), F32)])(z, z, ln_g, ln_b, sgu_w, sgu_bt, dy, head_sum)
    return outs


def _cmul(ar, ai, br, bi):
    return ar * br - ai * bi, ar * bi + ai * br


def _ssm_prep(lam_re, lam_im, lam_re_rep, lam_im_rep, log_dt, b_re, b_im):
    def disc(lr, li, dt):
        mag = jnp.exp(lr * dt)
        return mag * jnp.cos(li * dt), mag * jnp.sin(li * dt)

    def body(lr_ref, li_ref, lrr_ref, lir_ref, ldt_ref, br_ref, bi_ref, or_ref, oi_ref, bbr_ref, bbi_ref):
        dt = jnp.exp(ldt_ref[...])
        or_ref[...], oi_ref[...] = disc(lr_ref[...], li_ref[...], dt)
        lr, li = lrr_ref[...], lir_ref[...]
        er, ei = disc(lr, li, dt)
        den = lr * lr + li * li
        kr = ((er - 1.0) * lr + ei * li) / den
        ki = (ei * lr - (er - 1.0) * li) / den
        bbr_ref[...], bbi_ref[...] = _cmul(kr, ki, br_ref[...], bi_ref[...])

    small = pl.BlockSpec((SSM_G, SSM_P), lambda i: (0, 0))
    wide = pl.BlockSpec((SSM_G, SSM_P * SSM_N), lambda i: (0, 0))
    col = pl.BlockSpec((SSM_G, 1), lambda i: (0, 0))
    return _pc(body, "ssm_prep", (1,), [small, small, wide, wide, col, wide, wide], [small, small, wide, wide],
               [_sds((SSM_G, SSM_P))] * 2 + [_sds((SSM_G, SSM_P * SSM_N))] * 2)(
        lam_re, lam_im, lam_re_rep, lam_im_rep, log_dt, b_re, b_im)


def _ssm_param_bwd(g_lam_re, g_lam_im, g_bb_re, g_bb_im, lam_re, lam_im, lam_re_rep, lam_im_rep, log_dt, b_re, b_im, seg):
    def body(glr_ref, gli_ref, gbr_ref, gbi_ref, lr_ref, li_ref, lrr_ref, lir_ref, ldt_ref, br_ref, bi_ref, seg_ref,
             dlr_ref, dli_ref, ddt_ref, dbr_ref, dbi_ref):
        dt = jnp.exp(ldt_ref[...])
        lr, li = lrr_ref[...], lir_ref[...]
        mag = jnp.exp(lr * dt)
        er, ei = mag * jnp.cos(li * dt), mag * jnp.sin(li * dt)
        den = lr * lr + li * li
        kr = ((er - 1.0) * lr + ei * li) / den
        ki = (ei * lr - (er - 1.0) * li) / den
        gbr, gbi = gbr_ref[...], gbi_ref[...]
        dbr_ref[...], dbi_ref[...] = _cmul(kr, -ki, gbr, gbi)
        tr, ti = _cmul(br_ref[...], -bi_ref[...], gbr, gbi)
        gkr = jnp.dot(tr, seg_ref[...], preferred_element_type=F32, precision=HIGHEST)
        gki = jnp.dot(ti, seg_ref[...], preferred_element_type=F32, precision=HIGHEST)
        lr, li = lr_ref[...], li_ref[...]
        mag = jnp.exp(lr * dt)
        er, ei = mag * jnp.cos(li * dt), mag * jnp.sin(li * dt)
        den = lr * lr + li * li
        ir, ii = lr / den, -li / den
        kr, ki = _cmul(er - 1.0, ei, ir, ii)
        ar, ai = _cmul(ir, -ii, gkr, gki)
        glr, gli = glr_ref[...] + ar, gli_ref[...] + ai
        qr, qi = _cmul(kr, ki, ir, ii)
        g1r, g1i = _cmul(-qr, qi, gkr, gki)
        g2r, g2i = _cmul(dt * er, -dt * ei, glr, gli)
        dlr_ref[...] = g1r + g2r
        dli_ref[...] = g1i + g2i
        wr, wi = _cmul(lr, li, er, ei)
        g_dt = jnp.sum(wr * glr + wi * gli, axis=-1, keepdims=True)
        ddt_ref[...] = jnp.broadcast_to(dt * g_dt, (SSM_G, LANE))

    small = pl.BlockSpec((SSM_G, SSM_P), lambda i: (0, 0))
    wide = pl.BlockSpec((SSM_G, SSM_P * SSM_N), lambda i: (0, 0))
    col = pl.BlockSpec((SSM_G, 1), lambda i: (0, 0))
    segs = pl.BlockSpec((SSM_P * SSM_N, SSM_P), lambda i: (0, 0))
    return _pc(body, "ssm_param_bwd", (1,), [small, small, wide, wide, small, small, wide, wide, col, wide, wide, segs],
               [small, small, pl.BlockSpec((SSM_G, LANE), lambda i: (0, 0)), wide, wide],
               [_sds((SSM_G, SSM_P))] * 2 + [_sds((SSM_G, LANE))] + [_sds((SSM_G, SSM_P * SSM_N))] * 2)(
        g_lam_re, g_lam_im, g_bb_re, g_bb_im, lam_re, lam_im, lam_re_rep, lam_im_rep, log_dt, b_re, b_im, seg)


SCAN_LANES = 256
SCAN_ROWS = 8


def _ssm_scan(b_re, b_im, lam_re, lam_im, reverse):
    s = b_re.shape[0]
    nt = s // SCAN_ROWS
    ln, rows = SCAN_LANES, SCAN_ROWS

    def body(lr_ref, li_ref, br_ref, bi_ref, or_ref, oi_ref):
        l1 = (lr_ref[...], li_ref[...])
        pw = [l1]
        for _ in range(rows - 1):
            pw.append(_cmul(*pw[-1], *l1))
        row = lax.broadcasted_iota(jnp.int32, (rows, ln), 0)
        expo = (rows - row) if reverse else (row + 1)
        pr = jnp.zeros((rows, ln), F32)
        pi = jnp.zeros((rows, ln), F32)
        for e in range(1, rows + 1):
            pr = jnp.where(expo == e, pw[e - 1][0], pr)
            pi = jnp.where(expo == e, pw[e - 1][1], pi)
        lk = {k: (jnp.broadcast_to(pw[k - 1][0], (rows, ln)), jnp.broadcast_to(pw[k - 1][1], (rows, ln))) for k in (1, 2, 4)}

        def step(i, carry):
            cr, ci = carry
            t = (nt - 1 - i) if reverse else i
            r0 = pl.multiple_of(t * rows, rows)
            xr, xi = br_ref[pl.ds(r0, rows), :], bi_ref[pl.ds(r0, rows), :]
            for k in (1, 2, 4):
                sr = _shift_rows(xr, k, row, rows, reverse)
                si = _shift_rows(xi, k, row, rows, reverse)
                ar, ai = _cmul(lk[k][0], lk[k][1], sr, si)
                xr, xi = xr + ar, xi + ai
            ar, ai = _cmul(pr, pi, cr, ci)
            xr, xi = xr + ar, xi + ai
            or_ref[pl.ds(r0, rows), :] = xr
            oi_ref[pl.ds(r0, rows), :] = xi
            if reverse:
                return xr[0:1], xi[0:1]
            return xr[rows - 1:rows], xi[rows - 1:rows]

        zero = jnp.zeros((1, ln), F32)
        lax.fori_loop(0, nt, step, (zero, zero))

    vec = pl.BlockSpec((1, ln), lambda j: (0, j))
    blk = pl.BlockSpec((s, ln), lambda j: (0, j))
    return _pc(body, "ssm_scan_bwd" if reverse else "ssm_scan_fwd", (SSM_L // ln,), [vec, vec, blk, blk], [blk, blk],
               [_sds((s, SSM_L))] * 2)(lam_re, lam_im, b_re, b_im)


def _ssm_in(name, v, w_bd):
    s = v.shape[0]
    ts = _tile(s)
    half = SSM_GB * SSM_P

    def body(v_ref, w_ref, or_ref, oi_ref):
        r = jnp.dot(_bf(v_ref[...]), w_ref[...], preferred_element_type=F32)
        or_ref[...] = r[:, :half]
        oi_ref[...] = r[:, half:]

    out = pl.BlockSpec((ts, half), lambda q, t: (t, q))
    return _pc(body, name, (SSM_NB, s // ts),
               [pl.BlockSpec((ts, SSM_GB * SSM_N), lambda q, t: (t, q)), pl.BlockSpec((None, SSM_GB * SSM_N, 2 * half), lambda q, t: (q, 0, 0))],
               [out, out], [_sds((s, SSM_L))] * 2)(v, w_bd)


def _ssm_out(name, x_re, x_im, w_bd):
    s = x_re.shape[0]
    ts = _tile(s)
    half = SSM_GB * SSM_P
    nt = (((1,), (1,)), ((), ()))

    def body(xr_ref, xi_ref, w_ref, o_ref):
        w = w_ref[...]
        o_ref[...] = (lax.dot_general(_bf(xr_ref[...]), w[:, :half], nt, preferred_element_type=F32)
                      + lax.dot_general(_bf(xi_ref[...]), w[:, half:], nt, preferred_element_type=F32))

    xin = pl.BlockSpec((ts, half), lambda q, t: (t, q))
    return _pc(body, name, (SSM_NB, s // ts),
               [xin, xin, pl.BlockSpec((None, SSM_GB * SSM_N, 2 * half), lambda q, t: (q, 0, 0))],
               pl.BlockSpec((ts, SSM_GB * SSM_N), lambda q, t: (t, q)), _sds((s, SSM_G * SSM_N)))(x_re, x_im, w_bd)


def _ssm_outer(name, v, x_re, x_im):
    s = v.shape[0]
    ts = min(s, 512)
    nt = s // ts
    half = SSM_GB * SSM_P
    rows = SSM_GB * SSM_N
    tn = (((0,), (0,)), ((), ()))

    def body(v_ref, xr_ref, xi_ref, or_ref, oi_ref, acc_ref):
        vv = _bf(v_ref[...])
        pr = lax.dot_general(vv, _bf(xr_ref[...]), tn, preferred_element_type=F32)
        pi = lax.dot_general(vv, _bf(xi_ref[...]), tn, preferred_element_type=F32)
        t = pl.program_id(1)

        @pl.when(t == 0)
        def _():
            acc_ref[:, :half] = pr
            acc_ref[:, half:] = pi

        @pl.when(t > 0)
        def _():
            acc_ref[:, :half] += pr
            acc_ref[:, half:] += pi

        @pl.when(t == nt - 1)
        def _():
            row_g = lax.broadcasted_iota(jnp.int32, (rows, LANE), 0) // SSM_N
            lane_g = lax.broadcasted_iota(jnp.int32, (rows, LANE), 1) // SSM_P
            for part, o_ref in enumerate((or_ref, oi_ref)):
                fold = jnp.zeros((rows, LANE), F32)
                for cb in range(half // LANE):
                    blk = acc_ref[:, part * half + cb * LANE:part * half + (cb + 1) * LANE]
                    fold = fold + jnp.where(2 * cb + lane_g == row_g, blk, 0.0)
                o_ref[...] = jnp.where(row_g % 2 == 0, fold, pltpu.roll(fold, SSM_P, 1))

    xin = pl.BlockSpec((ts, half), lambda q, t: (t, q))
    out = pl.BlockSpec((None, rows, LANE), lambda q, t: (q, 0, 0))
    return _pc(body, name, (SSM_NB, nt), [pl.BlockSpec((ts, rows), lambda q, t: (t, q)), xin, xin], [out, out],
               [_sds((SSM_NB, rows, LANE))] * 2, scratch=[pltpu.VMEM((rows, 2 * half), F32)])(v, x_re, x_im)


def _ssm_dlam(x_re, x_im, a_re, a_im):
    s = x_re.shape[0]
    ln = SCAN_LANES

    def body(xr_ref, xi_ref, ar_ref, ai_ref, or_ref, oi_ref):
        row = lax.broadcasted_iota(jnp.int32, (s, ln), 0)
        xr = _shift_rows(xr_ref[...], 1, row, s, False)
        xi = _shift_rows(xi_ref[...], 1, row, s, False)
        ar, ai = ar_ref[...], ai_ref[...]
        or_ref[...] = _colsum(xr * ar + xi * ai)
        oi_ref[...] = _colsum(xr * ai - xi * ar)

    blk = pl.BlockSpec((s, ln), lambda j: (0, j))
    vec = pl.BlockSpec((1, ln), lambda j: (0, j))
    return _pc(body, "ssm_dlam", (SSM_L // ln,), [blk] * 4, [vec, vec], [_sds((1, SSM_L))] * 2)(x_re, x_im, a_re, a_im)


def _ssm_act_fwd(y, u, d_skip):
    s = y.shape[0]
    ts = min(s, 512)

    def body(y_ref, u_ref, d_ref, o_ref):
        o_ref[...] = _gelu(y_ref[...] + d_ref[...] * u_ref[...]).astype(BF16)

    return _pc(body, "ssm_act_fwd", (s // ts,), [_row_spec(ts, D)] * 2 + [_vec_spec(D)], _row_spec(ts, D),
               _sds((s, D), BF16))(y, u, d_skip)


def _ssm_act_bwd(dg, y, u, d_skip):
    s = y.shape[0]
    ts = min(s, 512)

    def body(dg_ref, y_ref, u_ref, d_ref, dy_ref, dd_ref):
        uv = u_ref[...]
        dy = dg_ref[...] * _gelu_grad(y_ref[...] + d_ref[...] * uv)
        dy_ref[...] = dy.astype(BF16)
        _acc(dd_ref, pl.program_id(0) == 0, _colsum(dy * uv))

    return _pc(body, "ssm_act_bwd", (s // ts,), [_row_spec(ts, D)] * 3 + [_vec_spec(D)], [_row_spec(ts, D), _vec_spec(D)],
               [_sds((s, D), BF16), _sds((1, D))])(dg, y, u, d_skip)


def _axpy(a, b, d_skip):
    s = a.shape[0]
    ts = min(s, 512)

    def body(a_ref, b_ref, d_ref, o_ref):
        o_ref[...] = (a_ref[...] + d_ref[...] * b_ref[...].astype(F32)).astype(BF16)

    return _pc(body, "ssm_du", (s // ts,), [_row_spec(ts, D)] * 2 + [_vec_spec(D)], _row_spec(ts, D),
               _sds((s, D), BF16))(a, b, d_skip)


def _glu_fwd(zz):
    s = zz.shape[0]
    ts = min(s, 512)

    def body(a_ref, b_ref, o_ref):
        o_ref[...] = a_ref[...] * _sigmoid(b_ref[...])

    return _pc(body, "glu_fwd", (s // ts,), [_row_spec(ts, D, 0), _row_spec(ts, D, 1)], _row_spec(ts, D), _sds((s, D)))(zz, zz)


def _glu_bwd(zz, df):
    s = zz.shape[0]
    ts = min(s, 512)

    def body(a_ref, b_ref, df_ref, o_ref):
        sg = _sigmoid(b_ref[...])
        dfv = df_ref[...].astype(F32)
        o_ref[:, :D] = (dfv * sg).astype(BF16)
        o_ref[:, D:] = (dfv * a_ref[...] * sg * (1.0 - sg)).astype(BF16)

    return _pc(body, "glu_bwd", (s // ts,), [_row_spec(ts, D, 0), _row_spec(ts, D, 1), _row_spec(ts, D)],
               _row_spec(ts, 2 * D), _sds((s, 2 * D), BF16))(zz, zz, df)


def _ssm_block_diag(m_re, m_im):
    rows, half = SSM_GB * SSM_N, SSM_GB * SSM_P
    expand = jnp.tile(jnp.eye(SSM_P, dtype=BF16), (1, SSM_GB))

    def body(mr_ref, mi_ref, e_ref, o_ref):
        keep = (lax.broadcasted_iota(jnp.int32, (rows, half), 0) // SSM_N
                == lax.broadcasted_iota(jnp.int32, (rows, half), 1) // SSM_P)
        for part, m_ref in enumerate((mr_ref, mi_ref)):
            t = jnp.dot(_bf(m_ref[...]), e_ref[...], preferred_element_type=F32)
            o_ref[:, part * half:(part + 1) * half] = jnp.where(keep, t, 0.0).astype(BF16)

    blk = pl.BlockSpec((rows, SSM_P), lambda q: (q, 0))
    return _pc(body, "ssm_block_diag", (SSM_NB,), [blk, blk, pl.BlockSpec((SSM_P, half), lambda q: (0, 0))],
               pl.BlockSpec((None, rows, 2 * half), lambda q: (q, 0, 0)), _sds((SSM_NB, rows, 2 * half), BF16))(m_re, m_im, expand)


def _mod_part(c_all, ada_w):
    n = ada_w.shape[-1]

    def body(c_ref, w_ref, o_ref):
        cv = c_ref[...]
        cond = _bf(cv * _sigmoid(cv))
        o_ref[...] = jnp.dot(cond, _bf(w_ref[...]), preferred_element_type=F32)

    return _pc(body, "mod_part", (2,), [pl.BlockSpec((N_DEV, D), lambda l: (0, 0)), pl.BlockSpec((None, D, n), lambda l: (l, 0, 0))],
               pl.BlockSpec((None, N_DEV, n), lambda l: (l, 0, 0)), _sds((2, N_DEV, n)))(c_all, ada_w)


def _ada_w_grad(c_all_t, dmod):
    n = dmod.shape[-1]
    tr = 128

    def body(c_ref, d_ref, o_ref):
        cv = c_ref[...]
        cond = _bf(cv * _sigmoid(cv)).astype(F32)
        dm = _bf(d_ref[...]).astype(F32)
        acc = cond[:, 0:1] * dm[0:1, :]
        for b in range(1, N_DEV):
            acc = acc + cond[:, b:b + 1] * dm[b:b + 1, :]
        o_ref[...] = acc

    return _pc(body, "ada_w_grad", (2, D // tr),
               [pl.BlockSpec((tr, N_DEV), lambda l, t: (t, 0)), pl.BlockSpec((None, N_DEV, n), lambda l, t: (l, 0, 0))],
               pl.BlockSpec((None, tr, n), lambda l, t: (l, t, 0)), _sds((2, D, n)))(c_all_t, dmod)


def _adamw(name, parts, w, m, v, slot=0, prev=None):
    p, r, c = parts.shape
    tr = r
    while tr * c * 4 > (1 << 20) and tr % 16 == 0:
        tr //= 2
    nt = r // tr

    def body(p_ref, w_ref, m_ref, v_ref, *rest):
        g_ref, d_ref, nm_ref, nv_ref = rest[-4:]
        g = p_ref[0].astype(F32)
        for i in range(1, p):
            g = g + p_ref[i].astype(F32)
        g_ref[...] = g
        m2 = B1 * m_ref[...] + (1.0 - B1) * g
        v2 = B2 * v_ref[...] + (1.0 - B2) * (g * g)
        nm_ref[...] = m2
        nv_ref[...] = v2
        m_hat = m2 / (1.0 - B1 ** STEP)
        v_hat = v2 / (1.0 - B2 ** STEP)
        d_ref[...] = -LR * (m_hat / (jnp.sqrt(v_hat) + ADAM_EPS) + WD * w_ref[...])

    blk = pl.BlockSpec((tr, c), lambda t: (slot * nt + t, 0))
    in_specs = [pl.BlockSpec((p, tr, c), lambda t: (0, t, 0)), blk, blk, blk]
    if prev is None:
        return _pc(body, name, (nt,), in_specs, [blk] * 4, [_sds(w.shape)] * 4)(parts, w, m, v)
    return pl.pallas_call(
        body, name=name, grid=(nt,), in_specs=in_specs + [pl.BlockSpec(memory_space=pl.ANY)] * 4, out_specs=[blk] * 4,
        out_shape=[_sds(w.shape)] * 4, input_output_aliases={4 + i: i for i in range(4)},
        compiler_params=pltpu.CompilerParams(dimension_semantics=("arbitrary",), vmem_limit_bytes=VMEM_LIMIT_BYTES))(parts, w, m, v, *prev)


def _sum_parts(parts):
    p, r, c = parts.shape
    tr = r
    while tr * c * 4 > (1 << 19) and tr % 16 == 0:
        tr //= 2

    def body(p_ref, o_ref):
        g = p_ref[0]
        for i in range(1, p):
            g = g + p_ref[i]
        o_ref[...] = g

    return _pc(body, "sum_parts", (r // tr,), [pl.BlockSpec((p, tr, c), lambda t: (0, t, 0))], pl.BlockSpec((tr, c), lambda t: (t, 0)),
               _sds((r, c)))(parts)


def _place():
    x, y, c = lax.axis_index("x"), lax.axis_index("y"), lax.axis_index("c")
    peers = []
    for k in range(1, N_DEV):
        px = (1 - x) if k & 4 else x
        py = (1 - y) if k & 2 else y
        pc = (1 - c) if k & 1 else c
        peers.append(((px, py, pc), 4 * px + 2 * py + pc))
    return 4 * x + 2 * y + c, peers


def _at(ref, idx):
    return ref if idx is None else ref.at[idx]


def _exchange_copies(plan, n, src_refs, dst_refs, send_sems, recv_sems, local_sems=None, with_arrivals=True):
    me, peers = _place()
    local = [] if local_sems is None else [
        pltpu.make_async_copy(_at(src_refs[si], sx), _at(dst_refs[di], dx), local_sems.at[i])
        for i, (si, sx, di, dx) in enumerate(plan(me, me, 0))]

    def remote(k, i, dev, entry):
        si, sx, di, dx = entry
        return pltpu.make_async_remote_copy(_at(src_refs[si], sx), _at(dst_refs[di], dx), send_sems.at[k * n + i], recv_sems.at[k * n + i],
                                            device_id=dev, device_id_type=MESH)

    sends = [remote(k, i, dev, e) for k, (dev, peer) in enumerate(peers) for i, e in enumerate(plan(me, peer, k + 1))]
    if not with_arrivals:
        return local, sends, []
    arrivals = [remote(k, i, dev, e) for k, (dev, peer) in enumerate(peers) for i, e in enumerate(plan(peer, me, k + 1))]
    return local, sends, arrivals


def _sem_shapes(n_copies, local=True):
    sems = [pltpu.SemaphoreType.DMA(((N_DEV - 1) * n_copies,)), pltpu.SemaphoreType.DMA(((N_DEV - 1) * n_copies,))]
    return sems + [pltpu.SemaphoreType.DMA((n_copies,))] if local else sems


def _exchange(name, srcs, dst_shapes, plan, n_copies):
    ns, nd = len(srcs), len(dst_shapes)

    def body(*refs):
        local, sends, arrivals = _exchange_copies(plan, n_copies, refs[:ns], refs[ns:ns + nd], *refs[ns + nd:])
        for cp in local + sends:
            cp.start()
        for cp in arrivals:
            cp.wait_recv()
        for cp in sends:
            cp.wait_send()
        for cp in local:
            cp.wait()

    any_spec = pl.BlockSpec(memory_space=pl.ANY)
    return pl.pallas_call(
        body, name=name, in_specs=[any_spec] * ns, out_specs=[any_spec] * nd, out_shape=list(dst_shapes),
        scratch_shapes=_sem_shapes(n_copies))(*srcs)


HBM_SPEC = pl.BlockSpec(memory_space=pltpu.HBM)
SEM_SPEC = pl.BlockSpec(memory_space=pltpu.SEMAPHORE)
ANY_SPEC = pl.BlockSpec(memory_space=pl.ANY)
TOKEN_SPEC = pl.BlockSpec(memory_space=pltpu.VMEM)
SIDE_EFFECT = pltpu.SideEffectType.DATAFLOW_SIDE_EFFECTING


def _wait_all(local, sends, arrivals):
    for cp in arrivals:
        cp.wait_recv()
    for cp in sends:
        cp.wait_send()
    for cp in local:
        cp.wait()


def _exchange_start(name, srcs, dst_shapes, plan, n_copies, order):
    ns, nd = len(srcs), len(dst_shapes)
    nb = ns + nd

    def body(*refs):
        local, sends, _ = _exchange_copies(plan, n_copies, refs[:ns], refs[ns:nb], *refs[nb + 1:nb + 4], with_arrivals=False)
        for cp in local + sends:
            cp.start()
        refs[-1][...] = jnp.zeros((8, LANE), F32)

    lands = [pltpu.with_memory_space_constraint(lax.empty(d.shape, d.dtype), pltpu.HBM) for d in dst_shapes]
    srcs = [pltpu.with_memory_space_constraint(a, pltpu.HBM) for a in srcs]
    bufs = srcs + lands
    out = pl.pallas_call(
        body, name=name, in_specs=[HBM_SPEC] * nb + [ANY_SPEC],
        out_specs=[SEM_SPEC] * 3 + [HBM_SPEC] * nb + [TOKEN_SPEC],
        out_shape=_sem_shapes(n_copies) + [pltpu.HBM(a.shape, a.dtype) for a in bufs] + [_sds((8, LANE))],
        input_output_aliases={i: 3 + i for i in range(nb)},
        compiler_params=pltpu.CompilerParams(has_side_effects=SIDE_EFFECT))(*bufs, order)
    return out[:3], out[3:3 + ns], out[3 + ns:3 + nb], out[-1]


def _exchange_relay(name, sems, srcs, lands, plan, n_copies, plan2, n_copies2, after):
    ns, nd = len(srcs), len(lands)
    nb = ns + nd

    def body(*refs):
        land_refs = refs[ns:nb]
        _wait_all(*_exchange_copies(plan, n_copies, refs[:ns], land_refs, *refs[nb:nb + 3]))
        _, sends, _ = _exchange_copies(plan2, n_copies2, land_refs, land_refs, *refs[nb + 4:nb + 6], with_arrivals=False)
        for cp in sends:
            cp.start()
        refs[-1][...] = jnp.zeros((8, LANE), F32)

    out = pl.pallas_call(
        body, name=name, in_specs=[HBM_SPEC] * nb + [SEM_SPEC] * 3 + [ANY_SPEC],
        out_specs=[SEM_SPEC] * 2 + [HBM_SPEC] * nd + [TOKEN_SPEC],
        out_shape=_sem_shapes(n_copies2, local=False) + [pltpu.HBM(a.shape, a.dtype) for a in lands] + [_sds((8, LANE))],
        input_output_aliases={ns + i: 2 + i for i in range(nd)},
        compiler_params=pltpu.CompilerParams(has_side_effects=SIDE_EFFECT))(*srcs, *lands, *sems, after)
    return out[:2], out[2:2 + nd], out[-1]


def _exchange_wait(name, sems, srcs, lands, plan, n_copies, after):
    srcs = [] if srcs is None else list(srcs)
    ns, nd = len(srcs), len(lands)
    nb = ns + nd

    def body(*refs):
        land_refs = refs[ns:nb]
        _wait_all(*_exchange_copies(plan, n_copies, refs[:ns] if ns else land_refs, land_refs, *refs[nb:nb + len(sems)]))

    bufs = srcs + list(lands)
    out = pl.pallas_call(
        body, name=name, in_specs=[HBM_SPEC] * nb + [SEM_SPEC] * len(sems) + [ANY_SPEC],
        out_specs=[HBM_SPEC] * nb, out_shape=[pltpu.HBM(a.shape, a.dtype) for a in bufs],
        input_output_aliases={i: i for i in range(nb)},
        compiler_params=pltpu.CompilerParams(has_side_effects=SIDE_EFFECT))(*bufs, *sems, after)
    return out[ns:]


def _all_gather(name, arrs):
    plan = lambda me, peer, k: [(i, None, i, me) for i in range(len(arrs))]
    return _exchange(name, arrs, [_sds((N_DEV,) + a.shape, a.dtype) for a in arrs], plan, len(arrs))


def _sublayer_fwd(x, fn, mod3, g_pre, g_post, rw):
    h = _prenorm_fwd(x, g_pre, mod3[1:2], mod3[0:1])
    f, saved = fn(h)
    return _postnorm_fwd(x, f, g_post, mod3[2:3], rw), (x, f, saved)


def _sublayer_bwd(dout, saved, fn_bwd, mod3, g_pre, g_post, rw):
    x, f, inner = saved
    df, dgate, dg_post = _postnorm_bwd(dout, f, g_post, mod3[2:3], rw)
    dh, extra = fn_bwd(df, inner)
    dx, dshift, dscale, dg_pre = _prenorm_bwd(dout, dh, x, g_pre, mod3[1:2])
    return dx, jnp.concatenate([dshift, dscale, dgate], axis=0), dg_pre, dg_post, extra


def _mix0_fwd(h, p):
    z = _mm_nt("mix0_in", h, p["ab_w_in"])
    y_a, d = _pool_fwd(z, p["pool_w"], p["pool_scale"])
    y_b = _sgu_fwd(z, p["sgu_ln_g"], p["sgu_ln_b"], p["sgu_w"], p["sgu_bt"])
    ycat = jnp.concatenate([y_a, y_b], axis=1)
    return _mm_nn("mix0_out", ycat, p["ab_w_out"]), (h, z, d, ycat)


def _mix0_bwd(df, saved, p):
    h, z, d, ycat = saved
    dycat = _mm_nt("mix0_out_dx", df, p["ab_w_out"])
    g = {"ab_w_out": _mm_tn("mix0_out_dw", ycat, df, BF16)}
    dz_p, g["pool_w"], g["pool_scale"] = _pool_bwd(dycat, d, p["pool_w"], p["pool_scale"])
    dz_u, dz_v, g["sgu_ln_g"], g["sgu_ln_b"], g["sgu_w"], dbt = _sgu_bwd(
        z, dycat, p["sgu_ln_g"], p["sgu_ln_b"], p["sgu_w"], p["sgu_bt"], p["head_sum"])
    g["sgu_b"] = dbt[:, :NH].T
    dz = jnp.concatenate([dz_p, dz_u, dz_v], axis=1)
    g["ab_w_in"] = _mm_tn("mix0_in_dw", dz, h, BF16)
    return _mm_nn("mix0_in_dx", dz, p["ab_w_in"]), g


def _mix1_fwd(h, p):
    u = _mm_nn("ssm_w_in", h, p["ssm_w_in"])
    bu_re, bu_im = _ssm_in("ssm_bu", u, p["wb_bd"])
    x_re, x_im = _ssm_scan(bu_re, bu_im, p["lam_bar_re"], p["lam_bar_im"], False)
    y = _ssm_out("ssm_y", x_re, x_im, p["wc_bd"])
    g = _ssm_act_fwd(y, u, p["ssm_d"])
    zz = _mm_nn("ssm_glu", g, p["ssm_w_glu"])
    return _glu_fwd(zz), (h, u, x_re, x_im, y, g, zz)


def _mix1_bwd(df, saved, p):
    h, u, x_re, x_im, y, g, zz = saved
    gr = {}
    dzz = _glu_bwd(zz, df)
    dg = _mm_nt("ssm_glu_dx", dzz, p["ssm_w_glu"])
    gr["ssm_w_glu"] = _mm_tn("ssm_glu_dw", g, dzz, BF16)
    dy, gr["ssm_d"] = _ssm_act_bwd(dg, y, u, p["ssm_d"])
    gx_re, gx_im = _ssm_in("ssm_gx", dy, p["wct_bd"])
    a_re, a_im = _ssm_scan(gx_re, gx_im, p["lam_bar_re"], -p["lam_bar_im"], True)
    du_ssm = _ssm_out("ssm_du_mm", a_re, a_im, p["wbt_bd"])
    du = _axpy(du_ssm, dy, p["ssm_d"])
    gr["ssm_w_in"] = _mm_tn("ssm_w_in_dw", h, du, BF16)
    dh = _mm_nt("ssm_w_in_dx", du, p["ssm_w_in"])
    g_lam_re, g_lam_im = _ssm_dlam(x_re, x_im, a_re, a_im)
    mb_re, mb_im = _ssm_outer("ssm_db", u, a_re, a_im)
    mc_re, mc_im = _ssm_outer("ssm_dc", dy, x_re, x_im)
    per_group = lambda m: m[:, :, :SSM_P].reshape(SSM_G, SSM_N, SSM_P)
    gr["ssm_c_re"] = per_group(mc_re)
    gr["ssm_c_im"] = -per_group(mc_im)
    dlr, dli, ddt, dbr, dbi = _ssm_param_bwd(
        g_lam_re.reshape(SSM_G, SSM_P), g_lam_im.reshape(SSM_G, SSM_P),
        per_group(mb_re).reshape(SSM_G, SSM_N * SSM_P), per_group(mb_im).reshape(SSM_G, SSM_N * SSM_P),
        p["lam_re"], p["lam_im"], p["lam_re_rep"], p["lam_im_rep"], p["log_dt"], p["b_re"], p["b_im"], p["seg"])
    gr["ssm_lam_re"], gr["ssm_lam_im"], gr["ssm_log_dt"] = dlr, dli, ddt[:, 0]
    gr["ssm_b_re"] = dbr.reshape(SSM_G, SSM_N, SSM_P).transpose(0, 2, 1)
    gr["ssm_b_im"] = dbi.reshape(SSM_G, SSM_N, SSM_P).transpose(0, 2, 1)
    return dh, gr


def _ssm_params(lam_re, lam_im, b_re, b_im, c_re, c_im, log_dt):
    wide = lambda b: b.transpose(0, 2, 1).reshape(SSM_G, SSM_N * SSM_P)
    p = {"lam_re": lam_re, "lam_im": lam_im, "log_dt": log_dt.reshape(SSM_G, 1),
         "lam_re_rep": jnp.tile(lam_re, (1, SSM_N)), "lam_im_rep": jnp.tile(lam_im, (1, SSM_N)), "b_re": wide(b_re), "b_im": wide(b_im)}
    lbr, lbi, bbr, bbi = _ssm_prep(lam_re, lam_im, p["lam_re_rep"], p["lam_im_rep"], p["log_dt"], p["b_re"], p["b_im"])
    p["lam_bar_re"], p["lam_bar_im"] = lbr.reshape(1, SSM_L), lbi.reshape(1, SSM_L)
    rows = lambda m: m.reshape(SSM_G * SSM_N, SSM_P)
    p["wb_bd"] = p["wbt_bd"] = _ssm_block_diag(rows(bbr), rows(bbi))
    p["wc_bd"] = p["wct_bd"] = _ssm_block_diag(rows(c_re), rows(-c_im))
    p["seg"] = jnp.tile(jnp.eye(SSM_P, dtype=F32), (SSM_N, 1))
    return p


RES_WEIGHT = (0.5, 1.0, 0.5)


def _local_step(x, tgt, mod, norm_pre, norm_post, weights_of, on_part, on_grads):
    def fns(i, w):
        if i % 3 != 1:
            return (lambda h: _ffn_fwd(h, *w)), (lambda df, sv: (_ffn_bwd(df, sv, *w, lambda tag, part: on_part(i, tag, part)), None))
        if i == 1:
            return (lambda h: _mix0_fwd(h, w)), (lambda df, sv: _mix0_bwd(df, sv, w))
        return (lambda h: _mix1_fwd(h, w)), (lambda df, sv: _mix1_bwd(df, sv, w))

    saved, bwd = [], []
    for i in range(6):
        l, s = divmod(i, 3)
        w, token = weights_of(i, x)
        f, b = fns(i, w)
        x, sv = _sublayer_fwd(x, f, mod[l, s] + token[0:1, 0:1], norm_pre[l, s][None], norm_post[l, s][None], RES_WEIGHT[s])
        saved.append(sv)
        bwd.append(b)
    loss_row, dx = _loss_fwd_bwd(x, tgt)
    token = jnp.zeros((8, LANE), F32)
    for i in reversed(range(6)):
        l, s = divmod(i, 3)
        mod3 = mod[l, s] + token[0:1, 0:1]
        dx, dmod, dpre, dpost, extra = _sublayer_bwd(
            dx, saved[i], bwd[i], mod3, norm_pre[l, s][None], norm_post[l, s][None], RES_WEIGHT[s])
        token = on_grads(i, extra, dmod, dpre, dpost, loss_row)
    return dx


def _pad_rows(v, rows):
    return jnp.pad(v, (0, rows * LANE - v.shape[0])).reshape(rows, LANE)


def _pack(parts):
    flat, layout, off = [], [], 0
    for a in parts:
        n = a.size
        padded = -(-n // LANE) * LANE
        flat.append(jnp.pad(a.reshape(-1).astype(F32), (0, padded - n)))
        layout.append((off, n, a.shape))
        off += padded
    return jnp.concatenate(flat), layout


def _unpack(flat, layout):
    return [flat[off:off + n].reshape(shape) for off, n, shape in layout]


SMALL_REPLICATED = ["ada_b", "pool_w", "pool_scale", "sgu_ln_g", "sgu_ln_b", "sgu_w", "sgu_b", "ssm_lam_re", "ssm_lam_im",
                    "ssm_b_re", "ssm_b_im", "ssm_c_re", "ssm_c_im", "ssm_log_dt"]
SMALL_SHARDED = ["norm_pre", "norm_post", "ssm_d"]
TRANSPOSED = ["ffn_w_in", "ab_w_in"]
WEIGHTS = ['ada_w', 'ada_b', 'norm_pre', 'norm_post', 'ffn_w_in', 'ffn_w_out', 'ab_w_in', 'pool_w', 'pool_scale', 'sgu_ln_g',
           'sgu_ln_b', 'sgu_w', 'sgu_b', 'ab_w_out', 'ssm_w_in', 'ssm_lam_re', 'ssm_lam_im', 'ssm_b_re', 'ssm_b_im', 'ssm_c_re',
           'ssm_c_im', 'ssm_d', 'ssm_log_dt', 'ssm_w_glu']


def kernel(x, c, ada_w, ada_b, norm_pre, norm_post, ffn_w_in, ffn_w_out, ab_w_in, pool_w, pool_scale, sgu_ln_g, sgu_ln_b, sgu_w, sgu_b, ab_w_out, ssm_w_in, ssm_lam_re, ssm_lam_im, ssm_b_re, ssm_b_im, ssm_c_re, ssm_c_im, ssm_d, ssm_log_dt, ssm_w_glu, loss_target, m_ada_w, m_ada_b, m_norm_pre, m_norm_post, m_ffn_w_in, m_ffn_w_out, m_ab_w_in, m_pool_w, m_pool_scale, m_sgu_ln_g, m_sgu_ln_b, m_sgu_w, m_sgu_b, m_ab_w_out, m_ssm_w_in, m_ssm_lam_re, m_ssm_lam_im, m_ssm_b_re, m_ssm_b_im, m_ssm_c_re, m_ssm_c_im, m_ssm_d, m_ssm_log_dt, m_ssm_w_glu, v_ada_w, v_ada_b, v_norm_pre, v_norm_post, v_ffn_w_in, v_ffn_w_out, v_ab_w_in, v_pool_w, v_pool_scale, v_sgu_ln_g, v_sgu_ln_b, v_sgu_w, v_sgu_b, v_ab_w_out, v_ssm_w_in, v_ssm_lam_re, v_ssm_lam_im, v_ssm_b_re, v_ssm_b_im, v_ssm_c_re, v_ssm_c_im, v_ssm_d, v_ssm_log_dt, v_ssm_w_glu):
    args = locals()
    wts = {n: args[n] for n in WEIGHTS}
    mom = {n: args["m_" + n] for n in WEIGHTS}
    var = {n: args["v_" + n] for n in WEIGHTS}
    for n in TRANSPOSED:
        for t in (wts, mom, var):
            t[n] = jnp.swapaxes(t[n], -1, -2)
    me = 4 * lax.axis_index("x") + 2 * lax.axis_index("y") + lax.axis_index("c")
    s = x.shape[1]
    nd = D // N_DEV

    small_in, small_in_layout = _pack([c, norm_pre, norm_post, ssm_d])
    small_rows = -(-small_in.shape[0] // (8 * LANE)) * 8
    (g_small,) = _all_gather("gather_small", [_pad_rows(small_in, small_rows)])
    g_small = g_small.reshape(N_DEV, -1)
    c_all, npre_g, npost_g, sd_g = [jnp.stack([_unpack(g_small[j], small_in_layout)[i] for j in range(N_DEV)]) for i in range(4)]
    c_all = c_all.reshape(N_DEV, D)
    norm_pre_full = npre_g.transpose(1, 2, 0, 3).reshape(2, 3, D)
    norm_post_full = npost_g.transpose(1, 2, 0, 3).reshape(2, 3, D)
    ssm_d_full = sd_g.transpose(1, 0, 2).reshape(1, D)

    nw = ada_w.shape[-1]
    (mod_g,) = _all_gather("gather_mod", [_mod_part(c_all, ada_w)])
    mod = lax.dynamic_index_in_dim(mod_g, me, axis=2, keepdims=False)
    mod = (mod.transpose(1, 0, 2).reshape(2, N_DEV * nw) + ada_b).reshape(2, 3, 3, D)

    w_in_t = wts["ffn_w_in"]
    shards = [[w_in_t[0, 0], ffn_w_out[0, 0]], [wts["ab_w_in"][0], ab_w_out[0]], [w_in_t[0, 1], ffn_w_out[0, 1]],
              [w_in_t[1, 0], ffn_w_out[1, 0]], [ssm_w_in[0], ssm_w_glu[0]], [w_in_t[1, 1], ffn_w_out[1, 1]]]
    same_core = (2, 4, 6)
    gather_plan = lambda me_, peer_, k: [(0, None, 0, me_), (1, None, 1, me_)] if k in (0, 1) + same_core else []
    relay_plan = lambda me_, peer_, k: [(a, me_ ^ kk, a, me_ ^ kk) for kk in same_core for a in (0, 1)] if k == 1 else []
    gathers, relays = [], {}
    token = mod_g
    for i, pair in enumerate(shards):
        pair = [a.astype(BF16) for a in pair]
        sems, srcs_thru, lands, token = _exchange_start(
            f"gather_start_{i}", pair, [_sds((N_DEV,) + a.shape, BF16) for a in pair], gather_plan, 2, token)
        gathers.append((sems, srcs_thru, lands))
    mod = mod + token[0, 0]

    def relay(i, after):
        sems, srcs_thru, lands = gathers[i]
        relays[i] = _exchange_relay(f"gather_relay_{i}", sems, srcs_thru, lands, gather_plan, 2, relay_plan, 6, after)

    head_sum = jnp.repeat(jnp.eye(NH, LANE, dtype=F32), HD, axis=0)
    mix0 = {"pool_w": pool_w[0], "pool_scale": pool_scale, "sgu_ln_g": sgu_ln_g, "sgu_ln_b": sgu_ln_b, "sgu_w": sgu_w[0],
            "sgu_bt": jnp.pad(sgu_b[0].T, ((0, 0), (0, LANE - NH))), "head_sum": head_sum}
    mix1 = _ssm_params(ssm_lam_re[0], ssm_lam_im[0], ssm_b_re[0], ssm_b_im[0], ssm_c_re[0], ssm_c_im[0], ssm_log_dt[0])
    mix1["ssm_d"] = ssm_d_full

    def weights_of(i, x_in):
        if i == 0:
            relay(0, x_in)
        sems, lands, token = relays[i]
        a, b = _exchange_wait(f"gather_wait_{i}", sems, None, lands, relay_plan, 6, x_in)
        if i + 1 < len(gathers):
            relay(i + 1, a)
            token = relays[i + 1][2]
        if i % 3 != 1:
            return (a, b), token
        if i == 1:
            return dict(mix0, ab_w_in=a.reshape(-1, D), ab_w_out=b.reshape(D, D)), token
        return dict(mix1, ssm_w_in=a.reshape(D, D), ssm_w_glu=b.transpose(1, 0, 2).reshape(D, -1)), token

    def shard_cols(a):
        r = a.shape[0]
        return a.reshape(r, N_DEV, -1).transpose(1, 0, 2)

    scatter_plan = lambda me_, peer_, k: [(0, peer_, 0, me_), (1, peer_, 1, me_)]
    scatter_plan1 = lambda me_, peer_, k: [(0, peer_, 0, me_)]
    scatters = []
    last_token = [jnp.zeros((8, LANE), F32)]
    pieces, mixer, bundles = {}, {}, {}
    bundle_plan = lambda me_, peer_, k: [(0, None, 0, me_)]

    def on_part(i, tag, part):
        sems, srcs_thru, lands, last_token[0] = _exchange_start(
            f"scatter_start_{i}_{tag}", [part], [_sds(part.shape, BF16)], scatter_plan1, 1, last_token[0])
        scatters.append((i, ("ffn_" + tag,), scatter_plan1, sems, srcs_thru, lands))
        return last_token[0]
    mix0_names = ["pool_w", "pool_scale", "sgu_ln_g", "sgu_ln_b", "sgu_w", "sgu_b"]
    mix1_names = ["ssm_lam_re", "ssm_lam_im", "ssm_b_re", "ssm_b_im", "ssm_c_re", "ssm_c_im", "ssm_log_dt", "ssm_d"]

    def start_bundle(tag, arrays):
        flat, layout = _pack(arrays)
        rows = -(-flat.shape[0] // (8 * LANE)) * 8
        sems, srcs_thru, lands, last_token[0] = _exchange_start(
            f"small_start_{tag}", [_pad_rows(flat, rows)], [_sds((N_DEV, rows, LANE))], bundle_plan, 1, last_token[0])
        bundles[tag] = (sems, srcs_thru, lands, layout)

    def on_grads(i, extra, dmod_i, dpre_i, dpost_i, loss_row):
        pieces[i] = (dmod_i, dpre_i, dpost_i)
        if i == 4:
            mixer.update({n: extra[n] for n in mix1_names})
        if i == 1:
            mixer.update({n: extra[n] for n in mix0_names})
            rest = range(1, 6)
            start_bundle("a", [jnp.stack([pieces[j][0] for j in rest])] + [jnp.concatenate([pieces[j][k] for j in rest]) for k in (1, 2)]
                         + [mixer[n] for n in mix0_names + mix1_names])
        if i == 0:
            start_bundle("b", [dmod_i, dpre_i, dpost_i, loss_row])
        if i % 3 != 1:
            return last_token[0]
        if i == 1:
            names, parts = ("ab_w_in", "ab_w_out"), [extra["ab_w_in"].reshape(N_DEV, -1, D), extra["ab_w_out"].reshape(N_DEV, nd, D)]
        else:
            names, parts = ("ssm_w_in", "ssm_w_glu"), [extra["ssm_w_in"].reshape(N_DEV, nd, D), shard_cols(extra["ssm_w_glu"])]
        sems, srcs_thru, lands, last_token[0] = _exchange_start(
            f"scatter_start_{i}", parts, [_sds(a.shape, BF16) for a in parts], scatter_plan, 2, last_token[0])
        scatters.append((i, names, scatter_plan, sems, srcs_thru, lands))
        return last_token[0]

    grad_x = _local_step(x[0], loss_target[0], mod, norm_pre_full, norm_post_full, weights_of, on_part, on_grads)

    out_g, out_d, out_m, out_v = {}, {}, {}, {}
    big_out = {}

    def adam_big(name, recv, n, slot=0):
        c_ = wts[n].shape[-1]
        big_out[n] = _adamw(name, recv.reshape(recv.shape[0], -1, c_), *[t[n].reshape(-1, c_) for t in (wts, mom, var)],
                            slot=slot, prev=big_out.get(n))
        return big_out[n][0]

    ffn_slot = {0: 0, 2: 1, 3: 2, 5: 3}

    def land_and_update(entries, after):
        for i, names, plan, sems, srcs_thru, lands in entries:
            recv = _exchange_wait(f"scatter_wait_{i}_{names[0]}", sems, srcs_thru, lands, plan, len(names), after)
            for n, r in zip(names, recv):
                after = adam_big(f"adamw_{n}_{i}", r, n, ffn_slot.get(i, 0))
        return after

    after = land_and_update([e for e in scatters if e[0] != 0], grad_x)

    gathered, sums = {}, {}
    for tag in ("a", "b"):
        sems, srcs_thru, lands, layout = bundles[tag]
        (g_parts,) = _exchange_wait(f"small_wait_{tag}", sems, srcs_thru, lands, bundle_plan, 1, after)
        total = _sum_parts(g_parts)
        after = total
        off, n, shape = layout[0]
        gathered[tag] = g_parts.reshape(N_DEV, -1)[:, off:off + n].reshape((N_DEV,) + shape)
        sums[tag] = _unpack(total.reshape(-1), layout)
    dmod_a, dpre_a, dpost_a = sums["a"][:3]
    dmod_b, dpre_b, dpost_b, loss_sum = sums["b"]
    small = dict(zip(mix0_names + mix1_names, sums["a"][3:]))
    small["ada_b"] = jnp.concatenate([dmod_b[None], dmod_a])
    small["norm_pre"] = jnp.concatenate([dpre_b, dpre_a]).reshape(2, 3, D)
    small["norm_post"] = jnp.concatenate([dpost_b, dpost_a]).reshape(2, 3, D)
    loss = loss_sum[0, 0]

    def adam_flat(name, gnames, grads):
        gf, lay = _pack(grads)
        r = -(-gf.shape[0] // (8 * LANE)) * 8
        packed = [_pad_rows(_pack([t[n] for n in gnames])[0], r) for t in (wts, mom, var)]
        res = _adamw(name, _pad_rows(gf, r)[None], *packed)
        for o, arr in zip((out_g, out_d, out_m, out_v), res):
            o.update(dict(zip(gnames, _unpack(arr.reshape(-1), lay))))
        return res[0]

    adam_flat("adamw_replicated", SMALL_REPLICATED, [small[n].reshape(wts[n].shape) for n in SMALL_REPLICATED])
    sliced = [lax.dynamic_slice_in_dim(small[n], me * nd, nd, axis=small[n].ndim - 1).reshape(wts[n].shape) for n in SMALL_SHARDED]
    after = adam_flat("adamw_sliced", SMALL_SHARDED, sliced)

    dmod_all = jnp.concatenate([gathered["b"][:, None], gathered["a"]], axis=1).reshape(N_DEV, 2, N_DEV, nw)
    dmod_mine = lax.dynamic_index_in_dim(dmod_all, me, axis=2, keepdims=False).transpose(1, 0, 2)
    g_ada_w = _ada_w_grad(c_all.T, dmod_mine)
    after = after[0:1, 0:1] + adam_big("adamw_ada_w", g_ada_w[None], "ada_w")[0:1, 0:1]

    land_and_update([e for e in scatters if e[0] == 0], after)
    for n, res in big_out.items():
        for o, arr in zip((out_g, out_d, out_m, out_v), res):
            o[n] = arr.reshape(wts[n].shape)
            if n in TRANSPOSED:
                o[n] = jnp.swapaxes(o[n], -1, -2)

    return (loss, grad_x[None], *[out_g[n] for n in WEIGHTS], *[out_d[n] for n in WEIGHTS],
            *[out_m[n] for n in WEIGHTS], *[out_v[n] for n in WEIGHTS])
```

```python
import functools
import math

import jax
import jax.numpy as jnp
from jax import lax
from jax.experimental import pallas as pl
from jax.experimental.pallas import tpu as pltpu

F32 = jnp.float32
BF16 = jnp.bfloat16
MESH = pl.DeviceIdType.MESH
HIGHEST = lax.Precision.HIGHEST

N_DEV = 8
D = 1024
D_FF = 2816
FSH = 2 * D_FF // N_DEV
EPS = 1e-6
POOL_WINDOWS = (2, 4, 8, 16)
HD = 128
NH = 4
SSM_G, SSM_P, SSM_N = 64, 64, 16
SSM_GB = 16
SSM_NB = SSM_G // SSM_GB
SSM_L = SSM_G * SSM_P
LR, B1, B2, ADAM_EPS, WD, STEP = 0.001, 0.9, 0.999, 1e-08, 0.01, 10
GELU_C = math.sqrt(2.0 / math.pi)
VMEM_LIMIT_BYTES = 48 * 1024 * 1024
LANE = 128


def _pc(body, name, grid, in_specs, out_specs, out_shape, scratch=()):
    return pl.pallas_call(
        body, name=name, grid=grid, in_specs=in_specs, out_specs=out_specs, out_shape=out_shape,
        scratch_shapes=list(scratch),
        compiler_params=pltpu.CompilerParams(dimension_semantics=("arbitrary",) * len(grid),
                                             vmem_limit_bytes=VMEM_LIMIT_BYTES))


def _sds(shape, dtype=F32):
    return jax.ShapeDtypeStruct(tuple(shape), dtype)


def _bf(v):
    return v if v.dtype == BF16 else v.astype(BF16)


def _row_spec(ts, width, col=0):
    return pl.BlockSpec((ts, width), lambda t, _c=col: (t, _c))


def _vec_spec(width, col=0):
    return pl.BlockSpec((1, width), lambda t, _c=col: (0, _c))


def _mm(name, a, b, contract, grid, a_spec, b_spec, o_spec, out_shape, acc_axis=None, after=None):
    dn = (contract, ((), ()))

    def body(a_ref, b_ref, *rest):
        o_ref = rest[-1]
        r = lax.dot_general(_bf(a_ref[...]), _bf(b_ref[...]), dn, preferred_element_type=F32)
        if acc_axis is None:
            o_ref[...] = r.astype(o_ref.dtype)
        else:
            k = pl.program_id(acc_axis)

            @pl.when(k == 0)
            def _():
                o_ref[...] = r

            @pl.when(k > 0)
            def _():
                o_ref[...] += r

    if after is None:
        return _pc(body, name, grid, [a_spec, b_spec], o_spec, out_shape)(a, b)
    return _pc(body, name, grid, [a_spec, b_spec, pl.BlockSpec(memory_space=pl.ANY)], o_spec, out_shape)(a, b, after)


def _tile(s):
    return min(s, 1024)


def _div_tile(n, cap=1024):
    t = min(n, cap) // LANE * LANE
    while n % t:
        t -= LANE
    return t


def _mm_nn(name, a, b, out_dtype=F32):
    s, k = a.shape
    n = b.shape[1]
    ts, tn = _tile(s), _div_tile(n)
    return _mm(name, a, b, ((1,), (0,)), (n // tn, s // ts),
               pl.BlockSpec((ts, k), lambda j, t: (t, 0)), pl.BlockSpec((k, tn), lambda j, t: (0, j)),
               pl.BlockSpec((ts, tn), lambda j, t: (t, j)), _sds((s, n), out_dtype))


def _mm_nt(name, a, b, out_dtype=F32):
    s, n = a.shape
    k = b.shape[0]
    ts, tk = _tile(s), _div_tile(k)
    return _mm(name, a, b, ((1,), (1,)), (k // tk, s // ts),
               pl.BlockSpec((ts, n), lambda j, t: (t, 0)), pl.BlockSpec((tk, n), lambda j, t: (j, 0)),
               pl.BlockSpec((ts, tk), lambda j, t: (t, j)), _sds((s, k), out_dtype))


def _mm_tn(name, a, b, out_dtype=F32, tm=512, tn=512):
    s, m = a.shape
    n = b.shape[1]
    tm, tn = min(m, tm), min(n, tn)
    return _mm(name, a, b, ((0,), (0,)), (m // tm, n // tn),
               pl.BlockSpec((s, tm), lambda i, j: (0, i)), pl.BlockSpec((s, tn), lambda i, j: (0, j)),
               pl.BlockSpec((tm, tn), lambda i, j: (i, j)), _sds((m, n), out_dtype))


def _rstd(v):
    return lax.rsqrt(jnp.mean(v * v, axis=-1, keepdims=True) + EPS)


def _prenorm_fwd(x, g, scale, shift):
    s = x.shape[0]
    ts = min(s, 512)

    def body(x_ref, g_ref, sc_ref, sh_ref, h_ref):
        xv = x_ref[...]
        h_ref[...] = ((xv * _rstd(xv) * g_ref[...]) * (1.0 + sc_ref[...]) + sh_ref[...]).astype(BF16)

    return _pc(body, "prenorm_fwd", (s // ts,), [_row_spec(ts, D)] + [_vec_spec(D)] * 3, _row_spec(ts, D),
               _sds((s, D), BF16))(x, g, scale, shift)


def _postnorm_fwd(x, f, g, gate, rw):
    s = x.shape[0]
    ts = min(s, 512)

    def body(x_ref, f_ref, g_ref, gt_ref, o_ref):
        fv = f_ref[...]
        o_ref[...] = x_ref[...] + (rw * gt_ref[...]) * (fv * _rstd(fv) * g_ref[...])

    return _pc(body, "postnorm_fwd", (s // ts,), [_row_spec(ts, D)] * 2 + [_vec_spec(D)] * 2, _row_spec(ts, D),
               _sds((s, D)))(x, f, g, gate)


def _acc(ref, first, v):
    @pl.when(first)
    def _():
        ref[...] = v

    @pl.when(jnp.logical_not(first))
    def _():
        ref[...] += v


def _colsum(v):
    return jnp.sum(v, axis=0, keepdims=True)


def _postnorm_bwd(dout, f, g, gate, rw):
    s = dout.shape[0]
    ts = min(s, 512)

    def body(do_ref, f_ref, g_ref, gt_ref, df_ref, dgate_ref, dg_ref):
        first = pl.program_id(0) == 0
        do, fv, gv = do_ref[...], f_ref[...], g_ref[...]
        r = _rstd(fv)
        fn = fv * r
        _acc(dgate_ref, first, rw * _colsum(do * (fn * gv)))
        dy = (rw * gt_ref[...]) * do
        _acc(dg_ref, first, _colsum(dy * fn))
        dfn = dy * gv
        df_ref[...] = (r * (dfn - fn * jnp.mean(dfn * fn, axis=-1, keepdims=True))).astype(BF16)

    return _pc(body, "postnorm_bwd", (s // ts,), [_row_spec(ts, D)] * 2 + [_vec_spec(D)] * 2,
               [_row_spec(ts, D), _vec_spec(D), _vec_spec(D)],
               [_sds((s, D), BF16), _sds((1, D)), _sds((1, D))])(dout, f, g, gate)


def _prenorm_bwd(dout, dh, x, g, scale):
    s = dout.shape[0]
    ts = min(s, 512)

    def body(do_ref, dh_ref, x_ref, g_ref, sc_ref, dx_ref, dsh_ref, dsc_ref, dg_ref):
        first = pl.program_id(0) == 0
        dhv, xv, gv = dh_ref[...], x_ref[...], g_ref[...]
        r = _rstd(xv)
        xn = xv * r
        _acc(dsh_ref, first, _colsum(dhv))
        _acc(dsc_ref, first, _colsum(dhv * (xn * gv)))
        dhp = dhv * (1.0 + sc_ref[...])
        _acc(dg_ref, first, _colsum(dhp * xn))
        dxn = dhp * gv
        dx_ref[...] = do_ref[...] + r * (dxn - xn * jnp.mean(dxn * xn, axis=-1, keepdims=True))

    return _pc(body, "prenorm_bwd", (s // ts,), [_row_spec(ts, D)] * 3 + [_vec_spec(D)] * 2,
               [_row_spec(ts, D)] + [_vec_spec(D)] * 3,
               [_sds((s, D))] + [_sds((1, D))] * 3)(dout, dh, x, g, scale)


def _loss_fwd_bwd(y, tgt):
    s = y.shape[0]
    ts = min(s, 512)
    nt = s // ts

    def body(y_ref, t_ref, loss_ref, dy_ref, acc_ref):
        t = pl.program_id(0)
        e = y_ref[...] - t_ref[...]
        dy_ref[...] = e * (1.0 / D)
        _acc(acc_ref, t == 0, _colsum(e * e))

        @pl.when(t == nt - 1)
        def _():
            loss_ref[...] = jnp.full((1, LANE), 0.5 / D, F32) * jnp.sum(acc_ref[...])

    return _pc(body, "loss", (nt,), [_row_spec(ts, D)] * 2,
               [pl.BlockSpec((1, LANE), lambda t: (0, 0)), _row_spec(ts, D)],
               [_sds((1, LANE)), _sds((s, D))], scratch=[pltpu.VMEM((1, D), F32)])(y, tgt)


def _sigmoid(v):
    return 1.0 / (1.0 + jnp.exp(-v))


def _swiglu_fwd(z):
    _, s, _ = z.shape
    ts = min(s, 512)
    z4 = z.reshape(2, 4, s, FSH)

    def body(z_ref, o_ref):
        a, b = z_ref[0], z_ref[1]
        o_ref[...] = (a * _sigmoid(a) * b).astype(BF16)

    return _pc(body, "swiglu_fwd", (4, s // ts), [pl.BlockSpec((2, None, ts, FSH), lambda k, t: (0, k, t, 0))],
               pl.BlockSpec((None, ts, FSH), lambda k, t: (k, t, 0)), _sds((4, s, FSH), BF16))(z4)


def _swiglu_bwd(z, dact, after):
    _, s, _ = z.shape
    ts = min(s, 512)
    z4 = z.reshape(2, 4, s, FSH)

    def body(z_ref, d_ref, after_ref, o_ref):
        a, b, d = z_ref[0], z_ref[1], d_ref[...]
        sg = _sigmoid(a)
        o_ref[0] = (d * b * (sg * (1.0 + a * (1.0 - sg)))).astype(BF16)
        o_ref[1] = (d * (a * sg)).astype(BF16)

    spec = pl.BlockSpec((2, None, ts, FSH), lambda k, t: (0, k, t, 0))
    out = _pc(body, "swiglu_bwd", (4, s // ts),
              [spec, pl.BlockSpec((None, ts, FSH), lambda k, t: (k, t, 0)), pl.BlockSpec(memory_space=pl.ANY)],
              spec, _sds((2, 4, s, FSH), BF16))(z4, dact, after)
    return out.reshape(8, s, FSH)


def _ffn_fwd(h, win, wout_of):
    s = h.shape[0]
    ts = _tile(s)
    z = _mm("ffn_in", h, win, ((1,), (1,)), (N_DEV, s // ts),
            pl.BlockSpec((ts, D), lambda j, t: (t, 0)), pl.BlockSpec((None, FSH, D), lambda j, t: (j, 0, 0)),
            pl.BlockSpec((None, ts, FSH), lambda j, t: (j, t, 0)), _sds((N_DEV, s, FSH)))
    wout = wout_of(z).reshape(4, FSH, D)
    act = _swiglu_fwd(z)
    f = _mm("ffn_out", act, wout, ((1,), (0,)), (s // ts, 4),
            pl.BlockSpec((None, ts, FSH), lambda t, k: (k, t, 0)), pl.BlockSpec((None, FSH, D), lambda t, k: (k, 0, 0)),
            pl.BlockSpec((ts, D), lambda t, k: (t, 0)), _sds((s, D)), acc_axis=1)
    return f, (h, z, act)


def _ffn_bwd(df, saved, win, wout, send):
    h, z, act = saved
    s = h.shape[0]
    ts = _tile(s)
    wout = wout.reshape(4, FSH, D)
    dact = _mm("ffn_out_dx", df, wout, ((1,), (1,)), (4, s // ts),
               pl.BlockSpec((ts, D), lambda k, t: (t, 0)), pl.BlockSpec((None, FSH, D), lambda k, t: (k, 0, 0)),
               pl.BlockSpec((None, ts, FSH), lambda k, t: (k, t, 0)), _sds((4, s, FSH)))
    dwout = _mm("ffn_out_dw", act, df, ((0,), (0,)), (4, 2),
                pl.BlockSpec((None, s, FSH), lambda k, j: (k, 0, 0)), pl.BlockSpec((s, D // 2), lambda k, j: (0, j)),
                pl.BlockSpec((None, FSH, D // 2), lambda k, j: (k, 0, j)), _sds((4, FSH, D), BF16))
    dz = _swiglu_bwd(z, dact, send("w_out", dwout.reshape(N_DEV, D_FF // N_DEV, D)))
    dwin = _mm("ffn_in_dw", dz, h, ((0,), (0,)), (N_DEV, 2),
               pl.BlockSpec((None, s, FSH), lambda j, i: (j, 0, 0)), pl.BlockSpec((s, D // 2), lambda j, i: (0, i)),
               pl.BlockSpec((None, FSH, D // 2), lambda j, i: (j, 0, i)), _sds((N_DEV, FSH, D), BF16))
    return _mm("ffn_in_dx", dz, win, ((1,), (0,)), (s // ts, N_DEV),
               pl.BlockSpec((None, ts, FSH), lambda t, j: (j, t, 0)), pl.BlockSpec((None, FSH, D), lambda t, j: (j, 0, 0)),
               pl.BlockSpec((ts, D), lambda t, j: (t, 0)), _sds((s, D)), acc_axis=1, after=send("w_in", dwin))


def _shift_rows(v, k, row, s, back):
    if back:
        return jnp.where(row < s - k, pltpu.roll(v, s - k, 0), 0.0)
    return jnp.where(row >= k, pltpu.roll(v, k, 0), 0.0)


def _window_sum(v, w, row, s, back):
    k = 1
    while k < w:
        v = v + _shift_rows(v, k, row, s, back)
        k *= 2
    return v


def _pool_fwd(z, pool_w, pool_scale):
    s = z.shape[0]

    def body(z_ref, w_ref, sc_ref, y_ref, d_ref):
        row = lax.broadcasted_iota(jnp.int32, (s, HD), 0)
        for g, w in enumerate(POOL_WINDOWS):
            sl = slice(g * HD, (g + 1) * HD)
            a = z_ref[:, sl]
            cnt = jnp.minimum(row + 1, w).astype(F32)
            d = (_window_sum(a, w, row, s, False) / cnt - a).astype(BF16)
            d_ref[:, sl] = d
            y = jnp.dot(d, _bf(w_ref[g]), preferred_element_type=F32)
            y_ref[:, sl] = (y * sc_ref[:, sl]).astype(BF16)

    return _pc(body, "pool_fwd", (1,),
               [pl.BlockSpec((s, NH * HD), lambda i: (0, 0)), pl.BlockSpec((NH, HD, HD), lambda i: (0, 0, 0)),
                pl.BlockSpec((1, NH * HD), lambda i: (0, 0))],
               [pl.BlockSpec((s, NH * HD), lambda i: (0, 0))] * 2,
               [_sds((s, NH * HD), BF16)] * 2)(z, pool_w, pool_scale)


def _pool_bwd(dy, d, pool_w, pool_scale):
    s = dy.shape[0]

    def body(dy_ref, d_ref, w_ref, sc_ref, dz_ref, dw_ref, dsc_ref):
        row = lax.broadcasted_iota(jnp.int32, (s, HD), 0)
        for g, w in enumerate(POOL_WINDOWS):
            sl = slice(g * HD, (g + 1) * HD)
            dyg, dg, wg = dy_ref[:, sl], d_ref[:, sl], _bf(w_ref[g])
            yraw = jnp.dot(dg, wg, preferred_element_type=F32)
            dsc_ref[:, sl] = _colsum(dyg * yraw)
            dyr = _bf(dyg * sc_ref[:, sl])
            dw_ref[g] = lax.dot_general(dg, dyr, (((0,), (0,)), ((), ())), preferred_element_type=F32)
            dd = lax.dot_general(dyr, wg, (((1,), (1,)), ((), ())), preferred_element_type=F32)
            cnt = jnp.minimum(row + 1, w).astype(F32)
            dz_ref[:, sl] = (_window_sum(dd / cnt, w, row, s, True) - dd).astype(BF16)

    return _pc(body, "pool_bwd", (1,),
               [pl.BlockSpec((s, NH * HD), lambda i: (0, 0)), pl.BlockSpec((s, NH * HD), lambda i: (0, 0)),
                pl.BlockSpec((NH, HD, HD), lambda i: (0, 0, 0)), pl.BlockSpec((1, NH * HD), lambda i: (0, 0))],
               [pl.BlockSpec((s, NH * HD), lambda i: (0, 0)), pl.BlockSpec((NH, HD, HD), lambda i: (0, 0, 0)),
                pl.BlockSpec((1, NH * HD), lambda i: (0, 0))],
               [_sds((s, NH * HD), BF16), _sds((NH, HD, HD)), _sds((1, NH * HD))])(dy, d, pool_w, pool_scale)


def _gelu(v):
    return 0.5 * v * (1.0 + jnp.tanh(GELU_C * (v + 0.044715 * (v * v * v))))


def _gelu_grad(v):
    t = jnp.tanh(GELU_C * (v + 0.044715 * (v * v * v)))
    return 0.5 * (1.0 + t) + 0.5 * v * (1.0 - t * t) * (GELU_C * (1.0 + 3.0 * 0.044715 * (v * v)))


def _causal_mask():
    return lax.broadcasted_iota(jnp.int32, (HD, HD), 0) >= lax.broadcasted_iota(jnp.int32, (HD, HD), 1)


def _sgu_specs():
    w = NH * HD
    return [pl.BlockSpec((HD, w), lambda c: (c, 1)), pl.BlockSpec((HD, w), lambda c: (c, 2)),
            pl.BlockSpec((1, w), lambda c: (0, 0)), pl.BlockSpec((1, w), lambda c: (0, 0)),
            pl.BlockSpec((NH, HD, HD), lambda c: (0, 0, 0)), pl.BlockSpec((HD, LANE), lambda c: (0, 0))]


def _sgu_head(v, lng_ref, lnb_ref, w_ref, h):
    sl = slice(h * HD, (h + 1) * HD)
    vh = v[:, sl]
    xc = vh - jnp.mean(vh, axis=-1, keepdims=True)
    rs = lax.rsqrt(jnp.mean(xc * xc, axis=-1, keepdims=True) + EPS)
    vhat = xc * rs
    vn = _bf(vhat * lng_ref[:, sl] + lnb_ref[:, sl])
    wc = _bf(jnp.where(_causal_mask(), w_ref[h], 0.0))
    return sl, rs, vhat, vn, wc


def _sgu_fwd(z, ln_g, ln_b, sgu_w, sgu_bt):
    s = z.shape[0]

    def body(zu_ref, zv_ref, lng_ref, lnb_ref, w_ref, bt_ref, y_ref):
        u, v = _gelu(zu_ref[...]), _gelu(zv_ref[...])
        for h in range(NH):
            sl, _, _, vn, wc = _sgu_head(v, lng_ref, lnb_ref, w_ref, h)
            sp = jnp.dot(wc, vn, preferred_element_type=F32) + bt_ref[:, h:h + 1]
            y_ref[:, sl] = (u[:, sl] * sp).astype(BF16)

    return _pc(body, "sgu_fwd", (s // HD,), _sgu_specs(), pl.BlockSpec((HD, NH * HD), lambda c: (c, 0)),
               _sds((s, NH * HD), BF16))(z, z, ln_g, ln_b, sgu_w, sgu_bt)


def _sgu_bwd(z, dy, ln_g, ln_b, sgu_w, sgu_bt, head_sum):
    s = z.shape[0]
    w = NH * HD
    nc = s // HD

    def body(zu_ref, zv_ref, lng_ref, lnb_ref, w_ref, bt_ref, dy_ref, hs_ref,
             dzu_ref, dzv_ref, dlng_ref, dlnb_ref, dw_ref, dbt_ref, dsacc_ref):
        c = pl.program_id(0)
        first = c == 0
        zu, zv = zu_ref[...], zv_ref[...]
        u, v = _gelu(zu), _gelu(zv)
        dyv = dy_ref[...]
        gu, gv = _gelu_grad(zu), _gelu_grad(zv)
        ds = dyv * u
        _acc(dsacc_ref, first, ds)
        for h in range(NH):
            sl, rs, vhat, vn, wc = _sgu_head(v, lng_ref, lnb_ref, w_ref, h)
            sp = jnp.dot(wc, vn, preferred_element_type=F32) + bt_ref[:, h:h + 1]
            dzu_ref[:, sl] = (dyv[:, sl] * sp * gu[:, sl]).astype(BF16)
            dsh = _bf(ds[:, sl])
            dwh = lax.dot_general(dsh, vn, (((1,), (1,)), ((), ())), preferred_element_type=F32)
            dwh = jnp.where(_causal_mask(), dwh, 0.0)

            @pl.when(first)
            def _():
                dw_ref[h] = dwh

            @pl.when(jnp.logical_not(first))
            def _():
                dw_ref[h] += dwh

            dvn = lax.dot_general(wc, dsh, (((0,), (0,)), ((), ())), preferred_element_type=F32)
            g_col = _colsum(dvn * vhat)
            b_col = _colsum(dvn)

            @pl.when(first)
            def _():
                dlng_ref[:, sl] = g_col
                dlnb_ref[:, sl] = b_col

            @pl.when(jnp.logical_not(first))
            def _():
                dlng_ref[:, sl] += g_col
                dlnb_ref[:, sl] += b_col

            dvh = dvn * lng_ref[:, sl]
            dv = rs * (dvh - jnp.mean(dvh, axis=-1, keepdims=True) - vhat * jnp.mean(dvh * vhat, axis=-1, keepdims=True))
            dzv_ref[:, sl] = (dv * gv[:, sl]).astype(BF16)

        @pl.when(c == nc - 1)
        def _():
            dbt_ref[...] = jnp.dot(dsacc_ref[...], hs_ref[...], preferred_element_type=F32, precision=HIGHEST)

    outs = _pc(body, "sgu_bwd", (nc,),
               _sgu_specs() + [pl.BlockSpec((HD, w), lambda c: (c, 1)), pl.BlockSpec((w, LANE), lambda c: (0, 0))],
               [pl.BlockSpec((HD, w), lambda c: (c, 0))] * 2 + [pl.BlockSpec((1, w), lambda c: (0, 0))] * 2
               + [pl.BlockSpec((NH, HD, HD), lambda c: (0, 0, 0)), pl.BlockSpec((HD, LANE), lambda c: (0, 0))],
               [_sds((s, w), BF16)] * 2 + [_sds((1, w))] * 2 + [_sds((NH, HD, HD)), _sds((HD, LANE))],
               scratch=[pltpu.VMEM((HD, w), F32)])(z, z, ln_g, ln_b, sgu_w, sgu_bt, dy, head_sum)
    return outs


def _cmul(ar, ai, br, bi):
    return ar * br - ai * bi, ar * bi + ai * br


def _ssm_prep(lam_re, lam_im, lam_re_rep, lam_im_rep, log_dt, b_re, b_im):
    def disc(lr, li, dt):
        mag = jnp.exp(lr * dt)
        return mag * jnp.cos(li * dt), mag * jnp.sin(li * dt)

    def body(lr_ref, li_ref, lrr_ref, lir_ref, ldt_ref, br_ref, bi_ref, or_ref, oi_ref, bbr_ref, bbi_ref):
        dt = jnp.exp(ldt_ref[...])
        or_ref[...], oi_ref[...] = disc(lr_ref[...], li_ref[...], dt)
        lr, li = lrr_ref[...], lir_ref[...]
        er, ei = disc(lr, li, dt)
        den = lr * lr + li * li
        kr = ((er - 1.0) * lr + ei * li) / den
        ki = (ei * lr - (er - 1.0) * li) / den
        bbr_ref[...], bbi_ref[...] = _cmul(kr, ki, br_ref[...], bi_ref[...])

    small = pl.BlockSpec((SSM_G, SSM_P), lambda i: (0, 0))
    wide = pl.BlockSpec((SSM_G, SSM_P * SSM_N), lambda i: (0, 0))
    col = pl.BlockSpec((SSM_G, 1), lambda i: (0, 0))
    return _pc(body, "ssm_prep", (1,), [small, small, wide, wide, col, wide, wide], [small, small, wide, wide],
               [_sds((SSM_G, SSM_P))] * 2 + [_sds((SSM_G, SSM_P * SSM_N))] * 2)(
        lam_re, lam_im, lam_re_rep, lam_im_rep, log_dt, b_re, b_im)


def _ssm_param_bwd(g_lam_re, g_lam_im, g_bb_re, g_bb_im, lam_re, lam_im, lam_re_rep, lam_im_rep, log_dt, b_re, b_im, seg):
    def body(glr_ref, gli_ref, gbr_ref, gbi_ref, lr_ref, li_ref, lrr_ref, lir_ref, ldt_ref, br_ref, bi_ref, seg_ref,
             dlr_ref, dli_ref, ddt_ref, dbr_ref, dbi_ref):
        dt = jnp.exp(ldt_ref[...])
        lr, li = lrr_ref[...], lir_ref[...]
        mag = jnp.exp(lr * dt)
        er, ei = mag * jnp.cos(li * dt), mag * jnp.sin(li * dt)
        den = lr * lr + li * li
        kr = ((er - 1.0) * lr + ei * li) / den
        ki = (ei * lr - (er - 1.0) * li) / den
        gbr, gbi = gbr_ref[...], gbi_ref[...]
        dbr_ref[...], dbi_ref[...] = _cmul(kr, -ki, gbr, gbi)
        tr, ti = _cmul(br_ref[...], -bi_ref[...], gbr, gbi)
        gkr = jnp.dot(tr, seg_ref[...], preferred_element_type=F32, precision=HIGHEST)
        gki = jnp.dot(ti, seg_ref[...], preferred_element_type=F32, precision=HIGHEST)
        lr, li = lr_ref[...], li_ref[...]
        mag = jnp.exp(lr * dt)
        er, ei = mag * jnp.cos(li * dt), mag * jnp.sin(li * dt)
        den = lr * lr + li * li
        ir, ii = lr / den, -li / den
        kr, ki = _cmul(er - 1.0, ei, ir, ii)
        ar, ai = _cmul(ir, -ii, gkr, gki)
        glr, gli = glr_ref[...] + ar, gli_ref[...] + ai
        qr, qi = _cmul(kr, ki, ir, ii)
        g1r, g1i = _cmul(-qr, qi, gkr, gki)
        g2r, g2i = _cmul(dt * er, -dt * ei, glr, gli)
        dlr_ref[...] = g1r + g2r
        dli_ref[...] = g1i + g2i
        wr, wi = _cmul(lr, li, er, ei)
        g_dt = jnp.sum(wr * glr + wi * gli, axis=-1, keepdims=True)
        ddt_ref[...] = jnp.broadcast_to(dt * g_dt, (SSM_G, LANE))

    small = pl.BlockSpec((SSM_G, SSM_P), lambda i: (0, 0))
    wide = pl.BlockSpec((SSM_G, SSM_P * SSM_N), lambda i: (0, 0))
    col = pl.BlockSpec((SSM_G, 1), lambda i: (0, 0))
    segs = pl.BlockSpec((SSM_P * SSM_N, SSM_P), lambda i: (0, 0))
    return _pc(body, "ssm_param_bwd", (1,), [small, small, wide, wide, small, small, wide, wide, col, wide, wide, segs],
               [small, small, pl.BlockSpec((SSM_G, LANE), lambda i: (0, 0)), wide, wide],
               [_sds((SSM_G, SSM_P))] * 2 + [_sds((SSM_G, LANE))] + [_sds((SSM_G, SSM_P * SSM_N))] * 2)(
        g_lam_re, g_lam_im, g_bb_re, g_bb_im, lam_re, lam_im, lam_re_rep, lam_im_rep, log_dt, b_re, b_im, seg)


SCAN_LANES = 512
SCAN_ROWS = 8


def _ssm_scan(b_re, b_im, lam_re, lam_im, reverse):
    s = b_re.shape[0]
    nt = s // SCAN_ROWS
    ln, rows = SCAN_LANES, SCAN_ROWS

    def body(lr_ref, li_ref, br_ref, bi_ref, or_ref, oi_ref):
        l1 = (lr_ref[...], li_ref[...])
        pw = [l1]
        for _ in range(rows - 1):
            pw.append(_cmul(*pw[-1], *l1))
        row = lax.broadcasted_iota(jnp.int32, (rows, ln), 0)
        expo = (rows - row) if reverse else (row + 1)
        pr = jnp.zeros((rows, ln), F32)
        pi = jnp.zeros((rows, ln), F32)
        for e in range(1, rows + 1):
            pr = jnp.where(expo == e, pw[e - 1][0], pr)
            pi = jnp.where(expo == e, pw[e - 1][1], pi)
        lk = {k: (jnp.broadcast_to(pw[k - 1][0], (rows, ln)), jnp.broadcast_to(pw[k - 1][1], (rows, ln))) for k in (1, 2, 4)}

        def step(i, carry):
            cr, ci = carry
            t = (nt - 1 - i) if reverse else i
            r0 = pl.multiple_of(t * rows, rows)
            xr, xi = br_ref[pl.ds(r0, rows), :], bi_ref[pl.ds(r0, rows), :]
            for k in (1, 2, 4):
                sr = _shift_rows(xr, k, row, rows, reverse)
                si = _shift_rows(xi, k, row, rows, reverse)
                ar, ai = _cmul(lk[k][0], lk[k][1], sr, si)
                xr, xi = xr + ar, xi + ai
            ar, ai = _cmul(pr, pi, cr, ci)
            xr, xi = xr + ar, xi + ai
            or_ref[pl.ds(r0, rows), :] = xr
            oi_ref[pl.ds(r0, rows), :] = xi
            if reverse:
                return xr[0:1], xi[0:1]
            return xr[rows - 1:rows], xi[rows - 1:rows]

        zero = jnp.zeros((1, ln), F32)
        lax.fori_loop(0, nt, step, (zero, zero))

    vec = pl.BlockSpec((1, ln), lambda j: (0, j))
    blk = pl.BlockSpec((s, ln), lambda j: (0, j))
    return _pc(body, "ssm_scan_bwd" if reverse else "ssm_scan_fwd", (SSM_L // ln,), [vec, vec, blk, blk], [blk, blk],
               [_sds((s, SSM_L))] * 2)(lam_re, lam_im, b_re, b_im)


def _ssm_in(name, v, w_bd):
    s = v.shape[0]
    ts = _tile(s)
    half = SSM_GB * SSM_P

    def body(v_ref, w_ref, or_ref, oi_ref):
        r = jnp.dot(_bf(v_ref[...]), w_ref[...], preferred_element_type=F32)
        or_ref[...] = r[:, :half]
        oi_ref[...] = r[:, half:]

    out = pl.BlockSpec((ts, half), lambda q, t: (t, q))
    return _pc(body, name, (SSM_NB, s // ts),
               [pl.BlockSpec((ts, SSM_GB * SSM_N), lambda q, t: (t, q)), pl.BlockSpec((None, SSM_GB * SSM_N, 2 * half), lambda q, t: (q, 0, 0))],
               [out, out], [_sds((s, SSM_L))] * 2)(v, w_bd)


def _ssm_out(name, x_re, x_im, w_bd):
    s = x_re.shape[0]
    ts = _tile(s)
    half = SSM_GB * SSM_P
    nt = (((1,), (1,)), ((), ()))

    def body(xr_ref, xi_ref, w_ref, o_ref):
        w = w_ref[...]
        o_ref[...] = (lax.dot_general(_bf(xr_ref[...]), w[:, :half], nt, preferred_element_type=F32)
                      + lax.dot_general(_bf(xi_ref[...]), w[:, half:], nt, preferred_element_type=F32))

    xin = pl.BlockSpec((ts, half), lambda q, t: (t, q))
    return _pc(body, name, (SSM_NB, s // ts),
               [xin, xin, pl.BlockSpec((None, SSM_GB * SSM_N, 2 * half), lambda q, t: (q, 0, 0))],
               pl.BlockSpec((ts, SSM_GB * SSM_N), lambda q, t: (t, q)), _sds((s, SSM_G * SSM_N)))(x_re, x_im, w_bd)


def _ssm_outer(name, v, x_re, x_im):
    s = v.shape[0]
    ts = min(s, 512)
    nt = s // ts
    half = SSM_GB * SSM_P
    rows = SSM_GB * SSM_N
    tn = (((0,), (0,)), ((), ()))

    def body(v_ref, xr_ref, xi_ref, or_ref, oi_ref, acc_ref):
        vv = _bf(v_ref[...])
        pr = lax.dot_general(vv, _bf(xr_ref[...]), tn, preferred_element_type=F32)
        pi = lax.dot_general(vv, _bf(xi_ref[...]), tn, preferred_element_type=F32)
        t = pl.program_id(1)

        @pl.when(t == 0)
        def _():
            acc_ref[:, :half] = pr
            acc_ref[:, half:] = pi

        @pl.when(t > 0)
        def _():
            acc_ref[:, :half] += pr
            acc_ref[:, half:] += pi

        @pl.when(t == nt - 1)
        def _():
            row_g = lax.broadcasted_iota(jnp.int32, (rows, LANE), 0) // SSM_N
            lane_g = lax.broadcasted_iota(jnp.int32, (rows, LANE), 1) // SSM_P
            for part, o_ref in enumerate((or_ref, oi_ref)):
                fold = jnp.zeros((rows, LANE), F32)
                for cb in range(half // LANE):
                    blk = acc_ref[:, part * half + cb * LANE:part * half + (cb + 1) * LANE]
                    fold = fold + jnp.where(2 * cb + lane_g == row_g, blk, 0.0)
                o_ref[...] = jnp.where(row_g % 2 == 0, fold, pltpu.roll(fold, SSM_P, 1))

    xin = pl.BlockSpec((ts, half), lambda q, t: (t, q))
    out = pl.BlockSpec((None, rows, LANE), lambda q, t: (q, 0, 0))
    return _pc(body, name, (SSM_NB, nt), [pl.BlockSpec((ts, rows), lambda q, t: (t, q)), xin, xin], [out, out],
               [_sds((SSM_NB, rows, LANE))] * 2, scratch=[pltpu.VMEM((rows, 2 * half), F32)])(v, x_re, x_im)


def _ssm_dlam(x_re, x_im, a_re, a_im):
    s = x_re.shape[0]
    ln = SCAN_LANES

    def body(xr_ref, xi_ref, ar_ref, ai_ref, or_ref, oi_ref):
        row = lax.broadcasted_iota(jnp.int32, (s, ln), 0)
        xr = _shift_rows(xr_ref[...], 1, row, s, False)
        xi = _shift_rows(xi_ref[...], 1, row, s, False)
        ar, ai = ar_ref[...], ai_ref[...]
        or_ref[...] = _colsum(xr * ar + xi * ai)
        oi_ref[...] = _colsum(xr * ai - xi * ar)

    blk = pl.BlockSpec((s, ln), lambda j: (0, j))
    vec = pl.BlockSpec((1, ln), lambda j: (0, j))
    return _pc(body, "ssm_dlam", (SSM_L // ln,), [blk] * 4, [vec, vec], [_sds((1, SSM_L))] * 2)(x_re, x_im, a_re, a_im)


def _ssm_act_fwd(y, u, d_skip):
    s = y.shape[0]
    ts = min(s, 512)

    def body(y_ref, u_ref, d_ref, o_ref):
        o_ref[...] = _gelu(y_ref[...] + d_ref[...] * u_ref[...]).astype(BF16)

    return _pc(body, "ssm_act_fwd", (s // ts,), [_row_spec(ts, D)] * 2 + [_vec_spec(D)], _row_spec(ts, D),
               _sds((s, D), BF16))(y, u, d_skip)


def _ssm_act_bwd(dg, y, u, d_skip):
    s = y.shape[0]
    ts = min(s, 512)

    def body(dg_ref, y_ref, u_ref, d_ref, dy_ref, dd_ref):
        uv = u_ref[...]
        dy = dg_ref[...] * _gelu_grad(y_ref[...] + d_ref[...] * uv)
        dy_ref[...] = dy.astype(BF16)
        _acc(dd_ref, pl.program_id(0) == 0, _colsum(dy * uv))

    return _pc(body, "ssm_act_bwd", (s // ts,), [_row_spec(ts, D)] * 3 + [_vec_spec(D)], [_row_spec(ts, D), _vec_spec(D)],
               [_sds((s, D), BF16), _sds((1, D))])(dg, y, u, d_skip)


def _axpy(a, b, d_skip):
    s = a.shape[0]
    ts = min(s, 512)

    def body(a_ref, b_ref, d_ref, o_ref):
        o_ref[...] = (a_ref[...] + d_ref[...] * b_ref[...].astype(F32)).astype(BF16)

    return _pc(body, "ssm_du", (s // ts,), [_row_spec(ts, D)] * 2 + [_vec_spec(D)], _row_spec(ts, D),
               _sds((s, D), BF16))(a, b, d_skip)


def _glu_fwd(zz):
    s = zz.shape[0]
    ts = min(s, 512)

    def body(a_ref, b_ref, o_ref):
        o_ref[...] = a_ref[...] * _sigmoid(b_ref[...])

    return _pc(body, "glu_fwd", (s // ts,), [_row_spec(ts, D, 0), _row_spec(ts, D, 1)], _row_spec(ts, D), _sds((s, D)))(zz, zz)


def _glu_bwd(zz, df):
    s = zz.shape[0]
    ts = min(s, 512)

    def body(a_ref, b_ref, df_ref, o_ref):
        sg = _sigmoid(b_ref[...])
        dfv = df_ref[...].astype(F32)
        o_ref[:, :D] = (dfv * sg).astype(BF16)
        o_ref[:, D:] = (dfv * a_ref[...] * sg * (1.0 - sg)).astype(BF16)

    return _pc(body, "glu_bwd", (s // ts,), [_row_spec(ts, D, 0), _row_spec(ts, D, 1), _row_spec(ts, D)],
               _row_spec(ts, 2 * D), _sds((s, 2 * D), BF16))(zz, zz, df)


def _ssm_block_diag(m_re, m_im):
    rows, half = SSM_GB * SSM_N, SSM_GB * SSM_P
    expand = jnp.tile(jnp.eye(SSM_P, dtype=BF16), (1, SSM_GB))

    def body(mr_ref, mi_ref, e_ref, o_ref):
        keep = (lax.broadcasted_iota(jnp.int32, (rows, half), 0) // SSM_N
                == lax.broadcasted_iota(jnp.int32, (rows, half), 1) // SSM_P)
        for part, m_ref in enumerate((mr_ref, mi_ref)):
            t = jnp.dot(_bf(m_ref[...]), e_ref[...], preferred_element_type=F32)
            o_ref[:, part * half:(part + 1) * half] = jnp.where(keep, t, 0.0).astype(BF16)

    blk = pl.BlockSpec((rows, SSM_P), lambda q: (q, 0))
    return _pc(body, "ssm_block_diag", (SSM_NB,), [blk, blk, pl.BlockSpec((SSM_P, half), lambda q: (0, 0))],
               pl.BlockSpec((None, rows, 2 * half), lambda q: (q, 0, 0)), _sds((SSM_NB, rows, 2 * half), BF16))(m_re, m_im, expand)


def _mod_part(c_all, ada_w):
    n = ada_w.shape[-1]

    def body(c_ref, w_ref, o_ref):
        cv = c_ref[...]
        cond = _bf(cv * _sigmoid(cv))
        o_ref[...] = jnp.dot(cond, _bf(w_ref[...]), preferred_element_type=F32)

    return _pc(body, "mod_part", (2,), [pl.BlockSpec((N_DEV, D), lambda l: (0, 0)), pl.BlockSpec((None, D, n), lambda l: (l, 0, 0))],
               pl.BlockSpec((None, N_DEV, n), lambda l: (l, 0, 0)), _sds((2, N_DEV, n)))(c_all, ada_w)


def _ada_w_grad(c_all_t, dmod):
    n = dmod.shape[-1]
    tr = 128

    def body(c_ref, d_ref, o_ref):
        cv = c_ref[...]
        cond = _bf(cv * _sigmoid(cv)).astype(F32)
        dm = _bf(d_ref[...]).astype(F32)
        acc = cond[:, 0:1] * dm[0:1, :]
        for b in range(1, N_DEV):
            acc = acc + cond[:, b:b + 1] * dm[b:b + 1, :]
        o_ref[...] = acc

    return _pc(body, "ada_w_grad", (2, D // tr),
               [pl.BlockSpec((tr, N_DEV), lambda l, t: (t, 0)), pl.BlockSpec((None, N_DEV, n), lambda l, t: (l, 0, 0))],
               pl.BlockSpec((None, tr, n), lambda l, t: (l, t, 0)), _sds((2, D, n)))(c_all_t, dmod)


def _adamw(name, parts, w, m, v, slot=0, prev=None):
    p, r, c = parts.shape
    tr = r
    while tr * c * 4 > (1 << 20) and tr % 16 == 0:
        tr //= 2
    nt = r // tr

    def body(p_ref, w_ref, m_ref, v_ref, *rest):
        g_ref, d_ref, nm_ref, nv_ref = rest[-4:]
        g = p_ref[0].astype(F32)
        for i in range(1, p):
            g = g + p_ref[i].astype(F32)
        g_ref[...] = g
        m2 = B1 * m_ref[...] + (1.0 - B1) * g
        v2 = B2 * v_ref[...] + (1.0 - B2) * (g * g)
        nm_ref[...] = m2
        nv_ref[...] = v2
        m_hat = m2 / (1.0 - B1 ** STEP)
        v_hat = v2 / (1.0 - B2 ** STEP)
        d_ref[...] = -LR * (m_hat / (jnp.sqrt(v_hat) + ADAM_EPS) + WD * w_ref[...])

    blk = pl.BlockSpec((tr, c), lambda t: (slot * nt + t, 0))
    in_specs = [pl.BlockSpec((p, tr, c), lambda t: (0, t, 0)), blk, blk, blk]
    if prev is None:
        return _pc(body, name, (nt,), in_specs, [blk] * 4, [_sds(w.shape)] * 4)(parts, w, m, v)
    return pl.pallas_call(
        body, name=name, grid=(nt,), in_specs=in_specs + [pl.BlockSpec(memory_space=pl.ANY)] * 4, out_specs=[blk] * 4,
        out_shape=[_sds(w.shape)] * 4, input_output_aliases={4 + i: i for i in range(4)},
        compiler_params=pltpu.CompilerParams(dimension_semantics=("arbitrary",), vmem_limit_bytes=VMEM_LIMIT_BYTES))(parts, w, m, v, *prev)


def _sum_parts(parts):
    p, r, c = parts.shape
    tr = r
    while tr * c * 4 > (1 << 19) and tr % 16 == 0:
        tr //= 2

    def body(p_ref, o_ref):
        g = p_ref[0]
        for i in range(1, p):
            g = g + p_ref[i]
        o_ref[...] = g

    return _pc(body, "sum_parts", (r // tr,), [pl.BlockSpec((p, tr, c), lambda t: (0, t, 0))], pl.BlockSpec((tr, c), lambda t: (t, 0)),
               _sds((r, c)))(parts)


def _place():
    x, y, c = lax.axis_index("x"), lax.axis_index("y"), lax.axis_index("c")
    peers = []
    for k in range(1, N_DEV):
        px = (1 - x) if k & 4 else x
        py = (1 - y) if k & 2 else y
        pc = (1 - c) if k & 1 else c
        peers.append(((px, py, pc), 4 * px + 2 * py + pc))
    return 4 * x + 2 * y + c, peers


def _at(ref, idx):
    return ref if idx is None else ref.at[idx]


def _exchange_copies(plan, n, src_refs, dst_refs, send_sems, recv_sems, local_sems=None, with_arrivals=True):
    me, peers = _place()
    local = [] if local_sems is None else [
        pltpu.make_async_copy(_at(src_refs[si], sx), _at(dst_refs[di], dx), local_sems.at[i])
        for i, (si, sx, di, dx) in enumerate(plan(me, me, 0))]

    def remote(k, i, dev, entry):
        si, sx, di, dx = entry
        return pltpu.make_async_remote_copy(_at(src_refs[si], sx), _at(dst_refs[di], dx), send_sems.at[k * n + i], recv_sems.at[k * n + i],
                                            device_id=dev, device_id_type=MESH)

    sends = [remote(k, i, dev, e) for k, (dev, peer) in enumerate(peers) for i, e in enumerate(plan(me, peer, k + 1))]
    if not with_arrivals:
        return local, sends, []
    arrivals = [remote(k, i, dev, e) for k, (dev, peer) in enumerate(peers) for i, e in enumerate(plan(peer, me, k + 1))]
    return local, sends, arrivals


def _sem_shapes(n_copies, local=True):
    sems = [pltpu.SemaphoreType.DMA(((N_DEV - 1) * n_copies,)), pltpu.SemaphoreType.DMA(((N_DEV - 1) * n_copies,))]
    return sems + [pltpu.SemaphoreType.DMA((n_copies,))] if local else sems


def _exchange(name, srcs, dst_shapes, plan, n_copies):
    ns, nd = len(srcs), len(dst_shapes)

    def body(*refs):
        local, sends, arrivals = _exchange_copies(plan, n_copies, refs[:ns], refs[ns:ns + nd], *refs[ns + nd:])
        for cp in local + sends:
            cp.start()
        for cp in arrivals:
            cp.wait_recv()
        for cp in sends:
            cp.wait_send()
        for cp in local:
            cp.wait()

    any_spec = pl.BlockSpec(memory_space=pl.ANY)
    return pl.pallas_call(
        body, name=name, in_specs=[any_spec] * ns, out_specs=[any_spec] * nd, out_shape=list(dst_shapes),
        scratch_shapes=_sem_shapes(n_copies))(*srcs)


HBM_SPEC = pl.BlockSpec(memory_space=pltpu.HBM)
SEM_SPEC = pl.BlockSpec(memory_space=pltpu.SEMAPHORE)
ANY_SPEC = pl.BlockSpec(memory_space=pl.ANY)
TOKEN_SPEC = pl.BlockSpec(memory_space=pltpu.VMEM)
SIDE_EFFECT = pltpu.SideEffectType.DATAFLOW_SIDE_EFFECTING


def _wait_all(local, sends, arrivals):
    for cp in arrivals:
        cp.wait_recv()
    for cp in sends:
        cp.wait_send()
    for cp in local:
        cp.wait()


def _exchange_start(name, srcs, dst_shapes, plan, n_copies, order):
    ns, nd = len(srcs), len(dst_shapes)
    nb = ns + nd

    def body(*refs):
        local, sends, _ = _exchange_copies(plan, n_copies, refs[:ns], refs[ns:nb], *refs[nb + 1:nb + 4], with_arrivals=False)
        for cp in local + sends:
            cp.start()
        refs[-1][...] = jnp.zeros((8, LANE), F32)

    lands = [pltpu.with_memory_space_constraint(lax.empty(d.shape, d.dtype), pltpu.HBM) for d in dst_shapes]
    srcs = [pltpu.with_memory_space_constraint(a, pltpu.HBM) for a in srcs]
    bufs = srcs + lands
    out = pl.pallas_call(
        body, name=name, in_specs=[HBM_SPEC] * nb + [ANY_SPEC],
        out_specs=[SEM_SPEC] * 3 + [HBM_SPEC] * nb + [TOKEN_SPEC],
        out_shape=_sem_shapes(n_copies) + [pltpu.HBM(a.shape, a.dtype) for a in bufs] + [_sds((8, LANE))],
        input_output_aliases={i: 3 + i for i in range(nb)},
        compiler_params=pltpu.CompilerParams(has_side_effects=SIDE_EFFECT))(*bufs, order)
    return out[:3], out[3:3 + ns], out[3 + ns:3 + nb], out[-1]


def _exchange_relay(name, sems, srcs, lands, plan, n_copies, plan2, n_copies2, after):
    ns, nd = len(srcs), len(lands)
    nb = ns + nd

    def body(*refs):
        land_refs = refs[ns:nb]
        _wait_all(*_exchange_copies(plan, n_copies, refs[:ns], land_refs, *refs[nb:nb + 3]))
        _, sends, _ = _exchange_copies(plan2, n_copies2, land_refs, land_refs, *refs[nb + 4:nb + 6], with_arrivals=False)
        for cp in sends:
            cp.start()
        refs[-1][...] = jnp.zeros((8, LANE), F32)

    out = pl.pallas_call(
        body, name=name, in_specs=[HBM_SPEC] * nb + [SEM_SPEC] * 3 + [ANY_SPEC],
        out_specs=[SEM_SPEC] * 2 + [HBM_SPEC] * nd + [TOKEN_SPEC],
        out_shape=_sem_shapes(n_copies2, local=False) + [pltpu.HBM(a.shape, a.dtype) for a in lands] + [_sds((8, LANE))],
        input_output_aliases={ns + i: 2 + i for i in range(nd)},
        compiler_params=pltpu.CompilerParams(has_side_effects=SIDE_EFFECT))(*srcs, *lands, *sems, after)
    return out[:2], out[2:2 + nd], out[-1]


def _exchange_wait(name, sems, srcs, lands, plan, n_copies, after):
    srcs = [] if srcs is None else list(srcs)
    ns, nd = len(srcs), len(lands)
    nb = ns + nd

    def body(*refs):
        land_refs = refs[ns:nb]
        _wait_all(*_exchange_copies(plan, n_copies, refs[:ns] if ns else land_refs, land_refs, *refs[nb:nb + len(sems)]))

    bufs = srcs + list(lands)
    out = pl.pallas_call(
        body, name=name, in_specs=[HBM_SPEC] * nb + [SEM_SPEC] * len(sems) + [ANY_SPEC],
        out_specs=[HBM_SPEC] * nb, out_shape=[pltpu.HBM(a.shape, a.dtype) for a in bufs],
        input_output_aliases={i: i for i in range(nb)},
        compiler_params=pltpu.CompilerParams(has_side_effects=SIDE_EFFECT))(*bufs, *sems, after)
    return out[ns:]


def _all_gather(name, arrs):
    plan = lambda me, peer, k: [(i, None, i, me) for i in range(len(arrs))]
    return _exchange(name, arrs, [_sds((N_DEV,) + a.shape, a.dtype) for a in arrs], plan, len(arrs))


def _sublayer_fwd(x, fn, mod3, g_pre, g_post, rw):
    h = _prenorm_fwd(x, g_pre, mod3[1:2], mod3[0:1])
    f, saved = fn(h)
    return _postnorm_fwd(x, f, g_post, mod3[2:3], rw), (x, f, saved)


def _sublayer_bwd(dout, saved, fn_bwd, mod3, g_pre, g_post, rw):
    x, f, inner = saved
    df, dgate, dg_post = _postnorm_bwd(dout, f, g_post, mod3[2:3], rw)
    dh, extra = fn_bwd(df, inner)
    dx, dshift, dscale, dg_pre = _prenorm_bwd(dout, dh, x, g_pre, mod3[1:2])
    return dx, jnp.concatenate([dshift, dscale, dgate], axis=0), dg_pre, dg_post, extra


def _mix0_fwd(h, p):
    z = _mm_nt("mix0_in", h, p["ab_w_in"])
    y_a, d = _pool_fwd(z, p["pool_w"], p["pool_scale"])
    y_b = _sgu_fwd(z, p["sgu_ln_g"], p["sgu_ln_b"], p["sgu_w"], p["sgu_bt"])
    ycat = jnp.concatenate([y_a, y_b], axis=1)
    return _mm_nn("mix0_out", ycat, p["ab_w_out"]), (h, z, d, ycat)


def _mix0_bwd(df, saved, p):
    h, z, d, ycat = saved
    dycat = _mm_nt("mix0_out_dx", df, p["ab_w_out"])
    g = {"ab_w_out": _mm_tn("mix0_out_dw", ycat, df, BF16)}
    dz_p, g["pool_w"], g["pool_scale"] = _pool_bwd(dycat, d, p["pool_w"], p["pool_scale"])
    dz_u, dz_v, g["sgu_ln_g"], g["sgu_ln_b"], g["sgu_w"], dbt = _sgu_bwd(
        z, dycat, p["sgu_ln_g"], p["sgu_ln_b"], p["sgu_w"], p["sgu_bt"], p["head_sum"])
    g["sgu_b"] = dbt[:, :NH].T
    dz = jnp.concatenate([dz_p, dz_u, dz_v], axis=1)
    g["ab_w_in"] = _mm_tn("mix0_in_dw", dz, h, BF16)
    return _mm_nn("mix0_in_dx", dz, p["ab_w_in"]), g


def _mix1_fwd(h, p):
    u = _mm_nn("ssm_w_in", h, p["ssm_w_in"])
    bu_re, bu_im = _ssm_in("ssm_bu", u, p["wb_bd"])
    x_re, x_im = _ssm_scan(bu_re, bu_im, p["lam_bar_re"], p["lam_bar_im"], False)
    y = _ssm_out("ssm_y", x_re, x_im, p["wc_bd"])
    g = _ssm_act_fwd(y, u, p["ssm_d"])
    zz = _mm_nn("ssm_glu", g, p["ssm_w_glu"])
    return _glu_fwd(zz), (h, u, x_re, x_im, y, g, zz)


def _mix1_bwd(df, saved, p):
    h, u, x_re, x_im, y, g, zz = saved
    gr = {}
    dzz = _glu_bwd(zz, df)
    dg = _mm_nt("ssm_glu_dx", dzz, p["ssm_w_glu"])
    gr["ssm_w_glu"] = _mm_tn("ssm_glu_dw", g, dzz, BF16)
    dy, gr["ssm_d"] = _ssm_act_bwd(dg, y, u, p["ssm_d"])
    gx_re, gx_im = _ssm_in("ssm_gx", dy, p["wct_bd"])
    a_re, a_im = _ssm_scan(gx_re, gx_im, p["lam_bar_re"], -p["lam_bar_im"], True)
    du_ssm = _ssm_out("ssm_du_mm", a_re, a_im, p["wbt_bd"])
    du = _axpy(du_ssm, dy, p["ssm_d"])
    gr["ssm_w_in"] = _mm_tn("ssm_w_in_dw", h, du, BF16)
    dh = _mm_nt("ssm_w_in_dx", du, p["ssm_w_in"])
    g_lam_re, g_lam_im = _ssm_dlam(x_re, x_im, a_re, a_im)
    mb_re, mb_im = _ssm_outer("ssm_db", u, a_re, a_im)
    mc_re, mc_im = _ssm_outer("ssm_dc", dy, x_re, x_im)
    per_group = lambda m: m[:, :, :SSM_P].reshape(SSM_G, SSM_N, SSM_P)
    gr["ssm_c_re"] = per_group(mc_re)
    gr["ssm_c_im"] = -per_group(mc_im)
    dlr, dli, ddt, dbr, dbi = _ssm_param_bwd(
        g_lam_re.reshape(SSM_G, SSM_P), g_lam_im.reshape(SSM_G, SSM_P),
        per_group(mb_re).reshape(SSM_G, SSM_N * SSM_P), per_group(mb_im).reshape(SSM_G, SSM_N * SSM_P),
        p["lam_re"], p["lam_im"], p["lam_re_rep"], p["lam_im_rep"], p["log_dt"], p["b_re"], p["b_im"], p["seg"])
    gr["ssm_lam_re"], gr["ssm_lam_im"], gr["ssm_log_dt"] = dlr, dli, ddt[:, 0]
    gr["ssm_b_re"] = dbr.reshape(SSM_G, SSM_N, SSM_P).transpose(0, 2, 1)
    gr["ssm_b_im"] = dbi.reshape(SSM_G, SSM_N, SSM_P).transpose(0, 2, 1)
    return dh, gr


def _ssm_params(lam_re, lam_im, b_re, b_im, c_re, c_im, log_dt):
    wide = lambda b: b.transpose(0, 2, 1).reshape(SSM_G, SSM_N * SSM_P)
    p = {"lam_re": lam_re, "lam_im": lam_im, "log_dt": log_dt.reshape(SSM_G, 1),
         "lam_re_rep": jnp.tile(lam_re, (1, SSM_N)), "lam_im_rep": jnp.tile(lam_im, (1, SSM_N)), "b_re": wide(b_re), "b_im": wide(b_im)}
    lbr, lbi, bbr, bbi = _ssm_prep(lam_re, lam_im, p["lam_re_rep"], p["lam_im_rep"], p["log_dt"], p["b_re"], p["b_im"])
    p["lam_bar_re"], p["lam_bar_im"] = lbr.reshape(1, SSM_L), lbi.reshape(1, SSM_L)
    rows = lambda m: m.reshape(SSM_G * SSM_N, SSM_P)
    p["wb_bd"] = p["wbt_bd"] = _ssm_block_diag(rows(bbr), rows(bbi))
    p["wc_bd"] = p["wct_bd"] = _ssm_block_diag(rows(c_re), rows(-c_im))
    p["seg"] = jnp.tile(jnp.eye(SSM_P, dtype=F32), (SSM_N, 1))
    return p


RES_WEIGHT = (0.5, 1.0, 0.5)


def _local_step(x, tgt, mod, norm_pre, norm_post, weights_of, on_part, on_grads):
    def fns(i, w):
        if i % 3 != 1:
            win, wout_of = w
            return ((lambda h: _ffn_fwd(h, win, wout_of)),
                    (lambda df, sv: (_ffn_bwd(df, sv, win, wout_of(None), lambda tag, part: on_part(i, tag, part)), None)))
        if i == 1:
            return (lambda h: _mix0_fwd(h, w)), (lambda df, sv: _mix0_bwd(df, sv, w))
        return (lambda h: _mix1_fwd(h, w)), (lambda df, sv: _mix1_bwd(df, sv, w))

    saved, bwd = [], []
    for i in range(6):
        l, s = divmod(i, 3)
        w, token = weights_of(i, x)
        f, b = fns(i, w)
        x, sv = _sublayer_fwd(x, f, mod[l, s] + token[0:1, 0:1], norm_pre[l, s][None], norm_post[l, s][None], RES_WEIGHT[s])
        saved.append(sv)
        bwd.append(b)
    loss_row, dx = _loss_fwd_bwd(x, tgt)
    token = jnp.zeros((8, LANE), F32)
    for i in reversed(range(6)):
        l, s = divmod(i, 3)
        mod3 = mod[l, s] + token[0:1, 0:1]
        dx, dmod, dpre, dpost, extra = _sublayer_bwd(
            dx, saved[i], bwd[i], mod3, norm_pre[l, s][None], norm_post[l, s][None], RES_WEIGHT[s])
        token = on_grads(i, extra, dmod, dpre, dpost, loss_row)
    return dx


def _pad_rows(v, rows):
    return jnp.pad(v, (0, rows * LANE - v.shape[0])).reshape(rows, LANE)


def _pack(parts):
    flat, layout, off = [], [], 0
    for a in parts:
        n = a.size
        padded = -(-n // LANE) * LANE
        flat.append(jnp.pad(a.reshape(-1).astype(F32), (0, padded - n)))
        layout.append((off, n, a.shape))
        off += padded
    return jnp.concatenate(flat), layout


def _unpack(flat, layout):
    return [flat[off:off + n].reshape(shape) for off, n, shape in layout]


SMALL_REPLICATED = ["ada_b", "pool_w", "pool_scale", "sgu_ln_g", "sgu_ln_b", "sgu_w", "sgu_b", "ssm_lam_re", "ssm_lam_im",
                    "ssm_b_re", "ssm_b_im", "ssm_c_re", "ssm_c_im", "ssm_log_dt"]
SMALL_SHARDED = ["norm_pre", "norm_post", "ssm_d"]
TRANSPOSED = ["ffn_w_in", "ab_w_in"]
WEIGHTS = ['ada_w', 'ada_b', 'norm_pre', 'norm_post', 'ffn_w_in', 'ffn_w_out', 'ab_w_in', 'pool_w', 'pool_scale', 'sgu_ln_g',
           'sgu_ln_b', 'sgu_w', 'sgu_b', 'ab_w_out', 'ssm_w_in', 'ssm_lam_re', 'ssm_lam_im', 'ssm_b_re', 'ssm_b_im', 'ssm_c_re',
           'ssm_c_im', 'ssm_d', 'ssm_log_dt', 'ssm_w_glu']


def kernel(x, c, ada_w, ada_b, norm_pre, norm_post, ffn_w_in, ffn_w_out, ab_w_in, pool_w, pool_scale, sgu_ln_g, sgu_ln_b, sgu_w, sgu_b, ab_w_out, ssm_w_in, ssm_lam_re, ssm_lam_im, ssm_b_re, ssm_b_im, ssm_c_re, ssm_c_im, ssm_d, ssm_log_dt, ssm_w_glu, loss_target, m_ada_w, m_ada_b, m_norm_pre, m_norm_post, m_ffn_w_in, m_ffn_w_out, m_ab_w_in, m_pool_w, m_pool_scale, m_sgu_ln_g, m_sgu_ln_b, m_sgu_w, m_sgu_b, m_ab_w_out, m_ssm_w_in, m_ssm_lam_re, m_ssm_lam_im, m_ssm_b_re, m_ssm_b_im, m_ssm_c_re, m_ssm_c_im, m_ssm_d, m_ssm_log_dt, m_ssm_w_glu, v_ada_w, v_ada_b, v_norm_pre, v_norm_post, v_ffn_w_in, v_ffn_w_out, v_ab_w_in, v_pool_w, v_pool_scale, v_sgu_ln_g, v_sgu_ln_b, v_sgu_w, v_sgu_b, v_ab_w_out, v_ssm_w_in, v_ssm_lam_re, v_ssm_lam_im, v_ssm_b_re, v_ssm_b_im, v_ssm_c_re, v_ssm_c_im, v_ssm_d, v_ssm_log_dt, v_ssm_w_glu):
    args = locals()
    wts = {n: args[n] for n in WEIGHTS}
    mom = {n: args["m_" + n] for n in WEIGHTS}
    var = {n: args["v_" + n] for n in WEIGHTS}
    for n in TRANSPOSED:
        for t in (wts, mom, var):
            t[n] = jnp.swapaxes(t[n], -1, -2)
    me = 4 * lax.axis_index("x") + 2 * lax.axis_index("y") + lax.axis_index("c")
    s = x.shape[1]
    nd = D // N_DEV

    small_in, small_in_layout = _pack([c, norm_pre, norm_post, ssm_d])
    small_rows = -(-small_in.shape[0] // (8 * LANE)) * 8
    (g_small,) = _all_gather("gather_small", [_pad_rows(small_in, small_rows)])
    g_small = g_small.reshape(N_DEV, -1)
    c_all, npre_g, npost_g, sd_g = [jnp.stack([_unpack(g_small[j], small_in_layout)[i] for j in range(N_DEV)]) for i in range(4)]
    c_all = c_all.reshape(N_DEV, D)
    norm_pre_full = npre_g.transpose(1, 2, 0, 3).reshape(2, 3, D)
    norm_post_full = npost_g.transpose(1, 2, 0, 3).reshape(2, 3, D)
    ssm_d_full = sd_g.transpose(1, 0, 2).reshape(1, D)

    nw = ada_w.shape[-1]
    (mod_g,) = _all_gather("gather_mod", [_mod_part(c_all, ada_w)])
    mod = lax.dynamic_index_in_dim(mod_g, me, axis=2, keepdims=False)
    mod = (mod.transpose(1, 0, 2).reshape(2, N_DEV * nw) + ada_b).reshape(2, 3, 3, D)

    w_in_t = wts["ffn_w_in"]
    shards = [[w_in_t[0, 0]], [ffn_w_out[0, 0]], [wts["ab_w_in"][0], ab_w_out[0]], [w_in_t[0, 1], ffn_w_out[0, 1]],
              [w_in_t[1, 0], ffn_w_out[1, 0]], [ssm_w_in[0], ssm_w_glu[0]], [w_in_t[1, 1], ffn_w_out[1, 1]]]
    same_core = (2, 4, 6)

    def gather_plan(n):
        return lambda me_, peer_, k: [(a, None, a, me_) for a in range(n)] if k in (0, 1) + same_core else []

    def relay_plan(n):
        return lambda me_, peer_, k: [(a, me_ ^ kk, a, me_ ^ kk) for kk in same_core for a in range(n)] if k == 1 else []

    gathers, relays = [], {}
    token = mod_g
    for g, group in enumerate(shards):
        group = [a.astype(BF16) for a in group]
        sems, srcs_thru, lands, token = _exchange_start(
            f"gather_start_{g}", group, [_sds((N_DEV,) + a.shape, BF16) for a in group], gather_plan(len(group)), len(group), token)
        gathers.append((sems, srcs_thru, lands))
    mod = mod + token[0, 0]

    def relay(g, after):
        sems, srcs_thru, lands = gathers[g]
        n = len(lands)
        relays[g] = _exchange_relay(f"gather_relay_{g}", sems, srcs_thru, lands, gather_plan(n), n, relay_plan(n), 3 * n, after)

    def fetch(g, after):
        if g not in relays:
            relay(g, after)
        sems, lands, token = relays[g]
        n = len(lands)
        got = _exchange_wait(f"gather_wait_{g}", sems, None, lands, relay_plan(n), 3 * n, after)
        if g + 1 < len(gathers):
            relay(g + 1, got[0])
            token = relays[g + 1][2]
        return got, token

    head_sum = jnp.repeat(jnp.eye(NH, LANE, dtype=F32), HD, axis=0)
    mix0 = {"pool_w": pool_w[0], "pool_scale": pool_scale, "sgu_ln_g": sgu_ln_g, "sgu_ln_b": sgu_ln_b, "sgu_w": sgu_w[0],
            "sgu_bt": jnp.pad(sgu_b[0].T, ((0, 0), (0, LANE - NH))), "head_sum": head_sum}
    mix1 = _ssm_params(ssm_lam_re[0], ssm_lam_im[0], ssm_b_re[0], ssm_b_im[0], ssm_c_re[0], ssm_c_im[0], ssm_log_dt[0])
    mix1["ssm_d"] = ssm_d_full

    def weights_of(i, x_in):
        if i == 0:
            (win,), token = fetch(0, x_in)
            cache = []

            def wout_of(z):
                if not cache:
                    cache.append(fetch(1, z)[0][0])
                return cache[0]

            return (win, wout_of), token
        (a, b), token = fetch(i + 1, x_in)
        if i % 3 != 1:
            return (a, lambda z: b), token
        if i == 1:
            return dict(mix0, ab_w_in=a.reshape(-1, D), ab_w_out=b.reshape(D, D)), token
        return dict(mix1, ssm_w_in=a.reshape(D, D), ssm_w_glu=b.transpose(1, 0, 2).reshape(D, -1)), token

    def shard_cols(a):
        r = a.shape[0]
        return a.reshape(r, N_DEV, -1).transpose(1, 0, 2)

    scatter_plan = lambda me_, peer_, k: [(0, peer_, 0, me_), (1, peer_, 1, me_)]
    scatter_plan1 = lambda me_, peer_, k: [(0, peer_, 0, me_)]
    scatters = []
    last_token = [jnp.zeros((8, LANE), F32)]
    pieces, mixer, bundles = {}, {}, {}
    bundle_plan = lambda me_, peer_, k: [(0, None, 0, me_)]

    def on_part(i, tag, part):
        sems, srcs_thru, lands, last_token[0] = _exchange_start(
            f"scatter_start_{i}_{tag}", [part], [_sds(part.shape, BF16)], scatter_plan1, 1, last_token[0])
        scatters.append((i, ("ffn_" + tag,), scatter_plan1, sems, srcs_thru, lands))
        return last_token[0]
    mix0_names = ["pool_w", "pool_scale", "sgu_ln_g", "sgu_ln_b", "sgu_w", "sgu_b"]
    mix1_names = ["ssm_lam_re", "ssm_lam_im", "ssm_b_re", "ssm_b_im", "ssm_c_re", "ssm_c_im", "ssm_log_dt", "ssm_d"]

    def start_bundle(tag, arrays):
        flat, layout = _pack(arrays)
        rows = -(-flat.shape[0] // (8 * LANE)) * 8
        plan = gather_plan(1) if tag == "a" else bundle_plan
        sems, srcs_thru, lands, last_token[0] = _exchange_start(
            f"small_start_{tag}", [_pad_rows(flat, rows)], [_sds((N_DEV, rows, LANE))], plan, 1, last_token[0])
        bundles[tag] = (sems, srcs_thru, lands, layout)

    def on_grads(i, extra, dmod_i, dpre_i, dpost_i, loss_row):
        pieces[i] = (dmod_i, dpre_i, dpost_i)
        if i == 4:
            mixer.update({n: extra[n] for n in mix1_names})
        if i == 1:
            mixer.update({n: extra[n] for n in mix0_names})
            rest = range(1, 6)
            start_bundle("a", [jnp.stack([pieces[j][0] for j in rest])] + [jnp.concatenate([pieces[j][k] for j in rest]) for k in (1, 2)]
                         + [mixer[n] for n in mix0_names + mix1_names])
        if i == 0:
            start_bundle("b", [dmod_i, dpre_i, dpost_i, loss_row])
        if i % 3 != 1:
            return last_token[0]
        if i == 1:
            names, parts = ("ab_w_in", "ab_w_out"), [extra["ab_w_in"].reshape(N_DEV, -1, D), extra["ab_w_out"].reshape(N_DEV, nd, D)]
        else:
            names, parts = ("ssm_w_in", "ssm_w_glu"), [extra["ssm_w_in"].reshape(N_DEV, nd, D), shard_cols(extra["ssm_w_glu"])]
        sems, srcs_thru, lands, last_token[0] = _exchange_start(
            f"scatter_start_{i}", parts, [_sds(a.shape, BF16) for a in parts], scatter_plan, 2, last_token[0])
        scatters.append((i, names, scatter_plan, sems, srcs_thru, lands))
        return last_token[0]

    grad_x = _local_step(x[0], loss_target[0], mod, norm_pre_full, norm_post_full, weights_of, on_part, on_grads)

    out_g, out_d, out_m, out_v = {}, {}, {}, {}
    big_out = {}

    def adam_big(name, recv, n, slot=0):
        c_ = wts[n].shape[-1]
        big_out[n] = _adamw(name, recv.reshape(recv.shape[0], -1, c_), *[t[n].reshape(-1, c_) for t in (wts, mom, var)],
                            slot=slot, prev=big_out.get(n))
        return big_out[n][0]

    ffn_slot = {0: 0, 2: 1, 3: 2, 5: 3}

    def land_and_update(entries, after):
        for i, names, plan, sems, srcs_thru, lands in entries:
            recv = _exchange_wait(f"scatter_wait_{i}_{names[0]}", sems, srcs_thru, lands, plan, len(names), after)
            for n, r in zip(names, recv):
                after = adam_big(f"adamw_{n}_{i}", r, n, ffn_slot.get(i, 0))
        return after

    after = land_and_update([e for e in scatters if e[0] != 0], last_token[0])

    gathered, sums = {}, {}
    for tag in ("a", "b"):
        sems, srcs_thru, lands, layout = bundles[tag]
        if tag == "a":
            sems, lands, _ = _exchange_relay("small_relay_a", sems, srcs_thru, lands, gather_plan(1), 1, relay_plan(1), 3, after)
            (g_parts,) = _exchange_wait("small_wait_a", sems, None, lands, relay_plan(1), 3, after)
        else:
            (g_parts,) = _exchange_wait("small_wait_b", sems, srcs_thru, lands, bundle_plan, 1, after)
        total = _sum_parts(g_parts)
        after = total
        off, n, shape = layout[0]
        gathered[tag] = g_parts.reshape(N_DEV, -1)[:, off:off + n].reshape((N_DEV,) + shape)
        sums[tag] = _unpack(total.reshape(-1), layout)
    dmod_a, dpre_a, dpost_a = sums["a"][:3]
    dmod_b, dpre_b, dpost_b, loss_sum = sums["b"]
    small = dict(zip(mix0_names + mix1_names, sums["a"][3:]))
    small["ada_b"] = jnp.concatenate([dmod_b[None], dmod_a])
    small["norm_pre"] = jnp.concatenate([dpre_b, dpre_a]).reshape(2, 3, D)
    small["norm_post"] = jnp.concatenate([dpost_b, dpost_a]).reshape(2, 3, D)
    loss = loss_sum[0, 0]

    def adam_flat(name, gnames, grads):
        gf, lay = _pack(grads)
        r = -(-gf.shape[0] // (8 * LANE)) * 8
        packed = [_pad_rows(_pack([t[n] for n in gnames])[0], r) for t in (wts, mom, var)]
        res = _adamw(name, _pad_rows(gf, r)[None], *packed)
        for o, arr in zip((out_g, out_d, out_m, out_v), res):
            o.update(dict(zip(gnames, _unpack(arr.reshape(-1), lay))))
        return res[0]

    adam_flat("adamw_replicated", SMALL_REPLICATED, [small[n].reshape(wts[n].shape) for n in SMALL_REPLICATED])
    sliced = [lax.dynamic_slice_in_dim(small[n], me * nd, nd, axis=small[n].ndim - 1).reshape(wts[n].shape) for n in SMALL_SHARDED]
    after = adam_flat("adamw_sliced", SMALL_SHARDED, sliced)

    dmod_all = jnp.concatenate([gathered["b"][:, None], gathered["a"]], axis=1).reshape(N_DEV, 2, N_DEV, nw)
    dmod_mine = lax.dynamic_index_in_dim(dmod_all, me, axis=2, keepdims=False).transpose(1, 0, 2)
    g_ada_w = _ada_w_grad(c_all.T, dmod_mine)
    after = after[0:1, 0:1] + adam_big("adamw_ada_w", g_ada_w[None], "ada_w")[0:1, 0:1]

    land_and_update([e for e in scatters if e[0] == 0], after)
    for n, res in big_out.items():
        for o, arr in zip((out_g, out_d, out_m, out_v), res):
            o[n] = arr.reshape(wts[n].shape)
            if n in TRANSPOSED:
                o[n] = jnp.swapaxes(o[n], -1, -2)

    return (loss, grad_x[None], *[out_g[n] for n in WEIGHTS], *[out_d[n] for n in WEIGHTS],
            *[out_m[n] for n in WEIGHTS], *[out_v[n] for n in WEIGHTS])
```

```python
import functools
import math

import jax
import jax.numpy as jnp
from jax import lax
from jax.experimental import pallas as pl
from jax.experimental.pallas import tpu as pltpu

F32 = jnp.float32
BF16 = jnp.bfloat16
MESH = pl.DeviceIdType.MESH
HIGHEST = lax.Precision.HIGHEST

N_DEV = 8
D = 1024
D_FF = 2816
FSH = 2 * D_FF // N_DEV
EPS = 1e-6
POOL_WINDOWS = (2, 4, 8, 16)
HD = 128
NH = 4
SSM_G, SSM_P, SSM_N = 64, 64, 16
SSM_GB = 16
SSM_NB = SSM_G // SSM_GB
SSM_L = SSM_G * SSM_P
LR, B1, B2, ADAM_EPS, WD, STEP = 0.001, 0.9, 0.999, 1e-08, 0.01, 10
GELU_C = math.sqrt(2.0 / math.pi)
VMEM_LIMIT_BYTES = 48 * 1024 * 1024
LANE = 128


def _pc(body, name, grid, in_specs, out_specs, out_shape, scratch=()):
    return pl.pallas_call(
        body, name=name, grid=grid, in_specs=in_specs, out_specs=out_specs, out_shape=out_shape,
        scratch_shapes=list(scratch),
        compiler_params=pltpu.CompilerParams(dimension_semantics=("arbitrary",) * len(grid),
                                             vmem_limit_bytes=VMEM_LIMIT_BYTES))


def _sds(shape, dtype=F32):
    return jax.ShapeDtypeStruct(tuple(shape), dtype)


def _bf(v):
    return v if v.dtype == BF16 else v.astype(BF16)


def _row_spec(ts, width, col=0):
    return pl.BlockSpec((ts, width), lambda t, _c=col: (t, _c))


def _vec_spec(width, col=0):
    return pl.BlockSpec((1, width), lambda t, _c=col: (0, _c))


def _mm(name, a, b, contract, grid, a_spec, b_spec, o_spec, out_shape, acc_axis=None, after=None):
    dn = (contract, ((), ()))

    def body(a_ref, b_ref, *rest):
        o_ref = rest[-1]
        r = lax.dot_general(_bf(a_ref[...]), _bf(b_ref[...]), dn, preferred_element_type=F32)
        if acc_axis is None:
            o_ref[...] = r.astype(o_ref.dtype)
        else:
            k = pl.program_id(acc_axis)

            @pl.when(k == 0)
            def _():
                o_ref[...] = r

            @pl.when(k > 0)
            def _():
                o_ref[...] += r

    if after is None:
        return _pc(body, name, grid, [a_spec, b_spec], o_spec, out_shape)(a, b)
    return _pc(body, name, grid, [a_spec, b_spec, pl.BlockSpec(memory_space=pl.ANY)], o_spec, out_shape)(a, b, after)


def _tile(s):
    return min(s, 1024)


def _div_tile(n, cap=1024):
    t = min(n, cap) // LANE * LANE
    while n % t:
        t -= LANE
    return t


def _mm_nn(name, a, b, out_dtype=F32):
    s, k = a.shape
    n = b.shape[1]
    ts, tn = _tile(s), _div_tile(n)
    return _mm(name, a, b, ((1,), (0,)), (n // tn, s // ts),
               pl.BlockSpec((ts, k), lambda j, t: (t, 0)), pl.BlockSpec((k, tn), lambda j, t: (0, j)),
               pl.BlockSpec((ts, tn), lambda j, t: (t, j)), _sds((s, n), out_dtype))


def _mm_nt(name, a, b, out_dtype=F32):
    s, n = a.shape
    k = b.shape[0]
    ts, tk = _tile(s), _div_tile(k)
    return _mm(name, a, b, ((1,), (1,)), (k // tk, s // ts),
               pl.BlockSpec((ts, n), lambda j, t: (t, 0)), pl.BlockSpec((tk, n), lambda j, t: (j, 0)),
               pl.BlockSpec((ts, tk), lambda j, t: (t, j)), _sds((s, k), out_dtype))


def _mm_tn(name, a, b, out_dtype=F32, tm=512, tn=512):
    s, m = a.shape
    n = b.shape[1]
    tm, tn = min(m, tm), min(n, tn)
    return _mm(name, a, b, ((0,), (0,)), (m // tm, n // tn),
               pl.BlockSpec((s, tm), lambda i, j: (0, i)), pl.BlockSpec((s, tn), lambda i, j: (0, j)),
               pl.BlockSpec((tm, tn), lambda i, j: (i, j)), _sds((m, n), out_dtype))


def _rstd(v):
    return lax.rsqrt(jnp.mean(v * v, axis=-1, keepdims=True) + EPS)


def _prenorm_fwd(x, g, scale, shift):
    s = x.shape[0]
    ts = min(s, 512)

    def body(x_ref, g_ref, sc_ref, sh_ref, h_ref):
        xv = x_ref[...]
        h_ref[...] = ((xv * _rstd(xv) * g_ref[...]) * (1.0 + sc_ref[...]) + sh_ref[...]).astype(BF16)

    return _pc(body, "prenorm_fwd", (s // ts,), [_row_spec(ts, D)] + [_vec_spec(D)] * 3, _row_spec(ts, D),
               _sds((s, D), BF16))(x, g, scale, shift)


def _postnorm_fwd(x, f, g, gate, rw):
    s = x.shape[0]
    ts = min(s, 512)

    def body(x_ref, f_ref, g_ref, gt_ref, o_ref):
        fv = f_ref[...]
        o_ref[...] = x_ref[...] + (rw * gt_ref[...]) * (fv * _rstd(fv) * g_ref[...])

    return _pc(body, "postnorm_fwd", (s // ts,), [_row_spec(ts, D)] * 2 + [_vec_spec(D)] * 2, _row_spec(ts, D),
               _sds((s, D)))(x, f, g, gate)


def _acc(ref, first, v):
    @pl.when(first)
    def _():
        ref[...] = v

    @pl.when(jnp.logical_not(first))
    def _():
        ref[...] += v


def _colsum(v):
    return jnp.sum(v, axis=0, keepdims=True)


def _postnorm_bwd(dout, f, g, gate, rw):
    s = dout.shape[0]
    ts = min(s, 512)

    def body(do_ref, f_ref, g_ref, gt_ref, df_ref, dgate_ref, dg_ref):
        first = pl.program_id(0) == 0
        do, fv, gv = do_ref[...], f_ref[...], g_ref[...]
        r = _rstd(fv)
        fn = fv * r
        _acc(dgate_ref, first, rw * _colsum(do * (fn * gv)))
        dy = (rw * gt_ref[...]) * do
        _acc(dg_ref, first, _colsum(dy * fn))
        dfn = dy * gv
        df_ref[...] = (r * (dfn - fn * jnp.mean(dfn * fn, axis=-1, keepdims=True))).astype(BF16)

    return _pc(body, "postnorm_bwd", (s // ts,), [_row_spec(ts, D)] * 2 + [_vec_spec(D)] * 2,
               [_row_spec(ts, D), _vec_spec(D), _vec_spec(D)],
               [_sds((s, D), BF16), _sds((1, D)), _sds((1, D))])(dout, f, g, gate)


def _prenorm_bwd(dout, dh, x, g, scale):
    s = dout.shape[0]
    ts = min(s, 512)

    def body(do_ref, dh_ref, x_ref, g_ref, sc_ref, dx_ref, dsh_ref, dsc_ref, dg_ref):
        first = pl.program_id(0) == 0
        dhv, xv, gv = dh_ref[...], x_ref[...], g_ref[...]
        r = _rstd(xv)
        xn = xv * r
        _acc(dsh_ref, first, _colsum(dhv))
        _acc(dsc_ref, first, _colsum(dhv * (xn * gv)))
        dhp = dhv * (1.0 + sc_ref[...])
        _acc(dg_ref, first, _colsum(dhp * xn))
        dxn = dhp * gv
        dx_ref[...] = do_ref[...] + r * (dxn - xn * jnp.mean(dxn * xn, axis=-1, keepdims=True))

    return _pc(body, "prenorm_bwd", (s // ts,), [_row_spec(ts, D)] * 3 + [_vec_spec(D)] * 2,
               [_row_spec(ts, D)] + [_vec_spec(D)] * 3,
               [_sds((s, D))] + [_sds((1, D))] * 3)(dout, dh, x, g, scale)


def _loss_fwd_bwd(y, tgt):
    s = y.shape[0]
    ts = min(s, 512)
    nt = s // ts

    def body(y_ref, t_ref, loss_ref, dy_ref, acc_ref):
        t = pl.program_id(0)
        e = y_ref[...] - t_ref[...]
        dy_ref[...] = e * (1.0 / D)
        _acc(acc_ref, t == 0, _colsum(e * e))

        @pl.when(t == nt - 1)
        def _():
            loss_ref[...] = jnp.full((1, LANE), 0.5 / D, F32) * jnp.sum(acc_ref[...])

    return _pc(body, "loss", (nt,), [_row_spec(ts, D)] * 2,
               [pl.BlockSpec((1, LANE), lambda t: (0, 0)), _row_spec(ts, D)],
               [_sds((1, LANE)), _sds((s, D))], scratch=[pltpu.VMEM((1, D), F32)])(y, tgt)


def _sigmoid(v):
    return 1.0 / (1.0 + jnp.exp(-v))


def _ffn_in_swiglu(h, win):
    s = h.shape[0]
    ts = _tile(s)
    nt = (((1,), (1,)), ((), ()))

    def body(h_ref, wa_ref, wb_ref, z_ref, act_ref):
        hv = h_ref[...]
        a = lax.dot_general(hv, wa_ref[...], nt, preferred_element_type=F32)
        b = lax.dot_general(hv, wb_ref[...], nt, preferred_element_type=F32)
        z_ref[0] = a
        z_ref[1] = b
        act_ref[...] = (a * _sigmoid(a) * b).astype(BF16)

    z4, act = _pc(body, "ffn_in", (4, s // ts),
                  [pl.BlockSpec((ts, D), lambda k, t: (t, 0)), pl.BlockSpec((None, FSH, D), lambda k, t: (k, 0, 0)),
                   pl.BlockSpec((None, FSH, D), lambda k, t: (k + 4, 0, 0))],
                  [pl.BlockSpec((2, None, ts, FSH), lambda k, t: (0, k, t, 0)), pl.BlockSpec((None, ts, FSH), lambda k, t: (k, t, 0))],
                  [_sds((2, 4, s, FSH)), _sds((4, s, FSH), BF16)])(h, win, win)
    return z4.reshape(N_DEV, s, FSH), act


def _ffn_out_dx_swiglu(df, wout, z, after):
    s = df.shape[0]
    ts = _tile(s)
    z4 = z.reshape(2, 4, s, FSH)
    nt = (((1,), (1,)), ((), ()))

    def body(df_ref, w_ref, z_ref, after_ref, o_ref):
        d = lax.dot_general(df_ref[...], w_ref[...], nt, preferred_element_type=F32)
        a, b = z_ref[0], z_ref[1]
        sg = _sigmoid(a)
        o_ref[0] = (d * b * (sg * (1.0 + a * (1.0 - sg)))).astype(BF16)
        o_ref[1] = (d * (a * sg)).astype(BF16)

    spec = pl.BlockSpec((2, None, ts, FSH), lambda k, t: (0, k, t, 0))
    out = _pc(body, "ffn_out_dx", (4, s // ts),
              [pl.BlockSpec((ts, D), lambda k, t: (t, 0)), pl.BlockSpec((None, FSH, D), lambda k, t: (k, 0, 0)), spec,
               pl.BlockSpec(memory_space=pl.ANY)],
              spec, _sds((2, 4, s, FSH), BF16))(df, wout, z4, after)
    return out.reshape(N_DEV, s, FSH)


def _ffn_fwd(h, win, wout_of):
    s = h.shape[0]
    ts = _tile(s)
    z, act = _ffn_in_swiglu(h, win)
    wout = wout_of(z).reshape(4, FSH, D)
    f = _mm("ffn_out", act, wout, ((1,), (0,)), (s // ts, 4),
            pl.BlockSpec((None, ts, FSH), lambda t, k: (k, t, 0)), pl.BlockSpec((None, FSH, D), lambda t, k: (k, 0, 0)),
            pl.BlockSpec((ts, D), lambda t, k: (t, 0)), _sds((s, D)), acc_axis=1)
    return f, (h, z, act)


def _ffn_bwd(df, saved, win, wout, send):
    h, z, act = saved
    s = h.shape[0]
    ts = _tile(s)
    wout = wout.reshape(4, FSH, D)
    dwout = _mm("ffn_out_dw", act, df, ((0,), (0,)), (4, 2),
                pl.BlockSpec((None, s, FSH), lambda k, j: (k, 0, 0)), pl.BlockSpec((s, D // 2), lambda k, j: (0, j)),
                pl.BlockSpec((None, FSH, D // 2), lambda k, j: (k, 0, j)), _sds((4, FSH, D), BF16))
    dz = _ffn_out_dx_swiglu(df, wout, z, send("w_out", dwout.reshape(N_DEV, D_FF // N_DEV, D)))
    dwin = _mm("ffn_in_dw", dz, h, ((0,), (0,)), (N_DEV, 2),
               pl.BlockSpec((None, s, FSH), lambda j, i: (j, 0, 0)), pl.BlockSpec((s, D // 2), lambda j, i: (0, i)),
               pl.BlockSpec((None, FSH, D // 2), lambda j, i: (j, 0, i)), _sds((N_DEV, FSH, D), BF16))
    return _mm("ffn_in_dx", dz, win, ((1,), (0,)), (s // ts, N_DEV),
               pl.BlockSpec((None, ts, FSH), lambda t, j: (j, t, 0)), pl.BlockSpec((None, FSH, D), lambda t, j: (j, 0, 0)),
               pl.BlockSpec((ts, D), lambda t, j: (t, 0)), _sds((s, D)), acc_axis=1, after=send("w_in", dwin))


def _shift_rows(v, k, row, s, back):
    if back:
        return jnp.where(row < s - k, pltpu.roll(v, s - k, 0), 0.0)
    return jnp.where(row >= k, pltpu.roll(v, k, 0), 0.0)


def _window_sum(v, w, row, s, back):
    k = 1
    while k < w:
        v = v + _shift_rows(v, k, row, s, back)
        k *= 2
    return v


def _pool_fwd(z, pool_w, pool_scale):
    s = z.shape[0]

    def body(z_ref, w_ref, sc_ref, y_ref, d_ref):
        row = lax.broadcasted_iota(jnp.int32, (s, HD), 0)
        for g, w in enumerate(POOL_WINDOWS):
            sl = slice(g * HD, (g + 1) * HD)
            a = z_ref[:, sl]
            cnt = jnp.minimum(row + 1, w).astype(F32)
            d = (_window_sum(a, w, row, s, False) / cnt - a).astype(BF16)
            d_ref[:, sl] = d
            y = jnp.dot(d, _bf(w_ref[g]), preferred_element_type=F32)
            y_ref[:, sl] = (y * sc_ref[:, sl]).astype(BF16)

    return _pc(body, "pool_fwd", (1,),
               [pl.BlockSpec((s, NH * HD), lambda i: (0, 0)), pl.BlockSpec((NH, HD, HD), lambda i: (0, 0, 0)),
                pl.BlockSpec((1, NH * HD), lambda i: (0, 0))],
               [pl.BlockSpec((s, NH * HD), lambda i: (0, 0))] * 2,
               [_sds((s, NH * HD), BF16)] * 2)(z, pool_w, pool_scale)


def _pool_bwd(dy, d, pool_w, pool_scale):
    s = dy.shape[0]

    def body(dy_ref, d_ref, w_ref, sc_ref, dz_ref, dw_ref, dsc_ref):
        row = lax.broadcasted_iota(jnp.int32, (s, HD), 0)
        for g, w in enumerate(POOL_WINDOWS):
            sl = slice(g * HD, (g + 1) * HD)
            dyg, dg, wg = dy_ref[:, sl], d_ref[:, sl], _bf(w_ref[g])
            yraw = jnp.dot(dg, wg, preferred_element_type=F32)
            dsc_ref[:, sl] = _colsum(dyg * yraw)
            dyr = _bf(dyg * sc_ref[:, sl])
            dw_ref[g] = lax.dot_general(dg, dyr, (((0,), (0,)), ((), ())), preferred_element_type=F32)
            dd = lax.dot_general(dyr, wg, (((1,), (1,)), ((), ())), preferred_element_type=F32)
            cnt = jnp.minimum(row + 1, w).astype(F32)
            dz_ref[:, sl] = (_window_sum(dd / cnt, w, row, s, True) - dd).astype(BF16)

    return _pc(body, "pool_bwd", (1,),
               [pl.BlockSpec((s, NH * HD), lambda i: (0, 0)), pl.BlockSpec((s, NH * HD), lambda i: (0, 0)),
                pl.BlockSpec((NH, HD, HD), lambda i: (0, 0, 0)), pl.BlockSpec((1, NH * HD), lambda i: (0, 0))],
               [pl.BlockSpec((s, NH * HD), lambda i: (0, 0)), pl.BlockSpec((NH, HD, HD), lambda i: (0, 0, 0)),
                pl.BlockSpec((1, NH * HD), lambda i: (0, 0))],
               [_sds((s, NH * HD), BF16), _sds((NH, HD, HD)), _sds((1, NH * HD))])(dy, d, pool_w, pool_scale)


def _gelu(v):
    return 0.5 * v * (1.0 + jnp.tanh(GELU_C * (v + 0.044715 * (v * v * v))))


def _gelu_grad(v):
    t = jnp.tanh(GELU_C * (v + 0.044715 * (v * v * v)))
    return 0.5 * (1.0 + t) + 0.5 * v * (1.0 - t * t) * (GELU_C * (1.0 + 3.0 * 0.044715 * (v * v)))


def _causal_mask():
    return lax.broadcasted_iota(jnp.int32, (HD, HD), 0) >= lax.broadcasted_iota(jnp.int32, (HD, HD), 1)


def _sgu_specs():
    w = NH * HD
    return [pl.BlockSpec((HD, w), lambda c: (c, 1)), pl.BlockSpec((HD, w), lambda c: (c, 2)),
            pl.BlockSpec((1, w), lambda c: (0, 0)), pl.BlockSpec((1, w), lambda c: (0, 0)),
            pl.BlockSpec((NH, HD, HD), lambda c: (0, 0, 0)), pl.BlockSpec((HD, LANE), lambda c: (0, 0))]


def _sgu_head(v, lng_ref, lnb_ref, w_ref, h):
    sl = slice(h * HD, (h + 1) * HD)
    vh = v[:, sl]
    xc = vh - jnp.mean(vh, axis=-1, keepdims=True)
    rs = lax.rsqrt(jnp.mean(xc * xc, axis=-1, keepdims=True) + EPS)
    vhat = xc * rs
    vn = _bf(vhat * lng_ref[:, sl] + lnb_ref[:, sl])
    wc = _bf(jnp.where(_causal_mask(), w_ref[h], 0.0))
    return sl, rs, vhat, vn, wc


def _sgu_fwd(z, ln_g, ln_b, sgu_w, sgu_bt):
    s = z.shape[0]

    def body(zu_ref, zv_ref, lng_ref, lnb_ref, w_ref, bt_ref, y_ref):
        u, v = _gelu(zu_ref[...]), _gelu(zv_ref[...])
        for h in range(NH):
            sl, _, _, vn, wc = _sgu_head(v, lng_ref, lnb_ref, w_ref, h)
            sp = jnp.dot(wc, vn, preferred_element_type=F32) + bt_ref[:, h:h + 1]
            y_ref[:, sl] = (u[:, sl] * sp).astype(BF16)

    return _pc(body, "sgu_fwd", (s // HD,), _sgu_specs(), pl.BlockSpec((HD, NH * HD), lambda c: (c, 0)),
               _sds((s, NH * HD), BF16))(z, z, ln_g, ln_b, sgu_w, sgu_bt)


def _sgu_bwd(z, dy, ln_g, ln_b, sgu_w, sgu_bt, head_sum):
    s = z.shape[0]
    w = NH * HD
    nc = s // HD

    def body(zu_ref, zv_ref, lng_ref, lnb_ref, w_ref, bt_ref, dy_ref, hs_ref,
             dzu_ref, dzv_ref, dlng_ref, dlnb_ref, dw_ref, dbt_ref, dsacc_ref):
        c = pl.program_id(0)
        first = c == 0
        zu, zv = zu_ref[...], zv_ref[...]
        u, v = _gelu(zu), _gelu(zv)
        dyv = dy_ref[...]
        gu, gv = _gelu_grad(zu), _gelu_grad(zv)
        ds = dyv * u
        _acc(dsacc_ref, first, ds)
        for h in range(NH):
            sl, rs, vhat, vn, wc = _sgu_head(v, lng_ref, lnb_ref, w_ref, h)
            sp = jnp.dot(wc, vn, preferred_element_type=F32) + bt_ref[:, h:h + 1]
            dzu_ref[:, sl] = (dyv[:, sl] * sp * gu[:, sl]).astype(BF16)
            dsh = _bf(ds[:, sl])
            dwh = lax.dot_general(dsh, vn, (((1,), (1,)), ((), ())), preferred_element_type=F32)
            dwh = jnp.where(_causal_mask(), dwh, 0.0)

            @pl.when(first)
            def _():
                dw_ref[h] = dwh

            @pl.when(jnp.logical_not(first))
            def _():
                dw_ref[h] += dwh

            dvn = lax.dot_general(wc, dsh, (((0,), (0,)), ((), ())), preferred_element_type=F32)
            g_col = _colsum(dvn * vhat)
            b_col = _colsum(dvn)

            @pl.when(first)
            def _():
                dlng_ref[:, sl] = g_col
                dlnb_ref[:, sl] = b_col

            @pl.when(jnp.logical_not(first))
            def _():
                dlng_ref[:, sl] += g_col
                dlnb_ref[:, sl] += b_col

            dvh = dvn * lng_ref[:, sl]
            dv = rs * (dvh - jnp.mean(dvh, axis=-1, keepdims=True) - vhat * jnp.mean(dvh * vhat, axis=-1, keepdims=True))
            dzv_ref[:, sl] = (dv * gv[:, sl]).astype(BF16)

        @pl.when(c == nc - 1)
        def _():
            dbt_ref[...] = jnp.dot(dsacc_ref[...], hs_ref[...], preferred_element_type=F32, precision=HIGHEST)

    outs = _pc(body, "sgu_bwd", (nc,),
               _sgu_specs() + [pl.BlockSpec((HD, w), lambda c: (c, 1)), pl.BlockSpec((w, LANE), lambda c: (0, 0))],
               [pl.BlockSpec((HD, w), lambda c: (c, 0))] * 2 + [pl.BlockSpec((1, w), lambda c: (0, 0))] * 2
               + [pl.BlockSpec((NH, HD, HD), lambda c: (0, 0, 0)), pl.BlockSpec((HD, LANE), lambda c: (0, 0))],
               [_sds((s, w), BF16)] * 2 + [_sds((1, w))] * 2 + [_sds((NH, HD, HD)), _sds((HD, LANE))],
               scratch=[pltpu.VMEM((HD, w), F32)])(z, z, ln_g, ln_b, sgu_w, sgu_bt, dy, head_sum)
    return outs


def _cmul(ar, ai, br, bi):
    return ar * br - ai * bi, ar * bi + ai * br


def _ssm_prep(lam_re, lam_im, lam_re_rep, lam_im_rep, log_dt, b_re, b_im):
    def disc(lr, li, dt):
        mag = jnp.exp(lr * dt)
        return mag * jnp.cos(li * dt), mag * jnp.sin(li * dt)

    def body(lr_ref, li_ref, lrr_ref, lir_ref, ldt_ref, br_ref, bi_ref, or_ref, oi_ref, bbr_ref, bbi_ref):
        dt = jnp.exp(ldt_ref[...])
        or_ref[...], oi_ref[...] = disc(lr_ref[...], li_ref[...], dt)
        lr, li = lrr_ref[...], lir_ref[...]
        er, ei = disc(lr, li, dt)
        den = lr * lr + li * li
        kr = ((er - 1.0) * lr + ei * li) / den
        ki = (ei * lr - (er - 1.0) * li) / den
        bbr_ref[...], bbi_ref[...] = _cmul(kr, ki, br_ref[...], bi_ref[...])

    small = pl.BlockSpec((SSM_G, SSM_P), lambda i: (0, 0))
    wide = pl.BlockSpec((SSM_G, SSM_P * SSM_N), lambda i: (0, 0))
    col = pl.BlockSpec((SSM_G, 1), lambda i: (0, 0))
    return _pc(body, "ssm_prep", (1,), [small, small, wide, wide, col, wide, wide], [small, small, wide, wide],
               [_sds((SSM_G, SSM_P))] * 2 + [_sds((SSM_G, SSM_P * SSM_N))] * 2)(
        lam_re, lam_im, lam_re_rep, lam_im_rep, log_dt, b_re, b_im)


def _ssm_param_bwd(g_lam_re, g_lam_im, g_bb_re, g_bb_im, lam_re, lam_im, lam_re_rep, lam_im_rep, log_dt, b_re, b_im, seg):
    def body(glr_ref, gli_ref, gbr_ref, gbi_ref, lr_ref, li_ref, lrr_ref, lir_ref, ldt_ref, br_ref, bi_ref, seg_ref,
             dlr_ref, dli_ref, ddt_ref, dbr_ref, dbi_ref):
        dt = jnp.exp(ldt_ref[...])
        lr, li = lrr_ref[...], lir_ref[...]
        mag = jnp.exp(lr * dt)
        er, ei = mag * jnp.cos(li * dt), mag * jnp.sin(li * dt)
        den = lr * lr + li * li
        kr = ((er - 1.0) * lr + ei * li) / den
        ki = (ei * lr - (er - 1.0) * li) / den
        gbr, gbi = gbr_ref[...], gbi_ref[...]
        dbr_ref[...], dbi_ref[...] = _cmul(kr, -ki, gbr, gbi)
        tr, ti = _cmul(br_ref[...], -bi_ref[...], gbr, gbi)
        gkr = jnp.dot(tr, seg_ref[...], preferred_element_type=F32, precision=HIGHEST)
        gki = jnp.dot(ti, seg_ref[...], preferred_element_type=F32, precision=HIGHEST)
        lr, li = lr_ref[...], li_ref[...]
        mag = jnp.exp(lr * dt)
        er, ei = mag * jnp.cos(li * dt), mag * jnp.sin(li * dt)
        den = lr * lr + li * li
        ir, ii = lr / den, -li / den
        kr, ki = _cmul(er - 1.0, ei, ir, ii)
        ar, ai = _cmul(ir, -ii, gkr, gki)
        glr, gli = glr_ref[...] + ar, gli_ref[...] + ai
        qr, qi = _cmul(kr, ki, ir, ii)
        g1r, g1i = _cmul(-qr, qi, gkr, gki)
        g2r, g2i = _cmul(dt * er, -dt * ei, glr, gli)
        dlr_ref[...] = g1r + g2r
        dli_ref[...] = g1i + g2i
        wr, wi = _cmul(lr, li, er, ei)
        g_dt = jnp.sum(wr * glr + wi * gli, axis=-1, keepdims=True)
        ddt_ref[...] = jnp.broadcast_to(dt * g_dt, (SSM_G, LANE))

    small = pl.BlockSpec((SSM_G, SSM_P), lambda i: (0, 0))
    wide = pl.BlockSpec((SSM_G, SSM_P * SSM_N), lambda i: (0, 0))
    col = pl.BlockSpec((SSM_G, 1), lambda i: (0, 0))
    segs = pl.BlockSpec((SSM_P * SSM_N, SSM_P), lambda i: (0, 0))
    return _pc(body, "ssm_param_bwd", (1,), [small, small, wide, wide, small, small, wide, wide, col, wide, wide, segs],
               [small, small, pl.BlockSpec((SSM_G, LANE), lambda i: (0, 0)), wide, wide],
               [_sds((SSM_G, SSM_P))] * 2 + [_sds((SSM_G, LANE))] + [_sds((SSM_G, SSM_P * SSM_N))] * 2)(
        g_lam_re, g_lam_im, g_bb_re, g_bb_im, lam_re, lam_im, lam_re_rep, lam_im_rep, log_dt, b_re, b_im, seg)


SCAN_LANES = 512
SCAN_ROWS = 8


def _ssm_scan(b_re, b_im, lam_re, lam_im, reverse):
    s = b_re.shape[0]
    nt = s // SCAN_ROWS
    ln, rows = SCAN_LANES, SCAN_ROWS

    def body(lr_ref, li_ref, br_ref, bi_ref, or_ref, oi_ref):
        l1 = (lr_ref[...], li_ref[...])
        pw = [l1]
        for _ in range(rows - 1):
            pw.append(_cmul(*pw[-1], *l1))
        row = lax.broadcasted_iota(jnp.int32, (rows, ln), 0)
        expo = (rows - row) if reverse else (row + 1)
        pr = jnp.zeros((rows, ln), F32)
        pi = jnp.zeros((rows, ln), F32)
        for e in range(1, rows + 1):
            pr = jnp.where(expo == e, pw[e - 1][0], pr)
            pi = jnp.where(expo == e, pw[e - 1][1], pi)
        lk = {k: (jnp.broadcast_to(pw[k - 1][0], (rows, ln)), jnp.broadcast_to(pw[k - 1][1], (rows, ln))) for k in (1, 2, 4)}

        def step(i, carry):
            cr, ci = carry
            t = (nt - 1 - i) if reverse else i
            r0 = pl.multiple_of(t * rows, rows)
            xr, xi = br_ref[pl.ds(r0, rows), :], bi_ref[pl.ds(r0, rows), :]
            for k in (1, 2, 4):
                sr = _shift_rows(xr, k, row, rows, reverse)
                si = _shift_rows(xi, k, row, rows, reverse)
                ar, ai = _cmul(lk[k][0], lk[k][1], sr, si)
                xr, xi = xr + ar, xi + ai
            ar, ai = _cmul(pr, pi, cr, ci)
            xr, xi = xr + ar, xi + ai
            or_ref[pl.ds(r0, rows), :] = xr
            oi_ref[pl.ds(r0, rows), :] = xi
            if reverse:
                return xr[0:1], xi[0:1]
            return xr[rows - 1:rows], xi[rows - 1:rows]

        zero = jnp.zeros((1, ln), F32)
        lax.fori_loop(0, nt, step, (zero, zero))

    vec = pl.BlockSpec((1, ln), lambda j: (0, j))
    blk = pl.BlockSpec((s, ln), lambda j: (0, j))
    return _pc(body, "ssm_scan_bwd" if reverse else "ssm_scan_fwd", (SSM_L // ln,), [vec, vec, blk, blk], [blk, blk],
               [_sds((s, SSM_L))] * 2)(lam_re, lam_im, b_re, b_im)


def _ssm_in(name, v, w_bd):
    s = v.shape[0]
    ts = _tile(s)
    half = SSM_GB * SSM_P

    def body(v_ref, w_ref, or_ref, oi_ref):
        r = jnp.dot(_bf(v_ref[...]), w_ref[...], preferred_element_type=F32)
        or_ref[...] = r[:, :half]
        oi_ref[...] = r[:, half:]

    out = pl.BlockSpec((ts, half), lambda q, t: (t, q))
    return _pc(body, name, (SSM_NB, s // ts),
               [pl.BlockSpec((ts, SSM_GB * SSM_N), lambda q, t: (t, q)), pl.BlockSpec((None, SSM_GB * SSM_N, 2 * half), lambda q, t: (q, 0, 0))],
               [out, out], [_sds((s, SSM_L))] * 2)(v, w_bd)


def _ssm_out(name, x_re, x_im, w_bd):
    s = x_re.shape[0]
    ts = _tile(s)
    half = SSM_GB * SSM_P
    nt = (((1,), (1,)), ((), ()))

    def body(xr_ref, xi_ref, w_ref, o_ref):
        w = w_ref[...]
        o_ref[...] = (lax.dot_general(_bf(xr_ref[...]), w[:, :half], nt, preferred_element_type=F32)
                      + lax.dot_general(_bf(xi_ref[...]), w[:, half:], nt, preferred_element_type=F32))

    xin = pl.BlockSpec((ts, half), lambda q, t: (t, q))
    return _pc(body, name, (SSM_NB, s // ts),
               [xin, xin, pl.BlockSpec((None, SSM_GB * SSM_N, 2 * half), lambda q, t: (q, 0, 0))],
               pl.BlockSpec((ts, SSM_GB * SSM_N), lambda q, t: (t, q)), _sds((s, SSM_G * SSM_N)))(x_re, x_im, w_bd)


def _ssm_outer(name, v, x_re, x_im):
    s = v.shape[0]
    ts = min(s, 512)
    nt = s // ts
    half = SSM_GB * SSM_P
    rows = SSM_GB * SSM_N
    tn = (((0,), (0,)), ((), ()))

    def body(v_ref, xr_ref, xi_ref, or_ref, oi_ref, acc_ref):
        vv = _bf(v_ref[...])
        pr = lax.dot_general(vv, _bf(xr_ref[...]), tn, preferred_element_type=F32)
        pi = lax.dot_general(vv, _bf(xi_ref[...]), tn, preferred_element_type=F32)
        t = pl.program_id(1)

        @pl.when(t == 0)
        def _():
            acc_ref[:, :half] = pr
            acc_ref[:, half:] = pi

        @pl.when(t > 0)
        def _():
            acc_ref[:, :half] += pr
            acc_ref[:, half:] += pi

        @pl.when(t == nt - 1)
        def _():
            row_g = lax.broadcasted_iota(jnp.int32, (rows, LANE), 0) // SSM_N
            lane_g = lax.broadcasted_iota(jnp.int32, (rows, LANE), 1) // SSM_P
            for part, o_ref in enumerate((or_ref, oi_ref)):
                fold = jnp.zeros((rows, LANE), F32)
                for cb in range(half // LANE):
                    blk = acc_ref[:, part * half + cb * LANE:part * half + (cb + 1) * LANE]
                    fold = fold + jnp.where(2 * cb + lane_g == row_g, blk, 0.0)
                o_ref[...] = jnp.where(row_g % 2 == 0, fold, pltpu.roll(fold, SSM_P, 1))

    xin = pl.BlockSpec((ts, half), lambda q, t: (t, q))
    out = pl.BlockSpec((None, rows, LANE), lambda q, t: (q, 0, 0))
    return _pc(body, name, (SSM_NB, nt), [pl.BlockSpec((ts, rows), lambda q, t: (t, q)), xin, xin], [out, out],
               [_sds((SSM_NB, rows, LANE))] * 2, scratch=[pltpu.VMEM((rows, 2 * half), F32)])(v, x_re, x_im)


def _ssm_dlam(x_re, x_im, a_re, a_im):
    s = x_re.shape[0]
    ln = SCAN_LANES

    def body(xr_ref, xi_ref, ar_ref, ai_ref, or_ref, oi_ref):
        row = lax.broadcasted_iota(jnp.int32, (s, ln), 0)
        xr = _shift_rows(xr_ref[...], 1, row, s, False)
        xi = _shift_rows(xi_ref[...], 1, row, s, False)
        ar, ai = ar_ref[...], ai_ref[...]
        or_ref[...] = _colsum(xr * ar + xi * ai)
        oi_ref[...] = _colsum(xr * ai - xi * ar)

    blk = pl.BlockSpec((s, ln), lambda j: (0, j))
    vec = pl.BlockSpec((1, ln), lambda j: (0, j))
    return _pc(body, "ssm_dlam", (SSM_L // ln,), [blk] * 4, [vec, vec], [_sds((1, SSM_L))] * 2)(x_re, x_im, a_re, a_im)


def _ssm_act_fwd(y, u, d_skip):
    s = y.shape[0]
    ts = min(s, 512)

    def body(y_ref, u_ref, d_ref, o_ref):
        o_ref[...] = _gelu(y_ref[...] + d_ref[...] * u_ref[...]).astype(BF16)

    return _pc(body, "ssm_act_fwd", (s // ts,), [_row_spec(ts, D)] * 2 + [_vec_spec(D)], _row_spec(ts, D),
               _sds((s, D), BF16))(y, u, d_skip)


def _ssm_act_bwd(dg, y, u, d_skip):
    s = y.shape[0]
    ts = min(s, 512)

    def body(dg_ref, y_ref, u_ref, d_ref, dy_ref, dd_ref):
        uv = u_ref[...]
        dy = dg_ref[...] * _gelu_grad(y_ref[...] + d_ref[...] * uv)
        dy_ref[...] = dy.astype(BF16)
        _acc(dd_ref, pl.program_id(0) == 0, _colsum(dy * uv))

    return _pc(body, "ssm_act_bwd", (s // ts,), [_row_spec(ts, D)] * 3 + [_vec_spec(D)], [_row_spec(ts, D), _vec_spec(D)],
               [_sds((s, D), BF16), _sds((1, D))])(dg, y, u, d_skip)


def _axpy(a, b, d_skip):
    s = a.shape[0]
    ts = min(s, 512)

    def body(a_ref, b_ref, d_ref, o_ref):
        o_ref[...] = (a_ref[...] + d_ref[...] * b_ref[...].astype(F32)).astype(BF16)

    return _pc(body, "ssm_du", (s // ts,), [_row_spec(ts, D)] * 2 + [_vec_spec(D)], _row_spec(ts, D),
               _sds((s, D), BF16))(a, b, d_skip)


def _glu_fwd(zz):
    s = zz.shape[0]
    ts = min(s, 512)

    def body(a_ref, b_ref, o_ref):
        o_ref[...] = a_ref[...] * _sigmoid(b_ref[...])

    return _pc(body, "glu_fwd", (s // ts,), [_row_spec(ts, D, 0), _row_spec(ts, D, 1)], _row_spec(ts, D), _sds((s, D)))(zz, zz)


def _glu_bwd(zz, df):
    s = zz.shape[0]
    ts = min(s, 512)

    def body(a_ref, b_ref, df_ref, o_ref):
        sg = _sigmoid(b_ref[...])
        dfv = df_ref[...].astype(F32)
        o_ref[:, :D] = (dfv * sg).astype(BF16)
        o_ref[:, D:] = (dfv * a_ref[...] * sg * (1.0 - sg)).astype(BF16)

    return _pc(body, "glu_bwd", (s // ts,), [_row_spec(ts, D, 0), _row_spec(ts, D, 1), _row_spec(ts, D)],
               _row_spec(ts, 2 * D), _sds((s, 2 * D), BF16))(zz, zz, df)


def _ssm_block_diag(m_re, m_im):
    rows, half = SSM_GB * SSM_N, SSM_GB * SSM_P
    expand = jnp.tile(jnp.eye(SSM_P, dtype=BF16), (1, SSM_GB))

    def body(mr_ref, mi_ref, e_ref, o_ref):
        keep = (lax.broadcasted_iota(jnp.int32, (rows, half), 0) // SSM_N
                == lax.broadcasted_iota(jnp.int32, (rows, half), 1) // SSM_P)
        for part, m_ref in enumerate((mr_ref, mi_ref)):
            t = jnp.dot(_bf(m_ref[...]), e_ref[...], preferred_element_type=F32)
            o_ref[:, part * half:(part + 1) * half] = jnp.where(keep, t, 0.0).astype(BF16)

    blk = pl.BlockSpec((rows, SSM_P), lambda q: (q, 0))
    return _pc(body, "ssm_block_diag", (SSM_NB,), [blk, blk, pl.BlockSpec((SSM_P, half), lambda q: (0, 0))],
               pl.BlockSpec((None, rows, 2 * half), lambda q: (q, 0, 0)), _sds((SSM_NB, rows, 2 * half), BF16))(m_re, m_im, expand)


def _mod_part(c_all, ada_w):
    n = ada_w.shape[-1]

    def body(c_ref, w_ref, o_ref):
        cv = c_ref[...]
        cond = _bf(cv * _sigmoid(cv))
        o_ref[...] = jnp.dot(cond, _bf(w_ref[...]), preferred_element_type=F32)

    return _pc(body, "mod_part", (2,), [pl.BlockSpec((N_DEV, D), lambda l: (0, 0)), pl.BlockSpec((None, D, n), lambda l: (l, 0, 0))],
               pl.BlockSpec((None, N_DEV, n), lambda l: (l, 0, 0)), _sds((2, N_DEV, n)))(c_all, ada_w)


def _ada_w_grad(c_all_t, dmod):
    n = dmod.shape[-1]
    tr = 128

    def body(c_ref, d_ref, o_ref):
        cv = c_ref[...]
        cond = _bf(cv * _sigmoid(cv)).astype(F32)
        dm = _bf(d_ref[...]).astype(F32)
        acc = cond[:, 0:1] * dm[0:1, :]
        for b in range(1, N_DEV):
            acc = acc + cond[:, b:b + 1] * dm[b:b + 1, :]
        o_ref[...] = acc

    return _pc(body, "ada_w_grad", (2, D // tr),
               [pl.BlockSpec((tr, N_DEV), lambda l, t: (t, 0)), pl.BlockSpec((None, N_DEV, n), lambda l, t: (l, 0, 0))],
               pl.BlockSpec((None, tr, n), lambda l, t: (l, t, 0)), _sds((2, D, n)))(c_all_t, dmod)


def _adamw(name, parts, w, m, v, slot=0, prev=None):
    p, r, c = parts.shape
    tr = r
    while tr * c * 4 > (1 << 20) and tr % 16 == 0:
        tr //= 2
    nt = r // tr

    def body(p_ref, w_ref, m_ref, v_ref, *rest):
        g_ref, d_ref, nm_ref, nv_ref = rest[-4:]
        g = p_ref[0].astype(F32)
        for i in range(1, p):
            g = g + p_ref[i].astype(F32)
        g_ref[...] = g
        m2 = B1 * m_ref[...] + (1.0 - B1) * g
        v2 = B2 * v_ref[...] + (1.0 - B2) * (g * g)
        nm_ref[...] = m2
        nv_ref[...] = v2
        m_hat = m2 / (1.0 - B1 ** STEP)
        v_hat = v2 / (1.0 - B2 ** STEP)
        d_ref[...] = -LR * (m_hat / (jnp.sqrt(v_hat) + ADAM_EPS) + WD * w_ref[...])

    blk = pl.BlockSpec((tr, c), lambda t: (slot * nt + t, 0))
    in_specs = [pl.BlockSpec((p, tr, c), lambda t: (0, t, 0)), blk, blk, blk]
    if prev is None:
        return _pc(body, name, (nt,), in_specs, [blk] * 4, [_sds(w.shape)] * 4)(parts, w, m, v)
    return pl.pallas_call(
        body, name=name, grid=(nt,), in_specs=in_specs + [pl.BlockSpec(memory_space=pl.ANY)] * 4, out_specs=[blk] * 4,
        out_shape=[_sds(w.shape)] * 4, input_output_aliases={4 + i: i for i in range(4)},
        compiler_params=pltpu.CompilerParams(dimension_semantics=("arbitrary",), vmem_limit_bytes=VMEM_LIMIT_BYTES))(parts, w, m, v, *prev)


def _sum_parts(parts):
    p, r, c = parts.shape
    tr = r
    while tr * c * 4 > (1 << 19) and tr % 16 == 0:
        tr //= 2

    def body(p_ref, o_ref):
        g = p_ref[0]
        for i in range(1, p):
            g = g + p_ref[i]
        o_ref[...] = g

    return _pc(body, "sum_parts", (r // tr,), [pl.BlockSpec((p, tr, c), lambda t: (0, t, 0))], pl.BlockSpec((tr, c), lambda t: (t, 0)),
               _sds((r, c)))(parts)


def _place():
    x, y, c = lax.axis_index("x"), lax.axis_index("y"), lax.axis_index("c")
    peers = []
    for k in range(1, N_DEV):
        px = (1 - x) if k & 4 else x
        py = (1 - y) if k & 2 else y
        pc = (1 - c) if k & 1 else c
        peers.append(((px, py, pc), 4 * px + 2 * py + pc))
    return 4 * x + 2 * y + c, peers


def _at(ref, idx):
    return ref if idx is None else ref.at[idx]


def _exchange_copies(plan, n, src_refs, dst_refs, send_sems, recv_sems, local_sems=None, with_arrivals=True):
    me, peers = _place()
    local = [] if local_sems is None else [
        pltpu.make_async_copy(_at(src_refs[si], sx), _at(dst_refs[di], dx), local_sems.at[i])
        for i, (si, sx, di, dx) in enumerate(plan(me, me, 0))]

    def remote(k, i, dev, entry):
        si, sx, di, dx = entry
        return pltpu.make_async_remote_copy(_at(src_refs[si], sx), _at(dst_refs[di], dx), send_sems.at[k * n + i], recv_sems.at[k * n + i],
                                            device_id=dev, device_id_type=MESH)

    sends = [remote(k, i, dev, e) for k, (dev, peer) in enumerate(peers) for i, e in enumerate(plan(me, peer, k + 1))]
    if not with_arrivals:
        return local, sends, []
    arrivals = [remote(k, i, dev, e) for k, (dev, peer) in enumerate(peers) for i, e in enumerate(plan(peer, me, k + 1))]
    return local, sends, arrivals


def _sem_shapes(n_copies, local=True):
    sems = [pltpu.SemaphoreType.DMA(((N_DEV - 1) * n_copies,)), pltpu.SemaphoreType.DMA(((N_DEV - 1) * n_copies,))]
    return sems + [pltpu.SemaphoreType.DMA((n_copies,))] if local else sems


def _exchange(name, srcs, dst_shapes, plan, n_copies):
    ns, nd = len(srcs), len(dst_shapes)

    def body(*refs):
        local, sends, arrivals = _exchange_copies(plan, n_copies, refs[:ns], refs[ns:ns + nd], *refs[ns + nd:])
        for cp in local + sends:
            cp.start()
        for cp in arrivals:
            cp.wait_recv()
        for cp in sends:
            cp.wait_send()
        for cp in local:
            cp.wait()

    any_spec = pl.BlockSpec(memory_space=pl.ANY)
    return pl.pallas_call(
        body, name=name, in_specs=[any_spec] * ns, out_specs=[any_spec] * nd, out_shape=list(dst_shapes),
        scratch_shapes=_sem_shapes(n_copies))(*srcs)


HBM_SPEC = pl.BlockSpec(memory_space=pltpu.HBM)
SEM_SPEC = pl.BlockSpec(memory_space=pltpu.SEMAPHORE)
ANY_SPEC = pl.BlockSpec(memory_space=pl.ANY)
TOKEN_SPEC = pl.BlockSpec(memory_space=pltpu.VMEM)
SIDE_EFFECT = pltpu.SideEffectType.DATAFLOW_SIDE_EFFECTING


def _wait_all(local, sends, arrivals):
    for cp in arrivals:
        cp.wait_recv()
    for cp in sends:
        cp.wait_send()
    for cp in local:
        cp.wait()


def _exchange_start(name, srcs, dst_shapes, plan, n_copies, order):
    ns, nd = len(srcs), len(dst_shapes)
    nb = ns + nd

    def body(*refs):
        local, sends, _ = _exchange_copies(plan, n_copies, refs[:ns], refs[ns:nb], *refs[nb + 1:nb + 4], with_arrivals=False)
        for cp in local + sends:
            cp.start()
        refs[-1][...] = jnp.zeros((8, LANE), F32)

    lands = [pltpu.with_memory_space_constraint(lax.empty(d.shape, d.dtype), pltpu.HBM) for d in dst_shapes]
    srcs = [pltpu.with_memory_space_constraint(a, pltpu.HBM) for a in srcs]
    bufs = srcs + lands
    out = pl.pallas_call(
        body, name=name, in_specs=[HBM_SPEC] * nb + [ANY_SPEC],
        out_specs=[SEM_SPEC] * 3 + [HBM_SPEC] * nb + [TOKEN_SPEC],
        out_shape=_sem_shapes(n_copies) + [pltpu.HBM(a.shape, a.dtype) for a in bufs] + [_sds((8, LANE))],
        input_output_aliases={i: 3 + i for i in range(nb)},
        compiler_params=pltpu.CompilerParams(has_side_effects=SIDE_EFFECT))(*bufs, order)
    return out[:3], out[3:3 + ns], out[3 + ns:3 + nb], out[-1]


def _exchange_relay(name, sems, srcs, lands, plan, n_copies, plan2, n_copies2, after):
    ns, nd = len(srcs), len(lands)
    nb = ns + nd

    def body(*refs):
        land_refs = refs[ns:nb]
        _wait_all(*_exchange_copies(plan, n_copies, refs[:ns], land_refs, *refs[nb:nb + 3]))
        _, sends, _ = _exchange_copies(plan2, n_copies2, land_refs, land_refs, *refs[nb + 4:nb + 6], with_arrivals=False)
        for cp in sends:
            cp.start()
        refs[-1][...] = jnp.zeros((8, LANE), F32)

    out = pl.pallas_call(
        body, name=name, in_specs=[HBM_SPEC] * nb + [SEM_SPEC] * 3 + [ANY_SPEC],
        out_specs=[SEM_SPEC] * 2 + [HBM_SPEC] * nd + [TOKEN_SPEC],
        out_shape=_sem_shapes(n_copies2, local=False) + [pltpu.HBM(a.shape, a.dtype) for a in lands] + [_sds((8, LANE))],
        input_output_aliases={ns + i: 2 + i for i in range(nd)},
        compiler_params=pltpu.CompilerParams(has_side_effects=SIDE_EFFECT))(*srcs, *lands, *sems, after)
    return out[:2], out[2:2 + nd], out[-1]


def _exchange_wait(name, sems, srcs, lands, plan, n_copies, after):
    srcs = [] if srcs is None else list(srcs)
    ns, nd = len(srcs), len(lands)
    nb = ns + nd

    def body(*refs):
        land_refs = refs[ns:nb]
        _wait_all(*_exchange_copies(plan, n_copies, refs[:ns] if ns else land_refs, land_refs, *refs[nb:nb + len(sems)]))

    bufs = srcs + list(lands)
    out = pl.pallas_call(
        body, name=name, in_specs=[HBM_SPEC] * nb + [SEM_SPEC] * len(sems) + [ANY_SPEC],
        out_specs=[HBM_SPEC] * nb, out_shape=[pltpu.HBM(a.shape, a.dtype) for a in bufs],
        input_output_aliases={i: i for i in range(nb)},
        compiler_params=pltpu.CompilerParams(has_side_effects=SIDE_EFFECT))(*bufs, *sems, after)
    return out[ns:]


def _all_gather(name, arrs):
    plan = lambda me, peer, k: [(i, None, i, me) for i in range(len(arrs))]
    return _exchange(name, arrs, [_sds((N_DEV,) + a.shape, a.dtype) for a in arrs], plan, len(arrs))


def _sublayer_fwd(x, fn, mod3, g_pre, g_post, rw):
    h = _prenorm_fwd(x, g_pre, mod3[1:2], mod3[0:1])
    f, saved = fn(h)
    return _postnorm_fwd(x, f, g_post, mod3[2:3], rw), (x, f, saved)


def _sublayer_bwd(dout, saved, fn_bwd, mod3, g_pre, g_post, rw):
    x, f, inner = saved
    df, dgate, dg_post = _postnorm_bwd(dout, f, g_post, mod3[2:3], rw)
    dh, extra = fn_bwd(df, inner)
    dx, dshift, dscale, dg_pre = _prenorm_bwd(dout, dh, x, g_pre, mod3[1:2])
    return dx, jnp.concatenate([dshift, dscale, dgate], axis=0), dg_pre, dg_post, extra


def _mix0_fwd(h, p):
    z = _mm_nt("mix0_in", h, p["ab_w_in"])
    y_a, d = _pool_fwd(z, p["pool_w"], p["pool_scale"])
    y_b = _sgu_fwd(z, p["sgu_ln_g"], p["sgu_ln_b"], p["sgu_w"], p["sgu_bt"])
    ycat = jnp.concatenate([y_a, y_b], axis=1)
    return _mm_nn("mix0_out", ycat, p["ab_w_out"]), (h, z, d, ycat)


def _mix0_bwd(df, saved, p):
    h, z, d, ycat = saved
    dycat = _mm_nt("mix0_out_dx", df, p["ab_w_out"])
    g = {"ab_w_out": _mm_tn("mix0_out_dw", ycat, df, BF16)}
    dz_p, g["pool_w"], g["pool_scale"] = _pool_bwd(dycat, d, p["pool_w"], p["pool_scale"])
    dz_u, dz_v, g["sgu_ln_g"], g["sgu_ln_b"], g["sgu_w"], dbt = _sgu_bwd(
        z, dycat, p["sgu_ln_g"], p["sgu_ln_b"], p["sgu_w"], p["sgu_bt"], p["head_sum"])
    g["sgu_b"] = dbt[:, :NH].T
    dz = jnp.concatenate([dz_p, dz_u, dz_v], axis=1)
    g["ab_w_in"] = _mm_tn("mix0_in_dw", dz, h, BF16)
    return _mm_nn("mix0_in_dx", dz, p["ab_w_in"]), g


def _mix1_fwd(h, p):
    u = _mm_nn("ssm_w_in", h, p["ssm_w_in"])
    bu_re, bu_im = _ssm_in("ssm_bu", u, p["wb_bd"])
    x_re, x_im = _ssm_scan(bu_re, bu_im, p["lam_bar_re"], p["lam_bar_im"], False)
    y = _ssm_out("ssm_y", x_re, x_im, p["wc_bd"])
    g = _ssm_act_fwd(y, u, p["ssm_d"])
    zz = _mm_nn("ssm_glu", g, p["ssm_w_glu"])
    return _glu_fwd(zz), (h, u, x_re, x_im, y, g, zz)


def _mix1_bwd(df, saved, p):
    h, u, x_re, x_im, y, g, zz = saved
    gr = {}
    dzz = _glu_bwd(zz, df)
    dg = _mm_nt("ssm_glu_dx", dzz, p["ssm_w_glu"])
    gr["ssm_w_glu"] = _mm_tn("ssm_glu_dw", g, dzz, BF16)
    dy, gr["ssm_d"] = _ssm_act_bwd(dg, y, u, p["ssm_d"])
    gx_re, gx_im = _ssm_in("ssm_gx", dy, p["wct_bd"])
    a_re, a_im = _ssm_scan(gx_re, gx_im, p["lam_bar_re"], -p["lam_bar_im"], True)
    du_ssm = _ssm_out("ssm_du_mm", a_re, a_im, p["wbt_bd"])
    du = _axpy(du_ssm, dy, p["ssm_d"])
    gr["ssm_w_in"] = _mm_tn("ssm_w_in_dw", h, du, BF16)
    dh = _mm_nt("ssm_w_in_dx", du, p["ssm_w_in"])
    g_lam_re, g_lam_im = _ssm_dlam(x_re, x_im, a_re, a_im)
    mb_re, mb_im = _ssm_outer("ssm_db", u, a_re, a_im)
    mc_re, mc_im = _ssm_outer("ssm_dc", dy, x_re, x_im)
    per_group = lambda m: m[:, :, :SSM_P].reshape(SSM_G, SSM_N, SSM_P)
    gr["ssm_c_re"] = per_group(mc_re)
    gr["ssm_c_im"] = -per_group(mc_im)
    dlr, dli, ddt, dbr, dbi = _ssm_param_bwd(
        g_lam_re.reshape(SSM_G, SSM_P), g_lam_im.reshape(SSM_G, SSM_P),
        per_group(mb_re).reshape(SSM_G, SSM_N * SSM_P), per_group(mb_im).reshape(SSM_G, SSM_N * SSM_P),
        p["lam_re"], p["lam_im"], p["lam_re_rep"], p["lam_im_rep"], p["log_dt"], p["b_re"], p["b_im"], p["seg"])
    gr["ssm_lam_re"], gr["ssm_lam_im"], gr["ssm_log_dt"] = dlr, dli, ddt[:, 0]
    gr["ssm_b_re"] = dbr.reshape(SSM_G, SSM_N, SSM_P).transpose(0, 2, 1)
    gr["ssm_b_im"] = dbi.reshape(SSM_G, SSM_N, SSM_P).transpose(0, 2, 1)
    return dh, gr


def _ssm_params(lam_re, lam_im, b_re, b_im, c_re, c_im, log_dt):
    wide = lambda b: b.transpose(0, 2, 1).reshape(SSM_G, SSM_N * SSM_P)
    p = {"lam_re": lam_re, "lam_im": lam_im, "log_dt": log_dt.reshape(SSM_G, 1),
         "lam_re_rep": jnp.tile(lam_re, (1, SSM_N)), "lam_im_rep": jnp.tile(lam_im, (1, SSM_N)), "b_re": wide(b_re), "b_im": wide(b_im)}
    lbr, lbi, bbr, bbi = _ssm_prep(lam_re, lam_im, p["lam_re_rep"], p["lam_im_rep"], p["log_dt"], p["b_re"], p["b_im"])
    p["lam_bar_re"], p["lam_bar_im"] = lbr.reshape(1, SSM_L), lbi.reshape(1, SSM_L)
    rows = lambda m: m.reshape(SSM_G * SSM_N, SSM_P)
    p["wb_bd"] = p["wbt_bd"] = _ssm_block_diag(rows(bbr), rows(bbi))
    p["wc_bd"] = p["wct_bd"] = _ssm_block_diag(rows(c_re), rows(-c_im))
    p["seg"] = jnp.tile(jnp.eye(SSM_P, dtype=F32), (SSM_N, 1))
    return p


RES_WEIGHT = (0.5, 1.0, 0.5)


def _local_step(x, tgt, mod, norm_pre, norm_post, weights_of, on_part, on_grads):
    def fns(i, w):
        if i % 3 != 1:
            win, wout_of = w
            return ((lambda h: _ffn_fwd(h, win, wout_of)),
                    (lambda df, sv: (_ffn_bwd(df, sv, win, wout_of(None), lambda tag, part: on_part(i, tag, part)), None)))
        if i == 1:
            return (lambda h: _mix0_fwd(h, w)), (lambda df, sv: _mix0_bwd(df, sv, w))
        return (lambda h: _mix1_fwd(h, w)), (lambda df, sv: _mix1_bwd(df, sv, w))

    saved, bwd = [], []
    for i in range(6):
        l, s = divmod(i, 3)
        w, token = weights_of(i, x)
        f, b = fns(i, w)
        x, sv = _sublayer_fwd(x, f, mod[l, s] + token[0:1, 0:1], norm_pre[l, s][None], norm_post[l, s][None], RES_WEIGHT[s])
        saved.append(sv)
        bwd.append(b)
    loss_row, dx = _loss_fwd_bwd(x, tgt)
    token = jnp.zeros((8, LANE), F32)
    for i in reversed(range(6)):
        l, s = divmod(i, 3)
        mod3 = mod[l, s] + token[0:1, 0:1]
        dx, dmod, dpre, dpost, extra = _sublayer_bwd(
            dx, saved[i], bwd[i], mod3, norm_pre[l, s][None], norm_post[l, s][None], RES_WEIGHT[s])
        token = on_grads(i, extra, dmod, dpre, dpost, loss_row)
    return dx


def _pad_rows(v, rows):
    return jnp.pad(v, (0, rows * LANE - v.shape[0])).reshape(rows, LANE)


def _pack(parts):
    flat, layout, off = [], [], 0
    for a in parts:
        n = a.size
        padded = -(-n // LANE) * LANE
        flat.append(jnp.pad(a.reshape(-1).astype(F32), (0, padded - n)))
        layout.append((off, n, a.shape))
        off += padded
    return jnp.concatenate(flat), layout


def _unpack(flat, layout):
    return [flat[off:off + n].reshape(shape) for off, n, shape in layout]


SMALL_REPLICATED = ["ada_b", "pool_w", "pool_scale", "sgu_ln_g", "sgu_ln_b", "sgu_w", "sgu_b", "ssm_lam_re", "ssm_lam_im",
                    "ssm_b_re", "ssm_b_im", "ssm_c_re", "ssm_c_im", "ssm_log_dt"]
SMALL_SHARDED = ["norm_pre", "norm_post", "ssm_d"]
TRANSPOSED = ["ffn_w_in", "ab_w_in"]
WEIGHTS = ['ada_w', 'ada_b', 'norm_pre', 'norm_post', 'ffn_w_in', 'ffn_w_out', 'ab_w_in', 'pool_w', 'pool_scale', 'sgu_ln_g',
           'sgu_ln_b', 'sgu_w', 'sgu_b', 'ab_w_out', 'ssm_w_in', 'ssm_lam_re', 'ssm_lam_im', 'ssm_b_re', 'ssm_b_im', 'ssm_c_re',
           'ssm_c_im', 'ssm_d', 'ssm_log_dt', 'ssm_w_glu']


def kernel(x, c, ada_w, ada_b, norm_pre, norm_post, ffn_w_in, ffn_w_out, ab_w_in, pool_w, pool_scale, sgu_ln_g, sgu_ln_b, sgu_w, sgu_b, ab_w_out, ssm_w_in, ssm_lam_re, ssm_lam_im, ssm_b_re, ssm_b_im, ssm_c_re, ssm_c_im, ssm_d, ssm_log_dt, ssm_w_glu, loss_target, m_ada_w, m_ada_b, m_norm_pre, m_norm_post, m_ffn_w_in, m_ffn_w_out, m_ab_w_in, m_pool_w, m_pool_scale, m_sgu_ln_g, m_sgu_ln_b, m_sgu_w, m_sgu_b, m_ab_w_out, m_ssm_w_in, m_ssm_lam_re, m_ssm_lam_im, m_ssm_b_re, m_ssm_b_im, m_ssm_c_re, m_ssm_c_im, m_ssm_d, m_ssm_log_dt, m_ssm_w_glu, v_ada_w, v_ada_b, v_norm_pre, v_norm_post, v_ffn_w_in, v_ffn_w_out, v_ab_w_in, v_pool_w, v_pool_scale, v_sgu_ln_g, v_sgu_ln_b, v_sgu_w, v_sgu_b, v_ab_w_out, v_ssm_w_in, v_ssm_lam_re, v_ssm_lam_im, v_ssm_b_re, v_ssm_b_im, v_ssm_c_re, v_ssm_c_im, v_ssm_d, v_ssm_log_dt, v_ssm_w_glu):
    args = locals()
    wts = {n: args[n] for n in WEIGHTS}
    mom = {n: args["m_" + n] for n in WEIGHTS}
    var = {n: args["v_" + n] for n in WEIGHTS}
    for n in TRANSPOSED:
        for t in (wts, mom, var):
            t[n] = jnp.swapaxes(t[n], -1, -2)
    me = 4 * lax.axis_index("x") + 2 * lax.axis_index("y") + lax.axis_index("c")
    s = x.shape[1]
    nd = D // N_DEV

    small_in, small_in_layout = _pack([c, norm_pre, norm_post, ssm_d])
    small_rows = -(-small_in.shape[0] // (8 * LANE)) * 8
    (g_small,) = _all_gather("gather_small", [_pad_rows(small_in, small_rows)])
    g_small = g_small.reshape(N_DEV, -1)
    c_all, npre_g, npost_g, sd_g = [jnp.stack([_unpack(g_small[j], small_in_layout)[i] for j in range(N_DEV)]) for i in range(4)]
    c_all = c_all.reshape(N_DEV, D)
    norm_pre_full = npre_g.transpose(1, 2, 0, 3).reshape(2, 3, D)
    norm_post_full = npost_g.transpose(1, 2, 0, 3).reshape(2, 3, D)
    ssm_d_full = sd_g.transpose(1, 0, 2).reshape(1, D)

    nw = ada_w.shape[-1]
    (mod_g,) = _all_gather("gather_mod", [_mod_part(c_all, ada_w)])
    mod = lax.dynamic_index_in_dim(mod_g, me, axis=2, keepdims=False)
    mod = (mod.transpose(1, 0, 2).reshape(2, N_DEV * nw) + ada_b).reshape(2, 3, 3, D)

    w_in_t = wts["ffn_w_in"]
    shards = [[w_in_t[0, 0]], [ffn_w_out[0, 0]], [wts["ab_w_in"][0], ab_w_out[0]], [w_in_t[0, 1], ffn_w_out[0, 1]],
              [w_in_t[1, 0], ffn_w_out[1, 0]], [ssm_w_in[0], ssm_w_glu[0]], [w_in_t[1, 1], ffn_w_out[1, 1]]]
    same_core = (2, 4, 6)

    def gather_plan(n):
        return lambda me_, peer_, k: [(a, None, a, me_) for a in range(n)] if k in (0, 1) + same_core else []

    def relay_plan(n):
        return lambda me_, peer_, k: [(a, me_ ^ kk, a, me_ ^ kk) for kk in same_core for a in range(n)] if k == 1 else []

    gathers, relays = [], {}
    token = mod_g
    for g, group in enumerate(shards):
        group = [a.astype(BF16) for a in group]
        sems, srcs_thru, lands, token = _exchange_start(
            f"gather_start_{g}", group, [_sds((N_DEV,) + a.shape, BF16) for a in group], gather_plan(len(group)), len(group), token)
        gathers.append((sems, srcs_thru, lands))
    mod = mod + token[0, 0]

    def relay(g, after):
        sems, srcs_thru, lands = gathers[g]
        n = len(lands)
        relays[g] = _exchange_relay(f"gather_relay_{g}", sems, srcs_thru, lands, gather_plan(n), n, relay_plan(n), 3 * n, after)

    def fetch(g, after):
        if g not in relays:
            relay(g, after)
        sems, lands, token = relays[g]
        n = len(lands)
        got = _exchange_wait(f"gather_wait_{g}", sems, None, lands, relay_plan(n), 3 * n, after)
        if g + 1 < len(gathers):
            relay(g + 1, got[0])
            token = relays[g + 1][2]
        return got, token

    head_sum = jnp.repeat(jnp.eye(NH, LANE, dtype=F32), HD, axis=0)
    mix0 = {"pool_w": pool_w[0], "pool_scale": pool_scale, "sgu_ln_g": sgu_ln_g, "sgu_ln_b": sgu_ln_b, "sgu_w": sgu_w[0],
            "sgu_bt": jnp.pad(sgu_b[0].T, ((0, 0), (0, LANE - NH))), "head_sum": head_sum}
    mix1 = _ssm_params(ssm_lam_re[0], ssm_lam_im[0], ssm_b_re[0], ssm_b_im[0], ssm_c_re[0], ssm_c_im[0], ssm_log_dt[0])
    mix1["ssm_d"] = ssm_d_full

    def weights_of(i, x_in):
        if i == 0:
            (win,), token = fetch(0, x_in)
            cache = []

            def wout_of(z):
                if not cache:
                    cache.append(fetch(1, z)[0][0])
                return cache[0]

            return (win, wout_of), token
        (a, b), token = fetch(i + 1, x_in)
        if i % 3 != 1:
            return (a, lambda z: b), token
        if i == 1:
            return dict(mix0, ab_w_in=a.reshape(-1, D), ab_w_out=b.reshape(D, D)), token
        return dict(mix1, ssm_w_in=a.reshape(D, D), ssm_w_glu=b.transpose(1, 0, 2).reshape(D, -1)), token

    def shard_cols(a):
        r = a.shape[0]
        return a.reshape(r, N_DEV, -1).transpose(1, 0, 2)

    scatter_plan = lambda me_, peer_, k: [(0, peer_, 0, me_), (1, peer_, 1, me_)]
    scatter_plan1 = lambda me_, peer_, k: [(0, peer_, 0, me_)]
    scatters = []
    last_token = [jnp.zeros((8, LANE), F32)]
    pieces, mixer, bundles = {}, {}, {}
    bundle_plan = lambda me_, peer_, k: [(0, None, 0, me_)]

    def on_part(i, tag, part):
        sems, srcs_thru, lands, last_token[0] = _exchange_start(
            f"scatter_start_{i}_{tag}", [part], [_sds(part.shape, BF16)], scatter_plan1, 1, last_token[0])
        scatters.append((i, ("ffn_" + tag,), scatter_plan1, sems, srcs_thru, lands))
        return last_token[0]
    mix0_names = ["pool_w", "pool_scale", "sgu_ln_g", "sgu_ln_b", "sgu_w", "sgu_b"]
    mix1_names = ["ssm_lam_re", "ssm_lam_im", "ssm_b_re", "ssm_b_im", "ssm_c_re", "ssm_c_im", "ssm_log_dt", "ssm_d"]

    def start_bundle(tag, arrays):
        flat, layout = _pack(arrays)
        rows = -(-flat.shape[0] // (8 * LANE)) * 8
        plan = gather_plan(1) if tag == "a" else bundle_plan
        sems, srcs_thru, lands, last_token[0] = _exchange_start(
            f"small_start_{tag}", [_pad_rows(flat, rows)], [_sds((N_DEV, rows, LANE))], plan, 1, last_token[0])
        bundles[tag] = (sems, srcs_thru, lands, layout)

    def on_grads(i, extra, dmod_i, dpre_i, dpost_i, loss_row):
        pieces[i] = (dmod_i, dpre_i, dpost_i)
        if i == 4:
            mixer.update({n: extra[n] for n in mix1_names})
        if i == 1:
            mixer.update({n: extra[n] for n in mix0_names})
            rest = range(1, 6)
            start_bundle("a", [jnp.stack([pieces[j][0] for j in rest])] + [jnp.concatenate([pieces[j][k] for j in rest]) for k in (1, 2)]
                         + [mixer[n] for n in mix0_names + mix1_names])
        if i == 0:
            start_bundle("b", [dmod_i, dpre_i, dpost_i, loss_row])
        if i % 3 != 1:
            return last_token[0]
        if i == 1:
            names, parts = ("ab_w_in", "ab_w_out"), [extra["ab_w_in"].reshape(N_DEV, -1, D), extra["ab_w_out"].reshape(N_DEV, nd, D)]
        else:
            names, parts = ("ssm_w_in", "ssm_w_glu"), [extra["ssm_w_in"].reshape(N_DEV, nd, D), shard_cols(extra["ssm_w_glu"])]
        sems, srcs_thru, lands, last_token[0] = _exchange_start(
            f"scatter_start_{i}", parts, [_sds(a.shape, BF16) for a in parts], scatter_plan, 2, last_token[0])
        scatters.append((i, names, scatter_plan, sems, srcs_thru, lands))
        return last_token[0]

    grad_x = _local_step(x[0], loss_target[0], mod, norm_pre_full, norm_post_full, weights_of, on_part, on_grads)

    out_g, out_d, out_m, out_v = {}, {}, {}, {}
    big_out = {}

    def adam_big(name, recv, n, slot=0):
        c_ = wts[n].shape[-1]
        big_out[n] = _adamw(name, recv.reshape(recv.shape[0], -1, c_), *[t[n].reshape(-1, c_) for t in (wts, mom, var)],
                            slot=slot, prev=big_out.get(n))
        return big_out[n][0]

    ffn_slot = {0: 0, 2: 1, 3: 2, 5: 3}

    def land_and_update(entries, after):
        for i, names, plan, sems, srcs_thru, lands in entries:
            recv = _exchange_wait(f"scatter_wait_{i}_{names[0]}", sems, srcs_thru, lands, plan, len(names), after)
            for n, r in zip(names, recv):
                after = adam_big(f"adamw_{n}_{i}", r, n, ffn_slot.get(i, 0))
        return after

    after = land_and_update([e for e in scatters if e[0] != 0], last_token[0])

    gathered, sums = {}, {}
    for tag in ("a", "b"):
        sems, srcs_thru, lands, layout = bundles[tag]
        if tag == "a":
            sems, lands, _ = _exchange_relay("small_relay_a", sems, srcs_thru, lands, gather_plan(1), 1, relay_plan(1), 3, after)
            (g_parts,) = _exchange_wait("small_wait_a", sems, None, lands, relay_plan(1), 3, after)
        else:
            (g_parts,) = _exchange_wait("small_wait_b", sems, srcs_thru, lands, bundle_plan, 1, after)
        total = _sum_parts(g_parts)
        after = total
        off, n, shape = layout[0]
        gathered[tag] = g_parts.reshape(N_DEV, -1)[:, off:off + n].reshape((N_DEV,) + shape)
        sums[tag] = _unpack(total.reshape(-1), layout)
    dmod_a, dpre_a, dpost_a = sums["a"][:3]
    dmod_b, dpre_b, dpost_b, loss_sum = sums["b"]
    small = dict(zip(mix0_names + mix1_names, sums["a"][3:]))
    small["ada_b"] = jnp.concatenate([dmod_b[None], dmod_a])
    small["norm_pre"] = jnp.concatenate([dpre_b, dpre_a]).reshape(2, 3, D)
    small["norm_post"] = jnp.concatenate([dpost_b, dpost_a]).reshape(2, 3, D)
    loss = loss_sum[0, 0]

    def adam_small(n, g):
        cols = LANE if g.size % LANE == 0 else g.size
        res = _adamw(f"adamw_{n}", g.reshape(1, -1, cols), *[t[n].reshape(-1, cols) for t in (wts, mom, var)])
        for o, arr in zip((out_g, out_d, out_m, out_v), res):
            o[n] = arr.reshape(wts[n].shape)
        return res[0]

    for n in SMALL_REPLICATED:
        after = adam_small(n, small[n])
    for n in SMALL_SHARDED:
        after = adam_small(n, lax.dynamic_slice_in_dim(small[n], me * nd, nd, axis=small[n].ndim - 1))

    dmod_all = jnp.concatenate([gathered["b"][:, None], gathered["a"]], axis=1).reshape(N_DEV, 2, N_DEV, nw)
    dmod_mine = lax.dynamic_index_in_dim(dmod_all, me, axis=2, keepdims=False).transpose(1, 0, 2)
    g_ada_w = _ada_w_grad(c_all.T, dmod_mine)
    after = after[0:1, 0:1] + adam_big("adamw_ada_w", g_ada_w[None], "ada_w")[0:1, 0:1]

    land_and_update([e for e in scatters if e[0] == 0], after)
    for n, res in big_out.items():
        for o, arr in zip((out_g, out_d, out_m, out_v), res):
            o[n] = arr.reshape(wts[n].shape)
            if n in TRANSPOSED:
                o[n] = jnp.swapaxes(o[n], -1, -2)

    return (loss, grad_x[None], *[out_g[n] for n in WEIGHTS], *[out_d[n] for n in WEIGHTS],
            *[out_m[n] for n in WEIGHTS], *[out_v[n] for n in WEIGHTS])
```

```python
import functools
import math

import jax
import jax.numpy as jnp
from jax import lax
from jax.experimental import pallas as pl
from jax.experimental.pallas import tpu as pltpu

F32 = jnp.float32
BF16 = jnp.bfloat16
MESH = pl.DeviceIdType.MESH
HIGHEST = lax.Precision.HIGHEST

N_DEV = 8
D = 1024
D_FF = 2816
FSH = 2 * D_FF // N_DEV
EPS = 1e-6
POOL_WINDOWS = (2, 4, 8, 16)
HD = 128
NH = 4
SSM_G, SSM_P, SSM_N = 64, 64, 16
SSM_GB = 16
SSM_NB = SSM_G // SSM_GB
SSM_L = SSM_G * SSM_P
LR, B1, B2, ADAM_EPS, WD, STEP = 0.001, 0.9, 0.999, 1e-08, 0.01, 10
GELU_C = math.sqrt(2.0 / math.pi)
VMEM_LIMIT_BYTES = 48 * 1024 * 1024
LANE = 128


def _pc(body, name, grid, in_specs, out_specs, out_shape, scratch=()):
    return pl.pallas_call(
        body, name=name, grid=grid, in_specs=in_specs, out_specs=out_specs, out_shape=out_shape,
        scratch_shapes=list(scratch),
        compiler_params=pltpu.CompilerParams(dimension_semantics=("arbitrary",) * len(grid),
                                             vmem_limit_bytes=VMEM_LIMIT_BYTES))


def _sds(shape, dtype=F32):
    return jax.ShapeDtypeStruct(tuple(shape), dtype)


def _bf(v):
    return v if v.dtype == BF16 else v.astype(BF16)


def _row_spec(ts, width, col=0):
    return pl.BlockSpec((ts, width), lambda t, _c=col: (t, _c))


def _vec_spec(width, col=0):
    return pl.BlockSpec((1, width), lambda t, _c=col: (0, _c))


def _mm(name, a, b, contract, grid, a_spec, b_spec, o_spec, out_shape, acc_axis=None, after=None):
    dn = (contract, ((), ()))

    def body(a_ref, b_ref, *rest):
        o_ref = rest[-1]
        r = lax.dot_general(_bf(a_ref[...]), _bf(b_ref[...]), dn, preferred_element_type=F32)
        if acc_axis is None:
            o_ref[...] = r.astype(o_ref.dtype)
        else:
            k = pl.program_id(acc_axis)

            @pl.when(k == 0)
            def _():
                o_ref[...] = r

            @pl.when(k > 0)
            def _():
                o_ref[...] += r

    if after is None:
        return _pc(body, name, grid, [a_spec, b_spec], o_spec, out_shape)(a, b)
    return _pc(body, name, grid, [a_spec, b_spec, pl.BlockSpec(memory_space=pl.ANY)], o_spec, out_shape)(a, b, after)


def _tile(s):
    return min(s, 1024)


def _div_tile(n, cap=1024):
    t = min(n, cap) // LANE * LANE
    while n % t:
        t -= LANE
    return t


def _mm_nn(name, a, b, out_dtype=F32):
    s, k = a.shape
    n = b.shape[1]
    ts, tn = _tile(s), _div_tile(n)
    return _mm(name, a, b, ((1,), (0,)), (n // tn, s // ts),
               pl.BlockSpec((ts, k), lambda j, t: (t, 0)), pl.BlockSpec((k, tn), lambda j, t: (0, j)),
               pl.BlockSpec((ts, tn), lambda j, t: (t, j)), _sds((s, n), out_dtype))


def _mm_nt(name, a, b, out_dtype=F32):
    s, n = a.shape
    k = b.shape[0]
    ts, tk = _tile(s), _div_tile(k)
    return _mm(name, a, b, ((1,), (1,)), (k // tk, s // ts),
               pl.BlockSpec((ts, n), lambda j, t: (t, 0)), pl.BlockSpec((tk, n), lambda j, t: (j, 0)),
               pl.BlockSpec((ts, tk), lambda j, t: (t, j)), _sds((s, k), out_dtype))


def _mm_tn(name, a, b, out_dtype=F32, tm=512, tn=512):
    s, m = a.shape
    n = b.shape[1]
    tm, tn = min(m, tm), min(n, tn)
    return _mm(name, a, b, ((0,), (0,)), (m // tm, n // tn),
               pl.BlockSpec((s, tm), lambda i, j: (0, i)), pl.BlockSpec((s, tn), lambda i, j: (0, j)),
               pl.BlockSpec((tm, tn), lambda i, j: (i, j)), _sds((m, n), out_dtype))


def _rstd(v):
    return lax.rsqrt(jnp.mean(v * v, axis=-1, keepdims=True) + EPS)


def _prenorm_fwd(x, g, scale, shift):
    s = x.shape[0]
    ts = min(s, 512)

    def body(x_ref, g_ref, sc_ref, sh_ref, h_ref):
        xv = x_ref[...]
        h_ref[...] = ((xv * _rstd(xv) * g_ref[...]) * (1.0 + sc_ref[...]) + sh_ref[...]).astype(BF16)

    return _pc(body, "prenorm_fwd", (s // ts,), [_row_spec(ts, D)] + [_vec_spec(D)] * 3, _row_spec(ts, D),
               _sds((s, D), BF16))(x, g, scale, shift)


def _postnorm_fwd(x, f, g, gate, rw):
    s = x.shape[0]
    ts = min(s, 512)

    def body(x_ref, f_ref, g_ref, gt_ref, o_ref):
        fv = f_ref[...]
        o_ref[...] = x_ref[...] + (rw * gt_ref[...]) * (fv * _rstd(fv) * g_ref[...])

    return _pc(body, "postnorm_fwd", (s // ts,), [_row_spec(ts, D)] * 2 + [_vec_spec(D)] * 2, _row_spec(ts, D),
               _sds((s, D)))(x, f, g, gate)


def _acc(ref, first, v):
    @pl.when(first)
    def _():
        ref[...] = v

    @pl.when(jnp.logical_not(first))
    def _():
        ref[...] += v


def _colsum(v):
    return jnp.sum(v, axis=0, keepdims=True)


def _postnorm_bwd(dout, f, g, gate, rw):
    s = dout.shape[0]
    ts = min(s, 512)

    def body(do_ref, f_ref, g_ref, gt_ref, df_ref, dgate_ref, dg_ref):
        first = pl.program_id(0) == 0
        do, fv, gv = do_ref[...], f_ref[...], g_ref[...]
        r = _rstd(fv)
        fn = fv * r
        _acc(dgate_ref, first, rw * _colsum(do * (fn * gv)))
        dy = (rw * gt_ref[...]) * do
        _acc(dg_ref, first, _colsum(dy * fn))
        dfn = dy * gv
        df_ref[...] = (r * (dfn - fn * jnp.mean(dfn * fn, axis=-1, keepdims=True))).astype(BF16)

    return _pc(body, "postnorm_bwd", (s // ts,), [_row_spec(ts, D)] * 2 + [_vec_spec(D)] * 2,
               [_row_spec(ts, D), _vec_spec(D), _vec_spec(D)],
               [_sds((s, D), BF16), _sds((1, D)), _sds((1, D))])(dout, f, g, gate)


def _prenorm_bwd(dout, dh, x, g, scale):
    s = dout.shape[0]
    ts = min(s, 512)

    def body(do_ref, dh_ref, x_ref, g_ref, sc_ref, dx_ref, dsh_ref, dsc_ref, dg_ref):
        first = pl.program_id(0) == 0
        dhv, xv, gv = dh_ref[...], x_ref[...], g_ref[...]
        r = _rstd(xv)
        xn = xv * r
        _acc(dsh_ref, first, _colsum(dhv))
        _acc(dsc_ref, first, _colsum(dhv * (xn * gv)))
        dhp = dhv * (1.0 + sc_ref[...])
        _acc(dg_ref, first, _colsum(dhp * xn))
        dxn = dhp * gv
        dx_ref[...] = do_ref[...] + r * (dxn - xn * jnp.mean(dxn * xn, axis=-1, keepdims=True))

    return _pc(body, "prenorm_bwd", (s // ts,), [_row_spec(ts, D)] * 3 + [_vec_spec(D)] * 2,
               [_row_spec(ts, D)] + [_vec_spec(D)] * 3,
               [_sds((s, D))] + [_sds((1, D))] * 3)(dout, dh, x, g, scale)


def _loss_fwd_bwd(y, tgt):
    s = y.shape[0]
    ts = min(s, 512)
    nt = s // ts

    def body(y_ref, t_ref, loss_ref, dy_ref, acc_ref):
        t = pl.program_id(0)
        e = y_ref[...] - t_ref[...]
        dy_ref[...] = e * (1.0 / D)
        _acc(acc_ref, t == 0, _colsum(e * e))

        @pl.when(t == nt - 1)
        def _():
            loss_ref[...] = jnp.full((1, LANE), 0.5 / D, F32) * jnp.sum(acc_ref[...])

    return _pc(body, "loss", (nt,), [_row_spec(ts, D)] * 2,
               [pl.BlockSpec((1, LANE), lambda t: (0, 0)), _row_spec(ts, D)],
               [_sds((1, LANE)), _sds((s, D))], scratch=[pltpu.VMEM((1, D), F32)])(y, tgt)


def _sigmoid(v):
    return 1.0 / (1.0 + jnp.exp(-v))


def _ffn_in_swiglu(h, win):
    s = h.shape[0]
    ts = _tile(s)
    nt = (((1,), (1,)), ((), ()))

    def body(h_ref, wa_ref, wb_ref, z_ref, act_ref):
        hv = h_ref[...]
        a = lax.dot_general(hv, wa_ref[...], nt, preferred_element_type=F32)
        b = lax.dot_general(hv, wb_ref[...], nt, preferred_element_type=F32)
        z_ref[0] = a
        z_ref[1] = b
        act_ref[...] = (a * _sigmoid(a) * b).astype(BF16)

    z4, act = _pc(body, "ffn_in", (4, s // ts),
                  [pl.BlockSpec((ts, D), lambda k, t: (t, 0)), pl.BlockSpec((None, FSH, D), lambda k, t: (k, 0, 0)),
                   pl.BlockSpec((None, FSH, D), lambda k, t: (k + 4, 0, 0))],
                  [pl.BlockSpec((2, None, ts, FSH), lambda k, t: (0, k, t, 0)), pl.BlockSpec((None, ts, FSH), lambda k, t: (k, t, 0))],
                  [_sds((2, 4, s, FSH)), _sds((4, s, FSH), BF16)])(h, win, win)
    return z4.reshape(N_DEV, s, FSH), act


def _ffn_out_dx_swiglu(df, wout, z, after):
    s = df.shape[0]
    ts = _tile(s)
    z4 = z.reshape(2, 4, s, FSH)
    nt = (((1,), (1,)), ((), ()))

    def body(df_ref, w_ref, z_ref, after_ref, o_ref):
        d = lax.dot_general(df_ref[...], w_ref[...], nt, preferred_element_type=F32)
        a, b = z_ref[0], z_ref[1]
        sg = _sigmoid(a)
        o_ref[0] = (d * b * (sg * (1.0 + a * (1.0 - sg)))).astype(BF16)
        o_ref[1] = (d * (a * sg)).astype(BF16)

    spec = pl.BlockSpec((2, None, ts, FSH), lambda k, t: (0, k, t, 0))
    out = _pc(body, "ffn_out_dx", (4, s // ts),
              [pl.BlockSpec((ts, D), lambda k, t: (t, 0)), pl.BlockSpec((None, FSH, D), lambda k, t: (k, 0, 0)), spec,
               pl.BlockSpec(memory_space=pl.ANY)],
              spec, _sds((2, 4, s, FSH), BF16))(df, wout, z4, after)
    return out.reshape(N_DEV, s, FSH)


def _ffn_fwd(h, win, wout_of):
    s = h.shape[0]
    ts = _tile(s)
    z, act = _ffn_in_swiglu(h, win)
    wout = wout_of(z).reshape(4, FSH, D)
    f = _mm("ffn_out", act, wout, ((1,), (0,)), (s // ts, 4),
            pl.BlockSpec((None, ts, FSH), lambda t, k: (k, t, 0)), pl.BlockSpec((None, FSH, D), lambda t, k: (k, 0, 0)),
            pl.BlockSpec((ts, D), lambda t, k: (t, 0)), _sds((s, D)), acc_axis=1)
    return f, (h, z, act)


def _ffn_bwd(df, saved, win, wout, send):
    h, z, act = saved
    s = h.shape[0]
    ts = _tile(s)
    wout = wout.reshape(4, FSH, D)
    dwout = _mm("ffn_out_dw", act, df, ((0,), (0,)), (4, 2),
                pl.BlockSpec((None, s, FSH), lambda k, j: (k, 0, 0)), pl.BlockSpec((s, D // 2), lambda k, j: (0, j)),
                pl.BlockSpec((None, FSH, D // 2), lambda k, j: (k, 0, j)), _sds((4, FSH, D), BF16))
    dz = _ffn_out_dx_swiglu(df, wout, z, send("w_out", dwout.reshape(N_DEV, D_FF // N_DEV, D)))
    dwin = _mm("ffn_in_dw", dz, h, ((0,), (0,)), (N_DEV, 2),
               pl.BlockSpec((None, s, FSH), lambda j, i: (j, 0, 0)), pl.BlockSpec((s, D // 2), lambda j, i: (0, i)),
               pl.BlockSpec((None, FSH, D // 2), lambda j, i: (j, 0, i)), _sds((N_DEV, FSH, D), BF16))
    return _mm("ffn_in_dx", dz, win, ((1,), (0,)), (s // ts, N_DEV),
               pl.BlockSpec((None, ts, FSH), lambda t, j: (j, t, 0)), pl.BlockSpec((None, FSH, D), lambda t, j: (j, 0, 0)),
               pl.BlockSpec((ts, D), lambda t, j: (t, 0)), _sds((s, D)), acc_axis=1, after=send("w_in", dwin))


def _shift_rows(v, k, row, s, back):
    if back:
        return jnp.where(row < s - k, pltpu.roll(v, s - k, 0), 0.0)
    return jnp.where(row >= k, pltpu.roll(v, k, 0), 0.0)


def _window_sum(v, w, row, s, back):
    k = 1
    while k < w:
        v = v + _shift_rows(v, k, row, s, back)
        k *= 2
    return v


def _pool_fwd(z, pool_w, pool_scale):
    s = z.shape[0]

    def body(z_ref, w_ref, sc_ref, y_ref, d_ref):
        row = lax.broadcasted_iota(jnp.int32, (s, HD), 0)
        for g, w in enumerate(POOL_WINDOWS):
            sl = slice(g * HD, (g + 1) * HD)
            a = z_ref[:, sl]
            cnt = jnp.minimum(row + 1, w).astype(F32)
            d = (_window_sum(a, w, row, s, False) / cnt - a).astype(BF16)
            d_ref[:, sl] = d
            y = jnp.dot(d, _bf(w_ref[g]), preferred_element_type=F32)
            y_ref[:, sl] = (y * sc_ref[:, sl]).astype(BF16)

    return _pc(body, "pool_fwd", (1,),
               [pl.BlockSpec((s, NH * HD), lambda i: (0, 0)), pl.BlockSpec((NH, HD, HD), lambda i: (0, 0, 0)),
                pl.BlockSpec((1, NH * HD), lambda i: (0, 0))],
               [pl.BlockSpec((s, NH * HD), lambda i: (0, 0))] * 2,
               [_sds((s, NH * HD), BF16)] * 2)(z, pool_w, pool_scale)


def _pool_bwd(dy, d, pool_w, pool_scale):
    s = dy.shape[0]

    def body(dy_ref, d_ref, w_ref, sc_ref, dz_ref, dw_ref, dsc_ref):
        row = lax.broadcasted_iota(jnp.int32, (s, HD), 0)
        for g, w in enumerate(POOL_WINDOWS):
            sl = slice(g * HD, (g + 1) * HD)
            dyg, dg, wg = dy_ref[:, sl], d_ref[:, sl], _bf(w_ref[g])
            yraw = jnp.dot(dg, wg, preferred_element_type=F32)
            dsc_ref[:, sl] = _colsum(dyg * yraw)
            dyr = _bf(dyg * sc_ref[:, sl])
            dw_ref[g] = lax.dot_general(dg, dyr, (((0,), (0,)), ((), ())), preferred_element_type=F32)
            dd = lax.dot_general(dyr, wg, (((1,), (1,)), ((), ())), preferred_element_type=F32)
            cnt = jnp.minimum(row + 1, w).astype(F32)
            dz_ref[:, sl] = (_window_sum(dd / cnt, w, row, s, True) - dd).astype(BF16)

    return _pc(body, "pool_bwd", (1,),
               [pl.BlockSpec((s, NH * HD), lambda i: (0, 0)), pl.BlockSpec((s, NH * HD), lambda i: (0, 0)),
                pl.BlockSpec((NH, HD, HD), lambda i: (0, 0, 0)), pl.BlockSpec((1, NH * HD), lambda i: (0, 0))],
               [pl.BlockSpec((s, NH * HD), lambda i: (0, 0)), pl.BlockSpec((NH, HD, HD), lambda i: (0, 0, 0)),
                pl.BlockSpec((1, NH * HD), lambda i: (0, 0))],
               [_sds((s, NH * HD), BF16), _sds((NH, HD, HD)), _sds((1, NH * HD))])(dy, d, pool_w, pool_scale)


def _gelu(v):
    return 0.5 * v * (1.0 + jnp.tanh(GELU_C * (v + 0.044715 * (v * v * v))))


def _gelu_grad(v):
    t = jnp.tanh(GELU_C * (v + 0.044715 * (v * v * v)))
    return 0.5 * (1.0 + t) + 0.5 * v * (1.0 - t * t) * (GELU_C * (1.0 + 3.0 * 0.044715 * (v * v)))


def _causal_mask():
    return lax.broadcasted_iota(jnp.int32, (HD, HD), 0) >= lax.broadcasted_iota(jnp.int32, (HD, HD), 1)


def _sgu_specs():
    w = NH * HD
    return [pl.BlockSpec((HD, w), lambda c: (c, 1)), pl.BlockSpec((HD, w), lambda c: (c, 2)),
            pl.BlockSpec((1, w), lambda c: (0, 0)), pl.BlockSpec((1, w), lambda c: (0, 0)),
            pl.BlockSpec((NH, HD, HD), lambda c: (0, 0, 0)), pl.BlockSpec((HD, LANE), lambda c: (0, 0))]


def _sgu_head(v, lng_ref, lnb_ref, w_ref, h):
    sl = slice(h * HD, (h + 1) * HD)
    vh = v[:, sl]
    xc = vh - jnp.mean(vh, axis=-1, keepdims=True)
    rs = lax.rsqrt(jnp.mean(xc * xc, axis=-1, keepdims=True) + EPS)
    vhat = xc * rs
    vn = _bf(vhat * lng_ref[:, sl] + lnb_ref[:, sl])
    wc = _bf(jnp.where(_causal_mask(), w_ref[h], 0.0))
    return sl, rs, vhat, vn, wc


def _sgu_fwd(z, ln_g, ln_b, sgu_w, sgu_bt):
    s = z.shape[0]

    def body(zu_ref, zv_ref, lng_ref, lnb_ref, w_ref, bt_ref, y_ref):
        u, v = _gelu(zu_ref[...]), _gelu(zv_ref[...])
        for h in range(NH):
            sl, _, _, vn, wc = _sgu_head(v, lng_ref, lnb_ref, w_ref, h)
            sp = jnp.dot(wc, vn, preferred_element_type=F32) + bt_ref[:, h:h + 1]
            y_ref[:, sl] = (u[:, sl] * sp).astype(BF16)

    return _pc(body, "sgu_fwd", (s // HD,), _sgu_specs(), pl.BlockSpec((HD, NH * HD), lambda c: (c, 0)),
               _sds((s, NH * HD), BF16))(z, z, ln_g, ln_b, sgu_w, sgu_bt)


def _sgu_bwd(z, dy, ln_g, ln_b, sgu_w, sgu_bt, head_sum):
    s = z.shape[0]
    w = NH * HD
    nc = s // HD

    def body(zu_ref, zv_ref, lng_ref, lnb_ref, w_ref, bt_ref, dy_ref, hs_ref,
             dzu_ref, dzv_ref, dlng_ref, dlnb_ref, dw_ref, dbt_ref, dsacc_ref):
        c = pl.program_id(0)
        first = c == 0
        zu, zv = zu_ref[...], zv_ref[...]
        u, v = _gelu(zu), _gelu(zv)
        dyv = dy_ref[...]
        gu, gv = _gelu_grad(zu), _gelu_grad(zv)
        ds = dyv * u
        _acc(dsacc_ref, first, ds)
        for h in range(NH):
            sl, rs, vhat, vn, wc = _sgu_head(v, lng_ref, lnb_ref, w_ref, h)
            sp = jnp.dot(wc, vn, preferred_element_type=F32) + bt_ref[:, h:h + 1]
            dzu_ref[:, sl] = (dyv[:, sl] * sp * gu[:, sl]).astype(BF16)
            dsh = _bf(ds[:, sl])
            dwh = lax.dot_general(dsh, vn, (((1,), (1,)), ((), ())), preferred_element_type=F32)
            dwh = jnp.where(_causal_mask(), dwh, 0.0)

            @pl.when(first)
            def _():
                dw_ref[h] = dwh

            @pl.when(jnp.logical_not(first))
            def _():
                dw_ref[h] += dwh

            dvn = lax.dot_general(wc, dsh, (((0,), (0,)), ((), ())), preferred_element_type=F32)
            g_col = _colsum(dvn * vhat)
            b_col = _colsum(dvn)

            @pl.when(first)
            def _():
                dlng_ref[:, sl] = g_col
                dlnb_ref[:, sl] = b_col

            @pl.when(jnp.logical_not(first))
            def _():
                dlng_ref[:, sl] += g_col
                dlnb_ref[:, sl] += b_col

            dvh = dvn * lng_ref[:, sl]
            dv = rs * (dvh - jnp.mean(dvh, axis=-1, keepdims=True) - vhat * jnp.mean(dvh * vhat, axis=-1, keepdims=True))
            dzv_ref[:, sl] = (dv * gv[:, sl]).astype(BF16)

        @pl.when(c == nc - 1)
        def _():
            dbt_ref[...] = jnp.dot(dsacc_ref[...], hs_ref[...], preferred_element_type=F32, precision=HIGHEST)

    outs = _pc(body, "sgu_bwd", (nc,),
               _sgu_specs() + [pl.BlockSpec((HD, w), lambda c: (c, 1)), pl.BlockSpec((w, LANE), lambda c: (0, 0))],
               [pl.BlockSpec((HD, w), lambda c: (c, 0))] * 2 + [pl.BlockSpec((1, w), lambda c: (0, 0))] * 2
               + [pl.BlockSpec((NH, HD, HD), lambda c: (0, 0, 0)), pl.BlockSpec((HD, LANE), lambda c: (0, 0))],
               [_sds((s, w), BF16)] * 2 + [_sds((1, w))] * 2 + [_sds((NH, HD, HD)), _sds((HD, LANE))],
               scratch=[pltpu.VMEM((HD, w), F32)])(z, z, ln_g, ln_b, sgu_w, sgu_bt, dy, head_sum)
    return outs


def _cmul(ar, ai, br, bi):
    return ar * br - ai * bi, ar * bi + ai * br


def _ssm_prep(lam_re, lam_im, lam_re_rep, lam_im_rep, log_dt, b_re, b_im):
    def disc(lr, li, dt):
        mag = jnp.exp(lr * dt)
        return mag * jnp.cos(li * dt), mag * jnp.sin(li * dt)

    def body(lr_ref, li_ref, lrr_ref, lir_ref, ldt_ref, br_ref, bi_ref, or_ref, oi_ref, bbr_ref, bbi_ref):
        dt = jnp.exp(ldt_ref[...])
        or_ref[...], oi_ref[...] = disc(lr_ref[...], li_ref[...], dt)
        lr, li = lrr_ref[...], lir_ref[...]
        er, ei = disc(lr, li, dt)
        den = lr * lr + li * li
        kr = ((er - 1.0) * lr + ei * li) / den
        ki = (ei * lr - (er - 1.0) * li) / den
        bbr_ref[...], bbi_ref[...] = _cmul(kr, ki, br_ref[...], bi_ref[...])

    small = pl.BlockSpec((SSM_G, SSM_P), lambda i: (0, 0))
    wide = pl.BlockSpec((SSM_G, SSM_P * SSM_N), lambda i: (0, 0))
    col = pl.BlockSpec((SSM_G, 1), lambda i: (0, 0))
    return _pc(body, "ssm_prep", (1,), [small, small, wide, wide, col, wide, wide], [small, small, wide, wide],
               [_sds((SSM_G, SSM_P))] * 2 + [_sds((SSM_G, SSM_P * SSM_N))] * 2)(
        lam_re, lam_im, lam_re_rep, lam_im_rep, log_dt, b_re, b_im)


def _ssm_param_bwd(g_lam_re, g_lam_im, g_bb_re, g_bb_im, lam_re, lam_im, lam_re_rep, lam_im_rep, log_dt, b_re, b_im, seg):
    def body(glr_ref, gli_ref, gbr_ref, gbi_ref, lr_ref, li_ref, lrr_ref, lir_ref, ldt_ref, br_ref, bi_ref, seg_ref,
             dlr_ref, dli_ref, ddt_ref, dbr_ref, dbi_ref):
        dt = jnp.exp(ldt_ref[...])
        lr, li = lrr_ref[...], lir_ref[...]
        mag = jnp.exp(lr * dt)
        er, ei = mag * jnp.cos(li * dt), mag * jnp.sin(li * dt)
        den = lr * lr + li * li
        kr = ((er - 1.0) * lr + ei * li) / den
        ki = (ei * lr - (er - 1.0) * li) / den
        gbr, gbi = gbr_ref[...], gbi_ref[...]
        dbr_ref[...], dbi_ref[...] = _cmul(kr, -ki, gbr, gbi)
        tr, ti = _cmul(br_ref[...], -bi_ref[...], gbr, gbi)
        gkr = jnp.dot(tr, seg_ref[...], preferred_element_type=F32, precision=HIGHEST)
        gki = jnp.dot(ti, seg_ref[...], preferred_element_type=F32, precision=HIGHEST)
        lr, li = lr_ref[...], li_ref[...]
        mag = jnp.exp(lr * dt)
        er, ei = mag * jnp.cos(li * dt), mag * jnp.sin(li * dt)
        den = lr * lr + li * li
        ir, ii = lr / den, -li / den
        kr, ki = _cmul(er - 1.0, ei, ir, ii)
        ar, ai = _cmul(ir, -ii, gkr, gki)
        glr, gli = glr_ref[...] + ar, gli_ref[...] + ai
        qr, qi = _cmul(kr, ki, ir, ii)
        g1r, g1i = _cmul(-qr, qi, gkr, gki)
        g2r, g2i = _cmul(dt * er, -dt * ei, glr, gli)
        dlr_ref[...] = g1r + g2r
        dli_ref[...] = g1i + g2i
        wr, wi = _cmul(lr, li, er, ei)
        g_dt = jnp.sum(wr * glr + wi * gli, axis=-1, keepdims=True)
        ddt_ref[...] = jnp.broadcast_to(dt * g_dt, (SSM_G, LANE))

    small = pl.BlockSpec((SSM_G, SSM_P), lambda i: (0, 0))
    wide = pl.BlockSpec((SSM_G, SSM_P * SSM_N), lambda i: (0, 0))
    col = pl.BlockSpec((SSM_G, 1), lambda i: (0, 0))
    segs = pl.BlockSpec((SSM_P * SSM_N, SSM_P), lambda i: (0, 0))
    return _pc(body, "ssm_param_bwd", (1,), [small, small, wide, wide, small, small, wide, wide, col, wide, wide, segs],
               [small, small, pl.BlockSpec((SSM_G, LANE), lambda i: (0, 0)), wide, wide],
               [_sds((SSM_G, SSM_P))] * 2 + [_sds((SSM_G, LANE))] + [_sds((SSM_G, SSM_P * SSM_N))] * 2)(
        g_lam_re, g_lam_im, g_bb_re, g_bb_im, lam_re, lam_im, lam_re_rep, lam_im_rep, log_dt, b_re, b_im, seg)


SCAN_LANES = 512
SCAN_ROWS = 8


SCAN_GROUPS = SCAN_LANES // SSM_P
SCAN_COLS = SCAN_GROUPS * SSM_N
SCAN_CHUNK = 256


def _ssm_scan(name, v, w_in, lam_re, lam_im, w_out, reverse, states=None):
    s = v.shape[0]
    ln, rows, ch = SCAN_LANES, SCAN_ROWS, min(SCAN_CHUNK, s)
    nch, ntile = s // ch, ch // rows
    nt_dims = (((1,), (1,)), ((), ()))
    with_sum = states is not None

    def body(*refs):
        v_ref, win_ref, lr_ref, li_ref, wout_ref = refs[:5]
        n_in = 7 if with_sum else 5
        or_ref, oi_ref, y_ref = refs[n_in:n_in + 3]
        br_s, bi_s = refs[-2:]
        l1 = (lr_ref[...], li_ref[...])
        pw = [l1]
        for _ in range(rows - 1):
            pw.append(_cmul(*pw[-1], *l1))
        row = lax.broadcasted_iota(jnp.int32, (rows, ln), 0)
        expo = (rows - row) if reverse else (row + 1)
        pr = jnp.zeros((rows, ln), F32)
        pi = jnp.zeros((rows, ln), F32)
        for e in range(1, rows + 1):
            pr = jnp.where(expo == e, pw[e - 1][0], pr)
            pi = jnp.where(expo == e, pw[e - 1][1], pi)
        lk = {}
        for k in (1, 2, 4):
            keep = (row < rows - k) if reverse else (row >= k)
            lk[k] = (jnp.where(keep, pw[k - 1][0], 0.0), jnp.where(keep, pw[k - 1][1], 0.0))

        def chunk(c, carry):
            q0 = pl.multiple_of(((nch - 1 - c) if reverse else c) * ch, ch)
            b = jnp.dot(_bf(v_ref[pl.ds(q0, ch), :]), win_ref[...], preferred_element_type=F32)
            br_s[...] = b[:, :ln]
            bi_s[...] = b[:, ln:]

            def step(i, carry):
                cr, ci = carry[:2]
                r0 = pl.multiple_of(((ntile - 1 - i) if reverse else i) * rows, rows)
                xr, xi = br_s[pl.ds(r0, rows), :], bi_s[pl.ds(r0, rows), :]
                for k in (1, 2, 4):
                    shift = rows - k if reverse else k
                    ar, ai = _cmul(lk[k][0], lk[k][1], pltpu.roll(xr, shift, 0), pltpu.roll(xi, shift, 0))
                    xr, xi = xr + ar, xi + ai
                ar, ai = _cmul(pr, pi, cr, ci)
                xr, xi = xr + ar, xi + ai
                g0 = pl.multiple_of(q0 + r0, rows)
                or_ref[pl.ds(g0, rows), :] = xr
                oi_ref[pl.ds(g0, rows), :] = xi
                if not with_sum:
                    return (xr[rows - 1:rows], xi[rows - 1:rows]) if not reverse else (xr[0:1], xi[0:1])
                nr = jnp.where(row == rows - 1, cr, pltpu.roll(xr, rows - 1, 0))
                ni = jnp.where(row == rows - 1, ci, pltpu.roll(xi, rows - 1, 0))
                sr, si = refs[5][pl.ds(g0, rows), :], refs[6][pl.ds(g0, rows), :]
                return xr[0:1], xi[0:1], carry[2] + (sr * nr + si * ni), carry[3] + (sr * ni - si * nr)

            carry = lax.fori_loop(0, ntile, step, carry)
            w = wout_ref[...]
            y_ref[pl.ds(q0, ch), :] = (
                lax.dot_general(_bf(or_ref[pl.ds(q0, ch), :]), w[:, :ln], nt_dims, preferred_element_type=F32)
                + lax.dot_general(_bf(oi_ref[pl.ds(q0, ch), :]), w[:, ln:], nt_dims, preferred_element_type=F32))
            return carry

        zero = jnp.zeros((1, ln), F32)
        init = (zero, zero) + ((jnp.zeros((rows, ln), F32),) * 2 if with_sum else ())
        carry = lax.fori_loop(0, nch, chunk, init)
        if with_sum:
            refs[n_in + 3][...] = _colsum(carry[2])
            refs[n_in + 4][...] = _colsum(carry[3])

    vec = pl.BlockSpec((1, ln), lambda j: (0, j))
    blk = pl.BlockSpec((s, ln), lambda j: (0, j))
    cols = pl.BlockSpec((s, SCAN_COLS), lambda j: (0, j))
    wspec = pl.BlockSpec((None, SCAN_COLS, 2 * ln), lambda j: (j, 0, 0))
    ins, args = [cols, wspec, vec, vec, wspec], [v, w_in, lam_re, lam_im, w_out]
    outs, shapes = [blk, blk, cols], [_sds((s, SSM_L))] * 2 + [_sds((s, SSM_G * SSM_N))]
    if with_sum:
        ins, args = ins + [blk, blk], args + list(states)
        outs, shapes = outs + [vec, vec], shapes + [_sds((1, SSM_L))] * 2
    return _pc(body, name, (SSM_L // ln,), ins, outs, shapes, scratch=[pltpu.VMEM((ch, ln), F32)] * 2)(*args)


def _ssm_outer(name, v, x_re, x_im):
    s = v.shape[0]
    ts = min(s, 512)
    nt = s // ts
    half = SSM_GB * SSM_P
    rows = SSM_GB * SSM_N
    tn = (((0,), (0,)), ((), ()))

    def body(v_ref, xr_ref, xi_ref, or_ref, oi_ref, acc_ref):
        vv = _bf(v_ref[...])
        pr = lax.dot_general(vv, _bf(xr_ref[...]), tn, preferred_element_type=F32)
        pi = lax.dot_general(vv, _bf(xi_ref[...]), tn, preferred_element_type=F32)
        t = pl.program_id(1)

        @pl.when(t == 0)
        def _():
            acc_ref[:, :half] = pr
            acc_ref[:, half:] = pi

        @pl.when(t > 0)
        def _():
            acc_ref[:, :half] += pr
            acc_ref[:, half:] += pi

        @pl.when(t == nt - 1)
        def _():
            row_g = lax.broadcasted_iota(jnp.int32, (rows, LANE), 0) // SSM_N
            lane_g = lax.broadcasted_iota(jnp.int32, (rows, LANE), 1) // SSM_P
            for part, o_ref in enumerate((or_ref, oi_ref)):
                fold = jnp.zeros((rows, LANE), F32)
                for cb in range(half // LANE):
                    blk = acc_ref[:, part * half + cb * LANE:part * half + (cb + 1) * LANE]
                    fold = fold + jnp.where(2 * cb + lane_g == row_g, blk, 0.0)
                o_ref[...] = jnp.where(row_g % 2 == 0, fold, pltpu.roll(fold, SSM_P, 1))

    xin = pl.BlockSpec((ts, half), lambda q, t: (t, q))
    out = pl.BlockSpec((None, rows, LANE), lambda q, t: (q, 0, 0))
    return _pc(body, name, (SSM_NB, nt), [pl.BlockSpec((ts, rows), lambda q, t: (t, q)), xin, xin], [out, out],
               [_sds((SSM_NB, rows, LANE))] * 2, scratch=[pltpu.VMEM((rows, 2 * half), F32)])(v, x_re, x_im)


def _ssm_act_fwd(y, u, d_skip):
    s = y.shape[0]
    ts = min(s, 512)

    def body(y_ref, u_ref, d_ref, o_ref):
        o_ref[...] = _gelu(y_ref[...] + d_ref[...] * u_ref[...]).astype(BF16)

    return _pc(body, "ssm_act_fwd", (s // ts,), [_row_spec(ts, D)] * 2 + [_vec_spec(D)], _row_spec(ts, D),
               _sds((s, D), BF16))(y, u, d_skip)


def _ssm_act_bwd(dg, y, u, d_skip):
    s = y.shape[0]
    ts = min(s, 512)

    def body(dg_ref, y_ref, u_ref, d_ref, dy_ref, dd_ref):
        uv = u_ref[...]
        dy = dg_ref[...] * _gelu_grad(y_ref[...] + d_ref[...] * uv)
        dy_ref[...] = dy.astype(BF16)
        _acc(dd_ref, pl.program_id(0) == 0, _colsum(dy * uv))

    return _pc(body, "ssm_act_bwd", (s // ts,), [_row_spec(ts, D)] * 3 + [_vec_spec(D)], [_row_spec(ts, D), _vec_spec(D)],
               [_sds((s, D), BF16), _sds((1, D))])(dg, y, u, d_skip)


def _axpy(a, b, d_skip):
    s = a.shape[0]
    ts = min(s, 512)

    def body(a_ref, b_ref, d_ref, o_ref):
        o_ref[...] = (a_ref[...] + d_ref[...] * b_ref[...].astype(F32)).astype(BF16)

    return _pc(body, "ssm_du", (s // ts,), [_row_spec(ts, D)] * 2 + [_vec_spec(D)], _row_spec(ts, D),
               _sds((s, D), BF16))(a, b, d_skip)


def _glu_fwd(zz):
    s = zz.shape[0]
    ts = min(s, 512)

    def body(a_ref, b_ref, o_ref):
        o_ref[...] = a_ref[...] * _sigmoid(b_ref[...])

    return _pc(body, "glu_fwd", (s // ts,), [_row_spec(ts, D, 0), _row_spec(ts, D, 1)], _row_spec(ts, D), _sds((s, D)))(zz, zz)


def _glu_bwd(zz, df):
    s = zz.shape[0]
    ts = min(s, 512)

    def body(a_ref, b_ref, df_ref, o_ref):
        sg = _sigmoid(b_ref[...])
        dfv = df_ref[...].astype(F32)
        o_ref[:, :D] = (dfv * sg).astype(BF16)
        o_ref[:, D:] = (dfv * a_ref[...] * sg * (1.0 - sg)).astype(BF16)

    return _pc(body, "glu_bwd", (s // ts,), [_row_spec(ts, D, 0), _row_spec(ts, D, 1), _row_spec(ts, D)],
               _row_spec(ts, 2 * D), _sds((s, 2 * D), BF16))(zz, zz, df)


def _ssm_block_diag(m_re, m_im):
    rows, half = SCAN_COLS, SCAN_LANES
    expand = jnp.tile(jnp.eye(SSM_P, dtype=BF16), (1, SCAN_GROUPS))

    def body(mr_ref, mi_ref, e_ref, o_ref):
        keep = (lax.broadcasted_iota(jnp.int32, (rows, half), 0) // SSM_N
                == lax.broadcasted_iota(jnp.int32, (rows, half), 1) // SSM_P)
        for part, m_ref in enumerate((mr_ref, mi_ref)):
            t = jnp.dot(_bf(m_ref[...]), e_ref[...], preferred_element_type=F32)
            o_ref[:, part * half:(part + 1) * half] = jnp.where(keep, t, 0.0).astype(BF16)

    blk = pl.BlockSpec((rows, SSM_P), lambda q: (q, 0))
    nb = SSM_G // SCAN_GROUPS
    return _pc(body, "ssm_block_diag", (nb,), [blk, blk, pl.BlockSpec((SSM_P, half), lambda q: (0, 0))],
               pl.BlockSpec((None, rows, 2 * half), lambda q: (q, 0, 0)), _sds((nb, rows, 2 * half), BF16))(m_re, m_im, expand)


def _mod_part(c_all, ada_w):
    n = ada_w.shape[-1]

    def body(c_ref, w_ref, o_ref):
        cv = c_ref[...]
        cond = _bf(cv * _sigmoid(cv))
        o_ref[...] = jnp.dot(cond, _bf(w_ref[...]), preferred_element_type=F32)

    return _pc(body, "mod_part", (2,), [pl.BlockSpec((N_DEV, D), lambda l: (0, 0)), pl.BlockSpec((None, D, n), lambda l: (l, 0, 0))],
               pl.BlockSpec((None, N_DEV, n), lambda l: (l, 0, 0)), _sds((2, N_DEV, n)))(c_all, ada_w)


def _ada_w_grad(c_all_t, dmod):
    n = dmod.shape[-1]
    tr = 128

    def body(c_ref, d_ref, o_ref):
        cv = c_ref[...]
        cond = _bf(cv * _sigmoid(cv)).astype(F32)
        dm = _bf(d_ref[...]).astype(F32)
        acc = cond[:, 0:1] * dm[0:1, :]
        for b in range(1, N_DEV):
            acc = acc + cond[:, b:b + 1] * dm[b:b + 1, :]
        o_ref[...] = acc

    return _pc(body, "ada_w_grad", (2, D // tr),
               [pl.BlockSpec((tr, N_DEV), lambda l, t: (t, 0)), pl.BlockSpec((None, N_DEV, n), lambda l, t: (l, 0, 0))],
               pl.BlockSpec((None, tr, n), lambda l, t: (l, t, 0)), _sds((2, D, n)))(c_all_t, dmod)


def _adamw(name, parts, w, m, v, slot=0, prev=None):
    p, r, c = parts.shape
    tr = r
    while tr * c * 4 > (1 << 20) and tr % 16 == 0:
        tr //= 2
    nt = r // tr

    def body(p_ref, w_ref, m_ref, v_ref, *rest):
        g_ref, d_ref, nm_ref, nv_ref = rest[-4:]
        g = p_ref[0].astype(F32)
        for i in range(1, p):
            g = g + p_ref[i].astype(F32)
        g_ref[...] = g
        m2 = B1 * m_ref[...] + (1.0 - B1) * g
        v2 = B2 * v_ref[...] + (1.0 - B2) * (g * g)
        nm_ref[...] = m2
        nv_ref[...] = v2
        m_hat = m2 / (1.0 - B1 ** STEP)
        v_hat = v2 / (1.0 - B2 ** STEP)
        d_ref[...] = -LR * (m_hat / (jnp.sqrt(v_hat) + ADAM_EPS) + WD * w_ref[...])

    blk = pl.BlockSpec((tr, c), lambda t: (slot * nt + t, 0))
    in_specs = [pl.BlockSpec((p, tr, c), lambda t: (0, t, 0)), blk, blk, blk]
    if prev is None:
        return _pc(body, name, (nt,), in_specs, [blk] * 4, [_sds(w.shape)] * 4)(parts, w, m, v)
    return pl.pallas_call(
        body, name=name, grid=(nt,), in_specs=in_specs + [pl.BlockSpec(memory_space=pl.ANY)] * 4, out_specs=[blk] * 4,
        out_shape=[_sds(w.shape)] * 4, input_output_aliases={4 + i: i for i in range(4)},
        compiler_params=pltpu.CompilerParams(dimension_semantics=("arbitrary",), vmem_limit_bytes=VMEM_LIMIT_BYTES))(parts, w, m, v, *prev)


def _sum_parts(parts):
    p, r, c = parts.shape
    tr = r
    while tr * c * 4 > (1 << 19) and tr % 16 == 0:
        tr //= 2

    def body(p_ref, o_ref):
        g = p_ref[0]
        for i in range(1, p):
            g = g + p_ref[i]
        o_ref[...] = g

    return _pc(body, "sum_parts", (r // tr,), [pl.BlockSpec((p, tr, c), lambda t: (0, t, 0))], pl.BlockSpec((tr, c), lambda t: (t, 0)),
               _sds((r, c)))(parts)


def _place():
    x, y, c = lax.axis_index("x"), lax.axis_index("y"), lax.axis_index("c")
    peers = []
    for k in range(1, N_DEV):
        px = (1 - x) if k & 4 else x
        py = (1 - y) if k & 2 else y
        pc = (1 - c) if k & 1 else c
        peers.append(((px, py, pc), 4 * px + 2 * py + pc))
    return 4 * x + 2 * y + c, peers


def _at(ref, idx):
    return ref if idx is None else ref.at[idx]


def _exchange_copies(plan, n, src_refs, dst_refs, send_sems, recv_sems, local_sems=None, with_arrivals=True):
    me, peers = _place()
    local = [] if local_sems is None else [
        pltpu.make_async_copy(_at(src_refs[si], sx), _at(dst_refs[di], dx), local_sems.at[i])
        for i, (si, sx, di, dx) in enumerate(plan(me, me, 0))]

    def remote(k, i, dev, entry):
        si, sx, di, dx = entry
        return pltpu.make_async_remote_copy(_at(src_refs[si], sx), _at(dst_refs[di], dx), send_sems.at[k * n + i], recv_sems.at[k * n + i],
                                            device_id=dev, device_id_type=MESH)

    sends = [remote(k, i, dev, e) for k, (dev, peer) in enumerate(peers) for i, e in enumerate(plan(me, peer, k + 1))]
    if not with_arrivals:
        return local, sends, []
    arrivals = [remote(k, i, dev, e) for k, (dev, peer) in enumerate(peers) for i, e in enumerate(plan(peer, me, k + 1))]
    return local, sends, arrivals


def _sem_shapes(n_copies, local=True):
    sems = [pltpu.SemaphoreType.DMA(((N_DEV - 1) * n_copies,)), pltpu.SemaphoreType.DMA(((N_DEV - 1) * n_copies,))]
    return sems + [pltpu.SemaphoreType.DMA((n_copies,))] if local else sems


def _exchange(name, srcs, dst_shapes, plan, n_copies):
    ns, nd = len(srcs), len(dst_shapes)

    def body(*refs):
        local, sends, arrivals = _exchange_copies(plan, n_copies, refs[:ns], refs[ns:ns + nd], *refs[ns + nd:])
        for cp in local + sends:
            cp.start()
        for cp in arrivals:
            cp.wait_recv()
        for cp in sends:
            cp.wait_send()
        for cp in local:
            cp.wait()

    any_spec = pl.BlockSpec(memory_space=pl.ANY)
    return pl.pallas_call(
        body, name=name, in_specs=[any_spec] * ns, out_specs=[any_spec] * nd, out_shape=list(dst_shapes),
        scratch_shapes=_sem_shapes(n_copies))(*srcs)


HBM_SPEC = pl.BlockSpec(memory_space=pltpu.HBM)
SEM_SPEC = pl.BlockSpec(memory_space=pltpu.SEMAPHORE)
ANY_SPEC = pl.BlockSpec(memory_space=pl.ANY)
TOKEN_SPEC = pl.BlockSpec(memory_space=pltpu.VMEM)
SIDE_EFFECT = pltpu.SideEffectType.DATAFLOW_SIDE_EFFECTING


def _wait_all(local, sends, arrivals):
    for cp in arrivals:
        cp.wait_recv()
    for cp in sends:
        cp.wait_send()
    for cp in local:
        cp.wait()


def _exchange_start(name, srcs, dst_shapes, plan, n_copies, order):
    ns, nd = len(srcs), len(dst_shapes)
    nb = ns + nd

    def body(*refs):
        local, sends, _ = _exchange_copies(plan, n_copies, refs[:ns], refs[ns:nb], *refs[nb + 1:nb + 4], with_arrivals=False)
        for cp in local + sends:
            cp.start()
        refs[-1][...] = jnp.zeros((8, LANE), F32)

    lands = [pltpu.with_memory_space_constraint(lax.empty(d.shape, d.dtype), pltpu.HBM) for d in dst_shapes]
    srcs = [pltpu.with_memory_space_constraint(a, pltpu.HBM) for a in srcs]
    bufs = srcs + lands
    out = pl.pallas_call(
        body, name=name, in_specs=[HBM_SPEC] * nb + [ANY_SPEC],
        out_specs=[SEM_SPEC] * 3 + [HBM_SPEC] * nb + [TOKEN_SPEC],
        out_shape=_sem_shapes(n_copies) + [pltpu.HBM(a.shape, a.dtype) for a in bufs] + [_sds((8, LANE))],
        input_output_aliases={i: 3 + i for i in range(nb)},
        compiler_params=pltpu.CompilerParams(has_side_effects=SIDE_EFFECT))(*bufs, order)
    return out[:3], out[3:3 + ns], out[3 + ns:3 + nb], out[-1]


def _exchange_relay(name, sems, srcs, lands, plan, n_copies, plan2, n_copies2, after):
    ns, nd = len(srcs), len(lands)
    nb = ns + nd

    def body(*refs):
        land_refs = refs[ns:nb]
        _wait_all(*_exchange_copies(plan, n_copies, refs[:ns], land_refs, *refs[nb:nb + 3]))
        _, sends, _ = _exchange_copies(plan2, n_copies2, land_refs, land_refs, *refs[nb + 4:nb + 6], with_arrivals=False)
        for cp in sends:
            cp.start()
        refs[-1][...] = jnp.zeros((8, LANE), F32)

    out = pl.pallas_call(
        body, name=name, in_specs=[HBM_SPEC] * nb + [SEM_SPEC] * 3 + [ANY_SPEC],
        out_specs=[SEM_SPEC] * 2 + [HBM_SPEC] * nd + [TOKEN_SPEC],
        out_shape=_sem_shapes(n_copies2, local=False) + [pltpu.HBM(a.shape, a.dtype) for a in lands] + [_sds((8, LANE))],
        input_output_aliases={ns + i: 2 + i for i in range(nd)},
        compiler_params=pltpu.CompilerParams(has_side_effects=SIDE_EFFECT))(*srcs, *lands, *sems, after)
    return out[:2], out[2:2 + nd], out[-1]


def _exchange_wait(name, sems, srcs, lands, plan, n_copies, after):
    srcs = [] if srcs is None else list(srcs)
    ns, nd = len(srcs), len(lands)
    nb = ns + nd

    def body(*refs):
        land_refs = refs[ns:nb]
        _wait_all(*_exchange_copies(plan, n_copies, refs[:ns] if ns else land_refs, land_refs, *refs[nb:nb + len(sems)]))

    bufs = srcs + list(lands)
    out = pl.pallas_call(
        body, name=name, in_specs=[HBM_SPEC] * nb + [SEM_SPEC] * len(sems) + [ANY_SPEC],
        out_specs=[HBM_SPEC] * nb, out_shape=[pltpu.HBM(a.shape, a.dtype) for a in bufs],
        input_output_aliases={i: i for i in range(nb)},
        compiler_params=pltpu.CompilerParams(has_side_effects=SIDE_EFFECT))(*bufs, *sems, after)
    return out[ns:]


def _all_gather(name, arrs):
    plan = lambda me, peer, k: [(i, None, i, me) for i in range(len(arrs))]
    return _exchange(name, arrs, [_sds((N_DEV,) + a.shape, a.dtype) for a in arrs], plan, len(arrs))


def _sublayer_fwd(x, fn, mod3, g_pre, g_post, rw):
    h = _prenorm_fwd(x, g_pre, mod3[1:2], mod3[0:1])
    f, saved = fn(h)
    return _postnorm_fwd(x, f, g_post, mod3[2:3], rw), (x, f, saved)


def _sublayer_bwd(dout, saved, fn_bwd, mod3, g_pre, g_post, rw):
    x, f, inner = saved
    df, dgate, dg_post = _postnorm_bwd(dout, f, g_post, mod3[2:3], rw)
    dh, extra = fn_bwd(df, inner)
    dx, dshift, dscale, dg_pre = _prenorm_bwd(dout, dh, x, g_pre, mod3[1:2])
    return dx, jnp.concatenate([dshift, dscale, dgate], axis=0), dg_pre, dg_post, extra


def _mix0_fwd(h, p):
    z = _mm_nt("mix0_in", h, p["ab_w_in"])
    y_a, d = _pool_fwd(z, p["pool_w"], p["pool_scale"])
    y_b = _sgu_fwd(z, p["sgu_ln_g"], p["sgu_ln_b"], p["sgu_w"], p["sgu_bt"])
    ycat = jnp.concatenate([y_a, y_b], axis=1)
    return _mm_nn("mix0_out", ycat, p["ab_w_out"]), (h, z, d, ycat)


def _mix0_bwd(df, saved, p):
    h, z, d, ycat = saved
    dycat = _mm_nt("mix0_out_dx", df, p["ab_w_out"])
    g = {"ab_w_out": _mm_tn("mix0_out_dw", ycat, df, BF16)}
    dz_p, g["pool_w"], g["pool_scale"] = _pool_bwd(dycat, d, p["pool_w"], p["pool_scale"])
    dz_u, dz_v, g["sgu_ln_g"], g["sgu_ln_b"], g["sgu_w"], dbt = _sgu_bwd(
        z, dycat, p["sgu_ln_g"], p["sgu_ln_b"], p["sgu_w"], p["sgu_bt"], p["head_sum"])
    g["sgu_b"] = dbt[:, :NH].T
    dz = jnp.concatenate([dz_p, dz_u, dz_v], axis=1)
    g["ab_w_in"] = _mm_tn("mix0_in_dw", dz, h, BF16)
    return _mm_nn("mix0_in_dx", dz, p["ab_w_in"]), g


def _mix1_fwd(h, p):
    u = _mm_nn("ssm_w_in", h, p["ssm_w_in"])
    x_re, x_im, y = _ssm_scan("ssm_scan_fwd", u, p["wb_bd"], p["lam_bar_re"], p["lam_bar_im"], p["wc_bd"], False)
    g = _ssm_act_fwd(y, u, p["ssm_d"])
    zz = _mm_nn("ssm_glu", g, p["ssm_w_glu"])
    return _glu_fwd(zz), (h, u, x_re, x_im, y, g, zz)


def _mix1_bwd(df, saved, p):
    h, u, x_re, x_im, y, g, zz = saved
    gr = {}
    dzz = _glu_bwd(zz, df)
    dg = _mm_nt("ssm_glu_dx", dzz, p["ssm_w_glu"])
    gr["ssm_w_glu"] = _mm_tn("ssm_glu_dw", g, dzz, BF16)
    dy, gr["ssm_d"] = _ssm_act_bwd(dg, y, u, p["ssm_d"])
    a_re, a_im, du_ssm, g_lam_re, g_lam_im = _ssm_scan(
        "ssm_scan_bwd", dy, p["wc_bd"], p["lam_bar_re"], -p["lam_bar_im"], p["wb_bd"], True, states=(x_re, x_im))
    du = _axpy(du_ssm, dy, p["ssm_d"])
    gr["ssm_w_in"] = _mm_tn("ssm_w_in_dw", h, du, BF16)
    dh = _mm_nt("ssm_w_in_dx", du, p["ssm_w_in"])
    mb_re, mb_im = _ssm_outer("ssm_db", u, a_re, a_im)
    mc_re, mc_im = _ssm_outer("ssm_dc", dy, x_re, x_im)
    per_group = lambda m: m[:, :, :SSM_P].reshape(SSM_G, SSM_N, SSM_P)
    gr["ssm_c_re"] = per_group(mc_re)
    gr["ssm_c_im"] = -per_group(mc_im)
    dlr, dli, ddt, dbr, dbi = _ssm_param_bwd(
        g_lam_re.reshape(SSM_G, SSM_P), g_lam_im.reshape(SSM_G, SSM_P),
        per_group(mb_re).reshape(SSM_G, SSM_N * SSM_P), per_group(mb_im).reshape(SSM_G, SSM_N * SSM_P),
        p["lam_re"], p["lam_im"], p["lam_re_rep"], p["lam_im_rep"], p["log_dt"], p["b_re"], p["b_im"], p["seg"])
    gr["ssm_lam_re"], gr["ssm_lam_im"], gr["ssm_log_dt"] = dlr, dli, ddt[:, 0]
    gr["ssm_b_re"] = dbr.reshape(SSM_G, SSM_N, SSM_P)
    gr["ssm_b_im"] = dbi.reshape(SSM_G, SSM_N, SSM_P)
    return dh, gr


def _ssm_params(lam_re, lam_im, b_re, b_im, c_re, c_im, log_dt):
    wide = lambda b: b.transpose(0, 2, 1).reshape(SSM_G, SSM_N * SSM_P)
    p = {"lam_re": lam_re, "lam_im": lam_im, "log_dt": log_dt.reshape(SSM_G, 1),
         "lam_re_rep": jnp.tile(lam_re, (1, SSM_N)), "lam_im_rep": jnp.tile(lam_im, (1, SSM_N)), "b_re": wide(b_re), "b_im": wide(b_im)}
    lbr, lbi, bbr, bbi = _ssm_prep(lam_re, lam_im, p["lam_re_rep"], p["lam_im_rep"], p["log_dt"], p["b_re"], p["b_im"])
    p["lam_bar_re"], p["lam_bar_im"] = lbr.reshape(1, SSM_L), lbi.reshape(1, SSM_L)
    rows = lambda m: m.reshape(SSM_G * SSM_N, SSM_P)
    p["wb_bd"] = _ssm_block_diag(rows(bbr), rows(bbi))
    p["wc_bd"] = _ssm_block_diag(rows(c_re), rows(-c_im))
    p["seg"] = jnp.tile(jnp.eye(SSM_P, dtype=F32), (SSM_N, 1))
    return p


RES_WEIGHT = (0.5, 1.0, 0.5)


def _local_step(x, tgt, mod, norm_pre, norm_post, weights_of, on_part, on_grads):
    def fns(i, w):
        if i % 3 != 1:
            win, wout_of = w
            return ((lambda h: _ffn_fwd(h, win, wout_of)),
                    (lambda df, sv: (_ffn_bwd(df, sv, win, wout_of(None), lambda tag, part: on_part(i, tag, part)), None)))
        if i == 1:
            return (lambda h: _mix0_fwd(h, w)), (lambda df, sv: _mix0_bwd(df, sv, w))
        return (lambda h: _mix1_fwd(h, w)), (lambda df, sv: _mix1_bwd(df, sv, w))

    saved, bwd = [], []
    for i in range(6):
        l, s = divmod(i, 3)
        w, token = weights_of(i, x)
        f, b = fns(i, w)
        x, sv = _sublayer_fwd(x, f, mod[l, s] + token[0:1, 0:1], norm_pre[l, s][None], norm_post[l, s][None], RES_WEIGHT[s])
        saved.append(sv)
        bwd.append(b)
    loss_row, dx = _loss_fwd_bwd(x, tgt)
    token = jnp.zeros((8, LANE), F32)
    for i in reversed(range(6)):
        l, s = divmod(i, 3)
        mod3 = mod[l, s] + token[0:1, 0:1]
        dx, dmod, dpre, dpost, extra = _sublayer_bwd(
            dx, saved[i], bwd[i], mod3, norm_pre[l, s][None], norm_post[l, s][None], RES_WEIGHT[s])
        token = on_grads(i, extra, dmod, dpre, dpost, loss_row)
    return dx


def _pad_rows(v, rows):
    return jnp.pad(v, (0, rows * LANE - v.shape[0])).reshape(rows, LANE)


def _pack(parts):
    flat, layout, off = [], [], 0
    for a in parts:
        n = a.size
        padded = -(-n // LANE) * LANE
        flat.append(jnp.pad(a.reshape(-1).astype(F32), (0, padded - n)))
        layout.append((off, n, a.shape))
        off += padded
    return jnp.concatenate(flat), layout


def _unpack(flat, layout):
    return [flat[off:off + n].reshape(shape) for off, n, shape in layout]


SMALL_REPLICATED = ["ada_b", "pool_w", "pool_scale", "sgu_ln_g", "sgu_ln_b", "sgu_w", "sgu_b", "ssm_lam_re", "ssm_lam_im",
                    "ssm_b_re", "ssm_b_im", "ssm_c_re", "ssm_c_im", "ssm_log_dt"]
SMALL_SHARDED = ["norm_pre", "norm_post", "ssm_d"]
TRANSPOSED = ["ffn_w_in", "ab_w_in", "ssm_b_re", "ssm_b_im"]
WEIGHTS = ['ada_w', 'ada_b', 'norm_pre', 'norm_post', 'ffn_w_in', 'ffn_w_out', 'ab_w_in', 'pool_w', 'pool_scale', 'sgu_ln_g',
           'sgu_ln_b', 'sgu_w', 'sgu_b', 'ab_w_out', 'ssm_w_in', 'ssm_lam_re', 'ssm_lam_im', 'ssm_b_re', 'ssm_b_im', 'ssm_c_re',
           'ssm_c_im', 'ssm_d', 'ssm_log_dt', 'ssm_w_glu']


def kernel(x, c, ada_w, ada_b, norm_pre, norm_post, ffn_w_in, ffn_w_out, ab_w_in, pool_w, pool_scale, sgu_ln_g, sgu_ln_b, sgu_w, sgu_b, ab_w_out, ssm_w_in, ssm_lam_re, ssm_lam_im, ssm_b_re, ssm_b_im, ssm_c_re, ssm_c_im, ssm_d, ssm_log_dt, ssm_w_glu, loss_target, m_ada_w, m_ada_b, m_norm_pre, m_norm_post, m_ffn_w_in, m_ffn_w_out, m_ab_w_in, m_pool_w, m_pool_scale, m_sgu_ln_g, m_sgu_ln_b, m_sgu_w, m_sgu_b, m_ab_w_out, m_ssm_w_in, m_ssm_lam_re, m_ssm_lam_im, m_ssm_b_re, m_ssm_b_im, m_ssm_c_re, m_ssm_c_im, m_ssm_d, m_ssm_log_dt, m_ssm_w_glu, v_ada_w, v_ada_b, v_norm_pre, v_norm_post, v_ffn_w_in, v_ffn_w_out, v_ab_w_in, v_pool_w, v_pool_scale, v_sgu_ln_g, v_sgu_ln_b, v_sgu_w, v_sgu_b, v_ab_w_out, v_ssm_w_in, v_ssm_lam_re, v_ssm_lam_im, v_ssm_b_re, v_ssm_b_im, v_ssm_c_re, v_ssm_c_im, v_ssm_d, v_ssm_log_dt, v_ssm_w_glu):
    args = locals()
    wts = {n: args[n] for n in WEIGHTS}
    mom = {n: args["m_" + n] for n in WEIGHTS}
    var = {n: args["v_" + n] for n in WEIGHTS}
    for n in TRANSPOSED:
        for t in (wts, mom, var):
            t[n] = jnp.swapaxes(t[n], -1, -2)
    me = 4 * lax.axis_index("x") + 2 * lax.axis_index("y") + lax.axis_index("c")
    s = x.shape[1]
    nd = D // N_DEV

    small_in, small_in_layout = _pack([c, norm_pre, norm_post, ssm_d])
    small_rows = -(-small_in.shape[0] // (8 * LANE)) * 8
    (g_small,) = _all_gather("gather_small", [_pad_rows(small_in, small_rows)])
    g_small = g_small.reshape(N_DEV, -1)
    c_all, npre_g, npost_g, sd_g = [jnp.stack([_unpack(g_small[j], small_in_layout)[i] for j in range(N_DEV)]) for i in range(4)]
    c_all = c_all.reshape(N_DEV, D)
    norm_pre_full = npre_g.transpose(1, 2, 0, 3).reshape(2, 3, D)
    norm_post_full = npost_g.transpose(1, 2, 0, 3).reshape(2, 3, D)
    ssm_d_full = sd_g.transpose(1, 0, 2).reshape(1, D)

    nw = ada_w.shape[-1]
    (mod_g,) = _all_gather("gather_mod", [_mod_part(c_all, ada_w)])
    mod = lax.dynamic_index_in_dim(mod_g, me, axis=2, keepdims=False)
    mod = (mod.transpose(1, 0, 2).reshape(2, N_DEV * nw) + ada_b).reshape(2, 3, 3, D)

    w_in_t = wts["ffn_w_in"]
    shards = [[w_in_t[0, 0]], [ffn_w_out[0, 0]], [wts["ab_w_in"][0], ab_w_out[0]], [w_in_t[0, 1], ffn_w_out[0, 1]],
              [w_in_t[1, 0], ffn_w_out[1, 0]], [ssm_w_in[0], ssm_w_glu[0]], [w_in_t[1, 1], ffn_w_out[1, 1]]]
    same_core = (2, 4, 6)

    def gather_plan(n):
        return lambda me_, peer_, k: [(a, None, a, me_) for a in range(n)] if k in (0, 1) + same_core else []

    def relay_plan(n):
        return lambda me_, peer_, k: [(a, me_ ^ kk, a, me_ ^ kk) for kk in same_core for a in range(n)] if k == 1 else []

    gathers, relays = [], {}
    token = mod_g
    for g, group in enumerate(shards):
        group = [a.astype(BF16) for a in group]
        sems, srcs_thru, lands, token = _exchange_start(
            f"gather_start_{g}", group, [_sds((N_DEV,) + a.shape, BF16) for a in group], gather_plan(len(group)), len(group), token)
        gathers.append((sems, srcs_thru, lands))
    mod = mod + token[0, 0]

    def relay(g, after):
        sems, srcs_thru, lands = gathers[g]
        n = len(lands)
        relays[g] = _exchange_relay(f"gather_relay_{g}", sems, srcs_thru, lands, gather_plan(n), n, relay_plan(n), 3 * n, after)

    def fetch(g, after):
        if g not in relays:
            relay(g, after)
        sems, lands, token = relays[g]
        n = len(lands)
        got = _exchange_wait(f"gather_wait_{g}", sems, None, lands, relay_plan(n), 3 * n, after)
        if g + 1 < len(gathers):
            relay(g + 1, got[0])
            token = relays[g + 1][2]
        return got, token

    head_sum = jnp.repeat(jnp.eye(NH, LANE, dtype=F32), HD, axis=0)
    mix0 = {"pool_w": pool_w[0], "pool_scale": pool_scale, "sgu_ln_g": sgu_ln_g, "sgu_ln_b": sgu_ln_b, "sgu_w": sgu_w[0],
            "sgu_bt": jnp.pad(sgu_b[0].T, ((0, 0), (0, LANE - NH))), "head_sum": head_sum}
    mix1 = _ssm_params(ssm_lam_re[0], ssm_lam_im[0], ssm_b_re[0], ssm_b_im[0], ssm_c_re[0], ssm_c_im[0], ssm_log_dt[0])
    mix1["ssm_d"] = ssm_d_full

    def weights_of(i, x_in):
        if i == 0:
            (win,), token = fetch(0, x_in)
            cache = []

            def wout_of(z):
                if not cache:
                    cache.append(fetch(1, z)[0][0])
                return cache[0]

            return (win, wout_of), token
        (a, b), token = fetch(i + 1, x_in)
        if i % 3 != 1:
            return (a, lambda z: b), token
        if i == 1:
            return dict(mix0, ab_w_in=a.reshape(-1, D), ab_w_out=b.reshape(D, D)), token
        return dict(mix1, ssm_w_in=a.reshape(D, D), ssm_w_glu=b.transpose(1, 0, 2).reshape(D, -1)), token

    def shard_cols(a):
        r = a.shape[0]
        return a.reshape(r, N_DEV, -1).transpose(1, 0, 2)

    scatter_plan = lambda me_, peer_, k: [(0, peer_, 0, me_), (1, peer_, 1, me_)]
    scatter_plan1 = lambda me_, peer_, k: [(0, peer_, 0, me_)]
    scatters = []
    last_token = [jnp.zeros((8, LANE), F32)]
    pieces, mixer, bundles = {}, {}, {}
    bundle_plan = lambda me_, peer_, k: [(0, None, 0, me_)]

    def on_part(i, tag, part):
        sems, srcs_thru, lands, last_token[0] = _exchange_start(
            f"scatter_start_{i}_{tag}", [part], [_sds(part.shape, BF16)], scatter_plan1, 1, last_token[0])
        scatters.append((i, ("ffn_" + tag,), scatter_plan1, sems, srcs_thru, lands))
        return last_token[0]
    mix0_names = ["pool_w", "pool_scale", "sgu_ln_g", "sgu_ln_b", "sgu_w", "sgu_b"]
    mix1_names = ["ssm_lam_re", "ssm_lam_im", "ssm_b_re", "ssm_b_im", "ssm_c_re", "ssm_c_im", "ssm_log_dt", "ssm_d"]

    def start_bundle(tag, arrays):
        flat, layout = _pack(arrays)
        rows = -(-flat.shape[0] // (8 * LANE)) * 8
        plan = gather_plan(1) if tag == "a" else bundle_plan
        sems, srcs_thru, lands, last_token[0] = _exchange_start(
            f"small_start_{tag}", [_pad_rows(flat, rows)], [_sds((N_DEV, rows, LANE))], plan, 1, last_token[0])
        bundles[tag] = (sems, srcs_thru, lands, layout)

    def on_grads(i, extra, dmod_i, dpre_i, dpost_i, loss_row):
        pieces[i] = (dmod_i, dpre_i, dpost_i)
        if i == 4:
            mixer.update({n: extra[n] for n in mix1_names})
        if i == 1:
            mixer.update({n: extra[n] for n in mix0_names})
            rest = range(1, 6)
            start_bundle("a", [jnp.stack([pieces[j][0] for j in rest])] + [jnp.concatenate([pieces[j][k] for j in rest]) for k in (1, 2)]
                         + [mixer[n] for n in mix0_names + mix1_names])
        if i == 0:
            start_bundle("b", [dmod_i, dpre_i, dpost_i, loss_row])
        if i % 3 != 1:
            return last_token[0]
        if i == 1:
            names, parts = ("ab_w_in", "ab_w_out"), [extra["ab_w_in"].reshape(N_DEV, -1, D), extra["ab_w_out"].reshape(N_DEV, nd, D)]
        else:
            names, parts = ("ssm_w_in", "ssm_w_glu"), [extra["ssm_w_in"].reshape(N_DEV, nd, D), shard_cols(extra["ssm_w_glu"])]
        sems, srcs_thru, lands, last_token[0] = _exchange_start(
            f"scatter_start_{i}", parts, [_sds(a.shape, BF16) for a in parts], scatter_plan, 2, last_token[0])
        scatters.append((i, names, scatter_plan, sems, srcs_thru, lands))
        return last_token[0]

    grad_x = _local_step(x[0], loss_target[0], mod, norm_pre_full, norm_post_full, weights_of, on_part, on_grads)

    out_g, out_d, out_m, out_v = {}, {}, {}, {}
    big_out = {}

    def adam_big(name, recv, n, slot=0):
        c_ = wts[n].shape[-1]
        big_out[n] = _adamw(name, recv.reshape(recv.shape[0], -1, c_), *[t[n].reshape(-1, c_) for t in (wts, mom, var)],
                            slot=slot, prev=big_out.get(n))
        return big_out[n][0]

    ffn_slot = {0: 0, 2: 1, 3: 2, 5: 3}

    def land_and_update(entries, after):
        for i, names, plan, sems, srcs_thru, lands in entries:
            recv = _exchange_wait(f"scatter_wait_{i}_{names[0]}", sems, srcs_thru, lands, plan, len(names), after)
            for n, r in zip(names, recv):
                after = adam_big(f"adamw_{n}_{i}", r, n, ffn_slot.get(i, 0))
        return after

    after = land_and_update([e for e in scatters if e[0] != 0], last_token[0])

    def landed(tag, g_parts):
        layout = bundles[tag][3]
        off, n, shape = layout[0]
        dmods = g_parts.reshape(N_DEV, -1)[:, off:off + n].reshape((N_DEV,) + shape)
        total = _sum_parts(g_parts)
        return dmods, _unpack(total.reshape(-1), layout), total

    def adam_small(n, g):
        cols = wts[n].shape[-1]
        res = _adamw(f"adamw_{n}", g.reshape(1, -1, cols), *[t[n].reshape(-1, cols) for t in (wts, mom, var)])
        for o, arr in zip((out_g, out_d, out_m, out_v), res):
            o[n] = arr.reshape(wts[n].shape)
            if n in TRANSPOSED:
                o[n] = jnp.swapaxes(o[n], -1, -2)
        return res[0]

    sems, srcs_thru, lands, _ = bundles["a"]
    sems, lands, _ = _exchange_relay("small_relay_a", sems, srcs_thru, lands, gather_plan(1), 1, relay_plan(1), 3, after)
    (parts_a,) = _exchange_wait("small_wait_a", sems, None, lands, relay_plan(1), 3, after)
    dmods_a, sums_a, after = landed("a", parts_a)
    dmod_a, dpre_a, dpost_a = sums_a[:3]
    small = dict(zip(mix0_names + mix1_names, sums_a[3:]))
    for n in mix0_names + mix1_names:
        g = small[n] if n not in SMALL_SHARDED else lax.dynamic_slice_in_dim(small[n], me * nd, nd, axis=small[n].ndim - 1)
        after = adam_small(n, g)
    sems, srcs_thru, lands, _ = bundles["b"]
    (parts_b,) = _exchange_wait("small_wait_b", sems, srcs_thru, lands, bundle_plan, 1, after)
    dmods_b, (dmod_b, dpre_b, dpost_b, loss_sum), _ = landed("b", parts_b)
    gathered = {"a": dmods_a, "b": dmods_b}
    loss = loss_sum[0, 0]
    adam_small("ada_b", jnp.concatenate([dmod_b[None], dmod_a]))
    for n, first, rest in (("norm_pre", dpre_b, dpre_a), ("norm_post", dpost_b, dpost_a)):
        after = adam_small(n, lax.dynamic_slice_in_dim(jnp.concatenate([first, rest]), me * nd, nd, axis=1))

    dmod_all = jnp.concatenate([gathered["b"][:, None], gathered["a"]], axis=1).reshape(N_DEV, 2, N_DEV, nw)
    dmod_mine = lax.dynamic_index_in_dim(dmod_all, me, axis=2, keepdims=False).transpose(1, 0, 2)
    g_ada_w = _ada_w_grad(c_all.T, dmod_mine)
    after = after[0:1, 0:1] + adam_big("adamw_ada_w", g_ada_w[None], "ada_w")[0:1, 0:1]

    land_and_update([e for e in scatters if e[0] == 0], after)
    for n, res in big_out.items():
        for o, arr in zip((out_g, out_d, out_m, out_v), res):
            o[n] = arr.reshape(wts[n].shape)
            if n in TRANSPOSED:
                o[n] = jnp.swapaxes(o[n], -1, -2)

    return (loss, grad_x[None], *[out_g[n] for n in WEIGHTS], *[out_d[n] for n in WEIGHTS],
            *[out_m[n] for n in WEIGHTS], *[out_v[n] for n in WEIGHTS])
```

```python
import functools
import math

import jax
import jax.numpy as jnp
from jax import lax
from jax.experimental import pallas as pl
from jax.experimental.pallas import tpu as pltpu

F32 = jnp.float32
BF16 = jnp.bfloat16
MESH = pl.DeviceIdType.MESH
HIGHEST = lax.Precision.HIGHEST

N_DEV = 8
D = 1024
D_FF = 2816
FSH = 2 * D_FF // N_DEV
EPS = 1e-6
POOL_WINDOWS = (2, 4, 8, 16)
HD = 128
NH = 4
SSM_G, SSM_P, SSM_N = 64, 64, 16
SSM_GB = 16
SSM_NB = SSM_G // SSM_GB
SSM_L = SSM_G * SSM_P
LR, B1, B2, ADAM_EPS, WD, STEP = 0.001, 0.9, 0.999, 1e-08, 0.01, 10
GELU_C = math.sqrt(2.0 / math.pi)
VMEM_LIMIT_BYTES = 48 * 1024 * 1024
LANE = 128


def _pc(body, name, grid, in_specs, out_specs, out_shape, scratch=()):
    return pl.pallas_call(
        body, name=name, grid=grid, in_specs=in_specs, out_specs=out_specs, out_shape=out_shape,
        scratch_shapes=list(scratch),
        compiler_params=pltpu.CompilerParams(dimension_semantics=("arbitrary",) * len(grid),
                                             vmem_limit_bytes=VMEM_LIMIT_BYTES))


def _sds(shape, dtype=F32):
    return jax.ShapeDtypeStruct(tuple(shape), dtype)


def _bf(v):
    return v if v.dtype == BF16 else v.astype(BF16)


def _row_spec(ts, width, col=0):
    return pl.BlockSpec((ts, width), lambda t, _c=col: (t, _c))


def _vec_spec(width, col=0):
    return pl.BlockSpec((1, width), lambda t, _c=col: (0, _c))


def _mm(name, a, b, contract, grid, a_spec, b_spec, o_spec, out_shape, acc_axis=None, after=None):
    dn = (contract, ((), ()))

    def body(a_ref, b_ref, *rest):
        o_ref = rest[-1]
        r = lax.dot_general(_bf(a_ref[...]), _bf(b_ref[...]), dn, preferred_element_type=F32)
        if acc_axis is None:
            o_ref[...] = r.astype(o_ref.dtype)
        else:
            k = pl.program_id(acc_axis)

            @pl.when(k == 0)
            def _():
                o_ref[...] = r

            @pl.when(k > 0)
            def _():
                o_ref[...] += r

    if after is None:
        return _pc(body, name, grid, [a_spec, b_spec], o_spec, out_shape)(a, b)
    return _pc(body, name, grid, [a_spec, b_spec, pl.BlockSpec(memory_space=pl.ANY)], o_spec, out_shape)(a, b, after)


def _tile(s):
    return min(s, 1024)


def _div_tile(n, cap=1024):
    t = min(n, cap) // LANE * LANE
    while n % t:
        t -= LANE
    return t


def _mm_nn(name, a, b, out_dtype=F32):
    s, k = a.shape
    n = b.shape[1]
    ts, tn = _tile(s), _div_tile(n)
    return _mm(name, a, b, ((1,), (0,)), (n // tn, s // ts),
               pl.BlockSpec((ts, k), lambda j, t: (t, 0)), pl.BlockSpec((k, tn), lambda j, t: (0, j)),
               pl.BlockSpec((ts, tn), lambda j, t: (t, j)), _sds((s, n), out_dtype))


def _mm_nt(name, a, b, out_dtype=F32, after=None):
    s, n = a.shape
    k = b.shape[0]
    ts, tk = _tile(s), _div_tile(k)
    return _mm(name, a, b, ((1,), (1,)), (k // tk, s // ts),
               pl.BlockSpec((ts, n), lambda j, t: (t, 0)), pl.BlockSpec((tk, n), lambda j, t: (j, 0)),
               pl.BlockSpec((ts, tk), lambda j, t: (t, j)), _sds((s, k), out_dtype), after=after)


def _mm_tn(name, a, b, out_dtype=F32, tm=512, tn=512):
    s, m = a.shape
    n = b.shape[1]
    tm, tn = min(m, tm), min(n, tn)
    return _mm(name, a, b, ((0,), (0,)), (m // tm, n // tn),
               pl.BlockSpec((s, tm), lambda i, j: (0, i)), pl.BlockSpec((s, tn), lambda i, j: (0, j)),
               pl.BlockSpec((tm, tn), lambda i, j: (i, j)), _sds((m, n), out_dtype))


def _rstd(v):
    return lax.rsqrt(jnp.mean(v * v, axis=-1, keepdims=True) + EPS)


def _prenorm_fwd(x, g, scale, shift):
    s = x.shape[0]
    ts = min(s, 512)

    def body(x_ref, g_ref, sc_ref, sh_ref, h_ref):
        xv = x_ref[...]
        h_ref[...] = ((xv * _rstd(xv) * g_ref[...]) * (1.0 + sc_ref[...]) + sh_ref[...]).astype(BF16)

    return _pc(body, "prenorm_fwd", (s // ts,), [_row_spec(ts, D)] + [_vec_spec(D)] * 3, _row_spec(ts, D),
               _sds((s, D), BF16))(x, g, scale, shift)


def _postnorm_fwd(x, f, g, gate, rw):
    s = x.shape[0]
    ts = min(s, 512)

    def body(x_ref, f_ref, g_ref, gt_ref, o_ref):
        fv = f_ref[...]
        o_ref[...] = x_ref[...] + (rw * gt_ref[...]) * (fv * _rstd(fv) * g_ref[...])

    return _pc(body, "postnorm_fwd", (s // ts,), [_row_spec(ts, D)] * 2 + [_vec_spec(D)] * 2, _row_spec(ts, D),
               _sds((s, D)))(x, f, g, gate)


def _acc(ref, first, v):
    @pl.when(first)
    def _():
        ref[...] = v

    @pl.when(jnp.logical_not(first))
    def _():
        ref[...] += v


def _colsum(v):
    return jnp.sum(v, axis=0, keepdims=True)


def _postnorm_bwd(dout, f, g, gate, rw):
    s = dout.shape[0]
    ts = min(s, 512)

    def body(do_ref, f_ref, g_ref, gt_ref, df_ref, dgate_ref, dg_ref):
        first = pl.program_id(0) == 0
        do, fv, gv = do_ref[...], f_ref[...], g_ref[...]
        r = _rstd(fv)
        fn = fv * r
        _acc(dgate_ref, first, rw * _colsum(do * (fn * gv)))
        dy = (rw * gt_ref[...]) * do
        _acc(dg_ref, first, _colsum(dy * fn))
        dfn = dy * gv
        df_ref[...] = (r * (dfn - fn * jnp.mean(dfn * fn, axis=-1, keepdims=True))).astype(BF16)

    return _pc(body, "postnorm_bwd", (s // ts,), [_row_spec(ts, D)] * 2 + [_vec_spec(D)] * 2,
               [_row_spec(ts, D), _vec_spec(D), _vec_spec(D)],
               [_sds((s, D), BF16), _sds((1, D)), _sds((1, D))])(dout, f, g, gate)


def _prenorm_bwd(dout, dh, x, g, scale):
    s = dout.shape[0]
    ts = min(s, 512)

    def body(do_ref, dh_ref, x_ref, g_ref, sc_ref, dx_ref, dsh_ref, dsc_ref, dg_ref):
        first = pl.program_id(0) == 0
        dhv, xv, gv = dh_ref[...], x_ref[...], g_ref[...]
        r = _rstd(xv)
        xn = xv * r
        _acc(dsh_ref, first, _colsum(dhv))
        _acc(dsc_ref, first, _colsum(dhv * (xn * gv)))
        dhp = dhv * (1.0 + sc_ref[...])
        _acc(dg_ref, first, _colsum(dhp * xn))
        dxn = dhp * gv
        dx_ref[...] = do_ref[...] + r * (dxn - xn * jnp.mean(dxn * xn, axis=-1, keepdims=True))

    return _pc(body, "prenorm_bwd", (s // ts,), [_row_spec(ts, D)] * 3 + [_vec_spec(D)] * 2,
               [_row_spec(ts, D)] + [_vec_spec(D)] * 3,
               [_sds((s, D))] + [_sds((1, D))] * 3)(dout, dh, x, g, scale)


def _loss_fwd_bwd(y, tgt):
    s = y.shape[0]
    ts = min(s, 512)
    nt = s // ts

    def body(y_ref, t_ref, loss_ref, dy_ref, acc_ref):
        t = pl.program_id(0)
        e = y_ref[...] - t_ref[...]
        dy_ref[...] = e * (1.0 / D)
        _acc(acc_ref, t == 0, _colsum(e * e))

        @pl.when(t == nt - 1)
        def _():
            loss_ref[...] = jnp.full((1, LANE), 0.5 / D, F32) * jnp.sum(acc_ref[...])

    return _pc(body, "loss", (nt,), [_row_spec(ts, D)] * 2,
               [pl.BlockSpec((1, LANE), lambda t: (0, 0)), _row_spec(ts, D)],
               [_sds((1, LANE)), _sds((s, D))], scratch=[pltpu.VMEM((1, D), F32)])(y, tgt)


def _sigmoid(v):
    return 1.0 / (1.0 + jnp.exp(-v))


def _ffn_in_swiglu(h, win):
    s = h.shape[0]
    ts = _tile(s)
    nt = (((1,), (1,)), ((), ()))

    def body(h_ref, wa_ref, wb_ref, z_ref, act_ref):
        hv = h_ref[...]
        a = lax.dot_general(hv, wa_ref[...], nt, preferred_element_type=F32)
        b = lax.dot_general(hv, wb_ref[...], nt, preferred_element_type=F32)
        z_ref[0] = a
        z_ref[1] = b
        act_ref[...] = (a * _sigmoid(a) * b).astype(BF16)

    z4, act = _pc(body, "ffn_in", (4, s // ts),
                  [pl.BlockSpec((ts, D), lambda k, t: (t, 0)), pl.BlockSpec((None, FSH, D), lambda k, t: (k, 0, 0)),
                   pl.BlockSpec((None, FSH, D), lambda k, t: (k + 4, 0, 0))],
                  [pl.BlockSpec((2, None, ts, FSH), lambda k, t: (0, k, t, 0)), pl.BlockSpec((None, ts, FSH), lambda k, t: (k, t, 0))],
                  [_sds((2, 4, s, FSH)), _sds((4, s, FSH), BF16)])(h, win, win)
    return z4.reshape(N_DEV, s, FSH), act


def _ffn_out_dx_swiglu(df, wout, z, after):
    s = df.shape[0]
    ts = _tile(s)
    z4 = z.reshape(2, 4, s, FSH)
    nt = (((1,), (1,)), ((), ()))

    def body(df_ref, w_ref, z_ref, after_ref, o_ref):
        d = lax.dot_general(df_ref[...], w_ref[...], nt, preferred_element_type=F32)
        a, b = z_ref[0], z_ref[1]
        sg = _sigmoid(a)
        o_ref[0] = (d * b * (sg * (1.0 + a * (1.0 - sg)))).astype(BF16)
        o_ref[1] = (d * (a * sg)).astype(BF16)

    spec = pl.BlockSpec((2, None, ts, FSH), lambda k, t: (0, k, t, 0))
    out = _pc(body, "ffn_out_dx", (4, s // ts),
              [pl.BlockSpec((ts, D), lambda k, t: (t, 0)), pl.BlockSpec((None, FSH, D), lambda k, t: (k, 0, 0)), spec,
               pl.BlockSpec(memory_space=pl.ANY)],
              spec, _sds((2, 4, s, FSH), BF16))(df, wout, z4, after)
    return out.reshape(N_DEV, s, FSH)


def _ffn_fwd(h, win, wout_of):
    s = h.shape[0]
    ts = _tile(s)
    z, act = _ffn_in_swiglu(h, win)
    wout = wout_of(z).reshape(4, FSH, D)
    f = _mm("ffn_out", act, wout, ((1,), (0,)), (s // ts, 4),
            pl.BlockSpec((None, ts, FSH), lambda t, k: (k, t, 0)), pl.BlockSpec((None, FSH, D), lambda t, k: (k, 0, 0)),
            pl.BlockSpec((ts, D), lambda t, k: (t, 0)), _sds((s, D)), acc_axis=1)
    return f, (h, z, act)


def _ffn_bwd(df, saved, win, wout, send, after):
    h, z, act = saved
    s = h.shape[0]
    ts = _tile(s)
    wout = wout.reshape(4, FSH, D)
    dwout = _mm("ffn_out_dw", act, df, ((0,), (0,)), (4, 2),
                pl.BlockSpec((None, s, FSH), lambda k, j: (k, 0, 0)), pl.BlockSpec((s, D // 2), lambda k, j: (0, j)),
                pl.BlockSpec((None, FSH, D // 2), lambda k, j: (k, 0, j)), _sds((4, FSH, D), BF16), after=after)
    dz = _ffn_out_dx_swiglu(df, wout, z, send("w_out", dwout.reshape(N_DEV, D_FF // N_DEV, D)))
    dwin = _mm("ffn_in_dw", dz, h, ((0,), (0,)), (N_DEV, 2),
               pl.BlockSpec((None, s, FSH), lambda j, i: (j, 0, 0)), pl.BlockSpec((s, D // 2), lambda j, i: (0, i)),
               pl.BlockSpec((None, FSH, D // 2), lambda j, i: (j, 0, i)), _sds((N_DEV, FSH, D), BF16))
    return _mm("ffn_in_dx", dz, win, ((1,), (0,)), (s // ts, N_DEV),
               pl.BlockSpec((None, ts, FSH), lambda t, j: (j, t, 0)), pl.BlockSpec((None, FSH, D), lambda t, j: (j, 0, 0)),
               pl.BlockSpec((ts, D), lambda t, j: (t, 0)), _sds((s, D)), acc_axis=1, after=send("w_in", dwin))


def _shift_rows(v, k, row, s, back):
    if back:
        return jnp.where(row < s - k, pltpu.roll(v, s - k, 0), 0.0)
    return jnp.where(row >= k, pltpu.roll(v, k, 0), 0.0)


def _window_sum(v, w, row, s, back):
    k = 1
    while k < w:
        v = v + _shift_rows(v, k, row, s, back)
        k *= 2
    return v


def _pool_fwd(z, pool_w, pool_scale):
    s = z.shape[0]

    def body(z_ref, w_ref, sc_ref, y_ref, d_ref):
        row = lax.broadcasted_iota(jnp.int32, (s, HD), 0)
        for g, w in enumerate(POOL_WINDOWS):
            sl = slice(g * HD, (g + 1) * HD)
            a = z_ref[:, sl]
            cnt = jnp.minimum(row + 1, w).astype(F32)
            d = (_window_sum(a, w, row, s, False) / cnt - a).astype(BF16)
            d_ref[:, sl] = d
            y = jnp.dot(d, _bf(w_ref[g]), preferred_element_type=F32)
            y_ref[:, sl] = (y * sc_ref[:, sl]).astype(BF16)

    return _pc(body, "pool_fwd", (1,),
               [pl.BlockSpec((s, NH * HD), lambda i: (0, 0)), pl.BlockSpec((NH, HD, HD), lambda i: (0, 0, 0)),
                pl.BlockSpec((1, NH * HD), lambda i: (0, 0))],
               [pl.BlockSpec((s, NH * HD), lambda i: (0, 0))] * 2,
               [_sds((s, NH * HD), BF16)] * 2)(z, pool_w, pool_scale)


def _pool_bwd(dy, d, pool_w, pool_scale):
    s = dy.shape[0]

    def body(dy_ref, d_ref, w_ref, sc_ref, dz_ref, dw_ref, dsc_ref):
        row = lax.broadcasted_iota(jnp.int32, (s, HD), 0)
        for g, w in enumerate(POOL_WINDOWS):
            sl = slice(g * HD, (g + 1) * HD)
            dyg, dg, wg = dy_ref[:, sl], d_ref[:, sl], _bf(w_ref[g])
            yraw = jnp.dot(dg, wg, preferred_element_type=F32)
            dsc_ref[:, sl] = _colsum(dyg * yraw)
            dyr = _bf(dyg * sc_ref[:, sl])
            dw_ref[g] = lax.dot_general(dg, dyr, (((0,), (0,)), ((), ())), preferred_element_type=F32)
            dd = lax.dot_general(dyr, wg, (((1,), (1,)), ((), ())), preferred_element_type=F32)
            cnt = jnp.minimum(row + 1, w).astype(F32)
            dz_ref[:, sl] = (_window_sum(dd / cnt, w, row, s, True) - dd).astype(BF16)

    return _pc(body, "pool_bwd", (1,),
               [pl.BlockSpec((s, NH * HD), lambda i: (0, 0)), pl.BlockSpec((s, NH * HD), lambda i: (0, 0)),
                pl.BlockSpec((NH, HD, HD), lambda i: (0, 0, 0)), pl.BlockSpec((1, NH * HD), lambda i: (0, 0))],
               [pl.BlockSpec((s, NH * HD), lambda i: (0, 0)), pl.BlockSpec((NH, HD, HD), lambda i: (0, 0, 0)),
                pl.BlockSpec((1, NH * HD), lambda i: (0, 0))],
               [_sds((s, NH * HD), BF16), _sds((NH, HD, HD)), _sds((1, NH * HD))])(dy, d, pool_w, pool_scale)


def _gelu(v):
    return 0.5 * v * (1.0 + jnp.tanh(GELU_C * (v + 0.044715 * (v * v * v))))


def _gelu_grad(v):
    t = jnp.tanh(GELU_C * (v + 0.044715 * (v * v * v)))
    return 0.5 * (1.0 + t) + 0.5 * v * (1.0 - t * t) * (GELU_C * (1.0 + 3.0 * 0.044715 * (v * v)))


def _causal_mask():
    return lax.broadcasted_iota(jnp.int32, (HD, HD), 0) >= lax.broadcasted_iota(jnp.int32, (HD, HD), 1)


def _sgu_specs():
    w = NH * HD
    return [pl.BlockSpec((HD, w), lambda c: (c, 1)), pl.BlockSpec((HD, w), lambda c: (c, 2)),
            pl.BlockSpec((1, w), lambda c: (0, 0)), pl.BlockSpec((1, w), lambda c: (0, 0)),
            pl.BlockSpec((NH, HD, HD), lambda c: (0, 0, 0)), pl.BlockSpec((HD, LANE), lambda c: (0, 0))]


def _sgu_head(v, lng_ref, lnb_ref, w_ref, h):
    sl = slice(h * HD, (h + 1) * HD)
    vh = v[:, sl]
    xc = vh - jnp.mean(vh, axis=-1, keepdims=True)
    rs = lax.rsqrt(jnp.mean(xc * xc, axis=-1, keepdims=True) + EPS)
    vhat = xc * rs
    vn = _bf(vhat * lng_ref[:, sl] + lnb_ref[:, sl])
    wc = _bf(jnp.where(_causal_mask(), w_ref[h], 0.0))
    return sl, rs, vhat, vn, wc


def _sgu_fwd(z, ln_g, ln_b, sgu_w, sgu_bt):
    s = z.shape[0]

    def body(zu_ref, zv_ref, lng_ref, lnb_ref, w_ref, bt_ref, y_ref):
        u, v = _gelu(zu_ref[...]), _gelu(zv_ref[...])
        for h in range(NH):
            sl, _, _, vn, wc = _sgu_head(v, lng_ref, lnb_ref, w_ref, h)
            sp = jnp.dot(wc, vn, preferred_element_type=F32) + bt_ref[:, h:h + 1]
            y_ref[:, sl] = (u[:, sl] * sp).astype(BF16)

    return _pc(body, "sgu_fwd", (s // HD,), _sgu_specs(), pl.BlockSpec((HD, NH * HD), lambda c: (c, 0)),
               _sds((s, NH * HD), BF16))(z, z, ln_g, ln_b, sgu_w, sgu_bt)


def _sgu_bwd(z, dy, ln_g, ln_b, sgu_w, sgu_bt, head_sum):
    s = z.shape[0]
    w = NH * HD
    nc = s // HD

    def body(zu_ref, zv_ref, lng_ref, lnb_ref, w_ref, bt_ref, dy_ref, hs_ref,
             dzu_ref, dzv_ref, dlng_ref, dlnb_ref, dw_ref, dbt_ref, dsacc_ref):
        c = pl.program_id(0)
        first = c == 0
        zu, zv = zu_ref[...], zv_ref[...]
        u, v = _gelu(zu), _gelu(zv)
        dyv = dy_ref[...]
        gu, gv = _gelu_grad(zu), _gelu_grad(zv)
        ds = dyv * u
        _acc(dsacc_ref, first, ds)
        for h in range(NH):
            sl, rs, vhat, vn, wc = _sgu_head(v, lng_ref, lnb_ref, w_ref, h)
            sp = jnp.dot(wc, vn, preferred_element_type=F32) + bt_ref[:, h:h + 1]
            dzu_ref[:, sl] = (dyv[:, sl] * sp * gu[:, sl]).astype(BF16)
            dsh = _bf(ds[:, sl])
            dwh = lax.dot_general(dsh, vn, (((1,), (1,)), ((), ())), preferred_element_type=F32)
            dwh = jnp.where(_causal_mask(), dwh, 0.0)

            @pl.when(first)
            def _():
                dw_ref[h] = dwh

            @pl.when(jnp.logical_not(first))
            def _():
                dw_ref[h] += dwh

            dvn = lax.dot_general(wc, dsh, (((0,), (0,)), ((), ())), preferred_element_type=F32)
            g_col = _colsum(dvn * vhat)
            b_col = _colsum(dvn)

            @pl.when(first)
            def _():
                dlng_ref[:, sl] = g_col
                dlnb_ref[:, sl] = b_col

            @pl.when(jnp.logical_not(first))
            def _():
                dlng_ref[:, sl] += g_col
                dlnb_ref[:, sl] += b_col

            dvh = dvn * lng_ref[:, sl]
            dv = rs * (dvh - jnp.mean(dvh, axis=-1, keepdims=True) - vhat * jnp.mean(dvh * vhat, axis=-1, keepdims=True))
            dzv_ref[:, sl] = (dv * gv[:, sl]).astype(BF16)

        @pl.when(c == nc - 1)
        def _():
            dbt_ref[...] = jnp.dot(dsacc_ref[...], hs_ref[...], preferred_element_type=F32, precision=HIGHEST)

    outs = _pc(body, "sgu_bwd", (nc,),
               _sgu_specs() + [pl.BlockSpec((HD, w), lambda c: (c, 1)), pl.BlockSpec((w, LANE), lambda c: (0, 0))],
               [pl.BlockSpec((HD, w), lambda c: (c, 0))] * 2 + [pl.BlockSpec((1, w), lambda c: (0, 0))] * 2
               + [pl.BlockSpec((NH, HD, HD), lambda c: (0, 0, 0)), pl.BlockSpec((HD, LANE), lambda c: (0, 0))],
               [_sds((s, w), BF16)] * 2 + [_sds((1, w))] * 2 + [_sds((NH, HD, HD)), _sds((HD, LANE))],
               scratch=[pltpu.VMEM((HD, w), F32)])(z, z, ln_g, ln_b, sgu_w, sgu_bt, dy, head_sum)
    return outs


def _cmul(ar, ai, br, bi):
    return ar * br - ai * bi, ar * bi + ai * br


def _ssm_prep(lam_re, lam_im, lam_re_rep, lam_im_rep, log_dt, b_re, b_im):
    def disc(lr, li, dt):
        mag = jnp.exp(lr * dt)
        return mag * jnp.cos(li * dt), mag * jnp.sin(li * dt)

    def body(lr_ref, li_ref, lrr_ref, lir_ref, ldt_ref, br_ref, bi_ref, or_ref, oi_ref, bbr_ref, bbi_ref):
        dt = jnp.exp(ldt_ref[...])
        or_ref[...], oi_ref[...] = disc(lr_ref[...], li_ref[...], dt)
        lr, li = lrr_ref[...], lir_ref[...]
        er, ei = disc(lr, li, dt)
        den = lr * lr + li * li
        kr = ((er - 1.0) * lr + ei * li) / den
        ki = (ei * lr - (er - 1.0) * li) / den
        bbr_ref[...], bbi_ref[...] = _cmul(kr, ki, br_ref[...], bi_ref[...])

    small = pl.BlockSpec((SSM_G, SSM_P), lambda i: (0, 0))
    wide = pl.BlockSpec((SSM_G, SSM_P * SSM_N), lambda i: (0, 0))
    col = pl.BlockSpec((SSM_G, 1), lambda i: (0, 0))
    return _pc(body, "ssm_prep", (1,), [small, small, wide, wide, col, wide, wide], [small, small, wide, wide],
               [_sds((SSM_G, SSM_P))] * 2 + [_sds((SSM_G, SSM_P * SSM_N))] * 2)(
        lam_re, lam_im, lam_re_rep, lam_im_rep, log_dt, b_re, b_im)


def _ssm_param_bwd(g_lam_re, g_lam_im, g_bb_re, g_bb_im, lam_re, lam_im, lam_re_rep, lam_im_rep, log_dt, b_re, b_im, seg):
    def body(glr_ref, gli_ref, gbr_ref, gbi_ref, lr_ref, li_ref, lrr_ref, lir_ref, ldt_ref, br_ref, bi_ref, seg_ref,
             dlr_ref, dli_ref, ddt_ref, dbr_ref, dbi_ref):
        dt = jnp.exp(ldt_ref[...])
        lr, li = lrr_ref[...], lir_ref[...]
        mag = jnp.exp(lr * dt)
        er, ei = mag * jnp.cos(li * dt), mag * jnp.sin(li * dt)
        den = lr * lr + li * li
        kr = ((er - 1.0) * lr + ei * li) / den
        ki = (ei * lr - (er - 1.0) * li) / den
        gbr, gbi = gbr_ref[...], gbi_ref[...]
        dbr_ref[...], dbi_ref[...] = _cmul(kr, -ki, gbr, gbi)
        tr, ti = _cmul(br_ref[...], -bi_ref[...], gbr, gbi)
        gkr = jnp.dot(tr, seg_ref[...], preferred_element_type=F32, precision=HIGHEST)
        gki = jnp.dot(ti, seg_ref[...], preferred_element_type=F32, precision=HIGHEST)
        lr, li = lr_ref[...], li_ref[...]
        mag = jnp.exp(lr * dt)
        er, ei = mag * jnp.cos(li * dt), mag * jnp.sin(li * dt)
        den = lr * lr + li * li
        ir, ii = lr / den, -li / den
        kr, ki = _cmul(er - 1.0, ei, ir, ii)
        ar, ai = _cmul(ir, -ii, gkr, gki)
        glr, gli = glr_ref[...] + ar, gli_ref[...] + ai
        qr, qi = _cmul(kr, ki, ir, ii)
        g1r, g1i = _cmul(-qr, qi, gkr, gki)
        g2r, g2i = _cmul(dt * er, -dt * ei, glr, gli)
        dlr_ref[...] = g1r + g2r
        dli_ref[...] = g1i + g2i
        wr, wi = _cmul(lr, li, er, ei)
        g_dt = jnp.sum(wr * glr + wi * gli, axis=-1, keepdims=True)
        ddt_ref[...] = jnp.broadcast_to(dt * g_dt, (SSM_G, LANE))

    small = pl.BlockSpec((SSM_G, SSM_P), lambda i: (0, 0))
    wide = pl.BlockSpec((SSM_G, SSM_P * SSM_N), lambda i: (0, 0))
    col = pl.BlockSpec((SSM_G, 1), lambda i: (0, 0))
    segs = pl.BlockSpec((SSM_P * SSM_N, SSM_P), lambda i: (0, 0))
    return _pc(body, "ssm_param_bwd", (1,), [small, small, wide, wide, small, small, wide, wide, col, wide, wide, segs],
               [small, small, pl.BlockSpec((SSM_G, LANE), lambda i: (0, 0)), wide, wide],
               [_sds((SSM_G, SSM_P))] * 2 + [_sds((SSM_G, LANE))] + [_sds((SSM_G, SSM_P * SSM_N))] * 2)(
        g_lam_re, g_lam_im, g_bb_re, g_bb_im, lam_re, lam_im, lam_re_rep, lam_im_rep, log_dt, b_re, b_im, seg)


SCAN_LANES = 512
SCAN_ROWS = 8


SCAN_GROUPS = SCAN_LANES // SSM_P
SCAN_COLS = SCAN_GROUPS * SSM_N
SCAN_CHUNK = 256


def _ssm_scan(name, v, w_in, lam_re, lam_im, w_out, reverse, states=None):
    s = v.shape[0]
    ln, rows, ch = SCAN_LANES, SCAN_ROWS, min(SCAN_CHUNK, s)
    nch, ntile = s // ch, ch // rows
    nt_dims = (((1,), (1,)), ((), ()))
    with_sum = states is not None

    def body(*refs):
        v_ref, win_ref, lr_ref, li_ref, wout_ref = refs[:5]
        n_in = 7 if with_sum else 5
        or_ref, oi_ref, y_ref = refs[n_in:n_in + 3]
        br_s, bi_s = refs[-2:]
        l1 = (lr_ref[...], li_ref[...])
        pw = [l1]
        for _ in range(rows - 1):
            pw.append(_cmul(*pw[-1], *l1))
        row = lax.broadcasted_iota(jnp.int32, (rows, ln), 0)
        expo = (rows - row) if reverse else (row + 1)
        pr = jnp.zeros((rows, ln), F32)
        pi = jnp.zeros((rows, ln), F32)
        for e in range(1, rows + 1):
            pr = jnp.where(expo == e, pw[e - 1][0], pr)
            pi = jnp.where(expo == e, pw[e - 1][1], pi)
        lk = {}
        for k in (1, 2, 4):
            keep = (row < rows - k) if reverse else (row >= k)
            lk[k] = (jnp.where(keep, pw[k - 1][0], 0.0), jnp.where(keep, pw[k - 1][1], 0.0))

        def chunk(c, carry):
            q0 = pl.multiple_of(((nch - 1 - c) if reverse else c) * ch, ch)
            b = jnp.dot(_bf(v_ref[pl.ds(q0, ch), :]), win_ref[...], preferred_element_type=F32)
            br_s[...] = b[:, :ln]
            bi_s[...] = b[:, ln:]

            def step(i, carry):
                cr, ci = carry[:2]
                r0 = pl.multiple_of(((ntile - 1 - i) if reverse else i) * rows, rows)
                xr, xi = br_s[pl.ds(r0, rows), :], bi_s[pl.ds(r0, rows), :]
                for k in (1, 2, 4):
                    shift = rows - k if reverse else k
                    ar, ai = _cmul(lk[k][0], lk[k][1], pltpu.roll(xr, shift, 0), pltpu.roll(xi, shift, 0))
                    xr, xi = xr + ar, xi + ai
                ar, ai = _cmul(pr, pi, cr, ci)
                xr, xi = xr + ar, xi + ai
                g0 = pl.multiple_of(q0 + r0, rows)
                or_ref[pl.ds(g0, rows), :] = xr
                oi_ref[pl.ds(g0, rows), :] = xi
                if not with_sum:
                    return (xr[rows - 1:rows], xi[rows - 1:rows]) if not reverse else (xr[0:1], xi[0:1])
                nr = jnp.where(row == rows - 1, cr, pltpu.roll(xr, rows - 1, 0))
                ni = jnp.where(row == rows - 1, ci, pltpu.roll(xi, rows - 1, 0))
                sr, si = refs[5][pl.ds(g0, rows), :], refs[6][pl.ds(g0, rows), :]
                return xr[0:1], xi[0:1], carry[2] + (sr * nr + si * ni), carry[3] + (sr * ni - si * nr)

            carry = lax.fori_loop(0, ntile, step, carry)
            w = wout_ref[...]
            y_ref[pl.ds(q0, ch), :] = (
                lax.dot_general(_bf(or_ref[pl.ds(q0, ch), :]), w[:, :ln], nt_dims, preferred_element_type=F32)
                + lax.dot_general(_bf(oi_ref[pl.ds(q0, ch), :]), w[:, ln:], nt_dims, preferred_element_type=F32))
            return carry

        zero = jnp.zeros((1, ln), F32)
        init = (zero, zero) + ((jnp.zeros((rows, ln), F32),) * 2 if with_sum else ())
        carry = lax.fori_loop(0, nch, chunk, init)
        if with_sum:
            refs[n_in + 3][...] = _colsum(carry[2])
            refs[n_in + 4][...] = _colsum(carry[3])

    vec = pl.BlockSpec((1, ln), lambda j: (0, j))
    blk = pl.BlockSpec((s, ln), lambda j: (0, j))
    cols = pl.BlockSpec((s, SCAN_COLS), lambda j: (0, j))
    wspec = pl.BlockSpec((None, SCAN_COLS, 2 * ln), lambda j: (j, 0, 0))
    ins, args = [cols, wspec, vec, vec, wspec], [v, w_in, lam_re, lam_im, w_out]
    outs, shapes = [blk, blk, cols], [_sds((s, SSM_L))] * 2 + [_sds((s, SSM_G * SSM_N))]
    if with_sum:
        ins, args = ins + [blk, blk], args + list(states)
        outs, shapes = outs + [vec, vec], shapes + [_sds((1, SSM_L))] * 2
    return _pc(body, name, (SSM_L // ln,), ins, outs, shapes, scratch=[pltpu.VMEM((ch, ln), F32)] * 2)(*args)


def _ssm_outer(name, v, x_re, x_im):
    s = v.shape[0]
    ts = min(s, 512)
    nt = s // ts
    half = SSM_GB * SSM_P
    rows = SSM_GB * SSM_N
    tn = (((0,), (0,)), ((), ()))

    def body(v_ref, xr_ref, xi_ref, or_ref, oi_ref, acc_ref):
        vv = _bf(v_ref[...])
        pr = lax.dot_general(vv, _bf(xr_ref[...]), tn, preferred_element_type=F32)
        pi = lax.dot_general(vv, _bf(xi_ref[...]), tn, preferred_element_type=F32)
        t = pl.program_id(1)

        @pl.when(t == 0)
        def _():
            acc_ref[:, :half] = pr
            acc_ref[:, half:] = pi

        @pl.when(t > 0)
        def _():
            acc_ref[:, :half] += pr
            acc_ref[:, half:] += pi

        @pl.when(t == nt - 1)
        def _():
            row_g = lax.broadcasted_iota(jnp.int32, (rows, LANE), 0) // SSM_N
            lane_g = lax.broadcasted_iota(jnp.int32, (rows, LANE), 1) // SSM_P
            for part, o_ref in enumerate((or_ref, oi_ref)):
                fold = jnp.zeros((rows, LANE), F32)
                for cb in range(half // LANE):
                    blk = acc_ref[:, part * half + cb * LANE:part * half + (cb + 1) * LANE]
                    fold = fold + jnp.where(2 * cb + lane_g == row_g, blk, 0.0)
                o_ref[...] = jnp.where(row_g % 2 == 0, fold, pltpu.roll(fold, SSM_P, 1))

    xin = pl.BlockSpec((ts, half), lambda q, t: (t, q))
    out = pl.BlockSpec((None, rows, LANE), lambda q, t: (q, 0, 0))
    return _pc(body, name, (SSM_NB, nt), [pl.BlockSpec((ts, rows), lambda q, t: (t, q)), xin, xin], [out, out],
               [_sds((SSM_NB, rows, LANE))] * 2, scratch=[pltpu.VMEM((rows, 2 * half), F32)])(v, x_re, x_im)


def _ssm_act_fwd(y, u, d_skip):
    s = y.shape[0]
    ts = min(s, 512)

    def body(y_ref, u_ref, d_ref, o_ref):
        o_ref[...] = _gelu(y_ref[...] + d_ref[...] * u_ref[...]).astype(BF16)

    return _pc(body, "ssm_act_fwd", (s // ts,), [_row_spec(ts, D)] * 2 + [_vec_spec(D)], _row_spec(ts, D),
               _sds((s, D), BF16))(y, u, d_skip)


def _ssm_act_bwd(dg, y, u, d_skip):
    s = y.shape[0]
    ts = min(s, 512)

    def body(dg_ref, y_ref, u_ref, d_ref, dy_ref, dd_ref):
        uv = u_ref[...]
        dy = dg_ref[...] * _gelu_grad(y_ref[...] + d_ref[...] * uv)
        dy_ref[...] = dy.astype(BF16)
        _acc(dd_ref, pl.program_id(0) == 0, _colsum(dy * uv))

    return _pc(body, "ssm_act_bwd", (s // ts,), [_row_spec(ts, D)] * 3 + [_vec_spec(D)], [_row_spec(ts, D), _vec_spec(D)],
               [_sds((s, D), BF16), _sds((1, D))])(dg, y, u, d_skip)


def _axpy(a, b, d_skip):
    s = a.shape[0]
    ts = min(s, 512)

    def body(a_ref, b_ref, d_ref, o_ref):
        o_ref[...] = (a_ref[...] + d_ref[...] * b_ref[...].astype(F32)).astype(BF16)

    return _pc(body, "ssm_du", (s // ts,), [_row_spec(ts, D)] * 2 + [_vec_spec(D)], _row_spec(ts, D),
               _sds((s, D), BF16))(a, b, d_skip)


def _glu_fwd(zz):
    s = zz.shape[0]
    ts = min(s, 512)

    def body(a_ref, b_ref, o_ref):
        o_ref[...] = a_ref[...] * _sigmoid(b_ref[...])

    return _pc(body, "glu_fwd", (s // ts,), [_row_spec(ts, D, 0), _row_spec(ts, D, 1)], _row_spec(ts, D), _sds((s, D)))(zz, zz)


def _glu_bwd(zz, df):
    s = zz.shape[0]
    ts = min(s, 512)

    def body(a_ref, b_ref, df_ref, o_ref):
        sg = _sigmoid(b_ref[...])
        dfv = df_ref[...].astype(F32)
        o_ref[:, :D] = (dfv * sg).astype(BF16)
        o_ref[:, D:] = (dfv * a_ref[...] * sg * (1.0 - sg)).astype(BF16)

    return _pc(body, "glu_bwd", (s // ts,), [_row_spec(ts, D, 0), _row_spec(ts, D, 1), _row_spec(ts, D)],
               _row_spec(ts, 2 * D), _sds((s, 2 * D), BF16))(zz, zz, df)


def _ssm_block_diag(m_re, m_im):
    rows, half = SCAN_COLS, SCAN_LANES
    expand = jnp.tile(jnp.eye(SSM_P, dtype=BF16), (1, SCAN_GROUPS))

    def body(mr_ref, mi_ref, e_ref, o_ref):
        keep = (lax.broadcasted_iota(jnp.int32, (rows, half), 0) // SSM_N
                == lax.broadcasted_iota(jnp.int32, (rows, half), 1) // SSM_P)
        for part, m_ref in enumerate((mr_ref, mi_ref)):
            t = jnp.dot(_bf(m_ref[...]), e_ref[...], preferred_element_type=F32)
            o_ref[:, part * half:(part + 1) * half] = jnp.where(keep, t, 0.0).astype(BF16)

    blk = pl.BlockSpec((rows, SSM_P), lambda q: (q, 0))
    nb = SSM_G // SCAN_GROUPS
    return _pc(body, "ssm_block_diag", (nb,), [blk, blk, pl.BlockSpec((SSM_P, half), lambda q: (0, 0))],
               pl.BlockSpec((None, rows, 2 * half), lambda q: (q, 0, 0)), _sds((nb, rows, 2 * half), BF16))(m_re, m_im, expand)


def _mod_part(c_all, ada_w):
    n = ada_w.shape[-1]

    def body(c_ref, w_ref, o_ref):
        cv = c_ref[...]
        cond = _bf(cv * _sigmoid(cv))
        o_ref[...] = jnp.dot(cond, _bf(w_ref[...]), preferred_element_type=F32)

    return _pc(body, "mod_part", (2,), [pl.BlockSpec((N_DEV, D), lambda l: (0, 0)), pl.BlockSpec((None, D, n), lambda l: (l, 0, 0))],
               pl.BlockSpec((None, N_DEV, n), lambda l: (l, 0, 0)), _sds((2, N_DEV, n)))(c_all, ada_w)


def _ada_w_grad(c_all_t, dmod):
    n = dmod.shape[-1]
    tr = 128

    def body(c_ref, d_ref, o_ref):
        cv = c_ref[...]
        cond = _bf(cv * _sigmoid(cv)).astype(F32)
        dm = _bf(d_ref[...]).astype(F32)
        acc = cond[:, 0:1] * dm[0:1, :]
        for b in range(1, N_DEV):
            acc = acc + cond[:, b:b + 1] * dm[b:b + 1, :]
        o_ref[...] = acc

    return _pc(body, "ada_w_grad", (2, D // tr),
               [pl.BlockSpec((tr, N_DEV), lambda l, t: (t, 0)), pl.BlockSpec((None, N_DEV, n), lambda l, t: (l, 0, 0))],
               pl.BlockSpec((None, tr, n), lambda l, t: (l, t, 0)), _sds((2, D, n)))(c_all_t, dmod)


def _adamw(name, parts, w, m, v, slot=0, prev=None, after=None):
    p, r, c = parts.shape
    tr = r
    while tr * c * 4 > (1 << 20) and tr % 16 == 0:
        tr //= 2
    nt = r // tr

    def body(p_ref, w_ref, m_ref, v_ref, *rest):
        g_ref, d_ref, nm_ref, nv_ref = rest[-4:]
        g = p_ref[0].astype(F32)
        for i in range(1, p):
            g = g + p_ref[i].astype(F32)
        g_ref[...] = g
        m2 = B1 * m_ref[...] + (1.0 - B1) * g
        v2 = B2 * v_ref[...] + (1.0 - B2) * (g * g)
        nm_ref[...] = m2
        nv_ref[...] = v2
        m_hat = m2 / (1.0 - B1 ** STEP)
        v_hat = v2 / (1.0 - B2 ** STEP)
        d_ref[...] = -LR * (m_hat / (jnp.sqrt(v_hat) + ADAM_EPS) + WD * w_ref[...])

    blk = pl.BlockSpec((tr, c), lambda t: (slot * nt + t, 0))
    in_specs = [pl.BlockSpec((p, tr, c), lambda t: (0, t, 0)), blk, blk, blk]
    unread = list(prev or []) + ([after] if after is not None else [])
    return pl.pallas_call(
        body, name=name, grid=(nt,), in_specs=in_specs + [pl.BlockSpec(memory_space=pl.ANY)] * len(unread), out_specs=[blk] * 4,
        out_shape=[_sds(w.shape)] * 4, input_output_aliases={4 + i: i for i in range(4)} if prev else {},
        compiler_params=pltpu.CompilerParams(dimension_semantics=("arbitrary",), vmem_limit_bytes=VMEM_LIMIT_BYTES))(parts, w, m, v, *unread)


def _sum_parts(parts):
    p, r, c = parts.shape
    tr = r
    while tr * c * 4 > (1 << 19) and tr % 16 == 0:
        tr //= 2

    def body(p_ref, o_ref):
        g = p_ref[0]
        for i in range(1, p):
            g = g + p_ref[i]
        o_ref[...] = g

    return _pc(body, "sum_parts", (r // tr,), [pl.BlockSpec((p, tr, c), lambda t: (0, t, 0))], pl.BlockSpec((tr, c), lambda t: (t, 0)),
               _sds((r, c)))(parts)


def _place():
    x, y, c = lax.axis_index("x"), lax.axis_index("y"), lax.axis_index("c")
    peers = []
    for k in range(1, N_DEV):
        px = (1 - x) if k & 4 else x
        py = (1 - y) if k & 2 else y
        pc = (1 - c) if k & 1 else c
        peers.append(((px, py, pc), 4 * px + 2 * py + pc))
    return 4 * x + 2 * y + c, peers


def _at(ref, idx):
    return ref if idx is None else ref.at[idx]


def _exchange_copies(plan, n, src_refs, dst_refs, send_sems, recv_sems, local_sems=None, with_arrivals=True):
    me, peers = _place()
    local = [] if local_sems is None else [
        pltpu.make_async_copy(_at(src_refs[si], sx), _at(dst_refs[di], dx), local_sems.at[i])
        for i, (si, sx, di, dx) in enumerate(plan(me, me, 0))]

    def remote(k, i, dev, entry):
        si, sx, di, dx = entry
        return pltpu.make_async_remote_copy(_at(src_refs[si], sx), _at(dst_refs[di], dx), send_sems.at[k * n + i], recv_sems.at[k * n + i],
                                            device_id=dev, device_id_type=MESH)

    sends = [remote(k, i, dev, e) for k, (dev, peer) in enumerate(peers) for i, e in enumerate(plan(me, peer, k + 1))]
    if not with_arrivals:
        return local, sends, []
    arrivals = [remote(k, i, dev, e) for k, (dev, peer) in enumerate(peers) for i, e in enumerate(plan(peer, me, k + 1))]
    return local, sends, arrivals


def _sem_shapes(n_copies, local=True):
    sems = [pltpu.SemaphoreType.DMA(((N_DEV - 1) * n_copies,)), pltpu.SemaphoreType.DMA(((N_DEV - 1) * n_copies,))]
    return sems + [pltpu.SemaphoreType.DMA((n_copies,))] if local else sems


def _exchange(name, srcs, dst_shapes, plan, n_copies):
    ns, nd = len(srcs), len(dst_shapes)

    def body(*refs):
        local, sends, arrivals = _exchange_copies(plan, n_copies, refs[:ns], refs[ns:ns + nd], *refs[ns + nd:])
        for cp in local + sends:
            cp.start()
        for cp in arrivals:
            cp.wait_recv()
        for cp in sends:
            cp.wait_send()
        for cp in local:
            cp.wait()

    any_spec = pl.BlockSpec(memory_space=pl.ANY)
    return pl.pallas_call(
        body, name=name, in_specs=[any_spec] * ns, out_specs=[any_spec] * nd, out_shape=list(dst_shapes),
        scratch_shapes=_sem_shapes(n_copies))(*srcs)


HBM_SPEC = pl.BlockSpec(memory_space=pltpu.HBM)
SEM_SPEC = pl.BlockSpec(memory_space=pltpu.SEMAPHORE)
ANY_SPEC = pl.BlockSpec(memory_space=pl.ANY)
TOKEN_SPEC = pl.BlockSpec(memory_space=pltpu.VMEM)
SIDE_EFFECT = pltpu.SideEffectType.DATAFLOW_SIDE_EFFECTING


def _wait_all(local, sends, arrivals):
    for cp in arrivals:
        cp.wait_recv()
    for cp in sends:
        cp.wait_send()
    for cp in local:
        cp.wait()


def _exchange_start(name, srcs, dst_shapes, plan, n_copies, order):
    ns, nd = len(srcs), len(dst_shapes)
    nb = ns + nd

    def body(*refs):
        local, sends, _ = _exchange_copies(plan, n_copies, refs[:ns], refs[ns:nb], *refs[nb + 1:nb + 4], with_arrivals=False)
        for cp in local + sends:
            cp.start()
        refs[-1][...] = jnp.zeros((8, LANE), F32)

    lands = [pltpu.with_memory_space_constraint(lax.empty(d.shape, d.dtype), pltpu.HBM) for d in dst_shapes]
    srcs = [pltpu.with_memory_space_constraint(a, pltpu.HBM) for a in srcs]
    bufs = srcs + lands
    out = pl.pallas_call(
        body, name=name, in_specs=[HBM_SPEC] * nb + [ANY_SPEC],
        out_specs=[SEM_SPEC] * 3 + [HBM_SPEC] * nb + [TOKEN_SPEC],
        out_shape=_sem_shapes(n_copies) + [pltpu.HBM(a.shape, a.dtype) for a in bufs] + [_sds((8, LANE))],
        input_output_aliases={i: 3 + i for i in range(nb)},
        compiler_params=pltpu.CompilerParams(has_side_effects=SIDE_EFFECT))(*bufs, order)
    return out[:3], out[3:3 + ns], out[3 + ns:3 + nb], out[-1]


def _exchange_relay(name, sems, srcs, lands, plan, n_copies, plan2, n_copies2, after):
    ns, nd = len(srcs), len(lands)
    nb = ns + nd

    def body(*refs):
        land_refs = refs[ns:nb]
        _wait_all(*_exchange_copies(plan, n_copies, refs[:ns], land_refs, *refs[nb:nb + 3]))
        _, sends, _ = _exchange_copies(plan2, n_copies2, land_refs, land_refs, *refs[nb + 4:nb + 6], with_arrivals=False)
        for cp in sends:
            cp.start()
        refs[-1][...] = jnp.zeros((8, LANE), F32)

    out = pl.pallas_call(
        body, name=name, in_specs=[HBM_SPEC] * nb + [SEM_SPEC] * 3 + [ANY_SPEC],
        out_specs=[SEM_SPEC] * 2 + [HBM_SPEC] * nd + [TOKEN_SPEC],
        out_shape=_sem_shapes(n_copies2, local=False) + [pltpu.HBM(a.shape, a.dtype) for a in lands] + [_sds((8, LANE))],
        input_output_aliases={ns + i: 2 + i for i in range(nd)},
        compiler_params=pltpu.CompilerParams(has_side_effects=SIDE_EFFECT))(*srcs, *lands, *sems, after)
    return out[:2], out[2:2 + nd], out[-1]


def _exchange_wait(name, sems, srcs, lands, plan, n_copies, after):
    srcs = [] if srcs is None else list(srcs)
    ns, nd = len(srcs), len(lands)
    nb = ns + nd

    def body(*refs):
        land_refs = refs[ns:nb]
        _wait_all(*_exchange_copies(plan, n_copies, refs[:ns] if ns else land_refs, land_refs, *refs[nb:nb + len(sems)]))

    bufs = srcs + list(lands)
    out = pl.pallas_call(
        body, name=name, in_specs=[HBM_SPEC] * nb + [SEM_SPEC] * len(sems) + [ANY_SPEC],
        out_specs=[HBM_SPEC] * nb, out_shape=[pltpu.HBM(a.shape, a.dtype) for a in bufs],
        input_output_aliases={i: i for i in range(nb)},
        compiler_params=pltpu.CompilerParams(has_side_effects=SIDE_EFFECT))(*bufs, *sems, after)
    return out[ns:]


def _all_gather(name, arrs):
    plan = lambda me, peer, k: [(i, None, i, me) for i in range(len(arrs))]
    return _exchange(name, arrs, [_sds((N_DEV,) + a.shape, a.dtype) for a in arrs], plan, len(arrs))


def _post_pre_fwd(x, f, g_post, gate, rw, g_pre, scale, shift):
    s = x.shape[0]
    ts = min(s, 256)

    def body(x_ref, f_ref, gp_ref, gt_ref, g_ref, sc_ref, sh_ref, xo_ref, h_ref):
        fv = f_ref[...]
        xv = x_ref[...] + (rw * gt_ref[...]) * (fv * _rstd(fv) * gp_ref[...])
        xo_ref[...] = xv
        h_ref[...] = ((xv * _rstd(xv) * g_ref[...]) * (1.0 + sc_ref[...]) + sh_ref[...]).astype(BF16)

    return _pc(body, "post_pre_fwd", (s // ts,), [_row_spec(ts, D)] * 2 + [_vec_spec(D)] * 5, [_row_spec(ts, D)] * 2,
               [_sds((s, D)), _sds((s, D), BF16)])(x, f, g_post, gate, g_pre, scale, shift)


def _pre_post_bwd(dout, dh, x, g_pre, scale, f, g_post, gate, rw):
    s = dout.shape[0]
    ts = min(s, 256)

    def body(do_ref, dh_ref, x_ref, g_ref, sc_ref, f_ref, gp_ref, gt_ref,
             dx_ref, dsh_ref, dsc_ref, dg_ref, df_ref, dgate_ref, dgp_ref):
        first = pl.program_id(0) == 0
        dhv, xv, gv = dh_ref[...], x_ref[...], g_ref[...]
        r = _rstd(xv)
        xn = xv * r
        _acc(dsh_ref, first, _colsum(dhv))
        _acc(dsc_ref, first, _colsum(dhv * (xn * gv)))
        dhp = dhv * (1.0 + sc_ref[...])
        _acc(dg_ref, first, _colsum(dhp * xn))
        dxn = dhp * gv
        dx = do_ref[...] + r * (dxn - xn * jnp.mean(dxn * xn, axis=-1, keepdims=True))
        dx_ref[...] = dx
        fv, gpv = f_ref[...], gp_ref[...]
        rf = _rstd(fv)
        fn = fv * rf
        _acc(dgate_ref, first, rw * _colsum(dx * (fn * gpv)))
        dy = (rw * gt_ref[...]) * dx
        _acc(dgp_ref, first, _colsum(dy * fn))
        dfn = dy * gpv
        df_ref[...] = (rf * (dfn - fn * jnp.mean(dfn * fn, axis=-1, keepdims=True))).astype(BF16)

    rows, vec = _row_spec(ts, D), _vec_spec(D)
    return _pc(body, "pre_post_bwd", (s // ts,), [rows] * 3 + [vec] * 2 + [rows] + [vec] * 2,
               [rows, vec, vec, vec, rows, vec, vec],
               [_sds((s, D))] + [_sds((1, D))] * 3 + [_sds((s, D), BF16)] + [_sds((1, D))] * 2)(
        dout, dh, x, g_pre, scale, f, g_post, gate)


def _mix0_fwd(h, p):
    z = _mm_nt("mix0_in", h, p["ab_w_in"])
    y_a, d = _pool_fwd(z, p["pool_w"], p["pool_scale"])
    y_b = _sgu_fwd(z, p["sgu_ln_g"], p["sgu_ln_b"], p["sgu_w"], p["sgu_bt"])
    ycat = jnp.concatenate([y_a, y_b], axis=1)
    return _mm_nn("mix0_out", ycat, p["ab_w_out"]), (h, z, d, ycat)


def _mix0_bwd(df, saved, p, after):
    h, z, d, ycat = saved
    dycat = _mm_nt("mix0_out_dx", df, p["ab_w_out"], after=after)
    g = {"ab_w_out": _mm_tn("mix0_out_dw", ycat, df, BF16)}
    dz_p, g["pool_w"], g["pool_scale"] = _pool_bwd(dycat, d, p["pool_w"], p["pool_scale"])
    dz_u, dz_v, g["sgu_ln_g"], g["sgu_ln_b"], g["sgu_w"], dbt = _sgu_bwd(
        z, dycat, p["sgu_ln_g"], p["sgu_ln_b"], p["sgu_w"], p["sgu_bt"], p["head_sum"])
    g["sgu_b"] = dbt[:, :NH].T
    dz = jnp.concatenate([dz_p, dz_u, dz_v], axis=1)
    g["ab_w_in"] = _mm_tn("mix0_in_dw", dz, h, BF16)
    return _mm_nn("mix0_in_dx", dz, p["ab_w_in"]), g


def _mix1_fwd(h, p):
    u = _mm_nn("ssm_w_in", h, p["ssm_w_in"])
    x_re, x_im, y = _ssm_scan("ssm_scan_fwd", u, p["wb_bd"], p["lam_bar_re"], p["lam_bar_im"], p["wc_bd"], False)
    g = _ssm_act_fwd(y, u, p["ssm_d"])
    zz = _mm_nn("ssm_glu", g, p["ssm_w_glu"])
    return _glu_fwd(zz), (h, u, x_re, x_im, y, g, zz)


def _mix1_bwd(df, saved, p, after):
    h, u, x_re, x_im, y, g, zz = saved
    gr = {}
    dzz = _glu_bwd(zz, df)
    dg = _mm_nt("ssm_glu_dx", dzz, p["ssm_w_glu"], after=after)
    gr["ssm_w_glu"] = _mm_tn("ssm_glu_dw", g, dzz, BF16)
    dy, gr["ssm_d"] = _ssm_act_bwd(dg, y, u, p["ssm_d"])
    a_re, a_im, du_ssm, g_lam_re, g_lam_im = _ssm_scan(
        "ssm_scan_bwd", dy, p["wc_bd"], p["lam_bar_re"], -p["lam_bar_im"], p["wb_bd"], True, states=(x_re, x_im))
    du = _axpy(du_ssm, dy, p["ssm_d"])
    gr["ssm_w_in"] = _mm_tn("ssm_w_in_dw", h, du, BF16)
    dh = _mm_nt("ssm_w_in_dx", du, p["ssm_w_in"])
    mb_re, mb_im = _ssm_outer("ssm_db", u, a_re, a_im)
    mc_re, mc_im = _ssm_outer("ssm_dc", dy, x_re, x_im)
    per_group = lambda m: m[:, :, :SSM_P].reshape(SSM_G, SSM_N, SSM_P)
    gr["ssm_c_re"] = per_group(mc_re)
    gr["ssm_c_im"] = -per_group(mc_im)
    dlr, dli, ddt, dbr, dbi = _ssm_param_bwd(
        g_lam_re.reshape(SSM_G, SSM_P), g_lam_im.reshape(SSM_G, SSM_P),
        per_group(mb_re).reshape(SSM_G, SSM_N * SSM_P), per_group(mb_im).reshape(SSM_G, SSM_N * SSM_P),
        p["lam_re"], p["lam_im"], p["lam_re_rep"], p["lam_im_rep"], p["log_dt"], p["b_re"], p["b_im"], p["seg"])
    gr["ssm_lam_re"], gr["ssm_lam_im"], gr["ssm_log_dt"] = dlr, dli, ddt[:, 0]
    gr["ssm_b_re"] = dbr.reshape(SSM_G, SSM_N, SSM_P)
    gr["ssm_b_im"] = dbi.reshape(SSM_G, SSM_N, SSM_P)
    return dh, gr


def _ssm_params(lam_re, lam_im, b_re, b_im, c_re, c_im, log_dt):
    wide = lambda b: b.transpose(0, 2, 1).reshape(SSM_G, SSM_N * SSM_P)
    p = {"lam_re": lam_re, "lam_im": lam_im, "log_dt": log_dt.reshape(SSM_G, 1),
         "lam_re_rep": jnp.tile(lam_re, (1, SSM_N)), "lam_im_rep": jnp.tile(lam_im, (1, SSM_N)), "b_re": wide(b_re), "b_im": wide(b_im)}
    lbr, lbi, bbr, bbi = _ssm_prep(lam_re, lam_im, p["lam_re_rep"], p["lam_im_rep"], p["log_dt"], p["b_re"], p["b_im"])
    p["lam_bar_re"], p["lam_bar_im"] = lbr.reshape(1, SSM_L), lbi.reshape(1, SSM_L)
    rows = lambda m: m.reshape(SSM_G * SSM_N, SSM_P)
    p["wb_bd"] = _ssm_block_diag(rows(bbr), rows(bbi))
    p["wc_bd"] = _ssm_block_diag(rows(c_re), rows(-c_im))
    p["seg"] = jnp.tile(jnp.eye(SSM_P, dtype=F32), (SSM_N, 1))
    return p


RES_WEIGHT = (0.5, 1.0, 0.5)


def _local_step(x, tgt, mod, norm_pre, norm_post, weights_of, on_part, on_grads):
    def fns(i, w):
        if i % 3 != 1:
            win, wout_of = w
            return ((lambda h: _ffn_fwd(h, win, wout_of)),
                    (lambda df, sv, after: (_ffn_bwd(df, sv, win, wout_of(None), lambda tag, part: on_part(i, tag, part), after), None)))
        if i == 1:
            return (lambda h: _mix0_fwd(h, w)), (lambda df, sv, after: _mix0_bwd(df, sv, w, after))
        return (lambda h: _mix1_fwd(h, w)), (lambda df, sv, after: _mix1_bwd(df, sv, w, after))

    g_pre = [norm_pre[l, s][None] for l in range(2) for s in range(3)]
    g_post = [norm_post[l, s][None] for l in range(2) for s in range(3)]
    rw = RES_WEIGHT * 2
    mods, saved, bwd = [], [], []
    f = None
    for i in range(6):
        w, token = weights_of(i, x if i == 0 else f)
        fwd, b = fns(i, w)
        m3 = mod[i // 3, i % 3] + token[0:1, 0:1]
        if i == 0:
            h = _prenorm_fwd(x, g_pre[0], m3[1:2], m3[0:1])
        else:
            x, h = _post_pre_fwd(x, f, g_post[i - 1], mods[i - 1][2:3], rw[i - 1], g_pre[i], m3[1:2], m3[0:1])
        f, inner = fwd(h)
        mods.append(m3)
        saved.append((x, f, inner))
        bwd.append(b)
    loss_row, dx = _loss_fwd_bwd(_postnorm_fwd(x, f, g_post[5], mods[5][2:3], rw[5]), tgt)
    df, dgate, dg_post = _postnorm_bwd(dx, f, g_post[5], mods[5][2:3], rw[5])
    token = jnp.zeros((8, LANE), F32)
    for i in reversed(range(6)):
        x_i, _, inner = saved[i]
        dh, extra = bwd[i](df, inner, token)
        if i > 0:
            dx, dshift, dscale, dg_pre, df, dgate_prev, dg_post_prev = _pre_post_bwd(
                dx, dh, x_i, g_pre[i], mods[i][1:2], saved[i - 1][1], g_post[i - 1], mods[i - 1][2:3], rw[i - 1])
        else:
            dx, dshift, dscale, dg_pre = _prenorm_bwd(dx, dh, x_i, g_pre[0], mods[0][1:2])
        token = on_grads(i, extra, jnp.concatenate([dshift, dscale, dgate], axis=0), dg_pre, dg_post, loss_row)
        if i > 0:
            dgate, dg_post = dgate_prev, dg_post_prev
    return dx


def _pad_rows(v, rows):
    return jnp.pad(v, (0, rows * LANE - v.shape[0])).reshape(rows, LANE)


def _pack(parts):
    flat, layout, off = [], [], 0
    for a in parts:
        n = a.size
        padded = -(-n // LANE) * LANE
        flat.append(jnp.pad(a.reshape(-1).astype(F32), (0, padded - n)))
        layout.append((off, n, a.shape))
        off += padded
    return jnp.concatenate(flat), layout


def _unpack(flat, layout):
    return [flat[off:off + n].reshape(shape) for off, n, shape in layout]


SMALL_REPLICATED = ["ada_b", "pool_w", "pool_scale", "sgu_ln_g", "sgu_ln_b", "sgu_w", "sgu_b", "ssm_lam_re", "ssm_lam_im",
                    "ssm_b_re", "ssm_b_im", "ssm_c_re", "ssm_c_im", "ssm_log_dt"]
SMALL_SHARDED = ["norm_pre", "norm_post", "ssm_d"]
TRANSPOSED = ["ffn_w_in", "ab_w_in", "ssm_b_re", "ssm_b_im"]
WEIGHTS = ['ada_w', 'ada_b', 'norm_pre', 'norm_post', 'ffn_w_in', 'ffn_w_out', 'ab_w_in', 'pool_w', 'pool_scale', 'sgu_ln_g',
           'sgu_ln_b', 'sgu_w', 'sgu_b', 'ab_w_out', 'ssm_w_in', 'ssm_lam_re', 'ssm_lam_im', 'ssm_b_re', 'ssm_b_im', 'ssm_c_re',
           'ssm_c_im', 'ssm_d', 'ssm_log_dt', 'ssm_w_glu']


def kernel(x, c, ada_w, ada_b, norm_pre, norm_post, ffn_w_in, ffn_w_out, ab_w_in, pool_w, pool_scale, sgu_ln_g, sgu_ln_b, sgu_w, sgu_b, ab_w_out, ssm_w_in, ssm_lam_re, ssm_lam_im, ssm_b_re, ssm_b_im, ssm_c_re, ssm_c_im, ssm_d, ssm_log_dt, ssm_w_glu, loss_target, m_ada_w, m_ada_b, m_norm_pre, m_norm_post, m_ffn_w_in, m_ffn_w_out, m_ab_w_in, m_pool_w, m_pool_scale, m_sgu_ln_g, m_sgu_ln_b, m_sgu_w, m_sgu_b, m_ab_w_out, m_ssm_w_in, m_ssm_lam_re, m_ssm_lam_im, m_ssm_b_re, m_ssm_b_im, m_ssm_c_re, m_ssm_c_im, m_ssm_d, m_ssm_log_dt, m_ssm_w_glu, v_ada_w, v_ada_b, v_norm_pre, v_norm_post, v_ffn_w_in, v_ffn_w_out, v_ab_w_in, v_pool_w, v_pool_scale, v_sgu_ln_g, v_sgu_ln_b, v_sgu_w, v_sgu_b, v_ab_w_out, v_ssm_w_in, v_ssm_lam_re, v_ssm_lam_im, v_ssm_b_re, v_ssm_b_im, v_ssm_c_re, v_ssm_c_im, v_ssm_d, v_ssm_log_dt, v_ssm_w_glu):
    args = locals()
    wts = {n: args[n] for n in WEIGHTS}
    mom = {n: args["m_" + n] for n in WEIGHTS}
    var = {n: args["v_" + n] for n in WEIGHTS}
    for n in TRANSPOSED:
        for t in (wts, mom, var):
            t[n] = jnp.swapaxes(t[n], -1, -2)
    me = 4 * lax.axis_index("x") + 2 * lax.axis_index("y") + lax.axis_index("c")
    s = x.shape[1]
    nd = D // N_DEV

    small_in, small_in_layout = _pack([c, norm_pre, norm_post, ssm_d])
    small_rows = -(-small_in.shape[0] // (8 * LANE)) * 8
    (g_small,) = _all_gather("gather_small", [_pad_rows(small_in, small_rows)])
    g_small = g_small.reshape(N_DEV, -1)
    c_all, npre_g, npost_g, sd_g = [jnp.stack([_unpack(g_small[j], small_in_layout)[i] for j in range(N_DEV)]) for i in range(4)]
    c_all = c_all.reshape(N_DEV, D)
    norm_pre_full = npre_g.transpose(1, 2, 0, 3).reshape(2, 3, D)
    norm_post_full = npost_g.transpose(1, 2, 0, 3).reshape(2, 3, D)
    ssm_d_full = sd_g.transpose(1, 0, 2).reshape(1, D)

    nw = ada_w.shape[-1]
    (mod_g,) = _all_gather("gather_mod", [_mod_part(c_all, ada_w)])
    mod = lax.dynamic_index_in_dim(mod_g, me, axis=2, keepdims=False)
    mod = (mod.transpose(1, 0, 2).reshape(2, N_DEV * nw) + ada_b).reshape(2, 3, 3, D)

    w_in_t = wts["ffn_w_in"]
    shards = [[w_in_t[0, 0]], [ffn_w_out[0, 0]], [wts["ab_w_in"][0], ab_w_out[0]], [w_in_t[0, 1], ffn_w_out[0, 1]],
              [w_in_t[1, 0], ffn_w_out[1, 0]], [ssm_w_in[0], ssm_w_glu[0]], [w_in_t[1, 1], ffn_w_out[1, 1]]]
    same_core = (2, 4, 6)

    def gather_plan(n):
        return lambda me_, peer_, k: [(a, None, a, me_) for a in range(n)] if k in (0, 1) + same_core else []

    def relay_plan(n):
        return lambda me_, peer_, k: [(a, me_ ^ kk, a, me_ ^ kk) for kk in same_core for a in range(n)] if k == 1 else []

    gathers, relays = [], {}
    token = mod_g
    for g, group in enumerate(shards):
        group = [a.astype(BF16) for a in group]
        sems, srcs_thru, lands, token = _exchange_start(
            f"gather_start_{g}", group, [_sds((N_DEV,) + a.shape, BF16) for a in group], gather_plan(len(group)), len(group), token)
        gathers.append((sems, srcs_thru, lands))
    mod = mod + token[0, 0]

    def relay(g, after):
        sems, srcs_thru, lands = gathers[g]
        n = len(lands)
        relays[g] = _exchange_relay(f"gather_relay_{g}", sems, srcs_thru, lands, gather_plan(n), n, relay_plan(n), 3 * n, after)

    def fetch(g, after):
        if g not in relays:
            relay(g, after)
        sems, lands, token = relays[g]
        n = len(lands)
        got = _exchange_wait(f"gather_wait_{g}", sems, None, lands, relay_plan(n), 3 * n, after)
        if 0 < g < len(gathers) - 1:
            relay(g + 1, got[0])
            token = relays[g + 1][2]
        return got, token

    head_sum = jnp.repeat(jnp.eye(NH, LANE, dtype=F32), HD, axis=0)
    mix0 = {"pool_w": pool_w[0], "pool_scale": pool_scale, "sgu_ln_g": sgu_ln_g, "sgu_ln_b": sgu_ln_b, "sgu_w": sgu_w[0],
            "sgu_bt": jnp.pad(sgu_b[0].T, ((0, 0), (0, LANE - NH))), "head_sum": head_sum}
    mix1 = _ssm_params(ssm_lam_re[0], ssm_lam_im[0], ssm_b_re[0], ssm_b_im[0], ssm_c_re[0], ssm_c_im[0], ssm_log_dt[0])
    mix1["ssm_d"] = ssm_d_full

    def weights_of(i, x_in):
        if i == 0:
            (win,), token = fetch(0, x_in)
            cache = []

            def wout_of(z):
                if not cache:
                    cache.append(fetch(1, z)[0][0])
                return cache[0]

            return (win, wout_of), token
        (a, b), token = fetch(i + 1, x_in)
        if i % 3 != 1:
            return (a, lambda z: b), token
        if i == 1:
            return dict(mix0, ab_w_in=a.reshape(-1, D), ab_w_out=b.reshape(D, D)), token
        return dict(mix1, ssm_w_in=a.reshape(D, D), ssm_w_glu=b.transpose(1, 0, 2).reshape(D, -1)), token

    def shard_cols(a):
        r = a.shape[0]
        return a.reshape(r, N_DEV, -1).transpose(1, 0, 2)

    scatter_plan = lambda me_, peer_, k: [(0, peer_, 0, me_), (1, peer_, 1, me_)]
    scatter_plan1 = lambda me_, peer_, k: [(0, peer_, 0, me_)]
    scatters = []
    last_token = [jnp.zeros((8, LANE), F32)]
    pieces, mixer, bundles = {}, {}, {}
    bundle_plan = lambda me_, peer_, k: [(0, None, 0, me_)]

    def on_part(i, tag, part):
        sems, srcs_thru, lands, last_token[0] = _exchange_start(
            f"scatter_start_{i}_{tag}", [part], [_sds(part.shape, BF16)], scatter_plan1, 1, last_token[0])
        scatters.append((i, ("ffn_" + tag,), scatter_plan1, sems, srcs_thru, lands))
        return last_token[0]
    mix0_names = ["pool_w", "pool_scale", "sgu_ln_g", "sgu_ln_b", "sgu_w", "sgu_b"]
    mix1_names = ["ssm_lam_re", "ssm_lam_im", "ssm_b_re", "ssm_b_im", "ssm_c_re", "ssm_c_im", "ssm_log_dt", "ssm_d"]

    def start_bundle(tag, arrays):
        flat, layout = _pack(arrays)
        rows = -(-flat.shape[0] // (8 * LANE)) * 8
        plan = gather_plan(1) if tag == "a" else bundle_plan
        sems, srcs_thru, lands, last_token[0] = _exchange_start(
            f"small_start_{tag}", [_pad_rows(flat, rows)], [_sds((N_DEV, rows, LANE))], plan, 1, last_token[0])
        bundles[tag] = (sems, srcs_thru, lands, layout)

    def on_grads(i, extra, dmod_i, dpre_i, dpost_i, loss_row):
        pieces[i] = (dmod_i, dpre_i, dpost_i)
        if i == 4:
            mixer.update({n: extra[n] for n in mix1_names})
        if i == 1:
            mixer.update({n: extra[n] for n in mix0_names})
            rest = range(1, 6)
            start_bundle("a", [jnp.stack([pieces[j][0] for j in rest])] + [jnp.concatenate([pieces[j][k] for j in rest]) for k in (1, 2)]
                         + [mixer[n] for n in mix0_names + mix1_names])
        if i == 0:
            start_bundle("b", [dmod_i, dpre_i, dpost_i, loss_row])
        if i % 3 != 1:
            return last_token[0]
        if i == 1:
            names, parts = ("ab_w_in", "ab_w_out"), [extra["ab_w_in"].reshape(N_DEV, -1, D), extra["ab_w_out"].reshape(N_DEV, nd, D)]
        else:
            names, parts = ("ssm_w_in", "ssm_w_glu"), [extra["ssm_w_in"].reshape(N_DEV, nd, D), shard_cols(extra["ssm_w_glu"])]
        sems, srcs_thru, lands, last_token[0] = _exchange_start(
            f"scatter_start_{i}", parts, [_sds(a.shape, BF16) for a in parts], scatter_plan, 2, last_token[0])
        scatters.append((i, names, scatter_plan, sems, srcs_thru, lands))
        return last_token[0]

    grad_x = _local_step(x[0], loss_target[0], mod, norm_pre_full, norm_post_full, weights_of, on_part, on_grads)

    out_g, out_d, out_m, out_v = {}, {}, {}, {}
    big_out = {}

    def adam_big(name, recv, n, slot=0):
        c_ = wts[n].shape[-1]
        big_out[n] = _adamw(name, recv.reshape(recv.shape[0], -1, c_), *[t[n].reshape(-1, c_) for t in (wts, mom, var)],
                            slot=slot, prev=big_out.get(n))
        return big_out[n][0]

    ffn_slot = {0: 0, 2: 1, 3: 2, 5: 3}

    def land_and_update(entries, after):
        for i, names, plan, sems, srcs_thru, lands in entries:
            recv = _exchange_wait(f"scatter_wait_{i}_{names[0]}", sems, srcs_thru, lands, plan, len(names), after)
            for n, r in zip(names, recv):
                after = adam_big(f"adamw_{n}_{i}", r, n, ffn_slot.get(i, 0))
        return after

    after = land_and_update([e for e in scatters if e[0] != 0], last_token[0])

    def landed(tag, g_parts):
        layout = bundles[tag][3]
        off, n, shape = layout[0]
        dmods = g_parts.reshape(N_DEV, -1)[:, off:off + n].reshape((N_DEV,) + shape)
        total = _sum_parts(g_parts)
        return dmods, _unpack(total.reshape(-1), layout), total

    def adam_small(n, g, after=None):
        cols = wts[n].shape[-1]
        res = _adamw(f"adamw_{n}", g.reshape(1, -1, cols), *[t[n].reshape(-1, cols) for t in (wts, mom, var)], after=after)
        for o, arr in zip((out_g, out_d, out_m, out_v), res):
            o[n] = arr.reshape(wts[n].shape)
            if n in TRANSPOSED:
                o[n] = jnp.swapaxes(o[n], -1, -2)
        return res[0]

    sems, srcs_thru, lands, _ = bundles["a"]
    sems, lands, _ = _exchange_relay("small_relay_a", sems, srcs_thru, lands, gather_plan(1), 1, relay_plan(1), 3, after)
    (parts_a,) = _exchange_wait("small_wait_a", sems, None, lands, relay_plan(1), 3, after)
    dmods_a, sums_a, after = landed("a", parts_a)
    dmod_a, dpre_a, dpost_a = sums_a[:3]
    small = dict(zip(mix0_names + mix1_names, sums_a[3:]))
    for n in mix0_names + mix1_names:
        g = small[n] if n not in SMALL_SHARDED else lax.dynamic_slice_in_dim(small[n], me * nd, nd, axis=small[n].ndim - 1)
        after = adam_small(n, g, after)
    sems, srcs_thru, lands, _ = bundles["b"]
    (parts_b,) = _exchange_wait("small_wait_b", sems, srcs_thru, lands, bundle_plan, 1, after)
    dmods_b, (dmod_b, dpre_b, dpost_b, loss_sum), _ = landed("b", parts_b)
    gathered = {"a": dmods_a, "b": dmods_b}
    loss = loss_sum[0, 0]
    adam_small("ada_b", jnp.concatenate([dmod_b[None], dmod_a]))
    for n, first, rest in (("norm_pre", dpre_b, dpre_a), ("norm_post", dpost_b, dpost_a)):
        after = adam_small(n, lax.dynamic_slice_in_dim(jnp.concatenate([first, rest]), me * nd, nd, axis=1))

    dmod_all = jnp.concatenate([gathered["b"][:, None], gathered["a"]], axis=1).reshape(N_DEV, 2, N_DEV, nw)
    dmod_mine = lax.dynamic_index_in_dim(dmod_all, me, axis=2, keepdims=False).transpose(1, 0, 2)
    g_ada_w = _ada_w_grad(c_all.T, dmod_mine)
    after = after[0:1, 0:1] + adam_big("adamw_ada_w", g_ada_w[None], "ada_w")[0:1, 0:1]

    land_and_update([e for e in scatters if e[0] == 0], after)
    for n, res in big_out.items():
        for o, arr in zip((out_g, out_d, out_m, out_v), res):
            o[n] = arr.reshape(wts[n].shape)
            if n in TRANSPOSED:
                o[n] = jnp.swapaxes(o[n], -1, -2)

    return (loss, grad_x[None], *[out_g[n] for n in WEIGHTS], *[out_d[n] for n in WEIGHTS],
            *[out_m[n] for n in WEIGHTS], *[out_v[n] for n in WEIGHTS])
```

```python
import functools
import math

import jax
import jax.numpy as jnp
from jax import lax
from jax.experimental import pallas as pl
from jax.experimental.pallas import tpu as pltpu

F32 = jnp.float32
BF16 = jnp.bfloat16
MESH = pl.DeviceIdType.MESH
HIGHEST = lax.Precision.HIGHEST

N_DEV = 8
D = 1024
D_FF = 2816
FSH = 2 * D_FF // N_DEV
EPS = 1e-6
POOL_WINDOWS = (2, 4, 8, 16)
HD = 128
NH = 4
SSM_G, SSM_P, SSM_N = 64, 64, 16
SSM_GB = 16
SSM_NB = SSM_G // SSM_GB
SSM_L = SSM_G * SSM_P
LR, B1, B2, ADAM_EPS, WD, STEP = 0.001, 0.9, 0.999, 1e-08, 0.01, 10
GELU_C = math.sqrt(2.0 / math.pi)
VMEM_LIMIT_BYTES = 48 * 1024 * 1024
LANE = 128


def _pc(body, name, grid, in_specs, out_specs, out_shape, scratch=()):
    return pl.pallas_call(
        body, name=name, grid=grid, in_specs=in_specs, out_specs=out_specs, out_shape=out_shape,
        scratch_shapes=list(scratch),
        compiler_params=pltpu.CompilerParams(dimension_semantics=("arbitrary",) * len(grid),
                                             vmem_limit_bytes=VMEM_LIMIT_BYTES))


def _sds(shape, dtype=F32):
    return jax.ShapeDtypeStruct(tuple(shape), dtype)


def _bf(v):
    return v if v.dtype == BF16 else v.astype(BF16)


def _row_spec(ts, width, col=0):
    return pl.BlockSpec((ts, width), lambda t, _c=col: (t, _c))


def _vec_spec(width, col=0):
    return pl.BlockSpec((1, width), lambda t, _c=col: (0, _c))


def _mm(name, a, b, contract, grid, a_spec, b_spec, o_spec, out_shape, acc_axis=None, after=None):
    dn = (contract, ((), ()))

    def body(a_ref, b_ref, *rest):
        o_ref = rest[-1]
        r = lax.dot_general(_bf(a_ref[...]), _bf(b_ref[...]), dn, preferred_element_type=F32)
        if acc_axis is None:
            o_ref[...] = r.astype(o_ref.dtype)
        else:
            k = pl.program_id(acc_axis)

            @pl.when(k == 0)
            def _():
                o_ref[...] = r

            @pl.when(k > 0)
            def _():
                o_ref[...] += r

    if after is None:
        return _pc(body, name, grid, [a_spec, b_spec], o_spec, out_shape)(a, b)
    return _pc(body, name, grid, [a_spec, b_spec, pl.BlockSpec(memory_space=pl.ANY)], o_spec, out_shape)(a, b, after)


def _tile(s):
    return min(s, 1024)


def _div_tile(n, cap=1024):
    t = min(n, cap) // LANE * LANE
    while n % t:
        t -= LANE
    return t


def _mm_nn(name, a, b, out_dtype=F32):
    s, k = a.shape
    n = b.shape[1]
    ts, tn = _tile(s), _div_tile(n)
    return _mm(name, a, b, ((1,), (0,)), (n // tn, s // ts),
               pl.BlockSpec((ts, k), lambda j, t: (t, 0)), pl.BlockSpec((k, tn), lambda j, t: (0, j)),
               pl.BlockSpec((ts, tn), lambda j, t: (t, j)), _sds((s, n), out_dtype))


def _mm_nt(name, a, b, out_dtype=F32, after=None):
    s, n = a.shape
    k = b.shape[0]
    ts, tk = _tile(s), _div_tile(k)
    return _mm(name, a, b, ((1,), (1,)), (k // tk, s // ts),
               pl.BlockSpec((ts, n), lambda j, t: (t, 0)), pl.BlockSpec((tk, n), lambda j, t: (j, 0)),
               pl.BlockSpec((ts, tk), lambda j, t: (t, j)), _sds((s, k), out_dtype), after=after)


def _mm_tn(name, a, b, out_dtype=F32, tm=512, tn=512):
    s, m = a.shape
    n = b.shape[1]
    tm, tn = min(m, tm), min(n, tn)
    return _mm(name, a, b, ((0,), (0,)), (m // tm, n // tn),
               pl.BlockSpec((s, tm), lambda i, j: (0, i)), pl.BlockSpec((s, tn), lambda i, j: (0, j)),
               pl.BlockSpec((tm, tn), lambda i, j: (i, j)), _sds((m, n), out_dtype))


def _rstd(v):
    return lax.rsqrt(jnp.mean(v * v, axis=-1, keepdims=True) + EPS)


def _prenorm_fwd(x, g, scale, shift):
    s = x.shape[0]
    ts = min(s, 512)

    def body(x_ref, g_ref, sc_ref, sh_ref, h_ref):
        xv = x_ref[...]
        h_ref[...] = ((xv * _rstd(xv) * g_ref[...]) * (1.0 + sc_ref[...]) + sh_ref[...]).astype(BF16)

    return _pc(body, "prenorm_fwd", (s // ts,), [_row_spec(ts, D)] + [_vec_spec(D)] * 3, _row_spec(ts, D),
               _sds((s, D), BF16))(x, g, scale, shift)


def _postnorm_fwd(x, f, g, gate, rw):
    s = x.shape[0]
    ts = min(s, 512)

    def body(x_ref, f_ref, g_ref, gt_ref, o_ref):
        fv = f_ref[...]
        o_ref[...] = x_ref[...] + (rw * gt_ref[...]) * (fv * _rstd(fv) * g_ref[...])

    return _pc(body, "postnorm_fwd", (s // ts,), [_row_spec(ts, D)] * 2 + [_vec_spec(D)] * 2, _row_spec(ts, D),
               _sds((s, D)))(x, f, g, gate)


def _acc(ref, first, v):
    @pl.when(first)
    def _():
        ref[...] = v

    @pl.when(jnp.logical_not(first))
    def _():
        ref[...] += v


def _colsum(v):
    return jnp.sum(v, axis=0, keepdims=True)


def _postnorm_bwd(dout, f, g, gate, rw):
    s = dout.shape[0]
    ts = min(s, 512)

    def body(do_ref, f_ref, g_ref, gt_ref, df_ref, dgate_ref, dg_ref):
        first = pl.program_id(0) == 0
        do, fv, gv = do_ref[...], f_ref[...], g_ref[...]
        r = _rstd(fv)
        fn = fv * r
        _acc(dgate_ref, first, rw * _colsum(do * (fn * gv)))
        dy = (rw * gt_ref[...]) * do
        _acc(dg_ref, first, _colsum(dy * fn))
        dfn = dy * gv
        df_ref[...] = (r * (dfn - fn * jnp.mean(dfn * fn, axis=-1, keepdims=True))).astype(BF16)

    return _pc(body, "postnorm_bwd", (s // ts,), [_row_spec(ts, D)] * 2 + [_vec_spec(D)] * 2,
               [_row_spec(ts, D), _vec_spec(D), _vec_spec(D)],
               [_sds((s, D), BF16), _sds((1, D)), _sds((1, D))])(dout, f, g, gate)


def _prenorm_bwd(dout, dh, x, g, scale):
    s = dout.shape[0]
    ts = min(s, 512)

    def body(do_ref, dh_ref, x_ref, g_ref, sc_ref, dx_ref, dsh_ref, dsc_ref, dg_ref):
        first = pl.program_id(0) == 0
        dhv, xv, gv = dh_ref[...], x_ref[...], g_ref[...]
        r = _rstd(xv)
        xn = xv * r
        _acc(dsh_ref, first, _colsum(dhv))
        _acc(dsc_ref, first, _colsum(dhv * (xn * gv)))
        dhp = dhv * (1.0 + sc_ref[...])
        _acc(dg_ref, first, _colsum(dhp * xn))
        dxn = dhp * gv
        dx_ref[...] = do_ref[...] + r * (dxn - xn * jnp.mean(dxn * xn, axis=-1, keepdims=True))

    return _pc(body, "prenorm_bwd", (s // ts,), [_row_spec(ts, D)] * 3 + [_vec_spec(D)] * 2,
               [_row_spec(ts, D)] + [_vec_spec(D)] * 3,
               [_sds((s, D))] + [_sds((1, D))] * 3)(dout, dh, x, g, scale)


def _loss_fwd_bwd(y, tgt):
    s = y.shape[0]
    ts = min(s, 512)
    nt = s // ts

    def body(y_ref, t_ref, loss_ref, dy_ref, acc_ref):
        t = pl.program_id(0)
        e = y_ref[...] - t_ref[...]
        dy_ref[...] = e * (1.0 / D)
        _acc(acc_ref, t == 0, _colsum(e * e))

        @pl.when(t == nt - 1)
        def _():
            loss_ref[...] = jnp.full((1, LANE), 0.5 / D, F32) * jnp.sum(acc_ref[...])

    return _pc(body, "loss", (nt,), [_row_spec(ts, D)] * 2,
               [pl.BlockSpec((1, LANE), lambda t: (0, 0)), _row_spec(ts, D)],
               [_sds((1, LANE)), _sds((s, D))], scratch=[pltpu.VMEM((1, D), F32)])(y, tgt)


def _sigmoid(v):
    return 1.0 / (1.0 + jnp.exp(-v))


def _ffn_in_swiglu(h, win):
    s = h.shape[0]
    ts = _tile(s)
    nt = (((1,), (1,)), ((), ()))

    def body(h_ref, wa_ref, wb_ref, z_ref, act_ref):
        hv = h_ref[...]
        a = lax.dot_general(hv, wa_ref[...], nt, preferred_element_type=F32)
        b = lax.dot_general(hv, wb_ref[...], nt, preferred_element_type=F32)
        z_ref[0] = a
        z_ref[1] = b
        act_ref[...] = (a * _sigmoid(a) * b).astype(BF16)

    z4, act = _pc(body, "ffn_in", (4, s // ts),
                  [pl.BlockSpec((ts, D), lambda k, t: (t, 0)), pl.BlockSpec((None, FSH, D), lambda k, t: (k, 0, 0)),
                   pl.BlockSpec((None, FSH, D), lambda k, t: (k + 4, 0, 0))],
                  [pl.BlockSpec((2, None, ts, FSH), lambda k, t: (0, k, t, 0)), pl.BlockSpec((None, ts, FSH), lambda k, t: (k, t, 0))],
                  [_sds((2, 4, s, FSH)), _sds((4, s, FSH), BF16)])(h, win, win)
    return z4.reshape(N_DEV, s, FSH), act


def _ffn_out_dx_swiglu(df, wout, z, after):
    s = df.shape[0]
    ts = _tile(s)
    z4 = z.reshape(2, 4, s, FSH)
    nt = (((1,), (1,)), ((), ()))

    def body(df_ref, w_ref, z_ref, after_ref, o_ref):
        d = lax.dot_general(df_ref[...], w_ref[...], nt, preferred_element_type=F32)
        a, b = z_ref[0], z_ref[1]
        sg = _sigmoid(a)
        o_ref[0] = (d * b * (sg * (1.0 + a * (1.0 - sg)))).astype(BF16)
        o_ref[1] = (d * (a * sg)).astype(BF16)

    spec = pl.BlockSpec((2, None, ts, FSH), lambda k, t: (0, k, t, 0))
    out = _pc(body, "ffn_out_dx", (4, s // ts),
              [pl.BlockSpec((ts, D), lambda k, t: (t, 0)), pl.BlockSpec((None, FSH, D), lambda k, t: (k, 0, 0)), spec,
               pl.BlockSpec(memory_space=pl.ANY)],
              spec, _sds((2, 4, s, FSH), BF16))(df, wout, z4, after)
    return out.reshape(N_DEV, s, FSH)


def _ffn_fwd(h, win, wout_of):
    s = h.shape[0]
    ts = _tile(s)
    z, act = _ffn_in_swiglu(h, win)
    wout = wout_of(z).reshape(4, FSH, D)
    f = _mm("ffn_out", act, wout, ((1,), (0,)), (s // ts, 4),
            pl.BlockSpec((None, ts, FSH), lambda t, k: (k, t, 0)), pl.BlockSpec((None, FSH, D), lambda t, k: (k, 0, 0)),
            pl.BlockSpec((ts, D), lambda t, k: (t, 0)), _sds((s, D)), acc_axis=1)
    return f, (h, z, act)


def _ffn_bwd(df, saved, win, wout, send, after):
    h, z, act = saved
    s = h.shape[0]
    ts = _tile(s)
    wout = wout.reshape(4, FSH, D)
    dwout = _mm("ffn_out_dw", act, df, ((0,), (0,)), (4, 2),
                pl.BlockSpec((None, s, FSH), lambda k, j: (k, 0, 0)), pl.BlockSpec((s, D // 2), lambda k, j: (0, j)),
                pl.BlockSpec((None, FSH, D // 2), lambda k, j: (k, 0, j)), _sds((4, FSH, D), BF16), after=after)
    dz = _ffn_out_dx_swiglu(df, wout, z, send("w_out", dwout.reshape(N_DEV, D_FF // N_DEV, D)))
    dwin = _mm("ffn_in_dw", dz, h, ((0,), (0,)), (N_DEV, 2),
               pl.BlockSpec((None, s, FSH), lambda j, i: (j, 0, 0)), pl.BlockSpec((s, D // 2), lambda j, i: (0, i)),
               pl.BlockSpec((None, FSH, D // 2), lambda j, i: (j, 0, i)), _sds((N_DEV, FSH, D), BF16))
    return _mm("ffn_in_dx", dz, win, ((1,), (0,)), (s // ts, N_DEV),
               pl.BlockSpec((None, ts, FSH), lambda t, j: (j, t, 0)), pl.BlockSpec((None, FSH, D), lambda t, j: (j, 0, 0)),
               pl.BlockSpec((ts, D), lambda t, j: (t, 0)), _sds((s, D)), acc_axis=1, after=send("w_in", dwin))


def _shift_rows(v, k, row, s, back):
    if back:
        return jnp.where(row < s - k, pltpu.roll(v, s - k, 0), 0.0)
    return jnp.where(row >= k, pltpu.roll(v, k, 0), 0.0)


def _window_sum(v, w, row, s, back):
    k = 1
    while k < w:
        v = v + _shift_rows(v, k, row, s, back)
        k *= 2
    return v


def _pool_fwd(z, pool_w, pool_scale):
    s = z.shape[0]

    def body(z_ref, w_ref, sc_ref, y_ref, d_ref):
        row = lax.broadcasted_iota(jnp.int32, (s, HD), 0)
        for g, w in enumerate(POOL_WINDOWS):
            sl = slice(g * HD, (g + 1) * HD)
            a = z_ref[:, sl]
            cnt = jnp.minimum(row + 1, w).astype(F32)
            d = (_window_sum(a, w, row, s, False) / cnt - a).astype(BF16)
            d_ref[:, sl] = d
            y = jnp.dot(d, _bf(w_ref[g]), preferred_element_type=F32)
            y_ref[:, sl] = (y * sc_ref[:, sl]).astype(BF16)

    return _pc(body, "pool_fwd", (1,),
               [pl.BlockSpec((s, NH * HD), lambda i: (0, 0)), pl.BlockSpec((NH, HD, HD), lambda i: (0, 0, 0)),
                pl.BlockSpec((1, NH * HD), lambda i: (0, 0))],
               [pl.BlockSpec((s, NH * HD), lambda i: (0, 0))] * 2,
               [_sds((s, NH * HD), BF16)] * 2)(z, pool_w, pool_scale)


def _pool_bwd(dy, d, pool_w, pool_scale):
    s = dy.shape[0]

    def body(dy_ref, d_ref, w_ref, sc_ref, dz_ref, dw_ref, dsc_ref):
        row = lax.broadcasted_iota(jnp.int32, (s, HD), 0)
        for g, w in enumerate(POOL_WINDOWS):
            sl = slice(g * HD, (g + 1) * HD)
            dyg, dg, wg = dy_ref[:, sl], d_ref[:, sl], _bf(w_ref[g])
            yraw = jnp.dot(dg, wg, preferred_element_type=F32)
            dsc_ref[:, sl] = _colsum(dyg * yraw)
            dyr = _bf(dyg * sc_ref[:, sl])
            dw_ref[g] = lax.dot_general(dg, dyr, (((0,), (0,)), ((), ())), preferred_element_type=F32)
            dd = lax.dot_general(dyr, wg, (((1,), (1,)), ((), ())), preferred_element_type=F32)
            cnt = jnp.minimum(row + 1, w).astype(F32)
            dz_ref[:, sl] = (_window_sum(dd / cnt, w, row, s, True) - dd).astype(BF16)

    return _pc(body, "pool_bwd", (1,),
               [pl.BlockSpec((s, NH * HD), lambda i: (0, 0)), pl.BlockSpec((s, NH * HD), lambda i: (0, 0)),
                pl.BlockSpec((NH, HD, HD), lambda i: (0, 0, 0)), pl.BlockSpec((1, NH * HD), lambda i: (0, 0))],
               [pl.BlockSpec((s, NH * HD), lambda i: (0, 0)), pl.BlockSpec((NH, HD, HD), lambda i: (0, 0, 0)),
                pl.BlockSpec((1, NH * HD), lambda i: (0, 0))],
               [_sds((s, NH * HD), BF16), _sds((NH, HD, HD)), _sds((1, NH * HD))])(dy, d, pool_w, pool_scale)


def _gelu(v):
    return 0.5 * v * (1.0 + jnp.tanh(GELU_C * (v + 0.044715 * (v * v * v))))


def _gelu_and_grad(v):
    t = jnp.tanh(GELU_C * (v + 0.044715 * (v * v * v)))
    return 0.5 * v * (1.0 + t), 0.5 * (1.0 + t) + 0.5 * v * (1.0 - t * t) * (GELU_C * (1.0 + 3.0 * 0.044715 * (v * v)))


def _gelu_grad(v):
    return _gelu_and_grad(v)[1]


def _causal_mask():
    return lax.broadcasted_iota(jnp.int32, (HD, HD), 0) >= lax.broadcasted_iota(jnp.int32, (HD, HD), 1)


def _sgu_specs():
    w = NH * HD
    return [pl.BlockSpec((HD, w), lambda c: (c, 1)), pl.BlockSpec((HD, w), lambda c: (c, 2)),
            pl.BlockSpec((1, w), lambda c: (0, 0)), pl.BlockSpec((1, w), lambda c: (0, 0)),
            pl.BlockSpec((NH, HD, HD), lambda c: (0, 0, 0)), pl.BlockSpec((HD, LANE), lambda c: (0, 0))]


def _sgu_head(v, lng_ref, lnb_ref, w_ref, h):
    sl = slice(h * HD, (h + 1) * HD)
    vh = v[:, sl]
    xc = vh - jnp.mean(vh, axis=-1, keepdims=True)
    rs = lax.rsqrt(jnp.mean(xc * xc, axis=-1, keepdims=True) + EPS)
    vhat = xc * rs
    vn = _bf(vhat * lng_ref[:, sl] + lnb_ref[:, sl])
    wc = _bf(jnp.where(_causal_mask(), w_ref[h], 0.0))
    return sl, rs, vhat, vn, wc


def _sgu_fwd(z, ln_g, ln_b, sgu_w, sgu_bt):
    s = z.shape[0]

    def body(zu_ref, zv_ref, lng_ref, lnb_ref, w_ref, bt_ref, y_ref):
        u, v = _gelu(zu_ref[...]), _gelu(zv_ref[...])
        for h in range(NH):
            sl, _, _, vn, wc = _sgu_head(v, lng_ref, lnb_ref, w_ref, h)
            sp = jnp.dot(wc, vn, preferred_element_type=F32) + bt_ref[:, h:h + 1]
            y_ref[:, sl] = (u[:, sl] * sp).astype(BF16)

    return _pc(body, "sgu_fwd", (s // HD,), _sgu_specs(), pl.BlockSpec((HD, NH * HD), lambda c: (c, 0)),
               _sds((s, NH * HD), BF16))(z, z, ln_g, ln_b, sgu_w, sgu_bt)


def _sgu_bwd(z, dy, ln_g, ln_b, sgu_w, sgu_bt, head_sum):
    s = z.shape[0]
    w = NH * HD
    nc = s // HD

    def body(zu_ref, zv_ref, lng_ref, lnb_ref, w_ref, bt_ref, dy_ref, hs_ref,
             dzu_ref, dzv_ref, dlng_ref, dlnb_ref, dw_ref, dbt_ref, dsacc_ref):
        c = pl.program_id(0)
        first = c == 0
        zu, zv = zu_ref[...], zv_ref[...]
        (u, gu), (v, gv) = _gelu_and_grad(zu), _gelu_and_grad(zv)
        dyv = dy_ref[...]
        ds = dyv * u
        _acc(dsacc_ref, first, ds)
        for h in range(NH):
            sl, rs, vhat, vn, wc = _sgu_head(v, lng_ref, lnb_ref, w_ref, h)
            sp = jnp.dot(wc, vn, preferred_element_type=F32) + bt_ref[:, h:h + 1]
            dzu_ref[:, sl] = (dyv[:, sl] * sp * gu[:, sl]).astype(BF16)
            dsh = _bf(ds[:, sl])
            dwh = lax.dot_general(dsh, vn, (((1,), (1,)), ((), ())), preferred_element_type=F32)
            dwh = jnp.where(_causal_mask(), dwh, 0.0)

            @pl.when(first)
            def _():
                dw_ref[h] = dwh

            @pl.when(jnp.logical_not(first))
            def _():
                dw_ref[h] += dwh

            dvn = lax.dot_general(wc, dsh, (((0,), (0,)), ((), ())), preferred_element_type=F32)
            g_col = _colsum(dvn * vhat)
            b_col = _colsum(dvn)

            @pl.when(first)
            def _():
                dlng_ref[:, sl] = g_col
                dlnb_ref[:, sl] = b_col

            @pl.when(jnp.logical_not(first))
            def _():
                dlng_ref[:, sl] += g_col
                dlnb_ref[:, sl] += b_col

            dvh = dvn * lng_ref[:, sl]
            dv = rs * (dvh - jnp.mean(dvh, axis=-1, keepdims=True) - vhat * jnp.mean(dvh * vhat, axis=-1, keepdims=True))
            dzv_ref[:, sl] = (dv * gv[:, sl]).astype(BF16)

        @pl.when(c == nc - 1)
        def _():
            dbt_ref[...] = jnp.dot(dsacc_ref[...], hs_ref[...], preferred_element_type=F32, precision=HIGHEST)

    outs = _pc(body, "sgu_bwd", (nc,),
               _sgu_specs() + [pl.BlockSpec((HD, w), lambda c: (c, 1)), pl.BlockSpec((w, LANE), lambda c: (0, 0))],
               [pl.BlockSpec((HD, w), lambda c: (c, 0))] * 2 + [pl.BlockSpec((1, w), lambda c: (0, 0))] * 2
               + [pl.BlockSpec((NH, HD, HD), lambda c: (0, 0, 0)), pl.BlockSpec((HD, LANE), lambda c: (0, 0))],
               [_sds((s, w), BF16)] * 2 + [_sds((1, w))] * 2 + [_sds((NH, HD, HD)), _sds((HD, LANE))],
               scratch=[pltpu.VMEM((HD, w), F32)])(z, z, ln_g, ln_b, sgu_w, sgu_bt, dy, head_sum)
    return outs


def _cmul(ar, ai, br, bi):
    return ar * br - ai * bi, ar * bi + ai * br


def _ssm_prep(lam_re, lam_im, lam_re_rep, lam_im_rep, log_dt, b_re, b_im):
    def disc(lr, li, dt):
        mag = jnp.exp(lr * dt)
        return mag * jnp.cos(li * dt), mag * jnp.sin(li * dt)

    def body(lr_ref, li_ref, lrr_ref, lir_ref, ldt_ref, br_ref, bi_ref, or_ref, oi_ref, bbr_ref, bbi_ref):
        dt = jnp.exp(ldt_ref[...])
        or_ref[...], oi_ref[...] = disc(lr_ref[...], li_ref[...], dt)
        lr, li = lrr_ref[...], lir_ref[...]
        er, ei = disc(lr, li, dt)
        den = lr * lr + li * li
        kr = ((er - 1.0) * lr + ei * li) / den
        ki = (ei * lr - (er - 1.0) * li) / den
        bbr_ref[...], bbi_ref[...] = _cmul(kr, ki, br_ref[...], bi_ref[...])

    small = pl.BlockSpec((SSM_G, SSM_P), lambda i: (0, 0))
    wide = pl.BlockSpec((SSM_G, SSM_P * SSM_N), lambda i: (0, 0))
    col = pl.BlockSpec((SSM_G, 1), lambda i: (0, 0))
    return _pc(body, "ssm_prep", (1,), [small, small, wide, wide, col, wide, wide], [small, small, wide, wide],
               [_sds((SSM_G, SSM_P))] * 2 + [_sds((SSM_G, SSM_P * SSM_N))] * 2)(
        lam_re, lam_im, lam_re_rep, lam_im_rep, log_dt, b_re, b_im)


def _ssm_param_bwd(g_lam_re, g_lam_im, g_bb_re, g_bb_im, lam_re, lam_im, lam_re_rep, lam_im_rep, log_dt, b_re, b_im, seg):
    def body(glr_ref, gli_ref, gbr_ref, gbi_ref, lr_ref, li_ref, lrr_ref, lir_ref, ldt_ref, br_ref, bi_ref, seg_ref,
             dlr_ref, dli_ref, ddt_ref, dbr_ref, dbi_ref):
        dt = jnp.exp(ldt_ref[...])
        lr, li = lrr_ref[...], lir_ref[...]
        mag = jnp.exp(lr * dt)
        er, ei = mag * jnp.cos(li * dt), mag * jnp.sin(li * dt)
        den = lr * lr + li * li
        kr = ((er - 1.0) * lr + ei * li) / den
        ki = (ei * lr - (er - 1.0) * li) / den
        gbr, gbi = gbr_ref[...], gbi_ref[...]
        dbr_ref[...], dbi_ref[...] = _cmul(kr, -ki, gbr, gbi)
        tr, ti = _cmul(br_ref[...], -bi_ref[...], gbr, gbi)
        gkr = jnp.dot(tr, seg_ref[...], preferred_element_type=F32, precision=HIGHEST)
        gki = jnp.dot(ti, seg_ref[...], preferred_element_type=F32, precision=HIGHEST)
        lr, li = lr_ref[...], li_ref[...]
        mag = jnp.exp(lr * dt)
        er, ei = mag * jnp.cos(li * dt), mag * jnp.sin(li * dt)
        den = lr * lr + li * li
        ir, ii = lr / den, -li / den
        kr, ki = _cmul(er - 1.0, ei, ir, ii)
        ar, ai = _cmul(ir, -ii, gkr, gki)
        glr, gli = glr_ref[...] + ar, gli_ref[...] + ai
        qr, qi = _cmul(kr, ki, ir, ii)
        g1r, g1i = _cmul(-qr, qi, gkr, gki)
        g2r, g2i = _cmul(dt * er, -dt * ei, glr, gli)
        dlr_ref[...] = g1r + g2r
        dli_ref[...] = g1i + g2i
        wr, wi = _cmul(lr, li, er, ei)
        g_dt = jnp.sum(wr * glr + wi * gli, axis=-1, keepdims=True)
        ddt_ref[...] = jnp.broadcast_to(dt * g_dt, (SSM_G, LANE))

    small = pl.BlockSpec((SSM_G, SSM_P), lambda i: (0, 0))
    wide = pl.BlockSpec((SSM_G, SSM_P * SSM_N), lambda i: (0, 0))
    col = pl.BlockSpec((SSM_G, 1), lambda i: (0, 0))
    segs = pl.BlockSpec((SSM_P * SSM_N, SSM_P), lambda i: (0, 0))
    return _pc(body, "ssm_param_bwd", (1,), [small, small, wide, wide, small, small, wide, wide, col, wide, wide, segs],
               [small, small, pl.BlockSpec((SSM_G, LANE), lambda i: (0, 0)), wide, wide],
               [_sds((SSM_G, SSM_P))] * 2 + [_sds((SSM_G, LANE))] + [_sds((SSM_G, SSM_P * SSM_N))] * 2)(
        g_lam_re, g_lam_im, g_bb_re, g_bb_im, lam_re, lam_im, lam_re_rep, lam_im_rep, log_dt, b_re, b_im, seg)


SCAN_LANES = 512
SCAN_ROWS = 8


SCAN_GROUPS = SCAN_LANES // SSM_P
SCAN_COLS = SCAN_GROUPS * SSM_N
SCAN_CHUNK = 256


def _ssm_scan(name, v, w_in, lam_re, lam_im, w_out, reverse, states=None):
    s = v.shape[0]
    ln, rows, ch = SCAN_LANES, SCAN_ROWS, min(SCAN_CHUNK, s)
    nch, ntile = s // ch, ch // rows
    nt_dims = (((1,), (1,)), ((), ()))
    with_sum = states is not None

    def body(*refs):
        v_ref, win_ref, lr_ref, li_ref, wout_ref = refs[:5]
        n_in = 7 if with_sum else 5
        or_ref, oi_ref, y_ref = refs[n_in:n_in + 3]
        br_s, bi_s = refs[-2:]
        l1 = (lr_ref[...], li_ref[...])
        pw = [l1]
        for _ in range(rows - 1):
            pw.append(_cmul(*pw[-1], *l1))
        row = lax.broadcasted_iota(jnp.int32, (rows, ln), 0)
        expo = (rows - row) if reverse else (row + 1)
        pr = jnp.zeros((rows, ln), F32)
        pi = jnp.zeros((rows, ln), F32)
        for e in range(1, rows + 1):
            pr = jnp.where(expo == e, pw[e - 1][0], pr)
            pi = jnp.where(expo == e, pw[e - 1][1], pi)
        lk = {}
        for k in (1, 2, 4):
            keep = (row < rows - k) if reverse else (row >= k)
            lk[k] = (jnp.where(keep, pw[k - 1][0], 0.0), jnp.where(keep, pw[k - 1][1], 0.0))

        def chunk(c, carry):
            q0 = pl.multiple_of(((nch - 1 - c) if reverse else c) * ch, ch)
            b = jnp.dot(_bf(v_ref[pl.ds(q0, ch), :]), win_ref[...], preferred_element_type=F32)
            br_s[...] = b[:, :ln]
            bi_s[...] = b[:, ln:]

            def step(i, carry):
                cr, ci = carry[:2]
                r0 = pl.multiple_of(((ntile - 1 - i) if reverse else i) * rows, rows)
                xr, xi = br_s[pl.ds(r0, rows), :], bi_s[pl.ds(r0, rows), :]
                for k in (1, 2, 4):
                    shift = rows - k if reverse else k
                    ar, ai = _cmul(lk[k][0], lk[k][1], pltpu.roll(xr, shift, 0), pltpu.roll(xi, shift, 0))
                    xr, xi = xr + ar, xi + ai
                ar, ai = _cmul(pr, pi, cr, ci)
                xr, xi = xr + ar, xi + ai
                g0 = pl.multiple_of(q0 + r0, rows)
                or_ref[pl.ds(g0, rows), :] = xr
                oi_ref[pl.ds(g0, rows), :] = xi
                if not with_sum:
                    return (xr[rows - 1:rows], xi[rows - 1:rows]) if not reverse else (xr[0:1], xi[0:1])
                nr = jnp.where(row == rows - 1, cr, pltpu.roll(xr, rows - 1, 0))
                ni = jnp.where(row == rows - 1, ci, pltpu.roll(xi, rows - 1, 0))
                sr, si = refs[5][pl.ds(g0, rows), :], refs[6][pl.ds(g0, rows), :]
                return xr[0:1], xi[0:1], carry[2] + (sr * nr + si * ni), carry[3] + (sr * ni - si * nr)

            carry = lax.fori_loop(0, ntile, step, carry)
            w = wout_ref[...]
            y_ref[pl.ds(q0, ch), :] = (
                lax.dot_general(_bf(or_ref[pl.ds(q0, ch), :]), w[:, :ln], nt_dims, preferred_element_type=F32)
                + lax.dot_general(_bf(oi_ref[pl.ds(q0, ch), :]), w[:, ln:], nt_dims, preferred_element_type=F32))
            return carry

        zero = jnp.zeros((1, ln), F32)
        init = (zero, zero) + ((jnp.zeros((rows, ln), F32),) * 2 if with_sum else ())
        carry = lax.fori_loop(0, nch, chunk, init)
        if with_sum:
            refs[n_in + 3][...] = _colsum(carry[2])
            refs[n_in + 4][...] = _colsum(carry[3])

    vec = pl.BlockSpec((1, ln), lambda j: (0, j))
    blk = pl.BlockSpec((s, ln), lambda j: (0, j))
    cols = pl.BlockSpec((s, SCAN_COLS), lambda j: (0, j))
    wspec = pl.BlockSpec((None, SCAN_COLS, 2 * ln), lambda j: (j, 0, 0))
    ins, args = [cols, wspec, vec, vec, wspec], [v, w_in, lam_re, lam_im, w_out]
    outs, shapes = [blk, blk, cols], [_sds((s, SSM_L))] * 2 + [_sds((s, SSM_G * SSM_N))]
    if with_sum:
        ins, args = ins + [blk, blk], args + list(states)
        outs, shapes = outs + [vec, vec], shapes + [_sds((1, SSM_L))] * 2
    return _pc(body, name, (SSM_L // ln,), ins, outs, shapes, scratch=[pltpu.VMEM((ch, ln), F32)] * 2)(*args)


def _ssm_outer(name, v, x_re, x_im):
    s = v.shape[0]
    ts = min(s, 512)
    nt = s // ts
    half = SSM_GB * SSM_P
    rows = SSM_GB * SSM_N
    tn = (((0,), (0,)), ((), ()))

    def body(v_ref, xr_ref, xi_ref, or_ref, oi_ref, acc_ref):
        vv = _bf(v_ref[...])
        pr = lax.dot_general(vv, _bf(xr_ref[...]), tn, preferred_element_type=F32)
        pi = lax.dot_general(vv, _bf(xi_ref[...]), tn, preferred_element_type=F32)
        t = pl.program_id(1)

        @pl.when(t == 0)
        def _():
            acc_ref[:, :half] = pr
            acc_ref[:, half:] = pi

        @pl.when(t > 0)
        def _():
            acc_ref[:, :half] += pr
            acc_ref[:, half:] += pi

        @pl.when(t == nt - 1)
        def _():
            row_g = lax.broadcasted_iota(jnp.int32, (rows, LANE), 0) // SSM_N
            lane_g = lax.broadcasted_iota(jnp.int32, (rows, LANE), 1) // SSM_P
            for part, o_ref in enumerate((or_ref, oi_ref)):
                fold = jnp.zeros((rows, LANE), F32)
                for cb in range(half // LANE):
                    blk = acc_ref[:, part * half + cb * LANE:part * half + (cb + 1) * LANE]
                    fold = fold + jnp.where(2 * cb + lane_g == row_g, blk, 0.0)
                o_ref[...] = jnp.where(row_g % 2 == 0, fold, pltpu.roll(fold, SSM_P, 1))

    xin = pl.BlockSpec((ts, half), lambda q, t: (t, q))
    out = pl.BlockSpec((None, rows, LANE), lambda q, t: (q, 0, 0))
    return _pc(body, name, (SSM_NB, nt), [pl.BlockSpec((ts, rows), lambda q, t: (t, q)), xin, xin], [out, out],
               [_sds((SSM_NB, rows, LANE))] * 2, scratch=[pltpu.VMEM((rows, 2 * half), F32)])(v, x_re, x_im)


def _ssm_act_fwd(y, u, d_skip):
    s = y.shape[0]
    ts = min(s, 512)

    def body(y_ref, u_ref, d_ref, o_ref):
        o_ref[...] = _gelu(y_ref[...] + d_ref[...] * u_ref[...]).astype(BF16)

    return _pc(body, "ssm_act_fwd", (s // ts,), [_row_spec(ts, D)] * 2 + [_vec_spec(D)], _row_spec(ts, D),
               _sds((s, D), BF16))(y, u, d_skip)


def _ssm_act_bwd(dg, y, u, d_skip):
    s = y.shape[0]
    ts = min(s, 512)

    def body(dg_ref, y_ref, u_ref, d_ref, dy_ref, dd_ref):
        uv = u_ref[...]
        dy = dg_ref[...] * _gelu_grad(y_ref[...] + d_ref[...] * uv)
        dy_ref[...] = dy.astype(BF16)
        _acc(dd_ref, pl.program_id(0) == 0, _colsum(dy * uv))

    return _pc(body, "ssm_act_bwd", (s // ts,), [_row_spec(ts, D)] * 3 + [_vec_spec(D)], [_row_spec(ts, D), _vec_spec(D)],
               [_sds((s, D), BF16), _sds((1, D))])(dg, y, u, d_skip)


def _axpy(a, b, d_skip):
    s = a.shape[0]
    ts = min(s, 512)

    def body(a_ref, b_ref, d_ref, o_ref):
        o_ref[...] = (a_ref[...] + d_ref[...] * b_ref[...].astype(F32)).astype(BF16)

    return _pc(body, "ssm_du", (s // ts,), [_row_spec(ts, D)] * 2 + [_vec_spec(D)], _row_spec(ts, D),
               _sds((s, D), BF16))(a, b, d_skip)


def _glu_fwd(zz):
    s = zz.shape[0]
    ts = min(s, 512)

    def body(a_ref, b_ref, o_ref):
        o_ref[...] = a_ref[...] * _sigmoid(b_ref[...])

    return _pc(body, "glu_fwd", (s // ts,), [_row_spec(ts, D, 0), _row_spec(ts, D, 1)], _row_spec(ts, D), _sds((s, D)))(zz, zz)


def _glu_bwd(zz, df):
    s = zz.shape[0]
    ts = min(s, 512)

    def body(a_ref, b_ref, df_ref, o_ref):
        sg = _sigmoid(b_ref[...])
        dfv = df_ref[...].astype(F32)
        o_ref[:, :D] = (dfv * sg).astype(BF16)
        o_ref[:, D:] = (dfv * a_ref[...] * sg * (1.0 - sg)).astype(BF16)

    return _pc(body, "glu_bwd", (s // ts,), [_row_spec(ts, D, 0), _row_spec(ts, D, 1), _row_spec(ts, D)],
               _row_spec(ts, 2 * D), _sds((s, 2 * D), BF16))(zz, zz, df)


def _ssm_block_diag(m_re, m_im):
    rows, half = SCAN_COLS, SCAN_LANES
    expand = jnp.tile(jnp.eye(SSM_P, dtype=BF16), (1, SCAN_GROUPS))

    def body(mr_ref, mi_ref, e_ref, o_ref):
        keep = (lax.broadcasted_iota(jnp.int32, (rows, half), 0) // SSM_N
                == lax.broadcasted_iota(jnp.int32, (rows, half), 1) // SSM_P)
        for part, m_ref in enumerate((mr_ref, mi_ref)):
            t = jnp.dot(_bf(m_ref[...]), e_ref[...], preferred_element_type=F32)
            o_ref[:, part * half:(part + 1) * half] = jnp.where(keep, t, 0.0).astype(BF16)

    blk = pl.BlockSpec((rows, SSM_P), lambda q: (q, 0))
    nb = SSM_G // SCAN_GROUPS
    return _pc(body, "ssm_block_diag", (nb,), [blk, blk, pl.BlockSpec((SSM_P, half), lambda q: (0, 0))],
               pl.BlockSpec((None, rows, 2 * half), lambda q: (q, 0, 0)), _sds((nb, rows, 2 * half), BF16))(m_re, m_im, expand)


def _mod_part(c_all, ada_w):
    n = ada_w.shape[-1]

    def body(c_ref, w_ref, o_ref):
        cv = c_ref[...]
        cond = _bf(cv * _sigmoid(cv))
        o_ref[...] = jnp.dot(cond, _bf(w_ref[...]), preferred_element_type=F32)

    return _pc(body, "mod_part", (2,), [pl.BlockSpec((N_DEV, D), lambda l: (0, 0)), pl.BlockSpec((None, D, n), lambda l: (l, 0, 0))],
               pl.BlockSpec((None, N_DEV, n), lambda l: (l, 0, 0)), _sds((2, N_DEV, n)))(c_all, ada_w)


def _ada_w_grad(c_all_t, dmod):
    n = dmod.shape[-1]
    tr = 128

    def body(c_ref, d_ref, o_ref):
        cv = c_ref[...]
        cond = _bf(cv * _sigmoid(cv)).astype(F32)
        dm = _bf(d_ref[...]).astype(F32)
        acc = cond[:, 0:1] * dm[0:1, :]
        for b in range(1, N_DEV):
            acc = acc + cond[:, b:b + 1] * dm[b:b + 1, :]
        o_ref[...] = acc

    return _pc(body, "ada_w_grad", (2, D // tr),
               [pl.BlockSpec((tr, N_DEV), lambda l, t: (t, 0)), pl.BlockSpec((None, N_DEV, n), lambda l, t: (l, 0, 0))],
               pl.BlockSpec((None, tr, n), lambda l, t: (l, t, 0)), _sds((2, D, n)))(c_all_t, dmod)


def _adamw(name, parts, w, m, v, slot=0, prev=None, after=None):
    p, r, c = parts.shape
    tr = r
    while tr * c * 4 > (1 << 20) and tr % 16 == 0:
        tr //= 2
    nt = r // tr

    def body(p_ref, w_ref, m_ref, v_ref, *rest):
        g_ref, d_ref, nm_ref, nv_ref = rest[-4:]
        g = p_ref[0].astype(F32)
        for i in range(1, p):
            g = g + p_ref[i].astype(F32)
        g_ref[...] = g
        d_ref[...], nm_ref[...], nv_ref[...] = _adam_update(g, w_ref[...], m_ref[...], v_ref[...])

    blk = pl.BlockSpec((tr, c), lambda t: (slot * nt + t, 0))
    in_specs = [pl.BlockSpec((p, tr, c), lambda t: (0, t, 0)), blk, blk, blk]
    unread = list(prev or []) + ([after] if after is not None else [])
    return pl.pallas_call(
        body, name=name, grid=(nt,), in_specs=in_specs + [pl.BlockSpec(memory_space=pl.ANY)] * len(unread), out_specs=[blk] * 4,
        out_shape=[_sds(w.shape)] * 4, input_output_aliases={4 + i: i for i in range(4)} if prev else {},
        compiler_params=pltpu.CompilerParams(dimension_semantics=("arbitrary",), vmem_limit_bytes=VMEM_LIMIT_BYTES))(parts, w, m, v, *unread)


def _adam_update(g, w, m, v):
    m2 = B1 * m + (1.0 - B1) * g
    v2 = B2 * v + (1.0 - B2) * (g * g)
    m_hat = m2 / (1.0 - B1 ** STEP)
    v_hat = v2 / (1.0 - B2 ** STEP)
    return -LR * (m_hat / (jnp.sqrt(v_hat) + ADAM_EPS) + WD * w), m2, v2


def _adamw_many(name, items, after):
    n = len(items)

    def body(*refs):
        outs = refs[4 * n + 1:]
        for i in range(n):
            g, w, m, v = (r[...] for r in refs[4 * i:4 * i + 4])
            for o, val in zip(outs[3 * i:3 * i + 3], _adam_update(g, w, m, v)):
                o[...] = val

    full = lambda a: pl.BlockSpec(a.shape, lambda t: (0, 0))
    flat = [a for item in items for a in item]
    res = _pc(body, name, (1,), [full(a) for a in flat] + [pl.BlockSpec(memory_space=pl.ANY)],
              [full(item[1]) for item in items for _ in range(3)],
              [_sds(item[1].shape) for item in items for _ in range(3)])(*flat, after)
    return [tuple(res[3 * i:3 * i + 3]) for i in range(n)]


def _sum_parts(parts):
    p, r, c = parts.shape
    tr = r
    while tr * c * 4 > (1 << 19) and tr % 16 == 0:
        tr //= 2

    def body(p_ref, o_ref):
        g = p_ref[0]
        for i in range(1, p):
            g = g + p_ref[i]
        o_ref[...] = g

    return _pc(body, "sum_parts", (r // tr,), [pl.BlockSpec((p, tr, c), lambda t: (0, t, 0))], pl.BlockSpec((tr, c), lambda t: (t, 0)),
               _sds((r, c)))(parts)


def _place():
    x, y, c = lax.axis_index("x"), lax.axis_index("y"), lax.axis_index("c")
    peers = []
    for k in range(1, N_DEV):
        px = (1 - x) if k & 4 else x
        py = (1 - y) if k & 2 else y
        pc = (1 - c) if k & 1 else c
        peers.append(((px, py, pc), 4 * px + 2 * py + pc))
    return 4 * x + 2 * y + c, peers


def _at(ref, idx):
    return ref if idx is None else ref.at[idx]


def _exchange_copies(plan, n, src_refs, dst_refs, send_sems, recv_sems, local_sems=None, with_arrivals=True):
    me, peers = _place()
    local = [] if local_sems is None else [
        pltpu.make_async_copy(_at(src_refs[si], sx), _at(dst_refs[di], dx), local_sems.at[i])
        for i, (si, sx, di, dx) in enumerate(plan(me, me, 0))]

    def remote(k, i, dev, entry):
        si, sx, di, dx = entry
        return pltpu.make_async_remote_copy(_at(src_refs[si], sx), _at(dst_refs[di], dx), send_sems.at[k * n + i], recv_sems.at[k * n + i],
                                            device_id=dev, device_id_type=MESH)

    sends = [remote(k, i, dev, e) for k, (dev, peer) in enumerate(peers) for i, e in enumerate(plan(me, peer, k + 1))]
    if not with_arrivals:
        return local, sends, []
    arrivals = [remote(k, i, dev, e) for k, (dev, peer) in enumerate(peers) for i, e in enumerate(plan(peer, me, k + 1))]
    return local, sends, arrivals


def _sem_shapes(n_copies, local=True):
    sems = [pltpu.SemaphoreType.DMA(((N_DEV - 1) * n_copies,)), pltpu.SemaphoreType.DMA(((N_DEV - 1) * n_copies,))]
    return sems + [pltpu.SemaphoreType.DMA((n_copies,))] if local else sems


def _exchange(name, srcs, dst_shapes, plan, n_copies):
    ns, nd = len(srcs), len(dst_shapes)

    def body(*refs):
        local, sends, arrivals = _exchange_copies(plan, n_copies, refs[:ns], refs[ns:ns + nd], *refs[ns + nd:])
        for cp in local + sends:
            cp.start()
        for cp in arrivals:
            cp.wait_recv()
        for cp in sends:
            cp.wait_send()
        for cp in local:
            cp.wait()

    any_spec = pl.BlockSpec(memory_space=pl.ANY)
    return pl.pallas_call(
        body, name=name, in_specs=[any_spec] * ns, out_specs=[any_spec] * nd, out_shape=list(dst_shapes),
        scratch_shapes=_sem_shapes(n_copies))(*srcs)


HBM_SPEC = pl.BlockSpec(memory_space=pltpu.HBM)
SEM_SPEC = pl.BlockSpec(memory_space=pltpu.SEMAPHORE)
ANY_SPEC = pl.BlockSpec(memory_space=pl.ANY)
TOKEN_SPEC = pl.BlockSpec(memory_space=pltpu.VMEM)
SIDE_EFFECT = pltpu.SideEffectType.DATAFLOW_SIDE_EFFECTING


def _wait_all(local, sends, arrivals):
    for cp in arrivals:
        cp.wait_recv()
    for cp in sends:
        cp.wait_send()
    for cp in local:
        cp.wait()


def _exchange_start(name, srcs, dst_shapes, plan, n_copies, order):
    ns, nd = len(srcs), len(dst_shapes)
    nb = ns + nd

    def body(*refs):
        local, sends, _ = _exchange_copies(plan, n_copies, refs[:ns], refs[ns:nb], *refs[nb + 1:nb + 4], with_arrivals=False)
        for cp in local + sends:
            cp.start()
        refs[-1][...] = jnp.zeros((8, LANE), F32)

    lands = [pltpu.with_memory_space_constraint(lax.empty(d.shape, d.dtype), pltpu.HBM) for d in dst_shapes]
    srcs = [pltpu.with_memory_space_constraint(a, pltpu.HBM) for a in srcs]
    bufs = srcs + lands
    out = pl.pallas_call(
        body, name=name, in_specs=[HBM_SPEC] * nb + [ANY_SPEC],
        out_specs=[SEM_SPEC] * 3 + [HBM_SPEC] * nb + [TOKEN_SPEC],
        out_shape=_sem_shapes(n_copies) + [pltpu.HBM(a.shape, a.dtype) for a in bufs] + [_sds((8, LANE))],
        input_output_aliases={i: 3 + i for i in range(nb)},
        compiler_params=pltpu.CompilerParams(has_side_effects=SIDE_EFFECT))(*bufs, order)
    return out[:3], out[3:3 + ns], out[3 + ns:3 + nb], out[-1]


def _exchange_relay(name, sems, srcs, lands, plan, n_copies, plan2, n_copies2, after):
    ns, nd = len(srcs), len(lands)
    nb = ns + nd

    def body(*refs):
        land_refs = refs[ns:nb]
        _wait_all(*_exchange_copies(plan, n_copies, refs[:ns], land_refs, *refs[nb:nb + 3]))
        _, sends, _ = _exchange_copies(plan2, n_copies2, land_refs, land_refs, *refs[nb + 4:nb + 6], with_arrivals=False)
        for cp in sends:
            cp.start()
        refs[-1][...] = jnp.zeros((8, LANE), F32)

    out = pl.pallas_call(
        body, name=name, in_specs=[HBM_SPEC] * nb + [SEM_SPEC] * 3 + [ANY_SPEC],
        out_specs=[SEM_SPEC] * 2 + [HBM_SPEC] * nd + [TOKEN_SPEC],
        out_shape=_sem_shapes(n_copies2, local=False) + [pltpu.HBM(a.shape, a.dtype) for a in lands] + [_sds((8, LANE))],
        input_output_aliases={ns + i: 2 + i for i in range(nd)},
        compiler_params=pltpu.CompilerParams(has_side_effects=SIDE_EFFECT))(*srcs, *lands, *sems, after)
    return out[:2], out[2:2 + nd], out[-1]


def _exchange_wait(name, sems, srcs, lands, plan, n_copies, after):
    srcs = [] if srcs is None else list(srcs)
    ns, nd = len(srcs), len(lands)
    nb = ns + nd

    def body(*refs):
        land_refs = refs[ns:nb]
        _wait_all(*_exchange_copies(plan, n_copies, refs[:ns] if ns else land_refs, land_refs, *refs[nb:nb + len(sems)]))

    bufs = srcs + list(lands)
    out = pl.pallas_call(
        body, name=name, in_specs=[HBM_SPEC] * nb + [SEM_SPEC] * len(sems) + [ANY_SPEC],
        out_specs=[HBM_SPEC] * nb, out_shape=[pltpu.HBM(a.shape, a.dtype) for a in bufs],
        input_output_aliases={i: i for i in range(nb)},
        compiler_params=pltpu.CompilerParams(has_side_effects=SIDE_EFFECT))(*bufs, *sems, after)
    return out[ns:]


def _all_gather(name, arrs):
    plan = lambda me, peer, k: [(i, None, i, me) for i in range(len(arrs))]
    return _exchange(name, arrs, [_sds((N_DEV,) + a.shape, a.dtype) for a in arrs], plan, len(arrs))


def _post_pre_fwd(x, f, g_post, gate, rw, g_pre, scale, shift):
    s = x.shape[0]
    ts = min(s, 256)

    def body(x_ref, f_ref, gp_ref, gt_ref, g_ref, sc_ref, sh_ref, xo_ref, h_ref):
        fv = f_ref[...]
        xv = x_ref[...] + (rw * gt_ref[...]) * (fv * _rstd(fv) * gp_ref[...])
        xo_ref[...] = xv
        h_ref[...] = ((xv * _rstd(xv) * g_ref[...]) * (1.0 + sc_ref[...]) + sh_ref[...]).astype(BF16)

    return _pc(body, "post_pre_fwd", (s // ts,), [_row_spec(ts, D)] * 2 + [_vec_spec(D)] * 5, [_row_spec(ts, D)] * 2,
               [_sds((s, D)), _sds((s, D), BF16)])(x, f, g_post, gate, g_pre, scale, shift)


def _pre_post_bwd(dout, dh, x, g_pre, scale, f, g_post, gate, rw):
    s = dout.shape[0]
    ts = min(s, 256)

    def body(do_ref, dh_ref, x_ref, g_ref, sc_ref, f_ref, gp_ref, gt_ref,
             dx_ref, dsh_ref, dsc_ref, dg_ref, df_ref, dgate_ref, dgp_ref):
        first = pl.program_id(0) == 0
        dhv, xv, gv = dh_ref[...], x_ref[...], g_ref[...]
        r = _rstd(xv)
        xn = xv * r
        _acc(dsh_ref, first, _colsum(dhv))
        _acc(dsc_ref, first, _colsum(dhv * (xn * gv)))
        dhp = dhv * (1.0 + sc_ref[...])
        _acc(dg_ref, first, _colsum(dhp * xn))
        dxn = dhp * gv
        dx = do_ref[...] + r * (dxn - xn * jnp.mean(dxn * xn, axis=-1, keepdims=True))
        dx_ref[...] = dx
        fv, gpv = f_ref[...], gp_ref[...]
        rf = _rstd(fv)
        fn = fv * rf
        _acc(dgate_ref, first, rw * _colsum(dx * (fn * gpv)))
        dy = (rw * gt_ref[...]) * dx
        _acc(dgp_ref, first, _colsum(dy * fn))
        dfn = dy * gpv
        df_ref[...] = (rf * (dfn - fn * jnp.mean(dfn * fn, axis=-1, keepdims=True))).astype(BF16)

    rows, vec = _row_spec(ts, D), _vec_spec(D)
    return _pc(body, "pre_post_bwd", (s // ts,), [rows] * 3 + [vec] * 2 + [rows] + [vec] * 2,
               [rows, vec, vec, vec, rows, vec, vec],
               [_sds((s, D))] + [_sds((1, D))] * 3 + [_sds((s, D), BF16)] + [_sds((1, D))] * 2)(
        dout, dh, x, g_pre, scale, f, g_post, gate)


def _mix0_fwd(h, p):
    z = _mm_nt("mix0_in", h, p["ab_w_in"])
    y_a, d = _pool_fwd(z, p["pool_w"], p["pool_scale"])
    y_b = _sgu_fwd(z, p["sgu_ln_g"], p["sgu_ln_b"], p["sgu_w"], p["sgu_bt"])
    ycat = jnp.concatenate([y_a, y_b], axis=1)
    return _mm_nn("mix0_out", ycat, p["ab_w_out"]), (h, z, d, ycat)


def _mix0_bwd(df, saved, p, after):
    h, z, d, ycat = saved
    dycat = _mm_nt("mix0_out_dx", df, p["ab_w_out"], after=after)
    g = {"ab_w_out": _mm_tn("mix0_out_dw", ycat, df, BF16)}
    dz_p, g["pool_w"], g["pool_scale"] = _pool_bwd(dycat, d, p["pool_w"], p["pool_scale"])
    dz_u, dz_v, g["sgu_ln_g"], g["sgu_ln_b"], g["sgu_w"], dbt = _sgu_bwd(
        z, dycat, p["sgu_ln_g"], p["sgu_ln_b"], p["sgu_w"], p["sgu_bt"], p["head_sum"])
    g["sgu_b"] = dbt[:, :NH].T
    dz = jnp.concatenate([dz_p, dz_u, dz_v], axis=1)
    g["ab_w_in"] = _mm_tn("mix0_in_dw", dz, h, BF16)
    return _mm_nn("mix0_in_dx", dz, p["ab_w_in"]), g


def _mix1_fwd(h, p):
    u = _mm_nn("ssm_w_in", h, p["ssm_w_in"])
    x_re, x_im, y = _ssm_scan("ssm_scan_fwd", u, p["wb_bd"], p["lam_bar_re"], p["lam_bar_im"], p["wc_bd"], False)
    g = _ssm_act_fwd(y, u, p["ssm_d"])
    zz = _mm_nn("ssm_glu", g, p["ssm_w_glu"])
    return _glu_fwd(zz), (h, u, x_re, x_im, y, g, zz)


def _mix1_bwd(df, saved, p, after):
    h, u, x_re, x_im, y, g, zz = saved
    gr = {}
    dzz = _glu_bwd(zz, df)
    dg = _mm_nt("ssm_glu_dx", dzz, p["ssm_w_glu"], after=after)
    gr["ssm_w_glu"] = _mm_tn("ssm_glu_dw", g, dzz, BF16)
    dy, gr["ssm_d"] = _ssm_act_bwd(dg, y, u, p["ssm_d"])
    a_re, a_im, du_ssm, g_lam_re, g_lam_im = _ssm_scan(
        "ssm_scan_bwd", dy, p["wc_bd"], p["lam_bar_re"], -p["lam_bar_im"], p["wb_bd"], True, states=(x_re, x_im))
    du = _axpy(du_ssm, dy, p["ssm_d"])
    gr["ssm_w_in"] = _mm_tn("ssm_w_in_dw", h, du, BF16)
    dh = _mm_nt("ssm_w_in_dx", du, p["ssm_w_in"])
    mb_re, mb_im = _ssm_outer("ssm_db", u, a_re, a_im)
    mc_re, mc_im = _ssm_outer("ssm_dc", dy, x_re, x_im)
    per_group = lambda m: m[:, :, :SSM_P].reshape(SSM_G, SSM_N, SSM_P)
    gr["ssm_c_re"] = per_group(mc_re)
    gr["ssm_c_im"] = -per_group(mc_im)
    dlr, dli, ddt, dbr, dbi = _ssm_param_bwd(
        g_lam_re.reshape(SSM_G, SSM_P), g_lam_im.reshape(SSM_G, SSM_P),
        per_group(mb_re).reshape(SSM_G, SSM_N * SSM_P), per_group(mb_im).reshape(SSM_G, SSM_N * SSM_P),
        p["lam_re"], p["lam_im"], p["lam_re_rep"], p["lam_im_rep"], p["log_dt"], p["b_re"], p["b_im"], p["seg"])
    gr["ssm_lam_re"], gr["ssm_lam_im"], gr["ssm_log_dt"] = dlr, dli, ddt[:, 0]
    gr["ssm_b_re"] = dbr.reshape(SSM_G, SSM_N, SSM_P)
    gr["ssm_b_im"] = dbi.reshape(SSM_G, SSM_N, SSM_P)
    return dh, gr


def _ssm_params(lam_re, lam_im, b_re, b_im, c_re, c_im, log_dt):
    wide = lambda b: b.transpose(0, 2, 1).reshape(SSM_G, SSM_N * SSM_P)
    p = {"lam_re": lam_re, "lam_im": lam_im, "log_dt": log_dt.reshape(SSM_G, 1),
         "lam_re_rep": jnp.tile(lam_re, (1, SSM_N)), "lam_im_rep": jnp.tile(lam_im, (1, SSM_N)), "b_re": wide(b_re), "b_im": wide(b_im)}
    lbr, lbi, bbr, bbi = _ssm_prep(lam_re, lam_im, p["lam_re_rep"], p["lam_im_rep"], p["log_dt"], p["b_re"], p["b_im"])
    p["lam_bar_re"], p["lam_bar_im"] = lbr.reshape(1, SSM_L), lbi.reshape(1, SSM_L)
    rows = lambda m: m.reshape(SSM_G * SSM_N, SSM_P)
    p["wb_bd"] = _ssm_block_diag(rows(bbr), rows(bbi))
    p["wc_bd"] = _ssm_block_diag(rows(c_re), rows(-c_im))
    p["seg"] = jnp.tile(jnp.eye(SSM_P, dtype=F32), (SSM_N, 1))
    return p


RES_WEIGHT = (0.5, 1.0, 0.5)


def _local_step(x, tgt, mod, norm_pre, norm_post, weights_of, on_part, on_grads):
    def fns(i, w):
        if i % 3 != 1:
            win, wout_of = w
            return ((lambda h: _ffn_fwd(h, win, wout_of)),
                    (lambda df, sv, after: (_ffn_bwd(df, sv, win, wout_of(None), lambda tag, part: on_part(i, tag, part), after), None)))
        if i == 1:
            return (lambda h: _mix0_fwd(h, w)), (lambda df, sv, after: _mix0_bwd(df, sv, w, after))
        return (lambda h: _mix1_fwd(h, w)), (lambda df, sv, after: _mix1_bwd(df, sv, w, after))

    g_pre = [norm_pre[l, s][None] for l in range(2) for s in range(3)]
    g_post = [norm_post[l, s][None] for l in range(2) for s in range(3)]
    rw = RES_WEIGHT * 2
    mods, saved, bwd = [], [], []
    f = None
    for i in range(6):
        w, token = weights_of(i, x if i == 0 else f)
        fwd, b = fns(i, w)
        m3 = mod[i // 3, i % 3] + token[0:1, 0:1]
        if i == 0:
            h = _prenorm_fwd(x, g_pre[0], m3[1:2], m3[0:1])
        else:
            x, h = _post_pre_fwd(x, f, g_post[i - 1], mods[i - 1][2:3], rw[i - 1], g_pre[i], m3[1:2], m3[0:1])
        f, inner = fwd(h)
        mods.append(m3)
        saved.append((x, f, inner))
        bwd.append(b)
    loss_row, dx = _loss_fwd_bwd(_postnorm_fwd(x, f, g_post[5], mods[5][2:3], rw[5]), tgt)
    df, dgate, dg_post = _postnorm_bwd(dx, f, g_post[5], mods[5][2:3], rw[5])
    token = jnp.zeros((8, LANE), F32)
    for i in reversed(range(6)):
        x_i, _, inner = saved[i]
        dh, extra = bwd[i](df, inner, token)
        if i > 0:
            dx, dshift, dscale, dg_pre, df, dgate_prev, dg_post_prev = _pre_post_bwd(
                dx, dh, x_i, g_pre[i], mods[i][1:2], saved[i - 1][1], g_post[i - 1], mods[i - 1][2:3], rw[i - 1])
        else:
            dx, dshift, dscale, dg_pre = _prenorm_bwd(dx, dh, x_i, g_pre[0], mods[0][1:2])
        token = on_grads(i, extra, jnp.concatenate([dshift, dscale, dgate], axis=0), dg_pre, dg_post, loss_row)
        if i > 0:
            dgate, dg_post = dgate_prev, dg_post_prev
    return dx


def _pad_rows(v, rows):
    return jnp.pad(v, (0, rows * LANE - v.shape[0])).reshape(rows, LANE)


def _pack(parts):
    flat, layout, off = [], [], 0
    for a in parts:
        n = a.size
        padded = -(-n // LANE) * LANE
        flat.append(jnp.pad(a.reshape(-1).astype(F32), (0, padded - n)))
        layout.append((off, n, a.shape))
        off += padded
    return jnp.concatenate(flat), layout


def _unpack(flat, layout):
    return [flat[off:off + n].reshape(shape) for off, n, shape in layout]


SMALL_REPLICATED = ["ada_b", "pool_w", "pool_scale", "sgu_ln_g", "sgu_ln_b", "sgu_w", "sgu_b", "ssm_lam_re", "ssm_lam_im",
                    "ssm_b_re", "ssm_b_im", "ssm_c_re", "ssm_c_im", "ssm_log_dt"]
SMALL_SHARDED = ["norm_pre", "norm_post", "ssm_d"]
TRANSPOSED = ["ffn_w_in", "ab_w_in", "ssm_b_re", "ssm_b_im"]
WEIGHTS = ['ada_w', 'ada_b', 'norm_pre', 'norm_post', 'ffn_w_in', 'ffn_w_out', 'ab_w_in', 'pool_w', 'pool_scale', 'sgu_ln_g',
           'sgu_ln_b', 'sgu_w', 'sgu_b', 'ab_w_out', 'ssm_w_in', 'ssm_lam_re', 'ssm_lam_im', 'ssm_b_re', 'ssm_b_im', 'ssm_c_re',
           'ssm_c_im', 'ssm_d', 'ssm_log_dt', 'ssm_w_glu']


def kernel(x, c, ada_w, ada_b, norm_pre, norm_post, ffn_w_in, ffn_w_out, ab_w_in, pool_w, pool_scale, sgu_ln_g, sgu_ln_b, sgu_w, sgu_b, ab_w_out, ssm_w_in, ssm_lam_re, ssm_lam_im, ssm_b_re, ssm_b_im, ssm_c_re, ssm_c_im, ssm_d, ssm_log_dt, ssm_w_glu, loss_target, m_ada_w, m_ada_b, m_norm_pre, m_norm_post, m_ffn_w_in, m_ffn_w_out, m_ab_w_in, m_pool_w, m_pool_scale, m_sgu_ln_g, m_sgu_ln_b, m_sgu_w, m_sgu_b, m_ab_w_out, m_ssm_w_in, m_ssm_lam_re, m_ssm_lam_im, m_ssm_b_re, m_ssm_b_im, m_ssm_c_re, m_ssm_c_im, m_ssm_d, m_ssm_log_dt, m_ssm_w_glu, v_ada_w, v_ada_b, v_norm_pre, v_norm_post, v_ffn_w_in, v_ffn_w_out, v_ab_w_in, v_pool_w, v_pool_scale, v_sgu_ln_g, v_sgu_ln_b, v_sgu_w, v_sgu_b, v_ab_w_out, v_ssm_w_in, v_ssm_lam_re, v_ssm_lam_im, v_ssm_b_re, v_ssm_b_im, v_ssm_c_re, v_ssm_c_im, v_ssm_d, v_ssm_log_dt, v_ssm_w_glu):
    args = locals()
    wts = {n: args[n] for n in WEIGHTS}
    mom = {n: args["m_" + n] for n in WEIGHTS}
    var = {n: args["v_" + n] for n in WEIGHTS}
    for n in TRANSPOSED:
        for t in (wts, mom, var):
            t[n] = jnp.swapaxes(t[n], -1, -2)
    me = 4 * lax.axis_index("x") + 2 * lax.axis_index("y") + lax.axis_index("c")
    s = x.shape[1]
    nd = D // N_DEV

    small_in, small_in_layout = _pack([c, norm_pre, norm_post, ssm_d])
    small_rows = -(-small_in.shape[0] // (8 * LANE)) * 8
    (g_small,) = _all_gather("gather_small", [_pad_rows(small_in, small_rows)])
    g_small = g_small.reshape(N_DEV, -1)
    c_all, npre_g, npost_g, sd_g = [jnp.stack([_unpack(g_small[j], small_in_layout)[i] for j in range(N_DEV)]) for i in range(4)]
    c_all = c_all.reshape(N_DEV, D)
    norm_pre_full = npre_g.transpose(1, 2, 0, 3).reshape(2, 3, D)
    norm_post_full = npost_g.transpose(1, 2, 0, 3).reshape(2, 3, D)
    ssm_d_full = sd_g.transpose(1, 0, 2).reshape(1, D)

    nw = ada_w.shape[-1]
    (mod_g,) = _all_gather("gather_mod", [_mod_part(c_all, ada_w)])
    mod = lax.dynamic_index_in_dim(mod_g, me, axis=2, keepdims=False)
    mod = (mod.transpose(1, 0, 2).reshape(2, N_DEV * nw) + ada_b).reshape(2, 3, 3, D)

    w_in_t = wts["ffn_w_in"]
    shards = [[w_in_t[0, 0]], [ffn_w_out[0, 0]], [wts["ab_w_in"][0], ab_w_out[0]], [w_in_t[0, 1], ffn_w_out[0, 1]],
              [w_in_t[1, 0], ffn_w_out[1, 0]], [ssm_w_in[0], ssm_w_glu[0]], [w_in_t[1, 1], ffn_w_out[1, 1]]]
    same_core = (2, 4, 6)

    def gather_plan(n):
        return lambda me_, peer_, k: [(a, None, a, me_) for a in range(n)] if k in (0, 1) + same_core else []

    def relay_plan(n):
        return lambda me_, peer_, k: [(a, me_ ^ kk, a, me_ ^ kk) for kk in same_core for a in range(n)] if k == 1 else []

    gathers, relays = [], {}
    token = mod_g
    for g, group in enumerate(shards):
        group = [a.astype(BF16) for a in group]
        sems, srcs_thru, lands, token = _exchange_start(
            f"gather_start_{g}", group, [_sds((N_DEV,) + a.shape, BF16) for a in group], gather_plan(len(group)), len(group), token)
        gathers.append((sems, srcs_thru, lands))
    mod = mod + token[0, 0]

    def relay(g, after):
        sems, srcs_thru, lands = gathers[g]
        n = len(lands)
        relays[g] = _exchange_relay(f"gather_relay_{g}", sems, srcs_thru, lands, gather_plan(n), n, relay_plan(n), 3 * n, after)

    def fetch(g, after):
        if g not in relays:
            relay(g, after)
        sems, lands, token = relays[g]
        n = len(lands)
        got = _exchange_wait(f"gather_wait_{g}", sems, None, lands, relay_plan(n), 3 * n, after)
        if 0 < g < len(gathers) - 1:
            relay(g + 1, got[0])
            token = relays[g + 1][2]
        return got, token

    head_sum = jnp.repeat(jnp.eye(NH, LANE, dtype=F32), HD, axis=0)
    mix0 = {"pool_w": pool_w[0], "pool_scale": pool_scale, "sgu_ln_g": sgu_ln_g, "sgu_ln_b": sgu_ln_b, "sgu_w": sgu_w[0],
            "sgu_bt": jnp.pad(sgu_b[0].T, ((0, 0), (0, LANE - NH))), "head_sum": head_sum}
    mix1 = _ssm_params(ssm_lam_re[0], ssm_lam_im[0], ssm_b_re[0], ssm_b_im[0], ssm_c_re[0], ssm_c_im[0], ssm_log_dt[0])
    mix1["ssm_d"] = ssm_d_full

    def weights_of(i, x_in):
        if i == 0:
            (win,), token = fetch(0, x_in)
            cache = []

            def wout_of(z):
                if not cache:
                    cache.append(fetch(1, z)[0][0])
                return cache[0]

            return (win, wout_of), token
        (a, b), token = fetch(i + 1, x_in)
        if i % 3 != 1:
            return (a, lambda z: b), token
        if i == 1:
            return dict(mix0, ab_w_in=a.reshape(-1, D), ab_w_out=b.reshape(D, D)), token
        return dict(mix1, ssm_w_in=a.reshape(D, D), ssm_w_glu=b.transpose(1, 0, 2).reshape(D, -1)), token

    def shard_cols(a):
        r = a.shape[0]
        return a.reshape(r, N_DEV, -1).transpose(1, 0, 2)

    scatter_plan = lambda me_, peer_, k: [(0, peer_, 0, me_), (1, peer_, 1, me_)]
    scatter_plan1 = lambda me_, peer_, k: [(0, peer_, 0, me_)]
    scatters = []
    last_token = [jnp.zeros((8, LANE), F32)]
    pieces, mixer, bundles = {}, {}, {}
    bundle_plan = lambda me_, peer_, k: [(0, None, 0, me_)]

    held = {}

    def on_part(i, tag, part):
        if i != 0 and tag == "w_out":
            held[i] = part
            return last_token[0]
        names, parts, plan = (("ffn_" + tag,), [part], scatter_plan1) if i == 0 else (("ffn_w_out", "ffn_w_in"), [held[i], part], scatter_plan)
        sems, srcs_thru, lands, last_token[0] = _exchange_start(
            f"scatter_start_{i}_{tag}", parts, [_sds(a.shape, BF16) for a in parts], plan, len(parts), last_token[0])
        scatters.append((i, names, plan, sems, srcs_thru, lands))
        return last_token[0]
    mix0_names = ["pool_w", "pool_scale", "sgu_ln_g", "sgu_ln_b", "sgu_w", "sgu_b"]
    mix1_names = ["ssm_lam_re", "ssm_lam_im", "ssm_b_re", "ssm_b_im", "ssm_c_re", "ssm_c_im", "ssm_log_dt", "ssm_d"]

    def start_bundle(tag, arrays):
        flat, layout = _pack(arrays)
        rows = -(-flat.shape[0] // (8 * LANE)) * 8
        plan = gather_plan(1) if tag == "a" else bundle_plan
        sems, srcs_thru, lands, last_token[0] = _exchange_start(
            f"small_start_{tag}", [_pad_rows(flat, rows)], [_sds((N_DEV, rows, LANE))], plan, 1, last_token[0])
        bundles[tag] = (sems, srcs_thru, lands, layout)

    def on_grads(i, extra, dmod_i, dpre_i, dpost_i, loss_row):
        pieces[i] = (dmod_i, dpre_i, dpost_i)
        if i == 4:
            mixer.update({n: extra[n] for n in mix1_names})
        if i == 1:
            mixer.update({n: extra[n] for n in mix0_names})
            rest = range(1, 6)
            start_bundle("a", [jnp.stack([pieces[j][0] for j in rest])] + [jnp.concatenate([pieces[j][k] for j in rest]) for k in (1, 2)]
                         + [mixer[n] for n in mix0_names + mix1_names])
        if i == 0:
            start_bundle("b", [dmod_i, dpre_i, dpost_i, loss_row])
        if i % 3 != 1:
            return last_token[0]
        if i == 1:
            names, parts = ("ab_w_in", "ab_w_out"), [extra["ab_w_in"].reshape(N_DEV, -1, D), extra["ab_w_out"].reshape(N_DEV, nd, D)]
        else:
            names, parts = ("ssm_w_in", "ssm_w_glu"), [extra["ssm_w_in"].reshape(N_DEV, nd, D), shard_cols(extra["ssm_w_glu"])]
        sems, srcs_thru, lands, last_token[0] = _exchange_start(
            f"scatter_start_{i}", parts, [_sds(a.shape, BF16) for a in parts], scatter_plan, 2, last_token[0])
        scatters.append((i, names, scatter_plan, sems, srcs_thru, lands))
        return last_token[0]

    grad_x = _local_step(x[0], loss_target[0], mod, norm_pre_full, norm_post_full, weights_of, on_part, on_grads)

    out_g, out_d, out_m, out_v = {}, {}, {}, {}
    big_out = {}

    def adam_big(name, recv, n, slot=0):
        c_ = wts[n].shape[-1]
        big_out[n] = _adamw(name, recv.reshape(recv.shape[0], -1, c_), *[t[n].reshape(-1, c_) for t in (wts, mom, var)],
                            slot=slot, prev=big_out.get(n))
        return big_out[n][0]

    ffn_slot = {0: 0, 2: 1, 3: 2, 5: 3}

    def land_and_update(entries, after):
        for i, names, plan, sems, srcs_thru, lands in entries:
            recv = _exchange_wait(f"scatter_wait_{i}_{names[0]}", sems, srcs_thru, lands, plan, len(names), after)
            for n, r in zip(names, recv):
                after = adam_big(f"adamw_{n}_{i}", r, n, ffn_slot.get(i, 0))
        return after

    after = land_and_update([e for e in scatters if e[0] != 0], last_token[0])

    def landed(tag, g_parts):
        layout = bundles[tag][3]
        off, n, shape = layout[0]
        dmods = g_parts.reshape(N_DEV, -1)[:, off:off + n].reshape((N_DEV,) + shape)
        total = _sum_parts(g_parts)
        return dmods, _unpack(total.reshape(-1), layout), total

    def adam_small(n, g, after=None):
        cols = wts[n].shape[-1]
        res = _adamw(f"adamw_{n}", g.reshape(1, -1, cols), *[t[n].reshape(-1, cols) for t in (wts, mom, var)], after=after)
        for o, arr in zip((out_g, out_d, out_m, out_v), res):
            o[n] = arr.reshape(wts[n].shape)
            if n in TRANSPOSED:
                o[n] = jnp.swapaxes(o[n], -1, -2)
        return res[0]

    sems, srcs_thru, lands, _ = bundles["a"]
    sems, lands, _ = _exchange_relay("small_relay_a", sems, srcs_thru, lands, gather_plan(1), 1, relay_plan(1), 3, after)
    (parts_a,) = _exchange_wait("small_wait_a", sems, None, lands, relay_plan(1), 3, after)
    dmods_a, sums_a, after = landed("a", parts_a)
    dmod_a, dpre_a, dpost_a = sums_a[:3]
    small = dict(zip(mix0_names + mix1_names, sums_a[3:]))
    def adam_tiny(name, grads, after):
        view = lambda n, a: a.reshape(-1, wts[n].shape[-1])
        items = [(view(n, g),) + tuple(view(n, t[n]) for t in (wts, mom, var)) for n, g in grads.items()]
        for (n, _), item, res in zip(grads.items(), items, _adamw_many(name, items, after)):
            for o, arr in zip((out_g, out_d, out_m, out_v), (item[0],) + res):
                o[n] = arr.reshape(wts[n].shape)
        return res[0]

    tiny = ["pool_scale", "sgu_ln_g", "sgu_ln_b", "sgu_b", "ssm_lam_re", "ssm_lam_im", "ssm_log_dt"]
    for n in [n for n in mix0_names + mix1_names if n not in tiny and n != "ssm_d"]:
        after = adam_small(n, small[n], after)
    after = adam_tiny("adamw_tiny_mixers", dict({n: small[n] for n in tiny},
                                                ssm_d=lax.dynamic_slice_in_dim(small["ssm_d"], me * nd, nd, axis=1)), after)
    sems, srcs_thru, lands, _ = bundles["b"]
    (parts_b,) = _exchange_wait("small_wait_b", sems, srcs_thru, lands, bundle_plan, 1, after)
    dmods_b, (dmod_b, dpre_b, dpost_b, loss_sum), after = landed("b", parts_b)
    gathered = {"a": dmods_a, "b": dmods_b}
    loss = loss_sum[0, 0]
    own = lambda first, rest: lax.dynamic_slice_in_dim(jnp.concatenate([first, rest]), me * nd, nd, axis=1)
    after = adam_tiny("adamw_tiny_shell", {"ada_b": jnp.concatenate([dmod_b[None], dmod_a]),
                                           "norm_pre": own(dpre_b, dpre_a), "norm_post": own(dpost_b, dpost_a)}, after)

    dmod_all = jnp.concatenate([gathered["b"][:, None], gathered["a"]], axis=1).reshape(N_DEV, 2, N_DEV, nw)
    dmod_mine = lax.dynamic_index_in_dim(dmod_all, me, axis=2, keepdims=False).transpose(1, 0, 2)
    g_ada_w = _ada_w_grad(c_all.T, dmod_mine)
    after = after[0:1, 0:1] + adam_big("adamw_ada_w", g_ada_w[None], "ada_w")[0:1, 0:1]

    land_and_update([e for e in scatters if e[0] == 0], after)
    for n, res in big_out.items():
        for o, arr in zip((out_g, out_d, out_m, out_v), res):
            o[n] = arr.reshape(wts[n].shape)
            if n in TRANSPOSED:
                o[n] = jnp.swapaxes(o[n], -1, -2)

    return (loss, grad_x[None], *[out_g[n] for n in WEIGHTS], *[out_d[n] for n in WEIGHTS],
            *[out_m[n] for n in WEIGHTS], *[out_v[n] for n in WEIGHTS])
```

```python
import functools
import math

import jax
import jax.numpy as jnp
from jax import lax
from jax.experimental import pallas as pl
from jax.experimental.pallas import tpu as pltpu

F32 = jnp.float32
BF16 = jnp.bfloat16
MESH = pl.DeviceIdType.MESH
HIGHEST = lax.Precision.HIGHEST

N_DEV = 8
D = 1024
D_FF = 2816
FSH = 2 * D_FF // N_DEV
EPS = 1e-6
POOL_WINDOWS = (2, 4, 8, 16)
HD = 128
NH = 4
SSM_G, SSM_P, SSM_N = 64, 64, 16
SSM_GB = 16
SSM_NB = SSM_G // SSM_GB
SSM_L = SSM_G * SSM_P
LR, B1, B2, ADAM_EPS, WD, STEP = 0.001, 0.9, 0.999, 1e-08, 0.01, 10
GELU_C = math.sqrt(2.0 / math.pi)
VMEM_LIMIT_BYTES = 48 * 1024 * 1024
LANE = 128


def _pc(body, name, grid, in_specs, out_specs, out_shape, scratch=()):
    return pl.pallas_call(
        body, name=name, grid=grid, in_specs=in_specs, out_specs=out_specs, out_shape=out_shape,
        scratch_shapes=list(scratch),
        compiler_params=pltpu.CompilerParams(dimension_semantics=("arbitrary",) * len(grid),
                                             vmem_limit_bytes=VMEM_LIMIT_BYTES))


def _sds(shape, dtype=F32):
    return jax.ShapeDtypeStruct(tuple(shape), dtype)


def _bf(v):
    return v if v.dtype == BF16 else v.astype(BF16)


def _row_spec(ts, width, col=0):
    return pl.BlockSpec((ts, width), lambda t, _c=col: (t, _c))


def _vec_spec(width, col=0):
    return pl.BlockSpec((1, width), lambda t, _c=col: (0, _c))


def _mm(name, a, b, contract, grid, a_spec, b_spec, o_spec, out_shape, acc_axis=None, after=None):
    dn = (contract, ((), ()))

    def body(a_ref, b_ref, *rest):
        o_ref = rest[-1]
        r = lax.dot_general(_bf(a_ref[...]), _bf(b_ref[...]), dn, preferred_element_type=F32)
        if acc_axis is None:
            o_ref[...] = r.astype(o_ref.dtype)
        else:
            k = pl.program_id(acc_axis)

            @pl.when(k == 0)
            def _():
                o_ref[...] = r

            @pl.when(k > 0)
            def _():
                o_ref[...] += r

    if after is None:
        return _pc(body, name, grid, [a_spec, b_spec], o_spec, out_shape)(a, b)
    return _pc(body, name, grid, [a_spec, b_spec, pl.BlockSpec(memory_space=pl.ANY)], o_spec, out_shape)(a, b, after)


def _mm_sum(name, a, b, ts, after=None):
    nj, s, k = a.shape
    n = b.shape[2]

    def body(a_ref, b_ref, *rest):
        acc = jnp.dot(a_ref[0], b_ref[0], preferred_element_type=F32)
        for j in range(1, nj):
            acc = acc + jnp.dot(a_ref[j], b_ref[j], preferred_element_type=F32)
        rest[-1][...] = acc

    specs = [pl.BlockSpec((nj, ts, k), lambda t: (0, t, 0)), pl.BlockSpec((nj, k, n), lambda t: (0, 0, 0))]
    args = (a, b)
    if after is not None:
        specs, args = specs + [pl.BlockSpec(memory_space=pl.ANY)], args + (after,)
    return _pc(body, name, (s // ts,), specs, pl.BlockSpec((ts, n), lambda t: (t, 0)), _sds((s, n)))(*args)


def _tile(s):
    return min(s, 1024)


def _div_tile(n, cap=1024):
    t = min(n, cap) // LANE * LANE
    while n % t:
        t -= LANE
    return t


def _mm_nn(name, a, b, out_dtype=F32):
    s, k = a.shape
    n = b.shape[1]
    ts, tn = _tile(s), _div_tile(n)
    return _mm(name, a, b, ((1,), (0,)), (n // tn, s // ts),
               pl.BlockSpec((ts, k), lambda j, t: (t, 0)), pl.BlockSpec((k, tn), lambda j, t: (0, j)),
               pl.BlockSpec((ts, tn), lambda j, t: (t, j)), _sds((s, n), out_dtype))


def _mm_nt(name, a, b, out_dtype=F32, after=None):
    s, n = a.shape
    k = b.shape[0]
    ts, tk = _tile(s), _div_tile(k)
    return _mm(name, a, b, ((1,), (1,)), (k // tk, s // ts),
               pl.BlockSpec((ts, n), lambda j, t: (t, 0)), pl.BlockSpec((tk, n), lambda j, t: (j, 0)),
               pl.BlockSpec((ts, tk), lambda j, t: (t, j)), _sds((s, k), out_dtype), after=after)


def _mm_tn(name, a, b, out_dtype=F32, tm=512, tn=512):
    s, m = a.shape
    n = b.shape[1]
    tm, tn = min(m, tm), min(n, tn)
    return _mm(name, a, b, ((0,), (0,)), (m // tm, n // tn),
               pl.BlockSpec((s, tm), lambda i, j: (0, i)), pl.BlockSpec((s, tn), lambda i, j: (0, j)),
               pl.BlockSpec((tm, tn), lambda i, j: (i, j)), _sds((m, n), out_dtype))


def _rstd(v):
    return lax.rsqrt(jnp.mean(v * v, axis=-1, keepdims=True) + EPS)


def _prenorm_fwd(x, g, scale, shift):
    s = x.shape[0]
    ts = min(s, 512)

    def body(x_ref, g_ref, sc_ref, sh_ref, h_ref):
        xv = x_ref[...]
        h_ref[...] = ((xv * _rstd(xv) * g_ref[...]) * (1.0 + sc_ref[...]) + sh_ref[...]).astype(BF16)

    return _pc(body, "prenorm_fwd", (s // ts,), [_row_spec(ts, D)] + [_vec_spec(D)] * 3, _row_spec(ts, D),
               _sds((s, D), BF16))(x, g, scale, shift)


def _postnorm_fwd(x, f, g, gate, rw):
    s = x.shape[0]
    ts = min(s, 512)

    def body(x_ref, f_ref, g_ref, gt_ref, o_ref):
        fv = f_ref[...]
        o_ref[...] = x_ref[...] + (rw * gt_ref[...]) * (fv * _rstd(fv) * g_ref[...])

    return _pc(body, "postnorm_fwd", (s // ts,), [_row_spec(ts, D)] * 2 + [_vec_spec(D)] * 2, _row_spec(ts, D),
               _sds((s, D)))(x, f, g, gate)


def _acc(ref, first, v):
    @pl.when(first)
    def _():
        ref[...] = v

    @pl.when(jnp.logical_not(first))
    def _():
        ref[...] += v


def _colsum(v):
    return jnp.sum(v, axis=0, keepdims=True)


def _postnorm_bwd(dout, f, g, gate, rw):
    s = dout.shape[0]
    ts = min(s, 512)

    def body(do_ref, f_ref, g_ref, gt_ref, df_ref, dgate_ref, dg_ref):
        first = pl.program_id(0) == 0
        do, fv, gv = do_ref[...], f_ref[...], g_ref[...]
        r = _rstd(fv)
        fn = fv * r
        _acc(dgate_ref, first, rw * _colsum(do * (fn * gv)))
        dy = (rw * gt_ref[...]) * do
        _acc(dg_ref, first, _colsum(dy * fn))
        dfn = dy * gv
        df_ref[...] = (r * (dfn - fn * jnp.mean(dfn * fn, axis=-1, keepdims=True))).astype(BF16)

    return _pc(body, "postnorm_bwd", (s // ts,), [_row_spec(ts, D)] * 2 + [_vec_spec(D)] * 2,
               [_row_spec(ts, D), _vec_spec(D), _vec_spec(D)],
               [_sds((s, D), BF16), _sds((1, D)), _sds((1, D))])(dout, f, g, gate)


def _prenorm_bwd(dout, dh, x, g, scale):
    s = dout.shape[0]
    ts = min(s, 512)

    def body(do_ref, dh_ref, x_ref, g_ref, sc_ref, dx_ref, dsh_ref, dsc_ref, dg_ref):
        first = pl.program_id(0) == 0
        dhv, xv, gv = dh_ref[...], x_ref[...], g_ref[...]
        r = _rstd(xv)
        xn = xv * r
        _acc(dsh_ref, first, _colsum(dhv))
        _acc(dsc_ref, first, _colsum(dhv * (xn * gv)))
        dhp = dhv * (1.0 + sc_ref[...])
        _acc(dg_ref, first, _colsum(dhp * xn))
        dxn = dhp * gv
        dx_ref[...] = do_ref[...] + r * (dxn - xn * jnp.mean(dxn * xn, axis=-1, keepdims=True))

    return _pc(body, "prenorm_bwd", (s // ts,), [_row_spec(ts, D)] * 3 + [_vec_spec(D)] * 2,
               [_row_spec(ts, D)] + [_vec_spec(D)] * 3,
               [_sds((s, D))] + [_sds((1, D))] * 3)(dout, dh, x, g, scale)


def _loss_fwd_bwd(y, tgt):
    s = y.shape[0]
    ts = min(s, 512)
    nt = s // ts

    def body(y_ref, t_ref, loss_ref, dy_ref, acc_ref):
        t = pl.program_id(0)
        e = y_ref[...] - t_ref[...]
        dy_ref[...] = e * (1.0 / D)
        _acc(acc_ref, t == 0, _colsum(e * e))

        @pl.when(t == nt - 1)
        def _():
            loss_ref[...] = jnp.full((1, LANE), 0.5 / D, F32) * jnp.sum(acc_ref[...])

    return _pc(body, "loss", (nt,), [_row_spec(ts, D)] * 2,
               [pl.BlockSpec((1, LANE), lambda t: (0, 0)), _row_spec(ts, D)],
               [_sds((1, LANE)), _sds((s, D))], scratch=[pltpu.VMEM((1, D), F32)])(y, tgt)


def _sigmoid(v):
    return 1.0 / (1.0 + jnp.exp(-v))


def _ffn_in_swiglu(h, win):
    s = h.shape[0]
    ts = _tile(s)
    nt = (((1,), (1,)), ((), ()))

    def body(h_ref, wa_ref, wb_ref, z_ref, act_ref):
        hv = h_ref[...]
        a = lax.dot_general(hv, wa_ref[...], nt, preferred_element_type=F32)
        b = lax.dot_general(hv, wb_ref[...], nt, preferred_element_type=F32)
        z_ref[0] = a
        z_ref[1] = b
        act_ref[...] = (a * _sigmoid(a) * b).astype(BF16)

    z4, act = _pc(body, "ffn_in", (4, s // ts),
                  [pl.BlockSpec((ts, D), lambda k, t: (t, 0)), pl.BlockSpec((None, FSH, D), lambda k, t: (k, 0, 0)),
                   pl.BlockSpec((None, FSH, D), lambda k, t: (k + 4, 0, 0))],
                  [pl.BlockSpec((2, None, ts, FSH), lambda k, t: (0, k, t, 0)), pl.BlockSpec((None, ts, FSH), lambda k, t: (k, t, 0))],
                  [_sds((2, 4, s, FSH)), _sds((4, s, FSH), BF16)])(h, win, win)
    return z4.reshape(N_DEV, s, FSH), act


def _ffn_out_dx_swiglu(df, wout, z, after):
    s = df.shape[0]
    ts = _tile(s)
    z4 = z.reshape(2, 4, s, FSH)
    nt = (((1,), (1,)), ((), ()))

    def body(df_ref, w_ref, z_ref, after_ref, o_ref):
        d = lax.dot_general(df_ref[...], w_ref[...], nt, preferred_element_type=F32)
        a, b = z_ref[0], z_ref[1]
        sg = _sigmoid(a)
        o_ref[0] = (d * b * (sg * (1.0 + a * (1.0 - sg)))).astype(BF16)
        o_ref[1] = (d * (a * sg)).astype(BF16)

    spec = pl.BlockSpec((2, None, ts, FSH), lambda k, t: (0, k, t, 0))
    out = _pc(body, "ffn_out_dx", (4, s // ts),
              [pl.BlockSpec((ts, D), lambda k, t: (t, 0)), pl.BlockSpec((None, FSH, D), lambda k, t: (k, 0, 0)), spec,
               pl.BlockSpec(memory_space=pl.ANY)],
              spec, _sds((2, 4, s, FSH), BF16))(df, wout, z4, after)
    return out.reshape(N_DEV, s, FSH)


def _ffn_fwd(h, win, wout_of):
    s = h.shape[0]
    z, act = _ffn_in_swiglu(h, win)
    f = _mm_sum("ffn_out", act, wout_of(z).reshape(4, FSH, D), min(s, 512))
    return f, (h, z, act)


def _ffn_bwd(df, saved, win, wout, send, after):
    h, z, act = saved
    s = h.shape[0]
    ts = s
    wout = wout.reshape(4, FSH, D)
    dwout = _mm("ffn_out_dw", act, df, ((0,), (0,)), (4, 2),
                pl.BlockSpec((None, s, FSH), lambda k, j: (k, 0, 0)), pl.BlockSpec((s, D // 2), lambda k, j: (0, j)),
                pl.BlockSpec((None, FSH, D // 2), lambda k, j: (k, 0, j)), _sds((4, FSH, D), BF16), after=after)
    dz = _ffn_out_dx_swiglu(df, wout, z, send("w_out", dwout.reshape(N_DEV, D_FF // N_DEV, D)))
    dwin = _mm("ffn_in_dw", dz, h, ((0,), (0,)), (N_DEV, 2),
               pl.BlockSpec((None, s, FSH), lambda j, i: (j, 0, 0)), pl.BlockSpec((s, D // 2), lambda j, i: (0, i)),
               pl.BlockSpec((None, FSH, D // 2), lambda j, i: (j, 0, i)), _sds((N_DEV, FSH, D), BF16))
    return _mm_sum("ffn_in_dx", dz, win, min(s, 512), after=send("w_in", dwin))


def _shift_rows(v, k, row, s, back):
    if back:
        return jnp.where(row < s - k, pltpu.roll(v, s - k, 0), 0.0)
    return jnp.where(row >= k, pltpu.roll(v, k, 0), 0.0)


def _window_sum(v, w, row, s, back):
    k = 1
    while k < w:
        v = v + _shift_rows(v, k, row, s, back)
        k *= 2
    return v


def _pool_fwd(z, pool_w, pool_scale):
    s = z.shape[0]

    def body(z_ref, w_ref, sc_ref, y_ref, d_ref):
        row = lax.broadcasted_iota(jnp.int32, (s, HD), 0)
        for g, w in enumerate(POOL_WINDOWS):
            sl = slice(g * HD, (g + 1) * HD)
            a = z_ref[:, sl]
            cnt = jnp.minimum(row + 1, w).astype(F32)
            d = (_window_sum(a, w, row, s, False) / cnt - a).astype(BF16)
            d_ref[:, sl] = d
            y = jnp.dot(d, _bf(w_ref[g]), preferred_element_type=F32)
            y_ref[:, sl] = (y * sc_ref[:, sl]).astype(BF16)

    return _pc(body, "pool_fwd", (1,),
               [pl.BlockSpec((s, NH * HD), lambda i: (0, 0)), pl.BlockSpec((NH, HD, HD), lambda i: (0, 0, 0)),
                pl.BlockSpec((1, NH * HD), lambda i: (0, 0))],
               [pl.BlockSpec((s, NH * HD), lambda i: (0, 0))] * 2,
               [_sds((s, NH * HD), BF16)] * 2)(z, pool_w, pool_scale)


def _pool_bwd(dy, d, pool_w, pool_scale):
    s = dy.shape[0]

    def body(dy_ref, d_ref, w_ref, sc_ref, dz_ref, dw_ref, dsc_ref):
        row = lax.broadcasted_iota(jnp.int32, (s, HD), 0)
        for g, w in enumerate(POOL_WINDOWS):
            sl = slice(g * HD, (g + 1) * HD)
            dyg, dg, wg = dy_ref[:, sl], d_ref[:, sl], _bf(w_ref[g])
            yraw = jnp.dot(dg, wg, preferred_element_type=F32)
            dsc_ref[:, sl] = _colsum(dyg * yraw)
            dyr = _bf(dyg * sc_ref[:, sl])
            dw_ref[g] = lax.dot_general(dg, dyr, (((0,), (0,)), ((), ())), preferred_element_type=F32)
            dd = lax.dot_general(dyr, wg, (((1,), (1,)), ((), ())), preferred_element_type=F32)
            cnt = jnp.minimum(row + 1, w).astype(F32)
            dz_ref[:, sl] = (_window_sum(dd / cnt, w, row, s, True) - dd).astype(BF16)

    return _pc(body, "pool_bwd", (1,),
               [pl.BlockSpec((s, NH * HD), lambda i: (0, 0)), pl.BlockSpec((s, NH * HD), lambda i: (0, 0)),
                pl.BlockSpec((NH, HD, HD), lambda i: (0, 0, 0)), pl.BlockSpec((1, NH * HD), lambda i: (0, 0))],
               [pl.BlockSpec((s, NH * HD), lambda i: (0, 0)), pl.BlockSpec((NH, HD, HD), lambda i: (0, 0, 0)),
                pl.BlockSpec((1, NH * HD), lambda i: (0, 0))],
               [_sds((s, NH * HD), BF16), _sds((NH, HD, HD)), _sds((1, NH * HD))])(dy, d, pool_w, pool_scale)


def _gelu(v):
    return 0.5 * v * (1.0 + jnp.tanh(GELU_C * (v + 0.044715 * (v * v * v))))


def _gelu_and_grad(v):
    t = jnp.tanh(GELU_C * (v + 0.044715 * (v * v * v)))
    return 0.5 * v * (1.0 + t), 0.5 * (1.0 + t) + 0.5 * v * (1.0 - t * t) * (GELU_C * (1.0 + 3.0 * 0.044715 * (v * v)))


def _gelu_grad(v):
    return _gelu_and_grad(v)[1]


def _causal_mask():
    return lax.broadcasted_iota(jnp.int32, (HD, HD), 0) >= lax.broadcasted_iota(jnp.int32, (HD, HD), 1)


def _sgu_specs():
    w = NH * HD
    return [pl.BlockSpec((HD, w), lambda c: (c, 1)), pl.BlockSpec((HD, w), lambda c: (c, 2)),
            pl.BlockSpec((1, w), lambda c: (0, 0)), pl.BlockSpec((1, w), lambda c: (0, 0)),
            pl.BlockSpec((NH, HD, HD), lambda c: (0, 0, 0)), pl.BlockSpec((HD, LANE), lambda c: (0, 0))]


def _sgu_head(v, lng_ref, lnb_ref, w_ref, h):
    sl = slice(h * HD, (h + 1) * HD)
    vh = v[:, sl]
    xc = vh - jnp.mean(vh, axis=-1, keepdims=True)
    rs = lax.rsqrt(jnp.mean(xc * xc, axis=-1, keepdims=True) + EPS)
    vhat = xc * rs
    vn = _bf(vhat * lng_ref[:, sl] + lnb_ref[:, sl])
    wc = _bf(jnp.where(_causal_mask(), w_ref[h], 0.0))
    return sl, rs, vhat, vn, wc


def _sgu_fwd(z, ln_g, ln_b, sgu_w, sgu_bt):
    s = z.shape[0]

    def body(zu_ref, zv_ref, lng_ref, lnb_ref, w_ref, bt_ref, y_ref):
        u, v = _gelu(zu_ref[...]), _gelu(zv_ref[...])
        for h in range(NH):
            sl, _, _, vn, wc = _sgu_head(v, lng_ref, lnb_ref, w_ref, h)
            sp = jnp.dot(wc, vn, preferred_element_type=F32) + bt_ref[:, h:h + 1]
            y_ref[:, sl] = (u[:, sl] * sp).astype(BF16)

    return _pc(body, "sgu_fwd", (s // HD,), _sgu_specs(), pl.BlockSpec((HD, NH * HD), lambda c: (c, 0)),
               _sds((s, NH * HD), BF16))(z, z, ln_g, ln_b, sgu_w, sgu_bt)


def _sgu_bwd(z, dy, ln_g, ln_b, sgu_w, sgu_bt, head_sum):
    s = z.shape[0]
    w = NH * HD
    nc = s // HD

    def body(zu_ref, zv_ref, lng_ref, lnb_ref, w_ref, bt_ref, dy_ref, hs_ref,
             dzu_ref, dzv_ref, dlng_ref, dlnb_ref, dw_ref, dbt_ref, dsacc_ref):
        c = pl.program_id(0)
        first = c == 0
        zu, zv = zu_ref[...], zv_ref[...]
        (u, gu), (v, gv) = _gelu_and_grad(zu), _gelu_and_grad(zv)
        dyv = dy_ref[...]
        ds = dyv * u
        _acc(dsacc_ref, first, ds)
        for h in range(NH):
            sl, rs, vhat, vn, wc = _sgu_head(v, lng_ref, lnb_ref, w_ref, h)
            sp = jnp.dot(wc, vn, preferred_element_type=F32) + bt_ref[:, h:h + 1]
            dzu_ref[:, sl] = (dyv[:, sl] * sp * gu[:, sl]).astype(BF16)
            dsh = _bf(ds[:, sl])
            dwh = lax.dot_general(dsh, vn, (((1,), (1,)), ((), ())), preferred_element_type=F32)
            dwh = jnp.where(_causal_mask(), dwh, 0.0)

            @pl.when(first)
            def _():
                dw_ref[h] = dwh

            @pl.when(jnp.logical_not(first))
            def _():
                dw_ref[h] += dwh

            dvn = lax.dot_general(wc, dsh, (((0,), (0,)), ((), ())), preferred_element_type=F32)
            g_col = _colsum(dvn * vhat)
            b_col = _colsum(dvn)

            @pl.when(first)
            def _():
                dlng_ref[:, sl] = g_col
                dlnb_ref[:, sl] = b_col

            @pl.when(jnp.logical_not(first))
            def _():
                dlng_ref[:, sl] += g_col
                dlnb_ref[:, sl] += b_col

            dvh = dvn * lng_ref[:, sl]
            dv = rs * (dvh - jnp.mean(dvh, axis=-1, keepdims=True) - vhat * jnp.mean(dvh * vhat, axis=-1, keepdims=True))
            dzv_ref[:, sl] = (dv * gv[:, sl]).astype(BF16)

        @pl.when(c == nc - 1)
        def _():
            dbt_ref[...] = jnp.dot(dsacc_ref[...], hs_ref[...], preferred_element_type=F32, precision=HIGHEST)

    outs = _pc(body, "sgu_bwd", (nc,),
               _sgu_specs() + [pl.BlockSpec((HD, w), lambda c: (c, 1)), pl.BlockSpec((w, LANE), lambda c: (0, 0))],
               [pl.BlockSpec((HD, w), lambda c: (c, 0))] * 2 + [pl.BlockSpec((1, w), lambda c: (0, 0))] * 2
               + [pl.BlockSpec((NH, HD, HD), lambda c: (0, 0, 0)), pl.BlockSpec((HD, LANE), lambda c: (0, 0))],
               [_sds((s, w), BF16)] * 2 + [_sds((1, w))] * 2 + [_sds((NH, HD, HD)), _sds((HD, LANE))],
               scratch=[pltpu.VMEM((HD, w), F32)])(z, z, ln_g, ln_b, sgu_w, sgu_bt, dy, head_sum)
    return outs


def _cmul(ar, ai, br, bi):
    return ar * br - ai * bi, ar * bi + ai * br


def _ssm_prep(lam_re, lam_im, lam_re_rep, lam_im_rep, log_dt, b_re, b_im):
    def disc(lr, li, dt):
        mag = jnp.exp(lr * dt)
        return mag * jnp.cos(li * dt), mag * jnp.sin(li * dt)

    def body(lr_ref, li_ref, lrr_ref, lir_ref, ldt_ref, br_ref, bi_ref, or_ref, oi_ref, bbr_ref, bbi_ref):
        dt = jnp.exp(ldt_ref[...])
        or_ref[...], oi_ref[...] = disc(lr_ref[...], li_ref[...], dt)
        lr, li = lrr_ref[...], lir_ref[...]
        er, ei = disc(lr, li, dt)
        den = lr * lr + li * li
        kr = ((er - 1.0) * lr + ei * li) / den
        ki = (ei * lr - (er - 1.0) * li) / den
        bbr_ref[...], bbi_ref[...] = _cmul(kr, ki, br_ref[...], bi_ref[...])

    small = pl.BlockSpec((SSM_G, SSM_P), lambda i: (0, 0))
    wide = pl.BlockSpec((SSM_G, SSM_P * SSM_N), lambda i: (0, 0))
    col = pl.BlockSpec((SSM_G, 1), lambda i: (0, 0))
    return _pc(body, "ssm_prep", (1,), [small, small, wide, wide, col, wide, wide], [small, small, wide, wide],
               [_sds((SSM_G, SSM_P))] * 2 + [_sds((SSM_G, SSM_P * SSM_N))] * 2)(
        lam_re, lam_im, lam_re_rep, lam_im_rep, log_dt, b_re, b_im)


def _ssm_param_bwd(g_lam_re, g_lam_im, g_bb_re, g_bb_im, lam_re, lam_im, lam_re_rep, lam_im_rep, log_dt, b_re, b_im, seg):
    def body(glr_ref, gli_ref, gbr_ref, gbi_ref, lr_ref, li_ref, lrr_ref, lir_ref, ldt_ref, br_ref, bi_ref, seg_ref,
             dlr_ref, dli_ref, ddt_ref, dbr_ref, dbi_ref):
        dt = jnp.exp(ldt_ref[...])
        lr, li = lrr_ref[...], lir_ref[...]
        mag = jnp.exp(lr * dt)
        er, ei = mag * jnp.cos(li * dt), mag * jnp.sin(li * dt)
        den = lr * lr + li * li
        kr = ((er - 1.0) * lr + ei * li) / den
        ki = (ei * lr - (er - 1.0) * li) / den
        gbr, gbi = gbr_ref[...], gbi_ref[...]
        dbr_ref[...], dbi_ref[...] = _cmul(kr, -ki, gbr, gbi)
        tr, ti = _cmul(br_ref[...], -bi_ref[...], gbr, gbi)
        gkr = jnp.dot(tr, seg_ref[...], preferred_element_type=F32, precision=HIGHEST)
        gki = jnp.dot(ti, seg_ref[...], preferred_element_type=F32, precision=HIGHEST)
        lr, li = lr_ref[...], li_ref[...]
        mag = jnp.exp(lr * dt)
        er, ei = mag * jnp.cos(li * dt), mag * jnp.sin(li * dt)
        den = lr * lr + li * li
        ir, ii = lr / den, -li / den
        kr, ki = _cmul(er - 1.0, ei, ir, ii)
        ar, ai = _cmul(ir, -ii, gkr, gki)
        glr, gli = glr_ref[...] + ar, gli_ref[...] + ai
        qr, qi = _cmul(kr, ki, ir, ii)
        g1r, g1i = _cmul(-qr, qi, gkr, gki)
        g2r, g2i = _cmul(dt * er, -dt * ei, glr, gli)
        dlr_ref[...] = g1r + g2r
        dli_ref[...] = g1i + g2i
        wr, wi = _cmul(lr, li, er, ei)
        g_dt = jnp.sum(wr * glr + wi * gli, axis=-1, keepdims=True)
        ddt_ref[...] = jnp.broadcast_to(dt * g_dt, (SSM_G, LANE))

    small = pl.BlockSpec((SSM_G, SSM_P), lambda i: (0, 0))
    wide = pl.BlockSpec((SSM_G, SSM_P * SSM_N), lambda i: (0, 0))
    col = pl.BlockSpec((SSM_G, 1), lambda i: (0, 0))
    segs = pl.BlockSpec((SSM_P * SSM_N, SSM_P), lambda i: (0, 0))
    return _pc(body, "ssm_param_bwd", (1,), [small, small, wide, wide, small, small, wide, wide, col, wide, wide, segs],
               [small, small, pl.BlockSpec((SSM_G, LANE), lambda i: (0, 0)), wide, wide],
               [_sds((SSM_G, SSM_P))] * 2 + [_sds((SSM_G, LANE))] + [_sds((SSM_G, SSM_P * SSM_N))] * 2)(
        g_lam_re, g_lam_im, g_bb_re, g_bb_im, lam_re, lam_im, lam_re_rep, lam_im_rep, log_dt, b_re, b_im, seg)


SCAN_LANES = 512
SCAN_ROWS = 8


SCAN_GROUPS = SCAN_LANES // SSM_P
SCAN_COLS = SCAN_GROUPS * SSM_N
SCAN_CHUNK = 256


def _ssm_scan(name, v, w_in, lam_re, lam_im, w_out, reverse, states=None):
    s = v.shape[0]
    ln, rows, ch = SCAN_LANES, SCAN_ROWS, min(SCAN_CHUNK, s)
    nch, ntile = s // ch, ch // rows
    nt_dims = (((1,), (1,)), ((), ()))
    with_sum = states is not None

    def body(*refs):
        v_ref, win_ref, lr_ref, li_ref, wout_ref = refs[:5]
        n_in = 7 if with_sum else 5
        or_ref, oi_ref, y_ref = refs[n_in:n_in + 3]
        br_s, bi_s = refs[-2:]
        l1 = (lr_ref[...], li_ref[...])
        pw = [l1]
        for _ in range(rows - 1):
            pw.append(_cmul(*pw[-1], *l1))
        row = lax.broadcasted_iota(jnp.int32, (rows, ln), 0)
        expo = (rows - row) if reverse else (row + 1)
        pr = jnp.zeros((rows, ln), F32)
        pi = jnp.zeros((rows, ln), F32)
        for e in range(1, rows + 1):
            pr = jnp.where(expo == e, pw[e - 1][0], pr)
            pi = jnp.where(expo == e, pw[e - 1][1], pi)
        lk = {}
        for k in (1, 2, 4):
            keep = (row < rows - k) if reverse else (row >= k)
            lk[k] = (jnp.where(keep, pw[k - 1][0], 0.0), jnp.where(keep, pw[k - 1][1], 0.0))

        def chunk(c, carry):
            q0 = pl.multiple_of(((nch - 1 - c) if reverse else c) * ch, ch)
            b = jnp.dot(_bf(v_ref[pl.ds(q0, ch), :]), win_ref[...], preferred_element_type=F32)
            br_s[...] = b[:, :ln]
            bi_s[...] = b[:, ln:]

            def step(i, carry):
                cr, ci = carry[:2]
                r0 = pl.multiple_of(((ntile - 1 - i) if reverse else i) * rows, rows)
                xr, xi = br_s[pl.ds(r0, rows), :], bi_s[pl.ds(r0, rows), :]
                for k in (1, 2, 4):
                    shift = rows - k if reverse else k
                    ar, ai = _cmul(lk[k][0], lk[k][1], pltpu.roll(xr, shift, 0), pltpu.roll(xi, shift, 0))
                    xr, xi = xr + ar, xi + ai
                ar, ai = _cmul(pr, pi, cr, ci)
                xr, xi = xr + ar, xi + ai
                g0 = pl.multiple_of(q0 + r0, rows)
                or_ref[pl.ds(g0, rows), :] = xr
                oi_ref[pl.ds(g0, rows), :] = xi
                if not with_sum:
                    return (xr[rows - 1:rows], xi[rows - 1:rows]) if not reverse else (xr[0:1], xi[0:1])
                nr = jnp.where(row == rows - 1, cr, pltpu.roll(xr, rows - 1, 0))
                ni = jnp.where(row == rows - 1, ci, pltpu.roll(xi, rows - 1, 0))
                sr, si = refs[5][pl.ds(g0, rows), :], refs[6][pl.ds(g0, rows), :]
                return xr[0:1], xi[0:1], carry[2] + (sr * nr + si * ni), carry[3] + (sr * ni - si * nr)

            carry = lax.fori_loop(0, ntile, step, carry)
            w = wout_ref[...]
            y_ref[pl.ds(q0, ch), :] = (
                lax.dot_general(_bf(or_ref[pl.ds(q0, ch), :]), w[:, :ln], nt_dims, preferred_element_type=F32)
                + lax.dot_general(_bf(oi_ref[pl.ds(q0, ch), :]), w[:, ln:], nt_dims, preferred_element_type=F32))
            return carry

        zero = jnp.zeros((1, ln), F32)
        init = (zero, zero) + ((jnp.zeros((rows, ln), F32),) * 2 if with_sum else ())
        carry = lax.fori_loop(0, nch, chunk, init)
        if with_sum:
            refs[n_in + 3][...] = _colsum(carry[2])
            refs[n_in + 4][...] = _colsum(carry[3])

    vec = pl.BlockSpec((1, ln), lambda j: (0, j))
    blk = pl.BlockSpec((s, ln), lambda j: (0, j))
    cols = pl.BlockSpec((s, SCAN_COLS), lambda j: (0, j))
    wspec = pl.BlockSpec((None, SCAN_COLS, 2 * ln), lambda j: (j, 0, 0))
    ins, args = [cols, wspec, vec, vec, wspec], [v, w_in, lam_re, lam_im, w_out]
    outs, shapes = [blk, blk, cols], [_sds((s, SSM_L))] * 2 + [_sds((s, SSM_G * SSM_N))]
    if with_sum:
        ins, args = ins + [blk, blk], args + list(states)
        outs, shapes = outs + [vec, vec], shapes + [_sds((1, SSM_L))] * 2
    return _pc(body, name, (SSM_L // ln,), ins, outs, shapes, scratch=[pltpu.VMEM((ch, ln), F32)] * 2)(*args)


def _ssm_outer(name, v, x_re, x_im):
    s = v.shape[0]
    ts = min(s, 512)
    nt = s // ts
    half = SSM_GB * SSM_P
    rows = SSM_GB * SSM_N
    tn = (((0,), (0,)), ((), ()))

    def body(v_ref, xr_ref, xi_ref, or_ref, oi_ref, acc_ref):
        vv = _bf(v_ref[...])
        pr = lax.dot_general(vv, _bf(xr_ref[...]), tn, preferred_element_type=F32)
        pi = lax.dot_general(vv, _bf(xi_ref[...]), tn, preferred_element_type=F32)
        t = pl.program_id(1)

        @pl.when(t == 0)
        def _():
            acc_ref[:, :half] = pr
            acc_ref[:, half:] = pi

        @pl.when(t > 0)
        def _():
            acc_ref[:, :half] += pr
            acc_ref[:, half:] += pi

        @pl.when(t == nt - 1)
        def _():
            row_g = lax.broadcasted_iota(jnp.int32, (rows, LANE), 0) // SSM_N
            lane_g = lax.broadcasted_iota(jnp.int32, (rows, LANE), 1) // SSM_P
            for part, o_ref in enumerate((or_ref, oi_ref)):
                fold = jnp.zeros((rows, LANE), F32)
                for cb in range(half // LANE):
                    blk = acc_ref[:, part * half + cb * LANE:part * half + (cb + 1) * LANE]
                    fold = fold + jnp.where(2 * cb + lane_g == row_g, blk, 0.0)
                o_ref[...] = jnp.where(row_g % 2 == 0, fold, pltpu.roll(fold, SSM_P, 1))

    xin = pl.BlockSpec((ts, half), lambda q, t: (t, q))
    out = pl.BlockSpec((None, rows, LANE), lambda q, t: (q, 0, 0))
    return _pc(body, name, (SSM_NB, nt), [pl.BlockSpec((ts, rows), lambda q, t: (t, q)), xin, xin], [out, out],
               [_sds((SSM_NB, rows, LANE))] * 2, scratch=[pltpu.VMEM((rows, 2 * half), F32)])(v, x_re, x_im)


def _ssm_act_fwd(y, u, d_skip):
    s = y.shape[0]
    ts = min(s, 512)

    def body(y_ref, u_ref, d_ref, o_ref):
        o_ref[...] = _gelu(y_ref[...] + d_ref[...] * u_ref[...]).astype(BF16)

    return _pc(body, "ssm_act_fwd", (s // ts,), [_row_spec(ts, D)] * 2 + [_vec_spec(D)], _row_spec(ts, D),
               _sds((s, D), BF16))(y, u, d_skip)


def _ssm_act_bwd(dg, y, u, d_skip):
    s = y.shape[0]
    ts = min(s, 512)

    def body(dg_ref, y_ref, u_ref, d_ref, dy_ref, dd_ref):
        uv = u_ref[...]
        dy = dg_ref[...] * _gelu_grad(y_ref[...] + d_ref[...] * uv)
        dy_ref[...] = dy.astype(BF16)
        _acc(dd_ref, pl.program_id(0) == 0, _colsum(dy * uv))

    return _pc(body, "ssm_act_bwd", (s // ts,), [_row_spec(ts, D)] * 3 + [_vec_spec(D)], [_row_spec(ts, D), _vec_spec(D)],
               [_sds((s, D), BF16), _sds((1, D))])(dg, y, u, d_skip)


def _axpy(a, b, d_skip):
    s = a.shape[0]
    ts = min(s, 512)

    def body(a_ref, b_ref, d_ref, o_ref):
        o_ref[...] = (a_ref[...] + d_ref[...] * b_ref[...].astype(F32)).astype(BF16)

    return _pc(body, "ssm_du", (s // ts,), [_row_spec(ts, D)] * 2 + [_vec_spec(D)], _row_spec(ts, D),
               _sds((s, D), BF16))(a, b, d_skip)


def _glu_fwd(zz):
    s = zz.shape[0]
    ts = min(s, 512)

    def body(a_ref, b_ref, o_ref):
        o_ref[...] = a_ref[...] * _sigmoid(b_ref[...])

    return _pc(body, "glu_fwd", (s // ts,), [_row_spec(ts, D, 0), _row_spec(ts, D, 1)], _row_spec(ts, D), _sds((s, D)))(zz, zz)


def _glu_bwd(zz, df):
    s = zz.shape[0]
    ts = min(s, 512)

    def body(a_ref, b_ref, df_ref, o_ref):
        sg = _sigmoid(b_ref[...])
        dfv = df_ref[...].astype(F32)
        o_ref[:, :D] = (dfv * sg).astype(BF16)
        o_ref[:, D:] = (dfv * a_ref[...] * sg * (1.0 - sg)).astype(BF16)

    return _pc(body, "glu_bwd", (s // ts,), [_row_spec(ts, D, 0), _row_spec(ts, D, 1), _row_spec(ts, D)],
               _row_spec(ts, 2 * D), _sds((s, 2 * D), BF16))(zz, zz, df)


def _ssm_block_diag(m_re, m_im):
    rows, half = SCAN_COLS, SCAN_LANES
    expand = jnp.tile(jnp.eye(SSM_P, dtype=BF16), (1, SCAN_GROUPS))

    def body(mr_ref, mi_ref, e_ref, o_ref):
        keep = (lax.broadcasted_iota(jnp.int32, (rows, half), 0) // SSM_N
                == lax.broadcasted_iota(jnp.int32, (rows, half), 1) // SSM_P)
        for part, m_ref in enumerate((mr_ref, mi_ref)):
            t = jnp.dot(_bf(m_ref[...]), e_ref[...], preferred_element_type=F32)
            o_ref[:, part * half:(part + 1) * half] = jnp.where(keep, t, 0.0).astype(BF16)

    blk = pl.BlockSpec((rows, SSM_P), lambda q: (q, 0))
    nb = SSM_G // SCAN_GROUPS
    return _pc(body, "ssm_block_diag", (nb,), [blk, blk, pl.BlockSpec((SSM_P, half), lambda q: (0, 0))],
               pl.BlockSpec((None, rows, 2 * half), lambda q: (q, 0, 0)), _sds((nb, rows, 2 * half), BF16))(m_re, m_im, expand)


def _mod_part(c_all, ada_w):
    n = ada_w.shape[-1]

    def body(c_ref, w_ref, o_ref):
        cv = c_ref[...]
        cond = _bf(cv * _sigmoid(cv))
        o_ref[...] = jnp.dot(cond, _bf(w_ref[...]), preferred_element_type=F32)

    return _pc(body, "mod_part", (2,), [pl.BlockSpec((N_DEV, D), lambda l: (0, 0)), pl.BlockSpec((None, D, n), lambda l: (l, 0, 0))],
               pl.BlockSpec((None, N_DEV, n), lambda l: (l, 0, 0)), _sds((2, N_DEV, n)))(c_all, ada_w)


def _ada_w_grad(c_all_t, dmod):
    n = dmod.shape[-1]
    tr = 128

    def body(c_ref, d_ref, o_ref):
        cv = c_ref[...]
        cond = _bf(cv * _sigmoid(cv)).astype(F32)
        dm = _bf(d_ref[...]).astype(F32)
        acc = cond[:, 0:1] * dm[0:1, :]
        for b in range(1, N_DEV):
            acc = acc + cond[:, b:b + 1] * dm[b:b + 1, :]
        o_ref[...] = acc

    return _pc(body, "ada_w_grad", (2, D // tr),
               [pl.BlockSpec((tr, N_DEV), lambda l, t: (t, 0)), pl.BlockSpec((None, N_DEV, n), lambda l, t: (l, 0, 0))],
               pl.BlockSpec((None, tr, n), lambda l, t: (l, t, 0)), _sds((2, D, n)))(c_all_t, dmod)


def _adamw(name, parts, w, m, v, slot=0, prev=None, after=None):
    p, r, c = parts.shape
    tr = r
    while tr * c * 4 > (1 << 20) and tr % 16 == 0:
        tr //= 2
    nt = r // tr

    def body(p_ref, w_ref, m_ref, v_ref, *rest):
        g_ref, d_ref, nm_ref, nv_ref = rest[-4:]
        g = p_ref[0].astype(F32)
        for i in range(1, p):
            g = g + p_ref[i].astype(F32)
        g_ref[...] = g
        d_ref[...], nm_ref[...], nv_ref[...] = _adam_update(g, w_ref[...], m_ref[...], v_ref[...])

    blk = pl.BlockSpec((tr, c), lambda t: (slot * nt + t, 0))
    in_specs = [pl.BlockSpec((p, tr, c), lambda t: (0, t, 0)), blk, blk, blk]
    unread = list(prev or []) + ([after] if after is not None else [])
    return pl.pallas_call(
        body, name=name, grid=(nt,), in_specs=in_specs + [pl.BlockSpec(memory_space=pl.ANY)] * len(unread), out_specs=[blk] * 4,
        out_shape=[_sds(w.shape)] * 4, input_output_aliases={4 + i: i for i in range(4)} if prev else {},
        compiler_params=pltpu.CompilerParams(dimension_semantics=("arbitrary",), vmem_limit_bytes=VMEM_LIMIT_BYTES))(parts, w, m, v, *unread)


def _adam_update(g, w, m, v):
    m2 = B1 * m + (1.0 - B1) * g
    v2 = B2 * v + (1.0 - B2) * (g * g)
    m_hat = m2 / (1.0 - B1 ** STEP)
    v_hat = v2 / (1.0 - B2 ** STEP)
    return -LR * (m_hat / (jnp.sqrt(v_hat) + ADAM_EPS) + WD * w), m2, v2


def _adamw_many(name, items, after):
    n = len(items)

    def body(*refs):
        outs = refs[4 * n + 1:]
        for i in range(n):
            g, w, m, v = (r[...] for r in refs[4 * i:4 * i + 4])
            for o, val in zip(outs[3 * i:3 * i + 3], _adam_update(g, w, m, v)):
                o[...] = val

    full = lambda a: pl.BlockSpec(a.shape, lambda t: (0, 0))
    flat = [a for item in items for a in item]
    res = _pc(body, name, (1,), [full(a) for a in flat] + [pl.BlockSpec(memory_space=pl.ANY)],
              [full(item[1]) for item in items for _ in range(3)],
              [_sds(item[1].shape) for item in items for _ in range(3)])(*flat, after)
    return [tuple(res[3 * i:3 * i + 3]) for i in range(n)]


def _sum_parts(parts):
    p, r, c = parts.shape
    tr = r
    while tr * c * 4 > (1 << 19) and tr % 16 == 0:
        tr //= 2

    def body(p_ref, o_ref):
        g = p_ref[0]
        for i in range(1, p):
            g = g + p_ref[i]
        o_ref[...] = g

    return _pc(body, "sum_parts", (r // tr,), [pl.BlockSpec((p, tr, c), lambda t: (0, t, 0))], pl.BlockSpec((tr, c), lambda t: (t, 0)),
               _sds((r, c)))(parts)


def _place():
    x, y, c = lax.axis_index("x"), lax.axis_index("y"), lax.axis_index("c")
    peers = []
    for k in range(1, N_DEV):
        px = (1 - x) if k & 4 else x
        py = (1 - y) if k & 2 else y
        pc = (1 - c) if k & 1 else c
        peers.append(((px, py, pc), 4 * px + 2 * py + pc))
    return 4 * x + 2 * y + c, peers


def _at(ref, idx):
    return ref if idx is None else ref.at[idx]


def _exchange_copies(plan, n, src_refs, dst_refs, send_sems, recv_sems, local_sems=None, with_arrivals=True):
    me, peers = _place()
    local = [] if local_sems is None else [
        pltpu.make_async_copy(_at(src_refs[si], sx), _at(dst_refs[di], dx), local_sems.at[i])
        for i, (si, sx, di, dx) in enumerate(plan(me, me, 0))]

    def remote(k, i, dev, entry):
        si, sx, di, dx = entry
        return pltpu.make_async_remote_copy(_at(src_refs[si], sx), _at(dst_refs[di], dx), send_sems.at[k * n + i], recv_sems.at[k * n + i],
                                            device_id=dev, device_id_type=MESH)

    sends = [remote(k, i, dev, e) for k, (dev, peer) in enumerate(peers) for i, e in enumerate(plan(me, peer, k + 1))]
    if not with_arrivals:
        return local, sends, []
    arrivals = [remote(k, i, dev, e) for k, (dev, peer) in enumerate(peers) for i, e in enumerate(plan(peer, me, k + 1))]
    return local, sends, arrivals


def _sem_shapes(n_copies, local=True):
    sems = [pltpu.SemaphoreType.DMA(((N_DEV - 1) * n_copies,)), pltpu.SemaphoreType.DMA(((N_DEV - 1) * n_copies,))]
    return sems + [pltpu.SemaphoreType.DMA((n_copies,))] if local else sems


def _exchange(name, srcs, dst_shapes, plan, n_copies):
    ns, nd = len(srcs), len(dst_shapes)

    def body(*refs):
        local, sends, arrivals = _exchange_copies(plan, n_copies, refs[:ns], refs[ns:ns + nd], *refs[ns + nd:])
        for cp in local + sends:
            cp.start()
        for cp in arrivals:
            cp.wait_recv()
        for cp in sends:
            cp.wait_send()
        for cp in local:
            cp.wait()

    any_spec = pl.BlockSpec(memory_space=pl.ANY)
    return pl.pallas_call(
        body, name=name, in_specs=[any_spec] * ns, out_specs=[any_spec] * nd, out_shape=list(dst_shapes),
        scratch_shapes=_sem_shapes(n_copies))(*srcs)


HBM_SPEC = pl.BlockSpec(memory_space=pltpu.HBM)
SEM_SPEC = pl.BlockSpec(memory_space=pltpu.SEMAPHORE)
ANY_SPEC = pl.BlockSpec(memory_space=pl.ANY)
TOKEN_SPEC = pl.BlockSpec(memory_space=pltpu.VMEM)
SIDE_EFFECT = pltpu.SideEffectType.DATAFLOW_SIDE_EFFECTING


def _wait_all(local, sends, arrivals):
    for cp in arrivals:
        cp.wait_recv()
    for cp in sends:
        cp.wait_send()
    for cp in local:
        cp.wait()


def _exchange_start(name, srcs, dst_shapes, plan, n_copies, order):
    ns, nd = len(srcs), len(dst_shapes)
    nb = ns + nd

    def body(*refs):
        local, sends, _ = _exchange_copies(plan, n_copies, refs[:ns], refs[ns:nb], *refs[nb + 1:nb + 4], with_arrivals=False)
        for cp in local + sends:
            cp.start()
        refs[-1][...] = jnp.zeros((8, LANE), F32)

    lands = [pltpu.with_memory_space_constraint(lax.empty(d.shape, d.dtype), pltpu.HBM) for d in dst_shapes]
    srcs = [pltpu.with_memory_space_constraint(a, pltpu.HBM) for a in srcs]
    bufs = srcs + lands
    out = pl.pallas_call(
        body, name=name, in_specs=[HBM_SPEC] * nb + [ANY_SPEC],
        out_specs=[SEM_SPEC] * 3 + [HBM_SPEC] * nb + [TOKEN_SPEC],
        out_shape=_sem_shapes(n_copies) + [pltpu.HBM(a.shape, a.dtype) for a in bufs] + [_sds((8, LANE))],
        input_output_aliases={i: 3 + i for i in range(nb)},
        compiler_params=pltpu.CompilerParams(has_side_effects=SIDE_EFFECT))(*bufs, order)
    return out[:3], out[3:3 + ns], out[3 + ns:3 + nb], out[-1]


def _exchange_relay(name, sems, srcs, lands, plan, n_copies, plan2, n_copies2, after):
    ns, nd = len(srcs), len(lands)
    nb = ns + nd

    def body(*refs):
        land_refs = refs[ns:nb]
        _wait_all(*_exchange_copies(plan, n_copies, refs[:ns], land_refs, *refs[nb:nb + 3]))
        _, sends, _ = _exchange_copies(plan2, n_copies2, land_refs, land_refs, *refs[nb + 4:nb + 6], with_arrivals=False)
        for cp in sends:
            cp.start()
        refs[-1][...] = jnp.zeros((8, LANE), F32)

    out = pl.pallas_call(
        body, name=name, in_specs=[HBM_SPEC] * nb + [SEM_SPEC] * 3 + [ANY_SPEC],
        out_specs=[SEM_SPEC] * 2 + [HBM_SPEC] * nd + [TOKEN_SPEC],
        out_shape=_sem_shapes(n_copies2, local=False) + [pltpu.HBM(a.shape, a.dtype) for a in lands] + [_sds((8, LANE))],
        input_output_aliases={ns + i: 2 + i for i in range(nd)},
        compiler_params=pltpu.CompilerParams(has_side_effects=SIDE_EFFECT))(*srcs, *lands, *sems, after)
    return out[:2], out[2:2 + nd], out[-1]


def _exchange_wait(name, sems, srcs, lands, plan, n_copies, after):
    srcs = [] if srcs is None else list(srcs)
    ns, nd = len(srcs), len(lands)
    nb = ns + nd

    def body(*refs):
        land_refs = refs[ns:nb]
        _wait_all(*_exchange_copies(plan, n_copies, refs[:ns] if ns else land_refs, land_refs, *refs[nb:nb + len(sems)]))

    bufs = srcs + list(lands)
    out = pl.pallas_call(
        body, name=name, in_specs=[HBM_SPEC] * nb + [SEM_SPEC] * len(sems) + [ANY_SPEC],
        out_specs=[HBM_SPEC] * nb, out_shape=[pltpu.HBM(a.shape, a.dtype) for a in bufs],
        input_output_aliases={i: i for i in range(nb)},
        compiler_params=pltpu.CompilerParams(has_side_effects=SIDE_EFFECT))(*bufs, *sems, after)
    return out[ns:]


def _all_gather(name, arrs):
    plan = lambda me, peer, k: [(i, None, i, me) for i in range(len(arrs))]
    return _exchange(name, arrs, [_sds((N_DEV,) + a.shape, a.dtype) for a in arrs], plan, len(arrs))


def _post_pre_fwd(x, f, g_post, gate, rw, g_pre, scale, shift):
    s = x.shape[0]
    ts = min(s, 256)

    def body(x_ref, f_ref, gp_ref, gt_ref, g_ref, sc_ref, sh_ref, xo_ref, h_ref):
        fv = f_ref[...]
        xv = x_ref[...] + (rw * gt_ref[...]) * (fv * _rstd(fv) * gp_ref[...])
        xo_ref[...] = xv
        h_ref[...] = ((xv * _rstd(xv) * g_ref[...]) * (1.0 + sc_ref[...]) + sh_ref[...]).astype(BF16)

    return _pc(body, "post_pre_fwd", (s // ts,), [_row_spec(ts, D)] * 2 + [_vec_spec(D)] * 5, [_row_spec(ts, D)] * 2,
               [_sds((s, D)), _sds((s, D), BF16)])(x, f, g_post, gate, g_pre, scale, shift)


def _pre_post_bwd(dout, dh, x, g_pre, scale, f, g_post, gate, rw):
    s = dout.shape[0]
    ts = min(s, 256)

    def body(do_ref, dh_ref, x_ref, g_ref, sc_ref, f_ref, gp_ref, gt_ref,
             dx_ref, dsh_ref, dsc_ref, dg_ref, df_ref, dgate_ref, dgp_ref):
        first = pl.program_id(0) == 0
        dhv, xv, gv = dh_ref[...], x_ref[...], g_ref[...]
        r = _rstd(xv)
        xn = xv * r
        _acc(dsh_ref, first, _colsum(dhv))
        _acc(dsc_ref, first, _colsum(dhv * (xn * gv)))
        dhp = dhv * (1.0 + sc_ref[...])
        _acc(dg_ref, first, _colsum(dhp * xn))
        dxn = dhp * gv
        dx = do_ref[...] + r * (dxn - xn * jnp.mean(dxn * xn, axis=-1, keepdims=True))
        dx_ref[...] = dx
        fv, gpv = f_ref[...], gp_ref[...]
        rf = _rstd(fv)
        fn = fv * rf
        _acc(dgate_ref, first, rw * _colsum(dx * (fn * gpv)))
        dy = (rw * gt_ref[...]) * dx
        _acc(dgp_ref, first, _colsum(dy * fn))
        dfn = dy * gpv
        df_ref[...] = (rf * (dfn - fn * jnp.mean(dfn * fn, axis=-1, keepdims=True))).astype(BF16)

    rows, vec = _row_spec(ts, D), _vec_spec(D)
    return _pc(body, "pre_post_bwd", (s // ts,), [rows] * 3 + [vec] * 2 + [rows] + [vec] * 2,
               [rows, vec, vec, vec, rows, vec, vec],
               [_sds((s, D))] + [_sds((1, D))] * 3 + [_sds((s, D), BF16)] + [_sds((1, D))] * 2)(
        dout, dh, x, g_pre, scale, f, g_post, gate)


def _mix0_fwd(h, p):
    z = _mm_nt("mix0_in", h, p["ab_w_in"])
    y_a, d = _pool_fwd(z, p["pool_w"], p["pool_scale"])
    y_b = _sgu_fwd(z, p["sgu_ln_g"], p["sgu_ln_b"], p["sgu_w"], p["sgu_bt"])
    ycat = jnp.concatenate([y_a, y_b], axis=1)
    return _mm_nn("mix0_out", ycat, p["ab_w_out"]), (h, z, d, ycat)


def _mix0_bwd(df, saved, p, after):
    h, z, d, ycat = saved
    dycat = _mm_nt("mix0_out_dx", df, p["ab_w_out"], after=after)
    g = {"ab_w_out": _mm_tn("mix0_out_dw", ycat, df, BF16)}
    dz_p, g["pool_w"], g["pool_scale"] = _pool_bwd(dycat, d, p["pool_w"], p["pool_scale"])
    dz_u, dz_v, g["sgu_ln_g"], g["sgu_ln_b"], g["sgu_w"], dbt = _sgu_bwd(
        z, dycat, p["sgu_ln_g"], p["sgu_ln_b"], p["sgu_w"], p["sgu_bt"], p["head_sum"])
    g["sgu_b"] = dbt[:, :NH].T
    dz = jnp.concatenate([dz_p, dz_u, dz_v], axis=1)
    g["ab_w_in"] = _mm_tn("mix0_in_dw", dz, h, BF16)
    return _mm_nn("mix0_in_dx", dz, p["ab_w_in"]), g


def _mix1_fwd(h, p):
    u = _mm_nn("ssm_w_in", h, p["ssm_w_in"])
    x_re, x_im, y = _ssm_scan("ssm_scan_fwd", u, p["wb_bd"], p["lam_bar_re"], p["lam_bar_im"], p["wc_bd"], False)
    g = _ssm_act_fwd(y, u, p["ssm_d"])
    zz = _mm_nn("ssm_glu", g, p["ssm_w_glu"])
    return _glu_fwd(zz), (h, u, x_re, x_im, y, g, zz)


def _mix1_bwd(df, saved, p, after):
    h, u, x_re, x_im, y, g, zz = saved
    gr = {}
    dzz = _glu_bwd(zz, df)
    dg = _mm_nt("ssm_glu_dx", dzz, p["ssm_w_glu"], after=after)
    gr["ssm_w_glu"] = _mm_tn("ssm_glu_dw", g, dzz, BF16)
    dy, gr["ssm_d"] = _ssm_act_bwd(dg, y, u, p["ssm_d"])
    a_re, a_im, du_ssm, g_lam_re, g_lam_im = _ssm_scan(
        "ssm_scan_bwd", dy, p["wc_bd"], p["lam_bar_re"], -p["lam_bar_im"], p["wb_bd"], True, states=(x_re, x_im))
    du = _axpy(du_ssm, dy, p["ssm_d"])
    gr["ssm_w_in"] = _mm_tn("ssm_w_in_dw", h, du, BF16)
    dh = _mm_nt("ssm_w_in_dx", du, p["ssm_w_in"])
    mb_re, mb_im = _ssm_outer("ssm_db", u, a_re, a_im)
    mc_re, mc_im = _ssm_outer("ssm_dc", dy, x_re, x_im)
    per_group = lambda m: m[:, :, :SSM_P].reshape(SSM_G, SSM_N, SSM_P)
    gr["ssm_c_re"] = per_group(mc_re)
    gr["ssm_c_im"] = -per_group(mc_im)
    dlr, dli, ddt, dbr, dbi = _ssm_param_bwd(
        g_lam_re.reshape(SSM_G, SSM_P), g_lam_im.reshape(SSM_G, SSM_P),
        per_group(mb_re).reshape(SSM_G, SSM_N * SSM_P), per_group(mb_im).reshape(SSM_G, SSM_N * SSM_P),
        p["lam_re"], p["lam_im"], p["lam_re_rep"], p["lam_im_rep"], p["log_dt"], p["b_re"], p["b_im"], p["seg"])
    gr["ssm_lam_re"], gr["ssm_lam_im"], gr["ssm_log_dt"] = dlr, dli, ddt[:, 0]
    gr["ssm_b_re"] = dbr.reshape(SSM_G, SSM_N, SSM_P)
    gr["ssm_b_im"] = dbi.reshape(SSM_G, SSM_N, SSM_P)
    return dh, gr


def _ssm_params(lam_re, lam_im, b_re, b_im, c_re, c_im, log_dt):
    wide = lambda b: b.transpose(0, 2, 1).reshape(SSM_G, SSM_N * SSM_P)
    p = {"lam_re": lam_re, "lam_im": lam_im, "log_dt": log_dt.reshape(SSM_G, 1),
         "lam_re_rep": jnp.tile(lam_re, (1, SSM_N)), "lam_im_rep": jnp.tile(lam_im, (1, SSM_N)), "b_re": wide(b_re), "b_im": wide(b_im)}
    lbr, lbi, bbr, bbi = _ssm_prep(lam_re, lam_im, p["lam_re_rep"], p["lam_im_rep"], p["log_dt"], p["b_re"], p["b_im"])
    p["lam_bar_re"], p["lam_bar_im"] = lbr.reshape(1, SSM_L), lbi.reshape(1, SSM_L)
    rows = lambda m: m.reshape(SSM_G * SSM_N, SSM_P)
    p["wb_bd"] = _ssm_block_diag(rows(bbr), rows(bbi))
    p["wc_bd"] = _ssm_block_diag(rows(c_re), rows(-c_im))
    p["seg"] = jnp.tile(jnp.eye(SSM_P, dtype=F32), (SSM_N, 1))
    return p


RES_WEIGHT = (0.5, 1.0, 0.5)


def _local_step(x, tgt, mod, norm_pre, norm_post, weights_of, on_part, on_grads):
    def fns(i, w):
        if i % 3 != 1:
            win, wout_of = w
            return ((lambda h: _ffn_fwd(h, win, wout_of)),
                    (lambda df, sv, after: (_ffn_bwd(df, sv, win, wout_of(None), lambda tag, part: on_part(i, tag, part), after), None)))
        if i == 1:
            return (lambda h: _mix0_fwd(h, w)), (lambda df, sv, after: _mix0_bwd(df, sv, w, after))
        return (lambda h: _mix1_fwd(h, w)), (lambda df, sv, after: _mix1_bwd(df, sv, w, after))

    g_pre = [norm_pre[l, s][None] for l in range(2) for s in range(3)]
    g_post = [norm_post[l, s][None] for l in range(2) for s in range(3)]
    rw = RES_WEIGHT * 2
    mods, saved, bwd = [], [], []
    f = None
    for i in range(6):
        w, token = weights_of(i, x if i == 0 else f)
        fwd, b = fns(i, w)
        m3 = mod[i // 3, i % 3] + token[0:1, 0:1]
        if i == 0:
            h = _prenorm_fwd(x, g_pre[0], m3[1:2], m3[0:1])
        else:
            x, h = _post_pre_fwd(x, f, g_post[i - 1], mods[i - 1][2:3], rw[i - 1], g_pre[i], m3[1:2], m3[0:1])
        f, inner = fwd(h)
        mods.append(m3)
        saved.append((x, f, inner))
        bwd.append(b)
    loss_row, dx = _loss_fwd_bwd(_postnorm_fwd(x, f, g_post[5], mods[5][2:3], rw[5]), tgt)
    df, dgate, dg_post = _postnorm_bwd(dx, f, g_post[5], mods[5][2:3], rw[5])
    token = jnp.zeros((8, LANE), F32)
    for i in reversed(range(6)):
        x_i, _, inner = saved[i]
        dh, extra = bwd[i](df, inner, token)
        if i > 0:
            dx, dshift, dscale, dg_pre, df, dgate_prev, dg_post_prev = _pre_post_bwd(
                dx, dh, x_i, g_pre[i], mods[i][1:2], saved[i - 1][1], g_post[i - 1], mods[i - 1][2:3], rw[i - 1])
        else:
            dx, dshift, dscale, dg_pre = _prenorm_bwd(dx, dh, x_i, g_pre[0], mods[0][1:2])
        token = on_grads(i, extra, jnp.concatenate([dshift, dscale, dgate], axis=0), dg_pre, dg_post, loss_row)
        if i > 0:
            dgate, dg_post = dgate_prev, dg_post_prev
    return dx


def _pad_rows(v, rows):
    return jnp.pad(v, (0, rows * LANE - v.shape[0])).reshape(rows, LANE)


def _pack(parts):
    flat, layout, off = [], [], 0
    for a in parts:
        n = a.size
        padded = -(-n // LANE) * LANE
        flat.append(jnp.pad(a.reshape(-1).astype(F32), (0, padded - n)))
        layout.append((off, n, a.shape))
        off += padded
    return jnp.concatenate(flat), layout


def _unpack(flat, layout):
    return [flat[off:off + n].reshape(shape) for off, n, shape in layout]


SMALL_REPLICATED = ["ada_b", "pool_w", "pool_scale", "sgu_ln_g", "sgu_ln_b", "sgu_w", "sgu_b", "ssm_lam_re", "ssm_lam_im",
                    "ssm_b_re", "ssm_b_im", "ssm_c_re", "ssm_c_im", "ssm_log_dt"]
SMALL_SHARDED = ["norm_pre", "norm_post", "ssm_d"]
TRANSPOSED = ["ffn_w_in", "ab_w_in", "ssm_b_re", "ssm_b_im"]
WEIGHTS = ['ada_w', 'ada_b', 'norm_pre', 'norm_post', 'ffn_w_in', 'ffn_w_out', 'ab_w_in', 'pool_w', 'pool_scale', 'sgu_ln_g',
           'sgu_ln_b', 'sgu_w', 'sgu_b', 'ab_w_out', 'ssm_w_in', 'ssm_lam_re', 'ssm_lam_im', 'ssm_b_re', 'ssm_b_im', 'ssm_c_re',
           'ssm_c_im', 'ssm_d', 'ssm_log_dt', 'ssm_w_glu']


def kernel(x, c, ada_w, ada_b, norm_pre, norm_post, ffn_w_in, ffn_w_out, ab_w_in, pool_w, pool_scale, sgu_ln_g, sgu_ln_b, sgu_w, sgu_b, ab_w_out, ssm_w_in, ssm_lam_re, ssm_lam_im, ssm_b_re, ssm_b_im, ssm_c_re, ssm_c_im, ssm_d, ssm_log_dt, ssm_w_glu, loss_target, m_ada_w, m_ada_b, m_norm_pre, m_norm_post, m_ffn_w_in, m_ffn_w_out, m_ab_w_in, m_pool_w, m_pool_scale, m_sgu_ln_g, m_sgu_ln_b, m_sgu_w, m_sgu_b, m_ab_w_out, m_ssm_w_in, m_ssm_lam_re, m_ssm_lam_im, m_ssm_b_re, m_ssm_b_im, m_ssm_c_re, m_ssm_c_im, m_ssm_d, m_ssm_log_dt, m_ssm_w_glu, v_ada_w, v_ada_b, v_norm_pre, v_norm_post, v_ffn_w_in, v_ffn_w_out, v_ab_w_in, v_pool_w, v_pool_scale, v_sgu_ln_g, v_sgu_ln_b, v_sgu_w, v_sgu_b, v_ab_w_out, v_ssm_w_in, v_ssm_lam_re, v_ssm_lam_im, v_ssm_b_re, v_ssm_b_im, v_ssm_c_re, v_ssm_c_im, v_ssm_d, v_ssm_log_dt, v_ssm_w_glu):
    args = locals()
    wts = {n: args[n] for n in WEIGHTS}
    mom = {n: args["m_" + n] for n in WEIGHTS}
    var = {n: args["v_" + n] for n in WEIGHTS}
    for n in TRANSPOSED:
        for t in (wts, mom, var):
            t[n] = jnp.swapaxes(t[n], -1, -2)
    me = 4 * lax.axis_index("x") + 2 * lax.axis_index("y") + lax.axis_index("c")
    s = x.shape[1]
    nd = D // N_DEV

    small_in, small_in_layout = _pack([c, norm_pre, norm_post, ssm_d])
    small_rows = -(-small_in.shape[0] // (8 * LANE)) * 8
    (g_small,) = _all_gather("gather_small", [_pad_rows(small_in, small_rows)])
    g_small = g_small.reshape(N_DEV, -1)
    c_all, npre_g, npost_g, sd_g = [jnp.stack([_unpack(g_small[j], small_in_layout)[i] for j in range(N_DEV)]) for i in range(4)]
    c_all = c_all.reshape(N_DEV, D)
    norm_pre_full = npre_g.transpose(1, 2, 0, 3).reshape(2, 3, D)
    norm_post_full = npost_g.transpose(1, 2, 0, 3).reshape(2, 3, D)
    ssm_d_full = sd_g.transpose(1, 0, 2).reshape(1, D)

    nw = ada_w.shape[-1]
    (mod_g,) = _all_gather("gather_mod", [_mod_part(c_all, ada_w)])
    mod = lax.dynamic_index_in_dim(mod_g, me, axis=2, keepdims=False)
    mod = (mod.transpose(1, 0, 2).reshape(2, N_DEV * nw) + ada_b).reshape(2, 3, 3, D)

    w_in_t = wts["ffn_w_in"]
    shards = [[w_in_t[0, 0]], [ffn_w_out[0, 0]], [wts["ab_w_in"][0], ab_w_out[0]], [w_in_t[0, 1], ffn_w_out[0, 1]],
              [w_in_t[1, 0], ffn_w_out[1, 0]], [ssm_w_in[0], ssm_w_glu[0]], [w_in_t[1, 1], ffn_w_out[1, 1]]]
    same_core = (2, 4, 6)

    def gather_plan(n):
        return lambda me_, peer_, k: [(a, None, a, me_) for a in range(n)] if k in (0, 1) + same_core else []

    def relay_plan(n):
        return lambda me_, peer_, k: [(a, me_ ^ kk, a, me_ ^ kk) for kk in same_core for a in range(n)] if k == 1 else []

    gathers, relays = [], {}
    token = mod_g
    for g, group in enumerate(shards):
        group = [a.astype(BF16) for a in group]
        sems, srcs_thru, lands, token = _exchange_start(
            f"gather_start_{g}", group, [_sds((N_DEV,) + a.shape, BF16) for a in group], gather_plan(len(group)), len(group), token)
        gathers.append((sems, srcs_thru, lands))
    mod = mod + token[0, 0]

    def relay(g, after):
        sems, srcs_thru, lands = gathers[g]
        n = len(lands)
        relays[g] = _exchange_relay(f"gather_relay_{g}", sems, srcs_thru, lands, gather_plan(n), n, relay_plan(n), 3 * n, after)

    def fetch(g, after):
        if g not in relays:
            relay(g, after)
        sems, lands, token = relays[g]
        n = len(lands)
        got = _exchange_wait(f"gather_wait_{g}", sems, None, lands, relay_plan(n), 3 * n, after)
        if 0 < g < len(gathers) - 1:
            relay(g + 1, got[0])
            token = relays[g + 1][2]
        return got, token

    head_sum = jnp.repeat(jnp.eye(NH, LANE, dtype=F32), HD, axis=0)
    mix0 = {"pool_w": pool_w[0], "pool_scale": pool_scale, "sgu_ln_g": sgu_ln_g, "sgu_ln_b": sgu_ln_b, "sgu_w": sgu_w[0],
            "sgu_bt": jnp.pad(sgu_b[0].T, ((0, 0), (0, LANE - NH))), "head_sum": head_sum}
    mix1 = _ssm_params(ssm_lam_re[0], ssm_lam_im[0], ssm_b_re[0], ssm_b_im[0], ssm_c_re[0], ssm_c_im[0], ssm_log_dt[0])
    mix1["ssm_d"] = ssm_d_full

    def weights_of(i, x_in):
        if i == 0:
            (win,), token = fetch(0, x_in)
            cache = []

            def wout_of(z):
                if not cache:
                    cache.append(fetch(1, z)[0][0])
                return cache[0]

            return (win, wout_of), token
        (a, b), token = fetch(i + 1, x_in)
        if i % 3 != 1:
            return (a, lambda z: b), token
        if i == 1:
            return dict(mix0, ab_w_in=a.reshape(-1, D), ab_w_out=b.reshape(D, D)), token
        return dict(mix1, ssm_w_in=a.reshape(D, D), ssm_w_glu=b.transpose(1, 0, 2).reshape(D, -1)), token

    def shard_cols(a):
        r = a.shape[0]
        return a.reshape(r, N_DEV, -1).transpose(1, 0, 2)

    scatter_plan = lambda me_, peer_, k: [(0, peer_, 0, me_), (1, peer_, 1, me_)]
    scatter_plan1 = lambda me_, peer_, k: [(0, peer_, 0, me_)]
    scatters = []
    last_token = [jnp.zeros((8, LANE), F32)]
    pieces, mixer, bundles = {}, {}, {}
    bundle_plan = lambda me_, peer_, k: [(0, None, 0, me_)]

    held = {}

    def on_part(i, tag, part):
        if i != 0 and tag == "w_out":
            held[i] = part
            return last_token[0]
        names, parts, plan = (("ffn_" + tag,), [part], scatter_plan1) if i == 0 else (("ffn_w_out", "ffn_w_in"), [held[i], part], scatter_plan)
        sems, srcs_thru, lands, last_token[0] = _exchange_start(
            f"scatter_start_{i}_{tag}", parts, [_sds(a.shape, BF16) for a in parts], plan, len(parts), last_token[0])
        scatters.append((i, names, plan, sems, srcs_thru, lands))
        return last_token[0]
    mix0_names = ["pool_w", "pool_scale", "sgu_ln_g", "sgu_ln_b", "sgu_w", "sgu_b"]
    mix1_names = ["ssm_lam_re", "ssm_lam_im", "ssm_b_re", "ssm_b_im", "ssm_c_re", "ssm_c_im", "ssm_log_dt", "ssm_d"]

    def start_bundle(tag, arrays):
        flat, layout = _pack(arrays)
        rows = -(-flat.shape[0] // (8 * LANE)) * 8
        plan = gather_plan(1) if tag == "a" else bundle_plan
        sems, srcs_thru, lands, last_token[0] = _exchange_start(
            f"small_start_{tag}", [_pad_rows(flat, rows)], [_sds((N_DEV, rows, LANE))], plan, 1, last_token[0])
        bundles[tag] = (sems, srcs_thru, lands, layout)

    def on_grads(i, extra, dmod_i, dpre_i, dpost_i, loss_row):
        pieces[i] = (dmod_i, dpre_i, dpost_i)
        if i == 4:
            mixer.update({n: extra[n] for n in mix1_names})
        if i == 1:
            mixer.update({n: extra[n] for n in mix0_names})
            rest = range(1, 6)
            start_bundle("a", [jnp.stack([pieces[j][0] for j in rest])] + [jnp.concatenate([pieces[j][k] for j in rest]) for k in (1, 2)]
                         + [mixer[n] for n in mix0_names + mix1_names])
        if i == 0:
            start_bundle("b", [dmod_i, dpre_i, dpost_i, loss_row])
        if i % 3 != 1:
            return last_token[0]
        if i == 1:
            names, parts = ("ab_w_in", "ab_w_out"), [extra["ab_w_in"].reshape(N_DEV, -1, D), extra["ab_w_out"].reshape(N_DEV, nd, D)]
        else:
            names, parts = ("ssm_w_in", "ssm_w_glu"), [extra["ssm_w_in"].reshape(N_DEV, nd, D), shard_cols(extra["ssm_w_glu"])]
        sems, srcs_thru, lands, last_token[0] = _exchange_start(
            f"scatter_start_{i}", parts, [_sds(a.shape, BF16) for a in parts], scatter_plan, 2, last_token[0])
        scatters.append((i, names, scatter_plan, sems, srcs_thru, lands))
        return last_token[0]

    grad_x = _local_step(x[0], loss_target[0], mod, norm_pre_full, norm_post_full, weights_of, on_part, on_grads)

    out_g, out_d, out_m, out_v = {}, {}, {}, {}
    big_out = {}

    def adam_big(name, recv, n, slot=0, after=None):
        c_ = wts[n].shape[-1]
        big_out[n] = _adamw(name, recv.reshape(recv.shape[0], -1, c_), *[t[n].reshape(-1, c_) for t in (wts, mom, var)],
                            slot=slot, prev=big_out.get(n), after=after)
        return big_out[n][0]

    ffn_slot = {0: 0, 2: 1, 3: 2, 5: 3}

    def land_and_update(entries, after):
        for i, names, plan, sems, srcs_thru, lands in entries:
            recv = _exchange_wait(f"scatter_wait_{i}_{names[0]}", sems, srcs_thru, lands, plan, len(names), after)
            for n, r in zip(names, recv):
                after = adam_big(f"adamw_{n}_{i}", r, n, ffn_slot.get(i, 0), after)
        return after

    after = land_and_update([e for e in scatters if e[0] != 0], last_token[0])

    def landed(tag, g_parts):
        layout = bundles[tag][3]
        off, n, shape = layout[0]
        dmods = g_parts.reshape(N_DEV, -1)[:, off:off + n].reshape((N_DEV,) + shape)
        total = _sum_parts(g_parts)
        return dmods, _unpack(total.reshape(-1), layout), total

    def adam_small(n, g, after=None):
        cols = wts[n].shape[-1]
        res = _adamw(f"adamw_{n}", g.reshape(1, -1, cols), *[t[n].reshape(-1, cols) for t in (wts, mom, var)], after=after)
        for o, arr in zip((out_g, out_d, out_m, out_v), res):
            o[n] = arr.reshape(wts[n].shape)
            if n in TRANSPOSED:
                o[n] = jnp.swapaxes(o[n], -1, -2)
        return res[0]

    sems, srcs_thru, lands, _ = bundles["a"]
    sems, lands, _ = _exchange_relay("small_relay_a", sems, srcs_thru, lands, gather_plan(1), 1, relay_plan(1), 3, after)
    (parts_a,) = _exchange_wait("small_wait_a", sems, None, lands, relay_plan(1), 3, after)
    dmods_a, sums_a, after = landed("a", parts_a)
    dmod_a, dpre_a, dpost_a = sums_a[:3]
    small = dict(zip(mix0_names + mix1_names, sums_a[3:]))
    def adam_tiny(name, grads, after):
        view = lambda n, a: a.reshape(-1, wts[n].shape[-1])
        items = [(view(n, g),) + tuple(view(n, t[n]) for t in (wts, mom, var)) for n, g in grads.items()]
        for (n, _), item, res in zip(grads.items(), items, _adamw_many(name, items, after)):
            for o, arr in zip((out_g, out_d, out_m, out_v), (item[0],) + res):
                o[n] = arr.reshape(wts[n].shape)
        return res[0]

    tiny = ["pool_scale", "sgu_ln_g", "sgu_ln_b", "sgu_b", "ssm_lam_re", "ssm_lam_im", "ssm_log_dt"]
    for n in [n for n in mix0_names + mix1_names if n not in tiny and n != "ssm_d"]:
        after = adam_small(n, small[n], after)
    after = adam_tiny("adamw_tiny_mixers", dict({n: small[n] for n in tiny},
                                                ssm_d=lax.dynamic_slice_in_dim(small["ssm_d"], me * nd, nd, axis=1)), after)
    sems, srcs_thru, lands, _ = bundles["b"]
    (parts_b,) = _exchange_wait("small_wait_b", sems, srcs_thru, lands, bundle_plan, 1, after)
    dmods_b, (dmod_b, dpre_b, dpost_b, loss_sum), after = landed("b", parts_b)
    gathered = {"a": dmods_a, "b": dmods_b}
    loss = loss_sum[0, 0]
    own = lambda first, rest: lax.dynamic_slice_in_dim(jnp.concatenate([first, rest]), me * nd, nd, axis=1)
    after = adam_tiny("adamw_tiny_shell", {"ada_b": jnp.concatenate([dmod_b[None], dmod_a]),
                                           "norm_pre": own(dpre_b, dpre_a), "norm_post": own(dpost_b, dpost_a)}, after)

    dmod_all = jnp.concatenate([gathered["b"][:, None], gathered["a"]], axis=1).reshape(N_DEV, 2, N_DEV, nw)
    dmod_mine = lax.dynamic_index_in_dim(dmod_all, me, axis=2, keepdims=False).transpose(1, 0, 2)
    g_ada_w = _ada_w_grad(c_all.T, dmod_mine)
    after = after[0:1, 0:1] + adam_big("adamw_ada_w", g_ada_w[None], "ada_w")[0:1, 0:1]

    land_and_update([e for e in scatters if e[0] == 0], after)
    for n, res in big_out.items():
        for o, arr in zip((out_g, out_d, out_m, out_v), res):
            o[n] = arr.reshape(wts[n].shape)
            if n in TRANSPOSED:
                o[n] = jnp.swapaxes(o[n], -1, -2)

    return (loss, grad_x[None], *[out_g[n] for n in WEIGHTS], *[out_d[n] for n in WEIGHTS],
            *[out_m[n] for n in WEIGHTS], *[out_v[n] for n in WEIGHTS])
```

```python
import functools
import math

import jax
import jax.numpy as jnp
from jax import lax
from jax.experimental import pallas as pl
from jax.experimental.pallas import tpu as pltpu

F32 = jnp.float32
BF16 = jnp.bfloat16
MESH = pl.DeviceIdType.MESH
HIGHEST = lax.Precision.HIGHEST

N_DEV = 8
D = 1024
D_FF = 2816
FSH = 2 * D_FF // N_DEV
EPS = 1e-6
POOL_WINDOWS = (2, 4, 8, 16)
HD = 128
NH = 4
SSM_G, SSM_P, SSM_N = 64, 64, 16
SSM_L = SSM_G * SSM_P
LR, B1, B2, ADAM_EPS, WD, STEP = 0.001, 0.9, 0.999, 1e-08, 0.01, 10
GELU_C = math.sqrt(2.0 / math.pi)
VMEM_LIMIT_BYTES = 48 * 1024 * 1024
LANE = 128


def _pc(body, name, grid, in_specs, out_specs, out_shape, scratch=()):
    return pl.pallas_call(
        body, name=name, grid=grid, in_specs=in_specs, out_specs=out_specs, out_shape=out_shape,
        scratch_shapes=list(scratch),
        compiler_params=pltpu.CompilerParams(dimension_semantics=("arbitrary",) * len(grid),
                                             vmem_limit_bytes=VMEM_LIMIT_BYTES))


def _sds(shape, dtype=F32):
    return jax.ShapeDtypeStruct(tuple(shape), dtype)


def _bf(v):
    return v if v.dtype == BF16 else v.astype(BF16)


def _row_spec(ts, width, col=0):
    return pl.BlockSpec((ts, width), lambda t, _c=col: (t, _c))


def _vec_spec(width, col=0):
    return pl.BlockSpec((1, width), lambda t, _c=col: (0, _c))


def _mm(name, a, b, contract, grid, a_spec, b_spec, o_spec, out_shape, acc_axis=None, after=None):
    dn = (contract, ((), ()))

    def body(a_ref, b_ref, *rest):
        o_ref = rest[-1]
        r = lax.dot_general(_bf(a_ref[...]), _bf(b_ref[...]), dn, preferred_element_type=F32)
        if acc_axis is None:
            o_ref[...] = r.astype(o_ref.dtype)
        else:
            k = pl.program_id(acc_axis)

            @pl.when(k == 0)
            def _():
                o_ref[...] = r

            @pl.when(k > 0)
            def _():
                o_ref[...] += r

    if after is None:
        return _pc(body, name, grid, [a_spec, b_spec], o_spec, out_shape)(a, b)
    return _pc(body, name, grid, [a_spec, b_spec, pl.BlockSpec(memory_space=pl.ANY)], o_spec, out_shape)(a, b, after)


def _mm_sum(name, a, b, ts, after=None):
    nj, s, k = a.shape
    n = b.shape[2]

    def body(a_ref, b_ref, *rest):
        acc = jnp.dot(a_ref[0], b_ref[0], preferred_element_type=F32)
        for j in range(1, nj):
            acc = acc + jnp.dot(a_ref[j], b_ref[j], preferred_element_type=F32)
        rest[-1][...] = acc

    specs = [pl.BlockSpec((nj, ts, k), lambda t: (0, t, 0)), pl.BlockSpec((nj, k, n), lambda t: (0, 0, 0))]
    args = (a, b)
    if after is not None:
        specs, args = specs + [pl.BlockSpec(memory_space=pl.ANY)], args + (after,)
    return _pc(body, name, (s // ts,), specs, pl.BlockSpec((ts, n), lambda t: (t, 0)), _sds((s, n)))(*args)


def _tile(s):
    return min(s, 1024)


def _div_tile(n, cap=1024):
    t = min(n, cap) // LANE * LANE
    while n % t:
        t -= LANE
    return t


def _mm_nn(name, a, b, out_dtype=F32):
    s, k = a.shape
    n = b.shape[1]
    ts, tn = _tile(s), _div_tile(n)
    return _mm(name, a, b, ((1,), (0,)), (n // tn, s // ts),
               pl.BlockSpec((ts, k), lambda j, t: (t, 0)), pl.BlockSpec((k, tn), lambda j, t: (0, j)),
               pl.BlockSpec((ts, tn), lambda j, t: (t, j)), _sds((s, n), out_dtype))


def _mm_nt(name, a, b, out_dtype=F32, after=None):
    s, n = a.shape
    k = b.shape[0]
    ts, tk = _tile(s), _div_tile(k)
    return _mm(name, a, b, ((1,), (1,)), (k // tk, s // ts),
               pl.BlockSpec((ts, n), lambda j, t: (t, 0)), pl.BlockSpec((tk, n), lambda j, t: (j, 0)),
               pl.BlockSpec((ts, tk), lambda j, t: (t, j)), _sds((s, k), out_dtype), after=after)


def _mm_tn(name, a, b, out_dtype=F32, tm=512, tn=512):
    s, m = a.shape
    n = b.shape[1]
    tm, tn = min(m, tm), min(n, tn)
    return _mm(name, a, b, ((0,), (0,)), (m // tm, n // tn),
               pl.BlockSpec((s, tm), lambda i, j: (0, i)), pl.BlockSpec((s, tn), lambda i, j: (0, j)),
               pl.BlockSpec((tm, tn), lambda i, j: (i, j)), _sds((m, n), out_dtype))


def _rstd(v):
    return lax.rsqrt(jnp.mean(v * v, axis=-1, keepdims=True) + EPS)


def _prenorm_fwd(x, g, scale, shift):
    s = x.shape[0]
    ts = min(s, 512)

    def body(x_ref, g_ref, sc_ref, sh_ref, h_ref):
        xv = x_ref[...]
        h_ref[...] = ((xv * _rstd(xv) * g_ref[...]) * (1.0 + sc_ref[...]) + sh_ref[...]).astype(BF16)

    return _pc(body, "prenorm_fwd", (s // ts,), [_row_spec(ts, D)] + [_vec_spec(D)] * 3, _row_spec(ts, D),
               _sds((s, D), BF16))(x, g, scale, shift)


def _postnorm_fwd(x, f, g, gate, rw):
    s = x.shape[0]
    ts = min(s, 512)

    def body(x_ref, f_ref, g_ref, gt_ref, o_ref):
        fv = f_ref[...]
        o_ref[...] = x_ref[...] + (rw * gt_ref[...]) * (fv * _rstd(fv) * g_ref[...])

    return _pc(body, "postnorm_fwd", (s // ts,), [_row_spec(ts, D)] * 2 + [_vec_spec(D)] * 2, _row_spec(ts, D),
               _sds((s, D)))(x, f, g, gate)


def _acc(ref, first, v):
    @pl.when(first)
    def _():
        ref[...] = v

    @pl.when(jnp.logical_not(first))
    def _():
        ref[...] += v


def _colsum(v):
    return jnp.sum(v, axis=0, keepdims=True)


def _postnorm_bwd(dout, f, g, gate, rw):
    s = dout.shape[0]
    ts = min(s, 512)

    def body(do_ref, f_ref, g_ref, gt_ref, df_ref, dgate_ref, dg_ref):
        first = pl.program_id(0) == 0
        do, fv, gv = do_ref[...], f_ref[...], g_ref[...]
        r = _rstd(fv)
        fn = fv * r
        _acc(dgate_ref, first, rw * _colsum(do * (fn * gv)))
        dy = (rw * gt_ref[...]) * do
        _acc(dg_ref, first, _colsum(dy * fn))
        dfn = dy * gv
        df_ref[...] = (r * (dfn - fn * jnp.mean(dfn * fn, axis=-1, keepdims=True))).astype(BF16)

    return _pc(body, "postnorm_bwd", (s // ts,), [_row_spec(ts, D)] * 2 + [_vec_spec(D)] * 2,
               [_row_spec(ts, D), _vec_spec(D), _vec_spec(D)],
               [_sds((s, D), BF16), _sds((1, D)), _sds((1, D))])(dout, f, g, gate)


def _prenorm_bwd(dout, dh, x, g, scale):
    s = dout.shape[0]
    ts = min(s, 512)

    def body(do_ref, dh_ref, x_ref, g_ref, sc_ref, dx_ref, dsh_ref, dsc_ref, dg_ref):
        first = pl.program_id(0) == 0
        dhv, xv, gv = dh_ref[...], x_ref[...], g_ref[...]
        r = _rstd(xv)
        xn = xv * r
        _acc(dsh_ref, first, _colsum(dhv))
        _acc(dsc_ref, first, _colsum(dhv * (xn * gv)))
        dhp = dhv * (1.0 + sc_ref[...])
        _acc(dg_ref, first, _colsum(dhp * xn))
        dxn = dhp * gv
        dx_ref[...] = do_ref[...] + r * (dxn - xn * jnp.mean(dxn * xn, axis=-1, keepdims=True))

    return _pc(body, "prenorm_bwd", (s // ts,), [_row_spec(ts, D)] * 3 + [_vec_spec(D)] * 2,
               [_row_spec(ts, D)] + [_vec_spec(D)] * 3,
               [_sds((s, D))] + [_sds((1, D))] * 3)(dout, dh, x, g, scale)


def _loss_fwd_bwd(y, tgt):
    s = y.shape[0]
    ts = min(s, 512)
    nt = s // ts

    def body(y_ref, t_ref, loss_ref, dy_ref, acc_ref):
        t = pl.program_id(0)
        e = y_ref[...] - t_ref[...]
        dy_ref[...] = e * (1.0 / D)
        _acc(acc_ref, t == 0, _colsum(e * e))

        @pl.when(t == nt - 1)
        def _():
            loss_ref[...] = jnp.full((1, LANE), 0.5 / D, F32) * jnp.sum(acc_ref[...])

    return _pc(body, "loss", (nt,), [_row_spec(ts, D)] * 2,
               [pl.BlockSpec((1, LANE), lambda t: (0, 0)), _row_spec(ts, D)],
               [_sds((1, LANE)), _sds((s, D))], scratch=[pltpu.VMEM((1, D), F32)])(y, tgt)


def _sigmoid(v):
    return 1.0 / (1.0 + jnp.exp(-v))


def _ffn_in_swiglu(h, win):
    s = h.shape[0]
    ts = _tile(s)
    nt = (((1,), (1,)), ((), ()))

    def body(h_ref, wa_ref, wb_ref, fac_ref, act_ref):
        hv = h_ref[...]
        a = lax.dot_general(hv, wa_ref[...], nt, preferred_element_type=F32)
        b = lax.dot_general(hv, wb_ref[...], nt, preferred_element_type=F32)
        sg = _sigmoid(a)
        silu = a * sg
        fac_ref[0] = (b * (sg * (1.0 + a * (1.0 - sg)))).astype(BF16)
        fac_ref[1] = silu.astype(BF16)
        act_ref[...] = (silu * b).astype(BF16)

    return _pc(body, "ffn_in", (4, s // ts),
               [pl.BlockSpec((ts, D), lambda k, t: (t, 0)), pl.BlockSpec((None, FSH, D), lambda k, t: (k, 0, 0)),
                pl.BlockSpec((None, FSH, D), lambda k, t: (k + 4, 0, 0))],
               [pl.BlockSpec((2, None, ts, FSH), lambda k, t: (0, k, t, 0)), pl.BlockSpec((None, ts, FSH), lambda k, t: (k, t, 0))],
               [_sds((2, 4, s, FSH), BF16), _sds((4, s, FSH), BF16)])(h, win, win)


def _ffn_out_dx_swiglu(df, wout, fac, after):
    s = df.shape[0]
    ts = _tile(s)
    nt = (((1,), (1,)), ((), ()))

    def body(df_ref, w_ref, fac_ref, after_ref, o_ref):
        d = lax.dot_general(df_ref[...], w_ref[...], nt, preferred_element_type=F32)
        o_ref[0] = (d * fac_ref[0]).astype(BF16)
        o_ref[1] = (d * fac_ref[1]).astype(BF16)

    spec = pl.BlockSpec((2, None, ts, FSH), lambda k, t: (0, k, t, 0))
    out = _pc(body, "ffn_out_dx", (4, s // ts),
              [pl.BlockSpec((ts, D), lambda k, t: (t, 0)), pl.BlockSpec((None, FSH, D), lambda k, t: (k, 0, 0)), spec,
               pl.BlockSpec(memory_space=pl.ANY)],
              spec, _sds((2, 4, s, FSH), BF16))(df, wout, fac, after)
    return out.reshape(N_DEV, s, FSH)


def _ffn_fwd(h, win, wout_of):
    s = h.shape[0]
    fac, act = _ffn_in_swiglu(h, win)
    f = _mm_sum("ffn_out", act, wout_of(act).reshape(4, FSH, D), min(s, 512))
    return f, (h, fac, act)


def _ffn_bwd(df, saved, win, wout, send, after):
    h, fac, act = saved
    s = h.shape[0]
    ts = s
    wout = wout.reshape(4, FSH, D)
    dwout = _mm("ffn_out_dw", act, df, ((0,), (0,)), (4, 2),
                pl.BlockSpec((None, s, FSH), lambda k, j: (k, 0, 0)), pl.BlockSpec((s, D // 2), lambda k, j: (0, j)),
                pl.BlockSpec((None, FSH, D // 2), lambda k, j: (k, 0, j)), _sds((4, FSH, D), BF16), after=after)
    dz = _ffn_out_dx_swiglu(df, wout, fac, send("w_out", dwout.reshape(N_DEV, D_FF // N_DEV, D)))
    dwin = _mm("ffn_in_dw", dz, h, ((0,), (0,)), (N_DEV, 2),
               pl.BlockSpec((None, s, FSH), lambda j, i: (j, 0, 0)), pl.BlockSpec((s, D // 2), lambda j, i: (0, i)),
               pl.BlockSpec((None, FSH, D // 2), lambda j, i: (j, 0, i)), _sds((N_DEV, FSH, D), BF16))
    return _mm_sum("ffn_in_dx", dz, win, min(s, 512), after=send("w_in", dwin))


def _shift_rows(v, k, row, s, back):
    if back:
        return jnp.where(row < s - k, pltpu.roll(v, s - k, 0), 0.0)
    return jnp.where(row >= k, pltpu.roll(v, k, 0), 0.0)


def _window_sum(v, w, row, s, back):
    k = 1
    while k < w:
        v = v + _shift_rows(v, k, row, s, back)
        k *= 2
    return v


def _pool_fwd(z, pool_w, pool_scale):
    s = z.shape[0]

    def body(z_ref, w_ref, sc_ref, y_ref, d_ref):
        row = lax.broadcasted_iota(jnp.int32, (s, HD), 0)
        for g, w in enumerate(POOL_WINDOWS):
            sl = slice(g * HD, (g + 1) * HD)
            a = z_ref[:, sl]
            cnt = jnp.minimum(row + 1, w).astype(F32)
            d = (_window_sum(a, w, row, s, False) / cnt - a).astype(BF16)
            d_ref[:, sl] = d
            y = jnp.dot(d, _bf(w_ref[g]), preferred_element_type=F32)
            y_ref[:, sl] = (y * sc_ref[:, sl]).astype(BF16)

    return _pc(body, "pool_fwd", (1,),
               [pl.BlockSpec((s, NH * HD), lambda i: (0, 0)), pl.BlockSpec((NH, HD, HD), lambda i: (0, 0, 0)),
                pl.BlockSpec((1, NH * HD), lambda i: (0, 0))],
               [pl.BlockSpec((s, NH * HD), lambda i: (0, 0))] * 2,
               [_sds((s, NH * HD), BF16)] * 2)(z, pool_w, pool_scale)


def _pool_bwd(dy, d, pool_w, pool_scale):
    s = dy.shape[0]

    def body(dy_ref, d_ref, w_ref, sc_ref, dz_ref, dw_ref, dsc_ref):
        row = lax.broadcasted_iota(jnp.int32, (s, HD), 0)
        for g, w in enumerate(POOL_WINDOWS):
            sl = slice(g * HD, (g + 1) * HD)
            dyg, dg, wg = dy_ref[:, sl], d_ref[:, sl], _bf(w_ref[g])
            yraw = jnp.dot(dg, wg, preferred_element_type=F32)
            dsc_ref[:, sl] = _colsum(dyg * yraw)
            dyr = _bf(dyg * sc_ref[:, sl])
            dw_ref[g] = lax.dot_general(dg, dyr, (((0,), (0,)), ((), ())), preferred_element_type=F32)
            dd = lax.dot_general(dyr, wg, (((1,), (1,)), ((), ())), preferred_element_type=F32)
            cnt = jnp.minimum(row + 1, w).astype(F32)
            dz_ref[:, sl] = (_window_sum(dd / cnt, w, row, s, True) - dd).astype(BF16)

    return _pc(body, "pool_bwd", (1,),
               [pl.BlockSpec((s, NH * HD), lambda i: (0, 0)), pl.BlockSpec((s, NH * HD), lambda i: (0, 0)),
                pl.BlockSpec((NH, HD, HD), lambda i: (0, 0, 0)), pl.BlockSpec((1, NH * HD), lambda i: (0, 0))],
               [pl.BlockSpec((s, NH * HD), lambda i: (0, 0)), pl.BlockSpec((NH, HD, HD), lambda i: (0, 0, 0)),
                pl.BlockSpec((1, NH * HD), lambda i: (0, 0))],
               [_sds((s, NH * HD), BF16), _sds((NH, HD, HD)), _sds((1, NH * HD))])(dy, d, pool_w, pool_scale)


def _gelu(v):
    return 0.5 * v * (1.0 + jnp.tanh(GELU_C * (v + 0.044715 * (v * v * v))))


def _gelu_and_grad(v):
    t = jnp.tanh(GELU_C * (v + 0.044715 * (v * v * v)))
    return 0.5 * v * (1.0 + t), 0.5 * (1.0 + t) + 0.5 * v * (1.0 - t * t) * (GELU_C * (1.0 + 3.0 * 0.044715 * (v * v)))


def _gelu_grad(v):
    return _gelu_and_grad(v)[1]


def _causal_mask():
    return lax.broadcasted_iota(jnp.int32, (HD, HD), 0) >= lax.broadcasted_iota(jnp.int32, (HD, HD), 1)


def _sgu_specs():
    w = NH * HD
    return [pl.BlockSpec((HD, w), lambda c: (c, 1)), pl.BlockSpec((HD, w), lambda c: (c, 2)),
            pl.BlockSpec((1, w), lambda c: (0, 0)), pl.BlockSpec((1, w), lambda c: (0, 0)),
            pl.BlockSpec((NH, HD, HD), lambda c: (0, 0, 0)), pl.BlockSpec((HD, LANE), lambda c: (0, 0))]


def _sgu_head(v, lng_ref, lnb_ref, w_ref, h):
    sl = slice(h * HD, (h + 1) * HD)
    vh = v[:, sl]
    xc = vh - jnp.mean(vh, axis=-1, keepdims=True)
    rs = lax.rsqrt(jnp.mean(xc * xc, axis=-1, keepdims=True) + EPS)
    vhat = xc * rs
    vn = _bf(vhat * lng_ref[:, sl] + lnb_ref[:, sl])
    wc = _bf(jnp.where(_causal_mask(), w_ref[h], 0.0))
    return sl, rs, vhat, vn, wc


def _sgu_fwd(z, ln_g, ln_b, sgu_w, sgu_bt):
    s = z.shape[0]

    def body(zu_ref, zv_ref, lng_ref, lnb_ref, w_ref, bt_ref, y_ref):
        u, v = _gelu(zu_ref[...]), _gelu(zv_ref[...])
        for h in range(NH):
            sl, _, _, vn, wc = _sgu_head(v, lng_ref, lnb_ref, w_ref, h)
            sp = jnp.dot(wc, vn, preferred_element_type=F32) + bt_ref[:, h:h + 1]
            y_ref[:, sl] = (u[:, sl] * sp).astype(BF16)

    return _pc(body, "sgu_fwd", (s // HD,), _sgu_specs(), pl.BlockSpec((HD, NH * HD), lambda c: (c, 0)),
               _sds((s, NH * HD), BF16))(z, z, ln_g, ln_b, sgu_w, sgu_bt)


def _sgu_bwd(z, dy, ln_g, ln_b, sgu_w, sgu_bt, head_sum):
    s = z.shape[0]
    w = NH * HD
    nc = s // HD

    def body(zu_ref, zv_ref, lng_ref, lnb_ref, w_ref, bt_ref, dy_ref, hs_ref,
             dzu_ref, dzv_ref, dlng_ref, dlnb_ref, dw_ref, dbt_ref, dsacc_ref):
        c = pl.program_id(0)
        first = c == 0
        zu, zv = zu_ref[...], zv_ref[...]
        (u, gu), (v, gv) = _gelu_and_grad(zu), _gelu_and_grad(zv)
        dyv = dy_ref[...]
        ds = dyv * u
        _acc(dsacc_ref, first, ds)
        for h in range(NH):
            sl, rs, vhat, vn, wc = _sgu_head(v, lng_ref, lnb_ref, w_ref, h)
            sp = jnp.dot(wc, vn, preferred_element_type=F32) + bt_ref[:, h:h + 1]
            dzu_ref[:, sl] = (dyv[:, sl] * sp * gu[:, sl]).astype(BF16)
            dsh = _bf(ds[:, sl])
            dwh = lax.dot_general(dsh, vn, (((1,), (1,)), ((), ())), preferred_element_type=F32)
            dwh = jnp.where(_causal_mask(), dwh, 0.0)

            @pl.when(first)
            def _():
                dw_ref[h] = dwh

            @pl.when(jnp.logical_not(first))
            def _():
                dw_ref[h] += dwh

            dvn = lax.dot_general(wc, dsh, (((0,), (0,)), ((), ())), preferred_element_type=F32)
            g_col = _colsum(dvn * vhat)
            b_col = _colsum(dvn)

            @pl.when(first)
            def _():
                dlng_ref[:, sl] = g_col
                dlnb_ref[:, sl] = b_col

            @pl.when(jnp.logical_not(first))
            def _():
                dlng_ref[:, sl] += g_col
                dlnb_ref[:, sl] += b_col

            dvh = dvn * lng_ref[:, sl]
            dv = rs * (dvh - jnp.mean(dvh, axis=-1, keepdims=True) - vhat * jnp.mean(dvh * vhat, axis=-1, keepdims=True))
            dzv_ref[:, sl] = (dv * gv[:, sl]).astype(BF16)

        @pl.when(c == nc - 1)
        def _():
            dbt_ref[...] = jnp.dot(dsacc_ref[...], hs_ref[...], preferred_element_type=F32, precision=HIGHEST)

    outs = _pc(body, "sgu_bwd", (nc,),
               _sgu_specs() + [pl.BlockSpec((HD, w), lambda c: (c, 1)), pl.BlockSpec((w, LANE), lambda c: (0, 0))],
               [pl.BlockSpec((HD, w), lambda c: (c, 0))] * 2 + [pl.BlockSpec((1, w), lambda c: (0, 0))] * 2
               + [pl.BlockSpec((NH, HD, HD), lambda c: (0, 0, 0)), pl.BlockSpec((HD, LANE), lambda c: (0, 0))],
               [_sds((s, w), BF16)] * 2 + [_sds((1, w))] * 2 + [_sds((NH, HD, HD)), _sds((HD, LANE))],
               scratch=[pltpu.VMEM((HD, w), F32)])(z, z, ln_g, ln_b, sgu_w, sgu_bt, dy, head_sum)
    return outs


def _cmul(ar, ai, br, bi):
    return ar * br - ai * bi, ar * bi + ai * br


def _ssm_prep(lam_re, lam_im, lam_re_rep, lam_im_rep, log_dt, b_re, b_im):
    def disc(lr, li, dt):
        mag = jnp.exp(lr * dt)
        return mag * jnp.cos(li * dt), mag * jnp.sin(li * dt)

    def body(lr_ref, li_ref, lrr_ref, lir_ref, ldt_ref, br_ref, bi_ref, or_ref, oi_ref, bbr_ref, bbi_ref):
        dt = jnp.exp(ldt_ref[...])
        or_ref[...], oi_ref[...] = disc(lr_ref[...], li_ref[...], dt)
        lr, li = lrr_ref[...], lir_ref[...]
        er, ei = disc(lr, li, dt)
        den = lr * lr + li * li
        kr = ((er - 1.0) * lr + ei * li) / den
        ki = (ei * lr - (er - 1.0) * li) / den
        bbr_ref[...], bbi_ref[...] = _cmul(kr, ki, br_ref[...], bi_ref[...])

    small = pl.BlockSpec((SSM_G, SSM_P), lambda i: (0, 0))
    wide = pl.BlockSpec((SSM_G, SSM_P * SSM_N), lambda i: (0, 0))
    col = pl.BlockSpec((SSM_G, 1), lambda i: (0, 0))
    return _pc(body, "ssm_prep", (1,), [small, small, wide, wide, col, wide, wide], [small, small, wide, wide],
               [_sds((SSM_G, SSM_P))] * 2 + [_sds((SSM_G, SSM_P * SSM_N))] * 2)(
        lam_re, lam_im, lam_re_rep, lam_im_rep, log_dt, b_re, b_im)


def _ssm_param_bwd(g_lam_re, g_lam_im, g_bb_re, g_bb_im, lam_re, lam_im, lam_re_rep, lam_im_rep, log_dt, b_re, b_im, seg):
    def body(glr_ref, gli_ref, gbr_ref, gbi_ref, lr_ref, li_ref, lrr_ref, lir_ref, ldt_ref, br_ref, bi_ref, seg_ref,
             dlr_ref, dli_ref, ddt_ref, dbr_ref, dbi_ref):
        dt = jnp.exp(ldt_ref[...])
        lr, li = lrr_ref[...], lir_ref[...]
        mag = jnp.exp(lr * dt)
        er, ei = mag * jnp.cos(li * dt), mag * jnp.sin(li * dt)
        den = lr * lr + li * li
        kr = ((er - 1.0) * lr + ei * li) / den
        ki = (ei * lr - (er - 1.0) * li) / den
        gbr, gbi = gbr_ref[...], gbi_ref[...]
        dbr_ref[...], dbi_ref[...] = _cmul(kr, -ki, gbr, gbi)
        tr, ti = _cmul(br_ref[...], -bi_ref[...], gbr, gbi)
        gkr = jnp.dot(tr, seg_ref[...], preferred_element_type=F32, precision=HIGHEST)
        gki = jnp.dot(ti, seg_ref[...], preferred_element_type=F32, precision=HIGHEST)
        lr, li = lr_ref[...], li_ref[...]
        mag = jnp.exp(lr * dt)
        er, ei = mag * jnp.cos(li * dt), mag * jnp.sin(li * dt)
        den = lr * lr + li * li
        ir, ii = lr / den, -li / den
        kr, ki = _cmul(er - 1.0, ei, ir, ii)
        ar, ai = _cmul(ir, -ii, gkr, gki)
        glr, gli = glr_ref[...] + ar, gli_ref[...] + ai
        qr, qi = _cmul(kr, ki, ir, ii)
        g1r, g1i = _cmul(-qr, qi, gkr, gki)
        g2r, g2i = _cmul(dt * er, -dt * ei, glr, gli)
        dlr_ref[...] = g1r + g2r
        dli_ref[...] = g1i + g2i
        wr, wi = _cmul(lr, li, er, ei)
        g_dt = jnp.sum(wr * glr + wi * gli, axis=-1, keepdims=True)
        ddt_ref[...] = jnp.broadcast_to(dt * g_dt, (SSM_G, LANE))

    small = pl.BlockSpec((SSM_G, SSM_P), lambda i: (0, 0))
    wide = pl.BlockSpec((SSM_G, SSM_P * SSM_N), lambda i: (0, 0))
    col = pl.BlockSpec((SSM_G, 1), lambda i: (0, 0))
    segs = pl.BlockSpec((SSM_P * SSM_N, SSM_P), lambda i: (0, 0))
    return _pc(body, "ssm_param_bwd", (1,), [small, small, wide, wide, small, small, wide, wide, col, wide, wide, segs],
               [small, small, pl.BlockSpec((SSM_G, LANE), lambda i: (0, 0)), wide, wide],
               [_sds((SSM_G, SSM_P))] * 2 + [_sds((SSM_G, LANE))] + [_sds((SSM_G, SSM_P * SSM_N))] * 2)(
        g_lam_re, g_lam_im, g_bb_re, g_bb_im, lam_re, lam_im, lam_re_rep, lam_im_rep, log_dt, b_re, b_im, seg)


SCAN_LANES = 512
SCAN_ROWS = 8


SCAN_GROUPS = SCAN_LANES // SSM_P
SCAN_COLS = SCAN_GROUPS * SSM_N
SCAN_CHUNK = 256


def _ssm_scan(name, v, w_in, lam_re, lam_im, w_out, reverse, states=None, u=None):
    s = v.shape[0]
    ln, rows, ch = SCAN_LANES, SCAN_ROWS, min(SCAN_CHUNK, s)
    nch, ntile = s // ch, ch // rows
    nt_dims = (((1,), (1,)), ((), ()))
    with_sum = states is not None
    tn_dims = (((0,), (0,)), ((), ()))

    def body(*refs):
        v_ref, win_ref, lr_ref, li_ref, wout_ref = refs[:5]
        n_in = 8 if with_sum else 5
        or_ref, oi_ref, y_ref = refs[n_in:n_in + 3]
        br_s, bi_s = refs[n_in + (9 if with_sum else 3):][:2]
        if with_sum:
            mb_s, mc_s = refs[-2:]
            mb_s[...] = jnp.zeros_like(mb_s)
            mc_s[...] = jnp.zeros_like(mc_s)
        l1 = (lr_ref[...], li_ref[...])
        pw = [l1]
        for _ in range(rows - 1):
            pw.append(_cmul(*pw[-1], *l1))
        row = lax.broadcasted_iota(jnp.int32, (rows, ln), 0)
        expo = (rows - row) if reverse else (row + 1)
        pr = jnp.zeros((rows, ln), F32)
        pi = jnp.zeros((rows, ln), F32)
        for e in range(1, rows + 1):
            pr = jnp.where(expo == e, pw[e - 1][0], pr)
            pi = jnp.where(expo == e, pw[e - 1][1], pi)
        lk = {}
        for k in (1, 2, 4):
            keep = (row < rows - k) if reverse else (row >= k)
            lk[k] = (jnp.where(keep, pw[k - 1][0], 0.0), jnp.where(keep, pw[k - 1][1], 0.0))

        def chunk(c, carry):
            q0 = pl.multiple_of(((nch - 1 - c) if reverse else c) * ch, ch)
            b = jnp.dot(_bf(v_ref[pl.ds(q0, ch), :]), win_ref[...], preferred_element_type=F32)
            br_s[...] = b[:, :ln]
            bi_s[...] = b[:, ln:]

            def step(i, carry):
                cr, ci = carry[:2]
                r0 = pl.multiple_of(((ntile - 1 - i) if reverse else i) * rows, rows)
                xr, xi = br_s[pl.ds(r0, rows), :], bi_s[pl.ds(r0, rows), :]
                for k in (1, 2, 4):
                    shift = rows - k if reverse else k
                    ar, ai = _cmul(lk[k][0], lk[k][1], pltpu.roll(xr, shift, 0), pltpu.roll(xi, shift, 0))
                    xr, xi = xr + ar, xi + ai
                ar, ai = _cmul(pr, pi, cr, ci)
                xr, xi = xr + ar, xi + ai
                g0 = pl.multiple_of(q0 + r0, rows)
                or_ref[pl.ds(g0, rows), :] = xr
                oi_ref[pl.ds(g0, rows), :] = xi
                if not with_sum:
                    return (xr[rows - 1:rows], xi[rows - 1:rows]) if not reverse else (xr[0:1], xi[0:1])
                nr = jnp.where(row == rows - 1, cr, pltpu.roll(xr, rows - 1, 0))
                ni = jnp.where(row == rows - 1, ci, pltpu.roll(xi, rows - 1, 0))
                sr, si = refs[5][pl.ds(g0, rows), :], refs[6][pl.ds(g0, rows), :]
                return xr[0:1], xi[0:1], carry[2] + (sr * nr + si * ni), carry[3] + (sr * ni - si * nr)

            carry = lax.fori_loop(0, ntile, step, carry)
            if with_sum:
                rows_c = pl.ds(q0, ch)
                uc, vc = _bf(refs[7][rows_c, :]), _bf(v_ref[rows_c, :])
                for scr, left, (right_re, right_im) in ((mb_s, uc, (or_ref, oi_ref)), (mc_s, vc, (refs[5], refs[6]))):
                    scr[:, :ln] += lax.dot_general(left, _bf(right_re[rows_c, :]), tn_dims, preferred_element_type=F32)
                    scr[:, ln:] += lax.dot_general(left, _bf(right_im[rows_c, :]), tn_dims, preferred_element_type=F32)
            w = wout_ref[...]
            y_ref[pl.ds(q0, ch), :] = (
                lax.dot_general(_bf(or_ref[pl.ds(q0, ch), :]), w[:, :ln], nt_dims, preferred_element_type=F32)
                + lax.dot_general(_bf(oi_ref[pl.ds(q0, ch), :]), w[:, ln:], nt_dims, preferred_element_type=F32))
            return carry

        zero = jnp.zeros((1, ln), F32)
        init = (zero, zero) + ((jnp.zeros((rows, ln), F32),) * 2 if with_sum else ())
        carry = lax.fori_loop(0, nch, chunk, init)
        if with_sum:
            refs[n_in + 3][...] = _colsum(carry[2])
            refs[n_in + 4][...] = _colsum(carry[3])
            row_g = lax.broadcasted_iota(jnp.int32, (SCAN_COLS, LANE), 0) // SSM_N
            lane_g = lax.broadcasted_iota(jnp.int32, (SCAN_COLS, LANE), 1) // SSM_P
            for scr, o_re, o_im in ((mb_s, refs[n_in + 5], refs[n_in + 6]), (mc_s, refs[n_in + 7], refs[n_in + 8])):
                for part, o_ref in enumerate((o_re, o_im)):
                    fold = jnp.zeros((SCAN_COLS, LANE), F32)
                    for cb in range(ln // LANE):
                        fold = fold + jnp.where(2 * cb + lane_g == row_g, scr[:, part * ln + cb * LANE:part * ln + (cb + 1) * LANE], 0.0)
                    o_ref[...] = jnp.where(row_g % 2 == 0, fold, pltpu.roll(fold, SSM_P, 1))

    vec = pl.BlockSpec((1, ln), lambda j: (0, j))
    blk = pl.BlockSpec((s, ln), lambda j: (0, j))
    cols = pl.BlockSpec((s, SCAN_COLS), lambda j: (0, j))
    wspec = pl.BlockSpec((None, SCAN_COLS, 2 * ln), lambda j: (j, 0, 0))
    ins, args = [cols, wspec, vec, vec, wspec], [v, w_in, lam_re, lam_im, w_out]
    outs, shapes = [blk, blk, cols], [_sds((s, SSM_L))] * 2 + [_sds((s, SSM_G * SSM_N))]
    scratch = [pltpu.VMEM((ch, ln), F32)] * 2
    if with_sum:
        own = pl.BlockSpec((None, SCAN_COLS, LANE), lambda j: (j, 0, 0))
        ins, args = ins + [blk, blk, cols], args + list(states) + [u]
        outs = outs + [vec, vec] + [own] * 4
        shapes = shapes + [_sds((1, SSM_L))] * 2 + [_sds((SSM_L // ln, SCAN_COLS, LANE))] * 4
        scratch = scratch + [pltpu.VMEM((SCAN_COLS, 2 * ln), F32)] * 2
    return _pc(body, name, (SSM_L // ln,), ins, outs, shapes, scratch=scratch)(*args)


def _ssm_act_fwd(y, u, d_skip):
    s = y.shape[0]
    ts = min(s, 512)

    def body(y_ref, u_ref, d_ref, o_ref):
        o_ref[...] = _gelu(y_ref[...] + d_ref[...] * u_ref[...]).astype(BF16)

    return _pc(body, "ssm_act_fwd", (s // ts,), [_row_spec(ts, D)] * 2 + [_vec_spec(D)], _row_spec(ts, D),
               _sds((s, D), BF16))(y, u, d_skip)


def _ssm_act_bwd(dg, y, u, d_skip):
    s = y.shape[0]
    ts = min(s, 512)

    def body(dg_ref, y_ref, u_ref, d_ref, dy_ref, dd_ref):
        uv = u_ref[...]
        dy = dg_ref[...] * _gelu_grad(y_ref[...] + d_ref[...] * uv)
        dy_ref[...] = dy.astype(BF16)
        _acc(dd_ref, pl.program_id(0) == 0, _colsum(dy * uv))

    return _pc(body, "ssm_act_bwd", (s // ts,), [_row_spec(ts, D)] * 3 + [_vec_spec(D)], [_row_spec(ts, D), _vec_spec(D)],
               [_sds((s, D), BF16), _sds((1, D))])(dg, y, u, d_skip)


def _axpy(a, b, d_skip):
    s = a.shape[0]
    ts = min(s, 512)

    def body(a_ref, b_ref, d_ref, o_ref):
        o_ref[...] = (a_ref[...] + d_ref[...] * b_ref[...].astype(F32)).astype(BF16)

    return _pc(body, "ssm_du", (s // ts,), [_row_spec(ts, D)] * 2 + [_vec_spec(D)], _row_spec(ts, D),
               _sds((s, D), BF16))(a, b, d_skip)


def _glu_fwd(zz):
    s = zz.shape[0]
    ts = min(s, 512)

    def body(a_ref, b_ref, o_ref):
        o_ref[...] = a_ref[...] * _sigmoid(b_ref[...])

    return _pc(body, "glu_fwd", (s // ts,), [_row_spec(ts, D, 0), _row_spec(ts, D, 1)], _row_spec(ts, D), _sds((s, D)))(zz, zz)


def _glu_bwd(zz, df):
    s = zz.shape[0]
    ts = min(s, 512)

    def body(a_ref, b_ref, df_ref, o_ref):
        sg = _sigmoid(b_ref[...])
        dfv = df_ref[...].astype(F32)
        o_ref[:, :D] = (dfv * sg).astype(BF16)
        o_ref[:, D:] = (dfv * a_ref[...] * sg * (1.0 - sg)).astype(BF16)

    return _pc(body, "glu_bwd", (s // ts,), [_row_spec(ts, D, 0), _row_spec(ts, D, 1), _row_spec(ts, D)],
               _row_spec(ts, 2 * D), _sds((s, 2 * D), BF16))(zz, zz, df)


def _ssm_block_diag(m_re, m_im):
    rows, half = SCAN_COLS, SCAN_LANES
    expand = jnp.tile(jnp.eye(SSM_P, dtype=BF16), (1, SCAN_GROUPS))

    def body(mr_ref, mi_ref, e_ref, o_ref):
        keep = (lax.broadcasted_iota(jnp.int32, (rows, half), 0) // SSM_N
                == lax.broadcasted_iota(jnp.int32, (rows, half), 1) // SSM_P)
        for part, m_ref in enumerate((mr_ref, mi_ref)):
            t = jnp.dot(_bf(m_ref[...]), e_ref[...], preferred_element_type=F32)
            o_ref[:, part * half:(part + 1) * half] = jnp.where(keep, t, 0.0).astype(BF16)

    blk = pl.BlockSpec((rows, SSM_P), lambda q: (q, 0))
    nb = SSM_G // SCAN_GROUPS
    return _pc(body, "ssm_block_diag", (nb,), [blk, blk, pl.BlockSpec((SSM_P, half), lambda q: (0, 0))],
               pl.BlockSpec((None, rows, 2 * half), lambda q: (q, 0, 0)), _sds((nb, rows, 2 * half), BF16))(m_re, m_im, expand)


def _mod_part(c_all, ada_w):
    n = ada_w.shape[-1]

    def body(c_ref, w_ref, o_ref):
        cv = c_ref[...]
        cond = _bf(cv * _sigmoid(cv))
        o_ref[...] = jnp.dot(cond, _bf(w_ref[...]), preferred_element_type=F32)

    return _pc(body, "mod_part", (2,), [pl.BlockSpec((N_DEV, D), lambda l: (0, 0)), pl.BlockSpec((None, D, n), lambda l: (l, 0, 0))],
               pl.BlockSpec((None, N_DEV, n), lambda l: (l, 0, 0)), _sds((2, N_DEV, n)))(c_all, ada_w)


def _ada_w_grad(c_all_t, dmod):
    n = dmod.shape[-1]
    tr = 128

    def body(c_ref, d_ref, o_ref):
        cv = c_ref[...]
        cond = _bf(cv * _sigmoid(cv)).astype(F32)
        dm = _bf(d_ref[...]).astype(F32)
        acc = cond[:, 0:1] * dm[0:1, :]
        for b in range(1, N_DEV):
            acc = acc + cond[:, b:b + 1] * dm[b:b + 1, :]
        o_ref[...] = acc

    return _pc(body, "ada_w_grad", (2, D // tr),
               [pl.BlockSpec((tr, N_DEV), lambda l, t: (t, 0)), pl.BlockSpec((None, N_DEV, n), lambda l, t: (l, 0, 0))],
               pl.BlockSpec((None, tr, n), lambda l, t: (l, t, 0)), _sds((2, D, n)))(c_all_t, dmod)


def _adamw(name, parts, w, m, v, slot=0, prev=None, after=None):
    p, r, c = parts.shape
    tr = r
    while tr * c * 4 > (1 << 20) and tr % 16 == 0:
        tr //= 2
    nt = r // tr

    def body(p_ref, w_ref, m_ref, v_ref, *rest):
        g_ref, d_ref, nm_ref, nv_ref = rest[-4:]
        g = p_ref[0].astype(F32)
        for i in range(1, p):
            g = g + p_ref[i].astype(F32)
        g_ref[...] = g
        d_ref[...], nm_ref[...], nv_ref[...] = _adam_update(g, w_ref[...], m_ref[...], v_ref[...])

    blk = pl.BlockSpec((tr, c), lambda t: (slot * nt + t, 0))
    in_specs = [pl.BlockSpec((p, tr, c), lambda t: (0, t, 0)), blk, blk, blk]
    unread = list(prev or []) + ([after] if after is not None else [])
    return pl.pallas_call(
        body, name=name, grid=(nt,), in_specs=in_specs + [pl.BlockSpec(memory_space=pl.ANY)] * len(unread), out_specs=[blk] * 4,
        out_shape=[_sds(w.shape)] * 4, input_output_aliases={4 + i: i for i in range(4)} if prev else {},
        compiler_params=pltpu.CompilerParams(dimension_semantics=("arbitrary",), vmem_limit_bytes=VMEM_LIMIT_BYTES))(parts, w, m, v, *unread)


def _adam_update(g, w, m, v):
    m2 = B1 * m + (1.0 - B1) * g
    v2 = B2 * v + (1.0 - B2) * (g * g)
    m_hat = m2 / (1.0 - B1 ** STEP)
    v_hat = v2 / (1.0 - B2 ** STEP)
    return -LR * (m_hat / (jnp.sqrt(v_hat) + ADAM_EPS) + WD * w), m2, v2


def _adamw_many(name, items, after):
    n = len(items)

    def body(*refs):
        outs = refs[4 * n + 1:]
        for i in range(n):
            g, w, m, v = (r[...] for r in refs[4 * i:4 * i + 4])
            for o, val in zip(outs[3 * i:3 * i + 3], _adam_update(g, w, m, v)):
                o[...] = val

    full = lambda a: pl.BlockSpec(a.shape, lambda t: (0, 0))
    flat = [a for item in items for a in item]
    res = _pc(body, name, (1,), [full(a) for a in flat] + [pl.BlockSpec(memory_space=pl.ANY)],
              [full(item[1]) for item in items for _ in range(3)],
              [_sds(item[1].shape) for item in items for _ in range(3)])(*flat, after)
    return [tuple(res[3 * i:3 * i + 3]) for i in range(n)]


def _sum_parts(parts):
    p, r, c = parts.shape
    tr = r
    while tr * c * 4 > (1 << 19) and tr % 16 == 0:
        tr //= 2

    def body(p_ref, o_ref):
        g = p_ref[0]
        for i in range(1, p):
            g = g + p_ref[i]
        o_ref[...] = g

    return _pc(body, "sum_parts", (r // tr,), [pl.BlockSpec((p, tr, c), lambda t: (0, t, 0))], pl.BlockSpec((tr, c), lambda t: (t, 0)),
               _sds((r, c)))(parts)


def _place():
    x, y, c = lax.axis_index("x"), lax.axis_index("y"), lax.axis_index("c")
    peers = []
    for k in range(1, N_DEV):
        px = (1 - x) if k & 4 else x
        py = (1 - y) if k & 2 else y
        pc = (1 - c) if k & 1 else c
        peers.append(((px, py, pc), 4 * px + 2 * py + pc))
    return 4 * x + 2 * y + c, peers


def _at(ref, idx):
    return ref if idx is None else ref.at[idx]


def _exchange_copies(plan, n, src_refs, dst_refs, send_sems, recv_sems, local_sems=None, with_arrivals=True):
    me, peers = _place()
    local = [] if local_sems is None else [
        pltpu.make_async_copy(_at(src_refs[si], sx), _at(dst_refs[di], dx), local_sems.at[i])
        for i, (si, sx, di, dx) in enumerate(plan(me, me, 0))]

    def remote(k, i, dev, entry):
        si, sx, di, dx = entry
        return pltpu.make_async_remote_copy(_at(src_refs[si], sx), _at(dst_refs[di], dx), send_sems.at[k * n + i], recv_sems.at[k * n + i],
                                            device_id=dev, device_id_type=MESH)

    sends = [remote(k, i, dev, e) for k, (dev, peer) in enumerate(peers) for i, e in enumerate(plan(me, peer, k + 1))]
    if not with_arrivals:
        return local, sends, []
    arrivals = [remote(k, i, dev, e) for k, (dev, peer) in enumerate(peers) for i, e in enumerate(plan(peer, me, k + 1))]
    return local, sends, arrivals


def _sem_shapes(n_copies, local=True):
    sems = [pltpu.SemaphoreType.DMA(((N_DEV - 1) * n_copies,)), pltpu.SemaphoreType.DMA(((N_DEV - 1) * n_copies,))]
    return sems + [pltpu.SemaphoreType.DMA((n_copies,))] if local else sems


def _exchange(name, srcs, dst_shapes, plan, n_copies):
    ns, nd = len(srcs), len(dst_shapes)

    def body(*refs):
        local, sends, arrivals = _exchange_copies(plan, n_copies, refs[:ns], refs[ns:ns + nd], *refs[ns + nd:])
        for cp in local + sends:
            cp.start()
        for cp in arrivals:
            cp.wait_recv()
        for cp in sends:
            cp.wait_send()
        for cp in local:
            cp.wait()

    any_spec = pl.BlockSpec(memory_space=pl.ANY)
    return pl.pallas_call(
        body, name=name, in_specs=[any_spec] * ns, out_specs=[any_spec] * nd, out_shape=list(dst_shapes),
        scratch_shapes=_sem_shapes(n_copies))(*srcs)


HBM_SPEC = pl.BlockSpec(memory_space=pltpu.HBM)
SEM_SPEC = pl.BlockSpec(memory_space=pltpu.SEMAPHORE)
ANY_SPEC = pl.BlockSpec(memory_space=pl.ANY)
TOKEN_SPEC = pl.BlockSpec(memory_space=pltpu.VMEM)
SIDE_EFFECT = pltpu.SideEffectType.DATAFLOW_SIDE_EFFECTING


def _wait_all(local, sends, arrivals):
    for cp in arrivals:
        cp.wait_recv()
    for cp in sends:
        cp.wait_send()
    for cp in local:
        cp.wait()


def _exchange_start(name, srcs, dst_shapes, plan, n_copies, order):
    ns, nd = len(srcs), len(dst_shapes)
    nb = ns + nd

    def body(*refs):
        local, sends, _ = _exchange_copies(plan, n_copies, refs[:ns], refs[ns:nb], *refs[nb + 1:nb + 4], with_arrivals=False)
        for cp in local + sends:
            cp.start()
        refs[-1][...] = jnp.zeros((8, LANE), F32)

    lands = [pltpu.with_memory_space_constraint(lax.empty(d.shape, d.dtype), pltpu.HBM) for d in dst_shapes]
    srcs = [pltpu.with_memory_space_constraint(a, pltpu.HBM) for a in srcs]
    bufs = srcs + lands
    out = pl.pallas_call(
        body, name=name, in_specs=[HBM_SPEC] * nb + [ANY_SPEC],
        out_specs=[SEM_SPEC] * 3 + [HBM_SPEC] * nb + [TOKEN_SPEC],
        out_shape=_sem_shapes(n_copies) + [pltpu.HBM(a.shape, a.dtype) for a in bufs] + [_sds((8, LANE))],
        input_output_aliases={i: 3 + i for i in range(nb)},
        compiler_params=pltpu.CompilerParams(has_side_effects=SIDE_EFFECT))(*bufs, order)
    return out[:3], out[3:3 + ns], out[3 + ns:3 + nb], out[-1]


def _exchange_relay(name, sems, srcs, lands, plan, n_copies, plan2, n_copies2, after):
    ns, nd = len(srcs), len(lands)
    nb = ns + nd

    def body(*refs):
        land_refs = refs[ns:nb]
        _wait_all(*_exchange_copies(plan, n_copies, refs[:ns], land_refs, *refs[nb:nb + 3]))
        _, sends, _ = _exchange_copies(plan2, n_copies2, land_refs, land_refs, *refs[nb + 4:nb + 6], with_arrivals=False)
        for cp in sends:
            cp.start()
        refs[-1][...] = jnp.zeros((8, LANE), F32)

    out = pl.pallas_call(
        body, name=name, in_specs=[HBM_SPEC] * nb + [SEM_SPEC] * 3 + [ANY_SPEC],
        out_specs=[SEM_SPEC] * 2 + [HBM_SPEC] * nd + [TOKEN_SPEC],
        out_shape=_sem_shapes(n_copies2, local=False) + [pltpu.HBM(a.shape, a.dtype) for a in lands] + [_sds((8, LANE))],
        input_output_aliases={ns + i: 2 + i for i in range(nd)},
        compiler_params=pltpu.CompilerParams(has_side_effects=SIDE_EFFECT))(*srcs, *lands, *sems, after)
    return out[:2], out[2:2 + nd], out[-1]


def _exchange_wait(name, sems, srcs, lands, plan, n_copies, after):
    srcs = [] if srcs is None else list(srcs)
    ns, nd = len(srcs), len(lands)
    nb = ns + nd

    def body(*refs):
        land_refs = refs[ns:nb]
        _wait_all(*_exchange_copies(plan, n_copies, refs[:ns] if ns else land_refs, land_refs, *refs[nb:nb + len(sems)]))

    bufs = srcs + list(lands)
    out = pl.pallas_call(
        body, name=name, in_specs=[HBM_SPEC] * nb + [SEM_SPEC] * len(sems) + [ANY_SPEC],
        out_specs=[HBM_SPEC] * nb, out_shape=[pltpu.HBM(a.shape, a.dtype) for a in bufs],
        input_output_aliases={i: i for i in range(nb)},
        compiler_params=pltpu.CompilerParams(has_side_effects=SIDE_EFFECT))(*bufs, *sems, after)
    return out[ns:]


def _all_gather(name, arrs):
    plan = lambda me, peer, k: [(i, None, i, me) for i in range(len(arrs))]
    return _exchange(name, arrs, [_sds((N_DEV,) + a.shape, a.dtype) for a in arrs], plan, len(arrs))


def _post_pre_fwd(x, f, g_post, gate, rw, g_pre, scale, shift):
    s = x.shape[0]
    ts = min(s, 256)

    def body(x_ref, f_ref, gp_ref, gt_ref, g_ref, sc_ref, sh_ref, xo_ref, h_ref):
        fv = f_ref[...]
        xv = x_ref[...] + (rw * gt_ref[...]) * (fv * _rstd(fv) * gp_ref[...])
        xo_ref[...] = xv
        h_ref[...] = ((xv * _rstd(xv) * g_ref[...]) * (1.0 + sc_ref[...]) + sh_ref[...]).astype(BF16)

    return _pc(body, "post_pre_fwd", (s // ts,), [_row_spec(ts, D)] * 2 + [_vec_spec(D)] * 5, [_row_spec(ts, D)] * 2,
               [_sds((s, D)), _sds((s, D), BF16)])(x, f, g_post, gate, g_pre, scale, shift)


def _pre_post_bwd(dout, dh, x, g_pre, scale, f, g_post, gate, rw):
    s = dout.shape[0]
    ts = min(s, 256)

    def body(do_ref, dh_ref, x_ref, g_ref, sc_ref, f_ref, gp_ref, gt_ref,
             dx_ref, dsh_ref, dsc_ref, dg_ref, df_ref, dgate_ref, dgp_ref):
        first = pl.program_id(0) == 0
        dhv, xv, gv = dh_ref[...], x_ref[...], g_ref[...]
        r = _rstd(xv)
        xn = xv * r
        _acc(dsh_ref, first, _colsum(dhv))
        _acc(dsc_ref, first, _colsum(dhv * (xn * gv)))
        dhp = dhv * (1.0 + sc_ref[...])
        _acc(dg_ref, first, _colsum(dhp * xn))
        dxn = dhp * gv
        dx = do_ref[...] + r * (dxn - xn * jnp.mean(dxn * xn, axis=-1, keepdims=True))
        dx_ref[...] = dx
        fv, gpv = f_ref[...], gp_ref[...]
        rf = _rstd(fv)
        fn = fv * rf
        _acc(dgate_ref, first, rw * _colsum(dx * (fn * gpv)))
        dy = (rw * gt_ref[...]) * dx
        _acc(dgp_ref, first, _colsum(dy * fn))
        dfn = dy * gpv
        df_ref[...] = (rf * (dfn - fn * jnp.mean(dfn * fn, axis=-1, keepdims=True))).astype(BF16)

    rows, vec = _row_spec(ts, D), _vec_spec(D)
    return _pc(body, "pre_post_bwd", (s // ts,), [rows] * 3 + [vec] * 2 + [rows] + [vec] * 2,
               [rows, vec, vec, vec, rows, vec, vec],
               [_sds((s, D))] + [_sds((1, D))] * 3 + [_sds((s, D), BF16)] + [_sds((1, D))] * 2)(
        dout, dh, x, g_pre, scale, f, g_post, gate)


def _mix0_fwd(h, p):
    z = _mm_nt("mix0_in", h, p["ab_w_in"])
    y_a, d = _pool_fwd(z, p["pool_w"], p["pool_scale"])
    y_b = _sgu_fwd(z, p["sgu_ln_g"], p["sgu_ln_b"], p["sgu_w"], p["sgu_bt"])
    ycat = jnp.concatenate([y_a, y_b], axis=1)
    return _mm_nn("mix0_out", ycat, p["ab_w_out"]), (h, z, d, ycat)


def _mix0_bwd(df, saved, p, after):
    h, z, d, ycat = saved
    dycat = _mm_nt("mix0_out_dx", df, p["ab_w_out"], after=after)
    g = {"ab_w_out": _mm_tn("mix0_out_dw", ycat, df, BF16)}
    dz_p, g["pool_w"], g["pool_scale"] = _pool_bwd(dycat, d, p["pool_w"], p["pool_scale"])
    dz_u, dz_v, g["sgu_ln_g"], g["sgu_ln_b"], g["sgu_w"], dbt = _sgu_bwd(
        z, dycat, p["sgu_ln_g"], p["sgu_ln_b"], p["sgu_w"], p["sgu_bt"], p["head_sum"])
    g["sgu_b"] = dbt[:, :NH].T
    dz = jnp.concatenate([dz_p, dz_u, dz_v], axis=1)
    g["ab_w_in"] = _mm_tn("mix0_in_dw", dz, h, BF16)
    return _mm_nn("mix0_in_dx", dz, p["ab_w_in"]), g


def _mix1_fwd(h, p):
    u = _mm_nn("ssm_w_in", h, p["ssm_w_in"])
    x_re, x_im, y = _ssm_scan("ssm_scan_fwd", u, p["wb_bd"], p["lam_bar_re"], p["lam_bar_im"], p["wc_bd"], False)
    g = _ssm_act_fwd(y, u, p["ssm_d"])
    zz = _mm_nn("ssm_glu", g, p["ssm_w_glu"])
    return _glu_fwd(zz), (h, u, x_re, x_im, y, g, zz)


def _mix1_bwd(df, saved, p, after):
    h, u, x_re, x_im, y, g, zz = saved
    gr = {}
    dzz = _glu_bwd(zz, df)
    dg = _mm_nt("ssm_glu_dx", dzz, p["ssm_w_glu"], after=after)
    gr["ssm_w_glu"] = _mm_tn("ssm_glu_dw", g, dzz, BF16)
    dy, gr["ssm_d"] = _ssm_act_bwd(dg, y, u, p["ssm_d"])
    _, _, du_ssm, g_lam_re, g_lam_im, mb_re, mb_im, mc_re, mc_im = _ssm_scan(
        "ssm_scan_bwd", dy, p["wc_bd"], p["lam_bar_re"], -p["lam_bar_im"], p["wb_bd"], True, states=(x_re, x_im), u=u)
    du = _axpy(du_ssm, dy, p["ssm_d"])
    gr["ssm_w_in"] = _mm_tn("ssm_w_in_dw", h, du, BF16)
    dh = _mm_nt("ssm_w_in_dx", du, p["ssm_w_in"])
    per_group = lambda m: m[:, :, :SSM_P].reshape(SSM_G, SSM_N, SSM_P)
    gr["ssm_c_re"] = per_group(mc_re)
    gr["ssm_c_im"] = -per_group(mc_im)
    dlr, dli, ddt, dbr, dbi = _ssm_param_bwd(
        g_lam_re.reshape(SSM_G, SSM_P), g_lam_im.reshape(SSM_G, SSM_P),
        per_group(mb_re).reshape(SSM_G, SSM_N * SSM_P), per_group(mb_im).reshape(SSM_G, SSM_N * SSM_P),
        p["lam_re"], p["lam_im"], p["lam_re_rep"], p["lam_im_rep"], p["log_dt"], p["b_re"], p["b_im"], p["seg"])
    gr["ssm_lam_re"], gr["ssm_lam_im"], gr["ssm_log_dt"] = dlr, dli, ddt[:, 0]
    gr["ssm_b_re"] = dbr.reshape(SSM_G, SSM_N, SSM_P)
    gr["ssm_b_im"] = dbi.reshape(SSM_G, SSM_N, SSM_P)
    return dh, gr


def _ssm_params(lam_re, lam_im, b_re, b_im, c_re, c_im, log_dt):
    wide = lambda b: b.transpose(0, 2, 1).reshape(SSM_G, SSM_N * SSM_P)
    p = {"lam_re": lam_re, "lam_im": lam_im, "log_dt": log_dt.reshape(SSM_G, 1),
         "lam_re_rep": jnp.tile(lam_re, (1, SSM_N)), "lam_im_rep": jnp.tile(lam_im, (1, SSM_N)), "b_re": wide(b_re), "b_im": wide(b_im)}
    lbr, lbi, bbr, bbi = _ssm_prep(lam_re, lam_im, p["lam_re_rep"], p["lam_im_rep"], p["log_dt"], p["b_re"], p["b_im"])
    p["lam_bar_re"], p["lam_bar_im"] = lbr.reshape(1, SSM_L), lbi.reshape(1, SSM_L)
    rows = lambda m: m.reshape(SSM_G * SSM_N, SSM_P)
    p["wb_bd"] = _ssm_block_diag(rows(bbr), rows(bbi))
    p["wc_bd"] = _ssm_block_diag(rows(c_re), rows(-c_im))
    p["seg"] = jnp.tile(jnp.eye(SSM_P, dtype=F32), (SSM_N, 1))
    return p


RES_WEIGHT = (0.5, 1.0, 0.5)


def _local_step(x, tgt, mod, norm_pre, norm_post, weights_of, on_part, on_grads):
    def fns(i, w):
        if i % 3 != 1:
            win, wout_of = w
            return ((lambda h: _ffn_fwd(h, win, wout_of)),
                    (lambda df, sv, after: (_ffn_bwd(df, sv, win, wout_of(None), lambda tag, part: on_part(i, tag, part), after), None)))
        if i == 1:
            return (lambda h: _mix0_fwd(h, w)), (lambda df, sv, after: _mix0_bwd(df, sv, w, after))
        return (lambda h: _mix1_fwd(h, w)), (lambda df, sv, after: _mix1_bwd(df, sv, w, after))

    g_pre = [norm_pre[l, s][None] for l in range(2) for s in range(3)]
    g_post = [norm_post[l, s][None] for l in range(2) for s in range(3)]
    rw = RES_WEIGHT * 2
    mods, saved, bwd = [], [], []
    f = None
    for i in range(6):
        w, token = weights_of(i, x if i == 0 else f)
        fwd, b = fns(i, w)
        m3 = mod[i // 3, i % 3] + token[0:1, 0:1]
        if i == 0:
            h = _prenorm_fwd(x, g_pre[0], m3[1:2], m3[0:1])
        else:
            x, h = _post_pre_fwd(x, f, g_post[i - 1], mods[i - 1][2:3], rw[i - 1], g_pre[i], m3[1:2], m3[0:1])
        f, inner = fwd(h)
        mods.append(m3)
        saved.append((x, f, inner))
        bwd.append(b)
    loss_row, dx = _loss_fwd_bwd(_postnorm_fwd(x, f, g_post[5], mods[5][2:3], rw[5]), tgt)
    df, dgate, dg_post = _postnorm_bwd(dx, f, g_post[5], mods[5][2:3], rw[5])
    token = jnp.zeros((8, LANE), F32)
    for i in reversed(range(6)):
        x_i, _, inner = saved[i]
        dh, extra = bwd[i](df, inner, token)
        if i > 0:
            dx, dshift, dscale, dg_pre, df, dgate_prev, dg_post_prev = _pre_post_bwd(
                dx, dh, x_i, g_pre[i], mods[i][1:2], saved[i - 1][1], g_post[i - 1], mods[i - 1][2:3], rw[i - 1])
        else:
            dx, dshift, dscale, dg_pre = _prenorm_bwd(dx, dh, x_i, g_pre[0], mods[0][1:2])
        token = on_grads(i, extra, jnp.concatenate([dshift, dscale, dgate], axis=0), dg_pre, dg_post, loss_row)
        if i > 0:
            dgate, dg_post = dgate_prev, dg_post_prev
    return dx


def _pad_rows(v, rows):
    return jnp.pad(v, (0, rows * LANE - v.shape[0])).reshape(rows, LANE)


def _pack(parts):
    flat, layout, off = [], [], 0
    for a in parts:
        n = a.size
        padded = -(-n // LANE) * LANE
        flat.append(jnp.pad(a.reshape(-1).astype(F32), (0, padded - n)))
        layout.append((off, n, a.shape))
        off += padded
    return jnp.concatenate(flat), layout


def _unpack(flat, layout):
    return [flat[off:off + n].reshape(shape) for off, n, shape in layout]


SMALL_REPLICATED = ["ada_b", "pool_w", "pool_scale", "sgu_ln_g", "sgu_ln_b", "sgu_w", "sgu_b", "ssm_lam_re", "ssm_lam_im",
                    "ssm_b_re", "ssm_b_im", "ssm_c_re", "ssm_c_im", "ssm_log_dt"]
SMALL_SHARDED = ["norm_pre", "norm_post", "ssm_d"]
TRANSPOSED = ["ffn_w_in", "ab_w_in", "ssm_b_re", "ssm_b_im"]
WEIGHTS = ['ada_w', 'ada_b', 'norm_pre', 'norm_post', 'ffn_w_in', 'ffn_w_out', 'ab_w_in', 'pool_w', 'pool_scale', 'sgu_ln_g',
           'sgu_ln_b', 'sgu_w', 'sgu_b', 'ab_w_out', 'ssm_w_in', 'ssm_lam_re', 'ssm_lam_im', 'ssm_b_re', 'ssm_b_im', 'ssm_c_re',
           'ssm_c_im', 'ssm_d', 'ssm_log_dt', 'ssm_w_glu']


def kernel(x, c, ada_w, ada_b, norm_pre, norm_post, ffn_w_in, ffn_w_out, ab_w_in, pool_w, pool_scale, sgu_ln_g, sgu_ln_b, sgu_w, sgu_b, ab_w_out, ssm_w_in, ssm_lam_re, ssm_lam_im, ssm_b_re, ssm_b_im, ssm_c_re, ssm_c_im, ssm_d, ssm_log_dt, ssm_w_glu, loss_target, m_ada_w, m_ada_b, m_norm_pre, m_norm_post, m_ffn_w_in, m_ffn_w_out, m_ab_w_in, m_pool_w, m_pool_scale, m_sgu_ln_g, m_sgu_ln_b, m_sgu_w, m_sgu_b, m_ab_w_out, m_ssm_w_in, m_ssm_lam_re, m_ssm_lam_im, m_ssm_b_re, m_ssm_b_im, m_ssm_c_re, m_ssm_c_im, m_ssm_d, m_ssm_log_dt, m_ssm_w_glu, v_ada_w, v_ada_b, v_norm_pre, v_norm_post, v_ffn_w_in, v_ffn_w_out, v_ab_w_in, v_pool_w, v_pool_scale, v_sgu_ln_g, v_sgu_ln_b, v_sgu_w, v_sgu_b, v_ab_w_out, v_ssm_w_in, v_ssm_lam_re, v_ssm_lam_im, v_ssm_b_re, v_ssm_b_im, v_ssm_c_re, v_ssm_c_im, v_ssm_d, v_ssm_log_dt, v_ssm_w_glu):
    args = locals()
    wts = {n: args[n] for n in WEIGHTS}
    mom = {n: args["m_" + n] for n in WEIGHTS}
    var = {n: args["v_" + n] for n in WEIGHTS}
    for n in TRANSPOSED:
        for t in (wts, mom, var):
            t[n] = jnp.swapaxes(t[n], -1, -2)
    me = 4 * lax.axis_index("x") + 2 * lax.axis_index("y") + lax.axis_index("c")
    s = x.shape[1]
    nd = D // N_DEV

    small_in, small_in_layout = _pack([c, norm_pre, norm_post, ssm_d])
    small_rows = -(-small_in.shape[0] // (8 * LANE)) * 8
    (g_small,) = _all_gather("gather_small", [_pad_rows(small_in, small_rows)])
    g_small = g_small.reshape(N_DEV, -1)
    c_all, npre_g, npost_g, sd_g = [jnp.stack([_unpack(g_small[j], small_in_layout)[i] for j in range(N_DEV)]) for i in range(4)]
    c_all = c_all.reshape(N_DEV, D)
    norm_pre_full = npre_g.transpose(1, 2, 0, 3).reshape(2, 3, D)
    norm_post_full = npost_g.transpose(1, 2, 0, 3).reshape(2, 3, D)
    ssm_d_full = sd_g.transpose(1, 0, 2).reshape(1, D)

    nw = ada_w.shape[-1]
    (mod_g,) = _all_gather("gather_mod", [_mod_part(c_all, ada_w)])
    mod = lax.dynamic_index_in_dim(mod_g, me, axis=2, keepdims=False)
    mod = (mod.transpose(1, 0, 2).reshape(2, N_DEV * nw) + ada_b).reshape(2, 3, 3, D)

    w_in_t = wts["ffn_w_in"]
    shards = [[w_in_t[0, 0]], [ffn_w_out[0, 0]], [wts["ab_w_in"][0], ab_w_out[0]], [w_in_t[0, 1], ffn_w_out[0, 1]],
              [w_in_t[1, 0], ffn_w_out[1, 0]], [ssm_w_in[0], ssm_w_glu[0]], [w_in_t[1, 1], ffn_w_out[1, 1]]]
    same_core = (2, 4, 6)

    def gather_plan(n):
        return lambda me_, peer_, k: [(a, None, a, me_) for a in range(n)] if k in (0, 1) + same_core else []

    def relay_plan(n):
        return lambda me_, peer_, k: [(a, me_ ^ kk, a, me_ ^ kk) for kk in same_core for a in range(n)] if k == 1 else []

    gathers, relays = [], {}
    token = mod_g
    for g, group in enumerate(shards):
        group = [a.astype(BF16) for a in group]
        sems, srcs_thru, lands, token = _exchange_start(
            f"gather_start_{g}", group, [_sds((N_DEV,) + a.shape, BF16) for a in group], gather_plan(len(group)), len(group), token)
        gathers.append((sems, srcs_thru, lands))
    mod = mod + token[0, 0]

    def relay(g, after):
        sems, srcs_thru, lands = gathers[g]
        n = len(lands)
        relays[g] = _exchange_relay(f"gather_relay_{g}", sems, srcs_thru, lands, gather_plan(n), n, relay_plan(n), 3 * n, after)

    def fetch(g, after):
        if g not in relays:
            relay(g, after)
        sems, lands, token = relays[g]
        n = len(lands)
        got = _exchange_wait(f"gather_wait_{g}", sems, None, lands, relay_plan(n), 3 * n, after)
        if 0 < g < len(gathers) - 1:
            relay(g + 1, got[0])
            token = relays[g + 1][2]
        return got, token

    head_sum = jnp.repeat(jnp.eye(NH, LANE, dtype=F32), HD, axis=0)
    mix0 = {"pool_w": pool_w[0], "pool_scale": pool_scale, "sgu_ln_g": sgu_ln_g, "sgu_ln_b": sgu_ln_b, "sgu_w": sgu_w[0],
            "sgu_bt": jnp.pad(sgu_b[0].T, ((0, 0), (0, LANE - NH))), "head_sum": head_sum}
    mix1 = _ssm_params(ssm_lam_re[0], ssm_lam_im[0], ssm_b_re[0], ssm_b_im[0], ssm_c_re[0], ssm_c_im[0], ssm_log_dt[0])
    mix1["ssm_d"] = ssm_d_full

    def weights_of(i, x_in):
        if i == 0:
            (win,), token = fetch(0, x_in)
            cache = []

            def wout_of(z):
                if not cache:
                    cache.append(fetch(1, z)[0][0])
                return cache[0]

            return (win, wout_of), token
        (a, b), token = fetch(i + 1, x_in)
        if i % 3 != 1:
            return (a, lambda z: b), token
        if i == 1:
            return dict(mix0, ab_w_in=a.reshape(-1, D), ab_w_out=b.reshape(D, D)), token
        return dict(mix1, ssm_w_in=a.reshape(D, D), ssm_w_glu=b.transpose(1, 0, 2).reshape(D, -1)), token

    def shard_cols(a):
        r = a.shape[0]
        return a.reshape(r, N_DEV, -1).transpose(1, 0, 2)

    scatter_plan = lambda me_, peer_, k: [(0, peer_, 0, me_), (1, peer_, 1, me_)]
    scatter_plan1 = lambda me_, peer_, k: [(0, peer_, 0, me_)]
    scatters = []
    last_token = [jnp.zeros((8, LANE), F32)]
    pieces, mixer, bundles = {}, {}, {}
    bundle_plan = lambda me_, peer_, k: [(0, None, 0, me_)]

    held = {}

    def on_part(i, tag, part):
        if i != 0 and tag == "w_out":
            held[i] = part
            return last_token[0]
        names, parts, plan = (("ffn_" + tag,), [part], scatter_plan1) if i == 0 else (("ffn_w_out", "ffn_w_in"), [held[i], part], scatter_plan)
        sems, srcs_thru, lands, last_token[0] = _exchange_start(
            f"scatter_start_{i}_{tag}", parts, [_sds(a.shape, BF16) for a in parts], plan, len(parts), last_token[0])
        scatters.append((i, names, plan, sems, srcs_thru, lands))
        return last_token[0]
    mix0_names = ["pool_w", "pool_scale", "sgu_ln_g", "sgu_ln_b", "sgu_w", "sgu_b"]
    mix1_names = ["ssm_lam_re", "ssm_lam_im", "ssm_b_re", "ssm_b_im", "ssm_c_re", "ssm_c_im", "ssm_log_dt", "ssm_d"]

    def start_bundle(tag, arrays):
        flat, layout = _pack(arrays)
        rows = -(-flat.shape[0] // (8 * LANE)) * 8
        plan = gather_plan(1) if tag == "a" else bundle_plan
        sems, srcs_thru, lands, last_token[0] = _exchange_start(
            f"small_start_{tag}", [_pad_rows(flat, rows)], [_sds((N_DEV, rows, LANE))], plan, 1, last_token[0])
        bundles[tag] = (sems, srcs_thru, lands, layout)

    def on_grads(i, extra, dmod_i, dpre_i, dpost_i, loss_row):
        pieces[i] = (dmod_i, dpre_i, dpost_i)
        if i == 4:
            mixer.update({n: extra[n] for n in mix1_names})
        if i == 1:
            mixer.update({n: extra[n] for n in mix0_names})
            rest = range(1, 6)
            start_bundle("a", [jnp.stack([pieces[j][0] for j in rest])] + [jnp.concatenate([pieces[j][k] for j in rest]) for k in (1, 2)]
                         + [mixer[n] for n in mix0_names + mix1_names])
        if i == 0:
            start_bundle("b", [dmod_i, dpre_i, dpost_i, loss_row])
        if i % 3 != 1:
            return last_token[0]
        if i == 1:
            names, parts = ("ab_w_in", "ab_w_out"), [extra["ab_w_in"].reshape(N_DEV, -1, D), extra["ab_w_out"].reshape(N_DEV, nd, D)]
        else:
            names, parts = ("ssm_w_in", "ssm_w_glu"), [extra["ssm_w_in"].reshape(N_DEV, nd, D), shard_cols(extra["ssm_w_glu"])]
        sems, srcs_thru, lands, last_token[0] = _exchange_start(
            f"scatter_start_{i}", parts, [_sds(a.shape, BF16) for a in parts], scatter_plan, 2, last_token[0])
        scatters.append((i, names, scatter_plan, sems, srcs_thru, lands))
        return last_token[0]

    grad_x = _local_step(x[0], loss_target[0], mod, norm_pre_full, norm_post_full, weights_of, on_part, on_grads)

    out_g, out_d, out_m, out_v = {}, {}, {}, {}
    big_out = {}

    def adam_big(name, recv, n, slot=0, after=None):
        c_ = wts[n].shape[-1]
        big_out[n] = _adamw(name, recv.reshape(recv.shape[0], -1, c_), *[t[n].reshape(-1, c_) for t in (wts, mom, var)],
                            slot=slot, prev=big_out.get(n), after=after)
        return big_out[n][0]

    ffn_slot = {0: 0, 2: 1, 3: 2, 5: 3}

    def land_and_update(entries, after):
        for i, names, plan, sems, srcs_thru, lands in entries:
            recv = _exchange_wait(f"scatter_wait_{i}_{names[0]}", sems, srcs_thru, lands, plan, len(names), after)
            for n, r in zip(names, recv):
                after = adam_big(f"adamw_{n}_{i}", r, n, ffn_slot.get(i, 0), after)
        return after

    after = land_and_update([e for e in scatters if e[0] != 0], last_token[0])

    def landed(tag, g_parts):
        layout = bundles[tag][3]
        off, n, shape = layout[0]
        dmods = g_parts.reshape(N_DEV, -1)[:, off:off + n].reshape((N_DEV,) + shape)
        total = _sum_parts(g_parts)
        return dmods, _unpack(total.reshape(-1), layout), total

    def adam_small(n, g, after=None):
        cols = wts[n].shape[-1]
        res = _adamw(f"adamw_{n}", g.reshape(1, -1, cols), *[t[n].reshape(-1, cols) for t in (wts, mom, var)], after=after)
        for o, arr in zip((out_g, out_d, out_m, out_v), res):
            o[n] = arr.reshape(wts[n].shape)
            if n in TRANSPOSED:
                o[n] = jnp.swapaxes(o[n], -1, -2)
        return res[0]

    sems, srcs_thru, lands, _ = bundles["a"]
    sems, lands, _ = _exchange_relay("small_relay_a", sems, srcs_thru, lands, gather_plan(1), 1, relay_plan(1), 3, after)
    (parts_a,) = _exchange_wait("small_wait_a", sems, None, lands, relay_plan(1), 3, after)
    dmods_a, sums_a, after = landed("a", parts_a)
    dmod_a, dpre_a, dpost_a = sums_a[:3]
    small = dict(zip(mix0_names + mix1_names, sums_a[3:]))
    def adam_tiny(name, grads, after):
        view = lambda n, a: a.reshape(-1, wts[n].shape[-1])
        items = [(view(n, g),) + tuple(view(n, t[n]) for t in (wts, mom, var)) for n, g in grads.items()]
        for (n, _), item, res in zip(grads.items(), items, _adamw_many(name, items, after)):
            for o, arr in zip((out_g, out_d, out_m, out_v), (item[0],) + res):
                o[n] = arr.reshape(wts[n].shape)
        return res[0]

    tiny = ["pool_scale", "sgu_ln_g", "sgu_ln_b", "sgu_b", "ssm_lam_re", "ssm_lam_im", "ssm_log_dt"]
    for n in [n for n in mix0_names + mix1_names if n not in tiny and n != "ssm_d"]:
        after = adam_small(n, small[n], after)
    after = adam_tiny("adamw_tiny_mixers", dict({n: small[n] for n in tiny},
                                                ssm_d=lax.dynamic_slice_in_dim(small["ssm_d"], me * nd, nd, axis=1)), after)
    sems, srcs_thru, lands, _ = bundles["b"]
    (parts_b,) = _exchange_wait("small_wait_b", sems, srcs_thru, lands, bundle_plan, 1, after)
    dmods_b, (dmod_b, dpre_b, dpost_b, loss_sum), after = landed("b", parts_b)
    gathered = {"a": dmods_a, "b": dmods_b}
    loss = loss_sum[0, 0]
    own = lambda first, rest: lax.dynamic_slice_in_dim(jnp.concatenate([first, rest]), me * nd, nd, axis=1)
    after = adam_tiny("adamw_tiny_shell", {"ada_b": jnp.concatenate([dmod_b[None], dmod_a]),
                                           "norm_pre": own(dpre_b, dpre_a), "norm_post": own(dpost_b, dpost_a)}, after)

    dmod_all = jnp.concatenate([gathered["b"][:, None], gathered["a"]], axis=1).reshape(N_DEV, 2, N_DEV, nw)
    dmod_mine = lax.dynamic_index_in_dim(dmod_all, me, axis=2, keepdims=False).transpose(1, 0, 2)
    g_ada_w = _ada_w_grad(c_all.T, dmod_mine)
    after = after[0:1, 0:1] + adam_big("adamw_ada_w", g_ada_w[None], "ada_w")[0:1, 0:1]

    land_and_update([e for e in scatters if e[0] == 0], after)
    for n, res in big_out.items():
        for o, arr in zip((out_g, out_d, out_m, out_v), res):
            o[n] = arr.reshape(wts[n].shape)
            if n in TRANSPOSED:
                o[n] = jnp.swapaxes(o[n], -1, -2)

    return (loss, grad_x[None], *[out_g[n] for n in WEIGHTS], *[out_d[n] for n in WEIGHTS],
            *[out_m[n] for n in WEIGHTS], *[out_v[n] for n in WEIGHTS])
```

```python
import functools
import math

import jax
import jax.numpy as jnp
from jax import lax
from jax.experimental import pallas as pl
from jax.experimental.pallas import tpu as pltpu

F32 = jnp.float32
BF16 = jnp.bfloat16
MESH = pl.DeviceIdType.MESH
HIGHEST = lax.Precision.HIGHEST

N_DEV = 8
D = 1024
D_FF = 2816
FSH = 2 * D_FF // N_DEV
EPS = 1e-6
POOL_WINDOWS = (2, 4, 8, 16)
HD = 128
NH = 4
SSM_G, SSM_P, SSM_N = 64, 64, 16
SSM_L = SSM_G * SSM_P
LR, B1, B2, ADAM_EPS, WD, STEP = 0.001, 0.9, 0.999, 1e-08, 0.01, 10
GELU_C = math.sqrt(2.0 / math.pi)
VMEM_LIMIT_BYTES = 48 * 1024 * 1024
LANE = 128


def _pc(body, name, grid, in_specs, out_specs, out_shape, scratch=()):
    return pl.pallas_call(
        body, name=name, grid=grid, in_specs=in_specs, out_specs=out_specs, out_shape=out_shape,
        scratch_shapes=list(scratch),
        compiler_params=pltpu.CompilerParams(dimension_semantics=("arbitrary",) * len(grid),
                                             vmem_limit_bytes=VMEM_LIMIT_BYTES))


def _sds(shape, dtype=F32):
    return jax.ShapeDtypeStruct(tuple(shape), dtype)


def _bf(v):
    return v if v.dtype == BF16 else v.astype(BF16)


def _row_spec(ts, width, col=0):
    return pl.BlockSpec((ts, width), lambda t, _c=col: (t, _c))


def _vec_spec(width, col=0):
    return pl.BlockSpec((1, width), lambda t, _c=col: (0, _c))


def _mm(name, a, b, contract, grid, a_spec, b_spec, o_spec, out_shape, acc_axis=None, after=None):
    dn = (contract, ((), ()))

    def body(a_ref, b_ref, *rest):
        o_ref = rest[-1]
        r = lax.dot_general(_bf(a_ref[...]), _bf(b_ref[...]), dn, preferred_element_type=F32)
        if acc_axis is None:
            o_ref[...] = r.astype(o_ref.dtype)
        else:
            k = pl.program_id(acc_axis)

            @pl.when(k == 0)
            def _():
                o_ref[...] = r

            @pl.when(k > 0)
            def _():
                o_ref[...] += r

    if after is None:
        return _pc(body, name, grid, [a_spec, b_spec], o_spec, out_shape)(a, b)
    return _pc(body, name, grid, [a_spec, b_spec, pl.BlockSpec(memory_space=pl.ANY)], o_spec, out_shape)(a, b, after)


def _mm_sum(name, a, b, ts, after=None):
    nj, s, k = a.shape
    n = b.shape[2]

    def body(a_ref, b_ref, *rest):
        acc = jnp.dot(a_ref[0], b_ref[0], preferred_element_type=F32)
        for j in range(1, nj):
            acc = acc + jnp.dot(a_ref[j], b_ref[j], preferred_element_type=F32)
        rest[-1][...] = acc

    specs = [pl.BlockSpec((nj, ts, k), lambda t: (0, t, 0)), pl.BlockSpec((nj, k, n), lambda t: (0, 0, 0))]
    args = (a, b)
    if after is not None:
        specs, args = specs + [pl.BlockSpec(memory_space=pl.ANY)], args + (after,)
    return _pc(body, name, (s // ts,), specs, pl.BlockSpec((ts, n), lambda t: (t, 0)), _sds((s, n)))(*args)


def _tile(s):
    return min(s, 1024)


def _div_tile(n, cap=1024):
    t = min(n, cap) // LANE * LANE
    while n % t:
        t -= LANE
    return t


def _mm_nn(name, a, b, out_dtype=F32):
    s, k = a.shape
    n = b.shape[1]
    ts, tn = _tile(s), _div_tile(n)
    return _mm(name, a, b, ((1,), (0,)), (n // tn, s // ts),
               pl.BlockSpec((ts, k), lambda j, t: (t, 0)), pl.BlockSpec((k, tn), lambda j, t: (0, j)),
               pl.BlockSpec((ts, tn), lambda j, t: (t, j)), _sds((s, n), out_dtype))


def _mm_nt(name, a, b, out_dtype=F32, after=None):
    s, n = a.shape
    k = b.shape[0]
    ts, tk = _tile(s), _div_tile(k)
    return _mm(name, a, b, ((1,), (1,)), (k // tk, s // ts),
               pl.BlockSpec((ts, n), lambda j, t: (t, 0)), pl.BlockSpec((tk, n), lambda j, t: (j, 0)),
               pl.BlockSpec((ts, tk), lambda j, t: (t, j)), _sds((s, k), out_dtype), after=after)


def _mm_tn(name, a, b, out_dtype=F32, tm=512, tn=512):
    s, m = a.shape
    n = b.shape[1]
    tm, tn = min(m, tm), min(n, tn)
    return _mm(name, a, b, ((0,), (0,)), (m // tm, n // tn),
               pl.BlockSpec((s, tm), lambda i, j: (0, i)), pl.BlockSpec((s, tn), lambda i, j: (0, j)),
               pl.BlockSpec((tm, tn), lambda i, j: (i, j)), _sds((m, n), out_dtype))


def _rstd(v):
    return lax.rsqrt(jnp.mean(v * v, axis=-1, keepdims=True) + EPS)


def _prenorm_fwd(x, g, scale, shift):
    s = x.shape[0]
    ts = min(s, 512)

    def body(x_ref, g_ref, sc_ref, sh_ref, h_ref):
        xv = x_ref[...]
        h_ref[...] = ((xv * _rstd(xv) * g_ref[...]) * (1.0 + sc_ref[...]) + sh_ref[...]).astype(BF16)

    return _pc(body, "prenorm_fwd", (s // ts,), [_row_spec(ts, D)] + [_vec_spec(D)] * 3, _row_spec(ts, D),
               _sds((s, D), BF16))(x, g, scale, shift)


def _postnorm_fwd(x, f, g, gate, rw):
    s = x.shape[0]
    ts = min(s, 512)

    def body(x_ref, f_ref, g_ref, gt_ref, o_ref):
        fv = f_ref[...]
        o_ref[...] = x_ref[...] + (rw * gt_ref[...]) * (fv * _rstd(fv) * g_ref[...])

    return _pc(body, "postnorm_fwd", (s // ts,), [_row_spec(ts, D)] * 2 + [_vec_spec(D)] * 2, _row_spec(ts, D),
               _sds((s, D)))(x, f, g, gate)


def _zero_at_first(first, *refs):
    @pl.when(first)
    def _():
        for ref in refs:
            ref[...] = jnp.zeros_like(ref)


def _acc(ref, first, v):
    @pl.when(first)
    def _():
        ref[...] = v

    @pl.when(jnp.logical_not(first))
    def _():
        ref[...] += v


def _colsum(v):
    return jnp.sum(v, axis=0, keepdims=True)


def _postnorm_bwd(dout, f, g, gate, rw):
    s = dout.shape[0]
    ts = min(s, 512)

    def body(do_ref, f_ref, g_ref, gt_ref, df_ref, dgate_ref, dg_ref):
        first = pl.program_id(0) == 0
        do, fv, gv = do_ref[...], f_ref[...], g_ref[...]
        r = _rstd(fv)
        fn = fv * r
        _acc(dgate_ref, first, rw * _colsum(do * (fn * gv)))
        dy = (rw * gt_ref[...]) * do
        _acc(dg_ref, first, _colsum(dy * fn))
        dfn = dy * gv
        df_ref[...] = (r * (dfn - fn * jnp.mean(dfn * fn, axis=-1, keepdims=True))).astype(BF16)

    return _pc(body, "postnorm_bwd", (s // ts,), [_row_spec(ts, D)] * 2 + [_vec_spec(D)] * 2,
               [_row_spec(ts, D), _vec_spec(D), _vec_spec(D)],
               [_sds((s, D), BF16), _sds((1, D)), _sds((1, D))])(dout, f, g, gate)


def _prenorm_bwd(dout, dh, x, g, scale):
    s = dout.shape[0]
    ts = min(s, 512)

    def body(do_ref, dh_ref, x_ref, g_ref, sc_ref, dx_ref, dsh_ref, dsc_ref, dg_ref):
        first = pl.program_id(0) == 0
        dhv, xv, gv = dh_ref[...], x_ref[...], g_ref[...]
        r = _rstd(xv)
        xn = xv * r
        _acc(dsh_ref, first, _colsum(dhv))
        _acc(dsc_ref, first, _colsum(dhv * (xn * gv)))
        dhp = dhv * (1.0 + sc_ref[...])
        _acc(dg_ref, first, _colsum(dhp * xn))
        dxn = dhp * gv
        dx_ref[...] = do_ref[...] + r * (dxn - xn * jnp.mean(dxn * xn, axis=-1, keepdims=True))

    return _pc(body, "prenorm_bwd", (s // ts,), [_row_spec(ts, D)] * 3 + [_vec_spec(D)] * 2,
               [_row_spec(ts, D)] + [_vec_spec(D)] * 3,
               [_sds((s, D))] + [_sds((1, D))] * 3)(dout, dh, x, g, scale)


def _loss_fwd_bwd(y, tgt):
    s = y.shape[0]
    ts = min(s, 512)
    nt = s // ts

    def body(y_ref, t_ref, loss_ref, dy_ref, acc_ref):
        t = pl.program_id(0)
        e = y_ref[...] - t_ref[...]
        dy_ref[...] = e * (1.0 / D)
        _acc(acc_ref, t == 0, _colsum(e * e))

        @pl.when(t == nt - 1)
        def _():
            loss_ref[...] = jnp.full((1, LANE), 0.5 / D, F32) * jnp.sum(acc_ref[...])

    return _pc(body, "loss", (nt,), [_row_spec(ts, D)] * 2,
               [pl.BlockSpec((1, LANE), lambda t: (0, 0)), _row_spec(ts, D)],
               [_sds((1, LANE)), _sds((s, D))], scratch=[pltpu.VMEM((1, D), F32)])(y, tgt)


def _sigmoid(v):
    return 1.0 / (1.0 + jnp.exp(-v))


def _ffn_in_swiglu(h, win):
    s = h.shape[0]
    ts = _tile(s)
    nt = (((1,), (1,)), ((), ()))

    def body(h_ref, wa_ref, wb_ref, fac_ref, act_ref):
        hv = h_ref[...]
        a = lax.dot_general(hv, wa_ref[...], nt, preferred_element_type=F32)
        b = lax.dot_general(hv, wb_ref[...], nt, preferred_element_type=F32)
        sg = _sigmoid(a)
        silu = a * sg
        fac_ref[0] = (b * (sg * (1.0 + a * (1.0 - sg)))).astype(BF16)
        fac_ref[1] = silu.astype(BF16)
        act_ref[...] = (silu * b).astype(BF16)

    return _pc(body, "ffn_in", (4, s // ts),
               [pl.BlockSpec((ts, D), lambda k, t: (t, 0)), pl.BlockSpec((None, FSH, D), lambda k, t: (k, 0, 0)),
                pl.BlockSpec((None, FSH, D), lambda k, t: (k + 4, 0, 0))],
               [pl.BlockSpec((2, None, ts, FSH), lambda k, t: (0, k, t, 0)), pl.BlockSpec((None, ts, FSH), lambda k, t: (k, t, 0))],
               [_sds((2, 4, s, FSH), BF16), _sds((4, s, FSH), BF16)])(h, win, win)


def _ffn_out_dx_swiglu(df, wout, fac, after):
    s = df.shape[0]
    ts = _tile(s)
    nt = (((1,), (1,)), ((), ()))

    def body(df_ref, w_ref, fac_ref, after_ref, o_ref):
        d = lax.dot_general(df_ref[...], w_ref[...], nt, preferred_element_type=F32)
        o_ref[0] = (d * fac_ref[0]).astype(BF16)
        o_ref[1] = (d * fac_ref[1]).astype(BF16)

    spec = pl.BlockSpec((2, None, ts, FSH), lambda k, t: (0, k, t, 0))
    out = _pc(body, "ffn_out_dx", (4, s // ts),
              [pl.BlockSpec((ts, D), lambda k, t: (t, 0)), pl.BlockSpec((None, FSH, D), lambda k, t: (k, 0, 0)), spec,
               pl.BlockSpec(memory_space=pl.ANY)],
              spec, _sds((2, 4, s, FSH), BF16))(df, wout, fac, after)
    return out.reshape(N_DEV, s, FSH)


def _ffn_fwd(h, win, wout_of):
    s = h.shape[0]
    fac, act = _ffn_in_swiglu(h, win)
    f = _mm_sum("ffn_out", act, wout_of(act).reshape(4, FSH, D), min(s, 512))
    return f, (h, fac, act)


def _ffn_bwd(df, saved, win, wout, send, after):
    h, fac, act = saved
    s = h.shape[0]
    ts = s
    wout = wout.reshape(4, FSH, D)
    dwout = _mm("ffn_out_dw", act, df, ((0,), (0,)), (4, 2),
                pl.BlockSpec((None, s, FSH), lambda k, j: (k, 0, 0)), pl.BlockSpec((s, D // 2), lambda k, j: (0, j)),
                pl.BlockSpec((None, FSH, D // 2), lambda k, j: (k, 0, j)), _sds((4, FSH, D), BF16), after=after)
    dz = _ffn_out_dx_swiglu(df, wout, fac, send("w_out", dwout.reshape(N_DEV, D_FF // N_DEV, D)))
    dwin = _mm("ffn_in_dw", dz, h, ((0,), (0,)), (N_DEV, 2),
               pl.BlockSpec((None, s, FSH), lambda j, i: (j, 0, 0)), pl.BlockSpec((s, D // 2), lambda j, i: (0, i)),
               pl.BlockSpec((None, FSH, D // 2), lambda j, i: (j, 0, i)), _sds((N_DEV, FSH, D), BF16))
    return _mm_sum("ffn_in_dx", dz, win, min(s, 512), after=send("w_in", dwin))


def _shift_rows(v, k, row, s, back):
    if back:
        return jnp.where(row < s - k, pltpu.roll(v, s - k, 0), 0.0)
    return jnp.where(row >= k, pltpu.roll(v, k, 0), 0.0)


def _window_sum(v, w, row, s, back):
    k = 1
    while k < w:
        v = v + _shift_rows(v, k, row, s, back)
        k *= 2
    return v


def _pool_fwd(z, pool_w, pool_scale):
    s = z.shape[0]

    def body(z_ref, w_ref, sc_ref, y_ref, d_ref):
        row = lax.broadcasted_iota(jnp.int32, (s, HD), 0)
        for g, w in enumerate(POOL_WINDOWS):
            sl = slice(g * HD, (g + 1) * HD)
            a = z_ref[:, sl]
            cnt = jnp.minimum(row + 1, w).astype(F32)
            d = (_window_sum(a, w, row, s, False) / cnt - a).astype(BF16)
            d_ref[:, sl] = d
            y = jnp.dot(d, _bf(w_ref[g]), preferred_element_type=F32)
            y_ref[:, sl] = (y * sc_ref[:, sl]).astype(BF16)

    return _pc(body, "pool_fwd", (1,),
               [pl.BlockSpec((s, NH * HD), lambda i: (0, 0)), pl.BlockSpec((NH, HD, HD), lambda i: (0, 0, 0)),
                pl.BlockSpec((1, NH * HD), lambda i: (0, 0))],
               [pl.BlockSpec((s, NH * HD), lambda i: (0, 0))] * 2,
               [_sds((s, NH * HD), BF16)] * 2)(z, pool_w, pool_scale)


def _pool_bwd(dy, d, pool_w, pool_scale):
    s = dy.shape[0]

    def body(dy_ref, d_ref, w_ref, sc_ref, dz_ref, dw_ref, dsc_ref):
        row = lax.broadcasted_iota(jnp.int32, (s, HD), 0)
        for g, w in enumerate(POOL_WINDOWS):
            sl = slice(g * HD, (g + 1) * HD)
            dyg, dg, wg = dy_ref[:, sl], d_ref[:, sl], _bf(w_ref[g])
            yraw = jnp.dot(dg, wg, preferred_element_type=F32)
            dsc_ref[:, sl] = _colsum(dyg * yraw)
            dyr = _bf(dyg * sc_ref[:, sl])
            dw_ref[g] = lax.dot_general(dg, dyr, (((0,), (0,)), ((), ())), preferred_element_type=F32)
            dd = lax.dot_general(dyr, wg, (((1,), (1,)), ((), ())), preferred_element_type=F32)
            cnt = jnp.minimum(row + 1, w).astype(F32)
            dz_ref[:, sl] = (_window_sum(dd / cnt, w, row, s, True) - dd).astype(BF16)

    return _pc(body, "pool_bwd", (1,),
               [pl.BlockSpec((s, NH * HD), lambda i: (0, 0)), pl.BlockSpec((s, NH * HD), lambda i: (0, 0)),
                pl.BlockSpec((NH, HD, HD), lambda i: (0, 0, 0)), pl.BlockSpec((1, NH * HD), lambda i: (0, 0))],
               [pl.BlockSpec((s, NH * HD), lambda i: (0, 0)), pl.BlockSpec((NH, HD, HD), lambda i: (0, 0, 0)),
                pl.BlockSpec((1, NH * HD), lambda i: (0, 0))],
               [_sds((s, NH * HD), BF16), _sds((NH, HD, HD)), _sds((1, NH * HD))])(dy, d, pool_w, pool_scale)


def _gelu(v):
    return 0.5 * v * (1.0 + jnp.tanh(GELU_C * (v + 0.044715 * (v * v * v))))


def _gelu_and_grad(v):
    t = jnp.tanh(GELU_C * (v + 0.044715 * (v * v * v)))
    return 0.5 * v * (1.0 + t), 0.5 * (1.0 + t) + 0.5 * v * (1.0 - t * t) * (GELU_C * (1.0 + 3.0 * 0.044715 * (v * v)))


def _gelu_grad(v):
    return _gelu_and_grad(v)[1]


def _causal_mask():
    return lax.broadcasted_iota(jnp.int32, (HD, HD), 0) >= lax.broadcasted_iota(jnp.int32, (HD, HD), 1)


def _sgu_specs():
    w = NH * HD
    return [pl.BlockSpec((HD, w), lambda c: (c, 1)), pl.BlockSpec((HD, w), lambda c: (c, 2)),
            pl.BlockSpec((1, w), lambda c: (0, 0)), pl.BlockSpec((1, w), lambda c: (0, 0)),
            pl.BlockSpec((NH, HD, HD), lambda c: (0, 0, 0)), pl.BlockSpec((HD, LANE), lambda c: (0, 0))]


def _sgu_head(v, lng_ref, lnb_ref, w_ref, h):
    sl = slice(h * HD, (h + 1) * HD)
    vh = v[:, sl]
    xc = vh - jnp.mean(vh, axis=-1, keepdims=True)
    rs = lax.rsqrt(jnp.mean(xc * xc, axis=-1, keepdims=True) + EPS)
    vhat = xc * rs
    vn = _bf(vhat * lng_ref[:, sl] + lnb_ref[:, sl])
    wc = _bf(jnp.where(_causal_mask(), w_ref[h], 0.0))
    return sl, rs, vhat, vn, wc


def _sgu_fwd(z, ln_g, ln_b, sgu_w, sgu_bt):
    s = z.shape[0]

    def body(zu_ref, zv_ref, lng_ref, lnb_ref, w_ref, bt_ref, y_ref):
        u, v = _gelu(zu_ref[...]), _gelu(zv_ref[...])
        for h in range(NH):
            sl, _, _, vn, wc = _sgu_head(v, lng_ref, lnb_ref, w_ref, h)
            sp = jnp.dot(wc, vn, preferred_element_type=F32) + bt_ref[:, h:h + 1]
            y_ref[:, sl] = (u[:, sl] * sp).astype(BF16)

    return _pc(body, "sgu_fwd", (s // HD,), _sgu_specs(), pl.BlockSpec((HD, NH * HD), lambda c: (c, 0)),
               _sds((s, NH * HD), BF16))(z, z, ln_g, ln_b, sgu_w, sgu_bt)


def _sgu_bwd(z, dy, ln_g, ln_b, sgu_w, sgu_bt, head_sum):
    s = z.shape[0]
    w = NH * HD
    nc = s // HD

    def body(zu_ref, zv_ref, lng_ref, lnb_ref, w_ref, bt_ref, dy_ref, hs_ref,
             dzu_ref, dzv_ref, dlng_ref, dlnb_ref, dw_ref, dbt_ref, dsacc_ref):
        c = pl.program_id(0)
        _zero_at_first(c == 0, dsacc_ref, dw_ref, dlng_ref, dlnb_ref)
        zu, zv = zu_ref[...], zv_ref[...]
        (u, gu), (v, gv) = _gelu_and_grad(zu), _gelu_and_grad(zv)
        dyv = dy_ref[...]
        ds = dyv * u
        dsacc_ref[...] += ds
        for h in range(NH):
            sl, rs, vhat, vn, wc = _sgu_head(v, lng_ref, lnb_ref, w_ref, h)
            sp = jnp.dot(wc, vn, preferred_element_type=F32) + bt_ref[:, h:h + 1]
            dzu_ref[:, sl] = (dyv[:, sl] * sp * gu[:, sl]).astype(BF16)
            dsh = _bf(ds[:, sl])
            dwh = lax.dot_general(dsh, vn, (((1,), (1,)), ((), ())), preferred_element_type=F32)
            dw_ref[h] += jnp.where(_causal_mask(), dwh, 0.0)
            dvn = lax.dot_general(wc, dsh, (((0,), (0,)), ((), ())), preferred_element_type=F32)
            dlng_ref[:, sl] += _colsum(dvn * vhat)
            dlnb_ref[:, sl] += _colsum(dvn)
            dvh = dvn * lng_ref[:, sl]
            dv = rs * (dvh - jnp.mean(dvh, axis=-1, keepdims=True) - vhat * jnp.mean(dvh * vhat, axis=-1, keepdims=True))
            dzv_ref[:, sl] = (dv * gv[:, sl]).astype(BF16)

        @pl.when(c == nc - 1)
        def _():
            dbt_ref[...] = jnp.dot(dsacc_ref[...], hs_ref[...], preferred_element_type=F32, precision=HIGHEST)

    outs = _pc(body, "sgu_bwd", (nc,),
               _sgu_specs() + [pl.BlockSpec((HD, w), lambda c: (c, 1)), pl.BlockSpec((w, LANE), lambda c: (0, 0))],
               [pl.BlockSpec((HD, w), lambda c: (c, 0))] * 2 + [pl.BlockSpec((1, w), lambda c: (0, 0))] * 2
               + [pl.BlockSpec((NH, HD, HD), lambda c: (0, 0, 0)), pl.BlockSpec((HD, LANE), lambda c: (0, 0))],
               [_sds((s, w), BF16)] * 2 + [_sds((1, w))] * 2 + [_sds((NH, HD, HD)), _sds((HD, LANE))],
               scratch=[pltpu.VMEM((HD, w), F32)])(z, z, ln_g, ln_b, sgu_w, sgu_bt, dy, head_sum)
    return outs


def _cmul(ar, ai, br, bi):
    return ar * br - ai * bi, ar * bi + ai * br


def _ssm_prep(lam_re, lam_im, lam_re_rep, lam_im_rep, log_dt, b_re, b_im):
    def disc(lr, li, dt):
        mag = jnp.exp(lr * dt)
        return mag * jnp.cos(li * dt), mag * jnp.sin(li * dt)

    def body(lr_ref, li_ref, lrr_ref, lir_ref, ldt_ref, br_ref, bi_ref, or_ref, oi_ref, bbr_ref, bbi_ref):
        dt = jnp.exp(ldt_ref[...])
        or_ref[...], oi_ref[...] = disc(lr_ref[...], li_ref[...], dt)
        lr, li = lrr_ref[...], lir_ref[...]
        er, ei = disc(lr, li, dt)
        den = lr * lr + li * li
        kr = ((er - 1.0) * lr + ei * li) / den
        ki = (ei * lr - (er - 1.0) * li) / den
        bbr_ref[...], bbi_ref[...] = _cmul(kr, ki, br_ref[...], bi_ref[...])

    small = pl.BlockSpec((SSM_G, SSM_P), lambda i: (0, 0))
    wide = pl.BlockSpec((SSM_G, SSM_P * SSM_N), lambda i: (0, 0))
    col = pl.BlockSpec((SSM_G, 1), lambda i: (0, 0))
    return _pc(body, "ssm_prep", (1,), [small, small, wide, wide, col, wide, wide], [small, small, wide, wide],
               [_sds((SSM_G, SSM_P))] * 2 + [_sds((SSM_G, SSM_P * SSM_N))] * 2)(
        lam_re, lam_im, lam_re_rep, lam_im_rep, log_dt, b_re, b_im)


def _ssm_param_bwd(g_lam_re, g_lam_im, g_bb_re, g_bb_im, lam_re, lam_im, lam_re_rep, lam_im_rep, log_dt, b_re, b_im, seg):
    def body(glr_ref, gli_ref, gbr_ref, gbi_ref, lr_ref, li_ref, lrr_ref, lir_ref, ldt_ref, br_ref, bi_ref, seg_ref,
             dlr_ref, dli_ref, ddt_ref, dbr_ref, dbi_ref):
        dt = jnp.exp(ldt_ref[...])
        lr, li = lrr_ref[...], lir_ref[...]
        mag = jnp.exp(lr * dt)
        er, ei = mag * jnp.cos(li * dt), mag * jnp.sin(li * dt)
        den = lr * lr + li * li
        kr = ((er - 1.0) * lr + ei * li) / den
        ki = (ei * lr - (er - 1.0) * li) / den
        gbr, gbi = gbr_ref[...], gbi_ref[...]
        dbr_ref[...], dbi_ref[...] = _cmul(kr, -ki, gbr, gbi)
        tr, ti = _cmul(br_ref[...], -bi_ref[...], gbr, gbi)
        gkr = jnp.dot(tr, seg_ref[...], preferred_element_type=F32, precision=HIGHEST)
        gki = jnp.dot(ti, seg_ref[...], preferred_element_type=F32, precision=HIGHEST)
        lr, li = lr_ref[...], li_ref[...]
        mag = jnp.exp(lr * dt)
        er, ei = mag * jnp.cos(li * dt), mag * jnp.sin(li * dt)
        den = lr * lr + li * li
        ir, ii = lr / den, -li / den
        kr, ki = _cmul(er - 1.0, ei, ir, ii)
        ar, ai = _cmul(ir, -ii, gkr, gki)
        glr, gli = glr_ref[...] + ar, gli_ref[...] + ai
        qr, qi = _cmul(kr, ki, ir, ii)
        g1r, g1i = _cmul(-qr, qi, gkr, gki)
        g2r, g2i = _cmul(dt * er, -dt * ei, glr, gli)
        dlr_ref[...] = g1r + g2r
        dli_ref[...] = g1i + g2i
        wr, wi = _cmul(lr, li, er, ei)
        g_dt = jnp.sum(wr * glr + wi * gli, axis=-1, keepdims=True)
        ddt_ref[...] = jnp.broadcast_to(dt * g_dt, (SSM_G, LANE))

    small = pl.BlockSpec((SSM_G, SSM_P), lambda i: (0, 0))
    wide = pl.BlockSpec((SSM_G, SSM_P * SSM_N), lambda i: (0, 0))
    col = pl.BlockSpec((SSM_G, 1), lambda i: (0, 0))
    segs = pl.BlockSpec((SSM_P * SSM_N, SSM_P), lambda i: (0, 0))
    return _pc(body, "ssm_param_bwd", (1,), [small, small, wide, wide, small, small, wide, wide, col, wide, wide, segs],
               [small, small, pl.BlockSpec((SSM_G, LANE), lambda i: (0, 0)), wide, wide],
               [_sds((SSM_G, SSM_P))] * 2 + [_sds((SSM_G, LANE))] + [_sds((SSM_G, SSM_P * SSM_N))] * 2)(
        g_lam_re, g_lam_im, g_bb_re, g_bb_im, lam_re, lam_im, lam_re_rep, lam_im_rep, log_dt, b_re, b_im, seg)


SCAN_LANES = 512
SCAN_ROWS = 8


SCAN_GROUPS = SCAN_LANES // SSM_P
SCAN_COLS = SCAN_GROUPS * SSM_N
SCAN_CHUNK = 256


def _ssm_scan(name, v, w_in, lam_re, lam_im, w_out, reverse, states=None, u=None):
    s = v.shape[0]
    ln, rows, ch = SCAN_LANES, SCAN_ROWS, min(SCAN_CHUNK, s)
    nch, ntile = s // ch, ch // rows
    nt_dims = (((1,), (1,)), ((), ()))
    with_sum = states is not None
    tn_dims = (((0,), (0,)), ((), ()))

    def body(*refs):
        v_ref, win_ref, lr_ref, li_ref, wout_ref = refs[:5]
        n_in = 8 if with_sum else 5
        or_ref, oi_ref, y_ref = refs[n_in:n_in + 3]
        br_s, bi_s = refs[n_in + (9 if with_sum else 3):][:2]
        if with_sum:
            mb_s, mc_s = refs[-2:]
            mb_s[...] = jnp.zeros_like(mb_s)
            mc_s[...] = jnp.zeros_like(mc_s)
        l1 = (lr_ref[...], li_ref[...])
        pw = [l1]
        for _ in range(rows - 1):
            pw.append(_cmul(*pw[-1], *l1))
        row = lax.broadcasted_iota(jnp.int32, (rows, ln), 0)
        expo = (rows - row) if reverse else (row + 1)
        pr = jnp.zeros((rows, ln), F32)
        pi = jnp.zeros((rows, ln), F32)
        for e in range(1, rows + 1):
            pr = jnp.where(expo == e, pw[e - 1][0], pr)
            pi = jnp.where(expo == e, pw[e - 1][1], pi)
        lk = {}
        for k in (1, 2, 4):
            keep = (row < rows - k) if reverse else (row >= k)
            lk[k] = (jnp.where(keep, pw[k - 1][0], 0.0), jnp.where(keep, pw[k - 1][1], 0.0))

        def chunk(c, carry):
            q0 = pl.multiple_of(((nch - 1 - c) if reverse else c) * ch, ch)
            b = jnp.dot(_bf(v_ref[pl.ds(q0, ch), :]), win_ref[...], preferred_element_type=F32)
            br_s[...] = b[:, :ln]
            bi_s[...] = b[:, ln:]

            def step(i, carry):
                cr, ci = carry[:2]
                r0 = pl.multiple_of(((ntile - 1 - i) if reverse else i) * rows, rows)
                xr, xi = br_s[pl.ds(r0, rows), :], bi_s[pl.ds(r0, rows), :]
                for k in (1, 2, 4):
                    shift = rows - k if reverse else k
                    ar, ai = _cmul(lk[k][0], lk[k][1], pltpu.roll(xr, shift, 0), pltpu.roll(xi, shift, 0))
                    xr, xi = xr + ar, xi + ai
                ar, ai = _cmul(pr, pi, cr, ci)
                xr, xi = xr + ar, xi + ai
                g0 = pl.multiple_of(q0 + r0, rows)
                or_ref[pl.ds(g0, rows), :] = xr
                oi_ref[pl.ds(g0, rows), :] = xi
                if not with_sum:
                    return (xr[rows - 1:rows], xi[rows - 1:rows]) if not reverse else (xr[0:1], xi[0:1])
                nr = jnp.where(row == rows - 1, cr, pltpu.roll(xr, rows - 1, 0))
                ni = jnp.where(row == rows - 1, ci, pltpu.roll(xi, rows - 1, 0))
                sr, si = refs[5][pl.ds(g0, rows), :], refs[6][pl.ds(g0, rows), :]
                return xr[0:1], xi[0:1], carry[2] + (sr * nr + si * ni), carry[3] + (sr * ni - si * nr)

            carry = lax.fori_loop(0, ntile, step, carry)
            if with_sum:
                rows_c = pl.ds(q0, ch)
                uc, vc = _bf(refs[7][rows_c, :]), _bf(v_ref[rows_c, :])
                for scr, left, (right_re, right_im) in ((mb_s, uc, (or_ref, oi_ref)), (mc_s, vc, (refs[5], refs[6]))):
                    scr[:, :ln] += lax.dot_general(left, _bf(right_re[rows_c, :]), tn_dims, preferred_element_type=F32)
                    scr[:, ln:] += lax.dot_general(left, _bf(right_im[rows_c, :]), tn_dims, preferred_element_type=F32)
            w = wout_ref[...]
            y_ref[pl.ds(q0, ch), :] = (
                lax.dot_general(_bf(or_ref[pl.ds(q0, ch), :]), w[:, :ln], nt_dims, preferred_element_type=F32)
                + lax.dot_general(_bf(oi_ref[pl.ds(q0, ch), :]), w[:, ln:], nt_dims, preferred_element_type=F32))
            return carry

        zero = jnp.zeros((1, ln), F32)
        init = (zero, zero) + ((jnp.zeros((rows, ln), F32),) * 2 if with_sum else ())
        carry = lax.fori_loop(0, nch, chunk, init)
        if with_sum:
            refs[n_in + 3][...] = _colsum(carry[2])
            refs[n_in + 4][...] = _colsum(carry[3])
            row_g = lax.broadcasted_iota(jnp.int32, (SCAN_COLS, LANE), 0) // SSM_N
            lane_g = lax.broadcasted_iota(jnp.int32, (SCAN_COLS, LANE), 1) // SSM_P
            for scr, o_re, o_im in ((mb_s, refs[n_in + 5], refs[n_in + 6]), (mc_s, refs[n_in + 7], refs[n_in + 8])):
                for part, o_ref in enumerate((o_re, o_im)):
                    fold = jnp.zeros((SCAN_COLS, LANE), F32)
                    for cb in range(ln // LANE):
                        fold = fold + jnp.where(2 * cb + lane_g == row_g, scr[:, part * ln + cb * LANE:part * ln + (cb + 1) * LANE], 0.0)
                    o_ref[...] = jnp.where(row_g % 2 == 0, fold, pltpu.roll(fold, SSM_P, 1))

    vec = pl.BlockSpec((1, ln), lambda j: (0, j))
    blk = pl.BlockSpec((s, ln), lambda j: (0, j))
    cols = pl.BlockSpec((s, SCAN_COLS), lambda j: (0, j))
    wspec = pl.BlockSpec((None, SCAN_COLS, 2 * ln), lambda j: (j, 0, 0))
    ins, args = [cols, wspec, vec, vec, wspec], [v, w_in, lam_re, lam_im, w_out]
    outs, shapes = [blk, blk, cols], [_sds((s, SSM_L))] * 2 + [_sds((s, SSM_G * SSM_N))]
    scratch = [pltpu.VMEM((ch, ln), F32)] * 2
    if with_sum:
        own = pl.BlockSpec((None, SCAN_COLS, LANE), lambda j: (j, 0, 0))
        ins, args = ins + [blk, blk, cols], args + list(states) + [u]
        outs = outs + [vec, vec] + [own] * 4
        shapes = shapes + [_sds((1, SSM_L))] * 2 + [_sds((SSM_L // ln, SCAN_COLS, LANE))] * 4
        scratch = scratch + [pltpu.VMEM((SCAN_COLS, 2 * ln), F32)] * 2
    return _pc(body, name, (SSM_L // ln,), ins, outs, shapes, scratch=scratch)(*args)


def _ssm_act_fwd(y, u, d_skip):
    s = y.shape[0]
    ts = min(s, 512)

    def body(y_ref, u_ref, d_ref, o_ref):
        o_ref[...] = _gelu(y_ref[...] + d_ref[...] * u_ref[...]).astype(BF16)

    return _pc(body, "ssm_act_fwd", (s // ts,), [_row_spec(ts, D)] * 2 + [_vec_spec(D)], _row_spec(ts, D),
               _sds((s, D), BF16))(y, u, d_skip)


def _ssm_act_bwd(dg, y, u, d_skip):
    s = y.shape[0]
    ts = min(s, 512)

    def body(dg_ref, y_ref, u_ref, d_ref, dy_ref, dd_ref):
        uv = u_ref[...]
        dy = dg_ref[...] * _gelu_grad(y_ref[...] + d_ref[...] * uv)
        dy_ref[...] = dy.astype(BF16)
        _acc(dd_ref, pl.program_id(0) == 0, _colsum(dy * uv))

    return _pc(body, "ssm_act_bwd", (s // ts,), [_row_spec(ts, D)] * 3 + [_vec_spec(D)], [_row_spec(ts, D), _vec_spec(D)],
               [_sds((s, D), BF16), _sds((1, D))])(dg, y, u, d_skip)


def _axpy(a, b, d_skip):
    s = a.shape[0]
    ts = min(s, 512)

    def body(a_ref, b_ref, d_ref, o_ref):
        o_ref[...] = (a_ref[...] + d_ref[...] * b_ref[...].astype(F32)).astype(BF16)

    return _pc(body, "ssm_du", (s // ts,), [_row_spec(ts, D)] * 2 + [_vec_spec(D)], _row_spec(ts, D),
               _sds((s, D), BF16))(a, b, d_skip)


def _glu_fwd(zz):
    s = zz.shape[0]
    ts = min(s, 512)

    def body(a_ref, b_ref, o_ref):
        o_ref[...] = a_ref[...] * _sigmoid(b_ref[...])

    return _pc(body, "glu_fwd", (s // ts,), [_row_spec(ts, D, 0), _row_spec(ts, D, 1)], _row_spec(ts, D), _sds((s, D)))(zz, zz)


def _glu_bwd(zz, df):
    s = zz.shape[0]
    ts = min(s, 512)

    def body(a_ref, b_ref, df_ref, o_ref):
        sg = _sigmoid(b_ref[...])
        dfv = df_ref[...].astype(F32)
        o_ref[:, :D] = (dfv * sg).astype(BF16)
        o_ref[:, D:] = (dfv * a_ref[...] * sg * (1.0 - sg)).astype(BF16)

    return _pc(body, "glu_bwd", (s // ts,), [_row_spec(ts, D, 0), _row_spec(ts, D, 1), _row_spec(ts, D)],
               _row_spec(ts, 2 * D), _sds((s, 2 * D), BF16))(zz, zz, df)


def _ssm_block_diag(m_re, m_im):
    rows, half = SCAN_COLS, SCAN_LANES
    expand = jnp.tile(jnp.eye(SSM_P, dtype=BF16), (1, SCAN_GROUPS))

    def body(mr_ref, mi_ref, e_ref, o_ref):
        keep = (lax.broadcasted_iota(jnp.int32, (rows, half), 0) // SSM_N
                == lax.broadcasted_iota(jnp.int32, (rows, half), 1) // SSM_P)
        for part, m_ref in enumerate((mr_ref, mi_ref)):
            t = jnp.dot(_bf(m_ref[...]), e_ref[...], preferred_element_type=F32)
            o_ref[:, part * half:(part + 1) * half] = jnp.where(keep, t, 0.0).astype(BF16)

    blk = pl.BlockSpec((rows, SSM_P), lambda q: (q, 0))
    nb = SSM_G // SCAN_GROUPS
    return _pc(body, "ssm_block_diag", (nb,), [blk, blk, pl.BlockSpec((SSM_P, half), lambda q: (0, 0))],
               pl.BlockSpec((None, rows, 2 * half), lambda q: (q, 0, 0)), _sds((nb, rows, 2 * half), BF16))(m_re, m_im, expand)


def _mod_part(c_all, ada_w):
    n = ada_w.shape[-1]

    def body(c_ref, w_ref, o_ref):
        cv = c_ref[...]
        cond = _bf(cv * _sigmoid(cv))
        o_ref[...] = jnp.dot(cond, _bf(w_ref[...]), preferred_element_type=F32)

    return _pc(body, "mod_part", (2,), [pl.BlockSpec((N_DEV, D), lambda l: (0, 0)), pl.BlockSpec((None, D, n), lambda l: (l, 0, 0))],
               pl.BlockSpec((None, N_DEV, n), lambda l: (l, 0, 0)), _sds((2, N_DEV, n)))(c_all, ada_w)


def _ada_w_grad(c_all_t, dmod):
    n = dmod.shape[-1]
    tr = 128

    def body(c_ref, d_ref, o_ref):
        cv = c_ref[...]
        cond = _bf(cv * _sigmoid(cv)).astype(F32)
        dm = _bf(d_ref[...]).astype(F32)
        acc = cond[:, 0:1] * dm[0:1, :]
        for b in range(1, N_DEV):
            acc = acc + cond[:, b:b + 1] * dm[b:b + 1, :]
        o_ref[...] = acc

    return _pc(body, "ada_w_grad", (2, D // tr),
               [pl.BlockSpec((tr, N_DEV), lambda l, t: (t, 0)), pl.BlockSpec((None, N_DEV, n), lambda l, t: (l, 0, 0))],
               pl.BlockSpec((None, tr, n), lambda l, t: (l, t, 0)), _sds((2, D, n)))(c_all_t, dmod)


def _adamw(name, parts, w, m, v, slot=0, prev=None, after=None):
    p, r, c = parts.shape
    tr = r
    while tr * c * 4 > (1 << 20) and tr % 16 == 0:
        tr //= 2
    nt = r // tr

    def body(p_ref, w_ref, m_ref, v_ref, *rest):
        g_ref, d_ref, nm_ref, nv_ref = rest[-4:]
        g = p_ref[0].astype(F32)
        for i in range(1, p):
            g = g + p_ref[i].astype(F32)
        g_ref[...] = g
        d_ref[...], nm_ref[...], nv_ref[...] = _adam_update(g, w_ref[...], m_ref[...], v_ref[...])

    blk = pl.BlockSpec((tr, c), lambda t: (slot * nt + t, 0))
    in_specs = [pl.BlockSpec((p, tr, c), lambda t: (0, t, 0)), blk, blk, blk]
    unread = list(prev or []) + ([after] if after is not None else [])
    return pl.pallas_call(
        body, name=name, grid=(nt,), in_specs=in_specs + [pl.BlockSpec(memory_space=pl.ANY)] * len(unread), out_specs=[blk] * 4,
        out_shape=[_sds(w.shape)] * 4, input_output_aliases={4 + i: i for i in range(4)} if prev else {},
        compiler_params=pltpu.CompilerParams(dimension_semantics=("arbitrary",), vmem_limit_bytes=VMEM_LIMIT_BYTES))(parts, w, m, v, *unread)


def _adam_update(g, w, m, v):
    m2 = B1 * m + (1.0 - B1) * g
    v2 = B2 * v + (1.0 - B2) * (g * g)
    m_hat = m2 / (1.0 - B1 ** STEP)
    v_hat = v2 / (1.0 - B2 ** STEP)
    return -LR * (m_hat / (jnp.sqrt(v_hat) + ADAM_EPS) + WD * w), m2, v2


def _adamw_many(name, items, after):
    n = len(items)

    def body(*refs):
        outs = refs[4 * n + 1:]
        for i in range(n):
            g, w, m, v = (r[...] for r in refs[4 * i:4 * i + 4])
            for o, val in zip(outs[3 * i:3 * i + 3], _adam_update(g, w, m, v)):
                o[...] = val

    full = lambda a: pl.BlockSpec(a.shape, lambda t: (0, 0))
    flat = [a for item in items for a in item]
    res = _pc(body, name, (1,), [full(a) for a in flat] + [pl.BlockSpec(memory_space=pl.ANY)],
              [full(item[1]) for item in items for _ in range(3)],
              [_sds(item[1].shape) for item in items for _ in range(3)])(*flat, after)
    return [tuple(res[3 * i:3 * i + 3]) for i in range(n)]


def _sum_parts(parts):
    p, r, c = parts.shape
    tr = r
    while tr * c * 4 > (1 << 19) and tr % 16 == 0:
        tr //= 2

    def body(p_ref, o_ref):
        g = p_ref[0]
        for i in range(1, p):
            g = g + p_ref[i]
        o_ref[...] = g

    return _pc(body, "sum_parts", (r // tr,), [pl.BlockSpec((p, tr, c), lambda t: (0, t, 0))], pl.BlockSpec((tr, c), lambda t: (t, 0)),
               _sds((r, c)))(parts)


def _place():
    x, y, c = lax.axis_index("x"), lax.axis_index("y"), lax.axis_index("c")
    peers = []
    for k in range(1, N_DEV):
        px = (1 - x) if k & 4 else x
        py = (1 - y) if k & 2 else y
        pc = (1 - c) if k & 1 else c
        peers.append(((px, py, pc), 4 * px + 2 * py + pc))
    return 4 * x + 2 * y + c, peers


def _at(ref, idx):
    return ref if idx is None else ref.at[idx]


def _exchange_copies(plan, n, src_refs, dst_refs, send_sems, recv_sems, local_sems=None, with_arrivals=True):
    me, peers = _place()
    local = [] if local_sems is None else [
        pltpu.make_async_copy(_at(src_refs[si], sx), _at(dst_refs[di], dx), local_sems.at[i])
        for i, (si, sx, di, dx) in enumerate(plan(me, me, 0))]

    def remote(k, i, dev, entry):
        si, sx, di, dx = entry
        return pltpu.make_async_remote_copy(_at(src_refs[si], sx), _at(dst_refs[di], dx), send_sems.at[k * n + i], recv_sems.at[k * n + i],
                                            device_id=dev, device_id_type=MESH)

    sends = [remote(k, i, dev, e) for k, (dev, peer) in enumerate(peers) for i, e in enumerate(plan(me, peer, k + 1))]
    if not with_arrivals:
        return local, sends, []
    arrivals = [remote(k, i, dev, e) for k, (dev, peer) in enumerate(peers) for i, e in enumerate(plan(peer, me, k + 1))]
    return local, sends, arrivals


def _sem_shapes(n_copies, local=True):
    sems = [pltpu.SemaphoreType.DMA(((N_DEV - 1) * n_copies,)), pltpu.SemaphoreType.DMA(((N_DEV - 1) * n_copies,))]
    return sems + [pltpu.SemaphoreType.DMA((n_copies,))] if local else sems


def _exchange(name, srcs, dst_shapes, plan, n_copies):
    ns, nd = len(srcs), len(dst_shapes)

    def body(*refs):
        local, sends, arrivals = _exchange_copies(plan, n_copies, refs[:ns], refs[ns:ns + nd], *refs[ns + nd:])
        for cp in local + sends:
            cp.start()
        for cp in arrivals:
            cp.wait_recv()
        for cp in sends:
            cp.wait_send()
        for cp in local:
            cp.wait()

    any_spec = pl.BlockSpec(memory_space=pl.ANY)
    return pl.pallas_call(
        body, name=name, in_specs=[any_spec] * ns, out_specs=[any_spec] * nd, out_shape=list(dst_shapes),
        scratch_shapes=_sem_shapes(n_copies))(*srcs)


HBM_SPEC = pl.BlockSpec(memory_space=pltpu.HBM)
SEM_SPEC = pl.BlockSpec(memory_space=pltpu.SEMAPHORE)
ANY_SPEC = pl.BlockSpec(memory_space=pl.ANY)
TOKEN_SPEC = pl.BlockSpec(memory_space=pltpu.VMEM)
SIDE_EFFECT = pltpu.SideEffectType.DATAFLOW_SIDE_EFFECTING


def _wait_all(local, sends, arrivals):
    for cp in arrivals:
        cp.wait_recv()
    for cp in sends:
        cp.wait_send()
    for cp in local:
        cp.wait()


def _exchange_start(name, srcs, dst_shapes, plan, n_copies, order):
    ns, nd = len(srcs), len(dst_shapes)
    nb = ns + nd

    def body(*refs):
        local, sends, _ = _exchange_copies(plan, n_copies, refs[:ns], refs[ns:nb], *refs[nb + 1:nb + 4], with_arrivals=False)
        for cp in local + sends:
            cp.start()
        refs[-1][...] = jnp.zeros((8, LANE), F32)

    lands = [pltpu.with_memory_space_constraint(lax.empty(d.shape, d.dtype), pltpu.HBM) for d in dst_shapes]
    srcs = [pltpu.with_memory_space_constraint(a, pltpu.HBM) for a in srcs]
    bufs = srcs + lands
    out = pl.pallas_call(
        body, name=name, in_specs=[HBM_SPEC] * nb + [ANY_SPEC],
        out_specs=[SEM_SPEC] * 3 + [HBM_SPEC] * nb + [TOKEN_SPEC],
        out_shape=_sem_shapes(n_copies) + [pltpu.HBM(a.shape, a.dtype) for a in bufs] + [_sds((8, LANE))],
        input_output_aliases={i: 3 + i for i in range(nb)},
        compiler_params=pltpu.CompilerParams(has_side_effects=SIDE_EFFECT))(*bufs, order)
    return out[:3], out[3:3 + ns], out[3 + ns:3 + nb], out[-1]


def _exchange_relay(name, sems, srcs, lands, plan, n_copies, plan2, n_copies2, after):
    ns, nd = len(srcs), len(lands)
    nb = ns + nd

    def body(*refs):
        land_refs = refs[ns:nb]
        _wait_all(*_exchange_copies(plan, n_copies, refs[:ns], land_refs, *refs[nb:nb + 3]))
        _, sends, _ = _exchange_copies(plan2, n_copies2, land_refs, land_refs, *refs[nb + 4:nb + 6], with_arrivals=False)
        for cp in sends:
            cp.start()
        refs[-1][...] = jnp.zeros((8, LANE), F32)

    out = pl.pallas_call(
        body, name=name, in_specs=[HBM_SPEC] * nb + [SEM_SPEC] * 3 + [ANY_SPEC],
        out_specs=[SEM_SPEC] * 2 + [HBM_SPEC] * nd + [TOKEN_SPEC],
        out_shape=_sem_shapes(n_copies2, local=False) + [pltpu.HBM(a.shape, a.dtype) for a in lands] + [_sds((8, LANE))],
        input_output_aliases={ns + i: 2 + i for i in range(nd)},
        compiler_params=pltpu.CompilerParams(has_side_effects=SIDE_EFFECT))(*srcs, *lands, *sems, after)
    return out[:2], out[2:2 + nd], out[-1]


def _exchange_wait(name, sems, srcs, lands, plan, n_copies, after):
    srcs = [] if srcs is None else list(srcs)
    ns, nd = len(srcs), len(lands)
    nb = ns + nd

    def body(*refs):
        land_refs = refs[ns:nb]
        _wait_all(*_exchange_copies(plan, n_copies, refs[:ns] if ns else land_refs, land_refs, *refs[nb:nb + len(sems)]))

    bufs = srcs + list(lands)
    out = pl.pallas_call(
        body, name=name, in_specs=[HBM_SPEC] * nb + [SEM_SPEC] * len(sems) + [ANY_SPEC],
        out_specs=[HBM_SPEC] * nb, out_shape=[pltpu.HBM(a.shape, a.dtype) for a in bufs],
        input_output_aliases={i: i for i in range(nb)},
        compiler_params=pltpu.CompilerParams(has_side_effects=SIDE_EFFECT))(*bufs, *sems, after)
    return out[ns:]


def _all_gather(name, arrs):
    plan = lambda me, peer, k: [(i, None, i, me) for i in range(len(arrs))]
    return _exchange(name, arrs, [_sds((N_DEV,) + a.shape, a.dtype) for a in arrs], plan, len(arrs))


def _post_pre_fwd(x, f, g_post, gate, rw, g_pre, scale, shift):
    s = x.shape[0]
    ts = min(s, 512)

    def body(x_ref, f_ref, gp_ref, gt_ref, g_ref, sc_ref, sh_ref, xo_ref, h_ref):
        fv = f_ref[...]
        xv = x_ref[...] + (rw * gt_ref[...]) * (fv * _rstd(fv) * gp_ref[...])
        xo_ref[...] = xv
        h_ref[...] = ((xv * _rstd(xv) * g_ref[...]) * (1.0 + sc_ref[...]) + sh_ref[...]).astype(BF16)

    return _pc(body, "post_pre_fwd", (s // ts,), [_row_spec(ts, D)] * 2 + [_vec_spec(D)] * 5, [_row_spec(ts, D)] * 2,
               [_sds((s, D)), _sds((s, D), BF16)])(x, f, g_post, gate, g_pre, scale, shift)


def _pre_post_bwd(dout, dh, x, g_pre, scale, f, g_post, gate, rw):
    s = dout.shape[0]
    ts = min(s, 128)

    def body(do_ref, dh_ref, x_ref, g_ref, sc_ref, f_ref, gp_ref, gt_ref,
             dx_ref, dsh_ref, dsc_ref, dg_ref, df_ref, dgate_ref, dgp_ref):
        _zero_at_first(pl.program_id(0) == 0, dsh_ref, dsc_ref, dg_ref, dgate_ref, dgp_ref)
        dhv, xv, gv = dh_ref[...], x_ref[...], g_ref[...]
        r = _rstd(xv)
        xn = xv * r
        dsh_ref[...] += _colsum(dhv)
        dsc_ref[...] += _colsum(dhv * (xn * gv))
        dhp = dhv * (1.0 + sc_ref[...])
        dg_ref[...] += _colsum(dhp * xn)
        dxn = dhp * gv
        dx = do_ref[...] + r * (dxn - xn * jnp.mean(dxn * xn, axis=-1, keepdims=True))
        dx_ref[...] = dx
        fv, gpv = f_ref[...], gp_ref[...]
        rf = _rstd(fv)
        fn = fv * rf
        dgate_ref[...] += rw * _colsum(dx * (fn * gpv))
        dy = (rw * gt_ref[...]) * dx
        dgp_ref[...] += _colsum(dy * fn)
        dfn = dy * gpv
        df_ref[...] = (rf * (dfn - fn * jnp.mean(dfn * fn, axis=-1, keepdims=True))).astype(BF16)

    rows, vec = _row_spec(ts, D), _vec_spec(D)
    return _pc(body, "pre_post_bwd", (s // ts,), [rows] * 3 + [vec] * 2 + [rows] + [vec] * 2,
               [rows, vec, vec, vec, rows, vec, vec],
               [_sds((s, D))] + [_sds((1, D))] * 3 + [_sds((s, D), BF16)] + [_sds((1, D))] * 2)(
        dout, dh, x, g_pre, scale, f, g_post, gate)


def _mix0_fwd(h, p):
    z = _mm_nt("mix0_in", h, p["ab_w_in"])
    y_a, d = _pool_fwd(z, p["pool_w"], p["pool_scale"])
    y_b = _sgu_fwd(z, p["sgu_ln_g"], p["sgu_ln_b"], p["sgu_w"], p["sgu_bt"])
    ycat = jnp.concatenate([y_a, y_b], axis=1)
    return _mm_nn("mix0_out", ycat, p["ab_w_out"]), (h, z, d, ycat)


def _mix0_bwd(df, saved, p, after):
    h, z, d, ycat = saved
    dycat = _mm_nt("mix0_out_dx", df, p["ab_w_out"], after=after)
    g = {"ab_w_out": _mm_tn("mix0_out_dw", ycat, df, BF16)}
    dz_p, g["pool_w"], g["pool_scale"] = _pool_bwd(dycat, d, p["pool_w"], p["pool_scale"])
    dz_u, dz_v, g["sgu_ln_g"], g["sgu_ln_b"], g["sgu_w"], dbt = _sgu_bwd(
        z, dycat, p["sgu_ln_g"], p["sgu_ln_b"], p["sgu_w"], p["sgu_bt"], p["head_sum"])
    g["sgu_b"] = dbt[:, :NH].T
    dz = jnp.concatenate([dz_p, dz_u, dz_v], axis=1)
    g["ab_w_in"] = _mm_tn("mix0_in_dw", dz, h, BF16)
    return _mm_nn("mix0_in_dx", dz, p["ab_w_in"]), g


def _mix1_fwd(h, p):
    u = _mm_nn("ssm_w_in", h, p["ssm_w_in"])
    x_re, x_im, y = _ssm_scan("ssm_scan_fwd", u, p["wb_bd"], p["lam_bar_re"], p["lam_bar_im"], p["wc_bd"], False)
    g = _ssm_act_fwd(y, u, p["ssm_d"])
    zz = _mm_nn("ssm_glu", g, p["ssm_w_glu"])
    return _glu_fwd(zz), (h, u, x_re, x_im, y, g, zz)


def _mix1_bwd(df, saved, p, after):
    h, u, x_re, x_im, y, g, zz = saved
    gr = {}
    dzz = _glu_bwd(zz, df)
    dg = _mm_nt("ssm_glu_dx", dzz, p["ssm_w_glu"], after=after)
    gr["ssm_w_glu"] = _mm_tn("ssm_glu_dw", g, dzz, BF16)
    dy, gr["ssm_d"] = _ssm_act_bwd(dg, y, u, p["ssm_d"])
    _, _, du_ssm, g_lam_re, g_lam_im, mb_re, mb_im, mc_re, mc_im = _ssm_scan(
        "ssm_scan_bwd", dy, p["wc_bd"], p["lam_bar_re"], -p["lam_bar_im"], p["wb_bd"], True, states=(x_re, x_im), u=u)
    du = _axpy(du_ssm, dy, p["ssm_d"])
    gr["ssm_w_in"] = _mm_tn("ssm_w_in_dw", h, du, BF16)
    dh = _mm_nt("ssm_w_in_dx", du, p["ssm_w_in"])
    per_group = lambda m: m[:, :, :SSM_P].reshape(SSM_G, SSM_N, SSM_P)
    gr["ssm_c_re"] = per_group(mc_re)
    gr["ssm_c_im"] = -per_group(mc_im)
    dlr, dli, ddt, dbr, dbi = _ssm_param_bwd(
        g_lam_re.reshape(SSM_G, SSM_P), g_lam_im.reshape(SSM_G, SSM_P),
        per_group(mb_re).reshape(SSM_G, SSM_N * SSM_P), per_group(mb_im).reshape(SSM_G, SSM_N * SSM_P),
        p["lam_re"], p["lam_im"], p["lam_re_rep"], p["lam_im_rep"], p["log_dt"], p["b_re"], p["b_im"], p["seg"])
    gr["ssm_lam_re"], gr["ssm_lam_im"], gr["ssm_log_dt"] = dlr, dli, ddt[:, 0]
    gr["ssm_b_re"] = dbr.reshape(SSM_G, SSM_N, SSM_P)
    gr["ssm_b_im"] = dbi.reshape(SSM_G, SSM_N, SSM_P)
    return dh, gr


def _ssm_params(lam_re, lam_im, b_re, b_im, c_re, c_im, log_dt):
    wide = lambda b: b.transpose(0, 2, 1).reshape(SSM_G, SSM_N * SSM_P)
    p = {"lam_re": lam_re, "lam_im": lam_im, "log_dt": log_dt.reshape(SSM_G, 1),
         "lam_re_rep": jnp.tile(lam_re, (1, SSM_N)), "lam_im_rep": jnp.tile(lam_im, (1, SSM_N)), "b_re": wide(b_re), "b_im": wide(b_im)}
    lbr, lbi, bbr, bbi = _ssm_prep(lam_re, lam_im, p["lam_re_rep"], p["lam_im_rep"], p["log_dt"], p["b_re"], p["b_im"])
    p["lam_bar_re"], p["lam_bar_im"] = lbr.reshape(1, SSM_L), lbi.reshape(1, SSM_L)
    rows = lambda m: m.reshape(SSM_G * SSM_N, SSM_P)
    p["wb_bd"] = _ssm_block_diag(rows(bbr), rows(bbi))
    p["wc_bd"] = _ssm_block_diag(rows(c_re), rows(-c_im))
    p["seg"] = jnp.tile(jnp.eye(SSM_P, dtype=F32), (SSM_N, 1))
    return p


RES_WEIGHT = (0.5, 1.0, 0.5)


def _local_step(x, tgt, mod, norm_pre, norm_post, weights_of, on_part, on_grads):
    def fns(i, w):
        if i % 3 != 1:
            win, wout_of = w
            return ((lambda h: _ffn_fwd(h, win, wout_of)),
                    (lambda df, sv, after: (_ffn_bwd(df, sv, win, wout_of(None), lambda tag, part: on_part(i, tag, part), after), None)))
        if i == 1:
            return (lambda h: _mix0_fwd(h, w)), (lambda df, sv, after: _mix0_bwd(df, sv, w, after))
        return (lambda h: _mix1_fwd(h, w)), (lambda df, sv, after: _mix1_bwd(df, sv, w, after))

    g_pre = [norm_pre[l, s][None] for l in range(2) for s in range(3)]
    g_post = [norm_post[l, s][None] for l in range(2) for s in range(3)]
    rw = RES_WEIGHT * 2
    mods, saved, bwd = [], [], []
    f = None
    for i in range(6):
        w, token = weights_of(i, x if i == 0 else f)
        fwd, b = fns(i, w)
        m3 = mod[i // 3, i % 3] + token[0:1, 0:1]
        if i == 0:
            h = _prenorm_fwd(x, g_pre[0], m3[1:2], m3[0:1])
        else:
            x, h = _post_pre_fwd(x, f, g_post[i - 1], mods[i - 1][2:3], rw[i - 1], g_pre[i], m3[1:2], m3[0:1])
        f, inner = fwd(h)
        mods.append(m3)
        saved.append((x, f, inner))
        bwd.append(b)
    loss_row, dx = _loss_fwd_bwd(_postnorm_fwd(x, f, g_post[5], mods[5][2:3], rw[5]), tgt)
    df, dgate, dg_post = _postnorm_bwd(dx, f, g_post[5], mods[5][2:3], rw[5])
    token = jnp.zeros((8, LANE), F32)
    for i in reversed(range(6)):
        x_i, _, inner = saved[i]
        dh, extra = bwd[i](df, inner, token)
        if i > 0:
            dx, dshift, dscale, dg_pre, df, dgate_prev, dg_post_prev = _pre_post_bwd(
                dx, dh, x_i, g_pre[i], mods[i][1:2], saved[i - 1][1], g_post[i - 1], mods[i - 1][2:3], rw[i - 1])
        else:
            dx, dshift, dscale, dg_pre = _prenorm_bwd(dx, dh, x_i, g_pre[0], mods[0][1:2])
        token = on_grads(i, extra, jnp.concatenate([dshift, dscale, dgate], axis=0), dg_pre, dg_post, loss_row)
        if i > 0:
            dgate, dg_post = dgate_prev, dg_post_prev
    return dx


def _pad_rows(v, rows):
    return jnp.pad(v, (0, rows * LANE - v.shape[0])).reshape(rows, LANE)


def _pack(parts):
    flat, layout, off = [], [], 0
    for a in parts:
        n = a.size
        padded = -(-n // LANE) * LANE
        flat.append(jnp.pad(a.reshape(-1).astype(F32), (0, padded - n)))
        layout.append((off, n, a.shape))
        off += padded
    return jnp.concatenate(flat), layout


def _unpack(flat, layout):
    return [flat[off:off + n].reshape(shape) for off, n, shape in layout]


SMALL_REPLICATED = ["ada_b", "pool_w", "pool_scale", "sgu_ln_g", "sgu_ln_b", "sgu_w", "sgu_b", "ssm_lam_re", "ssm_lam_im",
                    "ssm_b_re", "ssm_b_im", "ssm_c_re", "ssm_c_im", "ssm_log_dt"]
SMALL_SHARDED = ["norm_pre", "norm_post", "ssm_d"]
TRANSPOSED = ["ffn_w_in", "ab_w_in", "ssm_b_re", "ssm_b_im"]
WEIGHTS = ['ada_w', 'ada_b', 'norm_pre', 'norm_post', 'ffn_w_in', 'ffn_w_out', 'ab_w_in', 'pool_w', 'pool_scale', 'sgu_ln_g',
           'sgu_ln_b', 'sgu_w', 'sgu_b', 'ab_w_out', 'ssm_w_in', 'ssm_lam_re', 'ssm_lam_im', 'ssm_b_re', 'ssm_b_im', 'ssm_c_re',
           'ssm_c_im', 'ssm_d', 'ssm_log_dt', 'ssm_w_glu']


def kernel(x, c, ada_w, ada_b, norm_pre, norm_post, ffn_w_in, ffn_w_out, ab_w_in, pool_w, pool_scale, sgu_ln_g, sgu_ln_b, sgu_w, sgu_b, ab_w_out, ssm_w_in, ssm_lam_re, ssm_lam_im, ssm_b_re, ssm_b_im, ssm_c_re, ssm_c_im, ssm_d, ssm_log_dt, ssm_w_glu, loss_target, m_ada_w, m_ada_b, m_norm_pre, m_norm_post, m_ffn_w_in, m_ffn_w_out, m_ab_w_in, m_pool_w, m_pool_scale, m_sgu_ln_g, m_sgu_ln_b, m_sgu_w, m_sgu_b, m_ab_w_out, m_ssm_w_in, m_ssm_lam_re, m_ssm_lam_im, m_ssm_b_re, m_ssm_b_im, m_ssm_c_re, m_ssm_c_im, m_ssm_d, m_ssm_log_dt, m_ssm_w_glu, v_ada_w, v_ada_b, v_norm_pre, v_norm_post, v_ffn_w_in, v_ffn_w_out, v_ab_w_in, v_pool_w, v_pool_scale, v_sgu_ln_g, v_sgu_ln_b, v_sgu_w, v_sgu_b, v_ab_w_out, v_ssm_w_in, v_ssm_lam_re, v_ssm_lam_im, v_ssm_b_re, v_ssm_b_im, v_ssm_c_re, v_ssm_c_im, v_ssm_d, v_ssm_log_dt, v_ssm_w_glu):
    args = locals()
    wts = {n: args[n] for n in WEIGHTS}
    mom = {n: args["m_" + n] for n in WEIGHTS}
    var = {n: args["v_" + n] for n in WEIGHTS}
    for n in TRANSPOSED:
        for t in (wts, mom, var):
            t[n] = jnp.swapaxes(t[n], -1, -2)
    me = 4 * lax.axis_index("x") + 2 * lax.axis_index("y") + lax.axis_index("c")
    s = x.shape[1]
    nd = D // N_DEV

    small_in, small_in_layout = _pack([c, norm_pre, norm_post, ssm_d])
    small_rows = -(-small_in.shape[0] // (8 * LANE)) * 8
    (g_small,) = _all_gather("gather_small", [_pad_rows(small_in, small_rows)])
    g_small = g_small.reshape(N_DEV, -1)
    c_all, npre_g, npost_g, sd_g = [jnp.stack([_unpack(g_small[j], small_in_layout)[i] for j in range(N_DEV)]) for i in range(4)]
    c_all = c_all.reshape(N_DEV, D)
    norm_pre_full = npre_g.transpose(1, 2, 0, 3).reshape(2, 3, D)
    norm_post_full = npost_g.transpose(1, 2, 0, 3).reshape(2, 3, D)
    ssm_d_full = sd_g.transpose(1, 0, 2).reshape(1, D)

    nw = ada_w.shape[-1]
    (mod_g,) = _all_gather("gather_mod", [_mod_part(c_all, ada_w)])
    mod = lax.dynamic_index_in_dim(mod_g, me, axis=2, keepdims=False)
    mod = (mod.transpose(1, 0, 2).reshape(2, N_DEV * nw) + ada_b).reshape(2, 3, 3, D)

    w_in_t = wts["ffn_w_in"]
    shards = [[w_in_t[0, 0]], [ffn_w_out[0, 0]], [wts["ab_w_in"][0], ab_w_out[0]], [w_in_t[0, 1], ffn_w_out[0, 1]],
              [w_in_t[1, 0], ffn_w_out[1, 0]], [ssm_w_in[0], ssm_w_glu[0]], [w_in_t[1, 1], ffn_w_out[1, 1]]]
    same_core = (2, 4, 6)

    def gather_plan(n):
        return lambda me_, peer_, k: [(a, None, a, me_) for a in range(n)] if k in (0, 1) + same_core else []

    def relay_plan(n):
        return lambda me_, peer_, k: [(a, me_ ^ kk, a, me_ ^ kk) for kk in same_core for a in range(n)] if k == 1 else []

    gathers, relays = [], {}
    token = mod_g
    for g, group in enumerate(shards):
        group = [a.astype(BF16) for a in group]
        sems, srcs_thru, lands, token = _exchange_start(
            f"gather_start_{g}", group, [_sds((N_DEV,) + a.shape, BF16) for a in group], gather_plan(len(group)), len(group), token)
        gathers.append((sems, srcs_thru, lands))
    mod = mod + token[0, 0]

    def relay(g, after):
        sems, srcs_thru, lands = gathers[g]
        n = len(lands)
        relays[g] = _exchange_relay(f"gather_relay_{g}", sems, srcs_thru, lands, gather_plan(n), n, relay_plan(n), 3 * n, after)

    def fetch(g, after):
        if g not in relays:
            relay(g, after)
        sems, lands, token = relays[g]
        n = len(lands)
        got = _exchange_wait(f"gather_wait_{g}", sems, None, lands, relay_plan(n), 3 * n, after)
        if 0 < g < len(gathers) - 1:
            relay(g + 1, got[0])
            token = relays[g + 1][2]
        return got, token

    head_sum = jnp.repeat(jnp.eye(NH, LANE, dtype=F32), HD, axis=0)
    mix0 = {"pool_w": pool_w[0], "pool_scale": pool_scale, "sgu_ln_g": sgu_ln_g, "sgu_ln_b": sgu_ln_b, "sgu_w": sgu_w[0],
            "sgu_bt": jnp.pad(sgu_b[0].T, ((0, 0), (0, LANE - NH))), "head_sum": head_sum}
    mix1 = _ssm_params(ssm_lam_re[0], ssm_lam_im[0], ssm_b_re[0], ssm_b_im[0], ssm_c_re[0], ssm_c_im[0], ssm_log_dt[0])
    mix1["ssm_d"] = ssm_d_full

    def weights_of(i, x_in):
        if i == 0:
            (win,), token = fetch(0, x_in)
            cache = []

            def wout_of(z):
                if not cache:
                    cache.append(fetch(1, z)[0][0])
                return cache[0]

            return (win, wout_of), token
        (a, b), token = fetch(i + 1, x_in)
        if i % 3 != 1:
            return (a, lambda z: b), token
        if i == 1:
            return dict(mix0, ab_w_in=a.reshape(-1, D), ab_w_out=b.reshape(D, D)), token
        return dict(mix1, ssm_w_in=a.reshape(D, D), ssm_w_glu=b.transpose(1, 0, 2).reshape(D, -1)), token

    def shard_cols(a):
        r = a.shape[0]
        return a.reshape(r, N_DEV, -1).transpose(1, 0, 2)

    scatter_plan = lambda me_, peer_, k: [(0, peer_, 0, me_), (1, peer_, 1, me_)]
    scatter_plan1 = lambda me_, peer_, k: [(0, peer_, 0, me_)]
    scatters = []
    last_token = [jnp.zeros((8, LANE), F32)]
    pieces, mixer, bundles = {}, {}, {}
    bundle_plan = lambda me_, peer_, k: [(0, None, 0, me_)]

    held = {}

    def on_part(i, tag, part):
        if i != 0 and tag == "w_out":
            held[i] = part
            return last_token[0]
        names, parts, plan = (("ffn_" + tag,), [part], scatter_plan1) if i == 0 else (("ffn_w_out", "ffn_w_in"), [held[i], part], scatter_plan)
        sems, srcs_thru, lands, last_token[0] = _exchange_start(
            f"scatter_start_{i}_{tag}", parts, [_sds(a.shape, BF16) for a in parts], plan, len(parts), last_token[0])
        scatters.append((i, names, plan, sems, srcs_thru, lands))
        return last_token[0]
    mix0_names = ["pool_w", "pool_scale", "sgu_ln_g", "sgu_ln_b", "sgu_w", "sgu_b"]
    mix1_names = ["ssm_lam_re", "ssm_lam_im", "ssm_b_re", "ssm_b_im", "ssm_c_re", "ssm_c_im", "ssm_log_dt", "ssm_d"]

    def start_bundle(tag, arrays):
        flat, layout = _pack(arrays)
        rows = -(-flat.shape[0] // (8 * LANE)) * 8
        plan = gather_plan(1) if tag == "a" else bundle_plan
        sems, srcs_thru, lands, last_token[0] = _exchange_start(
            f"small_start_{tag}", [_pad_rows(flat, rows)], [_sds((N_DEV, rows, LANE))], plan, 1, last_token[0])
        bundles[tag] = (sems, srcs_thru, lands, layout)

    def on_grads(i, extra, dmod_i, dpre_i, dpost_i, loss_row):
        pieces[i] = (dmod_i, dpre_i, dpost_i)
        if i == 4:
            mixer.update({n: extra[n] for n in mix1_names})
        if i == 1:
            mixer.update({n: extra[n] for n in mix0_names})
            rest = range(1, 6)
            start_bundle("a", [jnp.stack([pieces[j][0] for j in rest])] + [jnp.concatenate([pieces[j][k] for j in rest]) for k in (1, 2)]
                         + [mixer[n] for n in mix0_names + mix1_names])
        if i == 0:
            start_bundle("b", [dmod_i, dpre_i, dpost_i, loss_row])
        if i % 3 != 1:
            return last_token[0]
        if i == 1:
            names, parts = ("ab_w_in", "ab_w_out"), [extra["ab_w_in"].reshape(N_DEV, -1, D), extra["ab_w_out"].reshape(N_DEV, nd, D)]
        else:
            names, parts = ("ssm_w_in", "ssm_w_glu"), [extra["ssm_w_in"].reshape(N_DEV, nd, D), shard_cols(extra["ssm_w_glu"])]
        sems, srcs_thru, lands, last_token[0] = _exchange_start(
            f"scatter_start_{i}", parts, [_sds(a.shape, BF16) for a in parts], scatter_plan, 2, last_token[0])
        scatters.append((i, names, scatter_plan, sems, srcs_thru, lands))
        return last_token[0]

    grad_x = _local_step(x[0], loss_target[0], mod, norm_pre_full, norm_post_full, weights_of, on_part, on_grads)

    out_g, out_d, out_m, out_v = {}, {}, {}, {}
    big_out = {}

    def adam_big(name, recv, n, slot=0, after=None):
        c_ = wts[n].shape[-1]
        big_out[n] = _adamw(name, recv.reshape(recv.shape[0], -1, c_), *[t[n].reshape(-1, c_) for t in (wts, mom, var)],
                            slot=slot, prev=big_out.get(n), after=after)
        return big_out[n][0]

    ffn_slot = {0: 0, 2: 1, 3: 2, 5: 3}

    def land_and_update(entries, after):
        for i, names, plan, sems, srcs_thru, lands in entries:
            recv = _exchange_wait(f"scatter_wait_{i}_{names[0]}", sems, srcs_thru, lands, plan, len(names), after)
            for n, r in zip(names, recv):
                after = adam_big(f"adamw_{n}_{i}", r, n, ffn_slot.get(i, 0), after)
        return after

    after = land_and_update([e for e in scatters if e[0] != 0], last_token[0])

    def landed(tag, g_parts):
        layout = bundles[tag][3]
        off, n, shape = layout[0]
        dmods = g_parts.reshape(N_DEV, -1)[:, off:off + n].reshape((N_DEV,) + shape)
        total = _sum_parts(g_parts)
        return dmods, _unpack(total.reshape(-1), layout), total

    def adam_small(n, g, after=None):
        cols = wts[n].shape[-1]
        res = _adamw(f"adamw_{n}", g.reshape(1, -1, cols), *[t[n].reshape(-1, cols) for t in (wts, mom, var)], after=after)
        for o, arr in zip((out_g, out_d, out_m, out_v), res):
            o[n] = arr.reshape(wts[n].shape)
            if n in TRANSPOSED:
                o[n] = jnp.swapaxes(o[n], -1, -2)
        return res[0]

    sems, srcs_thru, lands, _ = bundles["a"]
    sems, lands, _ = _exchange_relay("small_relay_a", sems, srcs_thru, lands, gather_plan(1), 1, relay_plan(1), 3, after)
    (parts_a,) = _exchange_wait("small_wait_a", sems, None, lands, relay_plan(1), 3, after)
    dmods_a, sums_a, after = landed("a", parts_a)
    dmod_a, dpre_a, dpost_a = sums_a[:3]
    small = dict(zip(mix0_names + mix1_names, sums_a[3:]))
    def adam_tiny(name, grads, after):
        view = lambda n, a: a.reshape(-1, wts[n].shape[-1])
        items = [(view(n, g),) + tuple(view(n, t[n]) for t in (wts, mom, var)) for n, g in grads.items()]
        for (n, _), item, res in zip(grads.items(), items, _adamw_many(name, items, after)):
            for o, arr in zip((out_g, out_d, out_m, out_v), (item[0],) + res):
                o[n] = arr.reshape(wts[n].shape)
        return res[0]

    tiny = ["pool_scale", "sgu_ln_g", "sgu_ln_b", "sgu_b", "ssm_lam_re", "ssm_lam_im", "ssm_log_dt"]
    for n in [n for n in mix0_names + mix1_names if n not in tiny and n != "ssm_d"]:
        after = adam_small(n, small[n], after)
    after = adam_tiny("adamw_tiny_mixers", dict({n: small[n] for n in tiny},
                                                ssm_d=lax.dynamic_slice_in_dim(small["ssm_d"], me * nd, nd, axis=1)), after)
    sems, srcs_thru, lands, _ = bundles["b"]
    (parts_b,) = _exchange_wait("small_wait_b", sems, srcs_thru, lands, bundle_plan, 1, after)
    dmods_b, (dmod_b, dpre_b, dpost_b, loss_sum), after = landed("b", parts_b)
    gathered = {"a": dmods_a, "b": dmods_b}
    loss = loss_sum[0, 0]
    own = lambda first, rest: lax.dynamic_slice_in_dim(jnp.concatenate([first, rest]), me * nd, nd, axis=1)
    after = adam_tiny("adamw_tiny_shell", {"ada_b": jnp.concatenate([dmod_b[None], dmod_a]),
                                           "norm_pre": own(dpre_b, dpre_a), "norm_post": own(dpost_b, dpost_a)}, after)

    dmod_all = jnp.concatenate([gathered["b"][:, None], gathered["a"]], axis=1).reshape(N_DEV, 2, N_DEV, nw)
    dmod_mine = lax.dynamic_index_in_dim(dmod_all, me, axis=2, keepdims=False).transpose(1, 0, 2)
    g_ada_w = _ada_w_grad(c_all.T, dmod_mine)
    after = after[0:1, 0:1] + adam_big("adamw_ada_w", g_ada_w[None], "ada_w")[0:1, 0:1]

    land_and_update([e for e in scatters if e[0] == 0], after)
    for n, res in big_out.items():
        for o, arr in zip((out_g, out_d, out_m, out_v), res):
            o[n] = arr.reshape(wts[n].shape)
            if n in TRANSPOSED:
                o[n] = jnp.swapaxes(o[n], -1, -2)

    return (loss, grad_x[None], *[out_g[n] for n in WEIGHTS], *[out_d[n] for n in WEIGHTS],
            *[out_m[n] for n in WEIGHTS], *[out_v[n] for n in WEIGHTS])
```

```python
import functools
import math

import jax
import jax.numpy as jnp
from jax import lax
from jax.experimental import pallas as pl
from jax.experimental.pallas import tpu as pltpu

F32 = jnp.float32
BF16 = jnp.bfloat16
MESH = pl.DeviceIdType.MESH
HIGHEST = lax.Precision.HIGHEST

N_DEV = 8
D = 1024
D_FF = 2816
FSH = 2 * D_FF // N_DEV
EPS = 1e-6
POOL_WINDOWS = (2, 4, 8, 16)
HD = 128
NH = 4
SSM_G, SSM_P, SSM_N = 64, 64, 16
SSM_L = SSM_G * SSM_P
LR, B1, B2, ADAM_EPS, WD, STEP = 0.001, 0.9, 0.999, 1e-08, 0.01, 10
GELU_C = math.sqrt(2.0 / math.pi)
VMEM_LIMIT_BYTES = 48 * 1024 * 1024
LANE = 128


def _pc(body, name, grid, in_specs, out_specs, out_shape, scratch=()):
    return pl.pallas_call(
        body, name=name, grid=grid, in_specs=in_specs, out_specs=out_specs, out_shape=out_shape,
        scratch_shapes=list(scratch),
        compiler_params=pltpu.CompilerParams(dimension_semantics=("arbitrary",) * len(grid),
                                             vmem_limit_bytes=VMEM_LIMIT_BYTES))


def _sds(shape, dtype=F32):
    return jax.ShapeDtypeStruct(tuple(shape), dtype)


def _bf(v):
    return v if v.dtype == BF16 else v.astype(BF16)


def _row_spec(ts, width, col=0):
    return pl.BlockSpec((ts, width), lambda t, _c=col: (t, _c))


def _vec_spec(width, col=0):
    return pl.BlockSpec((1, width), lambda t, _c=col: (0, _c))


def _mm(name, a, b, contract, grid, a_spec, b_spec, o_spec, out_shape, acc_axis=None, after=None):
    dn = (contract, ((), ()))

    def body(a_ref, b_ref, *rest):
        o_ref = rest[-1]
        r = lax.dot_general(_bf(a_ref[...]), _bf(b_ref[...]), dn, preferred_element_type=F32)
        if acc_axis is None:
            o_ref[...] = r.astype(o_ref.dtype)
        else:
            k = pl.program_id(acc_axis)

            @pl.when(k == 0)
            def _():
                o_ref[...] = r

            @pl.when(k > 0)
            def _():
                o_ref[...] += r

    if after is None:
        return _pc(body, name, grid, [a_spec, b_spec], o_spec, out_shape)(a, b)
    return _pc(body, name, grid, [a_spec, b_spec, pl.BlockSpec(memory_space=pl.ANY)], o_spec, out_shape)(a, b, after)


def _mm_sum(name, a, b, ts, after=None):
    nj, s, k = a.shape
    n = b.shape[2]

    def body(a_ref, b_ref, *rest):
        acc = jnp.dot(a_ref[0], b_ref[0], preferred_element_type=F32)
        for j in range(1, nj):
            acc = acc + jnp.dot(a_ref[j], b_ref[j], preferred_element_type=F32)
        rest[-1][...] = acc

    specs = [pl.BlockSpec((nj, ts, k), lambda t: (0, t, 0)), pl.BlockSpec((nj, k, n), lambda t: (0, 0, 0))]
    args = (a, b)
    if after is not None:
        specs, args = specs + [pl.BlockSpec(memory_space=pl.ANY)], args + (after,)
    return _pc(body, name, (s // ts,), specs, pl.BlockSpec((ts, n), lambda t: (t, 0)), _sds((s, n)))(*args)


def _tile(s):
    return min(s, 1024)


def _div_tile(n, cap=1024):
    t = min(n, cap) // LANE * LANE
    while n % t:
        t -= LANE
    return t


def _mm_nn(name, a, b, out_dtype=F32):
    s, k = a.shape
    n = b.shape[1]
    ts, tn = _tile(s), _div_tile(n)
    return _mm(name, a, b, ((1,), (0,)), (n // tn, s // ts),
               pl.BlockSpec((ts, k), lambda j, t: (t, 0)), pl.BlockSpec((k, tn), lambda j, t: (0, j)),
               pl.BlockSpec((ts, tn), lambda j, t: (t, j)), _sds((s, n), out_dtype))


def _mm_nt(name, a, b, out_dtype=F32, after=None):
    s, n = a.shape
    k = b.shape[0]
    ts, tk = _tile(s), _div_tile(k)
    return _mm(name, a, b, ((1,), (1,)), (k // tk, s // ts),
               pl.BlockSpec((ts, n), lambda j, t: (t, 0)), pl.BlockSpec((tk, n), lambda j, t: (j, 0)),
               pl.BlockSpec((ts, tk), lambda j, t: (t, j)), _sds((s, k), out_dtype), after=after)


def _mm_tn(name, a, b, out_dtype=F32, tm=512, tn=512):
    s, m = a.shape
    n = b.shape[1]
    tm, tn = min(m, tm), min(n, tn)
    return _mm(name, a, b, ((0,), (0,)), (m // tm, n // tn),
               pl.BlockSpec((s, tm), lambda i, j: (0, i)), pl.BlockSpec((s, tn), lambda i, j: (0, j)),
               pl.BlockSpec((tm, tn), lambda i, j: (i, j)), _sds((m, n), out_dtype))


def _rstd(v):
    return lax.rsqrt(jnp.mean(v * v, axis=-1, keepdims=True) + EPS)


V_GPRE, V_SCALE, V_SHIFT, V_GPOST, V_GATE = range(5)


def _vrow(v, r):
    return v[r:r + 1]


def _vblock(i):
    return pl.BlockSpec((None, 8, D), lambda t: (i, 0, 0))


def _head(xv, v):
    return ((xv * _rstd(xv) * _vrow(v, V_GPRE)) * (1.0 + _vrow(v, V_SCALE)) + _vrow(v, V_SHIFT)).astype(BF16)


def _tail(xv, fv, v, rw):
    return xv + (rw * _vrow(v, V_GATE)) * (fv * _rstd(fv) * _vrow(v, V_GPOST))


def _prenorm_fwd(x, vecs, i, after):
    s = x.shape[0]
    ts = min(s, 512)

    def body(x_ref, v_ref, after_ref, h_ref):
        h_ref[...] = _head(x_ref[...], v_ref[...])

    return _pc(body, "prenorm_fwd", (s // ts,), [_row_spec(ts, D), _vblock(i), pl.BlockSpec(memory_space=pl.ANY)], _row_spec(ts, D),
               _sds((s, D), BF16))(x, vecs, after)


def _postnorm_fwd(x, f, vecs, i, rw):
    s = x.shape[0]
    ts = min(s, 512)

    def body(x_ref, f_ref, v_ref, o_ref):
        o_ref[...] = _tail(x_ref[...], f_ref[...], v_ref[...], rw)

    return _pc(body, "postnorm_fwd", (s // ts,), [_row_spec(ts, D)] * 2 + [_vblock(i)], _row_spec(ts, D), _sds((s, D)))(x, f, vecs)


def _post_pre_fwd(x, f, vecs, i, rw_prev, after):
    s = x.shape[0]
    ts = min(s, 512)

    def body(x_ref, f_ref, vp_ref, vc_ref, after_ref, xo_ref, h_ref):
        xv = _tail(x_ref[...], f_ref[...], vp_ref[...], rw_prev)
        xo_ref[...] = xv
        h_ref[...] = _head(xv, vc_ref[...])

    return _pc(body, "post_pre_fwd", (s // ts,),
               [_row_spec(ts, D)] * 2 + [_vblock(i - 1), _vblock(i), pl.BlockSpec(memory_space=pl.ANY)], [_row_spec(ts, D)] * 2,
               [_sds((s, D)), _sds((s, D), BF16)])(x, f, vecs, vecs, after)


def _zero_at_first(first, *refs):
    @pl.when(first)
    def _():
        for ref in refs:
            ref[...] = jnp.zeros_like(ref)


def _acc(ref, first, v):
    @pl.when(first)
    def _():
        ref[...] = v

    @pl.when(jnp.logical_not(first))
    def _():
        ref[...] += v


def _colsum(v):
    return jnp.sum(v, axis=0, keepdims=True)


def _tail_bwd(do, fv, v, rw, dv_ref):
    gv = _vrow(v, V_GPOST)
    r = _rstd(fv)
    fn = fv * r
    dv_ref[V_GATE:V_GATE + 1, :] += rw * _colsum(do * (fn * gv))
    dy = (rw * _vrow(v, V_GATE)) * do
    dv_ref[V_GPOST:V_GPOST + 1, :] += _colsum(dy * fn)
    dfn = dy * gv
    return (r * (dfn - fn * jnp.mean(dfn * fn, axis=-1, keepdims=True))).astype(BF16)


def _head_bwd(do, dhv, xv, v, dv_ref):
    gv = _vrow(v, V_GPRE)
    r = _rstd(xv)
    xn = xv * r
    dv_ref[V_SHIFT:V_SHIFT + 1, :] += _colsum(dhv)
    dv_ref[V_SCALE:V_SCALE + 1, :] += _colsum(dhv * (xn * gv))
    dhp = dhv * (1.0 + _vrow(v, V_SCALE))
    dv_ref[V_GPRE:V_GPRE + 1, :] += _colsum(dhp * xn)
    dxn = dhp * gv
    return do + r * (dxn - xn * jnp.mean(dxn * xn, axis=-1, keepdims=True))


DV_SPEC = pl.BlockSpec((8, D), lambda t: (0, 0))


def _postnorm_bwd(dout, f, vecs, i, rw):
    s = dout.shape[0]
    ts = min(s, 512)

    def body(do_ref, f_ref, v_ref, df_ref, dv_ref):
        _zero_at_first(pl.program_id(0) == 0, dv_ref)
        df_ref[...] = _tail_bwd(do_ref[...], f_ref[...], v_ref[...], rw, dv_ref)

    return _pc(body, "postnorm_bwd", (s // ts,), [_row_spec(ts, D)] * 2 + [_vblock(i)], [_row_spec(ts, D), DV_SPEC],
               [_sds((s, D), BF16), _sds((8, D))])(dout, f, vecs)


def _prenorm_bwd(dout, dh, x, vecs, i):
    s = dout.shape[0]
    ts = min(s, 512)

    def body(do_ref, dh_ref, x_ref, v_ref, dx_ref, dv_ref):
        _zero_at_first(pl.program_id(0) == 0, dv_ref)
        dx_ref[...] = _head_bwd(do_ref[...], dh_ref[...], x_ref[...], v_ref[...], dv_ref)

    return _pc(body, "prenorm_bwd", (s // ts,), [_row_spec(ts, D)] * 3 + [_vblock(i)], [_row_spec(ts, D), DV_SPEC],
               [_sds((s, D)), _sds((8, D))])(dout, dh, x, vecs)


def _pre_post_bwd(dout, dh, x, f_prev, vecs, i, rw_prev):
    s = dout.shape[0]
    ts = min(s, 256)

    def body(do_ref, dh_ref, x_ref, f_ref, vc_ref, vp_ref, dx_ref, df_ref, dv_ref):
        _zero_at_first(pl.program_id(0) == 0, dv_ref)
        dx = _head_bwd(do_ref[...], dh_ref[...], x_ref[...], vc_ref[...], dv_ref)
        dx_ref[...] = dx
        df_ref[...] = _tail_bwd(dx, f_ref[...], vp_ref[...], rw_prev, dv_ref)

    rows = _row_spec(ts, D)
    return _pc(body, "pre_post_bwd", (s // ts,), [rows] * 4 + [_vblock(i), _vblock(i - 1)], [rows, rows, DV_SPEC],
               [_sds((s, D)), _sds((s, D), BF16), _sds((8, D))])(dout, dh, x, f_prev, vecs, vecs)


def _loss_fwd_bwd(y, tgt):
    s = y.shape[0]
    ts = min(s, 512)
    nt = s // ts

    def body(y_ref, t_ref, loss_ref, dy_ref, acc_ref):
        t = pl.program_id(0)
        e = y_ref[...] - t_ref[...]
        dy_ref[...] = e * (1.0 / D)
        _acc(acc_ref, t == 0, _colsum(e * e))

        @pl.when(t == nt - 1)
        def _():
            loss_ref[...] = jnp.full((1, LANE), 0.5 / D, F32) * jnp.sum(acc_ref[...])

    return _pc(body, "loss", (nt,), [_row_spec(ts, D)] * 2,
               [pl.BlockSpec((1, LANE), lambda t: (0, 0)), _row_spec(ts, D)],
               [_sds((1, LANE)), _sds((s, D))], scratch=[pltpu.VMEM((1, D), F32)])(y, tgt)


def _sigmoid(v):
    return 1.0 / (1.0 + jnp.exp(-v))


def _ffn_in_swiglu(h, win):
    s = h.shape[0]
    ts = _tile(s)
    nt = (((1,), (1,)), ((), ()))

    def body(h_ref, wa_ref, wb_ref, fac_ref, act_ref):
        hv = h_ref[...]
        a = lax.dot_general(hv, wa_ref[...], nt, preferred_element_type=F32)
        b = lax.dot_general(hv, wb_ref[...], nt, preferred_element_type=F32)
        sg = _sigmoid(a)
        silu = a * sg
        fac_ref[0] = (b * (sg * (1.0 + a * (1.0 - sg)))).astype(BF16)
        fac_ref[1] = silu.astype(BF16)
        act_ref[...] = (silu * b).astype(BF16)

    return _pc(body, "ffn_in", (4, s // ts),
               [pl.BlockSpec((ts, D), lambda k, t: (t, 0)), pl.BlockSpec((None, FSH, D), lambda k, t: (k, 0, 0)),
                pl.BlockSpec((None, FSH, D), lambda k, t: (k + 4, 0, 0))],
               [pl.BlockSpec((2, None, ts, FSH), lambda k, t: (0, k, t, 0)), pl.BlockSpec((None, ts, FSH), lambda k, t: (k, t, 0))],
               [_sds((2, 4, s, FSH), BF16), _sds((4, s, FSH), BF16)])(h, win, win)


def _ffn_out_dx_swiglu(df, wout, fac, after):
    s = df.shape[0]
    ts = _tile(s)
    nt = (((1,), (1,)), ((), ()))

    def body(df_ref, w_ref, fac_ref, after_ref, o_ref):
        d = lax.dot_general(df_ref[...], w_ref[...], nt, preferred_element_type=F32)
        o_ref[0] = (d * fac_ref[0]).astype(BF16)
        o_ref[1] = (d * fac_ref[1]).astype(BF16)

    spec = pl.BlockSpec((2, None, ts, FSH), lambda k, t: (0, k, t, 0))
    out = _pc(body, "ffn_out_dx", (4, s // ts),
              [pl.BlockSpec((ts, D), lambda k, t: (t, 0)), pl.BlockSpec((None, FSH, D), lambda k, t: (k, 0, 0)), spec,
               pl.BlockSpec(memory_space=pl.ANY)],
              spec, _sds((2, 4, s, FSH), BF16))(df, wout, fac, after)
    return out.reshape(N_DEV, s, FSH)


def _ffn_fwd(h, win, wout_of):
    s = h.shape[0]
    fac, act = _ffn_in_swiglu(h, win)
    f = _mm_sum("ffn_out", act, wout_of(act).reshape(4, FSH, D), min(s, 512))
    return f, (h, fac, act)


def _ffn_bwd(df, saved, win, wout, send, after):
    h, fac, act = saved
    s = h.shape[0]
    ts = s
    wout = wout.reshape(4, FSH, D)
    dwout = _mm("ffn_out_dw", act, df, ((0,), (0,)), (4, 2),
                pl.BlockSpec((None, s, FSH), lambda k, j: (k, 0, 0)), pl.BlockSpec((s, D // 2), lambda k, j: (0, j)),
                pl.BlockSpec((None, FSH, D // 2), lambda k, j: (k, 0, j)), _sds((4, FSH, D), BF16), after=after)
    dz = _ffn_out_dx_swiglu(df, wout, fac, send("w_out", dwout.reshape(N_DEV, D_FF // N_DEV, D)))
    dwin = _mm("ffn_in_dw", dz, h, ((0,), (0,)), (N_DEV, 2),
               pl.BlockSpec((None, s, FSH), lambda j, i: (j, 0, 0)), pl.BlockSpec((s, D // 2), lambda j, i: (0, i)),
               pl.BlockSpec((None, FSH, D // 2), lambda j, i: (j, 0, i)), _sds((N_DEV, FSH, D), BF16))
    return _mm_sum("ffn_in_dx", dz, win, min(s, 512), after=send("w_in", dwin))


def _shift_rows(v, k, row, s, back):
    if back:
        return jnp.where(row < s - k, pltpu.roll(v, s - k, 0), 0.0)
    return jnp.where(row >= k, pltpu.roll(v, k, 0), 0.0)


def _window_sum(v, w, row, s, back):
    k = 1
    while k < w:
        v = v + _shift_rows(v, k, row, s, back)
        k *= 2
    return v


def _pool_fwd(z, pool_w, pool_scale):
    s = z.shape[0]

    def body(z_ref, w_ref, sc_ref, y_ref, d_ref):
        row = lax.broadcasted_iota(jnp.int32, (s, HD), 0)
        for g, w in enumerate(POOL_WINDOWS):
            sl = slice(g * HD, (g + 1) * HD)
            a = z_ref[:, sl]
            cnt = jnp.minimum(row + 1, w).astype(F32)
            d = (_window_sum(a, w, row, s, False) / cnt - a).astype(BF16)
            d_ref[:, sl] = d
            y = jnp.dot(d, _bf(w_ref[g]), preferred_element_type=F32)
            y_ref[:, sl] = (y * sc_ref[:, sl]).astype(BF16)

    return _pc(body, "pool_fwd", (1,),
               [pl.BlockSpec((s, NH * HD), lambda i: (0, 0)), pl.BlockSpec((NH, HD, HD), lambda i: (0, 0, 0)),
                pl.BlockSpec((1, NH * HD), lambda i: (0, 0))],
               [pl.BlockSpec((s, NH * HD), lambda i: (0, 0))] * 2,
               [_sds((s, NH * HD), BF16)] * 2)(z, pool_w, pool_scale)


def _pool_bwd(dy, d, pool_w, pool_scale):
    s = dy.shape[0]

    def body(dy_ref, d_ref, w_ref, sc_ref, dz_ref, dw_ref, dsc_ref):
        row = lax.broadcasted_iota(jnp.int32, (s, HD), 0)
        for g, w in enumerate(POOL_WINDOWS):
            sl = slice(g * HD, (g + 1) * HD)
            dyg, dg, wg = dy_ref[:, sl], d_ref[:, sl], _bf(w_ref[g])
            yraw = jnp.dot(dg, wg, preferred_element_type=F32)
            dsc_ref[:, sl] = _colsum(dyg * yraw)
            dyr = _bf(dyg * sc_ref[:, sl])
            dw_ref[g] = lax.dot_general(dg, dyr, (((0,), (0,)), ((), ())), preferred_element_type=F32)
            dd = lax.dot_general(dyr, wg, (((1,), (1,)), ((), ())), preferred_element_type=F32)
            cnt = jnp.minimum(row + 1, w).astype(F32)
            dz_ref[:, sl] = (_window_sum(dd / cnt, w, row, s, True) - dd).astype(BF16)

    return _pc(body, "pool_bwd", (1,),
               [pl.BlockSpec((s, NH * HD), lambda i: (0, 0)), pl.BlockSpec((s, NH * HD), lambda i: (0, 0)),
                pl.BlockSpec((NH, HD, HD), lambda i: (0, 0, 0)), pl.BlockSpec((1, NH * HD), lambda i: (0, 0))],
               [pl.BlockSpec((s, NH * HD), lambda i: (0, 0)), pl.BlockSpec((NH, HD, HD), lambda i: (0, 0, 0)),
                pl.BlockSpec((1, NH * HD), lambda i: (0, 0))],
               [_sds((s, NH * HD), BF16), _sds((NH, HD, HD)), _sds((1, NH * HD))])(dy, d, pool_w, pool_scale)


def _gelu(v):
    return 0.5 * v * (1.0 + jnp.tanh(GELU_C * (v + 0.044715 * (v * v * v))))


def _gelu_and_grad(v):
    t = jnp.tanh(GELU_C * (v + 0.044715 * (v * v * v)))
    return 0.5 * v * (1.0 + t), 0.5 * (1.0 + t) + 0.5 * v * (1.0 - t * t) * (GELU_C * (1.0 + 3.0 * 0.044715 * (v * v)))


def _gelu_grad(v):
    return _gelu_and_grad(v)[1]


def _causal_mask():
    return lax.broadcasted_iota(jnp.int32, (HD, HD), 0) >= lax.broadcasted_iota(jnp.int32, (HD, HD), 1)


def _sgu_specs():
    w = NH * HD
    return [pl.BlockSpec((HD, w), lambda c: (c, 1)), pl.BlockSpec((HD, w), lambda c: (c, 2)),
            pl.BlockSpec((1, w), lambda c: (0, 0)), pl.BlockSpec((1, w), lambda c: (0, 0)),
            pl.BlockSpec((NH, HD, HD), lambda c: (0, 0, 0)), pl.BlockSpec((HD, LANE), lambda c: (0, 0))]


def _sgu_head(v, lng_ref, lnb_ref, w_ref, h):
    sl = slice(h * HD, (h + 1) * HD)
    vh = v[:, sl]
    xc = vh - jnp.mean(vh, axis=-1, keepdims=True)
    rs = lax.rsqrt(jnp.mean(xc * xc, axis=-1, keepdims=True) + EPS)
    vhat = xc * rs
    vn = _bf(vhat * lng_ref[:, sl] + lnb_ref[:, sl])
    wc = _bf(jnp.where(_causal_mask(), w_ref[h], 0.0))
    return sl, rs, vhat, vn, wc


def _sgu_fwd(z, ln_g, ln_b, sgu_w, sgu_bt):
    s = z.shape[0]

    def body(zu_ref, zv_ref, lng_ref, lnb_ref, w_ref, bt_ref, y_ref):
        u, v = _gelu(zu_ref[...]), _gelu(zv_ref[...])
        for h in range(NH):
            sl, _, _, vn, wc = _sgu_head(v, lng_ref, lnb_ref, w_ref, h)
            sp = jnp.dot(wc, vn, preferred_element_type=F32) + bt_ref[:, h:h + 1]
            y_ref[:, sl] = (u[:, sl] * sp).astype(BF16)

    return _pc(body, "sgu_fwd", (s // HD,), _sgu_specs(), pl.BlockSpec((HD, NH * HD), lambda c: (c, 0)),
               _sds((s, NH * HD), BF16))(z, z, ln_g, ln_b, sgu_w, sgu_bt)


def _sgu_bwd(z, dy, ln_g, ln_b, sgu_w, sgu_bt, head_sum):
    s = z.shape[0]
    w = NH * HD
    nc = s // HD

    def body(zu_ref, zv_ref, lng_ref, lnb_ref, w_ref, bt_ref, dy_ref, hs_ref,
             dzu_ref, dzv_ref, dlng_ref, dlnb_ref, dw_ref, dbt_ref, dsacc_ref):
        c = pl.program_id(0)
        _zero_at_first(c == 0, dsacc_ref, dw_ref, dlng_ref, dlnb_ref)
        zu, zv = zu_ref[...], zv_ref[...]
        (u, gu), (v, gv) = _gelu_and_grad(zu), _gelu_and_grad(zv)
        dyv = dy_ref[...]
        ds = dyv * u
        dsacc_ref[...] += ds
        for h in range(NH):
            sl, rs, vhat, vn, wc = _sgu_head(v, lng_ref, lnb_ref, w_ref, h)
            sp = jnp.dot(wc, vn, preferred_element_type=F32) + bt_ref[:, h:h + 1]
            dzu_ref[:, sl] = (dyv[:, sl] * sp * gu[:, sl]).astype(BF16)
            dsh = _bf(ds[:, sl])
            dwh = lax.dot_general(dsh, vn, (((1,), (1,)), ((), ())), preferred_element_type=F32)
            dw_ref[h] += jnp.where(_causal_mask(), dwh, 0.0)
            dvn = lax.dot_general(wc, dsh, (((0,), (0,)), ((), ())), preferred_element_type=F32)
            dlng_ref[:, sl] += _colsum(dvn * vhat)
            dlnb_ref[:, sl] += _colsum(dvn)
            dvh = dvn * lng_ref[:, sl]
            dv = rs * (dvh - jnp.mean(dvh, axis=-1, keepdims=True) - vhat * jnp.mean(dvh * vhat, axis=-1, keepdims=True))
            dzv_ref[:, sl] = (dv * gv[:, sl]).astype(BF16)

        @pl.when(c == nc - 1)
        def _():
            dbt_ref[...] = jnp.dot(dsacc_ref[...], hs_ref[...], preferred_element_type=F32, precision=HIGHEST)

    outs = _pc(body, "sgu_bwd", (nc,),
               _sgu_specs() + [pl.BlockSpec((HD, w), lambda c: (c, 1)), pl.BlockSpec((w, LANE), lambda c: (0, 0))],
               [pl.BlockSpec((HD, w), lambda c: (c, 0))] * 2 + [pl.BlockSpec((1, w), lambda c: (0, 0))] * 2
               + [pl.BlockSpec((NH, HD, HD), lambda c: (0, 0, 0)), pl.BlockSpec((HD, LANE), lambda c: (0, 0))],
               [_sds((s, w), BF16)] * 2 + [_sds((1, w))] * 2 + [_sds((NH, HD, HD)), _sds((HD, LANE))],
               scratch=[pltpu.VMEM((HD, w), F32)])(z, z, ln_g, ln_b, sgu_w, sgu_bt, dy, head_sum)
    return outs


def _cmul(ar, ai, br, bi):
    return ar * br - ai * bi, ar * bi + ai * br


def _ssm_prep(lam_re, lam_im, lam_re_rep, lam_im_rep, log_dt, b_re, b_im):
    def disc(lr, li, dt):
        mag = jnp.exp(lr * dt)
        return mag * jnp.cos(li * dt), mag * jnp.sin(li * dt)

    def body(lr_ref, li_ref, lrr_ref, lir_ref, ldt_ref, br_ref, bi_ref, or_ref, oi_ref, bbr_ref, bbi_ref):
        dt = jnp.exp(ldt_ref[...])
        or_ref[...], oi_ref[...] = disc(lr_ref[...], li_ref[...], dt)
        lr, li = lrr_ref[...], lir_ref[...]
        er, ei = disc(lr, li, dt)
        den = lr * lr + li * li
        kr = ((er - 1.0) * lr + ei * li) / den
        ki = (ei * lr - (er - 1.0) * li) / den
        bbr_ref[...], bbi_ref[...] = _cmul(kr, ki, br_ref[...], bi_ref[...])

    small = pl.BlockSpec((SSM_G, SSM_P), lambda i: (0, 0))
    wide = pl.BlockSpec((SSM_G, SSM_P * SSM_N), lambda i: (0, 0))
    col = pl.BlockSpec((SSM_G, 1), lambda i: (0, 0))
    return _pc(body, "ssm_prep", (1,), [small, small, wide, wide, col, wide, wide], [small, small, wide, wide],
               [_sds((SSM_G, SSM_P))] * 2 + [_sds((SSM_G, SSM_P * SSM_N))] * 2)(
        lam_re, lam_im, lam_re_rep, lam_im_rep, log_dt, b_re, b_im)


def _ssm_param_bwd(g_lam_re, g_lam_im, g_bb_re, g_bb_im, lam_re, lam_im, lam_re_rep, lam_im_rep, log_dt, b_re, b_im, seg):
    def body(glr_ref, gli_ref, gbr_ref, gbi_ref, lr_ref, li_ref, lrr_ref, lir_ref, ldt_ref, br_ref, bi_ref, seg_ref,
             dlr_ref, dli_ref, ddt_ref, dbr_ref, dbi_ref):
        dt = jnp.exp(ldt_ref[...])
        lr, li = lrr_ref[...], lir_ref[...]
        mag = jnp.exp(lr * dt)
        er, ei = mag * jnp.cos(li * dt), mag * jnp.sin(li * dt)
        den = lr * lr + li * li
        kr = ((er - 1.0) * lr + ei * li) / den
        ki = (ei * lr - (er - 1.0) * li) / den
        gbr, gbi = gbr_ref[...], gbi_ref[...]
        dbr_ref[...], dbi_ref[...] = _cmul(kr, -ki, gbr, gbi)
        tr, ti = _cmul(br_ref[...], -bi_ref[...], gbr, gbi)
        gkr = jnp.dot(tr, seg_ref[...], preferred_element_type=F32, precision=HIGHEST)
        gki = jnp.dot(ti, seg_ref[...], preferred_element_type=F32, precision=HIGHEST)
        lr, li = lr_ref[...], li_ref[...]
        mag = jnp.exp(lr * dt)
        er, ei = mag * jnp.cos(li * dt), mag * jnp.sin(li * dt)
        den = lr * lr + li * li
        ir, ii = lr / den, -li / den
        kr, ki = _cmul(er - 1.0, ei, ir, ii)
        ar, ai = _cmul(ir, -ii, gkr, gki)
        glr, gli = glr_ref[...] + ar, gli_ref[...] + ai
        qr, qi = _cmul(kr, ki, ir, ii)
        g1r, g1i = _cmul(-qr, qi, gkr, gki)
        g2r, g2i = _cmul(dt * er, -dt * ei, glr, gli)
        dlr_ref[...] = g1r + g2r
        dli_ref[...] = g1i + g2i
        wr, wi = _cmul(lr, li, er, ei)
        g_dt = jnp.sum(wr * glr + wi * gli, axis=-1, keepdims=True)
        ddt_ref[...] = jnp.broadcast_to(dt * g_dt, (SSM_G, LANE))

    small = pl.BlockSpec((SSM_G, SSM_P), lambda i: (0, 0))
    wide = pl.BlockSpec((SSM_G, SSM_P * SSM_N), lambda i: (0, 0))
    col = pl.BlockSpec((SSM_G, 1), lambda i: (0, 0))
    segs = pl.BlockSpec((SSM_P * SSM_N, SSM_P), lambda i: (0, 0))
    return _pc(body, "ssm_param_bwd", (1,), [small, small, wide, wide, small, small, wide, wide, col, wide, wide, segs],
               [small, small, pl.BlockSpec((SSM_G, LANE), lambda i: (0, 0)), wide, wide],
               [_sds((SSM_G, SSM_P))] * 2 + [_sds((SSM_G, LANE))] + [_sds((SSM_G, SSM_P * SSM_N))] * 2)(
        g_lam_re, g_lam_im, g_bb_re, g_bb_im, lam_re, lam_im, lam_re_rep, lam_im_rep, log_dt, b_re, b_im, seg)


SCAN_LANES = 512
SCAN_ROWS = 8


SCAN_GROUPS = SCAN_LANES // SSM_P
SCAN_COLS = SCAN_GROUPS * SSM_N
SCAN_CHUNK = 256


def _ssm_scan(name, v, w_in, lam_re, lam_im, w_out, reverse, states=None, u=None):
    s = v.shape[0]
    ln, rows, ch = SCAN_LANES, SCAN_ROWS, min(SCAN_CHUNK, s)
    nch, ntile = s // ch, ch // rows
    nt_dims = (((1,), (1,)), ((), ()))
    with_sum = states is not None
    tn_dims = (((0,), (0,)), ((), ()))

    def body(*refs):
        v_ref, win_ref, lr_ref, li_ref, wout_ref = refs[:5]
        n_in = 8 if with_sum else 5
        or_ref, oi_ref, y_ref = refs[n_in:n_in + 3]
        br_s, bi_s = refs[n_in + (9 if with_sum else 3):][:2]
        if with_sum:
            mb_s, mc_s = refs[-2:]
            mb_s[...] = jnp.zeros_like(mb_s)
            mc_s[...] = jnp.zeros_like(mc_s)
        l1 = (lr_ref[...], li_ref[...])
        pw = [l1]
        for _ in range(rows - 1):
            pw.append(_cmul(*pw[-1], *l1))
        row = lax.broadcasted_iota(jnp.int32, (rows, ln), 0)
        expo = (rows - row) if reverse else (row + 1)
        pr = jnp.zeros((rows, ln), F32)
        pi = jnp.zeros((rows, ln), F32)
        for e in range(1, rows + 1):
            pr = jnp.where(expo == e, pw[e - 1][0], pr)
            pi = jnp.where(expo == e, pw[e - 1][1], pi)
        lk = {}
        for k in (1, 2, 4):
            keep = (row < rows - k) if reverse else (row >= k)
            lk[k] = (jnp.where(keep, pw[k - 1][0], 0.0), jnp.where(keep, pw[k - 1][1], 0.0))

        def chunk(c, carry):
            q0 = pl.multiple_of(((nch - 1 - c) if reverse else c) * ch, ch)
            b = jnp.dot(_bf(v_ref[pl.ds(q0, ch), :]), win_ref[...], preferred_element_type=F32)
            br_s[...] = b[:, :ln]
            bi_s[...] = b[:, ln:]

            def step(i, carry):
                cr, ci = carry[:2]
                r0 = pl.multiple_of(((ntile - 1 - i) if reverse else i) * rows, rows)
                xr, xi = br_s[pl.ds(r0, rows), :], bi_s[pl.ds(r0, rows), :]
                for k in (1, 2, 4):
                    shift = rows - k if reverse else k
                    ar, ai = _cmul(lk[k][0], lk[k][1], pltpu.roll(xr, shift, 0), pltpu.roll(xi, shift, 0))
                    xr, xi = xr + ar, xi + ai
                ar, ai = _cmul(pr, pi, cr, ci)
                xr, xi = xr + ar, xi + ai
                g0 = pl.multiple_of(q0 + r0, rows)
                or_ref[pl.ds(g0, rows), :] = xr
                oi_ref[pl.ds(g0, rows), :] = xi
                if not with_sum:
                    return (xr[rows - 1:rows], xi[rows - 1:rows]) if not reverse else (xr[0:1], xi[0:1])
                nr = jnp.where(row == rows - 1, cr, pltpu.roll(xr, rows - 1, 0))
                ni = jnp.where(row == rows - 1, ci, pltpu.roll(xi, rows - 1, 0))
                sr, si = refs[5][pl.ds(g0, rows), :], refs[6][pl.ds(g0, rows), :]
                return xr[0:1], xi[0:1], carry[2] + (sr * nr + si * ni), carry[3] + (sr * ni - si * nr)

            carry = lax.fori_loop(0, ntile, step, carry)
            if with_sum:
                rows_c = pl.ds(q0, ch)
                uc, vc = _bf(refs[7][rows_c, :]), _bf(v_ref[rows_c, :])
                for scr, left, (right_re, right_im) in ((mb_s, uc, (or_ref, oi_ref)), (mc_s, vc, (refs[5], refs[6]))):
                    scr[:, :ln] += lax.dot_general(left, _bf(right_re[rows_c, :]), tn_dims, preferred_element_type=F32)
                    scr[:, ln:] += lax.dot_general(left, _bf(right_im[rows_c, :]), tn_dims, preferred_element_type=F32)
            w = wout_ref[...]
            y_ref[pl.ds(q0, ch), :] = (
                lax.dot_general(_bf(or_ref[pl.ds(q0, ch), :]), w[:, :ln], nt_dims, preferred_element_type=F32)
                + lax.dot_general(_bf(oi_ref[pl.ds(q0, ch), :]), w[:, ln:], nt_dims, preferred_element_type=F32))
            return carry

        zero = jnp.zeros((1, ln), F32)
        init = (zero, zero) + ((jnp.zeros((rows, ln), F32),) * 2 if with_sum else ())
        carry = lax.fori_loop(0, nch, chunk, init)
        if with_sum:
            refs[n_in + 3][...] = _colsum(carry[2])
            refs[n_in + 4][...] = _colsum(carry[3])
            row_g = lax.broadcasted_iota(jnp.int32, (SCAN_COLS, LANE), 0) // SSM_N
            lane_g = lax.broadcasted_iota(jnp.int32, (SCAN_COLS, LANE), 1) // SSM_P
            for scr, o_re, o_im in ((mb_s, refs[n_in + 5], refs[n_in + 6]), (mc_s, refs[n_in + 7], refs[n_in + 8])):
                for part, o_ref in enumerate((o_re, o_im)):
                    fold = jnp.zeros((SCAN_COLS, LANE), F32)
                    for cb in range(ln // LANE):
                        fold = fold + jnp.where(2 * cb + lane_g == row_g, scr[:, part * ln + cb * LANE:part * ln + (cb + 1) * LANE], 0.0)
                    o_ref[...] = jnp.where(row_g % 2 == 0, fold, pltpu.roll(fold, SSM_P, 1))

    vec = pl.BlockSpec((1, ln), lambda j: (0, j))
    blk = pl.BlockSpec((s, ln), lambda j: (0, j))
    cols = pl.BlockSpec((s, SCAN_COLS), lambda j: (0, j))
    wspec = pl.BlockSpec((None, SCAN_COLS, 2 * ln), lambda j: (j, 0, 0))
    ins, args = [cols, wspec, vec, vec, wspec], [v, w_in, lam_re, lam_im, w_out]
    outs, shapes = [blk, blk, cols], [_sds((s, SSM_L))] * 2 + [_sds((s, SSM_G * SSM_N))]
    scratch = [pltpu.VMEM((ch, ln), F32)] * 2
    if with_sum:
        own = pl.BlockSpec((None, SCAN_COLS, LANE), lambda j: (j, 0, 0))
        ins, args = ins + [blk, blk, cols], args + list(states) + [u]
        outs = outs + [vec, vec] + [own] * 4
        shapes = shapes + [_sds((1, SSM_L))] * 2 + [_sds((SSM_L // ln, SCAN_COLS, LANE))] * 4
        scratch = scratch + [pltpu.VMEM((SCAN_COLS, 2 * ln), F32)] * 2
    return _pc(body, name, (SSM_L // ln,), ins, outs, shapes, scratch=scratch)(*args)


def _ssm_act_fwd(y, u, d_skip):
    s = y.shape[0]
    ts = min(s, 512)

    def body(y_ref, u_ref, d_ref, o_ref):
        o_ref[...] = _gelu(y_ref[...] + d_ref[...] * u_ref[...]).astype(BF16)

    return _pc(body, "ssm_act_fwd", (s // ts,), [_row_spec(ts, D)] * 2 + [_vec_spec(D)], _row_spec(ts, D),
               _sds((s, D), BF16))(y, u, d_skip)


def _ssm_act_bwd(dg, y, u, d_skip):
    s = y.shape[0]
    ts = min(s, 512)

    def body(dg_ref, y_ref, u_ref, d_ref, dy_ref, dd_ref):
        uv = u_ref[...]
        dy = dg_ref[...] * _gelu_grad(y_ref[...] + d_ref[...] * uv)
        dy_ref[...] = dy.astype(BF16)
        _acc(dd_ref, pl.program_id(0) == 0, _colsum(dy * uv))

    return _pc(body, "ssm_act_bwd", (s // ts,), [_row_spec(ts, D)] * 3 + [_vec_spec(D)], [_row_spec(ts, D), _vec_spec(D)],
               [_sds((s, D), BF16), _sds((1, D))])(dg, y, u, d_skip)


def _axpy(a, b, d_skip):
    s = a.shape[0]
    ts = min(s, 512)

    def body(a_ref, b_ref, d_ref, o_ref):
        o_ref[...] = (a_ref[...] + d_ref[...] * b_ref[...].astype(F32)).astype(BF16)

    return _pc(body, "ssm_du", (s // ts,), [_row_spec(ts, D)] * 2 + [_vec_spec(D)], _row_spec(ts, D),
               _sds((s, D), BF16))(a, b, d_skip)


def _glu_fwd(zz):
    s = zz.shape[0]
    ts = min(s, 512)

    def body(a_ref, b_ref, o_ref):
        o_ref[...] = a_ref[...] * _sigmoid(b_ref[...])

    return _pc(body, "glu_fwd", (s // ts,), [_row_spec(ts, D, 0), _row_spec(ts, D, 1)], _row_spec(ts, D), _sds((s, D)))(zz, zz)


def _glu_bwd(zz, df):
    s = zz.shape[0]
    ts = min(s, 512)

    def body(a_ref, b_ref, df_ref, o_ref):
        sg = _sigmoid(b_ref[...])
        dfv = df_ref[...].astype(F32)
        o_ref[:, :D] = (dfv * sg).astype(BF16)
        o_ref[:, D:] = (dfv * a_ref[...] * sg * (1.0 - sg)).astype(BF16)

    return _pc(body, "glu_bwd", (s // ts,), [_row_spec(ts, D, 0), _row_spec(ts, D, 1), _row_spec(ts, D)],
               _row_spec(ts, 2 * D), _sds((s, 2 * D), BF16))(zz, zz, df)


def _ssm_block_diag(m_re, m_im):
    rows, half = SCAN_COLS, SCAN_LANES
    expand = jnp.tile(jnp.eye(SSM_P, dtype=BF16), (1, SCAN_GROUPS))

    def body(mr_ref, mi_ref, e_ref, o_ref):
        keep = (lax.broadcasted_iota(jnp.int32, (rows, half), 0) // SSM_N
                == lax.broadcasted_iota(jnp.int32, (rows, half), 1) // SSM_P)
        for part, m_ref in enumerate((mr_ref, mi_ref)):
            t = jnp.dot(_bf(m_ref[...]), e_ref[...], preferred_element_type=F32)
            o_ref[:, part * half:(part + 1) * half] = jnp.where(keep, t, 0.0).astype(BF16)

    blk = pl.BlockSpec((rows, SSM_P), lambda q: (q, 0))
    nb = SSM_G // SCAN_GROUPS
    return _pc(body, "ssm_block_diag", (nb,), [blk, blk, pl.BlockSpec((SSM_P, half), lambda q: (0, 0))],
               pl.BlockSpec((None, rows, 2 * half), lambda q: (q, 0, 0)), _sds((nb, rows, 2 * half), BF16))(m_re, m_im, expand)


def _mod_part(c_all, ada_w):
    n = ada_w.shape[-1]

    def body(c_ref, w_ref, o_ref):
        cv = c_ref[...]
        cond = _bf(cv * _sigmoid(cv))
        o_ref[...] = jnp.dot(cond, _bf(w_ref[...]), preferred_element_type=F32)

    return _pc(body, "mod_part", (2,), [pl.BlockSpec((N_DEV, D), lambda l: (0, 0)), pl.BlockSpec((None, D, n), lambda l: (l, 0, 0))],
               pl.BlockSpec((None, N_DEV, n), lambda l: (l, 0, 0)), _sds((2, N_DEV, n)))(c_all, ada_w)


def _ada_w_grad(c_all_t, dmod):
    n = dmod.shape[-1]
    tr = 128

    def body(c_ref, d_ref, o_ref):
        cv = c_ref[...]
        cond = _bf(cv * _sigmoid(cv)).astype(F32)
        dm = _bf(d_ref[...]).astype(F32)
        acc = cond[:, 0:1] * dm[0:1, :]
        for b in range(1, N_DEV):
            acc = acc + cond[:, b:b + 1] * dm[b:b + 1, :]
        o_ref[...] = acc

    return _pc(body, "ada_w_grad", (2, D // tr),
               [pl.BlockSpec((tr, N_DEV), lambda l, t: (t, 0)), pl.BlockSpec((None, N_DEV, n), lambda l, t: (l, 0, 0))],
               pl.BlockSpec((None, tr, n), lambda l, t: (l, t, 0)), _sds((2, D, n)))(c_all_t, dmod)


def _adamw(name, parts, w, m, v, slot=0, prev=None, after=None):
    p, r, c = parts.shape
    tr = r
    while tr * c * 4 > (1 << 20) and tr % 16 == 0:
        tr //= 2
    nt = r // tr

    def body(p_ref, w_ref, m_ref, v_ref, *rest):
        g_ref, d_ref, nm_ref, nv_ref = rest[-4:]
        g = p_ref[0].astype(F32)
        for i in range(1, p):
            g = g + p_ref[i].astype(F32)
        g_ref[...] = g
        d_ref[...], nm_ref[...], nv_ref[...] = _adam_update(g, w_ref[...], m_ref[...], v_ref[...])

    blk = pl.BlockSpec((tr, c), lambda t: (slot * nt + t, 0))
    in_specs = [pl.BlockSpec((p, tr, c), lambda t: (0, t, 0)), blk, blk, blk]
    unread = list(prev or []) + ([after] if after is not None else [])
    return pl.pallas_call(
        body, name=name, grid=(nt,), in_specs=in_specs + [pl.BlockSpec(memory_space=pl.ANY)] * len(unread), out_specs=[blk] * 4,
        out_shape=[_sds(w.shape)] * 4, input_output_aliases={4 + i: i for i in range(4)} if prev else {},
        compiler_params=pltpu.CompilerParams(dimension_semantics=("arbitrary",), vmem_limit_bytes=VMEM_LIMIT_BYTES))(parts, w, m, v, *unread)


def _adam_update(g, w, m, v):
    m2 = B1 * m + (1.0 - B1) * g
    v2 = B2 * v + (1.0 - B2) * (g * g)
    m_hat = m2 / (1.0 - B1 ** STEP)
    v_hat = v2 / (1.0 - B2 ** STEP)
    return -LR * (m_hat / (jnp.sqrt(v_hat) + ADAM_EPS) + WD * w), m2, v2


def _adamw_many(name, items, after):
    n = len(items)

    def body(*refs):
        outs = refs[4 * n + 1:]
        for i in range(n):
            g, w, m, v = (r[...] for r in refs[4 * i:4 * i + 4])
            for o, val in zip(outs[3 * i:3 * i + 3], _adam_update(g, w, m, v)):
                o[...] = val

    full = lambda a: pl.BlockSpec(a.shape, lambda t: (0, 0))
    flat = [a for item in items for a in item]
    res = _pc(body, name, (1,), [full(a) for a in flat] + [pl.BlockSpec(memory_space=pl.ANY)],
              [full(item[1]) for item in items for _ in range(3)],
              [_sds(item[1].shape) for item in items for _ in range(3)])(*flat, after)
    return [tuple(res[3 * i:3 * i + 3]) for i in range(n)]


def _sum_parts(parts):
    p, r, c = parts.shape
    tr = r
    while tr * c * 4 > (1 << 19) and tr % 16 == 0:
        tr //= 2

    def body(p_ref, o_ref):
        g = p_ref[0]
        for i in range(1, p):
            g = g + p_ref[i]
        o_ref[...] = g

    return _pc(body, "sum_parts", (r // tr,), [pl.BlockSpec((p, tr, c), lambda t: (0, t, 0))], pl.BlockSpec((tr, c), lambda t: (t, 0)),
               _sds((r, c)))(parts)


def _place():
    x, y, c = lax.axis_index("x"), lax.axis_index("y"), lax.axis_index("c")
    peers = []
    for k in range(1, N_DEV):
        px = (1 - x) if k & 4 else x
        py = (1 - y) if k & 2 else y
        pc = (1 - c) if k & 1 else c
        peers.append(((px, py, pc), 4 * px + 2 * py + pc))
    return 4 * x + 2 * y + c, peers


def _at(ref, idx):
    return ref if idx is None else ref.at[idx]


def _exchange_copies(plan, n, src_refs, dst_refs, send_sems, recv_sems, local_sems=None, with_arrivals=True):
    me, peers = _place()
    local = [] if local_sems is None else [
        pltpu.make_async_copy(_at(src_refs[si], sx), _at(dst_refs[di], dx), local_sems.at[i])
        for i, (si, sx, di, dx) in enumerate(plan(me, me, 0))]

    def remote(k, i, dev, entry):
        si, sx, di, dx = entry
        return pltpu.make_async_remote_copy(_at(src_refs[si], sx), _at(dst_refs[di], dx), send_sems.at[k * n + i], recv_sems.at[k * n + i],
                                            device_id=dev, device_id_type=MESH)

    sends = [remote(k, i, dev, e) for k, (dev, peer) in enumerate(peers) for i, e in enumerate(plan(me, peer, k + 1))]
    if not with_arrivals:
        return local, sends, []
    arrivals = [remote(k, i, dev, e) for k, (dev, peer) in enumerate(peers) for i, e in enumerate(plan(peer, me, k + 1))]
    return local, sends, arrivals


def _sem_shapes(n_copies, local=True):
    sems = [pltpu.SemaphoreType.DMA(((N_DEV - 1) * n_copies,)), pltpu.SemaphoreType.DMA(((N_DEV - 1) * n_copies,))]
    return sems + [pltpu.SemaphoreType.DMA((n_copies,))] if local else sems


def _exchange(name, srcs, dst_shapes, plan, n_copies):
    ns, nd = len(srcs), len(dst_shapes)

    def body(*refs):
        local, sends, arrivals = _exchange_copies(plan, n_copies, refs[:ns], refs[ns:ns + nd], *refs[ns + nd:])
        for cp in local + sends:
            cp.start()
        for cp in arrivals:
            cp.wait_recv()
        for cp in sends:
            cp.wait_send()
        for cp in local:
            cp.wait()

    any_spec = pl.BlockSpec(memory_space=pl.ANY)
    return pl.pallas_call(
        body, name=name, in_specs=[any_spec] * ns, out_specs=[any_spec] * nd, out_shape=list(dst_shapes),
        scratch_shapes=_sem_shapes(n_copies))(*srcs)


HBM_SPEC = pl.BlockSpec(memory_space=pltpu.HBM)
SEM_SPEC = pl.BlockSpec(memory_space=pltpu.SEMAPHORE)
ANY_SPEC = pl.BlockSpec(memory_space=pl.ANY)
TOKEN_SPEC = pl.BlockSpec(memory_space=pltpu.VMEM)
SIDE_EFFECT = pltpu.SideEffectType.DATAFLOW_SIDE_EFFECTING


def _wait_all(local, sends, arrivals):
    for cp in arrivals:
        cp.wait_recv()
    for cp in sends:
        cp.wait_send()
    for cp in local:
        cp.wait()


def _exchange_start(name, srcs, dst_shapes, plan, n_copies, order):
    ns, nd = len(srcs), len(dst_shapes)
    nb = ns + nd

    def body(*refs):
        local, sends, _ = _exchange_copies(plan, n_copies, refs[:ns], refs[ns:nb], *refs[nb + 1:nb + 4], with_arrivals=False)
        for cp in local + sends:
            cp.start()
        refs[-1][...] = jnp.zeros((8, LANE), F32)

    lands = [pltpu.with_memory_space_constraint(lax.empty(d.shape, d.dtype), pltpu.HBM) for d in dst_shapes]
    srcs = [pltpu.with_memory_space_constraint(a, pltpu.HBM) for a in srcs]
    bufs = srcs + lands
    out = pl.pallas_call(
        body, name=name, in_specs=[HBM_SPEC] * nb + [ANY_SPEC],
        out_specs=[SEM_SPEC] * 3 + [HBM_SPEC] * nb + [TOKEN_SPEC],
        out_shape=_sem_shapes(n_copies) + [pltpu.HBM(a.shape, a.dtype) for a in bufs] + [_sds((8, LANE))],
        input_output_aliases={i: 3 + i for i in range(nb)},
        compiler_params=pltpu.CompilerParams(has_side_effects=SIDE_EFFECT))(*bufs, order)
    return out[:3], out[3:3 + ns], out[3 + ns:3 + nb], out[-1]


def _exchange_relay(name, sems, srcs, lands, plan, n_copies, plan2, n_copies2, after):
    ns, nd = len(srcs), len(lands)
    nb = ns + nd

    def body(*refs):
        land_refs = refs[ns:nb]
        _wait_all(*_exchange_copies(plan, n_copies, refs[:ns], land_refs, *refs[nb:nb + 3]))
        _, sends, _ = _exchange_copies(plan2, n_copies2, land_refs, land_refs, *refs[nb + 4:nb + 6], with_arrivals=False)
        for cp in sends:
            cp.start()
        refs[-1][...] = jnp.zeros((8, LANE), F32)

    out = pl.pallas_call(
        body, name=name, in_specs=[HBM_SPEC] * nb + [SEM_SPEC] * 3 + [ANY_SPEC],
        out_specs=[SEM_SPEC] * 2 + [HBM_SPEC] * nd + [TOKEN_SPEC],
        out_shape=_sem_shapes(n_copies2, local=False) + [pltpu.HBM(a.shape, a.dtype) for a in lands] + [_sds((8, LANE))],
        input_output_aliases={ns + i: 2 + i for i in range(nd)},
        compiler_params=pltpu.CompilerParams(has_side_effects=SIDE_EFFECT))(*srcs, *lands, *sems, after)
    return out[:2], out[2:2 + nd], out[-1]


def _exchange_wait(name, sems, srcs, lands, plan, n_copies, after):
    srcs = [] if srcs is None else list(srcs)
    ns, nd = len(srcs), len(lands)
    nb = ns + nd

    def body(*refs):
        land_refs = refs[ns:nb]
        _wait_all(*_exchange_copies(plan, n_copies, refs[:ns] if ns else land_refs, land_refs, *refs[nb:nb + len(sems)]))

    bufs = srcs + list(lands)
    out = pl.pallas_call(
        body, name=name, in_specs=[HBM_SPEC] * nb + [SEM_SPEC] * len(sems) + [ANY_SPEC],
        out_specs=[HBM_SPEC] * nb, out_shape=[pltpu.HBM(a.shape, a.dtype) for a in bufs],
        input_output_aliases={i: i for i in range(nb)},
        compiler_params=pltpu.CompilerParams(has_side_effects=SIDE_EFFECT))(*bufs, *sems, after)
    return out[ns:]


def _all_gather(name, arrs):
    plan = lambda me, peer, k: [(i, None, i, me) for i in range(len(arrs))]
    return _exchange(name, arrs, [_sds((N_DEV,) + a.shape, a.dtype) for a in arrs], plan, len(arrs))


def _mix0_fwd(h, p):
    z = _mm_nt("mix0_in", h, p["ab_w_in"])
    y_a, d = _pool_fwd(z, p["pool_w"], p["pool_scale"])
    y_b = _sgu_fwd(z, p["sgu_ln_g"], p["sgu_ln_b"], p["sgu_w"], p["sgu_bt"])
    ycat = jnp.concatenate([y_a, y_b], axis=1)
    return _mm_nn("mix0_out", ycat, p["ab_w_out"]), (h, z, d, ycat)


def _mix0_bwd(df, saved, p, after):
    h, z, d, ycat = saved
    dycat = _mm_nt("mix0_out_dx", df, p["ab_w_out"], after=after)
    g = {"ab_w_out": _mm_tn("mix0_out_dw", ycat, df, BF16)}
    dz_p, g["pool_w"], g["pool_scale"] = _pool_bwd(dycat, d, p["pool_w"], p["pool_scale"])
    dz_u, dz_v, g["sgu_ln_g"], g["sgu_ln_b"], g["sgu_w"], dbt = _sgu_bwd(
        z, dycat, p["sgu_ln_g"], p["sgu_ln_b"], p["sgu_w"], p["sgu_bt"], p["head_sum"])
    g["sgu_b"] = dbt[:, :NH].T
    dz = jnp.concatenate([dz_p, dz_u, dz_v], axis=1)
    g["ab_w_in"] = _mm_tn("mix0_in_dw", dz, h, BF16)
    return _mm_nn("mix0_in_dx", dz, p["ab_w_in"]), g


def _mix1_fwd(h, p):
    u = _mm_nn("ssm_w_in", h, p["ssm_w_in"])
    x_re, x_im, y = _ssm_scan("ssm_scan_fwd", u, p["wb_bd"], p["lam_bar_re"], p["lam_bar_im"], p["wc_bd"], False)
    g = _ssm_act_fwd(y, u, p["ssm_d"])
    zz = _mm_nn("ssm_glu", g, p["ssm_w_glu"])
    return _glu_fwd(zz), (h, u, x_re, x_im, y, g, zz)


def _mix1_bwd(df, saved, p, after):
    h, u, x_re, x_im, y, g, zz = saved
    gr = {}
    dzz = _glu_bwd(zz, df)
    dg = _mm_nt("ssm_glu_dx", dzz, p["ssm_w_glu"], after=after)
    gr["ssm_w_glu"] = _mm_tn("ssm_glu_dw", g, dzz, BF16)
    dy, gr["ssm_d"] = _ssm_act_bwd(dg, y, u, p["ssm_d"])
    _, _, du_ssm, g_lam_re, g_lam_im, mb_re, mb_im, mc_re, mc_im = _ssm_scan(
        "ssm_scan_bwd", dy, p["wc_bd"], p["lam_bar_re"], -p["lam_bar_im"], p["wb_bd"], True, states=(x_re, x_im), u=u)
    du = _axpy(du_ssm, dy, p["ssm_d"])
    gr["ssm_w_in"] = _mm_tn("ssm_w_in_dw", h, du, BF16)
    dh = _mm_nt("ssm_w_in_dx", du, p["ssm_w_in"])
    per_group = lambda m: m[:, :, :SSM_P].reshape(SSM_G, SSM_N, SSM_P)
    gr["ssm_c_re"] = per_group(mc_re)
    gr["ssm_c_im"] = -per_group(mc_im)
    dlr, dli, ddt, dbr, dbi = _ssm_param_bwd(
        g_lam_re.reshape(SSM_G, SSM_P), g_lam_im.reshape(SSM_G, SSM_P),
        per_group(mb_re).reshape(SSM_G, SSM_N * SSM_P), per_group(mb_im).reshape(SSM_G, SSM_N * SSM_P),
        p["lam_re"], p["lam_im"], p["lam_re_rep"], p["lam_im_rep"], p["log_dt"], p["b_re"], p["b_im"], p["seg"])
    gr["ssm_lam_re"], gr["ssm_lam_im"], gr["ssm_log_dt"] = dlr, dli, ddt[:, 0]
    gr["ssm_b_re"] = dbr.reshape(SSM_G, SSM_N, SSM_P)
    gr["ssm_b_im"] = dbi.reshape(SSM_G, SSM_N, SSM_P)
    return dh, gr


def _ssm_params(lam_re, lam_im, b_re, b_im, c_re, c_im, log_dt):
    wide = lambda b: b.transpose(0, 2, 1).reshape(SSM_G, SSM_N * SSM_P)
    p = {"lam_re": lam_re, "lam_im": lam_im, "log_dt": log_dt.reshape(SSM_G, 1),
         "lam_re_rep": jnp.tile(lam_re, (1, SSM_N)), "lam_im_rep": jnp.tile(lam_im, (1, SSM_N)), "b_re": wide(b_re), "b_im": wide(b_im)}
    lbr, lbi, bbr, bbi = _ssm_prep(lam_re, lam_im, p["lam_re_rep"], p["lam_im_rep"], p["log_dt"], p["b_re"], p["b_im"])
    p["lam_bar_re"], p["lam_bar_im"] = lbr.reshape(1, SSM_L), lbi.reshape(1, SSM_L)
    rows = lambda m: m.reshape(SSM_G * SSM_N, SSM_P)
    p["wb_bd"] = _ssm_block_diag(rows(bbr), rows(bbi))
    p["wc_bd"] = _ssm_block_diag(rows(c_re), rows(-c_im))
    p["seg"] = jnp.tile(jnp.eye(SSM_P, dtype=F32), (SSM_N, 1))
    return p


RES_WEIGHT = (0.5, 1.0, 0.5)


def _local_step(x, tgt, vecs, weights_of, on_part, on_grads):
    def fns(i, w):
        if i % 3 != 1:
            win, wout_of = w
            return ((lambda h: _ffn_fwd(h, win, wout_of)),
                    (lambda df, sv, after: (_ffn_bwd(df, sv, win, wout_of(None), lambda tag, part: on_part(i, tag, part), after), None)))
        if i == 1:
            return (lambda h: _mix0_fwd(h, w)), (lambda df, sv, after: _mix0_bwd(df, sv, w, after))
        return (lambda h: _mix1_fwd(h, w)), (lambda df, sv, after: _mix1_bwd(df, sv, w, after))

    rw = RES_WEIGHT * 2
    saved, bwd = [], []
    f = None
    for i in range(6):
        w, token = weights_of(i, x if i == 0 else f)
        fwd, b = fns(i, w)
        if i == 0:
            h = _prenorm_fwd(x, vecs, 0, token)
        else:
            x, h = _post_pre_fwd(x, f, vecs, i, rw[i - 1], token)
        f, inner = fwd(h)
        saved.append((x, f, inner))
        bwd.append(b)
    loss_row, dx = _loss_fwd_bwd(_postnorm_fwd(x, f, vecs, 5, rw[5]), tgt)
    df, dv_top = _postnorm_bwd(dx, f, vecs, 5, rw[5])
    token = jnp.zeros((8, LANE), F32)
    for i in reversed(range(6)):
        x_i, _, inner = saved[i]
        dh, extra = bwd[i](df, inner, token)
        if i > 0:
            dx, df, dv = _pre_post_bwd(dx, dh, x_i, saved[i - 1][1], vecs, i, rw[i - 1])
        else:
            dx, dv = _prenorm_bwd(dx, dh, x_i, vecs, 0)
        token = on_grads(i, extra, dv, dv_top if i == 5 else None, loss_row)
    return dx


def _pad_rows(v, rows):
    return jnp.pad(v, (0, rows * LANE - v.shape[0])).reshape(rows, LANE)


def _pack(parts):
    flat, layout, off = [], [], 0
    for a in parts:
        n = a.size
        padded = -(-n // LANE) * LANE
        flat.append(jnp.pad(a.reshape(-1).astype(F32), (0, padded - n)))
        layout.append((off, n, a.shape))
        off += padded
    return jnp.concatenate(flat), layout


def _unpack(flat, layout):
    return [flat[off:off + n].reshape(shape) for off, n, shape in layout]


SMALL_REPLICATED = ["ada_b", "pool_w", "pool_scale", "sgu_ln_g", "sgu_ln_b", "sgu_w", "sgu_b", "ssm_lam_re", "ssm_lam_im",
                    "ssm_b_re", "ssm_b_im", "ssm_c_re", "ssm_c_im", "ssm_log_dt"]
SMALL_SHARDED = ["norm_pre", "norm_post", "ssm_d"]
TRANSPOSED = ["ffn_w_in", "ab_w_in", "ssm_b_re", "ssm_b_im"]
WEIGHTS = ['ada_w', 'ada_b', 'norm_pre', 'norm_post', 'ffn_w_in', 'ffn_w_out', 'ab_w_in', 'pool_w', 'pool_scale', 'sgu_ln_g',
           'sgu_ln_b', 'sgu_w', 'sgu_b', 'ab_w_out', 'ssm_w_in', 'ssm_lam_re', 'ssm_lam_im', 'ssm_b_re', 'ssm_b_im', 'ssm_c_re',
           'ssm_c_im', 'ssm_d', 'ssm_log_dt', 'ssm_w_glu']


def kernel(x, c, ada_w, ada_b, norm_pre, norm_post, ffn_w_in, ffn_w_out, ab_w_in, pool_w, pool_scale, sgu_ln_g, sgu_ln_b, sgu_w, sgu_b, ab_w_out, ssm_w_in, ssm_lam_re, ssm_lam_im, ssm_b_re, ssm_b_im, ssm_c_re, ssm_c_im, ssm_d, ssm_log_dt, ssm_w_glu, loss_target, m_ada_w, m_ada_b, m_norm_pre, m_norm_post, m_ffn_w_in, m_ffn_w_out, m_ab_w_in, m_pool_w, m_pool_scale, m_sgu_ln_g, m_sgu_ln_b, m_sgu_w, m_sgu_b, m_ab_w_out, m_ssm_w_in, m_ssm_lam_re, m_ssm_lam_im, m_ssm_b_re, m_ssm_b_im, m_ssm_c_re, m_ssm_c_im, m_ssm_d, m_ssm_log_dt, m_ssm_w_glu, v_ada_w, v_ada_b, v_norm_pre, v_norm_post, v_ffn_w_in, v_ffn_w_out, v_ab_w_in, v_pool_w, v_pool_scale, v_sgu_ln_g, v_sgu_ln_b, v_sgu_w, v_sgu_b, v_ab_w_out, v_ssm_w_in, v_ssm_lam_re, v_ssm_lam_im, v_ssm_b_re, v_ssm_b_im, v_ssm_c_re, v_ssm_c_im, v_ssm_d, v_ssm_log_dt, v_ssm_w_glu):
    args = locals()
    wts = {n: args[n] for n in WEIGHTS}
    mom = {n: args["m_" + n] for n in WEIGHTS}
    var = {n: args["v_" + n] for n in WEIGHTS}
    for n in TRANSPOSED:
        for t in (wts, mom, var):
            t[n] = jnp.swapaxes(t[n], -1, -2)
    me = 4 * lax.axis_index("x") + 2 * lax.axis_index("y") + lax.axis_index("c")
    s = x.shape[1]
    nd = D // N_DEV

    small_in, small_in_layout = _pack([c, norm_pre, norm_post, ssm_d])
    small_rows = -(-small_in.shape[0] // (8 * LANE)) * 8
    (g_small,) = _all_gather("gather_small", [_pad_rows(small_in, small_rows)])
    g_small = g_small.reshape(N_DEV, -1)
    c_all, npre_g, npost_g, sd_g = [jnp.stack([_unpack(g_small[j], small_in_layout)[i] for j in range(N_DEV)]) for i in range(4)]
    c_all = c_all.reshape(N_DEV, D)
    norm_pre_full = npre_g.transpose(1, 2, 0, 3).reshape(2, 3, D)
    norm_post_full = npost_g.transpose(1, 2, 0, 3).reshape(2, 3, D)
    ssm_d_full = sd_g.transpose(1, 0, 2).reshape(1, D)

    nw = ada_w.shape[-1]
    (mod_g,) = _all_gather("gather_mod", [_mod_part(c_all, ada_w)])
    mod = lax.dynamic_index_in_dim(mod_g, me, axis=2, keepdims=False)
    mod = (mod.transpose(1, 0, 2).reshape(2, N_DEV * nw) + ada_b).reshape(2, 3, 3, D)

    w_in_t = wts["ffn_w_in"]
    shards = [[w_in_t[0, 0]], [ffn_w_out[0, 0]], [wts["ab_w_in"][0], ab_w_out[0]], [w_in_t[0, 1], ffn_w_out[0, 1]],
              [w_in_t[1, 0], ffn_w_out[1, 0]], [ssm_w_in[0], ssm_w_glu[0]], [w_in_t[1, 1], ffn_w_out[1, 1]]]
    same_core = (2, 4, 6)

    def gather_plan(n):
        return lambda me_, peer_, k: [(a, None, a, me_) for a in range(n)] if k in (0, 1) + same_core else []

    def relay_plan(n):
        return lambda me_, peer_, k: [(a, me_ ^ kk, a, me_ ^ kk) for kk in same_core for a in range(n)] if k == 1 else []

    gathers, relays = [], {}
    token = mod_g
    for g, group in enumerate(shards):
        group = [a.astype(BF16) for a in group]
        sems, srcs_thru, lands, token = _exchange_start(
            f"gather_start_{g}", group, [_sds((N_DEV,) + a.shape, BF16) for a in group], gather_plan(len(group)), len(group), token)
        gathers.append((sems, srcs_thru, lands))
    mod6 = mod.reshape(6, 3, D)
    vecs = jnp.stack([norm_pre_full.reshape(6, D), mod6[:, 1], mod6[:, 0], norm_post_full.reshape(6, D), mod6[:, 2]]
                     + [jnp.zeros((6, D), F32)] * 3, axis=1)
    vecs = vecs + token[0, 0]

    def relay(g, after):
        sems, srcs_thru, lands = gathers[g]
        n = len(lands)
        relays[g] = _exchange_relay(f"gather_relay_{g}", sems, srcs_thru, lands, gather_plan(n), n, relay_plan(n), 3 * n, after)

    def fetch(g, after):
        if g not in relays:
            relay(g, after)
        sems, lands, token = relays[g]
        n = len(lands)
        got = _exchange_wait(f"gather_wait_{g}", sems, None, lands, relay_plan(n), 3 * n, after)
        if 0 < g < len(gathers) - 1:
            relay(g + 1, got[0])
            token = relays[g + 1][2]
        return got, token

    head_sum = jnp.repeat(jnp.eye(NH, LANE, dtype=F32), HD, axis=0)
    mix0 = {"pool_w": pool_w[0], "pool_scale": pool_scale, "sgu_ln_g": sgu_ln_g, "sgu_ln_b": sgu_ln_b, "sgu_w": sgu_w[0],
            "sgu_bt": jnp.pad(sgu_b[0].T, ((0, 0), (0, LANE - NH))), "head_sum": head_sum}
    mix1 = _ssm_params(ssm_lam_re[0], ssm_lam_im[0], ssm_b_re[0], ssm_b_im[0], ssm_c_re[0], ssm_c_im[0], ssm_log_dt[0])
    mix1["ssm_d"] = ssm_d_full

    def weights_of(i, x_in):
        if i == 0:
            (win,), token = fetch(0, x_in)
            cache = []

            def wout_of(z):
                if not cache:
                    cache.append(fetch(1, z)[0][0])
                return cache[0]

            return (win, wout_of), token
        (a, b), token = fetch(i + 1, x_in)
        if i % 3 != 1:
            return (a, lambda z: b), token
        if i == 1:
            return dict(mix0, ab_w_in=a.reshape(-1, D), ab_w_out=b.reshape(D, D)), token
        return dict(mix1, ssm_w_in=a.reshape(D, D), ssm_w_glu=b.transpose(1, 0, 2).reshape(D, -1)), token

    def shard_cols(a):
        r = a.shape[0]
        return a.reshape(r, N_DEV, -1).transpose(1, 0, 2)

    scatter_plan = lambda me_, peer_, k: [(0, peer_, 0, me_), (1, peer_, 1, me_)]
    scatter_plan1 = lambda me_, peer_, k: [(0, peer_, 0, me_)]
    scatters = []
    last_token = [jnp.zeros((8, LANE), F32)]
    pieces, mixer, bundles = {}, {}, {}
    bundle_plan = lambda me_, peer_, k: [(0, None, 0, me_)]

    held = {}

    def on_part(i, tag, part):
        if i != 0 and tag == "w_out":
            held[i] = part
            return last_token[0]
        names, parts, plan = (("ffn_" + tag,), [part], scatter_plan1) if i == 0 else (("ffn_w_out", "ffn_w_in"), [held[i], part], scatter_plan)
        sems, srcs_thru, lands, last_token[0] = _exchange_start(
            f"scatter_start_{i}_{tag}", parts, [_sds(a.shape, BF16) for a in parts], plan, len(parts), last_token[0])
        scatters.append((i, names, plan, sems, srcs_thru, lands))
        return last_token[0]
    mix0_names = ["pool_w", "pool_scale", "sgu_ln_g", "sgu_ln_b", "sgu_w", "sgu_b"]
    mix1_names = ["ssm_lam_re", "ssm_lam_im", "ssm_b_re", "ssm_b_im", "ssm_c_re", "ssm_c_im", "ssm_log_dt", "ssm_d"]

    def start_bundle(tag, arrays):
        flat, layout = _pack(arrays)
        rows = -(-flat.shape[0] // (8 * LANE)) * 8
        plan = gather_plan(1) if tag == "a" else bundle_plan
        sems, srcs_thru, lands, last_token[0] = _exchange_start(
            f"small_start_{tag}", [_pad_rows(flat, rows)], [_sds((N_DEV, rows, LANE))], plan, 1, last_token[0])
        bundles[tag] = (sems, srcs_thru, lands, layout)

    def on_grads(i, extra, dv, dv_top, loss_row):
        pieces[i] = dv
        if i == 5:
            pieces["top"] = dv_top
        if i == 4:
            mixer.update({n: extra[n] for n in mix1_names})
        if i == 1:
            mixer.update({n: extra[n] for n in mix0_names})
            start_bundle("a", [jnp.stack([pieces[j] for j in ("top", 5, 4, 3, 2, 1)])] + [mixer[n] for n in mix0_names + mix1_names])
        if i == 0:
            start_bundle("b", [dv, loss_row])
        if i % 3 != 1:
            return last_token[0]
        if i == 1:
            names, parts = ("ab_w_in", "ab_w_out"), [extra["ab_w_in"].reshape(N_DEV, -1, D), extra["ab_w_out"].reshape(N_DEV, nd, D)]
        else:
            names, parts = ("ssm_w_in", "ssm_w_glu"), [extra["ssm_w_in"].reshape(N_DEV, nd, D), shard_cols(extra["ssm_w_glu"])]
        sems, srcs_thru, lands, last_token[0] = _exchange_start(
            f"scatter_start_{i}", parts, [_sds(a.shape, BF16) for a in parts], scatter_plan, 2, last_token[0])
        scatters.append((i, names, scatter_plan, sems, srcs_thru, lands))
        return last_token[0]

    grad_x = _local_step(x[0], loss_target[0], vecs, weights_of, on_part, on_grads)

    out_g, out_d, out_m, out_v = {}, {}, {}, {}
    big_out = {}

    def adam_big(name, recv, n, slot=0, after=None):
        c_ = wts[n].shape[-1]
        big_out[n] = _adamw(name, recv.reshape(recv.shape[0], -1, c_), *[t[n].reshape(-1, c_) for t in (wts, mom, var)],
                            slot=slot, prev=big_out.get(n), after=after)
        return big_out[n][0]

    ffn_slot = {0: 0, 2: 1, 3: 2, 5: 3}

    def land_and_update(entries, after):
        for i, names, plan, sems, srcs_thru, lands in entries:
            recv = _exchange_wait(f"scatter_wait_{i}_{names[0]}", sems, srcs_thru, lands, plan, len(names), after)
            for n, r in zip(names, recv):
                after = adam_big(f"adamw_{n}_{i}", r, n, ffn_slot.get(i, 0), after)
        return after

    after = land_and_update([e for e in scatters if e[0] != 0], last_token[0])

    def landed(tag, g_parts):
        layout = bundles[tag][3]
        off, n, shape = layout[0]
        dmods = g_parts.reshape(N_DEV, -1)[:, off:off + n].reshape((N_DEV,) + shape)
        total = _sum_parts(g_parts)
        return dmods, _unpack(total.reshape(-1), layout), total

    def adam_small(n, g, after=None):
        cols = wts[n].shape[-1]
        res = _adamw(f"adamw_{n}", g.reshape(1, -1, cols), *[t[n].reshape(-1, cols) for t in (wts, mom, var)], after=after)
        for o, arr in zip((out_g, out_d, out_m, out_v), res):
            o[n] = arr.reshape(wts[n].shape)
            if n in TRANSPOSED:
                o[n] = jnp.swapaxes(o[n], -1, -2)
        return res[0]

    sems, srcs_thru, lands, _ = bundles["a"]
    sems, lands, _ = _exchange_relay("small_relay_a", sems, srcs_thru, lands, gather_plan(1), 1, relay_plan(1), 3, after)
    (parts_a,) = _exchange_wait("small_wait_a", sems, None, lands, relay_plan(1), 3, after)
    shells_a, sums_a, after = landed("a", parts_a)
    small = dict(zip(mix0_names + mix1_names, sums_a[1:]))
    def adam_tiny(name, grads, after):
        view = lambda n, a: a.reshape(-1, wts[n].shape[-1])
        items = [(view(n, g),) + tuple(view(n, t[n]) for t in (wts, mom, var)) for n, g in grads.items()]
        for (n, _), item, res in zip(grads.items(), items, _adamw_many(name, items, after)):
            for o, arr in zip((out_g, out_d, out_m, out_v), (item[0],) + res):
                o[n] = arr.reshape(wts[n].shape)
        return res[0]

    tiny = ["pool_scale", "sgu_ln_g", "sgu_ln_b", "sgu_b", "ssm_lam_re", "ssm_lam_im", "ssm_log_dt"]
    for n in [n for n in mix0_names + mix1_names if n not in tiny and n != "ssm_d"]:
        after = adam_small(n, small[n], after)
    after = adam_tiny("adamw_tiny_mixers", dict({n: small[n] for n in tiny},
                                                ssm_d=lax.dynamic_slice_in_dim(small["ssm_d"], me * nd, nd, axis=1)), after)
    sems, srcs_thru, lands, _ = bundles["b"]
    (parts_b,) = _exchange_wait("small_wait_b", sems, srcs_thru, lands, bundle_plan, 1, after)
    shell_b, (first_sum, loss_sum), after = landed("b", parts_b)
    loss = loss_sum[0, 0]

    def shell_grads(top, blocks, first):
        own_rows = jnp.concatenate([first[..., None, :, :], blocks[..., :0:-1, :, :]], axis=-3)
        next_rows = jnp.concatenate([own_rows[..., 1:, :, :], top[..., None, :, :]], axis=-3)
        dmod_ = jnp.stack([own_rows[..., V_SHIFT, :], own_rows[..., V_SCALE, :], next_rows[..., V_GATE, :]], axis=-2)
        return dmod_, own_rows[..., V_GPRE, :], next_rows[..., V_GPOST, :]

    dmod_sum, dg_pre_sum, dg_post_sum = shell_grads(sums_a[0][0], sums_a[0], first_sum)
    own = lambda a: lax.dynamic_slice_in_dim(a, me * nd, nd, axis=1)
    after = adam_tiny("adamw_tiny_shell", {"ada_b": dmod_sum, "norm_pre": own(dg_pre_sum), "norm_post": own(dg_post_sum)}, after)

    dmod_all = shell_grads(shells_a[:, 0], shells_a, shell_b)[0].reshape(N_DEV, 2, N_DEV, nw)
    dmod_mine = lax.dynamic_index_in_dim(dmod_all, me, axis=2, keepdims=False).transpose(1, 0, 2)
    g_ada_w = _ada_w_grad(c_all.T, dmod_mine)
    after = after[0:1, 0:1] + adam_big("adamw_ada_w", g_ada_w[None], "ada_w")[0:1, 0:1]

    land_and_update([e for e in scatters if e[0] == 0], after)
    for n, res in big_out.items():
        for o, arr in zip((out_g, out_d, out_m, out_v), res):
            o[n] = arr.reshape(wts[n].shape)
            if n in TRANSPOSED:
                o[n] = jnp.swapaxes(o[n], -1, -2)

    return (loss, grad_x[None], *[out_g[n] for n in WEIGHTS], *[out_d[n] for n in WEIGHTS],
            *[out_m[n] for n in WEIGHTS], *[out_v[n] for n in WEIGHTS])
```

```python
import functools
import math

import jax
import jax.numpy as jnp
from jax import lax
from jax.experimental import pallas as pl
from jax.experimental.pallas import tpu as pltpu

F32 = jnp.float32
BF16 = jnp.bfloat16
MESH = pl.DeviceIdType.MESH
HIGHEST = lax.Precision.HIGHEST

N_DEV = 8
D = 1024
D_FF = 2816
FSH = 2 * D_FF // N_DEV
EPS = 1e-6
POOL_WINDOWS = (2, 4, 8, 16)
HD = 128
NH = 4
SSM_G, SSM_P, SSM_N = 64, 64, 16
SSM_L = SSM_G * SSM_P
LR, B1, B2, ADAM_EPS, WD, STEP = 0.001, 0.9, 0.999, 1e-08, 0.01, 10
GELU_C = math.sqrt(2.0 / math.pi)
VMEM_LIMIT_BYTES = 48 * 1024 * 1024
LANE = 128


def _in_hbm(args):
    return [pltpu.with_memory_space_constraint(a, pltpu.HBM) for a in args]


def _pc(body, name, grid, in_specs, out_specs, out_shape, scratch=()):
    call = pl.pallas_call(
        body, name=name, grid=grid, in_specs=in_specs, out_specs=out_specs, out_shape=out_shape,
        scratch_shapes=list(scratch),
        compiler_params=pltpu.CompilerParams(dimension_semantics=("arbitrary",) * len(grid),
                                             vmem_limit_bytes=VMEM_LIMIT_BYTES))
    return lambda *args: call(*_in_hbm(args))


def _sds(shape, dtype=F32):
    return jax.ShapeDtypeStruct(tuple(shape), dtype)


def _bf(v):
    return v if v.dtype == BF16 else v.astype(BF16)


def _row_spec(ts, width, col=0):
    return pl.BlockSpec((ts, width), lambda t, _c=col: (t, _c))


def _vec_spec(width, col=0):
    return pl.BlockSpec((1, width), lambda t, _c=col: (0, _c))


def _mm(name, a, b, contract, grid, a_spec, b_spec, o_spec, out_shape, acc_axis=None, after=None):
    dn = (contract, ((), ()))

    def body(a_ref, b_ref, *rest):
        o_ref = rest[-1]
        r = lax.dot_general(_bf(a_ref[...]), _bf(b_ref[...]), dn, preferred_element_type=F32)
        if acc_axis is None:
            o_ref[...] = r.astype(o_ref.dtype)
        else:
            k = pl.program_id(acc_axis)

            @pl.when(k == 0)
            def _():
                o_ref[...] = r

            @pl.when(k > 0)
            def _():
                o_ref[...] += r

    if after is None:
        return _pc(body, name, grid, [a_spec, b_spec], o_spec, out_shape)(a, b)
    return _pc(body, name, grid, [a_spec, b_spec, pl.BlockSpec(memory_space=pl.ANY)], o_spec, out_shape)(a, b, after)


def _mm_sum(name, a, b, ts, after=None):
    nj, s, k = a.shape
    n = b.shape[2]

    def body(a_ref, b_ref, *rest):
        acc = jnp.dot(a_ref[0], b_ref[0], preferred_element_type=F32)
        for j in range(1, nj):
            acc = acc + jnp.dot(a_ref[j], b_ref[j], preferred_element_type=F32)
        rest[-1][...] = acc

    specs = [pl.BlockSpec((nj, ts, k), lambda t: (0, t, 0)), pl.BlockSpec((nj, k, n), lambda t: (0, 0, 0))]
    args = (a, b)
    if after is not None:
        specs, args = specs + [pl.BlockSpec(memory_space=pl.ANY)], args + (after,)
    return _pc(body, name, (s // ts,), specs, pl.BlockSpec((ts, n), lambda t: (t, 0)), _sds((s, n)))(*args)


def _tile(s):
    return min(s, 1024)


def _div_tile(n, cap=1024):
    t = min(n, cap) // LANE * LANE
    while n % t:
        t -= LANE
    return t


def _mm_nn(name, a, b, out_dtype=F32):
    s, k = a.shape
    n = b.shape[1]
    ts, tn = _tile(s), _div_tile(n)
    return _mm(name, a, b, ((1,), (0,)), (n // tn, s // ts),
               pl.BlockSpec((ts, k), lambda j, t: (t, 0)), pl.BlockSpec((k, tn), lambda j, t: (0, j)),
               pl.BlockSpec((ts, tn), lambda j, t: (t, j)), _sds((s, n), out_dtype))


def _mm_nt(name, a, b, out_dtype=F32, after=None):
    s, n = a.shape
    k = b.shape[0]
    ts, tk = _tile(s), _div_tile(k)
    return _mm(name, a, b, ((1,), (1,)), (k // tk, s // ts),
               pl.BlockSpec((ts, n), lambda j, t: (t, 0)), pl.BlockSpec((tk, n), lambda j, t: (j, 0)),
               pl.BlockSpec((ts, tk), lambda j, t: (t, j)), _sds((s, k), out_dtype), after=after)


def _mm_tn(name, a, b, out_dtype=F32, tm=512, tn=512):
    s, m = a.shape
    n = b.shape[1]
    tm, tn = min(m, tm), min(n, tn)
    return _mm(name, a, b, ((0,), (0,)), (m // tm, n // tn),
               pl.BlockSpec((s, tm), lambda i, j: (0, i)), pl.BlockSpec((s, tn), lambda i, j: (0, j)),
               pl.BlockSpec((tm, tn), lambda i, j: (i, j)), _sds((m, n), out_dtype))


def _rstd(v):
    return lax.rsqrt(jnp.mean(v * v, axis=-1, keepdims=True) + EPS)


V_GPRE, V_SCALE, V_SHIFT, V_GPOST, V_GATE = range(5)


def _vrow(v, r):
    return v[r:r + 1]


def _vblock(i):
    return pl.BlockSpec((None, 8, D), lambda t: (i, 0, 0))


def _head(xv, v):
    return ((xv * _rstd(xv) * _vrow(v, V_GPRE)) * (1.0 + _vrow(v, V_SCALE)) + _vrow(v, V_SHIFT)).astype(BF16)


def _tail(xv, fv, v, rw):
    return xv + (rw * _vrow(v, V_GATE)) * (fv * _rstd(fv) * _vrow(v, V_GPOST))


def _prenorm_fwd(x, vecs, i, after):
    s = x.shape[0]
    ts = min(s, 512)

    def body(x_ref, v_ref, after_ref, h_ref):
        h_ref[...] = _head(x_ref[...], v_ref[...])

    return _pc(body, "prenorm_fwd", (s // ts,), [_row_spec(ts, D), _vblock(i), pl.BlockSpec(memory_space=pl.ANY)], _row_spec(ts, D),
               _sds((s, D), BF16))(x, vecs, after)


def _postnorm_fwd(x, f, vecs, i, rw):
    s = x.shape[0]
    ts = min(s, 512)

    def body(x_ref, f_ref, v_ref, o_ref):
        o_ref[...] = _tail(x_ref[...], f_ref[...], v_ref[...], rw)

    return _pc(body, "postnorm_fwd", (s // ts,), [_row_spec(ts, D)] * 2 + [_vblock(i)], _row_spec(ts, D), _sds((s, D)))(x, f, vecs)


def _post_pre_fwd(x, f, vecs, i, rw_prev, after):
    s = x.shape[0]
    ts = min(s, 512)

    def body(x_ref, f_ref, vp_ref, vc_ref, after_ref, xo_ref, h_ref):
        xv = _tail(x_ref[...], f_ref[...], vp_ref[...], rw_prev)
        xo_ref[...] = xv
        h_ref[...] = _head(xv, vc_ref[...])

    return _pc(body, "post_pre_fwd", (s // ts,),
               [_row_spec(ts, D)] * 2 + [_vblock(i - 1), _vblock(i), pl.BlockSpec(memory_space=pl.ANY)], [_row_spec(ts, D)] * 2,
               [_sds((s, D)), _sds((s, D), BF16)])(x, f, vecs, vecs, after)


def _zero_at_first(first, *refs):
    @pl.when(first)
    def _():
        for ref in refs:
            ref[...] = jnp.zeros_like(ref)


def _acc(ref, first, v):
    @pl.when(first)
    def _():
        ref[...] = v

    @pl.when(jnp.logical_not(first))
    def _():
        ref[...] += v


def _colsum(v):
    return jnp.sum(v, axis=0, keepdims=True)


def _tail_bwd(do, fv, v, rw, dv_ref):
    gv = _vrow(v, V_GPOST)
    r = _rstd(fv)
    fn = fv * r
    dv_ref[V_GATE:V_GATE + 1, :] += rw * _colsum(do * (fn * gv))
    dy = (rw * _vrow(v, V_GATE)) * do
    dv_ref[V_GPOST:V_GPOST + 1, :] += _colsum(dy * fn)
    dfn = dy * gv
    return (r * (dfn - fn * jnp.mean(dfn * fn, axis=-1, keepdims=True))).astype(BF16)


def _head_bwd(do, dhv, xv, v, dv_ref):
    gv = _vrow(v, V_GPRE)
    r = _rstd(xv)
    xn = xv * r
    dv_ref[V_SHIFT:V_SHIFT + 1, :] += _colsum(dhv)
    dv_ref[V_SCALE:V_SCALE + 1, :] += _colsum(dhv * (xn * gv))
    dhp = dhv * (1.0 + _vrow(v, V_SCALE))
    dv_ref[V_GPRE:V_GPRE + 1, :] += _colsum(dhp * xn)
    dxn = dhp * gv
    return do + r * (dxn - xn * jnp.mean(dxn * xn, axis=-1, keepdims=True))


DV_SPEC = pl.BlockSpec((8, D), lambda t: (0, 0))


def _postnorm_bwd(dout, f, vecs, i, rw):
    s = dout.shape[0]
    ts = min(s, 512)

    def body(do_ref, f_ref, v_ref, df_ref, dv_ref):
        _zero_at_first(pl.program_id(0) == 0, dv_ref)
        df_ref[...] = _tail_bwd(do_ref[...], f_ref[...], v_ref[...], rw, dv_ref)

    return _pc(body, "postnorm_bwd", (s // ts,), [_row_spec(ts, D)] * 2 + [_vblock(i)], [_row_spec(ts, D), DV_SPEC],
               [_sds((s, D), BF16), _sds((8, D))])(dout, f, vecs)


def _prenorm_bwd(dout, dh, x, vecs, i):
    s = dout.shape[0]
    ts = min(s, 512)

    def body(do_ref, dh_ref, x_ref, v_ref, dx_ref, dv_ref):
        _zero_at_first(pl.program_id(0) == 0, dv_ref)
        dx_ref[...] = _head_bwd(do_ref[...], dh_ref[...], x_ref[...], v_ref[...], dv_ref)

    return _pc(body, "prenorm_bwd", (s // ts,), [_row_spec(ts, D)] * 3 + [_vblock(i)], [_row_spec(ts, D), DV_SPEC],
               [_sds((s, D)), _sds((8, D))])(dout, dh, x, vecs)


def _pre_post_bwd(dout, dh, x, f_prev, vecs, i, rw_prev):
    s = dout.shape[0]
    ts = min(s, 256)

    def body(do_ref, dh_ref, x_ref, f_ref, vc_ref, vp_ref, dx_ref, df_ref, dv_ref):
        _zero_at_first(pl.program_id(0) == 0, dv_ref)
        dx = _head_bwd(do_ref[...], dh_ref[...], x_ref[...], vc_ref[...], dv_ref)
        dx_ref[...] = dx
        df_ref[...] = _tail_bwd(dx, f_ref[...], vp_ref[...], rw_prev, dv_ref)

    rows = _row_spec(ts, D)
    return _pc(body, "pre_post_bwd", (s // ts,), [rows] * 4 + [_vblock(i), _vblock(i - 1)], [rows, rows, DV_SPEC],
               [_sds((s, D)), _sds((s, D), BF16), _sds((8, D))])(dout, dh, x, f_prev, vecs, vecs)


def _loss_fwd_bwd(y, tgt):
    s = y.shape[0]
    ts = min(s, 512)
    nt = s // ts

    def body(y_ref, t_ref, loss_ref, dy_ref, acc_ref):
        t = pl.program_id(0)
        e = y_ref[...] - t_ref[...]
        dy_ref[...] = e * (1.0 / D)
        _acc(acc_ref, t == 0, _colsum(e * e))

        @pl.when(t == nt - 1)
        def _():
            loss_ref[...] = jnp.full((1, LANE), 0.5 / D, F32) * jnp.sum(acc_ref[...])

    return _pc(body, "loss", (nt,), [_row_spec(ts, D)] * 2,
               [pl.BlockSpec((1, LANE), lambda t: (0, 0)), _row_spec(ts, D)],
               [_sds((1, LANE)), _sds((s, D))], scratch=[pltpu.VMEM((1, D), F32)])(y, tgt)


def _sigmoid(v):
    return 1.0 / (1.0 + jnp.exp(-v))


def _ffn_in_swiglu(h, win):
    s = h.shape[0]
    ts = _tile(s)
    nt = (((1,), (1,)), ((), ()))

    def body(h_ref, wa_ref, wb_ref, fac_ref, act_ref):
        hv = h_ref[...]
        a = lax.dot_general(hv, wa_ref[...], nt, preferred_element_type=F32)
        b = lax.dot_general(hv, wb_ref[...], nt, preferred_element_type=F32)
        sg = _sigmoid(a)
        silu = a * sg
        fac_ref[0] = (b * (sg * (1.0 + a * (1.0 - sg)))).astype(BF16)
        fac_ref[1] = silu.astype(BF16)
        act_ref[...] = (silu * b).astype(BF16)

    return _pc(body, "ffn_in", (4, s // ts),
               [pl.BlockSpec((ts, D), lambda k, t: (t, 0)), pl.BlockSpec((None, FSH, D), lambda k, t: (k, 0, 0)),
                pl.BlockSpec((None, FSH, D), lambda k, t: (k + 4, 0, 0))],
               [pl.BlockSpec((2, None, ts, FSH), lambda k, t: (0, k, t, 0)), pl.BlockSpec((None, ts, FSH), lambda k, t: (k, t, 0))],
               [_sds((2, 4, s, FSH), BF16), _sds((4, s, FSH), BF16)])(h, win, win)


def _ffn_out_dx_swiglu(df, wout, fac, after):
    s = df.shape[0]
    ts = _tile(s)
    nt = (((1,), (1,)), ((), ()))

    def body(df_ref, w_ref, fac_ref, after_ref, o_ref):
        d = lax.dot_general(df_ref[...], w_ref[...], nt, preferred_element_type=F32)
        o_ref[0] = (d * fac_ref[0]).astype(BF16)
        o_ref[1] = (d * fac_ref[1]).astype(BF16)

    spec = pl.BlockSpec((2, None, ts, FSH), lambda k, t: (0, k, t, 0))
    out = _pc(body, "ffn_out_dx", (4, s // ts),
              [pl.BlockSpec((ts, D), lambda k, t: (t, 0)), pl.BlockSpec((None, FSH, D), lambda k, t: (k, 0, 0)), spec,
               pl.BlockSpec(memory_space=pl.ANY)],
              spec, _sds((2, 4, s, FSH), BF16))(df, wout, fac, after)
    return out.reshape(N_DEV, s, FSH)


def _ffn_fwd(h, win, wout_of):
    s = h.shape[0]
    fac, act = _ffn_in_swiglu(h, win)
    f = _mm_sum("ffn_out", act, wout_of(act).reshape(4, FSH, D), min(s, 512))
    return f, (h, fac, act)


def _ffn_bwd(df, saved, win, wout, send, after):
    h, fac, act = saved
    s = h.shape[0]
    ts = s
    wout = wout.reshape(4, FSH, D)
    dwout = _mm("ffn_out_dw", act, df, ((0,), (0,)), (4, 2),
                pl.BlockSpec((None, s, FSH), lambda k, j: (k, 0, 0)), pl.BlockSpec((s, D // 2), lambda k, j: (0, j)),
                pl.BlockSpec((None, FSH, D // 2), lambda k, j: (k, 0, j)), _sds((4, FSH, D), BF16), after=after)
    dz = _ffn_out_dx_swiglu(df, wout, fac, send("w_out", dwout.reshape(N_DEV, D_FF // N_DEV, D)))
    dwin = _mm("ffn_in_dw", dz, h, ((0,), (0,)), (N_DEV, 2),
               pl.BlockSpec((None, s, FSH), lambda j, i: (j, 0, 0)), pl.BlockSpec((s, D // 2), lambda j, i: (0, i)),
               pl.BlockSpec((None, FSH, D // 2), lambda j, i: (j, 0, i)), _sds((N_DEV, FSH, D), BF16))
    return _mm_sum("ffn_in_dx", dz, win, min(s, 512), after=send("w_in", dwin))


def _shift_rows(v, k, row, s, back):
    if back:
        return jnp.where(row < s - k, pltpu.roll(v, s - k, 0), 0.0)
    return jnp.where(row >= k, pltpu.roll(v, k, 0), 0.0)


def _window_sum(v, w, row, s, back):
    k = 1
    while k < w:
        v = v + _shift_rows(v, k, row, s, back)
        k *= 2
    return v


def _pool_fwd(z, pool_w, pool_scale):
    s = z.shape[0]

    def body(z_ref, w_ref, sc_ref, y_ref, d_ref):
        row = lax.broadcasted_iota(jnp.int32, (s, HD), 0)
        for g, w in enumerate(POOL_WINDOWS):
            sl = slice(g * HD, (g + 1) * HD)
            a = z_ref[:, sl]
            cnt = jnp.minimum(row + 1, w).astype(F32)
            d = (_window_sum(a, w, row, s, False) / cnt - a).astype(BF16)
            d_ref[:, sl] = d
            y = jnp.dot(d, _bf(w_ref[g]), preferred_element_type=F32)
            y_ref[:, sl] = (y * sc_ref[:, sl]).astype(BF16)

    return _pc(body, "pool_fwd", (1,),
               [pl.BlockSpec((s, NH * HD), lambda i: (0, 0)), pl.BlockSpec((NH, HD, HD), lambda i: (0, 0, 0)),
                pl.BlockSpec((1, NH * HD), lambda i: (0, 0))],
               [pl.BlockSpec((s, NH * HD), lambda i: (0, 0))] * 2,
               [_sds((s, NH * HD), BF16)] * 2)(z, pool_w, pool_scale)


def _pool_bwd(dy, d, pool_w, pool_scale):
    s = dy.shape[0]

    def body(dy_ref, d_ref, w_ref, sc_ref, dz_ref, dw_ref, dsc_ref):
        row = lax.broadcasted_iota(jnp.int32, (s, HD), 0)
        for g, w in enumerate(POOL_WINDOWS):
            sl = slice(g * HD, (g + 1) * HD)
            dyg, dg, wg = dy_ref[:, sl], d_ref[:, sl], _bf(w_ref[g])
            yraw = jnp.dot(dg, wg, preferred_element_type=F32)
            dsc_ref[:, sl] = _colsum(dyg * yraw)
            dyr = _bf(dyg * sc_ref[:, sl])
            dw_ref[g] = lax.dot_general(dg, dyr, (((0,), (0,)), ((), ())), preferred_element_type=F32)
            dd = lax.dot_general(dyr, wg, (((1,), (1,)), ((), ())), preferred_element_type=F32)
            cnt = jnp.minimum(row + 1, w).astype(F32)
            dz_ref[:, sl] = (_window_sum(dd / cnt, w, row, s, True) - dd).astype(BF16)

    return _pc(body, "pool_bwd", (1,),
               [pl.BlockSpec((s, NH * HD), lambda i: (0, 0)), pl.BlockSpec((s, NH * HD), lambda i: (0, 0)),
                pl.BlockSpec((NH, HD, HD), lambda i: (0, 0, 0)), pl.BlockSpec((1, NH * HD), lambda i: (0, 0))],
               [pl.BlockSpec((s, NH * HD), lambda i: (0, 0)), pl.BlockSpec((NH, HD, HD), lambda i: (0, 0, 0)),
                pl.BlockSpec((1, NH * HD), lambda i: (0, 0))],
               [_sds((s, NH * HD), BF16), _sds((NH, HD, HD)), _sds((1, NH * HD))])(dy, d, pool_w, pool_scale)


def _gelu(v):
    return 0.5 * v * (1.0 + jnp.tanh(GELU_C * (v + 0.044715 * (v * v * v))))


def _gelu_and_grad(v):
    t = jnp.tanh(GELU_C * (v + 0.044715 * (v * v * v)))
    return 0.5 * v * (1.0 + t), 0.5 * (1.0 + t) + 0.5 * v * (1.0 - t * t) * (GELU_C * (1.0 + 3.0 * 0.044715 * (v * v)))


def _gelu_grad(v):
    return _gelu_and_grad(v)[1]


def _causal_mask():
    return lax.broadcasted_iota(jnp.int32, (HD, HD), 0) >= lax.broadcasted_iota(jnp.int32, (HD, HD), 1)


def _sgu_specs():
    w = NH * HD
    return [pl.BlockSpec((HD, w), lambda c: (c, 1)), pl.BlockSpec((HD, w), lambda c: (c, 2)),
            pl.BlockSpec((1, w), lambda c: (0, 0)), pl.BlockSpec((1, w), lambda c: (0, 0)),
            pl.BlockSpec((NH, HD, HD), lambda c: (0, 0, 0)), pl.BlockSpec((HD, LANE), lambda c: (0, 0))]


def _sgu_head(v, lng_ref, lnb_ref, w_ref, h):
    sl = slice(h * HD, (h + 1) * HD)
    vh = v[:, sl]
    xc = vh - jnp.mean(vh, axis=-1, keepdims=True)
    rs = lax.rsqrt(jnp.mean(xc * xc, axis=-1, keepdims=True) + EPS)
    vhat = xc * rs
    vn = _bf(vhat * lng_ref[:, sl] + lnb_ref[:, sl])
    wc = _bf(jnp.where(_causal_mask(), w_ref[h], 0.0))
    return sl, rs, vhat, vn, wc


def _sgu_fwd(z, ln_g, ln_b, sgu_w, sgu_bt):
    s = z.shape[0]

    def body(zu_ref, zv_ref, lng_ref, lnb_ref, w_ref, bt_ref, y_ref):
        u, v = _gelu(zu_ref[...]), _gelu(zv_ref[...])
        for h in range(NH):
            sl, _, _, vn, wc = _sgu_head(v, lng_ref, lnb_ref, w_ref, h)
            sp = jnp.dot(wc, vn, preferred_element_type=F32) + bt_ref[:, h:h + 1]
            y_ref[:, sl] = (u[:, sl] * sp).astype(BF16)

    return _pc(body, "sgu_fwd", (s // HD,), _sgu_specs(), pl.BlockSpec((HD, NH * HD), lambda c: (c, 0)),
               _sds((s, NH * HD), BF16))(z, z, ln_g, ln_b, sgu_w, sgu_bt)


def _sgu_bwd(z, dy, ln_g, ln_b, sgu_w, sgu_bt, head_sum):
    s = z.shape[0]
    w = NH * HD
    nc = s // HD

    def body(zu_ref, zv_ref, lng_ref, lnb_ref, w_ref, bt_ref, dy_ref, hs_ref,
             dzu_ref, dzv_ref, dlng_ref, dlnb_ref, dw_ref, dbt_ref, dsacc_ref):
        c = pl.program_id(0)
        _zero_at_first(c == 0, dsacc_ref, dw_ref, dlng_ref, dlnb_ref)
        zu, zv = zu_ref[...], zv_ref[...]
        (u, gu), (v, gv) = _gelu_and_grad(zu), _gelu_and_grad(zv)
        dyv = dy_ref[...]
        ds = dyv * u
        dsacc_ref[...] += ds
        for h in range(NH):
            sl, rs, vhat, vn, wc = _sgu_head(v, lng_ref, lnb_ref, w_ref, h)
            sp = jnp.dot(wc, vn, preferred_element_type=F32) + bt_ref[:, h:h + 1]
            dzu_ref[:, sl] = (dyv[:, sl] * sp * gu[:, sl]).astype(BF16)
            dsh = _bf(ds[:, sl])
            dwh = lax.dot_general(dsh, vn, (((1,), (1,)), ((), ())), preferred_element_type=F32)
            dw_ref[h] += jnp.where(_causal_mask(), dwh, 0.0)
            dvn = lax.dot_general(wc, dsh, (((0,), (0,)), ((), ())), preferred_element_type=F32)
            dlng_ref[:, sl] += _colsum(dvn * vhat)
            dlnb_ref[:, sl] += _colsum(dvn)
            dvh = dvn * lng_ref[:, sl]
            dv = rs * (dvh - jnp.mean(dvh, axis=-1, keepdims=True) - vhat * jnp.mean(dvh * vhat, axis=-1, keepdims=True))
            dzv_ref[:, sl] = (dv * gv[:, sl]).astype(BF16)

        @pl.when(c == nc - 1)
        def _():
            dbt_ref[...] = jnp.dot(dsacc_ref[...], hs_ref[...], preferred_element_type=F32, precision=HIGHEST)

    outs = _pc(body, "sgu_bwd", (nc,),
               _sgu_specs() + [pl.BlockSpec((HD, w), lambda c: (c, 1)), pl.BlockSpec((w, LANE), lambda c: (0, 0))],
               [pl.BlockSpec((HD, w), lambda c: (c, 0))] * 2 + [pl.BlockSpec((1, w), lambda c: (0, 0))] * 2
               + [pl.BlockSpec((NH, HD, HD), lambda c: (0, 0, 0)), pl.BlockSpec((HD, LANE), lambda c: (0, 0))],
               [_sds((s, w), BF16)] * 2 + [_sds((1, w))] * 2 + [_sds((NH, HD, HD)), _sds((HD, LANE))],
               scratch=[pltpu.VMEM((HD, w), F32)])(z, z, ln_g, ln_b, sgu_w, sgu_bt, dy, head_sum)
    return outs


def _cmul(ar, ai, br, bi):
    return ar * br - ai * bi, ar * bi + ai * br


def _ssm_prep(lam_re, lam_im, lam_re_rep, lam_im_rep, log_dt, b_re, b_im):
    def disc(lr, li, dt):
        mag = jnp.exp(lr * dt)
        return mag * jnp.cos(li * dt), mag * jnp.sin(li * dt)

    def body(lr_ref, li_ref, lrr_ref, lir_ref, ldt_ref, br_ref, bi_ref, or_ref, oi_ref, bbr_ref, bbi_ref):
        dt = jnp.exp(ldt_ref[...])
        or_ref[...], oi_ref[...] = disc(lr_ref[...], li_ref[...], dt)
        lr, li = lrr_ref[...], lir_ref[...]
        er, ei = disc(lr, li, dt)
        den = lr * lr + li * li
        kr = ((er - 1.0) * lr + ei * li) / den
        ki = (ei * lr - (er - 1.0) * li) / den
        bbr_ref[...], bbi_ref[...] = _cmul(kr, ki, br_ref[...], bi_ref[...])

    small = pl.BlockSpec((SSM_G, SSM_P), lambda i: (0, 0))
    wide = pl.BlockSpec((SSM_G, SSM_P * SSM_N), lambda i: (0, 0))
    col = pl.BlockSpec((SSM_G, 1), lambda i: (0, 0))
    return _pc(body, "ssm_prep", (1,), [small, small, wide, wide, col, wide, wide], [small, small, wide, wide],
               [_sds((SSM_G, SSM_P))] * 2 + [_sds((SSM_G, SSM_P * SSM_N))] * 2)(
        lam_re, lam_im, lam_re_rep, lam_im_rep, log_dt, b_re, b_im)


def _ssm_param_bwd(g_lam_re, g_lam_im, g_bb_re, g_bb_im, lam_re, lam_im, lam_re_rep, lam_im_rep, log_dt, b_re, b_im, seg):
    def body(glr_ref, gli_ref, gbr_ref, gbi_ref, lr_ref, li_ref, lrr_ref, lir_ref, ldt_ref, br_ref, bi_ref, seg_ref,
             dlr_ref, dli_ref, ddt_ref, dbr_ref, dbi_ref):
        dt = jnp.exp(ldt_ref[...])
        lr, li = lrr_ref[...], lir_ref[...]
        mag = jnp.exp(lr * dt)
        er, ei = mag * jnp.cos(li * dt), mag * jnp.sin(li * dt)
        den = lr * lr + li * li
        kr = ((er - 1.0) * lr + ei * li) / den
        ki = (ei * lr - (er - 1.0) * li) / den
        gbr, gbi = gbr_ref[...], gbi_ref[...]
        dbr_ref[...], dbi_ref[...] = _cmul(kr, -ki, gbr, gbi)
        tr, ti = _cmul(br_ref[...], -bi_ref[...], gbr, gbi)
        gkr = jnp.dot(tr, seg_ref[...], preferred_element_type=F32, precision=HIGHEST)
        gki = jnp.dot(ti, seg_ref[...], preferred_element_type=F32, precision=HIGHEST)
        lr, li = lr_ref[...], li_ref[...]
        mag = jnp.exp(lr * dt)
        er, ei = mag * jnp.cos(li * dt), mag * jnp.sin(li * dt)
        den = lr * lr + li * li
        ir, ii = lr / den, -li / den
        kr, ki = _cmul(er - 1.0, ei, ir, ii)
        ar, ai = _cmul(ir, -ii, gkr, gki)
        glr, gli = glr_ref[...] + ar, gli_ref[...] + ai
        qr, qi = _cmul(kr, ki, ir, ii)
        g1r, g1i = _cmul(-qr, qi, gkr, gki)
        g2r, g2i = _cmul(dt * er, -dt * ei, glr, gli)
        dlr_ref[...] = g1r + g2r
        dli_ref[...] = g1i + g2i
        wr, wi = _cmul(lr, li, er, ei)
        g_dt = jnp.sum(wr * glr + wi * gli, axis=-1, keepdims=True)
        ddt_ref[...] = jnp.broadcast_to(dt * g_dt, (SSM_G, LANE))

    small = pl.BlockSpec((SSM_G, SSM_P), lambda i: (0, 0))
    wide = pl.BlockSpec((SSM_G, SSM_P * SSM_N), lambda i: (0, 0))
    col = pl.BlockSpec((SSM_G, 1), lambda i: (0, 0))
    segs = pl.BlockSpec((SSM_P * SSM_N, SSM_P), lambda i: (0, 0))
    return _pc(body, "ssm_param_bwd", (1,), [small, small, wide, wide, small, small, wide, wide, col, wide, wide, segs],
               [small, small, pl.BlockSpec((SSM_G, LANE), lambda i: (0, 0)), wide, wide],
               [_sds((SSM_G, SSM_P))] * 2 + [_sds((SSM_G, LANE))] + [_sds((SSM_G, SSM_P * SSM_N))] * 2)(
        g_lam_re, g_lam_im, g_bb_re, g_bb_im, lam_re, lam_im, lam_re_rep, lam_im_rep, log_dt, b_re, b_im, seg)


SCAN_LANES = 512
SCAN_ROWS = 8


SCAN_GROUPS = SCAN_LANES // SSM_P
SCAN_COLS = SCAN_GROUPS * SSM_N
SCAN_CHUNK = 256


def _ssm_scan(name, v, w_in, lam_re, lam_im, w_out, reverse, states=None, u=None):
    s = v.shape[0]
    ln, rows, ch = SCAN_LANES, SCAN_ROWS, min(SCAN_CHUNK, s)
    nch, ntile = s // ch, ch // rows
    nt_dims = (((1,), (1,)), ((), ()))
    with_sum = states is not None
    tn_dims = (((0,), (0,)), ((), ()))

    def body(*refs):
        v_ref, win_ref, lr_ref, li_ref, wout_ref = refs[:5]
        n_in = 8 if with_sum else 5
        or_ref, oi_ref, y_ref = refs[n_in:n_in + 3]
        br_s, bi_s = refs[n_in + (9 if with_sum else 3):][:2]
        if with_sum:
            mb_s, mc_s = refs[-2:]
            mb_s[...] = jnp.zeros_like(mb_s)
            mc_s[...] = jnp.zeros_like(mc_s)
        l1 = (lr_ref[...], li_ref[...])
        pw = [l1]
        for _ in range(rows - 1):
            pw.append(_cmul(*pw[-1], *l1))
        row = lax.broadcasted_iota(jnp.int32, (rows, ln), 0)
        expo = (rows - row) if reverse else (row + 1)
        pr = jnp.zeros((rows, ln), F32)
        pi = jnp.zeros((rows, ln), F32)
        for e in range(1, rows + 1):
            pr = jnp.where(expo == e, pw[e - 1][0], pr)
            pi = jnp.where(expo == e, pw[e - 1][1], pi)
        lk = {}
        for k in (1, 2, 4):
            keep = (row < rows - k) if reverse else (row >= k)
            lk[k] = (jnp.where(keep, pw[k - 1][0], 0.0), jnp.where(keep, pw[k - 1][1], 0.0))

        def chunk(c, carry):
            q0 = pl.multiple_of(((nch - 1 - c) if reverse else c) * ch, ch)
            b = jnp.dot(_bf(v_ref[pl.ds(q0, ch), :]), win_ref[...], preferred_element_type=F32)
            br_s[...] = b[:, :ln]
            bi_s[...] = b[:, ln:]

            def step(i, carry):
                cr, ci = carry[:2]
                r0 = pl.multiple_of(((ntile - 1 - i) if reverse else i) * rows, rows)
                xr, xi = br_s[pl.ds(r0, rows), :], bi_s[pl.ds(r0, rows), :]
                for k in (1, 2, 4):
                    shift = rows - k if reverse else k
                    ar, ai = _cmul(lk[k][0], lk[k][1], pltpu.roll(xr, shift, 0), pltpu.roll(xi, shift, 0))
                    xr, xi = xr + ar, xi + ai
                ar, ai = _cmul(pr, pi, cr, ci)
                xr, xi = xr + ar, xi + ai
                g0 = pl.multiple_of(q0 + r0, rows)
                or_ref[pl.ds(g0, rows), :] = xr
                oi_ref[pl.ds(g0, rows), :] = xi
                if not with_sum:
                    return (xr[rows - 1:rows], xi[rows - 1:rows]) if not reverse else (xr[0:1], xi[0:1])
                nr = jnp.where(row == rows - 1, cr, pltpu.roll(xr, rows - 1, 0))
                ni = jnp.where(row == rows - 1, ci, pltpu.roll(xi, rows - 1, 0))
                sr, si = refs[5][pl.ds(g0, rows), :], refs[6][pl.ds(g0, rows), :]
                return xr[0:1], xi[0:1], carry[2] + (sr * nr + si * ni), carry[3] + (sr * ni - si * nr)

            carry = lax.fori_loop(0, ntile, step, carry)
            if with_sum:
                rows_c = pl.ds(q0, ch)
                uc, vc = _bf(refs[7][rows_c, :]), _bf(v_ref[rows_c, :])
                for scr, left, (right_re, right_im) in ((mb_s, uc, (or_ref, oi_ref)), (mc_s, vc, (refs[5], refs[6]))):
                    scr[:, :ln] += lax.dot_general(left, _bf(right_re[rows_c, :]), tn_dims, preferred_element_type=F32)
                    scr[:, ln:] += lax.dot_general(left, _bf(right_im[rows_c, :]), tn_dims, preferred_element_type=F32)
            w = wout_ref[...]
            y_ref[pl.ds(q0, ch), :] = (
                lax.dot_general(_bf(or_ref[pl.ds(q0, ch), :]), w[:, :ln], nt_dims, preferred_element_type=F32)
                + lax.dot_general(_bf(oi_ref[pl.ds(q0, ch), :]), w[:, ln:], nt_dims, preferred_element_type=F32))
            return carry

        zero = jnp.zeros((1, ln), F32)
        init = (zero, zero) + ((jnp.zeros((rows, ln), F32),) * 2 if with_sum else ())
        carry = lax.fori_loop(0, nch, chunk, init)
        if with_sum:
            refs[n_in + 3][...] = _colsum(carry[2])
            refs[n_in + 4][...] = _colsum(carry[3])
            row_g = lax.broadcasted_iota(jnp.int32, (SCAN_COLS, LANE), 0) // SSM_N
            lane_g = lax.broadcasted_iota(jnp.int32, (SCAN_COLS, LANE), 1) // SSM_P
            for scr, o_re, o_im in ((mb_s, refs[n_in + 5], refs[n_in + 6]), (mc_s, refs[n_in + 7], refs[n_in + 8])):
                for part, o_ref in enumerate((o_re, o_im)):
                    fold = jnp.zeros((SCAN_COLS, LANE), F32)
                    for cb in range(ln // LANE):
                        fold = fold + jnp.where(2 * cb + lane_g == row_g, scr[:, part * ln + cb * LANE:part * ln + (cb + 1) * LANE], 0.0)
                    o_ref[...] = jnp.where(row_g % 2 == 0, fold, pltpu.roll(fold, SSM_P, 1))

    vec = pl.BlockSpec((1, ln), lambda j: (0, j))
    blk = pl.BlockSpec((s, ln), lambda j: (0, j))
    cols = pl.BlockSpec((s, SCAN_COLS), lambda j: (0, j))
    wspec = pl.BlockSpec((None, SCAN_COLS, 2 * ln), lambda j: (j, 0, 0))
    ins, args = [cols, wspec, vec, vec, wspec], [v, w_in, lam_re, lam_im, w_out]
    outs, shapes = [blk, blk, cols], [_sds((s, SSM_L))] * 2 + [_sds((s, SSM_G * SSM_N))]
    scratch = [pltpu.VMEM((ch, ln), F32)] * 2
    if with_sum:
        own = pl.BlockSpec((None, SCAN_COLS, LANE), lambda j: (j, 0, 0))
        ins, args = ins + [blk, blk, cols], args + list(states) + [u]
        outs = outs + [vec, vec] + [own] * 4
        shapes = shapes + [_sds((1, SSM_L))] * 2 + [_sds((SSM_L // ln, SCAN_COLS, LANE))] * 4
        scratch = scratch + [pltpu.VMEM((SCAN_COLS, 2 * ln), F32)] * 2
    return _pc(body, name, (SSM_L // ln,), ins, outs, shapes, scratch=scratch)(*args)


def _ssm_act_fwd(y, u, d_skip):
    s = y.shape[0]
    ts = min(s, 512)

    def body(y_ref, u_ref, d_ref, o_ref):
        o_ref[...] = _gelu(y_ref[...] + d_ref[...] * u_ref[...]).astype(BF16)

    return _pc(body, "ssm_act_fwd", (s // ts,), [_row_spec(ts, D)] * 2 + [_vec_spec(D)], _row_spec(ts, D),
               _sds((s, D), BF16))(y, u, d_skip)


def _ssm_act_bwd(dg, y, u, d_skip):
    s = y.shape[0]
    ts = min(s, 512)

    def body(dg_ref, y_ref, u_ref, d_ref, dy_ref, dd_ref):
        uv = u_ref[...]
        dy = dg_ref[...] * _gelu_grad(y_ref[...] + d_ref[...] * uv)
        dy_ref[...] = dy.astype(BF16)
        _acc(dd_ref, pl.program_id(0) == 0, _colsum(dy * uv))

    return _pc(body, "ssm_act_bwd", (s // ts,), [_row_spec(ts, D)] * 3 + [_vec_spec(D)], [_row_spec(ts, D), _vec_spec(D)],
               [_sds((s, D), BF16), _sds((1, D))])(dg, y, u, d_skip)


def _axpy(a, b, d_skip):
    s = a.shape[0]
    ts = min(s, 512)

    def body(a_ref, b_ref, d_ref, o_ref):
        o_ref[...] = (a_ref[...] + d_ref[...] * b_ref[...].astype(F32)).astype(BF16)

    return _pc(body, "ssm_du", (s // ts,), [_row_spec(ts, D)] * 2 + [_vec_spec(D)], _row_spec(ts, D),
               _sds((s, D), BF16))(a, b, d_skip)


def _glu_fwd(zz):
    s = zz.shape[0]
    ts = min(s, 512)

    def body(a_ref, b_ref, o_ref):
        o_ref[...] = a_ref[...] * _sigmoid(b_ref[...])

    return _pc(body, "glu_fwd", (s // ts,), [_row_spec(ts, D, 0), _row_spec(ts, D, 1)], _row_spec(ts, D), _sds((s, D)))(zz, zz)


def _glu_bwd(zz, df):
    s = zz.shape[0]
    ts = min(s, 512)

    def body(a_ref, b_ref, df_ref, o_ref):
        sg = _sigmoid(b_ref[...])
        dfv = df_ref[...].astype(F32)
        o_ref[:, :D] = (dfv * sg).astype(BF16)
        o_ref[:, D:] = (dfv * a_ref[...] * sg * (1.0 - sg)).astype(BF16)

    return _pc(body, "glu_bwd", (s // ts,), [_row_spec(ts, D, 0), _row_spec(ts, D, 1), _row_spec(ts, D)],
               _row_spec(ts, 2 * D), _sds((s, 2 * D), BF16))(zz, zz, df)


def _ssm_block_diag(m_re, m_im):
    rows, half = SCAN_COLS, SCAN_LANES
    expand = jnp.tile(jnp.eye(SSM_P, dtype=BF16), (1, SCAN_GROUPS))

    def body(mr_ref, mi_ref, e_ref, o_ref):
        keep = (lax.broadcasted_iota(jnp.int32, (rows, half), 0) // SSM_N
                == lax.broadcasted_iota(jnp.int32, (rows, half), 1) // SSM_P)
        for part, m_ref in enumerate((mr_ref, mi_ref)):
            t = jnp.dot(_bf(m_ref[...]), e_ref[...], preferred_element_type=F32)
            o_ref[:, part * half:(part + 1) * half] = jnp.where(keep, t, 0.0).astype(BF16)

    blk = pl.BlockSpec((rows, SSM_P), lambda q: (q, 0))
    nb = SSM_G // SCAN_GROUPS
    return _pc(body, "ssm_block_diag", (nb,), [blk, blk, pl.BlockSpec((SSM_P, half), lambda q: (0, 0))],
               pl.BlockSpec((None, rows, 2 * half), lambda q: (q, 0, 0)), _sds((nb, rows, 2 * half), BF16))(m_re, m_im, expand)


def _mod_part(c_all, ada_w):
    n = ada_w.shape[-1]

    def body(c_ref, w_ref, o_ref):
        cv = c_ref[...]
        cond = _bf(cv * _sigmoid(cv))
        o_ref[...] = jnp.dot(cond, _bf(w_ref[...]), preferred_element_type=F32)

    return _pc(body, "mod_part", (2,), [pl.BlockSpec((N_DEV, D), lambda l: (0, 0)), pl.BlockSpec((None, D, n), lambda l: (l, 0, 0))],
               pl.BlockSpec((None, N_DEV, n), lambda l: (l, 0, 0)), _sds((2, N_DEV, n)))(c_all, ada_w)


def _ada_w_grad(c_all_t, dmod):
    n = dmod.shape[-1]
    tr = 128

    def body(c_ref, d_ref, o_ref):
        cv = c_ref[...]
        cond = _bf(cv * _sigmoid(cv)).astype(F32)
        dm = _bf(d_ref[...]).astype(F32)
        acc = cond[:, 0:1] * dm[0:1, :]
        for b in range(1, N_DEV):
            acc = acc + cond[:, b:b + 1] * dm[b:b + 1, :]
        o_ref[...] = acc

    return _pc(body, "ada_w_grad", (2, D // tr),
               [pl.BlockSpec((tr, N_DEV), lambda l, t: (t, 0)), pl.BlockSpec((None, N_DEV, n), lambda l, t: (l, 0, 0))],
               pl.BlockSpec((None, tr, n), lambda l, t: (l, t, 0)), _sds((2, D, n)))(c_all_t, dmod)


def _adamw(name, parts, w, m, v, slot=0, prev=None, after=None):
    p, r, c = parts.shape
    tr = r
    while tr * c * 4 > (1 << 20) and tr % 16 == 0:
        tr //= 2
    nt = r // tr

    def body(p_ref, w_ref, m_ref, v_ref, *rest):
        g_ref, d_ref, nm_ref, nv_ref = rest[-4:]
        g = p_ref[0].astype(F32)
        for i in range(1, p):
            g = g + p_ref[i].astype(F32)
        g_ref[...] = g
        d_ref[...], nm_ref[...], nv_ref[...] = _adam_update(g, w_ref[...], m_ref[...], v_ref[...])

    blk = pl.BlockSpec((tr, c), lambda t: (slot * nt + t, 0))
    in_specs = [pl.BlockSpec((p, tr, c), lambda t: (0, t, 0)), blk, blk, blk]
    unread = list(prev or []) + ([after] if after is not None else [])
    return pl.pallas_call(
        body, name=name, grid=(nt,), in_specs=in_specs + [pl.BlockSpec(memory_space=pl.ANY)] * len(unread), out_specs=[blk] * 4,
        out_shape=[_sds(w.shape)] * 4, input_output_aliases={4 + i: i for i in range(4)} if prev else {},
        compiler_params=pltpu.CompilerParams(dimension_semantics=("arbitrary",), vmem_limit_bytes=VMEM_LIMIT_BYTES))(
        *_in_hbm((parts, w, m, v)), *unread)


def _adam_update(g, w, m, v):
    m2 = B1 * m + (1.0 - B1) * g
    v2 = B2 * v + (1.0 - B2) * (g * g)
    m_hat = m2 / (1.0 - B1 ** STEP)
    v_hat = v2 / (1.0 - B2 ** STEP)
    return -LR * (m_hat / (jnp.sqrt(v_hat) + ADAM_EPS) + WD * w), m2, v2


def _adamw_many(name, items, after):
    n = len(items)

    def body(*refs):
        outs = refs[4 * n + 1:]
        for i in range(n):
            g, w, m, v = (r[...] for r in refs[4 * i:4 * i + 4])
            for o, val in zip(outs[3 * i:3 * i + 3], _adam_update(g, w, m, v)):
                o[...] = val

    full = lambda a: pl.BlockSpec(a.shape, lambda t: (0, 0))
    flat = [a for item in items for a in item]
    res = _pc(body, name, (1,), [full(a) for a in flat] + [pl.BlockSpec(memory_space=pl.ANY)],
              [full(item[1]) for item in items for _ in range(3)],
              [_sds(item[1].shape) for item in items for _ in range(3)])(*flat, after)
    return [tuple(res[3 * i:3 * i + 3]) for i in range(n)]


def _sum_parts(parts):
    p, r, c = parts.shape
    tr = r
    while tr * c * 4 > (1 << 19) and tr % 16 == 0:
        tr //= 2

    def body(p_ref, o_ref):
        g = p_ref[0]
        for i in range(1, p):
            g = g + p_ref[i]
        o_ref[...] = g

    return _pc(body, "sum_parts", (r // tr,), [pl.BlockSpec((p, tr, c), lambda t: (0, t, 0))], pl.BlockSpec((tr, c), lambda t: (t, 0)),
               _sds((r, c)))(parts)


def _place():
    x, y, c = lax.axis_index("x"), lax.axis_index("y"), lax.axis_index("c")
    peers = []
    for k in range(1, N_DEV):
        px = (1 - x) if k & 4 else x
        py = (1 - y) if k & 2 else y
        pc = (1 - c) if k & 1 else c
        peers.append(((px, py, pc), 4 * px + 2 * py + pc))
    return 4 * x + 2 * y + c, peers


def _at(ref, idx):
    return ref if idx is None else ref.at[idx]


def _exchange_copies(plan, n, src_refs, dst_refs, send_sems, recv_sems, local_sems=None, with_arrivals=True):
    me, peers = _place()
    local = [] if local_sems is None else [
        pltpu.make_async_copy(_at(src_refs[si], sx), _at(dst_refs[di], dx), local_sems.at[i])
        for i, (si, sx, di, dx) in enumerate(plan(me, me, 0))]

    def remote(k, i, dev, entry):
        si, sx, di, dx = entry
        return pltpu.make_async_remote_copy(_at(src_refs[si], sx), _at(dst_refs[di], dx), send_sems.at[k * n + i], recv_sems.at[k * n + i],
                                            device_id=dev, device_id_type=MESH)

    sends = [remote(k, i, dev, e) for k, (dev, peer) in enumerate(peers) for i, e in enumerate(plan(me, peer, k + 1))]
    if not with_arrivals:
        return local, sends, []
    arrivals = [remote(k, i, dev, e) for k, (dev, peer) in enumerate(peers) for i, e in enumerate(plan(peer, me, k + 1))]
    return local, sends, arrivals


def _sem_shapes(n_copies, local=True):
    sems = [pltpu.SemaphoreType.DMA(((N_DEV - 1) * n_copies,)), pltpu.SemaphoreType.DMA(((N_DEV - 1) * n_copies,))]
    return sems + [pltpu.SemaphoreType.DMA((n_copies,))] if local else sems


def _exchange(name, srcs, dst_shapes, plan, n_copies):
    ns, nd = len(srcs), len(dst_shapes)

    def body(*refs):
        local, sends, arrivals = _exchange_copies(plan, n_copies, refs[:ns], refs[ns:ns + nd], *refs[ns + nd:])
        for cp in local + sends:
            cp.start()
        for cp in arrivals:
            cp.wait_recv()
        for cp in sends:
            cp.wait_send()
        for cp in local:
            cp.wait()

    any_spec = pl.BlockSpec(memory_space=pl.ANY)
    return pl.pallas_call(
        body, name=name, in_specs=[any_spec] * ns, out_specs=[any_spec] * nd, out_shape=list(dst_shapes),
        scratch_shapes=_sem_shapes(n_copies))(*srcs)


HBM_SPEC = pl.BlockSpec(memory_space=pltpu.HBM)
SEM_SPEC = pl.BlockSpec(memory_space=pltpu.SEMAPHORE)
ANY_SPEC = pl.BlockSpec(memory_space=pl.ANY)
TOKEN_SPEC = pl.BlockSpec(memory_space=pltpu.VMEM)
SIDE_EFFECT = pltpu.SideEffectType.DATAFLOW_SIDE_EFFECTING


def _wait_all(local, sends, arrivals):
    for cp in arrivals:
        cp.wait_recv()
    for cp in sends:
        cp.wait_send()
    for cp in local:
        cp.wait()


def _exchange_start(name, srcs, dst_shapes, plan, n_copies, order):
    ns, nd = len(srcs), len(dst_shapes)
    nb = ns + nd

    def body(*refs):
        local, sends, _ = _exchange_copies(plan, n_copies, refs[:ns], refs[ns:nb], *refs[nb + 1:nb + 4], with_arrivals=False)
        for cp in local + sends:
            cp.start()
        refs[-1][...] = jnp.zeros((8, LANE), F32)

    lands = [pltpu.with_memory_space_constraint(lax.empty(d.shape, d.dtype), pltpu.HBM) for d in dst_shapes]
    srcs = [pltpu.with_memory_space_constraint(a, pltpu.HBM) for a in srcs]
    bufs = srcs + lands
    out = pl.pallas_call(
        body, name=name, in_specs=[HBM_SPEC] * nb + [ANY_SPEC],
        out_specs=[SEM_SPEC] * 3 + [HBM_SPEC] * nb + [TOKEN_SPEC],
        out_shape=_sem_shapes(n_copies) + [pltpu.HBM(a.shape, a.dtype) for a in bufs] + [_sds((8, LANE))],
        input_output_aliases={i: 3 + i for i in range(nb)},
        compiler_params=pltpu.CompilerParams(has_side_effects=SIDE_EFFECT))(*bufs, order)
    return out[:3], out[3:3 + ns], out[3 + ns:3 + nb], out[-1]


def _exchange_relay(name, sems, srcs, lands, plan, n_copies, plan2, n_copies2, after):
    ns, nd = len(srcs), len(lands)
    nb = ns + nd

    def body(*refs):
        land_refs = refs[ns:nb]
        _wait_all(*_exchange_copies(plan, n_copies, refs[:ns], land_refs, *refs[nb:nb + 3]))
        _, sends, _ = _exchange_copies(plan2, n_copies2, land_refs, land_refs, *refs[nb + 4:nb + 6], with_arrivals=False)
        for cp in sends:
            cp.start()
        refs[-1][...] = jnp.zeros((8, LANE), F32)

    out = pl.pallas_call(
        body, name=name, in_specs=[HBM_SPEC] * nb + [SEM_SPEC] * 3 + [ANY_SPEC],
        out_specs=[SEM_SPEC] * 2 + [HBM_SPEC] * nd + [TOKEN_SPEC],
        out_shape=_sem_shapes(n_copies2, local=False) + [pltpu.HBM(a.shape, a.dtype) for a in lands] + [_sds((8, LANE))],
        input_output_aliases={ns + i: 2 + i for i in range(nd)},
        compiler_params=pltpu.CompilerParams(has_side_effects=SIDE_EFFECT))(*srcs, *lands, *sems, after)
    return out[:2], out[2:2 + nd], out[-1]


def _exchange_wait(name, sems, srcs, lands, plan, n_copies, after):
    srcs = [] if srcs is None else list(srcs)
    ns, nd = len(srcs), len(lands)
    nb = ns + nd

    def body(*refs):
        land_refs = refs[ns:nb]
        _wait_all(*_exchange_copies(plan, n_copies, refs[:ns] if ns else land_refs, land_refs, *refs[nb:nb + len(sems)]))

    bufs = srcs + list(lands)
    out = pl.pallas_call(
        body, name=name, in_specs=[HBM_SPEC] * nb + [SEM_SPEC] * len(sems) + [ANY_SPEC],
        out_specs=[HBM_SPEC] * nb, out_shape=[pltpu.HBM(a.shape, a.dtype) for a in bufs],
        input_output_aliases={i: i for i in range(nb)},
        compiler_params=pltpu.CompilerParams(has_side_effects=SIDE_EFFECT))(*bufs, *sems, after)
    return out[ns:]


def _all_gather(name, arrs):
    plan = lambda me, peer, k: [(i, None, i, me) for i in range(len(arrs))]
    return _exchange(name, arrs, [_sds((N_DEV,) + a.shape, a.dtype) for a in arrs], plan, len(arrs))


def _mix0_fwd(h, p):
    z = _mm_nt("mix0_in", h, p["ab_w_in"])
    y_a, d = _pool_fwd(z, p["pool_w"], p["pool_scale"])
    y_b = _sgu_fwd(z, p["sgu_ln_g"], p["sgu_ln_b"], p["sgu_w"], p["sgu_bt"])
    ycat = jnp.concatenate([y_a, y_b], axis=1)
    return _mm_nn("mix0_out", ycat, p["ab_w_out"]), (h, z, d, ycat)


def _mix0_bwd(df, saved, p, after):
    h, z, d, ycat = saved
    dycat = _mm_nt("mix0_out_dx", df, p["ab_w_out"], after=after)
    g = {"ab_w_out": _mm_tn("mix0_out_dw", ycat, df, BF16)}
    dz_p, g["pool_w"], g["pool_scale"] = _pool_bwd(dycat, d, p["pool_w"], p["pool_scale"])
    dz_u, dz_v, g["sgu_ln_g"], g["sgu_ln_b"], g["sgu_w"], dbt = _sgu_bwd(
        z, dycat, p["sgu_ln_g"], p["sgu_ln_b"], p["sgu_w"], p["sgu_bt"], p["head_sum"])
    g["sgu_b"] = dbt[:, :NH].T
    dz = jnp.concatenate([dz_p, dz_u, dz_v], axis=1)
    g["ab_w_in"] = _mm_tn("mix0_in_dw", dz, h, BF16)
    return _mm_nn("mix0_in_dx", dz, p["ab_w_in"]), g


def _mix1_fwd(h, p):
    u = _mm_nn("ssm_w_in", h, p["ssm_w_in"])
    x_re, x_im, y = _ssm_scan("ssm_scan_fwd", u, p["wb_bd"], p["lam_bar_re"], p["lam_bar_im"], p["wc_bd"], False)
    g = _ssm_act_fwd(y, u, p["ssm_d"])
    zz = _mm_nn("ssm_glu", g, p["ssm_w_glu"])
    return _glu_fwd(zz), (h, u, x_re, x_im, y, g, zz)


def _mix1_bwd(df, saved, p, after):
    h, u, x_re, x_im, y, g, zz = saved
    gr = {}
    dzz = _glu_bwd(zz, df)
    dg = _mm_nt("ssm_glu_dx", dzz, p["ssm_w_glu"], after=after)
    gr["ssm_w_glu"] = _mm_tn("ssm_glu_dw", g, dzz, BF16)
    dy, gr["ssm_d"] = _ssm_act_bwd(dg, y, u, p["ssm_d"])
    _, _, du_ssm, g_lam_re, g_lam_im, mb_re, mb_im, mc_re, mc_im = _ssm_scan(
        "ssm_scan_bwd", dy, p["wc_bd"], p["lam_bar_re"], -p["lam_bar_im"], p["wb_bd"], True, states=(x_re, x_im), u=u)
    du = _axpy(du_ssm, dy, p["ssm_d"])
    gr["ssm_w_in"] = _mm_tn("ssm_w_in_dw", h, du, BF16)
    dh = _mm_nt("ssm_w_in_dx", du, p["ssm_w_in"])
    per_group = lambda m: m[:, :, :SSM_P].reshape(SSM_G, SSM_N, SSM_P)
    gr["ssm_c_re"] = per_group(mc_re)
    gr["ssm_c_im"] = -per_group(mc_im)
    dlr, dli, ddt, dbr, dbi = _ssm_param_bwd(
        g_lam_re.reshape(SSM_G, SSM_P), g_lam_im.reshape(SSM_G, SSM_P),
        per_group(mb_re).reshape(SSM_G, SSM_N * SSM_P), per_group(mb_im).reshape(SSM_G, SSM_N * SSM_P),
        p["lam_re"], p["lam_im"], p["lam_re_rep"], p["lam_im_rep"], p["log_dt"], p["b_re"], p["b_im"], p["seg"])
    gr["ssm_lam_re"], gr["ssm_lam_im"], gr["ssm_log_dt"] = dlr, dli, ddt[:, 0]
    gr["ssm_b_re"] = dbr.reshape(SSM_G, SSM_N, SSM_P)
    gr["ssm_b_im"] = dbi.reshape(SSM_G, SSM_N, SSM_P)
    return dh, gr


def _ssm_params(lam_re, lam_im, b_re, b_im, c_re, c_im, log_dt):
    wide = lambda b: b.transpose(0, 2, 1).reshape(SSM_G, SSM_N * SSM_P)
    p = {"lam_re": lam_re, "lam_im": lam_im, "log_dt": log_dt.reshape(SSM_G, 1),
         "lam_re_rep": jnp.tile(lam_re, (1, SSM_N)), "lam_im_rep": jnp.tile(lam_im, (1, SSM_N)), "b_re": wide(b_re), "b_im": wide(b_im)}
    lbr, lbi, bbr, bbi = _ssm_prep(lam_re, lam_im, p["lam_re_rep"], p["lam_im_rep"], p["log_dt"], p["b_re"], p["b_im"])
    p["lam_bar_re"], p["lam_bar_im"] = lbr.reshape(1, SSM_L), lbi.reshape(1, SSM_L)
    rows = lambda m: m.reshape(SSM_G * SSM_N, SSM_P)
    p["wb_bd"] = _ssm_block_diag(rows(bbr), rows(bbi))
    p["wc_bd"] = _ssm_block_diag(rows(c_re), rows(-c_im))
    p["seg"] = jnp.tile(jnp.eye(SSM_P, dtype=F32), (SSM_N, 1))
    return p


RES_WEIGHT = (0.5, 1.0, 0.5)


def _local_step(x, tgt, vecs, weights_of, on_part, on_grads):
    def fns(i, w):
        if i % 3 != 1:
            win, wout_of = w
            return ((lambda h: _ffn_fwd(h, win, wout_of)),
                    (lambda df, sv, after: (_ffn_bwd(df, sv, win, wout_of(None), lambda tag, part: on_part(i, tag, part), after), None)))
        if i == 1:
            return (lambda h: _mix0_fwd(h, w)), (lambda df, sv, after: _mix0_bwd(df, sv, w, after))
        return (lambda h: _mix1_fwd(h, w)), (lambda df, sv, after: _mix1_bwd(df, sv, w, after))

    rw = RES_WEIGHT * 2
    saved, bwd = [], []
    f = None
    for i in range(6):
        w, token = weights_of(i, x if i == 0 else f)
        fwd, b = fns(i, w)
        if i == 0:
            h = _prenorm_fwd(x, vecs, 0, token)
        else:
            x, h = _post_pre_fwd(x, f, vecs, i, rw[i - 1], token)
        f, inner = fwd(h)
        saved.append((x, f, inner))
        bwd.append(b)
    loss_row, dx = _loss_fwd_bwd(_postnorm_fwd(x, f, vecs, 5, rw[5]), tgt)
    df, dv_top = _postnorm_bwd(dx, f, vecs, 5, rw[5])
    token = jnp.zeros((8, LANE), F32)
    for i in reversed(range(6)):
        x_i, _, inner = saved[i]
        dh, extra = bwd[i](df, inner, token)
        if i > 0:
            dx, df, dv = _pre_post_bwd(dx, dh, x_i, saved[i - 1][1], vecs, i, rw[i - 1])
        else:
            dx, dv = _prenorm_bwd(dx, dh, x_i, vecs, 0)
        token = on_grads(i, extra, dv, dv_top if i == 5 else None, loss_row)
    return dx


def _pad_rows(v, rows):
    return jnp.pad(v, (0, rows * LANE - v.shape[0])).reshape(rows, LANE)


def _pack(parts):
    flat, layout, off = [], [], 0
    for a in parts:
        n = a.size
        padded = -(-n // LANE) * LANE
        flat.append(jnp.pad(a.reshape(-1).astype(F32), (0, padded - n)))
        layout.append((off, n, a.shape))
        off += padded
    return jnp.concatenate(flat), layout


def _unpack(flat, layout):
    return [flat[off:off + n].reshape(shape) for off, n, shape in layout]


SMALL_REPLICATED = ["ada_b", "pool_w", "pool_scale", "sgu_ln_g", "sgu_ln_b", "sgu_w", "sgu_b", "ssm_lam_re", "ssm_lam_im",
                    "ssm_b_re", "ssm_b_im", "ssm_c_re", "ssm_c_im", "ssm_log_dt"]
SMALL_SHARDED = ["norm_pre", "norm_post", "ssm_d"]
TRANSPOSED = ["ffn_w_in", "ab_w_in", "ssm_b_re", "ssm_b_im"]
WEIGHTS = ['ada_w', 'ada_b', 'norm_pre', 'norm_post', 'ffn_w_in', 'ffn_w_out', 'ab_w_in', 'pool_w', 'pool_scale', 'sgu_ln_g',
           'sgu_ln_b', 'sgu_w', 'sgu_b', 'ab_w_out', 'ssm_w_in', 'ssm_lam_re', 'ssm_lam_im', 'ssm_b_re', 'ssm_b_im', 'ssm_c_re',
           'ssm_c_im', 'ssm_d', 'ssm_log_dt', 'ssm_w_glu']


def kernel(x, c, ada_w, ada_b, norm_pre, norm_post, ffn_w_in, ffn_w_out, ab_w_in, pool_w, pool_scale, sgu_ln_g, sgu_ln_b, sgu_w, sgu_b, ab_w_out, ssm_w_in, ssm_lam_re, ssm_lam_im, ssm_b_re, ssm_b_im, ssm_c_re, ssm_c_im, ssm_d, ssm_log_dt, ssm_w_glu, loss_target, m_ada_w, m_ada_b, m_norm_pre, m_norm_post, m_ffn_w_in, m_ffn_w_out, m_ab_w_in, m_pool_w, m_pool_scale, m_sgu_ln_g, m_sgu_ln_b, m_sgu_w, m_sgu_b, m_ab_w_out, m_ssm_w_in, m_ssm_lam_re, m_ssm_lam_im, m_ssm_b_re, m_ssm_b_im, m_ssm_c_re, m_ssm_c_im, m_ssm_d, m_ssm_log_dt, m_ssm_w_glu, v_ada_w, v_ada_b, v_norm_pre, v_norm_post, v_ffn_w_in, v_ffn_w_out, v_ab_w_in, v_pool_w, v_pool_scale, v_sgu_ln_g, v_sgu_ln_b, v_sgu_w, v_sgu_b, v_ab_w_out, v_ssm_w_in, v_ssm_lam_re, v_ssm_lam_im, v_ssm_b_re, v_ssm_b_im, v_ssm_c_re, v_ssm_c_im, v_ssm_d, v_ssm_log_dt, v_ssm_w_glu):
    args = locals()
    wts = {n: args[n] for n in WEIGHTS}
    mom = {n: args["m_" + n] for n in WEIGHTS}
    var = {n: args["v_" + n] for n in WEIGHTS}
    for n in TRANSPOSED:
        for t in (wts, mom, var):
            t[n] = jnp.swapaxes(t[n], -1, -2)
    me = 4 * lax.axis_index("x") + 2 * lax.axis_index("y") + lax.axis_index("c")
    s = x.shape[1]
    nd = D // N_DEV

    small_in, small_in_layout = _pack([c, norm_pre, norm_post, ssm_d])
    small_rows = -(-small_in.shape[0] // (8 * LANE)) * 8
    (g_small,) = _all_gather("gather_small", [_pad_rows(small_in, small_rows)])
    g_small = g_small.reshape(N_DEV, -1)
    c_all, npre_g, npost_g, sd_g = [jnp.stack([_unpack(g_small[j], small_in_layout)[i] for j in range(N_DEV)]) for i in range(4)]
    c_all = c_all.reshape(N_DEV, D)
    norm_pre_full = npre_g.transpose(1, 2, 0, 3).reshape(2, 3, D)
    norm_post_full = npost_g.transpose(1, 2, 0, 3).reshape(2, 3, D)
    ssm_d_full = sd_g.transpose(1, 0, 2).reshape(1, D)

    nw = ada_w.shape[-1]
    (mod_g,) = _all_gather("gather_mod", [_mod_part(c_all, ada_w)])
    mod = lax.dynamic_index_in_dim(mod_g, me, axis=2, keepdims=False)
    mod = (mod.transpose(1, 0, 2).reshape(2, N_DEV * nw) + ada_b).reshape(2, 3, 3, D)

    w_in_t = wts["ffn_w_in"]
    shards = [[w_in_t[0, 0]], [ffn_w_out[0, 0]], [wts["ab_w_in"][0], ab_w_out[0]], [w_in_t[0, 1], ffn_w_out[0, 1]],
              [w_in_t[1, 0], ffn_w_out[1, 0]], [ssm_w_in[0], ssm_w_glu[0]], [w_in_t[1, 1], ffn_w_out[1, 1]]]
    same_core = (2, 4, 6)

    def gather_plan(n):
        return lambda me_, peer_, k: [(a, None, a, me_) for a in range(n)] if k in (0, 1) + same_core else []

    def relay_plan(n):
        return lambda me_, peer_, k: [(a, me_ ^ kk, a, me_ ^ kk) for kk in same_core for a in range(n)] if k == 1 else []

    gathers, relays = [], {}
    token = mod_g
    for g, group in enumerate(shards):
        group = [a.astype(BF16) for a in group]
        sems, srcs_thru, lands, token = _exchange_start(
            f"gather_start_{g}", group, [_sds((N_DEV,) + a.shape, BF16) for a in group], gather_plan(len(group)), len(group), token)
        gathers.append((sems, srcs_thru, lands))
    mod6 = mod.reshape(6, 3, D)
    vecs = jnp.stack([norm_pre_full.reshape(6, D), mod6[:, 1], mod6[:, 0], norm_post_full.reshape(6, D), mod6[:, 2]]
                     + [jnp.zeros((6, D), F32)] * 3, axis=1)
    vecs = vecs + token[0, 0]

    def relay(g, after):
        sems, srcs_thru, lands = gathers[g]
        n = len(lands)
        relays[g] = _exchange_relay(f"gather_relay_{g}", sems, srcs_thru, lands, gather_plan(n), n, relay_plan(n), 3 * n, after)

    def fetch(g, after):
        if g not in relays:
            relay(g, after)
        sems, lands, token = relays[g]
        n = len(lands)
        got = _exchange_wait(f"gather_wait_{g}", sems, None, lands, relay_plan(n), 3 * n, after)
        if 0 < g < len(gathers) - 1:
            relay(g + 1, got[0])
            token = relays[g + 1][2]
        return got, token

    head_sum = jnp.repeat(jnp.eye(NH, LANE, dtype=F32), HD, axis=0)
    mix0 = {"pool_w": pool_w[0], "pool_scale": pool_scale, "sgu_ln_g": sgu_ln_g, "sgu_ln_b": sgu_ln_b, "sgu_w": sgu_w[0],
            "sgu_bt": jnp.pad(sgu_b[0].T, ((0, 0), (0, LANE - NH))), "head_sum": head_sum}
    mix1 = _ssm_params(ssm_lam_re[0], ssm_lam_im[0], ssm_b_re[0], ssm_b_im[0], ssm_c_re[0], ssm_c_im[0], ssm_log_dt[0])
    mix1["ssm_d"] = ssm_d_full

    def weights_of(i, x_in):
        if i == 0:
            (win,), token = fetch(0, x_in)
            cache = []

            def wout_of(z):
                if not cache:
                    cache.append(fetch(1, z)[0][0])
                return cache[0]

            return (win, wout_of), token
        (a, b), token = fetch(i + 1, x_in)
        if i % 3 != 1:
            return (a, lambda z: b), token
        if i == 1:
            return dict(mix0, ab_w_in=a.reshape(-1, D), ab_w_out=b.reshape(D, D)), token
        return dict(mix1, ssm_w_in=a.reshape(D, D), ssm_w_glu=b.transpose(1, 0, 2).reshape(D, -1)), token

    def shard_cols(a):
        r = a.shape[0]
        return a.reshape(r, N_DEV, -1).transpose(1, 0, 2)

    scatter_plan = lambda me_, peer_, k: [(0, peer_, 0, me_), (1, peer_, 1, me_)]
    scatter_plan1 = lambda me_, peer_, k: [(0, peer_, 0, me_)]
    scatters = []
    last_token = [jnp.zeros((8, LANE), F32)]
    pieces, mixer, bundles = {}, {}, {}
    bundle_plan = lambda me_, peer_, k: [(0, None, 0, me_)]

    held = {}

    def on_part(i, tag, part):
        if i != 0 and tag == "w_out":
            held[i] = part
            return last_token[0]
        names, parts, plan = (("ffn_" + tag,), [part], scatter_plan1) if i == 0 else (("ffn_w_out", "ffn_w_in"), [held[i], part], scatter_plan)
        sems, srcs_thru, lands, last_token[0] = _exchange_start(
            f"scatter_start_{i}_{tag}", parts, [_sds(a.shape, BF16) for a in parts], plan, len(parts), last_token[0])
        scatters.append((i, names, plan, sems, srcs_thru, lands))
        return last_token[0]
    mix0_names = ["pool_w", "pool_scale", "sgu_ln_g", "sgu_ln_b", "sgu_w", "sgu_b"]
    mix1_names = ["ssm_lam_re", "ssm_lam_im", "ssm_b_re", "ssm_b_im", "ssm_c_re", "ssm_c_im", "ssm_log_dt", "ssm_d"]

    def start_bundle(tag, arrays):
        flat, layout = _pack(arrays)
        rows = -(-flat.shape[0] // (8 * LANE)) * 8
        plan = gather_plan(1) if tag == "a" else bundle_plan
        sems, srcs_thru, lands, last_token[0] = _exchange_start(
            f"small_start_{tag}", [_pad_rows(flat, rows)], [_sds((N_DEV, rows, LANE))], plan, 1, last_token[0])
        bundles[tag] = (sems, srcs_thru, lands, layout)

    def on_grads(i, extra, dv, dv_top, loss_row):
        pieces[i] = dv
        if i == 5:
            pieces["top"] = dv_top
        if i == 4:
            mixer.update({n: extra[n] for n in mix1_names})
        if i == 1:
            mixer.update({n: extra[n] for n in mix0_names})
            start_bundle("a", [jnp.stack([pieces[j] for j in ("top", 5, 4, 3, 2, 1)])] + [mixer[n] for n in mix0_names + mix1_names])
        if i == 0:
            start_bundle("b", [dv, loss_row])
        if i % 3 != 1:
            return last_token[0]
        if i == 1:
            names, parts = ("ab_w_in", "ab_w_out"), [extra["ab_w_in"].reshape(N_DEV, -1, D), extra["ab_w_out"].reshape(N_DEV, nd, D)]
        else:
            names, parts = ("ssm_w_in", "ssm_w_glu"), [extra["ssm_w_in"].reshape(N_DEV, nd, D), shard_cols(extra["ssm_w_glu"])]
        sems, srcs_thru, lands, last_token[0] = _exchange_start(
            f"scatter_start_{i}", parts, [_sds(a.shape, BF16) for a in parts], scatter_plan, 2, last_token[0])
        scatters.append((i, names, scatter_plan, sems, srcs_thru, lands))
        return last_token[0]

    grad_x = _local_step(x[0], loss_target[0], vecs, weights_of, on_part, on_grads)

    out_g, out_d, out_m, out_v = {}, {}, {}, {}
    big_out = {}

    def adam_big(name, recv, n, slot=0, after=None):
        c_ = wts[n].shape[-1]
        big_out[n] = _adamw(name, recv.reshape(recv.shape[0], -1, c_), *[t[n].reshape(-1, c_) for t in (wts, mom, var)],
                            slot=slot, prev=big_out.get(n), after=after)
        return big_out[n][0]

    ffn_slot = {0: 0, 2: 1, 3: 2, 5: 3}

    def land_and_update(entries, after):
        for i, names, plan, sems, srcs_thru, lands in entries:
            recv = _exchange_wait(f"scatter_wait_{i}_{names[0]}", sems, srcs_thru, lands, plan, len(names), after)
            for n, r in zip(names, recv):
                after = adam_big(f"adamw_{n}_{i}", r, n, ffn_slot.get(i, 0), after)
        return after

    after = land_and_update([e for e in scatters if e[0] != 0], last_token[0])

    def landed(tag, g_parts):
        layout = bundles[tag][3]
        off, n, shape = layout[0]
        dmods = g_parts.reshape(N_DEV, -1)[:, off:off + n].reshape((N_DEV,) + shape)
        total = _sum_parts(g_parts)
        return dmods, _unpack(total.reshape(-1), layout), total

    def adam_small(n, g, after=None):
        cols = wts[n].shape[-1]
        res = _adamw(f"adamw_{n}", g.reshape(1, -1, cols), *[t[n].reshape(-1, cols) for t in (wts, mom, var)], after=after)
        for o, arr in zip((out_g, out_d, out_m, out_v), res):
            o[n] = arr.reshape(wts[n].shape)
            if n in TRANSPOSED:
                o[n] = jnp.swapaxes(o[n], -1, -2)
        return res[0]

    sems, srcs_thru, lands, _ = bundles["a"]
    sems, lands, _ = _exchange_relay("small_relay_a", sems, srcs_thru, lands, gather_plan(1), 1, relay_plan(1), 3, after)
    (parts_a,) = _exchange_wait("small_wait_a", sems, None, lands, relay_plan(1), 3, after)
    shells_a, sums_a, after = landed("a", parts_a)
    small = dict(zip(mix0_names + mix1_names, sums_a[1:]))
    def adam_tiny(name, grads, after):
        view = lambda n, a: a.reshape(-1, wts[n].shape[-1])
        items = [(view(n, g),) + tuple(view(n, t[n]) for t in (wts, mom, var)) for n, g in grads.items()]
        for (n, _), item, res in zip(grads.items(), items, _adamw_many(name, items, after)):
            for o, arr in zip((out_g, out_d, out_m, out_v), (item[0],) + res):
                o[n] = arr.reshape(wts[n].shape)
        return res[0]

    tiny = ["pool_scale", "sgu_ln_g", "sgu_ln_b", "sgu_b", "ssm_lam_re", "ssm_lam_im", "ssm_log_dt"]
    for n in [n for n in mix0_names + mix1_names if n not in tiny and n != "ssm_d"]:
        after = adam_small(n, small[n], after)
    after = adam_tiny("adamw_tiny_mixers", dict({n: small[n] for n in tiny},
                                                ssm_d=lax.dynamic_slice_in_dim(small["ssm_d"], me * nd, nd, axis=1)), after)
    sems, srcs_thru, lands, _ = bundles["b"]
    (parts_b,) = _exchange_wait("small_wait_b", sems, srcs_thru, lands, bundle_plan, 1, after)
    shell_b, (first_sum, loss_sum), after = landed("b", parts_b)
    loss = loss_sum[0, 0]

    def shell_grads(top, blocks, first):
        own_rows = jnp.concatenate([first[..., None, :, :], blocks[..., :0:-1, :, :]], axis=-3)
        next_rows = jnp.concatenate([own_rows[..., 1:, :, :], top[..., None, :, :]], axis=-3)
        dmod_ = jnp.stack([own_rows[..., V_SHIFT, :], own_rows[..., V_SCALE, :], next_rows[..., V_GATE, :]], axis=-2)
        return dmod_, own_rows[..., V_GPRE, :], next_rows[..., V_GPOST, :]

    dmod_sum, dg_pre_sum, dg_post_sum = shell_grads(sums_a[0][0], sums_a[0], first_sum)
    own = lambda a: lax.dynamic_slice_in_dim(a, me * nd, nd, axis=1)
    after = adam_tiny("adamw_tiny_shell", {"ada_b": dmod_sum, "norm_pre": own(dg_pre_sum), "norm_post": own(dg_post_sum)}, after)

    dmod_all = shell_grads(shells_a[:, 0], shells_a, shell_b)[0].reshape(N_DEV, 2, N_DEV, nw)
    dmod_mine = lax.dynamic_index_in_dim(dmod_all, me, axis=2, keepdims=False).transpose(1, 0, 2)
    g_ada_w = _ada_w_grad(c_all.T, dmod_mine)
    after = after[0:1, 0:1] + adam_big("adamw_ada_w", g_ada_w[None], "ada_w")[0:1, 0:1]

    land_and_update([e for e in scatters if e[0] == 0], after)
    for n, res in big_out.items():
        for o, arr in zip((out_g, out_d, out_m, out_v), res):
            o[n] = arr.reshape(wts[n].shape)
            if n in TRANSPOSED:
                o[n] = jnp.swapaxes(o[n], -1, -2)

    return (loss, grad_x[None], *[out_g[n] for n in WEIGHTS], *[out_d[n] for n in WEIGHTS],
            *[out_m[n] for n in WEIGHTS], *[out_v[n] for n in WEIGHTS])
```

```python
import functools
import math

import jax
import jax.numpy as jnp
from jax import lax
from jax.experimental import pallas as pl
from jax.experimental.pallas import tpu as pltpu

F32 = jnp.float32
BF16 = jnp.bfloat16
MESH = pl.DeviceIdType.MESH
HIGHEST = lax.Precision.HIGHEST

N_DEV = 8
D = 1024
D_FF = 2816
FSH = 2 * D_FF // N_DEV
EPS = 1e-6
POOL_WINDOWS = (2, 4, 8, 16)
HD = 128
NH = 4
SSM_G, SSM_P, SSM_N = 64, 64, 16
SSM_L = SSM_G * SSM_P
LR, B1, B2, ADAM_EPS, WD, STEP = 0.001, 0.9, 0.999, 1e-08, 0.01, 10
GELU_C = math.sqrt(2.0 / math.pi)
VMEM_LIMIT_BYTES = 48 * 1024 * 1024
LANE = 128


def _pc(body, name, grid, in_specs, out_specs, out_shape, scratch=()):
    return pl.pallas_call(
        body, name=name, grid=grid, in_specs=in_specs, out_specs=out_specs, out_shape=out_shape,
        scratch_shapes=list(scratch),
        compiler_params=pltpu.CompilerParams(dimension_semantics=("arbitrary",) * len(grid),
                                             vmem_limit_bytes=VMEM_LIMIT_BYTES))


def _sds(shape, dtype=F32):
    return jax.ShapeDtypeStruct(tuple(shape), dtype)


def _bf(v):
    return v if v.dtype == BF16 else v.astype(BF16)


def _row_spec(ts, width, col=0):
    return pl.BlockSpec((ts, width), lambda t, _c=col: (t, _c))


def _vec_spec(width, col=0):
    return pl.BlockSpec((1, width), lambda t, _c=col: (0, _c))


def _mm(name, a, b, contract, grid, a_spec, b_spec, o_spec, out_shape, acc_axis=None, after=None):
    dn = (contract, ((), ()))

    def body(a_ref, b_ref, *rest):
        o_ref = rest[-1]
        r = lax.dot_general(_bf(a_ref[...]), _bf(b_ref[...]), dn, preferred_element_type=F32)
        if acc_axis is None:
            o_ref[...] = r.astype(o_ref.dtype)
        else:
            k = pl.program_id(acc_axis)

            @pl.when(k == 0)
            def _():
                o_ref[...] = r

            @pl.when(k > 0)
            def _():
                o_ref[...] += r

    if after is None:
        return _pc(body, name, grid, [a_spec, b_spec], o_spec, out_shape)(a, b)
    return _pc(body, name, grid, [a_spec, b_spec, pl.BlockSpec(memory_space=pl.ANY)], o_spec, out_shape)(a, b, after)


def _mm_sum(name, a, b, ts, after=None):
    nj, s, k = a.shape
    n = b.shape[2]

    def body(a_ref, b_ref, *rest):
        acc = jnp.dot(a_ref[0], b_ref[0], preferred_element_type=F32)
        for j in range(1, nj):
            acc = acc + jnp.dot(a_ref[j], b_ref[j], preferred_element_type=F32)
        rest[-1][...] = acc

    specs = [pl.BlockSpec((nj, ts, k), lambda t: (0, t, 0)), pl.BlockSpec((nj, k, n), lambda t: (0, 0, 0))]
    args = (a, b)
    if after is not None:
        specs, args = specs + [pl.BlockSpec(memory_space=pl.ANY)], args + (after,)
    return _pc(body, name, (s // ts,), specs, pl.BlockSpec((ts, n), lambda t: (t, 0)), _sds((s, n)))(*args)


def _tile(s):
    return min(s, 1024)


def _div_tile(n, cap=1024):
    t = min(n, cap) // LANE * LANE
    while n % t:
        t -= LANE
    return t


def _mm_nn(name, a, b, out_dtype=F32):
    s, k = a.shape
    n = b.shape[1]
    ts, tn = _tile(s), _div_tile(n)
    return _mm(name, a, b, ((1,), (0,)), (n // tn, s // ts),
               pl.BlockSpec((ts, k), lambda j, t: (t, 0)), pl.BlockSpec((k, tn), lambda j, t: (0, j)),
               pl.BlockSpec((ts, tn), lambda j, t: (t, j)), _sds((s, n), out_dtype))


def _mm_nt(name, a, b, out_dtype=F32, after=None):
    s, n = a.shape
    k = b.shape[0]
    ts, tk = _tile(s), _div_tile(k)
    return _mm(name, a, b, ((1,), (1,)), (k // tk, s // ts),
               pl.BlockSpec((ts, n), lambda j, t: (t, 0)), pl.BlockSpec((tk, n), lambda j, t: (j, 0)),
               pl.BlockSpec((ts, tk), lambda j, t: (t, j)), _sds((s, k), out_dtype), after=after)


def _mm_tn(name, a, b, out_dtype=F32, tm=512, tn=512):
    s, m = a.shape
    n = b.shape[1]
    tm, tn = min(m, tm), min(n, tn)
    return _mm(name, a, b, ((0,), (0,)), (m // tm, n // tn),
               pl.BlockSpec((s, tm), lambda i, j: (0, i)), pl.BlockSpec((s, tn), lambda i, j: (0, j)),
               pl.BlockSpec((tm, tn), lambda i, j: (i, j)), _sds((m, n), out_dtype))


def _rstd(v):
    return lax.rsqrt(jnp.mean(v * v, axis=-1, keepdims=True) + EPS)


V_GPRE, V_SCALE, V_SHIFT, V_GPOST, V_GATE = range(5)


def _vrow(v, r):
    return v[r:r + 1]


def _vblock(i):
    return pl.BlockSpec((None, 8, D), lambda t: (i, 0, 0))


def _head(xv, v):
    return ((xv * _rstd(xv) * _vrow(v, V_GPRE)) * (1.0 + _vrow(v, V_SCALE)) + _vrow(v, V_SHIFT)).astype(BF16)


def _tail(xv, fv, v, rw):
    return xv + (rw * _vrow(v, V_GATE)) * (fv * _rstd(fv) * _vrow(v, V_GPOST))


def _prenorm_fwd(x, vecs, i, after):
    s = x.shape[0]
    ts = min(s, 512)

    def body(x_ref, v_ref, after_ref, h_ref):
        h_ref[...] = _head(x_ref[...], v_ref[...])

    return _pc(body, "prenorm_fwd", (s // ts,), [_row_spec(ts, D), _vblock(i), pl.BlockSpec(memory_space=pl.ANY)], _row_spec(ts, D),
               _sds((s, D), BF16))(x, vecs, after)


def _postnorm_fwd(x, f, vecs, i, rw):
    s = x.shape[0]
    ts = min(s, 512)

    def body(x_ref, f_ref, v_ref, o_ref):
        o_ref[...] = _tail(x_ref[...], f_ref[...], v_ref[...], rw)

    return _pc(body, "postnorm_fwd", (s // ts,), [_row_spec(ts, D)] * 2 + [_vblock(i)], _row_spec(ts, D), _sds((s, D)))(x, f, vecs)


def _post_pre_fwd(x, f, vecs, i, rw_prev, after):
    s = x.shape[0]
    ts = min(s, 512)

    def body(x_ref, f_ref, vp_ref, vc_ref, after_ref, xo_ref, h_ref):
        xv = _tail(x_ref[...], f_ref[...], vp_ref[...], rw_prev)
        xo_ref[...] = xv
        h_ref[...] = _head(xv, vc_ref[...])

    return _pc(body, "post_pre_fwd", (s // ts,),
               [_row_spec(ts, D)] * 2 + [_vblock(i - 1), _vblock(i), pl.BlockSpec(memory_space=pl.ANY)], [_row_spec(ts, D)] * 2,
               [_sds((s, D)), _sds((s, D), BF16)])(x, f, vecs, vecs, after)


def _zero_at_first(first, *refs):
    @pl.when(first)
    def _():
        for ref in refs:
            ref[...] = jnp.zeros_like(ref)


def _acc(ref, first, v):
    @pl.when(first)
    def _():
        ref[...] = v

    @pl.when(jnp.logical_not(first))
    def _():
        ref[...] += v


def _colsum(v):
    return jnp.sum(v, axis=0, keepdims=True)


def _tail_bwd(do, fv, v, rw, dv_ref):
    gv = _vrow(v, V_GPOST)
    r = _rstd(fv)
    fn = fv * r
    dv_ref[V_GATE:V_GATE + 1, :] += rw * _colsum(do * (fn * gv))
    dy = (rw * _vrow(v, V_GATE)) * do
    dv_ref[V_GPOST:V_GPOST + 1, :] += _colsum(dy * fn)
    dfn = dy * gv
    return (r * (dfn - fn * jnp.mean(dfn * fn, axis=-1, keepdims=True))).astype(BF16)


def _head_bwd(do, dhv, xv, v, dv_ref):
    gv = _vrow(v, V_GPRE)
    r = _rstd(xv)
    xn = xv * r
    dv_ref[V_SHIFT:V_SHIFT + 1, :] += _colsum(dhv)
    dv_ref[V_SCALE:V_SCALE + 1, :] += _colsum(dhv * (xn * gv))
    dhp = dhv * (1.0 + _vrow(v, V_SCALE))
    dv_ref[V_GPRE:V_GPRE + 1, :] += _colsum(dhp * xn)
    dxn = dhp * gv
    return do + r * (dxn - xn * jnp.mean(dxn * xn, axis=-1, keepdims=True))


DV_SPEC = pl.BlockSpec((8, D), lambda t: (0, 0))


def _postnorm_bwd(dout, f, vecs, i, rw):
    s = dout.shape[0]
    ts = min(s, 512)

    def body(do_ref, f_ref, v_ref, df_ref, dv_ref):
        _zero_at_first(pl.program_id(0) == 0, dv_ref)
        df_ref[...] = _tail_bwd(do_ref[...], f_ref[...], v_ref[...], rw, dv_ref)

    return _pc(body, "postnorm_bwd", (s // ts,), [_row_spec(ts, D)] * 2 + [_vblock(i)], [_row_spec(ts, D), DV_SPEC],
               [_sds((s, D), BF16), _sds((8, D))])(dout, f, vecs)


def _prenorm_bwd(dout, dh, x, vecs, i):
    s = dout.shape[0]
    ts = min(s, 512)

    def body(do_ref, dh_ref, x_ref, v_ref, dx_ref, dv_ref):
        _zero_at_first(pl.program_id(0) == 0, dv_ref)
        dx_ref[...] = _head_bwd(do_ref[...], dh_ref[...], x_ref[...], v_ref[...], dv_ref)

    return _pc(body, "prenorm_bwd", (s // ts,), [_row_spec(ts, D)] * 3 + [_vblock(i)], [_row_spec(ts, D), DV_SPEC],
               [_sds((s, D)), _sds((8, D))])(dout, dh, x, vecs)


def _pre_post_bwd(dout, dh, x, f_prev, vecs, i, rw_prev):
    s = dout.shape[0]
    ts = min(s, 256)

    def body(do_ref, dh_ref, x_ref, f_ref, vc_ref, vp_ref, dx_ref, df_ref, dv_ref):
        _zero_at_first(pl.program_id(0) == 0, dv_ref)
        dx = _head_bwd(do_ref[...], dh_ref[...], x_ref[...], vc_ref[...], dv_ref)
        dx_ref[...] = dx
        df_ref[...] = _tail_bwd(dx, f_ref[...], vp_ref[...], rw_prev, dv_ref)

    rows = _row_spec(ts, D)
    return _pc(body, "pre_post_bwd", (s // ts,), [rows] * 4 + [_vblock(i), _vblock(i - 1)], [rows, rows, DV_SPEC],
               [_sds((s, D)), _sds((s, D), BF16), _sds((8, D))])(dout, dh, x, f_prev, vecs, vecs)


def _loss_fwd_bwd(y, tgt):
    s = y.shape[0]
    ts = min(s, 512)
    nt = s // ts

    def body(y_ref, t_ref, loss_ref, dy_ref, acc_ref):
        t = pl.program_id(0)
        e = y_ref[...] - t_ref[...]
        dy_ref[...] = e * (1.0 / D)
        _acc(acc_ref, t == 0, _colsum(e * e))

        @pl.when(t == nt - 1)
        def _():
            loss_ref[...] = jnp.full((1, LANE), 0.5 / D, F32) * jnp.sum(acc_ref[...])

    return _pc(body, "loss", (nt,), [_row_spec(ts, D)] * 2,
               [pl.BlockSpec((1, LANE), lambda t: (0, 0)), _row_spec(ts, D)],
               [_sds((1, LANE)), _sds((s, D))], scratch=[pltpu.VMEM((1, D), F32)])(y, tgt)


def _sigmoid(v):
    return 1.0 / (1.0 + jnp.exp(-v))


def _ffn_in_swiglu(h, win):
    s = h.shape[0]
    ts = _tile(s)
    nt = (((1,), (1,)), ((), ()))

    def body(h_ref, wa_ref, wb_ref, fac_ref, act_ref):
        hv = h_ref[...]
        a = lax.dot_general(hv, wa_ref[...], nt, preferred_element_type=F32)
        b = lax.dot_general(hv, wb_ref[...], nt, preferred_element_type=F32)
        sg = _sigmoid(a)
        silu = a * sg
        fac_ref[0] = (b * (sg * (1.0 + a * (1.0 - sg)))).astype(BF16)
        fac_ref[1] = silu.astype(BF16)
        act_ref[...] = (silu * b).astype(BF16)

    return _pc(body, "ffn_in", (4, s // ts),
               [pl.BlockSpec((ts, D), lambda k, t: (t, 0)), pl.BlockSpec((None, FSH, D), lambda k, t: (k, 0, 0)),
                pl.BlockSpec((None, FSH, D), lambda k, t: (k + 4, 0, 0))],
               [pl.BlockSpec((2, None, ts, FSH), lambda k, t: (0, k, t, 0)), pl.BlockSpec((None, ts, FSH), lambda k, t: (k, t, 0))],
               [_sds((2, 4, s, FSH), BF16), _sds((4, s, FSH), BF16)])(h, win, win)


def _ffn_out_dx_swiglu(df, wout, fac, after):
    s = df.shape[0]
    ts = _tile(s)
    nt = (((1,), (1,)), ((), ()))

    def body(df_ref, w_ref, fac_ref, after_ref, o_ref):
        d = lax.dot_general(df_ref[...], w_ref[...], nt, preferred_element_type=F32)
        o_ref[0] = (d * fac_ref[0]).astype(BF16)
        o_ref[1] = (d * fac_ref[1]).astype(BF16)

    spec = pl.BlockSpec((2, None, ts, FSH), lambda k, t: (0, k, t, 0))
    out = _pc(body, "ffn_out_dx", (4, s // ts),
              [pl.BlockSpec((ts, D), lambda k, t: (t, 0)), pl.BlockSpec((None, FSH, D), lambda k, t: (k, 0, 0)), spec,
               pl.BlockSpec(memory_space=pl.ANY)],
              spec, _sds((2, 4, s, FSH), BF16))(df, wout, fac, after)
    return out.reshape(N_DEV, s, FSH)


def _ffn_fwd(h, win, wout_of):
    s = h.shape[0]
    fac, act = _ffn_in_swiglu(h, win)
    f = _mm_sum("ffn_out", act, wout_of(act).reshape(4, FSH, D), min(s, 512))
    return f, (h, fac, act)


def _ffn_bwd(df, saved, win, wout, send, after):
    h, fac, act = saved
    s = h.shape[0]
    ts = s
    wout = wout.reshape(4, FSH, D)
    dwout = _mm("ffn_out_dw", act, df, ((0,), (0,)), (4, 2),
                pl.BlockSpec((None, s, FSH), lambda k, j: (k, 0, 0)), pl.BlockSpec((s, D // 2), lambda k, j: (0, j)),
                pl.BlockSpec((None, FSH, D // 2), lambda k, j: (k, 0, j)), _sds((4, FSH, D), BF16), after=after)
    dz = _ffn_out_dx_swiglu(df, wout, fac, send("w_out", dwout.reshape(N_DEV, D_FF // N_DEV, D)))
    dwin = _mm("ffn_in_dw", dz, h, ((0,), (0,)), (N_DEV, 2),
               pl.BlockSpec((None, s, FSH), lambda j, i: (j, 0, 0)), pl.BlockSpec((s, D // 2), lambda j, i: (0, i)),
               pl.BlockSpec((None, FSH, D // 2), lambda j, i: (j, 0, i)), _sds((N_DEV, FSH, D), BF16))
    return _mm_sum("ffn_in_dx", dz, win, min(s, 512), after=send("w_in", dwin))


def _shift_rows(v, k, row, s, back):
    if back:
        return jnp.where(row < s - k, pltpu.roll(v, s - k, 0), 0.0)
    return jnp.where(row >= k, pltpu.roll(v, k, 0), 0.0)


def _window_sum(v, w, row, s, back):
    k = 1
    while k < w:
        v = v + _shift_rows(v, k, row, s, back)
        k *= 2
    return v


def _pool_fwd(z, pool_w, pool_scale):
    s = z.shape[0]

    def body(z_ref, w_ref, sc_ref, y_ref, d_ref):
        row = lax.broadcasted_iota(jnp.int32, (s, HD), 0)
        for g, w in enumerate(POOL_WINDOWS):
            sl = slice(g * HD, (g + 1) * HD)
            a = z_ref[:, sl]
            cnt = jnp.minimum(row + 1, w).astype(F32)
            d = (_window_sum(a, w, row, s, False) / cnt - a).astype(BF16)
            d_ref[:, sl] = d
            y = jnp.dot(d, _bf(w_ref[g]), preferred_element_type=F32)
            y_ref[:, sl] = (y * sc_ref[:, sl]).astype(BF16)

    return _pc(body, "pool_fwd", (1,),
               [pl.BlockSpec((s, NH * HD), lambda i: (0, 0)), pl.BlockSpec((NH, HD, HD), lambda i: (0, 0, 0)),
                pl.BlockSpec((1, NH * HD), lambda i: (0, 0))],
               [pl.BlockSpec((s, NH * HD), lambda i: (0, 0))] * 2,
               [_sds((s, NH * HD), BF16)] * 2)(z, pool_w, pool_scale)


def _pool_bwd(dy, d, pool_w, pool_scale):
    s = dy.shape[0]

    def body(dy_ref, d_ref, w_ref, sc_ref, dz_ref, dw_ref, dsc_ref):
        row = lax.broadcasted_iota(jnp.int32, (s, HD), 0)
        for g, w in enumerate(POOL_WINDOWS):
            sl = slice(g * HD, (g + 1) * HD)
            dyg, dg, wg = dy_ref[:, sl], d_ref[:, sl], _bf(w_ref[g])
            yraw = jnp.dot(dg, wg, preferred_element_type=F32)
            dsc_ref[:, sl] = _colsum(dyg * yraw)
            dyr = _bf(dyg * sc_ref[:, sl])
            dw_ref[g] = lax.dot_general(dg, dyr, (((0,), (0,)), ((), ())), preferred_element_type=F32)
            dd = lax.dot_general(dyr, wg, (((1,), (1,)), ((), ())), preferred_element_type=F32)
            cnt = jnp.minimum(row + 1, w).astype(F32)
            dz_ref[:, sl] = (_window_sum(dd / cnt, w, row, s, True) - dd).astype(BF16)

    return _pc(body, "pool_bwd", (1,),
               [pl.BlockSpec((s, NH * HD), lambda i: (0, 0)), pl.BlockSpec((s, NH * HD), lambda i: (0, 0)),
                pl.BlockSpec((NH, HD, HD), lambda i: (0, 0, 0)), pl.BlockSpec((1, NH * HD), lambda i: (0, 0))],
               [pl.BlockSpec((s, NH * HD), lambda i: (0, 0)), pl.BlockSpec((NH, HD, HD), lambda i: (0, 0, 0)),
                pl.BlockSpec((1, NH * HD), lambda i: (0, 0))],
               [_sds((s, NH * HD), BF16), _sds((NH, HD, HD)), _sds((1, NH * HD))])(dy, d, pool_w, pool_scale)


def _gelu(v):
    return 0.5 * v * (1.0 + jnp.tanh(GELU_C * (v + 0.044715 * (v * v * v))))


def _gelu_and_grad(v):
    t = jnp.tanh(GELU_C * (v + 0.044715 * (v * v * v)))
    return 0.5 * v * (1.0 + t), 0.5 * (1.0 + t) + 0.5 * v * (1.0 - t * t) * (GELU_C * (1.0 + 3.0 * 0.044715 * (v * v)))


def _gelu_grad(v):
    return _gelu_and_grad(v)[1]


def _causal_mask():
    return lax.broadcasted_iota(jnp.int32, (HD, HD), 0) >= lax.broadcasted_iota(jnp.int32, (HD, HD), 1)


def _sgu_specs():
    w = NH * HD
    return [pl.BlockSpec((HD, w), lambda c: (c, 1)), pl.BlockSpec((HD, w), lambda c: (c, 2)),
            pl.BlockSpec((1, w), lambda c: (0, 0)), pl.BlockSpec((1, w), lambda c: (0, 0)),
            pl.BlockSpec((NH, HD, HD), lambda c: (0, 0, 0)), pl.BlockSpec((HD, LANE), lambda c: (0, 0))]


def _sgu_head(v, lng_ref, lnb_ref, w_ref, h):
    sl = slice(h * HD, (h + 1) * HD)
    vh = v[:, sl]
    xc = vh - jnp.mean(vh, axis=-1, keepdims=True)
    rs = lax.rsqrt(jnp.mean(xc * xc, axis=-1, keepdims=True) + EPS)
    vhat = xc * rs
    vn = _bf(vhat * lng_ref[:, sl] + lnb_ref[:, sl])
    wc = _bf(jnp.where(_causal_mask(), w_ref[h], 0.0))
    return sl, rs, vhat, vn, wc


def _sgu_fwd(z, ln_g, ln_b, sgu_w, sgu_bt):
    s = z.shape[0]

    def body(zu_ref, zv_ref, lng_ref, lnb_ref, w_ref, bt_ref, y_ref):
        u, v = _gelu(zu_ref[...]), _gelu(zv_ref[...])
        for h in range(NH):
            sl, _, _, vn, wc = _sgu_head(v, lng_ref, lnb_ref, w_ref, h)
            sp = jnp.dot(wc, vn, preferred_element_type=F32) + bt_ref[:, h:h + 1]
            y_ref[:, sl] = (u[:, sl] * sp).astype(BF16)

    return _pc(body, "sgu_fwd", (s // HD,), _sgu_specs(), pl.BlockSpec((HD, NH * HD), lambda c: (c, 0)),
               _sds((s, NH * HD), BF16))(z, z, ln_g, ln_b, sgu_w, sgu_bt)


def _sgu_bwd(z, dy, ln_g, ln_b, sgu_w, sgu_bt, head_sum):
    s = z.shape[0]
    w = NH * HD
    nc = s // HD

    def body(zu_ref, zv_ref, lng_ref, lnb_ref, w_ref, bt_ref, dy_ref, hs_ref,
             dzu_ref, dzv_ref, dlng_ref, dlnb_ref, dw_ref, dbt_ref, dsacc_ref):
        c = pl.program_id(0)
        _zero_at_first(c == 0, dsacc_ref, dw_ref, dlng_ref, dlnb_ref)
        zu, zv = zu_ref[...], zv_ref[...]
        (u, gu), (v, gv) = _gelu_and_grad(zu), _gelu_and_grad(zv)
        dyv = dy_ref[...]
        ds = dyv * u
        dsacc_ref[...] += ds
        for h in range(NH):
            sl, rs, vhat, vn, wc = _sgu_head(v, lng_ref, lnb_ref, w_ref, h)
            sp = jnp.dot(wc, vn, preferred_element_type=F32) + bt_ref[:, h:h + 1]
            dzu_ref[:, sl] = (dyv[:, sl] * sp * gu[:, sl]).astype(BF16)
            dsh = _bf(ds[:, sl])
            dwh = lax.dot_general(dsh, vn, (((1,), (1,)), ((), ())), preferred_element_type=F32)
            dw_ref[h] += jnp.where(_causal_mask(), dwh, 0.0)
            dvn = lax.dot_general(wc, dsh, (((0,), (0,)), ((), ())), preferred_element_type=F32)
            dlng_ref[:, sl] += _colsum(dvn * vhat)
            dlnb_ref[:, sl] += _colsum(dvn)
            dvh = dvn * lng_ref[:, sl]
            dv = rs * (dvh - jnp.mean(dvh, axis=-1, keepdims=True) - vhat * jnp.mean(dvh * vhat, axis=-1, keepdims=True))
            dzv_ref[:, sl] = (dv * gv[:, sl]).astype(BF16)

        @pl.when(c == nc - 1)
        def _():
            dbt_ref[...] = jnp.dot(dsacc_ref[...], hs_ref[...], preferred_element_type=F32, precision=HIGHEST)

    outs = _pc(body, "sgu_bwd", (nc,),
               _sgu_specs() + [pl.BlockSpec((HD, w), lambda c: (c, 1)), pl.BlockSpec((w, LANE), lambda c: (0, 0))],
               [pl.BlockSpec((HD, w), lambda c: (c, 0))] * 2 + [pl.BlockSpec((1, w), lambda c: (0, 0))] * 2
               + [pl.BlockSpec((NH, HD, HD), lambda c: (0, 0, 0)), pl.BlockSpec((HD, LANE), lambda c: (0, 0))],
               [_sds((s, w), BF16)] * 2 + [_sds((1, w))] * 2 + [_sds((NH, HD, HD)), _sds((HD, LANE))],
               scratch=[pltpu.VMEM((HD, w), F32)])(z, z, ln_g, ln_b, sgu_w, sgu_bt, dy, head_sum)
    return outs


def _cmul(ar, ai, br, bi):
    return ar * br - ai * bi, ar * bi + ai * br


def _ssm_prep(lam_re, lam_im, lam_re_rep, lam_im_rep, log_dt, b_re, b_im):
    def disc(lr, li, dt):
        mag = jnp.exp(lr * dt)
        return mag * jnp.cos(li * dt), mag * jnp.sin(li * dt)

    def body(lr_ref, li_ref, lrr_ref, lir_ref, ldt_ref, br_ref, bi_ref, or_ref, oi_ref, bbr_ref, bbi_ref):
        dt = jnp.exp(ldt_ref[...])
        or_ref[...], oi_ref[...] = disc(lr_ref[...], li_ref[...], dt)
        lr, li = lrr_ref[...], lir_ref[...]
        er, ei = disc(lr, li, dt)
        den = lr * lr + li * li
        kr = ((er - 1.0) * lr + ei * li) / den
        ki = (ei * lr - (er - 1.0) * li) / den
        bbr_ref[...], bbi_ref[...] = _cmul(kr, ki, br_ref[...], bi_ref[...])

    small = pl.BlockSpec((SSM_G, SSM_P), lambda i: (0, 0))
    wide = pl.BlockSpec((SSM_G, SSM_P * SSM_N), lambda i: (0, 0))
    col = pl.BlockSpec((SSM_G, 1), lambda i: (0, 0))
    return _pc(body, "ssm_prep", (1,), [small, small, wide, wide, col, wide, wide], [small, small, wide, wide],
               [_sds((SSM_G, SSM_P))] * 2 + [_sds((SSM_G, SSM_P * SSM_N))] * 2)(
        lam_re, lam_im, lam_re_rep, lam_im_rep, log_dt, b_re, b_im)


def _ssm_param_bwd(g_lam_re, g_lam_im, g_bb_re, g_bb_im, lam_re, lam_im, lam_re_rep, lam_im_rep, log_dt, b_re, b_im, seg):
    def body(glr_ref, gli_ref, gbr_ref, gbi_ref, lr_ref, li_ref, lrr_ref, lir_ref, ldt_ref, br_ref, bi_ref, seg_ref,
             dlr_ref, dli_ref, ddt_ref, dbr_ref, dbi_ref):
        dt = jnp.exp(ldt_ref[...])
        lr, li = lrr_ref[...], lir_ref[...]
        mag = jnp.exp(lr * dt)
        er, ei = mag * jnp.cos(li * dt), mag * jnp.sin(li * dt)
        den = lr * lr + li * li
        kr = ((er - 1.0) * lr + ei * li) / den
        ki = (ei * lr - (er - 1.0) * li) / den
        gbr, gbi = gbr_ref[...], gbi_ref[...]
        dbr_ref[...], dbi_ref[...] = _cmul(kr, -ki, gbr, gbi)
        tr, ti = _cmul(br_ref[...], -bi_ref[...], gbr, gbi)
        gkr = jnp.dot(tr, seg_ref[...], preferred_element_type=F32, precision=HIGHEST)
        gki = jnp.dot(ti, seg_ref[...], preferred_element_type=F32, precision=HIGHEST)
        lr, li = lr_ref[...], li_ref[...]
        mag = jnp.exp(lr * dt)
        er, ei = mag * jnp.cos(li * dt), mag * jnp.sin(li * dt)
        den = lr * lr + li * li
        ir, ii = lr / den, -li / den
        kr, ki = _cmul(er - 1.0, ei, ir, ii)
        ar, ai = _cmul(ir, -ii, gkr, gki)
        glr, gli = glr_ref[...] + ar, gli_ref[...] + ai
        qr, qi = _cmul(kr, ki, ir, ii)
        g1r, g1i = _cmul(-qr, qi, gkr, gki)
        g2r, g2i = _cmul(dt * er, -dt * ei, glr, gli)
        dlr_ref[...] = g1r + g2r
        dli_ref[...] = g1i + g2i
        wr, wi = _cmul(lr, li, er, ei)
        g_dt = jnp.sum(wr * glr + wi * gli, axis=-1, keepdims=True)
        ddt_ref[...] = jnp.broadcast_to(dt * g_dt, (SSM_G, LANE))

    small = pl.BlockSpec((SSM_G, SSM_P), lambda i: (0, 0))
    wide = pl.BlockSpec((SSM_G, SSM_P * SSM_N), lambda i: (0, 0))
    col = pl.BlockSpec((SSM_G, 1), lambda i: (0, 0))
    segs = pl.BlockSpec((SSM_P * SSM_N, SSM_P), lambda i: (0, 0))
    return _pc(body, "ssm_param_bwd", (1,), [small, small, wide, wide, small, small, wide, wide, col, wide, wide, segs],
               [small, small, pl.BlockSpec((SSM_G, LANE), lambda i: (0, 0)), wide, wide],
               [_sds((SSM_G, SSM_P))] * 2 + [_sds((SSM_G, LANE))] + [_sds((SSM_G, SSM_P * SSM_N))] * 2)(
        g_lam_re, g_lam_im, g_bb_re, g_bb_im, lam_re, lam_im, lam_re_rep, lam_im_rep, log_dt, b_re, b_im, seg)


SCAN_LANES = 512
SCAN_ROWS = 8


SCAN_GROUPS = SCAN_LANES // SSM_P
SCAN_COLS = SCAN_GROUPS * SSM_N
SCAN_CHUNK = 256


def _ssm_scan(name, v, w_in, lam_re, lam_im, w_out, reverse, states=None, u=None):
    s = v.shape[0]
    ln, rows, ch = SCAN_LANES, SCAN_ROWS, min(SCAN_CHUNK, s)
    nch, ntile = s // ch, ch // rows
    nt_dims = (((1,), (1,)), ((), ()))
    with_sum = states is not None
    tn_dims = (((0,), (0,)), ((), ()))

    def body(*refs):
        v_ref, win_ref, lr_ref, li_ref, wout_ref = refs[:5]
        n_in = 8 if with_sum else 5
        or_ref, oi_ref, y_ref = refs[n_in:n_in + 3]
        br_s, bi_s = refs[n_in + (9 if with_sum else 3):][:2]
        if with_sum:
            mb_s, mc_s = refs[-2:]
            mb_s[...] = jnp.zeros_like(mb_s)
            mc_s[...] = jnp.zeros_like(mc_s)
        l1 = (lr_ref[...], li_ref[...])
        pw = [l1]
        for _ in range(rows - 1):
            pw.append(_cmul(*pw[-1], *l1))
        row = lax.broadcasted_iota(jnp.int32, (rows, ln), 0)
        expo = (rows - row) if reverse else (row + 1)
        pr = jnp.zeros((rows, ln), F32)
        pi = jnp.zeros((rows, ln), F32)
        for e in range(1, rows + 1):
            pr = jnp.where(expo == e, pw[e - 1][0], pr)
            pi = jnp.where(expo == e, pw[e - 1][1], pi)
        lk = {}
        for k in (1, 2, 4):
            keep = (row < rows - k) if reverse else (row >= k)
            lk[k] = (jnp.where(keep, pw[k - 1][0], 0.0), jnp.where(keep, pw[k - 1][1], 0.0))

        def chunk(c, carry):
            q0 = pl.multiple_of(((nch - 1 - c) if reverse else c) * ch, ch)
            b = jnp.dot(_bf(v_ref[pl.ds(q0, ch), :]), win_ref[...], preferred_element_type=F32)
            br_s[...] = b[:, :ln]
            bi_s[...] = b[:, ln:]

            def step(i, carry):
                cr, ci = carry[:2]
                r0 = pl.multiple_of(((ntile - 1 - i) if reverse else i) * rows, rows)
                xr, xi = br_s[pl.ds(r0, rows), :], bi_s[pl.ds(r0, rows), :]
                for k in (1, 2, 4):
                    shift = rows - k if reverse else k
                    ar, ai = _cmul(lk[k][0], lk[k][1], pltpu.roll(xr, shift, 0), pltpu.roll(xi, shift, 0))
                    xr, xi = xr + ar, xi + ai
                ar, ai = _cmul(pr, pi, cr, ci)
                xr, xi = xr + ar, xi + ai
                g0 = pl.multiple_of(q0 + r0, rows)
                or_ref[pl.ds(g0, rows), :] = xr
                oi_ref[pl.ds(g0, rows), :] = xi
                if not with_sum:
                    return (xr[rows - 1:rows], xi[rows - 1:rows]) if not reverse else (xr[0:1], xi[0:1])
                nr = jnp.where(row == rows - 1, cr, pltpu.roll(xr, rows - 1, 0))
                ni = jnp.where(row == rows - 1, ci, pltpu.roll(xi, rows - 1, 0))
                sr, si = refs[5][pl.ds(g0, rows), :], refs[6][pl.ds(g0, rows), :]
                return xr[0:1], xi[0:1], carry[2] + (sr * nr + si * ni), carry[3] + (sr * ni - si * nr)

            carry = lax.fori_loop(0, ntile, step, carry)
            if with_sum:
                rows_c = pl.ds(q0, ch)
                uc, vc = _bf(refs[7][rows_c, :]), _bf(v_ref[rows_c, :])
                for scr, left, (right_re, right_im) in ((mb_s, uc, (or_ref, oi_ref)), (mc_s, vc, (refs[5], refs[6]))):
                    scr[:, :ln] += lax.dot_general(left, _bf(right_re[rows_c, :]), tn_dims, preferred_element_type=F32)
                    scr[:, ln:] += lax.dot_general(left, _bf(right_im[rows_c, :]), tn_dims, preferred_element_type=F32)
            w = wout_ref[...]
            y_ref[pl.ds(q0, ch), :] = (
                lax.dot_general(_bf(or_ref[pl.ds(q0, ch), :]), w[:, :ln], nt_dims, preferred_element_type=F32)
                + lax.dot_general(_bf(oi_ref[pl.ds(q0, ch), :]), w[:, ln:], nt_dims, preferred_element_type=F32))
            return carry

        zero = jnp.zeros((1, ln), F32)
        init = (zero, zero) + ((jnp.zeros((rows, ln), F32),) * 2 if with_sum else ())
        carry = lax.fori_loop(0, nch, chunk, init)
        if with_sum:
            refs[n_in + 3][...] = _colsum(carry[2])
            refs[n_in + 4][...] = _colsum(carry[3])
            row_g = lax.broadcasted_iota(jnp.int32, (SCAN_COLS, LANE), 0) // SSM_N
            lane_g = lax.broadcasted_iota(jnp.int32, (SCAN_COLS, LANE), 1) // SSM_P
            for scr, o_re, o_im in ((mb_s, refs[n_in + 5], refs[n_in + 6]), (mc_s, refs[n_in + 7], refs[n_in + 8])):
                for part, o_ref in enumerate((o_re, o_im)):
                    fold = jnp.zeros((SCAN_COLS, LANE), F32)
                    for cb in range(ln // LANE):
                        fold = fold + jnp.where(2 * cb + lane_g == row_g, scr[:, part * ln + cb * LANE:part * ln + (cb + 1) * LANE], 0.0)
                    o_ref[...] = jnp.where(row_g % 2 == 0, fold, pltpu.roll(fold, SSM_P, 1))

    vec = pl.BlockSpec((1, ln), lambda j: (0, j))
    blk = pl.BlockSpec((s, ln), lambda j: (0, j))
    cols = pl.BlockSpec((s, SCAN_COLS), lambda j: (0, j))
    wspec = pl.BlockSpec((None, SCAN_COLS, 2 * ln), lambda j: (j, 0, 0))
    ins, args = [cols, wspec, vec, vec, wspec], [v, w_in, lam_re, lam_im, w_out]
    outs, shapes = [blk, blk, cols], [_sds((s, SSM_L))] * 2 + [_sds((s, SSM_G * SSM_N))]
    scratch = [pltpu.VMEM((ch, ln), F32)] * 2
    if with_sum:
        own = pl.BlockSpec((None, SCAN_COLS, LANE), lambda j: (j, 0, 0))
        ins, args = ins + [blk, blk, cols], args + list(states) + [u]
        outs = outs + [vec, vec] + [own] * 4
        shapes = shapes + [_sds((1, SSM_L))] * 2 + [_sds((SSM_L // ln, SCAN_COLS, LANE))] * 4
        scratch = scratch + [pltpu.VMEM((SCAN_COLS, 2 * ln), F32)] * 2
    return _pc(body, name, (SSM_L // ln,), ins, outs, shapes, scratch=scratch)(*args)


def _ssm_act_fwd(y, u, d_skip):
    s = y.shape[0]
    ts = min(s, 512)

    def body(y_ref, u_ref, d_ref, o_ref):
        o_ref[...] = _gelu(y_ref[...] + d_ref[...] * u_ref[...]).astype(BF16)

    return _pc(body, "ssm_act_fwd", (s // ts,), [_row_spec(ts, D)] * 2 + [_vec_spec(D)], _row_spec(ts, D),
               _sds((s, D), BF16))(y, u, d_skip)


def _ssm_act_bwd(dg, y, u, d_skip):
    s = y.shape[0]
    ts = min(s, 512)

    def body(dg_ref, y_ref, u_ref, d_ref, dy_ref, dd_ref):
        uv = u_ref[...]
        dy = dg_ref[...] * _gelu_grad(y_ref[...] + d_ref[...] * uv)
        dy_ref[...] = dy.astype(BF16)
        _acc(dd_ref, pl.program_id(0) == 0, _colsum(dy * uv))

    return _pc(body, "ssm_act_bwd", (s // ts,), [_row_spec(ts, D)] * 3 + [_vec_spec(D)], [_row_spec(ts, D), _vec_spec(D)],
               [_sds((s, D), BF16), _sds((1, D))])(dg, y, u, d_skip)


def _axpy(a, b, d_skip):
    s = a.shape[0]
    ts = min(s, 512)

    def body(a_ref, b_ref, d_ref, o_ref):
        o_ref[...] = (a_ref[...] + d_ref[...] * b_ref[...].astype(F32)).astype(BF16)

    return _pc(body, "ssm_du", (s // ts,), [_row_spec(ts, D)] * 2 + [_vec_spec(D)], _row_spec(ts, D),
               _sds((s, D), BF16))(a, b, d_skip)


def _glu_fwd(zz):
    s = zz.shape[0]
    ts = min(s, 512)

    def body(a_ref, b_ref, o_ref):
        o_ref[...] = a_ref[...] * _sigmoid(b_ref[...])

    return _pc(body, "glu_fwd", (s // ts,), [_row_spec(ts, D, 0), _row_spec(ts, D, 1)], _row_spec(ts, D), _sds((s, D)))(zz, zz)


def _glu_bwd(zz, df):
    s = zz.shape[0]
    ts = min(s, 512)

    def body(a_ref, b_ref, df_ref, o_ref):
        sg = _sigmoid(b_ref[...])
        dfv = df_ref[...].astype(F32)
        o_ref[:, :D] = (dfv * sg).astype(BF16)
        o_ref[:, D:] = (dfv * a_ref[...] * sg * (1.0 - sg)).astype(BF16)

    return _pc(body, "glu_bwd", (s // ts,), [_row_spec(ts, D, 0), _row_spec(ts, D, 1), _row_spec(ts, D)],
               _row_spec(ts, 2 * D), _sds((s, 2 * D), BF16))(zz, zz, df)


def _ssm_block_diag(m_re, m_im):
    rows, half = SCAN_COLS, SCAN_LANES
    expand = jnp.tile(jnp.eye(SSM_P, dtype=BF16), (1, SCAN_GROUPS))

    def body(mr_ref, mi_ref, e_ref, o_ref):
        keep = (lax.broadcasted_iota(jnp.int32, (rows, half), 0) // SSM_N
                == lax.broadcasted_iota(jnp.int32, (rows, half), 1) // SSM_P)
        for part, m_ref in enumerate((mr_ref, mi_ref)):
            t = jnp.dot(_bf(m_ref[...]), e_ref[...], preferred_element_type=F32)
            o_ref[:, part * half:(part + 1) * half] = jnp.where(keep, t, 0.0).astype(BF16)

    blk = pl.BlockSpec((rows, SSM_P), lambda q: (q, 0))
    nb = SSM_G // SCAN_GROUPS
    return _pc(body, "ssm_block_diag", (nb,), [blk, blk, pl.BlockSpec((SSM_P, half), lambda q: (0, 0))],
               pl.BlockSpec((None, rows, 2 * half), lambda q: (q, 0, 0)), _sds((nb, rows, 2 * half), BF16))(m_re, m_im, expand)


def _mod_part(c_all, ada_w):
    n = ada_w.shape[-1]

    def body(c_ref, w_ref, o_ref):
        cv = c_ref[...]
        cond = _bf(cv * _sigmoid(cv))
        o_ref[...] = jnp.dot(cond, _bf(w_ref[...]), preferred_element_type=F32)

    return _pc(body, "mod_part", (2,), [pl.BlockSpec((N_DEV, D), lambda l: (0, 0)), pl.BlockSpec((None, D, n), lambda l: (l, 0, 0))],
               pl.BlockSpec((None, N_DEV, n), lambda l: (l, 0, 0)), _sds((2, N_DEV, n)))(c_all, ada_w)


def _ada_w_grad(c_all_t, dmod):
    nl, _, n = dmod.shape
    tr = 128

    def body(c_ref, d_ref, o_ref):
        cv = c_ref[...]
        cond = _bf(cv * _sigmoid(cv)).astype(F32)
        dm = _bf(d_ref[...]).astype(F32)
        acc = cond[:, 0:1] * dm[0:1, :]
        for b in range(1, N_DEV):
            acc = acc + cond[:, b:b + 1] * dm[b:b + 1, :]
        o_ref[...] = acc

    return _pc(body, "ada_w_grad", (nl, D // tr),
               [pl.BlockSpec((tr, N_DEV), lambda l, t: (t, 0)), pl.BlockSpec((None, N_DEV, n), lambda l, t: (l, 0, 0))],
               pl.BlockSpec((None, tr, n), lambda l, t: (l, t, 0)), _sds((nl, D, n)))(c_all_t, dmod)


def _adamw(name, parts, w, m, v, slot=0, prev=None, after=None):
    p, r, c = parts.shape
    tr = r
    while tr * c * 4 > (1 << 20) and tr % 16 == 0:
        tr //= 2
    nt = r // tr

    def body(p_ref, w_ref, m_ref, v_ref, *rest):
        g_ref, d_ref, nm_ref, nv_ref = rest[-4:]
        g = p_ref[0].astype(F32)
        for i in range(1, p):
            g = g + p_ref[i].astype(F32)
        g_ref[...] = g
        d_ref[...], nm_ref[...], nv_ref[...] = _adam_update(g, w_ref[...], m_ref[...], v_ref[...])

    blk = pl.BlockSpec((tr, c), lambda t: (slot * nt + t, 0))
    in_specs = [pl.BlockSpec((p, tr, c), lambda t: (0, t, 0)), blk, blk, blk]
    unread = list(prev or []) + ([after] if after is not None else [])
    return pl.pallas_call(
        body, name=name, grid=(nt,), in_specs=in_specs + [pl.BlockSpec(memory_space=pl.ANY)] * len(unread), out_specs=[blk] * 4,
        out_shape=[_sds(w.shape)] * 4, input_output_aliases={4 + i: i for i in range(4)} if prev else {},
        compiler_params=pltpu.CompilerParams(dimension_semantics=("arbitrary",), vmem_limit_bytes=VMEM_LIMIT_BYTES))(parts, w, m, v, *unread)


def _adam_update(g, w, m, v):
    m2 = B1 * m + (1.0 - B1) * g
    v2 = B2 * v + (1.0 - B2) * (g * g)
    m_hat = m2 / (1.0 - B1 ** STEP)
    v_hat = v2 / (1.0 - B2 ** STEP)
    return -LR * (m_hat / (jnp.sqrt(v_hat) + ADAM_EPS) + WD * w), m2, v2


def _adamw_many(name, items, after):
    n = len(items)

    def body(*refs):
        outs = refs[4 * n + 1:]
        for i in range(n):
            g, w, m, v = (r[...] for r in refs[4 * i:4 * i + 4])
            for o, val in zip(outs[3 * i:3 * i + 3], _adam_update(g, w, m, v)):
                o[...] = val

    full = lambda a: pl.BlockSpec(a.shape, lambda t: (0, 0))
    flat = [a for item in items for a in item]
    res = _pc(body, name, (1,), [full(a) for a in flat] + [pl.BlockSpec(memory_space=pl.ANY)],
              [full(item[1]) for item in items for _ in range(3)],
              [_sds(item[1].shape) for item in items for _ in range(3)])(*flat, after)
    return [tuple(res[3 * i:3 * i + 3]) for i in range(n)]


def _sum_parts(parts):
    p, r, c = parts.shape
    tr = r
    while tr * c * 4 > (1 << 19) and tr % 16 == 0:
        tr //= 2

    def body(p_ref, o_ref):
        g = p_ref[0]
        for i in range(1, p):
            g = g + p_ref[i]
        o_ref[...] = g

    return _pc(body, "sum_parts", (r // tr,), [pl.BlockSpec((p, tr, c), lambda t: (0, t, 0))], pl.BlockSpec((tr, c), lambda t: (t, 0)),
               _sds((r, c)))(parts)


def _place():
    x, y, c = lax.axis_index("x"), lax.axis_index("y"), lax.axis_index("c")
    peers = []
    for k in range(1, N_DEV):
        px = (1 - x) if k & 4 else x
        py = (1 - y) if k & 2 else y
        pc = (1 - c) if k & 1 else c
        peers.append(((px, py, pc), 4 * px + 2 * py + pc))
    return 4 * x + 2 * y + c, peers


def _at(ref, idx):
    return ref if idx is None else ref.at[idx]


def _exchange_copies(plan, n, src_refs, dst_refs, send_sems, recv_sems, local_sems=None, with_arrivals=True):
    me, peers = _place()
    local = [] if local_sems is None else [
        pltpu.make_async_copy(_at(src_refs[si], sx), _at(dst_refs[di], dx), local_sems.at[i])
        for i, (si, sx, di, dx) in enumerate(plan(me, me, 0))]

    def remote(k, i, dev, entry):
        si, sx, di, dx = entry
        return pltpu.make_async_remote_copy(_at(src_refs[si], sx), _at(dst_refs[di], dx), send_sems.at[k * n + i], recv_sems.at[k * n + i],
                                            device_id=dev, device_id_type=MESH)

    sends = [remote(k, i, dev, e) for k, (dev, peer) in enumerate(peers) for i, e in enumerate(plan(me, peer, k + 1))]
    if not with_arrivals:
        return local, sends, []
    arrivals = [remote(k, i, dev, e) for k, (dev, peer) in enumerate(peers) for i, e in enumerate(plan(peer, me, k + 1))]
    return local, sends, arrivals


def _sem_shapes(n_copies, local=True):
    sems = [pltpu.SemaphoreType.DMA(((N_DEV - 1) * n_copies,)), pltpu.SemaphoreType.DMA(((N_DEV - 1) * n_copies,))]
    return sems + [pltpu.SemaphoreType.DMA((n_copies,))] if local else sems


def _exchange(name, srcs, dst_shapes, plan, n_copies):
    ns, nd = len(srcs), len(dst_shapes)

    def body(*refs):
        local, sends, arrivals = _exchange_copies(plan, n_copies, refs[:ns], refs[ns:ns + nd], *refs[ns + nd:])
        for cp in local + sends:
            cp.start()
        for cp in arrivals:
            cp.wait_recv()
        for cp in sends:
            cp.wait_send()
        for cp in local:
            cp.wait()

    any_spec = pl.BlockSpec(memory_space=pl.ANY)
    return pl.pallas_call(
        body, name=name, in_specs=[any_spec] * ns, out_specs=[any_spec] * nd, out_shape=list(dst_shapes),
        scratch_shapes=_sem_shapes(n_copies))(*srcs)


HBM_SPEC = pl.BlockSpec(memory_space=pltpu.HBM)
SEM_SPEC = pl.BlockSpec(memory_space=pltpu.SEMAPHORE)
ANY_SPEC = pl.BlockSpec(memory_space=pl.ANY)
TOKEN_SPEC = pl.BlockSpec(memory_space=pltpu.VMEM)
SIDE_EFFECT = pltpu.SideEffectType.DATAFLOW_SIDE_EFFECTING


def _wait_all(local, sends, arrivals):
    for cp in arrivals:
        cp.wait_recv()
    for cp in sends:
        cp.wait_send()
    for cp in local:
        cp.wait()


def _exchange_start(name, srcs, dst_shapes, plan, n_copies, order):
    ns, nd = len(srcs), len(dst_shapes)
    nb = ns + nd

    def body(*refs):
        local, sends, _ = _exchange_copies(plan, n_copies, refs[:ns], refs[ns:nb], *refs[nb + 1:nb + 4], with_arrivals=False)
        for cp in local + sends:
            cp.start()
        refs[-1][...] = jnp.zeros((8, LANE), F32)

    lands = [pltpu.with_memory_space_constraint(lax.empty(d.shape, d.dtype), pltpu.HBM) for d in dst_shapes]
    srcs = [pltpu.with_memory_space_constraint(a, pltpu.HBM) for a in srcs]
    bufs = srcs + lands
    out = pl.pallas_call(
        body, name=name, in_specs=[HBM_SPEC] * nb + [ANY_SPEC],
        out_specs=[SEM_SPEC] * 3 + [HBM_SPEC] * nb + [TOKEN_SPEC],
        out_shape=_sem_shapes(n_copies) + [pltpu.HBM(a.shape, a.dtype) for a in bufs] + [_sds((8, LANE))],
        input_output_aliases={i: 3 + i for i in range(nb)},
        compiler_params=pltpu.CompilerParams(has_side_effects=SIDE_EFFECT))(*bufs, order)
    return out[:3], out[3:3 + ns], out[3 + ns:3 + nb], out[-1]


def _exchange_relay(name, sems, srcs, lands, plan, n_copies, plan2, n_copies2, after):
    ns, nd = len(srcs), len(lands)
    nb = ns + nd

    def body(*refs):
        land_refs = refs[ns:nb]
        _wait_all(*_exchange_copies(plan, n_copies, refs[:ns], land_refs, *refs[nb:nb + 3]))
        _, sends, _ = _exchange_copies(plan2, n_copies2, land_refs, land_refs, *refs[nb + 4:nb + 6], with_arrivals=False)
        for cp in sends:
            cp.start()
        refs[-1][...] = jnp.zeros((8, LANE), F32)

    out = pl.pallas_call(
        body, name=name, in_specs=[HBM_SPEC] * nb + [SEM_SPEC] * 3 + [ANY_SPEC],
        out_specs=[SEM_SPEC] * 2 + [HBM_SPEC] * nd + [TOKEN_SPEC],
        out_shape=_sem_shapes(n_copies2, local=False) + [pltpu.HBM(a.shape, a.dtype) for a in lands] + [_sds((8, LANE))],
        input_output_aliases={ns + i: 2 + i for i in range(nd)},
        compiler_params=pltpu.CompilerParams(has_side_effects=SIDE_EFFECT))(*srcs, *lands, *sems, after)
    return out[:2], out[2:2 + nd], out[-1]


def _exchange_wait(name, sems, srcs, lands, plan, n_copies, after):
    srcs = [] if srcs is None else list(srcs)
    ns, nd = len(srcs), len(lands)
    nb = ns + nd

    def body(*refs):
        land_refs = refs[ns:nb]
        _wait_all(*_exchange_copies(plan, n_copies, refs[:ns] if ns else land_refs, land_refs, *refs[nb:nb + len(sems)]))

    bufs = srcs + list(lands)
    out = pl.pallas_call(
        body, name=name, in_specs=[HBM_SPEC] * nb + [SEM_SPEC] * len(sems) + [ANY_SPEC],
        out_specs=[HBM_SPEC] * nb, out_shape=[pltpu.HBM(a.shape, a.dtype) for a in bufs],
        input_output_aliases={i: i for i in range(nb)},
        compiler_params=pltpu.CompilerParams(has_side_effects=SIDE_EFFECT))(*bufs, *sems, after)
    return out[ns:]


def _all_gather(name, arrs):
    plan = lambda me, peer, k: [(i, None, i, me) for i in range(len(arrs))]
    return _exchange(name, arrs, [_sds((N_DEV,) + a.shape, a.dtype) for a in arrs], plan, len(arrs))


def _mix0_fwd(h, p):
    z = _mm_nt("mix0_in", h, p["ab_w_in"])
    y_a, d = _pool_fwd(z, p["pool_w"], p["pool_scale"])
    y_b = _sgu_fwd(z, p["sgu_ln_g"], p["sgu_ln_b"], p["sgu_w"], p["sgu_bt"])
    ycat = jnp.concatenate([y_a, y_b], axis=1)
    return _mm_nn("mix0_out", ycat, p["ab_w_out"]), (h, z, d, ycat)


def _mix0_bwd(df, saved, p, after):
    h, z, d, ycat = saved
    dycat = _mm_nt("mix0_out_dx", df, p["ab_w_out"], after=after)
    g = {"ab_w_out": _mm_tn("mix0_out_dw", ycat, df, BF16)}
    dz_p, g["pool_w"], g["pool_scale"] = _pool_bwd(dycat, d, p["pool_w"], p["pool_scale"])
    dz_u, dz_v, g["sgu_ln_g"], g["sgu_ln_b"], g["sgu_w"], dbt = _sgu_bwd(
        z, dycat, p["sgu_ln_g"], p["sgu_ln_b"], p["sgu_w"], p["sgu_bt"], p["head_sum"])
    g["sgu_b"] = dbt[:, :NH].T
    dz = jnp.concatenate([dz_p, dz_u, dz_v], axis=1)
    g["ab_w_in"] = _mm_tn("mix0_in_dw", dz, h, BF16)
    return _mm_nn("mix0_in_dx", dz, p["ab_w_in"]), g


def _mix1_fwd(h, p):
    u = _mm_nn("ssm_w_in", h, p["ssm_w_in"])
    x_re, x_im, y = _ssm_scan("ssm_scan_fwd", u, p["wb_bd"], p["lam_bar_re"], p["lam_bar_im"], p["wc_bd"], False)
    g = _ssm_act_fwd(y, u, p["ssm_d"])
    zz = _mm_nn("ssm_glu", g, p["ssm_w_glu"])
    return _glu_fwd(zz), (h, u, x_re, x_im, y, g, zz)


def _mix1_bwd(df, saved, p, after):
    h, u, x_re, x_im, y, g, zz = saved
    gr = {}
    dzz = _glu_bwd(zz, df)
    dg = _mm_nt("ssm_glu_dx", dzz, p["ssm_w_glu"], after=after)
    gr["ssm_w_glu"] = _mm_tn("ssm_glu_dw", g, dzz, BF16)
    dy, gr["ssm_d"] = _ssm_act_bwd(dg, y, u, p["ssm_d"])
    _, _, du_ssm, g_lam_re, g_lam_im, mb_re, mb_im, mc_re, mc_im = _ssm_scan(
        "ssm_scan_bwd", dy, p["wc_bd"], p["lam_bar_re"], -p["lam_bar_im"], p["wb_bd"], True, states=(x_re, x_im), u=u)
    du = _axpy(du_ssm, dy, p["ssm_d"])
    gr["ssm_w_in"] = _mm_tn("ssm_w_in_dw", h, du, BF16)
    dh = _mm_nt("ssm_w_in_dx", du, p["ssm_w_in"])
    per_group = lambda m: m[:, :, :SSM_P].reshape(SSM_G, SSM_N, SSM_P)
    gr["ssm_c_re"] = per_group(mc_re)
    gr["ssm_c_im"] = -per_group(mc_im)
    dlr, dli, ddt, dbr, dbi = _ssm_param_bwd(
        g_lam_re.reshape(SSM_G, SSM_P), g_lam_im.reshape(SSM_G, SSM_P),
        per_group(mb_re).reshape(SSM_G, SSM_N * SSM_P), per_group(mb_im).reshape(SSM_G, SSM_N * SSM_P),
        p["lam_re"], p["lam_im"], p["lam_re_rep"], p["lam_im_rep"], p["log_dt"], p["b_re"], p["b_im"], p["seg"])
    gr["ssm_lam_re"], gr["ssm_lam_im"], gr["ssm_log_dt"] = dlr, dli, ddt[:, 0]
    gr["ssm_b_re"] = dbr.reshape(SSM_G, SSM_N, SSM_P)
    gr["ssm_b_im"] = dbi.reshape(SSM_G, SSM_N, SSM_P)
    return dh, gr


def _ssm_params(lam_re, lam_im, b_re, b_im, c_re, c_im, log_dt):
    wide = lambda b: b.transpose(0, 2, 1).reshape(SSM_G, SSM_N * SSM_P)
    p = {"lam_re": lam_re, "lam_im": lam_im, "log_dt": log_dt.reshape(SSM_G, 1),
         "lam_re_rep": jnp.tile(lam_re, (1, SSM_N)), "lam_im_rep": jnp.tile(lam_im, (1, SSM_N)), "b_re": wide(b_re), "b_im": wide(b_im)}
    lbr, lbi, bbr, bbi = _ssm_prep(lam_re, lam_im, p["lam_re_rep"], p["lam_im_rep"], p["log_dt"], p["b_re"], p["b_im"])
    p["lam_bar_re"], p["lam_bar_im"] = lbr.reshape(1, SSM_L), lbi.reshape(1, SSM_L)
    rows = lambda m: m.reshape(SSM_G * SSM_N, SSM_P)
    p["wb_bd"] = _ssm_block_diag(rows(bbr), rows(bbi))
    p["wc_bd"] = _ssm_block_diag(rows(c_re), rows(-c_im))
    p["seg"] = jnp.tile(jnp.eye(SSM_P, dtype=F32), (SSM_N, 1))
    return p


RES_WEIGHT = (0.5, 1.0, 0.5)


def _local_step(x, tgt, vecs, weights_of, on_part, on_grads):
    def fns(i, w):
        if i % 3 != 1:
            win, wout_of = w
            return ((lambda h: _ffn_fwd(h, win, wout_of)),
                    (lambda df, sv, after: (_ffn_bwd(df, sv, win, wout_of(None), lambda tag, part: on_part(i, tag, part), after), None)))
        if i == 1:
            return (lambda h: _mix0_fwd(h, w)), (lambda df, sv, after: _mix0_bwd(df, sv, w, after))
        return (lambda h: _mix1_fwd(h, w)), (lambda df, sv, after: _mix1_bwd(df, sv, w, after))

    rw = RES_WEIGHT * 2
    saved, bwd = [], []
    f = None
    for i in range(6):
        w, token = weights_of(i, x if i == 0 else f)
        fwd, b = fns(i, w)
        if i == 0:
            h = _prenorm_fwd(x, vecs, 0, token)
        else:
            x, h = _post_pre_fwd(x, f, vecs, i, rw[i - 1], token)
        f, inner = fwd(h)
        saved.append((x, f, inner))
        bwd.append(b)
    loss_row, dx = _loss_fwd_bwd(_postnorm_fwd(x, f, vecs, 5, rw[5]), tgt)
    df, dv_top = _postnorm_bwd(dx, f, vecs, 5, rw[5])
    token = jnp.zeros((8, LANE), F32)
    for i in reversed(range(6)):
        x_i, _, inner = saved[i]
        dh, extra = bwd[i](df, inner, token)
        if i > 0:
            dx, df, dv = _pre_post_bwd(dx, dh, x_i, saved[i - 1][1], vecs, i, rw[i - 1])
        else:
            dx, dv = _prenorm_bwd(dx, dh, x_i, vecs, 0)
        token = on_grads(i, extra, dv, dv_top if i == 5 else None, loss_row)
    return dx


def _pad_rows(v, rows):
    return jnp.pad(v, (0, rows * LANE - v.shape[0])).reshape(rows, LANE)


def _pack(parts):
    flat, layout, off = [], [], 0
    for a in parts:
        n = a.size
        padded = -(-n // LANE) * LANE
        flat.append(jnp.pad(a.reshape(-1).astype(F32), (0, padded - n)))
        layout.append((off, n, a.shape))
        off += padded
    return jnp.concatenate(flat), layout


def _unpack(flat, layout):
    return [flat[off:off + n].reshape(shape) for off, n, shape in layout]


SMALL_REPLICATED = ["ada_b", "pool_w", "pool_scale", "sgu_ln_g", "sgu_ln_b", "sgu_w", "sgu_b", "ssm_lam_re", "ssm_lam_im",
                    "ssm_b_re", "ssm_b_im", "ssm_c_re", "ssm_c_im", "ssm_log_dt"]
SMALL_SHARDED = ["norm_pre", "norm_post", "ssm_d"]
TRANSPOSED = ["ffn_w_in", "ab_w_in", "ssm_b_re", "ssm_b_im"]
WEIGHTS = ['ada_w', 'ada_b', 'norm_pre', 'norm_post', 'ffn_w_in', 'ffn_w_out', 'ab_w_in', 'pool_w', 'pool_scale', 'sgu_ln_g',
           'sgu_ln_b', 'sgu_w', 'sgu_b', 'ab_w_out', 'ssm_w_in', 'ssm_lam_re', 'ssm_lam_im', 'ssm_b_re', 'ssm_b_im', 'ssm_c_re',
           'ssm_c_im', 'ssm_d', 'ssm_log_dt', 'ssm_w_glu']


def kernel(x, c, ada_w, ada_b, norm_pre, norm_post, ffn_w_in, ffn_w_out, ab_w_in, pool_w, pool_scale, sgu_ln_g, sgu_ln_b, sgu_w, sgu_b, ab_w_out, ssm_w_in, ssm_lam_re, ssm_lam_im, ssm_b_re, ssm_b_im, ssm_c_re, ssm_c_im, ssm_d, ssm_log_dt, ssm_w_glu, loss_target, m_ada_w, m_ada_b, m_norm_pre, m_norm_post, m_ffn_w_in, m_ffn_w_out, m_ab_w_in, m_pool_w, m_pool_scale, m_sgu_ln_g, m_sgu_ln_b, m_sgu_w, m_sgu_b, m_ab_w_out, m_ssm_w_in, m_ssm_lam_re, m_ssm_lam_im, m_ssm_b_re, m_ssm_b_im, m_ssm_c_re, m_ssm_c_im, m_ssm_d, m_ssm_log_dt, m_ssm_w_glu, v_ada_w, v_ada_b, v_norm_pre, v_norm_post, v_ffn_w_in, v_ffn_w_out, v_ab_w_in, v_pool_w, v_pool_scale, v_sgu_ln_g, v_sgu_ln_b, v_sgu_w, v_sgu_b, v_ab_w_out, v_ssm_w_in, v_ssm_lam_re, v_ssm_lam_im, v_ssm_b_re, v_ssm_b_im, v_ssm_c_re, v_ssm_c_im, v_ssm_d, v_ssm_log_dt, v_ssm_w_glu):
    args = locals()
    wts = {n: args[n] for n in WEIGHTS}
    mom = {n: args["m_" + n] for n in WEIGHTS}
    var = {n: args["v_" + n] for n in WEIGHTS}
    for n in TRANSPOSED:
        for t in (wts, mom, var):
            t[n] = jnp.swapaxes(t[n], -1, -2)
    me = 4 * lax.axis_index("x") + 2 * lax.axis_index("y") + lax.axis_index("c")
    s = x.shape[1]
    nd = D // N_DEV

    small_in, small_in_layout = _pack([c, norm_pre, norm_post, ssm_d])
    small_rows = -(-small_in.shape[0] // (8 * LANE)) * 8
    (g_small,) = _all_gather("gather_small", [_pad_rows(small_in, small_rows)])
    g_small = g_small.reshape(N_DEV, -1)
    c_all, npre_g, npost_g, sd_g = [jnp.stack([_unpack(g_small[j], small_in_layout)[i] for j in range(N_DEV)]) for i in range(4)]
    c_all = c_all.reshape(N_DEV, D)
    norm_pre_full = npre_g.transpose(1, 2, 0, 3).reshape(2, 3, D)
    norm_post_full = npost_g.transpose(1, 2, 0, 3).reshape(2, 3, D)
    ssm_d_full = sd_g.transpose(1, 0, 2).reshape(1, D)

    nw = ada_w.shape[-1]
    (mod_g,) = _all_gather("gather_mod", [_mod_part(c_all, ada_w)])
    mod = lax.dynamic_index_in_dim(mod_g, me, axis=2, keepdims=False)
    mod = (mod.transpose(1, 0, 2).reshape(2, N_DEV * nw) + ada_b).reshape(2, 3, 3, D)

    w_in_t = wts["ffn_w_in"]
    shards = [[w_in_t[0, 0]], [ffn_w_out[0, 0]], [wts["ab_w_in"][0], ab_w_out[0]], [w_in_t[0, 1]], [ffn_w_out[0, 1]],
              [w_in_t[1, 0]], [ffn_w_out[1, 0]], [ssm_w_in[0], ssm_w_glu[0]], [w_in_t[1, 1]], [ffn_w_out[1, 1]]]
    first_group = {0: 0, 1: 2, 2: 3, 3: 5, 4: 7, 5: 8}
    first_groups = set(first_group.values())
    same_core = (2, 4, 6)

    def gather_plan(n):
        return lambda me_, peer_, k: [(a, None, a, me_) for a in range(n)] if k in (0, 1) + same_core else []

    def relay_plan(n):
        return lambda me_, peer_, k: [(a, me_ ^ kk, a, me_ ^ kk) for kk in same_core for a in range(n)] if k == 1 else []

    gathers, relays = [], {}
    token = mod_g
    for g, group in enumerate(shards):
        group = [a.astype(BF16) for a in group]
        sems, srcs_thru, lands, token = _exchange_start(
            f"gather_start_{g}", group, [_sds((N_DEV,) + a.shape, BF16) for a in group], gather_plan(len(group)), len(group), token)
        gathers.append((sems, srcs_thru, lands))
    mod6 = mod.reshape(6, 3, D)
    vecs = jnp.stack([norm_pre_full.reshape(6, D), mod6[:, 1], mod6[:, 0], norm_post_full.reshape(6, D), mod6[:, 2]]
                     + [jnp.zeros((6, D), F32)] * 3, axis=1)
    vecs = vecs + token[0, 0]

    def relay(g, after):
        sems, srcs_thru, lands = gathers[g]
        n = len(lands)
        relays[g] = _exchange_relay(f"gather_relay_{g}", sems, srcs_thru, lands, gather_plan(n), n, relay_plan(n), 3 * n, after)

    def fetch(g, after):
        if g not in relays:
            relay(g, after)
        sems, lands, token = relays[g]
        n = len(lands)
        got = _exchange_wait(f"gather_wait_{g}", sems, None, lands, relay_plan(n), 3 * n, after)
        if g + 1 in first_groups:
            relay(g + 1, got[0])
            token = relays[g + 1][2]
        return got, token

    head_sum = jnp.repeat(jnp.eye(NH, LANE, dtype=F32), HD, axis=0)
    mix0 = {"pool_w": pool_w[0], "pool_scale": pool_scale, "sgu_ln_g": sgu_ln_g, "sgu_ln_b": sgu_ln_b, "sgu_w": sgu_w[0],
            "sgu_bt": jnp.pad(sgu_b[0].T, ((0, 0), (0, LANE - NH))), "head_sum": head_sum}
    mix1 = _ssm_params(ssm_lam_re[0], ssm_lam_im[0], ssm_b_re[0], ssm_b_im[0], ssm_c_re[0], ssm_c_im[0], ssm_log_dt[0])
    mix1["ssm_d"] = ssm_d_full

    def weights_of(i, x_in):
        g = first_group[i]
        if i % 3 != 1:
            (win,), token = fetch(g, x_in)
            cache = []

            def wout_of(act):
                if not cache:
                    cache.append(fetch(g + 1, act)[0][0])
                return cache[0]

            return (win, wout_of), token
        (a, b), token = fetch(g, x_in)
        if i == 1:
            return dict(mix0, ab_w_in=a.reshape(-1, D), ab_w_out=b.reshape(D, D)), token
        return dict(mix1, ssm_w_in=a.reshape(D, D), ssm_w_glu=b.transpose(1, 0, 2).reshape(D, -1)), token

    def shard_cols(a):
        r = a.shape[0]
        return a.reshape(r, N_DEV, -1).transpose(1, 0, 2)

    scatter_plan = lambda me_, peer_, k: [(0, peer_, 0, me_), (1, peer_, 1, me_)]
    scatter_plan1 = lambda me_, peer_, k: [(0, peer_, 0, me_)]
    scatters = []
    last_token = [jnp.zeros((8, LANE), F32)]
    pieces, mixer, bundles = {}, {}, {}
    bundle_plan = lambda me_, peer_, k: [(0, None, 0, me_)]

    held = {}

    def on_part(i, tag, part):
        if i != 0 and tag == "w_out":
            held[i] = part
            return last_token[0]
        names, parts, plan = (("ffn_" + tag,), [part], scatter_plan1) if i == 0 else (("ffn_w_out", "ffn_w_in"), [held[i], part], scatter_plan)
        sems, srcs_thru, lands, last_token[0] = _exchange_start(
            f"scatter_start_{i}_{tag}", parts, [_sds(a.shape, BF16) for a in parts], plan, len(parts), last_token[0])
        scatters.append((i, names, plan, sems, srcs_thru, lands))
        return last_token[0]
    mix0_names = ["pool_w", "pool_scale", "sgu_ln_g", "sgu_ln_b", "sgu_w", "sgu_b"]
    mix1_names = ["ssm_lam_re", "ssm_lam_im", "ssm_b_re", "ssm_b_im", "ssm_c_re", "ssm_c_im", "ssm_log_dt", "ssm_d"]

    def start_bundle(tag, arrays):
        flat, layout = _pack(arrays)
        rows = -(-flat.shape[0] // (8 * LANE)) * 8
        plan = gather_plan(1) if tag == "a" else bundle_plan
        sems, srcs_thru, lands, last_token[0] = _exchange_start(
            f"small_start_{tag}", [_pad_rows(flat, rows)], [_sds((N_DEV, rows, LANE))], plan, 1, last_token[0])
        bundles[tag] = (sems, srcs_thru, lands, layout)

    def on_grads(i, extra, dv, dv_top, loss_row):
        pieces[i] = dv
        if i == 5:
            pieces["top"] = dv_top
        if i == 4:
            mixer.update({n: extra[n] for n in mix1_names})
        if i == 1:
            mixer.update({n: extra[n] for n in mix0_names})
            start_bundle("a", [jnp.stack([pieces[j] for j in ("top", 5, 4, 3, 2, 1)])] + [mixer[n] for n in mix0_names + mix1_names])
        if i == 0:
            start_bundle("b", [dv, loss_row])
        if i % 3 != 1:
            return last_token[0]
        if i == 1:
            names, parts = ("ab_w_in", "ab_w_out"), [extra["ab_w_in"].reshape(N_DEV, -1, D), extra["ab_w_out"].reshape(N_DEV, nd, D)]
        else:
            names, parts = ("ssm_w_in", "ssm_w_glu"), [extra["ssm_w_in"].reshape(N_DEV, nd, D), shard_cols(extra["ssm_w_glu"])]
        sems, srcs_thru, lands, last_token[0] = _exchange_start(
            f"scatter_start_{i}", parts, [_sds(a.shape, BF16) for a in parts], scatter_plan, 2, last_token[0])
        scatters.append((i, names, scatter_plan, sems, srcs_thru, lands))
        return last_token[0]

    grad_x = _local_step(x[0], loss_target[0], vecs, weights_of, on_part, on_grads)

    out_g, out_d, out_m, out_v = {}, {}, {}, {}
    big_out = {}

    def adam_big(name, recv, n, slot=0, after=None):
        c_ = wts[n].shape[-1]
        big_out[n] = _adamw(name, recv.reshape(recv.shape[0], -1, c_), *[t[n].reshape(-1, c_) for t in (wts, mom, var)],
                            slot=slot, prev=big_out.get(n), after=after)
        return big_out[n][0]

    ffn_slot = {0: 0, 2: 1, 3: 2, 5: 3}

    def land_and_update(entries, after):
        for i, names, plan, sems, srcs_thru, lands in entries:
            recv = _exchange_wait(f"scatter_wait_{i}_{names[0]}", sems, srcs_thru, lands, plan, len(names), after)
            for n, r in zip(names, recv):
                after = adam_big(f"adamw_{n}_{i}", r, n, ffn_slot.get(i, 0), after)
        return after

    after = land_and_update([e for e in scatters if e[0] != 0], last_token[0])

    def landed(tag, g_parts):
        layout = bundles[tag][3]
        off, n, shape = layout[0]
        dmods = g_parts.reshape(N_DEV, -1)[:, off:off + n].reshape((N_DEV,) + shape)
        total = _sum_parts(g_parts)
        return dmods, _unpack(total.reshape(-1), layout), total

    def adam_small(n, g, after=None):
        cols = wts[n].shape[-1]
        res = _adamw(f"adamw_{n}", g.reshape(1, -1, cols), *[t[n].reshape(-1, cols) for t in (wts, mom, var)], after=after)
        for o, arr in zip((out_g, out_d, out_m, out_v), res):
            o[n] = arr.reshape(wts[n].shape)
            if n in TRANSPOSED:
                o[n] = jnp.swapaxes(o[n], -1, -2)
        return res[0]

    sems, srcs_thru, lands, _ = bundles["a"]
    sems, lands, _ = _exchange_relay("small_relay_a", sems, srcs_thru, lands, gather_plan(1), 1, relay_plan(1), 3, after)
    (parts_a,) = _exchange_wait("small_wait_a", sems, None, lands, relay_plan(1), 3, after)
    shells_a, sums_a, after = landed("a", parts_a)
    small = dict(zip(mix0_names + mix1_names, sums_a[1:]))
    def adam_tiny(name, grads, after):
        view = lambda n, a: a.reshape(-1, wts[n].shape[-1])
        items = [(view(n, g),) + tuple(view(n, t[n]) for t in (wts, mom, var)) for n, g in grads.items()]
        for (n, _), item, res in zip(grads.items(), items, _adamw_many(name, items, after)):
            for o, arr in zip((out_g, out_d, out_m, out_v), (item[0],) + res):
                o[n] = arr.reshape(wts[n].shape)
        return res[0]

    tiny = ["pool_scale", "sgu_ln_g", "sgu_ln_b", "sgu_b", "ssm_lam_re", "ssm_lam_im", "ssm_log_dt"]
    for n in [n for n in mix0_names + mix1_names if n not in tiny and n != "ssm_d"]:
        after = adam_small(n, small[n], after)
    after = adam_tiny("adamw_tiny_mixers", dict({n: small[n] for n in tiny},
                                                ssm_d=lax.dynamic_slice_in_dim(small["ssm_d"], me * nd, nd, axis=1)), after)
    def shell_grads(top, blocks, first):
        own_rows = jnp.concatenate([first[..., None, :, :], blocks[..., :0:-1, :, :]], axis=-3)
        next_rows = jnp.concatenate([own_rows[..., 1:, :, :], top[..., None, :, :]], axis=-3)
        dmod_ = jnp.stack([own_rows[..., V_SHIFT, :], own_rows[..., V_SCALE, :], next_rows[..., V_GATE, :]], axis=-2)
        return dmod_, own_rows[..., V_GPRE, :], next_rows[..., V_GPOST, :]

    def ada_w_layer(l, dmod_l, after):
        mine = lax.dynamic_index_in_dim(dmod_l.reshape(N_DEV, N_DEV, nw), me, axis=1, keepdims=False)
        return adam_big(f"adamw_ada_w_{l}", _ada_w_grad(c_all.T, mine[None]), "ada_w", l, after)

    after = ada_w_layer(1, shell_grads(shells_a[:, 0], shells_a, jnp.zeros_like(shells_a[:, 0]))[0][:, 3:], after)
    sems, srcs_thru, lands, _ = bundles["b"]
    (parts_b,) = _exchange_wait("small_wait_b", sems, srcs_thru, lands, bundle_plan, 1, after)
    shell_b, (first_sum, loss_sum), after = landed("b", parts_b)
    loss = loss_sum[0, 0]

    dmod_sum, dg_pre_sum, dg_post_sum = shell_grads(sums_a[0][0], sums_a[0], first_sum)
    own = lambda a: lax.dynamic_slice_in_dim(a, me * nd, nd, axis=1)
    after = adam_tiny("adamw_tiny_shell", {"ada_b": dmod_sum, "norm_pre": own(dg_pre_sum), "norm_post": own(dg_post_sum)}, after)

    after = ada_w_layer(0, shell_grads(shells_a[:, 0], shells_a, shell_b)[0][:, :3], after)

    land_and_update([e for e in scatters if e[0] == 0], after)
    for n, res in big_out.items():
        for o, arr in zip((out_g, out_d, out_m, out_v), res):
            o[n] = arr.reshape(wts[n].shape)
            if n in TRANSPOSED:
                o[n] = jnp.swapaxes(o[n], -1, -2)

    return (loss, grad_x[None], *[out_g[n] for n in WEIGHTS], *[out_d[n] for n in WEIGHTS],
            *[out_m[n] for n in WEIGHTS], *[out_v[n] for n in WEIGHTS])
```

```python
import math

import jax
import jax.numpy as jnp
from jax import lax
from jax.experimental import pallas as pl
from jax.experimental.pallas import tpu as pltpu

F32 = jnp.float32
BF16 = jnp.bfloat16
MESH = pl.DeviceIdType.MESH
HIGHEST = lax.Precision.HIGHEST

N_DEV = 8
D = 1024
D_FF = 2816
FSH = 2 * D_FF // N_DEV
EPS = 1e-6
POOL_WINDOWS = (2, 4, 8, 16)
HD = 128
NH = 4
SSM_G, SSM_P, SSM_N = 64, 64, 16
SSM_L = SSM_G * SSM_P
LR, B1, B2, ADAM_EPS, WD, STEP = 0.001, 0.9, 0.999, 1e-08, 0.01, 10
GELU_C = math.sqrt(2.0 / math.pi)
VMEM_LIMIT_BYTES = 48 * 1024 * 1024
LANE = 128


def _pc(body, name, grid, in_specs, out_specs, out_shape, scratch=()):
    return pl.pallas_call(
        body, name=name, grid=grid, in_specs=in_specs, out_specs=out_specs, out_shape=out_shape,
        scratch_shapes=list(scratch),
        compiler_params=pltpu.CompilerParams(dimension_semantics=("arbitrary",) * len(grid),
                                             vmem_limit_bytes=VMEM_LIMIT_BYTES))


def _sds(shape, dtype=F32):
    return jax.ShapeDtypeStruct(tuple(shape), dtype)


def _bf(v):
    return v if v.dtype == BF16 else v.astype(BF16)


def _row_spec(ts, width, col=0):
    return pl.BlockSpec((ts, width), lambda t, _c=col: (t, _c))


def _vec_spec(width, col=0):
    return pl.BlockSpec((1, width), lambda t, _c=col: (0, _c))


def _mm(name, a, b, contract, grid, a_spec, b_spec, o_spec, out_shape, acc_axis=None, after=None):
    dn = (contract, ((), ()))

    def body(a_ref, b_ref, *rest):
        o_ref = rest[-1]
        r = lax.dot_general(_bf(a_ref[...]), _bf(b_ref[...]), dn, preferred_element_type=F32)
        if acc_axis is None:
            o_ref[...] = r.astype(o_ref.dtype)
        else:
            k = pl.program_id(acc_axis)

            @pl.when(k == 0)
            def _():
                o_ref[...] = r

            @pl.when(k > 0)
            def _():
                o_ref[...] += r

    if after is None:
        return _pc(body, name, grid, [a_spec, b_spec], o_spec, out_shape)(a, b)
    return _pc(body, name, grid, [a_spec, b_spec, pl.BlockSpec(memory_space=pl.ANY)], o_spec, out_shape)(a, b, after)


def _mm_sum(name, a, b, ts, after=None):
    nj, s, k = a.shape
    n = b.shape[2]

    def body(a_ref, b_ref, *rest):
        acc = jnp.dot(a_ref[0], b_ref[0], preferred_element_type=F32)
        for j in range(1, nj):
            acc = acc + jnp.dot(a_ref[j], b_ref[j], preferred_element_type=F32)
        rest[-1][...] = acc

    specs = [pl.BlockSpec((nj, ts, k), lambda t: (0, t, 0)), pl.BlockSpec((nj, k, n), lambda t: (0, 0, 0))]
    args = (a, b)
    if after is not None:
        specs, args = specs + [pl.BlockSpec(memory_space=pl.ANY)], args + (after,)
    return _pc(body, name, (s // ts,), specs, pl.BlockSpec((ts, n), lambda t: (t, 0)), _sds((s, n)))(*args)


def _tile(s):
    return min(s, 1024)


def _div_tile(n, cap=1024):
    t = min(n, cap) // LANE * LANE
    while n % t:
        t -= LANE
    return t


def _mm_nn(name, a, b, out_dtype=F32):
    s, k = a.shape
    n = b.shape[1]
    ts, tn = _tile(s), _div_tile(n)
    return _mm(name, a, b, ((1,), (0,)), (n // tn, s // ts),
               pl.BlockSpec((ts, k), lambda j, t: (t, 0)), pl.BlockSpec((k, tn), lambda j, t: (0, j)),
               pl.BlockSpec((ts, tn), lambda j, t: (t, j)), _sds((s, n), out_dtype))


def _mm_nt(name, a, b, out_dtype=F32, after=None):
    s, n = a.shape
    k = b.shape[0]
    ts, tk = _tile(s), _div_tile(k)
    return _mm(name, a, b, ((1,), (1,)), (k // tk, s // ts),
               pl.BlockSpec((ts, n), lambda j, t: (t, 0)), pl.BlockSpec((tk, n), lambda j, t: (j, 0)),
               pl.BlockSpec((ts, tk), lambda j, t: (t, j)), _sds((s, k), out_dtype), after=after)


def _mm_tn(name, a, b, out_dtype=F32, tm=512, tn=512):
    s, m = a.shape
    n = b.shape[1]
    tm, tn = min(m, tm), min(n, tn)
    return _mm(name, a, b, ((0,), (0,)), (m // tm, n // tn),
               pl.BlockSpec((s, tm), lambda i, j: (0, i)), pl.BlockSpec((s, tn), lambda i, j: (0, j)),
               pl.BlockSpec((tm, tn), lambda i, j: (i, j)), _sds((m, n), out_dtype))


def _rstd(v):
    return lax.rsqrt(jnp.mean(v * v, axis=-1, keepdims=True) + EPS)


V_GPRE, V_SCALE, V_SHIFT, V_GPOST, V_GATE = range(5)


def _vrow(v, r):
    return v[r:r + 1]


def _vblock(i):
    return pl.BlockSpec((None, 8, D), lambda t: (i, 0, 0))


def _head(xv, v):
    return ((xv * _rstd(xv) * _vrow(v, V_GPRE)) * (1.0 + _vrow(v, V_SCALE)) + _vrow(v, V_SHIFT)).astype(BF16)


def _tail(xv, fv, v, rw):
    return xv + (rw * _vrow(v, V_GATE)) * (fv * _rstd(fv) * _vrow(v, V_GPOST))


def _prenorm_fwd(x, vecs, i, after):
    s = x.shape[0]
    ts = min(s, 512)

    def body(x_ref, v_ref, after_ref, h_ref):
        h_ref[...] = _head(x_ref[...], v_ref[...])

    return _pc(body, "prenorm_fwd", (s // ts,), [_row_spec(ts, D), _vblock(i), pl.BlockSpec(memory_space=pl.ANY)], _row_spec(ts, D),
               _sds((s, D), BF16))(x, vecs, after)


def _postnorm_fwd(x, f, vecs, i, rw):
    s = x.shape[0]
    ts = min(s, 512)

    def body(x_ref, f_ref, v_ref, o_ref):
        o_ref[...] = _tail(x_ref[...], f_ref[...], v_ref[...], rw)

    return _pc(body, "postnorm_fwd", (s // ts,), [_row_spec(ts, D)] * 2 + [_vblock(i)], _row_spec(ts, D), _sds((s, D)))(x, f, vecs)


def _post_pre_fwd(x, f, vecs, i, rw_prev, after):
    s = x.shape[0]
    ts = min(s, 512)

    def body(x_ref, f_ref, vp_ref, vc_ref, after_ref, xo_ref, h_ref):
        xv = _tail(x_ref[...], f_ref[...], vp_ref[...], rw_prev)
        xo_ref[...] = xv
        h_ref[...] = _head(xv, vc_ref[...])

    return _pc(body, "post_pre_fwd", (s // ts,),
               [_row_spec(ts, D)] * 2 + [_vblock(i - 1), _vblock(i), pl.BlockSpec(memory_space=pl.ANY)], [_row_spec(ts, D)] * 2,
               [_sds((s, D)), _sds((s, D), BF16)])(x, f, vecs, vecs, after)


def _zero_at_first(first, *refs):
    @pl.when(first)
    def _():
        for ref in refs:
            ref[...] = jnp.zeros_like(ref)


def _acc(ref, first, v):
    @pl.when(first)
    def _():
        ref[...] = v

    @pl.when(jnp.logical_not(first))
    def _():
        ref[...] += v


def _colsum(v):
    return jnp.sum(v, axis=0, keepdims=True)


def _tail_bwd(do, fv, v, rw, dv_ref):
    gv = _vrow(v, V_GPOST)
    r = _rstd(fv)
    fn = fv * r
    dv_ref[V_GATE:V_GATE + 1, :] += rw * _colsum(do * (fn * gv))
    dy = (rw * _vrow(v, V_GATE)) * do
    dv_ref[V_GPOST:V_GPOST + 1, :] += _colsum(dy * fn)
    dfn = dy * gv
    return (r * (dfn - fn * jnp.mean(dfn * fn, axis=-1, keepdims=True))).astype(BF16)


def _head_bwd(do, dhv, xv, v, dv_ref):
    gv = _vrow(v, V_GPRE)
    r = _rstd(xv)
    xn = xv * r
    dv_ref[V_SHIFT:V_SHIFT + 1, :] += _colsum(dhv)
    dv_ref[V_SCALE:V_SCALE + 1, :] += _colsum(dhv * (xn * gv))
    dhp = dhv * (1.0 + _vrow(v, V_SCALE))
    dv_ref[V_GPRE:V_GPRE + 1, :] += _colsum(dhp * xn)
    dxn = dhp * gv
    return do + r * (dxn - xn * jnp.mean(dxn * xn, axis=-1, keepdims=True))


DV_SPEC = pl.BlockSpec((8, D), lambda t: (0, 0))


def _postnorm_bwd(dout, f, vecs, i, rw):
    s = dout.shape[0]
    ts = min(s, 512)

    def body(do_ref, f_ref, v_ref, df_ref, dv_ref):
        _zero_at_first(pl.program_id(0) == 0, dv_ref)
        df_ref[...] = _tail_bwd(do_ref[...], f_ref[...], v_ref[...], rw, dv_ref)

    return _pc(body, "postnorm_bwd", (s // ts,), [_row_spec(ts, D)] * 2 + [_vblock(i)], [_row_spec(ts, D), DV_SPEC],
               [_sds((s, D), BF16), _sds((8, D))])(dout, f, vecs)


def _prenorm_bwd(dout, dh, x, vecs, i):
    s = dout.shape[0]
    ts = min(s, 512)

    def body(do_ref, dh_ref, x_ref, v_ref, dx_ref, dv_ref):
        _zero_at_first(pl.program_id(0) == 0, dv_ref)
        dx_ref[...] = _head_bwd(do_ref[...], dh_ref[...], x_ref[...], v_ref[...], dv_ref)

    return _pc(body, "prenorm_bwd", (s // ts,), [_row_spec(ts, D)] * 3 + [_vblock(i)], [_row_spec(ts, D), DV_SPEC],
               [_sds((s, D)), _sds((8, D))])(dout, dh, x, vecs)


def _pre_post_bwd(dout, dh, x, f_prev, vecs, i, rw_prev):
    s = dout.shape[0]
    ts = min(s, 256)

    def body(do_ref, dh_ref, x_ref, f_ref, vc_ref, vp_ref, dx_ref, df_ref, dv_ref):
        _zero_at_first(pl.program_id(0) == 0, dv_ref)
        dx = _head_bwd(do_ref[...], dh_ref[...], x_ref[...], vc_ref[...], dv_ref)
        dx_ref[...] = dx
        df_ref[...] = _tail_bwd(dx, f_ref[...], vp_ref[...], rw_prev, dv_ref)

    rows = _row_spec(ts, D)
    return _pc(body, "pre_post_bwd", (s // ts,), [rows] * 4 + [_vblock(i), _vblock(i - 1)], [rows, rows, DV_SPEC],
               [_sds((s, D)), _sds((s, D), BF16), _sds((8, D))])(dout, dh, x, f_prev, vecs, vecs)


def _loss_fwd_bwd(y, tgt):
    s = y.shape[0]
    ts = min(s, 512)
    nt = s // ts

    def body(y_ref, t_ref, loss_ref, dy_ref, acc_ref):
        t = pl.program_id(0)
        e = y_ref[...] - t_ref[...]
        dy_ref[...] = e * (1.0 / D)
        _acc(acc_ref, t == 0, _colsum(e * e))

        @pl.when(t == nt - 1)
        def _():
            loss_ref[...] = jnp.full((1, LANE), 0.5 / D, F32) * jnp.sum(acc_ref[...])

    return _pc(body, "loss", (nt,), [_row_spec(ts, D)] * 2,
               [pl.BlockSpec((1, LANE), lambda t: (0, 0)), _row_spec(ts, D)],
               [_sds((1, LANE)), _sds((s, D))], scratch=[pltpu.VMEM((1, D), F32)])(y, tgt)


def _sigmoid(v):
    return 1.0 / (1.0 + jnp.exp(-v))


def _ffn_in_swiglu(h, win):
    s = h.shape[0]
    ts = _tile(s)
    nt = (((1,), (1,)), ((), ()))

    def body(h_ref, wa_ref, wb_ref, fac_ref, act_ref):
        hv = h_ref[...]
        a = lax.dot_general(hv, wa_ref[...], nt, preferred_element_type=F32)
        b = lax.dot_general(hv, wb_ref[...], nt, preferred_element_type=F32)
        sg = _sigmoid(a)
        silu = a * sg
        fac_ref[0] = (b * (sg * (1.0 + a * (1.0 - sg)))).astype(BF16)
        fac_ref[1] = silu.astype(BF16)
        act_ref[...] = (silu * b).astype(BF16)

    return _pc(body, "ffn_in", (4, s // ts),
               [pl.BlockSpec((ts, D), lambda k, t: (t, 0)), pl.BlockSpec((None, FSH, D), lambda k, t: (k, 0, 0)),
                pl.BlockSpec((None, FSH, D), lambda k, t: (k + 4, 0, 0))],
               [pl.BlockSpec((2, None, ts, FSH), lambda k, t: (0, k, t, 0)), pl.BlockSpec((None, ts, FSH), lambda k, t: (k, t, 0))],
               [_sds((2, 4, s, FSH), BF16), _sds((4, s, FSH), BF16)])(h, win, win)


def _ffn_out_dx_swiglu(df, wout, fac, after):
    s = df.shape[0]
    ts = _tile(s)
    nt = (((1,), (1,)), ((), ()))

    def body(df_ref, w_ref, fac_ref, after_ref, o_ref):
        d = lax.dot_general(df_ref[...], w_ref[...], nt, preferred_element_type=F32)
        o_ref[0] = (d * fac_ref[0]).astype(BF16)
        o_ref[1] = (d * fac_ref[1]).astype(BF16)

    spec = pl.BlockSpec((2, None, ts, FSH), lambda k, t: (0, k, t, 0))
    out = _pc(body, "ffn_out_dx", (4, s // ts),
              [pl.BlockSpec((ts, D), lambda k, t: (t, 0)), pl.BlockSpec((None, FSH, D), lambda k, t: (k, 0, 0)), spec,
               pl.BlockSpec(memory_space=pl.ANY)],
              spec, _sds((2, 4, s, FSH), BF16))(df, wout, fac, after)
    return out.reshape(N_DEV, s, FSH)


def _ffn_fwd(h, win, wout_of):
    s = h.shape[0]
    fac, act = _ffn_in_swiglu(h, win)
    f = _mm_sum("ffn_out", act, wout_of(act).reshape(4, FSH, D), min(s, 512))
    return f, (h, fac, act)


def _ffn_bwd(df, saved, win, wout, send, after):
    h, fac, act = saved
    s = h.shape[0]
    ts = s
    wout = wout.reshape(4, FSH, D)
    dwout = _mm("ffn_out_dw", act, df, ((0,), (0,)), (4, 2),
                pl.BlockSpec((None, s, FSH), lambda k, j: (k, 0, 0)), pl.BlockSpec((s, D // 2), lambda k, j: (0, j)),
                pl.BlockSpec((None, FSH, D // 2), lambda k, j: (k, 0, j)), _sds((4, FSH, D), BF16), after=after)
    dz = _ffn_out_dx_swiglu(df, wout, fac, send("w_out", dwout.reshape(N_DEV, D_FF // N_DEV, D)))
    dwin = _mm("ffn_in_dw", dz, h, ((0,), (0,)), (N_DEV, 2),
               pl.BlockSpec((None, s, FSH), lambda j, i: (j, 0, 0)), pl.BlockSpec((s, D // 2), lambda j, i: (0, i)),
               pl.BlockSpec((None, FSH, D // 2), lambda j, i: (j, 0, i)), _sds((N_DEV, FSH, D), BF16))
    return _mm_sum("ffn_in_dx", dz, win, min(s, 512), after=send("w_in", dwin))


def _shift_rows(v, k, row, s, back):
    if back:
        return jnp.where(row < s - k, pltpu.roll(v, s - k, 0), 0.0)
    return jnp.where(row >= k, pltpu.roll(v, k, 0), 0.0)


def _window_sum(v, w, row, s, back):
    k = 1
    while k < w:
        v = v + _shift_rows(v, k, row, s, back)
        k *= 2
    return v


def _pool_fwd(z, pool_w, pool_scale):
    s = z.shape[0]

    def body(z_ref, w_ref, sc_ref, y_ref, d_ref):
        row = lax.broadcasted_iota(jnp.int32, (s, HD), 0)
        for g, w in enumerate(POOL_WINDOWS):
            sl = slice(g * HD, (g + 1) * HD)
            a = z_ref[:, sl]
            cnt = jnp.minimum(row + 1, w).astype(F32)
            d = (_window_sum(a, w, row, s, False) / cnt - a).astype(BF16)
            d_ref[:, sl] = d
            y = jnp.dot(d, _bf(w_ref[g]), preferred_element_type=F32)
            y_ref[:, sl] = (y * sc_ref[:, sl]).astype(BF16)

    return _pc(body, "pool_fwd", (1,),
               [pl.BlockSpec((s, NH * HD), lambda i: (0, 0)), pl.BlockSpec((NH, HD, HD), lambda i: (0, 0, 0)),
                pl.BlockSpec((1, NH * HD), lambda i: (0, 0))],
               [pl.BlockSpec((s, NH * HD), lambda i: (0, 0))] * 2,
               [_sds((s, NH * HD), BF16)] * 2)(z, pool_w, pool_scale)


def _pool_bwd(dy, d, pool_w, pool_scale):
    s = dy.shape[0]

    def body(dy_ref, d_ref, w_ref, sc_ref, dz_ref, dw_ref, dsc_ref):
        row = lax.broadcasted_iota(jnp.int32, (s, HD), 0)
        for g, w in enumerate(POOL_WINDOWS):
            sl = slice(g * HD, (g + 1) * HD)
            dyg, dg, wg = dy_ref[:, sl], d_ref[:, sl], _bf(w_ref[g])
            yraw = jnp.dot(dg, wg, preferred_element_type=F32)
            dsc_ref[:, sl] = _colsum(dyg * yraw)
            dyr = _bf(dyg * sc_ref[:, sl])
            dw_ref[g] = lax.dot_general(dg, dyr, (((0,), (0,)), ((), ())), preferred_element_type=F32)
            dd = lax.dot_general(dyr, wg, (((1,), (1,)), ((), ())), preferred_element_type=F32)
            cnt = jnp.minimum(row + 1, w).astype(F32)
            dz_ref[:, sl] = (_window_sum(dd / cnt, w, row, s, True) - dd).astype(BF16)

    return _pc(body, "pool_bwd", (1,),
               [pl.BlockSpec((s, NH * HD), lambda i: (0, 0)), pl.BlockSpec((s, NH * HD), lambda i: (0, 0)),
                pl.BlockSpec((NH, HD, HD), lambda i: (0, 0, 0)), pl.BlockSpec((1, NH * HD), lambda i: (0, 0))],
               [pl.BlockSpec((s, NH * HD), lambda i: (0, 0)), pl.BlockSpec((NH, HD, HD), lambda i: (0, 0, 0)),
                pl.BlockSpec((1, NH * HD), lambda i: (0, 0))],
               [_sds((s, NH * HD), BF16), _sds((NH, HD, HD)), _sds((1, NH * HD))])(dy, d, pool_w, pool_scale)


def _gelu(v):
    return 0.5 * v * (1.0 + jnp.tanh(GELU_C * (v + 0.044715 * (v * v * v))))


def _gelu_and_grad(v):
    t = jnp.tanh(GELU_C * (v + 0.044715 * (v * v * v)))
    return 0.5 * v * (1.0 + t), 0.5 * (1.0 + t) + 0.5 * v * (1.0 - t * t) * (GELU_C * (1.0 + 3.0 * 0.044715 * (v * v)))


def _gelu_grad(v):
    return _gelu_and_grad(v)[1]


def _causal_mask():
    return lax.broadcasted_iota(jnp.int32, (HD, HD), 0) >= lax.broadcasted_iota(jnp.int32, (HD, HD), 1)


def _sgu_specs():
    w = NH * HD
    return [pl.BlockSpec((HD, w), lambda c: (c, 1)), pl.BlockSpec((HD, w), lambda c: (c, 2)),
            pl.BlockSpec((1, w), lambda c: (0, 0)), pl.BlockSpec((1, w), lambda c: (0, 0)),
            pl.BlockSpec((NH, HD, HD), lambda c: (0, 0, 0)), pl.BlockSpec((HD, LANE), lambda c: (0, 0))]


def _sgu_head(v, lng_ref, lnb_ref, w_ref, h):
    sl = slice(h * HD, (h + 1) * HD)
    vh = v[:, sl]
    xc = vh - jnp.mean(vh, axis=-1, keepdims=True)
    rs = lax.rsqrt(jnp.mean(xc * xc, axis=-1, keepdims=True) + EPS)
    vhat = xc * rs
    vn = _bf(vhat * lng_ref[:, sl] + lnb_ref[:, sl])
    wc = _bf(jnp.where(_causal_mask(), w_ref[h], 0.0))
    return sl, rs, vhat, vn, wc


def _sgu_fwd(z, ln_g, ln_b, sgu_w, sgu_bt):
    s = z.shape[0]

    def body(zu_ref, zv_ref, lng_ref, lnb_ref, w_ref, bt_ref, y_ref):
        u, v = _gelu(zu_ref[...]), _gelu(zv_ref[...])
        for h in range(NH):
            sl, _, _, vn, wc = _sgu_head(v, lng_ref, lnb_ref, w_ref, h)
            sp = jnp.dot(wc, vn, preferred_element_type=F32) + bt_ref[:, h:h + 1]
            y_ref[:, sl] = (u[:, sl] * sp).astype(BF16)

    return _pc(body, "sgu_fwd", (s // HD,), _sgu_specs(), pl.BlockSpec((HD, NH * HD), lambda c: (c, 0)),
               _sds((s, NH * HD), BF16))(z, z, ln_g, ln_b, sgu_w, sgu_bt)


def _sgu_bwd(z, dy, ln_g, ln_b, sgu_w, sgu_bt, head_sum):
    s = z.shape[0]
    w = NH * HD
    nc = s // HD

    def body(zu_ref, zv_ref, lng_ref, lnb_ref, w_ref, bt_ref, dy_ref, hs_ref,
             dzu_ref, dzv_ref, dlng_ref, dlnb_ref, dw_ref, dbt_ref, dsacc_ref):
        c = pl.program_id(0)
        _zero_at_first(c == 0, dsacc_ref, dw_ref, dlng_ref, dlnb_ref)
        zu, zv = zu_ref[...], zv_ref[...]
        (u, gu), (v, gv) = _gelu_and_grad(zu), _gelu_and_grad(zv)
        dyv = dy_ref[...]
        ds = dyv * u
        dsacc_ref[...] += ds
        for h in range(NH):
            sl, rs, vhat, vn, wc = _sgu_head(v, lng_ref, lnb_ref, w_ref, h)
            sp = jnp.dot(wc, vn, preferred_element_type=F32) + bt_ref[:, h:h + 1]
            dzu_ref[:, sl] = (dyv[:, sl] * sp * gu[:, sl]).astype(BF16)
            dsh = _bf(ds[:, sl])
            dwh = lax.dot_general(dsh, vn, (((1,), (1,)), ((), ())), preferred_element_type=F32)
            dw_ref[h] += jnp.where(_causal_mask(), dwh, 0.0)
            dvn = lax.dot_general(wc, dsh, (((0,), (0,)), ((), ())), preferred_element_type=F32)
            dlng_ref[:, sl] += _colsum(dvn * vhat)
            dlnb_ref[:, sl] += _colsum(dvn)
            dvh = dvn * lng_ref[:, sl]
            dv = rs * (dvh - jnp.mean(dvh, axis=-1, keepdims=True) - vhat * jnp.mean(dvh * vhat, axis=-1, keepdims=True))
            dzv_ref[:, sl] = (dv * gv[:, sl]).astype(BF16)

        @pl.when(c == nc - 1)
        def _():
            dbt_ref[...] = jnp.dot(dsacc_ref[...], hs_ref[...], preferred_element_type=F32, precision=HIGHEST)

    outs = _pc(body, "sgu_bwd", (nc,),
               _sgu_specs() + [pl.BlockSpec((HD, w), lambda c: (c, 1)), pl.BlockSpec((w, LANE), lambda c: (0, 0))],
               [pl.BlockSpec((HD, w), lambda c: (c, 0))] * 2 + [pl.BlockSpec((1, w), lambda c: (0, 0))] * 2
               + [pl.BlockSpec((NH, HD, HD), lambda c: (0, 0, 0)), pl.BlockSpec((HD, LANE), lambda c: (0, 0))],
               [_sds((s, w), BF16)] * 2 + [_sds((1, w))] * 2 + [_sds((NH, HD, HD)), _sds((HD, LANE))],
               scratch=[pltpu.VMEM((HD, w), F32)])(z, z, ln_g, ln_b, sgu_w, sgu_bt, dy, head_sum)
    return outs


def _cmul(ar, ai, br, bi):
    return ar * br - ai * bi, ar * bi + ai * br


def _ssm_prep(lam_re, lam_im, lam_re_rep, lam_im_rep, log_dt, b_re, b_im):
    def disc(lr, li, dt):
        mag = jnp.exp(lr * dt)
        return mag * jnp.cos(li * dt), mag * jnp.sin(li * dt)

    def body(lr_ref, li_ref, lrr_ref, lir_ref, ldt_ref, br_ref, bi_ref, or_ref, oi_ref, bbr_ref, bbi_ref):
        dt = jnp.exp(ldt_ref[...])
        or_ref[...], oi_ref[...] = disc(lr_ref[...], li_ref[...], dt)
        lr, li = lrr_ref[...], lir_ref[...]
        er, ei = disc(lr, li, dt)
        den = lr * lr + li * li
        kr = ((er - 1.0) * lr + ei * li) / den
        ki = (ei * lr - (er - 1.0) * li) / den
        bbr_ref[...], bbi_ref[...] = _cmul(kr, ki, br_ref[...], bi_ref[...])

    small = pl.BlockSpec((SSM_G, SSM_P), lambda i: (0, 0))
    wide = pl.BlockSpec((SSM_G, SSM_P * SSM_N), lambda i: (0, 0))
    col = pl.BlockSpec((SSM_G, 1), lambda i: (0, 0))
    return _pc(body, "ssm_prep", (1,), [small, small, wide, wide, col, wide, wide], [small, small, wide, wide],
               [_sds((SSM_G, SSM_P))] * 2 + [_sds((SSM_G, SSM_P * SSM_N))] * 2)(
        lam_re, lam_im, lam_re_rep, lam_im_rep, log_dt, b_re, b_im)


def _ssm_param_bwd(g_lam_re, g_lam_im, g_bb_re, g_bb_im, lam_re, lam_im, lam_re_rep, lam_im_rep, log_dt, b_re, b_im, seg):
    def body(glr_ref, gli_ref, gbr_ref, gbi_ref, lr_ref, li_ref, lrr_ref, lir_ref, ldt_ref, br_ref, bi_ref, seg_ref,
             dlr_ref, dli_ref, ddt_ref, dbr_ref, dbi_ref):
        dt = jnp.exp(ldt_ref[...])
        lr, li = lrr_ref[...], lir_ref[...]
        mag = jnp.exp(lr * dt)
        er, ei = mag * jnp.cos(li * dt), mag * jnp.sin(li * dt)
        den = lr * lr + li * li
        kr = ((er - 1.0) * lr + ei * li) / den
        ki = (ei * lr - (er - 1.0) * li) / den
        gbr, gbi = gbr_ref[...], gbi_ref[...]
        dbr_ref[...], dbi_ref[...] = _cmul(kr, -ki, gbr, gbi)
        tr, ti = _cmul(br_ref[...], -bi_ref[...], gbr, gbi)
        gkr = jnp.dot(tr, seg_ref[...], preferred_element_type=F32, precision=HIGHEST)
        gki = jnp.dot(ti, seg_ref[...], preferred_element_type=F32, precision=HIGHEST)
        lr, li = lr_ref[...], li_ref[...]
        mag = jnp.exp(lr * dt)
        er, ei = mag * jnp.cos(li * dt), mag * jnp.sin(li * dt)
        den = lr * lr + li * li
        ir, ii = lr / den, -li / den
        kr, ki = _cmul(er - 1.0, ei, ir, ii)
        ar, ai = _cmul(ir, -ii, gkr, gki)
        glr, gli = glr_ref[...] + ar, gli_ref[...] + ai
        qr, qi = _cmul(kr, ki, ir, ii)
        g1r, g1i = _cmul(-qr, qi, gkr, gki)
        g2r, g2i = _cmul(dt * er, -dt * ei, glr, gli)
        dlr_ref[...] = g1r + g2r
        dli_ref[...] = g1i + g2i
        wr, wi = _cmul(lr, li, er, ei)
        g_dt = jnp.sum(wr * glr + wi * gli, axis=-1, keepdims=True)
        ddt_ref[...] = jnp.broadcast_to(dt * g_dt, (SSM_G, LANE))

    small = pl.BlockSpec((SSM_G, SSM_P), lambda i: (0, 0))
    wide = pl.BlockSpec((SSM_G, SSM_P * SSM_N), lambda i: (0, 0))
    col = pl.BlockSpec((SSM_G, 1), lambda i: (0, 0))
    segs = pl.BlockSpec((SSM_P * SSM_N, SSM_P), lambda i: (0, 0))
    return _pc(body, "ssm_param_bwd", (1,), [small, small, wide, wide, small, small, wide, wide, col, wide, wide, segs],
               [small, small, pl.BlockSpec((SSM_G, LANE), lambda i: (0, 0)), wide, wide],
               [_sds((SSM_G, SSM_P))] * 2 + [_sds((SSM_G, LANE))] + [_sds((SSM_G, SSM_P * SSM_N))] * 2)(
        g_lam_re, g_lam_im, g_bb_re, g_bb_im, lam_re, lam_im, lam_re_rep, lam_im_rep, log_dt, b_re, b_im, seg)


SCAN_LANES = 512
SCAN_ROWS = 8


SCAN_GROUPS = SCAN_LANES // SSM_P
SCAN_COLS = SCAN_GROUPS * SSM_N
SCAN_CHUNK = 512


def _ssm_scan(name, v, w_in, lam_re, lam_im, w_out, reverse, states=None, u=None):
    s = v.shape[0]
    ln, rows, ch = SCAN_LANES, SCAN_ROWS, min(SCAN_CHUNK, s)
    nch, ntile = s // ch, ch // rows
    nt_dims = (((1,), (1,)), ((), ()))
    with_sum = states is not None
    tn_dims = (((0,), (0,)), ((), ()))

    def body(*refs):
        v_ref, win_ref, lr_ref, li_ref, wout_ref = refs[:5]
        n_in = 8 if with_sum else 5
        or_ref, oi_ref, y_ref = refs[n_in:n_in + 3]
        br_s, bi_s = refs[n_in + (9 if with_sum else 3):][:2]
        if with_sum:
            mb_s, mc_s = refs[-2:]
            mb_s[...] = jnp.zeros_like(mb_s)
            mc_s[...] = jnp.zeros_like(mc_s)
        l1 = (lr_ref[...], li_ref[...])
        pw = [l1]
        for _ in range(rows - 1):
            pw.append(_cmul(*pw[-1], *l1))
        row = lax.broadcasted_iota(jnp.int32, (rows, ln), 0)
        expo = (rows - row) if reverse else (row + 1)
        pr = jnp.zeros((rows, ln), F32)
        pi = jnp.zeros((rows, ln), F32)
        for e in range(1, rows + 1):
            pr = jnp.where(expo == e, pw[e - 1][0], pr)
            pi = jnp.where(expo == e, pw[e - 1][1], pi)
        lk = {}
        for k in (1, 2, 4):
            keep = (row < rows - k) if reverse else (row >= k)
            lk[k] = (jnp.where(keep, pw[k - 1][0], 0.0), jnp.where(keep, pw[k - 1][1], 0.0))

        def chunk(c, carry):
            q0 = pl.multiple_of(((nch - 1 - c) if reverse else c) * ch, ch)
            b = jnp.dot(_bf(v_ref[pl.ds(q0, ch), :]), win_ref[...], preferred_element_type=F32)
            br_s[...] = b[:, :ln]
            bi_s[...] = b[:, ln:]

            def step(i, carry):
                cr, ci = carry[:2]
                r0 = pl.multiple_of(((ntile - 1 - i) if reverse else i) * rows, rows)
                xr, xi = br_s[pl.ds(r0, rows), :], bi_s[pl.ds(r0, rows), :]
                for k in (1, 2, 4):
                    shift = rows - k if reverse else k
                    ar, ai = _cmul(lk[k][0], lk[k][1], pltpu.roll(xr, shift, 0), pltpu.roll(xi, shift, 0))
                    xr, xi = xr + ar, xi + ai
                ar, ai = _cmul(pr, pi, cr, ci)
                xr, xi = xr + ar, xi + ai
                g0 = pl.multiple_of(q0 + r0, rows)
                or_ref[pl.ds(g0, rows), :] = xr
                oi_ref[pl.ds(g0, rows), :] = xi
                if not with_sum:
                    return (xr[rows - 1:rows], xi[rows - 1:rows]) if not reverse else (xr[0:1], xi[0:1])
                nr = jnp.where(row == rows - 1, cr, pltpu.roll(xr, rows - 1, 0))
                ni = jnp.where(row == rows - 1, ci, pltpu.roll(xi, rows - 1, 0))
                sr, si = refs[5][pl.ds(g0, rows), :], refs[6][pl.ds(g0, rows), :]
                return xr[0:1], xi[0:1], carry[2] + (sr * nr + si * ni), carry[3] + (sr * ni - si * nr)

            carry = lax.fori_loop(0, ntile, step, carry)
            if with_sum:
                rows_c = pl.ds(q0, ch)
                uc, vc = _bf(refs[7][rows_c, :]), _bf(v_ref[rows_c, :])
                for scr, left, (right_re, right_im) in ((mb_s, uc, (or_ref, oi_ref)), (mc_s, vc, (refs[5], refs[6]))):
                    scr[:, :ln] += lax.dot_general(left, _bf(right_re[rows_c, :]), tn_dims, preferred_element_type=F32)
                    scr[:, ln:] += lax.dot_general(left, _bf(right_im[rows_c, :]), tn_dims, preferred_element_type=F32)
            w = wout_ref[...]
            y_ref[pl.ds(q0, ch), :] = (
                lax.dot_general(_bf(or_ref[pl.ds(q0, ch), :]), w[:, :ln], nt_dims, preferred_element_type=F32)
                + lax.dot_general(_bf(oi_ref[pl.ds(q0, ch), :]), w[:, ln:], nt_dims, preferred_element_type=F32))
            return carry

        zero = jnp.zeros((1, ln), F32)
        init = (zero, zero) + ((jnp.zeros((rows, ln), F32),) * 2 if with_sum else ())
        carry = lax.fori_loop(0, nch, chunk, init)
        if with_sum:
            refs[n_in + 3][...] = _colsum(carry[2])
            refs[n_in + 4][...] = _colsum(carry[3])
            row_g = lax.broadcasted_iota(jnp.int32, (SCAN_COLS, LANE), 0) // SSM_N
            lane_g = lax.broadcasted_iota(jnp.int32, (SCAN_COLS, LANE), 1) // SSM_P
            for scr, o_re, o_im in ((mb_s, refs[n_in + 5], refs[n_in + 6]), (mc_s, refs[n_in + 7], refs[n_in + 8])):
                for part, o_ref in enumerate((o_re, o_im)):
                    fold = jnp.zeros((SCAN_COLS, LANE), F32)
                    for cb in range(ln // LANE):
                        fold = fold + jnp.where(2 * cb + lane_g == row_g, scr[:, part * ln + cb * LANE:part * ln + (cb + 1) * LANE], 0.0)
                    o_ref[...] = jnp.where(row_g % 2 == 0, fold, pltpu.roll(fold, SSM_P, 1))

    vec = pl.BlockSpec((1, ln), lambda j: (0, j))
    blk = pl.BlockSpec((s, ln), lambda j: (0, j))
    cols = pl.BlockSpec((s, SCAN_COLS), lambda j: (0, j))
    wspec = pl.BlockSpec((None, SCAN_COLS, 2 * ln), lambda j: (j, 0, 0))
    ins, args = [cols, wspec, vec, vec, wspec], [v, w_in, lam_re, lam_im, w_out]
    outs, shapes = [blk, blk, cols], [_sds((s, SSM_L))] * 2 + [_sds((s, SSM_G * SSM_N))]
    scratch = [pltpu.VMEM((ch, ln), F32)] * 2
    if with_sum:
        own = pl.BlockSpec((None, SCAN_COLS, LANE), lambda j: (j, 0, 0))
        ins, args = ins + [blk, blk, cols], args + list(states) + [u]
        outs = outs + [vec, vec] + [own] * 4
        shapes = shapes + [_sds((1, SSM_L))] * 2 + [_sds((SSM_L // ln, SCAN_COLS, LANE))] * 4
        scratch = scratch + [pltpu.VMEM((SCAN_COLS, 2 * ln), F32)] * 2
    return _pc(body, name, (SSM_L // ln,), ins, outs, shapes, scratch=scratch)(*args)


def _ssm_act_fwd(y, u, d_skip):
    s = y.shape[0]
    ts = min(s, 512)

    def body(y_ref, u_ref, d_ref, o_ref):
        o_ref[...] = _gelu(y_ref[...] + d_ref[...] * u_ref[...]).astype(BF16)

    return _pc(body, "ssm_act_fwd", (s // ts,), [_row_spec(ts, D)] * 2 + [_vec_spec(D)], _row_spec(ts, D),
               _sds((s, D), BF16))(y, u, d_skip)


def _ssm_act_bwd(dg, y, u, d_skip):
    s = y.shape[0]
    ts = min(s, 512)

    def body(dg_ref, y_ref, u_ref, d_ref, dy_ref, dd_ref):
        uv = u_ref[...]
        dy = dg_ref[...] * _gelu_grad(y_ref[...] + d_ref[...] * uv)
        dy_ref[...] = dy.astype(BF16)
        _acc(dd_ref, pl.program_id(0) == 0, _colsum(dy * uv))

    return _pc(body, "ssm_act_bwd", (s // ts,), [_row_spec(ts, D)] * 3 + [_vec_spec(D)], [_row_spec(ts, D), _vec_spec(D)],
               [_sds((s, D), BF16), _sds((1, D))])(dg, y, u, d_skip)


def _axpy(a, b, d_skip):
    s = a.shape[0]
    ts = min(s, 512)

    def body(a_ref, b_ref, d_ref, o_ref):
        o_ref[...] = (a_ref[...] + d_ref[...] * b_ref[...].astype(F32)).astype(BF16)

    return _pc(body, "ssm_du", (s // ts,), [_row_spec(ts, D)] * 2 + [_vec_spec(D)], _row_spec(ts, D),
               _sds((s, D), BF16))(a, b, d_skip)


def _glu_fwd(zz):
    s = zz.shape[0]
    ts = min(s, 512)

    def body(a_ref, b_ref, o_ref):
        o_ref[...] = a_ref[...] * _sigmoid(b_ref[...])

    return _pc(body, "glu_fwd", (s // ts,), [_row_spec(ts, D, 0), _row_spec(ts, D, 1)], _row_spec(ts, D), _sds((s, D)))(zz, zz)


def _glu_bwd(zz, df):
    s = zz.shape[0]
    ts = min(s, 512)

    def body(a_ref, b_ref, df_ref, o_ref):
        sg = _sigmoid(b_ref[...])
        dfv = df_ref[...].astype(F32)
        o_ref[:, :D] = (dfv * sg).astype(BF16)
        o_ref[:, D:] = (dfv * a_ref[...] * sg * (1.0 - sg)).astype(BF16)

    return _pc(body, "glu_bwd", (s // ts,), [_row_spec(ts, D, 0), _row_spec(ts, D, 1), _row_spec(ts, D)],
               _row_spec(ts, 2 * D), _sds((s, 2 * D), BF16))(zz, zz, df)


def _ssm_block_diag(m_re, m_im):
    rows, half = SCAN_COLS, SCAN_LANES
    expand = jnp.tile(jnp.eye(SSM_P, dtype=BF16), (1, SCAN_GROUPS))

    def body(mr_ref, mi_ref, e_ref, o_ref):
        keep = (lax.broadcasted_iota(jnp.int32, (rows, half), 0) // SSM_N
                == lax.broadcasted_iota(jnp.int32, (rows, half), 1) // SSM_P)
        for part, m_ref in enumerate((mr_ref, mi_ref)):
            t = jnp.dot(_bf(m_ref[...]), e_ref[...], preferred_element_type=F32)
            o_ref[:, part * half:(part + 1) * half] = jnp.where(keep, t, 0.0).astype(BF16)

    blk = pl.BlockSpec((rows, SSM_P), lambda q: (q, 0))
    nb = SSM_G // SCAN_GROUPS
    return _pc(body, "ssm_block_diag", (nb,), [blk, blk, pl.BlockSpec((SSM_P, half), lambda q: (0, 0))],
               pl.BlockSpec((None, rows, 2 * half), lambda q: (q, 0, 0)), _sds((nb, rows, 2 * half), BF16))(m_re, m_im, expand)


def _mod_part(c_all, ada_w):
    n = ada_w.shape[-1]

    def body(c_ref, w_ref, o_ref):
        cv = c_ref[...]
        cond = _bf(cv * _sigmoid(cv))
        o_ref[...] = jnp.dot(cond, _bf(w_ref[...]), preferred_element_type=F32)

    return _pc(body, "mod_part", (2,), [pl.BlockSpec((N_DEV, D), lambda l: (0, 0)), pl.BlockSpec((None, D, n), lambda l: (l, 0, 0))],
               pl.BlockSpec((None, N_DEV, n), lambda l: (l, 0, 0)), _sds((2, N_DEV, n)))(c_all, ada_w)


def _ada_w_grad(c_all_t, dmod):
    nl, _, n = dmod.shape
    tr = 128

    def body(c_ref, d_ref, o_ref):
        cv = c_ref[...]
        cond = _bf(cv * _sigmoid(cv)).astype(F32)
        dm = _bf(d_ref[...]).astype(F32)
        acc = cond[:, 0:1] * dm[0:1, :]
        for b in range(1, N_DEV):
            acc = acc + cond[:, b:b + 1] * dm[b:b + 1, :]
        o_ref[...] = acc

    return _pc(body, "ada_w_grad", (nl, D // tr),
               [pl.BlockSpec((tr, N_DEV), lambda l, t: (t, 0)), pl.BlockSpec((None, N_DEV, n), lambda l, t: (l, 0, 0))],
               pl.BlockSpec((None, tr, n), lambda l, t: (l, t, 0)), _sds((nl, D, n)))(c_all_t, dmod)


def _adamw(name, parts, w, m, v, slot=0, prev=None, after=None):
    p, r, c = parts.shape
    tr = r
    while tr * c * 4 > (1 << 20) and tr % 16 == 0:
        tr //= 2
    nt = r // tr

    def body(p_ref, w_ref, m_ref, v_ref, *rest):
        g_ref, d_ref, nm_ref, nv_ref = rest[-4:]
        g = p_ref[0].astype(F32)
        for i in range(1, p):
            g = g + p_ref[i].astype(F32)
        g_ref[...] = g
        d_ref[...], nm_ref[...], nv_ref[...] = _adam_update(g, w_ref[...], m_ref[...], v_ref[...])

    blk = pl.BlockSpec((tr, c), lambda t: (slot * nt + t, 0))
    in_specs = [pl.BlockSpec((p, tr, c), lambda t: (0, t, 0)), blk, blk, blk]
    unread = list(prev or []) + ([after] if after is not None else [])
    return pl.pallas_call(
        body, name=name, grid=(nt,), in_specs=in_specs + [pl.BlockSpec(memory_space=pl.ANY)] * len(unread), out_specs=[blk] * 4,
        out_shape=[_sds(w.shape)] * 4, input_output_aliases={4 + i: i for i in range(4)} if prev else {},
        compiler_params=pltpu.CompilerParams(dimension_semantics=("arbitrary",), vmem_limit_bytes=VMEM_LIMIT_BYTES))(parts, w, m, v, *unread)


def _adam_update(g, w, m, v):
    m2 = B1 * m + (1.0 - B1) * g
    v2 = B2 * v + (1.0 - B2) * (g * g)
    m_hat = m2 / (1.0 - B1 ** STEP)
    v_hat = v2 / (1.0 - B2 ** STEP)
    return -LR * (m_hat / (jnp.sqrt(v_hat) + ADAM_EPS) + WD * w), m2, v2


def _adamw_many(name, items, after):
    n = len(items)

    def body(*refs):
        outs = refs[4 * n + 1:]
        for i in range(n):
            g, w, m, v = (r[...] for r in refs[4 * i:4 * i + 4])
            for o, val in zip(outs[3 * i:3 * i + 3], _adam_update(g, w, m, v)):
                o[...] = val

    full = lambda a: pl.BlockSpec(a.shape, lambda t: (0, 0))
    flat = [a for item in items for a in item]
    res = _pc(body, name, (1,), [full(a) for a in flat] + [pl.BlockSpec(memory_space=pl.ANY)],
              [full(item[1]) for item in items for _ in range(3)],
              [_sds(item[1].shape) for item in items for _ in range(3)])(*flat, after)
    return [tuple(res[3 * i:3 * i + 3]) for i in range(n)]


def _sum_parts(parts):
    p, r, c = parts.shape
    tr = r
    while tr * c * 4 > (1 << 19) and tr % 16 == 0:
        tr //= 2

    def body(p_ref, o_ref):
        g = p_ref[0]
        for i in range(1, p):
            g = g + p_ref[i]
        o_ref[...] = g

    return _pc(body, "sum_parts", (r // tr,), [pl.BlockSpec((p, tr, c), lambda t: (0, t, 0))], pl.BlockSpec((tr, c), lambda t: (t, 0)),
               _sds((r, c)))(parts)


def _place():
    x, y, c = lax.axis_index("x"), lax.axis_index("y"), lax.axis_index("c")
    peers = []
    for k in range(1, N_DEV):
        px = (1 - x) if k & 4 else x
        py = (1 - y) if k & 2 else y
        pc = (1 - c) if k & 1 else c
        peers.append(((px, py, pc), 4 * px + 2 * py + pc))
    return 4 * x + 2 * y + c, peers


def _at(ref, idx):
    return ref if idx is None else ref.at[idx]


def _exchange_copies(plan, n, src_refs, dst_refs, send_sems, recv_sems, local_sems=None, with_arrivals=True):
    me, peers = _place()
    local = [] if local_sems is None else [
        pltpu.make_async_copy(_at(src_refs[si], sx), _at(dst_refs[di], dx), local_sems.at[i])
        for i, (si, sx, di, dx) in enumerate(plan(me, me, 0))]

    def remote(k, i, dev, entry):
        si, sx, di, dx = entry
        return pltpu.make_async_remote_copy(_at(src_refs[si], sx), _at(dst_refs[di], dx), send_sems.at[k * n + i], recv_sems.at[k * n + i],
                                            device_id=dev, device_id_type=MESH)

    sends = [remote(k, i, dev, e) for k, (dev, peer) in enumerate(peers) for i, e in enumerate(plan(me, peer, k + 1))]
    if not with_arrivals:
        return local, sends, []
    arrivals = [remote(k, i, dev, e) for k, (dev, peer) in enumerate(peers) for i, e in enumerate(plan(peer, me, k + 1))]
    return local, sends, arrivals


def _sem_shapes(n_copies, local=True):
    sems = [pltpu.SemaphoreType.DMA(((N_DEV - 1) * n_copies,)), pltpu.SemaphoreType.DMA(((N_DEV - 1) * n_copies,))]
    return sems + [pltpu.SemaphoreType.DMA((n_copies,))] if local else sems


def _exchange(name, srcs, dst_shapes, plan, n_copies):
    ns, nd = len(srcs), len(dst_shapes)

    def body(*refs):
        local, sends, arrivals = _exchange_copies(plan, n_copies, refs[:ns], refs[ns:ns + nd], *refs[ns + nd:])
        for cp in local + sends:
            cp.start()
        for cp in arrivals:
            cp.wait_recv()
        for cp in sends:
            cp.wait_send()
        for cp in local:
            cp.wait()

    any_spec = pl.BlockSpec(memory_space=pl.ANY)
    return pl.pallas_call(
        body, name=name, in_specs=[any_spec] * ns, out_specs=[any_spec] * nd, out_shape=list(dst_shapes),
        scratch_shapes=_sem_shapes(n_copies))(*srcs)


HBM_SPEC = pl.BlockSpec(memory_space=pltpu.HBM)
SEM_SPEC = pl.BlockSpec(memory_space=pltpu.SEMAPHORE)
ANY_SPEC = pl.BlockSpec(memory_space=pl.ANY)
TOKEN_SPEC = pl.BlockSpec(memory_space=pltpu.VMEM)
SIDE_EFFECT = pltpu.SideEffectType.DATAFLOW_SIDE_EFFECTING


def _wait_all(local, sends, arrivals):
    for cp in arrivals:
        cp.wait_recv()
    for cp in sends:
        cp.wait_send()
    for cp in local:
        cp.wait()


def _exchange_start(name, srcs, dst_shapes, plan, n_copies, order):
    ns, nd = len(srcs), len(dst_shapes)
    nb = ns + nd

    def body(*refs):
        local, sends, _ = _exchange_copies(plan, n_copies, refs[:ns], refs[ns:nb], *refs[nb + 1:nb + 4], with_arrivals=False)
        for cp in local + sends:
            cp.start()
        refs[-1][...] = jnp.zeros((8, LANE), F32)

    lands = [pltpu.with_memory_space_constraint(lax.empty(d.shape, d.dtype), pltpu.HBM) for d in dst_shapes]
    srcs = [pltpu.with_memory_space_constraint(a, pltpu.HBM) for a in srcs]
    bufs = srcs + lands
    out = pl.pallas_call(
        body, name=name, in_specs=[HBM_SPEC] * nb + [ANY_SPEC],
        out_specs=[SEM_SPEC] * 3 + [HBM_SPEC] * nb + [TOKEN_SPEC],
        out_shape=_sem_shapes(n_copies) + [pltpu.HBM(a.shape, a.dtype) for a in bufs] + [_sds((8, LANE))],
        input_output_aliases={i: 3 + i for i in range(nb)},
        compiler_params=pltpu.CompilerParams(has_side_effects=SIDE_EFFECT))(*bufs, order)
    return out[:3], out[3:3 + ns], out[3 + ns:3 + nb], out[-1]


def _exchange_relay(name, sems, srcs, lands, plan, n_copies, plan2, n_copies2, after):
    ns, nd = len(srcs), len(lands)
    nb = ns + nd

    def body(*refs):
        land_refs = refs[ns:nb]
        _wait_all(*_exchange_copies(plan, n_copies, refs[:ns], land_refs, *refs[nb:nb + 3]))
        _, sends, _ = _exchange_copies(plan2, n_copies2, land_refs, land_refs, *refs[nb + 4:nb + 6], with_arrivals=False)
        for cp in sends:
            cp.start()
        refs[-1][...] = jnp.zeros((8, LANE), F32)

    out = pl.pallas_call(
        body, name=name, in_specs=[HBM_SPEC] * nb + [SEM_SPEC] * 3 + [ANY_SPEC],
        out_specs=[SEM_SPEC] * 2 + [HBM_SPEC] * nd + [TOKEN_SPEC],
        out_shape=_sem_shapes(n_copies2, local=False) + [pltpu.HBM(a.shape, a.dtype) for a in lands] + [_sds((8, LANE))],
        input_output_aliases={ns + i: 2 + i for i in range(nd)},
        compiler_params=pltpu.CompilerParams(has_side_effects=SIDE_EFFECT))(*srcs, *lands, *sems, after)
    return out[:2], out[2:2 + nd], out[-1]


def _exchange_wait(name, sems, srcs, lands, plan, n_copies, after):
    srcs = [] if srcs is None else list(srcs)
    ns, nd = len(srcs), len(lands)
    nb = ns + nd

    def body(*refs):
        land_refs = refs[ns:nb]
        _wait_all(*_exchange_copies(plan, n_copies, refs[:ns] if ns else land_refs, land_refs, *refs[nb:nb + len(sems)]))

    bufs = srcs + list(lands)
    out = pl.pallas_call(
        body, name=name, in_specs=[HBM_SPEC] * nb + [SEM_SPEC] * len(sems) + [ANY_SPEC],
        out_specs=[HBM_SPEC] * nb, out_shape=[pltpu.HBM(a.shape, a.dtype) for a in bufs],
        input_output_aliases={i: i for i in range(nb)},
        compiler_params=pltpu.CompilerParams(has_side_effects=SIDE_EFFECT))(*bufs, *sems, after)
    return out[ns:]


def _all_gather(name, arrs):
    plan = lambda me, peer, k: [(i, None, i, me) for i in range(len(arrs))]
    return _exchange(name, arrs, [_sds((N_DEV,) + a.shape, a.dtype) for a in arrs], plan, len(arrs))


def _mix0_fwd(h, p):
    z = _mm_nt("mix0_in", h, p["ab_w_in"])
    y_a, d = _pool_fwd(z, p["pool_w"], p["pool_scale"])
    y_b = _sgu_fwd(z, p["sgu_ln_g"], p["sgu_ln_b"], p["sgu_w"], p["sgu_bt"])
    ycat = jnp.concatenate([y_a, y_b], axis=1)
    return _mm_nn("mix0_out", ycat, p["ab_w_out"]), (h, z, d, ycat)


def _mix0_bwd(df, saved, p, after):
    h, z, d, ycat = saved
    dycat = _mm_nt("mix0_out_dx", df, p["ab_w_out"], after=after)
    g = {"ab_w_out": _mm_tn("mix0_out_dw", ycat, df, BF16)}
    dz_p, g["pool_w"], g["pool_scale"] = _pool_bwd(dycat, d, p["pool_w"], p["pool_scale"])
    dz_u, dz_v, g["sgu_ln_g"], g["sgu_ln_b"], g["sgu_w"], dbt = _sgu_bwd(
        z, dycat, p["sgu_ln_g"], p["sgu_ln_b"], p["sgu_w"], p["sgu_bt"], p["head_sum"])
    g["sgu_b"] = dbt[:, :NH].T
    dz = jnp.concatenate([dz_p, dz_u, dz_v], axis=1)
    g["ab_w_in"] = _mm_tn("mix0_in_dw", dz, h, BF16)
    return _mm_nn("mix0_in_dx", dz, p["ab_w_in"]), g


def _mix1_fwd(h, p):
    u = _mm_nn("ssm_w_in", h, p["ssm_w_in"])
    x_re, x_im, y = _ssm_scan("ssm_scan_fwd", u, p["wb_bd"], p["lam_bar_re"], p["lam_bar_im"], p["wc_bd"], False)
    g = _ssm_act_fwd(y, u, p["ssm_d"])
    zz = _mm_nn("ssm_glu", g, p["ssm_w_glu"])
    return _glu_fwd(zz), (h, u, x_re, x_im, y, g, zz)


def _mix1_bwd(df, saved, p, after):
    h, u, x_re, x_im, y, g, zz = saved
    gr = {}
    dzz = _glu_bwd(zz, df)
    dg = _mm_nt("ssm_glu_dx", dzz, p["ssm_w_glu"], after=after)
    gr["ssm_w_glu"] = _mm_tn("ssm_glu_dw", g, dzz, BF16)
    dy, gr["ssm_d"] = _ssm_act_bwd(dg, y, u, p["ssm_d"])
    _, _, du_ssm, g_lam_re, g_lam_im, mb_re, mb_im, mc_re, mc_im = _ssm_scan(
        "ssm_scan_bwd", dy, p["wc_bd"], p["lam_bar_re"], -p["lam_bar_im"], p["wb_bd"], True, states=(x_re, x_im), u=u)
    du = _axpy(du_ssm, dy, p["ssm_d"])
    gr["ssm_w_in"] = _mm_tn("ssm_w_in_dw", h, du, BF16)
    dh = _mm_nt("ssm_w_in_dx", du, p["ssm_w_in"])
    per_group = lambda m: m[:, :, :SSM_P].reshape(SSM_G, SSM_N, SSM_P)
    gr["ssm_c_re"] = per_group(mc_re)
    gr["ssm_c_im"] = -per_group(mc_im)
    dlr, dli, ddt, dbr, dbi = _ssm_param_bwd(
        g_lam_re.reshape(SSM_G, SSM_P), g_lam_im.reshape(SSM_G, SSM_P),
        per_group(mb_re).reshape(SSM_G, SSM_N * SSM_P), per_group(mb_im).reshape(SSM_G, SSM_N * SSM_P),
        p["lam_re"], p["lam_im"], p["lam_re_rep"], p["lam_im_rep"], p["log_dt"], p["b_re"], p["b_im"], p["seg"])
    gr["ssm_lam_re"], gr["ssm_lam_im"], gr["ssm_log_dt"] = dlr, dli, ddt[:, 0]
    gr["ssm_b_re"] = dbr.reshape(SSM_G, SSM_N, SSM_P)
    gr["ssm_b_im"] = dbi.reshape(SSM_G, SSM_N, SSM_P)
    return dh, gr


def _ssm_params(lam_re, lam_im, b_re, b_im, c_re, c_im, log_dt):
    wide = lambda b: b.transpose(0, 2, 1).reshape(SSM_G, SSM_N * SSM_P)
    p = {"lam_re": lam_re, "lam_im": lam_im, "log_dt": log_dt.reshape(SSM_G, 1),
         "lam_re_rep": jnp.tile(lam_re, (1, SSM_N)), "lam_im_rep": jnp.tile(lam_im, (1, SSM_N)), "b_re": wide(b_re), "b_im": wide(b_im)}
    lbr, lbi, bbr, bbi = _ssm_prep(lam_re, lam_im, p["lam_re_rep"], p["lam_im_rep"], p["log_dt"], p["b_re"], p["b_im"])
    p["lam_bar_re"], p["lam_bar_im"] = lbr.reshape(1, SSM_L), lbi.reshape(1, SSM_L)
    rows = lambda m: m.reshape(SSM_G * SSM_N, SSM_P)
    p["wb_bd"] = _ssm_block_diag(rows(bbr), rows(bbi))
    p["wc_bd"] = _ssm_block_diag(rows(c_re), rows(-c_im))
    p["seg"] = jnp.tile(jnp.eye(SSM_P, dtype=F32), (SSM_N, 1))
    return p


RES_WEIGHT = (0.5, 1.0, 0.5)


def _local_step(x, tgt, vecs, weights_of, on_part, on_grads):
    def fns(i, w):
        if i % 3 != 1:
            win, wout_of = w
            return ((lambda h: _ffn_fwd(h, win, wout_of)),
                    (lambda df, sv, after: (_ffn_bwd(df, sv, win, wout_of(None), lambda tag, part: on_part(i, tag, part), after), None)))
        if i == 1:
            return (lambda h: _mix0_fwd(h, w)), (lambda df, sv, after: _mix0_bwd(df, sv, w, after))
        return (lambda h: _mix1_fwd(h, w)), (lambda df, sv, after: _mix1_bwd(df, sv, w, after))

    rw = RES_WEIGHT * 2
    saved, bwd = [], []
    f = None
    for i in range(6):
        w, token = weights_of(i, x if i == 0 else f)
        fwd, b = fns(i, w)
        if i == 0:
            h = _prenorm_fwd(x, vecs, 0, token)
        else:
            x, h = _post_pre_fwd(x, f, vecs, i, rw[i - 1], token)
        f, inner = fwd(h)
        saved.append((x, f, inner))
        bwd.append(b)
    loss_row, dx = _loss_fwd_bwd(_postnorm_fwd(x, f, vecs, 5, rw[5]), tgt)
    df, dv_top = _postnorm_bwd(dx, f, vecs, 5, rw[5])
    token = jnp.zeros((8, LANE), F32)
    for i in reversed(range(6)):
        x_i, _, inner = saved[i]
        dh, extra = bwd[i](df, inner, token)
        if i > 0:
            dx, df, dv = _pre_post_bwd(dx, dh, x_i, saved[i - 1][1], vecs, i, rw[i - 1])
        else:
            dx, dv = _prenorm_bwd(dx, dh, x_i, vecs, 0)
        token = on_grads(i, extra, dv, dv_top if i == 5 else None, loss_row)
    return dx


def _pad_rows(v, rows):
    return jnp.pad(v, (0, rows * LANE - v.shape[0])).reshape(rows, LANE)


def _pack(parts):
    flat, layout, off = [], [], 0
    for a in parts:
        n = a.size
        padded = -(-n // LANE) * LANE
        flat.append(jnp.pad(a.reshape(-1).astype(F32), (0, padded - n)))
        layout.append((off, n, a.shape))
        off += padded
    return jnp.concatenate(flat), layout


def _unpack(flat, layout):
    return [flat[off:off + n].reshape(shape) for off, n, shape in layout]


TRANSPOSED = ["ffn_w_in", "ab_w_in", "ssm_b_re", "ssm_b_im"]
WEIGHTS = ['ada_w', 'ada_b', 'norm_pre', 'norm_post', 'ffn_w_in', 'ffn_w_out', 'ab_w_in', 'pool_w', 'pool_scale', 'sgu_ln_g',
           'sgu_ln_b', 'sgu_w', 'sgu_b', 'ab_w_out', 'ssm_w_in', 'ssm_lam_re', 'ssm_lam_im', 'ssm_b_re', 'ssm_b_im', 'ssm_c_re',
           'ssm_c_im', 'ssm_d', 'ssm_log_dt', 'ssm_w_glu']


def kernel(x, c, ada_w, ada_b, norm_pre, norm_post, ffn_w_in, ffn_w_out, ab_w_in, pool_w, pool_scale, sgu_ln_g, sgu_ln_b, sgu_w, sgu_b, ab_w_out, ssm_w_in, ssm_lam_re, ssm_lam_im, ssm_b_re, ssm_b_im, ssm_c_re, ssm_c_im, ssm_d, ssm_log_dt, ssm_w_glu, loss_target, m_ada_w, m_ada_b, m_norm_pre, m_norm_post, m_ffn_w_in, m_ffn_w_out, m_ab_w_in, m_pool_w, m_pool_scale, m_sgu_ln_g, m_sgu_ln_b, m_sgu_w, m_sgu_b, m_ab_w_out, m_ssm_w_in, m_ssm_lam_re, m_ssm_lam_im, m_ssm_b_re, m_ssm_b_im, m_ssm_c_re, m_ssm_c_im, m_ssm_d, m_ssm_log_dt, m_ssm_w_glu, v_ada_w, v_ada_b, v_norm_pre, v_norm_post, v_ffn_w_in, v_ffn_w_out, v_ab_w_in, v_pool_w, v_pool_scale, v_sgu_ln_g, v_sgu_ln_b, v_sgu_w, v_sgu_b, v_ab_w_out, v_ssm_w_in, v_ssm_lam_re, v_ssm_lam_im, v_ssm_b_re, v_ssm_b_im, v_ssm_c_re, v_ssm_c_im, v_ssm_d, v_ssm_log_dt, v_ssm_w_glu):
    args = locals()
    wts = {n: args[n] for n in WEIGHTS}
    mom = {n: args["m_" + n] for n in WEIGHTS}
    var = {n: args["v_" + n] for n in WEIGHTS}
    for n in TRANSPOSED:
        for t in (wts, mom, var):
            t[n] = jnp.swapaxes(t[n], -1, -2)
    me = 4 * lax.axis_index("x") + 2 * lax.axis_index("y") + lax.axis_index("c")
    s = x.shape[1]
    nd = D // N_DEV

    small_in, small_in_layout = _pack([c, norm_pre, norm_post, ssm_d])
    small_rows = -(-small_in.shape[0] // (8 * LANE)) * 8
    (g_small,) = _all_gather("gather_small", [_pad_rows(small_in, small_rows)])
    g_small = g_small.reshape(N_DEV, -1)
    c_all, npre_g, npost_g, sd_g = [jnp.stack([_unpack(g_small[j], small_in_layout)[i] for j in range(N_DEV)]) for i in range(4)]
    c_all = c_all.reshape(N_DEV, D)
    norm_pre_full = npre_g.transpose(1, 2, 0, 3).reshape(2, 3, D)
    norm_post_full = npost_g.transpose(1, 2, 0, 3).reshape(2, 3, D)
    ssm_d_full = sd_g.transpose(1, 0, 2).reshape(1, D)

    nw = ada_w.shape[-1]
    (mod_g,) = _all_gather("gather_mod", [_mod_part(c_all, ada_w)])
    mod = lax.dynamic_index_in_dim(mod_g, me, axis=2, keepdims=False)
    mod = (mod.transpose(1, 0, 2).reshape(2, N_DEV * nw) + ada_b).reshape(2, 3, 3, D)

    w_in_t = wts["ffn_w_in"]
    shards = [[w_in_t[0, 0]], [ffn_w_out[0, 0]], [wts["ab_w_in"][0], ab_w_out[0]], [w_in_t[0, 1]], [ffn_w_out[0, 1]],
              [w_in_t[1, 0]], [ffn_w_out[1, 0]], [ssm_w_in[0], ssm_w_glu[0]], [w_in_t[1, 1]], [ffn_w_out[1, 1]]]
    first_group = {0: 0, 1: 2, 2: 3, 3: 5, 4: 7, 5: 8}
    first_groups = set(first_group.values())
    same_core = (2, 4, 6)

    def gather_plan(n):
        return lambda me_, peer_, k: [(a, None, a, me_) for a in range(n)] if k in (0, 1) + same_core else []

    def relay_plan(n):
        return lambda me_, peer_, k: [(a, me_ ^ kk, a, me_ ^ kk) for kk in same_core for a in range(n)] if k == 1 else []

    gathers, relays = [], {}
    token = mod_g
    for g, group in enumerate(shards):
        group = [a.astype(BF16) for a in group]
        sems, srcs_thru, lands, token = _exchange_start(
            f"gather_start_{g}", group, [_sds((N_DEV,) + a.shape, BF16) for a in group], gather_plan(len(group)), len(group), token)
        gathers.append((sems, srcs_thru, lands))
    mod6 = mod.reshape(6, 3, D)
    vecs = jnp.stack([norm_pre_full.reshape(6, D), mod6[:, 1], mod6[:, 0], norm_post_full.reshape(6, D), mod6[:, 2]]
                     + [jnp.zeros((6, D), F32)] * 3, axis=1)
    vecs = vecs + token[0, 0]

    def relay(g, after):
        sems, srcs_thru, lands = gathers[g]
        n = len(lands)
        relays[g] = _exchange_relay(f"gather_relay_{g}", sems, srcs_thru, lands, gather_plan(n), n, relay_plan(n), 3 * n, after)

    def fetch(g, after):
        if g not in relays:
            relay(g, after)
        sems, lands, token = relays[g]
        n = len(lands)
        got = _exchange_wait(f"gather_wait_{g}", sems, None, lands, relay_plan(n), 3 * n, after)
        if g + 1 in first_groups:
            relay(g + 1, got[0])
            token = relays[g + 1][2]
        return got, token

    head_sum = jnp.repeat(jnp.eye(NH, LANE, dtype=F32), HD, axis=0)
    mix0 = {"pool_w": pool_w[0], "pool_scale": pool_scale, "sgu_ln_g": sgu_ln_g, "sgu_ln_b": sgu_ln_b, "sgu_w": sgu_w[0],
            "sgu_bt": jnp.pad(sgu_b[0].T, ((0, 0), (0, LANE - NH))), "head_sum": head_sum}
    mix1 = _ssm_params(ssm_lam_re[0], ssm_lam_im[0], ssm_b_re[0], ssm_b_im[0], ssm_c_re[0], ssm_c_im[0], ssm_log_dt[0])
    mix1["ssm_d"] = ssm_d_full

    def weights_of(i, x_in):
        g = first_group[i]
        if i % 3 != 1:
            (win,), token = fetch(g, x_in)
            cache = []

            def wout_of(act):
                if not cache:
                    cache.append(fetch(g + 1, act)[0][0])
                return cache[0]

            return (win, wout_of), token
        (a, b), token = fetch(g, x_in)
        if i == 1:
            return dict(mix0, ab_w_in=a.reshape(-1, D), ab_w_out=b.reshape(D, D)), token
        return dict(mix1, ssm_w_in=a.reshape(D, D), ssm_w_glu=b.transpose(1, 0, 2).reshape(D, -1)), token

    def shard_cols(a):
        r = a.shape[0]
        return a.reshape(r, N_DEV, -1).transpose(1, 0, 2)

    scatter_plan = lambda me_, peer_, k: [(0, peer_, 0, me_), (1, peer_, 1, me_)]
    scatter_plan1 = lambda me_, peer_, k: [(0, peer_, 0, me_)]
    scatters = []
    last_token = [jnp.zeros((8, LANE), F32)]
    pieces, mixer, bundles = {}, {}, {}
    bundle_plan = lambda me_, peer_, k: [(0, None, 0, me_)]

    held = {}

    def on_part(i, tag, part):
        if i != 0 and tag == "w_out":
            held[i] = part
            return last_token[0]
        names, parts, plan = (("ffn_" + tag,), [part], scatter_plan1) if i == 0 else (("ffn_w_out", "ffn_w_in"), [held[i], part], scatter_plan)
        sems, srcs_thru, lands, last_token[0] = _exchange_start(
            f"scatter_start_{i}_{tag}", parts, [_sds(a.shape, BF16) for a in parts], plan, len(parts), last_token[0])
        scatters.append((i, names, plan, sems, srcs_thru, lands))
        return last_token[0]
    mix0_names = ["pool_w", "pool_scale", "sgu_ln_g", "sgu_ln_b", "sgu_w", "sgu_b"]
    mix1_names = ["ssm_lam_re", "ssm_lam_im", "ssm_b_re", "ssm_b_im", "ssm_c_re", "ssm_c_im", "ssm_log_dt", "ssm_d"]

    def start_bundle(tag, arrays):
        flat, layout = _pack(arrays)
        rows = -(-flat.shape[0] // (8 * LANE)) * 8
        plan = gather_plan(1) if tag == "a" else bundle_plan
        sems, srcs_thru, lands, last_token[0] = _exchange_start(
            f"small_start_{tag}", [_pad_rows(flat, rows)], [_sds((N_DEV, rows, LANE))], plan, 1, last_token[0])
        bundles[tag] = (sems, srcs_thru, lands, layout)

    def on_grads(i, extra, dv, dv_top, loss_row):
        pieces[i] = dv
        if i == 5:
            pieces["top"] = dv_top
        if i == 4:
            mixer.update({n: extra[n] for n in mix1_names})
        if i == 1:
            mixer.update({n: extra[n] for n in mix0_names})
            start_bundle("a", [jnp.stack([pieces[j] for j in ("top", 5, 4, 3, 2, 1)])] + [mixer[n] for n in mix0_names + mix1_names])
        if i == 0:
            start_bundle("b", [dv, loss_row])
        if i % 3 != 1:
            return last_token[0]
        if i == 1:
            names, parts = ("ab_w_in", "ab_w_out"), [extra["ab_w_in"].reshape(N_DEV, -1, D), extra["ab_w_out"].reshape(N_DEV, nd, D)]
        else:
            names, parts = ("ssm_w_in", "ssm_w_glu"), [extra["ssm_w_in"].reshape(N_DEV, nd, D), shard_cols(extra["ssm_w_glu"])]
        sems, srcs_thru, lands, last_token[0] = _exchange_start(
            f"scatter_start_{i}", parts, [_sds(a.shape, BF16) for a in parts], scatter_plan, 2, last_token[0])
        scatters.append((i, names, scatter_plan, sems, srcs_thru, lands))
        return last_token[0]

    grad_x = _local_step(x[0], loss_target[0], vecs, weights_of, on_part, on_grads)

    out_g, out_d, out_m, out_v = {}, {}, {}, {}
    big_out = {}

    def adam_big(name, recv, n, slot=0, after=None):
        c_ = wts[n].shape[-1]
        big_out[n] = _adamw(name, recv.reshape(recv.shape[0], -1, c_), *[t[n].reshape(-1, c_) for t in (wts, mom, var)],
                            slot=slot, prev=big_out.get(n), after=after)
        return big_out[n][0]

    ffn_slot = {0: 0, 2: 1, 3: 2, 5: 3}

    def land_and_update(entries, after):
        for i, names, plan, sems, srcs_thru, lands in entries:
            recv = _exchange_wait(f"scatter_wait_{i}_{names[0]}", sems, srcs_thru, lands, plan, len(names), after)
            for n, r in zip(names, recv):
                after = adam_big(f"adamw_{n}_{i}", r, n, ffn_slot.get(i, 0), after)
        return after

    after = land_and_update([e for e in scatters if e[0] != 0], last_token[0])

    def landed(tag, g_parts):
        layout = bundles[tag][3]
        off, n, shape = layout[0]
        dmods = g_parts.reshape(N_DEV, -1)[:, off:off + n].reshape((N_DEV,) + shape)
        total = _sum_parts(g_parts)
        return dmods, _unpack(total.reshape(-1), layout), total

    def adam_small(n, g, after=None):
        cols = wts[n].shape[-1]
        res = _adamw(f"adamw_{n}", g.reshape(1, -1, cols), *[t[n].reshape(-1, cols) for t in (wts, mom, var)], after=after)
        for o, arr in zip((out_g, out_d, out_m, out_v), res):
            o[n] = arr.reshape(wts[n].shape)
            if n in TRANSPOSED:
                o[n] = jnp.swapaxes(o[n], -1, -2)
        return res[0]

    sems, srcs_thru, lands, _ = bundles["a"]
    sems, lands, _ = _exchange_relay("small_relay_a", sems, srcs_thru, lands, gather_plan(1), 1, relay_plan(1), 3, after)
    (parts_a,) = _exchange_wait("small_wait_a", sems, None, lands, relay_plan(1), 3, after)
    shells_a, sums_a, after = landed("a", parts_a)
    small = dict(zip(mix0_names + mix1_names, sums_a[1:]))
    def adam_tiny(name, grads, after):
        view = lambda n, a: a.reshape(-1, wts[n].shape[-1])
        items = [(view(n, g),) + tuple(view(n, t[n]) for t in (wts, mom, var)) for n, g in grads.items()]
        for (n, _), item, res in zip(grads.items(), items, _adamw_many(name, items, after)):
            for o, arr in zip((out_g, out_d, out_m, out_v), (item[0],) + res):
                o[n] = arr.reshape(wts[n].shape)
        return res[0]

    tiny = ["pool_scale", "sgu_ln_g", "sgu_ln_b", "sgu_b", "ssm_lam_re", "ssm_lam_im", "ssm_log_dt"]
    for n in [n for n in mix0_names + mix1_names if n not in tiny and n != "ssm_d"]:
        after = adam_small(n, small[n], after)
    after = adam_tiny("adamw_tiny_mixers", dict({n: small[n] for n in tiny},
                                                ssm_d=lax.dynamic_slice_in_dim(small["ssm_d"], me * nd, nd, axis=1)), after)
    def shell_grads(top, blocks, first):
        own_rows = jnp.concatenate([first[..., None, :, :], blocks[..., :0:-1, :, :]], axis=-3)
        next_rows = jnp.concatenate([own_rows[..., 1:, :, :], top[..., None, :, :]], axis=-3)
        dmod_ = jnp.stack([own_rows[..., V_SHIFT, :], own_rows[..., V_SCALE, :], next_rows[..., V_GATE, :]], axis=-2)
        return dmod_, own_rows[..., V_GPRE, :], next_rows[..., V_GPOST, :]

    def ada_w_layer(l, dmod_l, after):
        mine = lax.dynamic_index_in_dim(dmod_l.reshape(N_DEV, N_DEV, nw), me, axis=1, keepdims=False)
        return adam_big(f"adamw_ada_w_{l}", _ada_w_grad(c_all.T, mine[None]), "ada_w", l, after)

    after = ada_w_layer(1, shell_grads(shells_a[:, 0], shells_a, jnp.zeros_like(shells_a[:, 0]))[0][:, 3:], after)
    sems, srcs_thru, lands, _ = bundles["b"]
    (parts_b,) = _exchange_wait("small_wait_b", sems, srcs_thru, lands, bundle_plan, 1, after)
    shell_b, (first_sum, loss_sum), after = landed("b", parts_b)
    loss = loss_sum[0, 0]

    dmod_sum, dg_pre_sum, dg_post_sum = shell_grads(sums_a[0][0], sums_a[0], first_sum)
    own = lambda a: lax.dynamic_slice_in_dim(a, me * nd, nd, axis=1)
    after = adam_tiny("adamw_tiny_shell", {"ada_b": dmod_sum, "norm_pre": own(dg_pre_sum), "norm_post": own(dg_post_sum)}, after)

    after = ada_w_layer(0, shell_grads(shells_a[:, 0], shells_a, shell_b)[0][:, :3], after)

    land_and_update([e for e in scatters if e[0] == 0], after)
    for n, res in big_out.items():
        for o, arr in zip((out_g, out_d, out_m, out_v), res):
            o[n] = arr.reshape(wts[n].shape)
            if n in TRANSPOSED:
                o[n] = jnp.swapaxes(o[n], -1, -2)

    return (loss, grad_x[None], *[out_g[n] for n in WEIGHTS], *[out_d[n] for n in WEIGHTS],
            *[out_m[n] for n in WEIGHTS], *[out_v[n] for n in WEIGHTS])
```

```python
import math

import jax
import jax.numpy as jnp
from jax import lax
from jax.experimental import pallas as pl
from jax.experimental.pallas import tpu as pltpu

F32 = jnp.float32
BF16 = jnp.bfloat16
MESH = pl.DeviceIdType.MESH
HIGHEST = lax.Precision.HIGHEST

N_DEV = 8
D = 1024
D_FF = 2816
FSH = 2 * D_FF // N_DEV
EPS = 1e-6
POOL_WINDOWS = (2, 4, 8, 16)
HD = 128
NH = 4
SSM_G, SSM_P, SSM_N = 64, 64, 16
SSM_L = SSM_G * SSM_P
LR, B1, B2, ADAM_EPS, WD, STEP = 0.001, 0.9, 0.999, 1e-08, 0.01, 10
GELU_C = math.sqrt(2.0 / math.pi)
VMEM_LIMIT_BYTES = 48 * 1024 * 1024
LANE = 128


def _pc(body, name, grid, in_specs, out_specs, out_shape, scratch=()):
    return pl.pallas_call(
        body, name=name, grid=grid, in_specs=in_specs, out_specs=out_specs, out_shape=out_shape,
        scratch_shapes=list(scratch),
        compiler_params=pltpu.CompilerParams(dimension_semantics=("arbitrary",) * len(grid),
                                             vmem_limit_bytes=VMEM_LIMIT_BYTES))


def _sds(shape, dtype=F32):
    return jax.ShapeDtypeStruct(tuple(shape), dtype)


def _bf(v):
    return v if v.dtype == BF16 else v.astype(BF16)


def _row_spec(ts, width, col=0):
    return pl.BlockSpec((ts, width), lambda t, _c=col: (t, _c))


def _vec_spec(width, col=0):
    return pl.BlockSpec((1, width), lambda t, _c=col: (0, _c))


def _mm(name, a, b, contract, grid, a_spec, b_spec, o_spec, out_shape, acc_axis=None, after=None):
    dn = (contract, ((), ()))

    def body(a_ref, b_ref, *rest):
        o_ref = rest[-1]
        r = lax.dot_general(_bf(a_ref[...]), _bf(b_ref[...]), dn, preferred_element_type=F32)
        if acc_axis is None:
            o_ref[...] = r.astype(o_ref.dtype)
        else:
            k = pl.program_id(acc_axis)

            @pl.when(k == 0)
            def _():
                o_ref[...] = r

            @pl.when(k > 0)
            def _():
                o_ref[...] += r

    if after is None:
        return _pc(body, name, grid, [a_spec, b_spec], o_spec, out_shape)(a, b)
    return _pc(body, name, grid, [a_spec, b_spec, pl.BlockSpec(memory_space=pl.ANY)], o_spec, out_shape)(a, b, after)


def _mm_sum(name, a, b, ts, after=None):
    nj, s, k = a.shape
    n = b.shape[2]

    def body(a_ref, b_ref, *rest):
        acc = jnp.dot(a_ref[0], b_ref[0], preferred_element_type=F32)
        for j in range(1, nj):
            acc = acc + jnp.dot(a_ref[j], b_ref[j], preferred_element_type=F32)
        rest[-1][...] = acc

    specs = [pl.BlockSpec((nj, ts, k), lambda t: (0, t, 0)), pl.BlockSpec((nj, k, n), lambda t: (0, 0, 0))]
    args = (a, b)
    if after is not None:
        specs, args = specs + [pl.BlockSpec(memory_space=pl.ANY)], args + (after,)
    return _pc(body, name, (s // ts,), specs, pl.BlockSpec((ts, n), lambda t: (t, 0)), _sds((s, n)))(*args)


def _tile(s):
    return min(s, 1024)


def _div_tile(n, cap=1024):
    t = min(n, cap) // LANE * LANE
    while n % t:
        t -= LANE
    return t


def _mm_nn(name, a, b, out_dtype=F32):
    s, k = a.shape
    n = b.shape[1]
    ts, tn = _tile(s), _div_tile(n)
    return _mm(name, a, b, ((1,), (0,)), (n // tn, s // ts),
               pl.BlockSpec((ts, k), lambda j, t: (t, 0)), pl.BlockSpec((k, tn), lambda j, t: (0, j)),
               pl.BlockSpec((ts, tn), lambda j, t: (t, j)), _sds((s, n), out_dtype))


def _mm_nt(name, a, b, out_dtype=F32, after=None):
    s, n = a.shape
    k = b.shape[0]
    ts, tk = _tile(s), _div_tile(k)
    return _mm(name, a, b, ((1,), (1,)), (k // tk, s // ts),
               pl.BlockSpec((ts, n), lambda j, t: (t, 0)), pl.BlockSpec((tk, n), lambda j, t: (j, 0)),
               pl.BlockSpec((ts, tk), lambda j, t: (t, j)), _sds((s, k), out_dtype), after=after)


def _mm_tn(name, a, b, out_dtype=F32, tm=512, tn=512):
    s, m = a.shape
    n = b.shape[1]
    tm, tn = min(m, tm), min(n, tn)
    return _mm(name, a, b, ((0,), (0,)), (m // tm, n // tn),
               pl.BlockSpec((s, tm), lambda i, j: (0, i)), pl.BlockSpec((s, tn), lambda i, j: (0, j)),
               pl.BlockSpec((tm, tn), lambda i, j: (i, j)), _sds((m, n), out_dtype))


def _rstd(v):
    return lax.rsqrt(jnp.mean(v * v, axis=-1, keepdims=True) + EPS)


V_GPRE, V_SCALE, V_SHIFT, V_GPOST, V_GATE = range(5)


def _vrow(v, r):
    return v[r:r + 1]


def _vblock(i):
    return pl.BlockSpec((None, 8, D), lambda t: (i, 0, 0))


def _head(xv, v):
    return ((xv * _rstd(xv) * _vrow(v, V_GPRE)) * (1.0 + _vrow(v, V_SCALE)) + _vrow(v, V_SHIFT)).astype(BF16)


def _tail(xv, fv, v, rw):
    return xv + (rw * _vrow(v, V_GATE)) * (fv * _rstd(fv) * _vrow(v, V_GPOST))


def _prenorm_fwd(x, vecs, i, after):
    s = x.shape[0]
    ts = min(s, 512)

    def body(x_ref, v_ref, after_ref, h_ref):
        h_ref[...] = _head(x_ref[...], v_ref[...])

    return _pc(body, "prenorm_fwd", (s // ts,), [_row_spec(ts, D), _vblock(i), pl.BlockSpec(memory_space=pl.ANY)], _row_spec(ts, D),
               _sds((s, D), BF16))(x, vecs, after)


def _postnorm_fwd(x, f, vecs, i, rw):
    s = x.shape[0]
    ts = min(s, 512)

    def body(x_ref, f_ref, v_ref, o_ref):
        o_ref[...] = _tail(x_ref[...], f_ref[...], v_ref[...], rw)

    return _pc(body, "postnorm_fwd", (s // ts,), [_row_spec(ts, D)] * 2 + [_vblock(i)], _row_spec(ts, D), _sds((s, D)))(x, f, vecs)


def _post_pre_fwd(x, f, vecs, i, rw_prev, after):
    s = x.shape[0]
    ts = min(s, 512)

    def body(x_ref, f_ref, vp_ref, vc_ref, after_ref, xo_ref, h_ref):
        xv = _tail(x_ref[...], f_ref[...], vp_ref[...], rw_prev)
        xo_ref[...] = xv
        h_ref[...] = _head(xv, vc_ref[...])

    return _pc(body, "post_pre_fwd", (s // ts,),
               [_row_spec(ts, D)] * 2 + [_vblock(i - 1), _vblock(i), pl.BlockSpec(memory_space=pl.ANY)], [_row_spec(ts, D)] * 2,
               [_sds((s, D)), _sds((s, D), BF16)])(x, f, vecs, vecs, after)


def _zero_at_first(first, *refs):
    @pl.when(first)
    def _():
        for ref in refs:
            ref[...] = jnp.zeros_like(ref)


def _acc(ref, first, v):
    @pl.when(first)
    def _():
        ref[...] = v

    @pl.when(jnp.logical_not(first))
    def _():
        ref[...] += v


def _colsum(v):
    return jnp.sum(v, axis=0, keepdims=True)


def _tail_bwd(do, fv, v, rw, dv_ref):
    gv = _vrow(v, V_GPOST)
    r = _rstd(fv)
    fn = fv * r
    dv_ref[V_GATE:V_GATE + 1, :] += rw * _colsum(do * (fn * gv))
    dy = (rw * _vrow(v, V_GATE)) * do
    dv_ref[V_GPOST:V_GPOST + 1, :] += _colsum(dy * fn)
    dfn = dy * gv
    return (r * (dfn - fn * jnp.mean(dfn * fn, axis=-1, keepdims=True))).astype(BF16)


def _head_bwd(do, dhv, xv, v, dv_ref):
    gv = _vrow(v, V_GPRE)
    r = _rstd(xv)
    xn = xv * r
    dv_ref[V_SHIFT:V_SHIFT + 1, :] += _colsum(dhv)
    dv_ref[V_SCALE:V_SCALE + 1, :] += _colsum(dhv * (xn * gv))
    dhp = dhv * (1.0 + _vrow(v, V_SCALE))
    dv_ref[V_GPRE:V_GPRE + 1, :] += _colsum(dhp * xn)
    dxn = dhp * gv
    return do + r * (dxn - xn * jnp.mean(dxn * xn, axis=-1, keepdims=True))


DV_SPEC = pl.BlockSpec((8, D), lambda t: (0, 0))


def _postnorm_bwd(dout, f, vecs, i, rw):
    s = dout.shape[0]
    ts = min(s, 512)

    def body(do_ref, f_ref, v_ref, df_ref, dv_ref):
        _zero_at_first(pl.program_id(0) == 0, dv_ref)
        df_ref[...] = _tail_bwd(do_ref[...], f_ref[...], v_ref[...], rw, dv_ref)

    return _pc(body, "postnorm_bwd", (s // ts,), [_row_spec(ts, D)] * 2 + [_vblock(i)], [_row_spec(ts, D), DV_SPEC],
               [_sds((s, D), BF16), _sds((8, D))])(dout, f, vecs)


def _prenorm_bwd(dout, dh, x, vecs, i):
    s = dout.shape[0]
    ts = min(s, 512)

    def body(do_ref, dh_ref, x_ref, v_ref, dx_ref, dv_ref):
        _zero_at_first(pl.program_id(0) == 0, dv_ref)
        dx_ref[...] = _head_bwd(do_ref[...], dh_ref[...], x_ref[...], v_ref[...], dv_ref)

    return _pc(body, "prenorm_bwd", (s // ts,), [_row_spec(ts, D)] * 3 + [_vblock(i)], [_row_spec(ts, D), DV_SPEC],
               [_sds((s, D)), _sds((8, D))])(dout, dh, x, vecs)


def _pre_post_bwd(dout, dh, x, f_prev, vecs, i, rw_prev):
    s = dout.shape[0]
    ts = min(s, 256)

    def body(do_ref, dh_ref, x_ref, f_ref, vc_ref, vp_ref, dx_ref, df_ref, dv_ref):
        _zero_at_first(pl.program_id(0) == 0, dv_ref)
        dx = _head_bwd(do_ref[...], dh_ref[...], x_ref[...], vc_ref[...], dv_ref)
        dx_ref[...] = dx
        df_ref[...] = _tail_bwd(dx, f_ref[...], vp_ref[...], rw_prev, dv_ref)

    rows = _row_spec(ts, D)
    return _pc(body, "pre_post_bwd", (s // ts,), [rows] * 4 + [_vblock(i), _vblock(i - 1)], [rows, rows, DV_SPEC],
               [_sds((s, D)), _sds((s, D), BF16), _sds((8, D))])(dout, dh, x, f_prev, vecs, vecs)


def _loss_fwd_bwd(y, tgt):
    s = y.shape[0]
    ts = min(s, 512)
    nt = s // ts

    def body(y_ref, t_ref, loss_ref, dy_ref, acc_ref):
        t = pl.program_id(0)
        e = y_ref[...] - t_ref[...]
        dy_ref[...] = e * (1.0 / D)
        _acc(acc_ref, t == 0, _colsum(e * e))

        @pl.when(t == nt - 1)
        def _():
            loss_ref[...] = jnp.full((1, LANE), 0.5 / D, F32) * jnp.sum(acc_ref[...])

    return _pc(body, "loss", (nt,), [_row_spec(ts, D)] * 2,
               [pl.BlockSpec((1, LANE), lambda t: (0, 0)), _row_spec(ts, D)],
               [_sds((1, LANE)), _sds((s, D))], scratch=[pltpu.VMEM((1, D), F32)])(y, tgt)


def _sigmoid(v):
    return 1.0 / (1.0 + jnp.exp(-v))


def _ffn_in_swiglu(h, win):
    s = h.shape[0]
    ts = _tile(s)
    nt = (((1,), (1,)), ((), ()))

    def body(h_ref, wa_ref, wb_ref, fac_ref, act_ref):
        hv = h_ref[...]
        a = lax.dot_general(hv, wa_ref[...], nt, preferred_element_type=F32)
        b = lax.dot_general(hv, wb_ref[...], nt, preferred_element_type=F32)
        sg = _sigmoid(a)
        silu = a * sg
        fac_ref[0] = (b * (sg * (1.0 + a * (1.0 - sg)))).astype(BF16)
        fac_ref[1] = silu.astype(BF16)
        act_ref[...] = (silu * b).astype(BF16)

    return _pc(body, "ffn_in", (4, s // ts),
               [pl.BlockSpec((ts, D), lambda k, t: (t, 0)), pl.BlockSpec((None, FSH, D), lambda k, t: (k, 0, 0)),
                pl.BlockSpec((None, FSH, D), lambda k, t: (k + 4, 0, 0))],
               [pl.BlockSpec((2, None, ts, FSH), lambda k, t: (0, k, t, 0)), pl.BlockSpec((None, ts, FSH), lambda k, t: (k, t, 0))],
               [_sds((2, 4, s, FSH), BF16), _sds((4, s, FSH), BF16)])(h, win, win)


def _ffn_out_dx_swiglu(df, wout, fac, after):
    s = df.shape[0]
    ts = _tile(s)
    nt = (((1,), (1,)), ((), ()))

    def body(df_ref, w_ref, fac_ref, after_ref, o_ref):
        d = lax.dot_general(df_ref[...], w_ref[...], nt, preferred_element_type=F32)
        o_ref[0] = (d * fac_ref[0]).astype(BF16)
        o_ref[1] = (d * fac_ref[1]).astype(BF16)

    spec = pl.BlockSpec((2, None, ts, FSH), lambda k, t: (0, k, t, 0))
    out = _pc(body, "ffn_out_dx", (4, s // ts),
              [pl.BlockSpec((ts, D), lambda k, t: (t, 0)), pl.BlockSpec((None, FSH, D), lambda k, t: (k, 0, 0)), spec,
               pl.BlockSpec(memory_space=pl.ANY)],
              spec, _sds((2, 4, s, FSH), BF16))(df, wout, fac, after)
    return out.reshape(N_DEV, s, FSH)


def _ffn_fwd(h, win, wout_of):
    s = h.shape[0]
    fac, act = _ffn_in_swiglu(h, win)
    f = _mm_sum("ffn_out", act, wout_of(act).reshape(4, FSH, D), min(s, 512))
    return f, (h, fac, act)


def _ffn_bwd(df, saved, win, wout, send, after):
    h, fac, act = saved
    s = h.shape[0]
    ts = s
    wout = wout.reshape(4, FSH, D)
    dwout = _mm("ffn_out_dw", act, df, ((0,), (0,)), (4, 2),
                pl.BlockSpec((None, s, FSH), lambda k, j: (k, 0, 0)), pl.BlockSpec((s, D // 2), lambda k, j: (0, j)),
                pl.BlockSpec((None, FSH, D // 2), lambda k, j: (k, 0, j)), _sds((4, FSH, D), BF16), after=after)
    dz = _ffn_out_dx_swiglu(df, wout, fac, send("w_out", dwout.reshape(N_DEV, D_FF // N_DEV, D)))
    dwin = _mm("ffn_in_dw", dz, h, ((0,), (0,)), (N_DEV, 2),
               pl.BlockSpec((None, s, FSH), lambda j, i: (j, 0, 0)), pl.BlockSpec((s, D // 2), lambda j, i: (0, i)),
               pl.BlockSpec((None, FSH, D // 2), lambda j, i: (j, 0, i)), _sds((N_DEV, FSH, D), BF16))
    return _mm_sum("ffn_in_dx", dz, win, min(s, 512), after=send("w_in", dwin))


def _shift_rows(v, k, row, s, back):
    if back:
        return jnp.where(row < s - k, pltpu.roll(v, s - k, 0), 0.0)
    return jnp.where(row >= k, pltpu.roll(v, k, 0), 0.0)


def _window_sum(v, w, row, s, back):
    k = 1
    while k < w:
        v = v + _shift_rows(v, k, row, s, back)
        k *= 2
    return v


def _pool_fwd(z, pool_w, pool_scale):
    s = z.shape[0]

    def body(z_ref, w_ref, sc_ref, y_ref, d_ref):
        row = lax.broadcasted_iota(jnp.int32, (s, HD), 0)
        for g, w in enumerate(POOL_WINDOWS):
            sl = slice(g * HD, (g + 1) * HD)
            a = z_ref[:, sl]
            cnt = jnp.minimum(row + 1, w).astype(F32)
            d = (_window_sum(a, w, row, s, False) / cnt - a).astype(BF16)
            d_ref[:, sl] = d
            y = jnp.dot(d, _bf(w_ref[g]), preferred_element_type=F32)
            y_ref[:, sl] = (y * sc_ref[:, sl]).astype(BF16)

    return _pc(body, "pool_fwd", (1,),
               [pl.BlockSpec((s, NH * HD), lambda i: (0, 0)), pl.BlockSpec((NH, HD, HD), lambda i: (0, 0, 0)),
                pl.BlockSpec((1, NH * HD), lambda i: (0, 0))],
               [pl.BlockSpec((s, NH * HD), lambda i: (0, 0))] * 2,
               [_sds((s, NH * HD), BF16)] * 2)(z, pool_w, pool_scale)


def _pool_bwd(dy, d, pool_w, pool_scale):
    s = dy.shape[0]

    def body(dy_ref, d_ref, w_ref, sc_ref, dz_ref, dw_ref, dsc_ref):
        row = lax.broadcasted_iota(jnp.int32, (s, HD), 0)
        for g, w in enumerate(POOL_WINDOWS):
            sl = slice(g * HD, (g + 1) * HD)
            dyg, dg, wg = dy_ref[:, sl], d_ref[:, sl], _bf(w_ref[g])
            yraw = jnp.dot(dg, wg, preferred_element_type=F32)
            dsc_ref[:, sl] = _colsum(dyg * yraw)
            dyr = _bf(dyg * sc_ref[:, sl])
            dw_ref[g] = lax.dot_general(dg, dyr, (((0,), (0,)), ((), ())), preferred_element_type=F32)
            dd = lax.dot_general(dyr, wg, (((1,), (1,)), ((), ())), preferred_element_type=F32)
            cnt = jnp.minimum(row + 1, w).astype(F32)
            dz_ref[:, sl] = (_window_sum(dd / cnt, w, row, s, True) - dd).astype(BF16)

    return _pc(body, "pool_bwd", (1,),
               [pl.BlockSpec((s, NH * HD), lambda i: (0, 0)), pl.BlockSpec((s, NH * HD), lambda i: (0, 0)),
                pl.BlockSpec((NH, HD, HD), lambda i: (0, 0, 0)), pl.BlockSpec((1, NH * HD), lambda i: (0, 0))],
               [pl.BlockSpec((s, NH * HD), lambda i: (0, 0)), pl.BlockSpec((NH, HD, HD), lambda i: (0, 0, 0)),
                pl.BlockSpec((1, NH * HD), lambda i: (0, 0))],
               [_sds((s, NH * HD), BF16), _sds((NH, HD, HD)), _sds((1, NH * HD))])(dy, d, pool_w, pool_scale)


def _gelu(v):
    return 0.5 * v * (1.0 + jnp.tanh(GELU_C * (v + 0.044715 * (v * v * v))))


def _gelu_and_grad(v):
    t = jnp.tanh(GELU_C * (v + 0.044715 * (v * v * v)))
    return 0.5 * v * (1.0 + t), 0.5 * (1.0 + t) + 0.5 * v * (1.0 - t * t) * (GELU_C * (1.0 + 3.0 * 0.044715 * (v * v)))


def _gelu_grad(v):
    return _gelu_and_grad(v)[1]


def _causal_mask():
    return lax.broadcasted_iota(jnp.int32, (HD, HD), 0) >= lax.broadcasted_iota(jnp.int32, (HD, HD), 1)


def _sgu_specs():
    w = NH * HD
    return [pl.BlockSpec((HD, w), lambda c: (c, 1)), pl.BlockSpec((HD, w), lambda c: (c, 2)),
            pl.BlockSpec((1, w), lambda c: (0, 0)), pl.BlockSpec((1, w), lambda c: (0, 0)),
            pl.BlockSpec((NH, HD, HD), lambda c: (0, 0, 0)), pl.BlockSpec((HD, LANE), lambda c: (0, 0))]


def _sgu_head(v, lng_ref, lnb_ref, w_ref, h):
    sl = slice(h * HD, (h + 1) * HD)
    vh = v[:, sl]
    xc = vh - jnp.mean(vh, axis=-1, keepdims=True)
    rs = lax.rsqrt(jnp.mean(xc * xc, axis=-1, keepdims=True) + EPS)
    vhat = xc * rs
    vn = _bf(vhat * lng_ref[:, sl] + lnb_ref[:, sl])
    wc = _bf(jnp.where(_causal_mask(), w_ref[h], 0.0))
    return sl, rs, vhat, vn, wc


def _sgu_fwd(z, ln_g, ln_b, sgu_w, sgu_bt):
    s = z.shape[0]

    def body(zu_ref, zv_ref, lng_ref, lnb_ref, w_ref, bt_ref, y_ref):
        u, v = _gelu(zu_ref[...]), _gelu(zv_ref[...])
        for h in range(NH):
            sl, _, _, vn, wc = _sgu_head(v, lng_ref, lnb_ref, w_ref, h)
            sp = jnp.dot(wc, vn, preferred_element_type=F32) + bt_ref[:, h:h + 1]
            y_ref[:, sl] = (u[:, sl] * sp).astype(BF16)

    return _pc(body, "sgu_fwd", (s // HD,), _sgu_specs(), pl.BlockSpec((HD, NH * HD), lambda c: (c, 0)),
               _sds((s, NH * HD), BF16))(z, z, ln_g, ln_b, sgu_w, sgu_bt)


def _sgu_bwd(z, dy, ln_g, ln_b, sgu_w, sgu_bt, head_sum):
    s = z.shape[0]
    w = NH * HD
    nc = s // HD

    def body(zu_ref, zv_ref, lng_ref, lnb_ref, w_ref, bt_ref, dy_ref, hs_ref,
             dzu_ref, dzv_ref, dlng_ref, dlnb_ref, dw_ref, dbt_ref, dsacc_ref):
        c = pl.program_id(0)
        _zero_at_first(c == 0, dsacc_ref, dw_ref, dlng_ref, dlnb_ref)
        zu, zv = zu_ref[...], zv_ref[...]
        (u, gu), (v, gv) = _gelu_and_grad(zu), _gelu_and_grad(zv)
        dyv = dy_ref[...]
        ds = dyv * u
        dsacc_ref[...] += ds
        for h in range(NH):
            sl, rs, vhat, vn, wc = _sgu_head(v, lng_ref, lnb_ref, w_ref, h)
            sp = jnp.dot(wc, vn, preferred_element_type=F32) + bt_ref[:, h:h + 1]
            dzu_ref[:, sl] = (dyv[:, sl] * sp * gu[:, sl]).astype(BF16)
            dsh = _bf(ds[:, sl])
            dwh = lax.dot_general(dsh, vn, (((1,), (1,)), ((), ())), preferred_element_type=F32)
            dw_ref[h] += jnp.where(_causal_mask(), dwh, 0.0)
            dvn = lax.dot_general(wc, dsh, (((0,), (0,)), ((), ())), preferred_element_type=F32)
            dlng_ref[:, sl] += _colsum(dvn * vhat)
            dlnb_ref[:, sl] += _colsum(dvn)
            dvh = dvn * lng_ref[:, sl]
            dv = rs * (dvh - jnp.mean(dvh, axis=-1, keepdims=True) - vhat * jnp.mean(dvh * vhat, axis=-1, keepdims=True))
            dzv_ref[:, sl] = (dv * gv[:, sl]).astype(BF16)

        @pl.when(c == nc - 1)
        def _():
            dbt_ref[...] = jnp.dot(dsacc_ref[...], hs_ref[...], preferred_element_type=F32, precision=HIGHEST)

    outs = _pc(body, "sgu_bwd", (nc,),
               _sgu_specs() + [pl.BlockSpec((HD, w), lambda c: (c, 1)), pl.BlockSpec((w, LANE), lambda c: (0, 0))],
               [pl.BlockSpec((HD, w), lambda c: (c, 0))] * 2 + [pl.BlockSpec((1, w), lambda c: (0, 0))] * 2
               + [pl.BlockSpec((NH, HD, HD), lambda c: (0, 0, 0)), pl.BlockSpec((HD, LANE), lambda c: (0, 0))],
               [_sds((s, w), BF16)] * 2 + [_sds((1, w))] * 2 + [_sds((NH, HD, HD)), _sds((HD, LANE))],
               scratch=[pltpu.VMEM((HD, w), F32)])(z, z, ln_g, ln_b, sgu_w, sgu_bt, dy, head_sum)
    return outs


def _cmul(ar, ai, br, bi):
    return ar * br - ai * bi, ar * bi + ai * br


def _ssm_prep(lam_re, lam_im, lam_re_rep, lam_im_rep, log_dt, b_re, b_im):
    def disc(lr, li, dt):
        mag = jnp.exp(lr * dt)
        return mag * jnp.cos(li * dt), mag * jnp.sin(li * dt)

    def body(lr_ref, li_ref, lrr_ref, lir_ref, ldt_ref, br_ref, bi_ref, or_ref, oi_ref, bbr_ref, bbi_ref):
        dt = jnp.exp(ldt_ref[...])
        or_ref[...], oi_ref[...] = disc(lr_ref[...], li_ref[...], dt)
        lr, li = lrr_ref[...], lir_ref[...]
        er, ei = disc(lr, li, dt)
        den = lr * lr + li * li
        kr = ((er - 1.0) * lr + ei * li) / den
        ki = (ei * lr - (er - 1.0) * li) / den
        bbr_ref[...], bbi_ref[...] = _cmul(kr, ki, br_ref[...], bi_ref[...])

    small = pl.BlockSpec((SSM_G, SSM_P), lambda i: (0, 0))
    wide = pl.BlockSpec((SSM_G, SSM_P * SSM_N), lambda i: (0, 0))
    col = pl.BlockSpec((SSM_G, 1), lambda i: (0, 0))
    return _pc(body, "ssm_prep", (1,), [small, small, wide, wide, col, wide, wide], [small, small, wide, wide],
               [_sds((SSM_G, SSM_P))] * 2 + [_sds((SSM_G, SSM_P * SSM_N))] * 2)(
        lam_re, lam_im, lam_re_rep, lam_im_rep, log_dt, b_re, b_im)


def _ssm_param_bwd(g_lam_re, g_lam_im, g_bb_re, g_bb_im, lam_re, lam_im, lam_re_rep, lam_im_rep, log_dt, b_re, b_im, seg):
    def body(glr_ref, gli_ref, gbr_ref, gbi_ref, lr_ref, li_ref, lrr_ref, lir_ref, ldt_ref, br_ref, bi_ref, seg_ref,
             dlr_ref, dli_ref, ddt_ref, dbr_ref, dbi_ref):
        dt = jnp.exp(ldt_ref[...])
        lr, li = lrr_ref[...], lir_ref[...]
        mag = jnp.exp(lr * dt)
        er, ei = mag * jnp.cos(li * dt), mag * jnp.sin(li * dt)
        den = lr * lr + li * li
        kr = ((er - 1.0) * lr + ei * li) / den
        ki = (ei * lr - (er - 1.0) * li) / den
        gbr, gbi = gbr_ref[...], gbi_ref[...]
        dbr_ref[...], dbi_ref[...] = _cmul(kr, -ki, gbr, gbi)
        tr, ti = _cmul(br_ref[...], -bi_ref[...], gbr, gbi)
        gkr = jnp.dot(tr, seg_ref[...], preferred_element_type=F32, precision=HIGHEST)
        gki = jnp.dot(ti, seg_ref[...], preferred_element_type=F32, precision=HIGHEST)
        lr, li = lr_ref[...], li_ref[...]
        mag = jnp.exp(lr * dt)
        er, ei = mag * jnp.cos(li * dt), mag * jnp.sin(li * dt)
        den = lr * lr + li * li
        ir, ii = lr / den, -li / den
        kr, ki = _cmul(er - 1.0, ei, ir, ii)
        ar, ai = _cmul(ir, -ii, gkr, gki)
        glr, gli = glr_ref[...] + ar, gli_ref[...] + ai
        qr, qi = _cmul(kr, ki, ir, ii)
        g1r, g1i = _cmul(-qr, qi, gkr, gki)
        g2r, g2i = _cmul(dt * er, -dt * ei, glr, gli)
        dlr_ref[...] = g1r + g2r
        dli_ref[...] = g1i + g2i
        wr, wi = _cmul(lr, li, er, ei)
        g_dt = jnp.sum(wr * glr + wi * gli, axis=-1, keepdims=True)
        ddt_ref[...] = jnp.broadcast_to(dt * g_dt, (SSM_G, LANE))

    small = pl.BlockSpec((SSM_G, SSM_P), lambda i: (0, 0))
    wide = pl.BlockSpec((SSM_G, SSM_P * SSM_N), lambda i: (0, 0))
    col = pl.BlockSpec((SSM_G, 1), lambda i: (0, 0))
    segs = pl.BlockSpec((SSM_P * SSM_N, SSM_P), lambda i: (0, 0))
    return _pc(body, "ssm_param_bwd", (1,), [small, small, wide, wide, small, small, wide, wide, col, wide, wide, segs],
               [small, small, pl.BlockSpec((SSM_G, LANE), lambda i: (0, 0)), wide, wide],
               [_sds((SSM_G, SSM_P))] * 2 + [_sds((SSM_G, LANE))] + [_sds((SSM_G, SSM_P * SSM_N))] * 2)(
        g_lam_re, g_lam_im, g_bb_re, g_bb_im, lam_re, lam_im, lam_re_rep, lam_im_rep, log_dt, b_re, b_im, seg)


SCAN_LANES = 512
SCAN_ROWS = 8


SCAN_GROUPS = SCAN_LANES // SSM_P
SCAN_COLS = SCAN_GROUPS * SSM_N
SCAN_CHUNK = 512


def _ssm_scan(name, v, w_in, lam_re, lam_im, w_out, reverse, states=None, u=None):
    s = v.shape[0]
    ln, rows = SCAN_LANES, SCAN_ROWS
    ch = min(SCAN_CHUNK if states is not None else 2 * SCAN_CHUNK, s)
    nch, ntile = s // ch, ch // rows
    nt_dims = (((1,), (1,)), ((), ()))
    with_sum = states is not None
    tn_dims = (((0,), (0,)), ((), ()))

    def body(*refs):
        v_ref, win_ref, lr_ref, li_ref, wout_ref = refs[:5]
        n_in = 8 if with_sum else 5
        or_ref, oi_ref, y_ref = refs[n_in:n_in + 3]
        br_s, bi_s = refs[n_in + (9 if with_sum else 3):][:2]
        if with_sum:
            mb_s, mc_s = refs[-2:]
            mb_s[...] = jnp.zeros_like(mb_s)
            mc_s[...] = jnp.zeros_like(mc_s)
        l1 = (lr_ref[...], li_ref[...])
        pw = [l1]
        for _ in range(rows - 1):
            pw.append(_cmul(*pw[-1], *l1))
        row = lax.broadcasted_iota(jnp.int32, (rows, ln), 0)
        expo = (rows - row) if reverse else (row + 1)
        pr = jnp.zeros((rows, ln), F32)
        pi = jnp.zeros((rows, ln), F32)
        for e in range(1, rows + 1):
            pr = jnp.where(expo == e, pw[e - 1][0], pr)
            pi = jnp.where(expo == e, pw[e - 1][1], pi)
        lk = {}
        for k in (1, 2, 4):
            keep = (row < rows - k) if reverse else (row >= k)
            lk[k] = (jnp.where(keep, pw[k - 1][0], 0.0), jnp.where(keep, pw[k - 1][1], 0.0))

        def chunk(c, carry):
            q0 = pl.multiple_of(((nch - 1 - c) if reverse else c) * ch, ch)
            b = jnp.dot(_bf(v_ref[pl.ds(q0, ch), :]), win_ref[...], preferred_element_type=F32)
            br_s[...] = b[:, :ln]
            bi_s[...] = b[:, ln:]

            def step(i, carry):
                cr, ci = carry[:2]
                r0 = pl.multiple_of(((ntile - 1 - i) if reverse else i) * rows, rows)
                xr, xi = br_s[pl.ds(r0, rows), :], bi_s[pl.ds(r0, rows), :]
                for k in (1, 2, 4):
                    shift = rows - k if reverse else k
                    ar, ai = _cmul(lk[k][0], lk[k][1], pltpu.roll(xr, shift, 0), pltpu.roll(xi, shift, 0))
                    xr, xi = xr + ar, xi + ai
                ar, ai = _cmul(pr, pi, cr, ci)
                xr, xi = xr + ar, xi + ai
                g0 = pl.multiple_of(q0 + r0, rows)
                or_ref[pl.ds(g0, rows), :] = xr
                oi_ref[pl.ds(g0, rows), :] = xi
                if not with_sum:
                    return (xr[rows - 1:rows], xi[rows - 1:rows]) if not reverse else (xr[0:1], xi[0:1])
                nr = jnp.where(row == rows - 1, cr, pltpu.roll(xr, rows - 1, 0))
                ni = jnp.where(row == rows - 1, ci, pltpu.roll(xi, rows - 1, 0))
                sr, si = refs[5][pl.ds(g0, rows), :], refs[6][pl.ds(g0, rows), :]
                return xr[0:1], xi[0:1], carry[2] + (sr * nr + si * ni), carry[3] + (sr * ni - si * nr)

            carry = lax.fori_loop(0, ntile, step, carry)
            if with_sum:
                rows_c = pl.ds(q0, ch)
                uc, vc = _bf(refs[7][rows_c, :]), _bf(v_ref[rows_c, :])
                for scr, left, (right_re, right_im) in ((mb_s, uc, (or_ref, oi_ref)), (mc_s, vc, (refs[5], refs[6]))):
                    scr[:, :ln] += lax.dot_general(left, _bf(right_re[rows_c, :]), tn_dims, preferred_element_type=F32)
                    scr[:, ln:] += lax.dot_general(left, _bf(right_im[rows_c, :]), tn_dims, preferred_element_type=F32)
            w = wout_ref[...]
            y_ref[pl.ds(q0, ch), :] = (
                lax.dot_general(_bf(or_ref[pl.ds(q0, ch), :]), w[:, :ln], nt_dims, preferred_element_type=F32)
                + lax.dot_general(_bf(oi_ref[pl.ds(q0, ch), :]), w[:, ln:], nt_dims, preferred_element_type=F32))
            return carry

        zero = jnp.zeros((1, ln), F32)
        init = (zero, zero) + ((jnp.zeros((rows, ln), F32),) * 2 if with_sum else ())
        carry = lax.fori_loop(0, nch, chunk, init)
        if with_sum:
            refs[n_in + 3][...] = _colsum(carry[2])
            refs[n_in + 4][...] = _colsum(carry[3])
            row_g = lax.broadcasted_iota(jnp.int32, (SCAN_COLS, LANE), 0) // SSM_N
            lane_g = lax.broadcasted_iota(jnp.int32, (SCAN_COLS, LANE), 1) // SSM_P
            for scr, o_re, o_im in ((mb_s, refs[n_in + 5], refs[n_in + 6]), (mc_s, refs[n_in + 7], refs[n_in + 8])):
                for part, o_ref in enumerate((o_re, o_im)):
                    fold = jnp.zeros((SCAN_COLS, LANE), F32)
                    for cb in range(ln // LANE):
                        fold = fold + jnp.where(2 * cb + lane_g == row_g, scr[:, part * ln + cb * LANE:part * ln + (cb + 1) * LANE], 0.0)
                    o_ref[...] = jnp.where(row_g % 2 == 0, fold, pltpu.roll(fold, SSM_P, 1))

    vec = pl.BlockSpec((1, ln), lambda j: (0, j))
    blk = pl.BlockSpec((s, ln), lambda j: (0, j))
    cols = pl.BlockSpec((s, SCAN_COLS), lambda j: (0, j))
    wspec = pl.BlockSpec((None, SCAN_COLS, 2 * ln), lambda j: (j, 0, 0))
    ins, args = [cols, wspec, vec, vec, wspec], [v, w_in, lam_re, lam_im, w_out]
    outs, shapes = [blk, blk, cols], [_sds((s, SSM_L))] * 2 + [_sds((s, SSM_G * SSM_N))]
    scratch = [pltpu.VMEM((ch, ln), F32)] * 2
    if with_sum:
        own = pl.BlockSpec((None, SCAN_COLS, LANE), lambda j: (j, 0, 0))
        ins, args = ins + [blk, blk, cols], args + list(states) + [u]
        outs = outs + [vec, vec] + [own] * 4
        shapes = shapes + [_sds((1, SSM_L))] * 2 + [_sds((SSM_L // ln, SCAN_COLS, LANE))] * 4
        scratch = scratch + [pltpu.VMEM((SCAN_COLS, 2 * ln), F32)] * 2
    return _pc(body, name, (SSM_L // ln,), ins, outs, shapes, scratch=scratch)(*args)


def _ssm_act_fwd(y, u, d_skip):
    s = y.shape[0]
    ts = min(s, 512)

    def body(y_ref, u_ref, d_ref, o_ref):
        o_ref[...] = _gelu(y_ref[...] + d_ref[...] * u_ref[...]).astype(BF16)

    return _pc(body, "ssm_act_fwd", (s // ts,), [_row_spec(ts, D)] * 2 + [_vec_spec(D)], _row_spec(ts, D),
               _sds((s, D), BF16))(y, u, d_skip)


def _ssm_act_bwd(dg, y, u, d_skip):
    s = y.shape[0]
    ts = min(s, 512)

    def body(dg_ref, y_ref, u_ref, d_ref, dy_ref, dd_ref):
        uv = u_ref[...]
        dy = dg_ref[...] * _gelu_grad(y_ref[...] + d_ref[...] * uv)
        dy_ref[...] = dy.astype(BF16)
        _acc(dd_ref, pl.program_id(0) == 0, _colsum(dy * uv))

    return _pc(body, "ssm_act_bwd", (s // ts,), [_row_spec(ts, D)] * 3 + [_vec_spec(D)], [_row_spec(ts, D), _vec_spec(D)],
               [_sds((s, D), BF16), _sds((1, D))])(dg, y, u, d_skip)


def _axpy(a, b, d_skip):
    s = a.shape[0]
    ts = min(s, 512)

    def body(a_ref, b_ref, d_ref, o_ref):
        o_ref[...] = (a_ref[...] + d_ref[...] * b_ref[...].astype(F32)).astype(BF16)

    return _pc(body, "ssm_du", (s // ts,), [_row_spec(ts, D)] * 2 + [_vec_spec(D)], _row_spec(ts, D),
               _sds((s, D), BF16))(a, b, d_skip)


def _glu_fwd(zz):
    s = zz.shape[0]
    ts = min(s, 512)

    def body(a_ref, b_ref, o_ref):
        o_ref[...] = a_ref[...] * _sigmoid(b_ref[...])

    return _pc(body, "glu_fwd", (s // ts,), [_row_spec(ts, D, 0), _row_spec(ts, D, 1)], _row_spec(ts, D), _sds((s, D)))(zz, zz)


def _glu_bwd(zz, df):
    s = zz.shape[0]
    ts = min(s, 512)

    def body(a_ref, b_ref, df_ref, o_ref):
        sg = _sigmoid(b_ref[...])
        dfv = df_ref[...].astype(F32)
        o_ref[:, :D] = (dfv * sg).astype(BF16)
        o_ref[:, D:] = (dfv * a_ref[...] * sg * (1.0 - sg)).astype(BF16)

    return _pc(body, "glu_bwd", (s // ts,), [_row_spec(ts, D, 0), _row_spec(ts, D, 1), _row_spec(ts, D)],
               _row_spec(ts, 2 * D), _sds((s, 2 * D), BF16))(zz, zz, df)


def _ssm_block_diag(m_re, m_im):
    rows, half = SCAN_COLS, SCAN_LANES
    expand = jnp.tile(jnp.eye(SSM_P, dtype=BF16), (1, SCAN_GROUPS))

    def body(mr_ref, mi_ref, e_ref, o_ref):
        keep = (lax.broadcasted_iota(jnp.int32, (rows, half), 0) // SSM_N
                == lax.broadcasted_iota(jnp.int32, (rows, half), 1) // SSM_P)
        for part, m_ref in enumerate((mr_ref, mi_ref)):
            t = jnp.dot(_bf(m_ref[...]), e_ref[...], preferred_element_type=F32)
            o_ref[:, part * half:(part + 1) * half] = jnp.where(keep, t, 0.0).astype(BF16)

    blk = pl.BlockSpec((rows, SSM_P), lambda q: (q, 0))
    nb = SSM_G // SCAN_GROUPS
    return _pc(body, "ssm_block_diag", (nb,), [blk, blk, pl.BlockSpec((SSM_P, half), lambda q: (0, 0))],
               pl.BlockSpec((None, rows, 2 * half), lambda q: (q, 0, 0)), _sds((nb, rows, 2 * half), BF16))(m_re, m_im, expand)


def _mod_part(c_all, ada_w):
    n = ada_w.shape[-1]

    def body(c_ref, w_ref, o_ref):
        cv = c_ref[...]
        cond = _bf(cv * _sigmoid(cv))
        o_ref[...] = jnp.dot(cond, _bf(w_ref[...]), preferred_element_type=F32)

    return _pc(body, "mod_part", (2,), [pl.BlockSpec((N_DEV, D), lambda l: (0, 0)), pl.BlockSpec((None, D, n), lambda l: (l, 0, 0))],
               pl.BlockSpec((None, N_DEV, n), lambda l: (l, 0, 0)), _sds((2, N_DEV, n)))(c_all, ada_w)


def _ada_w_grad(c_all_t, dmod):
    nl, _, n = dmod.shape
    tr = 128

    def body(c_ref, d_ref, o_ref):
        cv = c_ref[...]
        cond = _bf(cv * _sigmoid(cv)).astype(F32)
        dm = _bf(d_ref[...]).astype(F32)
        acc = cond[:, 0:1] * dm[0:1, :]
        for b in range(1, N_DEV):
            acc = acc + cond[:, b:b + 1] * dm[b:b + 1, :]
        o_ref[...] = acc

    return _pc(body, "ada_w_grad", (nl, D // tr),
               [pl.BlockSpec((tr, N_DEV), lambda l, t: (t, 0)), pl.BlockSpec((None, N_DEV, n), lambda l, t: (l, 0, 0))],
               pl.BlockSpec((None, tr, n), lambda l, t: (l, t, 0)), _sds((nl, D, n)))(c_all_t, dmod)


def _adamw(name, parts, w, m, v, slot=0, prev=None, after=None):
    p, r, c = parts.shape
    tr = r
    while tr * c * 4 > (1 << 21) and tr % 16 == 0:
        tr //= 2
    nt = r // tr

    def body(p_ref, w_ref, m_ref, v_ref, *rest):
        g_ref, d_ref, nm_ref, nv_ref = rest[-4:]
        g = p_ref[0].astype(F32)
        for i in range(1, p):
            g = g + p_ref[i].astype(F32)
        g_ref[...] = g
        d_ref[...], nm_ref[...], nv_ref[...] = _adam_update(g, w_ref[...], m_ref[...], v_ref[...])

    blk = pl.BlockSpec((tr, c), lambda t: (slot * nt + t, 0))
    in_specs = [pl.BlockSpec((p, tr, c), lambda t: (0, t, 0)), blk, blk, blk]
    unread = list(prev or []) + ([after] if after is not None else [])
    return pl.pallas_call(
        body, name=name, grid=(nt,), in_specs=in_specs + [pl.BlockSpec(memory_space=pl.ANY)] * len(unread), out_specs=[blk] * 4,
        out_shape=[_sds(w.shape)] * 4, input_output_aliases={4 + i: i for i in range(4)} if prev else {},
        compiler_params=pltpu.CompilerParams(dimension_semantics=("arbitrary",), vmem_limit_bytes=VMEM_LIMIT_BYTES))(parts, w, m, v, *unread)


def _adam_update(g, w, m, v):
    m2 = B1 * m + (1.0 - B1) * g
    v2 = B2 * v + (1.0 - B2) * (g * g)
    m_hat = m2 / (1.0 - B1 ** STEP)
    v_hat = v2 / (1.0 - B2 ** STEP)
    return -LR * (m_hat / (jnp.sqrt(v_hat) + ADAM_EPS) + WD * w), m2, v2


def _adamw_many(name, items, after):
    n = len(items)

    def body(*refs):
        outs = refs[4 * n + 1:]
        for i in range(n):
            g, w, m, v = (r[...] for r in refs[4 * i:4 * i + 4])
            for o, val in zip(outs[3 * i:3 * i + 3], _adam_update(g, w, m, v)):
                o[...] = val

    full = lambda a: pl.BlockSpec(a.shape, lambda t: (0, 0))
    flat = [a for item in items for a in item]
    res = _pc(body, name, (1,), [full(a) for a in flat] + [pl.BlockSpec(memory_space=pl.ANY)],
              [full(item[1]) for item in items for _ in range(3)],
              [_sds(item[1].shape) for item in items for _ in range(3)])(*flat, after)
    return [tuple(res[3 * i:3 * i + 3]) for i in range(n)]


def _sum_parts(parts):
    p, r, c = parts.shape
    tr = r
    while tr * c * 4 > (1 << 19) and tr % 16 == 0:
        tr //= 2

    def body(p_ref, o_ref):
        g = p_ref[0]
        for i in range(1, p):
            g = g + p_ref[i]
        o_ref[...] = g

    return _pc(body, "sum_parts", (r // tr,), [pl.BlockSpec((p, tr, c), lambda t: (0, t, 0))], pl.BlockSpec((tr, c), lambda t: (t, 0)),
               _sds((r, c)))(parts)


def _place():
    x, y, c = lax.axis_index("x"), lax.axis_index("y"), lax.axis_index("c")
    peers = []
    for k in range(1, N_DEV):
        px = (1 - x) if k & 4 else x
        py = (1 - y) if k & 2 else y
        pc = (1 - c) if k & 1 else c
        peers.append(((px, py, pc), 4 * px + 2 * py + pc))
    return 4 * x + 2 * y + c, peers


def _at(ref, idx):
    return ref if idx is None else ref.at[idx]


def _exchange_copies(plan, n, src_refs, dst_refs, send_sems, recv_sems, local_sems=None, with_arrivals=True):
    me, peers = _place()
    local = [] if local_sems is None else [
        pltpu.make_async_copy(_at(src_refs[si], sx), _at(dst_refs[di], dx), local_sems.at[i])
        for i, (si, sx, di, dx) in enumerate(plan(me, me, 0))]

    def remote(k, i, dev, entry):
        si, sx, di, dx = entry
        return pltpu.make_async_remote_copy(_at(src_refs[si], sx), _at(dst_refs[di], dx), send_sems.at[k * n + i], recv_sems.at[k * n + i],
                                            device_id=dev, device_id_type=MESH)

    sends = [remote(k, i, dev, e) for k, (dev, peer) in enumerate(peers) for i, e in enumerate(plan(me, peer, k + 1))]
    if not with_arrivals:
        return local, sends, []
    arrivals = [remote(k, i, dev, e) for k, (dev, peer) in enumerate(peers) for i, e in enumerate(plan(peer, me, k + 1))]
    return local, sends, arrivals


def _sem_shapes(n_copies, local=True):
    sems = [pltpu.SemaphoreType.DMA(((N_DEV - 1) * n_copies,)), pltpu.SemaphoreType.DMA(((N_DEV - 1) * n_copies,))]
    return sems + [pltpu.SemaphoreType.DMA((n_copies,))] if local else sems


def _exchange(name, srcs, dst_shapes, plan, n_copies):
    ns, nd = len(srcs), len(dst_shapes)

    def body(*refs):
        local, sends, arrivals = _exchange_copies(plan, n_copies, refs[:ns], refs[ns:ns + nd], *refs[ns + nd:])
        for cp in local + sends:
            cp.start()
        for cp in arrivals:
            cp.wait_recv()
        for cp in sends:
            cp.wait_send()
        for cp in local:
            cp.wait()

    any_spec = pl.BlockSpec(memory_space=pl.ANY)
    return pl.pallas_call(
        body, name=name, in_specs=[any_spec] * ns, out_specs=[any_spec] * nd, out_shape=list(dst_shapes),
        scratch_shapes=_sem_shapes(n_copies))(*srcs)


HBM_SPEC = pl.BlockSpec(memory_space=pltpu.HBM)
SEM_SPEC = pl.BlockSpec(memory_space=pltpu.SEMAPHORE)
ANY_SPEC = pl.BlockSpec(memory_space=pl.ANY)
TOKEN_SPEC = pl.BlockSpec(memory_space=pltpu.VMEM)
SIDE_EFFECT = pltpu.SideEffectType.DATAFLOW_SIDE_EFFECTING


def _wait_all(local, sends, arrivals):
    for cp in arrivals:
        cp.wait_recv()
    for cp in sends:
        cp.wait_send()
    for cp in local:
        cp.wait()


def _exchange_start(name, srcs, dst_shapes, plan, n_copies, order):
    ns, nd = len(srcs), len(dst_shapes)
    nb = ns + nd

    def body(*refs):
        local, sends, _ = _exchange_copies(plan, n_copies, refs[:ns], refs[ns:nb], *refs[nb + 1:nb + 4], with_arrivals=False)
        for cp in local + sends:
            cp.start()
        refs[-1][...] = jnp.zeros((8, LANE), F32)

    lands = [pltpu.with_memory_space_constraint(lax.empty(d.shape, d.dtype), pltpu.HBM) for d in dst_shapes]
    srcs = [pltpu.with_memory_space_constraint(a, pltpu.HBM) for a in srcs]
    bufs = srcs + lands
    out = pl.pallas_call(
        body, name=name, in_specs=[HBM_SPEC] * nb + [ANY_SPEC],
        out_specs=[SEM_SPEC] * 3 + [HBM_SPEC] * nb + [TOKEN_SPEC],
        out_shape=_sem_shapes(n_copies) + [pltpu.HBM(a.shape, a.dtype) for a in bufs] + [_sds((8, LANE))],
        input_output_aliases={i: 3 + i for i in range(nb)},
        compiler_params=pltpu.CompilerParams(has_side_effects=SIDE_EFFECT))(*bufs, order)
    return out[:3], out[3:3 + ns], out[3 + ns:3 + nb], out[-1]


def _exchange_relay(name, sems, srcs, lands, plan, n_copies, plan2, n_copies2, after):
    ns, nd = len(srcs), len(lands)
    nb = ns + nd

    def body(*refs):
        land_refs = refs[ns:nb]
        _wait_all(*_exchange_copies(plan, n_copies, refs[:ns], land_refs, *refs[nb:nb + 3]))
        _, sends, _ = _exchange_copies(plan2, n_copies2, land_refs, land_refs, *refs[nb + 4:nb + 6], with_arrivals=False)
        for cp in sends:
            cp.start()
        refs[-1][...] = jnp.zeros((8, LANE), F32)

    out = pl.pallas_call(
        body, name=name, in_specs=[HBM_SPEC] * nb + [SEM_SPEC] * 3 + [ANY_SPEC],
        out_specs=[SEM_SPEC] * 2 + [HBM_SPEC] * nd + [TOKEN_SPEC],
        out_shape=_sem_shapes(n_copies2, local=False) + [pltpu.HBM(a.shape, a.dtype) for a in lands] + [_sds((8, LANE))],
        input_output_aliases={ns + i: 2 + i for i in range(nd)},
        compiler_params=pltpu.CompilerParams(has_side_effects=SIDE_EFFECT))(*srcs, *lands, *sems, after)
    return out[:2], out[2:2 + nd], out[-1]


def _exchange_wait(name, sems, srcs, lands, plan, n_copies, after):
    srcs = [] if srcs is None else list(srcs)
    ns, nd = len(srcs), len(lands)
    nb = ns + nd

    def body(*refs):
        land_refs = refs[ns:nb]
        _wait_all(*_exchange_copies(plan, n_copies, refs[:ns] if ns else land_refs, land_refs, *refs[nb:nb + len(sems)]))

    bufs = srcs + list(lands)
    out = pl.pallas_call(
        body, name=name, in_specs=[HBM_SPEC] * nb + [SEM_SPEC] * len(sems) + [ANY_SPEC],
        out_specs=[HBM_SPEC] * nb, out_shape=[pltpu.HBM(a.shape, a.dtype) for a in bufs],
        input_output_aliases={i: i for i in range(nb)},
        compiler_params=pltpu.CompilerParams(has_side_effects=SIDE_EFFECT))(*bufs, *sems, after)
    return out[ns:]


def _all_gather(name, arrs):
    plan = lambda me, peer, k: [(i, None, i, me) for i in range(len(arrs))]
    return _exchange(name, arrs, [_sds((N_DEV,) + a.shape, a.dtype) for a in arrs], plan, len(arrs))


def _mix0_fwd(h, p):
    z = _mm_nt("mix0_in", h, p["ab_w_in"])
    y_a, d = _pool_fwd(z, p["pool_w"], p["pool_scale"])
    y_b = _sgu_fwd(z, p["sgu_ln_g"], p["sgu_ln_b"], p["sgu_w"], p["sgu_bt"])
    ycat = jnp.concatenate([y_a, y_b], axis=1)
    return _mm_nn("mix0_out", ycat, p["ab_w_out"]), (h, z, d, ycat)


def _mix0_bwd(df, saved, p, after):
    h, z, d, ycat = saved
    dycat = _mm_nt("mix0_out_dx", df, p["ab_w_out"], after=after)
    g = {"ab_w_out": _mm_tn("mix0_out_dw", ycat, df, BF16)}
    dz_p, g["pool_w"], g["pool_scale"] = _pool_bwd(dycat, d, p["pool_w"], p["pool_scale"])
    dz_u, dz_v, g["sgu_ln_g"], g["sgu_ln_b"], g["sgu_w"], dbt = _sgu_bwd(
        z, dycat, p["sgu_ln_g"], p["sgu_ln_b"], p["sgu_w"], p["sgu_bt"], p["head_sum"])
    g["sgu_b"] = dbt[:, :NH].T
    dz = jnp.concatenate([dz_p, dz_u, dz_v], axis=1)
    g["ab_w_in"] = _mm_tn("mix0_in_dw", dz, h, BF16)
    return _mm_nn("mix0_in_dx", dz, p["ab_w_in"]), g


def _mix1_fwd(h, p):
    u = _mm_nn("ssm_w_in", h, p["ssm_w_in"])
    x_re, x_im, y = _ssm_scan("ssm_scan_fwd", u, p["wb_bd"], p["lam_bar_re"], p["lam_bar_im"], p["wc_bd"], False)
    g = _ssm_act_fwd(y, u, p["ssm_d"])
    zz = _mm_nn("ssm_glu", g, p["ssm_w_glu"])
    return _glu_fwd(zz), (h, u, x_re, x_im, y, g, zz)


def _mix1_bwd(df, saved, p, after):
    h, u, x_re, x_im, y, g, zz = saved
    gr = {}
    dzz = _glu_bwd(zz, df)
    dg = _mm_nt("ssm_glu_dx", dzz, p["ssm_w_glu"], after=after)
    gr["ssm_w_glu"] = _mm_tn("ssm_glu_dw", g, dzz, BF16)
    dy, gr["ssm_d"] = _ssm_act_bwd(dg, y, u, p["ssm_d"])
    _, _, du_ssm, g_lam_re, g_lam_im, mb_re, mb_im, mc_re, mc_im = _ssm_scan(
        "ssm_scan_bwd", dy, p["wc_bd"], p["lam_bar_re"], -p["lam_bar_im"], p["wb_bd"], True, states=(x_re, x_im), u=u)
    du = _axpy(du_ssm, dy, p["ssm_d"])
    gr["ssm_w_in"] = _mm_tn("ssm_w_in_dw", h, du, BF16)
    dh = _mm_nt("ssm_w_in_dx", du, p["ssm_w_in"])
    per_group = lambda m: m[:, :, :SSM_P].reshape(SSM_G, SSM_N, SSM_P)
    gr["ssm_c_re"] = per_group(mc_re)
    gr["ssm_c_im"] = -per_group(mc_im)
    dlr, dli, ddt, dbr, dbi = _ssm_param_bwd(
        g_lam_re.reshape(SSM_G, SSM_P), g_lam_im.reshape(SSM_G, SSM_P),
        per_group(mb_re).reshape(SSM_G, SSM_N * SSM_P), per_group(mb_im).reshape(SSM_G, SSM_N * SSM_P),
        p["lam_re"], p["lam_im"], p["lam_re_rep"], p["lam_im_rep"], p["log_dt"], p["b_re"], p["b_im"], p["seg"])
    gr["ssm_lam_re"], gr["ssm_lam_im"], gr["ssm_log_dt"] = dlr, dli, ddt[:, 0]
    gr["ssm_b_re"] = dbr.reshape(SSM_G, SSM_N, SSM_P)
    gr["ssm_b_im"] = dbi.reshape(SSM_G, SSM_N, SSM_P)
    return dh, gr


def _ssm_params(lam_re, lam_im, b_re, b_im, c_re, c_im, log_dt):
    wide = lambda b: b.transpose(0, 2, 1).reshape(SSM_G, SSM_N * SSM_P)
    p = {"lam_re": lam_re, "lam_im": lam_im, "log_dt": log_dt.reshape(SSM_G, 1),
         "lam_re_rep": jnp.tile(lam_re, (1, SSM_N)), "lam_im_rep": jnp.tile(lam_im, (1, SSM_N)), "b_re": wide(b_re), "b_im": wide(b_im)}
    lbr, lbi, bbr, bbi = _ssm_prep(lam_re, lam_im, p["lam_re_rep"], p["lam_im_rep"], p["log_dt"], p["b_re"], p["b_im"])
    p["lam_bar_re"], p["lam_bar_im"] = lbr.reshape(1, SSM_L), lbi.reshape(1, SSM_L)
    rows = lambda m: m.reshape(SSM_G * SSM_N, SSM_P)
    p["wb_bd"] = _ssm_block_diag(rows(bbr), rows(bbi))
    p["wc_bd"] = _ssm_block_diag(rows(c_re), rows(-c_im))
    p["seg"] = jnp.tile(jnp.eye(SSM_P, dtype=F32), (SSM_N, 1))
    return p


RES_WEIGHT = (0.5, 1.0, 0.5)


def _local_step(x, tgt, vecs, weights_of, on_part, on_grads):
    def fns(i, w):
        if i % 3 != 1:
            win, wout_of = w
            return ((lambda h: _ffn_fwd(h, win, wout_of)),
                    (lambda df, sv, after: (_ffn_bwd(df, sv, win, wout_of(None), lambda tag, part: on_part(i, tag, part), after), None)))
        if i == 1:
            return (lambda h: _mix0_fwd(h, w)), (lambda df, sv, after: _mix0_bwd(df, sv, w, after))
        return (lambda h: _mix1_fwd(h, w)), (lambda df, sv, after: _mix1_bwd(df, sv, w, after))

    rw = RES_WEIGHT * 2
    saved, bwd = [], []
    f = None
    for i in range(6):
        w, token = weights_of(i, x if i == 0 else f)
        fwd, b = fns(i, w)
        if i == 0:
            h = _prenorm_fwd(x, vecs, 0, token)
        else:
            x, h = _post_pre_fwd(x, f, vecs, i, rw[i - 1], token)
        f, inner = fwd(h)
        saved.append((x, f, inner))
        bwd.append(b)
    loss_row, dx = _loss_fwd_bwd(_postnorm_fwd(x, f, vecs, 5, rw[5]), tgt)
    df, dv_top = _postnorm_bwd(dx, f, vecs, 5, rw[5])
    token = jnp.zeros((8, LANE), F32)
    for i in reversed(range(6)):
        x_i, _, inner = saved[i]
        dh, extra = bwd[i](df, inner, token)
        if i > 0:
            dx, df, dv = _pre_post_bwd(dx, dh, x_i, saved[i - 1][1], vecs, i, rw[i - 1])
        else:
            dx, dv = _prenorm_bwd(dx, dh, x_i, vecs, 0)
        token = on_grads(i, extra, dv, dv_top if i == 5 else None, loss_row)
    return dx


def _pad_rows(v, rows):
    return jnp.pad(v, (0, rows * LANE - v.shape[0])).reshape(rows, LANE)


def _pack(parts):
    flat, layout, off = [], [], 0
    for a in parts:
        n = a.size
        padded = -(-n // LANE) * LANE
        flat.append(jnp.pad(a.reshape(-1).astype(F32), (0, padded - n)))
        layout.append((off, n, a.shape))
        off += padded
    return jnp.concatenate(flat), layout


def _unpack(flat, layout):
    return [flat[off:off + n].reshape(shape) for off, n, shape in layout]


TRANSPOSED = ["ffn_w_in", "ab_w_in", "ssm_b_re", "ssm_b_im"]
WEIGHTS = ['ada_w', 'ada_b', 'norm_pre', 'norm_post', 'ffn_w_in', 'ffn_w_out', 'ab_w_in', 'pool_w', 'pool_scale', 'sgu_ln_g',
           'sgu_ln_b', 'sgu_w', 'sgu_b', 'ab_w_out', 'ssm_w_in', 'ssm_lam_re', 'ssm_lam_im', 'ssm_b_re', 'ssm_b_im', 'ssm_c_re',
           'ssm_c_im', 'ssm_d', 'ssm_log_dt', 'ssm_w_glu']


def kernel(x, c, ada_w, ada_b, norm_pre, norm_post, ffn_w_in, ffn_w_out, ab_w_in, pool_w, pool_scale, sgu_ln_g, sgu_ln_b, sgu_w, sgu_b, ab_w_out, ssm_w_in, ssm_lam_re, ssm_lam_im, ssm_b_re, ssm_b_im, ssm_c_re, ssm_c_im, ssm_d, ssm_log_dt, ssm_w_glu, loss_target, m_ada_w, m_ada_b, m_norm_pre, m_norm_post, m_ffn_w_in, m_ffn_w_out, m_ab_w_in, m_pool_w, m_pool_scale, m_sgu_ln_g, m_sgu_ln_b, m_sgu_w, m_sgu_b, m_ab_w_out, m_ssm_w_in, m_ssm_lam_re, m_ssm_lam_im, m_ssm_b_re, m_ssm_b_im, m_ssm_c_re, m_ssm_c_im, m_ssm_d, m_ssm_log_dt, m_ssm_w_glu, v_ada_w, v_ada_b, v_norm_pre, v_norm_post, v_ffn_w_in, v_ffn_w_out, v_ab_w_in, v_pool_w, v_pool_scale, v_sgu_ln_g, v_sgu_ln_b, v_sgu_w, v_sgu_b, v_ab_w_out, v_ssm_w_in, v_ssm_lam_re, v_ssm_lam_im, v_ssm_b_re, v_ssm_b_im, v_ssm_c_re, v_ssm_c_im, v_ssm_d, v_ssm_log_dt, v_ssm_w_glu):
    args = locals()
    wts = {n: args[n] for n in WEIGHTS}
    mom = {n: args["m_" + n] for n in WEIGHTS}
    var = {n: args["v_" + n] for n in WEIGHTS}
    for n in TRANSPOSED:
        for t in (wts, mom, var):
            t[n] = jnp.swapaxes(t[n], -1, -2)
    me = 4 * lax.axis_index("x") + 2 * lax.axis_index("y") + lax.axis_index("c")
    s = x.shape[1]
    nd = D // N_DEV

    small_in, small_in_layout = _pack([c, norm_pre, norm_post, ssm_d])
    small_rows = -(-small_in.shape[0] // (8 * LANE)) * 8
    (g_small,) = _all_gather("gather_small", [_pad_rows(small_in, small_rows)])
    g_small = g_small.reshape(N_DEV, -1)
    c_all, npre_g, npost_g, sd_g = [jnp.stack([_unpack(g_small[j], small_in_layout)[i] for j in range(N_DEV)]) for i in range(4)]
    c_all = c_all.reshape(N_DEV, D)
    norm_pre_full = npre_g.transpose(1, 2, 0, 3).reshape(2, 3, D)
    norm_post_full = npost_g.transpose(1, 2, 0, 3).reshape(2, 3, D)
    ssm_d_full = sd_g.transpose(1, 0, 2).reshape(1, D)

    nw = ada_w.shape[-1]
    (mod_g,) = _all_gather("gather_mod", [_mod_part(c_all, ada_w)])
    mod = lax.dynamic_index_in_dim(mod_g, me, axis=2, keepdims=False)
    mod = (mod.transpose(1, 0, 2).reshape(2, N_DEV * nw) + ada_b).reshape(2, 3, 3, D)

    w_in_t = wts["ffn_w_in"]
    shards = [[w_in_t[0, 0]], [ffn_w_out[0, 0]], [wts["ab_w_in"][0], ab_w_out[0]], [w_in_t[0, 1]], [ffn_w_out[0, 1]],
              [w_in_t[1, 0]], [ffn_w_out[1, 0]], [ssm_w_in[0], ssm_w_glu[0]], [w_in_t[1, 1]], [ffn_w_out[1, 1]]]
    first_group = {0: 0, 1: 2, 2: 3, 3: 5, 4: 7, 5: 8}
    first_groups = set(first_group.values())
    same_core = (2, 4, 6)

    def gather_plan(n):
        return lambda me_, peer_, k: [(a, None, a, me_) for a in range(n)] if k in (0, 1) + same_core else []

    def relay_plan(n):
        return lambda me_, peer_, k: [(a, me_ ^ kk, a, me_ ^ kk) for kk in same_core for a in range(n)] if k == 1 else []

    gathers, relays = [], {}
    token = mod_g
    for g, group in enumerate(shards):
        group = [a.astype(BF16) for a in group]
        sems, srcs_thru, lands, token = _exchange_start(
            f"gather_start_{g}", group, [_sds((N_DEV,) + a.shape, BF16) for a in group], gather_plan(len(group)), len(group), token)
        gathers.append((sems, srcs_thru, lands))
    mod6 = mod.reshape(6, 3, D)
    vecs = jnp.stack([norm_pre_full.reshape(6, D), mod6[:, 1], mod6[:, 0], norm_post_full.reshape(6, D), mod6[:, 2]]
                     + [jnp.zeros((6, D), F32)] * 3, axis=1)
    vecs = vecs + token[0, 0]

    def relay(g, after):
        sems, srcs_thru, lands = gathers[g]
        n = len(lands)
        relays[g] = _exchange_relay(f"gather_relay_{g}", sems, srcs_thru, lands, gather_plan(n), n, relay_plan(n), 3 * n, after)

    def fetch(g, after):
        if g not in relays:
            relay(g, after)
        sems, lands, token = relays[g]
        n = len(lands)
        got = _exchange_wait(f"gather_wait_{g}", sems, None, lands, relay_plan(n), 3 * n, after)
        if g + 1 in first_groups:
            relay(g + 1, got[0])
            token = relays[g + 1][2]
        return got, token

    head_sum = jnp.repeat(jnp.eye(NH, LANE, dtype=F32), HD, axis=0)
    mix0 = {"pool_w": pool_w[0], "pool_scale": pool_scale, "sgu_ln_g": sgu_ln_g, "sgu_ln_b": sgu_ln_b, "sgu_w": sgu_w[0],
            "sgu_bt": jnp.pad(sgu_b[0].T, ((0, 0), (0, LANE - NH))), "head_sum": head_sum}
    mix1 = _ssm_params(ssm_lam_re[0], ssm_lam_im[0], ssm_b_re[0], ssm_b_im[0], ssm_c_re[0], ssm_c_im[0], ssm_log_dt[0])
    mix1["ssm_d"] = ssm_d_full

    def weights_of(i, x_in):
        g = first_group[i]
        if i % 3 != 1:
            (win,), token = fetch(g, x_in)
            cache = []

            def wout_of(act):
                if not cache:
                    cache.append(fetch(g + 1, act)[0][0])
                return cache[0]

            return (win, wout_of), token
        (a, b), token = fetch(g, x_in)
        if i == 1:
            return dict(mix0, ab_w_in=a.reshape(-1, D), ab_w_out=b.reshape(D, D)), token
        return dict(mix1, ssm_w_in=a.reshape(D, D), ssm_w_glu=b.transpose(1, 0, 2).reshape(D, -1)), token

    def shard_cols(a):
        r = a.shape[0]
        return a.reshape(r, N_DEV, -1).transpose(1, 0, 2)

    scatter_plan = lambda me_, peer_, k: [(0, peer_, 0, me_), (1, peer_, 1, me_)]
    scatter_plan1 = lambda me_, peer_, k: [(0, peer_, 0, me_)]
    scatters = []
    last_token = [jnp.zeros((8, LANE), F32)]
    pieces, mixer, bundles = {}, {}, {}
    bundle_plan = lambda me_, peer_, k: [(0, None, 0, me_)]

    held = {}

    def on_part(i, tag, part):
        if i != 0 and tag == "w_out":
            held[i] = part
            return last_token[0]
        names, parts, plan = (("ffn_" + tag,), [part], scatter_plan1) if i == 0 else (("ffn_w_out", "ffn_w_in"), [held[i], part], scatter_plan)
        sems, srcs_thru, lands, last_token[0] = _exchange_start(
            f"scatter_start_{i}_{tag}", parts, [_sds(a.shape, BF16) for a in parts], plan, len(parts), last_token[0])
        scatters.append((i, names, plan, sems, srcs_thru, lands))
        return last_token[0]
    mix0_names = ["pool_w", "pool_scale", "sgu_ln_g", "sgu_ln_b", "sgu_w", "sgu_b"]
    mix1_names = ["ssm_lam_re", "ssm_lam_im", "ssm_b_re", "ssm_b_im", "ssm_c_re", "ssm_c_im", "ssm_log_dt", "ssm_d"]

    def start_bundle(tag, arrays):
        flat, layout = _pack(arrays)
        rows = -(-flat.shape[0] // (8 * LANE)) * 8
        plan = gather_plan(1) if tag == "a" else bundle_plan
        sems, srcs_thru, lands, last_token[0] = _exchange_start(
            f"small_start_{tag}", [_pad_rows(flat, rows)], [_sds((N_DEV, rows, LANE))], plan, 1, last_token[0])
        bundles[tag] = (sems, srcs_thru, lands, layout)

    def on_grads(i, extra, dv, dv_top, loss_row):
        pieces[i] = dv
        if i == 5:
            pieces["top"] = dv_top
        if i == 4:
            mixer.update({n: extra[n] for n in mix1_names})
        if i == 1:
            mixer.update({n: extra[n] for n in mix0_names})
            start_bundle("a", [jnp.stack([pieces[j] for j in ("top", 5, 4, 3, 2, 1)])] + [mixer[n] for n in mix0_names + mix1_names])
        if i == 0:
            start_bundle("b", [dv, loss_row])
        if i % 3 != 1:
            return last_token[0]
        if i == 1:
            names, parts = ("ab_w_in", "ab_w_out"), [extra["ab_w_in"].reshape(N_DEV, -1, D), extra["ab_w_out"].reshape(N_DEV, nd, D)]
        else:
            names, parts = ("ssm_w_in", "ssm_w_glu"), [extra["ssm_w_in"].reshape(N_DEV, nd, D), shard_cols(extra["ssm_w_glu"])]
        sems, srcs_thru, lands, last_token[0] = _exchange_start(
            f"scatter_start_{i}", parts, [_sds(a.shape, BF16) for a in parts], scatter_plan, 2, last_token[0])
        scatters.append((i, names, scatter_plan, sems, srcs_thru, lands))
        return last_token[0]

    grad_x = _local_step(x[0], loss_target[0], vecs, weights_of, on_part, on_grads)

    out_g, out_d, out_m, out_v = {}, {}, {}, {}
    big_out = {}

    def adam_big(name, recv, n, slot=0, after=None):
        c_ = wts[n].shape[-1]
        big_out[n] = _adamw(name, recv.reshape(recv.shape[0], -1, c_), *[t[n].reshape(-1, c_) for t in (wts, mom, var)],
                            slot=slot, prev=big_out.get(n), after=after)
        return big_out[n][0]

    ffn_slot = {0: 0, 2: 1, 3: 2, 5: 3}

    def land_and_update(entries, after):
        for i, names, plan, sems, srcs_thru, lands in entries:
            recv = _exchange_wait(f"scatter_wait_{i}_{names[0]}", sems, srcs_thru, lands, plan, len(names), after)
            for n, r in zip(names, recv):
                after = adam_big(f"adamw_{n}_{i}", r, n, ffn_slot.get(i, 0), after)
        return after

    after = land_and_update([e for e in scatters if e[0] != 0], last_token[0])

    def landed(tag, g_parts):
        layout = bundles[tag][3]
        off, n, shape = layout[0]
        dmods = g_parts.reshape(N_DEV, -1)[:, off:off + n].reshape((N_DEV,) + shape)
        total = _sum_parts(g_parts)
        return dmods, _unpack(total.reshape(-1), layout), total

    def adam_small(n, g, after=None):
        cols = wts[n].shape[-1]
        res = _adamw(f"adamw_{n}", g.reshape(1, -1, cols), *[t[n].reshape(-1, cols) for t in (wts, mom, var)], after=after)
        for o, arr in zip((out_g, out_d, out_m, out_v), res):
            o[n] = arr.reshape(wts[n].shape)
            if n in TRANSPOSED:
                o[n] = jnp.swapaxes(o[n], -1, -2)
        return res[0]

    sems, srcs_thru, lands, _ = bundles["a"]
    sems, lands, _ = _exchange_relay("small_relay_a", sems, srcs_thru, lands, gather_plan(1), 1, relay_plan(1), 3, after)
    (parts_a,) = _exchange_wait("small_wait_a", sems, None, lands, relay_plan(1), 3, after)
    shells_a, sums_a, after = landed("a", parts_a)
    small = dict(zip(mix0_names + mix1_names, sums_a[1:]))
    def adam_tiny(name, grads, after):
        view = lambda n, a: a.reshape(-1, wts[n].shape[-1])
        items = [(view(n, g),) + tuple(view(n, t[n]) for t in (wts, mom, var)) for n, g in grads.items()]
        for (n, _), item, res in zip(grads.items(), items, _adamw_many(name, items, after)):
            for o, arr in zip((out_g, out_d, out_m, out_v), (item[0],) + res):
                o[n] = arr.reshape(wts[n].shape)
        return res[0]

    tiny = ["pool_scale", "sgu_ln_g", "sgu_ln_b", "sgu_b", "ssm_lam_re", "ssm_lam_im", "ssm_log_dt"]
    for n in [n for n in mix0_names + mix1_names if n not in tiny and n != "ssm_d"]:
        after = adam_small(n, small[n], after)
    after = adam_tiny("adamw_tiny_mixers", dict({n: small[n] for n in tiny},
                                                ssm_d=lax.dynamic_slice_in_dim(small["ssm_d"], me * nd, nd, axis=1)), after)
    def shell_grads(top, blocks, first):
        own_rows = jnp.concatenate([first[..., None, :, :], blocks[..., :0:-1, :, :]], axis=-3)
        next_rows = jnp.concatenate([own_rows[..., 1:, :, :], top[..., None, :, :]], axis=-3)
        dmod_ = jnp.stack([own_rows[..., V_SHIFT, :], own_rows[..., V_SCALE, :], next_rows[..., V_GATE, :]], axis=-2)
        return dmod_, own_rows[..., V_GPRE, :], next_rows[..., V_GPOST, :]

    def ada_w_layer(l, dmod_l, after):
        mine = lax.dynamic_index_in_dim(dmod_l.reshape(N_DEV, N_DEV, nw), me, axis=1, keepdims=False)
        return adam_big(f"adamw_ada_w_{l}", _ada_w_grad(c_all.T, mine[None]), "ada_w", l, after)

    after = ada_w_layer(1, shell_grads(shells_a[:, 0], shells_a, jnp.zeros_like(shells_a[:, 0]))[0][:, 3:], after)
    sems, srcs_thru, lands, _ = bundles["b"]
    (parts_b,) = _exchange_wait("small_wait_b", sems, srcs_thru, lands, bundle_plan, 1, after)
    shell_b, (first_sum, loss_sum), after = landed("b", parts_b)
    loss = loss_sum[0, 0]

    dmod_sum, dg_pre_sum, dg_post_sum = shell_grads(sums_a[0][0], sums_a[0], first_sum)
    own = lambda a: lax.dynamic_slice_in_dim(a, me * nd, nd, axis=1)
    after = adam_tiny("adamw_tiny_shell", {"ada_b": dmod_sum, "norm_pre": own(dg_pre_sum), "norm_post": own(dg_post_sum)}, after)

    after = ada_w_layer(0, shell_grads(shells_a[:, 0], shells_a, shell_b)[0][:, :3], after)

    land_and_update([e for e in scatters if e[0] == 0], after)
    for n, res in big_out.items():
        for o, arr in zip((out_g, out_d, out_m, out_v), res):
            o[n] = arr.reshape(wts[n].shape)
            if n in TRANSPOSED:
                o[n] = jnp.swapaxes(o[n], -1, -2)

    return (loss, grad_x[None], *[out_g[n] for n in WEIGHTS], *[out_d[n] for n in WEIGHTS],
            *[out_m[n] for n in WEIGHTS], *[out_v[n] for n in WEIGHTS])
```

```python
import math

import jax
import jax.numpy as jnp
from jax import lax
from jax.experimental import pallas as pl
from jax.experimental.pallas import tpu as pltpu

F32 = jnp.float32
BF16 = jnp.bfloat16
MESH = pl.DeviceIdType.MESH
HIGHEST = lax.Precision.HIGHEST

N_DEV = 8
D = 1024
D_FF = 2816
FSH = 2 * D_FF // N_DEV
EPS = 1e-6
POOL_WINDOWS = (2, 4, 8, 16)
HD = 128
NH = 4
SSM_G, SSM_P, SSM_N = 64, 64, 16
SSM_L = SSM_G * SSM_P
LR, B1, B2, ADAM_EPS, WD, STEP = 0.001, 0.9, 0.999, 1e-08, 0.01, 10
GELU_C = math.sqrt(2.0 / math.pi)
VMEM_LIMIT_BYTES = 48 * 1024 * 1024
LANE = 128


def _pc(body, name, grid, in_specs, out_specs, out_shape, scratch=()):
    return pl.pallas_call(
        body, name=name, grid=grid, in_specs=in_specs, out_specs=out_specs, out_shape=out_shape,
        scratch_shapes=list(scratch),
        compiler_params=pltpu.CompilerParams(dimension_semantics=("arbitrary",) * len(grid),
                                             vmem_limit_bytes=VMEM_LIMIT_BYTES))


def _sds(shape, dtype=F32):
    return jax.ShapeDtypeStruct(tuple(shape), dtype)


def _bf(v):
    return v if v.dtype == BF16 else v.astype(BF16)


def _row_spec(ts, width, col=0):
    return pl.BlockSpec((ts, width), lambda t, _c=col: (t, _c))


def _vec_spec(width, col=0):
    return pl.BlockSpec((1, width), lambda t, _c=col: (0, _c))


def _mm(name, a, b, contract, grid, a_spec, b_spec, o_spec, out_shape, acc_axis=None, after=None):
    dn = (contract, ((), ()))

    def body(a_ref, b_ref, *rest):
        o_ref = rest[-1]
        r = lax.dot_general(_bf(a_ref[...]), _bf(b_ref[...]), dn, preferred_element_type=F32)
        if acc_axis is None:
            o_ref[...] = r.astype(o_ref.dtype)
        else:
            k = pl.program_id(acc_axis)

            @pl.when(k == 0)
            def _():
                o_ref[...] = r

            @pl.when(k > 0)
            def _():
                o_ref[...] += r

    if after is None:
        return _pc(body, name, grid, [a_spec, b_spec], o_spec, out_shape)(a, b)
    return _pc(body, name, grid, [a_spec, b_spec, pl.BlockSpec(memory_space=pl.ANY)], o_spec, out_shape)(a, b, after)


def _mm_sum(name, a, b, ts, after=None):
    nj, s, k = a.shape
    n = b.shape[2]

    def body(a_ref, b_ref, *rest):
        acc = jnp.dot(a_ref[0], b_ref[0], preferred_element_type=F32)
        for j in range(1, nj):
            acc = acc + jnp.dot(a_ref[j], b_ref[j], preferred_element_type=F32)
        rest[-1][...] = acc

    specs = [pl.BlockSpec((nj, ts, k), lambda t: (0, t, 0)), pl.BlockSpec((nj, k, n), lambda t: (0, 0, 0))]
    args = (a, b)
    if after is not None:
        specs, args = specs + [pl.BlockSpec(memory_space=pl.ANY)], args + (after,)
    return _pc(body, name, (s // ts,), specs, pl.BlockSpec((ts, n), lambda t: (t, 0)), _sds((s, n)))(*args)


def _tile(s):
    return min(s, 1024)


def _div_tile(n, cap=1024):
    t = min(n, cap) // LANE * LANE
    while n % t:
        t -= LANE
    return t


def _mm_nn(name, a, b, out_dtype=F32):
    s, k = a.shape
    n = b.shape[1]
    ts, tn = _tile(s), _div_tile(n)
    return _mm(name, a, b, ((1,), (0,)), (n // tn, s // ts),
               pl.BlockSpec((ts, k), lambda j, t: (t, 0)), pl.BlockSpec((k, tn), lambda j, t: (0, j)),
               pl.BlockSpec((ts, tn), lambda j, t: (t, j)), _sds((s, n), out_dtype))


def _mm_nt(name, a, b, out_dtype=F32, after=None):
    s, n = a.shape
    k = b.shape[0]
    ts, tk = _tile(s), _div_tile(k)
    return _mm(name, a, b, ((1,), (1,)), (k // tk, s // ts),
               pl.BlockSpec((ts, n), lambda j, t: (t, 0)), pl.BlockSpec((tk, n), lambda j, t: (j, 0)),
               pl.BlockSpec((ts, tk), lambda j, t: (t, j)), _sds((s, k), out_dtype), after=after)


def _mm_tn(name, a, b, out_dtype=F32, tm=512, tn=512):
    s, m = a.shape
    n = b.shape[1]
    tm, tn = min(m, tm), min(n, tn)
    return _mm(name, a, b, ((0,), (0,)), (m // tm, n // tn),
               pl.BlockSpec((s, tm), lambda i, j: (0, i)), pl.BlockSpec((s, tn), lambda i, j: (0, j)),
               pl.BlockSpec((tm, tn), lambda i, j: (i, j)), _sds((m, n), out_dtype))


def _rstd(v):
    return lax.rsqrt(jnp.mean(v * v, axis=-1, keepdims=True) + EPS)


V_GPRE, V_SCALE, V_SHIFT, V_GPOST, V_GATE = range(5)


def _vrow(v, r):
    return v[r:r + 1]


def _vblock(i):
    return pl.BlockSpec((None, 8, D), lambda t: (i, 0, 0))


def _head(xv, v):
    return ((xv * _rstd(xv) * _vrow(v, V_GPRE)) * (1.0 + _vrow(v, V_SCALE)) + _vrow(v, V_SHIFT)).astype(BF16)


def _tail(xv, fv, v, rw):
    return xv + (rw * _vrow(v, V_GATE)) * (fv * _rstd(fv) * _vrow(v, V_GPOST))


def _prenorm_fwd(x, vecs, i, after):
    s = x.shape[0]
    ts = min(s, 512)

    def body(x_ref, v_ref, after_ref, h_ref):
        h_ref[...] = _head(x_ref[...], v_ref[...])

    return _pc(body, "prenorm_fwd", (s // ts,), [_row_spec(ts, D), _vblock(i), pl.BlockSpec(memory_space=pl.ANY)], _row_spec(ts, D),
               _sds((s, D), BF16))(x, vecs, after)


def _postnorm_fwd(x, f, vecs, i, rw):
    s = x.shape[0]
    ts = min(s, 512)

    def body(x_ref, f_ref, v_ref, o_ref):
        o_ref[...] = _tail(x_ref[...], f_ref[...], v_ref[...], rw)

    return _pc(body, "postnorm_fwd", (s // ts,), [_row_spec(ts, D)] * 2 + [_vblock(i)], _row_spec(ts, D), _sds((s, D)))(x, f, vecs)


def _post_pre_fwd(x, f, vecs, i, rw_prev, after):
    s = x.shape[0]
    ts = min(s, 512)

    def body(x_ref, f_ref, vp_ref, vc_ref, after_ref, xo_ref, h_ref):
        xv = _tail(x_ref[...], f_ref[...], vp_ref[...], rw_prev)
        xo_ref[...] = xv
        h_ref[...] = _head(xv, vc_ref[...])

    return _pc(body, "post_pre_fwd", (s // ts,),
               [_row_spec(ts, D)] * 2 + [_vblock(i - 1), _vblock(i), pl.BlockSpec(memory_space=pl.ANY)], [_row_spec(ts, D)] * 2,
               [_sds((s, D)), _sds((s, D), BF16)])(x, f, vecs, vecs, after)


def _zero_at_first(first, *refs):
    @pl.when(first)
    def _():
        for ref in refs:
            ref[...] = jnp.zeros_like(ref)


def _acc(ref, first, v):
    @pl.when(first)
    def _():
        ref[...] = v

    @pl.when(jnp.logical_not(first))
    def _():
        ref[...] += v


def _colsum(v):
    return jnp.sum(v, axis=0, keepdims=True)


def _tail_bwd(do, fv, v, rw, dv_ref):
    gv = _vrow(v, V_GPOST)
    r = _rstd(fv)
    fn = fv * r
    dv_ref[V_GATE:V_GATE + 1, :] += rw * _colsum(do * (fn * gv))
    dy = (rw * _vrow(v, V_GATE)) * do
    dv_ref[V_GPOST:V_GPOST + 1, :] += _colsum(dy * fn)
    dfn = dy * gv
    return (r * (dfn - fn * jnp.mean(dfn * fn, axis=-1, keepdims=True))).astype(BF16)


def _head_bwd(do, dhv, xv, v, dv_ref):
    gv = _vrow(v, V_GPRE)
    r = _rstd(xv)
    xn = xv * r
    dv_ref[V_SHIFT:V_SHIFT + 1, :] += _colsum(dhv)
    dv_ref[V_SCALE:V_SCALE + 1, :] += _colsum(dhv * (xn * gv))
    dhp = dhv * (1.0 + _vrow(v, V_SCALE))
    dv_ref[V_GPRE:V_GPRE + 1, :] += _colsum(dhp * xn)
    dxn = dhp * gv
    return do + r * (dxn - xn * jnp.mean(dxn * xn, axis=-1, keepdims=True))


DV_SPEC = pl.BlockSpec((8, D), lambda t: (0, 0))


def _postnorm_bwd(dout, f, vecs, i, rw):
    s = dout.shape[0]
    ts = min(s, 512)

    def body(do_ref, f_ref, v_ref, df_ref, dv_ref):
        _zero_at_first(pl.program_id(0) == 0, dv_ref)
        df_ref[...] = _tail_bwd(do_ref[...], f_ref[...], v_ref[...], rw, dv_ref)

    return _pc(body, "postnorm_bwd", (s // ts,), [_row_spec(ts, D)] * 2 + [_vblock(i)], [_row_spec(ts, D), DV_SPEC],
               [_sds((s, D), BF16), _sds((8, D))])(dout, f, vecs)


def _prenorm_bwd(dout, dh, x, vecs, i):
    s = dout.shape[0]
    ts = min(s, 512)

    def body(do_ref, dh_ref, x_ref, v_ref, dx_ref, dv_ref):
        _zero_at_first(pl.program_id(0) == 0, dv_ref)
        dx_ref[...] = _head_bwd(do_ref[...], dh_ref[...], x_ref[...], v_ref[...], dv_ref)

    return _pc(body, "prenorm_bwd", (s // ts,), [_row_spec(ts, D)] * 3 + [_vblock(i)], [_row_spec(ts, D), DV_SPEC],
               [_sds((s, D)), _sds((8, D))])(dout, dh, x, vecs)


def _pre_post_bwd(dout, dh, x, f_prev, vecs, i, rw_prev):
    s = dout.shape[0]
    ts = min(s, 256)

    def body(do_ref, dh_ref, x_ref, f_ref, vc_ref, vp_ref, dx_ref, df_ref, dv_ref):
        _zero_at_first(pl.program_id(0) == 0, dv_ref)
        dx = _head_bwd(do_ref[...], dh_ref[...], x_ref[...], vc_ref[...], dv_ref)
        dx_ref[...] = dx
        df_ref[...] = _tail_bwd(dx, f_ref[...], vp_ref[...], rw_prev, dv_ref)

    rows = _row_spec(ts, D)
    return _pc(body, "pre_post_bwd", (s // ts,), [rows] * 4 + [_vblock(i), _vblock(i - 1)], [rows, rows, DV_SPEC],
               [_sds((s, D)), _sds((s, D), BF16), _sds((8, D))])(dout, dh, x, f_prev, vecs, vecs)


def _loss_fwd_bwd(y, tgt):
    s = y.shape[0]
    ts = min(s, 512)
    nt = s // ts

    def body(y_ref, t_ref, loss_ref, dy_ref, acc_ref):
        t = pl.program_id(0)
        e = y_ref[...] - t_ref[...]
        dy_ref[...] = e * (1.0 / D)
        _acc(acc_ref, t == 0, _colsum(e * e))

        @pl.when(t == nt - 1)
        def _():
            loss_ref[...] = jnp.full((1, LANE), 0.5 / D, F32) * jnp.sum(acc_ref[...])

    return _pc(body, "loss", (nt,), [_row_spec(ts, D)] * 2,
               [pl.BlockSpec((1, LANE), lambda t: (0, 0)), _row_spec(ts, D)],
               [_sds((1, LANE)), _sds((s, D))], scratch=[pltpu.VMEM((1, D), F32)])(y, tgt)


def _sigmoid(v):
    return 1.0 / (1.0 + jnp.exp(-v))


def _ffn_in_swiglu(h, win):
    s = h.shape[0]
    ts = _tile(s)
    nt = (((1,), (1,)), ((), ()))

    def body(h_ref, wa_ref, wb_ref, fac_ref, act_ref):
        hv = h_ref[...]
        a = lax.dot_general(hv, wa_ref[...], nt, preferred_element_type=F32)
        b = lax.dot_general(hv, wb_ref[...], nt, preferred_element_type=F32)
        sg = _sigmoid(a)
        silu = a * sg
        fac_ref[0] = (b * (sg * (1.0 + a * (1.0 - sg)))).astype(BF16)
        fac_ref[1] = silu.astype(BF16)
        act_ref[...] = (silu * b).astype(BF16)

    return _pc(body, "ffn_in", (4, s // ts),
               [pl.BlockSpec((ts, D), lambda k, t: (t, 0)), pl.BlockSpec((None, FSH, D), lambda k, t: (k, 0, 0)),
                pl.BlockSpec((None, FSH, D), lambda k, t: (k + 4, 0, 0))],
               [pl.BlockSpec((2, None, ts, FSH), lambda k, t: (0, k, t, 0)), pl.BlockSpec((None, ts, FSH), lambda k, t: (k, t, 0))],
               [_sds((2, 4, s, FSH), BF16), _sds((4, s, FSH), BF16)])(h, win, win)


def _ffn_out_dx_swiglu(df, wout, fac, after):
    s = df.shape[0]
    ts = _tile(s)
    nt = (((1,), (1,)), ((), ()))

    def body(df_ref, w_ref, fac_ref, after_ref, o_ref):
        d = lax.dot_general(df_ref[...], w_ref[...], nt, preferred_element_type=F32)
        o_ref[0] = (d * fac_ref[0]).astype(BF16)
        o_ref[1] = (d * fac_ref[1]).astype(BF16)

    spec = pl.BlockSpec((2, None, ts, FSH), lambda k, t: (0, k, t, 0))
    out = _pc(body, "ffn_out_dx", (4, s // ts),
              [pl.BlockSpec((ts, D), lambda k, t: (t, 0)), pl.BlockSpec((None, FSH, D), lambda k, t: (k, 0, 0)), spec,
               pl.BlockSpec(memory_space=pl.ANY)],
              spec, _sds((2, 4, s, FSH), BF16))(df, wout, fac, after)
    return out.reshape(N_DEV, s, FSH)


def _ffn_fwd(h, win, wout_of):
    s = h.shape[0]
    fac, act = _ffn_in_swiglu(h, win)
    f = _mm_sum("ffn_out", act, wout_of(act).reshape(4, FSH, D), min(s, 512))
    return f, (h, fac, act)


def _ffn_bwd(df, saved, win, wout, send, after):
    h, fac, act = saved
    s = h.shape[0]
    ts = s
    wout = wout.reshape(4, FSH, D)
    dwout = _mm("ffn_out_dw", act, df, ((0,), (0,)), (4, 2),
                pl.BlockSpec((None, s, FSH), lambda k, j: (k, 0, 0)), pl.BlockSpec((s, D // 2), lambda k, j: (0, j)),
                pl.BlockSpec((None, FSH, D // 2), lambda k, j: (k, 0, j)), _sds((4, FSH, D), BF16), after=after)
    dz = _ffn_out_dx_swiglu(df, wout, fac, send("w_out", dwout.reshape(N_DEV, D_FF // N_DEV, D)))
    dwin = _mm("ffn_in_dw", dz, h, ((0,), (0,)), (N_DEV, 2),
               pl.BlockSpec((None, s, FSH), lambda j, i: (j, 0, 0)), pl.BlockSpec((s, D // 2), lambda j, i: (0, i)),
               pl.BlockSpec((None, FSH, D // 2), lambda j, i: (j, 0, i)), _sds((N_DEV, FSH, D), BF16))
    return _mm_sum("ffn_in_dx", dz, win, min(s, 512), after=send("w_in", dwin))


def _shift_rows(v, k, row, s, back):
    if back:
        return jnp.where(row < s - k, pltpu.roll(v, s - k, 0), 0.0)
    return jnp.where(row >= k, pltpu.roll(v, k, 0), 0.0)


def _window_sum(v, w, row, s, back):
    k = 1
    while k < w:
        v = v + _shift_rows(v, k, row, s, back)
        k *= 2
    return v


def _pool_fwd(z, pool_w, pool_scale):
    s = z.shape[0]

    def body(z_ref, w_ref, sc_ref, y_ref, d_ref):
        row = lax.broadcasted_iota(jnp.int32, (s, HD), 0)
        for g, w in enumerate(POOL_WINDOWS):
            sl = slice(g * HD, (g + 1) * HD)
            a = z_ref[:, sl]
            cnt = jnp.minimum(row + 1, w).astype(F32)
            d = (_window_sum(a, w, row, s, False) / cnt - a).astype(BF16)
            d_ref[:, sl] = d
            y = jnp.dot(d, _bf(w_ref[g]), preferred_element_type=F32)
            y_ref[:, sl] = (y * sc_ref[:, sl]).astype(BF16)

    return _pc(body, "pool_fwd", (1,),
               [pl.BlockSpec((s, NH * HD), lambda i: (0, 0)), pl.BlockSpec((NH, HD, HD), lambda i: (0, 0, 0)),
                pl.BlockSpec((1, NH * HD), lambda i: (0, 0))],
               [pl.BlockSpec((s, NH * HD), lambda i: (0, 0))] * 2,
               [_sds((s, NH * HD), BF16)] * 2)(z, pool_w, pool_scale)


def _pool_bwd(dy, d, pool_w, pool_scale):
    s = dy.shape[0]

    def body(dy_ref, d_ref, w_ref, sc_ref, dz_ref, dw_ref, dsc_ref):
        row = lax.broadcasted_iota(jnp.int32, (s, HD), 0)
        for g, w in enumerate(POOL_WINDOWS):
            sl = slice(g * HD, (g + 1) * HD)
            dyg, dg, wg = dy_ref[:, sl], d_ref[:, sl], _bf(w_ref[g])
            yraw = jnp.dot(dg, wg, preferred_element_type=F32)
            dsc_ref[:, sl] = _colsum(dyg * yraw)
            dyr = _bf(dyg * sc_ref[:, sl])
            dw_ref[g] = lax.dot_general(dg, dyr, (((0,), (0,)), ((), ())), preferred_element_type=F32)
            dd = lax.dot_general(dyr, wg, (((1,), (1,)), ((), ())), preferred_element_type=F32)
            cnt = jnp.minimum(row + 1, w).astype(F32)
            dz_ref[:, sl] = (_window_sum(dd / cnt, w, row, s, True) - dd).astype(BF16)

    return _pc(body, "pool_bwd", (1,),
               [pl.BlockSpec((s, NH * HD), lambda i: (0, 0)), pl.BlockSpec((s, NH * HD), lambda i: (0, 0)),
                pl.BlockSpec((NH, HD, HD), lambda i: (0, 0, 0)), pl.BlockSpec((1, NH * HD), lambda i: (0, 0))],
               [pl.BlockSpec((s, NH * HD), lambda i: (0, 0)), pl.BlockSpec((NH, HD, HD), lambda i: (0, 0, 0)),
                pl.BlockSpec((1, NH * HD), lambda i: (0, 0))],
               [_sds((s, NH * HD), BF16), _sds((NH, HD, HD)), _sds((1, NH * HD))])(dy, d, pool_w, pool_scale)


def _gelu(v):
    return 0.5 * v * (1.0 + jnp.tanh(GELU_C * (v + 0.044715 * (v * v * v))))


def _gelu_and_grad(v):
    t = jnp.tanh(GELU_C * (v + 0.044715 * (v * v * v)))
    return 0.5 * v * (1.0 + t), 0.5 * (1.0 + t) + 0.5 * v * (1.0 - t * t) * (GELU_C * (1.0 + 3.0 * 0.044715 * (v * v)))


def _gelu_grad(v):
    return _gelu_and_grad(v)[1]


def _causal_mask():
    return lax.broadcasted_iota(jnp.int32, (HD, HD), 0) >= lax.broadcasted_iota(jnp.int32, (HD, HD), 1)


def _sgu_specs():
    w = NH * HD
    return [pl.BlockSpec((HD, w), lambda c: (c, 1)), pl.BlockSpec((HD, w), lambda c: (c, 2)),
            pl.BlockSpec((1, w), lambda c: (0, 0)), pl.BlockSpec((1, w), lambda c: (0, 0)),
            pl.BlockSpec((NH, HD, HD), lambda c: (0, 0, 0)), pl.BlockSpec((HD, LANE), lambda c: (0, 0))]


def _sgu_head(v, lng_ref, lnb_ref, w_ref, h):
    sl = slice(h * HD, (h + 1) * HD)
    vh = v[:, sl]
    xc = vh - jnp.mean(vh, axis=-1, keepdims=True)
    rs = lax.rsqrt(jnp.mean(xc * xc, axis=-1, keepdims=True) + EPS)
    vhat = xc * rs
    vn = _bf(vhat * lng_ref[:, sl] + lnb_ref[:, sl])
    wc = _bf(jnp.where(_causal_mask(), w_ref[h], 0.0))
    return sl, rs, vhat, vn, wc


def _sgu_fwd(z, ln_g, ln_b, sgu_w, sgu_bt):
    s = z.shape[0]

    def body(zu_ref, zv_ref, lng_ref, lnb_ref, w_ref, bt_ref, y_ref):
        u, v = _gelu(zu_ref[...]), _gelu(zv_ref[...])
        for h in range(NH):
            sl, _, _, vn, wc = _sgu_head(v, lng_ref, lnb_ref, w_ref, h)
            sp = jnp.dot(wc, vn, preferred_element_type=F32) + bt_ref[:, h:h + 1]
            y_ref[:, sl] = (u[:, sl] * sp).astype(BF16)

    return _pc(body, "sgu_fwd", (s // HD,), _sgu_specs(), pl.BlockSpec((HD, NH * HD), lambda c: (c, 0)),
               _sds((s, NH * HD), BF16))(z, z, ln_g, ln_b, sgu_w, sgu_bt)


def _sgu_bwd(z, dy, ln_g, ln_b, sgu_w, sgu_bt, head_sum):
    s = z.shape[0]
    w = NH * HD
    nc = s // HD

    def body(zu_ref, zv_ref, lng_ref, lnb_ref, w_ref, bt_ref, dy_ref, hs_ref,
             dzu_ref, dzv_ref, dlng_ref, dlnb_ref, dw_ref, dbt_ref, dsacc_ref):
        c = pl.program_id(0)
        _zero_at_first(c == 0, dsacc_ref, dw_ref, dlng_ref, dlnb_ref)
        zu, zv = zu_ref[...], zv_ref[...]
        (u, gu), (v, gv) = _gelu_and_grad(zu), _gelu_and_grad(zv)
        dyv = dy_ref[...]
        ds = dyv * u
        dsacc_ref[...] += ds
        for h in range(NH):
            sl, rs, vhat, vn, wc = _sgu_head(v, lng_ref, lnb_ref, w_ref, h)
            sp = jnp.dot(wc, vn, preferred_element_type=F32) + bt_ref[:, h:h + 1]
            dzu_ref[:, sl] = (dyv[:, sl] * sp * gu[:, sl]).astype(BF16)
            dsh = _bf(ds[:, sl])
            dwh = lax.dot_general(dsh, vn, (((1,), (1,)), ((), ())), preferred_element_type=F32)
            dw_ref[h] += jnp.where(_causal_mask(), dwh, 0.0)
            dvn = lax.dot_general(wc, dsh, (((0,), (0,)), ((), ())), preferred_element_type=F32)
            dlng_ref[:, sl] += _colsum(dvn * vhat)
            dlnb_ref[:, sl] += _colsum(dvn)
            dvh = dvn * lng_ref[:, sl]
            dv = rs * (dvh - jnp.mean(dvh, axis=-1, keepdims=True) - vhat * jnp.mean(dvh * vhat, axis=-1, keepdims=True))
            dzv_ref[:, sl] = (dv * gv[:, sl]).astype(BF16)

        @pl.when(c == nc - 1)
        def _():
            dbt_ref[...] = jnp.dot(dsacc_ref[...], hs_ref[...], preferred_element_type=F32, precision=HIGHEST)

    outs = _pc(body, "sgu_bwd", (nc,),
               _sgu_specs() + [pl.BlockSpec((HD, w), lambda c: (c, 1)), pl.BlockSpec((w, LANE), lambda c: (0, 0))],
               [pl.BlockSpec((HD, w), lambda c: (c, 0))] * 2 + [pl.BlockSpec((1, w), lambda c: (0, 0))] * 2
               + [pl.BlockSpec((NH, HD, HD), lambda c: (0, 0, 0)), pl.BlockSpec((HD, LANE), lambda c: (0, 0))],
               [_sds((s, w), BF16)] * 2 + [_sds((1, w))] * 2 + [_sds((NH, HD, HD)), _sds((HD, LANE))],
               scratch=[pltpu.VMEM((HD, w), F32)])(z, z, ln_g, ln_b, sgu_w, sgu_bt, dy, head_sum)
    return outs


def _cmul(ar, ai, br, bi):
    return ar * br - ai * bi, ar * bi + ai * br


def _ssm_prep(lam_re, lam_im, lam_re_rep, lam_im_rep, log_dt, b_re, b_im):
    def disc(lr, li, dt):
        mag = jnp.exp(lr * dt)
        return mag * jnp.cos(li * dt), mag * jnp.sin(li * dt)

    def body(lr_ref, li_ref, lrr_ref, lir_ref, ldt_ref, br_ref, bi_ref, or_ref, oi_ref, bbr_ref, bbi_ref):
        dt = jnp.exp(ldt_ref[...])
        or_ref[...], oi_ref[...] = disc(lr_ref[...], li_ref[...], dt)
        lr, li = lrr_ref[...], lir_ref[...]
        er, ei = disc(lr, li, dt)
        den = lr * lr + li * li
        kr = ((er - 1.0) * lr + ei * li) / den
        ki = (ei * lr - (er - 1.0) * li) / den
        bbr_ref[...], bbi_ref[...] = _cmul(kr, ki, br_ref[...], bi_ref[...])

    small = pl.BlockSpec((SSM_G, SSM_P), lambda i: (0, 0))
    wide = pl.BlockSpec((SSM_G, SSM_P * SSM_N), lambda i: (0, 0))
    col = pl.BlockSpec((SSM_G, 1), lambda i: (0, 0))
    return _pc(body, "ssm_prep", (1,), [small, small, wide, wide, col, wide, wide], [small, small, wide, wide],
               [_sds((SSM_G, SSM_P))] * 2 + [_sds((SSM_G, SSM_P * SSM_N))] * 2)(
        lam_re, lam_im, lam_re_rep, lam_im_rep, log_dt, b_re, b_im)


def _ssm_param_bwd(g_lam_re, g_lam_im, g_bb_re, g_bb_im, lam_re, lam_im, lam_re_rep, lam_im_rep, log_dt, b_re, b_im, seg):
    def body(glr_ref, gli_ref, gbr_ref, gbi_ref, lr_ref, li_ref, lrr_ref, lir_ref, ldt_ref, br_ref, bi_ref, seg_ref,
             dlr_ref, dli_ref, ddt_ref, dbr_ref, dbi_ref):
        dt = jnp.exp(ldt_ref[...])
        lr, li = lrr_ref[...], lir_ref[...]
        mag = jnp.exp(lr * dt)
        er, ei = mag * jnp.cos(li * dt), mag * jnp.sin(li * dt)
        den = lr * lr + li * li
        kr = ((er - 1.0) * lr + ei * li) / den
        ki = (ei * lr - (er - 1.0) * li) / den
        gbr, gbi = gbr_ref[...], gbi_ref[...]
        dbr_ref[...], dbi_ref[...] = _cmul(kr, -ki, gbr, gbi)
        tr, ti = _cmul(br_ref[...], -bi_ref[...], gbr, gbi)
        gkr = jnp.dot(tr, seg_ref[...], preferred_element_type=F32, precision=HIGHEST)
        gki = jnp.dot(ti, seg_ref[...], preferred_element_type=F32, precision=HIGHEST)
        lr, li = lr_ref[...], li_ref[...]
        mag = jnp.exp(lr * dt)
        er, ei = mag * jnp.cos(li * dt), mag * jnp.sin(li * dt)
        den = lr * lr + li * li
        ir, ii = lr / den, -li / den
        kr, ki = _cmul(er - 1.0, ei, ir, ii)
        ar, ai = _cmul(ir, -ii, gkr, gki)
        glr, gli = glr_ref[...] + ar, gli_ref[...] + ai
        qr, qi = _cmul(kr, ki, ir, ii)
        g1r, g1i = _cmul(-qr, qi, gkr, gki)
        g2r, g2i = _cmul(dt * er, -dt * ei, glr, gli)
        dlr_ref[...] = g1r + g2r
        dli_ref[...] = g1i + g2i
        wr, wi = _cmul(lr, li, er, ei)
        g_dt = jnp.sum(wr * glr + wi * gli, axis=-1, keepdims=True)
        ddt_ref[...] = jnp.broadcast_to(dt * g_dt, (SSM_G, LANE))

    small = pl.BlockSpec((SSM_G, SSM_P), lambda i: (0, 0))
    wide = pl.BlockSpec((SSM_G, SSM_P * SSM_N), lambda i: (0, 0))
    col = pl.BlockSpec((SSM_G, 1), lambda i: (0, 0))
    segs = pl.BlockSpec((SSM_P * SSM_N, SSM_P), lambda i: (0, 0))
    return _pc(body, "ssm_param_bwd", (1,), [small, small, wide, wide, small, small, wide, wide, col, wide, wide, segs],
               [small, small, pl.BlockSpec((SSM_G, LANE), lambda i: (0, 0)), wide, wide],
               [_sds((SSM_G, SSM_P))] * 2 + [_sds((SSM_G, LANE))] + [_sds((SSM_G, SSM_P * SSM_N))] * 2)(
        g_lam_re, g_lam_im, g_bb_re, g_bb_im, lam_re, lam_im, lam_re_rep, lam_im_rep, log_dt, b_re, b_im, seg)


SCAN_LANES = 512
SCAN_ROWS = 8


SCAN_GROUPS = SCAN_LANES // SSM_P
SCAN_COLS = SCAN_GROUPS * SSM_N
SCAN_CHUNK = 1024


def _ssm_scan(name, v, w_in, lam_re, lam_im, w_out, reverse, states=None, u=None):
    s = v.shape[0]
    ln, rows, ch = SCAN_LANES, SCAN_ROWS, min(SCAN_CHUNK, s)
    nch, ntile = s // ch, ch // rows
    nt_dims = (((1,), (1,)), ((), ()))
    with_sum = states is not None
    tn_dims = (((0,), (0,)), ((), ()))

    def body(*refs):
        v_ref, win_ref, lr_ref, li_ref, wout_ref = refs[:5]
        if with_sum:
            n_in = 8
            y_ref, sum_refs = refs[n_in], refs[n_in + 1:n_in + 7]
            br_s, bi_s, mb_s, mc_s, or_ref, oi_ref = refs[n_in + 7:]
            mb_s[...] = jnp.zeros_like(mb_s)
            mc_s[...] = jnp.zeros_like(mc_s)
        else:
            n_in = 5
            or_ref, oi_ref, y_ref, br_s, bi_s = refs[n_in:]
        l1 = (lr_ref[...], li_ref[...])
        pw = [l1]
        for _ in range(rows - 1):
            pw.append(_cmul(*pw[-1], *l1))
        row = lax.broadcasted_iota(jnp.int32, (rows, ln), 0)
        expo = (rows - row) if reverse else (row + 1)
        pr = jnp.zeros((rows, ln), F32)
        pi = jnp.zeros((rows, ln), F32)
        for e in range(1, rows + 1):
            pr = jnp.where(expo == e, pw[e - 1][0], pr)
            pi = jnp.where(expo == e, pw[e - 1][1], pi)
        lk = {}
        for k in (1, 2, 4):
            keep = (row < rows - k) if reverse else (row >= k)
            lk[k] = (jnp.where(keep, pw[k - 1][0], 0.0), jnp.where(keep, pw[k - 1][1], 0.0))

        def chunk(c, carry):
            q0 = pl.multiple_of(((nch - 1 - c) if reverse else c) * ch, ch)
            b = jnp.dot(_bf(v_ref[pl.ds(q0, ch), :]), win_ref[...], preferred_element_type=F32)
            br_s[...] = b[:, :ln]
            bi_s[...] = b[:, ln:]

            def step(i, carry):
                cr, ci = carry[:2]
                r0 = pl.multiple_of(((ntile - 1 - i) if reverse else i) * rows, rows)
                xr, xi = br_s[pl.ds(r0, rows), :], bi_s[pl.ds(r0, rows), :]
                for k in (1, 2, 4):
                    shift = rows - k if reverse else k
                    ar, ai = _cmul(lk[k][0], lk[k][1], pltpu.roll(xr, shift, 0), pltpu.roll(xi, shift, 0))
                    xr, xi = xr + ar, xi + ai
                ar, ai = _cmul(pr, pi, cr, ci)
                xr, xi = xr + ar, xi + ai
                g0 = pl.multiple_of(q0 + r0, rows)
                or_ref[pl.ds(g0, rows), :] = xr
                oi_ref[pl.ds(g0, rows), :] = xi
                if not with_sum:
                    return (xr[rows - 1:rows], xi[rows - 1:rows]) if not reverse else (xr[0:1], xi[0:1])
                nr = jnp.where(row == rows - 1, cr, pltpu.roll(xr, rows - 1, 0))
                ni = jnp.where(row == rows - 1, ci, pltpu.roll(xi, rows - 1, 0))
                sr, si = refs[5][pl.ds(g0, rows), :], refs[6][pl.ds(g0, rows), :]
                return xr[0:1], xi[0:1], carry[2] + (sr * nr + si * ni), carry[3] + (sr * ni - si * nr)

            carry = lax.fori_loop(0, ntile, step, carry)
            if with_sum:
                rows_c = pl.ds(q0, ch)
                uc, vc = _bf(refs[7][rows_c, :]), _bf(v_ref[rows_c, :])
                for scr, left, (right_re, right_im) in ((mb_s, uc, (or_ref, oi_ref)), (mc_s, vc, (refs[5], refs[6]))):
                    scr[:, :ln] += lax.dot_general(left, _bf(right_re[rows_c, :]), tn_dims, preferred_element_type=F32)
                    scr[:, ln:] += lax.dot_general(left, _bf(right_im[rows_c, :]), tn_dims, preferred_element_type=F32)
            w = wout_ref[...]
            y_ref[pl.ds(q0, ch), :] = (
                lax.dot_general(_bf(or_ref[pl.ds(q0, ch), :]), w[:, :ln], nt_dims, preferred_element_type=F32)
                + lax.dot_general(_bf(oi_ref[pl.ds(q0, ch), :]), w[:, ln:], nt_dims, preferred_element_type=F32))
            return carry

        zero = jnp.zeros((1, ln), F32)
        init = (zero, zero) + ((jnp.zeros((rows, ln), F32),) * 2 if with_sum else ())
        carry = lax.fori_loop(0, nch, chunk, init)
        if with_sum:
            sum_refs[0][...] = _colsum(carry[2])
            sum_refs[1][...] = _colsum(carry[3])
            row_g = lax.broadcasted_iota(jnp.int32, (SCAN_COLS, LANE), 0) // SSM_N
            lane_g = lax.broadcasted_iota(jnp.int32, (SCAN_COLS, LANE), 1) // SSM_P
            for scr, o_re, o_im in ((mb_s, sum_refs[2], sum_refs[3]), (mc_s, sum_refs[4], sum_refs[5])):
                for part, o_ref in enumerate((o_re, o_im)):
                    fold = jnp.zeros((SCAN_COLS, LANE), F32)
                    for cb in range(ln // LANE):
                        fold = fold + jnp.where(2 * cb + lane_g == row_g, scr[:, part * ln + cb * LANE:part * ln + (cb + 1) * LANE], 0.0)
                    o_ref[...] = jnp.where(row_g % 2 == 0, fold, pltpu.roll(fold, SSM_P, 1))

    vec = pl.BlockSpec((1, ln), lambda j: (0, j))
    blk = pl.BlockSpec((s, ln), lambda j: (0, j))
    cols = pl.BlockSpec((s, SCAN_COLS), lambda j: (0, j))
    wspec = pl.BlockSpec((None, SCAN_COLS, 2 * ln), lambda j: (j, 0, 0))
    ins, args = [cols, wspec, vec, vec, wspec], [v, w_in, lam_re, lam_im, w_out]
    outs, shapes = [blk, blk, cols], [_sds((s, SSM_L))] * 2 + [_sds((s, SSM_G * SSM_N))]
    scratch = [pltpu.VMEM((ch, ln), F32)] * 2
    if with_sum:
        own = pl.BlockSpec((None, SCAN_COLS, LANE), lambda j: (j, 0, 0))
        ins, args = ins + [blk, blk, cols], args + list(states) + [u]
        outs = [cols, vec, vec] + [own] * 4
        shapes = [_sds((s, SSM_G * SSM_N))] + [_sds((1, SSM_L))] * 2 + [_sds((SSM_L // ln, SCAN_COLS, LANE))] * 4
        scratch = scratch + [pltpu.VMEM((SCAN_COLS, 2 * ln), F32)] * 2 + [pltpu.VMEM((s, ln), F32)] * 2
    return _pc(body, name, (SSM_L // ln,), ins, outs, shapes, scratch=scratch)(*args)


def _ssm_act_fwd(y, u, d_skip):
    s = y.shape[0]
    ts = min(s, 512)

    def body(y_ref, u_ref, d_ref, o_ref):
        o_ref[...] = _gelu(y_ref[...] + d_ref[...] * u_ref[...]).astype(BF16)

    return _pc(body, "ssm_act_fwd", (s // ts,), [_row_spec(ts, D)] * 2 + [_vec_spec(D)], _row_spec(ts, D),
               _sds((s, D), BF16))(y, u, d_skip)


def _ssm_act_bwd(dg, y, u, d_skip):
    s = y.shape[0]
    ts = min(s, 512)

    def body(dg_ref, y_ref, u_ref, d_ref, dy_ref, dd_ref):
        uv = u_ref[...]
        dy = dg_ref[...] * _gelu_grad(y_ref[...] + d_ref[...] * uv)
        dy_ref[...] = dy.astype(BF16)
        _acc(dd_ref, pl.program_id(0) == 0, _colsum(dy * uv))

    return _pc(body, "ssm_act_bwd", (s // ts,), [_row_spec(ts, D)] * 3 + [_vec_spec(D)], [_row_spec(ts, D), _vec_spec(D)],
               [_sds((s, D), BF16), _sds((1, D))])(dg, y, u, d_skip)


def _axpy(a, b, d_skip):
    s = a.shape[0]
    ts = min(s, 512)

    def body(a_ref, b_ref, d_ref, o_ref):
        o_ref[...] = (a_ref[...] + d_ref[...] * b_ref[...].astype(F32)).astype(BF16)

    return _pc(body, "ssm_du", (s // ts,), [_row_spec(ts, D)] * 2 + [_vec_spec(D)], _row_spec(ts, D),
               _sds((s, D), BF16))(a, b, d_skip)


def _glu_fwd(zz):
    s = zz.shape[0]
    ts = min(s, 512)

    def body(a_ref, b_ref, o_ref):
        o_ref[...] = a_ref[...] * _sigmoid(b_ref[...])

    return _pc(body, "glu_fwd", (s // ts,), [_row_spec(ts, D, 0), _row_spec(ts, D, 1)], _row_spec(ts, D), _sds((s, D)))(zz, zz)


def _glu_bwd(zz, df):
    s = zz.shape[0]
    ts = min(s, 512)

    def body(a_ref, b_ref, df_ref, o_ref):
        sg = _sigmoid(b_ref[...])
        dfv = df_ref[...].astype(F32)
        o_ref[:, :D] = (dfv * sg).astype(BF16)
        o_ref[:, D:] = (dfv * a_ref[...] * sg * (1.0 - sg)).astype(BF16)

    return _pc(body, "glu_bwd", (s // ts,), [_row_spec(ts, D, 0), _row_spec(ts, D, 1), _row_spec(ts, D)],
               _row_spec(ts, 2 * D), _sds((s, 2 * D), BF16))(zz, zz, df)


def _ssm_block_diag(m_re, m_im):
    rows, half = SCAN_COLS, SCAN_LANES
    expand = jnp.tile(jnp.eye(SSM_P, dtype=BF16), (1, SCAN_GROUPS))

    def body(mr_ref, mi_ref, e_ref, o_ref):
        keep = (lax.broadcasted_iota(jnp.int32, (rows, half), 0) // SSM_N
                == lax.broadcasted_iota(jnp.int32, (rows, half), 1) // SSM_P)
        for part, m_ref in enumerate((mr_ref, mi_ref)):
            t = jnp.dot(_bf(m_ref[...]), e_ref[...], preferred_element_type=F32)
            o_ref[:, part * half:(part + 1) * half] = jnp.where(keep, t, 0.0).astype(BF16)

    blk = pl.BlockSpec((rows, SSM_P), lambda q: (q, 0))
    nb = SSM_G // SCAN_GROUPS
    return _pc(body, "ssm_block_diag", (nb,), [blk, blk, pl.BlockSpec((SSM_P, half), lambda q: (0, 0))],
               pl.BlockSpec((None, rows, 2 * half), lambda q: (q, 0, 0)), _sds((nb, rows, 2 * half), BF16))(m_re, m_im, expand)


def _mod_part(c_all, ada_w):
    n = ada_w.shape[-1]

    def body(c_ref, w_ref, o_ref):
        cv = c_ref[...]
        cond = _bf(cv * _sigmoid(cv))
        o_ref[...] = jnp.dot(cond, _bf(w_ref[...]), preferred_element_type=F32)

    return _pc(body, "mod_part", (2,), [pl.BlockSpec((N_DEV, D), lambda l: (0, 0)), pl.BlockSpec((None, D, n), lambda l: (l, 0, 0))],
               pl.BlockSpec((None, N_DEV, n), lambda l: (l, 0, 0)), _sds((2, N_DEV, n)))(c_all, ada_w)


def _ada_w_grad(c_all_t, dmod):
    nl, _, n = dmod.shape
    tr = 128

    def body(c_ref, d_ref, o_ref):
        cv = c_ref[...]
        cond = _bf(cv * _sigmoid(cv)).astype(F32)
        dm = _bf(d_ref[...]).astype(F32)
        acc = cond[:, 0:1] * dm[0:1, :]
        for b in range(1, N_DEV):
            acc = acc + cond[:, b:b + 1] * dm[b:b + 1, :]
        o_ref[...] = acc

    return _pc(body, "ada_w_grad", (nl, D // tr),
               [pl.BlockSpec((tr, N_DEV), lambda l, t: (t, 0)), pl.BlockSpec((None, N_DEV, n), lambda l, t: (l, 0, 0))],
               pl.BlockSpec((None, tr, n), lambda l, t: (l, t, 0)), _sds((nl, D, n)))(c_all_t, dmod)


def _adamw(name, parts, w, m, v, slot=0, prev=None, after=None):
    p, r, c = parts.shape
    tr = r
    while tr * c * 4 > (1 << 21) and tr % 16 == 0:
        tr //= 2
    nt = r // tr

    def body(p_ref, w_ref, m_ref, v_ref, *rest):
        g_ref, d_ref, nm_ref, nv_ref = rest[-4:]
        g = p_ref[0].astype(F32)
        for i in range(1, p):
            g = g + p_ref[i].astype(F32)
        g_ref[...] = g
        d_ref[...], nm_ref[...], nv_ref[...] = _adam_update(g, w_ref[...], m_ref[...], v_ref[...])

    blk = pl.BlockSpec((tr, c), lambda t: (slot * nt + t, 0))
    in_specs = [pl.BlockSpec((p, tr, c), lambda t: (0, t, 0)), blk, blk, blk]
    unread = list(prev or []) + ([after] if after is not None else [])
    return pl.pallas_call(
        body, name=name, grid=(nt,), in_specs=in_specs + [pl.BlockSpec(memory_space=pl.ANY)] * len(unread), out_specs=[blk] * 4,
        out_shape=[_sds(w.shape)] * 4, input_output_aliases={4 + i: i for i in range(4)} if prev else {},
        compiler_params=pltpu.CompilerParams(dimension_semantics=("arbitrary",), vmem_limit_bytes=VMEM_LIMIT_BYTES))(parts, w, m, v, *unread)


def _adam_update(g, w, m, v):
    m2 = B1 * m + (1.0 - B1) * g
    v2 = B2 * v + (1.0 - B2) * (g * g)
    m_hat = m2 / (1.0 - B1 ** STEP)
    v_hat = v2 / (1.0 - B2 ** STEP)
    return -LR * (m_hat / (jnp.sqrt(v_hat) + ADAM_EPS) + WD * w), m2, v2


def _adamw_many(name, items, after):
    n = len(items)

    def body(*refs):
        outs = refs[4 * n + 1:]
        for i in range(n):
            g, w, m, v = (r[...] for r in refs[4 * i:4 * i + 4])
            for o, val in zip(outs[3 * i:3 * i + 3], _adam_update(g, w, m, v)):
                o[...] = val

    full = lambda a: pl.BlockSpec(a.shape, lambda t: (0, 0))
    flat = [a for item in items for a in item]
    res = _pc(body, name, (1,), [full(a) for a in flat] + [pl.BlockSpec(memory_space=pl.ANY)],
              [full(item[1]) for item in items for _ in range(3)],
              [_sds(item[1].shape) for item in items for _ in range(3)])(*flat, after)
    return [tuple(res[3 * i:3 * i + 3]) for i in range(n)]


def _sum_parts(parts):
    p, r, c = parts.shape
    tr = r
    while tr * c * 4 > (1 << 19) and tr % 16 == 0:
        tr //= 2

    def body(p_ref, o_ref):
        g = p_ref[0]
        for i in range(1, p):
            g = g + p_ref[i]
        o_ref[...] = g

    return _pc(body, "sum_parts", (r // tr,), [pl.BlockSpec((p, tr, c), lambda t: (0, t, 0))], pl.BlockSpec((tr, c), lambda t: (t, 0)),
               _sds((r, c)))(parts)


def _place():
    x, y, c = lax.axis_index("x"), lax.axis_index("y"), lax.axis_index("c")
    peers = []
    for k in range(1, N_DEV):
        px = (1 - x) if k & 4 else x
        py = (1 - y) if k & 2 else y
        pc = (1 - c) if k & 1 else c
        peers.append(((px, py, pc), 4 * px + 2 * py + pc))
    return 4 * x + 2 * y + c, peers


def _at(ref, idx):
    return ref if idx is None else ref.at[idx]


def _exchange_copies(plan, n, src_refs, dst_refs, send_sems, recv_sems, local_sems=None, with_arrivals=True):
    me, peers = _place()
    local = [] if local_sems is None else [
        pltpu.make_async_copy(_at(src_refs[si], sx), _at(dst_refs[di], dx), local_sems.at[i])
        for i, (si, sx, di, dx) in enumerate(plan(me, me, 0))]

    def remote(k, i, dev, entry):
        si, sx, di, dx = entry
        return pltpu.make_async_remote_copy(_at(src_refs[si], sx), _at(dst_refs[di], dx), send_sems.at[k * n + i], recv_sems.at[k * n + i],
                                            device_id=dev, device_id_type=MESH)

    sends = [remote(k, i, dev, e) for k, (dev, peer) in enumerate(peers) for i, e in enumerate(plan(me, peer, k + 1))]
    if not with_arrivals:
        return local, sends, []
    arrivals = [remote(k, i, dev, e) for k, (dev, peer) in enumerate(peers) for i, e in enumerate(plan(peer, me, k + 1))]
    return local, sends, arrivals


def _sem_shapes(n_copies, local=True):
    sems = [pltpu.SemaphoreType.DMA(((N_DEV - 1) * n_copies,)), pltpu.SemaphoreType.DMA(((N_DEV - 1) * n_copies,))]
    return sems + [pltpu.SemaphoreType.DMA((n_copies,))] if local else sems


def _exchange(name, srcs, dst_shapes, plan, n_copies):
    ns, nd = len(srcs), len(dst_shapes)

    def body(*refs):
        local, sends, arrivals = _exchange_copies(plan, n_copies, refs[:ns], refs[ns:ns + nd], *refs[ns + nd:])
        for cp in local + sends:
            cp.start()
        for cp in arrivals:
            cp.wait_recv()
        for cp in sends:
            cp.wait_send()
        for cp in local:
            cp.wait()

    any_spec = pl.BlockSpec(memory_space=pl.ANY)
    return pl.pallas_call(
        body, name=name, in_specs=[any_spec] * ns, out_specs=[any_spec] * nd, out_shape=list(dst_shapes),
        scratch_shapes=_sem_shapes(n_copies))(*srcs)


HBM_SPEC = pl.BlockSpec(memory_space=pltpu.HBM)
SEM_SPEC = pl.BlockSpec(memory_space=pltpu.SEMAPHORE)
ANY_SPEC = pl.BlockSpec(memory_space=pl.ANY)
TOKEN_SPEC = pl.BlockSpec(memory_space=pltpu.VMEM)
SIDE_EFFECT = pltpu.SideEffectType.DATAFLOW_SIDE_EFFECTING


def _wait_all(local, sends, arrivals):
    for cp in arrivals:
        cp.wait_recv()
    for cp in sends:
        cp.wait_send()
    for cp in local:
        cp.wait()


def _exchange_start(name, srcs, dst_shapes, plan, n_copies, order):
    ns, nd = len(srcs), len(dst_shapes)
    nb = ns + nd

    def body(*refs):
        local, sends, _ = _exchange_copies(plan, n_copies, refs[:ns], refs[ns:nb], *refs[nb + 1:nb + 4], with_arrivals=False)
        for cp in local + sends:
            cp.start()
        refs[-1][...] = jnp.zeros((8, LANE), F32)

    lands = [pltpu.with_memory_space_constraint(lax.empty(d.shape, d.dtype), pltpu.HBM) for d in dst_shapes]
    srcs = [pltpu.with_memory_space_constraint(a, pltpu.HBM) for a in srcs]
    bufs = srcs + lands
    out = pl.pallas_call(
        body, name=name, in_specs=[HBM_SPEC] * nb + [ANY_SPEC],
        out_specs=[SEM_SPEC] * 3 + [HBM_SPEC] * nb + [TOKEN_SPEC],
        out_shape=_sem_shapes(n_copies) + [pltpu.HBM(a.shape, a.dtype) for a in bufs] + [_sds((8, LANE))],
        input_output_aliases={i: 3 + i for i in range(nb)},
        compiler_params=pltpu.CompilerParams(has_side_effects=SIDE_EFFECT))(*bufs, order)
    return out[:3], out[3:3 + ns], out[3 + ns:3 + nb], out[-1]


def _exchange_relay(name, sems, srcs, lands, plan, n_copies, plan2, n_copies2, after):
    ns, nd = len(srcs), len(lands)
    nb = ns + nd

    def body(*refs):
        land_refs = refs[ns:nb]
        _wait_all(*_exchange_copies(plan, n_copies, refs[:ns], land_refs, *refs[nb:nb + 3]))
        _, sends, _ = _exchange_copies(plan2, n_copies2, land_refs, land_refs, *refs[nb + 4:nb + 6], with_arrivals=False)
        for cp in sends:
            cp.start()
        refs[-1][...] = jnp.zeros((8, LANE), F32)

    out = pl.pallas_call(
        body, name=name, in_specs=[HBM_SPEC] * nb + [SEM_SPEC] * 3 + [ANY_SPEC],
        out_specs=[SEM_SPEC] * 2 + [HBM_SPEC] * nd + [TOKEN_SPEC],
        out_shape=_sem_shapes(n_copies2, local=False) + [pltpu.HBM(a.shape, a.dtype) for a in lands] + [_sds((8, LANE))],
        input_output_aliases={ns + i: 2 + i for i in range(nd)},
        compiler_params=pltpu.CompilerParams(has_side_effects=SIDE_EFFECT))(*srcs, *lands, *sems, after)
    return out[:2], out[2:2 + nd], out[-1]


def _exchange_wait(name, sems, srcs, lands, plan, n_copies, after):
    srcs = [] if srcs is None else list(srcs)
    ns, nd = len(srcs), len(lands)
    nb = ns + nd

    def body(*refs):
        land_refs = refs[ns:nb]
        _wait_all(*_exchange_copies(plan, n_copies, refs[:ns] if ns else land_refs, land_refs, *refs[nb:nb + len(sems)]))

    bufs = srcs + list(lands)
    out = pl.pallas_call(
        body, name=name, in_specs=[HBM_SPEC] * nb + [SEM_SPEC] * len(sems) + [ANY_SPEC],
        out_specs=[HBM_SPEC] * nb, out_shape=[pltpu.HBM(a.shape, a.dtype) for a in bufs],
        input_output_aliases={i: i for i in range(nb)},
        compiler_params=pltpu.CompilerParams(has_side_effects=SIDE_EFFECT))(*bufs, *sems, after)
    return out[ns:]


def _all_gather(name, arrs):
    plan = lambda me, peer, k: [(i, None, i, me) for i in range(len(arrs))]
    return _exchange(name, arrs, [_sds((N_DEV,) + a.shape, a.dtype) for a in arrs], plan, len(arrs))


def _mix0_fwd(h, p):
    z = _mm_nt("mix0_in", h, p["ab_w_in"])
    y_a, d = _pool_fwd(z, p["pool_w"], p["pool_scale"])
    y_b = _sgu_fwd(z, p["sgu_ln_g"], p["sgu_ln_b"], p["sgu_w"], p["sgu_bt"])
    ycat = jnp.concatenate([y_a, y_b], axis=1)
    return _mm_nn("mix0_out", ycat, p["ab_w_out"]), (h, z, d, ycat)


def _mix0_bwd(df, saved, p, after):
    h, z, d, ycat = saved
    dycat = _mm_nt("mix0_out_dx", df, p["ab_w_out"], after=after)
    g = {"ab_w_out": _mm_tn("mix0_out_dw", ycat, df, BF16)}
    dz_p, g["pool_w"], g["pool_scale"] = _pool_bwd(dycat, d, p["pool_w"], p["pool_scale"])
    dz_u, dz_v, g["sgu_ln_g"], g["sgu_ln_b"], g["sgu_w"], dbt = _sgu_bwd(
        z, dycat, p["sgu_ln_g"], p["sgu_ln_b"], p["sgu_w"], p["sgu_bt"], p["head_sum"])
    g["sgu_b"] = dbt[:, :NH].T
    dz = jnp.concatenate([dz_p, dz_u, dz_v], axis=1)
    g["ab_w_in"] = _mm_tn("mix0_in_dw", dz, h, BF16)
    return _mm_nn("mix0_in_dx", dz, p["ab_w_in"]), g


def _mix1_fwd(h, p):
    u = _mm_nn("ssm_w_in", h, p["ssm_w_in"])
    x_re, x_im, y = _ssm_scan("ssm_scan_fwd", u, p["wb_bd"], p["lam_bar_re"], p["lam_bar_im"], p["wc_bd"], False)
    g = _ssm_act_fwd(y, u, p["ssm_d"])
    zz = _mm_nn("ssm_glu", g, p["ssm_w_glu"])
    return _glu_fwd(zz), (h, u, x_re, x_im, y, g, zz)


def _mix1_bwd(df, saved, p, after):
    h, u, x_re, x_im, y, g, zz = saved
    gr = {}
    dzz = _glu_bwd(zz, df)
    dg = _mm_nt("ssm_glu_dx", dzz, p["ssm_w_glu"], after=after)
    gr["ssm_w_glu"] = _mm_tn("ssm_glu_dw", g, dzz, BF16)
    dy, gr["ssm_d"] = _ssm_act_bwd(dg, y, u, p["ssm_d"])
    du_ssm, g_lam_re, g_lam_im, mb_re, mb_im, mc_re, mc_im = _ssm_scan(
        "ssm_scan_bwd", dy, p["wc_bd"], p["lam_bar_re"], -p["lam_bar_im"], p["wb_bd"], True, states=(x_re, x_im), u=u)
    du = _axpy(du_ssm, dy, p["ssm_d"])
    gr["ssm_w_in"] = _mm_tn("ssm_w_in_dw", h, du, BF16)
    dh = _mm_nt("ssm_w_in_dx", du, p["ssm_w_in"])
    per_group = lambda m: m[:, :, :SSM_P].reshape(SSM_G, SSM_N, SSM_P)
    gr["ssm_c_re"] = per_group(mc_re)
    gr["ssm_c_im"] = -per_group(mc_im)
    dlr, dli, ddt, dbr, dbi = _ssm_param_bwd(
        g_lam_re.reshape(SSM_G, SSM_P), g_lam_im.reshape(SSM_G, SSM_P),
        per_group(mb_re).reshape(SSM_G, SSM_N * SSM_P), per_group(mb_im).reshape(SSM_G, SSM_N * SSM_P),
        p["lam_re"], p["lam_im"], p["lam_re_rep"], p["lam_im_rep"], p["log_dt"], p["b_re"], p["b_im"], p["seg"])
    gr["ssm_lam_re"], gr["ssm_lam_im"], gr["ssm_log_dt"] = dlr, dli, ddt[:, 0]
    gr["ssm_b_re"] = dbr.reshape(SSM_G, SSM_N, SSM_P)
    gr["ssm_b_im"] = dbi.reshape(SSM_G, SSM_N, SSM_P)
    return dh, gr


def _ssm_params(lam_re, lam_im, b_re, b_im, c_re, c_im, log_dt):
    wide = lambda b: b.transpose(0, 2, 1).reshape(SSM_G, SSM_N * SSM_P)
    p = {"lam_re": lam_re, "lam_im": lam_im, "log_dt": log_dt.reshape(SSM_G, 1),
         "lam_re_rep": jnp.tile(lam_re, (1, SSM_N)), "lam_im_rep": jnp.tile(lam_im, (1, SSM_N)), "b_re": wide(b_re), "b_im": wide(b_im)}
    lbr, lbi, bbr, bbi = _ssm_prep(lam_re, lam_im, p["lam_re_rep"], p["lam_im_rep"], p["log_dt"], p["b_re"], p["b_im"])
    p["lam_bar_re"], p["lam_bar_im"] = lbr.reshape(1, SSM_L), lbi.reshape(1, SSM_L)
    rows = lambda m: m.reshape(SSM_G * SSM_N, SSM_P)
    p["wb_bd"] = _ssm_block_diag(rows(bbr), rows(bbi))
    p["wc_bd"] = _ssm_block_diag(rows(c_re), rows(-c_im))
    p["seg"] = jnp.tile(jnp.eye(SSM_P, dtype=F32), (SSM_N, 1))
    return p


RES_WEIGHT = (0.5, 1.0, 0.5)


def _local_step(x, tgt, vecs, weights_of, on_part, on_grads):
    def fns(i, w):
        if i % 3 != 1:
            win, wout_of = w
            return ((lambda h: _ffn_fwd(h, win, wout_of)),
                    (lambda df, sv, after: (_ffn_bwd(df, sv, win, wout_of(None), lambda tag, part: on_part(i, tag, part), after), None)))
        if i == 1:
            return (lambda h: _mix0_fwd(h, w)), (lambda df, sv, after: _mix0_bwd(df, sv, w, after))
        return (lambda h: _mix1_fwd(h, w)), (lambda df, sv, after: _mix1_bwd(df, sv, w, after))

    rw = RES_WEIGHT * 2
    saved, bwd = [], []
    f = None
    for i in range(6):
        w, token = weights_of(i, x if i == 0 else f)
        fwd, b = fns(i, w)
        if i == 0:
            h = _prenorm_fwd(x, vecs, 0, token)
        else:
            x, h = _post_pre_fwd(x, f, vecs, i, rw[i - 1], token)
        f, inner = fwd(h)
        saved.append((x, f, inner))
        bwd.append(b)
    loss_row, dx = _loss_fwd_bwd(_postnorm_fwd(x, f, vecs, 5, rw[5]), tgt)
    df, dv_top = _postnorm_bwd(dx, f, vecs, 5, rw[5])
    token = jnp.zeros((8, LANE), F32)
    for i in reversed(range(6)):
        x_i, _, inner = saved[i]
        dh, extra = bwd[i](df, inner, token)
        if i > 0:
            dx, df, dv = _pre_post_bwd(dx, dh, x_i, saved[i - 1][1], vecs, i, rw[i - 1])
        else:
            dx, dv = _prenorm_bwd(dx, dh, x_i, vecs, 0)
        token = on_grads(i, extra, dv, dv_top if i == 5 else None, loss_row)
    return dx


def _pad_rows(v, rows):
    return jnp.pad(v, (0, rows * LANE - v.shape[0])).reshape(rows, LANE)


def _pack(parts):
    flat, layout, off = [], [], 0
    for a in parts:
        n = a.size
        padded = -(-n // LANE) * LANE
        flat.append(jnp.pad(a.reshape(-1).astype(F32), (0, padded - n)))
        layout.append((off, n, a.shape))
        off += padded
    return jnp.concatenate(flat), layout


def _unpack(flat, layout):
    return [flat[off:off + n].reshape(shape) for off, n, shape in layout]


TRANSPOSED = ["ffn_w_in", "ab_w_in", "ssm_b_re", "ssm_b_im"]
WEIGHTS = ['ada_w', 'ada_b', 'norm_pre', 'norm_post', 'ffn_w_in', 'ffn_w_out', 'ab_w_in', 'pool_w', 'pool_scale', 'sgu_ln_g',
           'sgu_ln_b', 'sgu_w', 'sgu_b', 'ab_w_out', 'ssm_w_in', 'ssm_lam_re', 'ssm_lam_im', 'ssm_b_re', 'ssm_b_im', 'ssm_c_re',
           'ssm_c_im', 'ssm_d', 'ssm_log_dt', 'ssm_w_glu']


def kernel(x, c, ada_w, ada_b, norm_pre, norm_post, ffn_w_in, ffn_w_out, ab_w_in, pool_w, pool_scale, sgu_ln_g, sgu_ln_b, sgu_w, sgu_b, ab_w_out, ssm_w_in, ssm_lam_re, ssm_lam_im, ssm_b_re, ssm_b_im, ssm_c_re, ssm_c_im, ssm_d, ssm_log_dt, ssm_w_glu, loss_target, m_ada_w, m_ada_b, m_norm_pre, m_norm_post, m_ffn_w_in, m_ffn_w_out, m_ab_w_in, m_pool_w, m_pool_scale, m_sgu_ln_g, m_sgu_ln_b, m_sgu_w, m_sgu_b, m_ab_w_out, m_ssm_w_in, m_ssm_lam_re, m_ssm_lam_im, m_ssm_b_re, m_ssm_b_im, m_ssm_c_re, m_ssm_c_im, m_ssm_d, m_ssm_log_dt, m_ssm_w_glu, v_ada_w, v_ada_b, v_norm_pre, v_norm_post, v_ffn_w_in, v_ffn_w_out, v_ab_w_in, v_pool_w, v_pool_scale, v_sgu_ln_g, v_sgu_ln_b, v_sgu_w, v_sgu_b, v_ab_w_out, v_ssm_w_in, v_ssm_lam_re, v_ssm_lam_im, v_ssm_b_re, v_ssm_b_im, v_ssm_c_re, v_ssm_c_im, v_ssm_d, v_ssm_log_dt, v_ssm_w_glu):
    args = locals()
    wts = {n: args[n] for n in WEIGHTS}
    mom = {n: args["m_" + n] for n in WEIGHTS}
    var = {n: args["v_" + n] for n in WEIGHTS}
    for n in TRANSPOSED:
        for t in (wts, mom, var):
            t[n] = jnp.swapaxes(t[n], -1, -2)
    me = 4 * lax.axis_index("x") + 2 * lax.axis_index("y") + lax.axis_index("c")
    s = x.shape[1]
    nd = D // N_DEV

    small_in, small_in_layout = _pack([c, norm_pre, norm_post, ssm_d])
    small_rows = -(-small_in.shape[0] // (8 * LANE)) * 8
    (g_small,) = _all_gather("gather_small", [_pad_rows(small_in, small_rows)])
    g_small = g_small.reshape(N_DEV, -1)
    c_all, npre_g, npost_g, sd_g = [jnp.stack([_unpack(g_small[j], small_in_layout)[i] for j in range(N_DEV)]) for i in range(4)]
    c_all = c_all.reshape(N_DEV, D)
    norm_pre_full = npre_g.transpose(1, 2, 0, 3).reshape(2, 3, D)
    norm_post_full = npost_g.transpose(1, 2, 0, 3).reshape(2, 3, D)
    ssm_d_full = sd_g.transpose(1, 0, 2).reshape(1, D)

    nw = ada_w.shape[-1]
    (mod_g,) = _all_gather("gather_mod", [_mod_part(c_all, ada_w)])
    mod = lax.dynamic_index_in_dim(mod_g, me, axis=2, keepdims=False)
    mod = (mod.transpose(1, 0, 2).reshape(2, N_DEV * nw) + ada_b).reshape(2, 3, 3, D)

    w_in_t = wts["ffn_w_in"]
    shards = [[w_in_t[0, 0]], [ffn_w_out[0, 0]], [wts["ab_w_in"][0], ab_w_out[0]], [w_in_t[0, 1]], [ffn_w_out[0, 1]],
              [w_in_t[1, 0]], [ffn_w_out[1, 0]], [ssm_w_in[0], ssm_w_glu[0]], [w_in_t[1, 1]], [ffn_w_out[1, 1]]]
    first_group = {0: 0, 1: 2, 2: 3, 3: 5, 4: 7, 5: 8}
    first_groups = set(first_group.values())
    same_core = (2, 4, 6)

    def gather_plan(n):
        return lambda me_, peer_, k: [(a, None, a, me_) for a in range(n)] if k in (0, 1) + same_core else []

    def relay_plan(n):
        return lambda me_, peer_, k: [(a, me_ ^ kk, a, me_ ^ kk) for kk in same_core for a in range(n)] if k == 1 else []

    gathers, relays = [], {}
    token = mod_g
    for g, group in enumerate(shards):
        group = [a.astype(BF16) for a in group]
        sems, srcs_thru, lands, token = _exchange_start(
            f"gather_start_{g}", group, [_sds((N_DEV,) + a.shape, BF16) for a in group], gather_plan(len(group)), len(group), token)
        gathers.append((sems, srcs_thru, lands))
    mod6 = mod.reshape(6, 3, D)
    vecs = jnp.stack([norm_pre_full.reshape(6, D), mod6[:, 1], mod6[:, 0], norm_post_full.reshape(6, D), mod6[:, 2]]
                     + [jnp.zeros((6, D), F32)] * 3, axis=1)
    vecs = vecs + token[0, 0]

    def relay(g, after):
        sems, srcs_thru, lands = gathers[g]
        n = len(lands)
        relays[g] = _exchange_relay(f"gather_relay_{g}", sems, srcs_thru, lands, gather_plan(n), n, relay_plan(n), 3 * n, after)

    def fetch(g, after):
        if g not in relays:
            relay(g, after)
        sems, lands, token = relays[g]
        n = len(lands)
        got = _exchange_wait(f"gather_wait_{g}", sems, None, lands, relay_plan(n), 3 * n, after)
        if g + 1 in first_groups:
            relay(g + 1, got[0])
            token = relays[g + 1][2]
        return got, token

    head_sum = jnp.repeat(jnp.eye(NH, LANE, dtype=F32), HD, axis=0)
    mix0 = {"pool_w": pool_w[0], "pool_scale": pool_scale, "sgu_ln_g": sgu_ln_g, "sgu_ln_b": sgu_ln_b, "sgu_w": sgu_w[0],
            "sgu_bt": jnp.pad(sgu_b[0].T, ((0, 0), (0, LANE - NH))), "head_sum": head_sum}
    mix1 = _ssm_params(ssm_lam_re[0], ssm_lam_im[0], ssm_b_re[0], ssm_b_im[0], ssm_c_re[0], ssm_c_im[0], ssm_log_dt[0])
    mix1["ssm_d"] = ssm_d_full

    def weights_of(i, x_in):
        g = first_group[i]
        if i % 3 != 1:
            (win,), token = fetch(g, x_in)
            cache = []

            def wout_of(act):
                if not cache:
                    cache.append(fetch(g + 1, act)[0][0])
                return cache[0]

            return (win, wout_of), token
        (a, b), token = fetch(g, x_in)
        if i == 1:
            return dict(mix0, ab_w_in=a.reshape(-1, D), ab_w_out=b.reshape(D, D)), token
        return dict(mix1, ssm_w_in=a.reshape(D, D), ssm_w_glu=b.transpose(1, 0, 2).reshape(D, -1)), token

    def shard_cols(a):
        r = a.shape[0]
        return a.reshape(r, N_DEV, -1).transpose(1, 0, 2)

    scatter_plan = lambda me_, peer_, k: [(0, peer_, 0, me_), (1, peer_, 1, me_)]
    scatter_plan1 = lambda me_, peer_, k: [(0, peer_, 0, me_)]
    scatters = []
    last_token = [jnp.zeros((8, LANE), F32)]
    pieces, mixer, bundles = {}, {}, {}
    bundle_plan = lambda me_, peer_, k: [(0, None, 0, me_)]

    held = {}

    def on_part(i, tag, part):
        if i != 0 and tag == "w_out":
            held[i] = part
            return last_token[0]
        names, parts, plan = (("ffn_" + tag,), [part], scatter_plan1) if i == 0 else (("ffn_w_out", "ffn_w_in"), [held[i], part], scatter_plan)
        sems, srcs_thru, lands, last_token[0] = _exchange_start(
            f"scatter_start_{i}_{tag}", parts, [_sds(a.shape, BF16) for a in parts], plan, len(parts), last_token[0])
        scatters.append((i, names, plan, sems, srcs_thru, lands))
        return last_token[0]
    mix0_names = ["pool_w", "pool_scale", "sgu_ln_g", "sgu_ln_b", "sgu_w", "sgu_b"]
    mix1_names = ["ssm_lam_re", "ssm_lam_im", "ssm_b_re", "ssm_b_im", "ssm_c_re", "ssm_c_im", "ssm_log_dt", "ssm_d"]

    def start_bundle(tag, arrays):
        flat, layout = _pack(arrays)
        rows = -(-flat.shape[0] // (8 * LANE)) * 8
        plan = gather_plan(1) if tag == "a" else bundle_plan
        sems, srcs_thru, lands, last_token[0] = _exchange_start(
            f"small_start_{tag}", [_pad_rows(flat, rows)], [_sds((N_DEV, rows, LANE))], plan, 1, last_token[0])
        bundles[tag] = (sems, srcs_thru, lands, layout)

    def on_grads(i, extra, dv, dv_top, loss_row):
        pieces[i] = dv
        if i == 5:
            pieces["top"] = dv_top
        if i == 4:
            mixer.update({n: extra[n] for n in mix1_names})
        if i == 1:
            mixer.update({n: extra[n] for n in mix0_names})
            start_bundle("a", [jnp.stack([pieces[j] for j in ("top", 5, 4, 3, 2, 1)])] + [mixer[n] for n in mix0_names + mix1_names])
        if i == 0:
            start_bundle("b", [dv, loss_row])
        if i % 3 != 1:
            return last_token[0]
        if i == 1:
            names, parts = ("ab_w_in", "ab_w_out"), [extra["ab_w_in"].reshape(N_DEV, -1, D), extra["ab_w_out"].reshape(N_DEV, nd, D)]
        else:
            names, parts = ("ssm_w_in", "ssm_w_glu"), [extra["ssm_w_in"].reshape(N_DEV, nd, D), shard_cols(extra["ssm_w_glu"])]
        sems, srcs_thru, lands, last_token[0] = _exchange_start(
            f"scatter_start_{i}", parts, [_sds(a.shape, BF16) for a in parts], scatter_plan, 2, last_token[0])
        scatters.append((i, names, scatter_plan, sems, srcs_thru, lands))
        return last_token[0]

    grad_x = _local_step(x[0], loss_target[0], vecs, weights_of, on_part, on_grads)

    out_g, out_d, out_m, out_v = {}, {}, {}, {}
    big_out = {}

    def adam_big(name, recv, n, slot=0, after=None):
        c_ = wts[n].shape[-1]
        big_out[n] = _adamw(name, recv.reshape(recv.shape[0], -1, c_), *[t[n].reshape(-1, c_) for t in (wts, mom, var)],
                            slot=slot, prev=big_out.get(n), after=after)
        return big_out[n][0]

    ffn_slot = {0: 0, 2: 1, 3: 2, 5: 3}

    def land_and_update(entries, after):
        for i, names, plan, sems, srcs_thru, lands in entries:
            recv = _exchange_wait(f"scatter_wait_{i}_{names[0]}", sems, srcs_thru, lands, plan, len(names), after)
            for n, r in zip(names, recv):
                after = adam_big(f"adamw_{n}_{i}", r, n, ffn_slot.get(i, 0), after)
        return after

    after = land_and_update([e for e in scatters if e[0] != 0], last_token[0])

    def landed(tag, g_parts):
        layout = bundles[tag][3]
        off, n, shape = layout[0]
        dmods = g_parts.reshape(N_DEV, -1)[:, off:off + n].reshape((N_DEV,) + shape)
        total = _sum_parts(g_parts)
        return dmods, _unpack(total.reshape(-1), layout), total

    def adam_small(n, g, after=None):
        cols = wts[n].shape[-1]
        res = _adamw(f"adamw_{n}", g.reshape(1, -1, cols), *[t[n].reshape(-1, cols) for t in (wts, mom, var)], after=after)
        for o, arr in zip((out_g, out_d, out_m, out_v), res):
            o[n] = arr.reshape(wts[n].shape)
            if n in TRANSPOSED:
                o[n] = jnp.swapaxes(o[n], -1, -2)
        return res[0]

    sems, srcs_thru, lands, _ = bundles["a"]
    sems, lands, _ = _exchange_relay("small_relay_a", sems, srcs_thru, lands, gather_plan(1), 1, relay_plan(1), 3, after)
    (parts_a,) = _exchange_wait("small_wait_a", sems, None, lands, relay_plan(1), 3, after)
    shells_a, sums_a, after = landed("a", parts_a)
    small = dict(zip(mix0_names + mix1_names, sums_a[1:]))
    def adam_tiny(name, grads, after):
        view = lambda n, a: a.reshape(-1, wts[n].shape[-1])
        items = [(view(n, g),) + tuple(view(n, t[n]) for t in (wts, mom, var)) for n, g in grads.items()]
        for (n, _), item, res in zip(grads.items(), items, _adamw_many(name, items, after)):
            for o, arr in zip((out_g, out_d, out_m, out_v), (item[0],) + res):
                o[n] = arr.reshape(wts[n].shape)
        return res[0]

    tiny = ["pool_scale", "sgu_ln_g", "sgu_ln_b", "sgu_b", "ssm_lam_re", "ssm_lam_im", "ssm_log_dt"]
    for n in [n for n in mix0_names + mix1_names if n not in tiny and n != "ssm_d"]:
        after = adam_small(n, small[n], after)
    after = adam_tiny("adamw_tiny_mixers", dict({n: small[n] for n in tiny},
                                                ssm_d=lax.dynamic_slice_in_dim(small["ssm_d"], me * nd, nd, axis=1)), after)
    def shell_grads(top, blocks, first):
        own_rows = jnp.concatenate([first[..., None, :, :], blocks[..., :0:-1, :, :]], axis=-3)
        next_rows = jnp.concatenate([own_rows[..., 1:, :, :], top[..., None, :, :]], axis=-3)
        dmod_ = jnp.stack([own_rows[..., V_SHIFT, :], own_rows[..., V_SCALE, :], next_rows[..., V_GATE, :]], axis=-2)
        return dmod_, own_rows[..., V_GPRE, :], next_rows[..., V_GPOST, :]

    def ada_w_layer(l, dmod_l, after):
        mine = lax.dynamic_index_in_dim(dmod_l.reshape(N_DEV, N_DEV, nw), me, axis=1, keepdims=False)
        return adam_big(f"adamw_ada_w_{l}", _ada_w_grad(c_all.T, mine[None]), "ada_w", l, after)

    after = ada_w_layer(1, shell_grads(shells_a[:, 0], shells_a, jnp.zeros_like(shells_a[:, 0]))[0][:, 3:], after)
    sems, srcs_thru, lands, _ = bundles["b"]
    (parts_b,) = _exchange_wait("small_wait_b", sems, srcs_thru, lands, bundle_plan, 1, after)
    shell_b, (first_sum, loss_sum), after = landed("b", parts_b)
    loss = loss_sum[0, 0]

    dmod_sum, dg_pre_sum, dg_post_sum = shell_grads(sums_a[0][0], sums_a[0], first_sum)
    own = lambda a: lax.dynamic_slice_in_dim(a, me * nd, nd, axis=1)
    after = adam_tiny("adamw_tiny_shell", {"ada_b": dmod_sum, "norm_pre": own(dg_pre_sum), "norm_post": own(dg_post_sum)}, after)

    after = ada_w_layer(0, shell_grads(shells_a[:, 0], shells_a, shell_b)[0][:, :3], after)

    land_and_update([e for e in scatters if e[0] == 0], after)
    for n, res in big_out.items():
        for o, arr in zip((out_g, out_d, out_m, out_v), res):
            o[n] = arr.reshape(wts[n].shape)
            if n in TRANSPOSED:
                o[n] = jnp.swapaxes(o[n], -1, -2)

    return (loss, grad_x[None], *[out_g[n] for n in WEIGHTS], *[out_d[n] for n in WEIGHTS],
            *[out_m[n] for n in WEIGHTS], *[out_v[n] for n in WEIGHTS])
```

```python
import math

import jax
import jax.numpy as jnp
from jax import lax
from jax.experimental import pallas as pl
from jax.experimental.pallas import tpu as pltpu

F32 = jnp.float32
BF16 = jnp.bfloat16
MESH = pl.DeviceIdType.MESH
HIGHEST = lax.Precision.HIGHEST

N_DEV = 8
D = 1024
D_FF = 2816
FSH = 2 * D_FF // N_DEV
EPS = 1e-6
POOL_WINDOWS = (2, 4, 8, 16)
HD = 128
NH = 4
SSM_G, SSM_P, SSM_N = 64, 64, 16
SSM_L = SSM_G * SSM_P
LR, B1, B2, ADAM_EPS, WD, STEP = 0.001, 0.9, 0.999, 1e-08, 0.01, 10
GELU_C = math.sqrt(2.0 / math.pi)
VMEM_LIMIT_BYTES = 48 * 1024 * 1024
LANE = 128


def _pc(body, name, grid, in_specs, out_specs, out_shape, scratch=()):
    return pl.pallas_call(
        body, name=name, grid=grid, in_specs=in_specs, out_specs=out_specs, out_shape=out_shape,
        scratch_shapes=list(scratch),
        compiler_params=pltpu.CompilerParams(dimension_semantics=("arbitrary",) * len(grid),
                                             vmem_limit_bytes=VMEM_LIMIT_BYTES))


def _sds(shape, dtype=F32):
    return jax.ShapeDtypeStruct(tuple(shape), dtype)


def _bf(v):
    return v if v.dtype == BF16 else v.astype(BF16)


def _row_spec(ts, width, col=0):
    return pl.BlockSpec((ts, width), lambda t, _c=col: (t, _c))


def _vec_spec(width, col=0):
    return pl.BlockSpec((1, width), lambda t, _c=col: (0, _c))


def _mm(name, a, b, contract, grid, a_spec, b_spec, o_spec, out_shape, acc_axis=None, after=None):
    dn = (contract, ((), ()))

    def body(a_ref, b_ref, *rest):
        o_ref = rest[-1]
        r = lax.dot_general(_bf(a_ref[...]), _bf(b_ref[...]), dn, preferred_element_type=F32)
        if acc_axis is None:
            o_ref[...] = r.astype(o_ref.dtype)
        else:
            k = pl.program_id(acc_axis)

            @pl.when(k == 0)
            def _():
                o_ref[...] = r

            @pl.when(k > 0)
            def _():
                o_ref[...] += r

    if after is None:
        return _pc(body, name, grid, [a_spec, b_spec], o_spec, out_shape)(a, b)
    return _pc(body, name, grid, [a_spec, b_spec, pl.BlockSpec(memory_space=pl.ANY)], o_spec, out_shape)(a, b, after)


def _mm_sum(name, a, b, ts, after=None):
    nj, s, k = a.shape
    n = b.shape[2]

    def body(a_ref, b_ref, *rest):
        acc = jnp.dot(a_ref[0], b_ref[0], preferred_element_type=F32)
        for j in range(1, nj):
            acc = acc + jnp.dot(a_ref[j], b_ref[j], preferred_element_type=F32)
        rest[-1][...] = acc

    specs = [pl.BlockSpec((nj, ts, k), lambda t: (0, t, 0)), pl.BlockSpec((nj, k, n), lambda t: (0, 0, 0))]
    args = (a, b)
    if after is not None:
        specs, args = specs + [pl.BlockSpec(memory_space=pl.ANY)], args + (after,)
    return _pc(body, name, (s // ts,), specs, pl.BlockSpec((ts, n), lambda t: (t, 0)), _sds((s, n)))(*args)


def _tile(s):
    return min(s, 1024)


def _div_tile(n, cap=1024):
    t = min(n, cap) // LANE * LANE
    while n % t:
        t -= LANE
    return t


def _mm_nn(name, a, b, out_dtype=F32):
    s, k = a.shape
    n = b.shape[1]
    ts, tn = _tile(s), _div_tile(n)
    return _mm(name, a, b, ((1,), (0,)), (n // tn, s // ts),
               pl.BlockSpec((ts, k), lambda j, t: (t, 0)), pl.BlockSpec((k, tn), lambda j, t: (0, j)),
               pl.BlockSpec((ts, tn), lambda j, t: (t, j)), _sds((s, n), out_dtype))


def _mm_nt(name, a, b, out_dtype=F32, after=None):
    s, n = a.shape
    k = b.shape[0]
    ts, tk = _tile(s), _div_tile(k)
    return _mm(name, a, b, ((1,), (1,)), (k // tk, s // ts),
               pl.BlockSpec((ts, n), lambda j, t: (t, 0)), pl.BlockSpec((tk, n), lambda j, t: (j, 0)),
               pl.BlockSpec((ts, tk), lambda j, t: (t, j)), _sds((s, k), out_dtype), after=after)


def _mm_tn(name, a, b, out_dtype=F32, tm=512, tn=512):
    s, m = a.shape
    n = b.shape[1]
    tm, tn = min(m, tm), min(n, tn)
    return _mm(name, a, b, ((0,), (0,)), (m // tm, n // tn),
               pl.BlockSpec((s, tm), lambda i, j: (0, i)), pl.BlockSpec((s, tn), lambda i, j: (0, j)),
               pl.BlockSpec((tm, tn), lambda i, j: (i, j)), _sds((m, n), out_dtype))


def _rstd(v):
    return lax.rsqrt(jnp.mean(v * v, axis=-1, keepdims=True) + EPS)


V_GPRE, V_SCALE, V_SHIFT, V_GPOST, V_GATE = range(5)


def _vrow(v, r):
    return v[r:r + 1]


def _vblock(i):
    return pl.BlockSpec((None, 8, D), lambda t: (i, 0, 0))


def _head(xv, v):
    return ((xv * _rstd(xv) * _vrow(v, V_GPRE)) * (1.0 + _vrow(v, V_SCALE)) + _vrow(v, V_SHIFT)).astype(BF16)


def _tail(xv, fv, v, rw):
    return xv + (rw * _vrow(v, V_GATE)) * (fv * _rstd(fv) * _vrow(v, V_GPOST))


def _prenorm_fwd(x, vecs, i, after):
    s = x.shape[0]
    ts = min(s, 512)

    def body(x_ref, v_ref, after_ref, h_ref):
        h_ref[...] = _head(x_ref[...], v_ref[...])

    return _pc(body, "prenorm_fwd", (s // ts,), [_row_spec(ts, D), _vblock(i), pl.BlockSpec(memory_space=pl.ANY)], _row_spec(ts, D),
               _sds((s, D), BF16))(x, vecs, after)


def _post_pre_fwd(x, f, vecs, i, rw_prev, after):
    s = x.shape[0]
    ts = min(s, 512)

    def body(x_ref, f_ref, vp_ref, vc_ref, after_ref, xo_ref, h_ref):
        xv = _tail(x_ref[...], f_ref[...], vp_ref[...], rw_prev)
        xo_ref[...] = xv
        h_ref[...] = _head(xv, vc_ref[...])

    return _pc(body, "post_pre_fwd", (s // ts,),
               [_row_spec(ts, D)] * 2 + [_vblock(i - 1), _vblock(i), pl.BlockSpec(memory_space=pl.ANY)], [_row_spec(ts, D)] * 2,
               [_sds((s, D)), _sds((s, D), BF16)])(x, f, vecs, vecs, after)


def _zero_at_first(first, *refs):
    @pl.when(first)
    def _():
        for ref in refs:
            ref[...] = jnp.zeros_like(ref)


def _acc(ref, first, v):
    @pl.when(first)
    def _():
        ref[...] = v

    @pl.when(jnp.logical_not(first))
    def _():
        ref[...] += v


def _colsum(v):
    return jnp.sum(v, axis=0, keepdims=True)


def _tail_bwd(do, fv, v, rw, dv_ref):
    gv = _vrow(v, V_GPOST)
    r = _rstd(fv)
    fn = fv * r
    dv_ref[V_GATE:V_GATE + 1, :] += rw * _colsum(do * (fn * gv))
    dy = (rw * _vrow(v, V_GATE)) * do
    dv_ref[V_GPOST:V_GPOST + 1, :] += _colsum(dy * fn)
    dfn = dy * gv
    return (r * (dfn - fn * jnp.mean(dfn * fn, axis=-1, keepdims=True))).astype(BF16)


def _head_bwd(do, dhv, xv, v, dv_ref):
    gv = _vrow(v, V_GPRE)
    r = _rstd(xv)
    xn = xv * r
    dv_ref[V_SHIFT:V_SHIFT + 1, :] += _colsum(dhv)
    dv_ref[V_SCALE:V_SCALE + 1, :] += _colsum(dhv * (xn * gv))
    dhp = dhv * (1.0 + _vrow(v, V_SCALE))
    dv_ref[V_GPRE:V_GPRE + 1, :] += _colsum(dhp * xn)
    dxn = dhp * gv
    return do + r * (dxn - xn * jnp.mean(dxn * xn, axis=-1, keepdims=True))


DV_SPEC = pl.BlockSpec((8, D), lambda t: (0, 0))


def _prenorm_bwd(dout, dh, x, vecs, i):
    s = dout.shape[0]
    ts = min(s, 512)

    def body(do_ref, dh_ref, x_ref, v_ref, dx_ref, dv_ref):
        _zero_at_first(pl.program_id(0) == 0, dv_ref)
        dx_ref[...] = _head_bwd(do_ref[...], dh_ref[...], x_ref[...], v_ref[...], dv_ref)

    return _pc(body, "prenorm_bwd", (s // ts,), [_row_spec(ts, D)] * 3 + [_vblock(i)], [_row_spec(ts, D), DV_SPEC],
               [_sds((s, D)), _sds((8, D))])(dout, dh, x, vecs)


def _pre_post_bwd(dout, dh, x, f_prev, vecs, i, rw_prev):
    s = dout.shape[0]
    ts = min(s, 256)

    def body(do_ref, dh_ref, x_ref, f_ref, vc_ref, vp_ref, dx_ref, df_ref, dv_ref):
        _zero_at_first(pl.program_id(0) == 0, dv_ref)
        dx = _head_bwd(do_ref[...], dh_ref[...], x_ref[...], vc_ref[...], dv_ref)
        dx_ref[...] = dx
        df_ref[...] = _tail_bwd(dx, f_ref[...], vp_ref[...], rw_prev, dv_ref)

    rows = _row_spec(ts, D)
    return _pc(body, "pre_post_bwd", (s // ts,), [rows] * 4 + [_vblock(i), _vblock(i - 1)], [rows, rows, DV_SPEC],
               [_sds((s, D)), _sds((s, D), BF16), _sds((8, D))])(dout, dh, x, f_prev, vecs, vecs)


def _last_shell(x, f, tgt, vecs, i, rw):
    s = x.shape[0]
    ts = min(s, 512)
    nt = s // ts

    def body(x_ref, f_ref, t_ref, v_ref, loss_ref, dy_ref, df_ref, dv_ref, acc_ref):
        t = pl.program_id(0)
        _zero_at_first(t == 0, dv_ref, acc_ref)
        fv, v = f_ref[...], v_ref[...]
        e = _tail(x_ref[...], fv, v, rw) - t_ref[...]
        dy = e * (1.0 / D)
        dy_ref[...] = dy
        acc_ref[...] += _colsum(e * e)
        df_ref[...] = _tail_bwd(dy, fv, v, rw, dv_ref)

        @pl.when(t == nt - 1)
        def _():
            loss_ref[...] = jnp.full((1, LANE), 0.5 / D, F32) * jnp.sum(acc_ref[...])

    rows = _row_spec(ts, D)
    return _pc(body, "last_shell", (nt,), [rows] * 3 + [_vblock(i)],
               [pl.BlockSpec((1, LANE), lambda t: (0, 0)), rows, rows, DV_SPEC],
               [_sds((1, LANE)), _sds((s, D)), _sds((s, D), BF16), _sds((8, D))], scratch=[pltpu.VMEM((1, D), F32)])(x, f, tgt, vecs)


def _sigmoid(v):
    return 1.0 / (1.0 + jnp.exp(-v))


def _ffn_in_swiglu(h, win):
    s = h.shape[0]
    ts = _tile(s)
    nt = (((1,), (1,)), ((), ()))

    def body(h_ref, wa_ref, wb_ref, fac_ref, act_ref):
        hv = h_ref[...]
        a = lax.dot_general(hv, wa_ref[...], nt, preferred_element_type=F32)
        b = lax.dot_general(hv, wb_ref[...], nt, preferred_element_type=F32)
        sg = _sigmoid(a)
        silu = a * sg
        fac_ref[0] = (b * (sg * (1.0 + a * (1.0 - sg)))).astype(BF16)
        fac_ref[1] = silu.astype(BF16)
        act_ref[...] = (silu * b).astype(BF16)

    return _pc(body, "ffn_in", (4, s // ts),
               [pl.BlockSpec((ts, D), lambda k, t: (t, 0)), pl.BlockSpec((None, FSH, D), lambda k, t: (k, 0, 0)),
                pl.BlockSpec((None, FSH, D), lambda k, t: (k + 4, 0, 0))],
               [pl.BlockSpec((2, None, ts, FSH), lambda k, t: (0, k, t, 0)), pl.BlockSpec((None, ts, FSH), lambda k, t: (k, t, 0))],
               [_sds((2, 4, s, FSH), BF16), _sds((4, s, FSH), BF16)])(h, win, win)


def _ffn_out_dx_swiglu(df, wout, fac, after):
    s = df.shape[0]
    ts = _tile(s)
    nt = (((1,), (1,)), ((), ()))

    def body(df_ref, w_ref, fac_ref, after_ref, o_ref):
        d = lax.dot_general(df_ref[...], w_ref[...], nt, preferred_element_type=F32)
        o_ref[0] = (d * fac_ref[0]).astype(BF16)
        o_ref[1] = (d * fac_ref[1]).astype(BF16)

    spec = pl.BlockSpec((2, None, ts, FSH), lambda k, t: (0, k, t, 0))
    out = _pc(body, "ffn_out_dx", (4, s // ts),
              [pl.BlockSpec((ts, D), lambda k, t: (t, 0)), pl.BlockSpec((None, FSH, D), lambda k, t: (k, 0, 0)), spec,
               pl.BlockSpec(memory_space=pl.ANY)],
              spec, _sds((2, 4, s, FSH), BF16))(df, wout, fac, after)
    return out.reshape(N_DEV, s, FSH)


def _ffn_fwd(h, win, wout_of):
    s = h.shape[0]
    fac, act = _ffn_in_swiglu(h, win)
    f = _mm_sum("ffn_out", act, wout_of(act).reshape(4, FSH, D), min(s, 512))
    return f, (h, fac, act)


def _ffn_bwd(df, saved, win, wout, send, after):
    h, fac, act = saved
    s = h.shape[0]
    ts = s
    wout = wout.reshape(4, FSH, D)
    dwout = _mm("ffn_out_dw", act, df, ((0,), (0,)), (4, 2),
                pl.BlockSpec((None, s, FSH), lambda k, j: (k, 0, 0)), pl.BlockSpec((s, D // 2), lambda k, j: (0, j)),
                pl.BlockSpec((None, FSH, D // 2), lambda k, j: (k, 0, j)), _sds((4, FSH, D), BF16), after=after)
    dz = _ffn_out_dx_swiglu(df, wout, fac, send("w_out", dwout.reshape(N_DEV, D_FF // N_DEV, D)))
    dwin = _mm("ffn_in_dw", dz, h, ((0,), (0,)), (N_DEV, 2),
               pl.BlockSpec((None, s, FSH), lambda j, i: (j, 0, 0)), pl.BlockSpec((s, D // 2), lambda j, i: (0, i)),
               pl.BlockSpec((None, FSH, D // 2), lambda j, i: (j, 0, i)), _sds((N_DEV, FSH, D), BF16))
    return _mm_sum("ffn_in_dx", dz, win, min(s, 512), after=send("w_in", dwin))


def _shift_rows(v, k, row, s, back):
    if back:
        return jnp.where(row < s - k, pltpu.roll(v, s - k, 0), 0.0)
    return jnp.where(row >= k, pltpu.roll(v, k, 0), 0.0)


def _window_sum(v, w, row, s, back):
    k = 1
    while k < w:
        v = v + _shift_rows(v, k, row, s, back)
        k *= 2
    return v


def _pool_fwd(z, pool_w, pool_scale):
    s = z.shape[0]

    def body(z_ref, w_ref, sc_ref, y_ref, d_ref):
        row = lax.broadcasted_iota(jnp.int32, (s, HD), 0)
        for g, w in enumerate(POOL_WINDOWS):
            sl = slice(g * HD, (g + 1) * HD)
            a = z_ref[:, sl]
            cnt = jnp.minimum(row + 1, w).astype(F32)
            d = (_window_sum(a, w, row, s, False) / cnt - a).astype(BF16)
            d_ref[:, sl] = d
            y = jnp.dot(d, _bf(w_ref[g]), preferred_element_type=F32)
            y_ref[:, sl] = (y * sc_ref[:, sl]).astype(BF16)

    return _pc(body, "pool_fwd", (1,),
               [pl.BlockSpec((s, NH * HD), lambda i: (0, 0)), pl.BlockSpec((NH, HD, HD), lambda i: (0, 0, 0)),
                pl.BlockSpec((1, NH * HD), lambda i: (0, 0))],
               [pl.BlockSpec((s, NH * HD), lambda i: (0, 0))] * 2,
               [_sds((s, NH * HD), BF16)] * 2)(z, pool_w, pool_scale)


def _pool_bwd(dy, d, pool_w, pool_scale):
    s = dy.shape[0]

    def body(dy_ref, d_ref, w_ref, sc_ref, dz_ref, dw_ref, dsc_ref):
        row = lax.broadcasted_iota(jnp.int32, (s, HD), 0)
        for g, w in enumerate(POOL_WINDOWS):
            sl = slice(g * HD, (g + 1) * HD)
            dyg, dg, wg = dy_ref[:, sl], d_ref[:, sl], _bf(w_ref[g])
            yraw = jnp.dot(dg, wg, preferred_element_type=F32)
            dsc_ref[:, sl] = _colsum(dyg * yraw)
            dyr = _bf(dyg * sc_ref[:, sl])
            dw_ref[g] = lax.dot_general(dg, dyr, (((0,), (0,)), ((), ())), preferred_element_type=F32)
            dd = lax.dot_general(dyr, wg, (((1,), (1,)), ((), ())), preferred_element_type=F32)
            cnt = jnp.minimum(row + 1, w).astype(F32)
            dz_ref[:, sl] = (_window_sum(dd / cnt, w, row, s, True) - dd).astype(BF16)

    return _pc(body, "pool_bwd", (1,),
               [pl.BlockSpec((s, NH * HD), lambda i: (0, 0)), pl.BlockSpec((s, NH * HD), lambda i: (0, 0)),
                pl.BlockSpec((NH, HD, HD), lambda i: (0, 0, 0)), pl.BlockSpec((1, NH * HD), lambda i: (0, 0))],
               [pl.BlockSpec((s, NH * HD), lambda i: (0, 0)), pl.BlockSpec((NH, HD, HD), lambda i: (0, 0, 0)),
                pl.BlockSpec((1, NH * HD), lambda i: (0, 0))],
               [_sds((s, NH * HD), BF16), _sds((NH, HD, HD)), _sds((1, NH * HD))])(dy, d, pool_w, pool_scale)


def _gelu(v):
    return 0.5 * v * (1.0 + jnp.tanh(GELU_C * (v + 0.044715 * (v * v * v))))


def _gelu_and_grad(v):
    t = jnp.tanh(GELU_C * (v + 0.044715 * (v * v * v)))
    return 0.5 * v * (1.0 + t), 0.5 * (1.0 + t) + 0.5 * v * (1.0 - t * t) * (GELU_C * (1.0 + 3.0 * 0.044715 * (v * v)))


def _gelu_grad(v):
    return _gelu_and_grad(v)[1]


def _causal_mask():
    return lax.broadcasted_iota(jnp.int32, (HD, HD), 0) >= lax.broadcasted_iota(jnp.int32, (HD, HD), 1)


def _sgu_specs():
    w = NH * HD
    return [pl.BlockSpec((HD, w), lambda c: (c, 1)), pl.BlockSpec((HD, w), lambda c: (c, 2)),
            pl.BlockSpec((1, w), lambda c: (0, 0)), pl.BlockSpec((1, w), lambda c: (0, 0)),
            pl.BlockSpec((NH, HD, HD), lambda c: (0, 0, 0)), pl.BlockSpec((HD, LANE), lambda c: (0, 0))]


def _sgu_head(v, lng_ref, lnb_ref, w_ref, h):
    sl = slice(h * HD, (h + 1) * HD)
    vh = v[:, sl]
    xc = vh - jnp.mean(vh, axis=-1, keepdims=True)
    rs = lax.rsqrt(jnp.mean(xc * xc, axis=-1, keepdims=True) + EPS)
    vhat = xc * rs
    vn = _bf(vhat * lng_ref[:, sl] + lnb_ref[:, sl])
    wc = _bf(jnp.where(_causal_mask(), w_ref[h], 0.0))
    return sl, rs, vhat, vn, wc


def _sgu_fwd(z, ln_g, ln_b, sgu_w, sgu_bt):
    s = z.shape[0]

    def body(zu_ref, zv_ref, lng_ref, lnb_ref, w_ref, bt_ref, y_ref):
        u, v = _gelu(zu_ref[...]), _gelu(zv_ref[...])
        for h in range(NH):
            sl, _, _, vn, wc = _sgu_head(v, lng_ref, lnb_ref, w_ref, h)
            sp = jnp.dot(wc, vn, preferred_element_type=F32) + bt_ref[:, h:h + 1]
            y_ref[:, sl] = (u[:, sl] * sp).astype(BF16)

    return _pc(body, "sgu_fwd", (s // HD,), _sgu_specs(), pl.BlockSpec((HD, NH * HD), lambda c: (c, 0)),
               _sds((s, NH * HD), BF16))(z, z, ln_g, ln_b, sgu_w, sgu_bt)


def _sgu_bwd(z, dy, ln_g, ln_b, sgu_w, sgu_bt, head_sum):
    s = z.shape[0]
    w = NH * HD
    nc = s // HD

    def body(zu_ref, zv_ref, lng_ref, lnb_ref, w_ref, bt_ref, dy_ref, hs_ref,
             dzu_ref, dzv_ref, dlng_ref, dlnb_ref, dw_ref, dbt_ref, dsacc_ref):
        c = pl.program_id(0)
        _zero_at_first(c == 0, dsacc_ref, dw_ref, dlng_ref, dlnb_ref)
        zu, zv = zu_ref[...], zv_ref[...]
        (u, gu), (v, gv) = _gelu_and_grad(zu), _gelu_and_grad(zv)
        dyv = dy_ref[...]
        ds = dyv * u
        dsacc_ref[...] += ds
        for h in range(NH):
            sl, rs, vhat, vn, wc = _sgu_head(v, lng_ref, lnb_ref, w_ref, h)
            sp = jnp.dot(wc, vn, preferred_element_type=F32) + bt_ref[:, h:h + 1]
            dzu_ref[:, sl] = (dyv[:, sl] * sp * gu[:, sl]).astype(BF16)
            dsh = _bf(ds[:, sl])
            dwh = lax.dot_general(dsh, vn, (((1,), (1,)), ((), ())), preferred_element_type=F32)
            dw_ref[h] += jnp.where(_causal_mask(), dwh, 0.0)
            dvn = lax.dot_general(wc, dsh, (((0,), (0,)), ((), ())), preferred_element_type=F32)
            dlng_ref[:, sl] += _colsum(dvn * vhat)
            dlnb_ref[:, sl] += _colsum(dvn)
            dvh = dvn * lng_ref[:, sl]
            dv = rs * (dvh - jnp.mean(dvh, axis=-1, keepdims=True) - vhat * jnp.mean(dvh * vhat, axis=-1, keepdims=True))
            dzv_ref[:, sl] = (dv * gv[:, sl]).astype(BF16)

        @pl.when(c == nc - 1)
        def _():
            dbt_ref[...] = jnp.dot(dsacc_ref[...], hs_ref[...], preferred_element_type=F32, precision=HIGHEST)

    outs = _pc(body, "sgu_bwd", (nc,),
               _sgu_specs() + [pl.BlockSpec((HD, w), lambda c: (c, 1)), pl.BlockSpec((w, LANE), lambda c: (0, 0))],
               [pl.BlockSpec((HD, w), lambda c: (c, 0))] * 2 + [pl.BlockSpec((1, w), lambda c: (0, 0))] * 2
               + [pl.BlockSpec((NH, HD, HD), lambda c: (0, 0, 0)), pl.BlockSpec((HD, LANE), lambda c: (0, 0))],
               [_sds((s, w), BF16)] * 2 + [_sds((1, w))] * 2 + [_sds((NH, HD, HD)), _sds((HD, LANE))],
               scratch=[pltpu.VMEM((HD, w), F32)])(z, z, ln_g, ln_b, sgu_w, sgu_bt, dy, head_sum)
    return outs


def _cmul(ar, ai, br, bi):
    return ar * br - ai * bi, ar * bi + ai * br


def _ssm_prep(lam_re, lam_im, lam_re_rep, lam_im_rep, log_dt, b_re, b_im):
    def disc(lr, li, dt):
        mag = jnp.exp(lr * dt)
        return mag * jnp.cos(li * dt), mag * jnp.sin(li * dt)

    def body(lr_ref, li_ref, lrr_ref, lir_ref, ldt_ref, br_ref, bi_ref, or_ref, oi_ref, bbr_ref, bbi_ref):
        dt = jnp.exp(ldt_ref[...])
        or_ref[...], oi_ref[...] = disc(lr_ref[...], li_ref[...], dt)
        lr, li = lrr_ref[...], lir_ref[...]
        er, ei = disc(lr, li, dt)
        den = lr * lr + li * li
        kr = ((er - 1.0) * lr + ei * li) / den
        ki = (ei * lr - (er - 1.0) * li) / den
        bbr_ref[...], bbi_ref[...] = _cmul(kr, ki, br_ref[...], bi_ref[...])

    small = pl.BlockSpec((SSM_G, SSM_P), lambda i: (0, 0))
    wide = pl.BlockSpec((SSM_G, SSM_P * SSM_N), lambda i: (0, 0))
    col = pl.BlockSpec((SSM_G, 1), lambda i: (0, 0))
    return _pc(body, "ssm_prep", (1,), [small, small, wide, wide, col, wide, wide], [small, small, wide, wide],
               [_sds((SSM_G, SSM_P))] * 2 + [_sds((SSM_G, SSM_P * SSM_N))] * 2)(
        lam_re, lam_im, lam_re_rep, lam_im_rep, log_dt, b_re, b_im)


def _ssm_param_bwd(g_lam_re, g_lam_im, g_bb_re, g_bb_im, lam_re, lam_im, lam_re_rep, lam_im_rep, log_dt, b_re, b_im, seg):
    def body(glr_ref, gli_ref, gbr_ref, gbi_ref, lr_ref, li_ref, lrr_ref, lir_ref, ldt_ref, br_ref, bi_ref, seg_ref,
             dlr_ref, dli_ref, ddt_ref, dbr_ref, dbi_ref):
        dt = jnp.exp(ldt_ref[...])
        lr, li = lrr_ref[...], lir_ref[...]
        mag = jnp.exp(lr * dt)
        er, ei = mag * jnp.cos(li * dt), mag * jnp.sin(li * dt)
        den = lr * lr + li * li
        kr = ((er - 1.0) * lr + ei * li) / den
        ki = (ei * lr - (er - 1.0) * li) / den
        gbr, gbi = gbr_ref[...], gbi_ref[...]
        dbr_ref[...], dbi_ref[...] = _cmul(kr, -ki, gbr, gbi)
        tr, ti = _cmul(br_ref[...], -bi_ref[...], gbr, gbi)
        gkr = jnp.dot(tr, seg_ref[...], preferred_element_type=F32, precision=HIGHEST)
        gki = jnp.dot(ti, seg_ref[...], preferred_element_type=F32, precision=HIGHEST)
        lr, li = lr_ref[...], li_ref[...]
        mag = jnp.exp(lr * dt)
        er, ei = mag * jnp.cos(li * dt), mag * jnp.sin(li * dt)
        den = lr * lr + li * li
        ir, ii = lr / den, -li / den
        kr, ki = _cmul(er - 1.0, ei, ir, ii)
        ar, ai = _cmul(ir, -ii, gkr, gki)
        glr, gli = glr_ref[...] + ar, gli_ref[...] + ai
        qr, qi = _cmul(kr, ki, ir, ii)
        g1r, g1i = _cmul(-qr, qi, gkr, gki)
        g2r, g2i = _cmul(dt * er, -dt * ei, glr, gli)
        dlr_ref[...] = g1r + g2r
        dli_ref[...] = g1i + g2i
        wr, wi = _cmul(lr, li, er, ei)
        g_dt = jnp.sum(wr * glr + wi * gli, axis=-1, keepdims=True)
        ddt_ref[...] = jnp.broadcast_to(dt * g_dt, (SSM_G, LANE))

    small = pl.BlockSpec((SSM_G, SSM_P), lambda i: (0, 0))
    wide = pl.BlockSpec((SSM_G, SSM_P * SSM_N), lambda i: (0, 0))
    col = pl.BlockSpec((SSM_G, 1), lambda i: (0, 0))
    segs = pl.BlockSpec((SSM_P * SSM_N, SSM_P), lambda i: (0, 0))
    return _pc(body, "ssm_param_bwd", (1,), [small, small, wide, wide, small, small, wide, wide, col, wide, wide, segs],
               [small, small, pl.BlockSpec((SSM_G, LANE), lambda i: (0, 0)), wide, wide],
               [_sds((SSM_G, SSM_P))] * 2 + [_sds((SSM_G, LANE))] + [_sds((SSM_G, SSM_P * SSM_N))] * 2)(
        g_lam_re, g_lam_im, g_bb_re, g_bb_im, lam_re, lam_im, lam_re_rep, lam_im_rep, log_dt, b_re, b_im, seg)


SCAN_LANES = 512
SCAN_ROWS = 8


SCAN_GROUPS = SCAN_LANES // SSM_P
SCAN_COLS = SCAN_GROUPS * SSM_N
SCAN_CHUNK = 1024


def _ssm_scan(name, v, w_in, lam_re, lam_im, w_out, reverse, states=None, u=None):
    s = v.shape[0]
    ln, rows, ch = SCAN_LANES, SCAN_ROWS, min(SCAN_CHUNK, s)
    nch, ntile = s // ch, ch // rows
    nt_dims = (((1,), (1,)), ((), ()))
    with_sum = states is not None
    tn_dims = (((0,), (0,)), ((), ()))

    def body(*refs):
        v_ref, win_ref, lr_ref, li_ref, wout_ref = refs[:5]
        if with_sum:
            n_in = 8
            y_ref, sum_refs = refs[n_in], refs[n_in + 1:n_in + 7]
            br_s, bi_s, mb_s, mc_s, or_ref, oi_ref = refs[n_in + 7:]
            mb_s[...] = jnp.zeros_like(mb_s)
            mc_s[...] = jnp.zeros_like(mc_s)
        else:
            n_in = 5
            or_ref, oi_ref, y_ref, br_s, bi_s = refs[n_in:]
        l1 = (lr_ref[...], li_ref[...])
        pw = [l1]
        for _ in range(rows - 1):
            pw.append(_cmul(*pw[-1], *l1))
        row = lax.broadcasted_iota(jnp.int32, (rows, ln), 0)
        expo = (rows - row) if reverse else (row + 1)
        pr = jnp.zeros((rows, ln), F32)
        pi = jnp.zeros((rows, ln), F32)
        for e in range(1, rows + 1):
            pr = jnp.where(expo == e, pw[e - 1][0], pr)
            pi = jnp.where(expo == e, pw[e - 1][1], pi)
        lk = {}
        for k in (1, 2, 4):
            keep = (row < rows - k) if reverse else (row >= k)
            lk[k] = (jnp.where(keep, pw[k - 1][0], 0.0), jnp.where(keep, pw[k - 1][1], 0.0))

        def chunk(c, carry):
            q0 = pl.multiple_of(((nch - 1 - c) if reverse else c) * ch, ch)
            b = jnp.dot(_bf(v_ref[pl.ds(q0, ch), :]), win_ref[...], preferred_element_type=F32)
            br_s[...] = b[:, :ln]
            bi_s[...] = b[:, ln:]

            def step(i, carry):
                cr, ci = carry[:2]
                r0 = pl.multiple_of(((ntile - 1 - i) if reverse else i) * rows, rows)
                xr, xi = br_s[pl.ds(r0, rows), :], bi_s[pl.ds(r0, rows), :]
                for k in (1, 2, 4):
                    shift = rows - k if reverse else k
                    ar, ai = _cmul(lk[k][0], lk[k][1], pltpu.roll(xr, shift, 0), pltpu.roll(xi, shift, 0))
                    xr, xi = xr + ar, xi + ai
                ar, ai = _cmul(pr, pi, cr, ci)
                xr, xi = xr + ar, xi + ai
                g0 = pl.multiple_of(q0 + r0, rows)
                or_ref[pl.ds(g0, rows), :] = xr
                oi_ref[pl.ds(g0, rows), :] = xi
                if not with_sum:
                    return (xr[rows - 1:rows], xi[rows - 1:rows]) if not reverse else (xr[0:1], xi[0:1])
                nr = jnp.where(row == rows - 1, cr, pltpu.roll(xr, rows - 1, 0))
                ni = jnp.where(row == rows - 1, ci, pltpu.roll(xi, rows - 1, 0))
                sr, si = refs[5][pl.ds(g0, rows), :], refs[6][pl.ds(g0, rows), :]
                return xr[0:1], xi[0:1], carry[2] + (sr * nr + si * ni), carry[3] + (sr * ni - si * nr)

            carry = lax.fori_loop(0, ntile, step, carry)
            if with_sum:
                rows_c = pl.ds(q0, ch)
                uc, vc = _bf(refs[7][rows_c, :]), _bf(v_ref[rows_c, :])
                for scr, left, (right_re, right_im) in ((mb_s, uc, (or_ref, oi_ref)), (mc_s, vc, (refs[5], refs[6]))):
                    scr[:, :ln] += lax.dot_general(left, _bf(right_re[rows_c, :]), tn_dims, preferred_element_type=F32)
                    scr[:, ln:] += lax.dot_general(left, _bf(right_im[rows_c, :]), tn_dims, preferred_element_type=F32)
            w = wout_ref[...]
            y_ref[pl.ds(q0, ch), :] = (
                lax.dot_general(_bf(or_ref[pl.ds(q0, ch), :]), w[:, :ln], nt_dims, preferred_element_type=F32)
                + lax.dot_general(_bf(oi_ref[pl.ds(q0, ch), :]), w[:, ln:], nt_dims, preferred_element_type=F32))
            return carry

        zero = jnp.zeros((1, ln), F32)
        init = (zero, zero) + ((jnp.zeros((rows, ln), F32),) * 2 if with_sum else ())
        carry = lax.fori_loop(0, nch, chunk, init)
        if with_sum:
            sum_refs[0][...] = _colsum(carry[2])
            sum_refs[1][...] = _colsum(carry[3])
            row_g = lax.broadcasted_iota(jnp.int32, (SCAN_COLS, LANE), 0) // SSM_N
            lane_g = lax.broadcasted_iota(jnp.int32, (SCAN_COLS, LANE), 1) // SSM_P
            for scr, o_re, o_im in ((mb_s, sum_refs[2], sum_refs[3]), (mc_s, sum_refs[4], sum_refs[5])):
                for part, o_ref in enumerate((o_re, o_im)):
                    fold = jnp.zeros((SCAN_COLS, LANE), F32)
                    for cb in range(ln // LANE):
                        fold = fold + jnp.where(2 * cb + lane_g == row_g, scr[:, part * ln + cb * LANE:part * ln + (cb + 1) * LANE], 0.0)
                    o_ref[...] = jnp.where(row_g % 2 == 0, fold, pltpu.roll(fold, SSM_P, 1))

    vec = pl.BlockSpec((1, ln), lambda j: (0, j))
    blk = pl.BlockSpec((s, ln), lambda j: (0, j))
    cols = pl.BlockSpec((s, SCAN_COLS), lambda j: (0, j))
    wspec = pl.BlockSpec((None, SCAN_COLS, 2 * ln), lambda j: (j, 0, 0))
    ins, args = [cols, wspec, vec, vec, wspec], [v, w_in, lam_re, lam_im, w_out]
    outs, shapes = [blk, blk, cols], [_sds((s, SSM_L))] * 2 + [_sds((s, SSM_G * SSM_N))]
    scratch = [pltpu.VMEM((ch, ln), F32)] * 2
    if with_sum:
        own = pl.BlockSpec((None, SCAN_COLS, LANE), lambda j: (j, 0, 0))
        ins, args = ins + [blk, blk, cols], args + list(states) + [u]
        outs = [cols, vec, vec] + [own] * 4
        shapes = [_sds((s, SSM_G * SSM_N))] + [_sds((1, SSM_L))] * 2 + [_sds((SSM_L // ln, SCAN_COLS, LANE))] * 4
        scratch = scratch + [pltpu.VMEM((SCAN_COLS, 2 * ln), F32)] * 2 + [pltpu.VMEM((s, ln), F32)] * 2
    return _pc(body, name, (SSM_L // ln,), ins, outs, shapes, scratch=scratch)(*args)


def _ssm_act_fwd(y, u, d_skip):
    s = y.shape[0]
    ts = min(s, 512)

    def body(y_ref, u_ref, d_ref, o_ref):
        o_ref[...] = _gelu(y_ref[...] + d_ref[...] * u_ref[...]).astype(BF16)

    return _pc(body, "ssm_act_fwd", (s // ts,), [_row_spec(ts, D)] * 2 + [_vec_spec(D)], _row_spec(ts, D),
               _sds((s, D), BF16))(y, u, d_skip)


def _ssm_act_bwd(dg, y, u, d_skip):
    s = y.shape[0]
    ts = min(s, 512)

    def body(dg_ref, y_ref, u_ref, d_ref, dy_ref, dd_ref):
        uv = u_ref[...]
        dy = dg_ref[...] * _gelu_grad(y_ref[...] + d_ref[...] * uv)
        dy_ref[...] = dy.astype(BF16)
        _acc(dd_ref, pl.program_id(0) == 0, _colsum(dy * uv))

    return _pc(body, "ssm_act_bwd", (s // ts,), [_row_spec(ts, D)] * 3 + [_vec_spec(D)], [_row_spec(ts, D), _vec_spec(D)],
               [_sds((s, D), BF16), _sds((1, D))])(dg, y, u, d_skip)


def _axpy(a, b, d_skip):
    s = a.shape[0]
    ts = min(s, 512)

    def body(a_ref, b_ref, d_ref, o_ref):
        o_ref[...] = (a_ref[...] + d_ref[...] * b_ref[...].astype(F32)).astype(BF16)

    return _pc(body, "ssm_du", (s // ts,), [_row_spec(ts, D)] * 2 + [_vec_spec(D)], _row_spec(ts, D),
               _sds((s, D), BF16))(a, b, d_skip)


def _glu_fwd(zz):
    s = zz.shape[0]
    ts = min(s, 512)

    def body(a_ref, b_ref, o_ref):
        o_ref[...] = a_ref[...] * _sigmoid(b_ref[...])

    return _pc(body, "glu_fwd", (s // ts,), [_row_spec(ts, D, 0), _row_spec(ts, D, 1)], _row_spec(ts, D), _sds((s, D)))(zz, zz)


def _glu_bwd(zz, df):
    s = zz.shape[0]
    ts = min(s, 512)

    def body(a_ref, b_ref, df_ref, o_ref):
        sg = _sigmoid(b_ref[...])
        dfv = df_ref[...].astype(F32)
        o_ref[:, :D] = (dfv * sg).astype(BF16)
        o_ref[:, D:] = (dfv * a_ref[...] * sg * (1.0 - sg)).astype(BF16)

    return _pc(body, "glu_bwd", (s // ts,), [_row_spec(ts, D, 0), _row_spec(ts, D, 1), _row_spec(ts, D)],
               _row_spec(ts, 2 * D), _sds((s, 2 * D), BF16))(zz, zz, df)


def _ssm_block_diag(m_re, m_im):
    rows, half = SCAN_COLS, SCAN_LANES
    expand = jnp.tile(jnp.eye(SSM_P, dtype=BF16), (1, SCAN_GROUPS))

    def body(mr_ref, mi_ref, e_ref, o_ref):
        keep = (lax.broadcasted_iota(jnp.int32, (rows, half), 0) // SSM_N
                == lax.broadcasted_iota(jnp.int32, (rows, half), 1) // SSM_P)
        for part, m_ref in enumerate((mr_ref, mi_ref)):
            t = jnp.dot(_bf(m_ref[...]), e_ref[...], preferred_element_type=F32)
            o_ref[:, part * half:(part + 1) * half] = jnp.where(keep, t, 0.0).astype(BF16)

    blk = pl.BlockSpec((rows, SSM_P), lambda q: (q, 0))
    nb = SSM_G // SCAN_GROUPS
    return _pc(body, "ssm_block_diag", (nb,), [blk, blk, pl.BlockSpec((SSM_P, half), lambda q: (0, 0))],
               pl.BlockSpec((None, rows, 2 * half), lambda q: (q, 0, 0)), _sds((nb, rows, 2 * half), BF16))(m_re, m_im, expand)


def _mod_part(c_all, ada_w):
    n = ada_w.shape[-1]

    def body(c_ref, w_ref, o_ref):
        cv = c_ref[...]
        cond = _bf(cv * _sigmoid(cv))
        o_ref[...] = jnp.dot(cond, _bf(w_ref[...]), preferred_element_type=F32)

    return _pc(body, "mod_part", (2,), [pl.BlockSpec((N_DEV, D), lambda l: (0, 0)), pl.BlockSpec((None, D, n), lambda l: (l, 0, 0))],
               pl.BlockSpec((None, N_DEV, n), lambda l: (l, 0, 0)), _sds((2, N_DEV, n)))(c_all, ada_w)


def _ada_w_grad(c_all_t, dmod):
    nl, _, n = dmod.shape
    tr = 128

    def body(c_ref, d_ref, o_ref):
        cv = c_ref[...]
        cond = _bf(cv * _sigmoid(cv)).astype(F32)
        dm = _bf(d_ref[...]).astype(F32)
        acc = cond[:, 0:1] * dm[0:1, :]
        for b in range(1, N_DEV):
            acc = acc + cond[:, b:b + 1] * dm[b:b + 1, :]
        o_ref[...] = acc

    return _pc(body, "ada_w_grad", (nl, D // tr),
               [pl.BlockSpec((tr, N_DEV), lambda l, t: (t, 0)), pl.BlockSpec((None, N_DEV, n), lambda l, t: (l, 0, 0))],
               pl.BlockSpec((None, tr, n), lambda l, t: (l, t, 0)), _sds((nl, D, n)))(c_all_t, dmod)


def _adamw(name, parts, w, m, v, slot=0, prev=None, after=None):
    p, r, c = parts.shape
    tr = r
    while tr * c * 4 > (1 << 21) and tr % 16 == 0:
        tr //= 2
    nt = r // tr

    def body(p_ref, w_ref, m_ref, v_ref, *rest):
        g_ref, d_ref, nm_ref, nv_ref = rest[-4:]
        g = p_ref[0].astype(F32)
        for i in range(1, p):
            g = g + p_ref[i].astype(F32)
        g_ref[...] = g
        d_ref[...], nm_ref[...], nv_ref[...] = _adam_update(g, w_ref[...], m_ref[...], v_ref[...])

    blk = pl.BlockSpec((tr, c), lambda t: (slot * nt + t, 0))
    in_specs = [pl.BlockSpec((p, tr, c), lambda t: (0, t, 0)), blk, blk, blk]
    unread = list(prev or []) + ([after] if after is not None else [])
    return pl.pallas_call(
        body, name=name, grid=(nt,), in_specs=in_specs + [pl.BlockSpec(memory_space=pl.ANY)] * len(unread), out_specs=[blk] * 4,
        out_shape=[_sds(w.shape)] * 4, input_output_aliases={4 + i: i for i in range(4)} if prev else {},
        compiler_params=pltpu.CompilerParams(dimension_semantics=("arbitrary",), vmem_limit_bytes=VMEM_LIMIT_BYTES))(parts, w, m, v, *unread)


def _adam_update(g, w, m, v):
    m2 = B1 * m + (1.0 - B1) * g
    v2 = B2 * v + (1.0 - B2) * (g * g)
    m_hat = m2 / (1.0 - B1 ** STEP)
    v_hat = v2 / (1.0 - B2 ** STEP)
    return -LR * (m_hat / (jnp.sqrt(v_hat) + ADAM_EPS) + WD * w), m2, v2


def _adamw_many(name, items, after):
    n = len(items)

    def body(*refs):
        outs = refs[4 * n + 1:]
        for i in range(n):
            g, w, m, v = (r[...] for r in refs[4 * i:4 * i + 4])
            for o, val in zip(outs[3 * i:3 * i + 3], _adam_update(g, w, m, v)):
                o[...] = val

    full = lambda a: pl.BlockSpec(a.shape, lambda t: (0, 0))
    flat = [a for item in items for a in item]
    res = _pc(body, name, (1,), [full(a) for a in flat] + [pl.BlockSpec(memory_space=pl.ANY)],
              [full(item[1]) for item in items for _ in range(3)],
              [_sds(item[1].shape) for item in items for _ in range(3)])(*flat, after)
    return [tuple(res[3 * i:3 * i + 3]) for i in range(n)]


def _sum_parts(parts):
    p, r, c = parts.shape
    tr = r
    while tr * c * 4 > (1 << 19) and tr % 16 == 0:
        tr //= 2

    def body(p_ref, o_ref):
        g = p_ref[0]
        for i in range(1, p):
            g = g + p_ref[i]
        o_ref[...] = g

    return _pc(body, "sum_parts", (r // tr,), [pl.BlockSpec((p, tr, c), lambda t: (0, t, 0))], pl.BlockSpec((tr, c), lambda t: (t, 0)),
               _sds((r, c)))(parts)


def _place():
    x, y, c = lax.axis_index("x"), lax.axis_index("y"), lax.axis_index("c")
    peers = []
    for k in range(1, N_DEV):
        px = (1 - x) if k & 4 else x
        py = (1 - y) if k & 2 else y
        pc = (1 - c) if k & 1 else c
        peers.append(((px, py, pc), 4 * px + 2 * py + pc))
    return 4 * x + 2 * y + c, peers


def _at(ref, idx):
    return ref if idx is None else ref.at[idx]


def _exchange_copies(plan, n, src_refs, dst_refs, send_sems, recv_sems, local_sems=None, with_arrivals=True):
    me, peers = _place()
    local = [] if local_sems is None else [
        pltpu.make_async_copy(_at(src_refs[si], sx), _at(dst_refs[di], dx), local_sems.at[i])
        for i, (si, sx, di, dx) in enumerate(plan(me, me, 0))]

    def remote(k, i, dev, entry):
        si, sx, di, dx = entry
        return pltpu.make_async_remote_copy(_at(src_refs[si], sx), _at(dst_refs[di], dx), send_sems.at[k * n + i], recv_sems.at[k * n + i],
                                            device_id=dev, device_id_type=MESH)

    sends = [remote(k, i, dev, e) for k, (dev, peer) in enumerate(peers) for i, e in enumerate(plan(me, peer, k + 1))]
    if not with_arrivals:
        return local, sends, []
    arrivals = [remote(k, i, dev, e) for k, (dev, peer) in enumerate(peers) for i, e in enumerate(plan(peer, me, k + 1))]
    return local, sends, arrivals


def _sem_shapes(n_copies, local=True):
    sems = [pltpu.SemaphoreType.DMA(((N_DEV - 1) * n_copies,)), pltpu.SemaphoreType.DMA(((N_DEV - 1) * n_copies,))]
    return sems + [pltpu.SemaphoreType.DMA((n_copies,))] if local else sems


def _exchange(name, srcs, dst_shapes, plan, n_copies):
    ns, nd = len(srcs), len(dst_shapes)

    def body(*refs):
        local, sends, arrivals = _exchange_copies(plan, n_copies, refs[:ns], refs[ns:ns + nd], *refs[ns + nd:])
        for cp in local + sends:
            cp.start()
        for cp in arrivals:
            cp.wait_recv()
        for cp in sends:
            cp.wait_send()
        for cp in local:
            cp.wait()

    any_spec = pl.BlockSpec(memory_space=pl.ANY)
    return pl.pallas_call(
        body, name=name, in_specs=[any_spec] * ns, out_specs=[any_spec] * nd, out_shape=list(dst_shapes),
        scratch_shapes=_sem_shapes(n_copies))(*srcs)


HBM_SPEC = pl.BlockSpec(memory_space=pltpu.HBM)
SEM_SPEC = pl.BlockSpec(memory_space=pltpu.SEMAPHORE)
ANY_SPEC = pl.BlockSpec(memory_space=pl.ANY)
TOKEN_SPEC = pl.BlockSpec(memory_space=pltpu.VMEM)
SIDE_EFFECT = pltpu.SideEffectType.DATAFLOW_SIDE_EFFECTING


def _wait_all(local, sends, arrivals):
    for cp in arrivals:
        cp.wait_recv()
    for cp in sends:
        cp.wait_send()
    for cp in local:
        cp.wait()


def _exchange_start(name, srcs, dst_shapes, plan, n_copies, order):
    ns, nd = len(srcs), len(dst_shapes)
    nb = ns + nd

    def body(*refs):
        local, sends, _ = _exchange_copies(plan, n_copies, refs[:ns], refs[ns:nb], *refs[nb + 1:nb + 4], with_arrivals=False)
        for cp in local + sends:
            cp.start()
        refs[-1][...] = jnp.zeros((8, LANE), F32)

    lands = [pltpu.with_memory_space_constraint(lax.empty(d.shape, d.dtype), pltpu.HBM) for d in dst_shapes]
    srcs = [pltpu.with_memory_space_constraint(a, pltpu.HBM) for a in srcs]
    bufs = srcs + lands
    out = pl.pallas_call(
        body, name=name, in_specs=[HBM_SPEC] * nb + [ANY_SPEC],
        out_specs=[SEM_SPEC] * 3 + [HBM_SPEC] * nb + [TOKEN_SPEC],
        out_shape=_sem_shapes(n_copies) + [pltpu.HBM(a.shape, a.dtype) for a in bufs] + [_sds((8, LANE))],
        input_output_aliases={i: 3 + i for i in range(nb)},
        compiler_params=pltpu.CompilerParams(has_side_effects=SIDE_EFFECT))(*bufs, order)
    return out[:3], out[3:3 + ns], out[3 + ns:3 + nb], out[-1]


def _exchange_relay(name, sems, srcs, lands, plan, n_copies, plan2, n_copies2, after):
    ns, nd = len(srcs), len(lands)
    nb = ns + nd

    def body(*refs):
        land_refs = refs[ns:nb]
        _wait_all(*_exchange_copies(plan, n_copies, refs[:ns], land_refs, *refs[nb:nb + 3]))
        _, sends, _ = _exchange_copies(plan2, n_copies2, land_refs, land_refs, *refs[nb + 4:nb + 6], with_arrivals=False)
        for cp in sends:
            cp.start()
        refs[-1][...] = jnp.zeros((8, LANE), F32)

    out = pl.pallas_call(
        body, name=name, in_specs=[HBM_SPEC] * nb + [SEM_SPEC] * 3 + [ANY_SPEC],
        out_specs=[SEM_SPEC] * 2 + [HBM_SPEC] * nd + [TOKEN_SPEC],
        out_shape=_sem_shapes(n_copies2, local=False) + [pltpu.HBM(a.shape, a.dtype) for a in lands] + [_sds((8, LANE))],
        input_output_aliases={ns + i: 2 + i for i in range(nd)},
        compiler_params=pltpu.CompilerParams(has_side_effects=SIDE_EFFECT))(*srcs, *lands, *sems, after)
    return out[:2], out[2:2 + nd], out[-1]


def _exchange_wait(name, sems, srcs, lands, plan, n_copies, after):
    srcs = [] if srcs is None else list(srcs)
    ns, nd = len(srcs), len(lands)
    nb = ns + nd

    def body(*refs):
        land_refs = refs[ns:nb]
        _wait_all(*_exchange_copies(plan, n_copies, refs[:ns] if ns else land_refs, land_refs, *refs[nb:nb + len(sems)]))

    bufs = srcs + list(lands)
    out = pl.pallas_call(
        body, name=name, in_specs=[HBM_SPEC] * nb + [SEM_SPEC] * len(sems) + [ANY_SPEC],
        out_specs=[HBM_SPEC] * nb, out_shape=[pltpu.HBM(a.shape, a.dtype) for a in bufs],
        input_output_aliases={i: i for i in range(nb)},
        compiler_params=pltpu.CompilerParams(has_side_effects=SIDE_EFFECT))(*bufs, *sems, after)
    return out[ns:]


def _all_gather(name, arrs):
    plan = lambda me, peer, k: [(i, None, i, me) for i in range(len(arrs))]
    return _exchange(name, arrs, [_sds((N_DEV,) + a.shape, a.dtype) for a in arrs], plan, len(arrs))


def _mix0_fwd(h, p):
    z = _mm_nt("mix0_in", h, p["ab_w_in"])
    y_a, d = _pool_fwd(z, p["pool_w"], p["pool_scale"])
    y_b = _sgu_fwd(z, p["sgu_ln_g"], p["sgu_ln_b"], p["sgu_w"], p["sgu_bt"])
    ycat = jnp.concatenate([y_a, y_b], axis=1)
    return _mm_nn("mix0_out", ycat, p["ab_w_out"]), (h, z, d, ycat)


def _mix0_bwd(df, saved, p, after):
    h, z, d, ycat = saved
    dycat = _mm_nt("mix0_out_dx", df, p["ab_w_out"], after=after)
    g = {"ab_w_out": _mm_tn("mix0_out_dw", ycat, df, BF16)}
    dz_p, g["pool_w"], g["pool_scale"] = _pool_bwd(dycat, d, p["pool_w"], p["pool_scale"])
    dz_u, dz_v, g["sgu_ln_g"], g["sgu_ln_b"], g["sgu_w"], dbt = _sgu_bwd(
        z, dycat, p["sgu_ln_g"], p["sgu_ln_b"], p["sgu_w"], p["sgu_bt"], p["head_sum"])
    g["sgu_b"] = dbt[:, :NH].T
    dz = jnp.concatenate([dz_p, dz_u, dz_v], axis=1)
    g["ab_w_in"] = _mm_tn("mix0_in_dw", dz, h, BF16)
    return _mm_nn("mix0_in_dx", dz, p["ab_w_in"]), g


def _mix1_fwd(h, p):
    u = _mm_nn("ssm_w_in", h, p["ssm_w_in"])
    x_re, x_im, y = _ssm_scan("ssm_scan_fwd", u, p["wb_bd"], p["lam_bar_re"], p["lam_bar_im"], p["wc_bd"], False)
    g = _ssm_act_fwd(y, u, p["ssm_d"])
    zz = _mm_nn("ssm_glu", g, p["ssm_w_glu"])
    return _glu_fwd(zz), (h, u, x_re, x_im, y, g, zz)


def _mix1_bwd(df, saved, p, after):
    h, u, x_re, x_im, y, g, zz = saved
    gr = {}
    dzz = _glu_bwd(zz, df)
    dg = _mm_nt("ssm_glu_dx", dzz, p["ssm_w_glu"], after=after)
    gr["ssm_w_glu"] = _mm_tn("ssm_glu_dw", g, dzz, BF16)
    dy, gr["ssm_d"] = _ssm_act_bwd(dg, y, u, p["ssm_d"])
    du_ssm, g_lam_re, g_lam_im, mb_re, mb_im, mc_re, mc_im = _ssm_scan(
        "ssm_scan_bwd", dy, p["wc_bd"], p["lam_bar_re"], -p["lam_bar_im"], p["wb_bd"], True, states=(x_re, x_im), u=u)
    du = _axpy(du_ssm, dy, p["ssm_d"])
    gr["ssm_w_in"] = _mm_tn("ssm_w_in_dw", h, du, BF16)
    dh = _mm_nt("ssm_w_in_dx", du, p["ssm_w_in"])
    per_group = lambda m: m[:, :, :SSM_P].reshape(SSM_G, SSM_N, SSM_P)
    gr["ssm_c_re"] = per_group(mc_re)
    gr["ssm_c_im"] = -per_group(mc_im)
    dlr, dli, ddt, dbr, dbi = _ssm_param_bwd(
        g_lam_re.reshape(SSM_G, SSM_P), g_lam_im.reshape(SSM_G, SSM_P),
        per_group(mb_re).reshape(SSM_G, SSM_N * SSM_P), per_group(mb_im).reshape(SSM_G, SSM_N * SSM_P),
        p["lam_re"], p["lam_im"], p["lam_re_rep"], p["lam_im_rep"], p["log_dt"], p["b_re"], p["b_im"], p["seg"])
    gr["ssm_lam_re"], gr["ssm_lam_im"], gr["ssm_log_dt"] = dlr, dli, ddt[:, 0]
    gr["ssm_b_re"] = dbr.reshape(SSM_G, SSM_N, SSM_P)
    gr["ssm_b_im"] = dbi.reshape(SSM_G, SSM_N, SSM_P)
    return dh, gr


def _ssm_params(lam_re, lam_im, b_re, b_im, c_re, c_im, log_dt):
    wide = lambda b: b.transpose(0, 2, 1).reshape(SSM_G, SSM_N * SSM_P)
    p = {"lam_re": lam_re, "lam_im": lam_im, "log_dt": log_dt.reshape(SSM_G, 1),
         "lam_re_rep": jnp.tile(lam_re, (1, SSM_N)), "lam_im_rep": jnp.tile(lam_im, (1, SSM_N)), "b_re": wide(b_re), "b_im": wide(b_im)}
    lbr, lbi, bbr, bbi = _ssm_prep(lam_re, lam_im, p["lam_re_rep"], p["lam_im_rep"], p["log_dt"], p["b_re"], p["b_im"])
    p["lam_bar_re"], p["lam_bar_im"] = lbr.reshape(1, SSM_L), lbi.reshape(1, SSM_L)
    rows = lambda m: m.reshape(SSM_G * SSM_N, SSM_P)
    p["wb_bd"] = _ssm_block_diag(rows(bbr), rows(bbi))
    p["wc_bd"] = _ssm_block_diag(rows(c_re), rows(-c_im))
    p["seg"] = jnp.tile(jnp.eye(SSM_P, dtype=F32), (SSM_N, 1))
    return p


RES_WEIGHT = (0.5, 1.0, 0.5)


def _local_step(x, tgt, vecs, weights_of, on_part, on_grads):
    def fns(i, w):
        if i % 3 != 1:
            win, wout_of = w
            return ((lambda h: _ffn_fwd(h, win, wout_of)),
                    (lambda df, sv, after: (_ffn_bwd(df, sv, win, wout_of(None), lambda tag, part: on_part(i, tag, part), after), None)))
        if i == 1:
            return (lambda h: _mix0_fwd(h, w)), (lambda df, sv, after: _mix0_bwd(df, sv, w, after))
        return (lambda h: _mix1_fwd(h, w)), (lambda df, sv, after: _mix1_bwd(df, sv, w, after))

    rw = RES_WEIGHT * 2
    saved, bwd = [], []
    f = None
    for i in range(6):
        w, token = weights_of(i, x if i == 0 else f)
        fwd, b = fns(i, w)
        if i == 0:
            h = _prenorm_fwd(x, vecs, 0, token)
        else:
            x, h = _post_pre_fwd(x, f, vecs, i, rw[i - 1], token)
        f, inner = fwd(h)
        saved.append((x, f, inner))
        bwd.append(b)
    loss_row, dx, df, dv_top = _last_shell(x, f, tgt, vecs, 5, rw[5])
    token = jnp.zeros((8, LANE), F32)
    for i in reversed(range(6)):
        x_i, _, inner = saved[i]
        dh, extra = bwd[i](df, inner, token)
        if i > 0:
            dx, df, dv = _pre_post_bwd(dx, dh, x_i, saved[i - 1][1], vecs, i, rw[i - 1])
        else:
            dx, dv = _prenorm_bwd(dx, dh, x_i, vecs, 0)
        token = on_grads(i, extra, dv, dv_top if i == 5 else None, loss_row)
    return dx


def _pad_rows(v, rows):
    return jnp.pad(v, (0, rows * LANE - v.shape[0])).reshape(rows, LANE)


def _pack(parts):
    flat, layout, off = [], [], 0
    for a in parts:
        n = a.size
        padded = -(-n // LANE) * LANE
        flat.append(jnp.pad(a.reshape(-1).astype(F32), (0, padded - n)))
        layout.append((off, n, a.shape))
        off += padded
    return jnp.concatenate(flat), layout


def _unpack(flat, layout):
    return [flat[off:off + n].reshape(shape) for off, n, shape in layout]


TRANSPOSED = ["ffn_w_in", "ab_w_in", "ssm_b_re", "ssm_b_im"]
WEIGHTS = ['ada_w', 'ada_b', 'norm_pre', 'norm_post', 'ffn_w_in', 'ffn_w_out', 'ab_w_in', 'pool_w', 'pool_scale', 'sgu_ln_g',
           'sgu_ln_b', 'sgu_w', 'sgu_b', 'ab_w_out', 'ssm_w_in', 'ssm_lam_re', 'ssm_lam_im', 'ssm_b_re', 'ssm_b_im', 'ssm_c_re',
           'ssm_c_im', 'ssm_d', 'ssm_log_dt', 'ssm_w_glu']


def kernel(x, c, ada_w, ada_b, norm_pre, norm_post, ffn_w_in, ffn_w_out, ab_w_in, pool_w, pool_scale, sgu_ln_g, sgu_ln_b, sgu_w, sgu_b, ab_w_out, ssm_w_in, ssm_lam_re, ssm_lam_im, ssm_b_re, ssm_b_im, ssm_c_re, ssm_c_im, ssm_d, ssm_log_dt, ssm_w_glu, loss_target, m_ada_w, m_ada_b, m_norm_pre, m_norm_post, m_ffn_w_in, m_ffn_w_out, m_ab_w_in, m_pool_w, m_pool_scale, m_sgu_ln_g, m_sgu_ln_b, m_sgu_w, m_sgu_b, m_ab_w_out, m_ssm_w_in, m_ssm_lam_re, m_ssm_lam_im, m_ssm_b_re, m_ssm_b_im, m_ssm_c_re, m_ssm_c_im, m_ssm_d, m_ssm_log_dt, m_ssm_w_glu, v_ada_w, v_ada_b, v_norm_pre, v_norm_post, v_ffn_w_in, v_ffn_w_out, v_ab_w_in, v_pool_w, v_pool_scale, v_sgu_ln_g, v_sgu_ln_b, v_sgu_w, v_sgu_b, v_ab_w_out, v_ssm_w_in, v_ssm_lam_re, v_ssm_lam_im, v_ssm_b_re, v_ssm_b_im, v_ssm_c_re, v_ssm_c_im, v_ssm_d, v_ssm_log_dt, v_ssm_w_glu):
    args = locals()
    wts = {n: args[n] for n in WEIGHTS}
    mom = {n: args["m_" + n] for n in WEIGHTS}
    var = {n: args["v_" + n] for n in WEIGHTS}
    for n in TRANSPOSED:
        for t in (wts, mom, var):
            t[n] = jnp.swapaxes(t[n], -1, -2)
    me = 4 * lax.axis_index("x") + 2 * lax.axis_index("y") + lax.axis_index("c")
    s = x.shape[1]
    nd = D // N_DEV

    small_in, small_in_layout = _pack([c, norm_pre, norm_post, ssm_d])
    small_rows = -(-small_in.shape[0] // (8 * LANE)) * 8
    (g_small,) = _all_gather("gather_small", [_pad_rows(small_in, small_rows)])
    g_small = g_small.reshape(N_DEV, -1)
    c_all, npre_g, npost_g, sd_g = [jnp.stack([_unpack(g_small[j], small_in_layout)[i] for j in range(N_DEV)]) for i in range(4)]
    c_all = c_all.reshape(N_DEV, D)
    norm_pre_full = npre_g.transpose(1, 2, 0, 3).reshape(2, 3, D)
    norm_post_full = npost_g.transpose(1, 2, 0, 3).reshape(2, 3, D)
    ssm_d_full = sd_g.transpose(1, 0, 2).reshape(1, D)

    nw = ada_w.shape[-1]
    (mod_g,) = _all_gather("gather_mod", [_mod_part(c_all, ada_w)])
    mod = lax.dynamic_index_in_dim(mod_g, me, axis=2, keepdims=False)
    mod = (mod.transpose(1, 0, 2).reshape(2, N_DEV * nw) + ada_b).reshape(2, 3, 3, D)

    w_in_t = wts["ffn_w_in"]
    shards = [[w_in_t[0, 0]], [ffn_w_out[0, 0]], [wts["ab_w_in"][0], ab_w_out[0]], [w_in_t[0, 1]], [ffn_w_out[0, 1]],
              [w_in_t[1, 0]], [ffn_w_out[1, 0]], [ssm_w_in[0], ssm_w_glu[0]], [w_in_t[1, 1]], [ffn_w_out[1, 1]]]
    first_group = {0: 0, 1: 2, 2: 3, 3: 5, 4: 7, 5: 8}
    first_groups = set(first_group.values())
    same_core = (2, 4, 6)

    def gather_plan(n):
        return lambda me_, peer_, k: [(a, None, a, me_) for a in range(n)] if k in (0, 1) + same_core else []

    def relay_plan(n):
        return lambda me_, peer_, k: [(a, me_ ^ kk, a, me_ ^ kk) for kk in same_core for a in range(n)] if k == 1 else []

    gathers, relays = [], {}
    token = mod_g
    for g, group in enumerate(shards):
        group = [a.astype(BF16) for a in group]
        sems, srcs_thru, lands, token = _exchange_start(
            f"gather_start_{g}", group, [_sds((N_DEV,) + a.shape, BF16) for a in group], gather_plan(len(group)), len(group), token)
        gathers.append((sems, srcs_thru, lands))
    mod6 = mod.reshape(6, 3, D)
    vecs = jnp.stack([norm_pre_full.reshape(6, D), mod6[:, 1], mod6[:, 0], norm_post_full.reshape(6, D), mod6[:, 2]]
                     + [jnp.zeros((6, D), F32)] * 3, axis=1)
    vecs = vecs + token[0, 0]

    def relay(g, after):
        sems, srcs_thru, lands = gathers[g]
        n = len(lands)
        relays[g] = _exchange_relay(f"gather_relay_{g}", sems, srcs_thru, lands, gather_plan(n), n, relay_plan(n), 3 * n, after)

    def fetch(g, after):
        if g not in relays:
            relay(g, after)
        sems, lands, token = relays[g]
        n = len(lands)
        got = _exchange_wait(f"gather_wait_{g}", sems, None, lands, relay_plan(n), 3 * n, after)
        if g + 1 in first_groups:
            relay(g + 1, got[0])
            token = relays[g + 1][2]
        return got, token

    head_sum = jnp.repeat(jnp.eye(NH, LANE, dtype=F32), HD, axis=0)
    mix0 = {"pool_w": pool_w[0], "pool_scale": pool_scale, "sgu_ln_g": sgu_ln_g, "sgu_ln_b": sgu_ln_b, "sgu_w": sgu_w[0],
            "sgu_bt": jnp.pad(sgu_b[0].T, ((0, 0), (0, LANE - NH))), "head_sum": head_sum}
    mix1 = _ssm_params(ssm_lam_re[0], ssm_lam_im[0], ssm_b_re[0], ssm_b_im[0], ssm_c_re[0], ssm_c_im[0], ssm_log_dt[0])
    mix1["ssm_d"] = ssm_d_full

    def weights_of(i, x_in):
        g = first_group[i]
        if i % 3 != 1:
            (win,), token = fetch(g, x_in)
            cache = []

            def wout_of(act):
                if not cache:
                    cache.append(fetch(g + 1, act)[0][0])
                return cache[0]

            return (win, wout_of), token
        (a, b), token = fetch(g, x_in)
        if i == 1:
            return dict(mix0, ab_w_in=a.reshape(-1, D), ab_w_out=b.reshape(D, D)), token
        return dict(mix1, ssm_w_in=a.reshape(D, D), ssm_w_glu=b.transpose(1, 0, 2).reshape(D, -1)), token

    def shard_cols(a):
        r = a.shape[0]
        return a.reshape(r, N_DEV, -1).transpose(1, 0, 2)

    scatter_plan = lambda me_, peer_, k: [(0, peer_, 0, me_), (1, peer_, 1, me_)]
    scatter_plan1 = lambda me_, peer_, k: [(0, peer_, 0, me_)]
    scatters = []
    last_token = [jnp.zeros((8, LANE), F32)]
    pieces, mixer, bundles = {}, {}, {}
    bundle_plan = lambda me_, peer_, k: [(0, None, 0, me_)]

    held = {}

    def on_part(i, tag, part):
        if i != 0 and tag == "w_out":
            held[i] = part
            return last_token[0]
        names, parts, plan = (("ffn_" + tag,), [part], scatter_plan1) if i == 0 else (("ffn_w_out", "ffn_w_in"), [held[i], part], scatter_plan)
        sems, srcs_thru, lands, last_token[0] = _exchange_start(
            f"scatter_start_{i}_{tag}", parts, [_sds(a.shape, BF16) for a in parts], plan, len(parts), last_token[0])
        scatters.append((i, names, plan, sems, srcs_thru, lands))
        return last_token[0]
    mix0_names = ["pool_w", "pool_scale", "sgu_ln_g", "sgu_ln_b", "sgu_w", "sgu_b"]
    mix1_names = ["ssm_lam_re", "ssm_lam_im", "ssm_b_re", "ssm_b_im", "ssm_c_re", "ssm_c_im", "ssm_log_dt", "ssm_d"]

    def start_bundle(tag, arrays):
        flat, layout = _pack(arrays)
        rows = -(-flat.shape[0] // (8 * LANE)) * 8
        plan = gather_plan(1) if tag == "a" else bundle_plan
        sems, srcs_thru, lands, last_token[0] = _exchange_start(
            f"small_start_{tag}", [_pad_rows(flat, rows)], [_sds((N_DEV, rows, LANE))], plan, 1, last_token[0])
        bundles[tag] = (sems, srcs_thru, lands, layout)

    def on_grads(i, extra, dv, dv_top, loss_row):
        pieces[i] = dv
        if i == 5:
            pieces["top"] = dv_top
        if i == 4:
            mixer.update({n: extra[n] for n in mix1_names})
        if i == 1:
            mixer.update({n: extra[n] for n in mix0_names})
            start_bundle("a", [jnp.stack([pieces[j] for j in ("top", 5, 4, 3, 2, 1)])] + [mixer[n] for n in mix0_names + mix1_names])
        if i == 0:
            start_bundle("b", [dv, loss_row])
        if i % 3 != 1:
            return last_token[0]
        if i == 1:
            names, parts = ("ab_w_in", "ab_w_out"), [extra["ab_w_in"].reshape(N_DEV, -1, D), extra["ab_w_out"].reshape(N_DEV, nd, D)]
        else:
            names, parts = ("ssm_w_in", "ssm_w_glu"), [extra["ssm_w_in"].reshape(N_DEV, nd, D), shard_cols(extra["ssm_w_glu"])]
        sems, srcs_thru, lands, last_token[0] = _exchange_start(
            f"scatter_start_{i}", parts, [_sds(a.shape, BF16) for a in parts], scatter_plan, 2, last_token[0])
        scatters.append((i, names, scatter_plan, sems, srcs_thru, lands))
        return last_token[0]

    grad_x = _local_step(x[0], loss_target[0], vecs, weights_of, on_part, on_grads)

    out_g, out_d, out_m, out_v = {}, {}, {}, {}
    big_out = {}

    def adam_big(name, recv, n, slot=0, after=None):
        c_ = wts[n].shape[-1]
        big_out[n] = _adamw(name, recv.reshape(recv.shape[0], -1, c_), *[t[n].reshape(-1, c_) for t in (wts, mom, var)],
                            slot=slot, prev=big_out.get(n), after=after)
        return big_out[n][0]

    ffn_slot = {0: 0, 2: 1, 3: 2, 5: 3}

    def land_and_update(entries, after):
        for i, names, plan, sems, srcs_thru, lands in entries:
            recv = _exchange_wait(f"scatter_wait_{i}_{names[0]}", sems, srcs_thru, lands, plan, len(names), after)
            for n, r in zip(names, recv):
                after = adam_big(f"adamw_{n}_{i}", r, n, ffn_slot.get(i, 0), after)
        return after

    after = land_and_update([e for e in scatters if e[0] != 0], last_token[0])

    def landed(tag, g_parts):
        layout = bundles[tag][3]
        off, n, shape = layout[0]
        dmods = g_parts.reshape(N_DEV, -1)[:, off:off + n].reshape((N_DEV,) + shape)
        total = _sum_parts(g_parts)
        return dmods, _unpack(total.reshape(-1), layout), total

    def adam_small(n, g, after=None):
        cols = wts[n].shape[-1]
        res = _adamw(f"adamw_{n}", g.reshape(1, -1, cols), *[t[n].reshape(-1, cols) for t in (wts, mom, var)], after=after)
        for o, arr in zip((out_g, out_d, out_m, out_v), res):
            o[n] = arr.reshape(wts[n].shape)
            if n in TRANSPOSED:
                o[n] = jnp.swapaxes(o[n], -1, -2)
        return res[0]

    sems, srcs_thru, lands, _ = bundles["a"]
    sems, lands, _ = _exchange_relay("small_relay_a", sems, srcs_thru, lands, gather_plan(1), 1, relay_plan(1), 3, after)
    (parts_a,) = _exchange_wait("small_wait_a", sems, None, lands, relay_plan(1), 3, after)
    shells_a, sums_a, after = landed("a", parts_a)
    small = dict(zip(mix0_names + mix1_names, sums_a[1:]))
    def adam_tiny(name, grads, after):
        view = lambda n, a: a.reshape(-1, wts[n].shape[-1])
        items = [(view(n, g),) + tuple(view(n, t[n]) for t in (wts, mom, var)) for n, g in grads.items()]
        for (n, _), item, res in zip(grads.items(), items, _adamw_many(name, items, after)):
            for o, arr in zip((out_g, out_d, out_m, out_v), (item[0],) + res):
                o[n] = arr.reshape(wts[n].shape)
        return res[0]

    tiny = ["pool_scale", "sgu_ln_g", "sgu_ln_b", "sgu_b", "ssm_lam_re", "ssm_lam_im", "ssm_log_dt"]
    for n in [n for n in mix0_names + mix1_names if n not in tiny and n != "ssm_d"]:
        after = adam_small(n, small[n], after)
    after = adam_tiny("adamw_tiny_mixers", dict({n: small[n] for n in tiny},
                                                ssm_d=lax.dynamic_slice_in_dim(small["ssm_d"], me * nd, nd, axis=1)), after)
    def shell_grads(top, blocks, first):
        own_rows = jnp.concatenate([first[..., None, :, :], blocks[..., :0:-1, :, :]], axis=-3)
        next_rows = jnp.concatenate([own_rows[..., 1:, :, :], top[..., None, :, :]], axis=-3)
        dmod_ = jnp.stack([own_rows[..., V_SHIFT, :], own_rows[..., V_SCALE, :], next_rows[..., V_GATE, :]], axis=-2)
        return dmod_, own_rows[..., V_GPRE, :], next_rows[..., V_GPOST, :]

    def ada_w_layer(l, dmod_l, after):
        mine = lax.dynamic_index_in_dim(dmod_l.reshape(N_DEV, N_DEV, nw), me, axis=1, keepdims=False)
        return adam_big(f"adamw_ada_w_{l}", _ada_w_grad(c_all.T, mine[None]), "ada_w", l, after)

    after = ada_w_layer(1, shell_grads(shells_a[:, 0], shells_a, jnp.zeros_like(shells_a[:, 0]))[0][:, 3:], after)
    sems, srcs_thru, lands, _ = bundles["b"]
    (parts_b,) = _exchange_wait("small_wait_b", sems, srcs_thru, lands, bundle_plan, 1, after)
    shell_b, (first_sum, loss_sum), after = landed("b", parts_b)
    loss = loss_sum[0, 0]

    dmod_sum, dg_pre_sum, dg_post_sum = shell_grads(sums_a[0][0], sums_a[0], first_sum)
    own = lambda a: lax.dynamic_slice_in_dim(a, me * nd, nd, axis=1)
    after = adam_tiny("adamw_tiny_shell", {"ada_b": dmod_sum, "norm_pre": own(dg_pre_sum), "norm_post": own(dg_post_sum)}, after)

    after = ada_w_layer(0, shell_grads(shells_a[:, 0], shells_a, shell_b)[0][:, :3], after)

    land_and_update([e for e in scatters if e[0] == 0], after)
    for n, res in big_out.items():
        for o, arr in zip((out_g, out_d, out_m, out_v), res):
            o[n] = arr.reshape(wts[n].shape)
            if n in TRANSPOSED:
                o[n] = jnp.swapaxes(o[n], -1, -2)

    return (loss, grad_x[None], *[out_g[n] for n in WEIGHTS], *[out_d[n] for n in WEIGHTS],
            *[out_m[n] for n in WEIGHTS], *[out_v[n] for n in WEIGHTS])
```

```python
import math

import jax
import jax.numpy as jnp
from jax import lax
from jax.experimental import pallas as pl
from jax.experimental.pallas import tpu as pltpu

F32 = jnp.float32
BF16 = jnp.bfloat16
MESH = pl.DeviceIdType.MESH
HIGHEST = lax.Precision.HIGHEST

N_DEV = 8
D = 1024
D_FF = 2816
FSH = 2 * D_FF // N_DEV
EPS = 1e-6
POOL_WINDOWS = (2, 4, 8, 16)
HD = 128
NH = 4
SSM_G, SSM_P, SSM_N = 64, 64, 16
SSM_L = SSM_G * SSM_P
LR, B1, B2, ADAM_EPS, WD, STEP = 0.001, 0.9, 0.999, 1e-08, 0.01, 10
GELU_C = math.sqrt(2.0 / math.pi)
VMEM_LIMIT_BYTES = 48 * 1024 * 1024
LANE = 128


def _pc(body, name, grid, in_specs, out_specs, out_shape, scratch=()):
    return pl.pallas_call(
        body, name=name, grid=grid, in_specs=in_specs, out_specs=out_specs, out_shape=out_shape,
        scratch_shapes=list(scratch),
        compiler_params=pltpu.CompilerParams(dimension_semantics=("arbitrary",) * len(grid),
                                             vmem_limit_bytes=VMEM_LIMIT_BYTES))


def _sds(shape, dtype=F32):
    return jax.ShapeDtypeStruct(tuple(shape), dtype)


def _bf(v):
    return v if v.dtype == BF16 else v.astype(BF16)


def _row_spec(ts, width, col=0):
    return pl.BlockSpec((ts, width), lambda t, _c=col: (t, _c))


def _vec_spec(width, col=0):
    return pl.BlockSpec((1, width), lambda t, _c=col: (0, _c))


def _mm(name, a, b, contract, grid, a_spec, b_spec, o_spec, out_shape, acc_axis=None, after=None):
    dn = (contract, ((), ()))

    def body(a_ref, b_ref, *rest):
        o_ref = rest[-1]
        r = lax.dot_general(_bf(a_ref[...]), _bf(b_ref[...]), dn, preferred_element_type=F32)
        if acc_axis is None:
            o_ref[...] = r.astype(o_ref.dtype)
        else:
            k = pl.program_id(acc_axis)

            @pl.when(k == 0)
            def _():
                o_ref[...] = r

            @pl.when(k > 0)
            def _():
                o_ref[...] += r

    if after is None:
        return _pc(body, name, grid, [a_spec, b_spec], o_spec, out_shape)(a, b)
    return _pc(body, name, grid, [a_spec, b_spec, pl.BlockSpec(memory_space=pl.ANY)], o_spec, out_shape)(a, b, after)


def _mm_sum(name, a, b, ts, after=None):
    nj, s, k = a.shape
    n = b.shape[2]

    def body(a_ref, b_ref, *rest):
        acc = jnp.dot(a_ref[0], b_ref[0], preferred_element_type=F32)
        for j in range(1, nj):
            acc = acc + jnp.dot(a_ref[j], b_ref[j], preferred_element_type=F32)
        rest[-1][...] = acc

    specs = [pl.BlockSpec((nj, ts, k), lambda t: (0, t, 0)), pl.BlockSpec((nj, k, n), lambda t: (0, 0, 0))]
    args = (a, b)
    if after is not None:
        specs, args = specs + [pl.BlockSpec(memory_space=pl.ANY)], args + (after,)
    return _pc(body, name, (s // ts,), specs, pl.BlockSpec((ts, n), lambda t: (t, 0)), _sds((s, n)))(*args)


def _tile(s):
    return min(s, 1024)


def _div_tile(n, cap=1024):
    t = min(n, cap) // LANE * LANE
    while n % t:
        t -= LANE
    return t


def _mm_nn(name, a, b, out_dtype=F32):
    s, k = a.shape
    n = b.shape[1]
    ts, tn = _tile(s), _div_tile(n)
    return _mm(name, a, b, ((1,), (0,)), (n // tn, s // ts),
               pl.BlockSpec((ts, k), lambda j, t: (t, 0)), pl.BlockSpec((k, tn), lambda j, t: (0, j)),
               pl.BlockSpec((ts, tn), lambda j, t: (t, j)), _sds((s, n), out_dtype))


def _mm_nt(name, a, b, out_dtype=F32, after=None):
    s, n = a.shape
    k = b.shape[0]
    ts, tk = _tile(s), _div_tile(k)
    return _mm(name, a, b, ((1,), (1,)), (k // tk, s // ts),
               pl.BlockSpec((ts, n), lambda j, t: (t, 0)), pl.BlockSpec((tk, n), lambda j, t: (j, 0)),
               pl.BlockSpec((ts, tk), lambda j, t: (t, j)), _sds((s, k), out_dtype), after=after)


def _mm_tn(name, a, b, out_dtype=F32, tm=512, tn=512):
    s, m = a.shape
    n = b.shape[1]
    tm, tn = min(m, tm), min(n, tn)
    return _mm(name, a, b, ((0,), (0,)), (m // tm, n // tn),
               pl.BlockSpec((s, tm), lambda i, j: (0, i)), pl.BlockSpec((s, tn), lambda i, j: (0, j)),
               pl.BlockSpec((tm, tn), lambda i, j: (i, j)), _sds((m, n), out_dtype))


def _rstd(v):
    return lax.rsqrt(jnp.mean(v * v, axis=-1, keepdims=True) + EPS)


V_GPRE, V_SCALE, V_SHIFT, V_GPOST, V_GATE = range(5)


def _vrow(v, r):
    return v[r:r + 1]


def _vblock(i):
    return pl.BlockSpec((None, 8, D), lambda t: (i, 0, 0))


def _head(xv, v):
    return ((xv * _rstd(xv) * _vrow(v, V_GPRE)) * (1.0 + _vrow(v, V_SCALE)) + _vrow(v, V_SHIFT)).astype(BF16)


def _tail(xv, fv, v, rw):
    return xv + (rw * _vrow(v, V_GATE)) * (fv * _rstd(fv) * _vrow(v, V_GPOST))


def _prenorm_fwd(x, vecs, i, after):
    s = x.shape[0]
    ts = min(s, 512)

    def body(x_ref, v_ref, after_ref, h_ref):
        h_ref[...] = _head(x_ref[...], v_ref[...])

    return _pc(body, "prenorm_fwd", (s // ts,), [_row_spec(ts, D), _vblock(i), pl.BlockSpec(memory_space=pl.ANY)], _row_spec(ts, D),
               _sds((s, D), BF16))(x, vecs, after)


def _post_pre_fwd(x, f, vecs, i, rw_prev, after):
    s = x.shape[0]
    ts = min(s, 512)

    def body(x_ref, f_ref, vp_ref, vc_ref, after_ref, xo_ref, h_ref):
        xv = _tail(x_ref[...], f_ref[...], vp_ref[...], rw_prev)
        xo_ref[...] = xv
        h_ref[...] = _head(xv, vc_ref[...])

    return _pc(body, "post_pre_fwd", (s // ts,),
               [_row_spec(ts, D)] * 2 + [_vblock(i - 1), _vblock(i), pl.BlockSpec(memory_space=pl.ANY)], [_row_spec(ts, D)] * 2,
               [_sds((s, D)), _sds((s, D), BF16)])(x, f, vecs, vecs, after)


def _zero_at_first(first, *refs):
    @pl.when(first)
    def _():
        for ref in refs:
            ref[...] = jnp.zeros_like(ref)


def _acc(ref, first, v):
    @pl.when(first)
    def _():
        ref[...] = v

    @pl.when(jnp.logical_not(first))
    def _():
        ref[...] += v


def _colsum(v):
    return jnp.sum(v, axis=0, keepdims=True)


def _tail_bwd(do, fv, v, rw, dv_ref):
    gv = _vrow(v, V_GPOST)
    r = _rstd(fv)
    fn = fv * r
    dv_ref[V_GATE:V_GATE + 1, :] += rw * _colsum(do * (fn * gv))
    dy = (rw * _vrow(v, V_GATE)) * do
    dv_ref[V_GPOST:V_GPOST + 1, :] += _colsum(dy * fn)
    dfn = dy * gv
    return (r * (dfn - fn * jnp.mean(dfn * fn, axis=-1, keepdims=True))).astype(BF16)


def _head_bwd(do, dhv, xv, v, dv_ref):
    gv = _vrow(v, V_GPRE)
    r = _rstd(xv)
    xn = xv * r
    dv_ref[V_SHIFT:V_SHIFT + 1, :] += _colsum(dhv)
    dv_ref[V_SCALE:V_SCALE + 1, :] += _colsum(dhv * (xn * gv))
    dhp = dhv * (1.0 + _vrow(v, V_SCALE))
    dv_ref[V_GPRE:V_GPRE + 1, :] += _colsum(dhp * xn)
    dxn = dhp * gv
    return do + r * (dxn - xn * jnp.mean(dxn * xn, axis=-1, keepdims=True))


DV_SPEC = pl.BlockSpec((8, D), lambda t: (0, 0))


def _prenorm_bwd(dout, dh, x, vecs, i):
    s = dout.shape[0]
    ts = min(s, 512)

    def body(do_ref, dh_ref, x_ref, v_ref, dx_ref, dv_ref):
        _zero_at_first(pl.program_id(0) == 0, dv_ref)
        dx_ref[...] = _head_bwd(do_ref[...], dh_ref[...], x_ref[...], v_ref[...], dv_ref)

    return _pc(body, "prenorm_bwd", (s // ts,), [_row_spec(ts, D)] * 3 + [_vblock(i)], [_row_spec(ts, D), DV_SPEC],
               [_sds((s, D)), _sds((8, D))])(dout, dh, x, vecs)


def _pre_post_bwd(dout, dh, x, f_prev, vecs, i, rw_prev):
    s = dout.shape[0]
    ts = min(s, 256)

    def body(do_ref, dh_ref, x_ref, f_ref, vc_ref, vp_ref, dx_ref, df_ref, dv_ref):
        _zero_at_first(pl.program_id(0) == 0, dv_ref)
        dx = _head_bwd(do_ref[...], dh_ref[...], x_ref[...], vc_ref[...], dv_ref)
        dx_ref[...] = dx
        df_ref[...] = _tail_bwd(dx, f_ref[...], vp_ref[...], rw_prev, dv_ref)

    rows = _row_spec(ts, D)
    return _pc(body, "pre_post_bwd", (s // ts,), [rows] * 4 + [_vblock(i), _vblock(i - 1)], [rows, rows, DV_SPEC],
               [_sds((s, D)), _sds((s, D), BF16), _sds((8, D))])(dout, dh, x, f_prev, vecs, vecs)


def _last_shell(x, f, tgt, vecs, i, rw):
    s = x.shape[0]
    ts = min(s, 512)
    nt = s // ts

    def body(x_ref, f_ref, t_ref, v_ref, loss_ref, dy_ref, df_ref, dv_ref, acc_ref):
        t = pl.program_id(0)
        _zero_at_first(t == 0, dv_ref, acc_ref)
        fv, v = f_ref[...], v_ref[...]
        e = _tail(x_ref[...], fv, v, rw) - t_ref[...]
        dy = e * (1.0 / D)
        dy_ref[...] = dy
        acc_ref[...] += _colsum(e * e)
        df_ref[...] = _tail_bwd(dy, fv, v, rw, dv_ref)

        @pl.when(t == nt - 1)
        def _():
            loss_ref[...] = jnp.full((1, LANE), 0.5 / D, F32) * jnp.sum(acc_ref[...])

    rows = _row_spec(ts, D)
    return _pc(body, "last_shell", (nt,), [rows] * 3 + [_vblock(i)],
               [pl.BlockSpec((1, LANE), lambda t: (0, 0)), rows, rows, DV_SPEC],
               [_sds((1, LANE)), _sds((s, D)), _sds((s, D), BF16), _sds((8, D))], scratch=[pltpu.VMEM((1, D), F32)])(x, f, tgt, vecs)


def _sigmoid(v):
    return 1.0 / (1.0 + jnp.exp(-v))


def _ffn_in_swiglu(h, win):
    s = h.shape[0]
    ts = _tile(s)
    nt = (((1,), (1,)), ((), ()))

    def body(h_ref, wa_ref, wb_ref, fac_ref, act_ref):
        hv = h_ref[...]
        a = lax.dot_general(hv, wa_ref[...], nt, preferred_element_type=F32)
        b = lax.dot_general(hv, wb_ref[...], nt, preferred_element_type=F32)
        sg = _sigmoid(a)
        silu = a * sg
        fac_ref[0] = (b * (sg * (1.0 + a * (1.0 - sg)))).astype(BF16)
        fac_ref[1] = silu.astype(BF16)
        act_ref[...] = (silu * b).astype(BF16)

    return _pc(body, "ffn_in", (4, s // ts),
               [pl.BlockSpec((ts, D), lambda k, t: (t, 0)), pl.BlockSpec((None, FSH, D), lambda k, t: (k, 0, 0)),
                pl.BlockSpec((None, FSH, D), lambda k, t: (k + 4, 0, 0))],
               [pl.BlockSpec((2, None, ts, FSH), lambda k, t: (0, k, t, 0)), pl.BlockSpec((None, ts, FSH), lambda k, t: (k, t, 0))],
               [_sds((2, 4, s, FSH), BF16), _sds((4, s, FSH), BF16)])(h, win, win)


def _ffn_out_dx_swiglu(df, wout, fac, after):
    s = df.shape[0]
    ts = _tile(s)
    nt = (((1,), (1,)), ((), ()))

    def body(df_ref, w_ref, fac_ref, after_ref, o_ref):
        d = lax.dot_general(df_ref[...], w_ref[...], nt, preferred_element_type=F32)
        o_ref[0] = (d * fac_ref[0]).astype(BF16)
        o_ref[1] = (d * fac_ref[1]).astype(BF16)

    spec = pl.BlockSpec((2, None, ts, FSH), lambda k, t: (0, k, t, 0))
    out = _pc(body, "ffn_out_dx", (4, s // ts),
              [pl.BlockSpec((ts, D), lambda k, t: (t, 0)), pl.BlockSpec((None, FSH, D), lambda k, t: (k, 0, 0)), spec,
               pl.BlockSpec(memory_space=pl.ANY)],
              spec, _sds((2, 4, s, FSH), BF16))(df, wout, fac, after)
    return out.reshape(N_DEV, s, FSH)


def _ffn_fwd(h, win, wout_of):
    s = h.shape[0]
    fac, act = _ffn_in_swiglu(h, win)
    f = _mm_sum("ffn_out", act, wout_of(act).reshape(4, FSH, D), min(s, 512))
    return f, (h, fac, act)


def _ffn_bwd(df, saved, win, wout, send, after):
    h, fac, act = saved
    s = h.shape[0]
    ts = s
    wout = wout.reshape(4, FSH, D)
    dwout = _mm("ffn_out_dw", act, df, ((0,), (0,)), (4, 2),
                pl.BlockSpec((None, s, FSH), lambda k, j: (k, 0, 0)), pl.BlockSpec((s, D // 2), lambda k, j: (0, j)),
                pl.BlockSpec((None, FSH, D // 2), lambda k, j: (k, 0, j)), _sds((4, FSH, D), BF16), after=after)
    dz = _ffn_out_dx_swiglu(df, wout, fac, send("w_out", dwout.reshape(N_DEV, D_FF // N_DEV, D)))
    dwin = _mm("ffn_in_dw", dz, h, ((0,), (0,)), (N_DEV, 2),
               pl.BlockSpec((None, s, FSH), lambda j, i: (j, 0, 0)), pl.BlockSpec((s, D // 2), lambda j, i: (0, i)),
               pl.BlockSpec((None, FSH, D // 2), lambda j, i: (j, 0, i)), _sds((N_DEV, FSH, D), BF16))
    return _mm_sum("ffn_in_dx", dz, win, min(s, 512), after=send("w_in", dwin))


def _shift_rows(v, k, row, s, back):
    if back:
        return jnp.where(row < s - k, pltpu.roll(v, s - k, 0), 0.0)
    return jnp.where(row >= k, pltpu.roll(v, k, 0), 0.0)


def _window_sum(v, w, row, s, back):
    k = 1
    while k < w:
        v = v + _shift_rows(v, k, row, s, back)
        k *= 2
    return v


def _pool_fwd(z, pool_w, pool_scale):
    s = z.shape[0]

    def body(z_ref, w_ref, sc_ref, y_ref, d_ref):
        row = lax.broadcasted_iota(jnp.int32, (s, HD), 0)
        for g, w in enumerate(POOL_WINDOWS):
            sl = slice(g * HD, (g + 1) * HD)
            a = z_ref[:, sl]
            cnt = jnp.minimum(row + 1, w).astype(F32)
            d = (_window_sum(a, w, row, s, False) / cnt - a).astype(BF16)
            d_ref[:, sl] = d
            y = jnp.dot(d, _bf(w_ref[g]), preferred_element_type=F32)
            y_ref[:, sl] = (y * sc_ref[:, sl]).astype(BF16)

    return _pc(body, "pool_fwd", (1,),
               [pl.BlockSpec((s, NH * HD), lambda i: (0, 0)), pl.BlockSpec((NH, HD, HD), lambda i: (0, 0, 0)),
                pl.BlockSpec((1, NH * HD), lambda i: (0, 0))],
               [pl.BlockSpec((s, NH * HD), lambda i: (0, 0))] * 2,
               [_sds((s, NH * HD), BF16)] * 2)(z, pool_w, pool_scale)


def _pool_bwd(dy, d, pool_w, pool_scale):
    s = dy.shape[0]

    def body(dy_ref, d_ref, w_ref, sc_ref, dz_ref, dw_ref, dsc_ref):
        row = lax.broadcasted_iota(jnp.int32, (s, HD), 0)
        for g, w in enumerate(POOL_WINDOWS):
            sl = slice(g * HD, (g + 1) * HD)
            dyg, dg, wg = dy_ref[:, sl], d_ref[:, sl], _bf(w_ref[g])
            yraw = jnp.dot(dg, wg, preferred_element_type=F32)
            dsc_ref[:, sl] = _colsum(dyg * yraw)
            dyr = _bf(dyg * sc_ref[:, sl])
            dw_ref[g] = lax.dot_general(dg, dyr, (((0,), (0,)), ((), ())), preferred_element_type=F32)
            dd = lax.dot_general(dyr, wg, (((1,), (1,)), ((), ())), preferred_element_type=F32)
            cnt = jnp.minimum(row + 1, w).astype(F32)
            dz_ref[:, sl] = (_window_sum(dd / cnt, w, row, s, True) - dd).astype(BF16)

    return _pc(body, "pool_bwd", (1,),
               [pl.BlockSpec((s, NH * HD), lambda i: (0, 0)), pl.BlockSpec((s, NH * HD), lambda i: (0, 0)),
                pl.BlockSpec((NH, HD, HD), lambda i: (0, 0, 0)), pl.BlockSpec((1, NH * HD), lambda i: (0, 0))],
               [pl.BlockSpec((s, NH * HD), lambda i: (0, 0)), pl.BlockSpec((NH, HD, HD), lambda i: (0, 0, 0)),
                pl.BlockSpec((1, NH * HD), lambda i: (0, 0))],
               [_sds((s, NH * HD), BF16), _sds((NH, HD, HD)), _sds((1, NH * HD))])(dy, d, pool_w, pool_scale)


def _gelu(v):
    return 0.5 * v * (1.0 + jnp.tanh(GELU_C * (v + 0.044715 * (v * v * v))))


def _gelu_and_grad(v):
    t = jnp.tanh(GELU_C * (v + 0.044715 * (v * v * v)))
    return 0.5 * v * (1.0 + t), 0.5 * (1.0 + t) + 0.5 * v * (1.0 - t * t) * (GELU_C * (1.0 + 3.0 * 0.044715 * (v * v)))


def _gelu_grad(v):
    return _gelu_and_grad(v)[1]


def _causal_mask():
    return lax.broadcasted_iota(jnp.int32, (HD, HD), 0) >= lax.broadcasted_iota(jnp.int32, (HD, HD), 1)


def _sgu_specs():
    w = NH * HD
    return [pl.BlockSpec((HD, w), lambda c: (c, 1)), pl.BlockSpec((HD, w), lambda c: (c, 2)),
            pl.BlockSpec((1, w), lambda c: (0, 0)), pl.BlockSpec((1, w), lambda c: (0, 0)),
            pl.BlockSpec((NH, HD, HD), lambda c: (0, 0, 0)), pl.BlockSpec((HD, LANE), lambda c: (0, 0))]


def _sgu_head(v, lng_ref, lnb_ref, w_ref, h):
    sl = slice(h * HD, (h + 1) * HD)
    vh = v[:, sl]
    xc = vh - jnp.mean(vh, axis=-1, keepdims=True)
    rs = lax.rsqrt(jnp.mean(xc * xc, axis=-1, keepdims=True) + EPS)
    vhat = xc * rs
    vn = _bf(vhat * lng_ref[:, sl] + lnb_ref[:, sl])
    wc = _bf(jnp.where(_causal_mask(), w_ref[h], 0.0))
    return sl, rs, vhat, vn, wc


def _sgu_fwd(z, ln_g, ln_b, sgu_w, sgu_bt):
    s = z.shape[0]

    def body(zu_ref, zv_ref, lng_ref, lnb_ref, w_ref, bt_ref, y_ref):
        u, v = _gelu(zu_ref[...]), _gelu(zv_ref[...])
        for h in range(NH):
            sl, _, _, vn, wc = _sgu_head(v, lng_ref, lnb_ref, w_ref, h)
            sp = jnp.dot(wc, vn, preferred_element_type=F32) + bt_ref[:, h:h + 1]
            y_ref[:, sl] = (u[:, sl] * sp).astype(BF16)

    return _pc(body, "sgu_fwd", (s // HD,), _sgu_specs(), pl.BlockSpec((HD, NH * HD), lambda c: (c, 0)),
               _sds((s, NH * HD), BF16))(z, z, ln_g, ln_b, sgu_w, sgu_bt)


def _sgu_bwd(z, dy, ln_g, ln_b, sgu_w, sgu_bt, head_sum):
    s = z.shape[0]
    w = NH * HD
    nc = s // HD

    def body(zu_ref, zv_ref, lng_ref, lnb_ref, w_ref, bt_ref, dy_ref, hs_ref,
             dzu_ref, dzv_ref, dlng_ref, dlnb_ref, dw_ref, dbt_ref, dsacc_ref):
        c = pl.program_id(0)
        _zero_at_first(c == 0, dsacc_ref, dw_ref, dlng_ref, dlnb_ref)
        zu, zv = zu_ref[...], zv_ref[...]
        (u, gu), (v, gv) = _gelu_and_grad(zu), _gelu_and_grad(zv)
        dyv = dy_ref[...]
        ds = dyv * u
        dsacc_ref[...] += ds
        for h in range(NH):
            sl, rs, vhat, vn, wc = _sgu_head(v, lng_ref, lnb_ref, w_ref, h)
            sp = jnp.dot(wc, vn, preferred_element_type=F32) + bt_ref[:, h:h + 1]
            dzu_ref[:, sl] = (dyv[:, sl] * sp * gu[:, sl]).astype(BF16)
            dsh = _bf(ds[:, sl])
            dwh = lax.dot_general(dsh, vn, (((1,), (1,)), ((), ())), preferred_element_type=F32)
            dw_ref[h] += jnp.where(_causal_mask(), dwh, 0.0)
            dvn = lax.dot_general(wc, dsh, (((0,), (0,)), ((), ())), preferred_element_type=F32)
            dlng_ref[:, sl] += _colsum(dvn * vhat)
            dlnb_ref[:, sl] += _colsum(dvn)
            dvh = dvn * lng_ref[:, sl]
            dv = rs * (dvh - jnp.mean(dvh, axis=-1, keepdims=True) - vhat * jnp.mean(dvh * vhat, axis=-1, keepdims=True))
            dzv_ref[:, sl] = (dv * gv[:, sl]).astype(BF16)

        @pl.when(c == nc - 1)
        def _():
            dbt_ref[...] = jnp.dot(dsacc_ref[...], hs_ref[...], preferred_element_type=F32, precision=HIGHEST)

    outs = _pc(body, "sgu_bwd", (nc,),
               _sgu_specs() + [pl.BlockSpec((HD, w), lambda c: (c, 1)), pl.BlockSpec((w, LANE), lambda c: (0, 0))],
               [pl.BlockSpec((HD, w), lambda c: (c, 0))] * 2 + [pl.BlockSpec((1, w), lambda c: (0, 0))] * 2
               + [pl.BlockSpec((NH, HD, HD), lambda c: (0, 0, 0)), pl.BlockSpec((HD, LANE), lambda c: (0, 0))],
               [_sds((s, w), BF16)] * 2 + [_sds((1, w))] * 2 + [_sds((NH, HD, HD)), _sds((HD, LANE))],
               scratch=[pltpu.VMEM((HD, w), F32)])(z, z, ln_g, ln_b, sgu_w, sgu_bt, dy, head_sum)
    return outs


def _cmul(ar, ai, br, bi):
    return ar * br - ai * bi, ar * bi + ai * br


def _ssm_prep(lam_re, lam_im, lam_re_rep, lam_im_rep, log_dt, b_re, b_im):
    def disc(lr, li, dt):
        mag = jnp.exp(lr * dt)
        return mag * jnp.cos(li * dt), mag * jnp.sin(li * dt)

    def body(lr_ref, li_ref, lrr_ref, lir_ref, ldt_ref, br_ref, bi_ref, or_ref, oi_ref, bbr_ref, bbi_ref):
        dt = jnp.exp(ldt_ref[...])
        or_ref[...], oi_ref[...] = disc(lr_ref[...], li_ref[...], dt)
        lr, li = lrr_ref[...], lir_ref[...]
        er, ei = disc(lr, li, dt)
        den = lr * lr + li * li
        kr = ((er - 1.0) * lr + ei * li) / den
        ki = (ei * lr - (er - 1.0) * li) / den
        bbr_ref[...], bbi_ref[...] = _cmul(kr, ki, br_ref[...], bi_ref[...])

    small = pl.BlockSpec((SSM_G, SSM_P), lambda i: (0, 0))
    wide = pl.BlockSpec((SSM_G, SSM_P * SSM_N), lambda i: (0, 0))
    col = pl.BlockSpec((SSM_G, 1), lambda i: (0, 0))
    return _pc(body, "ssm_prep", (1,), [small, small, wide, wide, col, wide, wide], [small, small, wide, wide],
               [_sds((SSM_G, SSM_P))] * 2 + [_sds((SSM_G, SSM_P * SSM_N))] * 2)(
        lam_re, lam_im, lam_re_rep, lam_im_rep, log_dt, b_re, b_im)


def _ssm_param_bwd(g_lam_re, g_lam_im, g_bb_re, g_bb_im, lam_re, lam_im, lam_re_rep, lam_im_rep, log_dt, b_re, b_im, seg):
    def body(glr_ref, gli_ref, gbr_ref, gbi_ref, lr_ref, li_ref, lrr_ref, lir_ref, ldt_ref, br_ref, bi_ref, seg_ref,
             dlr_ref, dli_ref, ddt_ref, dbr_ref, dbi_ref):
        dt = jnp.exp(ldt_ref[...])
        lr, li = lrr_ref[...], lir_ref[...]
        mag = jnp.exp(lr * dt)
        er, ei = mag * jnp.cos(li * dt), mag * jnp.sin(li * dt)
        den = lr * lr + li * li
        kr = ((er - 1.0) * lr + ei * li) / den
        ki = (ei * lr - (er - 1.0) * li) / den
        gbr, gbi = gbr_ref[...], gbi_ref[...]
        dbr_ref[...], dbi_ref[...] = _cmul(kr, -ki, gbr, gbi)
        tr, ti = _cmul(br_ref[...], -bi_ref[...], gbr, gbi)
        gkr = jnp.dot(tr, seg_ref[...], preferred_element_type=F32, precision=HIGHEST)
        gki = jnp.dot(ti, seg_ref[...], preferred_element_type=F32, precision=HIGHEST)
        lr, li = lr_ref[...], li_ref[...]
        mag = jnp.exp(lr * dt)
        er, ei = mag * jnp.cos(li * dt), mag * jnp.sin(li * dt)
        den = lr * lr + li * li
        ir, ii = lr / den, -li / den
        kr, ki = _cmul(er - 1.0, ei, ir, ii)
        ar, ai = _cmul(ir, -ii, gkr, gki)
        glr, gli = glr_ref[...] + ar, gli_ref[...] + ai
        qr, qi = _cmul(kr, ki, ir, ii)
        g1r, g1i = _cmul(-qr, qi, gkr, gki)
        g2r, g2i = _cmul(dt * er, -dt * ei, glr, gli)
        dlr_ref[...] = g1r + g2r
        dli_ref[...] = g1i + g2i
        wr, wi = _cmul(lr, li, er, ei)
        g_dt = jnp.sum(wr * glr + wi * gli, axis=-1, keepdims=True)
        ddt_ref[...] = jnp.broadcast_to(dt * g_dt, (SSM_G, LANE))

    small = pl.BlockSpec((SSM_G, SSM_P), lambda i: (0, 0))
    wide = pl.BlockSpec((SSM_G, SSM_P * SSM_N), lambda i: (0, 0))
    col = pl.BlockSpec((SSM_G, 1), lambda i: (0, 0))
    segs = pl.BlockSpec((SSM_P * SSM_N, SSM_P), lambda i: (0, 0))
    return _pc(body, "ssm_param_bwd", (1,), [small, small, wide, wide, small, small, wide, wide, col, wide, wide, segs],
               [small, small, pl.BlockSpec((SSM_G, LANE), lambda i: (0, 0)), wide, wide],
               [_sds((SSM_G, SSM_P))] * 2 + [_sds((SSM_G, LANE))] + [_sds((SSM_G, SSM_P * SSM_N))] * 2)(
        g_lam_re, g_lam_im, g_bb_re, g_bb_im, lam_re, lam_im, lam_re_rep, lam_im_rep, log_dt, b_re, b_im, seg)


SCAN_LANES = 512
SCAN_ROWS = 8


SCAN_GROUPS = SCAN_LANES // SSM_P
SCAN_COLS = SCAN_GROUPS * SSM_N
SCAN_CHUNK = 1024


def _ssm_scan(name, v, w_in, lam_re, lam_im, w_out, reverse, states=None, u=None):
    s = v.shape[0]
    ln, rows, ch = SCAN_LANES, SCAN_ROWS, min(SCAN_CHUNK, s)
    nch, ntile = s // ch, ch // rows
    nt_dims = (((1,), (1,)), ((), ()))
    with_sum = states is not None
    tn_dims = (((0,), (0,)), ((), ()))

    def body(*refs):
        v_ref, win_ref, lr_ref, li_ref, wout_ref = refs[:5]
        if with_sum:
            n_in = 8
            y_ref, sum_refs = refs[n_in], refs[n_in + 1:n_in + 7]
            br_s, bi_s, mb_s, mc_s, or_ref, oi_ref = refs[n_in + 7:]
            mb_s[...] = jnp.zeros_like(mb_s)
            mc_s[...] = jnp.zeros_like(mc_s)
        else:
            n_in = 5
            or_ref, oi_ref, y_ref, br_s, bi_s = refs[n_in:]
        l1 = (lr_ref[...], li_ref[...])
        pw = [l1]
        for _ in range(rows - 1):
            pw.append(_cmul(*pw[-1], *l1))
        row = lax.broadcasted_iota(jnp.int32, (rows, ln), 0)
        expo = (rows - row) if reverse else (row + 1)
        pr = jnp.zeros((rows, ln), F32)
        pi = jnp.zeros((rows, ln), F32)
        for e in range(1, rows + 1):
            pr = jnp.where(expo == e, pw[e - 1][0], pr)
            pi = jnp.where(expo == e, pw[e - 1][1], pi)
        lk = {}
        for k in (1, 2, 4):
            keep = (row < rows - k) if reverse else (row >= k)
            lk[k] = (jnp.where(keep, pw[k - 1][0], 0.0), jnp.where(keep, pw[k - 1][1], 0.0))

        def chunk(c, carry):
            q0 = pl.multiple_of(((nch - 1 - c) if reverse else c) * ch, ch)
            b = jnp.dot(_bf(v_ref[pl.ds(q0, ch), :]), win_ref[...], preferred_element_type=F32)
            br_s[...] = b[:, :ln]
            bi_s[...] = b[:, ln:]

            def step(i, carry):
                cr, ci = carry[:2]
                r0 = pl.multiple_of(((ntile - 1 - i) if reverse else i) * rows, rows)
                xr, xi = br_s[pl.ds(r0, rows), :], bi_s[pl.ds(r0, rows), :]
                for k in (1, 2, 4):
                    shift = rows - k if reverse else k
                    ar, ai = _cmul(lk[k][0], lk[k][1], pltpu.roll(xr, shift, 0), pltpu.roll(xi, shift, 0))
                    xr, xi = xr + ar, xi + ai
                ar, ai = _cmul(pr, pi, cr, ci)
                xr, xi = xr + ar, xi + ai
                g0 = pl.multiple_of(q0 + r0, rows)
                or_ref[pl.ds(g0, rows), :] = xr
                oi_ref[pl.ds(g0, rows), :] = xi
                if not with_sum:
                    return (xr[rows - 1:rows], xi[rows - 1:rows]) if not reverse else (xr[0:1], xi[0:1])
                nr = jnp.where(row == rows - 1, cr, pltpu.roll(xr, rows - 1, 0))
                ni = jnp.where(row == rows - 1, ci, pltpu.roll(xi, rows - 1, 0))
                sr, si = refs[5][pl.ds(g0, rows), :], refs[6][pl.ds(g0, rows), :]
                return xr[0:1], xi[0:1], carry[2] + (sr * nr + si * ni), carry[3] + (sr * ni - si * nr)

            carry = lax.fori_loop(0, ntile, step, carry)
            if with_sum:
                rows_c = pl.ds(q0, ch)
                uc, vc = _bf(refs[7][rows_c, :]), _bf(v_ref[rows_c, :])
                for scr, left, (right_re, right_im) in ((mb_s, uc, (or_ref, oi_ref)), (mc_s, vc, (refs[5], refs[6]))):
                    scr[:, :ln] += lax.dot_general(left, _bf(right_re[rows_c, :]), tn_dims, preferred_element_type=F32)
                    scr[:, ln:] += lax.dot_general(left, _bf(right_im[rows_c, :]), tn_dims, preferred_element_type=F32)
            w = wout_ref[...]
            y_ref[pl.ds(q0, ch), :] = (
                lax.dot_general(_bf(or_ref[pl.ds(q0, ch), :]), w[:, :ln], nt_dims, preferred_element_type=F32)
                + lax.dot_general(_bf(oi_ref[pl.ds(q0, ch), :]), w[:, ln:], nt_dims, preferred_element_type=F32))
            return carry

        zero = jnp.zeros((1, ln), F32)
        init = (zero, zero) + ((jnp.zeros((rows, ln), F32),) * 2 if with_sum else ())
        carry = lax.fori_loop(0, nch, chunk, init)
        if with_sum:
            sum_refs[0][...] = _colsum(carry[2])
            sum_refs[1][...] = _colsum(carry[3])
            row_g = lax.broadcasted_iota(jnp.int32, (SCAN_COLS, LANE), 0) // SSM_N
            lane_g = lax.broadcasted_iota(jnp.int32, (SCAN_COLS, LANE), 1) // SSM_P
            for scr, o_re, o_im in ((mb_s, sum_refs[2], sum_refs[3]), (mc_s, sum_refs[4], sum_refs[5])):
                for part, o_ref in enumerate((o_re, o_im)):
                    fold = jnp.zeros((SCAN_COLS, LANE), F32)
                    for cb in range(ln // LANE):
                        fold = fold + jnp.where(2 * cb + lane_g == row_g, scr[:, part * ln + cb * LANE:part * ln + (cb + 1) * LANE], 0.0)
                    o_ref[...] = jnp.where(row_g % 2 == 0, fold, pltpu.roll(fold, SSM_P, 1))

    vec = pl.BlockSpec((1, ln), lambda j: (0, j))
    blk = pl.BlockSpec((s, ln), lambda j: (0, j))
    cols = pl.BlockSpec((s, SCAN_COLS), lambda j: (0, j))
    wspec = pl.BlockSpec((None, SCAN_COLS, 2 * ln), lambda j: (j, 0, 0))
    ins, args = [cols, wspec, vec, vec, wspec], [v, w_in, lam_re, lam_im, w_out]
    outs, shapes = [blk, blk, cols], [_sds((s, SSM_L))] * 2 + [_sds((s, SSM_G * SSM_N))]
    scratch = [pltpu.VMEM((ch, ln), F32)] * 2
    if with_sum:
        own = pl.BlockSpec((None, SCAN_COLS, LANE), lambda j: (j, 0, 0))
        ins, args = ins + [blk, blk, cols], args + list(states) + [u]
        outs = [cols, vec, vec] + [own] * 4
        shapes = [_sds((s, SSM_G * SSM_N))] + [_sds((1, SSM_L))] * 2 + [_sds((SSM_L // ln, SCAN_COLS, LANE))] * 4
        scratch = scratch + [pltpu.VMEM((SCAN_COLS, 2 * ln), F32)] * 2 + [pltpu.VMEM((s, ln), F32)] * 2
    return _pc(body, name, (SSM_L // ln,), ins, outs, shapes, scratch=scratch)(*args)


def _ssm_act_fwd(y, u, d_skip):
    s = y.shape[0]
    ts = min(s, 512)

    def body(y_ref, u_ref, d_ref, o_ref):
        o_ref[...] = _gelu(y_ref[...] + d_ref[...] * u_ref[...]).astype(BF16)

    return _pc(body, "ssm_act_fwd", (s // ts,), [_row_spec(ts, D)] * 2 + [_vec_spec(D)], _row_spec(ts, D),
               _sds((s, D), BF16))(y, u, d_skip)


def _ssm_act_bwd(dg, y, u, d_skip):
    s = y.shape[0]
    ts = min(s, 512)

    def body(dg_ref, y_ref, u_ref, d_ref, dy_ref, dd_ref):
        uv = u_ref[...]
        dy = dg_ref[...] * _gelu_grad(y_ref[...] + d_ref[...] * uv)
        dy_ref[...] = dy.astype(BF16)
        _acc(dd_ref, pl.program_id(0) == 0, _colsum(dy * uv))

    return _pc(body, "ssm_act_bwd", (s // ts,), [_row_spec(ts, D)] * 3 + [_vec_spec(D)], [_row_spec(ts, D), _vec_spec(D)],
               [_sds((s, D), BF16), _sds((1, D))])(dg, y, u, d_skip)


def _axpy(a, b, d_skip):
    s = a.shape[0]
    ts = min(s, 512)

    def body(a_ref, b_ref, d_ref, o_ref):
        o_ref[...] = (a_ref[...] + d_ref[...] * b_ref[...].astype(F32)).astype(BF16)

    return _pc(body, "ssm_du", (s // ts,), [_row_spec(ts, D)] * 2 + [_vec_spec(D)], _row_spec(ts, D),
               _sds((s, D), BF16))(a, b, d_skip)


def _glu_fwd(zz):
    s = zz.shape[0]
    ts = min(s, 512)

    def body(a_ref, b_ref, o_ref):
        o_ref[...] = a_ref[...] * _sigmoid(b_ref[...])

    return _pc(body, "glu_fwd", (s // ts,), [_row_spec(ts, D, 0), _row_spec(ts, D, 1)], _row_spec(ts, D), _sds((s, D)))(zz, zz)


def _glu_bwd(zz, df):
    s = zz.shape[0]
    ts = min(s, 512)

    def body(a_ref, b_ref, df_ref, o_ref):
        sg = _sigmoid(b_ref[...])
        dfv = df_ref[...].astype(F32)
        o_ref[:, :D] = (dfv * sg).astype(BF16)
        o_ref[:, D:] = (dfv * a_ref[...] * sg * (1.0 - sg)).astype(BF16)

    return _pc(body, "glu_bwd", (s // ts,), [_row_spec(ts, D, 0), _row_spec(ts, D, 1), _row_spec(ts, D)],
               _row_spec(ts, 2 * D), _sds((s, 2 * D), BF16))(zz, zz, df)


def _ssm_block_diag(m_re, m_im):
    rows, half = SCAN_COLS, SCAN_LANES
    expand = jnp.tile(jnp.eye(SSM_P, dtype=BF16), (1, SCAN_GROUPS))

    def body(mr_ref, mi_ref, e_ref, o_ref):
        keep = (lax.broadcasted_iota(jnp.int32, (rows, half), 0) // SSM_N
                == lax.broadcasted_iota(jnp.int32, (rows, half), 1) // SSM_P)
        for part, m_ref in enumerate((mr_ref, mi_ref)):
            t = jnp.dot(_bf(m_ref[...]), e_ref[...], preferred_element_type=F32)
            o_ref[:, part * half:(part + 1) * half] = jnp.where(keep, t, 0.0).astype(BF16)

    blk = pl.BlockSpec((rows, SSM_P), lambda q: (q, 0))
    nb = SSM_G // SCAN_GROUPS
    return _pc(body, "ssm_block_diag", (nb,), [blk, blk, pl.BlockSpec((SSM_P, half), lambda q: (0, 0))],
               pl.BlockSpec((None, rows, 2 * half), lambda q: (q, 0, 0)), _sds((nb, rows, 2 * half), BF16))(m_re, m_im, expand)


def _mod_part(c_all, ada_w):
    n = ada_w.shape[-1]

    def body(c_ref, w_ref, o_ref):
        cv = c_ref[...]
        cond = _bf(cv * _sigmoid(cv))
        o_ref[...] = jnp.dot(cond, _bf(w_ref[...]), preferred_element_type=F32)

    return _pc(body, "mod_part", (2,), [pl.BlockSpec((N_DEV, D), lambda l: (0, 0)), pl.BlockSpec((None, D, n), lambda l: (l, 0, 0))],
               pl.BlockSpec((None, N_DEV, n), lambda l: (l, 0, 0)), _sds((2, N_DEV, n)))(c_all, ada_w)


def _ada_w_grad(c_all_t, dmod):
    nl, _, n = dmod.shape
    tr = 128

    def body(c_ref, d_ref, o_ref):
        cv = c_ref[...]
        cond = _bf(cv * _sigmoid(cv)).astype(F32)
        dm = _bf(d_ref[...]).astype(F32)
        acc = cond[:, 0:1] * dm[0:1, :]
        for b in range(1, N_DEV):
            acc = acc + cond[:, b:b + 1] * dm[b:b + 1, :]
        o_ref[...] = acc

    return _pc(body, "ada_w_grad", (nl, D // tr),
               [pl.BlockSpec((tr, N_DEV), lambda l, t: (t, 0)), pl.BlockSpec((None, N_DEV, n), lambda l, t: (l, 0, 0))],
               pl.BlockSpec((None, tr, n), lambda l, t: (l, t, 0)), _sds((nl, D, n)))(c_all_t, dmod)


def _adamw(name, parts, w, m, v, slot=0, prev=None, after=None):
    p, r, c = parts.shape
    tr = r
    while tr * c * 4 > (1 << 21) and tr % 16 == 0:
        tr //= 2
    nt = r // tr

    def body(p_ref, w_ref, m_ref, v_ref, *rest):
        g_ref, d_ref, nm_ref, nv_ref = rest[-4:]
        g = p_ref[0].astype(F32)
        for i in range(1, p):
            g = g + p_ref[i].astype(F32)
        g_ref[...] = g
        d_ref[...], nm_ref[...], nv_ref[...] = _adam_update(g, w_ref[...], m_ref[...], v_ref[...])

    blk = pl.BlockSpec((tr, c), lambda t: (slot * nt + t, 0))
    in_specs = [pl.BlockSpec((p, tr, c), lambda t: (0, t, 0)), blk, blk, blk]
    unread = list(prev or []) + ([after] if after is not None else [])
    return pl.pallas_call(
        body, name=name, grid=(nt,), in_specs=in_specs + [pl.BlockSpec(memory_space=pl.ANY)] * len(unread), out_specs=[blk] * 4,
        out_shape=[_sds(w.shape)] * 4, input_output_aliases={4 + i: i for i in range(4)} if prev else {},
        compiler_params=pltpu.CompilerParams(dimension_semantics=("arbitrary",), vmem_limit_bytes=VMEM_LIMIT_BYTES))(parts, w, m, v, *unread)


def _adam_update(g, w, m, v):
    m2 = B1 * m + (1.0 - B1) * g
    v2 = B2 * v + (1.0 - B2) * (g * g)
    m_hat = m2 / (1.0 - B1 ** STEP)
    v_hat = v2 / (1.0 - B2 ** STEP)
    return -LR * (m_hat / (jnp.sqrt(v_hat) + ADAM_EPS) + WD * w), m2, v2


def _adamw_many(name, items, after):
    n = len(items)

    def body(*refs):
        outs = refs[4 * n + 1:]
        for i in range(n):
            g, w, m, v = (r[...] for r in refs[4 * i:4 * i + 4])
            for o, val in zip(outs[3 * i:3 * i + 3], _adam_update(g, w, m, v)):
                o[...] = val

    full = lambda a: pl.BlockSpec(a.shape, lambda t: (0, 0))
    flat = [a for item in items for a in item]
    res = _pc(body, name, (1,), [full(a) for a in flat] + [pl.BlockSpec(memory_space=pl.ANY)],
              [full(item[1]) for item in items for _ in range(3)],
              [_sds(item[1].shape) for item in items for _ in range(3)])(*flat, after)
    return [tuple(res[3 * i:3 * i + 3]) for i in range(n)]


def _sum_parts(parts):
    p, r, c = parts.shape
    tr = r
    while tr * c * 4 > (1 << 19) and tr % 16 == 0:
        tr //= 2

    def body(p_ref, o_ref):
        g = p_ref[0]
        for i in range(1, p):
            g = g + p_ref[i]
        o_ref[...] = g

    return _pc(body, "sum_parts", (r // tr,), [pl.BlockSpec((p, tr, c), lambda t: (0, t, 0))], pl.BlockSpec((tr, c), lambda t: (t, 0)),
               _sds((r, c)))(parts)


def _place():
    x, y, c = lax.axis_index("x"), lax.axis_index("y"), lax.axis_index("c")
    peers = []
    for k in range(1, N_DEV):
        px = (1 - x) if k & 4 else x
        py = (1 - y) if k & 2 else y
        pc = (1 - c) if k & 1 else c
        peers.append(((px, py, pc), 4 * px + 2 * py + pc))
    return 4 * x + 2 * y + c, peers


def _at(ref, idx):
    return ref if idx is None else ref.at[idx]


def _exchange_copies(plan, n, src_refs, dst_refs, send_sems, recv_sems, local_sems=None, with_arrivals=True):
    me, peers = _place()
    local = [] if local_sems is None else [
        pltpu.make_async_copy(_at(src_refs[si], sx), _at(dst_refs[di], dx), local_sems.at[i])
        for i, (si, sx, di, dx) in enumerate(plan(me, me, 0))]

    def remote(k, i, dev, entry):
        si, sx, di, dx = entry
        return pltpu.make_async_remote_copy(_at(src_refs[si], sx), _at(dst_refs[di], dx), send_sems.at[k * n + i], recv_sems.at[k * n + i],
                                            device_id=dev, device_id_type=MESH)

    sends = [remote(k, i, dev, e) for k, (dev, peer) in enumerate(peers) for i, e in enumerate(plan(me, peer, k + 1))]
    if not with_arrivals:
        return local, sends, []
    arrivals = [remote(k, i, dev, e) for k, (dev, peer) in enumerate(peers) for i, e in enumerate(plan(peer, me, k + 1))]
    return local, sends, arrivals


def _sem_shapes(n_copies, local=True):
    sems = [pltpu.SemaphoreType.DMA(((N_DEV - 1) * n_copies,)), pltpu.SemaphoreType.DMA(((N_DEV - 1) * n_copies,))]
    return sems + [pltpu.SemaphoreType.DMA((n_copies,))] if local else sems


def _exchange(name, srcs, dst_shapes, plan, n_copies):
    ns, nd = len(srcs), len(dst_shapes)

    def body(*refs):
        local, sends, arrivals = _exchange_copies(plan, n_copies, refs[:ns], refs[ns:ns + nd], *refs[ns + nd:])
        for cp in local + sends:
            cp.start()
        for cp in arrivals:
            cp.wait_recv()
        for cp in sends:
            cp.wait_send()
        for cp in local:
            cp.wait()

    any_spec = pl.BlockSpec(memory_space=pl.ANY)
    return pl.pallas_call(
        body, name=name, in_specs=[any_spec] * ns, out_specs=[any_spec] * nd, out_shape=list(dst_shapes),
        scratch_shapes=_sem_shapes(n_copies))(*srcs)


HBM_SPEC = pl.BlockSpec(memory_space=pltpu.HBM)
SEM_SPEC = pl.BlockSpec(memory_space=pltpu.SEMAPHORE)
ANY_SPEC = pl.BlockSpec(memory_space=pl.ANY)
TOKEN_SPEC = pl.BlockSpec(memory_space=pltpu.VMEM)
SIDE_EFFECT = pltpu.SideEffectType.DATAFLOW_SIDE_EFFECTING


def _wait_all(local, sends, arrivals):
    for cp in arrivals:
        cp.wait_recv()
    for cp in sends:
        cp.wait_send()
    for cp in local:
        cp.wait()


def _exchange_start(name, srcs, dst_shapes, plan, n_copies, order):
    ns, nd = len(srcs), len(dst_shapes)
    nb = ns + nd

    def body(*refs):
        local, sends, _ = _exchange_copies(plan, n_copies, refs[:ns], refs[ns:nb], *refs[nb + 1:nb + 4], with_arrivals=False)
        for cp in local + sends:
            cp.start()
        refs[-1][...] = jnp.zeros((8, LANE), F32)

    lands = [pltpu.with_memory_space_constraint(lax.empty(d.shape, d.dtype), pltpu.HBM) for d in dst_shapes]
    srcs = [pltpu.with_memory_space_constraint(a, pltpu.HBM) for a in srcs]
    bufs = srcs + lands
    out = pl.pallas_call(
        body, name=name, in_specs=[HBM_SPEC] * nb + [ANY_SPEC],
        out_specs=[SEM_SPEC] * 3 + [HBM_SPEC] * nb + [TOKEN_SPEC],
        out_shape=_sem_shapes(n_copies) + [pltpu.HBM(a.shape, a.dtype) for a in bufs] + [_sds((8, LANE))],
        input_output_aliases={i: 3 + i for i in range(nb)},
        compiler_params=pltpu.CompilerParams(has_side_effects=SIDE_EFFECT))(*bufs, order)
    return out[:3], out[3:3 + ns], out[3 + ns:3 + nb], out[-1]


def _exchange_relay(name, sems, srcs, lands, plan, n_copies, plan2, n_copies2, after):
    ns, nd = len(srcs), len(lands)
    nb = ns + nd

    def body(*refs):
        land_refs = refs[ns:nb]
        _wait_all(*_exchange_copies(plan, n_copies, refs[:ns], land_refs, *refs[nb:nb + 3]))
        _, sends, _ = _exchange_copies(plan2, n_copies2, land_refs, land_refs, *refs[nb + 4:nb + 6], with_arrivals=False)
        for cp in sends:
            cp.start()
        refs[-1][...] = jnp.zeros((8, LANE), F32)

    out = pl.pallas_call(
        body, name=name, in_specs=[HBM_SPEC] * nb + [SEM_SPEC] * 3 + [ANY_SPEC],
        out_specs=[SEM_SPEC] * 2 + [HBM_SPEC] * nd + [TOKEN_SPEC],
        out_shape=_sem_shapes(n_copies2, local=False) + [pltpu.HBM(a.shape, a.dtype) for a in lands] + [_sds((8, LANE))],
        input_output_aliases={ns + i: 2 + i for i in range(nd)},
        compiler_params=pltpu.CompilerParams(has_side_effects=SIDE_EFFECT))(*srcs, *lands, *sems, after)
    return out[:2], out[2:2 + nd], out[-1]


def _exchange_wait(name, sems, srcs, lands, plan, n_copies, after):
    srcs = [] if srcs is None else list(srcs)
    ns, nd = len(srcs), len(lands)
    nb = ns + nd

    def body(*refs):
        land_refs = refs[ns:nb]
        _wait_all(*_exchange_copies(plan, n_copies, refs[:ns] if ns else land_refs, land_refs, *refs[nb:nb + len(sems)]))

    bufs = srcs + list(lands)
    out = pl.pallas_call(
        body, name=name, in_specs=[HBM_SPEC] * nb + [SEM_SPEC] * len(sems) + [ANY_SPEC],
        out_specs=[HBM_SPEC] * nb, out_shape=[pltpu.HBM(a.shape, a.dtype) for a in bufs],
        input_output_aliases={i: i for i in range(nb)},
        compiler_params=pltpu.CompilerParams(has_side_effects=SIDE_EFFECT))(*bufs, *sems, after)
    return out[ns:]


def _all_gather(name, arrs):
    plan = lambda me, peer, k: [(i, None, i, me) for i in range(len(arrs))]
    return _exchange(name, arrs, [_sds((N_DEV,) + a.shape, a.dtype) for a in arrs], plan, len(arrs))


def _mix0_fwd(h, p):
    z = _mm_nt("mix0_in", h, p["ab_w_in"])
    y_a, d = _pool_fwd(z, p["pool_w"], p["pool_scale"])
    y_b = _sgu_fwd(z, p["sgu_ln_g"], p["sgu_ln_b"], p["sgu_w"], p["sgu_bt"])
    ycat = jnp.concatenate([y_a, y_b], axis=1)
    return _mm_nn("mix0_out", ycat, p["ab_w_out"]), (h, z, d, ycat)


def _mix0_bwd(df, saved, p, after):
    h, z, d, ycat = saved
    dycat = _mm_nt("mix0_out_dx", df, p["ab_w_out"], after=after)
    g = {"ab_w_out": _mm_tn("mix0_out_dw", ycat, df, BF16)}
    dz_p, g["pool_w"], g["pool_scale"] = _pool_bwd(dycat, d, p["pool_w"], p["pool_scale"])
    dz_u, dz_v, g["sgu_ln_g"], g["sgu_ln_b"], g["sgu_w"], dbt = _sgu_bwd(
        z, dycat, p["sgu_ln_g"], p["sgu_ln_b"], p["sgu_w"], p["sgu_bt"], p["head_sum"])
    g["sgu_b"] = dbt[:, :NH].T
    dz = jnp.concatenate([dz_p, dz_u, dz_v], axis=1)
    g["ab_w_in"] = _mm_tn("mix0_in_dw", dz, h, BF16)
    return _mm_nn("mix0_in_dx", dz, p["ab_w_in"]), g


def _mix1_fwd(h, p):
    u = _mm_nn("ssm_w_in", h, p["ssm_w_in"])
    x_re, x_im, y = _ssm_scan("ssm_scan_fwd", u, p["wb_bd"], p["lam_bar_re"], p["lam_bar_im"], p["wc_bd"], False)
    g = _ssm_act_fwd(y, u, p["ssm_d"])
    zz = _mm_nn("ssm_glu", g, p["ssm_w_glu"])
    return _glu_fwd(zz), (h, u, x_re, x_im, y, g, zz)


def _mix1_bwd(df, saved, p, after):
    h, u, x_re, x_im, y, g, zz = saved
    gr = {}
    dzz = _glu_bwd(zz, df)
    dg = _mm_nt("ssm_glu_dx", dzz, p["ssm_w_glu"], after=after)
    gr["ssm_w_glu"] = _mm_tn("ssm_glu_dw", g, dzz, BF16)
    dy, gr["ssm_d"] = _ssm_act_bwd(dg, y, u, p["ssm_d"])
    du_ssm, g_lam_re, g_lam_im, mb_re, mb_im, mc_re, mc_im = _ssm_scan(
        "ssm_scan_bwd", dy, p["wc_bd"], p["lam_bar_re"], -p["lam_bar_im"], p["wb_bd"], True, states=(x_re, x_im), u=u)
    du = _axpy(du_ssm, dy, p["ssm_d"])
    gr["ssm_w_in"] = _mm_tn("ssm_w_in_dw", h, du, BF16)
    dh = _mm_nt("ssm_w_in_dx", du, p["ssm_w_in"])
    per_group = lambda m: m[:, :, :SSM_P].reshape(SSM_G, SSM_N, SSM_P)
    gr["ssm_c_re"] = per_group(mc_re)
    gr["ssm_c_im"] = -per_group(mc_im)
    dlr, dli, ddt, dbr, dbi = _ssm_param_bwd(
        g_lam_re.reshape(SSM_G, SSM_P), g_lam_im.reshape(SSM_G, SSM_P),
        per_group(mb_re).reshape(SSM_G, SSM_N * SSM_P), per_group(mb_im).reshape(SSM_G, SSM_N * SSM_P),
        p["lam_re"], p["lam_im"], p["lam_re_rep"], p["lam_im_rep"], p["log_dt"], p["b_re"], p["b_im"], p["seg"])
    gr["ssm_lam_re"], gr["ssm_lam_im"], gr["ssm_log_dt"] = dlr, dli, ddt[:, 0]
    gr["ssm_b_re"] = dbr.reshape(SSM_G, SSM_N, SSM_P)
    gr["ssm_b_im"] = dbi.reshape(SSM_G, SSM_N, SSM_P)
    return dh, gr


def _ssm_params(lam_re, lam_im, b_re, b_im, c_re, c_im, log_dt):
    wide = lambda b: b.transpose(0, 2, 1).reshape(SSM_G, SSM_N * SSM_P)
    p = {"lam_re": lam_re, "lam_im": lam_im, "log_dt": log_dt.reshape(SSM_G, 1),
         "lam_re_rep": jnp.tile(lam_re, (1, SSM_N)), "lam_im_rep": jnp.tile(lam_im, (1, SSM_N)), "b_re": wide(b_re), "b_im": wide(b_im)}
    lbr, lbi, bbr, bbi = _ssm_prep(lam_re, lam_im, p["lam_re_rep"], p["lam_im_rep"], p["log_dt"], p["b_re"], p["b_im"])
    p["lam_bar_re"], p["lam_bar_im"] = lbr.reshape(1, SSM_L), lbi.reshape(1, SSM_L)
    rows = lambda m: m.reshape(SSM_G * SSM_N, SSM_P)
    p["wb_bd"] = _ssm_block_diag(rows(bbr), rows(bbi))
    p["wc_bd"] = _ssm_block_diag(rows(c_re), rows(-c_im))
    p["seg"] = jnp.tile(jnp.eye(SSM_P, dtype=F32), (SSM_N, 1))
    return p


RES_WEIGHT = (0.5, 1.0, 0.5)


def _local_step(x, tgt, vecs, weights_of, on_part, on_grads):
    def fns(i, w):
        if i % 3 != 1:
            win, wout_of = w
            return ((lambda h: _ffn_fwd(h, win, wout_of)),
                    (lambda df, sv, after: (_ffn_bwd(df, sv, win, wout_of(None), lambda tag, part: on_part(i, tag, part), after), None)))
        if i == 1:
            return (lambda h: _mix0_fwd(h, w)), (lambda df, sv, after: _mix0_bwd(df, sv, w, after))
        return (lambda h: _mix1_fwd(h, w)), (lambda df, sv, after: _mix1_bwd(df, sv, w, after))

    rw = RES_WEIGHT * 2
    saved, bwd = [], []
    f = None
    for i in range(6):
        w, token = weights_of(i, x if i == 0 else f)
        fwd, b = fns(i, w)
        if i == 0:
            h = _prenorm_fwd(x, vecs, 0, token)
        else:
            x, h = _post_pre_fwd(x, f, vecs, i, rw[i - 1], token)
        f, inner = fwd(h)
        saved.append((x, f, inner))
        bwd.append(b)
    loss_row, dx, df, dv_top = _last_shell(x, f, tgt, vecs, 5, rw[5])
    token = jnp.zeros((8, LANE), F32)
    for i in reversed(range(6)):
        x_i, _, inner = saved[i]
        dh, extra = bwd[i](df, inner, token)
        if i > 0:
            dx, df, dv = _pre_post_bwd(dx, dh, x_i, saved[i - 1][1], vecs, i, rw[i - 1])
        else:
            dx, dv = _prenorm_bwd(dx, dh, x_i, vecs, 0)
        token = on_grads(i, extra, dv, dv_top if i == 5 else None, loss_row)
    return dx


def _pad_rows(v, rows):
    return jnp.pad(v, (0, rows * LANE - v.shape[0])).reshape(rows, LANE)


def _pack(parts):
    flat, layout, off = [], [], 0
    for a in parts:
        n = a.size
        padded = -(-n // LANE) * LANE
        flat.append(jnp.pad(a.reshape(-1).astype(F32), (0, padded - n)))
        layout.append((off, n, a.shape))
        off += padded
    return jnp.concatenate(flat), layout


def _unpack(flat, layout):
    return [flat[off:off + n].reshape(shape) for off, n, shape in layout]


TRANSPOSED = ["ffn_w_in", "ab_w_in", "ssm_b_re", "ssm_b_im"]
WEIGHTS = ['ada_w', 'ada_b', 'norm_pre', 'norm_post', 'ffn_w_in', 'ffn_w_out', 'ab_w_in', 'pool_w', 'pool_scale', 'sgu_ln_g',
           'sgu_ln_b', 'sgu_w', 'sgu_b', 'ab_w_out', 'ssm_w_in', 'ssm_lam_re', 'ssm_lam_im', 'ssm_b_re', 'ssm_b_im', 'ssm_c_re',
           'ssm_c_im', 'ssm_d', 'ssm_log_dt', 'ssm_w_glu']


def kernel(x, c, ada_w, ada_b, norm_pre, norm_post, ffn_w_in, ffn_w_out, ab_w_in, pool_w, pool_scale, sgu_ln_g, sgu_ln_b, sgu_w, sgu_b, ab_w_out, ssm_w_in, ssm_lam_re, ssm_lam_im, ssm_b_re, ssm_b_im, ssm_c_re, ssm_c_im, ssm_d, ssm_log_dt, ssm_w_glu, loss_target, m_ada_w, m_ada_b, m_norm_pre, m_norm_post, m_ffn_w_in, m_ffn_w_out, m_ab_w_in, m_pool_w, m_pool_scale, m_sgu_ln_g, m_sgu_ln_b, m_sgu_w, m_sgu_b, m_ab_w_out, m_ssm_w_in, m_ssm_lam_re, m_ssm_lam_im, m_ssm_b_re, m_ssm_b_im, m_ssm_c_re, m_ssm_c_im, m_ssm_d, m_ssm_log_dt, m_ssm_w_glu, v_ada_w, v_ada_b, v_norm_pre, v_norm_post, v_ffn_w_in, v_ffn_w_out, v_ab_w_in, v_pool_w, v_pool_scale, v_sgu_ln_g, v_sgu_ln_b, v_sgu_w, v_sgu_b, v_ab_w_out, v_ssm_w_in, v_ssm_lam_re, v_ssm_lam_im, v_ssm_b_re, v_ssm_b_im, v_ssm_c_re, v_ssm_c_im, v_ssm_d, v_ssm_log_dt, v_ssm_w_glu):
    args = locals()
    wts = {n: args[n] for n in WEIGHTS}
    mom = {n: args["m_" + n] for n in WEIGHTS}
    var = {n: args["v_" + n] for n in WEIGHTS}
    for n in TRANSPOSED:
        for t in (wts, mom, var):
            t[n] = jnp.swapaxes(t[n], -1, -2)
    me = 4 * lax.axis_index("x") + 2 * lax.axis_index("y") + lax.axis_index("c")
    s = x.shape[1]
    nd = D // N_DEV

    small_in, small_in_layout = _pack([c, norm_pre, norm_post, ssm_d])
    small_rows = -(-small_in.shape[0] // (8 * LANE)) * 8
    (g_small,) = _all_gather("gather_small", [_pad_rows(small_in, small_rows)])
    g_small = g_small.reshape(N_DEV, -1)
    c_all, npre_g, npost_g, sd_g = [jnp.stack([_unpack(g_small[j], small_in_layout)[i] for j in range(N_DEV)]) for i in range(4)]
    c_all = c_all.reshape(N_DEV, D)
    norm_pre_full = npre_g.transpose(1, 2, 0, 3).reshape(2, 3, D)
    norm_post_full = npost_g.transpose(1, 2, 0, 3).reshape(2, 3, D)
    ssm_d_full = sd_g.transpose(1, 0, 2).reshape(1, D)

    nw = ada_w.shape[-1]
    (mod_g,) = _all_gather("gather_mod", [_mod_part(c_all, ada_w)])
    mod = lax.dynamic_index_in_dim(mod_g, me, axis=2, keepdims=False)
    mod = (mod.transpose(1, 0, 2).reshape(2, N_DEV * nw) + ada_b).reshape(2, 3, 3, D)

    w_in_t = wts["ffn_w_in"]
    shards = [[w_in_t[0, 0]], [ffn_w_out[0, 0]], [wts["ab_w_in"][0], ab_w_out[0]], [w_in_t[0, 1]], [ffn_w_out[0, 1]],
              [w_in_t[1, 0]], [ffn_w_out[1, 0]], [ssm_w_in[0], ssm_w_glu[0]], [w_in_t[1, 1]], [ffn_w_out[1, 1]]]
    first_group = {0: 0, 1: 2, 2: 3, 3: 5, 4: 7, 5: 8}
    first_groups = set(first_group.values())
    same_core = (2, 4, 6)

    def gather_plan(n):
        return lambda me_, peer_, k: [(a, None, a, me_) for a in range(n)] if k in (0, 1) + same_core else []

    def relay_plan(n):
        return lambda me_, peer_, k: [(a, me_ ^ kk, a, me_ ^ kk) for kk in same_core for a in range(n)] if k == 1 else []

    gathers, relays = [], {}
    token = mod_g
    for g, group in enumerate(shards):
        group = [a.astype(BF16) for a in group]
        sems, srcs_thru, lands, token = _exchange_start(
            f"gather_start_{g}", group, [_sds((N_DEV,) + a.shape, BF16) for a in group], gather_plan(len(group)), len(group), token)
        gathers.append((sems, srcs_thru, lands))
    mod6 = mod.reshape(6, 3, D)
    vecs = jnp.stack([norm_pre_full.reshape(6, D), mod6[:, 1], mod6[:, 0], norm_post_full.reshape(6, D), mod6[:, 2]]
                     + [jnp.zeros((6, D), F32)] * 3, axis=1)
    vecs = vecs + token[0, 0]

    def relay(g, after):
        sems, srcs_thru, lands = gathers[g]
        n = len(lands)
        relays[g] = _exchange_relay(f"gather_relay_{g}", sems, srcs_thru, lands, gather_plan(n), n, relay_plan(n), 3 * n, after)

    def fetch(g, after):
        if g not in relays:
            relay(g, after)
        sems, lands, token = relays[g]
        n = len(lands)
        got = _exchange_wait(f"gather_wait_{g}", sems, None, lands, relay_plan(n), 3 * n, after)
        if g + 1 in first_groups:
            relay(g + 1, got[0])
            token = relays[g + 1][2]
        return got, token

    head_sum = jnp.repeat(jnp.eye(NH, LANE, dtype=F32), HD, axis=0)
    mix0 = {"pool_w": pool_w[0], "pool_scale": pool_scale, "sgu_ln_g": sgu_ln_g, "sgu_ln_b": sgu_ln_b, "sgu_w": sgu_w[0],
            "sgu_bt": jnp.pad(sgu_b[0].T, ((0, 0), (0, LANE - NH))), "head_sum": head_sum}
    mix1 = _ssm_params(ssm_lam_re[0], ssm_lam_im[0], ssm_b_re[0], ssm_b_im[0], ssm_c_re[0], ssm_c_im[0], ssm_log_dt[0])
    mix1["ssm_d"] = ssm_d_full

    def weights_of(i, x_in):
        g = first_group[i]
        if i % 3 != 1:
            (win,), token = fetch(g, x_in)
            cache = []

            def wout_of(act):
                if not cache:
                    cache.append(fetch(g + 1, act)[0][0])
                return cache[0]

            return (win, wout_of), token
        (a, b), token = fetch(g, x_in)
        if i == 1:
            return dict(mix0, ab_w_in=a.reshape(-1, D), ab_w_out=b.reshape(D, D)), token
        return dict(mix1, ssm_w_in=a.reshape(D, D), ssm_w_glu=b.transpose(1, 0, 2).reshape(D, -1)), token

    def shard_cols(a):
        r = a.shape[0]
        return a.reshape(r, N_DEV, -1).transpose(1, 0, 2)

    scatter_plan = lambda me_, peer_, k: [(0, peer_, 0, me_), (1, peer_, 1, me_)]
    scatter_plan1 = lambda me_, peer_, k: [(0, peer_, 0, me_)]
    scatters = []
    last_token = [jnp.zeros((8, LANE), F32)]
    pieces, mixer, bundles = {}, {}, {}
    bundle_plan = lambda me_, peer_, k: [(0, None, 0, me_)]

    held = {}

    def send_parts(i, tag, names, parts, plan):
        sems, srcs_thru, lands, last_token[0] = _exchange_start(
            f"scatter_start_{i}_{tag}", parts, [_sds(a.shape, BF16) for a in parts], plan, len(parts), last_token[0])
        scatters.append((i, names, plan, sems, srcs_thru, lands))
        return last_token[0]

    def on_part(i, tag, part):
        if tag == "w_out" and i != 0:
            held[i] = part
            return last_token[0]
        if i == 0:
            if tag == "w_in":
                held[0] = part
                return last_token[0]
            return send_parts(0, tag, ("ffn_w_out",), [part], scatter_plan1)
        return send_parts(i, tag, ("ffn_w_out", "ffn_w_in"), [held[i], part], scatter_plan)
    mix0_names = ["pool_w", "pool_scale", "sgu_ln_g", "sgu_ln_b", "sgu_w", "sgu_b"]
    mix1_names = ["ssm_lam_re", "ssm_lam_im", "ssm_b_re", "ssm_b_im", "ssm_c_re", "ssm_c_im", "ssm_log_dt", "ssm_d"]

    def start_bundle(tag, arrays):
        flat, layout = _pack(arrays)
        rows = -(-flat.shape[0] // (8 * LANE)) * 8
        plan = gather_plan(1) if tag == "a" else bundle_plan
        sems, srcs_thru, lands, last_token[0] = _exchange_start(
            f"small_start_{tag}", [_pad_rows(flat, rows)], [_sds((N_DEV, rows, LANE))], plan, 1, last_token[0])
        bundles[tag] = (sems, srcs_thru, lands, layout)

    def on_grads(i, extra, dv, dv_top, loss_row):
        pieces[i] = dv
        if i == 5:
            pieces["top"] = dv_top
        if i == 4:
            mixer.update({n: extra[n] for n in mix1_names})
        if i == 1:
            mixer.update({n: extra[n] for n in mix0_names})
            start_bundle("a", [jnp.stack([pieces[j] for j in ("top", 5, 4, 3, 2, 1)])] + [mixer[n] for n in mix0_names + mix1_names])
        if i == 0:
            start_bundle("b", [dv, loss_row])
            return send_parts(0, "w_in", ("ffn_w_in",), [held[0]], scatter_plan1)
        if i % 3 != 1:
            return last_token[0]
        if i == 1:
            names, parts = ("ab_w_in", "ab_w_out"), [extra["ab_w_in"].reshape(N_DEV, -1, D), extra["ab_w_out"].reshape(N_DEV, nd, D)]
        else:
            names, parts = ("ssm_w_in", "ssm_w_glu"), [extra["ssm_w_in"].reshape(N_DEV, nd, D), shard_cols(extra["ssm_w_glu"])]
        sems, srcs_thru, lands, last_token[0] = _exchange_start(
            f"scatter_start_{i}", parts, [_sds(a.shape, BF16) for a in parts], scatter_plan, 2, last_token[0])
        scatters.append((i, names, scatter_plan, sems, srcs_thru, lands))
        return last_token[0]

    grad_x = _local_step(x[0], loss_target[0], vecs, weights_of, on_part, on_grads)

    out_g, out_d, out_m, out_v = {}, {}, {}, {}
    big_out = {}

    def adam_big(name, recv, n, slot=0, after=None):
        c_ = wts[n].shape[-1]
        big_out[n] = _adamw(name, recv.reshape(recv.shape[0], -1, c_), *[t[n].reshape(-1, c_) for t in (wts, mom, var)],
                            slot=slot, prev=big_out.get(n), after=after)
        return big_out[n][0]

    ffn_slot = {0: 0, 2: 1, 3: 2, 5: 3}

    def land_and_update(entries, after):
        for i, names, plan, sems, srcs_thru, lands in entries:
            recv = _exchange_wait(f"scatter_wait_{i}_{names[0]}", sems, srcs_thru, lands, plan, len(names), after)
            for n, r in zip(names, recv):
                after = adam_big(f"adamw_{n}_{i}", r, n, ffn_slot.get(i, 0), after)
        return after

    after = land_and_update([e for e in scatters if e[0] != 0], last_token[0])

    def landed(tag, g_parts):
        layout = bundles[tag][3]
        off, n, shape = layout[0]
        dmods = g_parts.reshape(N_DEV, -1)[:, off:off + n].reshape((N_DEV,) + shape)
        total = _sum_parts(g_parts)
        return dmods, _unpack(total.reshape(-1), layout), total

    def adam_small(n, g, after=None):
        cols = wts[n].shape[-1]
        res = _adamw(f"adamw_{n}", g.reshape(1, -1, cols), *[t[n].reshape(-1, cols) for t in (wts, mom, var)], after=after)
        for o, arr in zip((out_g, out_d, out_m, out_v), res):
            o[n] = arr.reshape(wts[n].shape)
            if n in TRANSPOSED:
                o[n] = jnp.swapaxes(o[n], -1, -2)
        return res[0]

    sems, srcs_thru, lands, _ = bundles["a"]
    sems, lands, _ = _exchange_relay("small_relay_a", sems, srcs_thru, lands, gather_plan(1), 1, relay_plan(1), 3, after)
    (parts_a,) = _exchange_wait("small_wait_a", sems, None, lands, relay_plan(1), 3, after)
    shells_a, sums_a, after = landed("a", parts_a)
    small = dict(zip(mix0_names + mix1_names, sums_a[1:]))
    def adam_tiny(name, grads, after):
        view = lambda n, a: a.reshape(-1, wts[n].shape[-1])
        items = [(view(n, g),) + tuple(view(n, t[n]) for t in (wts, mom, var)) for n, g in grads.items()]
        for (n, _), item, res in zip(grads.items(), items, _adamw_many(name, items, after)):
            for o, arr in zip((out_g, out_d, out_m, out_v), (item[0],) + res):
                o[n] = arr.reshape(wts[n].shape)
        return res[0]

    tiny = ["pool_scale", "sgu_ln_g", "sgu_ln_b", "sgu_b", "ssm_lam_re", "ssm_lam_im", "ssm_log_dt"]
    for n in [n for n in mix0_names + mix1_names if n not in tiny and n != "ssm_d"]:
        after = adam_small(n, small[n], after)
    after = adam_tiny("adamw_tiny_mixers", dict({n: small[n] for n in tiny},
                                                ssm_d=lax.dynamic_slice_in_dim(small["ssm_d"], me * nd, nd, axis=1)), after)
    def shell_grads(top, blocks, first):
        own_rows = jnp.concatenate([first[..., None, :, :], blocks[..., :0:-1, :, :]], axis=-3)
        next_rows = jnp.concatenate([own_rows[..., 1:, :, :], top[..., None, :, :]], axis=-3)
        dmod_ = jnp.stack([own_rows[..., V_SHIFT, :], own_rows[..., V_SCALE, :], next_rows[..., V_GATE, :]], axis=-2)
        return dmod_, own_rows[..., V_GPRE, :], next_rows[..., V_GPOST, :]

    def ada_w_layer(l, dmod_l, after):
        mine = lax.dynamic_index_in_dim(dmod_l.reshape(N_DEV, N_DEV, nw), me, axis=1, keepdims=False)
        return adam_big(f"adamw_ada_w_{l}", _ada_w_grad(c_all.T, mine[None]), "ada_w", l, after)

    after = ada_w_layer(1, shell_grads(shells_a[:, 0], shells_a, jnp.zeros_like(shells_a[:, 0]))[0][:, 3:], after)
    sems, srcs_thru, lands, _ = bundles["b"]
    (parts_b,) = _exchange_wait("small_wait_b", sems, srcs_thru, lands, bundle_plan, 1, after)
    shell_b, (first_sum, loss_sum), after = landed("b", parts_b)
    loss = loss_sum[0, 0]

    dmod_sum, dg_pre_sum, dg_post_sum = shell_grads(sums_a[0][0], sums_a[0], first_sum)
    own = lambda a: lax.dynamic_slice_in_dim(a, me * nd, nd, axis=1)
    after = adam_tiny("adamw_tiny_shell", {"ada_b": dmod_sum, "norm_pre": own(dg_pre_sum), "norm_post": own(dg_post_sum)}, after)

    after = ada_w_layer(0, shell_grads(shells_a[:, 0], shells_a, shell_b)[0][:, :3], after)

    land_and_update([e for e in scatters if e[0] == 0], after)
    for n, res in big_out.items():
        for o, arr in zip((out_g, out_d, out_m, out_v), res):
            o[n] = arr.reshape(wts[n].shape)
            if n in TRANSPOSED:
                o[n] = jnp.swapaxes(o[n], -1, -2)

    return (loss, grad_x[None], *[out_g[n] for n in WEIGHTS], *[out_d[n] for n in WEIGHTS],
            *[out_m[n] for n in WEIGHTS], *[out_v[n] for n in WEIGHTS])
```

```python
import math

import jax
import jax.numpy as jnp
from jax import lax
from jax.experimental import pallas as pl
from jax.experimental.pallas import tpu as pltpu

F32 = jnp.float32
BF16 = jnp.bfloat16
MESH = pl.DeviceIdType.MESH
HIGHEST = lax.Precision.HIGHEST

N_DEV = 8
D = 1024
D_FF = 2816
FSH = 2 * D_FF // N_DEV
EPS = 1e-6
POOL_WINDOWS = (2, 4, 8, 16)
HD = 128
NH = 4
SSM_G, SSM_P, SSM_N = 64, 64, 16
SSM_L = SSM_G * SSM_P
LR, B1, B2, ADAM_EPS, WD, STEP = 0.001, 0.9, 0.999, 1e-08, 0.01, 10
GELU_C = math.sqrt(2.0 / math.pi)
VMEM_LIMIT_BYTES = 48 * 1024 * 1024
LANE = 128


def _pc(body, name, grid, in_specs, out_specs, out_shape, scratch=()):
    return pl.pallas_call(
        body, name=name, grid=grid, in_specs=in_specs, out_specs=out_specs, out_shape=out_shape,
        scratch_shapes=list(scratch),
        compiler_params=pltpu.CompilerParams(dimension_semantics=("arbitrary",) * len(grid),
                                             vmem_limit_bytes=VMEM_LIMIT_BYTES))


def _sds(shape, dtype=F32):
    return jax.ShapeDtypeStruct(tuple(shape), dtype)


def _bf(v):
    return v if v.dtype == BF16 else v.astype(BF16)


def _row_spec(ts, width, col=0):
    return pl.BlockSpec((ts, width), lambda t, _c=col: (t, _c))


def _vec_spec(width, col=0):
    return pl.BlockSpec((1, width), lambda t, _c=col: (0, _c))


def _mm(name, a, b, contract, grid, a_spec, b_spec, o_spec, out_shape, acc_axis=None, after=None):
    dn = (contract, ((), ()))

    def body(a_ref, b_ref, *rest):
        o_ref = rest[-1]
        r = lax.dot_general(_bf(a_ref[...]), _bf(b_ref[...]), dn, preferred_element_type=F32)
        if acc_axis is None:
            o_ref[...] = r.astype(o_ref.dtype)
        else:
            k = pl.program_id(acc_axis)

            @pl.when(k == 0)
            def _():
                o_ref[...] = r

            @pl.when(k > 0)
            def _():
                o_ref[...] += r

    if after is None:
        return _pc(body, name, grid, [a_spec, b_spec], o_spec, out_shape)(a, b)
    return _pc(body, name, grid, [a_spec, b_spec, pl.BlockSpec(memory_space=pl.ANY)], o_spec, out_shape)(a, b, after)


def _mm_sum(name, a, b, ts, after=None):
    nj, s, k = a.shape
    n = b.shape[2]

    def body(a_ref, b_ref, *rest):
        acc = jnp.dot(a_ref[0], b_ref[0], preferred_element_type=F32)
        for j in range(1, nj):
            acc = acc + jnp.dot(a_ref[j], b_ref[j], preferred_element_type=F32)
        rest[-1][...] = acc

    specs = [pl.BlockSpec((nj, ts, k), lambda t: (0, t, 0)), pl.BlockSpec((nj, k, n), lambda t: (0, 0, 0))]
    args = (a, b)
    if after is not None:
        specs, args = specs + [pl.BlockSpec(memory_space=pl.ANY)], args + (after,)
    return _pc(body, name, (s // ts,), specs, pl.BlockSpec((ts, n), lambda t: (t, 0)), _sds((s, n)))(*args)


def _tile(s):
    return min(s, 1024)


def _div_tile(n, cap=1024):
    t = min(n, cap) // LANE * LANE
    while n % t:
        t -= LANE
    return t


def _mm_nn(name, a, b, out_dtype=F32):
    s, k = a.shape
    n = b.shape[1]
    ts, tn = _tile(s), _div_tile(n)
    return _mm(name, a, b, ((1,), (0,)), (n // tn, s // ts),
               pl.BlockSpec((ts, k), lambda j, t: (t, 0)), pl.BlockSpec((k, tn), lambda j, t: (0, j)),
               pl.BlockSpec((ts, tn), lambda j, t: (t, j)), _sds((s, n), out_dtype))


def _mm_nt(name, a, b, out_dtype=F32, after=None):
    s, n = a.shape
    k = b.shape[0]
    ts, tk = _tile(s), _div_tile(k)
    return _mm(name, a, b, ((1,), (1,)), (k // tk, s // ts),
               pl.BlockSpec((ts, n), lambda j, t: (t, 0)), pl.BlockSpec((tk, n), lambda j, t: (j, 0)),
               pl.BlockSpec((ts, tk), lambda j, t: (t, j)), _sds((s, k), out_dtype), after=after)


def _mm_tn(name, a, b, out_dtype=F32, tm=512, tn=512):
    s, m = a.shape
    n = b.shape[1]
    tm, tn = min(m, tm), min(n, tn)
    return _mm(name, a, b, ((0,), (0,)), (m // tm, n // tn),
               pl.BlockSpec((s, tm), lambda i, j: (0, i)), pl.BlockSpec((s, tn), lambda i, j: (0, j)),
               pl.BlockSpec((tm, tn), lambda i, j: (i, j)), _sds((m, n), out_dtype))


def _rstd(v):
    return lax.rsqrt(jnp.mean(v * v, axis=-1, keepdims=True) + EPS)


V_GPRE, V_SCALE, V_SHIFT, V_GPOST, V_GATE = range(5)


def _vrow(v, r):
    return v[r:r + 1]


def _vblock(i):
    return pl.BlockSpec((None, 8, D), lambda t: (i, 0, 0))


def _head(xv, v):
    return ((xv * _rstd(xv) * _vrow(v, V_GPRE)) * (1.0 + _vrow(v, V_SCALE)) + _vrow(v, V_SHIFT)).astype(BF16)


def _tail(xv, fv, v, rw):
    return xv + (rw * _vrow(v, V_GATE)) * (fv * _rstd(fv) * _vrow(v, V_GPOST))


def _prenorm_fwd(x, vecs, i, after):
    s = x.shape[0]
    ts = min(s, 512)

    def body(x_ref, v_ref, after_ref, h_ref):
        h_ref[...] = _head(x_ref[...], v_ref[...])

    return _pc(body, "prenorm_fwd", (s // ts,), [_row_spec(ts, D), _vblock(i), pl.BlockSpec(memory_space=pl.ANY)], _row_spec(ts, D),
               _sds((s, D), BF16))(x, vecs, after)


def _post_pre_fwd(x, f, vecs, i, rw_prev, after):
    s = x.shape[0]
    ts = min(s, 512)

    def body(x_ref, f_ref, vp_ref, vc_ref, after_ref, xo_ref, h_ref):
        xv = _tail(x_ref[...], f_ref[...], vp_ref[...], rw_prev)
        xo_ref[...] = xv
        h_ref[...] = _head(xv, vc_ref[...])

    return _pc(body, "post_pre_fwd", (s // ts,),
               [_row_spec(ts, D)] * 2 + [_vblock(i - 1), _vblock(i), pl.BlockSpec(memory_space=pl.ANY)], [_row_spec(ts, D)] * 2,
               [_sds((s, D)), _sds((s, D), BF16)])(x, f, vecs, vecs, after)


def _zero_at_first(first, *refs):
    @pl.when(first)
    def _():
        for ref in refs:
            ref[...] = jnp.zeros_like(ref)


def _acc(ref, first, v):
    @pl.when(first)
    def _():
        ref[...] = v

    @pl.when(jnp.logical_not(first))
    def _():
        ref[...] += v


def _colsum(v):
    return jnp.sum(v, axis=0, keepdims=True)


def _tail_bwd(do, fv, v, rw, dv_ref):
    gv = _vrow(v, V_GPOST)
    r = _rstd(fv)
    fn = fv * r
    dv_ref[V_GATE:V_GATE + 1, :] += rw * _colsum(do * (fn * gv))
    dy = (rw * _vrow(v, V_GATE)) * do
    dv_ref[V_GPOST:V_GPOST + 1, :] += _colsum(dy * fn)
    dfn = dy * gv
    return (r * (dfn - fn * jnp.mean(dfn * fn, axis=-1, keepdims=True))).astype(BF16)


def _head_bwd(do, dhv, xv, v, dv_ref):
    gv = _vrow(v, V_GPRE)
    r = _rstd(xv)
    xn = xv * r
    dv_ref[V_SHIFT:V_SHIFT + 1, :] += _colsum(dhv)
    dv_ref[V_SCALE:V_SCALE + 1, :] += _colsum(dhv * (xn * gv))
    dhp = dhv * (1.0 + _vrow(v, V_SCALE))
    dv_ref[V_GPRE:V_GPRE + 1, :] += _colsum(dhp * xn)
    dxn = dhp * gv
    return do + r * (dxn - xn * jnp.mean(dxn * xn, axis=-1, keepdims=True))


DV_SPEC = pl.BlockSpec((8, D), lambda t: (0, 0))


def _prenorm_bwd(dout, dh, x, vecs, i):
    s = dout.shape[0]
    ts = min(s, 512)

    def body(do_ref, dh_ref, x_ref, v_ref, dx_ref, dv_ref):
        _zero_at_first(pl.program_id(0) == 0, dv_ref)
        dx_ref[...] = _head_bwd(do_ref[...], dh_ref[...], x_ref[...], v_ref[...], dv_ref)

    return _pc(body, "prenorm_bwd", (s // ts,), [_row_spec(ts, D)] * 3 + [_vblock(i)], [_row_spec(ts, D), DV_SPEC],
               [_sds((s, D)), _sds((8, D))])(dout, dh, x, vecs)


def _pre_post_bwd(dout, dh, x, f_prev, vecs, i, rw_prev):
    s = dout.shape[0]
    ts = min(s, 256)

    def body(do_ref, dh_ref, x_ref, f_ref, vc_ref, vp_ref, dx_ref, df_ref, dv_ref):
        _zero_at_first(pl.program_id(0) == 0, dv_ref)
        dx = _head_bwd(do_ref[...], dh_ref[...], x_ref[...], vc_ref[...], dv_ref)
        dx_ref[...] = dx
        df_ref[...] = _tail_bwd(dx, f_ref[...], vp_ref[...], rw_prev, dv_ref)

    rows = _row_spec(ts, D)
    return _pc(body, "pre_post_bwd", (s // ts,), [rows] * 4 + [_vblock(i), _vblock(i - 1)], [rows, rows, DV_SPEC],
               [_sds((s, D)), _sds((s, D), BF16), _sds((8, D))])(dout, dh, x, f_prev, vecs, vecs)


def _last_shell(x, f, tgt, vecs, i, rw):
    s = x.shape[0]
    ts = min(s, 512)
    nt = s // ts

    def body(x_ref, f_ref, t_ref, v_ref, loss_ref, dy_ref, df_ref, dv_ref, acc_ref):
        t = pl.program_id(0)
        _zero_at_first(t == 0, dv_ref, acc_ref)
        fv, v = f_ref[...], v_ref[...]
        e = _tail(x_ref[...], fv, v, rw) - t_ref[...]
        dy = e * (1.0 / D)
        dy_ref[...] = dy
        acc_ref[...] += _colsum(e * e)
        df_ref[...] = _tail_bwd(dy, fv, v, rw, dv_ref)

        @pl.when(t == nt - 1)
        def _():
            loss_ref[...] = jnp.full((1, LANE), 0.5 / D, F32) * jnp.sum(acc_ref[...])

    rows = _row_spec(ts, D)
    return _pc(body, "last_shell", (nt,), [rows] * 3 + [_vblock(i)],
               [pl.BlockSpec((1, LANE), lambda t: (0, 0)), rows, rows, DV_SPEC],
               [_sds((1, LANE)), _sds((s, D)), _sds((s, D), BF16), _sds((8, D))], scratch=[pltpu.VMEM((1, D), F32)])(x, f, tgt, vecs)


def _sigmoid(v):
    return 1.0 / (1.0 + jnp.exp(-v))


def _ffn_in_swiglu(h, win):
    s = h.shape[0]
    ts = _tile(s)
    nt = (((1,), (1,)), ((), ()))

    def body(h_ref, wa_ref, wb_ref, fac_ref, act_ref):
        hv = h_ref[...]
        a = lax.dot_general(hv, wa_ref[...], nt, preferred_element_type=F32)
        b = lax.dot_general(hv, wb_ref[...], nt, preferred_element_type=F32)
        sg = _sigmoid(a)
        silu = a * sg
        fac_ref[0] = (b * (sg * (1.0 + a * (1.0 - sg)))).astype(BF16)
        fac_ref[1] = silu.astype(BF16)
        act_ref[...] = (silu * b).astype(BF16)

    return _pc(body, "ffn_in", (4, s // ts),
               [pl.BlockSpec((ts, D), lambda k, t: (t, 0)), pl.BlockSpec((None, FSH, D), lambda k, t: (k, 0, 0)),
                pl.BlockSpec((None, FSH, D), lambda k, t: (k + 4, 0, 0))],
               [pl.BlockSpec((2, None, ts, FSH), lambda k, t: (0, k, t, 0)), pl.BlockSpec((None, ts, FSH), lambda k, t: (k, t, 0))],
               [_sds((2, 4, s, FSH), BF16), _sds((4, s, FSH), BF16)])(h, win, win)


def _ffn_out_dx_swiglu(df, wout, fac, after):
    s = df.shape[0]
    ts = _tile(s)
    nt = (((1,), (1,)), ((), ()))

    def body(df_ref, w_ref, fac_ref, after_ref, o_ref):
        d = lax.dot_general(df_ref[...], w_ref[...], nt, preferred_element_type=F32)
        o_ref[0] = (d * fac_ref[0]).astype(BF16)
        o_ref[1] = (d * fac_ref[1]).astype(BF16)

    spec = pl.BlockSpec((2, None, ts, FSH), lambda k, t: (0, k, t, 0))
    out = _pc(body, "ffn_out_dx", (4, s // ts),
              [pl.BlockSpec((ts, D), lambda k, t: (t, 0)), pl.BlockSpec((None, FSH, D), lambda k, t: (k, 0, 0)), spec,
               pl.BlockSpec(memory_space=pl.ANY)],
              spec, _sds((2, 4, s, FSH), BF16))(df, wout, fac, after)
    return out.reshape(N_DEV, s, FSH)


def _ffn_fwd(h, win, wout_of):
    s = h.shape[0]
    fac, act = _ffn_in_swiglu(h, win)
    f = _mm_sum("ffn_out", act, wout_of(act).reshape(4, FSH, D), min(s, 512))
    return f, (h, fac, act)


def _ffn_bwd(df, saved, win, wout, send, after):
    h, fac, act = saved
    s = h.shape[0]
    ts = s
    wout = wout.reshape(4, FSH, D)
    dwout = _mm("ffn_out_dw", act, df, ((0,), (0,)), (4, 2),
                pl.BlockSpec((None, s, FSH), lambda k, j: (k, 0, 0)), pl.BlockSpec((s, D // 2), lambda k, j: (0, j)),
                pl.BlockSpec((None, FSH, D // 2), lambda k, j: (k, 0, j)), _sds((4, FSH, D), BF16), after=after)
    dz = _ffn_out_dx_swiglu(df, wout, fac, send("w_out", dwout.reshape(N_DEV, D_FF // N_DEV, D)))
    dwin = _mm("ffn_in_dw", dz, h, ((0,), (0,)), (N_DEV, 2),
               pl.BlockSpec((None, s, FSH), lambda j, i: (j, 0, 0)), pl.BlockSpec((s, D // 2), lambda j, i: (0, i)),
               pl.BlockSpec((None, FSH, D // 2), lambda j, i: (j, 0, i)), _sds((N_DEV, FSH, D), BF16))
    return _mm_sum("ffn_in_dx", dz, win, min(s, 512), after=send("w_in", dwin))


def _shift_rows(v, k, row, s, back):
    if back:
        return jnp.where(row < s - k, pltpu.roll(v, s - k, 0), 0.0)
    return jnp.where(row >= k, pltpu.roll(v, k, 0), 0.0)


def _window_sum(v, w, row, s, back):
    k = 1
    while k < w:
        v = v + _shift_rows(v, k, row, s, back)
        k *= 2
    return v


def _pool_fwd(z, pool_w, pool_scale):
    s = z.shape[0]

    def body(z_ref, w_ref, sc_ref, y_ref, d_ref):
        row = lax.broadcasted_iota(jnp.int32, (s, HD), 0)
        for g, w in enumerate(POOL_WINDOWS):
            sl = slice(g * HD, (g + 1) * HD)
            a = z_ref[:, sl]
            cnt = jnp.minimum(row + 1, w).astype(F32)
            d = (_window_sum(a, w, row, s, False) / cnt - a).astype(BF16)
            d_ref[:, sl] = d
            y = jnp.dot(d, _bf(w_ref[g]), preferred_element_type=F32)
            y_ref[:, sl] = (y * sc_ref[:, sl]).astype(BF16)

    return _pc(body, "pool_fwd", (1,),
               [pl.BlockSpec((s, NH * HD), lambda i: (0, 0)), pl.BlockSpec((NH, HD, HD), lambda i: (0, 0, 0)),
                pl.BlockSpec((1, NH * HD), lambda i: (0, 0))],
               [pl.BlockSpec((s, NH * HD), lambda i: (0, 0))] * 2,
               [_sds((s, NH * HD), BF16)] * 2)(z, pool_w, pool_scale)


def _pool_bwd(dy, d, pool_w, pool_scale):
    s = dy.shape[0]

    def body(dy_ref, d_ref, w_ref, sc_ref, dz_ref, dw_ref, dsc_ref):
        row = lax.broadcasted_iota(jnp.int32, (s, HD), 0)
        for g, w in enumerate(POOL_WINDOWS):
            sl = slice(g * HD, (g + 1) * HD)
            dyg, dg, wg = dy_ref[:, sl], d_ref[:, sl], _bf(w_ref[g])
            yraw = jnp.dot(dg, wg, preferred_element_type=F32)
            dsc_ref[:, sl] = _colsum(dyg * yraw)
            dyr = _bf(dyg * sc_ref[:, sl])
            dw_ref[g] = lax.dot_general(dg, dyr, (((0,), (0,)), ((), ())), preferred_element_type=F32)
            dd = lax.dot_general(dyr, wg, (((1,), (1,)), ((), ())), preferred_element_type=F32)
            cnt = jnp.minimum(row + 1, w).astype(F32)
            dz_ref[:, sl] = (_window_sum(dd / cnt, w, row, s, True) - dd).astype(BF16)

    return _pc(body, "pool_bwd", (1,),
               [pl.BlockSpec((s, NH * HD), lambda i: (0, 0)), pl.BlockSpec((s, NH * HD), lambda i: (0, 0)),
                pl.BlockSpec((NH, HD, HD), lambda i: (0, 0, 0)), pl.BlockSpec((1, NH * HD), lambda i: (0, 0))],
               [pl.BlockSpec((s, NH * HD), lambda i: (0, 0)), pl.BlockSpec((NH, HD, HD), lambda i: (0, 0, 0)),
                pl.BlockSpec((1, NH * HD), lambda i: (0, 0))],
               [_sds((s, NH * HD), BF16), _sds((NH, HD, HD)), _sds((1, NH * HD))])(dy, d, pool_w, pool_scale)


def _gelu(v):
    return 0.5 * v * (1.0 + jnp.tanh(GELU_C * (v + 0.044715 * (v * v * v))))


def _gelu_and_grad(v):
    t = jnp.tanh(GELU_C * (v + 0.044715 * (v * v * v)))
    return 0.5 * v * (1.0 + t), 0.5 * (1.0 + t) + 0.5 * v * (1.0 - t * t) * (GELU_C * (1.0 + 3.0 * 0.044715 * (v * v)))


def _gelu_grad(v):
    return _gelu_and_grad(v)[1]


def _causal_mask():
    return lax.broadcasted_iota(jnp.int32, (HD, HD), 0) >= lax.broadcasted_iota(jnp.int32, (HD, HD), 1)


def _sgu_specs():
    w = NH * HD
    return [pl.BlockSpec((HD, w), lambda c: (c, 1)), pl.BlockSpec((HD, w), lambda c: (c, 2)),
            pl.BlockSpec((1, w), lambda c: (0, 0)), pl.BlockSpec((1, w), lambda c: (0, 0)),
            pl.BlockSpec((NH, HD, HD), lambda c: (0, 0, 0)), pl.BlockSpec((HD, LANE), lambda c: (0, 0))]


def _sgu_head(v, lng_ref, lnb_ref, w_ref, h):
    sl = slice(h * HD, (h + 1) * HD)
    vh = v[:, sl]
    xc = vh - jnp.mean(vh, axis=-1, keepdims=True)
    rs = lax.rsqrt(jnp.mean(xc * xc, axis=-1, keepdims=True) + EPS)
    vhat = xc * rs
    vn = _bf(vhat * lng_ref[:, sl] + lnb_ref[:, sl])
    wc = _bf(jnp.where(_causal_mask(), w_ref[h], 0.0))
    return sl, rs, vhat, vn, wc


def _sgu_fwd(z, ln_g, ln_b, sgu_w, sgu_bt):
    s = z.shape[0]

    def body(zu_ref, zv_ref, lng_ref, lnb_ref, w_ref, bt_ref, y_ref):
        u, v = _gelu(zu_ref[...]), _gelu(zv_ref[...])
        for h in range(NH):
            sl, _, _, vn, wc = _sgu_head(v, lng_ref, lnb_ref, w_ref, h)
            sp = jnp.dot(wc, vn, preferred_element_type=F32) + bt_ref[:, h:h + 1]
            y_ref[:, sl] = (u[:, sl] * sp).astype(BF16)

    return _pc(body, "sgu_fwd", (s // HD,), _sgu_specs(), pl.BlockSpec((HD, NH * HD), lambda c: (c, 0)),
               _sds((s, NH * HD), BF16))(z, z, ln_g, ln_b, sgu_w, sgu_bt)


def _sgu_bwd(z, dy, ln_g, ln_b, sgu_w, sgu_bt, head_sum):
    s = z.shape[0]
    w = NH * HD
    nc = s // HD

    def body(zu_ref, zv_ref, lng_ref, lnb_ref, w_ref, bt_ref, dy_ref, hs_ref,
             dzu_ref, dzv_ref, dlng_ref, dlnb_ref, dw_ref, dbt_ref, dsacc_ref):
        c = pl.program_id(0)
        _zero_at_first(c == 0, dsacc_ref, dw_ref, dlng_ref, dlnb_ref)
        zu, zv = zu_ref[...], zv_ref[...]
        (u, gu), (v, gv) = _gelu_and_grad(zu), _gelu_and_grad(zv)
        dyv = dy_ref[...]
        ds = dyv * u
        dsacc_ref[...] += ds
        for h in range(NH):
            sl, rs, vhat, vn, wc = _sgu_head(v, lng_ref, lnb_ref, w_ref, h)
            sp = jnp.dot(wc, vn, preferred_element_type=F32) + bt_ref[:, h:h + 1]
            dzu_ref[:, sl] = (dyv[:, sl] * sp * gu[:, sl]).astype(BF16)
            dsh = _bf(ds[:, sl])
            dwh = lax.dot_general(dsh, vn, (((1,), (1,)), ((), ())), preferred_element_type=F32)
            dw_ref[h] += jnp.where(_causal_mask(), dwh, 0.0)
            dvn = lax.dot_general(wc, dsh, (((0,), (0,)), ((), ())), preferred_element_type=F32)
            dlng_ref[:, sl] += _colsum(dvn * vhat)
            dlnb_ref[:, sl] += _colsum(dvn)
            dvh = dvn * lng_ref[:, sl]
            dv = rs * (dvh - jnp.mean(dvh, axis=-1, keepdims=True) - vhat * jnp.mean(dvh * vhat, axis=-1, keepdims=True))
            dzv_ref[:, sl] = (dv * gv[:, sl]).astype(BF16)

        @pl.when(c == nc - 1)
        def _():
            dbt_ref[...] = jnp.dot(dsacc_ref[...], hs_ref[...], preferred_element_type=F32, precision=HIGHEST)

    outs = _pc(body, "sgu_bwd", (nc,),
               _sgu_specs() + [pl.BlockSpec((HD, w), lambda c: (c, 1)), pl.BlockSpec((w, LANE), lambda c: (0, 0))],
               [pl.BlockSpec((HD, w), lambda c: (c, 0))] * 2 + [pl.BlockSpec((1, w), lambda c: (0, 0))] * 2
               + [pl.BlockSpec((NH, HD, HD), lambda c: (0, 0, 0)), pl.BlockSpec((HD, LANE), lambda c: (0, 0))],
               [_sds((s, w), BF16)] * 2 + [_sds((1, w))] * 2 + [_sds((NH, HD, HD)), _sds((HD, LANE))],
               scratch=[pltpu.VMEM((HD, w), F32)])(z, z, ln_g, ln_b, sgu_w, sgu_bt, dy, head_sum)
    return outs


def _cmul(ar, ai, br, bi):
    return ar * br - ai * bi, ar * bi + ai * br


def _ssm_prep(lam_re, lam_im, lam_re_rep, lam_im_rep, log_dt, b_re, b_im):
    def disc(lr, li, dt):
        mag = jnp.exp(lr * dt)
        return mag * jnp.cos(li * dt), mag * jnp.sin(li * dt)

    def body(lr_ref, li_ref, lrr_ref, lir_ref, ldt_ref, br_ref, bi_ref, or_ref, oi_ref, bbr_ref, bbi_ref):
        dt = jnp.exp(ldt_ref[...])
        or_ref[...], oi_ref[...] = disc(lr_ref[...], li_ref[...], dt)
        lr, li = lrr_ref[...], lir_ref[...]
        er, ei = disc(lr, li, dt)
        den = lr * lr + li * li
        kr = ((er - 1.0) * lr + ei * li) / den
        ki = (ei * lr - (er - 1.0) * li) / den
        bbr_ref[...], bbi_ref[...] = _cmul(kr, ki, br_ref[...], bi_ref[...])

    small = pl.BlockSpec((SSM_G, SSM_P), lambda i: (0, 0))
    wide = pl.BlockSpec((SSM_G, SSM_P * SSM_N), lambda i: (0, 0))
    col = pl.BlockSpec((SSM_G, 1), lambda i: (0, 0))
    return _pc(body, "ssm_prep", (1,), [small, small, wide, wide, col, wide, wide], [small, small, wide, wide],
               [_sds((SSM_G, SSM_P))] * 2 + [_sds((SSM_G, SSM_P * SSM_N))] * 2)(
        lam_re, lam_im, lam_re_rep, lam_im_rep, log_dt, b_re, b_im)


def _ssm_param_bwd(g_lam_re, g_lam_im, g_bb_re, g_bb_im, lam_re, lam_im, lam_re_rep, lam_im_rep, log_dt, b_re, b_im, seg):
    def body(glr_ref, gli_ref, gbr_ref, gbi_ref, lr_ref, li_ref, lrr_ref, lir_ref, ldt_ref, br_ref, bi_ref, seg_ref,
             dlr_ref, dli_ref, ddt_ref, dbr_ref, dbi_ref):
        dt = jnp.exp(ldt_ref[...])
        lr, li = lrr_ref[...], lir_ref[...]
        mag = jnp.exp(lr * dt)
        er, ei = mag * jnp.cos(li * dt), mag * jnp.sin(li * dt)
        den = lr * lr + li * li
        kr = ((er - 1.0) * lr + ei * li) / den
        ki = (ei * lr - (er - 1.0) * li) / den
        gbr, gbi = gbr_ref[...], gbi_ref[...]
        dbr_ref[...], dbi_ref[...] = _cmul(kr, -ki, gbr, gbi)
        tr, ti = _cmul(br_ref[...], -bi_ref[...], gbr, gbi)
        gkr = jnp.dot(tr, seg_ref[...], preferred_element_type=F32, precision=HIGHEST)
        gki = jnp.dot(ti, seg_ref[...], preferred_element_type=F32, precision=HIGHEST)
        lr, li = lr_ref[...], li_ref[...]
        mag = jnp.exp(lr * dt)
        er, ei = mag * jnp.cos(li * dt), mag * jnp.sin(li * dt)
        den = lr * lr + li * li
        ir, ii = lr / den, -li / den
        kr, ki = _cmul(er - 1.0, ei, ir, ii)
        ar, ai = _cmul(ir, -ii, gkr, gki)
        glr, gli = glr_ref[...] + ar, gli_ref[...] + ai
        qr, qi = _cmul(kr, ki, ir, ii)
        g1r, g1i = _cmul(-qr, qi, gkr, gki)
        g2r, g2i = _cmul(dt * er, -dt * ei, glr, gli)
        dlr_ref[...] = g1r + g2r
        dli_ref[...] = g1i + g2i
        wr, wi = _cmul(lr, li, er, ei)
        g_dt = jnp.sum(wr * glr + wi * gli, axis=-1, keepdims=True)
        ddt_ref[...] = jnp.broadcast_to(dt * g_dt, (SSM_G, LANE))

    small = pl.BlockSpec((SSM_G, SSM_P), lambda i: (0, 0))
    wide = pl.BlockSpec((SSM_G, SSM_P * SSM_N), lambda i: (0, 0))
    col = pl.BlockSpec((SSM_G, 1), lambda i: (0, 0))
    segs = pl.BlockSpec((SSM_P * SSM_N, SSM_P), lambda i: (0, 0))
    return _pc(body, "ssm_param_bwd", (1,), [small, small, wide, wide, small, small, wide, wide, col, wide, wide, segs],
               [small, small, pl.BlockSpec((SSM_G, LANE), lambda i: (0, 0)), wide, wide],
               [_sds((SSM_G, SSM_P))] * 2 + [_sds((SSM_G, LANE))] + [_sds((SSM_G, SSM_P * SSM_N))] * 2)(
        g_lam_re, g_lam_im, g_bb_re, g_bb_im, lam_re, lam_im, lam_re_rep, lam_im_rep, log_dt, b_re, b_im, seg)


SCAN_LANES = 512
SCAN_ROWS = 8


SCAN_GROUPS = SCAN_LANES // SSM_P
SCAN_COLS = SCAN_GROUPS * SSM_N
SCAN_CHUNK = 1024


def _ssm_scan(name, v, w_in, lam_re, lam_im, w_out, reverse, states=None, u=None):
    s = v.shape[0]
    ln, rows, ch = SCAN_LANES, SCAN_ROWS, min(SCAN_CHUNK, s)
    nch, ntile = s // ch, ch // rows
    nt_dims = (((1,), (1,)), ((), ()))
    with_sum = states is not None
    tn_dims = (((0,), (0,)), ((), ()))

    def body(*refs):
        v_ref, win_ref, lr_ref, li_ref, wout_ref = refs[:5]
        if with_sum:
            n_in = 8
            y_ref, sum_refs = refs[n_in], refs[n_in + 1:n_in + 7]
            br_s, bi_s, mb_s, mc_s, or_ref, oi_ref = refs[n_in + 7:]
            mb_s[...] = jnp.zeros_like(mb_s)
            mc_s[...] = jnp.zeros_like(mc_s)
        else:
            n_in = 5
            or_ref, oi_ref, y_ref, br_s, bi_s = refs[n_in:]
        l1 = (lr_ref[...], li_ref[...])
        pw = [l1]
        for _ in range(rows - 1):
            pw.append(_cmul(*pw[-1], *l1))
        row = lax.broadcasted_iota(jnp.int32, (rows, ln), 0)
        expo = (rows - row) if reverse else (row + 1)
        pr = jnp.zeros((rows, ln), F32)
        pi = jnp.zeros((rows, ln), F32)
        for e in range(1, rows + 1):
            pr = jnp.where(expo == e, pw[e - 1][0], pr)
            pi = jnp.where(expo == e, pw[e - 1][1], pi)
        lk = {}
        for k in (1, 2, 4):
            keep = (row < rows - k) if reverse else (row >= k)
            lk[k] = (jnp.where(keep, pw[k - 1][0], 0.0), jnp.where(keep, pw[k - 1][1], 0.0))

        def chunk(c, carry):
            q0 = pl.multiple_of(((nch - 1 - c) if reverse else c) * ch, ch)
            b = jnp.dot(_bf(v_ref[pl.ds(q0, ch), :]), win_ref[...], preferred_element_type=F32)
            br_s[...] = b[:, :ln]
            bi_s[...] = b[:, ln:]

            def step(i, carry):
                cr, ci = carry[:2]
                r0 = pl.multiple_of(((ntile - 1 - i) if reverse else i) * rows, rows)
                xr, xi = br_s[pl.ds(r0, rows), :], bi_s[pl.ds(r0, rows), :]
                for k in (1, 2, 4):
                    shift = rows - k if reverse else k
                    ar, ai = _cmul(lk[k][0], lk[k][1], pltpu.roll(xr, shift, 0), pltpu.roll(xi, shift, 0))
                    xr, xi = xr + ar, xi + ai
                ar, ai = _cmul(pr, pi, cr, ci)
                xr, xi = xr + ar, xi + ai
                g0 = pl.multiple_of(q0 + r0, rows)
                or_ref[pl.ds(g0, rows), :] = xr
                oi_ref[pl.ds(g0, rows), :] = xi
                if not with_sum:
                    return (xr[rows - 1:rows], xi[rows - 1:rows]) if not reverse else (xr[0:1], xi[0:1])
                nr = jnp.where(row == rows - 1, cr, pltpu.roll(xr, rows - 1, 0))
                ni = jnp.where(row == rows - 1, ci, pltpu.roll(xi, rows - 1, 0))
                sr, si = refs[5][pl.ds(g0, rows), :], refs[6][pl.ds(g0, rows), :]
                return xr[0:1], xi[0:1], carry[2] + (sr * nr + si * ni), carry[3] + (sr * ni - si * nr)

            carry = lax.fori_loop(0, ntile, step, carry)
            if with_sum:
                rows_c = pl.ds(q0, ch)
                uc, vc = _bf(refs[7][rows_c, :]), _bf(v_ref[rows_c, :])
                for scr, left, (right_re, right_im) in ((mb_s, uc, (or_ref, oi_ref)), (mc_s, vc, (refs[5], refs[6]))):
                    scr[:, :ln] += lax.dot_general(left, _bf(right_re[rows_c, :]), tn_dims, preferred_element_type=F32)
                    scr[:, ln:] += lax.dot_general(left, _bf(right_im[rows_c, :]), tn_dims, preferred_element_type=F32)
            w = wout_ref[...]
            y_ref[pl.ds(q0, ch), :] = (
                lax.dot_general(_bf(or_ref[pl.ds(q0, ch), :]), w[:, :ln], nt_dims, preferred_element_type=F32)
                + lax.dot_general(_bf(oi_ref[pl.ds(q0, ch), :]), w[:, ln:], nt_dims, preferred_element_type=F32))
            return carry

        zero = jnp.zeros((1, ln), F32)
        init = (zero, zero) + ((jnp.zeros((rows, ln), F32),) * 2 if with_sum else ())
        carry = lax.fori_loop(0, nch, chunk, init)
        if with_sum:
            sum_refs[0][...] = _colsum(carry[2])
            sum_refs[1][...] = _colsum(carry[3])
            row_g = lax.broadcasted_iota(jnp.int32, (SCAN_COLS, LANE), 0) // SSM_N
            lane_g = lax.broadcasted_iota(jnp.int32, (SCAN_COLS, LANE), 1) // SSM_P
            for scr, o_re, o_im in ((mb_s, sum_refs[2], sum_refs[3]), (mc_s, sum_refs[4], sum_refs[5])):
                for part, o_ref in enumerate((o_re, o_im)):
                    fold = jnp.zeros((SCAN_COLS, LANE), F32)
                    for cb in range(ln // LANE):
                        fold = fold + jnp.where(2 * cb + lane_g == row_g, scr[:, part * ln + cb * LANE:part * ln + (cb + 1) * LANE], 0.0)
                    o_ref[...] = jnp.where(row_g % 2 == 0, fold, pltpu.roll(fold, SSM_P, 1))

    vec = pl.BlockSpec((1, ln), lambda j: (0, j))
    blk = pl.BlockSpec((s, ln), lambda j: (0, j))
    cols = pl.BlockSpec((s, SCAN_COLS), lambda j: (0, j))
    wspec = pl.BlockSpec((None, SCAN_COLS, 2 * ln), lambda j: (j, 0, 0))
    ins, args = [cols, wspec, vec, vec, wspec], [v, w_in, lam_re, lam_im, w_out]
    outs, shapes = [blk, blk, cols], [_sds((s, SSM_L))] * 2 + [_sds((s, SSM_G * SSM_N))]
    scratch = [pltpu.VMEM((ch, ln), F32)] * 2
    if with_sum:
        own = pl.BlockSpec((None, SCAN_COLS, LANE), lambda j: (j, 0, 0))
        ins, args = ins + [blk, blk, cols], args + list(states) + [u]
        outs = [cols, vec, vec] + [own] * 4
        shapes = [_sds((s, SSM_G * SSM_N))] + [_sds((1, SSM_L))] * 2 + [_sds((SSM_L // ln, SCAN_COLS, LANE))] * 4
        scratch = scratch + [pltpu.VMEM((SCAN_COLS, 2 * ln), F32)] * 2 + [pltpu.VMEM((s, ln), F32)] * 2
    return _pc(body, name, (SSM_L // ln,), ins, outs, shapes, scratch=scratch)(*args)


def _ssm_act_fwd(y, u, d_skip):
    s = y.shape[0]
    ts = min(s, 512)

    def body(y_ref, u_ref, d_ref, o_ref):
        o_ref[...] = _gelu(y_ref[...] + d_ref[...] * u_ref[...]).astype(BF16)

    return _pc(body, "ssm_act_fwd", (s // ts,), [_row_spec(ts, D)] * 2 + [_vec_spec(D)], _row_spec(ts, D),
               _sds((s, D), BF16))(y, u, d_skip)


def _ssm_act_bwd(dg, y, u, d_skip):
    s = y.shape[0]
    ts = min(s, 512)

    def body(dg_ref, y_ref, u_ref, d_ref, dy_ref, dd_ref):
        uv = u_ref[...]
        dy = dg_ref[...] * _gelu_grad(y_ref[...] + d_ref[...] * uv)
        dy_ref[...] = dy.astype(BF16)
        _acc(dd_ref, pl.program_id(0) == 0, _colsum(dy * uv))

    return _pc(body, "ssm_act_bwd", (s // ts,), [_row_spec(ts, D)] * 3 + [_vec_spec(D)], [_row_spec(ts, D), _vec_spec(D)],
               [_sds((s, D), BF16), _sds((1, D))])(dg, y, u, d_skip)


def _axpy(a, b, d_skip):
    s = a.shape[0]
    ts = min(s, 512)

    def body(a_ref, b_ref, d_ref, o_ref):
        o_ref[...] = (a_ref[...] + d_ref[...] * b_ref[...].astype(F32)).astype(BF16)

    return _pc(body, "ssm_du", (s // ts,), [_row_spec(ts, D)] * 2 + [_vec_spec(D)], _row_spec(ts, D),
               _sds((s, D), BF16))(a, b, d_skip)


def _glu_fwd(zz):
    s = zz.shape[0]
    ts = min(s, 512)

    def body(a_ref, b_ref, o_ref):
        o_ref[...] = a_ref[...] * _sigmoid(b_ref[...])

    return _pc(body, "glu_fwd", (s // ts,), [_row_spec(ts, D, 0), _row_spec(ts, D, 1)], _row_spec(ts, D), _sds((s, D)))(zz, zz)


def _glu_bwd(zz, df):
    s = zz.shape[0]
    ts = min(s, 512)

    def body(a_ref, b_ref, df_ref, o_ref):
        sg = _sigmoid(b_ref[...])
        dfv = df_ref[...].astype(F32)
        o_ref[:, :D] = (dfv * sg).astype(BF16)
        o_ref[:, D:] = (dfv * a_ref[...] * sg * (1.0 - sg)).astype(BF16)

    return _pc(body, "glu_bwd", (s // ts,), [_row_spec(ts, D, 0), _row_spec(ts, D, 1), _row_spec(ts, D)],
               _row_spec(ts, 2 * D), _sds((s, 2 * D), BF16))(zz, zz, df)


def _ssm_block_diag(m_re, m_im):
    rows, half = SCAN_COLS, SCAN_LANES
    expand = jnp.tile(jnp.eye(SSM_P, dtype=BF16), (1, SCAN_GROUPS))

    def body(mr_ref, mi_ref, e_ref, o_ref):
        keep = (lax.broadcasted_iota(jnp.int32, (rows, half), 0) // SSM_N
                == lax.broadcasted_iota(jnp.int32, (rows, half), 1) // SSM_P)
        for part, m_ref in enumerate((mr_ref, mi_ref)):
            t = jnp.dot(_bf(m_ref[...]), e_ref[...], preferred_element_type=F32)
            o_ref[:, part * half:(part + 1) * half] = jnp.where(keep, t, 0.0).astype(BF16)

    blk = pl.BlockSpec((rows, SSM_P), lambda q: (q, 0))
    nb = SSM_G // SCAN_GROUPS
    return _pc(body, "ssm_block_diag", (nb,), [blk, blk, pl.BlockSpec((SSM_P, half), lambda q: (0, 0))],
               pl.BlockSpec((None, rows, 2 * half), lambda q: (q, 0, 0)), _sds((nb, rows, 2 * half), BF16))(m_re, m_im, expand)


def _mod_part(c_all, ada_w):
    n = ada_w.shape[-1]

    def body(c_ref, w_ref, o_ref):
        cv = c_ref[...]
        cond = _bf(cv * _sigmoid(cv))
        o_ref[...] = jnp.dot(cond, _bf(w_ref[...]), preferred_element_type=F32)

    return _pc(body, "mod_part", (2,), [pl.BlockSpec((N_DEV, D), lambda l: (0, 0)), pl.BlockSpec((None, D, n), lambda l: (l, 0, 0))],
               pl.BlockSpec((None, N_DEV, n), lambda l: (l, 0, 0)), _sds((2, N_DEV, n)))(c_all, ada_w)


def _ada_w_grad(c_all_t, dmod):
    nl, _, n = dmod.shape
    tr = 128

    def body(c_ref, d_ref, o_ref):
        cv = c_ref[...]
        cond = _bf(cv * _sigmoid(cv)).astype(F32)
        dm = _bf(d_ref[...]).astype(F32)
        acc = cond[:, 0:1] * dm[0:1, :]
        for b in range(1, N_DEV):
            acc = acc + cond[:, b:b + 1] * dm[b:b + 1, :]
        o_ref[...] = acc

    return _pc(body, "ada_w_grad", (nl, D // tr),
               [pl.BlockSpec((tr, N_DEV), lambda l, t: (t, 0)), pl.BlockSpec((None, N_DEV, n), lambda l, t: (l, 0, 0))],
               pl.BlockSpec((None, tr, n), lambda l, t: (l, t, 0)), _sds((nl, D, n)))(c_all_t, dmod)


def _adamw(name, parts, w, m, v, slot=0, prev=None, after=None):
    p, r, c = parts.shape
    tr = r
    while tr * c * 4 > (1 << 21) and tr % 16 == 0:
        tr //= 2
    nt = r // tr

    def body(p_ref, w_ref, m_ref, v_ref, *rest):
        g_ref, d_ref, nm_ref, nv_ref = rest[-4:]
        g = p_ref[0].astype(F32)
        for i in range(1, p):
            g = g + p_ref[i].astype(F32)
        g_ref[...] = g
        d_ref[...], nm_ref[...], nv_ref[...] = _adam_update(g, w_ref[...], m_ref[...], v_ref[...])

    blk = pl.BlockSpec((tr, c), lambda t: (slot * nt + t, 0))
    in_specs = [pl.BlockSpec((p, tr, c), lambda t: (0, t, 0)), blk, blk, blk]
    unread = list(prev or []) + ([after] if after is not None else [])
    return pl.pallas_call(
        body, name=name, grid=(nt,), in_specs=in_specs + [pl.BlockSpec(memory_space=pl.ANY)] * len(unread), out_specs=[blk] * 4,
        out_shape=[_sds(w.shape)] * 4, input_output_aliases={4 + i: i for i in range(4)} if prev else {},
        compiler_params=pltpu.CompilerParams(dimension_semantics=("arbitrary",), vmem_limit_bytes=VMEM_LIMIT_BYTES))(parts, w, m, v, *unread)


def _adam_update(g, w, m, v):
    m2 = B1 * m + (1.0 - B1) * g
    v2 = B2 * v + (1.0 - B2) * (g * g)
    m_hat = m2 / (1.0 - B1 ** STEP)
    v_hat = v2 / (1.0 - B2 ** STEP)
    return -LR * (m_hat / (jnp.sqrt(v_hat) + ADAM_EPS) + WD * w), m2, v2


def _adamw_many(name, items, after):
    n = len(items)

    def body(*refs):
        outs = refs[4 * n + 1:]
        for i in range(n):
            g, w, m, v = (r[...] for r in refs[4 * i:4 * i + 4])
            for o, val in zip(outs[3 * i:3 * i + 3], _adam_update(g, w, m, v)):
                o[...] = val

    full = lambda a: pl.BlockSpec(a.shape, lambda t: (0, 0))
    flat = [a for item in items for a in item]
    res = _pc(body, name, (1,), [full(a) for a in flat] + [pl.BlockSpec(memory_space=pl.ANY)],
              [full(item[1]) for item in items for _ in range(3)],
              [_sds(item[1].shape) for item in items for _ in range(3)])(*flat, after)
    return [tuple(res[3 * i:3 * i + 3]) for i in range(n)]


def _sum_parts(parts):
    p, r, c = parts.shape
    tr = r
    while tr * c * 4 > (1 << 19) and tr % 16 == 0:
        tr //= 2

    def body(p_ref, o_ref):
        g = p_ref[0]
        for i in range(1, p):
            g = g + p_ref[i]
        o_ref[...] = g

    return _pc(body, "sum_parts", (r // tr,), [pl.BlockSpec((p, tr, c), lambda t: (0, t, 0))], pl.BlockSpec((tr, c), lambda t: (t, 0)),
               _sds((r, c)))(parts)


def _place():
    x, y, c = lax.axis_index("x"), lax.axis_index("y"), lax.axis_index("c")
    peers = []
    for k in range(1, N_DEV):
        px = (1 - x) if k & 4 else x
        py = (1 - y) if k & 2 else y
        pc = (1 - c) if k & 1 else c
        peers.append(((px, py, pc), 4 * px + 2 * py + pc))
    return 4 * x + 2 * y + c, peers


def _at(ref, idx):
    return ref if idx is None else ref.at[idx]


def _exchange_copies(plan, n, src_refs, dst_refs, send_sems, recv_sems, local_sems=None, with_arrivals=True):
    me, peers = _place()
    local = [] if local_sems is None else [
        pltpu.make_async_copy(_at(src_refs[si], sx), _at(dst_refs[di], dx), local_sems.at[i])
        for i, (si, sx, di, dx) in enumerate(plan(me, me, 0))]

    def remote(k, i, dev, entry):
        si, sx, di, dx = entry
        return pltpu.make_async_remote_copy(_at(src_refs[si], sx), _at(dst_refs[di], dx), send_sems.at[k * n + i], recv_sems.at[k * n + i],
                                            device_id=dev, device_id_type=MESH)

    sends = [remote(k, i, dev, e) for k, (dev, peer) in enumerate(peers) for i, e in enumerate(plan(me, peer, k + 1))]
    if not with_arrivals:
        return local, sends, []
    arrivals = [remote(k, i, dev, e) for k, (dev, peer) in enumerate(peers) for i, e in enumerate(plan(peer, me, k + 1))]
    return local, sends, arrivals


def _sem_shapes(n_copies, local=True):
    sems = [pltpu.SemaphoreType.DMA(((N_DEV - 1) * n_copies,)), pltpu.SemaphoreType.DMA(((N_DEV - 1) * n_copies,))]
    return sems + [pltpu.SemaphoreType.DMA((n_copies,))] if local else sems


def _exchange(name, srcs, dst_shapes, plan, n_copies):
    ns, nd = len(srcs), len(dst_shapes)

    def body(*refs):
        local, sends, arrivals = _exchange_copies(plan, n_copies, refs[:ns], refs[ns:ns + nd], *refs[ns + nd:])
        for cp in local + sends:
            cp.start()
        for cp in arrivals:
            cp.wait_recv()
        for cp in sends:
            cp.wait_send()
        for cp in local:
            cp.wait()

    any_spec = pl.BlockSpec(memory_space=pl.ANY)
    return pl.pallas_call(
        body, name=name, in_specs=[any_spec] * ns, out_specs=[any_spec] * nd, out_shape=list(dst_shapes),
        scratch_shapes=_sem_shapes(n_copies))(*srcs)


HBM_SPEC = pl.BlockSpec(memory_space=pltpu.HBM)
SEM_SPEC = pl.BlockSpec(memory_space=pltpu.SEMAPHORE)
ANY_SPEC = pl.BlockSpec(memory_space=pl.ANY)
TOKEN_SPEC = pl.BlockSpec(memory_space=pltpu.VMEM)
SIDE_EFFECT = pltpu.SideEffectType.DATAFLOW_SIDE_EFFECTING


def _wait_all(local, sends, arrivals):
    for cp in arrivals:
        cp.wait_recv()
    for cp in sends:
        cp.wait_send()
    for cp in local:
        cp.wait()


def _exchange_start(name, srcs, dst_shapes, plan, n_copies, order):
    ns, nd = len(srcs), len(dst_shapes)
    nb = ns + nd

    def body(*refs):
        local, sends, _ = _exchange_copies(plan, n_copies, refs[:ns], refs[ns:nb], *refs[nb + 1:nb + 4], with_arrivals=False)
        for cp in local + sends:
            cp.start()
        refs[-1][...] = jnp.zeros((8, LANE), F32)

    lands = [pltpu.with_memory_space_constraint(lax.empty(d.shape, d.dtype), pltpu.HBM) for d in dst_shapes]
    srcs = [pltpu.with_memory_space_constraint(a, pltpu.HBM) for a in srcs]
    bufs = srcs + lands
    out = pl.pallas_call(
        body, name=name, in_specs=[HBM_SPEC] * nb + [ANY_SPEC],
        out_specs=[SEM_SPEC] * 3 + [HBM_SPEC] * nb + [TOKEN_SPEC],
        out_shape=_sem_shapes(n_copies) + [pltpu.HBM(a.shape, a.dtype) for a in bufs] + [_sds((8, LANE))],
        input_output_aliases={i: 3 + i for i in range(nb)},
        compiler_params=pltpu.CompilerParams(has_side_effects=SIDE_EFFECT))(*bufs, order)
    return out[:3], out[3:3 + ns], out[3 + ns:3 + nb], out[-1]


def _exchange_relay(name, sems, srcs, lands, plan, n_copies, plan2, n_copies2, after):
    ns, nd = len(srcs), len(lands)
    nb = ns + nd

    def body(*refs):
        land_refs = refs[ns:nb]
        _wait_all(*_exchange_copies(plan, n_copies, refs[:ns], land_refs, *refs[nb:nb + 3]))
        _, sends, _ = _exchange_copies(plan2, n_copies2, land_refs, land_refs, *refs[nb + 4:nb + 6], with_arrivals=False)
        for cp in sends:
            cp.start()
        refs[-1][...] = jnp.zeros((8, LANE), F32)

    out = pl.pallas_call(
        body, name=name, in_specs=[HBM_SPEC] * nb + [SEM_SPEC] * 3 + [ANY_SPEC],
        out_specs=[SEM_SPEC] * 2 + [HBM_SPEC] * nd + [TOKEN_SPEC],
        out_shape=_sem_shapes(n_copies2, local=False) + [pltpu.HBM(a.shape, a.dtype) for a in lands] + [_sds((8, LANE))],
        input_output_aliases={ns + i: 2 + i for i in range(nd)},
        compiler_params=pltpu.CompilerParams(has_side_effects=SIDE_EFFECT))(*srcs, *lands, *sems, after)
    return out[:2], out[2:2 + nd], out[-1]


def _exchange_wait(name, sems, srcs, lands, plan, n_copies, after):
    srcs = [] if srcs is None else list(srcs)
    ns, nd = len(srcs), len(lands)
    nb = ns + nd

    def body(*refs):
        land_refs = refs[ns:nb]
        _wait_all(*_exchange_copies(plan, n_copies, refs[:ns] if ns else land_refs, land_refs, *refs[nb:nb + len(sems)]))

    bufs = srcs + list(lands)
    out = pl.pallas_call(
        body, name=name, in_specs=[HBM_SPEC] * nb + [SEM_SPEC] * len(sems) + [ANY_SPEC],
        out_specs=[HBM_SPEC] * nb, out_shape=[pltpu.HBM(a.shape, a.dtype) for a in bufs],
        input_output_aliases={i: i for i in range(nb)},
        compiler_params=pltpu.CompilerParams(has_side_effects=SIDE_EFFECT))(*bufs, *sems, after)
    return out[ns:]


def _all_gather(name, arrs):
    plan = lambda me, peer, k: [(i, None, i, me) for i in range(len(arrs))]
    return _exchange(name, arrs, [_sds((N_DEV,) + a.shape, a.dtype) for a in arrs], plan, len(arrs))


def _mix0_fwd(h, p):
    z = _mm_nt("mix0_in", h, p["ab_w_in"])
    y_a, d = _pool_fwd(z, p["pool_w"], p["pool_scale"])
    y_b = _sgu_fwd(z, p["sgu_ln_g"], p["sgu_ln_b"], p["sgu_w"], p["sgu_bt"])
    ycat = jnp.concatenate([y_a, y_b], axis=1)
    return _mm_nn("mix0_out", ycat, p["ab_w_out"]), (h, z, d, ycat)


def _mix0_bwd(df, saved, p, after):
    h, z, d, ycat = saved
    dycat = _mm_nt("mix0_out_dx", df, p["ab_w_out"], after=after)
    g = {"ab_w_out": _mm_tn("mix0_out_dw", ycat, df, BF16)}
    dz_p, g["pool_w"], g["pool_scale"] = _pool_bwd(dycat, d, p["pool_w"], p["pool_scale"])
    dz_u, dz_v, g["sgu_ln_g"], g["sgu_ln_b"], g["sgu_w"], dbt = _sgu_bwd(
        z, dycat, p["sgu_ln_g"], p["sgu_ln_b"], p["sgu_w"], p["sgu_bt"], p["head_sum"])
    g["sgu_b"] = dbt[:, :NH].T
    dz = jnp.concatenate([dz_p, dz_u, dz_v], axis=1)
    g["ab_w_in"] = _mm_tn("mix0_in_dw", dz, h, BF16)
    return _mm_nn("mix0_in_dx", dz, p["ab_w_in"]), g


def _mix1_fwd(h, p):
    u = _mm_nn("ssm_w_in", h, p["ssm_w_in"])
    x_re, x_im, y = _ssm_scan("ssm_scan_fwd", u, p["wb_bd"], p["lam_bar_re"], p["lam_bar_im"], p["wc_bd"], False)
    g = _ssm_act_fwd(y, u, p["ssm_d"])
    zz = _mm_nn("ssm_glu", g, p["ssm_w_glu"])
    return _glu_fwd(zz), (h, u, x_re, x_im, y, g, zz)


def _mix1_bwd(df, saved, p, after):
    h, u, x_re, x_im, y, g, zz = saved
    gr = {}
    dzz = _glu_bwd(zz, df)
    dg = _mm_nt("ssm_glu_dx", dzz, p["ssm_w_glu"], after=after)
    gr["ssm_w_glu"] = _mm_tn("ssm_glu_dw", g, dzz, BF16)
    dy, gr["ssm_d"] = _ssm_act_bwd(dg, y, u, p["ssm_d"])
    du_ssm, g_lam_re, g_lam_im, mb_re, mb_im, mc_re, mc_im = _ssm_scan(
        "ssm_scan_bwd", dy, p["wc_bd"], p["lam_bar_re"], -p["lam_bar_im"], p["wb_bd"], True, states=(x_re, x_im), u=u)
    du = _axpy(du_ssm, dy, p["ssm_d"])
    gr["ssm_w_in"] = _mm_tn("ssm_w_in_dw", h, du, BF16)
    dh = _mm_nt("ssm_w_in_dx", du, p["ssm_w_in"])
    per_group = lambda m: m[:, :, :SSM_P].reshape(SSM_G, SSM_N, SSM_P)
    gr["ssm_c_re"] = per_group(mc_re)
    gr["ssm_c_im"] = -per_group(mc_im)
    dlr, dli, ddt, dbr, dbi = _ssm_param_bwd(
        g_lam_re.reshape(SSM_G, SSM_P), g_lam_im.reshape(SSM_G, SSM_P),
        per_group(mb_re).reshape(SSM_G, SSM_N * SSM_P), per_group(mb_im).reshape(SSM_G, SSM_N * SSM_P),
        p["lam_re"], p["lam_im"], p["lam_re_rep"], p["lam_im_rep"], p["log_dt"], p["b_re"], p["b_im"], p["seg"])
    gr["ssm_lam_re"], gr["ssm_lam_im"], gr["ssm_log_dt"] = dlr, dli, ddt[:, 0]
    gr["ssm_b_re"] = dbr.reshape(SSM_G, SSM_N, SSM_P)
    gr["ssm_b_im"] = dbi.reshape(SSM_G, SSM_N, SSM_P)
    return dh, gr


def _ssm_params(lam_re, lam_im, b_re, b_im, c_re, c_im, log_dt):
    wide = lambda b: b.transpose(0, 2, 1).reshape(SSM_G, SSM_N * SSM_P)
    p = {"lam_re": lam_re, "lam_im": lam_im, "log_dt": log_dt.reshape(SSM_G, 1),
         "lam_re_rep": jnp.tile(lam_re, (1, SSM_N)), "lam_im_rep": jnp.tile(lam_im, (1, SSM_N)), "b_re": wide(b_re), "b_im": wide(b_im)}
    lbr, lbi, bbr, bbi = _ssm_prep(lam_re, lam_im, p["lam_re_rep"], p["lam_im_rep"], p["log_dt"], p["b_re"], p["b_im"])
    p["lam_bar_re"], p["lam_bar_im"] = lbr.reshape(1, SSM_L), lbi.reshape(1, SSM_L)
    rows = lambda m: m.reshape(SSM_G * SSM_N, SSM_P)
    p["wb_bd"] = _ssm_block_diag(rows(bbr), rows(bbi))
    p["wc_bd"] = _ssm_block_diag(rows(c_re), rows(-c_im))
    p["seg"] = jnp.tile(jnp.eye(SSM_P, dtype=F32), (SSM_N, 1))
    return p


RES_WEIGHT = (0.5, 1.0, 0.5)


def _local_step(x, tgt, vecs, weights_of, on_part, on_grads):
    def fns(i, w):
        if i % 3 != 1:
            win, wout_of = w
            return ((lambda h: _ffn_fwd(h, win, wout_of)),
                    (lambda df, sv, after: (_ffn_bwd(df, sv, win, wout_of(None), lambda tag, part: on_part(i, tag, part), after), None)))
        if i == 1:
            return (lambda h: _mix0_fwd(h, w)), (lambda df, sv, after: _mix0_bwd(df, sv, w, after))
        return (lambda h: _mix1_fwd(h, w)), (lambda df, sv, after: _mix1_bwd(df, sv, w, after))

    rw = RES_WEIGHT * 2
    saved, bwd = [], []
    f = None
    for i in range(6):
        w, token = weights_of(i, x if i == 0 else f)
        fwd, b = fns(i, w)
        if i == 0:
            h = _prenorm_fwd(x, vecs, 0, token)
        else:
            x, h = _post_pre_fwd(x, f, vecs, i, rw[i - 1], token)
        f, inner = fwd(h)
        saved.append((x, f, inner))
        bwd.append(b)
    loss_row, dx, df, dv_top = _last_shell(x, f, tgt, vecs, 5, rw[5])
    token = jnp.zeros((8, LANE), F32)
    for i in reversed(range(6)):
        x_i, _, inner = saved[i]
        dh, extra = bwd[i](df, inner, token)
        if i > 0:
            dx, df, dv = _pre_post_bwd(dx, dh, x_i, saved[i - 1][1], vecs, i, rw[i - 1])
        else:
            dx, dv = _prenorm_bwd(dx, dh, x_i, vecs, 0)
        token = on_grads(i, extra, dv, dv_top if i == 5 else None, loss_row)
    return dx


def _pad_rows(v, rows):
    return jnp.pad(v, (0, rows * LANE - v.shape[0])).reshape(rows, LANE)


def _pack(parts):
    flat, layout, off = [], [], 0
    for a in parts:
        n = a.size
        padded = -(-n // LANE) * LANE
        flat.append(jnp.pad(a.reshape(-1).astype(F32), (0, padded - n)))
        layout.append((off, n, a.shape))
        off += padded
    return jnp.concatenate(flat), layout


def _unpack(flat, layout):
    return [flat[off:off + n].reshape(shape) for off, n, shape in layout]


TRANSPOSED = ["ffn_w_in", "ab_w_in", "ssm_b_re", "ssm_b_im"]
WEIGHTS = ['ada_w', 'ada_b', 'norm_pre', 'norm_post', 'ffn_w_in', 'ffn_w_out', 'ab_w_in', 'pool_w', 'pool_scale', 'sgu_ln_g',
           'sgu_ln_b', 'sgu_w', 'sgu_b', 'ab_w_out', 'ssm_w_in', 'ssm_lam_re', 'ssm_lam_im', 'ssm_b_re', 'ssm_b_im', 'ssm_c_re',
           'ssm_c_im', 'ssm_d', 'ssm_log_dt', 'ssm_w_glu']


def kernel(x, c, ada_w, ada_b, norm_pre, norm_post, ffn_w_in, ffn_w_out, ab_w_in, pool_w, pool_scale, sgu_ln_g, sgu_ln_b, sgu_w, sgu_b, ab_w_out, ssm_w_in, ssm_lam_re, ssm_lam_im, ssm_b_re, ssm_b_im, ssm_c_re, ssm_c_im, ssm_d, ssm_log_dt, ssm_w_glu, loss_target, m_ada_w, m_ada_b, m_norm_pre, m_norm_post, m_ffn_w_in, m_ffn_w_out, m_ab_w_in, m_pool_w, m_pool_scale, m_sgu_ln_g, m_sgu_ln_b, m_sgu_w, m_sgu_b, m_ab_w_out, m_ssm_w_in, m_ssm_lam_re, m_ssm_lam_im, m_ssm_b_re, m_ssm_b_im, m_ssm_c_re, m_ssm_c_im, m_ssm_d, m_ssm_log_dt, m_ssm_w_glu, v_ada_w, v_ada_b, v_norm_pre, v_norm_post, v_ffn_w_in, v_ffn_w_out, v_ab_w_in, v_pool_w, v_pool_scale, v_sgu_ln_g, v_sgu_ln_b, v_sgu_w, v_sgu_b, v_ab_w_out, v_ssm_w_in, v_ssm_lam_re, v_ssm_lam_im, v_ssm_b_re, v_ssm_b_im, v_ssm_c_re, v_ssm_c_im, v_ssm_d, v_ssm_log_dt, v_ssm_w_glu):
    args = locals()
    wts = {n: args[n] for n in WEIGHTS}
    mom = {n: args["m_" + n] for n in WEIGHTS}
    var = {n: args["v_" + n] for n in WEIGHTS}
    for n in TRANSPOSED:
        for t in (wts, mom, var):
            t[n] = jnp.swapaxes(t[n], -1, -2)
    me = 4 * lax.axis_index("x") + 2 * lax.axis_index("y") + lax.axis_index("c")
    s = x.shape[1]
    nd = D // N_DEV

    small_in, small_in_layout = _pack([c, norm_pre, norm_post, ssm_d])
    small_rows = -(-small_in.shape[0] // (8 * LANE)) * 8
    (g_small,) = _all_gather("gather_small", [_pad_rows(small_in, small_rows)])
    g_small = g_small.reshape(N_DEV, -1)
    c_all, npre_g, npost_g, sd_g = [jnp.stack([_unpack(g_small[j], small_in_layout)[i] for j in range(N_DEV)]) for i in range(4)]
    c_all = c_all.reshape(N_DEV, D)
    norm_pre_full = npre_g.transpose(1, 2, 0, 3).reshape(2, 3, D)
    norm_post_full = npost_g.transpose(1, 2, 0, 3).reshape(2, 3, D)
    ssm_d_full = sd_g.transpose(1, 0, 2).reshape(1, D)

    nw = ada_w.shape[-1]
    (mod_g,) = _all_gather("gather_mod", [_mod_part(c_all, ada_w)])
    mod = lax.dynamic_index_in_dim(mod_g, me, axis=2, keepdims=False)
    mod = (mod.transpose(1, 0, 2).reshape(2, N_DEV * nw) + ada_b).reshape(2, 3, 3, D)

    w_in_t = wts["ffn_w_in"]
    shards = [[w_in_t[0, 0]], [ffn_w_out[0, 0]], [wts["ab_w_in"][0], ab_w_out[0]], [w_in_t[0, 1]], [ffn_w_out[0, 1]],
              [w_in_t[1, 0]], [ffn_w_out[1, 0]], [ssm_w_in[0], ssm_w_glu[0]], [w_in_t[1, 1]], [ffn_w_out[1, 1]]]
    first_group = {0: 0, 1: 2, 2: 3, 3: 5, 4: 7, 5: 8}
    first_groups = set(first_group.values())
    same_core = (2, 4, 6)

    def gather_plan(n):
        return lambda me_, peer_, k: [(a, None, a, me_) for a in range(n)] if k in (0, 1) + same_core else []

    def relay_plan(n):
        return lambda me_, peer_, k: [(a, me_ ^ kk, a, me_ ^ kk) for kk in same_core for a in range(n)] if k == 1 else []

    gathers, relays = [], {}
    token = mod_g
    for g, group in enumerate(shards):
        group = [a.astype(BF16) for a in group]
        sems, srcs_thru, lands, token = _exchange_start(
            f"gather_start_{g}", group, [_sds((N_DEV,) + a.shape, BF16) for a in group], gather_plan(len(group)), len(group), token)
        gathers.append((sems, srcs_thru, lands))
    mod6 = mod.reshape(6, 3, D)
    vecs = jnp.stack([norm_pre_full.reshape(6, D), mod6[:, 1], mod6[:, 0], norm_post_full.reshape(6, D), mod6[:, 2]]
                     + [jnp.zeros((6, D), F32)] * 3, axis=1)
    vecs = vecs + token[0, 0]

    def relay(g, after):
        sems, srcs_thru, lands = gathers[g]
        n = len(lands)
        relays[g] = _exchange_relay(f"gather_relay_{g}", sems, srcs_thru, lands, gather_plan(n), n, relay_plan(n), 3 * n, after)

    def fetch(g, after):
        if g not in relays:
            relay(g, after)
        sems, lands, token = relays[g]
        n = len(lands)
        got = _exchange_wait(f"gather_wait_{g}", sems, None, lands, relay_plan(n), 3 * n, after)
        if g + 1 in first_groups and g + 1 != first_group[2]:
            relay(g + 1, got[0])
            token = relays[g + 1][2]
        return got, token

    head_sum = jnp.repeat(jnp.eye(NH, LANE, dtype=F32), HD, axis=0)
    mix0 = {"pool_w": pool_w[0], "pool_scale": pool_scale, "sgu_ln_g": sgu_ln_g, "sgu_ln_b": sgu_ln_b, "sgu_w": sgu_w[0],
            "sgu_bt": jnp.pad(sgu_b[0].T, ((0, 0), (0, LANE - NH))), "head_sum": head_sum}
    mix1 = _ssm_params(ssm_lam_re[0], ssm_lam_im[0], ssm_b_re[0], ssm_b_im[0], ssm_c_re[0], ssm_c_im[0], ssm_log_dt[0])
    mix1["ssm_d"] = ssm_d_full

    def weights_of(i, x_in):
        g = first_group[i]
        if i % 3 != 1:
            (win,), token = fetch(g, x_in)
            cache = []

            def wout_of(act):
                if not cache:
                    cache.append(fetch(g + 1, act)[0][0])
                return cache[0]

            return (win, wout_of), token
        (a, b), token = fetch(g, x_in)
        if i == 1:
            return dict(mix0, ab_w_in=a.reshape(-1, D), ab_w_out=b.reshape(D, D)), token
        return dict(mix1, ssm_w_in=a.reshape(D, D), ssm_w_glu=b.transpose(1, 0, 2).reshape(D, -1)), token

    def shard_cols(a):
        r = a.shape[0]
        return a.reshape(r, N_DEV, -1).transpose(1, 0, 2)

    scatter_plan = lambda me_, peer_, k: [(0, peer_, 0, me_), (1, peer_, 1, me_)]
    scatter_plan1 = lambda me_, peer_, k: [(0, peer_, 0, me_)]
    scatters = []
    last_token = [jnp.zeros((8, LANE), F32)]
    pieces, mixer, bundles = {}, {}, {}
    bundle_plan = lambda me_, peer_, k: [(0, None, 0, me_)]

    held = {}

    def send_parts(i, tag, names, parts, plan):
        sems, srcs_thru, lands, last_token[0] = _exchange_start(
            f"scatter_start_{i}_{tag}", parts, [_sds(a.shape, BF16) for a in parts], plan, len(parts), last_token[0])
        scatters.append((i, names, plan, sems, srcs_thru, lands))
        return last_token[0]

    def on_part(i, tag, part):
        if tag == "w_out" and i != 0:
            held[i] = part
            return last_token[0]
        if i == 0:
            if tag == "w_in":
                held[0] = part
                return last_token[0]
            return send_parts(0, tag, ("ffn_w_out",), [part], scatter_plan1)
        return send_parts(i, tag, ("ffn_w_out", "ffn_w_in"), [held[i], part], scatter_plan)
    mix0_names = ["pool_w", "pool_scale", "sgu_ln_g", "sgu_ln_b", "sgu_w", "sgu_b"]
    mix1_names = ["ssm_lam_re", "ssm_lam_im", "ssm_b_re", "ssm_b_im", "ssm_c_re", "ssm_c_im", "ssm_log_dt", "ssm_d"]

    def start_bundle(tag, arrays):
        flat, layout = _pack(arrays)
        rows = -(-flat.shape[0] // (8 * LANE)) * 8
        plan = gather_plan(1) if tag == "a" else bundle_plan
        sems, srcs_thru, lands, last_token[0] = _exchange_start(
            f"small_start_{tag}", [_pad_rows(flat, rows)], [_sds((N_DEV, rows, LANE))], plan, 1, last_token[0])
        bundles[tag] = (sems, srcs_thru, lands, layout)

    def on_grads(i, extra, dv, dv_top, loss_row):
        pieces[i] = dv
        if i == 5:
            pieces["top"] = dv_top
        if i == 4:
            mixer.update({n: extra[n] for n in mix1_names})
        if i == 1:
            mixer.update({n: extra[n] for n in mix0_names})
            start_bundle("a", [jnp.stack([pieces[j] for j in ("top", 5, 4, 3, 2, 1)])] + [mixer[n] for n in mix0_names + mix1_names])
        if i == 0:
            start_bundle("b", [dv, loss_row])
            return send_parts(0, "w_in", ("ffn_w_in",), [held[0]], scatter_plan1)
        if i % 3 != 1:
            return last_token[0]
        if i == 1:
            names, parts = ("ab_w_in", "ab_w_out"), [extra["ab_w_in"].reshape(N_DEV, -1, D), extra["ab_w_out"].reshape(N_DEV, nd, D)]
        else:
            names, parts = ("ssm_w_in", "ssm_w_glu"), [extra["ssm_w_in"].reshape(N_DEV, nd, D), shard_cols(extra["ssm_w_glu"])]
        sems, srcs_thru, lands, last_token[0] = _exchange_start(
            f"scatter_start_{i}", parts, [_sds(a.shape, BF16) for a in parts], scatter_plan, 2, last_token[0])
        scatters.append((i, names, scatter_plan, sems, srcs_thru, lands))
        return last_token[0]

    grad_x = _local_step(x[0], loss_target[0], vecs, weights_of, on_part, on_grads)

    out_g, out_d, out_m, out_v = {}, {}, {}, {}
    big_out = {}

    def adam_big(name, recv, n, slot=0, after=None):
        c_ = wts[n].shape[-1]
        big_out[n] = _adamw(name, recv.reshape(recv.shape[0], -1, c_), *[t[n].reshape(-1, c_) for t in (wts, mom, var)],
                            slot=slot, prev=big_out.get(n), after=after)
        return big_out[n][0]

    ffn_slot = {0: 0, 2: 1, 3: 2, 5: 3}

    def land_and_update(entries, after):
        for i, names, plan, sems, srcs_thru, lands in entries:
            recv = _exchange_wait(f"scatter_wait_{i}_{names[0]}", sems, srcs_thru, lands, plan, len(names), after)
            for n, r in zip(names, recv):
                after = adam_big(f"adamw_{n}_{i}", r, n, ffn_slot.get(i, 0), after)
        return after

    after = land_and_update([e for e in scatters if e[0] != 0], last_token[0])

    def landed(tag, g_parts):
        layout = bundles[tag][3]
        off, n, shape = layout[0]
        dmods = g_parts.reshape(N_DEV, -1)[:, off:off + n].reshape((N_DEV,) + shape)
        total = _sum_parts(g_parts)
        return dmods, _unpack(total.reshape(-1), layout), total

    def adam_small(n, g, after=None):
        cols = wts[n].shape[-1]
        res = _adamw(f"adamw_{n}", g.reshape(1, -1, cols), *[t[n].reshape(-1, cols) for t in (wts, mom, var)], after=after)
        for o, arr in zip((out_g, out_d, out_m, out_v), res):
            o[n] = arr.reshape(wts[n].shape)
            if n in TRANSPOSED:
                o[n] = jnp.swapaxes(o[n], -1, -2)
        return res[0]

    sems, srcs_thru, lands, _ = bundles["a"]
    sems, lands, _ = _exchange_relay("small_relay_a", sems, srcs_thru, lands, gather_plan(1), 1, relay_plan(1), 3, after)
    (parts_a,) = _exchange_wait("small_wait_a", sems, None, lands, relay_plan(1), 3, after)
    shells_a, sums_a, after = landed("a", parts_a)
    small = dict(zip(mix0_names + mix1_names, sums_a[1:]))
    def adam_tiny(name, grads, after):
        view = lambda n, a: a.reshape(-1, wts[n].shape[-1])
        items = [(view(n, g),) + tuple(view(n, t[n]) for t in (wts, mom, var)) for n, g in grads.items()]
        for (n, _), item, res in zip(grads.items(), items, _adamw_many(name, items, after)):
            for o, arr in zip((out_g, out_d, out_m, out_v), (item[0],) + res):
                o[n] = arr.reshape(wts[n].shape)
        return res[0]

    tiny = ["pool_scale", "sgu_ln_g", "sgu_ln_b", "sgu_b", "ssm_lam_re", "ssm_lam_im", "ssm_log_dt"]
    for n in [n for n in mix0_names + mix1_names if n not in tiny and n != "ssm_d"]:
        after = adam_small(n, small[n], after)
    after = adam_tiny("adamw_tiny_mixers", dict({n: small[n] for n in tiny},
                                                ssm_d=lax.dynamic_slice_in_dim(small["ssm_d"], me * nd, nd, axis=1)), after)
    def shell_grads(top, blocks, first):
        own_rows = jnp.concatenate([first[..., None, :, :], blocks[..., :0:-1, :, :]], axis=-3)
        next_rows = jnp.concatenate([own_rows[..., 1:, :, :], top[..., None, :, :]], axis=-3)
        dmod_ = jnp.stack([own_rows[..., V_SHIFT, :], own_rows[..., V_SCALE, :], next_rows[..., V_GATE, :]], axis=-2)
        return dmod_, own_rows[..., V_GPRE, :], next_rows[..., V_GPOST, :]

    def ada_w_layer(l, dmod_l, after):
        mine = lax.dynamic_index_in_dim(dmod_l.reshape(N_DEV, N_DEV, nw), me, axis=1, keepdims=False)
        return adam_big(f"adamw_ada_w_{l}", _ada_w_grad(c_all.T, mine[None]), "ada_w", l, after)

    after = ada_w_layer(1, shell_grads(shells_a[:, 0], shells_a, jnp.zeros_like(shells_a[:, 0]))[0][:, 3:], after)
    sems, srcs_thru, lands, _ = bundles["b"]
    (parts_b,) = _exchange_wait("small_wait_b", sems, srcs_thru, lands, bundle_plan, 1, after)
    shell_b, (first_sum, loss_sum), after = landed("b", parts_b)
    loss = loss_sum[0, 0]

    dmod_sum, dg_pre_sum, dg_post_sum = shell_grads(sums_a[0][0], sums_a[0], first_sum)
    own = lambda a: lax.dynamic_slice_in_dim(a, me * nd, nd, axis=1)
    after = adam_tiny("adamw_tiny_shell", {"ada_b": dmod_sum, "norm_pre": own(dg_pre_sum), "norm_post": own(dg_post_sum)}, after)

    after = ada_w_layer(0, shell_grads(shells_a[:, 0], shells_a, shell_b)[0][:, :3], after)

    land_and_update([e for e in scatters if e[0] == 0], after)
    for n, res in big_out.items():
        for o, arr in zip((out_g, out_d, out_m, out_v), res):
            o[n] = arr.reshape(wts[n].shape)
            if n in TRANSPOSED:
                o[n] = jnp.swapaxes(o[n], -1, -2)

    return (loss, grad_x[None], *[out_g[n] for n in WEIGHTS], *[out_d[n] for n in WEIGHTS],
            *[out_m[n] for n in WEIGHTS], *[out_v[n] for n in WEIGHTS])
```
